```python
import jax, jax.numpy as jnp
from jax import lax
import numpy as np

D_MODEL = 1024
BATCH = 16
SEQ = 2048
DEPTH = 1

N_META = 16
BLOCK = 128
WINDOW = 128
PREFIX = BLOCK
N_PAD = PREFIX - N_META
HEAD_DIM = 64
A_HEADS = D_MODEL // 128
A_KV_HEADS = A_HEADS // 4
A_GROUP = A_HEADS // A_KV_HEADS
B_HEADS = D_MODEL // 128
A_WIDTH = A_HEADS * HEAD_DIM
A_KV_WIDTH = A_KV_HEADS * HEAD_DIM
B_WIDTH = B_HEADS * HEAD_DIM
W_IN_COLS = A_WIDTH + 2 * A_KV_WIDTH + 3 * B_WIDTH + B_HEADS + 2 * D_MODEL
D_FF = ((8 * D_MODEL // 3 + 127) // 128) * 128
EPS = 1e-6
NEG = -1e30

kernel_name = "hybrid_swa_sink_fox_macaron_block"


def rms_norm(x, g):
    xf = x.astype(jnp.float32)
    y = xf * lax.rsqrt(jnp.mean(xf * xf, axis=-1, keepdims=True) + EPS)
    return (y * g.astype(jnp.float32)).astype(x.dtype)


def swiglu(x, w_in, w_out):
    gu = x @ w_in
    g, u = jnp.split(gu, 2, axis=-1)
    return (jax.nn.silu(g) * u) @ w_out


def alibi_slopes(n_heads):
    return jnp.exp2(-8.0 * jnp.arange(1, n_heads + 1, dtype=jnp.float32) / n_heads)


def sliding_window_sink_attention(q, k, v, sinks):
    b, l, _, dh = q.shape
    nb = l // BLOCK
    qb = q.reshape(b, nb, BLOCK, A_KV_HEADS, A_GROUP, dh)
    kb = k.reshape(b, nb, BLOCK, A_KV_HEADS, dh)
    vb = v.reshape(b, nb, BLOCK, A_KV_HEADS, dh)
    pad_blk = ((0, 0), (1, 0), (0, 0), (0, 0), (0, 0))
    band_k = jnp.concatenate([jnp.pad(kb[:, :-1], pad_blk), kb], axis=2)
    band_v = jnp.concatenate([jnp.pad(vb[:, :-1], pad_blk), vb], axis=2)
    meta_k = jnp.broadcast_to(k[:, None, N_PAD:PREFIX], (b, nb, N_META, A_KV_HEADS, dh))
    meta_v = jnp.broadcast_to(v[:, None, N_PAD:PREFIX], (b, nb, N_META, A_KV_HEADS, dh))
    keys = jnp.concatenate([meta_k, band_k], axis=2)
    vals = jnp.concatenate([meta_v, band_v], axis=2)

    q_pos = jnp.arange(l).reshape(nb, BLOCK)
    band_pos = (jnp.arange(nb)[:, None] - 1) * BLOCK + jnp.arange(2 * BLOCK)[None, :]
    meta_pos = jnp.broadcast_to(N_PAD + jnp.arange(N_META)[None, :], (nb, N_META))
    k_pos = jnp.concatenate([meta_pos, band_pos], axis=1)
    is_band = jnp.concatenate([jnp.zeros((N_META,), bool), jnp.ones((2 * BLOCK,), bool)])
    dist = q_pos[:, :, None] - k_pos[:, None, :]
    band_ok = (dist < WINDOW) & (k_pos[:, None, :] >= PREFIX)
    allowed = (dist >= 0) & jnp.where(is_band[None, None, :], band_ok, True)

    slopes = alibi_slopes(A_HEADS).reshape(A_KV_HEADS, A_GROUP)
    s = jnp.einsum('bnqkgd,bnskd->bnkgqs', qb, keys).astype(jnp.float32) * (dh ** -0.5)
    s = s - slopes[None, None, :, :, None, None] * dist.astype(jnp.float32)[None, :, None, None, :, :]
    s = jnp.where(allowed[None, :, None, None, :, :], s, NEG)
    sink = jnp.broadcast_to(
        sinks.astype(jnp.float32).reshape(A_KV_HEADS, A_GROUP)[None, None, :, :, None, None],
        s.shape[:-1] + (1,))
    p = jax.nn.softmax(jnp.concatenate([s, sink], axis=-1), axis=-1)[..., :-1]
    o = jnp.einsum('bnkgqs,bnskd->bnqkgd', p.astype(v.dtype), vals)
    return o.reshape(b, l, A_HEADS * dh)


def forgetting_attention(q, k, v, log_f):
    b, l, h, dh = q.shape
    nb = l // BLOCK
    c = jnp.cumsum(log_f, axis=1).transpose(0, 2, 1)
    outs = []
    for i in range(nb):
        q_lo, k_hi = i * BLOCK, (i + 1) * BLOCK
        s = jnp.einsum('bqhd,bshd->bhqs', q[:, q_lo:k_hi], k[:, :k_hi]).astype(jnp.float32) * (dh ** -0.5)
        s = s + c[:, :, q_lo:k_hi, None] - c[:, :, None, :k_hi]
        q_pos = q_lo + jnp.arange(BLOCK)
        k_pos = jnp.arange(k_hi)
        allowed = (k_pos[None, :] <= q_pos[:, None]) & (k_pos[None, :] >= N_PAD)
        s = jnp.where(allowed[None, None], s, NEG)
        p = jax.nn.softmax(s, axis=-1)
        outs.append(jnp.einsum('bhqs,bshd->bqhd', p.astype(v.dtype), v[:, :k_hi]))
    return jnp.concatenate(outs, axis=1).reshape(b, l, h * dh)


def _fwd_setup_inputs(seed: int = 0) -> dict:
    key = jax.random.key(seed)
    ks = jax.random.split(key, 20)
    f32 = jnp.float32
    n = lambda k, shape, scale: jax.random.normal(k, shape, f32) * scale
    return {
        "x": n(ks[0], (BATCH, SEQ, D_MODEL), 1.0),
        "meta_tokens": n(ks[1], (N_META, D_MODEL), 1.0),
        "ffn1_norm": 1.0 + n(ks[2], (DEPTH, D_MODEL), 0.02),
        "ffn1_w_in": n(ks[3], (DEPTH, D_MODEL, 2 * D_FF), D_MODEL ** -0.5),
        "ffn1_w_out": n(ks[4], (DEPTH, D_FF, D_MODEL), D_FF ** -0.5),
        "mix_norm": 1.0 + n(ks[5], (DEPTH, D_MODEL), 0.02),
        "w_in": n(ks[6], (DEPTH, D_MODEL, W_IN_COLS), D_MODEL ** -0.5),
        "b_forget": 2.0 + n(ks[7], (DEPTH, B_HEADS), 0.1),
        "attn_sinks": n(ks[8], (DEPTH, A_HEADS), 0.5),
        "w_branch_a": n(ks[9], (DEPTH, A_WIDTH, D_MODEL), A_WIDTH ** -0.5),
        "w_branch_b": n(ks[10], (DEPTH, B_WIDTH, D_MODEL), B_WIDTH ** -0.5),
        "w_out": n(ks[11], (DEPTH, D_MODEL, D_MODEL), D_MODEL ** -0.5),
        "ffn2_norm": 1.0 + n(ks[12], (DEPTH, D_MODEL), 0.02),
        "ffn2_w_in": n(ks[13], (DEPTH, D_MODEL, 2 * D_FF), D_MODEL ** -0.5),
        "ffn2_w_out": n(ks[14], (DEPTH, D_FF, D_MODEL), D_FF ** -0.5),
        "final_norm": 1.0 + n(ks[15], (D_MODEL,), 0.02),
    }


def _fwd_reference(x, meta_tokens, ffn1_norm, ffn1_w_in, ffn1_w_out, mix_norm, w_in, b_forget,
              attn_sinks, w_branch_a, w_branch_b, w_out, ffn2_norm, ffn2_w_in, ffn2_w_out,
              final_norm):
    b = x.shape[0]
    pads = jnp.zeros((b, N_PAD, D_MODEL), x.dtype)
    meta = jnp.broadcast_to(meta_tokens.astype(x.dtype)[None], (b, N_META, D_MODEL))
    h = jnp.concatenate([pads, meta, x], axis=1)
    l = h.shape[1]

    sizes = [A_WIDTH, A_KV_WIDTH, A_KV_WIDTH, B_WIDTH, B_WIDTH, B_WIDTH, B_HEADS, D_MODEL, D_MODEL]
    offsets = []
    acc = 0
    for sz in sizes[:-1]:
        acc += sz
        offsets.append(acc)

    for i in range(DEPTH):
        h = h + 0.5 * swiglu(rms_norm(h, ffn1_norm[i]), ffn1_w_in[i], ffn1_w_out[i])

        u = rms_norm(h, mix_norm[i])
        proj = u @ w_in[i]
        qa, ka, va, qb, kb, vb, f_logit, g_a, g_b = jnp.split(proj, offsets, axis=-1)
        qa = qa.reshape(b, l, A_HEADS, HEAD_DIM)
        ka = ka.reshape(b, l, A_KV_HEADS, HEAD_DIM)
        va = va.reshape(b, l, A_KV_HEADS, HEAD_DIM)
        qb = qb.reshape(b, l, B_HEADS, HEAD_DIM)
        kb = kb.reshape(b, l, B_HEADS, HEAD_DIM)
        vb = vb.reshape(b, l, B_HEADS, HEAD_DIM)
        log_f = jax.nn.log_sigmoid((f_logit + b_forget[i]).astype(jnp.float32))

        y_a = sliding_window_sink_attention(qa, ka, va, attn_sinks[i]) @ w_branch_a[i]
        y_b = forgetting_attention(qb, kb, vb, log_f) @ w_branch_b[i]
        mixed = jax.nn.sigmoid(g_a) * y_a + jax.nn.sigmoid(g_b) * y_b
        h = h + mixed @ w_out[i]

        h = h + 0.5 * swiglu(rms_norm(h, ffn2_norm[i]), ffn2_w_in[i], ffn2_w_out[i])

    return rms_norm(h, final_norm)[:, PREFIX:]


import jax as _jax
import jax.numpy as _jnp

TWIN_FORMAT = 'train_step'
FWD_PARAMS = ['x', 'meta_tokens', 'ffn1_norm', 'ffn1_w_in', 'ffn1_w_out', 'mix_norm', 'w_in', 'b_forget', 'attn_sinks', 'w_branch_a', 'w_branch_b', 'w_out', 'ffn2_norm', 'ffn2_w_in', 'ffn2_w_out', 'final_norm']
TWIN_WEIGHTS = ['meta_tokens', 'ffn1_norm', 'ffn1_w_in', 'ffn1_w_out', 'mix_norm', 'w_in', 'b_forget', 'attn_sinks', 'w_branch_a', 'w_branch_b', 'w_out', 'ffn2_norm', 'ffn2_w_in', 'ffn2_w_out', 'final_norm']
TWIN_DIFF_INPUT = 'x'
TWIN_INPUTS = ['x', 'meta_tokens', 'ffn1_norm', 'ffn1_w_in', 'ffn1_w_out', 'mix_norm', 'w_in', 'b_forget', 'attn_sinks', 'w_branch_a', 'w_branch_b', 'w_out', 'ffn2_norm', 'ffn2_w_in', 'ffn2_w_out', 'final_norm', 'loss_target', 'm_meta_tokens', 'm_ffn1_norm', 'm_ffn1_w_in', 'm_ffn1_w_out', 'm_mix_norm', 'm_w_in', 'm_b_forget', 'm_attn_sinks', 'm_w_branch_a', 'm_w_branch_b', 'm_w_out', 'm_ffn2_norm', 'm_ffn2_w_in', 'm_ffn2_w_out', 'm_final_norm', 'v_meta_tokens', 'v_ffn1_norm', 'v_ffn1_w_in', 'v_ffn1_w_out', 'v_mix_norm', 'v_w_in', 'v_b_forget', 'v_attn_sinks', 'v_w_branch_a', 'v_w_branch_b', 'v_w_out', 'v_ffn2_norm', 'v_ffn2_w_in', 'v_ffn2_w_out', 'v_final_norm']
TWIN_OUTPUTS = ['loss', 'grad_x', 'grad_meta_tokens', 'grad_ffn1_norm', 'grad_ffn1_w_in', 'grad_ffn1_w_out', 'grad_mix_norm', 'grad_w_in', 'grad_b_forget', 'grad_attn_sinks', 'grad_w_branch_a', 'grad_w_branch_b', 'grad_w_out', 'grad_ffn2_norm', 'grad_ffn2_w_in', 'grad_ffn2_w_out', 'grad_final_norm', 'delta_meta_tokens', 'delta_ffn1_norm', 'delta_ffn1_w_in', 'delta_ffn1_w_out', 'delta_mix_norm', 'delta_w_in', 'delta_b_forget', 'delta_attn_sinks', 'delta_w_branch_a', 'delta_w_branch_b', 'delta_w_out', 'delta_ffn2_norm', 'delta_ffn2_w_in', 'delta_ffn2_w_out', 'delta_final_norm', 'new_m_meta_tokens', 'new_m_ffn1_norm', 'new_m_ffn1_w_in', 'new_m_ffn1_w_out', 'new_m_mix_norm', 'new_m_w_in', 'new_m_b_forget', 'new_m_attn_sinks', 'new_m_w_branch_a', 'new_m_w_branch_b', 'new_m_w_out', 'new_m_ffn2_norm', 'new_m_ffn2_w_in', 'new_m_ffn2_w_out', 'new_m_final_norm', 'new_v_meta_tokens', 'new_v_ffn1_norm', 'new_v_ffn1_w_in', 'new_v_ffn1_w_out', 'new_v_mix_norm', 'new_v_w_in', 'new_v_b_forget', 'new_v_attn_sinks', 'new_v_w_branch_a', 'new_v_w_branch_b', 'new_v_w_out', 'new_v_ffn2_norm', 'new_v_ffn2_w_in', 'new_v_ffn2_w_out', 'new_v_final_norm']
TWIN_LEAF_KINDS = {'loss': 'loss', 'grad_x': 'grad_x', 'grad_meta_tokens': 'grad_w', 'grad_ffn1_norm': 'grad_w', 'grad_ffn1_w_in': 'grad_w', 'grad_ffn1_w_out': 'grad_w', 'grad_mix_norm': 'grad_w', 'grad_w_in': 'grad_w', 'grad_b_forget': 'grad_w', 'grad_attn_sinks': 'grad_w', 'grad_w_branch_a': 'grad_w', 'grad_w_branch_b': 'grad_w', 'grad_w_out': 'grad_w', 'grad_ffn2_norm': 'grad_w', 'grad_ffn2_w_in': 'grad_w', 'grad_ffn2_w_out': 'grad_w', 'grad_final_norm': 'grad_w', 'delta_meta_tokens': 'delta_w', 'delta_ffn1_norm': 'delta_w', 'delta_ffn1_w_in': 'delta_w', 'delta_ffn1_w_out': 'delta_w', 'delta_mix_norm': 'delta_w', 'delta_w_in': 'delta_w', 'delta_b_forget': 'delta_w', 'delta_attn_sinks': 'delta_w', 'delta_w_branch_a': 'delta_w', 'delta_w_branch_b': 'delta_w', 'delta_w_out': 'delta_w', 'delta_ffn2_norm': 'delta_w', 'delta_ffn2_w_in': 'delta_w', 'delta_ffn2_w_out': 'delta_w', 'delta_final_norm': 'delta_w', 'new_m_meta_tokens': 'new_m', 'new_m_ffn1_norm': 'new_m', 'new_m_ffn1_w_in': 'new_m', 'new_m_ffn1_w_out': 'new_m', 'new_m_mix_norm': 'new_m', 'new_m_w_in': 'new_m', 'new_m_b_forget': 'new_m', 'new_m_attn_sinks': 'new_m', 'new_m_w_branch_a': 'new_m', 'new_m_w_branch_b': 'new_m', 'new_m_w_out': 'new_m', 'new_m_ffn2_norm': 'new_m', 'new_m_ffn2_w_in': 'new_m', 'new_m_ffn2_w_out': 'new_m', 'new_m_final_norm': 'new_m', 'new_v_meta_tokens': 'new_v', 'new_v_ffn1_norm': 'new_v', 'new_v_ffn1_w_in': 'new_v', 'new_v_ffn1_w_out': 'new_v', 'new_v_mix_norm': 'new_v', 'new_v_w_in': 'new_v', 'new_v_b_forget': 'new_v', 'new_v_attn_sinks': 'new_v', 'new_v_w_branch_a': 'new_v', 'new_v_w_branch_b': 'new_v', 'new_v_w_out': 'new_v', 'new_v_ffn2_norm': 'new_v', 'new_v_ffn2_w_in': 'new_v', 'new_v_ffn2_w_out': 'new_v', 'new_v_final_norm': 'new_v'}


def _forward(args):
    return _fwd_reference(*[args[k] for k in FWD_PARAMS])


def _output_shape():
    out = _jax.eval_shape(lambda: _forward(_fwd_setup_inputs(0)))
    return out.shape, out.dtype

N_MICROBATCH = 1
ADAM_LR = 0.001
ADAM_B1 = 0.9
ADAM_B2 = 0.999
ADAM_EPS = 1e-08
ADAM_WD = 0.01
ADAM_STEP = 10
PER_EXAMPLE_BATCH_AXIS = {'x': 0, 'loss_target': 0}
SHARED_INPUTS = []
_WEIGHT_DTYPES = {'meta_tokens': _jnp.float32, 'ffn1_norm': _jnp.float32, 'ffn1_w_in': _jnp.float32, 'ffn1_w_out': _jnp.float32, 'mix_norm': _jnp.float32, 'w_in': _jnp.float32, 'b_forget': _jnp.float32, 'attn_sinks': _jnp.float32, 'w_branch_a': _jnp.float32, 'w_branch_b': _jnp.float32, 'w_out': _jnp.float32, 'ffn2_norm': _jnp.float32, 'ffn2_w_in': _jnp.float32, 'ffn2_w_out': _jnp.float32, 'final_norm': _jnp.float32}
MOMENT_SCALE = {'meta_tokens': 2.486078e-03, 'ffn1_norm': 8.281959e-02, 'ffn1_w_in': 3.360207e-02, 'ffn1_w_out': 5.486699e-02, 'mix_norm': 7.446424e-02, 'w_in': 3.507454e-02, 'b_forget': 2.622835e-01, 'attn_sinks': 3.403210e-02, 'w_branch_a': 2.756592e-02, 'w_branch_b': 3.853234e-02, 'w_out': 4.687900e-02, 'ffn2_norm': 7.276587e-02, 'ffn2_w_in': 2.901302e-02, 'ffn2_w_out': 4.737649e-02, 'final_norm': 3.197980e+01}


def _to_microbatches(a, axis):
    t = _jnp.moveaxis(a, axis, 0)
    t = t.reshape((N_MICROBATCH, t.shape[0] // N_MICROBATCH) + t.shape[1:])
    return _jnp.moveaxis(t, 1, axis + 1)


def setup_inputs(seed: int = 0) -> dict:
    inp = _fwd_setup_inputs(seed)
    key = _jax.random.fold_in(_jax.random.key(seed), 7919)
    shape, _ = _output_shape()
    out = dict(inp)
    out["loss_target"] = _jax.random.normal(_jax.random.fold_in(key, 0), shape, _jnp.float32)
    for i, name in enumerate(TWIN_WEIGHTS):
        w = inp[name].astype(_jnp.float32)
        if MOMENT_SCALE is None:
            s = _jnp.sqrt(_jnp.mean(_jnp.square(w)) + 1e-30)
        else:
            s = MOMENT_SCALE[name]
        km, kv = _jax.random.split(_jax.random.fold_in(key, i + 1))
        out[name] = w
        out["m_" + name] = s * _jax.random.normal(km, w.shape, _jnp.float32)
        out["v_" + name] = (s * s) * _jax.random.uniform(kv, w.shape, _jnp.float32, 0.5, 1.5)
    if N_MICROBATCH > 1:
        for name, axis in PER_EXAMPLE_BATCH_AXIS.items():
            out[name] = _to_microbatches(out[name], axis)
    return {'x': out['x'], 'meta_tokens': out['meta_tokens'], 'ffn1_norm': out['ffn1_norm'], 'ffn1_w_in': out['ffn1_w_in'], 'ffn1_w_out': out['ffn1_w_out'], 'mix_norm': out['mix_norm'], 'w_in': out['w_in'], 'b_forget': out['b_forget'], 'attn_sinks': out['attn_sinks'], 'w_branch_a': out['w_branch_a'], 'w_branch_b': out['w_branch_b'], 'w_out': out['w_out'], 'ffn2_norm': out['ffn2_norm'], 'ffn2_w_in': out['ffn2_w_in'], 'ffn2_w_out': out['ffn2_w_out'], 'final_norm': out['final_norm'], 'loss_target': out['loss_target'], 'm_meta_tokens': out['m_meta_tokens'], 'm_ffn1_norm': out['m_ffn1_norm'], 'm_ffn1_w_in': out['m_ffn1_w_in'], 'm_ffn1_w_out': out['m_ffn1_w_out'], 'm_mix_norm': out['m_mix_norm'], 'm_w_in': out['m_w_in'], 'm_b_forget': out['m_b_forget'], 'm_attn_sinks': out['m_attn_sinks'], 'm_w_branch_a': out['m_w_branch_a'], 'm_w_branch_b': out['m_w_branch_b'], 'm_w_out': out['m_w_out'], 'm_ffn2_norm': out['m_ffn2_norm'], 'm_ffn2_w_in': out['m_ffn2_w_in'], 'm_ffn2_w_out': out['m_ffn2_w_out'], 'm_final_norm': out['m_final_norm'], 'v_meta_tokens': out['v_meta_tokens'], 'v_ffn1_norm': out['v_ffn1_norm'], 'v_ffn1_w_in': out['v_ffn1_w_in'], 'v_ffn1_w_out': out['v_ffn1_w_out'], 'v_mix_norm': out['v_mix_norm'], 'v_w_in': out['v_w_in'], 'v_b_forget': out['v_b_forget'], 'v_attn_sinks': out['v_attn_sinks'], 'v_w_branch_a': out['v_w_branch_a'], 'v_w_branch_b': out['v_w_branch_b'], 'v_w_out': out['v_w_out'], 'v_ffn2_norm': out['v_ffn2_norm'], 'v_ffn2_w_in': out['v_ffn2_w_in'], 'v_ffn2_w_out': out['v_ffn2_w_out'], 'v_final_norm': out['v_final_norm']}


def _loss(weights, diff, rest, loss_target):
    with _jax.named_scope("forward"):
        args = {**rest, TWIN_DIFF_INPUT: diff, **{k: w.astype(_WEIGHT_DTYPES[k]) for k, w in weights.items()}}
        y = _forward(args)
    with _jax.named_scope("loss_head"):
        err = _jnp.square(y.astype(_jnp.float32) - loss_target)
        return 0.5 * _jnp.sum(_jnp.mean(err, axis=-1)) if err.ndim else 0.5 * err


def _adamw(w, g, m, v):
    m = ADAM_B1 * m + (1.0 - ADAM_B1) * g
    v = ADAM_B2 * v + (1.0 - ADAM_B2) * _jnp.square(g)
    m_hat = m / (1.0 - ADAM_B1 ** ADAM_STEP)
    v_hat = v / (1.0 - ADAM_B2 ** ADAM_STEP)
    delta = -ADAM_LR * (m_hat / (_jnp.sqrt(v_hat) + ADAM_EPS) + ADAM_WD * w)
    return delta, m, v


def reference(x, meta_tokens, ffn1_norm, ffn1_w_in, ffn1_w_out, mix_norm, w_in, b_forget, attn_sinks, w_branch_a, w_branch_b, w_out, ffn2_norm, ffn2_w_in, ffn2_w_out, final_norm, loss_target, m_meta_tokens, m_ffn1_norm, m_ffn1_w_in, m_ffn1_w_out, m_mix_norm, m_w_in, m_b_forget, m_attn_sinks, m_w_branch_a, m_w_branch_b, m_w_out, m_ffn2_norm, m_ffn2_w_in, m_ffn2_w_out, m_final_norm, v_meta_tokens, v_ffn1_norm, v_ffn1_w_in, v_ffn1_w_out, v_mix_norm, v_w_in, v_b_forget, v_attn_sinks, v_w_branch_a, v_w_branch_b, v_w_out, v_ffn2_norm, v_ffn2_w_in, v_ffn2_w_out, v_final_norm):
    given = dict(x=x, meta_tokens=meta_tokens, ffn1_norm=ffn1_norm, ffn1_w_in=ffn1_w_in, ffn1_w_out=ffn1_w_out, mix_norm=mix_norm, w_in=w_in, b_forget=b_forget, attn_sinks=attn_sinks, w_branch_a=w_branch_a, w_branch_b=w_branch_b, w_out=w_out, ffn2_norm=ffn2_norm, ffn2_w_in=ffn2_w_in, ffn2_w_out=ffn2_w_out, final_norm=final_norm, loss_target=loss_target, m_meta_tokens=m_meta_tokens, m_ffn1_norm=m_ffn1_norm, m_ffn1_w_in=m_ffn1_w_in, m_ffn1_w_out=m_ffn1_w_out, m_mix_norm=m_mix_norm, m_w_in=m_w_in, m_b_forget=m_b_forget, m_attn_sinks=m_attn_sinks, m_w_branch_a=m_w_branch_a, m_w_branch_b=m_w_branch_b, m_w_out=m_w_out, m_ffn2_norm=m_ffn2_norm, m_ffn2_w_in=m_ffn2_w_in, m_ffn2_w_out=m_ffn2_w_out, m_final_norm=m_final_norm, v_meta_tokens=v_meta_tokens, v_ffn1_norm=v_ffn1_norm, v_ffn1_w_in=v_ffn1_w_in, v_ffn1_w_out=v_ffn1_w_out, v_mix_norm=v_mix_norm, v_w_in=v_w_in, v_b_forget=v_b_forget, v_attn_sinks=v_attn_sinks, v_w_branch_a=v_w_branch_a, v_w_branch_b=v_w_branch_b, v_w_out=v_w_out, v_ffn2_norm=v_ffn2_norm, v_ffn2_w_in=v_ffn2_w_in, v_ffn2_w_out=v_ffn2_w_out, v_final_norm=v_final_norm)
    weights = {n: given[n] for n in TWIN_WEIGHTS}
    shared = {n: given[n] for n in SHARED_INPUTS}
    per_example = {n: given[n] for n in ['x']}
    grad_fn = _jax.value_and_grad(_loss, argnums=(0, 1))

    def one_microbatch(ex, loss_target):
        ex = dict(ex)
        diff = ex.pop(TWIN_DIFF_INPUT)
        return grad_fn(weights, diff, {**shared, **ex}, loss_target)

    if N_MICROBATCH == 1:
        loss, (grad_w, grad_x) = one_microbatch(per_example, given["loss_target"])
    else:
        def body(carry, xs):
            loss_sum, grad_sum = carry
            l_k, (gw_k, gx_k) = one_microbatch(xs[0], xs[1])
            with _jax.named_scope("update"):
                return (loss_sum + l_k, _jax.tree.map(_jnp.add, grad_sum, gw_k)), gx_k

        init = (_jnp.zeros((), _jnp.float32), _jax.tree.map(_jnp.zeros_like, weights))
        (loss, grad_w), grad_x = _jax.lax.scan(body, init, (per_example, given["loss_target"]))
    with _jax.named_scope("update"):
        delta_w, new_m, new_v = {}, {}, {}
        for n in TWIN_WEIGHTS:
            delta_w[n], new_m[n], new_v[n] = _adamw(weights[n], grad_w[n], given["m_" + n], given["v_" + n])
    return (loss, grad_x, *[grad_w[n] for n in TWIN_WEIGHTS], *[delta_w[n] for n in TWIN_WEIGHTS],
            *[new_m[n] for n in TWIN_WEIGHTS], *[new_v[n] for n in TWIN_WEIGHTS])
```

```python
import functools

import jax
import jax.numpy as jnp
from jax import lax
from jax.experimental import pallas as pl
from jax.experimental.pallas import tpu as pltpu

F32 = jnp.float32
BF16 = jnp.bfloat16

D_MODEL = 1024
N_META = 16
BLOCK = 128
PREFIX = BLOCK
N_PAD = PREFIX - N_META
HEAD_DIM = 64
A_HEADS = 8
A_KV_HEADS = 2
A_GROUP = 4
B_HEADS = 8
A_WIDTH = A_HEADS * HEAD_DIM
A_KV_WIDTH = A_KV_HEADS * HEAD_DIM
B_WIDTH = B_HEADS * HEAD_DIM
W_IN_COLS = A_WIDTH + 2 * A_KV_WIDTH + 3 * B_WIDTH + B_HEADS + 2 * D_MODEL
F_COLS = 128
P_COLS = A_WIDTH + 2 * A_KV_WIDTH + 3 * B_WIDTH + 2 * D_MODEL + F_COLS
OFF_QA = 0
OFF_KA = OFF_QA + A_WIDTH
OFF_VA = OFF_KA + A_KV_WIDTH
OFF_QB = OFF_VA + A_KV_WIDTH
OFF_KB = OFF_QB + B_WIDTH
OFF_VB = OFF_KB + B_WIDTH
OFF_GA = OFF_VB + B_WIDTH
OFF_GB = OFF_GA + D_MODEL
OFF_F = OFF_GB + D_MODEL
EPS = 1e-6
NEG = -1e30
SCALE = HEAD_DIM ** -0.5

ADAM_LR = 0.001
ADAM_B1 = 0.9
ADAM_B2 = 0.999
ADAM_EPS = 1e-08
ADAM_WD = 0.01
ADAM_STEP = 10

N_CHIPS = 4
VMEM_LIMIT = 56 * 1024 * 1024

NT_DIMS = (((1,), (1,)), ((), ()))
TN_DIMS = (((0,), (0,)), ((), ()))


def _tile(n, target, mult=16):
    best = None
    for t in range(mult, min(n, target) + 1, mult):
        if n % t == 0:
            best = t
    return best if best is not None else n


def _cparams(sem):
    return pltpu.CompilerParams(dimension_semantics=sem, vmem_limit_bytes=VMEM_LIMIT)


def _rms_scale(h):
    return lax.rsqrt(jnp.mean(h * h, axis=-1, keepdims=True) + EPS)


def _rms_bwd(dn, h, w):
    r = _rms_scale(h)
    dw = jnp.sum(dn * (h * r), axis=0, keepdims=True)
    z = dn * w
    dh = r * z - h * ((r * r * r) * jnp.mean(z * h, axis=-1, keepdims=True))
    return dh, dw


def _ffn_fwd(h, norm_w, w_in, w_out):
    t, d = h.shape
    f = w_out.shape[0]
    tm = _tile(t, 544)
    tc = _tile(f, 256, 128)
    nj = f // tc

    def body(h_ref, nw_ref, wg_ref, wu_ref, wo_ref, hout_ref, g_ref, u_ref, n_scr, acc_scr):
        j = pl.program_id(1)

        @pl.when(j == 0)
        def _():
            hh = h_ref[...]
            n_scr[...] = ((hh * _rms_scale(hh)) * nw_ref[...]).astype(BF16)
            acc_scr[...] = jnp.zeros_like(acc_scr)

        n = n_scr[...]
        g = jnp.dot(n, wg_ref[...], preferred_element_type=F32)
        u = jnp.dot(n, wu_ref[...], preferred_element_type=F32)
        g_ref[...] = g
        u_ref[...] = u
        a = (g * jax.nn.sigmoid(g)) * u
        acc_scr[...] += jnp.dot(a.astype(BF16), wo_ref[...], preferred_element_type=F32)

        @pl.when(j == nj - 1)
        def _():
            hout_ref[...] = h_ref[...] + 0.5 * acc_scr[...]

    return pl.pallas_call(
        body,
        name="ffn_fwd",
        grid=(t // tm, nj),
        in_specs=[
            pl.BlockSpec((tm, d), lambda i, j: (i, 0)),
            pl.BlockSpec((1, d), lambda i, j: (0, 0)),
            pl.BlockSpec((d, tc), lambda i, j: (0, j)),
            pl.BlockSpec((d, tc), lambda i, j: (0, j + nj)),
            pl.BlockSpec((tc, d), lambda i, j: (j, 0)),
        ],
        out_specs=[
            pl.BlockSpec((tm, d), lambda i, j: (i, 0)),
            pl.BlockSpec((tm, tc), lambda i, j: (i, j)),
            pl.BlockSpec((tm, tc), lambda i, j: (i, j)),
        ],
        out_shape=[
            jax.ShapeDtypeStruct((t, d), F32),
            jax.ShapeDtypeStruct((t, f), F32),
            jax.ShapeDtypeStruct((t, f), F32),
        ],
        scratch_shapes=[pltpu.VMEM((tm, d), BF16), pltpu.VMEM((tm, d), F32)],
        compiler_params=_cparams(("parallel", "arbitrary")),
    )(h, norm_w, w_in, w_in, w_out)


def _ffn_bwd(dh_out, h, norm_w, g, u, w_in, w_out):
    t, d = h.shape
    f = w_out.shape[0]
    tm = _tile(t, 544)
    tc = _tile(f, 256, 128)
    nj = f // tc
    ni = t // tm

    def body(dho_ref, h_ref, nw_ref, g_ref, u_ref, wg_ref, wu_ref, wo_ref,
             dhin_ref, n_ref, a_ref, dg_ref, du_ref, df_ref, dnw_ref, dn_scr):
        j = pl.program_id(1)

        @pl.when(j == 0)
        def _():
            hh = h_ref[...]
            n_ref[...] = ((hh * _rms_scale(hh)) * nw_ref[...]).astype(BF16)
            df_ref[...] = (0.5 * dho_ref[...]).astype(BF16)
            dn_scr[...] = jnp.zeros_like(dn_scr)

        da = lax.dot_general(df_ref[...], wo_ref[...], NT_DIMS, preferred_element_type=F32)
        gg = g_ref[...]
        uu = u_ref[...]
        sig = jax.nn.sigmoid(gg)
        sl = gg * sig
        a_ref[...] = (sl * uu).astype(BF16)
        dg = ((da * uu) * (sig * (1.0 + gg * (1.0 - sig)))).astype(BF16)
        du = (da * sl).astype(BF16)
        dg_ref[...] = dg
        du_ref[...] = du
        dn_scr[...] += (lax.dot_general(dg, wg_ref[...], NT_DIMS, preferred_element_type=F32)
                        + lax.dot_general(du, wu_ref[...], NT_DIMS, preferred_element_type=F32))

        @pl.when(j == nj - 1)
        def _():
            dh, dw = _rms_bwd(dn_scr[...], h_ref[...], nw_ref[...])
            dhin_ref[...] = dho_ref[...] + dh
            dnw_ref[0] = dw

    return pl.pallas_call(
        body,
        name="ffn_bwd",
        grid=(ni, nj),
        in_specs=[
            pl.BlockSpec((tm, d), lambda i, j: (i, 0)),
            pl.BlockSpec((tm, d), lambda i, j: (i, 0)),
            pl.BlockSpec((1, d), lambda i, j: (0, 0)),
            pl.BlockSpec((tm, tc), lambda i, j: (i, j)),
            pl.BlockSpec((tm, tc), lambda i, j: (i, j)),
            pl.BlockSpec((d, tc), lambda i, j: (0, j)),
            pl.BlockSpec((d, tc), lambda i, j: (0, j + nj)),
            pl.BlockSpec((tc, d), lambda i, j: (j, 0)),
        ],
        out_specs=[
            pl.BlockSpec((tm, d), lambda i, j: (i, 0)),
            pl.BlockSpec((tm, d), lambda i, j: (i, 0)),
            pl.BlockSpec((tm, tc), lambda i, j: (i, j)),
            pl.BlockSpec((tm, tc), lambda i, j: (i, j)),
            pl.BlockSpec((tm, tc), lambda i, j: (i, j)),
            pl.BlockSpec((tm, d), lambda i, j: (i, 0)),
            pl.BlockSpec((1, 1, d), lambda i, j: (i, 0, 0)),
        ],
        out_shape=[
            jax.ShapeDtypeStruct((t, d), F32),
            jax.ShapeDtypeStruct((t, d), BF16),
            jax.ShapeDtypeStruct((t, f), BF16),
            jax.ShapeDtypeStruct((t, f), BF16),
            jax.ShapeDtypeStruct((t, f), BF16),
            jax.ShapeDtypeStruct((t, d), BF16),
            jax.ShapeDtypeStruct((ni, 1, d), F32),
        ],
        scratch_shapes=[pltpu.VMEM((tm, d), F32)],
        compiler_params=_cparams(("parallel", "arbitrary")),
    )(dh_out, h, norm_w, g, u, w_in, w_in, w_out)


def _tn_matmul(a, b, name):
    t, k = a.shape
    n = b.shape[1]
    tt = _tile(t, 1088)
    tk = _tile(k, 512, 128)
    tn = _tile(n, 512, 128)
    nt = t // tt

    def body(a_ref, b_ref, o_ref):
        s = pl.program_id(2)
        part = lax.dot_general(a_ref[...], b_ref[...], TN_DIMS, preferred_element_type=F32)

        @pl.when(s == 0)
        def _():
            o_ref[...] = part

        @pl.when(s > 0)
        def _():
            o_ref[...] += part

    return pl.pallas_call(
        body,
        name=name,
        grid=(k // tk, n // tn, nt),
        in_specs=[
            pl.BlockSpec((tt, tk), lambda i, j, s: (s, i)),
            pl.BlockSpec((tt, tn), lambda i, j, s: (s, j)),
        ],
        out_specs=pl.BlockSpec((tk, tn), lambda i, j, s: (i, j)),
        out_shape=jax.ShapeDtypeStruct((k, n), F32),
        compiler_params=_cparams(("parallel", "parallel", "arbitrary")),
    )(a, b)


def _proj_fwd(h, norm_w, w_p):
    t, d = h.shape
    n = w_p.shape[1]
    tm = _tile(t, 544)
    tn = _tile(n, 640, 128)

    def body(h_ref, nw_ref, w_ref, u_ref, p_ref):
        j = pl.program_id(1)

        @pl.when(j == 0)
        def _():
            hh = h_ref[...]
            u_ref[...] = ((hh * _rms_scale(hh)) * nw_ref[...]).astype(BF16)

        p_ref[...] = jnp.dot(u_ref[...], w_ref[...], preferred_element_type=F32)

    return pl.pallas_call(
        body,
        name="proj_fwd",
        grid=(t // tm, n // tn),
        in_specs=[
            pl.BlockSpec((tm, d), lambda i, j: (i, 0)),
            pl.BlockSpec((1, d), lambda i, j: (0, 0)),
            pl.BlockSpec((d, tn), lambda i, j: (0, j)),
        ],
        out_specs=[
            pl.BlockSpec((tm, d), lambda i, j: (i, 0)),
            pl.BlockSpec((tm, tn), lambda i, j: (i, j)),
        ],
        out_shape=[jax.ShapeDtypeStruct((t, d), BF16), jax.ShapeDtypeStruct((t, n), F32)],
        compiler_params=_cparams(("parallel", "arbitrary")),
    )(h, norm_w, w_p)


def _proj_bwd(dh_out, h, norm_w, dproj, w_p):
    t, d = h.shape
    n = w_p.shape[1]
    tm = _tile(t, 544)
    tn = _tile(n, 640, 128)
    nj = n // tn
    ni = t // tm

    def body(dho_ref, h_ref, nw_ref, dp_ref, w_ref, dhin_ref, dnw_ref, dn_scr):
        j = pl.program_id(1)

        @pl.when(j == 0)
        def _():
            dn_scr[...] = jnp.zeros_like(dn_scr)

        dn_scr[...] += lax.dot_general(dp_ref[...], w_ref[...], NT_DIMS, preferred_element_type=F32)

        @pl.when(j == nj - 1)
        def _():
            dh, dw = _rms_bwd(dn_scr[...], h_ref[...], nw_ref[...])
            dhin_ref[...] = dho_ref[...] + dh
            dnw_ref[0] = dw

    return pl.pallas_call(
        body,
        name="proj_bwd",
        grid=(ni, nj),
        in_specs=[
            pl.BlockSpec((tm, d), lambda i, j: (i, 0)),
            pl.BlockSpec((tm, d), lambda i, j: (i, 0)),
            pl.BlockSpec((1, d), lambda i, j: (0, 0)),
            pl.BlockSpec((tm, tn), lambda i, j: (i, j)),
            pl.BlockSpec((d, tn), lambda i, j: (0, j)),
        ],
        out_specs=[
            pl.BlockSpec((tm, d), lambda i, j: (i, 0)),
            pl.BlockSpec((1, 1, d), lambda i, j: (i, 0, 0)),
        ],
        out_shape=[jax.ShapeDtypeStruct((t, d), F32), jax.ShapeDtypeStruct((ni, 1, d), F32)],
        scratch_shapes=[pltpu.VMEM((tm, d), F32)],
        compiler_params=_cparams(("parallel", "arbitrary")),
    )(dh_out, h, norm_w, dproj, w_p)


def _merge_fwd(h, oa, ob, ga, gb, wa, wb, wo):
    t, d = h.shape
    tm = _tile(t, 544)

    def body(h_ref, oa_ref, ob_ref, ga_ref, gb_ref, wa_ref, wb_ref, wo_ref, hout_ref, mix_ref):
        ya = jnp.dot(oa_ref[...].astype(BF16), wa_ref[...], preferred_element_type=F32)
        yb = jnp.dot(ob_ref[...].astype(BF16), wb_ref[...], preferred_element_type=F32)
        mixed = (jax.nn.sigmoid(ga_ref[...]) * ya + jax.nn.sigmoid(gb_ref[...]) * yb).astype(BF16)
        mix_ref[...] = mixed
        hout_ref[...] = h_ref[...] + jnp.dot(mixed, wo_ref[...], preferred_element_type=F32)

    row = lambda w: pl.BlockSpec((tm, w), lambda i: (i, 0))
    full = lambda a: pl.BlockSpec(a.shape, lambda i: (0, 0))
    return pl.pallas_call(
        body,
        name="merge_fwd",
        grid=(t // tm,),
        in_specs=[row(d), row(A_WIDTH), row(B_WIDTH), row(d), row(d), full(wa), full(wb), full(wo)],
        out_specs=[row(d), row(d)],
        out_shape=[jax.ShapeDtypeStruct((t, d), F32), jax.ShapeDtypeStruct((t, d), BF16)],
        compiler_params=_cparams(("parallel",)),
    )(h, oa, ob, ga, gb, wa, wb, wo)


def _merge_bwd(dh, oa, ob, ga, gb, wa, wb, wo):
    t, d = dh.shape
    tm = _tile(t, 544)

    def body(dh_ref, oa_ref, ob_ref, ga_ref, gb_ref, wa_ref, wb_ref, wo_ref,
             dya_ref, dyb_ref, doa_ref, dob_ref, dga_ref, dgb_ref, dhb_ref):
        dhb = dh_ref[...].astype(BF16)
        dhb_ref[...] = dhb
        dmix = lax.dot_general(dhb, wo_ref[...], NT_DIMS, preferred_element_type=F32)
        for o_ref, g_ref, w_ref, dy_ref, do_ref, dg_ref in (
                (oa_ref, ga_ref, wa_ref, dya_ref, doa_ref, dga_ref),
                (ob_ref, gb_ref, wb_ref, dyb_ref, dob_ref, dgb_ref)):
            y = jnp.dot(o_ref[...].astype(BF16), w_ref[...], preferred_element_type=F32)
            s = jax.nn.sigmoid(g_ref[...])
            dy = (dmix * s).astype(BF16)
            dy_ref[...] = dy
            dg_ref[...] = ((dmix * y) * (s * (1.0 - s))).astype(BF16)
            do_ref[...] = lax.dot_general(dy, w_ref[...], NT_DIMS, preferred_element_type=F32)

    row = lambda w: pl.BlockSpec((tm, w), lambda i: (i, 0))
    full = lambda a: pl.BlockSpec(a.shape, lambda i: (0, 0))
    return pl.pallas_call(
        body,
        name="merge_bwd",
        grid=(t // tm,),
        in_specs=[row(d), row(A_WIDTH), row(B_WIDTH), row(d), row(d), full(wa), full(wb), full(wo)],
        out_specs=[row(d), row(d), row(A_WIDTH), row(B_WIDTH), row(d), row(d), row(d)],
        out_shape=[
            jax.ShapeDtypeStruct((t, d), BF16), jax.ShapeDtypeStruct((t, d), BF16),
            jax.ShapeDtypeStruct((t, A_WIDTH), F32), jax.ShapeDtypeStruct((t, B_WIDTH), F32),
            jax.ShapeDtypeStruct((t, d), BF16), jax.ShapeDtypeStruct((t, d), BF16),
            jax.ShapeDtypeStruct((t, d), BF16),
        ],
        compiler_params=_cparams(("parallel",)),
    )(dh, oa, ob, ga, gb, wa, wb, wo)


def _tri_dot(tri, x):
    hi = x.astype(BF16)
    r1 = x - hi.astype(F32)
    mid = r1.astype(BF16)
    lo = (r1 - mid.astype(F32)).astype(BF16)
    return (jnp.dot(tri, hi, preferred_element_type=F32)
            + jnp.dot(tri, mid, preferred_element_type=F32)
            + jnp.dot(tri, lo, preferred_element_type=F32))


def _forget_cumsum(f_logit, b_pad, nb):
    t, w = f_logit.shape
    bsz = t // (nb * BLOCK)

    def body(f_ref, b_ref, c_ref, carry):
        n = pl.program_id(1)

        @pl.when(n == 0)
        def _():
            carry[...] = jnp.zeros_like(carry)

        x = jax.nn.log_sigmoid(f_ref[...] + b_ref[...])
        rows = lax.broadcasted_iota(jnp.int32, (BLOCK, BLOCK), 0)
        cols = lax.broadcasted_iota(jnp.int32, (BLOCK, BLOCK), 1)
        tri = (cols <= rows).astype(BF16)
        c = _tri_dot(tri, x) + carry[...]
        c_ref[...] = c
        carry[...] = c[BLOCK - 1:BLOCK, :]

    return pl.pallas_call(
        body,
        name="forget_cumsum",
        grid=(bsz, nb),
        in_specs=[pl.BlockSpec((BLOCK, w), lambda b, n: (b * nb + n, 0)),
                  pl.BlockSpec((1, w), lambda b, n: (0, 0))],
        out_specs=pl.BlockSpec((BLOCK, w), lambda b, n: (b * nb + n, 0)),
        out_shape=jax.ShapeDtypeStruct((t, w), F32),
        scratch_shapes=[pltpu.VMEM((1, w), F32)],
        compiler_params=_cparams(("parallel", "arbitrary")),
    )(f_logit, b_pad)


def _forget_cumsum_bwd(dc, f_logit, b_pad, nb):
    t, w = f_logit.shape
    bsz = t // (nb * BLOCK)

    def body(dc_ref, f_ref, b_ref, df_ref, db_ref, carry):
        n = pl.program_id(1)

        @pl.when(n == 0)
        def _():
            carry[...] = jnp.zeros_like(carry)
            db_ref[...] = jnp.zeros_like(db_ref)

        rows = lax.broadcasted_iota(jnp.int32, (BLOCK, BLOCK), 0)
        cols = lax.broadcasted_iota(jnp.int32, (BLOCK, BLOCK), 1)
        tri = (cols >= rows).astype(BF16)
        dlf = _tri_dot(tri, dc_ref[...]) + carry[...]
        carry[...] = dlf[0:1, :]
        df = dlf * jax.nn.sigmoid(-(f_ref[...] + b_ref[...]))
        df_ref[...] = df
        db_ref[0] += jnp.sum(df, axis=0, keepdims=True)

    rev = lambda b, n: (b * nb + (nb - 1 - n), 0)
    return pl.pallas_call(
        body,
        name="forget_cumsum_bwd",
        grid=(bsz, nb),
        in_specs=[pl.BlockSpec((BLOCK, w), rev),
                  pl.BlockSpec((BLOCK, w), rev),
                  pl.BlockSpec((1, w), lambda b, n: (0, 0))],
        out_specs=[pl.BlockSpec((BLOCK, w), rev),
                   pl.BlockSpec((1, 1, w), lambda b, n: (b, 0, 0))],
        out_shape=[jax.ShapeDtypeStruct((t, w), F32), jax.ShapeDtypeStruct((bsz, 1, w), F32)],
        scratch_shapes=[pltpu.VMEM((1, w), F32)],
        compiler_params=_cparams(("parallel", "arbitrary")),
    )(dc, f_logit, b_pad)


GROUP_ROWS = A_GROUP * BLOCK


def _swa_logits(q, km, kp, kc, slope, n):
    qi = lax.broadcasted_iota(jnp.int32, (GROUP_ROWS, BLOCK), 0) & (BLOCK - 1)
    kj = lax.broadcasted_iota(jnp.int32, (GROUP_ROWS, BLOCK), 1)
    out = []
    for kk, dist, ok in (
            (km, n * BLOCK + qi - kj, (kj >= N_PAD) & (n * BLOCK + qi - kj >= 0)),
            (kp, BLOCK + qi - kj, (kj > qi) & (n >= 2)),
            (kc, qi - kj, (kj <= qi) & (n >= 1))):
        s = lax.dot_general(q, kk, NT_DIMS, preferred_element_type=F32) * SCALE
        s = s - slope * dist.astype(F32)
        out.append(jnp.where(ok, s, NEG))
    return out


def _swa_fwd(q, k, v, sink_rows, slope_rows):
    bsz, _, l, dh = q.shape
    nb = l // BLOCK

    def body(q_ref, km_ref, kp_ref, kc_ref, vm_ref, vp_ref, vc_ref, sink_ref, slope_ref, o_ref, lse_ref):
        n = pl.program_id(2)
        qq = q_ref[0].reshape(GROUP_ROWS, dh)
        sink = sink_ref[0]
        s_m, s_p, s_c = _swa_logits(qq, km_ref[0, 0], kp_ref[0, 0], kc_ref[0, 0], slope_ref[0], n)
        m = jnp.maximum(jnp.maximum(jnp.max(s_m, axis=-1, keepdims=True), jnp.max(s_p, axis=-1, keepdims=True)),
                        jnp.maximum(jnp.max(s_c, axis=-1, keepdims=True), sink))
        e_m = jnp.exp(s_m - m)
        e_p = jnp.exp(s_p - m)
        e_c = jnp.exp(s_c - m)
        z = (jnp.sum(e_m, axis=-1, keepdims=True) + jnp.sum(e_p, axis=-1, keepdims=True)
             + jnp.sum(e_c, axis=-1, keepdims=True) + jnp.exp(sink - m))
        inv = 1.0 / z
        o = (jnp.dot((e_m * inv).astype(BF16), vm_ref[0, 0], preferred_element_type=F32)
             + jnp.dot((e_p * inv).astype(BF16), vp_ref[0, 0], preferred_element_type=F32)
             + jnp.dot((e_c * inv).astype(BF16), vc_ref[0, 0], preferred_element_type=F32))
        o_ref[0] = o.reshape(A_GROUP, BLOCK, dh)
        lse_ref[0] = (m + jnp.log(z)).reshape(A_GROUP, BLOCK, 1)

    qspec = pl.BlockSpec((1, A_GROUP, BLOCK, dh), lambda b, g, n: (b, g, n, 0))
    kv_m = pl.BlockSpec((1, 1, BLOCK, dh), lambda b, g, n: (b, g, 0, 0))
    kv_p = pl.BlockSpec((1, 1, BLOCK, dh), lambda b, g, n: (b, g, jnp.maximum(n - 1, 0), 0))
    kv_c = pl.BlockSpec((1, 1, BLOCK, dh), lambda b, g, n: (b, g, n, 0))
    rowspec = pl.BlockSpec((1, GROUP_ROWS, 1), lambda b, g, n: (g, 0, 0))
    return pl.pallas_call(
        body,
        name="swa_fwd",
        grid=(bsz, A_KV_HEADS, nb),
        in_specs=[qspec, kv_m, kv_p, kv_c, kv_m, kv_p, kv_c, rowspec, rowspec],
        out_specs=[qspec, pl.BlockSpec((1, A_GROUP, BLOCK, 1), lambda b, g, n: (b, g, n, 0))],
        out_shape=[jax.ShapeDtypeStruct((bsz, A_HEADS, l, dh), F32),
                   jax.ShapeDtypeStruct((bsz, A_HEADS, l, 1), F32)],
        compiler_params=_cparams(("parallel", "parallel", "arbitrary")),
    )(q, k, k, k, v, v, v, sink_rows, slope_rows)


def _swa_bwd(q, k, v, do, lse, sink_rows, slope_rows):
    bsz, _, l, dh = q.shape
    nb = l // BLOCK

    def body(q_ref, km_ref, kp_ref, kc_ref, vm_ref, vp_ref, vc_ref, do_ref, lse_ref, sink_ref, slope_ref,
             dq_ref, dk_ref, dv_ref, dsink_ref):
        n = pl.program_id(2)

        @pl.when(n == 0)
        def _():
            dk_ref[...] = jnp.zeros_like(dk_ref)
            dv_ref[...] = jnp.zeros_like(dv_ref)
            dsink_ref[...] = jnp.zeros_like(dsink_ref)

        qq = q_ref[0].reshape(GROUP_ROWS, dh)
        dob = do_ref[0].reshape(GROUP_ROWS, dh).astype(BF16)
        lse = lse_ref[0].reshape(GROUP_ROWS, 1)
        logits = _swa_logits(qq, km_ref[0, 0], kp_ref[0, 0], kc_ref[0, 0], slope_ref[0], n)
        probs = [jnp.exp(s - lse) for s in logits]
        dps = [lax.dot_general(dob, v_ref[0, 0], NT_DIMS, preferred_element_type=F32)
               for v_ref in (vm_ref, vp_ref, vc_ref)]
        delta = sum(jnp.sum(p * dp, axis=-1, keepdims=True) for p, dp in zip(probs, dps))
        prev = jnp.maximum(n - 1, 0)
        dq = jnp.zeros((GROUP_ROWS, dh), F32)
        for p, dp, k_ref, start in ((probs[0], dps[0], km_ref, 0),
                                    (probs[1], dps[1], kp_ref, prev * BLOCK),
                                    (probs[2], dps[2], kc_ref, n * BLOCK)):
            ds = (p * (dp - delta)).astype(BF16)
            dq = dq + jnp.dot(ds, k_ref[0, 0], preferred_element_type=F32)
            rows = pl.ds(pl.multiple_of(start, BLOCK), BLOCK)
            dk_ref[0, 0, rows, :] += lax.dot_general(ds, qq, TN_DIMS, preferred_element_type=F32) * SCALE
            dv_ref[0, 0, rows, :] += lax.dot_general(p.astype(BF16), dob, TN_DIMS, preferred_element_type=F32)
        dq_ref[0] = (dq * SCALE).reshape(A_GROUP, BLOCK, dh)
        dsink_ref[0, 0] += -(jnp.exp(sink_ref[0] - lse) * delta)

    qspec = pl.BlockSpec((1, A_GROUP, BLOCK, dh), lambda b, g, n: (b, g, n, 0))
    kv_m = pl.BlockSpec((1, 1, BLOCK, dh), lambda b, g, n: (b, g, 0, 0))
    kv_p = pl.BlockSpec((1, 1, BLOCK, dh), lambda b, g, n: (b, g, jnp.maximum(n - 1, 0), 0))
    kv_c = pl.BlockSpec((1, 1, BLOCK, dh), lambda b, g, n: (b, g, n, 0))
    kv_all = pl.BlockSpec((1, 1, l, dh), lambda b, g, n: (b, g, 0, 0))
    rowspec = pl.BlockSpec((1, GROUP_ROWS, 1), lambda b, g, n: (g, 0, 0))
    lsespec = pl.BlockSpec((1, A_GROUP, BLOCK, 1), lambda b, g, n: (b, g, n, 0))
    return pl.pallas_call(
        body,
        name="swa_bwd",
        grid=(bsz, A_KV_HEADS, nb),
        in_specs=[qspec, kv_m, kv_p, kv_c, kv_m, kv_p, kv_c, qspec, lsespec, rowspec, rowspec],
        out_specs=[qspec, kv_all, kv_all,
                   pl.BlockSpec((1, 1, GROUP_ROWS, 1), lambda b, g, n: (b, g, 0, 0))],
        out_shape=[jax.ShapeDtypeStruct((bsz, A_HEADS, l, dh), F32),
                   jax.ShapeDtypeStruct((bsz, A_KV_HEADS, l, dh), F32),
                   jax.ShapeDtypeStruct((bsz, A_KV_HEADS, l, dh), F32),
                   jax.ShapeDtypeStruct((bsz, A_KV_HEADS, GROUP_ROWS, 1), F32)],
        compiler_params=_cparams(("parallel", "parallel", "arbitrary")),
    )(q, k, k, k, v, v, v, do, lse, sink_rows, slope_rows)


def _fox_logits(q, kk, c_t, c_s, qb, kb):
    s = lax.dot_general(q, kk, NT_DIMS, preferred_element_type=F32) * SCALE
    s = s + c_t - c_s
    q_pos = qb * BLOCK + lax.broadcasted_iota(jnp.int32, (BLOCK, BLOCK), 0)
    k_pos = kb * BLOCK + lax.broadcasted_iota(jnp.int32, (BLOCK, BLOCK), 1)
    return jnp.where((k_pos <= q_pos) & (k_pos >= N_PAD), s, NEG)


def _fox_fwd(q, k, v, c_col, c_row):
    bh, l, dh = q.shape
    nb = l // BLOCK

    def body(q_ref, k_ref, v_ref, cc_ref, cr_ref, o_ref, ox_ref, lse_ref):
        qb = pl.program_id(1)
        qq = q_ref[0]
        c_t = cc_ref[0]

        def step(kb, carry):
            m, z, acc, acc_lo = carry
            rows = pl.ds(pl.multiple_of(kb * BLOCK, BLOCK), BLOCK)
            s = _fox_logits(qq, k_ref[0, rows, :], c_t, cr_ref[0, kb], qb, kb)
            m_new = jnp.maximum(m, jnp.max(s, axis=-1, keepdims=True))
            alpha = jnp.exp(m - m_new)
            p = jnp.exp(s - m_new)
            z = alpha * z + jnp.sum(p, axis=-1, keepdims=True)
            p_hi = p.astype(BF16)
            p_lo = (p - p_hi.astype(F32)).astype(BF16)
            vv = v_ref[0, rows, :]
            acc = alpha * acc + jnp.dot(p_hi, vv, preferred_element_type=F32)
            acc_lo = alpha * acc_lo + jnp.dot(p_lo, vv, preferred_element_type=F32)
            return m_new, z, acc, acc_lo

        m, z, acc, acc_lo = lax.fori_loop(
            0, qb + 1, step,
            (jnp.full((BLOCK, 1), NEG, F32), jnp.zeros((BLOCK, 1), F32), jnp.zeros((BLOCK, dh), F32),
             jnp.zeros((BLOCK, dh), F32)))
        o_ref[0] = acc / z
        ox_ref[0] = (acc + acc_lo) / z
        lse_ref[0] = m + jnp.log(z)

    return pl.pallas_call(
        body,
        name="fox_fwd",
        grid=(bh, nb),
        in_specs=[
            pl.BlockSpec((1, BLOCK, dh), lambda h, i: (h, i, 0)),
            pl.BlockSpec((1, l, dh), lambda h, i: (h, 0, 0)),
            pl.BlockSpec((1, l, dh), lambda h, i: (h, 0, 0)),
            pl.BlockSpec((1, BLOCK, 1), lambda h, i: (h, i, 0)),
            pl.BlockSpec((1, nb, 1, BLOCK), lambda h, i: (h, 0, 0, 0)),
        ],
        out_specs=[pl.BlockSpec((1, BLOCK, dh), lambda h, i: (h, i, 0)),
                   pl.BlockSpec((1, BLOCK, dh), lambda h, i: (h, i, 0)),
                   pl.BlockSpec((1, BLOCK, 1), lambda h, i: (h, i, 0))],
        out_shape=[jax.ShapeDtypeStruct((bh, l, dh), F32), jax.ShapeDtypeStruct((bh, l, dh), F32),
                   jax.ShapeDtypeStruct((bh, l, 1), F32)],
        compiler_params=_cparams(("parallel", "arbitrary")),
    )(q, k, v, c_col, c_row)


def _fox_bwd(q, k, v, o, do, lse, c_col, c_row):
    bh, l, dh = q.shape
    nb = l // BLOCK

    def body(q_ref, k_ref, v_ref, o_ref, do_ref, lse_ref, cc_ref, cr_ref, dq_ref, dk_ref, dv_ref, dc_ref):
        dq_ref[...] = jnp.zeros_like(dq_ref)

        def key_block(kb, _):
            krows = pl.ds(pl.multiple_of(kb * BLOCK, BLOCK), BLOCK)
            kk = k_ref[0, krows, :]
            vv = v_ref[0, krows, :]
            c_s = cr_ref[0, kb]

            def query_block(qb, carry):
                dk, dv, dcs = carry
                qrows = pl.ds(pl.multiple_of(qb * BLOCK, BLOCK), BLOCK)
                qq = q_ref[0, qrows, :]
                dof = do_ref[0, qrows, :]
                dob = dof.astype(BF16)
                delta = jnp.sum(dob.astype(F32) * o_ref[0, qrows, :], axis=-1, keepdims=True)
                s = _fox_logits(qq, kk, cc_ref[0, qrows, :], c_s, qb, kb)
                p = jnp.exp(s - lse_ref[0, qrows, :])
                dp = lax.dot_general(dob, vv, NT_DIMS, preferred_element_type=F32)
                ds = p * (dp - delta)
                dsb = ds.astype(BF16)
                dq_ref[0, qrows, :] += jnp.dot(dsb, kk, preferred_element_type=F32) * SCALE
                dk = dk + lax.dot_general(dsb, qq, TN_DIMS, preferred_element_type=F32)
                dv = dv + lax.dot_general(p.astype(BF16), dob, TN_DIMS, preferred_element_type=F32)
                dcs = dcs + jnp.sum(ds, axis=0, keepdims=True)
                return dk, dv, dcs

            dk, dv, dcs = lax.fori_loop(
                kb, nb, query_block,
                (jnp.zeros((BLOCK, dh), F32), jnp.zeros((BLOCK, dh), F32), jnp.zeros((1, BLOCK), F32)))
            dk_ref[0, krows, :] = dk * SCALE
            dv_ref[0, krows, :] = dv
            dc_ref[0, kb] = -dcs
            return 0

        lax.fori_loop(0, nb, key_block, 0)

    full = lambda w: pl.BlockSpec((1, l, w), lambda h: (h, 0, 0))
    rowv = pl.BlockSpec((1, nb, 1, BLOCK), lambda h: (h, 0, 0, 0))
    return pl.pallas_call(
        body,
        name="fox_bwd",
        grid=(bh,),
        in_specs=[full(dh), full(dh), full(dh), full(dh), full(dh), full(1), full(1), rowv],
        out_specs=[full(dh), full(dh), full(dh), rowv],
        out_shape=[jax.ShapeDtypeStruct((bh, l, dh), F32), jax.ShapeDtypeStruct((bh, l, dh), F32),
                   jax.ShapeDtypeStruct((bh, l, dh), F32), jax.ShapeDtypeStruct((bh, nb, 1, BLOCK), F32)],
        compiler_params=_cparams(("parallel",)),
    )(q, k, v, o, do, lse, c_col, c_row)


def _loss_head(h, final_w, target):
    bsz, l, d = h.shape
    nb = l // BLOCK

    def body(h_ref, w_ref, t_ref, loss_ref, dh_ref, dw_ref):
        b = pl.program_id(0)
        n = pl.program_id(1)

        @pl.when((b == 0) & (n == 0))
        def _():
            loss_ref[...] = jnp.zeros_like(loss_ref)
            dw_ref[...] = jnp.zeros_like(dw_ref)

        @pl.when(n == 0)
        def _():
            dh_ref[...] = jnp.zeros_like(dh_ref)

        @pl.when(n > 0)
        def _():
            hh = h_ref[0]
            w = w_ref[...]
            r = _rms_scale(hh)
            err = (hh * r) * w - t_ref[0]
            loss_ref[...] += 0.5 * jnp.sum(jnp.mean(err * err, axis=-1, keepdims=True), axis=0, keepdims=True)
            dy = err * (1.0 / d)
            dh, dw = _rms_bwd(dy, hh, w)
            dh_ref[0] = dh
            dw_ref[...] += dw

    return pl.pallas_call(
        body,
        name="loss_head",
        grid=(bsz, nb),
        in_specs=[
            pl.BlockSpec((1, BLOCK, d), lambda b, n: (b, n, 0)),
            pl.BlockSpec((1, d), lambda b, n: (0, 0)),
            pl.BlockSpec((1, BLOCK, d), lambda b, n: (b, jnp.maximum(n - 1, 0), 0)),
        ],
        out_specs=[
            pl.BlockSpec((1, 128), lambda b, n: (0, 0)),
            pl.BlockSpec((1, BLOCK, d), lambda b, n: (b, n, 0)),
            pl.BlockSpec((1, d), lambda b, n: (0, 0)),
        ],
        out_shape=[jax.ShapeDtypeStruct((1, 128), F32), jax.ShapeDtypeStruct((bsz, l, d), F32),
                   jax.ShapeDtypeStruct((1, d), F32)],
        compiler_params=_cparams(("arbitrary", "arbitrary")),
    )(h, final_w, target)


def _reorder_w_in(w):
    o_f = A_WIDTH + 2 * A_KV_WIDTH + 3 * B_WIDTH
    pad = jnp.zeros((w.shape[0], F_COLS - B_HEADS), w.dtype)
    return jnp.concatenate([w[:, :o_f], w[:, o_f + B_HEADS:], w[:, o_f:o_f + B_HEADS], pad], axis=1)


def _restore_w_in(g):
    o_f = A_WIDTH + 2 * A_KV_WIDTH + 3 * B_WIDTH
    return jnp.concatenate([g[:, :o_f], g[:, OFF_F:OFF_F + B_HEADS], g[:, o_f:OFF_F]], axis=1)


def _to_heads(x, bsz, l, heads):
    return x.reshape(bsz, l, heads, HEAD_DIM).transpose(0, 2, 1, 3)


def _from_heads(x):
    b, h, l, dh = x.shape
    return x.transpose(0, 2, 1, 3).reshape(b * l, h * dh)


def _local_step(x, target, meta, norms, b_forget, sinks, w):
    n1, nmix, n2, nfin = norms
    w1i, w1o, wp, wa, wb, wo, w2i, w2o = w
    bsz, seq, d = x.shape
    l = PREFIX + seq
    nb = l // BLOCK
    t = bsz * l

    h0 = jnp.concatenate([jnp.zeros((bsz, N_PAD, d), F32),
                          jnp.broadcast_to(meta[None], (bsz, N_META, d)), x], axis=1).reshape(t, d)

    h1, g1, u1 = _ffn_fwd(h0, n1, w1i, w1o)
    un, proj = _proj_fwd(h1, nmix, wp)
    qa = _to_heads(proj[:, OFF_QA:OFF_KA].astype(BF16), bsz, l, A_HEADS)
    ka = _to_heads(proj[:, OFF_KA:OFF_VA].astype(BF16), bsz, l, A_KV_HEADS)
    va = _to_heads(proj[:, OFF_VA:OFF_QB].astype(BF16), bsz, l, A_KV_HEADS)
    qb = _to_heads(proj[:, OFF_QB:OFF_KB].astype(BF16), bsz, l, B_HEADS).reshape(bsz * B_HEADS, l, HEAD_DIM)
    kb = _to_heads(proj[:, OFF_KB:OFF_VB].astype(BF16), bsz, l, B_HEADS).reshape(bsz * B_HEADS, l, HEAD_DIM)
    vb = _to_heads(proj[:, OFF_VB:OFF_GA].astype(BF16), bsz, l, B_HEADS).reshape(bsz * B_HEADS, l, HEAD_DIM)
    ga = proj[:, OFF_GA:OFF_GB]
    gb = proj[:, OFF_GB:OFF_F]
    f_logit = proj[:, OFF_F:]
    b_pad = jnp.concatenate([b_forget, jnp.zeros((1, F_COLS - B_HEADS), F32)], axis=1)

    c = _forget_cumsum(f_logit, b_pad, nb)
    c_heads = c[:, :B_HEADS].reshape(bsz, l, B_HEADS).transpose(0, 2, 1).reshape(bsz * B_HEADS, l)
    c_col = c_heads[:, :, None]
    c_row = c_heads.reshape(bsz * B_HEADS, nb, 1, BLOCK)

    slopes = jnp.exp2(-8.0 * jnp.arange(1, A_HEADS + 1, dtype=F32) / A_HEADS)
    slope_rows = jnp.repeat(slopes.reshape(A_KV_HEADS, A_GROUP), BLOCK, axis=1)[:, :, None]
    sink_rows = jnp.repeat(sinks.reshape(A_KV_HEADS, A_GROUP), BLOCK, axis=1)[:, :, None]

    oa_h, lse_a = _swa_fwd(qa, ka, va, sink_rows, slope_rows)
    ob_h, ob_exact, lse_b = _fox_fwd(qb, kb, vb, c_col, c_row)
    oa = _from_heads(oa_h)
    ob = _from_heads(ob_h.reshape(bsz, B_HEADS, l, HEAD_DIM))
    h2, mixed = _merge_fwd(h1, oa, ob, ga, gb, wa, wb, wo)
    h3, g2, u2 = _ffn_fwd(h2, n2, w2i, w2o)
    loss, dh3, d_nfin = _loss_head(h3.reshape(bsz, l, d), nfin, target)

    dh2, n2b, a2, dg2, du2, df2, dn2_parts = _ffn_bwd(dh3.reshape(t, d), h2, n2, g2, u2, w2i, w2o)
    g_w2o = _tn_matmul(a2, df2, "grad_ffn2_w_out")
    g_w2i = jnp.concatenate([_tn_matmul(n2b, dg2, "grad_ffn2_w_in_gate"),
                             _tn_matmul(n2b, du2, "grad_ffn2_w_in_up")], axis=1)

    dya, dyb, doa, dob, dga, dgb, dh2b = _merge_bwd(dh2, oa, ob, ga, gb, wa, wb, wo)
    g_wo = _tn_matmul(mixed, dh2b, "grad_w_out")
    g_wa = _tn_matmul(oa.astype(BF16), dya, "grad_w_branch_a")
    g_wb = _tn_matmul(ob.astype(BF16), dyb, "grad_w_branch_b")

    dqa, dka, dva, dsink_rows = _swa_bwd(qa, ka, va, _to_heads(doa, bsz, l, A_HEADS), lse_a,
                                         sink_rows, slope_rows)
    dob_h = _to_heads(dob, bsz, l, B_HEADS).reshape(bsz * B_HEADS, l, HEAD_DIM)
    dqb, dkb, dvb, dc_row = _fox_bwd(qb, kb, vb, ob_exact, dob_h, lse_b, c_col, c_row)
    dc = dc_row.reshape(bsz, B_HEADS, l).transpose(0, 2, 1).reshape(t, B_HEADS)
    dc = jnp.concatenate([dc, jnp.zeros((t, F_COLS - B_HEADS), F32)], axis=1)
    df_logit, db_parts = _forget_cumsum_bwd(dc, f_logit, b_pad, nb)

    unh = lambda z: _from_heads(z.reshape(bsz, -1, l, HEAD_DIM)).astype(BF16)
    dproj = jnp.concatenate([unh(dqa), unh(dka), unh(dva), unh(dqb), unh(dkb), unh(dvb), dga, dgb,
                             df_logit.astype(BF16)], axis=1)
    dh1, dnmix_parts = _proj_bwd(dh2, h1, nmix, dproj, wp)
    g_wp = _restore_w_in(_tn_matmul(un, dproj, "grad_w_in"))

    dh0, n1b, a1, dg1, du1, df1, dn1_parts = _ffn_bwd(dh1, h0, n1, g1, u1, w1i, w1o)
    g_w1o = _tn_matmul(a1, df1, "grad_ffn1_w_out")
    g_w1i = jnp.concatenate([_tn_matmul(n1b, dg1, "grad_ffn1_w_in_gate"),
                             _tn_matmul(n1b, du1, "grad_ffn1_w_in_up")], axis=1)

    dh0 = dh0.reshape(bsz, l, d)
    grad_x = dh0[:, PREFIX:]
    small = dict(
        meta_tokens=jnp.sum(dh0[:, N_PAD:PREFIX], axis=0),
        ffn1_norm=jnp.sum(dn1_parts, axis=0),
        mix_norm=jnp.sum(dnmix_parts, axis=0),
        ffn2_norm=jnp.sum(dn2_parts, axis=0),
        final_norm=d_nfin,
        b_forget=jnp.sum(db_parts, axis=0)[:, :B_HEADS],
        attn_sinks=jnp.sum(dsink_rows.reshape(bsz, A_HEADS, BLOCK), axis=(0, 2)).reshape(1, A_HEADS),
    )
    big = dict(ffn1_w_in=g_w1i, ffn1_w_out=g_w1o, w_in=g_wp, w_branch_a=g_wa, w_branch_b=g_wb,
               w_out=g_wo, ffn2_w_in=g_w2i, ffn2_w_out=g_w2o)
    return loss, grad_x, small, big


MESH_ID = pl.DeviceIdType.MESH
HBM_SPEC = pl.BlockSpec(memory_space=pltpu.HBM)
VMEM_SPEC = pl.BlockSpec(memory_space=pltpu.VMEM)
LANES = 128

BIG = (
    ("ffn1_w_in", (D_MODEL, 1408), 1),
    ("ffn1_w_out", (704, D_MODEL), 0),
    ("w_in", (D_MODEL, W_IN_COLS // N_CHIPS), 1),
    ("w_branch_a", (A_WIDTH, D_MODEL // N_CHIPS), 1),
    ("w_branch_b", (B_WIDTH, D_MODEL // N_CHIPS), 1),
    ("w_out", (D_MODEL // N_CHIPS, D_MODEL), 0),
    ("ffn2_w_in", (D_MODEL, 1408), 1),
    ("ffn2_w_out", (704, D_MODEL), 0),
)
PACK_ELEMS = sum(s[0] * s[1] for _, s, _ in BIG)
HALF_ROW_UNIT = 1024
PACK_ROWS = -(-PACK_ELEMS // (LANES * 2 * HALF_ROW_UNIT)) * 2 * HALF_ROW_UNIT
HALF_ROWS = PACK_ROWS // 2


def _coords():
    return lax.axis_index("x"), lax.axis_index("y"), lax.axis_index("c")


def _other_chips(x, y):
    return ((1 - x, y), (x, 1 - y), (1 - x, 1 - y))


def _gather_weights(packed, meta):
    def body(p_ref, m_ref, pout_ref, mout_ref, send_sems, recv_sems, local_sems):
        x, y, c = _coords()
        mine = 2 * x + y
        local = [pltpu.make_async_copy(p_ref, pout_ref.at[mine], local_sems.at[0]),
                 pltpu.make_async_copy(m_ref, mout_ref.at[mine], local_sems.at[1])]
        for cp in local:
            cp.start()

        def copies(j, slot, chip):
            dev = (chip[0], chip[1], c)
            return (pltpu.make_async_remote_copy(p_ref, pout_ref.at[slot], send_sems.at[2 * j], recv_sems.at[2 * j],
                                                 device_id=dev, device_id_type=MESH_ID),
                    pltpu.make_async_remote_copy(m_ref, mout_ref.at[slot], send_sems.at[2 * j + 1],
                                                 recv_sems.at[2 * j + 1], device_id=dev, device_id_type=MESH_ID))

        chips = _other_chips(x, y)
        for j, chip in enumerate(chips):
            for cp in copies(j, mine, chip):
                cp.start()
        for j, chip in enumerate(chips):
            for cp in copies(j, 2 * chip[0] + chip[1], chip):
                cp.wait_recv()
        for j, chip in enumerate(chips):
            for cp in copies(j, mine, chip):
                cp.wait_send()
        for cp in local:
            cp.wait()

    return pl.pallas_call(
        body,
        name="gather_weights",
        in_specs=[HBM_SPEC, HBM_SPEC],
        out_specs=[HBM_SPEC, HBM_SPEC],
        out_shape=[jax.ShapeDtypeStruct((N_CHIPS,) + packed.shape, packed.dtype),
                   jax.ShapeDtypeStruct((N_CHIPS,) + meta.shape, meta.dtype)],
        scratch_shapes=[pltpu.SemaphoreType.DMA((6,)), pltpu.SemaphoreType.DMA((6,)),
                        pltpu.SemaphoreType.DMA((2,))],
    )(packed, meta)


def _exchange_halves(g):
    def body(g_ref, out_ref, send_sem, recv_sem):
        x, y, c = _coords()

        @pl.when(c == 0)
        def _():
            cp = pltpu.make_async_remote_copy(g_ref.at[:, 1], out_ref, send_sem, recv_sem,
                                              device_id=(x, y, 1), device_id_type=MESH_ID)
            cp.start()
            cp.wait()

        @pl.when(c == 1)
        def _():
            cp = pltpu.make_async_remote_copy(g_ref.at[:, 0], out_ref, send_sem, recv_sem,
                                              device_id=(x, y, 0), device_id_type=MESH_ID)
            cp.start()
            cp.wait()

    return pl.pallas_call(
        body,
        name="exchange_halves",
        in_specs=[HBM_SPEC],
        out_specs=HBM_SPEC,
        out_shape=jax.ShapeDtypeStruct((N_CHIPS, HALF_ROWS, LANES), F32),
        scratch_shapes=[pltpu.SemaphoreType.DMA, pltpu.SemaphoreType.DMA],
    )(g)


def _add_sibling(g, recv, c):
    tr = _tile(HALF_ROWS, 2048, 8)

    def body(c_ref, g_ref, r_ref, o_ref):
        o_ref[...] = g_ref[:, 0] + r_ref[...]

    return pl.pallas_call(
        body,
        name="add_sibling",
        grid_spec=pltpu.PrefetchScalarGridSpec(
            num_scalar_prefetch=1,
            grid=(N_CHIPS, HALF_ROWS // tr),
            in_specs=[pl.BlockSpec((1, 1, tr, LANES), lambda k, i, c_ref: (k, c_ref[0], i, 0)),
                      pl.BlockSpec((1, tr, LANES), lambda k, i, c_ref: (k, i, 0))],
            out_specs=pl.BlockSpec((1, tr, LANES), lambda k, i, c_ref: (k, i, 0)),
        ),
        out_shape=jax.ShapeDtypeStruct((N_CHIPS, HALF_ROWS, LANES), F32),
        compiler_params=_cparams(("parallel", "parallel")),
    )(c, g, recv)


def _scatter_chip_sums(p):
    def body(p_ref, out_ref, send_sems, recv_sems):
        x, y, c = _coords()
        chips = _other_chips(x, y)

        def copy(j, chip):
            return pltpu.make_async_remote_copy(p_ref.at[2 * chip[0] + chip[1]], out_ref.at[j], send_sems.at[j],
                                                recv_sems.at[j], device_id=(chip[0], chip[1], c),
                                                device_id_type=MESH_ID)

        for j, chip in enumerate(chips):
            copy(j, chip).start()
        for j, chip in enumerate(chips):
            copy(j, chip).wait()

    return pl.pallas_call(
        body,
        name="scatter_chip_sums",
        in_specs=[HBM_SPEC],
        out_specs=HBM_SPEC,
        out_shape=jax.ShapeDtypeStruct((3, HALF_ROWS, LANES), F32),
        scratch_shapes=[pltpu.SemaphoreType.DMA((3,)), pltpu.SemaphoreType.DMA((3,))],
    )(p)


def _add_chips(p, recv, mine):
    tr = _tile(HALF_ROWS, 2048, 8)

    def body(k_ref, p_ref, r_ref, o_ref):
        o_ref[...] = ((p_ref[0] + r_ref[0]) + r_ref[1]) + r_ref[2]

    return pl.pallas_call(
        body,
        name="add_chips",
        grid_spec=pltpu.PrefetchScalarGridSpec(
            num_scalar_prefetch=1,
            grid=(HALF_ROWS // tr,),
            in_specs=[pl.BlockSpec((1, tr, LANES), lambda i, k_ref: (k_ref[0], i, 0)),
                      pl.BlockSpec((3, tr, LANES), lambda i, k_ref: (0, i, 0))],
            out_specs=pl.BlockSpec((tr, LANES), lambda i, k_ref: (i, 0)),
        ),
        out_shape=jax.ShapeDtypeStruct((HALF_ROWS, LANES), F32),
        compiler_params=_cparams(("parallel",)),
    )(mine, p, recv)


def _share_with_sibling(r):
    def body(r_ref, out_ref, send_sem, recv_sem, local_sem):
        x, y, c = _coords()

        @pl.when(c == 0)
        def _():
            loc = pltpu.make_async_copy(r_ref, out_ref.at[0], local_sem)
            loc.start()
            cp = pltpu.make_async_remote_copy(r_ref, out_ref.at[0], send_sem, recv_sem,
                                              device_id=(x, y, 1), device_id_type=MESH_ID)
            cp.start()
            cp.wait_send()
            pltpu.make_async_remote_copy(r_ref, out_ref.at[1], send_sem, recv_sem,
                                         device_id=(x, y, 1), device_id_type=MESH_ID).wait_recv()
            loc.wait()

        @pl.when(c == 1)
        def _():
            loc = pltpu.make_async_copy(r_ref, out_ref.at[1], local_sem)
            loc.start()
            cp = pltpu.make_async_remote_copy(r_ref, out_ref.at[1], send_sem, recv_sem,
                                              device_id=(x, y, 0), device_id_type=MESH_ID)
            cp.start()
            cp.wait_send()
            pltpu.make_async_remote_copy(r_ref, out_ref.at[0], send_sem, recv_sem,
                                         device_id=(x, y, 0), device_id_type=MESH_ID).wait_recv()
            loc.wait()

    return pl.pallas_call(
        body,
        name="share_with_sibling",
        in_specs=[HBM_SPEC],
        out_specs=HBM_SPEC,
        out_shape=jax.ShapeDtypeStruct((2, HALF_ROWS, LANES), F32),
        scratch_shapes=[pltpu.SemaphoreType.DMA, pltpu.SemaphoreType.DMA, pltpu.SemaphoreType.DMA],
    )(r)


SMALL_ROWS = 168


def _all_reduce_small(buf):
    n_dev = 8

    def body(b_ref, out_ref, gathered, send_sems, recv_sems):
        x, y, c = _coords()
        me = 4 * x + 2 * y + c
        peers = [(x ^ fx, y ^ fy, c ^ fc) for fx in (0, 1) for fy in (0, 1) for fc in (0, 1)][1:]

        def copy(j, slot, dev):
            return pltpu.make_async_remote_copy(b_ref, gathered.at[slot], send_sems.at[j], recv_sems.at[j],
                                                device_id=dev, device_id_type=MESH_ID)

        for j, dev in enumerate(peers):
            copy(j, me, dev).start()
        gathered[me] = b_ref[...]
        for j, dev in enumerate(peers):
            copy(j, 4 * dev[0] + 2 * dev[1] + dev[2], dev).wait()
        acc = gathered[0]
        for d in range(1, n_dev):
            acc = acc + gathered[d]
        out_ref[...] = acc

    return pl.pallas_call(
        body,
        name="all_reduce_small",
        in_specs=[VMEM_SPEC],
        out_specs=VMEM_SPEC,
        out_shape=jax.ShapeDtypeStruct(buf.shape, F32),
        scratch_shapes=[pltpu.VMEM((n_dev,) + buf.shape, F32), pltpu.SemaphoreType.DMA((n_dev - 1,)),
                        pltpu.SemaphoreType.DMA((n_dev - 1,))],
    )(buf)


def _adamw(w, g, m, v):
    r, c = w.shape
    tr = _tile(r, 256, 8)

    def body(w_ref, g_ref, m_ref, v_ref, d_ref, mo_ref, vo_ref):
        gg = g_ref[...]
        mm = ADAM_B1 * m_ref[...] + (1.0 - ADAM_B1) * gg
        vv = ADAM_B2 * v_ref[...] + (1.0 - ADAM_B2) * (gg * gg)
        m_hat = mm / (1.0 - ADAM_B1 ** ADAM_STEP)
        v_hat = vv / (1.0 - ADAM_B2 ** ADAM_STEP)
        d_ref[...] = -ADAM_LR * (m_hat / (jnp.sqrt(v_hat) + ADAM_EPS) + ADAM_WD * w_ref[...])
        mo_ref[...] = mm
        vo_ref[...] = vv

    spec = pl.BlockSpec((tr, c), lambda i: (i, 0))
    return pl.pallas_call(
        body,
        name="adamw",
        grid=(r // tr,),
        in_specs=[spec] * 4,
        out_specs=[spec] * 3,
        out_shape=[jax.ShapeDtypeStruct((r, c), F32)] * 3,
        compiler_params=_cparams(("parallel",)),
    )(w, g, m, v)


def _pack(shards, dtype):
    flat = jnp.concatenate([s.astype(dtype).reshape(-1) for s in shards]
                           + [jnp.zeros((PACK_ROWS * LANES - PACK_ELEMS,), dtype)])
    return flat.reshape(PACK_ROWS, LANES)


def _unpack_full(stacked):
    flat = stacked.reshape(N_CHIPS, PACK_ROWS * LANES)
    out, off = [], 0
    for _, (r, c), axis in BIG:
        part = flat[:, off:off + r * c].reshape(N_CHIPS, r, c)
        off += r * c
        out.append(part.reshape(N_CHIPS * r, c) if axis == 0 else part.transpose(1, 0, 2).reshape(r, N_CHIPS * c))
    return out


def _pack_grads(grads):
    parts = []
    for g, (_, (r, c), axis) in zip(grads, BIG):
        gk = g.reshape(N_CHIPS, r, c) if axis == 0 else g.reshape(r, N_CHIPS, c).transpose(1, 0, 2)
        parts.append(gk.reshape(N_CHIPS, r * c))
    parts.append(jnp.zeros((N_CHIPS, PACK_ROWS * LANES - PACK_ELEMS), F32))
    return jnp.concatenate(parts, axis=1).reshape(N_CHIPS, PACK_ROWS, LANES)


def _unpack_shards(packed):
    flat = packed.reshape(PACK_ROWS * LANES)
    out, off = [], 0
    for _, (r, c), _ in BIG:
        out.append(flat[off:off + r * c].reshape(r, c))
        off += r * c
    return out


def kernel(x, meta_tokens, ffn1_norm, ffn1_w_in, ffn1_w_out, mix_norm, w_in, b_forget, attn_sinks, w_branch_a, w_branch_b, w_out, ffn2_norm, ffn2_w_in, ffn2_w_out, final_norm, loss_target, m_meta_tokens, m_ffn1_norm, m_ffn1_w_in, m_ffn1_w_out, m_mix_norm, m_w_in, m_b_forget, m_attn_sinks, m_w_branch_a, m_w_branch_b, m_w_out, m_ffn2_norm, m_ffn2_w_in, m_ffn2_w_out, m_final_norm, v_meta_tokens, v_ffn1_norm, v_ffn1_w_in, v_ffn1_w_out, v_mix_norm, v_w_in, v_b_forget, v_attn_sinks, v_w_branch_a, v_w_branch_b, v_w_out, v_ffn2_norm, v_ffn2_w_in, v_ffn2_w_out, v_final_norm):
    given = dict(locals())
    names = ["meta_tokens", "ffn1_norm", "ffn1_w_in", "ffn1_w_out", "mix_norm", "w_in", "b_forget", "attn_sinks",
             "w_branch_a", "w_branch_b", "w_out", "ffn2_norm", "ffn2_w_in", "ffn2_w_out", "final_norm"]
    big_names = [n for n, _, _ in BIG]
    cx, cy, cc = _coords()

    packed = _pack([given[n][0] for n in big_names], BF16)
    stacked, meta_stacked = _gather_weights(packed, meta_tokens)
    w1i, w1o, wp, wa, wb, wo, w2i, w2o = _unpack_full(stacked)
    meta_full = meta_stacked.transpose(1, 0, 2).reshape(N_META, D_MODEL)
    norms = (ffn1_norm, mix_norm, ffn2_norm, final_norm.reshape(1, D_MODEL))
    loss, grad_x, small, big = _local_step(x, loss_target, meta_full, norms, b_forget, attn_sinks,
                                           (w1i, w1o, _reorder_w_in(wp), wa, wb, wo, w2i, w2o))

    gp = _pack_grads([big[n] for n in big_names]).reshape(N_CHIPS, 2, HALF_ROWS, LANES)
    c_arr = cc.reshape(1).astype(jnp.int32)
    partial = _add_sibling(gp, _exchange_halves(gp), c_arr)
    mine = (2 * cx + cy).reshape(1).astype(jnp.int32)
    reduced_half = _add_chips(partial, _scatter_chip_sums(partial), mine)
    grads = dict(zip(big_names, _unpack_shards(_share_with_sibling(reduced_half))))

    pad_lanes = lambda a: jnp.concatenate([a, jnp.zeros((1, LANES - a.shape[1]), F32)], axis=1)
    buf = jnp.concatenate([
        small["meta_tokens"].reshape(128, LANES),
        small["ffn1_norm"].reshape(8, LANES), small["mix_norm"].reshape(8, LANES),
        small["ffn2_norm"].reshape(8, LANES), small["final_norm"].reshape(8, LANES),
        loss, pad_lanes(small["b_forget"]), pad_lanes(small["attn_sinks"]),
        jnp.zeros((SMALL_ROWS - 163, LANES), F32)], axis=0)
    red = _all_reduce_small(buf)
    meta_cols = red[:128].reshape(N_META, D_MODEL)
    grads["meta_tokens"] = lax.dynamic_slice_in_dim(meta_cols, (2 * cx + cy) * (D_MODEL // N_CHIPS),
                                                    D_MODEL // N_CHIPS, axis=1)
    grads["ffn1_norm"] = red[128:136].reshape(1, D_MODEL)
    grads["mix_norm"] = red[136:144].reshape(1, D_MODEL)
    grads["ffn2_norm"] = red[144:152].reshape(1, D_MODEL)
    grads["final_norm"] = red[152:160].reshape(1, D_MODEL)
    loss_out = red[160, 0]
    grads["b_forget"] = red[161:162, :B_HEADS]
    grads["attn_sinks"] = red[162:163, :A_HEADS]

    out_g, out_d, out_m, out_v = [], [], [], []
    for n in names:
        w_full = given[n]
        shape = w_full.shape
        two_d = (lambda a: a.reshape(shape[-2], shape[-1])) if len(shape) >= 2 else (lambda a: a.reshape(1, shape[0]))
        g2 = two_d(grads[n])
        d2, m2, v2 = _adamw(two_d(w_full), g2, two_d(given["m_" + n]), two_d(given["v_" + n]))
        out_g.append(g2.reshape(shape))
        out_d.append(d2.reshape(shape))
        out_m.append(m2.reshape(shape))
        out_v.append(v2.reshape(shape))
    return (loss_out, grad_x, *out_g, *out_d, *out_m, *out_v)
```

```python
import jax
import jax.numpy as jnp
from jax import lax
from jax.experimental import pallas as pl
from jax.experimental.pallas import tpu as pltpu

F32 = jnp.float32
BF16 = jnp.bfloat16

D_MODEL = 1024
N_META = 16
BLOCK = 128
LANES = 128
PREFIX = BLOCK
N_PAD = PREFIX - N_META
HEAD_DIM = 64
A_HEADS = 8
A_KV_HEADS = 2
A_GROUP = 4
B_HEADS = 8
B_PAIRS = B_HEADS // 2
A_WIDTH = A_HEADS * HEAD_DIM
A_KV_WIDTH = A_KV_HEADS * HEAD_DIM
B_WIDTH = B_HEADS * HEAD_DIM
W_IN_COLS = A_WIDTH + 2 * A_KV_WIDTH + 3 * B_WIDTH + B_HEADS + 2 * D_MODEL
SRC_KA = A_WIDTH
SRC_VA = SRC_KA + A_KV_WIDTH
SRC_QB = SRC_VA + A_KV_WIDTH
SRC_KB = SRC_QB + B_WIDTH
SRC_VB = SRC_KB + B_WIDTH
SRC_F = SRC_VB + B_WIDTH
SRC_GA = SRC_F + B_HEADS
SRC_GB = SRC_GA + D_MODEL
A_PAD_WIDTH = A_HEADS * LANES
B_PAD_WIDTH = B_HEADS * LANES
F_COLS = LANES
OFF_QA = 0
OFF_KA = OFF_QA + A_PAD_WIDTH
OFF_VA = OFF_KA + A_KV_WIDTH
OFF_QB = OFF_VA + A_KV_WIDTH
OFF_KB = OFF_QB + B_WIDTH
OFF_VB = OFF_KB + B_PAD_WIDTH
OFF_GA = OFF_VB + B_PAD_WIDTH
OFF_GB = OFF_GA + D_MODEL
OFF_F = OFF_GB + D_MODEL
TAIL_COLS = 128
P_COLS = OFF_F + F_COLS + TAIL_COLS
EPS = 1e-6
NEG = -1e30
SCALE = HEAD_DIM ** -0.5
KEY_BLOCKS = 4

ADAM_LR = 0.001
ADAM_B1 = 0.9
ADAM_B2 = 0.999
ADAM_EPS = 1e-08
ADAM_WD = 0.01
ADAM_STEP = 10

N_CHIPS = 4
N_DEV = 8
VMEM_LIMIT = 56 * 1024 * 1024

NT_DIMS = (((1,), (1,)), ((), ()))
TN_DIMS = (((0,), (0,)), ((), ()))
MESH_ID = pl.DeviceIdType.MESH
HBM_SPEC = pl.BlockSpec(memory_space=pltpu.HBM)
VMEM_SPEC = pl.BlockSpec(memory_space=pltpu.VMEM)


def _tile(n, target, mult=16):
    best = None
    for t in range(mult, min(n, target) + 1, mult):
        if n % t == 0:
            best = t
    return best if best is not None else n


def _cparams(sem):
    return pltpu.CompilerParams(dimension_semantics=sem, vmem_limit_bytes=VMEM_LIMIT)


def _rms_scale(h):
    return lax.rsqrt(jnp.mean(h * h, axis=-1, keepdims=True) + EPS)


def _rms_bwd(dn, h, w):
    r = _rms_scale(h)
    dw = jnp.sum(dn * (h * r), axis=0, keepdims=True)
    z = dn * w
    dh = r * z - h * ((r * r * r) * jnp.mean(z * h, axis=-1, keepdims=True))
    return dh, dw


def _ffn_fwd(h, norm_w, w_in, w_out):
    t, d = h.shape
    f = w_out.shape[0]
    tm = _tile(t, 544)
    tc = _tile(f, 256, 128)
    nj = f // tc

    def body(h_ref, nw_ref, wg_ref, wu_ref, wo_ref, hout_ref, g_ref, u_ref, n_scr, acc_scr):
        j = pl.program_id(1)

        @pl.when(j == 0)
        def _():
            hh = h_ref[...]
            n_scr[...] = ((hh * _rms_scale(hh)) * nw_ref[...]).astype(BF16)
            acc_scr[...] = jnp.zeros_like(acc_scr)

        n = n_scr[...]
        g = jnp.dot(n, wg_ref[...], preferred_element_type=F32)
        u = jnp.dot(n, wu_ref[...], preferred_element_type=F32)
        g_ref[...] = g
        u_ref[...] = u
        a = (g * jax.nn.sigmoid(g)) * u
        acc_scr[...] += jnp.dot(a.astype(BF16), wo_ref[...], preferred_element_type=F32)

        @pl.when(j == nj - 1)
        def _():
            hout_ref[...] = h_ref[...] + 0.5 * acc_scr[...]

    return pl.pallas_call(
        body,
        name="ffn_fwd",
        grid=(t // tm, nj),
        in_specs=[
            pl.BlockSpec((tm, d), lambda i, j: (i, 0)),
            pl.BlockSpec((1, d), lambda i, j: (0, 0)),
            pl.BlockSpec((d, tc), lambda i, j: (0, j)),
            pl.BlockSpec((d, tc), lambda i, j: (0, j + nj)),
            pl.BlockSpec((tc, d), lambda i, j: (j, 0)),
        ],
        out_specs=[
            pl.BlockSpec((tm, d), lambda i, j: (i, 0)),
            pl.BlockSpec((tm, tc), lambda i, j: (i, j)),
            pl.BlockSpec((tm, tc), lambda i, j: (i, j)),
        ],
        out_shape=[
            jax.ShapeDtypeStruct((t, d), F32),
            jax.ShapeDtypeStruct((t, f), F32),
            jax.ShapeDtypeStruct((t, f), F32),
        ],
        scratch_shapes=[pltpu.VMEM((tm, d), BF16), pltpu.VMEM((tm, d), F32)],
        compiler_params=_cparams(("parallel", "arbitrary")),
    )(h, norm_w, w_in, w_in, w_out)


def _ffn_bwd(dh_out, h, norm_w, g, u, w_in, w_out):
    t, d = h.shape
    f = w_out.shape[0]
    tm = _tile(t, 544)
    tc = _tile(f, 256, 128)
    nj = f // tc
    ni = t // tm

    def body(dho_ref, h_ref, nw_ref, g_ref, u_ref, wg_ref, wu_ref, wo_ref,
             dhin_ref, n_ref, a_ref, dgu_ref, df_ref, dnw_ref, dn_scr):
        j = pl.program_id(1)

        @pl.when(j == 0)
        def _():
            hh = h_ref[...]
            n_ref[...] = ((hh * _rms_scale(hh)) * nw_ref[...]).astype(BF16)
            df_ref[...] = (0.5 * dho_ref[...]).astype(BF16)
            dn_scr[...] = jnp.zeros_like(dn_scr)

        da = lax.dot_general(df_ref[...], wo_ref[...], NT_DIMS, preferred_element_type=F32)
        gg = g_ref[...]
        uu = u_ref[...]
        sig = jax.nn.sigmoid(gg)
        sl = gg * sig
        a_ref[...] = (sl * uu).astype(BF16)
        dg = ((da * uu) * (sig * (1.0 + gg * (1.0 - sig)))).astype(BF16)
        du = (da * sl).astype(BF16)
        dgu_ref[0] = dg
        dgu_ref[1] = du
        dn_scr[...] += (lax.dot_general(dg, wg_ref[...], NT_DIMS, preferred_element_type=F32)
                        + lax.dot_general(du, wu_ref[...], NT_DIMS, preferred_element_type=F32))

        @pl.when(j == nj - 1)
        def _():
            dh, dw = _rms_bwd(dn_scr[...], h_ref[...], nw_ref[...])
            dhin_ref[...] = dho_ref[...] + dh
            dnw_ref[0] = dw

    return pl.pallas_call(
        body,
        name="ffn_bwd",
        grid=(ni, nj),
        in_specs=[
            pl.BlockSpec((tm, d), lambda i, j: (i, 0)),
            pl.BlockSpec((tm, d), lambda i, j: (i, 0)),
            pl.BlockSpec((1, d), lambda i, j: (0, 0)),
            pl.BlockSpec((tm, tc), lambda i, j: (i, j)),
            pl.BlockSpec((tm, tc), lambda i, j: (i, j)),
            pl.BlockSpec((d, tc), lambda i, j: (0, j)),
            pl.BlockSpec((d, tc), lambda i, j: (0, j + nj)),
            pl.BlockSpec((tc, d), lambda i, j: (j, 0)),
        ],
        out_specs=[
            pl.BlockSpec((tm, d), lambda i, j: (i, 0)),
            pl.BlockSpec((tm, d), lambda i, j: (i, 0)),
            pl.BlockSpec((tm, tc), lambda i, j: (i, j)),
            pl.BlockSpec((2, tm, tc), lambda i, j: (0, i, j)),
            pl.BlockSpec((tm, d), lambda i, j: (i, 0)),
            pl.BlockSpec((1, 1, d), lambda i, j: (i, 0, 0)),
        ],
        out_shape=[
            jax.ShapeDtypeStruct((t, d), F32),
            jax.ShapeDtypeStruct((t, d), BF16),
            jax.ShapeDtypeStruct((t, f), BF16),
            jax.ShapeDtypeStruct((2, t, f), BF16),
            jax.ShapeDtypeStruct((t, d), BF16),
            jax.ShapeDtypeStruct((ni, 1, d), F32),
        ],
        scratch_shapes=[pltpu.VMEM((tm, d), F32)],
        compiler_params=_cparams(("parallel", "arbitrary")),
    )(dh_out, h, norm_w, g, u, w_in, w_in, w_out)


def _tn_matmul(a, b, name):
    t, k = a.shape
    split = b.ndim == 3
    n = 2 * b.shape[2] if split else b.shape[1]
    tt = _tile(t, 1088)
    tk = _tile(k, 512, 128)
    tn = _tile(b.shape[-1], 1408, 128)
    nt = t // tt
    per_half = b.shape[-1] // tn

    def body(a_ref, b_ref, o_ref):
        s = pl.program_id(2)
        part = lax.dot_general(a_ref[...], b_ref[...], TN_DIMS, preferred_element_type=F32)

        @pl.when(s == 0)
        def _():
            o_ref[...] = part

        @pl.when(s > 0)
        def _():
            o_ref[...] += part

    if split:
        b_spec = pl.BlockSpec((None, tt, tn), lambda i, j, s: (j // per_half, s, j % per_half))
    else:
        b_spec = pl.BlockSpec((tt, tn), lambda i, j, s: (s, j))
    return pl.pallas_call(
        body,
        name=name,
        grid=(k // tk, n // tn, nt),
        in_specs=[pl.BlockSpec((tt, tk), lambda i, j, s: (s, i)), b_spec],
        out_specs=pl.BlockSpec((tk, tn), lambda i, j, s: (i, j)),
        out_shape=jax.ShapeDtypeStruct((k, n), F32),
        compiler_params=_cparams(("parallel", "parallel", "arbitrary")),
    )(a, b)


PROJ_PARTS = (
    (OFF_QA, A_PAD_WIDTH, True), (OFF_KA, A_KV_WIDTH, True), (OFF_VA, A_KV_WIDTH, True),
    (OFF_QB, B_WIDTH, True), (OFF_KB, B_PAD_WIDTH, True), (OFF_VB, B_PAD_WIDTH, True),
    (OFF_GA, D_MODEL, False), (OFF_GB, D_MODEL, False), (OFF_F, F_COLS, False),
)


def _proj_fwd(h, norm_w, w_p):
    t, d = h.shape
    tm = _tile(t, 272)

    def body(h_ref, nw_ref, w_ref, u_ref, *part_refs):
        hh = h_ref[...]
        un = ((hh * _rms_scale(hh)) * nw_ref[...]).astype(BF16)
        u_ref[...] = un
        for (off, width, _), p_ref in zip(PROJ_PARTS, part_refs):
            p_ref[...] = jnp.dot(un, w_ref[:, off:off + width], preferred_element_type=F32).astype(p_ref.dtype)

    row = lambda w: pl.BlockSpec((tm, w), lambda i: (i, 0))
    return pl.pallas_call(
        body,
        name="proj_fwd",
        grid=(t // tm,),
        in_specs=[row(d), pl.BlockSpec((1, d), lambda i: (0, 0)), pl.BlockSpec(w_p.shape, lambda i: (0, 0))],
        out_specs=[row(d)] + [row(width) for _, width, _ in PROJ_PARTS],
        out_shape=[jax.ShapeDtypeStruct((t, d), BF16)]
        + [jax.ShapeDtypeStruct((t, width), BF16 if is_bf else F32) for _, width, is_bf in PROJ_PARTS],
        compiler_params=_cparams(("parallel",)),
    )(h, norm_w, w_p)


def _proj_bwd(dh_out, h, norm_w, dproj, w_p):
    t, d = h.shape
    n = w_p.shape[1]
    tm = _tile(t, 544)
    tn = _tile(n, 768, 128)
    nj = n // tn
    ni = t // tm

    def body(dho_ref, h_ref, nw_ref, dp_ref, w_ref, dhin_ref, dnw_ref, dn_scr):
        j = pl.program_id(1)

        @pl.when(j == 0)
        def _():
            dn_scr[...] = jnp.zeros_like(dn_scr)

        dn_scr[...] += lax.dot_general(dp_ref[...], w_ref[...], NT_DIMS, preferred_element_type=F32)

        @pl.when(j == nj - 1)
        def _():
            dh, dw = _rms_bwd(dn_scr[...], h_ref[...], nw_ref[...])
            dhin_ref[...] = dho_ref[...] + dh
            dnw_ref[0] = dw

    return pl.pallas_call(
        body,
        name="proj_bwd",
        grid=(ni, nj),
        in_specs=[
            pl.BlockSpec((tm, d), lambda i, j: (i, 0)),
            pl.BlockSpec((tm, d), lambda i, j: (i, 0)),
            pl.BlockSpec((1, d), lambda i, j: (0, 0)),
            pl.BlockSpec((tm, tn), lambda i, j: (i, j)),
            pl.BlockSpec((d, tn), lambda i, j: (0, j)),
        ],
        out_specs=[
            pl.BlockSpec((tm, d), lambda i, j: (i, 0)),
            pl.BlockSpec((1, 1, d), lambda i, j: (i, 0, 0)),
        ],
        out_shape=[jax.ShapeDtypeStruct((t, d), F32), jax.ShapeDtypeStruct((ni, 1, d), F32)],
        scratch_shapes=[pltpu.VMEM((tm, d), F32)],
        compiler_params=_cparams(("parallel", "arbitrary")),
    )(dh_out, h, norm_w, dproj, w_p)


def _merge_fwd(h, oa, ob, ga, gb, wa, wb, wo):
    t, d = h.shape
    tm = _tile(t, 544)

    def body(h_ref, oa_ref, ob_ref, ga_ref, gb_ref, wa_ref, wb_ref, wo_ref, hout_ref, mix_ref):
        ya = jnp.dot(oa_ref[...], wa_ref[...], preferred_element_type=F32)
        yb = jnp.dot(ob_ref[...], wb_ref[...], preferred_element_type=F32)
        mixed = (jax.nn.sigmoid(ga_ref[...]) * ya + jax.nn.sigmoid(gb_ref[...]) * yb).astype(BF16)
        mix_ref[...] = mixed
        hout_ref[...] = h_ref[...] + jnp.dot(mixed, wo_ref[...], preferred_element_type=F32)

    row = lambda w: pl.BlockSpec((tm, w), lambda i: (i, 0))
    full = lambda a: pl.BlockSpec(a.shape, lambda i: (0, 0))
    return pl.pallas_call(
        body,
        name="merge_fwd",
        grid=(t // tm,),
        in_specs=[row(d), row(oa.shape[1]), row(ob.shape[1]), row(d), row(d), full(wa), full(wb), full(wo)],
        out_specs=[row(d), row(d)],
        out_shape=[jax.ShapeDtypeStruct((t, d), F32), jax.ShapeDtypeStruct((t, d), BF16)],
        compiler_params=_cparams(("parallel",)),
    )(h, oa, ob, ga, gb, wa, wb, wo)


def _merge_bwd(dh, oa, ob, ga, gb, wa, wb, wo):
    t, d = dh.shape
    tm = _tile(t, 544)

    def body(dh_ref, oa_ref, ob_ref, ga_ref, gb_ref, wa_ref, wb_ref, wo_ref,
             dya_ref, dyb_ref, doa_ref, dob_ref, dga_ref, dgb_ref, dhb_ref):
        dhb = dh_ref[...].astype(BF16)
        dhb_ref[...] = dhb
        dmix = lax.dot_general(dhb, wo_ref[...], NT_DIMS, preferred_element_type=F32)
        for o_ref, g_ref, w_ref, dy_ref, do_ref, dg_ref in (
                (oa_ref, ga_ref, wa_ref, dya_ref, doa_ref, dga_ref),
                (ob_ref, gb_ref, wb_ref, dyb_ref, dob_ref, dgb_ref)):
            y = jnp.dot(o_ref[...], w_ref[...], preferred_element_type=F32)
            s = jax.nn.sigmoid(g_ref[...])
            dy = (dmix * s).astype(BF16)
            dy_ref[...] = dy
            dg_ref[...] = ((dmix * y) * (s * (1.0 - s))).astype(BF16)
            do_ref[...] = lax.dot_general(dy, w_ref[...], NT_DIMS, preferred_element_type=F32).astype(BF16)

    row = lambda w: pl.BlockSpec((tm, w), lambda i: (i, 0))
    full = lambda a: pl.BlockSpec(a.shape, lambda i: (0, 0))
    wa_w, wb_w = oa.shape[1], ob.shape[1]
    return pl.pallas_call(
        body,
        name="merge_bwd",
        grid=(t // tm,),
        in_specs=[row(d), row(wa_w), row(wb_w), row(d), row(d), full(wa), full(wb), full(wo)],
        out_specs=[row(d), row(d), row(wa_w), row(wb_w), row(d), row(d), row(d)],
        out_shape=[
            jax.ShapeDtypeStruct((t, d), BF16), jax.ShapeDtypeStruct((t, d), BF16),
            jax.ShapeDtypeStruct((t, wa_w), BF16), jax.ShapeDtypeStruct((t, wb_w), BF16),
            jax.ShapeDtypeStruct((t, d), BF16), jax.ShapeDtypeStruct((t, d), BF16),
            jax.ShapeDtypeStruct((t, d), BF16),
        ],
        compiler_params=_cparams(("parallel",)),
    )(dh, oa, ob, ga, gb, wa, wb, wo)


def _tri_dot(tri, x):
    hi = x.astype(BF16)
    r1 = x - hi.astype(F32)
    mid = r1.astype(BF16)
    lo = (r1 - mid.astype(F32)).astype(BF16)
    return (jnp.dot(tri, hi, preferred_element_type=F32)
            + jnp.dot(tri, mid, preferred_element_type=F32)
            + jnp.dot(tri, lo, preferred_element_type=F32))


def _forget_cumsum(f_logit, b_pad, nb):
    t, w = f_logit.shape
    bsz = t // (nb * BLOCK)

    def body(f_ref, b_ref, c_ref, carry):
        n = pl.program_id(1)

        @pl.when(n == 0)
        def _():
            carry[...] = jnp.zeros_like(carry)

        x = jax.nn.log_sigmoid(f_ref[...] + b_ref[...])
        rows = lax.broadcasted_iota(jnp.int32, (BLOCK, BLOCK), 0)
        cols = lax.broadcasted_iota(jnp.int32, (BLOCK, BLOCK), 1)
        tri = (cols <= rows).astype(BF16)
        c = _tri_dot(tri, x) + carry[...]
        c_ref[...] = c
        carry[...] = c[BLOCK - 1:BLOCK, :]

    return pl.pallas_call(
        body,
        name="forget_cumsum",
        grid=(bsz, nb),
        in_specs=[pl.BlockSpec((BLOCK, w), lambda b, n: (b * nb + n, 0)),
                  pl.BlockSpec((1, w), lambda b, n: (0, 0))],
        out_specs=pl.BlockSpec((BLOCK, w), lambda b, n: (b * nb + n, 0)),
        out_shape=jax.ShapeDtypeStruct((t, w), F32),
        scratch_shapes=[pltpu.VMEM((1, w), F32)],
        compiler_params=_cparams(("parallel", "arbitrary")),
    )(f_logit, b_pad)


def _forget_cumsum_bwd(dc, f_logit, b_pad, nb):
    t, w = f_logit.shape
    bsz = t // (nb * BLOCK)

    def body(dc_ref, f_ref, b_ref, df_ref, db_ref, carry):
        n = pl.program_id(1)

        @pl.when(n == 0)
        def _():
            carry[...] = jnp.zeros_like(carry)
            db_ref[...] = jnp.zeros_like(db_ref)

        rows = lax.broadcasted_iota(jnp.int32, (BLOCK, BLOCK), 0)
        cols = lax.broadcasted_iota(jnp.int32, (BLOCK, BLOCK), 1)
        tri = (cols >= rows).astype(BF16)
        dlf = _tri_dot(tri, dc_ref[...]) + carry[...]
        carry[...] = dlf[0:1, :]
        df = dlf * jax.nn.sigmoid(-(f_ref[...] + b_ref[...]))
        df_ref[...] = df.astype(BF16)
        db_ref[0] += jnp.sum(df, axis=0, keepdims=True)

    rev = lambda b, n: (b * nb + (nb - 1 - n), 0)
    return pl.pallas_call(
        body,
        name="forget_cumsum_bwd",
        grid=(bsz, nb),
        in_specs=[pl.BlockSpec((BLOCK, w), rev),
                  pl.BlockSpec((BLOCK, w), rev),
                  pl.BlockSpec((1, w), lambda b, n: (0, 0))],
        out_specs=[pl.BlockSpec((BLOCK, w), rev),
                   pl.BlockSpec((1, 1, w), lambda b, n: (b, 0, 0))],
        out_shape=[jax.ShapeDtypeStruct((t, w), BF16), jax.ShapeDtypeStruct((bsz, 1, w), F32)],
        scratch_shapes=[pltpu.VMEM((1, w), F32)],
        compiler_params=_cparams(("parallel", "arbitrary")),
    )(dc, f_logit, b_pad)


GROUP_ROWS = A_GROUP * BLOCK


def _stack_heads(ref):
    return jnp.concatenate([ref[:, i * LANES:(i + 1) * LANES] for i in range(A_GROUP)], axis=0)


def _unstack_heads(ref, x):
    for i in range(A_GROUP):
        ref[:, i * LANES:(i + 1) * LANES] = x[i * BLOCK:(i + 1) * BLOCK].astype(ref.dtype)


def _swa_logits(q, km, kp, kc, slope, n):
    qi = lax.broadcasted_iota(jnp.int32, (GROUP_ROWS, BLOCK), 0) & (BLOCK - 1)
    kj = lax.broadcasted_iota(jnp.int32, (GROUP_ROWS, BLOCK), 1)
    out = []
    for kk, dist, ok in (
            (km, n * BLOCK + qi - kj, (kj >= N_PAD) & (n * BLOCK + qi - kj >= 0)),
            (kp, BLOCK + qi - kj, (kj > qi) & (n >= 2)),
            (kc, qi - kj, (kj <= qi) & (n >= 1))):
        s = lax.dot_general(q, kk, NT_DIMS, preferred_element_type=F32) * SCALE
        s = s - slope * dist.astype(F32)
        out.append(jnp.where(ok, s, NEG))
    return out


def _swa_specs(nb):
    row = lambda b, n: b * nb + n
    qspec = pl.BlockSpec((BLOCK, A_GROUP * LANES), lambda b, g, n: (row(b, n), g))
    kv_m = pl.BlockSpec((BLOCK, LANES), lambda b, g, n: (row(b, 0), 0))
    kv_p = pl.BlockSpec((BLOCK, LANES), lambda b, g, n: (row(b, jnp.maximum(n - 1, 0)), 0))
    kv_c = pl.BlockSpec((BLOCK, LANES), lambda b, g, n: (row(b, n), 0))
    rowspec = pl.BlockSpec((1, GROUP_ROWS, 1), lambda b, g, n: (g, 0, 0))
    lsespec = pl.BlockSpec((1, 1, GROUP_ROWS, 1), lambda b, g, n: (row(b, n), g, 0, 0))
    return qspec, kv_m, kv_p, kv_c, rowspec, lsespec


def _swa_fwd(q, k, v, sink_rows, slope_rows, nb):
    t = q.shape[0]
    bsz = t // (nb * BLOCK)

    def body(q_ref, km_ref, kp_ref, kc_ref, vm_ref, vp_ref, vc_ref, sink_ref, slope_ref, o_ref, lse_ref):
        g = pl.program_id(1)
        n = pl.program_id(2)
        qq = _stack_heads(q_ref)
        sink = sink_ref[0]
        s_m, s_p, s_c = _swa_logits(qq, km_ref[...], kp_ref[...], kc_ref[...], slope_ref[0], n)
        m = jnp.maximum(jnp.maximum(jnp.max(s_m, axis=-1, keepdims=True), jnp.max(s_p, axis=-1, keepdims=True)),
                        jnp.maximum(jnp.max(s_c, axis=-1, keepdims=True), sink))
        e_m = jnp.exp(s_m - m)
        e_p = jnp.exp(s_p - m)
        e_c = jnp.exp(s_c - m)
        z = (jnp.sum(e_m, axis=-1, keepdims=True) + jnp.sum(e_p, axis=-1, keepdims=True)
             + jnp.sum(e_c, axis=-1, keepdims=True) + jnp.exp(sink - m))
        inv = 1.0 / z
        o = (jnp.dot((e_m * inv).astype(BF16), vm_ref[...], preferred_element_type=F32)
             + jnp.dot((e_p * inv).astype(BF16), vp_ref[...], preferred_element_type=F32)
             + jnp.dot((e_c * inv).astype(BF16), vc_ref[...], preferred_element_type=F32))
        lane_group = lax.broadcasted_iota(jnp.int32, (GROUP_ROWS, LANES), 1) // HEAD_DIM
        _unstack_heads(o_ref, jnp.where(lane_group == g, o, 0.0))
        lse_ref[0, 0] = m + jnp.log(z)

    qspec, kv_m, kv_p, kv_c, rowspec, lsespec = _swa_specs(nb)
    return pl.pallas_call(
        body,
        name="swa_fwd",
        grid=(bsz, A_KV_HEADS, nb),
        in_specs=[qspec, kv_m, kv_p, kv_c, kv_m, kv_p, kv_c, rowspec, rowspec],
        out_specs=[qspec, lsespec],
        out_shape=[jax.ShapeDtypeStruct((t, A_PAD_WIDTH), BF16),
                   jax.ShapeDtypeStruct((t // BLOCK, A_KV_HEADS, GROUP_ROWS, 1), F32)],
        compiler_params=_cparams(("parallel", "parallel", "arbitrary")),
    )(q, k, k, k, v, v, v, sink_rows, slope_rows)


def _swa_bwd(q, k, v, do, lse, sink_rows, slope_rows, nb):
    t = q.shape[0]
    l = nb * BLOCK
    bsz = t // l

    def body(q_ref, km_ref, kp_ref, kc_ref, vm_ref, vp_ref, vc_ref, do_ref, lse_ref, sink_ref, slope_ref,
             dq_ref, dk_ref, dv_ref, dsink_ref, dk_acc, dv_acc):
        g = pl.program_id(1)
        n = pl.program_id(2)

        @pl.when((g == 0) & (n == 0))
        def _():
            dk_acc[...] = jnp.zeros_like(dk_acc)
            dv_acc[...] = jnp.zeros_like(dv_acc)

        @pl.when(n == 0)
        def _():
            dsink_ref[...] = jnp.zeros_like(dsink_ref)

        qq = _stack_heads(q_ref)
        dob = _stack_heads(do_ref)
        lse = lse_ref[0, 0]
        logits = _swa_logits(qq, km_ref[...], kp_ref[...], kc_ref[...], slope_ref[0], n)
        probs = [jnp.exp(s - lse) for s in logits]
        dps = [lax.dot_general(dob, v_ref[...], NT_DIMS, preferred_element_type=F32)
               for v_ref in (vm_ref, vp_ref, vc_ref)]
        delta = sum(jnp.sum(p * dp, axis=-1, keepdims=True) for p, dp in zip(probs, dps))
        prev = jnp.maximum(n - 1, 0)
        dq = jnp.zeros((GROUP_ROWS, LANES), F32)
        for p, dp, k_ref, start in ((probs[0], dps[0], km_ref, 0),
                                    (probs[1], dps[1], kp_ref, prev * BLOCK),
                                    (probs[2], dps[2], kc_ref, n * BLOCK)):
            ds = (p * (dp - delta)).astype(BF16)
            dq = dq + jnp.dot(ds, k_ref[...], preferred_element_type=F32)
            rows = pl.ds(pl.multiple_of(start, BLOCK), BLOCK)
            dk_acc[rows, :] += lax.dot_general(ds, qq, TN_DIMS, preferred_element_type=F32) * SCALE
            dv_acc[rows, :] += lax.dot_general(p.astype(BF16), dob, TN_DIMS, preferred_element_type=F32)
        _unstack_heads(dq_ref, dq * SCALE)
        dsink_ref[0, 0] += -(jnp.exp(sink_ref[0] - lse) * delta)

        @pl.when((g == A_KV_HEADS - 1) & (n == nb - 1))
        def _():
            dk_ref[...] = dk_acc[...].astype(BF16)
            dv_ref[...] = dv_acc[...].astype(BF16)

    qspec, kv_m, kv_p, kv_c, rowspec, lsespec = _swa_specs(nb)
    kv_all = pl.BlockSpec((l, LANES), lambda b, g, n: (b, 0))
    return pl.pallas_call(
        body,
        name="swa_bwd",
        grid=(bsz, A_KV_HEADS, nb),
        in_specs=[qspec, kv_m, kv_p, kv_c, kv_m, kv_p, kv_c, qspec, lsespec, rowspec, rowspec],
        out_specs=[qspec, kv_all, kv_all,
                   pl.BlockSpec((1, 1, GROUP_ROWS, 1), lambda b, g, n: (b, g, 0, 0))],
        out_shape=[jax.ShapeDtypeStruct((t, A_PAD_WIDTH), BF16),
                   jax.ShapeDtypeStruct((t, LANES), BF16),
                   jax.ShapeDtypeStruct((t, LANES), BF16),
                   jax.ShapeDtypeStruct((bsz, A_KV_HEADS, GROUP_ROWS, 1), F32)],
        scratch_shapes=[pltpu.VMEM((l, LANES), F32), pltpu.VMEM((l, LANES), F32)],
        compiler_params=_cparams(("parallel", "arbitrary", "arbitrary")),
    )(q, k, k, k, v, v, v, do, lse, sink_rows, slope_rows)


CHUNK = KEY_BLOCKS * BLOCK


def _fox_chunk(qs, k_ref, cc_ref, cr_ref, qb, ci):
    sb = jnp.maximum(jnp.minimum(KEY_BLOCKS * ci, qb + 1 - KEY_BLOCKS), 0)
    lo = jnp.maximum(ci * CHUNK, N_PAD)
    krows = pl.ds(pl.multiple_of(sb * BLOCK, BLOCK), CHUNK)
    kk2 = jnp.concatenate([k_ref[krows, 0:LANES], k_ref[krows, LANES:2 * LANES]], axis=0)
    s01 = lax.dot_general(qs, kk2, NT_DIMS, preferred_element_type=F32)
    q_pos = qb * BLOCK + lax.broadcasted_iota(jnp.int32, (BLOCK, BLOCK), 0)
    lane = lax.broadcasted_iota(jnp.int32, (BLOCK, BLOCK), 1)
    logits = []
    for e in range(2):
        c_t = cc_ref[e]
        per_block = []
        for j in range(KEY_BLOCKS):
            k_pos = (sb + j) * BLOCK + lane
            col = (e * KEY_BLOCKS + j) * BLOCK
            s = s01[:, col:col + BLOCK] + c_t - cr_ref[e, sb + j]
            per_block.append(jnp.where((k_pos <= q_pos) & (k_pos >= lo), s, NEG))
        logits.append(per_block)
    return krows, sb, kk2, logits


def _fox_specs(nb):
    l = nb * BLOCK
    q_spec = pl.BlockSpec((BLOCK, LANES), lambda b, p, i: (b * nb + i, p))
    kv_spec = pl.BlockSpec((l, 2 * LANES), lambda b, p, i: (b, p))
    cc_spec = pl.BlockSpec((2, BLOCK, 1), lambda b, p, i: (b * B_PAIRS + p, i, 0))
    cr_spec = pl.BlockSpec((2, nb, 1, BLOCK), lambda b, p, i: (b * B_PAIRS + p, 0, 0, 0))
    return q_spec, kv_spec, cc_spec, cr_spec


def _fox_fwd(q, k, v, c_col, c_row, nb):
    t = q.shape[0]
    bsz = t // (nb * BLOCK)
    assert nb >= KEY_BLOCKS

    def body(q_ref, k_ref, v_ref, cc_ref, cr_ref, o_ref, ox_ref, lse_ref):
        qb = pl.program_id(2)
        qs = q_ref[...] * SCALE
        first_half = lax.broadcasted_iota(jnp.int32, (BLOCK, LANES), 1) < HEAD_DIM

        def step(ci, carry):
            m0, z0, m1, z1, acc, acc_lo = carry
            krows, _, _, logits = _fox_chunk(qs, k_ref, cc_ref, cr_ref, qb, ci)
            stats, p_hi, p_lo = [], [], []
            for (m, z), per_block in zip(((m0, z0), (m1, z1)), logits):
                m_new = m
                for s in per_block:
                    m_new = jnp.maximum(m_new, jnp.max(s, axis=-1, keepdims=True))
                alpha = jnp.exp(m - m_new)
                z = alpha * z
                for s in per_block:
                    p = jnp.exp(s - m_new)
                    z = z + jnp.sum(p, axis=-1, keepdims=True)
                    hi = p.astype(BF16)
                    p_hi.append(hi)
                    p_lo.append((p - hi.astype(F32)).astype(BF16))
                stats.append((m_new, z, alpha))
            vv2 = jnp.concatenate([v_ref[krows, 0:LANES], v_ref[krows, LANES:2 * LANES]], axis=0)
            alpha = jnp.where(first_half, stats[0][2], stats[1][2])
            acc = alpha * acc + jnp.dot(jnp.concatenate(p_hi, axis=1), vv2, preferred_element_type=F32)
            acc_lo = alpha * acc_lo + jnp.dot(jnp.concatenate(p_lo, axis=1), vv2, preferred_element_type=F32)
            return stats[0][0], stats[0][1], stats[1][0], stats[1][1], acc, acc_lo

        col = lambda val: jnp.full((BLOCK, 1), val, F32)
        m0, z0, m1, z1, acc, acc_lo = lax.fori_loop(
            0, (qb + KEY_BLOCKS) // KEY_BLOCKS, step,
            (col(NEG), col(0.0), col(NEG), col(0.0), jnp.zeros((BLOCK, LANES), F32), jnp.zeros((BLOCK, LANES), F32)))
        inv = 1.0 / jnp.where(first_half, z0, z1)
        o_ref[...] = (acc * inv).astype(BF16)
        ox_ref[...] = (acc + acc_lo) * inv
        lse_ref[0] = m0 + jnp.log(z0)
        lse_ref[1] = m1 + jnp.log(z1)

    q_spec, kv_spec, cc_spec, cr_spec = _fox_specs(nb)
    return pl.pallas_call(
        body,
        name="fox_fwd",
        grid=(bsz, B_PAIRS, nb),
        in_specs=[q_spec, kv_spec, kv_spec, cc_spec, cr_spec],
        out_specs=[q_spec, q_spec, cc_spec],
        out_shape=[jax.ShapeDtypeStruct((t, B_WIDTH), BF16), jax.ShapeDtypeStruct((t, B_WIDTH), F32),
                   jax.ShapeDtypeStruct((bsz * B_HEADS, nb * BLOCK, 1), F32)],
        compiler_params=_cparams(("parallel", "parallel", "arbitrary")),
    )(q, k, v, c_col, c_row)


def _fox_bwd(q, k, v, o_exact, do, lse, c_col, c_row, nb):
    t = q.shape[0]
    l = nb * BLOCK
    bsz = t // l

    def body(q_ref, k_ref, v_ref, ox_ref, do_ref, lse_ref, cc_ref, cr_ref,
             dq_ref, dk_ref, dv_ref, dc_ref, dk_acc, dv_acc):
        qb = pl.program_id(2)

        @pl.when(qb == 0)
        def _():
            dk_acc[...] = jnp.zeros_like(dk_acc)
            dv_acc[...] = jnp.zeros_like(dv_acc)
            dc_ref[...] = jnp.zeros_like(dc_ref)

        qs = q_ref[...] * SCALE
        dob = do_ref[...]
        first_half = lax.broadcasted_iota(jnp.int32, (BLOCK, LANES), 1) < HEAD_DIM
        weighted = dob.astype(F32) * ox_ref[...]
        deltas = (jnp.sum(jnp.where(first_half, weighted, 0.0), axis=-1, keepdims=True),
                  jnp.sum(jnp.where(first_half, 0.0, weighted), axis=-1, keepdims=True))

        def step(ci, dq):
            krows, sb, kk2, logits = _fox_chunk(qs, k_ref, cc_ref, cr_ref, qb, ci)
            vv2 = jnp.concatenate([v_ref[krows, 0:LANES], v_ref[krows, LANES:2 * LANES]], axis=0)
            dp01 = lax.dot_general(dob, vv2, NT_DIMS, preferred_element_type=F32)
            ps, dss = [], []
            for e in range(2):
                lse_e = lse_ref[e]
                for j in range(KEY_BLOCKS):
                    col = (e * KEY_BLOCKS + j) * BLOCK
                    p = jnp.exp(logits[e][j] - lse_e)
                    ds = p * (dp01[:, col:col + BLOCK] - deltas[e])
                    dc_ref[e, sb + j] -= jnp.sum(ds, axis=0, keepdims=True)
                    ps.append(p.astype(BF16))
                    dss.append(ds.astype(BF16))
            dsb = jnp.concatenate(dss, axis=1)
            pb = jnp.concatenate(ps, axis=1)
            dk2 = lax.dot_general(dsb, qs, TN_DIMS, preferred_element_type=F32)
            dv2 = lax.dot_general(pb, dob, TN_DIMS, preferred_element_type=F32)
            for e in range(2):
                dk_acc[krows, e * LANES:(e + 1) * LANES] += dk2[e * CHUNK:(e + 1) * CHUNK]
                dv_acc[krows, e * LANES:(e + 1) * LANES] += dv2[e * CHUNK:(e + 1) * CHUNK]
            return dq + jnp.dot(dsb, kk2, preferred_element_type=F32)

        dq = lax.fori_loop(0, (qb + KEY_BLOCKS) // KEY_BLOCKS, step, jnp.zeros((BLOCK, LANES), F32))
        dq_ref[...] = (dq * SCALE).astype(BF16)

        @pl.when(qb == nb - 1)
        def _():
            dk_ref[...] = dk_acc[...].astype(BF16)
            dv_ref[...] = dv_acc[...].astype(BF16)

    q_spec, kv_spec, cc_spec, cr_spec = _fox_specs(nb)
    return pl.pallas_call(
        body,
        name="fox_bwd",
        grid=(bsz, B_PAIRS, nb),
        in_specs=[q_spec, kv_spec, kv_spec, q_spec, q_spec, cc_spec, cc_spec, cr_spec],
        out_specs=[q_spec, kv_spec, kv_spec, cr_spec],
        out_shape=[jax.ShapeDtypeStruct((t, B_WIDTH), BF16), jax.ShapeDtypeStruct((t, B_PAD_WIDTH), BF16),
                   jax.ShapeDtypeStruct((t, B_PAD_WIDTH), BF16),
                   jax.ShapeDtypeStruct((bsz * B_HEADS, nb, 1, BLOCK), F32)],
        scratch_shapes=[pltpu.VMEM((l, 2 * LANES), F32), pltpu.VMEM((l, 2 * LANES), F32)],
        compiler_params=_cparams(("parallel", "parallel", "arbitrary")),
    )(q, k, v, o_exact, do, lse, c_col, c_row)


def _loss_head(h, final_w, target):
    bsz, l, d = h.shape
    nb = l // BLOCK

    def body(h_ref, w_ref, t_ref, loss_ref, dh_ref, dw_ref):
        b = pl.program_id(0)
        n = pl.program_id(1)

        @pl.when((b == 0) & (n == 0))
        def _():
            loss_ref[...] = jnp.zeros_like(loss_ref)
            dw_ref[...] = jnp.zeros_like(dw_ref)

        @pl.when(n == 0)
        def _():
            dh_ref[...] = jnp.zeros_like(dh_ref)

        @pl.when(n > 0)
        def _():
            hh = h_ref[0]
            w = w_ref[...]
            r = _rms_scale(hh)
            err = (hh * r) * w - t_ref[0]
            loss_ref[...] += 0.5 * jnp.sum(jnp.mean(err * err, axis=-1, keepdims=True), axis=0, keepdims=True)
            dy = err * (1.0 / d)
            dh, dw = _rms_bwd(dy, hh, w)
            dh_ref[0] = dh
            dw_ref[...] += dw

    return pl.pallas_call(
        body,
        name="loss_head",
        grid=(bsz, nb),
        in_specs=[
            pl.BlockSpec((1, BLOCK, d), lambda b, n: (b, n, 0)),
            pl.BlockSpec((1, d), lambda b, n: (0, 0)),
            pl.BlockSpec((1, BLOCK, d), lambda b, n: (b, jnp.maximum(n - 1, 0), 0)),
        ],
        out_specs=[
            pl.BlockSpec((1, 128), lambda b, n: (0, 0)),
            pl.BlockSpec((1, BLOCK, d), lambda b, n: (b, n, 0)),
            pl.BlockSpec((1, d), lambda b, n: (0, 0)),
        ],
        out_shape=[jax.ShapeDtypeStruct((1, 128), F32), jax.ShapeDtypeStruct((bsz, l, d), F32),
                   jax.ShapeDtypeStruct((1, d), F32)],
        compiler_params=_cparams(("arbitrary", "arbitrary")),
    )(h, final_w, target)


def _pad_tiles(w, src, heads, lane_slot, axis):
    pieces = []
    for h in range(heads):
        x = lax.slice_in_dim(w, src + HEAD_DIM * h, src + HEAD_DIM * (h + 1), axis=axis)
        z = jnp.zeros_like(x)
        pieces += [x, z] if lane_slot(h) == 0 else [z, x]
    return pieces


def _unpad_tiles(g, off, heads, lane_slot, axis):
    return [lax.slice_in_dim(g, off + LANES * h + HEAD_DIM * lane_slot(h),
                             off + LANES * h + HEAD_DIM * (lane_slot(h) + 1), axis=axis) for h in range(heads)]


A_SLOT = lambda h: h // A_GROUP
B_SLOT = lambda h: h % 2


def _layout_w_in(w):
    pad_f = jnp.zeros((w.shape[0], F_COLS - B_HEADS + TAIL_COLS), w.dtype)
    return jnp.concatenate(
        _pad_tiles(w, 0, A_HEADS, A_SLOT, 1) + [w[:, SRC_KA:SRC_KB]]
        + _pad_tiles(w, SRC_KB, B_HEADS, B_SLOT, 1) + _pad_tiles(w, SRC_VB, B_HEADS, B_SLOT, 1)
        + [w[:, SRC_GA:], w[:, SRC_F:SRC_GA], pad_f], axis=1)


def _unlayout_w_in(g):
    return jnp.concatenate(
        _unpad_tiles(g, OFF_QA, A_HEADS, A_SLOT, 1) + [g[:, OFF_KA:OFF_KB]]
        + _unpad_tiles(g, OFF_KB, B_HEADS, B_SLOT, 1) + _unpad_tiles(g, OFF_VB, B_HEADS, B_SLOT, 1)
        + [g[:, OFF_F:OFF_F + B_HEADS], g[:, OFF_GA:OFF_F]], axis=1)


def _local_step(x, target, meta, norms, b_forget, sinks, w):
    n1, nmix, n2, nfin = norms
    w1i, w1o, w_in, wa, wb, wo, w2i, w2o = w
    wp = _layout_w_in(w_in)
    wa_p = jnp.concatenate(_pad_tiles(wa, 0, A_HEADS, A_SLOT, 0), axis=0)
    bsz, seq, d = x.shape
    l = PREFIX + seq
    nb = l // BLOCK
    t = bsz * l

    h0 = jnp.concatenate([jnp.zeros((bsz, N_PAD, d), F32),
                          jnp.broadcast_to(meta[None], (bsz, N_META, d)), x], axis=1).reshape(t, d)

    h1, g1, u1 = _ffn_fwd(h0, n1, w1i, w1o)
    un, qa, ka, va, qb, kb, vb, ga, gb, f_logit = _proj_fwd(h1, nmix, wp)
    b_pad = jnp.concatenate([b_forget, jnp.zeros((1, F_COLS - B_HEADS), F32)], axis=1)
    c = _forget_cumsum(f_logit, b_pad, nb)
    c_heads = c[:, :B_HEADS].reshape(bsz, l, B_HEADS).transpose(0, 2, 1).reshape(bsz * B_HEADS, l)
    c_col = c_heads[:, :, None]
    c_row = c_heads.reshape(bsz * B_HEADS, nb, 1, BLOCK)

    slopes = jnp.exp2(-8.0 * jnp.arange(1, A_HEADS + 1, dtype=F32) / A_HEADS)
    slope_rows = jnp.repeat(slopes.reshape(A_KV_HEADS, A_GROUP), BLOCK, axis=1)[:, :, None]
    sink_rows = jnp.repeat(sinks.reshape(A_KV_HEADS, A_GROUP), BLOCK, axis=1)[:, :, None]

    oa, lse_a = _swa_fwd(qa, ka, va, sink_rows, slope_rows, nb)
    ob, ob_exact, lse_b = _fox_fwd(qb, kb, vb, c_col, c_row, nb)
    h2, mixed = _merge_fwd(h1, oa, ob, ga, gb, wa_p, wb, wo)
    h3, g2, u2 = _ffn_fwd(h2, n2, w2i, w2o)
    loss, dh3, d_nfin = _loss_head(h3.reshape(bsz, l, d), nfin, target)

    dh2, n2b, a2, dgu2, df2, dn2_parts = _ffn_bwd(dh3.reshape(t, d), h2, n2, g2, u2, w2i, w2o)
    g_w2o = _tn_matmul(a2, df2, "grad_ffn2_w_out")
    g_w2i = _tn_matmul(n2b, dgu2, "grad_ffn2_w_in")

    dya, dyb, doa, dob, dga, dgb, dh2b = _merge_bwd(dh2, oa, ob, ga, gb, wa_p, wb, wo)
    g_wo = _tn_matmul(mixed, dh2b, "grad_w_out")
    g_wa = jnp.concatenate(_unpad_tiles(_tn_matmul(oa, dya, "grad_w_branch_a"), 0, A_HEADS, A_SLOT, 0), axis=0)
    g_wb = _tn_matmul(ob, dyb, "grad_w_branch_b")

    dqa, dka, dva, dsink_rows = _swa_bwd(qa, ka, va, doa, lse_a, sink_rows, slope_rows, nb)
    dqb, dkb, dvb, dc_row = _fox_bwd(qb, kb, vb, ob_exact, dob, lse_b, c_col, c_row, nb)
    dc = dc_row.reshape(bsz, B_HEADS, l).transpose(0, 2, 1).reshape(t, B_HEADS)
    dc = jnp.concatenate([dc, jnp.zeros((t, F_COLS - B_HEADS), F32)], axis=1)
    df_logit, db_parts = _forget_cumsum_bwd(dc, f_logit, b_pad, nb)

    dproj = jnp.concatenate([dqa, dka, dva, dqb, dkb, dvb, dga, dgb, df_logit,
                             jnp.zeros((t, TAIL_COLS), BF16)], axis=1)
    dh1, dnmix_parts = _proj_bwd(dh2, h1, nmix, dproj, wp)
    g_win = _unlayout_w_in(_tn_matmul(un, dproj, "grad_w_in"))

    dh0, n1b, a1, dgu1, df1, dn1_parts = _ffn_bwd(dh1, h0, n1, g1, u1, w1i, w1o)
    g_w1o = _tn_matmul(a1, df1, "grad_ffn1_w_out")
    g_w1i = _tn_matmul(n1b, dgu1, "grad_ffn1_w_in")

    dh0 = dh0.reshape(bsz, l, d)
    grad_x = dh0[:, PREFIX:]
    small = dict(
        meta_tokens=jnp.sum(dh0[:, N_PAD:PREFIX], axis=0),
        ffn1_norm=jnp.sum(dn1_parts, axis=0),
        mix_norm=jnp.sum(dnmix_parts, axis=0),
        ffn2_norm=jnp.sum(dn2_parts, axis=0),
        final_norm=d_nfin,
        b_forget=jnp.sum(db_parts, axis=0)[:, :B_HEADS],
        attn_sinks=jnp.sum(dsink_rows.reshape(bsz, A_HEADS, BLOCK), axis=(0, 2)).reshape(1, A_HEADS),
    )
    big = dict(ffn1_w_in=g_w1i, ffn1_w_out=g_w1o, w_in=g_win, w_branch_a=g_wa, w_branch_b=g_wb,
               w_out=g_wo, ffn2_w_in=g_w2i, ffn2_w_out=g_w2o)
    return loss, grad_x, small, big


BIG = (
    ("ffn1_w_in", (D_MODEL, 5632), 1),
    ("ffn1_w_out", (2816, D_MODEL), 0),
    ("w_in", (D_MODEL, W_IN_COLS), 1),
    ("w_branch_a", (A_WIDTH, D_MODEL), 1),
    ("w_branch_b", (B_WIDTH, D_MODEL), 1),
    ("w_out", (D_MODEL, D_MODEL), 0),
    ("ffn2_w_in", (D_MODEL, 5632), 1),
    ("ffn2_w_out", (2816, D_MODEL), 0),
)
STACKED = "w_in"


def _coords():
    return lax.axis_index("x"), lax.axis_index("y"), lax.axis_index("c")


def _other_chips(x, y):
    return ((1 - x, y), (x, 1 - y), (1 - x, 1 - y))


def _chip_part(ref, name, shape, axis, k):
    if name == STACKED:
        return ref.at[k]
    size = shape[axis] // N_CHIPS
    start = pl.multiple_of(k * size, size)
    return ref.at[pl.ds(start, size), :] if axis == 0 else ref.at[:, pl.ds(start, size)]


def _full_shape(name, shape):
    return (N_CHIPS, shape[0], shape[1] // N_CHIPS) if name == STACKED else shape


def _gather_weights(shards, meta):
    n_w = len(BIG)

    def body(*refs):
        ins, outs = refs[:n_w + 1], refs[n_w + 1:2 * (n_w + 1)]
        send_sems, recv_sems, local_sems = refs[2 * (n_w + 1):]
        x, y, c = _coords()
        mine = 2 * x + y
        table = BIG + (("meta_tokens", (N_META, D_MODEL), 1),)

        def part(i, k):
            name, shape, axis = table[i]
            return _chip_part(outs[i], name, shape, axis, k)

        local = [pltpu.make_async_copy(ins[i], part(i, mine), local_sems.at[i]) for i in range(n_w + 1)]
        for cp in local:
            cp.start()

        def copy(i, j, slot, chip):
            sem = i * 3 + j
            return pltpu.make_async_remote_copy(ins[i], part(i, slot), send_sems.at[sem], recv_sems.at[sem],
                                                device_id=(chip[0], chip[1], c), device_id_type=MESH_ID)

        chips = _other_chips(x, y)
        for i in range(n_w + 1):
            for j, chip in enumerate(chips):
                copy(i, j, mine, chip).start()
        for i in range(n_w + 1):
            for j, chip in enumerate(chips):
                copy(i, j, 2 * chip[0] + chip[1], chip).wait_recv()
        for i in range(n_w + 1):
            for j, chip in enumerate(chips):
                copy(i, j, mine, chip).wait_send()
        for cp in local:
            cp.wait()

    out_shape = [jax.ShapeDtypeStruct(_full_shape(name, shape), BF16) for name, shape, _ in BIG]
    out_shape.append(jax.ShapeDtypeStruct((N_META, D_MODEL), F32))
    n_sems = 3 * (n_w + 1)
    return pl.pallas_call(
        body,
        name="gather_weights",
        in_specs=[HBM_SPEC] * (n_w + 1),
        out_specs=[HBM_SPEC] * (n_w + 1),
        out_shape=out_shape,
        scratch_shapes=[pltpu.SemaphoreType.DMA((n_sems,)), pltpu.SemaphoreType.DMA((n_sems,)),
                        pltpu.SemaphoreType.DMA((n_w + 1,))],
    )(*shards, meta)


def _halves_view(name, shape, axis):
    r, c = shape
    if name == STACKED:
        return (N_CHIPS, 2, r // 2, c // N_CHIPS), lambda ref, h: ref.at[:, h]
    if axis == 1:
        return (2, r // 2, c), lambda ref, h: ref.at[h]
    return (N_CHIPS, 2, r // N_CHIPS // 2, c), lambda ref, h: ref.at[:, h]


def _exchange_halves(grads):
    n_w = len(BIG)
    views = [_halves_view(*entry) for entry in BIG]

    def body(*refs):
        ins, outs = refs[:n_w], refs[n_w:2 * n_w]
        send_sems, recv_sems = refs[2 * n_w:]
        x, y, c = _coords()
        copies = [pltpu.make_async_remote_copy(views[i][1](ins[i], 1 - c), outs[i], send_sems.at[i], recv_sems.at[i],
                                               device_id=(x, y, 1 - c), device_id_type=MESH_ID) for i in range(n_w)]
        for cp in copies:
            cp.start()
        for cp in copies:
            cp.wait()

    half_shape = lambda v: tuple(d for i, d in enumerate(v) if i != (1 if len(v) == 4 else 0))
    return pl.pallas_call(
        body,
        name="exchange_halves",
        in_specs=[HBM_SPEC] * n_w,
        out_specs=[HBM_SPEC] * n_w,
        out_shape=[jax.ShapeDtypeStruct(half_shape(v[0]), F32) for v in views],
        scratch_shapes=[pltpu.SemaphoreType.DMA((n_w,)), pltpu.SemaphoreType.DMA((n_w,))],
    )(*[g.reshape(v[0]) for g, v in zip(grads, views)])


def _add_sibling(g_view, recv, c, name):
    shape = recv.shape
    if len(shape) == 2:
        tr = _tile(shape[0], 128, 8)
        grid = (shape[0] // tr,)
        g_spec = pl.BlockSpec((None, tr, shape[1]), lambda i, c_ref: (c_ref[0], i, 0))
        r_spec = pl.BlockSpec((tr, shape[1]), lambda i, c_ref: (i, 0))
    else:
        tr = _tile(shape[1], 256, 8)
        grid = (N_CHIPS, shape[1] // tr)
        g_spec = pl.BlockSpec((None, None, tr, shape[2]), lambda k, i, c_ref: (k, c_ref[0], i, 0))
        r_spec = pl.BlockSpec((None, tr, shape[2]), lambda k, i, c_ref: (k, i, 0))

    def body(c_ref, g_ref, r_ref, o_ref):
        o_ref[...] = g_ref[...] + r_ref[...]

    return pl.pallas_call(
        body,
        name="add_sibling_" + name,
        grid_spec=pltpu.PrefetchScalarGridSpec(num_scalar_prefetch=1, grid=grid, in_specs=[g_spec, r_spec],
                                               out_specs=r_spec),
        out_shape=jax.ShapeDtypeStruct(shape, F32),
        compiler_params=_cparams(("parallel",) * len(grid)),
    )(c, g_view, recv)


def _piece_of(ref, name, axis, k):
    if name == STACKED or axis == 0:
        return ref.at[k]
    size = ref.shape[1] // N_CHIPS
    return ref.at[:, pl.ds(pl.multiple_of(k * size, size), size)]


def _piece_shape(name, shape, axis):
    r, c = shape
    return (r // 2, c // N_CHIPS) if (axis == 1) else (r // N_CHIPS // 2, c)


def _scatter_chip_sums(partials):
    n_w = len(BIG)

    def body(*refs):
        ins, outs = refs[:n_w], refs[n_w:2 * n_w]
        send_sems, recv_sems = refs[2 * n_w:]
        x, y, c = _coords()
        chips = _other_chips(x, y)
        copies = []
        for i, (name, _, axis) in enumerate(BIG):
            for j, chip in enumerate(chips):
                sem = 3 * i + j
                copies.append(pltpu.make_async_remote_copy(
                    _piece_of(ins[i], name, axis, 2 * chip[0] + chip[1]), outs[i].at[j], send_sems.at[sem],
                    recv_sems.at[sem], device_id=(chip[0], chip[1], c), device_id_type=MESH_ID))
        for cp in copies:
            cp.start()
        for cp in copies:
            cp.wait()

    return pl.pallas_call(
        body,
        name="scatter_chip_sums",
        in_specs=[HBM_SPEC] * n_w,
        out_specs=[HBM_SPEC] * n_w,
        out_shape=[jax.ShapeDtypeStruct((3,) + _piece_shape(*entry), F32) for entry in BIG],
        scratch_shapes=[pltpu.SemaphoreType.DMA((3 * n_w,)), pltpu.SemaphoreType.DMA((3 * n_w,))],
    )(*partials)


def _add_chips(partial, recv, mine, name, axis):
    rows, cols = recv.shape[1:]
    tr = _tile(rows, 256, 8)
    if name == STACKED or axis == 0:
        p_spec = pl.BlockSpec((None, tr, cols), lambda i, k_ref: (k_ref[0], i, 0))
    else:
        p_spec = pl.BlockSpec((tr, cols), lambda i, k_ref: (i, k_ref[0]))

    def body(k_ref, p_ref, r_ref, o_ref):
        o_ref[...] = ((p_ref[...] + r_ref[0]) + r_ref[1]) + r_ref[2]

    return pl.pallas_call(
        body,
        name="add_chips_" + name,
        grid_spec=pltpu.PrefetchScalarGridSpec(
            num_scalar_prefetch=1, grid=(rows // tr,),
            in_specs=[p_spec, pl.BlockSpec((3, tr, cols), lambda i, k_ref: (0, i, 0))],
            out_specs=pl.BlockSpec((tr, cols), lambda i, k_ref: (i, 0))),
        out_shape=jax.ShapeDtypeStruct((rows, cols), F32),
        compiler_params=_cparams(("parallel",)),
    )(mine, partial, recv)


def _share_with_sibling(halves):
    n_w = len(halves)

    def body(*refs):
        ins, outs = refs[:n_w], refs[n_w:2 * n_w]
        send_sems, recv_sems, local_sems = refs[2 * n_w:]
        x, y, c = _coords()
        local = [pltpu.make_async_copy(ins[i], outs[i].at[c], local_sems.at[i]) for i in range(n_w)]
        sends = [pltpu.make_async_remote_copy(ins[i], outs[i].at[c], send_sems.at[i], recv_sems.at[i],
                                              device_id=(x, y, 1 - c), device_id_type=MESH_ID) for i in range(n_w)]
        for cp in local + sends:
            cp.start()
        for i in range(n_w):
            pltpu.make_async_remote_copy(ins[i], outs[i].at[1 - c], send_sems.at[i], recv_sems.at[i],
                                         device_id=(x, y, 1 - c), device_id_type=MESH_ID).wait_recv()
        for cp in sends:
            cp.wait_send()
        for cp in local:
            cp.wait()

    return pl.pallas_call(
        body,
        name="share_with_sibling",
        in_specs=[HBM_SPEC] * n_w,
        out_specs=[HBM_SPEC] * n_w,
        out_shape=[jax.ShapeDtypeStruct((2,) + h.shape, F32) for h in halves],
        scratch_shapes=[pltpu.SemaphoreType.DMA((n_w,)), pltpu.SemaphoreType.DMA((n_w,)),
                        pltpu.SemaphoreType.DMA((n_w,))],
    )(*halves)


SMALL_ROWS = 168


def _all_reduce_small(buf):
    def body(b_ref, out_ref, gathered, send_sems, recv_sems):
        x, y, c = _coords()
        me = 4 * x + 2 * y + c
        peers = [(x ^ fx, y ^ fy, c ^ fc) for fx in (0, 1) for fy in (0, 1) for fc in (0, 1)][1:]

        def copy(j, slot, dev):
            return pltpu.make_async_remote_copy(b_ref, gathered.at[slot], send_sems.at[j], recv_sems.at[j],
                                                device_id=dev, device_id_type=MESH_ID)

        for j, dev in enumerate(peers):
            copy(j, me, dev).start()
        gathered[me] = b_ref[...]
        for j, dev in enumerate(peers):
            copy(j, 4 * dev[0] + 2 * dev[1] + dev[2], dev).wait()
        acc = gathered[0]
        for d in range(1, N_DEV):
            acc = acc + gathered[d]
        out_ref[...] = acc

    return pl.pallas_call(
        body,
        name="all_reduce_small",
        in_specs=[VMEM_SPEC],
        out_specs=VMEM_SPEC,
        out_shape=jax.ShapeDtypeStruct(buf.shape, F32),
        scratch_shapes=[pltpu.VMEM((N_DEV,) + buf.shape, F32), pltpu.SemaphoreType.DMA((N_DEV - 1,)),
                        pltpu.SemaphoreType.DMA((N_DEV - 1,))],
    )(buf)


def _adamw(w, g, m, v):
    r, c = w.shape
    tr = _tile(r, 256, 8)

    def body(w_ref, g_ref, m_ref, v_ref, d_ref, mo_ref, vo_ref):
        gg = g_ref[...]
        mm = ADAM_B1 * m_ref[...] + (1.0 - ADAM_B1) * gg
        vv = ADAM_B2 * v_ref[...] + (1.0 - ADAM_B2) * (gg * gg)
        m_hat = mm / (1.0 - ADAM_B1 ** ADAM_STEP)
        v_hat = vv / (1.0 - ADAM_B2 ** ADAM_STEP)
        d_ref[...] = -ADAM_LR * (m_hat / (jnp.sqrt(v_hat) + ADAM_EPS) + ADAM_WD * w_ref[...])
        mo_ref[...] = mm
        vo_ref[...] = vv

    spec = pl.BlockSpec((tr, c), lambda i: (i, 0))
    return pl.pallas_call(
        body,
        name="adamw",
        grid=(r // tr,),
        in_specs=[spec] * 4,
        out_specs=[spec] * 3,
        out_shape=[jax.ShapeDtypeStruct((r, c), F32)] * 3,
        compiler_params=_cparams(("parallel",)),
    )(w, g, m, v)


def _reduce_scatter(grads, c_arr, mine_arr):
    views = [_halves_view(*entry) for entry in BIG]
    received = _exchange_halves(grads)
    partials = [_add_sibling(g.reshape(v[0]), r, c_arr, name)
                for g, v, r, (name, _, _) in zip(grads, views, received, BIG)]
    pieces = _scatter_chip_sums(partials)
    halves = [_add_chips(p, r, mine_arr, name, axis) for p, r, (name, _, axis) in zip(partials, pieces, BIG)]
    return [s.reshape(2 * s.shape[1], s.shape[2]) for s in _share_with_sibling(halves)]


def kernel(x, meta_tokens, ffn1_norm, ffn1_w_in, ffn1_w_out, mix_norm, w_in, b_forget, attn_sinks, w_branch_a, w_branch_b, w_out, ffn2_norm, ffn2_w_in, ffn2_w_out, final_norm, loss_target, m_meta_tokens, m_ffn1_norm, m_ffn1_w_in, m_ffn1_w_out, m_mix_norm, m_w_in, m_b_forget, m_attn_sinks, m_w_branch_a, m_w_branch_b, m_w_out, m_ffn2_norm, m_ffn2_w_in, m_ffn2_w_out, m_final_norm, v_meta_tokens, v_ffn1_norm, v_ffn1_w_in, v_ffn1_w_out, v_mix_norm, v_w_in, v_b_forget, v_attn_sinks, v_w_branch_a, v_w_branch_b, v_w_out, v_ffn2_norm, v_ffn2_w_in, v_ffn2_w_out, v_final_norm):
    given = dict(locals())
    names = ["meta_tokens", "ffn1_norm", "ffn1_w_in", "ffn1_w_out", "mix_norm", "w_in", "b_forget", "attn_sinks",
             "w_branch_a", "w_branch_b", "w_out", "ffn2_norm", "ffn2_w_in", "ffn2_w_out", "final_norm"]
    big_names = [n for n, _, _ in BIG]
    cx, cy, cc = _coords()
    c_arr = cc.reshape(1).astype(jnp.int32)
    mine_arr = (2 * cx + cy).reshape(1).astype(jnp.int32)

    *full, meta_full = _gather_weights([given[n][0].astype(BF16) for n in big_names], meta_tokens)
    weights = dict(zip(big_names, full))
    weights[STACKED] = weights[STACKED].transpose(1, 0, 2).reshape(D_MODEL, W_IN_COLS)
    norms = (ffn1_norm, mix_norm, ffn2_norm, final_norm.reshape(1, D_MODEL))
    loss, grad_x, small, big = _local_step(x, loss_target, meta_full, norms, b_forget, attn_sinks,
                                           [weights[n] for n in big_names])

    big[STACKED] = big[STACKED].reshape(D_MODEL, N_CHIPS, W_IN_COLS // N_CHIPS).transpose(1, 0, 2)
    grads = dict(zip(big_names, _reduce_scatter([big[n] for n in big_names], c_arr, mine_arr)))

    pad_lanes = lambda a: jnp.concatenate([a, jnp.zeros((1, LANES - a.shape[1]), F32)], axis=1)
    buf = jnp.concatenate([
        small["meta_tokens"].reshape(128, LANES),
        small["ffn1_norm"].reshape(8, LANES), small["mix_norm"].reshape(8, LANES),
        small["ffn2_norm"].reshape(8, LANES), small["final_norm"].reshape(8, LANES),
        loss, pad_lanes(small["b_forget"]), pad_lanes(small["attn_sinks"]),
        jnp.zeros((SMALL_ROWS - 163, LANES), F32)], axis=0)
    red = _all_reduce_small(buf)
    meta_cols = red[:128].reshape(N_META, D_MODEL)
    grads["meta_tokens"] = lax.dynamic_slice_in_dim(meta_cols, (2 * cx + cy) * (D_MODEL // N_CHIPS),
                                                    D_MODEL // N_CHIPS, axis=1)
    grads["ffn1_norm"] = red[128:136].reshape(1, D_MODEL)
    grads["mix_norm"] = red[136:144].reshape(1, D_MODEL)
    grads["ffn2_norm"] = red[144:152].reshape(1, D_MODEL)
    grads["final_norm"] = red[152:160].reshape(1, D_MODEL)
    loss_out = red[160, 0]
    grads["b_forget"] = red[161:162, :B_HEADS]
    grads["attn_sinks"] = red[162:163, :A_HEADS]

    out_g, out_d, out_m, out_v = [], [], [], []
    for n in names:
        w_full = given[n]
        shape = w_full.shape
        two_d = (lambda a: a.reshape(shape[-2], shape[-1])) if len(shape) >= 2 else (lambda a: a.reshape(1, shape[0]))
        g2 = two_d(grads[n])
        d2, m2, v2 = _adamw(two_d(w_full), g2, two_d(given["m_" + n]), two_d(given["v_" + n]))
        out_g.append(g2.reshape(shape))
        out_d.append(d2.reshape(shape))
        out_m.append(m2.reshape(shape))
        out_v.append(v2.reshape(shape))
    return (loss_out, grad_x, *out_g, *out_d, *out_m, *out_v)
```

```python
import jax
import jax.numpy as jnp
from jax import lax
from jax.experimental import pallas as pl
from jax.experimental.pallas import tpu as pltpu

F32 = jnp.float32
BF16 = jnp.bfloat16

D_MODEL = 1024
N_META = 16
BLOCK = 128
LANES = 128
PREFIX = BLOCK
N_PAD = PREFIX - N_META
HEAD_DIM = 64
A_HEADS = 8
A_KV_HEADS = 2
A_GROUP = 4
B_HEADS = 8
B_PAIRS = B_HEADS // 2
A_WIDTH = A_HEADS * HEAD_DIM
A_KV_WIDTH = A_KV_HEADS * HEAD_DIM
B_WIDTH = B_HEADS * HEAD_DIM
W_IN_COLS = A_WIDTH + 2 * A_KV_WIDTH + 3 * B_WIDTH + B_HEADS + 2 * D_MODEL
SRC_KA = A_WIDTH
SRC_VA = SRC_KA + A_KV_WIDTH
SRC_QB = SRC_VA + A_KV_WIDTH
SRC_KB = SRC_QB + B_WIDTH
SRC_VB = SRC_KB + B_WIDTH
SRC_F = SRC_VB + B_WIDTH
SRC_GA = SRC_F + B_HEADS
SRC_GB = SRC_GA + D_MODEL
A_PAD_WIDTH = A_HEADS * LANES
B_PAD_WIDTH = B_HEADS * LANES
F_COLS = LANES
OFF_QA = 0
OFF_KA = OFF_QA + A_PAD_WIDTH
OFF_VA = OFF_KA + A_KV_WIDTH
OFF_QB = OFF_VA + A_KV_WIDTH
OFF_KB = OFF_QB + B_WIDTH
OFF_VB = OFF_KB + B_PAD_WIDTH
OFF_GA = OFF_VB + B_PAD_WIDTH
OFF_GB = OFF_GA + D_MODEL
OFF_F = OFF_GB + D_MODEL
TAIL_COLS = 128
P_COLS = OFF_F + F_COLS + TAIL_COLS
EPS = 1e-6
NEG = -1e30
SCALE = HEAD_DIM ** -0.5
KEY_BLOCKS = 4

ADAM_LR = 0.001
ADAM_B1 = 0.9
ADAM_B2 = 0.999
ADAM_EPS = 1e-08
ADAM_WD = 0.01
ADAM_STEP = 10

N_CHIPS = 4
N_DEV = 8
VMEM_LIMIT = 56 * 1024 * 1024

NT_DIMS = (((1,), (1,)), ((), ()))
TN_DIMS = (((0,), (0,)), ((), ()))
MESH_ID = pl.DeviceIdType.MESH
HBM_SPEC = pl.BlockSpec(memory_space=pltpu.HBM)
VMEM_SPEC = pl.BlockSpec(memory_space=pltpu.VMEM)


def _tile(n, target, mult=16):
    best = None
    for t in range(mult, min(n, target) + 1, mult):
        if n % t == 0:
            best = t
    return best if best is not None else n


def _cparams(sem):
    return pltpu.CompilerParams(dimension_semantics=sem, vmem_limit_bytes=VMEM_LIMIT)


def _rms_scale(h):
    return lax.rsqrt(jnp.mean(h * h, axis=-1, keepdims=True) + EPS)


def _rms_bwd(dn, h, w):
    r = _rms_scale(h)
    dw = jnp.sum(dn * (h * r), axis=0, keepdims=True)
    z = dn * w
    dh = r * z - h * ((r * r * r) * jnp.mean(z * h, axis=-1, keepdims=True))
    return dh, dw


def _ffn_fwd(h, norm_w, w_in, w_out):
    t, d = h.shape
    f = w_out.shape[0]
    tm = _tile(t, 544)
    tc = _tile(f, 256, 128)
    nj = f // tc

    def body(h_ref, nw_ref, wg_ref, wu_ref, wo_ref, hout_ref, g_ref, u_ref, n_scr, acc_scr):
        j = pl.program_id(1)

        @pl.when(j == 0)
        def _():
            hh = h_ref[...]
            n_scr[...] = ((hh * _rms_scale(hh)) * nw_ref[...]).astype(BF16)
            acc_scr[...] = jnp.zeros_like(acc_scr)

        n = n_scr[...]
        g = jnp.dot(n, wg_ref[...], preferred_element_type=F32)
        u = jnp.dot(n, wu_ref[...], preferred_element_type=F32)
        g_ref[...] = g
        u_ref[...] = u
        a = (g * jax.nn.sigmoid(g)) * u
        acc_scr[...] += jnp.dot(a.astype(BF16), wo_ref[...], preferred_element_type=F32)

        @pl.when(j == nj - 1)
        def _():
            hout_ref[...] = h_ref[...] + 0.5 * acc_scr[...]

    return pl.pallas_call(
        body,
        name="ffn_fwd",
        grid=(t // tm, nj),
        in_specs=[
            pl.BlockSpec((tm, d), lambda i, j: (i, 0)),
            pl.BlockSpec((1, d), lambda i, j: (0, 0)),
            pl.BlockSpec((d, tc), lambda i, j: (0, j)),
            pl.BlockSpec((d, tc), lambda i, j: (0, j + nj)),
            pl.BlockSpec((tc, d), lambda i, j: (j, 0)),
        ],
        out_specs=[
            pl.BlockSpec((tm, d), lambda i, j: (i, 0)),
            pl.BlockSpec((tm, tc), lambda i, j: (i, j)),
            pl.BlockSpec((tm, tc), lambda i, j: (i, j)),
        ],
        out_shape=[
            jax.ShapeDtypeStruct((t, d), F32),
            jax.ShapeDtypeStruct((t, f), F32),
            jax.ShapeDtypeStruct((t, f), F32),
        ],
        scratch_shapes=[pltpu.VMEM((tm, d), BF16), pltpu.VMEM((tm, d), F32)],
        compiler_params=_cparams(("parallel", "arbitrary")),
    )(h, norm_w, w_in, w_in, w_out)


def _ffn_bwd(dh_out, h, norm_w, g, u, w_in, w_out):
    t, d = h.shape
    f = w_out.shape[0]
    tm = _tile(t, 544)
    tc = _tile(f, 256, 128)
    nj = f // tc
    ni = t // tm

    def body(dho_ref, h_ref, nw_ref, g_ref, u_ref, wg_ref, wu_ref, wo_ref,
             dhin_ref, n_ref, a_ref, dgu_ref, df_ref, dnw_ref, dn_scr):
        j = pl.program_id(1)

        @pl.when(j == 0)
        def _():
            hh = h_ref[...]
            n_ref[...] = ((hh * _rms_scale(hh)) * nw_ref[...]).astype(BF16)
            df_ref[...] = (0.5 * dho_ref[...]).astype(BF16)
            dn_scr[...] = jnp.zeros_like(dn_scr)

        da = lax.dot_general(df_ref[...], wo_ref[...], NT_DIMS, preferred_element_type=F32)
        gg = g_ref[...]
        uu = u_ref[...]
        sig = jax.nn.sigmoid(gg)
        sl = gg * sig
        a_ref[...] = (sl * uu).astype(BF16)
        dg = ((da * uu) * (sig * (1.0 + gg * (1.0 - sig)))).astype(BF16)
        du = (da * sl).astype(BF16)
        dgu_ref[0] = dg
        dgu_ref[1] = du
        dn_scr[...] += (lax.dot_general(dg, wg_ref[...], NT_DIMS, preferred_element_type=F32)
                        + lax.dot_general(du, wu_ref[...], NT_DIMS, preferred_element_type=F32))

        @pl.when(j == nj - 1)
        def _():
            dh, dw = _rms_bwd(dn_scr[...], h_ref[...], nw_ref[...])
            dhin_ref[...] = dho_ref[...] + dh
            dnw_ref[0] = dw

    return pl.pallas_call(
        body,
        name="ffn_bwd",
        grid=(ni, nj),
        in_specs=[
            pl.BlockSpec((tm, d), lambda i, j: (i, 0)),
            pl.BlockSpec((tm, d), lambda i, j: (i, 0)),
            pl.BlockSpec((1, d), lambda i, j: (0, 0)),
            pl.BlockSpec((tm, tc), lambda i, j: (i, j)),
            pl.BlockSpec((tm, tc), lambda i, j: (i, j)),
            pl.BlockSpec((d, tc), lambda i, j: (0, j)),
            pl.BlockSpec((d, tc), lambda i, j: (0, j + nj)),
            pl.BlockSpec((tc, d), lambda i, j: (j, 0)),
        ],
        out_specs=[
            pl.BlockSpec((tm, d), lambda i, j: (i, 0)),
            pl.BlockSpec((tm, d), lambda i, j: (i, 0)),
            pl.BlockSpec((tm, tc), lambda i, j: (i, j)),
            pl.BlockSpec((2, tm, tc), lambda i, j: (0, i, j)),
            pl.BlockSpec((tm, d), lambda i, j: (i, 0)),
            pl.BlockSpec((1, 1, d), lambda i, j: (i, 0, 0)),
        ],
        out_shape=[
            jax.ShapeDtypeStruct((t, d), F32),
            jax.ShapeDtypeStruct((t, d), BF16),
            jax.ShapeDtypeStruct((t, f), BF16),
            jax.ShapeDtypeStruct((2, t, f), BF16),
            jax.ShapeDtypeStruct((t, d), BF16),
            jax.ShapeDtypeStruct((ni, 1, d), F32),
        ],
        scratch_shapes=[pltpu.VMEM((tm, d), F32)],
        compiler_params=_cparams(("parallel", "arbitrary")),
    )(dh_out, h, norm_w, g, u, w_in, w_in, w_out)


def _tn_matmul(a, b, name):
    t, k = a.shape
    split = b.ndim == 3
    n = 2 * b.shape[2] if split else b.shape[1]
    tt = _tile(t, 1088)
    tk = _tile(k, 512, 128)
    tn = _tile(b.shape[-1], 1408, 128)
    nt = t // tt
    per_half = b.shape[-1] // tn

    def body(a_ref, b_ref, o_ref):
        s = pl.program_id(2)
        part = lax.dot_general(a_ref[...], b_ref[...], TN_DIMS, preferred_element_type=F32)

        @pl.when(s == 0)
        def _():
            o_ref[...] = part

        @pl.when(s > 0)
        def _():
            o_ref[...] += part

    if split:
        b_spec = pl.BlockSpec((None, tt, tn), lambda i, j, s: (j // per_half, s, j % per_half))
    else:
        b_spec = pl.BlockSpec((tt, tn), lambda i, j, s: (s, j))
    return pl.pallas_call(
        body,
        name=name,
        grid=(k // tk, n // tn, nt),
        in_specs=[pl.BlockSpec((tt, tk), lambda i, j, s: (s, i)), b_spec],
        out_specs=pl.BlockSpec((tk, tn), lambda i, j, s: (i, j)),
        out_shape=jax.ShapeDtypeStruct((k, n), F32),
        compiler_params=_cparams(("parallel", "parallel", "arbitrary")),
    )(a, b)


PROJ_PARTS = (
    (OFF_QA, A_PAD_WIDTH, True), (OFF_KA, A_KV_WIDTH, True), (OFF_VA, A_KV_WIDTH, True),
    (OFF_QB, B_WIDTH, True), (OFF_KB, B_PAD_WIDTH, True), (OFF_VB, B_PAD_WIDTH, True),
    (OFF_GA, D_MODEL, False), (OFF_GB, D_MODEL, False), (OFF_F, F_COLS, False),
)


def _proj_fwd(h, norm_w, w_p):
    t, d = h.shape
    tm = _tile(t, 272)

    def body(h_ref, nw_ref, w_ref, u_ref, *part_refs):
        hh = h_ref[...]
        un = ((hh * _rms_scale(hh)) * nw_ref[...]).astype(BF16)
        u_ref[...] = un
        for (off, width, _), p_ref in zip(PROJ_PARTS, part_refs):
            p_ref[...] = jnp.dot(un, w_ref[:, off:off + width], preferred_element_type=F32).astype(p_ref.dtype)

    row = lambda w: pl.BlockSpec((tm, w), lambda i: (i, 0))
    return pl.pallas_call(
        body,
        name="proj_fwd",
        grid=(t // tm,),
        in_specs=[row(d), pl.BlockSpec((1, d), lambda i: (0, 0)), pl.BlockSpec(w_p.shape, lambda i: (0, 0))],
        out_specs=[row(d)] + [row(width) for _, width, _ in PROJ_PARTS],
        out_shape=[jax.ShapeDtypeStruct((t, d), BF16)]
        + [jax.ShapeDtypeStruct((t, width), BF16 if is_bf else F32) for _, width, is_bf in PROJ_PARTS],
        compiler_params=_cparams(("parallel",)),
    )(h, norm_w, w_p)


def _proj_bwd(dh_out, h, norm_w, dproj, w_p):
    t, d = h.shape
    n = w_p.shape[1]
    tm = _tile(t, 544)
    tn = _tile(n, 768, 128)
    nj = n // tn
    ni = t // tm

    def body(dho_ref, h_ref, nw_ref, dp_ref, w_ref, dhin_ref, dnw_ref, dn_scr):
        j = pl.program_id(1)

        @pl.when(j == 0)
        def _():
            dn_scr[...] = jnp.zeros_like(dn_scr)

        dn_scr[...] += lax.dot_general(dp_ref[...], w_ref[...], NT_DIMS, preferred_element_type=F32)

        @pl.when(j == nj - 1)
        def _():
            dh, dw = _rms_bwd(dn_scr[...], h_ref[...], nw_ref[...])
            dhin_ref[...] = dho_ref[...] + dh
            dnw_ref[0] = dw

    return pl.pallas_call(
        body,
        name="proj_bwd",
        grid=(ni, nj),
        in_specs=[
            pl.BlockSpec((tm, d), lambda i, j: (i, 0)),
            pl.BlockSpec((tm, d), lambda i, j: (i, 0)),
            pl.BlockSpec((1, d), lambda i, j: (0, 0)),
            pl.BlockSpec((tm, tn), lambda i, j: (i, j)),
            pl.BlockSpec((d, tn), lambda i, j: (0, j)),
        ],
        out_specs=[
            pl.BlockSpec((tm, d), lambda i, j: (i, 0)),
            pl.BlockSpec((1, 1, d), lambda i, j: (i, 0, 0)),
        ],
        out_shape=[jax.ShapeDtypeStruct((t, d), F32), jax.ShapeDtypeStruct((ni, 1, d), F32)],
        scratch_shapes=[pltpu.VMEM((tm, d), F32)],
        compiler_params=_cparams(("parallel", "arbitrary")),
    )(dh_out, h, norm_w, dproj, w_p)


def _merge_fwd(h, oa, ob, ga, gb, wa, wb, wo):
    t, d = h.shape
    tm = _tile(t, 544)

    def body(h_ref, oa_ref, ob_ref, ga_ref, gb_ref, wa_ref, wb_ref, wo_ref, hout_ref, mix_ref):
        ya = jnp.dot(oa_ref[...], wa_ref[...], preferred_element_type=F32)
        yb = jnp.dot(ob_ref[...], wb_ref[...], preferred_element_type=F32)
        mixed = (jax.nn.sigmoid(ga_ref[...]) * ya + jax.nn.sigmoid(gb_ref[...]) * yb).astype(BF16)
        mix_ref[...] = mixed
        hout_ref[...] = h_ref[...] + jnp.dot(mixed, wo_ref[...], preferred_element_type=F32)

    row = lambda w: pl.BlockSpec((tm, w), lambda i: (i, 0))
    full = lambda a: pl.BlockSpec(a.shape, lambda i: (0, 0))
    return pl.pallas_call(
        body,
        name="merge_fwd",
        grid=(t // tm,),
        in_specs=[row(d), row(oa.shape[1]), row(ob.shape[1]), row(d), row(d), full(wa), full(wb), full(wo)],
        out_specs=[row(d), row(d)],
        out_shape=[jax.ShapeDtypeStruct((t, d), F32), jax.ShapeDtypeStruct((t, d), BF16)],
        compiler_params=_cparams(("parallel",)),
    )(h, oa, ob, ga, gb, wa, wb, wo)


def _merge_bwd(dh, oa, ob, ga, gb, wa, wb, wo):
    t, d = dh.shape
    tm = _tile(t, 544)

    def body(dh_ref, oa_ref, ob_ref, ga_ref, gb_ref, wa_ref, wb_ref, wo_ref,
             dya_ref, dyb_ref, doa_ref, dob_ref, dga_ref, dgb_ref, dhb_ref):
        dhb = dh_ref[...].astype(BF16)
        dhb_ref[...] = dhb
        dmix = lax.dot_general(dhb, wo_ref[...], NT_DIMS, preferred_element_type=F32)
        for o_ref, g_ref, w_ref, dy_ref, do_ref, dg_ref in (
                (oa_ref, ga_ref, wa_ref, dya_ref, doa_ref, dga_ref),
                (ob_ref, gb_ref, wb_ref, dyb_ref, dob_ref, dgb_ref)):
            y = jnp.dot(o_ref[...], w_ref[...], preferred_element_type=F32)
            s = jax.nn.sigmoid(g_ref[...])
            dy = (dmix * s).astype(BF16)
            dy_ref[...] = dy
            dg_ref[...] = ((dmix * y) * (s * (1.0 - s))).astype(BF16)
            do_ref[...] = lax.dot_general(dy, w_ref[...], NT_DIMS, preferred_element_type=F32).astype(BF16)

    row = lambda w: pl.BlockSpec((tm, w), lambda i: (i, 0))
    full = lambda a: pl.BlockSpec(a.shape, lambda i: (0, 0))
    wa_w, wb_w = oa.shape[1], ob.shape[1]
    return pl.pallas_call(
        body,
        name="merge_bwd",
        grid=(t // tm,),
        in_specs=[row(d), row(wa_w), row(wb_w), row(d), row(d), full(wa), full(wb), full(wo)],
        out_specs=[row(d), row(d), row(wa_w), row(wb_w), row(d), row(d), row(d)],
        out_shape=[
            jax.ShapeDtypeStruct((t, d), BF16), jax.ShapeDtypeStruct((t, d), BF16),
            jax.ShapeDtypeStruct((t, wa_w), BF16), jax.ShapeDtypeStruct((t, wb_w), BF16),
            jax.ShapeDtypeStruct((t, d), BF16), jax.ShapeDtypeStruct((t, d), BF16),
            jax.ShapeDtypeStruct((t, d), BF16),
        ],
        compiler_params=_cparams(("parallel",)),
    )(dh, oa, ob, ga, gb, wa, wb, wo)


def _tri_dot(tri, x):
    hi = x.astype(BF16)
    r1 = x - hi.astype(F32)
    mid = r1.astype(BF16)
    lo = (r1 - mid.astype(F32)).astype(BF16)
    return (jnp.dot(tri, hi, preferred_element_type=F32)
            + jnp.dot(tri, mid, preferred_element_type=F32)
            + jnp.dot(tri, lo, preferred_element_type=F32))


def _forget_cumsum(f_logit, b_pad, nb):
    t, w = f_logit.shape
    bsz = t // (nb * BLOCK)

    def body(f_ref, b_ref, c_ref, carry):
        n = pl.program_id(1)

        @pl.when(n == 0)
        def _():
            carry[...] = jnp.zeros_like(carry)

        x = jax.nn.log_sigmoid(f_ref[...] + b_ref[...])
        rows = lax.broadcasted_iota(jnp.int32, (BLOCK, BLOCK), 0)
        cols = lax.broadcasted_iota(jnp.int32, (BLOCK, BLOCK), 1)
        tri = (cols <= rows).astype(BF16)
        c = _tri_dot(tri, x) + carry[...]
        c_ref[...] = c
        carry[...] = c[BLOCK - 1:BLOCK, :]

    return pl.pallas_call(
        body,
        name="forget_cumsum",
        grid=(bsz, nb),
        in_specs=[pl.BlockSpec((BLOCK, w), lambda b, n: (b * nb + n, 0)),
                  pl.BlockSpec((1, w), lambda b, n: (0, 0))],
        out_specs=pl.BlockSpec((BLOCK, w), lambda b, n: (b * nb + n, 0)),
        out_shape=jax.ShapeDtypeStruct((t, w), F32),
        scratch_shapes=[pltpu.VMEM((1, w), F32)],
        compiler_params=_cparams(("parallel", "arbitrary")),
    )(f_logit, b_pad)


def _forget_cumsum_bwd(dc, f_logit, b_pad, nb):
    t, w = f_logit.shape
    bsz = t // (nb * BLOCK)

    def body(dc_ref, f_ref, b_ref, df_ref, db_ref, carry):
        n = pl.program_id(1)

        @pl.when(n == 0)
        def _():
            carry[...] = jnp.zeros_like(carry)
            db_ref[...] = jnp.zeros_like(db_ref)

        rows = lax.broadcasted_iota(jnp.int32, (BLOCK, BLOCK), 0)
        cols = lax.broadcasted_iota(jnp.int32, (BLOCK, BLOCK), 1)
        tri = (cols >= rows).astype(BF16)
        dlf = _tri_dot(tri, dc_ref[...]) + carry[...]
        carry[...] = dlf[0:1, :]
        df = dlf * jax.nn.sigmoid(-(f_ref[...] + b_ref[...]))
        df_ref[...] = df.astype(BF16)
        db_ref[0] += jnp.sum(df, axis=0, keepdims=True)

    rev = lambda b, n: (b * nb + (nb - 1 - n), 0)
    return pl.pallas_call(
        body,
        name="forget_cumsum_bwd",
        grid=(bsz, nb),
        in_specs=[pl.BlockSpec((BLOCK, w), rev),
                  pl.BlockSpec((BLOCK, w), rev),
                  pl.BlockSpec((1, w), lambda b, n: (0, 0))],
        out_specs=[pl.BlockSpec((BLOCK, w), rev),
                   pl.BlockSpec((1, 1, w), lambda b, n: (b, 0, 0))],
        out_shape=[jax.ShapeDtypeStruct((t, w), BF16), jax.ShapeDtypeStruct((bsz, 1, w), F32)],
        scratch_shapes=[pltpu.VMEM((1, w), F32)],
        compiler_params=_cparams(("parallel", "arbitrary")),
    )(dc, f_logit, b_pad)


GROUP_ROWS = A_GROUP * BLOCK


def _stack_heads(ref):
    return jnp.concatenate([ref[:, i * LANES:(i + 1) * LANES] for i in range(A_GROUP)], axis=0)


def _unstack_heads(ref, x):
    for i in range(A_GROUP):
        ref[:, i * LANES:(i + 1) * LANES] = x[i * BLOCK:(i + 1) * BLOCK].astype(ref.dtype)


def _swa_logits(q, km, kp, kc, slope, n):
    qi = lax.broadcasted_iota(jnp.int32, (GROUP_ROWS, BLOCK), 0) & (BLOCK - 1)
    kj = lax.broadcasted_iota(jnp.int32, (GROUP_ROWS, BLOCK), 1)
    out = []
    for kk, dist, ok in (
            (km, n * BLOCK + qi - kj, (kj >= N_PAD) & (n * BLOCK + qi - kj >= 0)),
            (kp, BLOCK + qi - kj, (kj > qi) & (n >= 2)),
            (kc, qi - kj, (kj <= qi) & (n >= 1))):
        s = lax.dot_general(q, kk, NT_DIMS, preferred_element_type=F32) * SCALE
        s = s - slope * dist.astype(F32)
        out.append(jnp.where(ok, s, NEG))
    return out


def _swa_specs(nb):
    row = lambda b, n: b * nb + n
    qspec = pl.BlockSpec((BLOCK, A_GROUP * LANES), lambda b, g, n: (row(b, n), g))
    kv_m = pl.BlockSpec((BLOCK, LANES), lambda b, g, n: (row(b, 0), 0))
    kv_p = pl.BlockSpec((BLOCK, LANES), lambda b, g, n: (row(b, jnp.maximum(n - 1, 0)), 0))
    kv_c = pl.BlockSpec((BLOCK, LANES), lambda b, g, n: (row(b, n), 0))
    rowspec = pl.BlockSpec((1, GROUP_ROWS, 1), lambda b, g, n: (g, 0, 0))
    lsespec = pl.BlockSpec((1, 1, GROUP_ROWS, 1), lambda b, g, n: (row(b, n), g, 0, 0))
    return qspec, kv_m, kv_p, kv_c, rowspec, lsespec


def _swa_fwd(q, k, v, sink_rows, slope_rows, nb):
    t = q.shape[0]
    bsz = t // (nb * BLOCK)

    def body(q_ref, km_ref, kp_ref, kc_ref, vm_ref, vp_ref, vc_ref, sink_ref, slope_ref, o_ref, lse_ref):
        g = pl.program_id(1)
        n = pl.program_id(2)
        qq = _stack_heads(q_ref)
        sink = sink_ref[0]
        s_m, s_p, s_c = _swa_logits(qq, km_ref[...], kp_ref[...], kc_ref[...], slope_ref[0], n)
        m = jnp.maximum(jnp.maximum(jnp.max(s_m, axis=-1, keepdims=True), jnp.max(s_p, axis=-1, keepdims=True)),
                        jnp.maximum(jnp.max(s_c, axis=-1, keepdims=True), sink))
        e_m = jnp.exp(s_m - m)
        e_p = jnp.exp(s_p - m)
        e_c = jnp.exp(s_c - m)
        z = (jnp.sum(e_m, axis=-1, keepdims=True) + jnp.sum(e_p, axis=-1, keepdims=True)
             + jnp.sum(e_c, axis=-1, keepdims=True) + jnp.exp(sink - m))
        inv = 1.0 / z
        o = (jnp.dot((e_m * inv).astype(BF16), vm_ref[...], preferred_element_type=F32)
             + jnp.dot((e_p * inv).astype(BF16), vp_ref[...], preferred_element_type=F32)
             + jnp.dot((e_c * inv).astype(BF16), vc_ref[...], preferred_element_type=F32))
        lane_group = lax.broadcasted_iota(jnp.int32, (GROUP_ROWS, LANES), 1) // HEAD_DIM
        _unstack_heads(o_ref, jnp.where(lane_group == g, o, 0.0))
        lse_ref[0, 0] = m + jnp.log(z)

    qspec, kv_m, kv_p, kv_c, rowspec, lsespec = _swa_specs(nb)
    return pl.pallas_call(
        body,
        name="swa_fwd",
        grid=(bsz, A_KV_HEADS, nb),
        in_specs=[qspec, kv_m, kv_p, kv_c, kv_m, kv_p, kv_c, rowspec, rowspec],
        out_specs=[qspec, lsespec],
        out_shape=[jax.ShapeDtypeStruct((t, A_PAD_WIDTH), BF16),
                   jax.ShapeDtypeStruct((t // BLOCK, A_KV_HEADS, GROUP_ROWS, 1), F32)],
        compiler_params=_cparams(("parallel", "parallel", "arbitrary")),
    )(q, k, k, k, v, v, v, sink_rows, slope_rows)


def _swa_bwd(q, k, v, do, lse, sink_rows, slope_rows, nb):
    t = q.shape[0]
    l = nb * BLOCK
    bsz = t // l

    def body(q_ref, km_ref, kp_ref, kc_ref, vm_ref, vp_ref, vc_ref, do_ref, lse_ref, sink_ref, slope_ref,
             dq_ref, dk_ref, dv_ref, dsink_ref, dk_acc, dv_acc):
        g = pl.program_id(1)
        n = pl.program_id(2)

        @pl.when((g == 0) & (n == 0))
        def _():
            dk_acc[...] = jnp.zeros_like(dk_acc)
            dv_acc[...] = jnp.zeros_like(dv_acc)

        @pl.when(n == 0)
        def _():
            dsink_ref[...] = jnp.zeros_like(dsink_ref)

        qq = _stack_heads(q_ref)
        dob = _stack_heads(do_ref)
        lse = lse_ref[0, 0]
        logits = _swa_logits(qq, km_ref[...], kp_ref[...], kc_ref[...], slope_ref[0], n)
        probs = [jnp.exp(s - lse) for s in logits]
        dps = [lax.dot_general(dob, v_ref[...], NT_DIMS, preferred_element_type=F32)
               for v_ref in (vm_ref, vp_ref, vc_ref)]
        delta = sum(jnp.sum(p * dp, axis=-1, keepdims=True) for p, dp in zip(probs, dps))
        prev = jnp.maximum(n - 1, 0)
        dq = jnp.zeros((GROUP_ROWS, LANES), F32)
        for p, dp, k_ref, start in ((probs[0], dps[0], km_ref, 0),
                                    (probs[1], dps[1], kp_ref, prev * BLOCK),
                                    (probs[2], dps[2], kc_ref, n * BLOCK)):
            ds = (p * (dp - delta)).astype(BF16)
            dq = dq + jnp.dot(ds, k_ref[...], preferred_element_type=F32)
            rows = pl.ds(pl.multiple_of(start, BLOCK), BLOCK)
            dk_acc[rows, :] += lax.dot_general(ds, qq, TN_DIMS, preferred_element_type=F32) * SCALE
            dv_acc[rows, :] += lax.dot_general(p.astype(BF16), dob, TN_DIMS, preferred_element_type=F32)
        _unstack_heads(dq_ref, dq * SCALE)
        dsink_ref[0, 0] += -(jnp.exp(sink_ref[0] - lse) * delta)

        @pl.when((g == A_KV_HEADS - 1) & (n == nb - 1))
        def _():
            dk_ref[...] = dk_acc[...].astype(BF16)
            dv_ref[...] = dv_acc[...].astype(BF16)

    qspec, kv_m, kv_p, kv_c, rowspec, lsespec = _swa_specs(nb)
    kv_all = pl.BlockSpec((l, LANES), lambda b, g, n: (b, 0))
    return pl.pallas_call(
        body,
        name="swa_bwd",
        grid=(bsz, A_KV_HEADS, nb),
        in_specs=[qspec, kv_m, kv_p, kv_c, kv_m, kv_p, kv_c, qspec, lsespec, rowspec, rowspec],
        out_specs=[qspec, kv_all, kv_all,
                   pl.BlockSpec((1, 1, GROUP_ROWS, 1), lambda b, g, n: (b, g, 0, 0))],
        out_shape=[jax.ShapeDtypeStruct((t, A_PAD_WIDTH), BF16),
                   jax.ShapeDtypeStruct((t, LANES), BF16),
                   jax.ShapeDtypeStruct((t, LANES), BF16),
                   jax.ShapeDtypeStruct((bsz, A_KV_HEADS, GROUP_ROWS, 1), F32)],
        scratch_shapes=[pltpu.VMEM((l, LANES), F32), pltpu.VMEM((l, LANES), F32)],
        compiler_params=_cparams(("parallel", "arbitrary", "arbitrary")),
    )(q, k, k, k, v, v, v, do, lse, sink_rows, slope_rows)


CHUNK = KEY_BLOCKS * BLOCK


def _fox_chunk(qs, k_ref, cc_ref, cr_ref, qb, ci):
    sb = jnp.maximum(jnp.minimum(KEY_BLOCKS * ci, qb + 1 - KEY_BLOCKS), 0)
    lo = jnp.maximum(ci * CHUNK, N_PAD)
    krows = pl.ds(pl.multiple_of(sb * BLOCK, BLOCK), CHUNK)
    kk2 = jnp.concatenate([k_ref[krows, 0:LANES], k_ref[krows, LANES:2 * LANES]], axis=0)
    s01 = lax.dot_general(qs, kk2, NT_DIMS, preferred_element_type=F32)
    q_pos = qb * BLOCK + lax.broadcasted_iota(jnp.int32, (BLOCK, BLOCK), 0)
    lane = lax.broadcasted_iota(jnp.int32, (BLOCK, BLOCK), 1)
    logits = []
    for e in range(2):
        c_t = cc_ref[e]
        per_block = []
        for j in range(KEY_BLOCKS):
            k_pos = (sb + j) * BLOCK + lane
            col = (e * KEY_BLOCKS + j) * BLOCK
            s = s01[:, col:col + BLOCK] + c_t - cr_ref[e, sb + j]
            per_block.append(jnp.where((k_pos <= q_pos) & (k_pos >= lo), s, NEG))
        logits.append(per_block)
    return krows, sb, kk2, logits


def _fox_specs(nb):
    l = nb * BLOCK
    q_spec = pl.BlockSpec((BLOCK, LANES), lambda b, p, i: (b * nb + i, p))
    kv_spec = pl.BlockSpec((l, 2 * LANES), lambda b, p, i: (b, p))
    cc_spec = pl.BlockSpec((2, BLOCK, 1), lambda b, p, i: (b * B_PAIRS + p, i, 0))
    cr_spec = pl.BlockSpec((2, nb, 1, BLOCK), lambda b, p, i: (b * B_PAIRS + p, 0, 0, 0))
    return q_spec, kv_spec, cc_spec, cr_spec


def _fox_fwd(q, k, v, c_col, c_row, nb):
    t = q.shape[0]
    bsz = t // (nb * BLOCK)
    assert nb >= KEY_BLOCKS

    def body(q_ref, k_ref, v_ref, cc_ref, cr_ref, o_ref, ox_ref, lse_ref):
        qb = pl.program_id(2)
        qs = q_ref[...] * SCALE
        first_half = lax.broadcasted_iota(jnp.int32, (BLOCK, LANES), 1) < HEAD_DIM

        def step(ci, carry):
            m0, z0, m1, z1, acc, acc_lo = carry
            krows, _, _, logits = _fox_chunk(qs, k_ref, cc_ref, cr_ref, qb, ci)
            stats, p_hi, p_lo = [], [], []
            for (m, z), per_block in zip(((m0, z0), (m1, z1)), logits):
                m_new = m
                for s in per_block:
                    m_new = jnp.maximum(m_new, jnp.max(s, axis=-1, keepdims=True))
                alpha = jnp.exp(m - m_new)
                z = alpha * z
                for s in per_block:
                    p = jnp.exp(s - m_new)
                    z = z + jnp.sum(p, axis=-1, keepdims=True)
                    hi = p.astype(BF16)
                    p_hi.append(hi)
                    p_lo.append((p - hi.astype(F32)).astype(BF16))
                stats.append((m_new, z, alpha))
            vv2 = jnp.concatenate([v_ref[krows, 0:LANES], v_ref[krows, LANES:2 * LANES]], axis=0)
            alpha = jnp.where(first_half, stats[0][2], stats[1][2])
            acc = alpha * acc + jnp.dot(jnp.concatenate(p_hi, axis=1), vv2, preferred_element_type=F32)
            acc_lo = alpha * acc_lo + jnp.dot(jnp.concatenate(p_lo, axis=1), vv2, preferred_element_type=F32)
            return stats[0][0], stats[0][1], stats[1][0], stats[1][1], acc, acc_lo

        col = lambda val: jnp.full((BLOCK, 1), val, F32)
        m0, z0, m1, z1, acc, acc_lo = lax.fori_loop(
            0, (qb + KEY_BLOCKS) // KEY_BLOCKS, step,
            (col(NEG), col(0.0), col(NEG), col(0.0), jnp.zeros((BLOCK, LANES), F32), jnp.zeros((BLOCK, LANES), F32)))
        inv = 1.0 / jnp.where(first_half, z0, z1)
        o_ref[...] = (acc * inv).astype(BF16)
        ox_ref[...] = (acc + acc_lo) * inv
        lse_ref[0] = m0 + jnp.log(z0)
        lse_ref[1] = m1 + jnp.log(z1)

    q_spec, kv_spec, cc_spec, cr_spec = _fox_specs(nb)
    return pl.pallas_call(
        body,
        name="fox_fwd",
        grid=(bsz, B_PAIRS, nb),
        in_specs=[q_spec, kv_spec, kv_spec, cc_spec, cr_spec],
        out_specs=[q_spec, q_spec, cc_spec],
        out_shape=[jax.ShapeDtypeStruct((t, B_WIDTH), BF16), jax.ShapeDtypeStruct((t, B_WIDTH), F32),
                   jax.ShapeDtypeStruct((bsz * B_HEADS, nb * BLOCK, 1), F32)],
        compiler_params=_cparams(("parallel", "parallel", "arbitrary")),
    )(q, k, v, c_col, c_row)


def _fox_bwd(q, k, v, o_exact, do, lse, c_col, c_row, nb):
    t = q.shape[0]
    l = nb * BLOCK
    bsz = t // l

    def body(q_ref, k_ref, v_ref, ox_ref, do_ref, lse_ref, cc_ref, cr_ref,
             dq_ref, dk_ref, dv_ref, dc_ref, dk_acc, dv_acc):
        qb = pl.program_id(2)

        @pl.when(qb == 0)
        def _():
            dk_acc[...] = jnp.zeros_like(dk_acc)
            dv_acc[...] = jnp.zeros_like(dv_acc)
            dc_ref[...] = jnp.zeros_like(dc_ref)

        qs = q_ref[...] * SCALE
        dob = do_ref[...]
        first_half = lax.broadcasted_iota(jnp.int32, (BLOCK, LANES), 1) < HEAD_DIM
        weighted = dob.astype(F32) * ox_ref[...]
        deltas = (jnp.sum(jnp.where(first_half, weighted, 0.0), axis=-1, keepdims=True),
                  jnp.sum(jnp.where(first_half, 0.0, weighted), axis=-1, keepdims=True))

        def step(ci, dq):
            krows, sb, kk2, logits = _fox_chunk(qs, k_ref, cc_ref, cr_ref, qb, ci)
            vv2 = jnp.concatenate([v_ref[krows, 0:LANES], v_ref[krows, LANES:2 * LANES]], axis=0)
            dp01 = lax.dot_general(dob, vv2, NT_DIMS, preferred_element_type=F32)
            ps, dss = [], []
            for e in range(2):
                lse_e = lse_ref[e]
                for j in range(KEY_BLOCKS):
                    col = (e * KEY_BLOCKS + j) * BLOCK
                    p = jnp.exp(logits[e][j] - lse_e)
                    ds = p * (dp01[:, col:col + BLOCK] - deltas[e])
                    dc_ref[e, sb + j] -= jnp.sum(ds, axis=0, keepdims=True)
                    ps.append(p.astype(BF16))
                    dss.append(ds.astype(BF16))
            dsb = jnp.concatenate(dss, axis=1)
            pb = jnp.concatenate(ps, axis=1)
            dk2 = lax.dot_general(dsb, qs, TN_DIMS, preferred_element_type=F32)
            dv2 = lax.dot_general(pb, dob, TN_DIMS, preferred_element_type=F32)
            for e in range(2):
                dk_acc[krows, e * LANES:(e + 1) * LANES] += dk2[e * CHUNK:(e + 1) * CHUNK]
                dv_acc[krows, e * LANES:(e + 1) * LANES] += dv2[e * CHUNK:(e + 1) * CHUNK]
            return dq + jnp.dot(dsb, kk2, preferred_element_type=F32)

        dq = lax.fori_loop(0, (qb + KEY_BLOCKS) // KEY_BLOCKS, step, jnp.zeros((BLOCK, LANES), F32))
        dq_ref[...] = (dq * SCALE).astype(BF16)

        @pl.when(qb == nb - 1)
        def _():
            dk_ref[...] = dk_acc[...].astype(BF16)
            dv_ref[...] = dv_acc[...].astype(BF16)

    q_spec, kv_spec, cc_spec, cr_spec = _fox_specs(nb)
    return pl.pallas_call(
        body,
        name="fox_bwd",
        grid=(bsz, B_PAIRS, nb),
        in_specs=[q_spec, kv_spec, kv_spec, q_spec, q_spec, cc_spec, cc_spec, cr_spec],
        out_specs=[q_spec, kv_spec, kv_spec, cr_spec],
        out_shape=[jax.ShapeDtypeStruct((t, B_WIDTH), BF16), jax.ShapeDtypeStruct((t, B_PAD_WIDTH), BF16),
                   jax.ShapeDtypeStruct((t, B_PAD_WIDTH), BF16),
                   jax.ShapeDtypeStruct((bsz * B_HEADS, nb, 1, BLOCK), F32)],
        scratch_shapes=[pltpu.VMEM((l, 2 * LANES), F32), pltpu.VMEM((l, 2 * LANES), F32)],
        compiler_params=_cparams(("parallel", "parallel", "arbitrary")),
    )(q, k, v, o_exact, do, lse, c_col, c_row)


def _loss_head(h, final_w, target):
    bsz, l, d = h.shape
    nb = l // BLOCK

    def body(h_ref, w_ref, t_ref, loss_ref, dh_ref, dw_ref):
        b = pl.program_id(0)
        n = pl.program_id(1)

        @pl.when((b == 0) & (n == 0))
        def _():
            loss_ref[...] = jnp.zeros_like(loss_ref)
            dw_ref[...] = jnp.zeros_like(dw_ref)

        @pl.when(n == 0)
        def _():
            dh_ref[...] = jnp.zeros_like(dh_ref)

        @pl.when(n > 0)
        def _():
            hh = h_ref[0]
            w = w_ref[...]
            r = _rms_scale(hh)
            err = (hh * r) * w - t_ref[0]
            loss_ref[...] += 0.5 * jnp.sum(jnp.mean(err * err, axis=-1, keepdims=True), axis=0, keepdims=True)
            dy = err * (1.0 / d)
            dh, dw = _rms_bwd(dy, hh, w)
            dh_ref[0] = dh
            dw_ref[...] += dw

    return pl.pallas_call(
        body,
        name="loss_head",
        grid=(bsz, nb),
        in_specs=[
            pl.BlockSpec((1, BLOCK, d), lambda b, n: (b, n, 0)),
            pl.BlockSpec((1, d), lambda b, n: (0, 0)),
            pl.BlockSpec((1, BLOCK, d), lambda b, n: (b, jnp.maximum(n - 1, 0), 0)),
        ],
        out_specs=[
            pl.BlockSpec((1, 128), lambda b, n: (0, 0)),
            pl.BlockSpec((1, BLOCK, d), lambda b, n: (b, n, 0)),
            pl.BlockSpec((1, d), lambda b, n: (0, 0)),
        ],
        out_shape=[jax.ShapeDtypeStruct((1, 128), F32), jax.ShapeDtypeStruct((bsz, l, d), F32),
                   jax.ShapeDtypeStruct((1, d), F32)],
        compiler_params=_cparams(("arbitrary", "arbitrary")),
    )(h, final_w, target)


def _pad_tiles(w, src, heads, lane_slot, axis):
    pieces = []
    for h in range(heads):
        x = lax.slice_in_dim(w, src + HEAD_DIM * h, src + HEAD_DIM * (h + 1), axis=axis)
        z = jnp.zeros_like(x)
        pieces += [x, z] if lane_slot(h) == 0 else [z, x]
    return pieces


def _unpad_tiles(g, off, heads, lane_slot, axis):
    return [lax.slice_in_dim(g, off + LANES * h + HEAD_DIM * lane_slot(h),
                             off + LANES * h + HEAD_DIM * (lane_slot(h) + 1), axis=axis) for h in range(heads)]


A_SLOT = lambda h: h // A_GROUP
B_SLOT = lambda h: h % 2


def _layout_w_in(w):
    pad_f = jnp.zeros((w.shape[0], F_COLS - B_HEADS + TAIL_COLS), w.dtype)
    return jnp.concatenate(
        _pad_tiles(w, 0, A_HEADS, A_SLOT, 1) + [w[:, SRC_KA:SRC_KB]]
        + _pad_tiles(w, SRC_KB, B_HEADS, B_SLOT, 1) + _pad_tiles(w, SRC_VB, B_HEADS, B_SLOT, 1)
        + [w[:, SRC_GA:], w[:, SRC_F:SRC_GA], pad_f], axis=1)


def _unlayout_w_in(g):
    return jnp.concatenate(
        _unpad_tiles(g, OFF_QA, A_HEADS, A_SLOT, 1) + [g[:, OFF_KA:OFF_KB]]
        + _unpad_tiles(g, OFF_KB, B_HEADS, B_SLOT, 1) + _unpad_tiles(g, OFF_VB, B_HEADS, B_SLOT, 1)
        + [g[:, OFF_F:OFF_F + B_HEADS], g[:, OFF_GA:OFF_F]], axis=1)


def _local_step(x, target, meta, norms, b_forget, sinks, w):
    n1, nmix, n2, nfin = norms
    w1i, w1o, w_in, wa, wb, wo, w2i, w2o = w
    wp = _layout_w_in(w_in)
    wa_p = jnp.concatenate(_pad_tiles(wa, 0, A_HEADS, A_SLOT, 0), axis=0)
    bsz, seq, d = x.shape
    l = PREFIX + seq
    nb = l // BLOCK
    t = bsz * l

    h0 = jnp.concatenate([jnp.zeros((bsz, N_PAD, d), F32),
                          jnp.broadcast_to(meta[None], (bsz, N_META, d)), x], axis=1).reshape(t, d)

    h1, g1, u1 = _ffn_fwd(h0, n1, w1i, w1o)
    un, qa, ka, va, qb, kb, vb, ga, gb, f_logit = _proj_fwd(h1, nmix, wp)
    b_pad = jnp.concatenate([b_forget, jnp.zeros((1, F_COLS - B_HEADS), F32)], axis=1)
    c = _forget_cumsum(f_logit, b_pad, nb)
    c_heads = c[:, :B_HEADS].reshape(bsz, l, B_HEADS).transpose(0, 2, 1).reshape(bsz * B_HEADS, l)
    c_col = c_heads[:, :, None]
    c_row = c_heads.reshape(bsz * B_HEADS, nb, 1, BLOCK)

    slopes = jnp.exp2(-8.0 * jnp.arange(1, A_HEADS + 1, dtype=F32) / A_HEADS)
    slope_rows = jnp.repeat(slopes.reshape(A_KV_HEADS, A_GROUP), BLOCK, axis=1)[:, :, None]
    sink_rows = jnp.repeat(sinks.reshape(A_KV_HEADS, A_GROUP), BLOCK, axis=1)[:, :, None]

    oa, lse_a = _swa_fwd(qa, ka, va, sink_rows, slope_rows, nb)
    ob, ob_exact, lse_b = _fox_fwd(qb, kb, vb, c_col, c_row, nb)
    h2, mixed = _merge_fwd(h1, oa, ob, ga, gb, wa_p, wb, wo)
    h3, g2, u2 = _ffn_fwd(h2, n2, w2i, w2o)
    loss, dh3, d_nfin = _loss_head(h3.reshape(bsz, l, d), nfin, target)

    dh2, n2b, a2, dgu2, df2, dn2_parts = _ffn_bwd(dh3.reshape(t, d), h2, n2, g2, u2, w2i, w2o)
    g_w2o = _tn_matmul(a2, df2, "grad_ffn2_w_out")
    g_w2i = _tn_matmul(n2b, dgu2, "grad_ffn2_w_in")

    dya, dyb, doa, dob, dga, dgb, dh2b = _merge_bwd(dh2, oa, ob, ga, gb, wa_p, wb, wo)
    g_wo = _tn_matmul(mixed, dh2b, "grad_w_out")
    g_wa = jnp.concatenate(_unpad_tiles(_tn_matmul(oa, dya, "grad_w_branch_a"), 0, A_HEADS, A_SLOT, 0), axis=0)
    g_wb = _tn_matmul(ob, dyb, "grad_w_branch_b")

    dqa, dka, dva, dsink_rows = _swa_bwd(qa, ka, va, doa, lse_a, sink_rows, slope_rows, nb)
    dqb, dkb, dvb, dc_row = _fox_bwd(qb, kb, vb, ob_exact, dob, lse_b, c_col, c_row, nb)
    dc = dc_row.reshape(bsz, B_HEADS, l).transpose(0, 2, 1).reshape(t, B_HEADS)
    dc = jnp.concatenate([dc, jnp.zeros((t, F_COLS - B_HEADS), F32)], axis=1)
    df_logit, db_parts = _forget_cumsum_bwd(dc, f_logit, b_pad, nb)

    dproj = jnp.concatenate([dqa, dka, dva, dqb, dkb, dvb, dga, dgb, df_logit,
                             jnp.zeros((t, TAIL_COLS), BF16)], axis=1)
    dh1, dnmix_parts = _proj_bwd(dh2, h1, nmix, dproj, wp)
    g_win = _unlayout_w_in(_tn_matmul(un, dproj, "grad_w_in"))

    dh0, n1b, a1, dgu1, df1, dn1_parts = _ffn_bwd(dh1, h0, n1, g1, u1, w1i, w1o)
    g_w1o = _tn_matmul(a1, df1, "grad_ffn1_w_out")
    g_w1i = _tn_matmul(n1b, dgu1, "grad_ffn1_w_in")

    dh0 = dh0.reshape(bsz, l, d)
    grad_x = dh0[:, PREFIX:]
    small = dict(
        meta_tokens=jnp.sum(dh0[:, N_PAD:PREFIX], axis=0),
        ffn1_norm=jnp.sum(dn1_parts, axis=0),
        mix_norm=jnp.sum(dnmix_parts, axis=0),
        ffn2_norm=jnp.sum(dn2_parts, axis=0),
        final_norm=d_nfin,
        b_forget=jnp.sum(db_parts, axis=0)[:, :B_HEADS],
        attn_sinks=jnp.sum(dsink_rows.reshape(bsz, A_HEADS, BLOCK), axis=(0, 2)).reshape(1, A_HEADS),
    )
    big = dict(ffn1_w_in=g_w1i, ffn1_w_out=g_w1o, w_in=g_win, w_branch_a=g_wa, w_branch_b=g_wb,
               w_out=g_wo, ffn2_w_in=g_w2i, ffn2_w_out=g_w2o)
    return loss, grad_x, small, big


BIG = (
    ("ffn1_w_in", (D_MODEL, 5632), 1),
    ("ffn1_w_out", (2816, D_MODEL), 0),
    ("w_in", (D_MODEL, W_IN_COLS), 1),
    ("w_branch_a", (A_WIDTH, D_MODEL), 1),
    ("w_branch_b", (B_WIDTH, D_MODEL), 1),
    ("w_out", (D_MODEL, D_MODEL), 0),
    ("ffn2_w_in", (D_MODEL, 5632), 1),
    ("ffn2_w_out", (2816, D_MODEL), 0),
)
STACKED = "w_in"


def _coords():
    return lax.axis_index("x"), lax.axis_index("y"), lax.axis_index("c")


def _other_chips(x, y):
    return ((1 - x, y), (x, 1 - y), (1 - x, 1 - y))


def _chip_part(ref, name, shape, axis, k):
    if name == STACKED:
        return ref.at[k]
    size = shape[axis] // N_CHIPS
    start = pl.multiple_of(k * size, size)
    return ref.at[pl.ds(start, size), :] if axis == 0 else ref.at[:, pl.ds(start, size)]


def _full_shape(name, shape):
    return (N_CHIPS, shape[0], shape[1] // N_CHIPS) if name == STACKED else shape


def _gather_weights(shards, meta):
    n_w = len(BIG)

    def body(*refs):
        ins, outs = refs[:n_w + 1], refs[n_w + 1:2 * (n_w + 1)]
        send_sems, recv_sems, local_sems = refs[2 * (n_w + 1):]
        x, y, c = _coords()
        mine = 2 * x + y
        table = BIG + (("meta_tokens", (N_META, D_MODEL), 1),)

        def part(i, k):
            name, shape, axis = table[i]
            return _chip_part(outs[i], name, shape, axis, k)

        def half(ref, h):
            rows = ref.shape[0] // 2
            return ref.at[pl.ds(pl.multiple_of(h * rows, rows), rows), :]

        local = [pltpu.make_async_copy(ins[i], part(i, mine), local_sems.at[i]) for i in range(n_w + 1)]
        for cp in local:
            cp.start()

        def fetch(i, j, slot, chip):
            sem = i * 3 + j
            if i == n_w:
                src, dst = ins[i], part(i, slot)
            else:
                src, dst = half(ins[i], c), half(part(i, slot), c)
            return pltpu.make_async_remote_copy(src, dst, send_sems.at[sem], recv_sems.at[sem],
                                                device_id=(chip[0], chip[1], c), device_id_type=MESH_ID)

        def forward(i, j, slot, h):
            sem = 3 * (n_w + 1) + i * 3 + j
            region = half(part(i, slot), h)
            return pltpu.make_async_remote_copy(region, region, send_sems.at[sem], recv_sems.at[sem],
                                                device_id=(x, y, 1 - c), device_id_type=MESH_ID)

        chips = _other_chips(x, y)
        slots = [2 * chip[0] + chip[1] for chip in chips]
        for i in range(n_w + 1):
            for j, chip in enumerate(chips):
                fetch(i, j, mine, chip).start()
        for i in range(n_w + 1):
            for j, chip in enumerate(chips):
                fetch(i, j, slots[j], chip).wait_recv()
                if i < n_w:
                    forward(i, j, slots[j], c).start()
        for i in range(n_w):
            for j in range(3):
                forward(i, j, slots[j], 1 - c).wait_recv()
        for i in range(n_w + 1):
            for j, chip in enumerate(chips):
                fetch(i, j, mine, chip).wait_send()
                if i < n_w:
                    forward(i, j, slots[j], c).wait_send()
        for cp in local:
            cp.wait()

    out_shape = [jax.ShapeDtypeStruct(_full_shape(name, shape), BF16) for name, shape, _ in BIG]
    out_shape.append(jax.ShapeDtypeStruct((N_META, D_MODEL), F32))
    n_sems = 3 * (n_w + 1) + 3 * n_w
    return pl.pallas_call(
        body,
        name="gather_weights",
        in_specs=[HBM_SPEC] * (n_w + 1),
        out_specs=[HBM_SPEC] * (n_w + 1),
        out_shape=out_shape,
        scratch_shapes=[pltpu.SemaphoreType.DMA((n_sems,)), pltpu.SemaphoreType.DMA((n_sems,)),
                        pltpu.SemaphoreType.DMA((n_w + 1,))],
    )(*shards, meta)


def _halves_view(name, shape, axis):
    r, c = shape
    if name == STACKED:
        return (N_CHIPS, 2, r // 2, c // N_CHIPS), lambda ref, h: ref.at[:, h]
    if axis == 1:
        return (2, r // 2, c), lambda ref, h: ref.at[h]
    return (N_CHIPS, 2, r // N_CHIPS // 2, c), lambda ref, h: ref.at[:, h]


def _exchange_halves(grads):
    n_w = len(BIG)
    views = [_halves_view(*entry) for entry in BIG]

    def body(*refs):
        ins, outs = refs[:n_w], refs[n_w:2 * n_w]
        send_sems, recv_sems = refs[2 * n_w:]
        x, y, c = _coords()
        copies = [pltpu.make_async_remote_copy(views[i][1](ins[i], 1 - c), outs[i], send_sems.at[i], recv_sems.at[i],
                                               device_id=(x, y, 1 - c), device_id_type=MESH_ID) for i in range(n_w)]
        for cp in copies:
            cp.start()
        for cp in copies:
            cp.wait()

    half_shape = lambda v: tuple(d for i, d in enumerate(v) if i != (1 if len(v) == 4 else 0))
    return pl.pallas_call(
        body,
        name="exchange_halves",
        in_specs=[HBM_SPEC] * n_w,
        out_specs=[HBM_SPEC] * n_w,
        out_shape=[jax.ShapeDtypeStruct(half_shape(v[0]), F32) for v in views],
        scratch_shapes=[pltpu.SemaphoreType.DMA((n_w,)), pltpu.SemaphoreType.DMA((n_w,))],
    )(*[g.reshape(v[0]) for g, v in zip(grads, views)])


def _add_sibling(g_view, recv, c, name):
    shape = recv.shape
    if len(shape) == 2:
        tr = _tile(shape[0], 128, 16)
        grid = (shape[0] // tr,)
        g_spec = pl.BlockSpec((None, tr, shape[1]), lambda i, c_ref: (c_ref[0], i, 0))
        r_spec = pl.BlockSpec((tr, shape[1]), lambda i, c_ref: (i, 0))
    else:
        tr = _tile(shape[1], 256, 16)
        grid = (N_CHIPS, shape[1] // tr)
        g_spec = pl.BlockSpec((None, None, tr, shape[2]), lambda k, i, c_ref: (k, c_ref[0], i, 0))
        r_spec = pl.BlockSpec((None, tr, shape[2]), lambda k, i, c_ref: (k, i, 0))

    def body(c_ref, g_ref, r_ref, o_ref):
        o_ref[...] = (g_ref[...] + r_ref[...]).astype(BF16)

    return pl.pallas_call(
        body,
        name="add_sibling_" + name,
        grid_spec=pltpu.PrefetchScalarGridSpec(num_scalar_prefetch=1, grid=grid, in_specs=[g_spec, r_spec],
                                               out_specs=r_spec),
        out_shape=jax.ShapeDtypeStruct(shape, BF16),
        compiler_params=_cparams(("parallel",) * len(grid)),
    )(c, g_view, recv)


def _piece_of(ref, name, axis, k):
    if name == STACKED or axis == 0:
        return ref.at[k]
    size = ref.shape[1] // N_CHIPS
    return ref.at[:, pl.ds(pl.multiple_of(k * size, size), size)]


def _piece_shape(name, shape, axis):
    r, c = shape
    return (r // 2, c // N_CHIPS) if (axis == 1) else (r // N_CHIPS // 2, c)


def _scatter_chip_sums(partials):
    n_w = len(BIG)

    def body(*refs):
        ins, outs = refs[:n_w], refs[n_w:2 * n_w]
        send_sems, recv_sems = refs[2 * n_w:]
        x, y, c = _coords()
        chips = _other_chips(x, y)
        copies = []
        for i, (name, _, axis) in enumerate(BIG):
            for j, chip in enumerate(chips):
                sem = 3 * i + j
                copies.append(pltpu.make_async_remote_copy(
                    _piece_of(ins[i], name, axis, 2 * chip[0] + chip[1]), outs[i].at[j], send_sems.at[sem],
                    recv_sems.at[sem], device_id=(chip[0], chip[1], c), device_id_type=MESH_ID))
        for cp in copies:
            cp.start()
        for cp in copies:
            cp.wait()

    return pl.pallas_call(
        body,
        name="scatter_chip_sums",
        in_specs=[HBM_SPEC] * n_w,
        out_specs=[HBM_SPEC] * n_w,
        out_shape=[jax.ShapeDtypeStruct((3,) + _piece_shape(*entry), BF16) for entry in BIG],
        scratch_shapes=[pltpu.SemaphoreType.DMA((3 * n_w,)), pltpu.SemaphoreType.DMA((3 * n_w,))],
    )(*partials)


def _add_chips(partial, recv, mine, name, axis):
    rows, cols = recv.shape[1:]
    tr = _tile(rows, 256, 16)
    if name == STACKED or axis == 0:
        p_spec = pl.BlockSpec((None, tr, cols), lambda i, k_ref: (k_ref[0], i, 0))
    else:
        p_spec = pl.BlockSpec((tr, cols), lambda i, k_ref: (i, k_ref[0]))

    def body(k_ref, p_ref, r_ref, o_ref):
        f32 = lambda a: a.astype(F32)
        o_ref[...] = ((f32(p_ref[...]) + f32(r_ref[0])) + f32(r_ref[1])) + f32(r_ref[2])

    return pl.pallas_call(
        body,
        name="add_chips_" + name,
        grid_spec=pltpu.PrefetchScalarGridSpec(
            num_scalar_prefetch=1, grid=(rows // tr,),
            in_specs=[p_spec, pl.BlockSpec((3, tr, cols), lambda i, k_ref: (0, i, 0))],
            out_specs=pl.BlockSpec((tr, cols), lambda i, k_ref: (i, 0))),
        out_shape=jax.ShapeDtypeStruct((rows, cols), F32),
        compiler_params=_cparams(("parallel",)),
    )(mine, partial, recv)


def _share_with_sibling(halves):
    n_w = len(halves)

    def body(*refs):
        ins, outs = refs[:n_w], refs[n_w:2 * n_w]
        send_sems, recv_sems = refs[2 * n_w:]
        x, y, c = _coords()
        copies = [pltpu.make_async_remote_copy(ins[i], outs[i], send_sems.at[i], recv_sems.at[i],
                                               device_id=(x, y, 1 - c), device_id_type=MESH_ID) for i in range(n_w)]
        for cp in copies:
            cp.start()
        for cp in copies:
            cp.wait()

    return pl.pallas_call(
        body,
        name="share_with_sibling",
        in_specs=[HBM_SPEC] * n_w,
        out_specs=[HBM_SPEC] * n_w,
        out_shape=[jax.ShapeDtypeStruct(h.shape, F32) for h in halves],
        scratch_shapes=[pltpu.SemaphoreType.DMA((n_w,)), pltpu.SemaphoreType.DMA((n_w,))],
    )(*halves)


SMALL_ROWS = 168


def _all_reduce_small(buf):
    def body(b_ref, out_ref, gathered, send_sems, recv_sems):
        x, y, c = _coords()
        me = 4 * x + 2 * y + c
        peers = [(x ^ fx, y ^ fy, c ^ fc) for fx in (0, 1) for fy in (0, 1) for fc in (0, 1)][1:]

        def copy(j, slot, dev):
            return pltpu.make_async_remote_copy(b_ref, gathered.at[slot], send_sems.at[j], recv_sems.at[j],
                                                device_id=dev, device_id_type=MESH_ID)

        for j, dev in enumerate(peers):
            copy(j, me, dev).start()
        gathered[me] = b_ref[...]
        for j, dev in enumerate(peers):
            copy(j, 4 * dev[0] + 2 * dev[1] + dev[2], dev).wait()
        acc = gathered[0]
        for d in range(1, N_DEV):
            acc = acc + gathered[d]
        out_ref[...] = acc

    return pl.pallas_call(
        body,
        name="all_reduce_small",
        in_specs=[VMEM_SPEC],
        out_specs=VMEM_SPEC,
        out_shape=jax.ShapeDtypeStruct(buf.shape, F32),
        scratch_shapes=[pltpu.VMEM((N_DEV,) + buf.shape, F32), pltpu.SemaphoreType.DMA((N_DEV - 1,)),
                        pltpu.SemaphoreType.DMA((N_DEV - 1,))],
    )(buf)


def _adamw(w, g, m, v):
    r, c = w.shape
    tr = _tile(r, 256, 8)

    def body(w_ref, g_ref, m_ref, v_ref, d_ref, mo_ref, vo_ref):
        gg = g_ref[...]
        mm = ADAM_B1 * m_ref[...] + (1.0 - ADAM_B1) * gg
        vv = ADAM_B2 * v_ref[...] + (1.0 - ADAM_B2) * (gg * gg)
        m_hat = mm / (1.0 - ADAM_B1 ** ADAM_STEP)
        v_hat = vv / (1.0 - ADAM_B2 ** ADAM_STEP)
        d_ref[...] = -ADAM_LR * (m_hat / (jnp.sqrt(v_hat) + ADAM_EPS) + ADAM_WD * w_ref[...])
        mo_ref[...] = mm
        vo_ref[...] = vv

    spec = pl.BlockSpec((tr, c), lambda i: (i, 0))
    return pl.pallas_call(
        body,
        name="adamw",
        grid=(r // tr,),
        in_specs=[spec] * 4,
        out_specs=[spec] * 3,
        out_shape=[jax.ShapeDtypeStruct((r, c), F32)] * 3,
        compiler_params=_cparams(("parallel",)),
    )(w, g, m, v)


def _adamw_halves(w, own, other, m, v, c, name):
    r, cols = w.shape
    half = r // 2
    tr = _tile(half, 256, 8)
    nt = half // tr

    def body(c_ref, w_ref, own_ref, other_ref, m_ref, v_ref, g_ref, d_ref, mo_ref, vo_ref):
        gg = jnp.where(pl.program_id(0) == c_ref[0], own_ref[...], other_ref[...])
        g_ref[...] = gg
        mm = ADAM_B1 * m_ref[...] + (1.0 - ADAM_B1) * gg
        vv = ADAM_B2 * v_ref[...] + (1.0 - ADAM_B2) * (gg * gg)
        m_hat = mm / (1.0 - ADAM_B1 ** ADAM_STEP)
        v_hat = vv / (1.0 - ADAM_B2 ** ADAM_STEP)
        d_ref[...] = -ADAM_LR * (m_hat / (jnp.sqrt(v_hat) + ADAM_EPS) + ADAM_WD * w_ref[...])
        mo_ref[...] = mm
        vo_ref[...] = vv

    whole = pl.BlockSpec((tr, cols), lambda h, i, c_ref: (h * nt + i, 0))
    part = pl.BlockSpec((tr, cols), lambda h, i, c_ref: (i, 0))
    return pl.pallas_call(
        body,
        name="adamw_" + name,
        grid_spec=pltpu.PrefetchScalarGridSpec(
            num_scalar_prefetch=1, grid=(2, nt),
            in_specs=[whole, part, part, whole, whole], out_specs=[whole] * 4),
        out_shape=[jax.ShapeDtypeStruct((r, cols), F32)] * 4,
        compiler_params=_cparams(("parallel", "parallel")),
    )(c, w, own, other, m, v)


def _reduce_scatter(grads, c_arr, mine_arr):
    views = [_halves_view(*entry) for entry in BIG]
    received = _exchange_halves(grads)
    partials = [_add_sibling(g.reshape(v[0]), r, c_arr, name)
                for g, v, r, (name, _, _) in zip(grads, views, received, BIG)]
    pieces = _scatter_chip_sums(partials)
    halves = [_add_chips(p, r, mine_arr, name, axis) for p, r, (name, _, axis) in zip(partials, pieces, BIG)]
    return list(zip(halves, _share_with_sibling(halves)))


def kernel(x, meta_tokens, ffn1_norm, ffn1_w_in, ffn1_w_out, mix_norm, w_in, b_forget, attn_sinks, w_branch_a, w_branch_b, w_out, ffn2_norm, ffn2_w_in, ffn2_w_out, final_norm, loss_target, m_meta_tokens, m_ffn1_norm, m_ffn1_w_in, m_ffn1_w_out, m_mix_norm, m_w_in, m_b_forget, m_attn_sinks, m_w_branch_a, m_w_branch_b, m_w_out, m_ffn2_norm, m_ffn2_w_in, m_ffn2_w_out, m_final_norm, v_meta_tokens, v_ffn1_norm, v_ffn1_w_in, v_ffn1_w_out, v_mix_norm, v_w_in, v_b_forget, v_attn_sinks, v_w_branch_a, v_w_branch_b, v_w_out, v_ffn2_norm, v_ffn2_w_in, v_ffn2_w_out, v_final_norm):
    given = dict(locals())
    names = ["meta_tokens", "ffn1_norm", "ffn1_w_in", "ffn1_w_out", "mix_norm", "w_in", "b_forget", "attn_sinks",
             "w_branch_a", "w_branch_b", "w_out", "ffn2_norm", "ffn2_w_in", "ffn2_w_out", "final_norm"]
    big_names = [n for n, _, _ in BIG]
    cx, cy, cc = _coords()
    c_arr = cc.reshape(1).astype(jnp.int32)
    mine_arr = (2 * cx + cy).reshape(1).astype(jnp.int32)

    *full, meta_full = _gather_weights([given[n][0].astype(BF16) for n in big_names], meta_tokens)
    weights = dict(zip(big_names, full))
    weights[STACKED] = weights[STACKED].transpose(1, 0, 2).reshape(D_MODEL, W_IN_COLS)
    norms = (ffn1_norm, mix_norm, ffn2_norm, final_norm.reshape(1, D_MODEL))
    loss, grad_x, small, big = _local_step(x, loss_target, meta_full, norms, b_forget, attn_sinks,
                                           [weights[n] for n in big_names])

    big[STACKED] = big[STACKED].reshape(D_MODEL, N_CHIPS, W_IN_COLS // N_CHIPS).transpose(1, 0, 2)
    grad_halves = dict(zip(big_names, _reduce_scatter([big[n] for n in big_names], c_arr, mine_arr)))
    grads = {}

    pad_lanes = lambda a: jnp.concatenate([a, jnp.zeros((1, LANES - a.shape[1]), F32)], axis=1)
    buf = jnp.concatenate([
        small["meta_tokens"].reshape(128, LANES),
        small["ffn1_norm"].reshape(8, LANES), small["mix_norm"].reshape(8, LANES),
        small["ffn2_norm"].reshape(8, LANES), small["final_norm"].reshape(8, LANES),
        loss, pad_lanes(small["b_forget"]), pad_lanes(small["attn_sinks"]),
        jnp.zeros((SMALL_ROWS - 163, LANES), F32)], axis=0)
    red = _all_reduce_small(buf)
    meta_cols = red[:128].reshape(N_META, D_MODEL)
    grads["meta_tokens"] = lax.dynamic_slice_in_dim(meta_cols, (2 * cx + cy) * (D_MODEL // N_CHIPS),
                                                    D_MODEL // N_CHIPS, axis=1)
    grads["ffn1_norm"] = red[128:136].reshape(1, D_MODEL)
    grads["mix_norm"] = red[136:144].reshape(1, D_MODEL)
    grads["ffn2_norm"] = red[144:152].reshape(1, D_MODEL)
    grads["final_norm"] = red[152:160].reshape(1, D_MODEL)
    loss_out = red[160, 0]
    grads["b_forget"] = red[161:162, :B_HEADS]
    grads["attn_sinks"] = red[162:163, :A_HEADS]

    out_g, out_d, out_m, out_v = [], [], [], []
    for n in names:
        w_full = given[n]
        shape = w_full.shape
        two_d = (lambda a: a.reshape(shape[-2], shape[-1])) if len(shape) >= 2 else (lambda a: a.reshape(1, shape[0]))
        if n in grad_halves:
            own, other = grad_halves[n]
            g2, d2, m2, v2 = _adamw_halves(two_d(w_full), own, other, two_d(given["m_" + n]),
                                           two_d(given["v_" + n]), c_arr, n)
        else:
            g2 = two_d(grads[n])
            d2, m2, v2 = _adamw(two_d(w_full), g2, two_d(given["m_" + n]), two_d(given["v_" + n]))
        out_g.append(g2.reshape(shape))
        out_d.append(d2.reshape(shape))
        out_m.append(m2.reshape(shape))
        out_v.append(v2.reshape(shape))
    return (loss_out, grad_x, *out_g, *out_d, *out_m, *out_v)
```

```python
import jax
import jax.numpy as jnp
from jax import lax
from jax.experimental import pallas as pl
from jax.experimental.pallas import tpu as pltpu

F32 = jnp.float32
BF16 = jnp.bfloat16

D_MODEL = 1024
N_META = 16
BLOCK = 128
LANES = 128
PREFIX = BLOCK
N_PAD = PREFIX - N_META
HEAD_DIM = 64
A_HEADS = 8
A_KV_HEADS = 2
A_GROUP = 4
B_HEADS = 8
B_PAIRS = B_HEADS // 2
A_WIDTH = A_HEADS * HEAD_DIM
A_KV_WIDTH = A_KV_HEADS * HEAD_DIM
B_WIDTH = B_HEADS * HEAD_DIM
W_IN_COLS = A_WIDTH + 2 * A_KV_WIDTH + 3 * B_WIDTH + B_HEADS + 2 * D_MODEL
SRC_KA = A_WIDTH
SRC_VA = SRC_KA + A_KV_WIDTH
SRC_QB = SRC_VA + A_KV_WIDTH
SRC_KB = SRC_QB + B_WIDTH
SRC_VB = SRC_KB + B_WIDTH
SRC_F = SRC_VB + B_WIDTH
SRC_GA = SRC_F + B_HEADS
SRC_GB = SRC_GA + D_MODEL
A_PAD_WIDTH = A_HEADS * LANES
B_PAD_WIDTH = B_HEADS * LANES
F_COLS = LANES
OFF_QA = 0
OFF_KA = OFF_QA + A_PAD_WIDTH
OFF_VA = OFF_KA + A_KV_WIDTH
OFF_QB = OFF_VA + A_KV_WIDTH
OFF_KB = OFF_QB + B_WIDTH
OFF_VB = OFF_KB + B_PAD_WIDTH
OFF_GA = OFF_VB + B_PAD_WIDTH
OFF_GB = OFF_GA + D_MODEL
OFF_F = OFF_GB + D_MODEL
TAIL_COLS = 128
P_COLS = OFF_F + F_COLS + TAIL_COLS
EPS = 1e-6
NEG = -1e30
SCALE = HEAD_DIM ** -0.5
KEY_BLOCKS = 4

ADAM_LR = 0.001
ADAM_B1 = 0.9
ADAM_B2 = 0.999
ADAM_EPS = 1e-08
ADAM_WD = 0.01
ADAM_STEP = 10

N_CHIPS = 4
N_DEV = 8
VMEM_LIMIT = 56 * 1024 * 1024

NT_DIMS = (((1,), (1,)), ((), ()))
TN_DIMS = (((0,), (0,)), ((), ()))
MESH_ID = pl.DeviceIdType.MESH
HBM_SPEC = pl.BlockSpec(memory_space=pltpu.HBM)
VMEM_SPEC = pl.BlockSpec(memory_space=pltpu.VMEM)


def _tile(n, target, mult=16):
    best = None
    for t in range(mult, min(n, target) + 1, mult):
        if n % t == 0:
            best = t
    return best if best is not None else n


def _cparams(sem):
    return pltpu.CompilerParams(dimension_semantics=sem, vmem_limit_bytes=VMEM_LIMIT)


def _rms_scale(h):
    return lax.rsqrt(jnp.mean(h * h, axis=-1, keepdims=True) + EPS)


def _rms_bwd(dn, h, w):
    r = _rms_scale(h)
    dw = jnp.sum(dn * (h * r), axis=0, keepdims=True)
    z = dn * w
    dh = r * z - h * ((r * r * r) * jnp.mean(z * h, axis=-1, keepdims=True))
    return dh, dw


def _ffn_fwd(h, norm_w, w_in, w_out, exchange=None):
    t, d = h.shape
    f = w_out.shape[0]
    tm = _tile(t, 544)
    tc = _tile(f, 256, 128)
    nj = f // tc
    ni = t // tm
    n_x = len(exchange.ins) if exchange else 0

    def body(*refs):
        h_ref, nw_ref, wg_ref, wu_ref, wo_ref = refs[:5]
        hout_ref, g_ref, u_ref = refs[5 + n_x:8 + n_x]
        n_scr, acc_scr = refs[8 + 2 * n_x:10 + 2 * n_x]
        i = pl.program_id(0)
        j = pl.program_id(1)
        if exchange:
            _host_exchange(exchange, refs[5:5 + n_x], refs[8 + n_x:8 + 2 * n_x], refs[10 + 2 * n_x:],
                           (i == 0) & (j == 0), (i == ni - 1) & (j == 0), (i == ni - 1) & (j == nj - 1))

        @pl.when(j == 0)
        def _():
            hh = h_ref[...]
            n_scr[...] = ((hh * _rms_scale(hh)) * nw_ref[...]).astype(BF16)
            acc_scr[...] = jnp.zeros_like(acc_scr)

        n = n_scr[...]
        g = jnp.dot(n, wg_ref[...], preferred_element_type=F32)
        u = jnp.dot(n, wu_ref[...], preferred_element_type=F32)
        g_ref[...] = g
        u_ref[...] = u
        a = (g * jax.nn.sigmoid(g)) * u
        acc_scr[...] += jnp.dot(a.astype(BF16), wo_ref[...], preferred_element_type=F32)

        @pl.when(j == nj - 1)
        def _():
            hout_ref[...] = h_ref[...] + 0.5 * acc_scr[...]

    outs = pl.pallas_call(
        body,
        name="ffn_fwd",
        grid=(ni, nj),
        in_specs=[
            pl.BlockSpec((tm, d), lambda i, j: (i, 0)),
            pl.BlockSpec((1, d), lambda i, j: (0, 0)),
            pl.BlockSpec((d, tc), lambda i, j: (0, j)),
            pl.BlockSpec((d, tc), lambda i, j: (0, j + nj)),
            pl.BlockSpec((tc, d), lambda i, j: (j, 0)),
        ] + [HBM_SPEC] * n_x,
        out_specs=[
            pl.BlockSpec((tm, d), lambda i, j: (i, 0)),
            pl.BlockSpec((tm, tc), lambda i, j: (i, j)),
            pl.BlockSpec((tm, tc), lambda i, j: (i, j)),
        ] + [HBM_SPEC] * n_x,
        out_shape=[
            jax.ShapeDtypeStruct((t, d), F32),
            jax.ShapeDtypeStruct((t, f), F32),
            jax.ShapeDtypeStruct((t, f), F32),
        ] + (exchange.out_shape if exchange else []),
        scratch_shapes=[pltpu.VMEM((tm, d), BF16), pltpu.VMEM((tm, d), F32)] + (exchange.scratch if exchange else []),
        compiler_params=_cparams(("arbitrary", "arbitrary") if exchange else ("parallel", "arbitrary")),
    )(h, norm_w, w_in, w_in, w_out, *(exchange.ins if exchange else []))
    return outs[:3], outs[3:]


def _ffn_bwd(dh_out, h, norm_w, g, u, w_in, w_out, exchange=None):
    t, d = h.shape
    f = w_out.shape[0]
    tm = _tile(t, 544)
    tc = _tile(f, 256, 128)
    nj = f // tc
    ni = t // tm
    n_x = len(exchange.ins) if exchange else 0

    def body(*refs):
        dho_ref, h_ref, nw_ref, g_ref, u_ref, wg_ref, wu_ref, wo_ref = refs[:8]
        dhin_ref, n_ref, a_ref, dgu_ref, df_ref, dnw_ref = refs[8 + n_x:14 + n_x]
        dn_scr = refs[14 + 2 * n_x]
        i = pl.program_id(0)
        j = pl.program_id(1)
        if exchange:
            _host_exchange(exchange, refs[8:8 + n_x], refs[14 + n_x:14 + 2 * n_x], refs[15 + 2 * n_x:],
                           (i == 0) & (j == 0), (i == ni - 1) & (j == 0), (i == ni - 1) & (j == nj - 1))

        @pl.when(j == 0)
        def _():
            hh = h_ref[...]
            n_ref[...] = ((hh * _rms_scale(hh)) * nw_ref[...]).astype(BF16)
            df_ref[...] = (0.5 * dho_ref[...]).astype(BF16)
            dn_scr[...] = jnp.zeros_like(dn_scr)

        da = lax.dot_general(df_ref[...], wo_ref[...], NT_DIMS, preferred_element_type=F32)
        gg = g_ref[...]
        uu = u_ref[...]
        sig = jax.nn.sigmoid(gg)
        sl = gg * sig
        a_ref[...] = (sl * uu).astype(BF16)
        dg = ((da * uu) * (sig * (1.0 + gg * (1.0 - sig)))).astype(BF16)
        du = (da * sl).astype(BF16)
        dgu_ref[0] = dg
        dgu_ref[1] = du
        dn_scr[...] += (lax.dot_general(dg, wg_ref[...], NT_DIMS, preferred_element_type=F32)
                        + lax.dot_general(du, wu_ref[...], NT_DIMS, preferred_element_type=F32))

        @pl.when(j == nj - 1)
        def _():
            dh, dw = _rms_bwd(dn_scr[...], h_ref[...], nw_ref[...])
            dhin_ref[...] = dho_ref[...] + dh
            dnw_ref[0] = dw

    outs = pl.pallas_call(
        body,
        name="ffn_bwd",
        grid=(ni, nj),
        in_specs=[
            pl.BlockSpec((tm, d), lambda i, j: (i, 0)),
            pl.BlockSpec((tm, d), lambda i, j: (i, 0)),
            pl.BlockSpec((1, d), lambda i, j: (0, 0)),
            pl.BlockSpec((tm, tc), lambda i, j: (i, j)),
            pl.BlockSpec((tm, tc), lambda i, j: (i, j)),
            pl.BlockSpec((d, tc), lambda i, j: (0, j)),
            pl.BlockSpec((d, tc), lambda i, j: (0, j + nj)),
            pl.BlockSpec((tc, d), lambda i, j: (j, 0)),
        ] + [HBM_SPEC] * n_x,
        out_specs=[
            pl.BlockSpec((tm, d), lambda i, j: (i, 0)),
            pl.BlockSpec((tm, d), lambda i, j: (i, 0)),
            pl.BlockSpec((tm, tc), lambda i, j: (i, j)),
            pl.BlockSpec((2, tm, tc), lambda i, j: (0, i, j)),
            pl.BlockSpec((tm, d), lambda i, j: (i, 0)),
            pl.BlockSpec((1, 1, d), lambda i, j: (i, 0, 0)),
        ] + [HBM_SPEC] * n_x,
        out_shape=[
            jax.ShapeDtypeStruct((t, d), F32),
            jax.ShapeDtypeStruct((t, d), BF16),
            jax.ShapeDtypeStruct((t, f), BF16),
            jax.ShapeDtypeStruct((2, t, f), BF16),
            jax.ShapeDtypeStruct((t, d), BF16),
            jax.ShapeDtypeStruct((ni, 1, d), F32),
        ] + (exchange.out_shape if exchange else []),
        scratch_shapes=[pltpu.VMEM((tm, d), F32)] + (exchange.scratch if exchange else []),
        compiler_params=_cparams(("arbitrary", "arbitrary") if exchange else ("parallel", "arbitrary")),
    )(dh_out, h, norm_w, g, u, w_in, w_in, w_out, *(exchange.ins if exchange else []))
    return outs[:6], outs[6:]


def _tn_matmul(a, b, name):
    t, k = a.shape
    split = b.ndim == 3
    n = 2 * b.shape[2] if split else b.shape[1]
    tt = _tile(t, 1088)
    tk = _tile(k, 512, 128)
    tn = _tile(b.shape[-1], 1408, 128)
    nt = t // tt
    per_half = b.shape[-1] // tn

    def body(a_ref, b_ref, o_ref):
        s = pl.program_id(2)
        part = lax.dot_general(a_ref[...], b_ref[...], TN_DIMS, preferred_element_type=F32)

        @pl.when(s == 0)
        def _():
            o_ref[...] = part

        @pl.when(s > 0)
        def _():
            o_ref[...] += part

    if split:
        b_spec = pl.BlockSpec((None, tt, tn), lambda i, j, s: (j // per_half, s, j % per_half))
    else:
        b_spec = pl.BlockSpec((tt, tn), lambda i, j, s: (s, j))
    return pl.pallas_call(
        body,
        name=name,
        grid=(k // tk, n // tn, nt),
        in_specs=[pl.BlockSpec((tt, tk), lambda i, j, s: (s, i)), b_spec],
        out_specs=pl.BlockSpec((tk, tn), lambda i, j, s: (i, j)),
        out_shape=jax.ShapeDtypeStruct((k, n), F32),
        compiler_params=_cparams(("parallel", "parallel", "arbitrary")),
    )(a, b)


PROJ_PARTS = (
    (OFF_QA, A_PAD_WIDTH, True), (OFF_KA, A_KV_WIDTH, True), (OFF_VA, A_KV_WIDTH, True),
    (OFF_QB, B_WIDTH, True), (OFF_KB, B_PAD_WIDTH, True), (OFF_VB, B_PAD_WIDTH, True),
    (OFF_GA, D_MODEL, False), (OFF_GB, D_MODEL, False), (OFF_F, F_COLS, False),
)


def _proj_fwd(h, norm_w, w_p):
    t, d = h.shape
    tm = _tile(t, 272)

    def body(h_ref, nw_ref, w_ref, u_ref, *part_refs):
        hh = h_ref[...]
        un = ((hh * _rms_scale(hh)) * nw_ref[...]).astype(BF16)
        u_ref[...] = un
        for (off, width, _), p_ref in zip(PROJ_PARTS, part_refs):
            p_ref[...] = jnp.dot(un, w_ref[:, off:off + width], preferred_element_type=F32).astype(p_ref.dtype)

    row = lambda w: pl.BlockSpec((tm, w), lambda i: (i, 0))
    return pl.pallas_call(
        body,
        name="proj_fwd",
        grid=(t // tm,),
        in_specs=[row(d), pl.BlockSpec((1, d), lambda i: (0, 0)), pl.BlockSpec(w_p.shape, lambda i: (0, 0))],
        out_specs=[row(d)] + [row(width) for _, width, _ in PROJ_PARTS],
        out_shape=[jax.ShapeDtypeStruct((t, d), BF16)]
        + [jax.ShapeDtypeStruct((t, width), BF16 if is_bf else F32) for _, width, is_bf in PROJ_PARTS],
        compiler_params=_cparams(("parallel",)),
    )(h, norm_w, w_p)


def _proj_bwd(dh_out, h, norm_w, dproj, w_p):
    t, d = h.shape
    n = w_p.shape[1]
    tm = _tile(t, 544)
    tn = _tile(n, 768, 128)
    nj = n // tn
    ni = t // tm

    def body(dho_ref, h_ref, nw_ref, dp_ref, w_ref, dhin_ref, dnw_ref, dn_scr):
        j = pl.program_id(1)

        @pl.when(j == 0)
        def _():
            dn_scr[...] = jnp.zeros_like(dn_scr)

        dn_scr[...] += lax.dot_general(dp_ref[...], w_ref[...], NT_DIMS, preferred_element_type=F32)

        @pl.when(j == nj - 1)
        def _():
            dh, dw = _rms_bwd(dn_scr[...], h_ref[...], nw_ref[...])
            dhin_ref[...] = dho_ref[...] + dh
            dnw_ref[0] = dw

    return pl.pallas_call(
        body,
        name="proj_bwd",
        grid=(ni, nj),
        in_specs=[
            pl.BlockSpec((tm, d), lambda i, j: (i, 0)),
            pl.BlockSpec((tm, d), lambda i, j: (i, 0)),
            pl.BlockSpec((1, d), lambda i, j: (0, 0)),
            pl.BlockSpec((tm, tn), lambda i, j: (i, j)),
            pl.BlockSpec((d, tn), lambda i, j: (0, j)),
        ],
        out_specs=[
            pl.BlockSpec((tm, d), lambda i, j: (i, 0)),
            pl.BlockSpec((1, 1, d), lambda i, j: (i, 0, 0)),
        ],
        out_shape=[jax.ShapeDtypeStruct((t, d), F32), jax.ShapeDtypeStruct((ni, 1, d), F32)],
        scratch_shapes=[pltpu.VMEM((tm, d), F32)],
        compiler_params=_cparams(("parallel", "arbitrary")),
    )(dh_out, h, norm_w, dproj, w_p)


def _merge_fwd(h, oa, ob, ga, gb, wa, wb, wo):
    t, d = h.shape
    tm = _tile(t, 544)

    def body(h_ref, oa_ref, ob_ref, ga_ref, gb_ref, wa_ref, wb_ref, wo_ref, hout_ref, mix_ref):
        ya = jnp.dot(oa_ref[...], wa_ref[...], preferred_element_type=F32)
        yb = jnp.dot(ob_ref[...], wb_ref[...], preferred_element_type=F32)
        mixed = (jax.nn.sigmoid(ga_ref[...]) * ya + jax.nn.sigmoid(gb_ref[...]) * yb).astype(BF16)
        mix_ref[...] = mixed
        hout_ref[...] = h_ref[...] + jnp.dot(mixed, wo_ref[...], preferred_element_type=F32)

    row = lambda w: pl.BlockSpec((tm, w), lambda i: (i, 0))
    full = lambda a: pl.BlockSpec(a.shape, lambda i: (0, 0))
    return pl.pallas_call(
        body,
        name="merge_fwd",
        grid=(t // tm,),
        in_specs=[row(d), row(oa.shape[1]), row(ob.shape[1]), row(d), row(d), full(wa), full(wb), full(wo)],
        out_specs=[row(d), row(d)],
        out_shape=[jax.ShapeDtypeStruct((t, d), F32), jax.ShapeDtypeStruct((t, d), BF16)],
        compiler_params=_cparams(("parallel",)),
    )(h, oa, ob, ga, gb, wa, wb, wo)


def _merge_bwd(dh, oa, ob, ga, gb, wa, wb, wo):
    t, d = dh.shape
    tm = _tile(t, 544)

    def body(dh_ref, oa_ref, ob_ref, ga_ref, gb_ref, wa_ref, wb_ref, wo_ref,
             dya_ref, dyb_ref, doa_ref, dob_ref, dga_ref, dgb_ref, dhb_ref):
        dhb = dh_ref[...].astype(BF16)
        dhb_ref[...] = dhb
        dmix = lax.dot_general(dhb, wo_ref[...], NT_DIMS, preferred_element_type=F32)
        for o_ref, g_ref, w_ref, dy_ref, do_ref, dg_ref in (
                (oa_ref, ga_ref, wa_ref, dya_ref, doa_ref, dga_ref),
                (ob_ref, gb_ref, wb_ref, dyb_ref, dob_ref, dgb_ref)):
            y = jnp.dot(o_ref[...], w_ref[...], preferred_element_type=F32)
            s = jax.nn.sigmoid(g_ref[...])
            dy = (dmix * s).astype(BF16)
            dy_ref[...] = dy
            dg_ref[...] = ((dmix * y) * (s * (1.0 - s))).astype(BF16)
            do_ref[...] = lax.dot_general(dy, w_ref[...], NT_DIMS, preferred_element_type=F32).astype(BF16)

    row = lambda w: pl.BlockSpec((tm, w), lambda i: (i, 0))
    full = lambda a: pl.BlockSpec(a.shape, lambda i: (0, 0))
    wa_w, wb_w = oa.shape[1], ob.shape[1]
    return pl.pallas_call(
        body,
        name="merge_bwd",
        grid=(t // tm,),
        in_specs=[row(d), row(wa_w), row(wb_w), row(d), row(d), full(wa), full(wb), full(wo)],
        out_specs=[row(d), row(d), row(wa_w), row(wb_w), row(d), row(d), row(d)],
        out_shape=[
            jax.ShapeDtypeStruct((t, d), BF16), jax.ShapeDtypeStruct((t, d), BF16),
            jax.ShapeDtypeStruct((t, wa_w), BF16), jax.ShapeDtypeStruct((t, wb_w), BF16),
            jax.ShapeDtypeStruct((t, d), BF16), jax.ShapeDtypeStruct((t, d), BF16),
            jax.ShapeDtypeStruct((t, d), BF16),
        ],
        compiler_params=_cparams(("parallel",)),
    )(dh, oa, ob, ga, gb, wa, wb, wo)


def _tri_dot(tri, x):
    hi = x.astype(BF16)
    r1 = x - hi.astype(F32)
    mid = r1.astype(BF16)
    lo = (r1 - mid.astype(F32)).astype(BF16)
    return (jnp.dot(tri, hi, preferred_element_type=F32)
            + jnp.dot(tri, mid, preferred_element_type=F32)
            + jnp.dot(tri, lo, preferred_element_type=F32))


def _forget_cumsum(f_logit, b_pad, nb):
    t, w = f_logit.shape
    bsz = t // (nb * BLOCK)

    def body(f_ref, b_ref, c_ref, carry):
        n = pl.program_id(1)

        @pl.when(n == 0)
        def _():
            carry[...] = jnp.zeros_like(carry)

        x = jax.nn.log_sigmoid(f_ref[...] + b_ref[...])
        rows = lax.broadcasted_iota(jnp.int32, (BLOCK, BLOCK), 0)
        cols = lax.broadcasted_iota(jnp.int32, (BLOCK, BLOCK), 1)
        tri = (cols <= rows).astype(BF16)
        c = _tri_dot(tri, x) + carry[...]
        c_ref[...] = c
        carry[...] = c[BLOCK - 1:BLOCK, :]

    return pl.pallas_call(
        body,
        name="forget_cumsum",
        grid=(bsz, nb),
        in_specs=[pl.BlockSpec((BLOCK, w), lambda b, n: (b * nb + n, 0)),
                  pl.BlockSpec((1, w), lambda b, n: (0, 0))],
        out_specs=pl.BlockSpec((BLOCK, w), lambda b, n: (b * nb + n, 0)),
        out_shape=jax.ShapeDtypeStruct((t, w), F32),
        scratch_shapes=[pltpu.VMEM((1, w), F32)],
        compiler_params=_cparams(("parallel", "arbitrary")),
    )(f_logit, b_pad)


def _forget_cumsum_bwd(dc, f_logit, b_pad, nb):
    t, w = f_logit.shape
    bsz = t // (nb * BLOCK)

    def body(dc_ref, f_ref, b_ref, df_ref, db_ref, carry):
        n = pl.program_id(1)

        @pl.when(n == 0)
        def _():
            carry[...] = jnp.zeros_like(carry)
            db_ref[...] = jnp.zeros_like(db_ref)

        rows = lax.broadcasted_iota(jnp.int32, (BLOCK, BLOCK), 0)
        cols = lax.broadcasted_iota(jnp.int32, (BLOCK, BLOCK), 1)
        tri = (cols >= rows).astype(BF16)
        dlf = _tri_dot(tri, dc_ref[...]) + carry[...]
        carry[...] = dlf[0:1, :]
        df = dlf * jax.nn.sigmoid(-(f_ref[...] + b_ref[...]))
        df_ref[...] = df.astype(BF16)
        db_ref[0] += jnp.sum(df, axis=0, keepdims=True)

    rev = lambda b, n: (b * nb + (nb - 1 - n), 0)
    return pl.pallas_call(
        body,
        name="forget_cumsum_bwd",
        grid=(bsz, nb),
        in_specs=[pl.BlockSpec((BLOCK, w), rev),
                  pl.BlockSpec((BLOCK, w), rev),
                  pl.BlockSpec((1, w), lambda b, n: (0, 0))],
        out_specs=[pl.BlockSpec((BLOCK, w), rev),
                   pl.BlockSpec((1, 1, w), lambda b, n: (b, 0, 0))],
        out_shape=[jax.ShapeDtypeStruct((t, w), BF16), jax.ShapeDtypeStruct((bsz, 1, w), F32)],
        scratch_shapes=[pltpu.VMEM((1, w), F32)],
        compiler_params=_cparams(("parallel", "arbitrary")),
    )(dc, f_logit, b_pad)


GROUP_ROWS = A_GROUP * BLOCK


def _stack_heads(ref):
    return jnp.concatenate([ref[:, i * LANES:(i + 1) * LANES] for i in range(A_GROUP)], axis=0)


def _unstack_heads(ref, x):
    for i in range(A_GROUP):
        ref[:, i * LANES:(i + 1) * LANES] = x[i * BLOCK:(i + 1) * BLOCK].astype(ref.dtype)


def _swa_logits(q, km, kp, kc, slope, n):
    qi = lax.broadcasted_iota(jnp.int32, (GROUP_ROWS, BLOCK), 0) & (BLOCK - 1)
    kj = lax.broadcasted_iota(jnp.int32, (GROUP_ROWS, BLOCK), 1)
    out = []
    for kk, dist, ok in (
            (km, n * BLOCK + qi - kj, (kj >= N_PAD) & (n * BLOCK + qi - kj >= 0)),
            (kp, BLOCK + qi - kj, (kj > qi) & (n >= 2)),
            (kc, qi - kj, (kj <= qi) & (n >= 1))):
        s = lax.dot_general(q, kk, NT_DIMS, preferred_element_type=F32) * SCALE
        s = s - slope * dist.astype(F32)
        out.append(jnp.where(ok, s, NEG))
    return out


def _swa_specs(nb):
    row = lambda b, n: b * nb + n
    qspec = pl.BlockSpec((BLOCK, A_GROUP * LANES), lambda b, g, n: (row(b, n), g))
    kv_m = pl.BlockSpec((BLOCK, LANES), lambda b, g, n: (row(b, 0), 0))
    kv_p = pl.BlockSpec((BLOCK, LANES), lambda b, g, n: (row(b, jnp.maximum(n - 1, 0)), 0))
    kv_c = pl.BlockSpec((BLOCK, LANES), lambda b, g, n: (row(b, n), 0))
    rowspec = pl.BlockSpec((1, GROUP_ROWS, 1), lambda b, g, n: (g, 0, 0))
    lsespec = pl.BlockSpec((1, 1, GROUP_ROWS, 1), lambda b, g, n: (row(b, n), g, 0, 0))
    return qspec, kv_m, kv_p, kv_c, rowspec, lsespec


def _swa_fwd(q, k, v, sink_rows, slope_rows, nb):
    t = q.shape[0]
    bsz = t // (nb * BLOCK)

    def body(q_ref, km_ref, kp_ref, kc_ref, vm_ref, vp_ref, vc_ref, sink_ref, slope_ref, o_ref, lse_ref):
        g = pl.program_id(1)
        n = pl.program_id(2)
        qq = _stack_heads(q_ref)
        sink = sink_ref[0]
        s_m, s_p, s_c = _swa_logits(qq, km_ref[...], kp_ref[...], kc_ref[...], slope_ref[0], n)
        m = jnp.maximum(jnp.maximum(jnp.max(s_m, axis=-1, keepdims=True), jnp.max(s_p, axis=-1, keepdims=True)),
                        jnp.maximum(jnp.max(s_c, axis=-1, keepdims=True), sink))
        e_m = jnp.exp(s_m - m)
        e_p = jnp.exp(s_p - m)
        e_c = jnp.exp(s_c - m)
        z = (jnp.sum(e_m, axis=-1, keepdims=True) + jnp.sum(e_p, axis=-1, keepdims=True)
             + jnp.sum(e_c, axis=-1, keepdims=True) + jnp.exp(sink - m))
        inv = 1.0 / z
        o = (jnp.dot((e_m * inv).astype(BF16), vm_ref[...], preferred_element_type=F32)
             + jnp.dot((e_p * inv).astype(BF16), vp_ref[...], preferred_element_type=F32)
             + jnp.dot((e_c * inv).astype(BF16), vc_ref[...], preferred_element_type=F32))
        lane_group = lax.broadcasted_iota(jnp.int32, (GROUP_ROWS, LANES), 1) // HEAD_DIM
        _unstack_heads(o_ref, jnp.where(lane_group == g, o, 0.0))
        lse_ref[0, 0] = m + jnp.log(z)

    qspec, kv_m, kv_p, kv_c, rowspec, lsespec = _swa_specs(nb)
    return pl.pallas_call(
        body,
        name="swa_fwd",
        grid=(bsz, A_KV_HEADS, nb),
        in_specs=[qspec, kv_m, kv_p, kv_c, kv_m, kv_p, kv_c, rowspec, rowspec],
        out_specs=[qspec, lsespec],
        out_shape=[jax.ShapeDtypeStruct((t, A_PAD_WIDTH), BF16),
                   jax.ShapeDtypeStruct((t // BLOCK, A_KV_HEADS, GROUP_ROWS, 1), F32)],
        compiler_params=_cparams(("parallel", "parallel", "arbitrary")),
    )(q, k, k, k, v, v, v, sink_rows, slope_rows)


def _swa_bwd(q, k, v, do, lse, sink_rows, slope_rows, nb):
    t = q.shape[0]
    l = nb * BLOCK
    bsz = t // l

    def body(q_ref, km_ref, kp_ref, kc_ref, vm_ref, vp_ref, vc_ref, do_ref, lse_ref, sink_ref, slope_ref,
             dq_ref, dk_ref, dv_ref, dsink_ref, dk_acc, dv_acc):
        g = pl.program_id(1)
        n = pl.program_id(2)

        @pl.when((g == 0) & (n == 0))
        def _():
            dk_acc[...] = jnp.zeros_like(dk_acc)
            dv_acc[...] = jnp.zeros_like(dv_acc)

        @pl.when(n == 0)
        def _():
            dsink_ref[...] = jnp.zeros_like(dsink_ref)

        qq = _stack_heads(q_ref)
        dob = _stack_heads(do_ref)
        lse = lse_ref[0, 0]
        logits = _swa_logits(qq, km_ref[...], kp_ref[...], kc_ref[...], slope_ref[0], n)
        probs = [jnp.exp(s - lse) for s in logits]
        dps = [lax.dot_general(dob, v_ref[...], NT_DIMS, preferred_element_type=F32)
               for v_ref in (vm_ref, vp_ref, vc_ref)]
        delta = sum(jnp.sum(p * dp, axis=-1, keepdims=True) for p, dp in zip(probs, dps))
        prev = jnp.maximum(n - 1, 0)
        dq = jnp.zeros((GROUP_ROWS, LANES), F32)
        for p, dp, k_ref, start in ((probs[0], dps[0], km_ref, 0),
                                    (probs[1], dps[1], kp_ref, prev * BLOCK),
                                    (probs[2], dps[2], kc_ref, n * BLOCK)):
            ds = (p * (dp - delta)).astype(BF16)
            dq = dq + jnp.dot(ds, k_ref[...], preferred_element_type=F32)
            rows = pl.ds(pl.multiple_of(start, BLOCK), BLOCK)
            dk_acc[rows, :] += lax.dot_general(ds, qq, TN_DIMS, preferred_element_type=F32) * SCALE
            dv_acc[rows, :] += lax.dot_general(p.astype(BF16), dob, TN_DIMS, preferred_element_type=F32)
        _unstack_heads(dq_ref, dq * SCALE)
        dsink_ref[0, 0] += -(jnp.exp(sink_ref[0] - lse) * delta)

        @pl.when((g == A_KV_HEADS - 1) & (n == nb - 1))
        def _():
            dk_ref[...] = dk_acc[...].astype(BF16)
            dv_ref[...] = dv_acc[...].astype(BF16)

    qspec, kv_m, kv_p, kv_c, rowspec, lsespec = _swa_specs(nb)
    kv_all = pl.BlockSpec((l, LANES), lambda b, g, n: (b, 0))
    return pl.pallas_call(
        body,
        name="swa_bwd",
        grid=(bsz, A_KV_HEADS, nb),
        in_specs=[qspec, kv_m, kv_p, kv_c, kv_m, kv_p, kv_c, qspec, lsespec, rowspec, rowspec],
        out_specs=[qspec, kv_all, kv_all,
                   pl.BlockSpec((1, 1, GROUP_ROWS, 1), lambda b, g, n: (b, g, 0, 0))],
        out_shape=[jax.ShapeDtypeStruct((t, A_PAD_WIDTH), BF16),
                   jax.ShapeDtypeStruct((t, LANES), BF16),
                   jax.ShapeDtypeStruct((t, LANES), BF16),
                   jax.ShapeDtypeStruct((bsz, A_KV_HEADS, GROUP_ROWS, 1), F32)],
        scratch_shapes=[pltpu.VMEM((l, LANES), F32), pltpu.VMEM((l, LANES), F32)],
        compiler_params=_cparams(("parallel", "arbitrary", "arbitrary")),
    )(q, k, k, k, v, v, v, do, lse, sink_rows, slope_rows)


CHUNK = KEY_BLOCKS * BLOCK


def _fox_chunk(qs, k_ref, cc_ref, cr_ref, qb, ci):
    sb = jnp.maximum(jnp.minimum(KEY_BLOCKS * ci, qb + 1 - KEY_BLOCKS), 0)
    lo = jnp.maximum(ci * CHUNK, N_PAD)
    krows = pl.ds(pl.multiple_of(sb * BLOCK, BLOCK), CHUNK)
    kk2 = jnp.concatenate([k_ref[krows, 0:LANES], k_ref[krows, LANES:2 * LANES]], axis=0)
    s01 = lax.dot_general(qs, kk2, NT_DIMS, preferred_element_type=F32)
    q_pos = qb * BLOCK + lax.broadcasted_iota(jnp.int32, (BLOCK, BLOCK), 0)
    lane = lax.broadcasted_iota(jnp.int32, (BLOCK, BLOCK), 1)
    logits = []
    for e in range(2):
        c_t = cc_ref[e]
        per_block = []
        for j in range(KEY_BLOCKS):
            k_pos = (sb + j) * BLOCK + lane
            col = (e * KEY_BLOCKS + j) * BLOCK
            s = s01[:, col:col + BLOCK] + c_t - cr_ref[e, sb + j]
            per_block.append(jnp.where((k_pos <= q_pos) & (k_pos >= lo), s, NEG))
        logits.append(per_block)
    return krows, sb, kk2, logits


def _fox_specs(nb):
    l = nb * BLOCK
    q_spec = pl.BlockSpec((BLOCK, LANES), lambda b, p, i: (b * nb + i, p))
    kv_spec = pl.BlockSpec((l, 2 * LANES), lambda b, p, i: (b, p))
    cc_spec = pl.BlockSpec((2, BLOCK, 1), lambda b, p, i: (b * B_PAIRS + p, i, 0))
    cr_spec = pl.BlockSpec((2, nb, 1, BLOCK), lambda b, p, i: (b * B_PAIRS + p, 0, 0, 0))
    return q_spec, kv_spec, cc_spec, cr_spec


def _fox_fwd(q, k, v, c_col, c_row, nb):
    t = q.shape[0]
    bsz = t // (nb * BLOCK)
    assert nb >= KEY_BLOCKS

    def body(q_ref, k_ref, v_ref, cc_ref, cr_ref, o_ref, ox_ref, lse_ref):
        qb = pl.program_id(2)
        qs = q_ref[...] * SCALE
        first_half = lax.broadcasted_iota(jnp.int32, (BLOCK, LANES), 1) < HEAD_DIM

        def step(ci, carry):
            m0, z0, m1, z1, acc, acc_lo = carry
            krows, _, _, logits = _fox_chunk(qs, k_ref, cc_ref, cr_ref, qb, ci)
            stats, p_hi, p_lo = [], [], []
            for (m, z), per_block in zip(((m0, z0), (m1, z1)), logits):
                m_new = m
                for s in per_block:
                    m_new = jnp.maximum(m_new, jnp.max(s, axis=-1, keepdims=True))
                alpha = jnp.exp(m - m_new)
                z = alpha * z
                for s in per_block:
                    p = jnp.exp(s - m_new)
                    z = z + jnp.sum(p, axis=-1, keepdims=True)
                    hi = p.astype(BF16)
                    p_hi.append(hi)
                    p_lo.append((p - hi.astype(F32)).astype(BF16))
                stats.append((m_new, z, alpha))
            vv2 = jnp.concatenate([v_ref[krows, 0:LANES], v_ref[krows, LANES:2 * LANES]], axis=0)
            alpha = jnp.where(first_half, stats[0][2], stats[1][2])
            acc = alpha * acc + jnp.dot(jnp.concatenate(p_hi, axis=1), vv2, preferred_element_type=F32)
            acc_lo = alpha * acc_lo + jnp.dot(jnp.concatenate(p_lo, axis=1), vv2, preferred_element_type=F32)
            return stats[0][0], stats[0][1], stats[1][0], stats[1][1], acc, acc_lo

        col = lambda val: jnp.full((BLOCK, 1), val, F32)
        m0, z0, m1, z1, acc, acc_lo = lax.fori_loop(
            0, (qb + KEY_BLOCKS) // KEY_BLOCKS, step,
            (col(NEG), col(0.0), col(NEG), col(0.0), jnp.zeros((BLOCK, LANES), F32), jnp.zeros((BLOCK, LANES), F32)))
        inv = 1.0 / jnp.where(first_half, z0, z1)
        o_ref[...] = (acc * inv).astype(BF16)
        ox_ref[...] = (acc + acc_lo) * inv
        lse_ref[0] = m0 + jnp.log(z0)
        lse_ref[1] = m1 + jnp.log(z1)

    q_spec, kv_spec, cc_spec, cr_spec = _fox_specs(nb)
    return pl.pallas_call(
        body,
        name="fox_fwd",
        grid=(bsz, B_PAIRS, nb),
        in_specs=[q_spec, kv_spec, kv_spec, cc_spec, cr_spec],
        out_specs=[q_spec, q_spec, cc_spec],
        out_shape=[jax.ShapeDtypeStruct((t, B_WIDTH), BF16), jax.ShapeDtypeStruct((t, B_WIDTH), F32),
                   jax.ShapeDtypeStruct((bsz * B_HEADS, nb * BLOCK, 1), F32)],
        compiler_params=_cparams(("parallel", "parallel", "arbitrary")),
    )(q, k, v, c_col, c_row)


def _fox_bwd(q, k, v, o_exact, do, lse, c_col, c_row, nb, exchange=None):
    t = q.shape[0]
    l = nb * BLOCK
    bsz = t // l

    n_x = len(exchange.ins) if exchange else 0

    def body(*refs):
        q_ref, k_ref, v_ref, ox_ref, do_ref, lse_ref, cc_ref, cr_ref = refs[:8]
        dq_ref, dk_ref, dv_ref, dc_ref = refs[8 + n_x:12 + n_x]
        dk_acc, dv_acc = refs[12 + 2 * n_x:14 + 2 * n_x]
        qb = pl.program_id(2)
        if exchange:
            first = (pl.program_id(0) == 0) & (pl.program_id(1) == 0)
            last = (pl.program_id(0) == bsz - 1) & (pl.program_id(1) == B_PAIRS - 1)
            _host_exchange(exchange, refs[8:8 + n_x], refs[12 + n_x:12 + 2 * n_x], refs[14 + 2 * n_x:],
                           first & (qb == 0), last & (qb == 0), last & (qb == nb - 1))

        @pl.when(qb == 0)
        def _():
            dk_acc[...] = jnp.zeros_like(dk_acc)
            dv_acc[...] = jnp.zeros_like(dv_acc)
            dc_ref[...] = jnp.zeros_like(dc_ref)

        qs = q_ref[...] * SCALE
        dob = do_ref[...]
        first_half = lax.broadcasted_iota(jnp.int32, (BLOCK, LANES), 1) < HEAD_DIM
        weighted = dob.astype(F32) * ox_ref[...]
        deltas = (jnp.sum(jnp.where(first_half, weighted, 0.0), axis=-1, keepdims=True),
                  jnp.sum(jnp.where(first_half, 0.0, weighted), axis=-1, keepdims=True))

        def step(ci, dq):
            krows, sb, kk2, logits = _fox_chunk(qs, k_ref, cc_ref, cr_ref, qb, ci)
            vv2 = jnp.concatenate([v_ref[krows, 0:LANES], v_ref[krows, LANES:2 * LANES]], axis=0)
            dp01 = lax.dot_general(dob, vv2, NT_DIMS, preferred_element_type=F32)
            ps, dss = [], []
            for e in range(2):
                lse_e = lse_ref[e]
                for j in range(KEY_BLOCKS):
                    col = (e * KEY_BLOCKS + j) * BLOCK
                    p = jnp.exp(logits[e][j] - lse_e)
                    ds = p * (dp01[:, col:col + BLOCK] - deltas[e])
                    dc_ref[e, sb + j] -= jnp.sum(ds, axis=0, keepdims=True)
                    ps.append(p.astype(BF16))
                    dss.append(ds.astype(BF16))
            dsb = jnp.concatenate(dss, axis=1)
            pb = jnp.concatenate(ps, axis=1)
            dk2 = lax.dot_general(dsb, qs, TN_DIMS, preferred_element_type=F32)
            dv2 = lax.dot_general(pb, dob, TN_DIMS, preferred_element_type=F32)
            for e in range(2):
                dk_acc[krows, e * LANES:(e + 1) * LANES] += dk2[e * CHUNK:(e + 1) * CHUNK]
                dv_acc[krows, e * LANES:(e + 1) * LANES] += dv2[e * CHUNK:(e + 1) * CHUNK]
            return dq + jnp.dot(dsb, kk2, preferred_element_type=F32)

        dq = lax.fori_loop(0, (qb + KEY_BLOCKS) // KEY_BLOCKS, step, jnp.zeros((BLOCK, LANES), F32))
        dq_ref[...] = (dq * SCALE).astype(BF16)

        @pl.when(qb == nb - 1)
        def _():
            dk_ref[...] = dk_acc[...].astype(BF16)
            dv_ref[...] = dv_acc[...].astype(BF16)

    q_spec, kv_spec, cc_spec, cr_spec = _fox_specs(nb)
    outs = pl.pallas_call(
        body,
        name="fox_bwd",
        grid=(bsz, B_PAIRS, nb),
        in_specs=[q_spec, kv_spec, kv_spec, q_spec, q_spec, cc_spec, cc_spec, cr_spec] + [HBM_SPEC] * n_x,
        out_specs=[q_spec, kv_spec, kv_spec, cr_spec] + [HBM_SPEC] * n_x,
        out_shape=[jax.ShapeDtypeStruct((t, B_WIDTH), BF16), jax.ShapeDtypeStruct((t, B_PAD_WIDTH), BF16),
                   jax.ShapeDtypeStruct((t, B_PAD_WIDTH), BF16),
                   jax.ShapeDtypeStruct((bsz * B_HEADS, nb, 1, BLOCK), F32)] + (exchange.out_shape if exchange else []),
        scratch_shapes=[pltpu.VMEM((l, 2 * LANES), F32), pltpu.VMEM((l, 2 * LANES), F32)]
        + (exchange.scratch if exchange else []),
        compiler_params=_cparams(("arbitrary",) * 3 if exchange else ("parallel", "parallel", "arbitrary")),
    )(q, k, v, o_exact, do, lse, c_col, c_row, *(exchange.ins if exchange else []))
    return outs[:4], outs[4:]


def _loss_head(h, final_w, target):
    bsz, l, d = h.shape
    nb = l // BLOCK

    def body(h_ref, w_ref, t_ref, loss_ref, dh_ref, dw_ref):
        b = pl.program_id(0)
        n = pl.program_id(1)

        @pl.when((b == 0) & (n == 0))
        def _():
            loss_ref[...] = jnp.zeros_like(loss_ref)
            dw_ref[...] = jnp.zeros_like(dw_ref)

        @pl.when(n == 0)
        def _():
            dh_ref[...] = jnp.zeros_like(dh_ref)

        @pl.when(n > 0)
        def _():
            hh = h_ref[0]
            w = w_ref[...]
            r = _rms_scale(hh)
            err = (hh * r) * w - t_ref[0]
            loss_ref[...] += 0.5 * jnp.sum(jnp.mean(err * err, axis=-1, keepdims=True), axis=0, keepdims=True)
            dy = err * (1.0 / d)
            dh, dw = _rms_bwd(dy, hh, w)
            dh_ref[0] = dh
            dw_ref[...] += dw

    return pl.pallas_call(
        body,
        name="loss_head",
        grid=(bsz, nb),
        in_specs=[
            pl.BlockSpec((1, BLOCK, d), lambda b, n: (b, n, 0)),
            pl.BlockSpec((1, d), lambda b, n: (0, 0)),
            pl.BlockSpec((1, BLOCK, d), lambda b, n: (b, jnp.maximum(n - 1, 0), 0)),
        ],
        out_specs=[
            pl.BlockSpec((1, 128), lambda b, n: (0, 0)),
            pl.BlockSpec((1, BLOCK, d), lambda b, n: (b, n, 0)),
            pl.BlockSpec((1, d), lambda b, n: (0, 0)),
        ],
        out_shape=[jax.ShapeDtypeStruct((1, 128), F32), jax.ShapeDtypeStruct((bsz, l, d), F32),
                   jax.ShapeDtypeStruct((1, d), F32)],
        compiler_params=_cparams(("arbitrary", "arbitrary")),
    )(h, final_w, target)


def _pad_tiles(w, src, heads, lane_slot, axis):
    pieces = []
    for h in range(heads):
        x = lax.slice_in_dim(w, src + HEAD_DIM * h, src + HEAD_DIM * (h + 1), axis=axis)
        z = jnp.zeros_like(x)
        pieces += [x, z] if lane_slot(h) == 0 else [z, x]
    return pieces


def _unpad_tiles(g, off, heads, lane_slot, axis):
    return [lax.slice_in_dim(g, off + LANES * h + HEAD_DIM * lane_slot(h),
                             off + LANES * h + HEAD_DIM * (lane_slot(h) + 1), axis=axis) for h in range(heads)]


A_SLOT = lambda h: h // A_GROUP
B_SLOT = lambda h: h % 2


def _layout_w_in(w):
    pad_f = jnp.zeros((w.shape[0], F_COLS - B_HEADS + TAIL_COLS), w.dtype)
    return jnp.concatenate(
        _pad_tiles(w, 0, A_HEADS, A_SLOT, 1) + [w[:, SRC_KA:SRC_KB]]
        + _pad_tiles(w, SRC_KB, B_HEADS, B_SLOT, 1) + _pad_tiles(w, SRC_VB, B_HEADS, B_SLOT, 1)
        + [w[:, SRC_GA:], w[:, SRC_F:SRC_GA], pad_f], axis=1)


def _unlayout_w_in(g):
    return jnp.concatenate(
        _unpad_tiles(g, OFF_QA, A_HEADS, A_SLOT, 1) + [g[:, OFF_KA:OFF_KB]]
        + _unpad_tiles(g, OFF_KB, B_HEADS, B_SLOT, 1) + _unpad_tiles(g, OFF_VB, B_HEADS, B_SLOT, 1)
        + [g[:, OFF_F:OFF_F + B_HEADS], g[:, OFF_GA:OFF_F]], axis=1)


def _local_step(x, target, meta, norms, b_forget, sinks, w, comm=None):
    n1, nmix, n2, nfin = norms
    w1i, w1o = w[:2]
    bsz, seq, d = x.shape
    l = PREFIX + seq
    nb = l // BLOCK
    t = bsz * l

    h0 = jnp.concatenate([jnp.zeros((bsz, N_PAD, d), F32),
                          jnp.broadcast_to(meta[None], (bsz, N_META, d)), x], axis=1).reshape(t, d)

    if comm is None:
        (h1, g1, u1), _ = _ffn_fwd(h0, n1, w1i, w1o)
        w_in, wa, wb, wo, w2i, w2o = w[2:]
    else:
        (h1, g1, u1), rest = _ffn_fwd(h0, n1, w1i, w1o, comm.rest_gather())
        w_in, wa, wb, wo, w2i, w2o = comm.rest_weights(rest)
    wp = _layout_w_in(w_in)
    wa_p = jnp.concatenate(_pad_tiles(wa, 0, A_HEADS, A_SLOT, 0), axis=0)
    un, qa, ka, va, qb, kb, vb, ga, gb, f_logit = _proj_fwd(h1, nmix, wp)
    b_pad = jnp.concatenate([b_forget, jnp.zeros((1, F_COLS - B_HEADS), F32)], axis=1)
    c = _forget_cumsum(f_logit, b_pad, nb)
    c_heads = c[:, :B_HEADS].reshape(bsz, l, B_HEADS).transpose(0, 2, 1).reshape(bsz * B_HEADS, l)
    c_col = c_heads[:, :, None]
    c_row = c_heads.reshape(bsz * B_HEADS, nb, 1, BLOCK)

    slopes = jnp.exp2(-8.0 * jnp.arange(1, A_HEADS + 1, dtype=F32) / A_HEADS)
    slope_rows = jnp.repeat(slopes.reshape(A_KV_HEADS, A_GROUP), BLOCK, axis=1)[:, :, None]
    sink_rows = jnp.repeat(sinks.reshape(A_KV_HEADS, A_GROUP), BLOCK, axis=1)[:, :, None]

    oa, lse_a = _swa_fwd(qa, ka, va, sink_rows, slope_rows, nb)
    ob, ob_exact, lse_b = _fox_fwd(qb, kb, vb, c_col, c_row, nb)
    h2, mixed = _merge_fwd(h1, oa, ob, ga, gb, wa_p, wb, wo)
    (h3, g2, u2), _ = _ffn_fwd(h2, n2, w2i, w2o)
    loss, dh3, d_nfin = _loss_head(h3.reshape(bsz, l, d), nfin, target)

    (dh2, n2b, a2, dgu2, df2, dn2_parts), _ = _ffn_bwd(dh3.reshape(t, d), h2, n2, g2, u2, w2i, w2o)
    g_w2o = _tn_matmul(a2, df2, "grad_ffn2_w_out")
    g_w2i = _tn_matmul(n2b, dgu2, "grad_ffn2_w_in")

    dya, dyb, doa, dob, dga, dgb, dh2b = _merge_bwd(dh2, oa, ob, ga, gb, wa_p, wb, wo)
    g_wo = _tn_matmul(mixed, dh2b, "grad_w_out")
    g_wa = jnp.concatenate(_unpad_tiles(_tn_matmul(oa, dya, "grad_w_branch_a"), 0, A_HEADS, A_SLOT, 0), axis=0)
    g_wb = _tn_matmul(ob, dyb, "grad_w_branch_b")

    dqa, dka, dva, dsink_rows = _swa_bwd(qa, ka, va, doa, lse_a, sink_rows, slope_rows, nb)
    hosted = comm.scatter("ffn2", dict(ffn2_w_in=g_w2i, ffn2_w_out=g_w2o)) if comm else None
    (dqb, dkb, dvb, dc_row), pieces = _fox_bwd(qb, kb, vb, ob_exact, dob, lse_b, c_col, c_row, nb, hosted)
    if comm:
        comm.received("ffn2", pieces)
    dc = dc_row.reshape(bsz, B_HEADS, l).transpose(0, 2, 1).reshape(t, B_HEADS)
    dc = jnp.concatenate([dc, jnp.zeros((t, F_COLS - B_HEADS), F32)], axis=1)
    df_logit, db_parts = _forget_cumsum_bwd(dc, f_logit, b_pad, nb)

    dproj = jnp.concatenate([dqa, dka, dva, dqb, dkb, dvb, dga, dgb, df_logit,
                             jnp.zeros((t, TAIL_COLS), BF16)], axis=1)
    dh1, dnmix_parts = _proj_bwd(dh2, h1, nmix, dproj, wp)
    g_win = _unlayout_w_in(_tn_matmul(un, dproj, "grad_w_in"))

    hosted = comm.scatter("mixer", dict(w_in=g_win, w_branch_a=g_wa, w_branch_b=g_wb, w_out=g_wo)) if comm else None
    (dh0, n1b, a1, dgu1, df1, dn1_parts), pieces = _ffn_bwd(dh1, h0, n1, g1, u1, w1i, w1o, hosted)
    if comm:
        comm.received("mixer", pieces)
    g_w1o = _tn_matmul(a1, df1, "grad_ffn1_w_out")
    g_w1i = _tn_matmul(n1b, dgu1, "grad_ffn1_w_in")

    dh0 = dh0.reshape(bsz, l, d)
    grad_x = dh0[:, PREFIX:]
    small = dict(
        meta_tokens=jnp.sum(dh0[:, N_PAD:PREFIX], axis=0),
        ffn1_norm=jnp.sum(dn1_parts, axis=0),
        mix_norm=jnp.sum(dnmix_parts, axis=0),
        ffn2_norm=jnp.sum(dn2_parts, axis=0),
        final_norm=d_nfin,
        b_forget=jnp.sum(db_parts, axis=0)[:, :B_HEADS],
        attn_sinks=jnp.sum(dsink_rows.reshape(bsz, A_HEADS, BLOCK), axis=(0, 2)).reshape(1, A_HEADS),
    )
    big = dict(ffn1_w_in=g_w1i, ffn1_w_out=g_w1o, w_in=g_win, w_branch_a=g_wa, w_branch_b=g_wb,
               w_out=g_wo, ffn2_w_in=g_w2i, ffn2_w_out=g_w2o)
    return loss, grad_x, small, big


BIG = (
    ("ffn1_w_in", (D_MODEL, 5632), 1),
    ("ffn1_w_out", (2816, D_MODEL), 0),
    ("w_in", (D_MODEL, W_IN_COLS), 1),
    ("w_branch_a", (A_WIDTH, D_MODEL), 1),
    ("w_branch_b", (B_WIDTH, D_MODEL), 1),
    ("w_out", (D_MODEL, D_MODEL), 0),
    ("ffn2_w_in", (D_MODEL, 5632), 1),
    ("ffn2_w_out", (2816, D_MODEL), 0),
)
STACKED = "w_in"


def _coords():
    return lax.axis_index("x"), lax.axis_index("y"), lax.axis_index("c")


def _other_chips(x, y):
    return ((1 - x, y), (x, 1 - y), (1 - x, 1 - y))


def _chip_part(ref, name, shape, axis, k):
    if name == STACKED:
        return ref.at[k]
    size = shape[axis] // N_CHIPS
    start = pl.multiple_of(k * size, size)
    return ref.at[pl.ds(start, size), :] if axis == 0 else ref.at[:, pl.ds(start, size)]


def _full_shape(name, shape):
    return (N_CHIPS, shape[0], shape[1] // N_CHIPS) if name == STACKED else shape


class _Exchange:
    def __init__(self, ins, out_shape, n_sems, ops):
        self.ins, self.out_shape, self.n_sems, self.ops = list(ins), list(out_shape), n_sems, ops

    @property
    def scratch(self):
        return [pltpu.SemaphoreType.DMA((self.n_sems,)), pltpu.SemaphoreType.DMA((self.n_sems,))]


SEMS_PER_GATHER = 7


def _gather_exchange(shards, table):
    n = len(table)

    def ops(ins, outs, send_sems, recv_sems):
        x, y, c = _coords()
        mine = 2 * x + y
        sibling = (x, y, 1 - c)
        chips = _other_chips(x, y)
        slots = [2 * chip[0] + chip[1] for chip in chips]

        def part(i, k):
            name, shape, axis = table[i][:3]
            return _chip_part(outs[i], name, shape, axis, k)

        def half(ref, h):
            rows = ref.shape[0] // 2
            return ref.at[pl.ds(pl.multiple_of(h * rows, rows), rows), :]

        def own(i):
            sem = SEMS_PER_GATHER * i
            return pltpu.make_async_remote_copy(ins[i], part(i, mine), send_sems.at[sem], recv_sems.at[sem],
                                                device_id=sibling, device_id_type=MESH_ID)

        def fetch(i, j, slot):
            sem = SEMS_PER_GATHER * i + 1 + j
            if table[i][4]:
                src, dst = half(ins[i], c), half(part(i, slot), c)
            else:
                src, dst = ins[i], part(i, slot)
            return pltpu.make_async_remote_copy(src, dst, send_sems.at[sem], recv_sems.at[sem],
                                                device_id=(chips[j][0], chips[j][1], c), device_id_type=MESH_ID)

        def forward(i, j, h):
            sem = SEMS_PER_GATHER * i + 4 + j
            region = half(part(i, slots[j]), h)
            return pltpu.make_async_remote_copy(region, region, send_sems.at[sem], recv_sems.at[sem],
                                                device_id=sibling, device_id_type=MESH_ID)

        def start():
            for i in range(n):
                for j in range(3):
                    fetch(i, j, mine).start()
            for i in range(n):
                own(i).start()

        def relay():
            for i in range(n):
                for j in range(3):
                    fetch(i, j, slots[j]).wait_recv()
                    if table[i][4]:
                        forward(i, j, c).start()

        def finish():
            for i in range(n):
                own(i).wait()
                for j in range(3):
                    if table[i][4]:
                        forward(i, j, 1 - c).wait_recv()
                        forward(i, j, c).wait_send()
                    fetch(i, j, mine).wait_send()

        return start, relay, finish

    out_shape = [jax.ShapeDtypeStruct(_full_shape(name, shape), dtype) for name, shape, _, dtype, _ in table]
    return _Exchange(shards, out_shape, SEMS_PER_GATHER * n, ops)


def _run_exchange(exchange, name):
    n = len(exchange.ins)

    def body(*refs):
        start, relay, finish = exchange.ops(refs[:n], refs[n:2 * n], *refs[2 * n:])
        start()
        relay()
        finish()

    return pl.pallas_call(
        body,
        name=name,
        in_specs=[HBM_SPEC] * n,
        out_specs=[HBM_SPEC] * n,
        out_shape=exchange.out_shape,
        scratch_shapes=exchange.scratch,
    )(*exchange.ins)


def _host_exchange(exchange, in_refs, out_refs, sem_refs, first, middle, last):
    start, relay, finish = exchange.ops(in_refs, out_refs, *sem_refs)
    pl.when(first)(start)
    pl.when(middle)(relay)
    pl.when(last)(finish)


def _halves_view(name, shape, axis):
    r, c = shape
    if name == STACKED:
        return (N_CHIPS, 2, r // 2, c // N_CHIPS), lambda ref, h: ref.at[:, h]
    if axis == 1:
        return (2, r // 2, c), lambda ref, h: ref.at[h]
    return (N_CHIPS, 2, r // N_CHIPS // 2, c), lambda ref, h: ref.at[:, h]


def _exchange_halves(grads, entries, tag):
    n_w = len(entries)
    views = [_halves_view(*entry) for entry in entries]

    def body(*refs):
        ins, outs = refs[:n_w], refs[n_w:2 * n_w]
        send_sems, recv_sems = refs[2 * n_w:]
        x, y, c = _coords()
        copies = [pltpu.make_async_remote_copy(views[i][1](ins[i], 1 - c), outs[i], send_sems.at[i], recv_sems.at[i],
                                               device_id=(x, y, 1 - c), device_id_type=MESH_ID) for i in range(n_w)]
        for cp in copies:
            cp.start()
        for cp in copies:
            cp.wait()

    half_shape = lambda v: tuple(d for i, d in enumerate(v) if i != (1 if len(v) == 4 else 0))
    return pl.pallas_call(
        body,
        name="exchange_halves_" + tag,
        in_specs=[HBM_SPEC] * n_w,
        out_specs=[HBM_SPEC] * n_w,
        out_shape=[jax.ShapeDtypeStruct(half_shape(v[0]), F32) for v in views],
        scratch_shapes=[pltpu.SemaphoreType.DMA((n_w,)), pltpu.SemaphoreType.DMA((n_w,))],
    )(*[g.reshape(v[0]) for g, v in zip(grads, views)])


def _add_sibling(g_view, recv, c, name):
    shape = recv.shape
    if len(shape) == 2:
        tr = _tile(shape[0], 128, 16)
        grid = (shape[0] // tr,)
        g_spec = pl.BlockSpec((None, tr, shape[1]), lambda i, c_ref: (c_ref[0], i, 0))
        r_spec = pl.BlockSpec((tr, shape[1]), lambda i, c_ref: (i, 0))
    else:
        tr = _tile(shape[1], 256, 16)
        grid = (N_CHIPS, shape[1] // tr)
        g_spec = pl.BlockSpec((None, None, tr, shape[2]), lambda k, i, c_ref: (k, c_ref[0], i, 0))
        r_spec = pl.BlockSpec((None, tr, shape[2]), lambda k, i, c_ref: (k, i, 0))

    def body(c_ref, g_ref, r_ref, o_ref):
        o_ref[...] = (g_ref[...] + r_ref[...]).astype(BF16)

    return pl.pallas_call(
        body,
        name="add_sibling_" + name,
        grid_spec=pltpu.PrefetchScalarGridSpec(num_scalar_prefetch=1, grid=grid, in_specs=[g_spec, r_spec],
                                               out_specs=r_spec),
        out_shape=jax.ShapeDtypeStruct(shape, BF16),
        compiler_params=_cparams(("parallel",) * len(grid)),
    )(c, g_view, recv)


def _piece_of(ref, name, axis, k):
    if name == STACKED or axis == 0:
        return ref.at[k]
    size = ref.shape[1] // N_CHIPS
    return ref.at[:, pl.ds(pl.multiple_of(k * size, size), size)]


def _piece_shape(name, shape, axis):
    r, c = shape
    return (r // 2, c // N_CHIPS) if (axis == 1) else (r // N_CHIPS // 2, c)


def _scatter_exchange(partials, entries):
    n_w = len(entries)

    def ops(ins, outs, send_sems, recv_sems):
        x, y, c = _coords()
        chips = _other_chips(x, y)
        copies = []
        for i, (name, _, axis) in enumerate(entries):
            for j, chip in enumerate(chips):
                sem = 3 * i + j
                copies.append(pltpu.make_async_remote_copy(
                    _piece_of(ins[i], name, axis, 2 * chip[0] + chip[1]), outs[i].at[j], send_sems.at[sem],
                    recv_sems.at[sem], device_id=(chip[0], chip[1], c), device_id_type=MESH_ID))

        def start():
            for cp in copies:
                cp.start()

        def finish():
            for cp in copies:
                cp.wait()

        return start, lambda: None, finish

    out_shape = [jax.ShapeDtypeStruct((3,) + _piece_shape(*entry), BF16) for entry in entries]
    return _Exchange(partials, out_shape, 3 * n_w, ops)


def _add_chips(partial, recv, mine, name, axis):
    rows, cols = recv.shape[1:]
    tr = _tile(rows, 256, 16)
    if name == STACKED or axis == 0:
        p_spec = pl.BlockSpec((None, tr, cols), lambda i, k_ref: (k_ref[0], i, 0))
    else:
        p_spec = pl.BlockSpec((tr, cols), lambda i, k_ref: (i, k_ref[0]))

    def body(k_ref, p_ref, r_ref, o_ref):
        f32 = lambda a: a.astype(F32)
        o_ref[...] = ((f32(p_ref[...]) + f32(r_ref[0])) + f32(r_ref[1])) + f32(r_ref[2])

    return pl.pallas_call(
        body,
        name="add_chips_" + name,
        grid_spec=pltpu.PrefetchScalarGridSpec(
            num_scalar_prefetch=1, grid=(rows // tr,),
            in_specs=[p_spec, pl.BlockSpec((3, tr, cols), lambda i, k_ref: (0, i, 0))],
            out_specs=pl.BlockSpec((tr, cols), lambda i, k_ref: (i, 0))),
        out_shape=jax.ShapeDtypeStruct((rows, cols), F32),
        compiler_params=_cparams(("parallel",)),
    )(mine, partial, recv)


def _share_with_sibling(halves):
    n_w = len(halves)

    def body(*refs):
        ins, outs = refs[:n_w], refs[n_w:2 * n_w]
        send_sems, recv_sems = refs[2 * n_w:]
        x, y, c = _coords()
        copies = [pltpu.make_async_remote_copy(ins[i], outs[i], send_sems.at[i], recv_sems.at[i],
                                               device_id=(x, y, 1 - c), device_id_type=MESH_ID) for i in range(n_w)]
        for cp in copies:
            cp.start()
        for cp in copies:
            cp.wait()

    return pl.pallas_call(
        body,
        name="share_with_sibling",
        in_specs=[HBM_SPEC] * n_w,
        out_specs=[HBM_SPEC] * n_w,
        out_shape=[jax.ShapeDtypeStruct(h.shape, F32) for h in halves],
        scratch_shapes=[pltpu.SemaphoreType.DMA((n_w,)), pltpu.SemaphoreType.DMA((n_w,))],
    )(*halves)


SMALL_ROWS = 168


def _all_reduce_small(buf):
    def body(b_ref, out_ref, gathered, send_sems, recv_sems):
        x, y, c = _coords()
        me = 4 * x + 2 * y + c
        peers = [(x ^ fx, y ^ fy, c ^ fc) for fx in (0, 1) for fy in (0, 1) for fc in (0, 1)][1:]

        def copy(j, slot, dev):
            return pltpu.make_async_remote_copy(b_ref, gathered.at[slot], send_sems.at[j], recv_sems.at[j],
                                                device_id=dev, device_id_type=MESH_ID)

        for j, dev in enumerate(peers):
            copy(j, me, dev).start()
        gathered[me] = b_ref[...]
        for j, dev in enumerate(peers):
            copy(j, 4 * dev[0] + 2 * dev[1] + dev[2], dev).wait()
        acc = gathered[0]
        for d in range(1, N_DEV):
            acc = acc + gathered[d]
        out_ref[...] = acc

    return pl.pallas_call(
        body,
        name="all_reduce_small",
        in_specs=[VMEM_SPEC],
        out_specs=VMEM_SPEC,
        out_shape=jax.ShapeDtypeStruct(buf.shape, F32),
        scratch_shapes=[pltpu.VMEM((N_DEV,) + buf.shape, F32), pltpu.SemaphoreType.DMA((N_DEV - 1,)),
                        pltpu.SemaphoreType.DMA((N_DEV - 1,))],
    )(buf)


def _adamw(w, g, m, v):
    r, c = w.shape
    tr = _tile(r, 256, 8)

    def body(w_ref, g_ref, m_ref, v_ref, d_ref, mo_ref, vo_ref):
        gg = g_ref[...]
        mm = ADAM_B1 * m_ref[...] + (1.0 - ADAM_B1) * gg
        vv = ADAM_B2 * v_ref[...] + (1.0 - ADAM_B2) * (gg * gg)
        m_hat = mm / (1.0 - ADAM_B1 ** ADAM_STEP)
        v_hat = vv / (1.0 - ADAM_B2 ** ADAM_STEP)
        d_ref[...] = -ADAM_LR * (m_hat / (jnp.sqrt(v_hat) + ADAM_EPS) + ADAM_WD * w_ref[...])
        mo_ref[...] = mm
        vo_ref[...] = vv

    spec = pl.BlockSpec((tr, c), lambda i: (i, 0))
    return pl.pallas_call(
        body,
        name="adamw",
        grid=(r // tr,),
        in_specs=[spec] * 4,
        out_specs=[spec] * 3,
        out_shape=[jax.ShapeDtypeStruct((r, c), F32)] * 3,
        compiler_params=_cparams(("parallel",)),
    )(w, g, m, v)


def _adamw_halves(w, own, other, m, v, c, name):
    r, cols = w.shape
    half = r // 2
    tr = _tile(half, 256, 8)
    nt = half // tr

    def body(c_ref, w_ref, own_ref, other_ref, m_ref, v_ref, g_ref, d_ref, mo_ref, vo_ref):
        gg = jnp.where(pl.program_id(0) == c_ref[0], own_ref[...], other_ref[...])
        g_ref[...] = gg
        mm = ADAM_B1 * m_ref[...] + (1.0 - ADAM_B1) * gg
        vv = ADAM_B2 * v_ref[...] + (1.0 - ADAM_B2) * (gg * gg)
        m_hat = mm / (1.0 - ADAM_B1 ** ADAM_STEP)
        v_hat = vv / (1.0 - ADAM_B2 ** ADAM_STEP)
        d_ref[...] = -ADAM_LR * (m_hat / (jnp.sqrt(v_hat) + ADAM_EPS) + ADAM_WD * w_ref[...])
        mo_ref[...] = mm
        vo_ref[...] = vv

    whole = pl.BlockSpec((tr, cols), lambda h, i, c_ref: (h * nt + i, 0))
    part = pl.BlockSpec((tr, cols), lambda h, i, c_ref: (i, 0))
    return pl.pallas_call(
        body,
        name="adamw_" + name,
        grid_spec=pltpu.PrefetchScalarGridSpec(
            num_scalar_prefetch=1, grid=(2, nt),
            in_specs=[whole, part, part, whole, whole], out_specs=[whole] * 4),
        out_shape=[jax.ShapeDtypeStruct((r, cols), F32)] * 4,
        compiler_params=_cparams(("parallel", "parallel")),
    )(c, w, own, other, m, v)


GATHER_FIRST = ("ffn1_w_in", "ffn1_w_out")


class _Comm:
    def __init__(self, shards, c_arr, mine_arr):
        self.shards, self.c, self.mine = shards, c_arr, mine_arr
        self.groups, self.halves = {}, {}
        self.by_name = {entry[0]: entry for entry in BIG}

    def rest_gather(self):
        names = [n for n, _, _ in BIG if n not in GATHER_FIRST]
        table = [self.by_name[n] + (BF16, True) for n in names]
        return _gather_exchange([self.shards[n] for n in names], table)

    def rest_weights(self, outs):
        full = dict(zip([n for n, _, _ in BIG if n not in GATHER_FIRST], outs))
        full[STACKED] = full[STACKED].transpose(1, 0, 2).reshape(D_MODEL, W_IN_COLS)
        return [full[n] for n, _, _ in BIG if n not in GATHER_FIRST]

    def scatter(self, tag, grads):
        entries = [self.by_name[n] for n in grads]
        arrays = [g.reshape(D_MODEL, N_CHIPS, W_IN_COLS // N_CHIPS).transpose(1, 0, 2) if n == STACKED else g
                  for n, g in grads.items()]
        views = [_halves_view(*entry) for entry in entries]
        received = _exchange_halves(arrays, entries, tag)
        partials = [_add_sibling(g.reshape(v[0]), r, self.c, name)
                    for g, v, r, (name, _, _) in zip(arrays, views, received, entries)]
        self.groups[tag] = (entries, partials)
        return _scatter_exchange(partials, entries)

    def received(self, tag, pieces):
        entries, partials = self.groups[tag]
        for p, r, (name, _, axis) in zip(partials, pieces, entries):
            self.halves[name] = _add_chips(p, r, self.mine, name, axis)

    def finish(self):
        names = [n for n, _, _ in BIG]
        own = [self.halves[n] for n in names]
        return dict(zip(names, zip(own, _share_with_sibling(own))))


def kernel(x, meta_tokens, ffn1_norm, ffn1_w_in, ffn1_w_out, mix_norm, w_in, b_forget, attn_sinks, w_branch_a, w_branch_b, w_out, ffn2_norm, ffn2_w_in, ffn2_w_out, final_norm, loss_target, m_meta_tokens, m_ffn1_norm, m_ffn1_w_in, m_ffn1_w_out, m_mix_norm, m_w_in, m_b_forget, m_attn_sinks, m_w_branch_a, m_w_branch_b, m_w_out, m_ffn2_norm, m_ffn2_w_in, m_ffn2_w_out, m_final_norm, v_meta_tokens, v_ffn1_norm, v_ffn1_w_in, v_ffn1_w_out, v_mix_norm, v_w_in, v_b_forget, v_attn_sinks, v_w_branch_a, v_w_branch_b, v_w_out, v_ffn2_norm, v_ffn2_w_in, v_ffn2_w_out, v_final_norm):
    given = dict(locals())
    names = ["meta_tokens", "ffn1_norm", "ffn1_w_in", "ffn1_w_out", "mix_norm", "w_in", "b_forget", "attn_sinks",
             "w_branch_a", "w_branch_b", "w_out", "ffn2_norm", "ffn2_w_in", "ffn2_w_out", "final_norm"]
    big_names = [n for n, _, _ in BIG]
    cx, cy, cc = _coords()
    c_arr = cc.reshape(1).astype(jnp.int32)
    mine_arr = (2 * cx + cy).reshape(1).astype(jnp.int32)

    comm = _Comm({n: given[n][0].astype(BF16) for n in big_names}, c_arr, mine_arr)
    table = [comm.by_name[n] + (BF16, True) for n in GATHER_FIRST] + [("meta_tokens", (N_META, D_MODEL), 1, F32, False)]
    first = _gather_exchange([comm.shards[n] for n in GATHER_FIRST] + [meta_tokens], table)
    w1i, w1o, meta_full = _run_exchange(first, "gather_first")
    norms = (ffn1_norm, mix_norm, ffn2_norm, final_norm.reshape(1, D_MODEL))
    loss, grad_x, small, big = _local_step(x, loss_target, meta_full, norms, b_forget, attn_sinks, (w1i, w1o), comm)

    last = comm.scatter("ffn1", dict(ffn1_w_in=big["ffn1_w_in"], ffn1_w_out=big["ffn1_w_out"]))
    comm.received("ffn1", _run_exchange(last, "scatter_chip_sums"))
    grad_halves = comm.finish()
    grads = {}

    pad_lanes = lambda a: jnp.concatenate([a, jnp.zeros((1, LANES - a.shape[1]), F32)], axis=1)
    buf = jnp.concatenate([
        small["meta_tokens"].reshape(128, LANES),
        small["ffn1_norm"].reshape(8, LANES), small["mix_norm"].reshape(8, LANES),
        small["ffn2_norm"].reshape(8, LANES), small["final_norm"].reshape(8, LANES),
        loss, pad_lanes(small["b_forget"]), pad_lanes(small["attn_sinks"]),
        jnp.zeros((SMALL_ROWS - 163, LANES), F32)], axis=0)
    red = _all_reduce_small(buf)
    meta_cols = red[:128].reshape(N_META, D_MODEL)
    grads["meta_tokens"] = lax.dynamic_slice_in_dim(meta_cols, (2 * cx + cy) * (D_MODEL // N_CHIPS),
                                                    D_MODEL // N_CHIPS, axis=1)
    grads["ffn1_norm"] = red[128:136].reshape(1, D_MODEL)
    grads["mix_norm"] = red[136:144].reshape(1, D_MODEL)
    grads["ffn2_norm"] = red[144:152].reshape(1, D_MODEL)
    grads["final_norm"] = red[152:160].reshape(1, D_MODEL)
    loss_out = red[160, 0]
    grads["b_forget"] = red[161:162, :B_HEADS]
    grads["attn_sinks"] = red[162:163, :A_HEADS]

    out_g, out_d, out_m, out_v = [], [], [], []
    for n in names:
        w_full = given[n]
        shape = w_full.shape
        two_d = (lambda a: a.reshape(shape[-2], shape[-1])) if len(shape) >= 2 else (lambda a: a.reshape(1, shape[0]))
        if n in grad_halves:
            own, other = grad_halves[n]
            g2, d2, m2, v2 = _adamw_halves(two_d(w_full), own, other, two_d(given["m_" + n]),
                                           two_d(given["v_" + n]), c_arr, n)
        else:
            g2 = two_d(grads[n])
            d2, m2, v2 = _adamw(two_d(w_full), g2, two_d(given["m_" + n]), two_d(given["v_" + n]))
        out_g.append(g2.reshape(shape))
        out_d.append(d2.reshape(shape))
        out_m.append(m2.reshape(shape))
        out_v.append(v2.reshape(shape))
    return (loss_out, grad_x, *out_g, *out_d, *out_m, *out_v)
```

```python
import jax
import jax.numpy as jnp
from jax import lax
from jax.experimental import pallas as pl
from jax.experimental.pallas import tpu as pltpu

F32 = jnp.float32
BF16 = jnp.bfloat16

D_MODEL = 1024
N_META = 16
BLOCK = 128
LANES = 128
PREFIX = BLOCK
N_PAD = PREFIX - N_META
HEAD_DIM = 64
A_HEADS = 8
A_KV_HEADS = 2
A_GROUP = 4
B_HEADS = 8
B_PAIRS = B_HEADS // 2
A_WIDTH = A_HEADS * HEAD_DIM
A_KV_WIDTH = A_KV_HEADS * HEAD_DIM
B_WIDTH = B_HEADS * HEAD_DIM
W_IN_COLS = A_WIDTH + 2 * A_KV_WIDTH + 3 * B_WIDTH + B_HEADS + 2 * D_MODEL
SRC_KA = A_WIDTH
SRC_VA = SRC_KA + A_KV_WIDTH
SRC_QB = SRC_VA + A_KV_WIDTH
SRC_KB = SRC_QB + B_WIDTH
SRC_VB = SRC_KB + B_WIDTH
SRC_F = SRC_VB + B_WIDTH
SRC_GA = SRC_F + B_HEADS
SRC_GB = SRC_GA + D_MODEL
A_PAD_WIDTH = A_HEADS * LANES
B_PAD_WIDTH = B_HEADS * LANES
F_COLS = LANES
OFF_QA = 0
OFF_KA = OFF_QA + A_PAD_WIDTH
OFF_VA = OFF_KA + A_KV_WIDTH
OFF_QB = OFF_VA + A_KV_WIDTH
OFF_KB = OFF_QB + B_WIDTH
OFF_VB = OFF_KB + B_PAD_WIDTH
OFF_GA = OFF_VB + B_PAD_WIDTH
OFF_GB = OFF_GA + D_MODEL
OFF_F = OFF_GB + D_MODEL
TAIL_COLS = 128
P_COLS = OFF_F + F_COLS + TAIL_COLS
EPS = 1e-6
NEG = -1e30
SCALE = HEAD_DIM ** -0.5
KEY_BLOCKS = 4

ADAM_LR = 0.001
ADAM_B1 = 0.9
ADAM_B2 = 0.999
ADAM_EPS = 1e-08
ADAM_WD = 0.01
ADAM_STEP = 10

N_CHIPS = 4
N_DEV = 8
VMEM_LIMIT = 56 * 1024 * 1024

NT_DIMS = (((1,), (1,)), ((), ()))
TN_DIMS = (((0,), (0,)), ((), ()))
MESH_ID = pl.DeviceIdType.MESH
HBM_SPEC = pl.BlockSpec(memory_space=pltpu.HBM)
VMEM_SPEC = pl.BlockSpec(memory_space=pltpu.VMEM)


def _tile(n, target, mult=16):
    best = None
    for t in range(mult, min(n, target) + 1, mult):
        if n % t == 0:
            best = t
    return best if best is not None else n


def _cparams(sem):
    return pltpu.CompilerParams(dimension_semantics=sem, vmem_limit_bytes=VMEM_LIMIT)


def _rms_scale(h):
    return lax.rsqrt(jnp.mean(h * h, axis=-1, keepdims=True) + EPS)


def _rms_bwd(dn, h, w):
    r = _rms_scale(h)
    dw = jnp.sum(dn * (h * r), axis=0, keepdims=True)
    z = dn * w
    dh = r * z - h * ((r * r * r) * jnp.mean(z * h, axis=-1, keepdims=True))
    return dh, dw


def _ffn_fwd(h, norm_w, w_in, w_out, exchange=None):
    t, d = h.shape
    f = w_out.shape[0]
    tm = _tile(t, 272)
    tc = _tile(f, 256, 128)
    nj = f // tc
    ni = t // tm
    n_x = len(exchange.ins) if exchange else 0

    def body(*refs):
        h_ref, nw_ref, wi_ref, wo_ref = refs[:4]
        hout_ref, g_ref, u_ref = refs[4 + n_x:7 + n_x]
        a_scr = refs[7 + 2 * n_x]
        i = pl.program_id(0)
        if exchange:
            _host_exchange(exchange, refs[4:4 + n_x], refs[7 + n_x:7 + 2 * n_x], refs[8 + 2 * n_x:],
                           i == 0, i == ni - 2, i == ni - 1)
        hh = h_ref[...]
        n = ((hh * _rms_scale(hh)) * nw_ref[...]).astype(BF16)
        for j in range(nj):
            cols = slice(j * tc, (j + 1) * tc)
            g = jnp.dot(n, wi_ref[:, j * tc:(j + 1) * tc], preferred_element_type=F32)
            u = jnp.dot(n, wi_ref[:, f + j * tc:f + (j + 1) * tc], preferred_element_type=F32)
            g_ref[:, cols] = g
            u_ref[:, cols] = u
            a_scr[:, cols] = ((g * jax.nn.sigmoid(g)) * u).astype(BF16)
        hout_ref[...] = hh + 0.5 * jnp.dot(a_scr[...], wo_ref[...], preferred_element_type=F32)

    resident = lambda a: pl.BlockSpec(a.shape, lambda i: (0, 0), pipeline_mode=pl.Buffered(1))
    row = lambda w: pl.BlockSpec((tm, w), lambda i: (i, 0))
    outs = pl.pallas_call(
        body,
        name="ffn_fwd",
        grid=(ni,),
        in_specs=[row(d), pl.BlockSpec((1, d), lambda i: (0, 0)), resident(w_in), resident(w_out)] + [HBM_SPEC] * n_x,
        out_specs=[row(d), row(f), row(f)] + [HBM_SPEC] * n_x,
        out_shape=[
            jax.ShapeDtypeStruct((t, d), F32),
            jax.ShapeDtypeStruct((t, f), F32),
            jax.ShapeDtypeStruct((t, f), F32),
        ] + (exchange.out_shape if exchange else []),
        scratch_shapes=[pltpu.VMEM((tm, f), BF16)] + (exchange.scratch if exchange else []),
        compiler_params=_cparams(("arbitrary",) if exchange else ("parallel",)),
    )(h, norm_w, w_in, w_out, *(exchange.ins if exchange else []))
    return outs[:3], outs[3:]


def _ffn_bwd(dh_out, h, norm_w, g, u, w_in, w_out, exchange=None):
    t, d = h.shape
    f = w_out.shape[0]
    tm = _tile(t, 544)
    tc = _tile(f, 256, 128)
    nj = f // tc
    ni = t // tm
    n_x = len(exchange.ins) if exchange else 0

    def body(*refs):
        dho_ref, h_ref, nw_ref, g_ref, u_ref, wg_ref, wu_ref, wo_ref = refs[:8]
        dhin_ref, n_ref, a_ref, dgu_ref, df_ref, dnw_ref = refs[8 + n_x:14 + n_x]
        dn_scr = refs[14 + 2 * n_x]
        i = pl.program_id(0)
        j = pl.program_id(1)
        if exchange:
            _host_exchange(exchange, refs[8:8 + n_x], refs[14 + n_x:14 + 2 * n_x], refs[15 + 2 * n_x:],
                           (i == 0) & (j == 0), (i == ni - 1) & (j == 0), (i == ni - 1) & (j == nj - 1))

        @pl.when(j == 0)
        def _():
            hh = h_ref[...]
            n_ref[...] = ((hh * _rms_scale(hh)) * nw_ref[...]).astype(BF16)
            df_ref[...] = (0.5 * dho_ref[...]).astype(BF16)
            dn_scr[...] = jnp.zeros_like(dn_scr)

        da = lax.dot_general(df_ref[...], wo_ref[...], NT_DIMS, preferred_element_type=F32)
        gg = g_ref[...]
        uu = u_ref[...]
        sig = jax.nn.sigmoid(gg)
        sl = gg * sig
        a_ref[...] = (sl * uu).astype(BF16)
        dg = ((da * uu) * (sig * (1.0 + gg * (1.0 - sig)))).astype(BF16)
        du = (da * sl).astype(BF16)
        dgu_ref[0] = dg
        dgu_ref[1] = du
        dn_scr[...] += (lax.dot_general(dg, wg_ref[...], NT_DIMS, preferred_element_type=F32)
                        + lax.dot_general(du, wu_ref[...], NT_DIMS, preferred_element_type=F32))

        @pl.when(j == nj - 1)
        def _():
            dh, dw = _rms_bwd(dn_scr[...], h_ref[...], nw_ref[...])
            dhin_ref[...] = dho_ref[...] + dh
            dnw_ref[0] = dw

    outs = pl.pallas_call(
        body,
        name="ffn_bwd",
        grid=(ni, nj),
        in_specs=[
            pl.BlockSpec((tm, d), lambda i, j: (i, 0)),
            pl.BlockSpec((tm, d), lambda i, j: (i, 0)),
            pl.BlockSpec((1, d), lambda i, j: (0, 0)),
            pl.BlockSpec((tm, tc), lambda i, j: (i, j)),
            pl.BlockSpec((tm, tc), lambda i, j: (i, j)),
            pl.BlockSpec((d, tc), lambda i, j: (0, j)),
            pl.BlockSpec((d, tc), lambda i, j: (0, j + nj)),
            pl.BlockSpec((tc, d), lambda i, j: (j, 0)),
        ] + [HBM_SPEC] * n_x,
        out_specs=[
            pl.BlockSpec((tm, d), lambda i, j: (i, 0)),
            pl.BlockSpec((tm, d), lambda i, j: (i, 0)),
            pl.BlockSpec((tm, tc), lambda i, j: (i, j)),
            pl.BlockSpec((2, tm, tc), lambda i, j: (0, i, j)),
            pl.BlockSpec((tm, d), lambda i, j: (i, 0)),
            pl.BlockSpec((1, 1, d), lambda i, j: (i, 0, 0)),
        ] + [HBM_SPEC] * n_x,
        out_shape=[
            jax.ShapeDtypeStruct((t, d), F32),
            jax.ShapeDtypeStruct((t, d), BF16),
            jax.ShapeDtypeStruct((t, f), BF16),
            jax.ShapeDtypeStruct((2, t, f), BF16),
            jax.ShapeDtypeStruct((t, d), BF16),
            jax.ShapeDtypeStruct((ni, 1, d), F32),
        ] + (exchange.out_shape if exchange else []),
        scratch_shapes=[pltpu.VMEM((tm, d), F32)] + (exchange.scratch if exchange else []),
        compiler_params=_cparams(("arbitrary", "arbitrary") if exchange else ("parallel", "arbitrary")),
    )(dh_out, h, norm_w, g, u, w_in, w_in, w_out, *(exchange.ins if exchange else []))
    return outs[:6], outs[6:]


def _tn_matmul(a, b, name):
    t, k = a.shape
    split = b.ndim == 3
    n = 2 * b.shape[2] if split else b.shape[1]
    tk = _tile(k, 512, 128)
    tn = _tile(b.shape[-1], 1408, 128)
    per_half = b.shape[-1] // tn

    def body(a_ref, b_ref, o_ref):
        o_ref[...] = lax.dot_general(a_ref[...], b_ref[...], TN_DIMS, preferred_element_type=F32)

    if split:
        b_spec = pl.BlockSpec((None, t, tn), lambda i, j: (j // per_half, 0, j % per_half))
    else:
        b_spec = pl.BlockSpec((t, tn), lambda i, j: (0, j))
    return pl.pallas_call(
        body,
        name=name,
        grid=(k // tk, n // tn),
        in_specs=[pl.BlockSpec((t, tk), lambda i, j: (0, i)), b_spec],
        out_specs=pl.BlockSpec((tk, tn), lambda i, j: (i, j)),
        out_shape=jax.ShapeDtypeStruct((k, n), F32),
        compiler_params=_cparams(("parallel", "parallel")),
    )(a, b)


PROJ_PARTS = (
    (OFF_QA, A_PAD_WIDTH, True), (OFF_KA, A_KV_WIDTH, True), (OFF_VA, A_KV_WIDTH, True),
    (OFF_QB, B_WIDTH, True), (OFF_KB, B_PAD_WIDTH, True), (OFF_VB, B_PAD_WIDTH, True),
    (OFF_GA, D_MODEL, False), (OFF_GB, D_MODEL, False), (OFF_F, F_COLS, False),
)


def _proj_fwd(h, norm_w, w_p):
    t, d = h.shape
    tm = _tile(t, 272)

    def body(h_ref, nw_ref, w_ref, u_ref, *part_refs):
        hh = h_ref[...]
        un = ((hh * _rms_scale(hh)) * nw_ref[...]).astype(BF16)
        u_ref[...] = un
        for (off, width, _), p_ref in zip(PROJ_PARTS, part_refs):
            p_ref[...] = jnp.dot(un, w_ref[:, off:off + width], preferred_element_type=F32).astype(p_ref.dtype)

    row = lambda w: pl.BlockSpec((tm, w), lambda i: (i, 0))
    return pl.pallas_call(
        body,
        name="proj_fwd",
        grid=(t // tm,),
        in_specs=[row(d), pl.BlockSpec((1, d), lambda i: (0, 0)), pl.BlockSpec(w_p.shape, lambda i: (0, 0))],
        out_specs=[row(d)] + [row(width) for _, width, _ in PROJ_PARTS],
        out_shape=[jax.ShapeDtypeStruct((t, d), BF16)]
        + [jax.ShapeDtypeStruct((t, width), BF16 if is_bf else F32) for _, width, is_bf in PROJ_PARTS],
        compiler_params=_cparams(("parallel",)),
    )(h, norm_w, w_p)


def _proj_bwd(dh_out, h, norm_w, dproj, w_p):
    t, d = h.shape
    n = w_p.shape[1]
    tm = _tile(t, 272)
    ni = t // tm

    def body(dho_ref, h_ref, nw_ref, dp_ref, w_ref, dhin_ref, dnw_ref):
        dn = lax.dot_general(dp_ref[...], w_ref[...], NT_DIMS, preferred_element_type=F32)
        dh, dw = _rms_bwd(dn, h_ref[...], nw_ref[...])
        dhin_ref[...] = dho_ref[...] + dh
        dnw_ref[0] = dw

    row = lambda w: pl.BlockSpec((tm, w), lambda i: (i, 0))
    return pl.pallas_call(
        body,
        name="proj_bwd",
        grid=(ni,),
        in_specs=[row(d), row(d), pl.BlockSpec((1, d), lambda i: (0, 0)), row(n),
                  pl.BlockSpec(w_p.shape, lambda i: (0, 0), pipeline_mode=pl.Buffered(1))],
        out_specs=[row(d), pl.BlockSpec((1, 1, d), lambda i: (i, 0, 0))],
        out_shape=[jax.ShapeDtypeStruct((t, d), F32), jax.ShapeDtypeStruct((ni, 1, d), F32)],
        compiler_params=_cparams(("parallel",)),
    )(dh_out, h, norm_w, dproj, w_p)


def _merge_fwd(h, oa, ob, ga, gb, wa, wb, wo):
    t, d = h.shape
    tm = _tile(t, 544)

    def body(h_ref, oa_ref, ob_ref, ga_ref, gb_ref, wa_ref, wb_ref, wo_ref, hout_ref, mix_ref):
        ya = jnp.dot(oa_ref[...], wa_ref[...], preferred_element_type=F32)
        yb = jnp.dot(ob_ref[...], wb_ref[...], preferred_element_type=F32)
        mixed = (jax.nn.sigmoid(ga_ref[...]) * ya + jax.nn.sigmoid(gb_ref[...]) * yb).astype(BF16)
        mix_ref[...] = mixed
        hout_ref[...] = h_ref[...] + jnp.dot(mixed, wo_ref[...], preferred_element_type=F32)

    row = lambda w: pl.BlockSpec((tm, w), lambda i: (i, 0))
    full = lambda a: pl.BlockSpec(a.shape, lambda i: (0, 0))
    return pl.pallas_call(
        body,
        name="merge_fwd",
        grid=(t // tm,),
        in_specs=[row(d), row(oa.shape[1]), row(ob.shape[1]), row(d), row(d), full(wa), full(wb), full(wo)],
        out_specs=[row(d), row(d)],
        out_shape=[jax.ShapeDtypeStruct((t, d), F32), jax.ShapeDtypeStruct((t, d), BF16)],
        compiler_params=_cparams(("parallel",)),
    )(h, oa, ob, ga, gb, wa, wb, wo)


def _merge_bwd(dh, oa, ob, ga, gb, wa, wb, wo):
    t, d = dh.shape
    tm = _tile(t, 544)

    def body(dh_ref, oa_ref, ob_ref, ga_ref, gb_ref, wa_ref, wb_ref, wo_ref,
             dya_ref, dyb_ref, doa_ref, dob_ref, dga_ref, dgb_ref, dhb_ref):
        dhb = dh_ref[...].astype(BF16)
        dhb_ref[...] = dhb
        dmix = lax.dot_general(dhb, wo_ref[...], NT_DIMS, preferred_element_type=F32)
        for o_ref, g_ref, w_ref, dy_ref, do_ref, dg_ref in (
                (oa_ref, ga_ref, wa_ref, dya_ref, doa_ref, dga_ref),
                (ob_ref, gb_ref, wb_ref, dyb_ref, dob_ref, dgb_ref)):
            y = jnp.dot(o_ref[...], w_ref[...], preferred_element_type=F32)
            s = jax.nn.sigmoid(g_ref[...])
            dy = (dmix * s).astype(BF16)
            dy_ref[...] = dy
            dg_ref[...] = ((dmix * y) * (s * (1.0 - s))).astype(BF16)
            do_ref[...] = lax.dot_general(dy, w_ref[...], NT_DIMS, preferred_element_type=F32).astype(BF16)

    row = lambda w: pl.BlockSpec((tm, w), lambda i: (i, 0))
    full = lambda a: pl.BlockSpec(a.shape, lambda i: (0, 0))
    wa_w, wb_w = oa.shape[1], ob.shape[1]
    return pl.pallas_call(
        body,
        name="merge_bwd",
        grid=(t // tm,),
        in_specs=[row(d), row(wa_w), row(wb_w), row(d), row(d), full(wa), full(wb), full(wo)],
        out_specs=[row(d), row(d), row(wa_w), row(wb_w), row(d), row(d), row(d)],
        out_shape=[
            jax.ShapeDtypeStruct((t, d), BF16), jax.ShapeDtypeStruct((t, d), BF16),
            jax.ShapeDtypeStruct((t, wa_w), BF16), jax.ShapeDtypeStruct((t, wb_w), BF16),
            jax.ShapeDtypeStruct((t, d), BF16), jax.ShapeDtypeStruct((t, d), BF16),
            jax.ShapeDtypeStruct((t, d), BF16),
        ],
        compiler_params=_cparams(("parallel",)),
    )(dh, oa, ob, ga, gb, wa, wb, wo)


def _tri_dot(tri, x):
    hi = x.astype(BF16)
    r1 = x - hi.astype(F32)
    mid = r1.astype(BF16)
    lo = (r1 - mid.astype(F32)).astype(BF16)
    return (jnp.dot(tri, hi, preferred_element_type=F32)
            + jnp.dot(tri, mid, preferred_element_type=F32)
            + jnp.dot(tri, lo, preferred_element_type=F32))


def _forget_cumsum(f_logit, b_pad, nb):
    t, w = f_logit.shape
    bsz = t // (nb * BLOCK)

    def body(f_ref, b_ref, c_ref, carry):
        n = pl.program_id(1)

        @pl.when(n == 0)
        def _():
            carry[...] = jnp.zeros_like(carry)

        x = jax.nn.log_sigmoid(f_ref[...] + b_ref[...])
        rows = lax.broadcasted_iota(jnp.int32, (BLOCK, BLOCK), 0)
        cols = lax.broadcasted_iota(jnp.int32, (BLOCK, BLOCK), 1)
        tri = (cols <= rows).astype(BF16)
        c = _tri_dot(tri, x) + carry[...]
        c_ref[...] = c
        carry[...] = c[BLOCK - 1:BLOCK, :]

    return pl.pallas_call(
        body,
        name="forget_cumsum",
        grid=(bsz, nb),
        in_specs=[pl.BlockSpec((BLOCK, w), lambda b, n: (b * nb + n, 0)),
                  pl.BlockSpec((1, w), lambda b, n: (0, 0))],
        out_specs=pl.BlockSpec((BLOCK, w), lambda b, n: (b * nb + n, 0)),
        out_shape=jax.ShapeDtypeStruct((t, w), F32),
        scratch_shapes=[pltpu.VMEM((1, w), F32)],
        compiler_params=_cparams(("parallel", "arbitrary")),
    )(f_logit, b_pad)


def _forget_cumsum_bwd(dc, f_logit, b_pad, nb):
    t, w = f_logit.shape
    bsz = t // (nb * BLOCK)

    def body(dc_ref, f_ref, b_ref, df_ref, db_ref, carry):
        n = pl.program_id(1)

        @pl.when(n == 0)
        def _():
            carry[...] = jnp.zeros_like(carry)
            db_ref[...] = jnp.zeros_like(db_ref)

        rows = lax.broadcasted_iota(jnp.int32, (BLOCK, BLOCK), 0)
        cols = lax.broadcasted_iota(jnp.int32, (BLOCK, BLOCK), 1)
        tri = (cols >= rows).astype(BF16)
        dlf = _tri_dot(tri, dc_ref[...]) + carry[...]
        carry[...] = dlf[0:1, :]
        df = dlf * jax.nn.sigmoid(-(f_ref[...] + b_ref[...]))
        df_ref[...] = df.astype(BF16)
        db_ref[0] += jnp.sum(df, axis=0, keepdims=True)

    rev = lambda b, n: (b * nb + (nb - 1 - n), 0)
    return pl.pallas_call(
        body,
        name="forget_cumsum_bwd",
        grid=(bsz, nb),
        in_specs=[pl.BlockSpec((BLOCK, w), rev),
                  pl.BlockSpec((BLOCK, w), rev),
                  pl.BlockSpec((1, w), lambda b, n: (0, 0))],
        out_specs=[pl.BlockSpec((BLOCK, w), rev),
                   pl.BlockSpec((1, 1, w), lambda b, n: (b, 0, 0))],
        out_shape=[jax.ShapeDtypeStruct((t, w), BF16), jax.ShapeDtypeStruct((bsz, 1, w), F32)],
        scratch_shapes=[pltpu.VMEM((1, w), F32)],
        compiler_params=_cparams(("parallel", "arbitrary")),
    )(dc, f_logit, b_pad)


GROUP_ROWS = A_GROUP * BLOCK


def _stack_heads(ref):
    return jnp.concatenate([ref[:, i * LANES:(i + 1) * LANES] for i in range(A_GROUP)], axis=0)


def _unstack_heads(ref, x):
    for i in range(A_GROUP):
        ref[:, i * LANES:(i + 1) * LANES] = x[i * BLOCK:(i + 1) * BLOCK].astype(ref.dtype)


def _swa_logits(q, km, kp, kc, slope, n):
    qi = lax.broadcasted_iota(jnp.int32, (GROUP_ROWS, BLOCK), 0) & (BLOCK - 1)
    kj = lax.broadcasted_iota(jnp.int32, (GROUP_ROWS, BLOCK), 1)
    out = []
    for kk, dist, ok in (
            (km, n * BLOCK + qi - kj, (kj >= N_PAD) & (n * BLOCK + qi - kj >= 0)),
            (kp, BLOCK + qi - kj, (kj > qi) & (n >= 2)),
            (kc, qi - kj, (kj <= qi) & (n >= 1))):
        s = lax.dot_general(q, kk, NT_DIMS, preferred_element_type=F32) * SCALE
        s = s - slope * dist.astype(F32)
        out.append(jnp.where(ok, s, NEG))
    return out


def _swa_specs(nb):
    row = lambda b, n: b * nb + n
    qspec = pl.BlockSpec((BLOCK, A_GROUP * LANES), lambda b, g, n: (row(b, n), g))
    kv_m = pl.BlockSpec((BLOCK, LANES), lambda b, g, n: (row(b, 0), 0))
    kv_p = pl.BlockSpec((BLOCK, LANES), lambda b, g, n: (row(b, jnp.maximum(n - 1, 0)), 0))
    kv_c = pl.BlockSpec((BLOCK, LANES), lambda b, g, n: (row(b, n), 0))
    rowspec = pl.BlockSpec((1, GROUP_ROWS, 1), lambda b, g, n: (g, 0, 0))
    lsespec = pl.BlockSpec((1, 1, GROUP_ROWS, 1), lambda b, g, n: (row(b, n), g, 0, 0))
    return qspec, kv_m, kv_p, kv_c, rowspec, lsespec


def _swa_fwd(q, k, v, sink_rows, slope_rows, nb):
    t = q.shape[0]
    bsz = t // (nb * BLOCK)

    def body(q_ref, km_ref, kp_ref, kc_ref, vm_ref, vp_ref, vc_ref, sink_ref, slope_ref, o_ref, lse_ref):
        g = pl.program_id(1)
        n = pl.program_id(2)
        qq = _stack_heads(q_ref)
        sink = sink_ref[0]
        s_m, s_p, s_c = _swa_logits(qq, km_ref[...], kp_ref[...], kc_ref[...], slope_ref[0], n)
        m = jnp.maximum(jnp.maximum(jnp.max(s_m, axis=-1, keepdims=True), jnp.max(s_p, axis=-1, keepdims=True)),
                        jnp.maximum(jnp.max(s_c, axis=-1, keepdims=True), sink))
        e_m = jnp.exp(s_m - m)
        e_p = jnp.exp(s_p - m)
        e_c = jnp.exp(s_c - m)
        z = (jnp.sum(e_m, axis=-1, keepdims=True) + jnp.sum(e_p, axis=-1, keepdims=True)
             + jnp.sum(e_c, axis=-1, keepdims=True) + jnp.exp(sink - m))
        inv = 1.0 / z
        o = (jnp.dot((e_m * inv).astype(BF16), vm_ref[...], preferred_element_type=F32)
             + jnp.dot((e_p * inv).astype(BF16), vp_ref[...], preferred_element_type=F32)
             + jnp.dot((e_c * inv).astype(BF16), vc_ref[...], preferred_element_type=F32))
        lane_group = lax.broadcasted_iota(jnp.int32, (GROUP_ROWS, LANES), 1) // HEAD_DIM
        _unstack_heads(o_ref, jnp.where(lane_group == g, o, 0.0))
        lse_ref[0, 0] = m + jnp.log(z)

    qspec, kv_m, kv_p, kv_c, rowspec, lsespec = _swa_specs(nb)
    return pl.pallas_call(
        body,
        name="swa_fwd",
        grid=(bsz, A_KV_HEADS, nb),
        in_specs=[qspec, kv_m, kv_p, kv_c, kv_m, kv_p, kv_c, rowspec, rowspec],
        out_specs=[qspec, lsespec],
        out_shape=[jax.ShapeDtypeStruct((t, A_PAD_WIDTH), BF16),
                   jax.ShapeDtypeStruct((t // BLOCK, A_KV_HEADS, GROUP_ROWS, 1), F32)],
        compiler_params=_cparams(("parallel", "parallel", "arbitrary")),
    )(q, k, k, k, v, v, v, sink_rows, slope_rows)


def _swa_bwd(q, k, v, do, lse, sink_rows, slope_rows, nb):
    t = q.shape[0]
    l = nb * BLOCK
    bsz = t // l

    def body(q_ref, km_ref, kp_ref, kc_ref, vm_ref, vp_ref, vc_ref, do_ref, lse_ref, sink_ref, slope_ref,
             dq_ref, dk_ref, dv_ref, dsink_ref, dk_acc, dv_acc):
        g = pl.program_id(1)
        n = pl.program_id(2)

        @pl.when((g == 0) & (n == 0))
        def _():
            dk_acc[...] = jnp.zeros_like(dk_acc)
            dv_acc[...] = jnp.zeros_like(dv_acc)

        @pl.when(n == 0)
        def _():
            dsink_ref[...] = jnp.zeros_like(dsink_ref)

        qq = _stack_heads(q_ref)
        dob = _stack_heads(do_ref)
        lse = lse_ref[0, 0]
        logits = _swa_logits(qq, km_ref[...], kp_ref[...], kc_ref[...], slope_ref[0], n)
        probs = [jnp.exp(s - lse) for s in logits]
        dps = [lax.dot_general(dob, v_ref[...], NT_DIMS, preferred_element_type=F32)
               for v_ref in (vm_ref, vp_ref, vc_ref)]
        delta = sum(jnp.sum(p * dp, axis=-1, keepdims=True) for p, dp in zip(probs, dps))
        prev = jnp.maximum(n - 1, 0)
        dq = jnp.zeros((GROUP_ROWS, LANES), F32)
        for p, dp, k_ref, start in ((probs[0], dps[0], km_ref, 0),
                                    (probs[1], dps[1], kp_ref, prev * BLOCK),
                                    (probs[2], dps[2], kc_ref, n * BLOCK)):
            ds = (p * (dp - delta)).astype(BF16)
            dq = dq + jnp.dot(ds, k_ref[...], preferred_element_type=F32)
            rows = pl.ds(pl.multiple_of(start, BLOCK), BLOCK)
            dk_acc[rows, :] += lax.dot_general(ds, qq, TN_DIMS, preferred_element_type=F32) * SCALE
            dv_acc[rows, :] += lax.dot_general(p.astype(BF16), dob, TN_DIMS, preferred_element_type=F32)
        _unstack_heads(dq_ref, dq * SCALE)
        dsink_ref[0, 0] += -(jnp.exp(sink_ref[0] - lse) * delta)

        @pl.when((g == A_KV_HEADS - 1) & (n == nb - 1))
        def _():
            dk_ref[...] = dk_acc[...].astype(BF16)
            dv_ref[...] = dv_acc[...].astype(BF16)

    qspec, kv_m, kv_p, kv_c, rowspec, lsespec = _swa_specs(nb)
    kv_all = pl.BlockSpec((l, LANES), lambda b, g, n: (b, 0))
    return pl.pallas_call(
        body,
        name="swa_bwd",
        grid=(bsz, A_KV_HEADS, nb),
        in_specs=[qspec, kv_m, kv_p, kv_c, kv_m, kv_p, kv_c, qspec, lsespec, rowspec, rowspec],
        out_specs=[qspec, kv_all, kv_all,
                   pl.BlockSpec((1, 1, GROUP_ROWS, 1), lambda b, g, n: (b, g, 0, 0))],
        out_shape=[jax.ShapeDtypeStruct((t, A_PAD_WIDTH), BF16),
                   jax.ShapeDtypeStruct((t, LANES), BF16),
                   jax.ShapeDtypeStruct((t, LANES), BF16),
                   jax.ShapeDtypeStruct((bsz, A_KV_HEADS, GROUP_ROWS, 1), F32)],
        scratch_shapes=[pltpu.VMEM((l, LANES), F32), pltpu.VMEM((l, LANES), F32)],
        compiler_params=_cparams(("parallel", "arbitrary", "arbitrary")),
    )(q, k, k, k, v, v, v, do, lse, sink_rows, slope_rows)


CHUNK = KEY_BLOCKS * BLOCK


def _fox_chunk(qb, ci):
    sb = jnp.maximum(jnp.minimum(KEY_BLOCKS * ci, qb + 1 - KEY_BLOCKS), 0)
    lo = jnp.maximum(ci * CHUNK, N_PAD)
    return sb, lo, pl.ds(pl.multiple_of(sb * BLOCK, BLOCK), CHUNK)


def _fox_logits(s_ref, cr_ref, e, j, sb, lo, qb):
    q_pos = qb * BLOCK + lax.broadcasted_iota(jnp.int32, (BLOCK, BLOCK), 0)
    k_pos = (sb + j) * BLOCK + lax.broadcasted_iota(jnp.int32, (BLOCK, BLOCK), 1)
    s = s_ref[e, :, j * BLOCK:(j + 1) * BLOCK] - cr_ref[e, sb + j]
    return jnp.where((k_pos <= q_pos) & (k_pos >= lo), s, NEG)


def _fox_specs(nb):
    l = nb * BLOCK
    q_spec = pl.BlockSpec((BLOCK, LANES), lambda b, p, i: (b * nb + i, p))
    kv_spec = pl.BlockSpec((l, 2 * LANES), lambda b, p, i: (b, p))
    cc_spec = pl.BlockSpec((2, BLOCK, 1), lambda b, p, i: (b * B_PAIRS + p, i, 0))
    cr_spec = pl.BlockSpec((2, nb, 1, BLOCK), lambda b, p, i: (b * B_PAIRS + p, 0, 0, 0))
    return q_spec, kv_spec, cc_spec, cr_spec


def _fox_fwd(q, k, v, c_row, nb):
    t = q.shape[0]
    bsz = t // (nb * BLOCK)
    assert nb >= KEY_BLOCKS

    def body(q_ref, k_ref, v_ref, cr_ref, o_ref, ox_ref, lse_ref, s_scr, hi_scr, lo_scr):
        qb = pl.program_id(2)
        qs = q_ref[...] * SCALE
        first_half = lax.broadcasted_iota(jnp.int32, (BLOCK, LANES), 1) < HEAD_DIM

        def step(ci, carry):
            stats, acc, acc_lo = carry[:4], carry[4], carry[5]
            sb, lo, krows = _fox_chunk(qb, ci)
            new_stats, alphas = [], []
            pv = jnp.zeros((BLOCK, LANES), F32)
            pv_lo = jnp.zeros((BLOCK, LANES), F32)
            for e in range(2):
                m, z = stats[2 * e], stats[2 * e + 1]
                tile = slice(e * LANES, (e + 1) * LANES)
                s_scr[e] = lax.dot_general(qs, k_ref[krows, tile], NT_DIMS, preferred_element_type=F32)
                m_new = m
                for j in range(KEY_BLOCKS):
                    s = _fox_logits(s_scr, cr_ref, e, j, sb, lo, qb)
                    s_scr[e, :, j * BLOCK:(j + 1) * BLOCK] = s
                    m_new = jnp.maximum(m_new, jnp.max(s, axis=-1, keepdims=True))
                alpha = jnp.exp(m - m_new)
                z = alpha * z
                for j in range(KEY_BLOCKS):
                    cols = slice(j * BLOCK, (j + 1) * BLOCK)
                    p = jnp.exp(s_scr[e, :, cols] - m_new)
                    z = z + jnp.sum(p, axis=-1, keepdims=True)
                    hi = p.astype(BF16)
                    hi_scr[e, :, cols] = hi
                    lo_scr[e, :, cols] = (p - hi.astype(F32)).astype(BF16)
                vv = v_ref[krows, tile]
                pv = pv + jnp.dot(hi_scr[e], vv, preferred_element_type=F32)
                pv_lo = pv_lo + jnp.dot(lo_scr[e], vv, preferred_element_type=F32)
                new_stats += [m_new, z]
                alphas.append(alpha)
            alpha = jnp.where(first_half, alphas[0], alphas[1])
            return (*new_stats, alpha * acc + pv, alpha * acc_lo + pv_lo)

        col = lambda val: jnp.full((BLOCK, 1), val, F32)
        m0, z0, m1, z1, acc, acc_lo = lax.fori_loop(
            0, (qb + KEY_BLOCKS) // KEY_BLOCKS, step,
            (col(NEG), col(0.0), col(NEG), col(0.0), jnp.zeros((BLOCK, LANES), F32), jnp.zeros((BLOCK, LANES), F32)))
        inv = 1.0 / jnp.where(first_half, z0, z1)
        o_ref[...] = (acc * inv).astype(BF16)
        ox_ref[...] = (acc + acc_lo) * inv
        lse_ref[0] = m0 + jnp.log(z0)
        lse_ref[1] = m1 + jnp.log(z1)

    q_spec, kv_spec, cc_spec, cr_spec = _fox_specs(nb)
    return pl.pallas_call(
        body,
        name="fox_fwd",
        grid=(bsz, B_PAIRS, nb),
        in_specs=[q_spec, kv_spec, kv_spec, cr_spec],
        out_specs=[q_spec, q_spec, cc_spec],
        out_shape=[jax.ShapeDtypeStruct((t, B_WIDTH), BF16), jax.ShapeDtypeStruct((t, B_WIDTH), F32),
                   jax.ShapeDtypeStruct((bsz * B_HEADS, nb * BLOCK, 1), F32)],
        scratch_shapes=[pltpu.VMEM((2, BLOCK, CHUNK), F32), pltpu.VMEM((2, BLOCK, CHUNK), BF16),
                        pltpu.VMEM((2, BLOCK, CHUNK), BF16)],
        compiler_params=_cparams(("parallel", "parallel", "arbitrary")),
    )(q, k, v, c_row)


def _fox_bwd(q, k, v, o_exact, do, lse, c_row, nb, exchange=None):
    t = q.shape[0]
    l = nb * BLOCK
    bsz = t // l

    n_x = len(exchange.ins) if exchange else 0

    def body(*refs):
        q_ref, k_ref, v_ref, ox_ref, do_ref, lse_ref, cr_ref = refs[:7]
        dq_ref, dk_ref, dv_ref, dc_ref = refs[7 + n_x:11 + n_x]
        dk_acc, dv_acc, s_scr, dp_scr, p_scr, ds_scr = refs[11 + 2 * n_x:17 + 2 * n_x]
        qb = pl.program_id(2)
        if exchange:
            first = (pl.program_id(0) == 0) & (pl.program_id(1) == 0)
            last = (pl.program_id(0) == bsz - 1) & (pl.program_id(1) == B_PAIRS - 1)
            _host_exchange(exchange, refs[7:7 + n_x], refs[11 + n_x:11 + 2 * n_x], refs[17 + 2 * n_x:],
                           first & (qb == 0), last & (qb == 0), last & (qb == nb - 1))

        @pl.when(qb == 0)
        def _():
            dk_acc[...] = jnp.zeros_like(dk_acc)
            dv_acc[...] = jnp.zeros_like(dv_acc)
            dc_ref[...] = jnp.zeros_like(dc_ref)

        qs = q_ref[...] * SCALE
        dob = do_ref[...]
        first_half = lax.broadcasted_iota(jnp.int32, (BLOCK, LANES), 1) < HEAD_DIM
        weighted = dob.astype(F32) * ox_ref[...]
        deltas = (jnp.sum(jnp.where(first_half, weighted, 0.0), axis=-1, keepdims=True),
                  jnp.sum(jnp.where(first_half, 0.0, weighted), axis=-1, keepdims=True))

        def step(ci, dq):
            sb, lo, krows = _fox_chunk(qb, ci)
            for e in range(2):
                tile = slice(e * LANES, (e + 1) * LANES)
                kk = k_ref[krows, tile]
                s_scr[e] = lax.dot_general(qs, kk, NT_DIMS, preferred_element_type=F32)
                dp_scr[e] = lax.dot_general(dob, v_ref[krows, tile], NT_DIMS, preferred_element_type=F32)
                lse_e = lse_ref[e]
                for j in range(KEY_BLOCKS):
                    cols = slice(j * BLOCK, (j + 1) * BLOCK)
                    p = jnp.exp(_fox_logits(s_scr, cr_ref, e, j, sb, lo, qb) - lse_e)
                    ds = p * (dp_scr[e, :, cols] - deltas[e])
                    dc_ref[e, sb + j] -= jnp.sum(ds, axis=0, keepdims=True)
                    p_scr[e, :, cols] = p.astype(BF16)
                    ds_scr[e, :, cols] = ds.astype(BF16)
                dsb = ds_scr[e]
                dq = dq + jnp.dot(dsb, kk, preferred_element_type=F32)
                dk_acc[krows, tile] += lax.dot_general(dsb, qs, TN_DIMS, preferred_element_type=F32)
                dv_acc[krows, tile] += lax.dot_general(p_scr[e], dob, TN_DIMS, preferred_element_type=F32)
            return dq

        dq = lax.fori_loop(0, (qb + KEY_BLOCKS) // KEY_BLOCKS, step, jnp.zeros((BLOCK, LANES), F32))
        dq_ref[...] = (dq * SCALE).astype(BF16)

        @pl.when(qb == nb - 1)
        def _():
            dk_ref[...] = dk_acc[...].astype(BF16)
            dv_ref[...] = dv_acc[...].astype(BF16)

    q_spec, kv_spec, cc_spec, cr_spec = _fox_specs(nb)
    outs = pl.pallas_call(
        body,
        name="fox_bwd",
        grid=(bsz, B_PAIRS, nb),
        in_specs=[q_spec, kv_spec, kv_spec, q_spec, q_spec, cc_spec, cr_spec] + [HBM_SPEC] * n_x,
        out_specs=[q_spec, kv_spec, kv_spec, cr_spec] + [HBM_SPEC] * n_x,
        out_shape=[jax.ShapeDtypeStruct((t, B_WIDTH), BF16), jax.ShapeDtypeStruct((t, B_PAD_WIDTH), BF16),
                   jax.ShapeDtypeStruct((t, B_PAD_WIDTH), BF16),
                   jax.ShapeDtypeStruct((bsz * B_HEADS, nb, 1, BLOCK), F32)] + (exchange.out_shape if exchange else []),
        scratch_shapes=[pltpu.VMEM((l, 2 * LANES), F32), pltpu.VMEM((l, 2 * LANES), F32),
                        pltpu.VMEM((2, BLOCK, CHUNK), F32), pltpu.VMEM((2, BLOCK, CHUNK), F32),
                        pltpu.VMEM((2, BLOCK, CHUNK), BF16), pltpu.VMEM((2, BLOCK, CHUNK), BF16)]
        + (exchange.scratch if exchange else []),
        compiler_params=_cparams(("arbitrary",) * 3 if exchange else ("parallel", "parallel", "arbitrary")),
    )(q, k, v, o_exact, do, lse, c_row, *(exchange.ins if exchange else []))
    return outs[:4], outs[4:]


def _loss_head(h, final_w, target):
    bsz, l, d = h.shape
    nb = l // BLOCK

    def body(h_ref, w_ref, t_ref, loss_ref, dh_ref, dw_ref):
        b = pl.program_id(0)
        n = pl.program_id(1)

        @pl.when((b == 0) & (n == 0))
        def _():
            loss_ref[...] = jnp.zeros_like(loss_ref)
            dw_ref[...] = jnp.zeros_like(dw_ref)

        @pl.when(n == 0)
        def _():
            dh_ref[...] = jnp.zeros_like(dh_ref)

        @pl.when(n > 0)
        def _():
            hh = h_ref[0]
            w = w_ref[...]
            r = _rms_scale(hh)
            err = (hh * r) * w - t_ref[0]
            loss_ref[...] += 0.5 * jnp.sum(jnp.mean(err * err, axis=-1, keepdims=True), axis=0, keepdims=True)
            dy = err * (1.0 / d)
            dh, dw = _rms_bwd(dy, hh, w)
            dh_ref[0] = dh
            dw_ref[...] += dw

    return pl.pallas_call(
        body,
        name="loss_head",
        grid=(bsz, nb),
        in_specs=[
            pl.BlockSpec((1, BLOCK, d), lambda b, n: (b, n, 0)),
            pl.BlockSpec((1, d), lambda b, n: (0, 0)),
            pl.BlockSpec((1, BLOCK, d), lambda b, n: (b, jnp.maximum(n - 1, 0), 0)),
        ],
        out_specs=[
            pl.BlockSpec((1, 128), lambda b, n: (0, 0)),
            pl.BlockSpec((1, BLOCK, d), lambda b, n: (b, n, 0)),
            pl.BlockSpec((1, d), lambda b, n: (0, 0)),
        ],
        out_shape=[jax.ShapeDtypeStruct((1, 128), F32), jax.ShapeDtypeStruct((bsz, l, d), F32),
                   jax.ShapeDtypeStruct((1, d), F32)],
        compiler_params=_cparams(("arbitrary", "arbitrary")),
    )(h, final_w, target)


def _pad_tiles(w, src, heads, lane_slot, axis):
    pieces = []
    for h in range(heads):
        x = lax.slice_in_dim(w, src + HEAD_DIM * h, src + HEAD_DIM * (h + 1), axis=axis)
        z = jnp.zeros_like(x)
        pieces += [x, z] if lane_slot(h) == 0 else [z, x]
    return pieces


def _unpad_tiles(g, off, heads, lane_slot, axis):
    return [lax.slice_in_dim(g, off + LANES * h + HEAD_DIM * lane_slot(h),
                             off + LANES * h + HEAD_DIM * (lane_slot(h) + 1), axis=axis) for h in range(heads)]


A_SLOT = lambda h: h // A_GROUP
B_SLOT = lambda h: h % 2


def _layout_w_in(w):
    pad_f = jnp.zeros((w.shape[0], F_COLS - B_HEADS + TAIL_COLS), w.dtype)
    return jnp.concatenate(
        _pad_tiles(w, 0, A_HEADS, A_SLOT, 1) + [w[:, SRC_KA:SRC_KB]]
        + _pad_tiles(w, SRC_KB, B_HEADS, B_SLOT, 1) + _pad_tiles(w, SRC_VB, B_HEADS, B_SLOT, 1)
        + [w[:, SRC_GA:], w[:, SRC_F:SRC_GA], pad_f], axis=1)


def _unlayout_w_in(g):
    return jnp.concatenate(
        _unpad_tiles(g, OFF_QA, A_HEADS, A_SLOT, 1) + [g[:, OFF_KA:OFF_KB]]
        + _unpad_tiles(g, OFF_KB, B_HEADS, B_SLOT, 1) + _unpad_tiles(g, OFF_VB, B_HEADS, B_SLOT, 1)
        + [g[:, OFF_F:OFF_F + B_HEADS], g[:, OFF_GA:OFF_F]], axis=1)


def _local_step(x, target, meta, norms, b_forget, sinks, w, comm=None):
    n1, nmix, n2, nfin = norms
    w1i, w1o = w[:2]
    bsz, seq, d = x.shape
    l = PREFIX + seq
    nb = l // BLOCK
    t = bsz * l

    h0 = jnp.concatenate([jnp.zeros((bsz, N_PAD, d), F32),
                          jnp.broadcast_to(meta[None], (bsz, N_META, d)), x], axis=1).reshape(t, d)

    if comm is None:
        (h1, g1, u1), _ = _ffn_fwd(h0, n1, w1i, w1o)
        w_in, wa, wb, wo, w2i, w2o = w[2:]
    else:
        (h1, g1, u1), rest = _ffn_fwd(h0, n1, w1i, w1o, comm.rest_gather())
        w_in, wa, wb, wo, w2i, w2o = comm.rest_weights(rest)
    wp = _layout_w_in(w_in)
    wa_p = jnp.concatenate(_pad_tiles(wa, 0, A_HEADS, A_SLOT, 0), axis=0)
    un, qa, ka, va, qb, kb, vb, ga, gb, f_logit = _proj_fwd(h1, nmix, wp)
    b_pad = jnp.concatenate([b_forget, jnp.zeros((1, F_COLS - B_HEADS), F32)], axis=1)
    c = _forget_cumsum(f_logit, b_pad, nb)
    c_heads = c[:, :B_HEADS].reshape(bsz, l, B_HEADS).transpose(0, 2, 1).reshape(bsz * B_HEADS, l)
    c_row = c_heads.reshape(bsz * B_HEADS, nb, 1, BLOCK)

    slopes = jnp.exp2(-8.0 * jnp.arange(1, A_HEADS + 1, dtype=F32) / A_HEADS)
    slope_rows = jnp.repeat(slopes.reshape(A_KV_HEADS, A_GROUP), BLOCK, axis=1)[:, :, None]
    sink_rows = jnp.repeat(sinks.reshape(A_KV_HEADS, A_GROUP), BLOCK, axis=1)[:, :, None]

    oa, lse_a = _swa_fwd(qa, ka, va, sink_rows, slope_rows, nb)
    ob, ob_exact, lse_b = _fox_fwd(qb, kb, vb, c_row, nb)
    h2, mixed = _merge_fwd(h1, oa, ob, ga, gb, wa_p, wb, wo)
    (h3, g2, u2), _ = _ffn_fwd(h2, n2, w2i, w2o)
    loss, dh3, d_nfin = _loss_head(h3.reshape(bsz, l, d), nfin, target)

    (dh2, n2b, a2, dgu2, df2, dn2_parts), _ = _ffn_bwd(dh3.reshape(t, d), h2, n2, g2, u2, w2i, w2o)
    g_w2o = _tn_matmul(a2, df2, "grad_ffn2_w_out")
    g_w2i = _tn_matmul(n2b, dgu2, "grad_ffn2_w_in")

    dya, dyb, doa, dob, dga, dgb, dh2b = _merge_bwd(dh2, oa, ob, ga, gb, wa_p, wb, wo)
    g_wo = _tn_matmul(mixed, dh2b, "grad_w_out")
    g_wa = jnp.concatenate(_unpad_tiles(_tn_matmul(oa, dya, "grad_w_branch_a"), 0, A_HEADS, A_SLOT, 0), axis=0)
    g_wb = _tn_matmul(ob, dyb, "grad_w_branch_b")

    dqa, dka, dva, dsink_rows = _swa_bwd(qa, ka, va, doa, lse_a, sink_rows, slope_rows, nb)
    hosted = comm.scatter("ffn2", dict(ffn2_w_in=g_w2i, ffn2_w_out=g_w2o)) if comm else None
    (dqb, dkb, dvb, dc_row), pieces = _fox_bwd(qb, kb, vb, ob_exact, dob, lse_b, c_row, nb, hosted)
    if comm:
        comm.received("ffn2", pieces)
    dc = dc_row.reshape(bsz, B_HEADS, l).transpose(0, 2, 1).reshape(t, B_HEADS)
    dc = jnp.concatenate([dc, jnp.zeros((t, F_COLS - B_HEADS), F32)], axis=1)
    df_logit, db_parts = _forget_cumsum_bwd(dc, f_logit, b_pad, nb)

    dproj = jnp.concatenate([dqa, dka, dva, dqb, dkb, dvb, dga, dgb, df_logit,
                             jnp.zeros((t, TAIL_COLS), BF16)], axis=1)
    dh1, dnmix_parts = _proj_bwd(dh2, h1, nmix, dproj, wp)
    g_win = _unlayout_w_in(_tn_matmul(un, dproj, "grad_w_in"))

    hosted = comm.scatter("mixer", dict(w_in=g_win, w_branch_a=g_wa, w_branch_b=g_wb, w_out=g_wo)) if comm else None
    (dh0, n1b, a1, dgu1, df1, dn1_parts), pieces = _ffn_bwd(dh1, h0, n1, g1, u1, w1i, w1o, hosted)
    if comm:
        comm.received("mixer", pieces)
    g_w1o = _tn_matmul(a1, df1, "grad_ffn1_w_out")
    g_w1i = _tn_matmul(n1b, dgu1, "grad_ffn1_w_in")

    dh0 = dh0.reshape(bsz, l, d)
    grad_x = dh0[:, PREFIX:]
    small = dict(
        meta_tokens=jnp.sum(dh0[:, N_PAD:PREFIX], axis=0),
        ffn1_norm=jnp.sum(dn1_parts, axis=0),
        mix_norm=jnp.sum(dnmix_parts, axis=0),
        ffn2_norm=jnp.sum(dn2_parts, axis=0),
        final_norm=d_nfin,
        b_forget=jnp.sum(db_parts, axis=0)[:, :B_HEADS],
        attn_sinks=jnp.sum(dsink_rows.reshape(bsz, A_HEADS, BLOCK), axis=(0, 2)).reshape(1, A_HEADS),
    )
    big = dict(ffn1_w_in=g_w1i, ffn1_w_out=g_w1o, w_in=g_win, w_branch_a=g_wa, w_branch_b=g_wb,
               w_out=g_wo, ffn2_w_in=g_w2i, ffn2_w_out=g_w2o)
    return loss, grad_x, small, big


BIG = (
    ("ffn1_w_in", (D_MODEL, 5632), 1),
    ("ffn1_w_out", (2816, D_MODEL), 0),
    ("w_in", (D_MODEL, W_IN_COLS), 1),
    ("w_branch_a", (A_WIDTH, D_MODEL), 1),
    ("w_branch_b", (B_WIDTH, D_MODEL), 1),
    ("w_out", (D_MODEL, D_MODEL), 0),
    ("ffn2_w_in", (D_MODEL, 5632), 1),
    ("ffn2_w_out", (2816, D_MODEL), 0),
)
STACKED = "w_in"


def _coords():
    return lax.axis_index("x"), lax.axis_index("y"), lax.axis_index("c")


def _other_chips(x, y):
    return ((1 - x, y), (x, 1 - y), (1 - x, 1 - y))


def _chip_part(ref, name, shape, axis, k):
    if name == STACKED:
        return ref.at[k]
    size = shape[axis] // N_CHIPS
    start = pl.multiple_of(k * size, size)
    return ref.at[pl.ds(start, size), :] if axis == 0 else ref.at[:, pl.ds(start, size)]


def _full_shape(name, shape):
    return (N_CHIPS, shape[0], shape[1] // N_CHIPS) if name == STACKED else shape


class _Exchange:
    def __init__(self, ins, out_shape, n_sems, ops):
        self.ins, self.out_shape, self.n_sems, self.ops = list(ins), list(out_shape), n_sems, ops

    @property
    def scratch(self):
        return [pltpu.SemaphoreType.DMA((self.n_sems,)), pltpu.SemaphoreType.DMA((self.n_sems,))]


SEMS_PER_GATHER = 7


def _gather_exchange(shards, table):
    n = len(table)

    def ops(ins, outs, send_sems, recv_sems):
        x, y, c = _coords()
        mine = 2 * x + y
        sibling = (x, y, 1 - c)
        chips = _other_chips(x, y)
        slots = [2 * chip[0] + chip[1] for chip in chips]

        def part(i, k):
            name, shape, axis = table[i][:3]
            return _chip_part(outs[i], name, shape, axis, k)

        def half(ref, h):
            rows = ref.shape[0] // 2
            return ref.at[pl.ds(pl.multiple_of(h * rows, rows), rows), :]

        def own(i):
            sem = SEMS_PER_GATHER * i
            return pltpu.make_async_remote_copy(ins[i], part(i, mine), send_sems.at[sem], recv_sems.at[sem],
                                                device_id=sibling, device_id_type=MESH_ID)

        def fetch(i, j, slot):
            sem = SEMS_PER_GATHER * i + 1 + j
            if table[i][4]:
                src, dst = half(ins[i], c), half(part(i, slot), c)
            else:
                src, dst = ins[i], part(i, slot)
            return pltpu.make_async_remote_copy(src, dst, send_sems.at[sem], recv_sems.at[sem],
                                                device_id=(chips[j][0], chips[j][1], c), device_id_type=MESH_ID)

        def forward(i, j, h):
            sem = SEMS_PER_GATHER * i + 4 + j
            region = half(part(i, slots[j]), h)
            return pltpu.make_async_remote_copy(region, region, send_sems.at[sem], recv_sems.at[sem],
                                                device_id=sibling, device_id_type=MESH_ID)

        def start():
            for i in range(n):
                for j in range(3):
                    fetch(i, j, mine).start()
            for i in range(n):
                own(i).start()

        def relay():
            for i in range(n):
                for j in range(3):
                    fetch(i, j, slots[j]).wait_recv()
                    if table[i][4]:
                        forward(i, j, c).start()

        def finish():
            for i in range(n):
                own(i).wait()
                for j in range(3):
                    if table[i][4]:
                        forward(i, j, 1 - c).wait_recv()
                        forward(i, j, c).wait_send()
                    fetch(i, j, mine).wait_send()

        return start, relay, finish

    out_shape = [jax.ShapeDtypeStruct(_full_shape(name, shape), dtype) for name, shape, _, dtype, _ in table]
    return _Exchange(shards, out_shape, SEMS_PER_GATHER * n, ops)


def _run_exchange(exchange, name):
    n = len(exchange.ins)

    def body(*refs):
        start, relay, finish = exchange.ops(refs[:n], refs[n:2 * n], *refs[2 * n:])
        start()
        relay()
        finish()

    return pl.pallas_call(
        body,
        name=name,
        in_specs=[HBM_SPEC] * n,
        out_specs=[HBM_SPEC] * n,
        out_shape=exchange.out_shape,
        scratch_shapes=exchange.scratch,
    )(*exchange.ins)


def _host_exchange(exchange, in_refs, out_refs, sem_refs, first, middle, last):
    start, relay, finish = exchange.ops(in_refs, out_refs, *sem_refs)
    pl.when(first)(start)
    pl.when(middle)(relay)
    pl.when(last)(finish)


def _halves_view(name, shape, axis):
    r, c = shape
    if name == STACKED:
        return (N_CHIPS, 2, r // 2, c // N_CHIPS), lambda ref, h: ref.at[:, h]
    if axis == 1:
        return (2, r // 2, c), lambda ref, h: ref.at[h]
    return (N_CHIPS, 2, r // N_CHIPS // 2, c), lambda ref, h: ref.at[:, h]


def _exchange_halves(grads, entries, tag):
    n_w = len(entries)
    views = [_halves_view(*entry) for entry in entries]

    def body(*refs):
        ins, outs = refs[:n_w], refs[n_w:2 * n_w]
        send_sems, recv_sems = refs[2 * n_w:]
        x, y, c = _coords()
        copies = [pltpu.make_async_remote_copy(views[i][1](ins[i], 1 - c), outs[i], send_sems.at[i], recv_sems.at[i],
                                               device_id=(x, y, 1 - c), device_id_type=MESH_ID) for i in range(n_w)]
        for cp in copies:
            cp.start()
        for cp in copies:
            cp.wait()

    half_shape = lambda v: tuple(d for i, d in enumerate(v) if i != (1 if len(v) == 4 else 0))
    return pl.pallas_call(
        body,
        name="exchange_halves_" + tag,
        in_specs=[HBM_SPEC] * n_w,
        out_specs=[HBM_SPEC] * n_w,
        out_shape=[jax.ShapeDtypeStruct(half_shape(v[0]), F32) for v in views],
        scratch_shapes=[pltpu.SemaphoreType.DMA((n_w,)), pltpu.SemaphoreType.DMA((n_w,))],
    )(*[g.reshape(v[0]) for g, v in zip(grads, views)])


def _add_sibling(g_view, recv, c, name):
    shape = recv.shape
    if len(shape) == 2:
        tr = _tile(shape[0], 128, 16)
        grid = (shape[0] // tr,)
        g_spec = pl.BlockSpec((None, tr, shape[1]), lambda i, c_ref: (c_ref[0], i, 0))
        r_spec = pl.BlockSpec((tr, shape[1]), lambda i, c_ref: (i, 0))
    else:
        tr = _tile(shape[1], 256, 16)
        grid = (N_CHIPS, shape[1] // tr)
        g_spec = pl.BlockSpec((None, None, tr, shape[2]), lambda k, i, c_ref: (k, c_ref[0], i, 0))
        r_spec = pl.BlockSpec((None, tr, shape[2]), lambda k, i, c_ref: (k, i, 0))

    def body(c_ref, g_ref, r_ref, o_ref):
        o_ref[...] = (g_ref[...] + r_ref[...]).astype(BF16)

    return pl.pallas_call(
        body,
        name="add_sibling_" + name,
        grid_spec=pltpu.PrefetchScalarGridSpec(num_scalar_prefetch=1, grid=grid, in_specs=[g_spec, r_spec],
                                               out_specs=r_spec),
        out_shape=jax.ShapeDtypeStruct(shape, BF16),
        compiler_params=_cparams(("parallel",) * len(grid)),
    )(c, g_view, recv)


def _piece_of(ref, name, axis, k):
    if name == STACKED or axis == 0:
        return ref.at[k]
    size = ref.shape[1] // N_CHIPS
    return ref.at[:, pl.ds(pl.multiple_of(k * size, size), size)]


def _piece_shape(name, shape, axis):
    r, c = shape
    return (r // 2, c // N_CHIPS) if (axis == 1) else (r // N_CHIPS // 2, c)


def _scatter_exchange(partials, entries):
    n_w = len(entries)

    def ops(ins, outs, send_sems, recv_sems):
        x, y, c = _coords()
        chips = _other_chips(x, y)
        copies = []
        for i, (name, _, axis) in enumerate(entries):
            for j, chip in enumerate(chips):
                sem = 3 * i + j
                copies.append(pltpu.make_async_remote_copy(
                    _piece_of(ins[i], name, axis, 2 * chip[0] + chip[1]), outs[i].at[j], send_sems.at[sem],
                    recv_sems.at[sem], device_id=(chip[0], chip[1], c), device_id_type=MESH_ID))

        def start():
            for cp in copies:
                cp.start()

        def finish():
            for cp in copies:
                cp.wait()

        return start, lambda: None, finish

    out_shape = [jax.ShapeDtypeStruct((3,) + _piece_shape(*entry), BF16) for entry in entries]
    return _Exchange(partials, out_shape, 3 * n_w, ops)


def _add_chips(partial, recv, mine, name, axis):
    rows, cols = recv.shape[1:]
    tr = _tile(rows, 256, 16)
    if name == STACKED or axis == 0:
        p_spec = pl.BlockSpec((None, tr, cols), lambda i, k_ref: (k_ref[0], i, 0))
    else:
        p_spec = pl.BlockSpec((tr, cols), lambda i, k_ref: (i, k_ref[0]))

    def body(k_ref, p_ref, r_ref, o_ref):
        f32 = lambda a: a.astype(F32)
        o_ref[...] = ((f32(p_ref[...]) + f32(r_ref[0])) + f32(r_ref[1])) + f32(r_ref[2])

    return pl.pallas_call(
        body,
        name="add_chips_" + name,
        grid_spec=pltpu.PrefetchScalarGridSpec(
            num_scalar_prefetch=1, grid=(rows // tr,),
            in_specs=[p_spec, pl.BlockSpec((3, tr, cols), lambda i, k_ref: (0, i, 0))],
            out_specs=pl.BlockSpec((tr, cols), lambda i, k_ref: (i, 0))),
        out_shape=jax.ShapeDtypeStruct((rows, cols), F32),
        compiler_params=_cparams(("parallel",)),
    )(mine, partial, recv)


def _share_with_sibling(halves):
    n_w = len(halves)

    def body(*refs):
        ins, outs = refs[:n_w], refs[n_w:2 * n_w]
        send_sems, recv_sems = refs[2 * n_w:]
        x, y, c = _coords()
        copies = [pltpu.make_async_remote_copy(ins[i], outs[i], send_sems.at[i], recv_sems.at[i],
                                               device_id=(x, y, 1 - c), device_id_type=MESH_ID) for i in range(n_w)]
        for cp in copies:
            cp.start()
        for cp in copies:
            cp.wait()

    return pl.pallas_call(
        body,
        name="share_with_sibling",
        in_specs=[HBM_SPEC] * n_w,
        out_specs=[HBM_SPEC] * n_w,
        out_shape=[jax.ShapeDtypeStruct(h.shape, F32) for h in halves],
        scratch_shapes=[pltpu.SemaphoreType.DMA((n_w,)), pltpu.SemaphoreType.DMA((n_w,))],
    )(*halves)


SMALL_ROWS = 168


def _all_reduce_small(buf):
    def body(b_ref, out_ref, gathered, send_sems, recv_sems):
        x, y, c = _coords()
        me = 4 * x + 2 * y + c
        peers = [(x ^ fx, y ^ fy, c ^ fc) for fx in (0, 1) for fy in (0, 1) for fc in (0, 1)][1:]

        def copy(j, slot, dev):
            return pltpu.make_async_remote_copy(b_ref, gathered.at[slot], send_sems.at[j], recv_sems.at[j],
                                                device_id=dev, device_id_type=MESH_ID)

        for j, dev in enumerate(peers):
            copy(j, me, dev).start()
        gathered[me] = b_ref[...]
        for j, dev in enumerate(peers):
            copy(j, 4 * dev[0] + 2 * dev[1] + dev[2], dev).wait()
        acc = gathered[0]
        for d in range(1, N_DEV):
            acc = acc + gathered[d]
        out_ref[...] = acc

    return pl.pallas_call(
        body,
        name="all_reduce_small",
        in_specs=[VMEM_SPEC],
        out_specs=VMEM_SPEC,
        out_shape=jax.ShapeDtypeStruct(buf.shape, F32),
        scratch_shapes=[pltpu.VMEM((N_DEV,) + buf.shape, F32), pltpu.SemaphoreType.DMA((N_DEV - 1,)),
                        pltpu.SemaphoreType.DMA((N_DEV - 1,))],
    )(buf)


def _adamw(w, g, m, v):
    r, c = w.shape
    tr = _tile(r, 256, 8)

    def body(w_ref, g_ref, m_ref, v_ref, d_ref, mo_ref, vo_ref):
        gg = g_ref[...]
        mm = ADAM_B1 * m_ref[...] + (1.0 - ADAM_B1) * gg
        vv = ADAM_B2 * v_ref[...] + (1.0 - ADAM_B2) * (gg * gg)
        m_hat = mm / (1.0 - ADAM_B1 ** ADAM_STEP)
        v_hat = vv / (1.0 - ADAM_B2 ** ADAM_STEP)
        d_ref[...] = -ADAM_LR * (m_hat / (jnp.sqrt(v_hat) + ADAM_EPS) + ADAM_WD * w_ref[...])
        mo_ref[...] = mm
        vo_ref[...] = vv

    spec = pl.BlockSpec((tr, c), lambda i: (i, 0))
    return pl.pallas_call(
        body,
        name="adamw",
        grid=(r // tr,),
        in_specs=[spec] * 4,
        out_specs=[spec] * 3,
        out_shape=[jax.ShapeDtypeStruct((r, c), F32)] * 3,
        compiler_params=_cparams(("parallel",)),
    )(w, g, m, v)


def _adamw_halves(w, own, other, m, v, c, name):
    r, cols = w.shape
    half = r // 2
    tr = _tile(half, 256, 8)
    nt = half // tr

    def body(c_ref, w_ref, own_ref, other_ref, m_ref, v_ref, g_ref, d_ref, mo_ref, vo_ref):
        gg = jnp.where(pl.program_id(0) == c_ref[0], own_ref[...], other_ref[...])
        g_ref[...] = gg
        mm = ADAM_B1 * m_ref[...] + (1.0 - ADAM_B1) * gg
        vv = ADAM_B2 * v_ref[...] + (1.0 - ADAM_B2) * (gg * gg)
        m_hat = mm / (1.0 - ADAM_B1 ** ADAM_STEP)
        v_hat = vv / (1.0 - ADAM_B2 ** ADAM_STEP)
        d_ref[...] = -ADAM_LR * (m_hat / (jnp.sqrt(v_hat) + ADAM_EPS) + ADAM_WD * w_ref[...])
        mo_ref[...] = mm
        vo_ref[...] = vv

    whole = pl.BlockSpec((tr, cols), lambda h, i, c_ref: (h * nt + i, 0))
    part = pl.BlockSpec((tr, cols), lambda h, i, c_ref: (i, 0))
    return pl.pallas_call(
        body,
        name="adamw_" + name,
        grid_spec=pltpu.PrefetchScalarGridSpec(
            num_scalar_prefetch=1, grid=(2, nt),
            in_specs=[whole, part, part, whole, whole], out_specs=[whole] * 4),
        out_shape=[jax.ShapeDtypeStruct((r, cols), F32)] * 4,
        compiler_params=_cparams(("parallel", "parallel")),
    )(c, w, own, other, m, v)


GATHER_FIRST = ("ffn1_w_in", "ffn1_w_out")


class _Comm:
    def __init__(self, shards, c_arr, mine_arr):
        self.shards, self.c, self.mine = shards, c_arr, mine_arr
        self.groups, self.halves = {}, {}
        self.by_name = {entry[0]: entry for entry in BIG}

    def rest_gather(self):
        names = [n for n, _, _ in BIG if n not in GATHER_FIRST]
        table = [self.by_name[n] + (BF16, True) for n in names]
        return _gather_exchange([self.shards[n] for n in names], table)

    def rest_weights(self, outs):
        full = dict(zip([n for n, _, _ in BIG if n not in GATHER_FIRST], outs))
        full[STACKED] = full[STACKED].transpose(1, 0, 2).reshape(D_MODEL, W_IN_COLS)
        return [full[n] for n, _, _ in BIG if n not in GATHER_FIRST]

    def scatter(self, tag, grads):
        entries = [self.by_name[n] for n in grads]
        arrays = [g.reshape(D_MODEL, N_CHIPS, W_IN_COLS // N_CHIPS).transpose(1, 0, 2) if n == STACKED else g
                  for n, g in grads.items()]
        views = [_halves_view(*entry) for entry in entries]
        received = _exchange_halves(arrays, entries, tag)
        partials = [_add_sibling(g.reshape(v[0]), r, self.c, name)
                    for g, v, r, (name, _, _) in zip(arrays, views, received, entries)]
        self.groups[tag] = (entries, partials)
        return _scatter_exchange(partials, entries)

    def received(self, tag, pieces):
        entries, partials = self.groups[tag]
        for p, r, (name, _, axis) in zip(partials, pieces, entries):
            self.halves[name] = _add_chips(p, r, self.mine, name, axis)

    def finish(self):
        names = [n for n, _, _ in BIG]
        own = [self.halves[n] for n in names]
        return dict(zip(names, zip(own, _share_with_sibling(own))))


def kernel(x, meta_tokens, ffn1_norm, ffn1_w_in, ffn1_w_out, mix_norm, w_in, b_forget, attn_sinks, w_branch_a, w_branch_b, w_out, ffn2_norm, ffn2_w_in, ffn2_w_out, final_norm, loss_target, m_meta_tokens, m_ffn1_norm, m_ffn1_w_in, m_ffn1_w_out, m_mix_norm, m_w_in, m_b_forget, m_attn_sinks, m_w_branch_a, m_w_branch_b, m_w_out, m_ffn2_norm, m_ffn2_w_in, m_ffn2_w_out, m_final_norm, v_meta_tokens, v_ffn1_norm, v_ffn1_w_in, v_ffn1_w_out, v_mix_norm, v_w_in, v_b_forget, v_attn_sinks, v_w_branch_a, v_w_branch_b, v_w_out, v_ffn2_norm, v_ffn2_w_in, v_ffn2_w_out, v_final_norm):
    given = dict(locals())
    names = ["meta_tokens", "ffn1_norm", "ffn1_w_in", "ffn1_w_out", "mix_norm", "w_in", "b_forget", "attn_sinks",
             "w_branch_a", "w_branch_b", "w_out", "ffn2_norm", "ffn2_w_in", "ffn2_w_out", "final_norm"]
    big_names = [n for n, _, _ in BIG]
    cx, cy, cc = _coords()
    c_arr = cc.reshape(1).astype(jnp.int32)
    mine_arr = (2 * cx + cy).reshape(1).astype(jnp.int32)

    comm = _Comm({n: given[n][0].astype(BF16) for n in big_names}, c_arr, mine_arr)
    table = [comm.by_name[n] + (BF16, True) for n in GATHER_FIRST] + [("meta_tokens", (N_META, D_MODEL), 1, F32, False)]
    first = _gather_exchange([comm.shards[n] for n in GATHER_FIRST] + [meta_tokens], table)
    w1i, w1o, meta_full = _run_exchange(first, "gather_first")
    norms = (ffn1_norm, mix_norm, ffn2_norm, final_norm.reshape(1, D_MODEL))
    loss, grad_x, small, big = _local_step(x, loss_target, meta_full, norms, b_forget, attn_sinks, (w1i, w1o), comm)

    last = comm.scatter("ffn1", dict(ffn1_w_in=big["ffn1_w_in"], ffn1_w_out=big["ffn1_w_out"]))
    comm.received("ffn1", _run_exchange(last, "scatter_chip_sums"))
    grad_halves = comm.finish()
    grads = {}

    pad_lanes = lambda a: jnp.concatenate([a, jnp.zeros((1, LANES - a.shape[1]), F32)], axis=1)
    buf = jnp.concatenate([
        small["meta_tokens"].reshape(128, LANES),
        small["ffn1_norm"].reshape(8, LANES), small["mix_norm"].reshape(8, LANES),
        small["ffn2_norm"].reshape(8, LANES), small["final_norm"].reshape(8, LANES),
        loss, pad_lanes(small["b_forget"]), pad_lanes(small["attn_sinks"]),
        jnp.zeros((SMALL_ROWS - 163, LANES), F32)], axis=0)
    red = _all_reduce_small(buf)
    meta_cols = red[:128].reshape(N_META, D_MODEL)
    grads["meta_tokens"] = lax.dynamic_slice_in_dim(meta_cols, (2 * cx + cy) * (D_MODEL // N_CHIPS),
                                                    D_MODEL // N_CHIPS, axis=1)
    grads["ffn1_norm"] = red[128:136].reshape(1, D_MODEL)
    grads["mix_norm"] = red[136:144].reshape(1, D_MODEL)
    grads["ffn2_norm"] = red[144:152].reshape(1, D_MODEL)
    grads["final_norm"] = red[152:160].reshape(1, D_MODEL)
    loss_out = red[160, 0]
    grads["b_forget"] = red[161:162, :B_HEADS]
    grads["attn_sinks"] = red[162:163, :A_HEADS]

    out_g, out_d, out_m, out_v = [], [], [], []
    for n in names:
        w_full = given[n]
        shape = w_full.shape
        two_d = (lambda a: a.reshape(shape[-2], shape[-1])) if len(shape) >= 2 else (lambda a: a.reshape(1, shape[0]))
        if n in grad_halves:
            own, other = grad_halves[n]
            g2, d2, m2, v2 = _adamw_halves(two_d(w_full), own, other, two_d(given["m_" + n]),
                                           two_d(given["v_" + n]), c_arr, n)
        else:
            g2 = two_d(grads[n])
            d2, m2, v2 = _adamw(two_d(w_full), g2, two_d(given["m_" + n]), two_d(given["v_" + n]))
        out_g.append(g2.reshape(shape))
        out_d.append(d2.reshape(shape))
        out_m.append(m2.reshape(shape))
        out_v.append(v2.reshape(shape))
    return (loss_out, grad_x, *out_g, *out_d, *out_m, *out_v)
```

```python
import jax
import jax.numpy as jnp
from jax import lax
from jax.experimental import pallas as pl
from jax.experimental.pallas import tpu as pltpu

F32 = jnp.float32
BF16 = jnp.bfloat16

D_MODEL = 1024
N_META = 16
BLOCK = 128
LANES = 128
PREFIX = BLOCK
N_PAD = PREFIX - N_META
HEAD_DIM = 64
A_HEADS = 8
A_KV_HEADS = 2
A_GROUP = 4
B_HEADS = 8
B_PAIRS = B_HEADS // 2
A_WIDTH = A_HEADS * HEAD_DIM
A_KV_WIDTH = A_KV_HEADS * HEAD_DIM
B_WIDTH = B_HEADS * HEAD_DIM
W_IN_COLS = A_WIDTH + 2 * A_KV_WIDTH + 3 * B_WIDTH + B_HEADS + 2 * D_MODEL
SRC_KA = A_WIDTH
SRC_VA = SRC_KA + A_KV_WIDTH
SRC_QB = SRC_VA + A_KV_WIDTH
SRC_KB = SRC_QB + B_WIDTH
SRC_VB = SRC_KB + B_WIDTH
SRC_F = SRC_VB + B_WIDTH
SRC_GA = SRC_F + B_HEADS
SRC_GB = SRC_GA + D_MODEL
A_PAD_WIDTH = A_HEADS * LANES
B_PAD_WIDTH = B_HEADS * LANES
F_COLS = LANES
OFF_QA = 0
OFF_KA = OFF_QA + A_PAD_WIDTH
OFF_VA = OFF_KA + A_KV_WIDTH
OFF_QB = OFF_VA + A_KV_WIDTH
OFF_KB = OFF_QB + B_WIDTH
OFF_VB = OFF_KB + B_PAD_WIDTH
OFF_GA = OFF_VB + B_PAD_WIDTH
OFF_GB = OFF_GA + D_MODEL
OFF_F = OFF_GB + D_MODEL
TAIL_COLS = 128
P_COLS = OFF_F + F_COLS + TAIL_COLS
EPS = 1e-6
NEG = -1e30
SCALE = HEAD_DIM ** -0.5
KEY_BLOCKS = 4

ADAM_LR = 0.001
ADAM_B1 = 0.9
ADAM_B2 = 0.999
ADAM_EPS = 1e-08
ADAM_WD = 0.01
ADAM_STEP = 10

N_CHIPS = 4
N_DEV = 8
VMEM_LIMIT = 56 * 1024 * 1024

NT_DIMS = (((1,), (1,)), ((), ()))
TN_DIMS = (((0,), (0,)), ((), ()))
MESH_ID = pl.DeviceIdType.MESH
HBM_SPEC = pl.BlockSpec(memory_space=pltpu.HBM)
VMEM_SPEC = pl.BlockSpec(memory_space=pltpu.VMEM)


def _tile(n, target, mult=16):
    best = None
    for t in range(mult, min(n, target) + 1, mult):
        if n % t == 0:
            best = t
    return best if best is not None else n


def _cparams(sem):
    return pltpu.CompilerParams(dimension_semantics=sem, vmem_limit_bytes=VMEM_LIMIT)


def _rms_scale(h):
    return lax.rsqrt(jnp.mean(h * h, axis=-1, keepdims=True) + EPS)


def _rms_bwd(dn, h, w):
    r = _rms_scale(h)
    dw = jnp.sum(dn * (h * r), axis=0, keepdims=True)
    z = dn * w
    dh = r * z - h * ((r * r * r) * jnp.mean(z * h, axis=-1, keepdims=True))
    return dh, dw


def _ffn_fwd(h, norm_w, w_in, w_out, exchange=None):
    t, d = h.shape
    f = w_out.shape[0]
    tm = _tile(t, 272)
    tc = _tile(f, 256, 128)
    nj = f // tc
    ni = t // tm
    n_x = len(exchange.ins) if exchange else 0

    def body(*refs):
        h_ref, nw_ref, wi_ref, wo_ref = refs[:4]
        hout_ref, g_ref, u_ref = refs[4 + n_x:7 + n_x]
        a_scr = refs[7 + 2 * n_x]
        i = pl.program_id(0)
        if exchange:
            _host_exchange(exchange, refs[4:4 + n_x], refs[7 + n_x:7 + 2 * n_x], refs[8 + 2 * n_x:],
                           i == 0, i == ni - 2, i == ni - 1)
        hh = h_ref[...]
        n = ((hh * _rms_scale(hh)) * nw_ref[...]).astype(BF16)
        for j in range(nj):
            cols = slice(j * tc, (j + 1) * tc)
            g = jnp.dot(n, wi_ref[:, j * tc:(j + 1) * tc], preferred_element_type=F32)
            u = jnp.dot(n, wi_ref[:, f + j * tc:f + (j + 1) * tc], preferred_element_type=F32)
            g_ref[:, cols] = g
            u_ref[:, cols] = u
            a_scr[:, cols] = ((g * jax.nn.sigmoid(g)) * u).astype(BF16)
        hout_ref[...] = hh + 0.5 * jnp.dot(a_scr[...], wo_ref[...], preferred_element_type=F32)

    resident = lambda a: pl.BlockSpec(a.shape, lambda i: (0, 0), pipeline_mode=pl.Buffered(1))
    row = lambda w: pl.BlockSpec((tm, w), lambda i: (i, 0))
    outs = pl.pallas_call(
        body,
        name="ffn_fwd",
        grid=(ni,),
        in_specs=[row(d), pl.BlockSpec((1, d), lambda i: (0, 0)), resident(w_in), resident(w_out)] + [HBM_SPEC] * n_x,
        out_specs=[row(d), row(f), row(f)] + [HBM_SPEC] * n_x,
        out_shape=[
            jax.ShapeDtypeStruct((t, d), F32),
            jax.ShapeDtypeStruct((t, f), F32),
            jax.ShapeDtypeStruct((t, f), F32),
        ] + (exchange.out_shape if exchange else []),
        scratch_shapes=[pltpu.VMEM((tm, f), BF16)] + (exchange.scratch if exchange else []),
        compiler_params=_cparams(("arbitrary",) if exchange else ("parallel",)),
    )(h, norm_w, w_in, w_out, *(exchange.ins if exchange else []))
    return outs[:3], outs[3:]


def _ffn_bwd(dh_out, h, norm_w, g, u, w_in, w_out, exchange=None):
    t, d = h.shape
    f = w_out.shape[0]
    tm = _tile(t, 544)
    tc = _tile(f, 256, 128)
    nj = f // tc
    ni = t // tm
    n_x = len(exchange.ins) if exchange else 0

    def body(*refs):
        dho_ref, h_ref, nw_ref, g_ref, u_ref, wg_ref, wu_ref, wo_ref = refs[:8]
        dhin_ref, n_ref, a_ref, dgu_ref, df_ref, dnw_ref = refs[8 + n_x:14 + n_x]
        dn_scr = refs[14 + 2 * n_x]
        i = pl.program_id(0)
        j = pl.program_id(1)
        if exchange:
            _host_exchange(exchange, refs[8:8 + n_x], refs[14 + n_x:14 + 2 * n_x], refs[15 + 2 * n_x:],
                           (i == 0) & (j == 0), (i == ni - 1) & (j == 0), (i == ni - 1) & (j == nj - 1))

        @pl.when(j == 0)
        def _():
            hh = h_ref[...]
            n_ref[...] = ((hh * _rms_scale(hh)) * nw_ref[...]).astype(BF16)
            df_ref[...] = (0.5 * dho_ref[...]).astype(BF16)
            dn_scr[...] = jnp.zeros_like(dn_scr)

        da = lax.dot_general(df_ref[...], wo_ref[...], NT_DIMS, preferred_element_type=F32)
        gg = g_ref[...]
        uu = u_ref[...]
        sig = jax.nn.sigmoid(gg)
        sl = gg * sig
        a_ref[...] = (sl * uu).astype(BF16)
        dg = ((da * uu) * (sig * (1.0 + gg * (1.0 - sig)))).astype(BF16)
        du = (da * sl).astype(BF16)
        dgu_ref[0] = dg
        dgu_ref[1] = du
        dn_scr[...] += (lax.dot_general(dg, wg_ref[...], NT_DIMS, preferred_element_type=F32)
                        + lax.dot_general(du, wu_ref[...], NT_DIMS, preferred_element_type=F32))

        @pl.when(j == nj - 1)
        def _():
            dh, dw = _rms_bwd(dn_scr[...], h_ref[...], nw_ref[...])
            dhin_ref[...] = dho_ref[...] + dh
            dnw_ref[0] = dw

    outs = pl.pallas_call(
        body,
        name="ffn_bwd",
        grid=(ni, nj),
        in_specs=[
            pl.BlockSpec((tm, d), lambda i, j: (i, 0)),
            pl.BlockSpec((tm, d), lambda i, j: (i, 0)),
            pl.BlockSpec((1, d), lambda i, j: (0, 0)),
            pl.BlockSpec((tm, tc), lambda i, j: (i, j)),
            pl.BlockSpec((tm, tc), lambda i, j: (i, j)),
            pl.BlockSpec((d, tc), lambda i, j: (0, j)),
            pl.BlockSpec((d, tc), lambda i, j: (0, j + nj)),
            pl.BlockSpec((tc, d), lambda i, j: (j, 0)),
        ] + [HBM_SPEC] * n_x,
        out_specs=[
            pl.BlockSpec((tm, d), lambda i, j: (i, 0)),
            pl.BlockSpec((tm, d), lambda i, j: (i, 0)),
            pl.BlockSpec((tm, tc), lambda i, j: (i, j)),
            pl.BlockSpec((2, tm, tc), lambda i, j: (0, i, j)),
            pl.BlockSpec((tm, d), lambda i, j: (i, 0)),
            pl.BlockSpec((1, 1, d), lambda i, j: (i, 0, 0)),
        ] + [HBM_SPEC] * n_x,
        out_shape=[
            jax.ShapeDtypeStruct((t, d), F32),
            jax.ShapeDtypeStruct((t, d), BF16),
            jax.ShapeDtypeStruct((t, f), BF16),
            jax.ShapeDtypeStruct((2, t, f), BF16),
            jax.ShapeDtypeStruct((t, d), BF16),
            jax.ShapeDtypeStruct((ni, 1, d), F32),
        ] + (exchange.out_shape if exchange else []),
        scratch_shapes=[pltpu.VMEM((tm, d), F32)] + (exchange.scratch if exchange else []),
        compiler_params=_cparams(("arbitrary", "arbitrary") if exchange else ("parallel", "arbitrary")),
    )(dh_out, h, norm_w, g, u, w_in, w_in, w_out, *(exchange.ins if exchange else []))
    return outs[:6], outs[6:]


def _tn_matmul(a, b, name):
    t, k = a.shape
    split = b.ndim == 3
    n = 2 * b.shape[2] if split else b.shape[1]
    tk = _tile(k, 512, 128)
    tn = _tile(b.shape[-1], 1408, 128)
    per_half = b.shape[-1] // tn

    def body(a_ref, b_ref, o_ref):
        o_ref[...] = lax.dot_general(a_ref[...], b_ref[...], TN_DIMS, preferred_element_type=F32)

    if split:
        b_spec = pl.BlockSpec((None, t, tn), lambda i, j: (j // per_half, 0, j % per_half))
    else:
        b_spec = pl.BlockSpec((t, tn), lambda i, j: (0, j))
    return pl.pallas_call(
        body,
        name=name,
        grid=(k // tk, n // tn),
        in_specs=[pl.BlockSpec((t, tk), lambda i, j: (0, i)), b_spec],
        out_specs=pl.BlockSpec((tk, tn), lambda i, j: (i, j)),
        out_shape=jax.ShapeDtypeStruct((k, n), F32),
        compiler_params=_cparams(("parallel", "parallel")),
    )(a, b)


PROJ_PARTS = (
    (OFF_QA, A_PAD_WIDTH, True), (OFF_KA, A_KV_WIDTH, True), (OFF_VA, A_KV_WIDTH, True),
    (OFF_QB, B_WIDTH, True), (OFF_KB, B_PAD_WIDTH, True), (OFF_VB, B_PAD_WIDTH, True),
    (OFF_GA, D_MODEL, False), (OFF_GB, D_MODEL, False), (OFF_F, F_COLS, False),
)


def _proj_fwd(h, norm_w, w_p):
    t, d = h.shape
    tm = _tile(t, 272)

    def body(h_ref, nw_ref, w_ref, u_ref, *part_refs):
        hh = h_ref[...]
        un = ((hh * _rms_scale(hh)) * nw_ref[...]).astype(BF16)
        u_ref[...] = un
        for (off, width, _), p_ref in zip(PROJ_PARTS, part_refs):
            p_ref[...] = jnp.dot(un, w_ref[:, off:off + width], preferred_element_type=F32).astype(p_ref.dtype)

    row = lambda w: pl.BlockSpec((tm, w), lambda i: (i, 0))
    return pl.pallas_call(
        body,
        name="proj_fwd",
        grid=(t // tm,),
        in_specs=[row(d), pl.BlockSpec((1, d), lambda i: (0, 0)), pl.BlockSpec(w_p.shape, lambda i: (0, 0))],
        out_specs=[row(d)] + [row(width) for _, width, _ in PROJ_PARTS],
        out_shape=[jax.ShapeDtypeStruct((t, d), BF16)]
        + [jax.ShapeDtypeStruct((t, width), BF16 if is_bf else F32) for _, width, is_bf in PROJ_PARTS],
        compiler_params=_cparams(("parallel",)),
    )(h, norm_w, w_p)


def _proj_bwd(dh_out, h, norm_w, dproj, w_p):
    t, d = h.shape
    n = w_p.shape[1]
    tm = _tile(t, 272)
    ni = t // tm

    def body(dho_ref, h_ref, nw_ref, dp_ref, w_ref, dhin_ref, dnw_ref):
        dn = lax.dot_general(dp_ref[...], w_ref[...], NT_DIMS, preferred_element_type=F32)
        dh, dw = _rms_bwd(dn, h_ref[...], nw_ref[...])
        dhin_ref[...] = dho_ref[...] + dh
        dnw_ref[0] = dw

    row = lambda w: pl.BlockSpec((tm, w), lambda i: (i, 0))
    return pl.pallas_call(
        body,
        name="proj_bwd",
        grid=(ni,),
        in_specs=[row(d), row(d), pl.BlockSpec((1, d), lambda i: (0, 0)), row(n),
                  pl.BlockSpec(w_p.shape, lambda i: (0, 0), pipeline_mode=pl.Buffered(1))],
        out_specs=[row(d), pl.BlockSpec((1, 1, d), lambda i: (i, 0, 0))],
        out_shape=[jax.ShapeDtypeStruct((t, d), F32), jax.ShapeDtypeStruct((ni, 1, d), F32)],
        compiler_params=_cparams(("parallel",)),
    )(dh_out, h, norm_w, dproj, w_p)


def _merge_fwd(h, oa, ob, ga, gb, wa, wb, wo):
    t, d = h.shape
    tm = _tile(t, 544)

    def body(h_ref, oa_ref, ob_ref, ga_ref, gb_ref, wa_ref, wb_ref, wo_ref, hout_ref, mix_ref):
        ya = jnp.dot(oa_ref[...], wa_ref[...], preferred_element_type=F32)
        yb = jnp.dot(ob_ref[...], wb_ref[...], preferred_element_type=F32)
        mixed = (jax.nn.sigmoid(ga_ref[...]) * ya + jax.nn.sigmoid(gb_ref[...]) * yb).astype(BF16)
        mix_ref[...] = mixed
        hout_ref[...] = h_ref[...] + jnp.dot(mixed, wo_ref[...], preferred_element_type=F32)

    row = lambda w: pl.BlockSpec((tm, w), lambda i: (i, 0))
    full = lambda a: pl.BlockSpec(a.shape, lambda i: (0, 0))
    return pl.pallas_call(
        body,
        name="merge_fwd",
        grid=(t // tm,),
        in_specs=[row(d), row(oa.shape[1]), row(ob.shape[1]), row(d), row(d), full(wa), full(wb), full(wo)],
        out_specs=[row(d), row(d)],
        out_shape=[jax.ShapeDtypeStruct((t, d), F32), jax.ShapeDtypeStruct((t, d), BF16)],
        compiler_params=_cparams(("parallel",)),
    )(h, oa, ob, ga, gb, wa, wb, wo)


def _merge_bwd(dh, oa, ob, ga, gb, wa, wb, wo):
    t, d = dh.shape
    tm = _tile(t, 544)

    def body(dh_ref, oa_ref, ob_ref, ga_ref, gb_ref, wa_ref, wb_ref, wo_ref,
             dya_ref, dyb_ref, doa_ref, dob_ref, dga_ref, dgb_ref, dhb_ref):
        dhb = dh_ref[...].astype(BF16)
        dhb_ref[...] = dhb
        dmix = lax.dot_general(dhb, wo_ref[...], NT_DIMS, preferred_element_type=F32)
        for o_ref, g_ref, w_ref, dy_ref, do_ref, dg_ref in (
                (oa_ref, ga_ref, wa_ref, dya_ref, doa_ref, dga_ref),
                (ob_ref, gb_ref, wb_ref, dyb_ref, dob_ref, dgb_ref)):
            y = jnp.dot(o_ref[...], w_ref[...], preferred_element_type=F32)
            s = jax.nn.sigmoid(g_ref[...])
            dy = (dmix * s).astype(BF16)
            dy_ref[...] = dy
            dg_ref[...] = ((dmix * y) * (s * (1.0 - s))).astype(BF16)
            do_ref[...] = lax.dot_general(dy, w_ref[...], NT_DIMS, preferred_element_type=F32).astype(BF16)

    row = lambda w: pl.BlockSpec((tm, w), lambda i: (i, 0))
    full = lambda a: pl.BlockSpec(a.shape, lambda i: (0, 0))
    wa_w, wb_w = oa.shape[1], ob.shape[1]
    return pl.pallas_call(
        body,
        name="merge_bwd",
        grid=(t // tm,),
        in_specs=[row(d), row(wa_w), row(wb_w), row(d), row(d), full(wa), full(wb), full(wo)],
        out_specs=[row(d), row(d), row(wa_w), row(wb_w), row(d), row(d), row(d)],
        out_shape=[
            jax.ShapeDtypeStruct((t, d), BF16), jax.ShapeDtypeStruct((t, d), BF16),
            jax.ShapeDtypeStruct((t, wa_w), BF16), jax.ShapeDtypeStruct((t, wb_w), BF16),
            jax.ShapeDtypeStruct((t, d), BF16), jax.ShapeDtypeStruct((t, d), BF16),
            jax.ShapeDtypeStruct((t, d), BF16),
        ],
        compiler_params=_cparams(("parallel",)),
    )(dh, oa, ob, ga, gb, wa, wb, wo)


def _tri_dot(tri, x):
    hi = x.astype(BF16)
    r1 = x - hi.astype(F32)
    mid = r1.astype(BF16)
    lo = (r1 - mid.astype(F32)).astype(BF16)
    return (jnp.dot(tri, hi, preferred_element_type=F32)
            + jnp.dot(tri, mid, preferred_element_type=F32)
            + jnp.dot(tri, lo, preferred_element_type=F32))


def _forget_cumsum(f_logit, b_pad, nb):
    t, w = f_logit.shape
    bsz = t // (nb * BLOCK)

    def body(f_ref, b_ref, c_ref, carry):
        n = pl.program_id(1)

        @pl.when(n == 0)
        def _():
            carry[...] = jnp.zeros_like(carry)

        x = jax.nn.log_sigmoid(f_ref[...] + b_ref[...])
        rows = lax.broadcasted_iota(jnp.int32, (BLOCK, BLOCK), 0)
        cols = lax.broadcasted_iota(jnp.int32, (BLOCK, BLOCK), 1)
        tri = (cols <= rows).astype(BF16)
        c = _tri_dot(tri, x) + carry[...]
        c_ref[...] = c
        carry[...] = c[BLOCK - 1:BLOCK, :]

    return pl.pallas_call(
        body,
        name="forget_cumsum",
        grid=(bsz, nb),
        in_specs=[pl.BlockSpec((BLOCK, w), lambda b, n: (b * nb + n, 0)),
                  pl.BlockSpec((1, w), lambda b, n: (0, 0))],
        out_specs=pl.BlockSpec((BLOCK, w), lambda b, n: (b * nb + n, 0)),
        out_shape=jax.ShapeDtypeStruct((t, w), F32),
        scratch_shapes=[pltpu.VMEM((1, w), F32)],
        compiler_params=_cparams(("parallel", "arbitrary")),
    )(f_logit, b_pad)


def _forget_cumsum_bwd(dc, f_logit, b_pad, nb):
    t, w = f_logit.shape
    bsz = t // (nb * BLOCK)

    def body(dc_ref, f_ref, b_ref, df_ref, db_ref, carry):
        n = pl.program_id(1)

        @pl.when(n == 0)
        def _():
            carry[...] = jnp.zeros_like(carry)
            db_ref[...] = jnp.zeros_like(db_ref)

        rows = lax.broadcasted_iota(jnp.int32, (BLOCK, BLOCK), 0)
        cols = lax.broadcasted_iota(jnp.int32, (BLOCK, BLOCK), 1)
        tri = (cols >= rows).astype(BF16)
        dlf = _tri_dot(tri, dc_ref[...]) + carry[...]
        carry[...] = dlf[0:1, :]
        df = dlf * jax.nn.sigmoid(-(f_ref[...] + b_ref[...]))
        df_ref[...] = df.astype(BF16)
        db_ref[0] += jnp.sum(df, axis=0, keepdims=True)

    rev = lambda b, n: (b * nb + (nb - 1 - n), 0)
    return pl.pallas_call(
        body,
        name="forget_cumsum_bwd",
        grid=(bsz, nb),
        in_specs=[pl.BlockSpec((BLOCK, w), rev),
                  pl.BlockSpec((BLOCK, w), rev),
                  pl.BlockSpec((1, w), lambda b, n: (0, 0))],
        out_specs=[pl.BlockSpec((BLOCK, w), rev),
                   pl.BlockSpec((1, 1, w), lambda b, n: (b, 0, 0))],
        out_shape=[jax.ShapeDtypeStruct((t, w), BF16), jax.ShapeDtypeStruct((bsz, 1, w), F32)],
        scratch_shapes=[pltpu.VMEM((1, w), F32)],
        compiler_params=_cparams(("parallel", "arbitrary")),
    )(dc, f_logit, b_pad)


GROUP_ROWS = A_GROUP * BLOCK


def _stack_heads(ref):
    return jnp.concatenate([ref[:, i * LANES:(i + 1) * LANES] for i in range(A_GROUP)], axis=0)


def _unstack_heads(ref, x):
    for i in range(A_GROUP):
        ref[:, i * LANES:(i + 1) * LANES] = x[i * BLOCK:(i + 1) * BLOCK].astype(ref.dtype)


def _swa_logits(q, km, kp, kc, slope, n):
    qi = lax.broadcasted_iota(jnp.int32, (GROUP_ROWS, BLOCK), 0) & (BLOCK - 1)
    kj = lax.broadcasted_iota(jnp.int32, (GROUP_ROWS, BLOCK), 1)
    out = []
    for kk, dist, ok in (
            (km, n * BLOCK + qi - kj, (kj >= N_PAD) & (n * BLOCK + qi - kj >= 0)),
            (kp, BLOCK + qi - kj, (kj > qi) & (n >= 2)),
            (kc, qi - kj, (kj <= qi) & (n >= 1))):
        s = lax.dot_general(q, kk, NT_DIMS, preferred_element_type=F32) * SCALE
        s = s - slope * dist.astype(F32)
        out.append(jnp.where(ok, s, NEG))
    return out


def _swa_specs(nb):
    row = lambda b, n: b * nb + n
    qspec = pl.BlockSpec((BLOCK, A_GROUP * LANES), lambda b, g, n: (row(b, n), g))
    kv_m = pl.BlockSpec((BLOCK, LANES), lambda b, g, n: (row(b, 0), 0))
    kv_p = pl.BlockSpec((BLOCK, LANES), lambda b, g, n: (row(b, jnp.maximum(n - 1, 0)), 0))
    kv_c = pl.BlockSpec((BLOCK, LANES), lambda b, g, n: (row(b, n), 0))
    rowspec = pl.BlockSpec((1, GROUP_ROWS, 1), lambda b, g, n: (g, 0, 0))
    lsespec = pl.BlockSpec((1, 1, GROUP_ROWS, 1), lambda b, g, n: (row(b, n), g, 0, 0))
    return qspec, kv_m, kv_p, kv_c, rowspec, lsespec


def _swa_fwd(q, k, v, sink_rows, slope_rows, nb):
    t = q.shape[0]
    bsz = t // (nb * BLOCK)

    def body(q_ref, km_ref, kp_ref, kc_ref, vm_ref, vp_ref, vc_ref, sink_ref, slope_ref, o_ref, lse_ref):
        g = pl.program_id(1)
        n = pl.program_id(2)
        qq = _stack_heads(q_ref)
        sink = sink_ref[0]
        s_m, s_p, s_c = _swa_logits(qq, km_ref[...], kp_ref[...], kc_ref[...], slope_ref[0], n)
        m = jnp.maximum(jnp.maximum(jnp.max(s_m, axis=-1, keepdims=True), jnp.max(s_p, axis=-1, keepdims=True)),
                        jnp.maximum(jnp.max(s_c, axis=-1, keepdims=True), sink))
        e_m = jnp.exp(s_m - m)
        e_p = jnp.exp(s_p - m)
        e_c = jnp.exp(s_c - m)
        z = (jnp.sum(e_m, axis=-1, keepdims=True) + jnp.sum(e_p, axis=-1, keepdims=True)
             + jnp.sum(e_c, axis=-1, keepdims=True) + jnp.exp(sink - m))
        inv = 1.0 / z
        o = (jnp.dot((e_m * inv).astype(BF16), vm_ref[...], preferred_element_type=F32)
             + jnp.dot((e_p * inv).astype(BF16), vp_ref[...], preferred_element_type=F32)
             + jnp.dot((e_c * inv).astype(BF16), vc_ref[...], preferred_element_type=F32))
        lane_group = lax.broadcasted_iota(jnp.int32, (GROUP_ROWS, LANES), 1) // HEAD_DIM
        _unstack_heads(o_ref, jnp.where(lane_group == g, o, 0.0))
        lse_ref[0, 0] = m + jnp.log(z)

    qspec, kv_m, kv_p, kv_c, rowspec, lsespec = _swa_specs(nb)
    return pl.pallas_call(
        body,
        name="swa_fwd",
        grid=(bsz, A_KV_HEADS, nb),
        in_specs=[qspec, kv_m, kv_p, kv_c, kv_m, kv_p, kv_c, rowspec, rowspec],
        out_specs=[qspec, lsespec],
        out_shape=[jax.ShapeDtypeStruct((t, A_PAD_WIDTH), BF16),
                   jax.ShapeDtypeStruct((t // BLOCK, A_KV_HEADS, GROUP_ROWS, 1), F32)],
        compiler_params=_cparams(("parallel", "parallel", "arbitrary")),
    )(q, k, k, k, v, v, v, sink_rows, slope_rows)


def _swa_bwd(q, k, v, do, lse, sink_rows, slope_rows, nb):
    t = q.shape[0]
    l = nb * BLOCK
    bsz = t // l

    def body(q_ref, km_ref, kp_ref, kc_ref, vm_ref, vp_ref, vc_ref, do_ref, lse_ref, sink_ref, slope_ref,
             dq_ref, dk_ref, dv_ref, dsink_ref, dk_acc, dv_acc):
        g = pl.program_id(1)
        n = pl.program_id(2)

        @pl.when((g == 0) & (n == 0))
        def _():
            dk_acc[...] = jnp.zeros_like(dk_acc)
            dv_acc[...] = jnp.zeros_like(dv_acc)

        @pl.when(n == 0)
        def _():
            dsink_ref[...] = jnp.zeros_like(dsink_ref)

        qq = _stack_heads(q_ref)
        dob = _stack_heads(do_ref)
        lse = lse_ref[0, 0]
        logits = _swa_logits(qq, km_ref[...], kp_ref[...], kc_ref[...], slope_ref[0], n)
        probs = [jnp.exp(s - lse) for s in logits]
        dps = [lax.dot_general(dob, v_ref[...], NT_DIMS, preferred_element_type=F32)
               for v_ref in (vm_ref, vp_ref, vc_ref)]
        delta = sum(jnp.sum(p * dp, axis=-1, keepdims=True) for p, dp in zip(probs, dps))
        prev = jnp.maximum(n - 1, 0)
        dq = jnp.zeros((GROUP_ROWS, LANES), F32)
        for p, dp, k_ref, start in ((probs[0], dps[0], km_ref, 0),
                                    (probs[1], dps[1], kp_ref, prev * BLOCK),
                                    (probs[2], dps[2], kc_ref, n * BLOCK)):
            ds = (p * (dp - delta)).astype(BF16)
            dq = dq + jnp.dot(ds, k_ref[...], preferred_element_type=F32)
            rows = pl.ds(pl.multiple_of(start, BLOCK), BLOCK)
            dk_acc[rows, :] += lax.dot_general(ds, qq, TN_DIMS, preferred_element_type=F32) * SCALE
            dv_acc[rows, :] += lax.dot_general(p.astype(BF16), dob, TN_DIMS, preferred_element_type=F32)
        _unstack_heads(dq_ref, dq * SCALE)
        dsink_ref[0, 0] += -(jnp.exp(sink_ref[0] - lse) * delta)

        @pl.when((g == A_KV_HEADS - 1) & (n == nb - 1))
        def _():
            dk_ref[...] = dk_acc[...].astype(BF16)
            dv_ref[...] = dv_acc[...].astype(BF16)

    qspec, kv_m, kv_p, kv_c, rowspec, lsespec = _swa_specs(nb)
    kv_all = pl.BlockSpec((l, LANES), lambda b, g, n: (b, 0))
    return pl.pallas_call(
        body,
        name="swa_bwd",
        grid=(bsz, A_KV_HEADS, nb),
        in_specs=[qspec, kv_m, kv_p, kv_c, kv_m, kv_p, kv_c, qspec, lsespec, rowspec, rowspec],
        out_specs=[qspec, kv_all, kv_all,
                   pl.BlockSpec((1, 1, GROUP_ROWS, 1), lambda b, g, n: (b, g, 0, 0))],
        out_shape=[jax.ShapeDtypeStruct((t, A_PAD_WIDTH), BF16),
                   jax.ShapeDtypeStruct((t, LANES), BF16),
                   jax.ShapeDtypeStruct((t, LANES), BF16),
                   jax.ShapeDtypeStruct((bsz, A_KV_HEADS, GROUP_ROWS, 1), F32)],
        scratch_shapes=[pltpu.VMEM((l, LANES), F32), pltpu.VMEM((l, LANES), F32)],
        compiler_params=_cparams(("parallel", "arbitrary", "arbitrary")),
    )(q, k, k, k, v, v, v, do, lse, sink_rows, slope_rows)


CHUNK = KEY_BLOCKS * BLOCK


def _fox_chunk(qb, ci):
    sb = jnp.maximum(jnp.minimum(KEY_BLOCKS * ci, qb + 1 - KEY_BLOCKS), 0)
    lo = jnp.maximum(ci * CHUNK, N_PAD)
    return sb, lo, pl.ds(pl.multiple_of(sb * BLOCK, BLOCK), CHUNK)


def _fox_logits(s_ref, cr_ref, e, j, sb, lo, qb):
    q_pos = qb * BLOCK + lax.broadcasted_iota(jnp.int32, (BLOCK, BLOCK), 0)
    k_pos = (sb + j) * BLOCK + lax.broadcasted_iota(jnp.int32, (BLOCK, BLOCK), 1)
    s = s_ref[e, :, j * BLOCK:(j + 1) * BLOCK] - cr_ref[e, sb + j]
    return jnp.where((k_pos <= q_pos) & (k_pos >= lo), s, NEG)


def _fox_specs(nb):
    l = nb * BLOCK
    q_spec = pl.BlockSpec((BLOCK, LANES), lambda b, p, i: (b * nb + i, p))
    kv_spec = pl.BlockSpec((l, 2 * LANES), lambda b, p, i: (b, p))
    cc_spec = pl.BlockSpec((2, BLOCK, 1), lambda b, p, i: (b * B_PAIRS + p, i, 0))
    cr_spec = pl.BlockSpec((2, nb, 1, BLOCK), lambda b, p, i: (b * B_PAIRS + p, 0, 0, 0))
    return q_spec, kv_spec, cc_spec, cr_spec


def _fox_fwd(q, k, v, c_row, nb, exchange=None):
    t = q.shape[0]
    bsz = t // (nb * BLOCK)
    assert nb >= KEY_BLOCKS

    n_x = len(exchange.ins) if exchange else 0

    def body(*refs):
        q_ref, k_ref, v_ref, cr_ref = refs[:4]
        o_ref, ox_ref, lse_ref = refs[4 + n_x:7 + n_x]
        s_scr, hi_scr, lo_scr = refs[7 + 2 * n_x:10 + 2 * n_x]
        qb = pl.program_id(2)
        if exchange:
            first = (pl.program_id(0) == 0) & (pl.program_id(1) == 0)
            last = (pl.program_id(0) == bsz - 1) & (pl.program_id(1) == B_PAIRS - 1)
            _host_exchange(exchange, refs[4:4 + n_x], refs[7 + n_x:7 + 2 * n_x], refs[10 + 2 * n_x:],
                           first & (qb == 0), last & (qb == 0), last & (qb == nb - 1))
        qs = q_ref[...] * SCALE
        first_half = lax.broadcasted_iota(jnp.int32, (BLOCK, LANES), 1) < HEAD_DIM

        def step(ci, carry):
            stats, acc, acc_lo = carry[:4], carry[4], carry[5]
            sb, lo, krows = _fox_chunk(qb, ci)
            new_stats, alphas = [], []
            pv = jnp.zeros((BLOCK, LANES), F32)
            pv_lo = jnp.zeros((BLOCK, LANES), F32)
            for e in range(2):
                m, z = stats[2 * e], stats[2 * e + 1]
                tile = slice(e * LANES, (e + 1) * LANES)
                s_scr[e] = lax.dot_general(qs, k_ref[krows, tile], NT_DIMS, preferred_element_type=F32)
                m_new = m
                for j in range(KEY_BLOCKS):
                    s = _fox_logits(s_scr, cr_ref, e, j, sb, lo, qb)
                    s_scr[e, :, j * BLOCK:(j + 1) * BLOCK] = s
                    m_new = jnp.maximum(m_new, jnp.max(s, axis=-1, keepdims=True))
                alpha = jnp.exp(m - m_new)
                z = alpha * z
                for j in range(KEY_BLOCKS):
                    cols = slice(j * BLOCK, (j + 1) * BLOCK)
                    p = jnp.exp(s_scr[e, :, cols] - m_new)
                    z = z + jnp.sum(p, axis=-1, keepdims=True)
                    hi = p.astype(BF16)
                    hi_scr[e, :, cols] = hi
                    lo_scr[e, :, cols] = (p - hi.astype(F32)).astype(BF16)
                vv = v_ref[krows, tile]
                pv = pv + jnp.dot(hi_scr[e], vv, preferred_element_type=F32)
                pv_lo = pv_lo + jnp.dot(lo_scr[e], vv, preferred_element_type=F32)
                new_stats += [m_new, z]
                alphas.append(alpha)
            alpha = jnp.where(first_half, alphas[0], alphas[1])
            return (*new_stats, alpha * acc + pv, alpha * acc_lo + pv_lo)

        col = lambda val: jnp.full((BLOCK, 1), val, F32)
        m0, z0, m1, z1, acc, acc_lo = lax.fori_loop(
            0, (qb + KEY_BLOCKS) // KEY_BLOCKS, step,
            (col(NEG), col(0.0), col(NEG), col(0.0), jnp.zeros((BLOCK, LANES), F32), jnp.zeros((BLOCK, LANES), F32)))
        inv = 1.0 / jnp.where(first_half, z0, z1)
        o_ref[...] = (acc * inv).astype(BF16)
        ox_ref[...] = (acc + acc_lo) * inv
        lse_ref[0] = m0 + jnp.log(z0)
        lse_ref[1] = m1 + jnp.log(z1)

    q_spec, kv_spec, cc_spec, cr_spec = _fox_specs(nb)
    outs = pl.pallas_call(
        body,
        name="fox_fwd",
        grid=(bsz, B_PAIRS, nb),
        in_specs=[q_spec, kv_spec, kv_spec, cr_spec] + [HBM_SPEC] * n_x,
        out_specs=[q_spec, q_spec, cc_spec] + [HBM_SPEC] * n_x,
        out_shape=[jax.ShapeDtypeStruct((t, B_WIDTH), BF16), jax.ShapeDtypeStruct((t, B_WIDTH), F32),
                   jax.ShapeDtypeStruct((bsz * B_HEADS, nb * BLOCK, 1), F32)] + (exchange.out_shape if exchange else []),
        scratch_shapes=[pltpu.VMEM((2, BLOCK, CHUNK), F32), pltpu.VMEM((2, BLOCK, CHUNK), BF16),
                        pltpu.VMEM((2, BLOCK, CHUNK), BF16)] + (exchange.scratch if exchange else []),
        compiler_params=_cparams(("arbitrary",) * 3 if exchange else ("parallel", "parallel", "arbitrary")),
    )(q, k, v, c_row, *(exchange.ins if exchange else []))
    return outs[:3], outs[3:]


def _fox_bwd(q, k, v, o_exact, do, lse, c_row, nb, exchange=None):
    t = q.shape[0]
    l = nb * BLOCK
    bsz = t // l

    n_x = len(exchange.ins) if exchange else 0

    def body(*refs):
        q_ref, k_ref, v_ref, ox_ref, do_ref, lse_ref, cr_ref = refs[:7]
        dq_ref, dk_ref, dv_ref, dc_ref = refs[7 + n_x:11 + n_x]
        dk_acc, dv_acc, s_scr, dp_scr, p_scr, ds_scr = refs[11 + 2 * n_x:17 + 2 * n_x]
        qb = pl.program_id(2)
        if exchange:
            first = (pl.program_id(0) == 0) & (pl.program_id(1) == 0)
            last = (pl.program_id(0) == bsz - 1) & (pl.program_id(1) == B_PAIRS - 1)
            _host_exchange(exchange, refs[7:7 + n_x], refs[11 + n_x:11 + 2 * n_x], refs[17 + 2 * n_x:],
                           first & (qb == 0), last & (qb == 0), last & (qb == nb - 1))

        @pl.when(qb == 0)
        def _():
            dk_acc[...] = jnp.zeros_like(dk_acc)
            dv_acc[...] = jnp.zeros_like(dv_acc)
            dc_ref[...] = jnp.zeros_like(dc_ref)

        qs = q_ref[...] * SCALE
        dob = do_ref[...]
        qs_t = qs.T
        dob_t = dob.T
        first_half = lax.broadcasted_iota(jnp.int32, (BLOCK, LANES), 1) < HEAD_DIM
        weighted = dob.astype(F32) * ox_ref[...]
        deltas = (jnp.sum(jnp.where(first_half, weighted, 0.0), axis=-1, keepdims=True),
                  jnp.sum(jnp.where(first_half, 0.0, weighted), axis=-1, keepdims=True))

        def step(ci, dq):
            sb, lo, krows = _fox_chunk(qb, ci)
            for e in range(2):
                tile = slice(e * LANES, (e + 1) * LANES)
                kk = k_ref[krows, tile]
                s_scr[e] = lax.dot_general(qs, kk, NT_DIMS, preferred_element_type=F32)
                dp_scr[e] = lax.dot_general(dob, v_ref[krows, tile], NT_DIMS, preferred_element_type=F32)
                lse_e = lse_ref[e]
                for j in range(KEY_BLOCKS):
                    cols = slice(j * BLOCK, (j + 1) * BLOCK)
                    p = jnp.exp(_fox_logits(s_scr, cr_ref, e, j, sb, lo, qb) - lse_e)
                    ds = p * (dp_scr[e, :, cols] - deltas[e])
                    dc_ref[e, sb + j] -= jnp.sum(ds, axis=0, keepdims=True)
                    p_scr[e, :, cols] = p.astype(BF16)
                    ds_scr[e, :, cols] = ds.astype(BF16)
                dsb = ds_scr[e]
                dq = dq + jnp.dot(dsb, kk, preferred_element_type=F32)
                dk_t = jnp.dot(qs_t, dsb, preferred_element_type=F32)
                dv_t = jnp.dot(dob_t, p_scr[e], preferred_element_type=F32)
                for j in range(KEY_BLOCKS):
                    cols = slice(j * BLOCK, (j + 1) * BLOCK)
                    dk_acc[sb + j, tile, :] += dk_t[:, cols]
                    dv_acc[sb + j, tile, :] += dv_t[:, cols]
            return dq

        dq = lax.fori_loop(0, (qb + KEY_BLOCKS) // KEY_BLOCKS, step, jnp.zeros((BLOCK, LANES), F32))
        dq_ref[...] = (dq * SCALE).astype(BF16)

        @pl.when(qb == nb - 1)
        def _():
            for kb in range(nb):
                rows = slice(kb * BLOCK, (kb + 1) * BLOCK)
                for e in range(2):
                    tile = slice(e * LANES, (e + 1) * LANES)
                    dk_ref[rows, tile] = dk_acc[kb, tile, :].T.astype(BF16)
                    dv_ref[rows, tile] = dv_acc[kb, tile, :].T.astype(BF16)

    q_spec, kv_spec, cc_spec, cr_spec = _fox_specs(nb)
    outs = pl.pallas_call(
        body,
        name="fox_bwd",
        grid=(bsz, B_PAIRS, nb),
        in_specs=[q_spec, kv_spec, kv_spec, q_spec, q_spec, cc_spec, cr_spec] + [HBM_SPEC] * n_x,
        out_specs=[q_spec, kv_spec, kv_spec, cr_spec] + [HBM_SPEC] * n_x,
        out_shape=[jax.ShapeDtypeStruct((t, B_WIDTH), BF16), jax.ShapeDtypeStruct((t, B_PAD_WIDTH), BF16),
                   jax.ShapeDtypeStruct((t, B_PAD_WIDTH), BF16),
                   jax.ShapeDtypeStruct((bsz * B_HEADS, nb, 1, BLOCK), F32)] + (exchange.out_shape if exchange else []),
        scratch_shapes=[pltpu.VMEM((nb, 2 * LANES, BLOCK), F32), pltpu.VMEM((nb, 2 * LANES, BLOCK), F32),
                        pltpu.VMEM((2, BLOCK, CHUNK), F32), pltpu.VMEM((2, BLOCK, CHUNK), F32),
                        pltpu.VMEM((2, BLOCK, CHUNK), BF16), pltpu.VMEM((2, BLOCK, CHUNK), BF16)]
        + (exchange.scratch if exchange else []),
        compiler_params=_cparams(("arbitrary",) * 3 if exchange else ("parallel", "parallel", "arbitrary")),
    )(q, k, v, o_exact, do, lse, c_row, *(exchange.ins if exchange else []))
    return outs[:4], outs[4:]


def _loss_head(h, final_w, target):
    bsz, l, d = h.shape
    nb = l // BLOCK

    def body(h_ref, w_ref, t_ref, loss_ref, dh_ref, dw_ref):
        b = pl.program_id(0)
        n = pl.program_id(1)

        @pl.when((b == 0) & (n == 0))
        def _():
            loss_ref[...] = jnp.zeros_like(loss_ref)
            dw_ref[...] = jnp.zeros_like(dw_ref)

        @pl.when(n == 0)
        def _():
            dh_ref[...] = jnp.zeros_like(dh_ref)

        @pl.when(n > 0)
        def _():
            hh = h_ref[0]
            w = w_ref[...]
            r = _rms_scale(hh)
            err = (hh * r) * w - t_ref[0]
            loss_ref[...] += 0.5 * jnp.sum(jnp.mean(err * err, axis=-1, keepdims=True), axis=0, keepdims=True)
            dy = err * (1.0 / d)
            dh, dw = _rms_bwd(dy, hh, w)
            dh_ref[0] = dh
            dw_ref[...] += dw

    return pl.pallas_call(
        body,
        name="loss_head",
        grid=(bsz, nb),
        in_specs=[
            pl.BlockSpec((1, BLOCK, d), lambda b, n: (b, n, 0)),
            pl.BlockSpec((1, d), lambda b, n: (0, 0)),
            pl.BlockSpec((1, BLOCK, d), lambda b, n: (b, jnp.maximum(n - 1, 0), 0)),
        ],
        out_specs=[
            pl.BlockSpec((1, 128), lambda b, n: (0, 0)),
            pl.BlockSpec((1, BLOCK, d), lambda b, n: (b, n, 0)),
            pl.BlockSpec((1, d), lambda b, n: (0, 0)),
        ],
        out_shape=[jax.ShapeDtypeStruct((1, 128), F32), jax.ShapeDtypeStruct((bsz, l, d), F32),
                   jax.ShapeDtypeStruct((1, d), F32)],
        compiler_params=_cparams(("arbitrary", "arbitrary")),
    )(h, final_w, target)


def _pad_tiles(w, src, heads, lane_slot, axis):
    pieces = []
    for h in range(heads):
        x = lax.slice_in_dim(w, src + HEAD_DIM * h, src + HEAD_DIM * (h + 1), axis=axis)
        z = jnp.zeros_like(x)
        pieces += [x, z] if lane_slot(h) == 0 else [z, x]
    return pieces


def _unpad_tiles(g, off, heads, lane_slot, axis):
    return [lax.slice_in_dim(g, off + LANES * h + HEAD_DIM * lane_slot(h),
                             off + LANES * h + HEAD_DIM * (lane_slot(h) + 1), axis=axis) for h in range(heads)]


A_SLOT = lambda h: h // A_GROUP
B_SLOT = lambda h: h % 2


def _layout_w_in(w):
    pad_f = jnp.zeros((w.shape[0], F_COLS - B_HEADS + TAIL_COLS), w.dtype)
    return jnp.concatenate(
        _pad_tiles(w, 0, A_HEADS, A_SLOT, 1) + [w[:, SRC_KA:SRC_KB]]
        + _pad_tiles(w, SRC_KB, B_HEADS, B_SLOT, 1) + _pad_tiles(w, SRC_VB, B_HEADS, B_SLOT, 1)
        + [w[:, SRC_GA:], w[:, SRC_F:SRC_GA], pad_f], axis=1)


def _unlayout_w_in(g):
    return jnp.concatenate(
        _unpad_tiles(g, OFF_QA, A_HEADS, A_SLOT, 1) + [g[:, OFF_KA:OFF_KB]]
        + _unpad_tiles(g, OFF_KB, B_HEADS, B_SLOT, 1) + _unpad_tiles(g, OFF_VB, B_HEADS, B_SLOT, 1)
        + [g[:, OFF_F:OFF_F + B_HEADS], g[:, OFF_GA:OFF_F]], axis=1)


def _local_step(x, target, meta, norms, b_forget, sinks, w, comm=None):
    n1, nmix, n2, nfin = norms
    w1i, w1o = w[:2]
    bsz, seq, d = x.shape
    l = PREFIX + seq
    nb = l // BLOCK
    t = bsz * l

    h0 = jnp.concatenate([jnp.zeros((bsz, N_PAD, d), F32),
                          jnp.broadcast_to(meta[None], (bsz, N_META, d)), x], axis=1).reshape(t, d)

    if comm is None:
        (h1, g1, u1), _ = _ffn_fwd(h0, n1, w1i, w1o)
        w_in, wa, wb, wo, w2i, w2o = w[2:]
    else:
        (h1, g1, u1), gathered = _ffn_fwd(h0, n1, w1i, w1o, comm.gather(GATHER_PROJ))
        w_in, = comm.gathered(GATHER_PROJ, gathered)
    wp = _layout_w_in(w_in)
    un, qa, ka, va, qb, kb, vb, ga, gb, f_logit = _proj_fwd(h1, nmix, wp)
    b_pad = jnp.concatenate([b_forget, jnp.zeros((1, F_COLS - B_HEADS), F32)], axis=1)
    c = _forget_cumsum(f_logit, b_pad, nb)
    c_heads = c[:, :B_HEADS].reshape(bsz, l, B_HEADS).transpose(0, 2, 1).reshape(bsz * B_HEADS, l)
    c_row = c_heads.reshape(bsz * B_HEADS, nb, 1, BLOCK)

    slopes = jnp.exp2(-8.0 * jnp.arange(1, A_HEADS + 1, dtype=F32) / A_HEADS)
    slope_rows = jnp.repeat(slopes.reshape(A_KV_HEADS, A_GROUP), BLOCK, axis=1)[:, :, None]
    sink_rows = jnp.repeat(sinks.reshape(A_KV_HEADS, A_GROUP), BLOCK, axis=1)[:, :, None]

    oa, lse_a = _swa_fwd(qa, ka, va, sink_rows, slope_rows, nb)
    if comm is None:
        (ob, ob_exact, lse_b), _ = _fox_fwd(qb, kb, vb, c_row, nb)
    else:
        (ob, ob_exact, lse_b), gathered = _fox_fwd(qb, kb, vb, c_row, nb, comm.gather(GATHER_LATE))
        wa, wb, wo, w2i, w2o = comm.gathered(GATHER_LATE, gathered)
    wa_p = jnp.concatenate(_pad_tiles(wa, 0, A_HEADS, A_SLOT, 0), axis=0)
    h2, mixed = _merge_fwd(h1, oa, ob, ga, gb, wa_p, wb, wo)
    (h3, g2, u2), _ = _ffn_fwd(h2, n2, w2i, w2o)
    loss, dh3, d_nfin = _loss_head(h3.reshape(bsz, l, d), nfin, target)

    (dh2, n2b, a2, dgu2, df2, dn2_parts), _ = _ffn_bwd(dh3.reshape(t, d), h2, n2, g2, u2, w2i, w2o)
    g_w2o = _tn_matmul(a2, df2, "grad_ffn2_w_out")
    g_w2i = _tn_matmul(n2b, dgu2, "grad_ffn2_w_in")

    dya, dyb, doa, dob, dga, dgb, dh2b = _merge_bwd(dh2, oa, ob, ga, gb, wa_p, wb, wo)
    g_wo = _tn_matmul(mixed, dh2b, "grad_w_out")
    g_wa = jnp.concatenate(_unpad_tiles(_tn_matmul(oa, dya, "grad_w_branch_a"), 0, A_HEADS, A_SLOT, 0), axis=0)
    g_wb = _tn_matmul(ob, dyb, "grad_w_branch_b")

    dqa, dka, dva, dsink_rows = _swa_bwd(qa, ka, va, doa, lse_a, sink_rows, slope_rows, nb)
    hosted = comm.scatter("ffn2", dict(ffn2_w_in=g_w2i, ffn2_w_out=g_w2o)) if comm else None
    (dqb, dkb, dvb, dc_row), pieces = _fox_bwd(qb, kb, vb, ob_exact, dob, lse_b, c_row, nb, hosted)
    if comm:
        comm.received("ffn2", pieces)
    dc = dc_row.reshape(bsz, B_HEADS, l).transpose(0, 2, 1).reshape(t, B_HEADS)
    dc = jnp.concatenate([dc, jnp.zeros((t, F_COLS - B_HEADS), F32)], axis=1)
    df_logit, db_parts = _forget_cumsum_bwd(dc, f_logit, b_pad, nb)

    dproj = jnp.concatenate([dqa, dka, dva, dqb, dkb, dvb, dga, dgb, df_logit,
                             jnp.zeros((t, TAIL_COLS), BF16)], axis=1)
    dh1, dnmix_parts = _proj_bwd(dh2, h1, nmix, dproj, wp)
    g_win = _unlayout_w_in(_tn_matmul(un, dproj, "grad_w_in"))

    hosted = comm.scatter("mixer", dict(w_in=g_win, w_branch_a=g_wa, w_branch_b=g_wb, w_out=g_wo)) if comm else None
    (dh0, n1b, a1, dgu1, df1, dn1_parts), pieces = _ffn_bwd(dh1, h0, n1, g1, u1, w1i, w1o, hosted)
    if comm:
        comm.received("mixer", pieces)
    g_w1o = _tn_matmul(a1, df1, "grad_ffn1_w_out")
    g_w1i = _tn_matmul(n1b, dgu1, "grad_ffn1_w_in")

    dh0 = dh0.reshape(bsz, l, d)
    grad_x = dh0[:, PREFIX:]
    small = dict(
        meta_tokens=jnp.sum(dh0[:, N_PAD:PREFIX], axis=0),
        ffn1_norm=jnp.sum(dn1_parts, axis=0),
        mix_norm=jnp.sum(dnmix_parts, axis=0),
        ffn2_norm=jnp.sum(dn2_parts, axis=0),
        final_norm=d_nfin,
        b_forget=jnp.sum(db_parts, axis=0)[:, :B_HEADS],
        attn_sinks=jnp.sum(dsink_rows.reshape(bsz, A_HEADS, BLOCK), axis=(0, 2)).reshape(1, A_HEADS),
    )
    big = dict(ffn1_w_in=g_w1i, ffn1_w_out=g_w1o, w_in=g_win, w_branch_a=g_wa, w_branch_b=g_wb,
               w_out=g_wo, ffn2_w_in=g_w2i, ffn2_w_out=g_w2o)
    return loss, grad_x, small, big


BIG = (
    ("ffn1_w_in", (D_MODEL, 5632), 1),
    ("ffn1_w_out", (2816, D_MODEL), 0),
    ("w_in", (D_MODEL, W_IN_COLS), 1),
    ("w_branch_a", (A_WIDTH, D_MODEL), 1),
    ("w_branch_b", (B_WIDTH, D_MODEL), 1),
    ("w_out", (D_MODEL, D_MODEL), 0),
    ("ffn2_w_in", (D_MODEL, 5632), 1),
    ("ffn2_w_out", (2816, D_MODEL), 0),
)
STACKED = "w_in"


def _coords():
    return lax.axis_index("x"), lax.axis_index("y"), lax.axis_index("c")


def _other_chips(x, y):
    return ((1 - x, y), (x, 1 - y), (1 - x, 1 - y))


def _chip_part(ref, name, shape, axis, k):
    if name == STACKED:
        return ref.at[k]
    size = shape[axis] // N_CHIPS
    start = pl.multiple_of(k * size, size)
    return ref.at[pl.ds(start, size), :] if axis == 0 else ref.at[:, pl.ds(start, size)]


def _full_shape(name, shape):
    return (N_CHIPS, shape[0], shape[1] // N_CHIPS) if name == STACKED else shape


class _Exchange:
    def __init__(self, ins, out_shape, n_sems, ops):
        self.ins, self.out_shape, self.n_sems, self.ops = list(ins), list(out_shape), n_sems, ops

    @property
    def scratch(self):
        return [pltpu.SemaphoreType.DMA((self.n_sems,)), pltpu.SemaphoreType.DMA((self.n_sems,))]


SEMS_PER_GATHER = 7


def _gather_exchange(shards, table):
    n = len(table)

    def ops(ins, outs, send_sems, recv_sems):
        x, y, c = _coords()
        mine = 2 * x + y
        sibling = (x, y, 1 - c)
        chips = _other_chips(x, y)
        slots = [2 * chip[0] + chip[1] for chip in chips]

        def part(i, k):
            name, shape, axis = table[i][:3]
            return _chip_part(outs[i], name, shape, axis, k)

        def half(ref, h):
            rows = ref.shape[0] // 2
            return ref.at[pl.ds(pl.multiple_of(h * rows, rows), rows), :]

        def own(i):
            sem = SEMS_PER_GATHER * i
            return pltpu.make_async_remote_copy(ins[i], part(i, mine), send_sems.at[sem], recv_sems.at[sem],
                                                device_id=sibling, device_id_type=MESH_ID)

        def fetch(i, j, slot):
            sem = SEMS_PER_GATHER * i + 1 + j
            if table[i][4]:
                src, dst = half(ins[i], c), half(part(i, slot), c)
            else:
                src, dst = ins[i], part(i, slot)
            return pltpu.make_async_remote_copy(src, dst, send_sems.at[sem], recv_sems.at[sem],
                                                device_id=(chips[j][0], chips[j][1], c), device_id_type=MESH_ID)

        def forward(i, j, h):
            sem = SEMS_PER_GATHER * i + 4 + j
            region = half(part(i, slots[j]), h)
            return pltpu.make_async_remote_copy(region, region, send_sems.at[sem], recv_sems.at[sem],
                                                device_id=sibling, device_id_type=MESH_ID)

        def start():
            for i in range(n):
                for j in range(3):
                    fetch(i, j, mine).start()
            for i in range(n):
                own(i).start()

        def relay():
            for i in range(n):
                for j in range(3):
                    fetch(i, j, slots[j]).wait_recv()
                    if table[i][4]:
                        forward(i, j, c).start()

        def finish():
            for i in range(n):
                own(i).wait()
                for j in range(3):
                    if table[i][4]:
                        forward(i, j, 1 - c).wait_recv()
                        forward(i, j, c).wait_send()
                    fetch(i, j, mine).wait_send()

        return start, relay, finish

    out_shape = [jax.ShapeDtypeStruct(_full_shape(name, shape), dtype) for name, shape, _, dtype, _ in table]
    return _Exchange(shards, out_shape, SEMS_PER_GATHER * n, ops)


def _run_exchange(exchange, name):
    n = len(exchange.ins)

    def body(*refs):
        start, relay, finish = exchange.ops(refs[:n], refs[n:2 * n], *refs[2 * n:])
        start()
        relay()
        finish()

    return pl.pallas_call(
        body,
        name=name,
        in_specs=[HBM_SPEC] * n,
        out_specs=[HBM_SPEC] * n,
        out_shape=exchange.out_shape,
        scratch_shapes=exchange.scratch,
    )(*exchange.ins)


def _host_exchange(exchange, in_refs, out_refs, sem_refs, first, middle, last):
    start, relay, finish = exchange.ops(in_refs, out_refs, *sem_refs)
    pl.when(first)(start)
    pl.when(middle)(relay)
    pl.when(last)(finish)


def _halves_view(name, shape, axis):
    r, c = shape
    if name == STACKED:
        return (N_CHIPS, 2, r // 2, c // N_CHIPS), lambda ref, h: ref.at[:, h]
    if axis == 1:
        return (2, r // 2, c), lambda ref, h: ref.at[h]
    return (N_CHIPS, 2, r // N_CHIPS // 2, c), lambda ref, h: ref.at[:, h]


def _exchange_halves(grads, entries, tag):
    n_w = len(entries)
    views = [_halves_view(*entry) for entry in entries]

    def body(*refs):
        ins, outs = refs[:n_w], refs[n_w:2 * n_w]
        send_sems, recv_sems = refs[2 * n_w:]
        x, y, c = _coords()
        copies = [pltpu.make_async_remote_copy(views[i][1](ins[i], 1 - c), outs[i], send_sems.at[i], recv_sems.at[i],
                                               device_id=(x, y, 1 - c), device_id_type=MESH_ID) for i in range(n_w)]
        for cp in copies:
            cp.start()
        for cp in copies:
            cp.wait()

    half_shape = lambda v: tuple(d for i, d in enumerate(v) if i != (1 if len(v) == 4 else 0))
    return pl.pallas_call(
        body,
        name="exchange_halves_" + tag,
        in_specs=[HBM_SPEC] * n_w,
        out_specs=[HBM_SPEC] * n_w,
        out_shape=[jax.ShapeDtypeStruct(half_shape(v[0]), F32) for v in views],
        scratch_shapes=[pltpu.SemaphoreType.DMA((n_w,)), pltpu.SemaphoreType.DMA((n_w,))],
    )(*[g.reshape(v[0]) for g, v in zip(grads, views)])


def _add_sibling(g_view, recv, c, name):
    shape = recv.shape
    if len(shape) == 2:
        tr = _tile(shape[0], 128, 16)
        grid = (shape[0] // tr,)
        g_spec = pl.BlockSpec((None, tr, shape[1]), lambda i, c_ref: (c_ref[0], i, 0))
        r_spec = pl.BlockSpec((tr, shape[1]), lambda i, c_ref: (i, 0))
    else:
        tr = _tile(shape[1], 256, 16)
        grid = (N_CHIPS, shape[1] // tr)
        g_spec = pl.BlockSpec((None, None, tr, shape[2]), lambda k, i, c_ref: (k, c_ref[0], i, 0))
        r_spec = pl.BlockSpec((None, tr, shape[2]), lambda k, i, c_ref: (k, i, 0))

    def body(c_ref, g_ref, r_ref, o_ref):
        o_ref[...] = (g_ref[...] + r_ref[...]).astype(BF16)

    return pl.pallas_call(
        body,
        name="add_sibling_" + name,
        grid_spec=pltpu.PrefetchScalarGridSpec(num_scalar_prefetch=1, grid=grid, in_specs=[g_spec, r_spec],
                                               out_specs=r_spec),
        out_shape=jax.ShapeDtypeStruct(shape, BF16),
        compiler_params=_cparams(("parallel",) * len(grid)),
    )(c, g_view, recv)


def _piece_of(ref, name, axis, k):
    if name == STACKED or axis == 0:
        return ref.at[k]
    size = ref.shape[1] // N_CHIPS
    return ref.at[:, pl.ds(pl.multiple_of(k * size, size), size)]


def _piece_shape(name, shape, axis):
    r, c = shape
    return (r // 2, c // N_CHIPS) if (axis == 1) else (r // N_CHIPS // 2, c)


def _scatter_exchange(partials, entries):
    n_w = len(entries)

    def ops(ins, outs, send_sems, recv_sems):
        x, y, c = _coords()
        chips = _other_chips(x, y)
        copies = []
        for i, (name, _, axis) in enumerate(entries):
            for j, chip in enumerate(chips):
                sem = 3 * i + j
                copies.append(pltpu.make_async_remote_copy(
                    _piece_of(ins[i], name, axis, 2 * chip[0] + chip[1]), outs[i].at[j], send_sems.at[sem],
                    recv_sems.at[sem], device_id=(chip[0], chip[1], c), device_id_type=MESH_ID))

        def start():
            for cp in copies:
                cp.start()

        def finish():
            for cp in copies:
                cp.wait()

        return start, lambda: None, finish

    out_shape = [jax.ShapeDtypeStruct((3,) + _piece_shape(*entry), BF16) for entry in entries]
    return _Exchange(partials, out_shape, 3 * n_w, ops)


def _add_chips(partial, recv, mine, name, axis):
    rows, cols = recv.shape[1:]
    tr = _tile(rows, 256, 16)
    if name == STACKED or axis == 0:
        p_spec = pl.BlockSpec((None, tr, cols), lambda i, k_ref: (k_ref[0], i, 0))
    else:
        p_spec = pl.BlockSpec((tr, cols), lambda i, k_ref: (i, k_ref[0]))

    def body(k_ref, p_ref, r_ref, o_ref):
        f32 = lambda a: a.astype(F32)
        o_ref[...] = ((f32(p_ref[...]) + f32(r_ref[0])) + f32(r_ref[1])) + f32(r_ref[2])

    return pl.pallas_call(
        body,
        name="add_chips_" + name,
        grid_spec=pltpu.PrefetchScalarGridSpec(
            num_scalar_prefetch=1, grid=(rows // tr,),
            in_specs=[p_spec, pl.BlockSpec((3, tr, cols), lambda i, k_ref: (0, i, 0))],
            out_specs=pl.BlockSpec((tr, cols), lambda i, k_ref: (i, 0))),
        out_shape=jax.ShapeDtypeStruct((rows, cols), F32),
        compiler_params=_cparams(("parallel",)),
    )(mine, partial, recv)


def _share_with_sibling(halves):
    n_w = len(halves)

    def body(*refs):
        ins, outs = refs[:n_w], refs[n_w:2 * n_w]
        send_sems, recv_sems = refs[2 * n_w:]
        x, y, c = _coords()
        copies = [pltpu.make_async_remote_copy(ins[i], outs[i], send_sems.at[i], recv_sems.at[i],
                                               device_id=(x, y, 1 - c), device_id_type=MESH_ID) for i in range(n_w)]
        for cp in copies:
            cp.start()
        for cp in copies:
            cp.wait()

    return pl.pallas_call(
        body,
        name="share_with_sibling",
        in_specs=[HBM_SPEC] * n_w,
        out_specs=[HBM_SPEC] * n_w,
        out_shape=[jax.ShapeDtypeStruct(h.shape, F32) for h in halves],
        scratch_shapes=[pltpu.SemaphoreType.DMA((n_w,)), pltpu.SemaphoreType.DMA((n_w,))],
    )(*halves)


SMALL_ROWS = 168


def _all_reduce_small(buf):
    def body(b_ref, out_ref, gathered, send_sems, recv_sems):
        x, y, c = _coords()
        me = 4 * x + 2 * y + c
        peers = [(x ^ fx, y ^ fy, c ^ fc) for fx in (0, 1) for fy in (0, 1) for fc in (0, 1)][1:]

        def copy(j, slot, dev):
            return pltpu.make_async_remote_copy(b_ref, gathered.at[slot], send_sems.at[j], recv_sems.at[j],
                                                device_id=dev, device_id_type=MESH_ID)

        for j, dev in enumerate(peers):
            copy(j, me, dev).start()
        gathered[me] = b_ref[...]
        for j, dev in enumerate(peers):
            copy(j, 4 * dev[0] + 2 * dev[1] + dev[2], dev).wait()
        acc = gathered[0]
        for d in range(1, N_DEV):
            acc = acc + gathered[d]
        out_ref[...] = acc

    return pl.pallas_call(
        body,
        name="all_reduce_small",
        in_specs=[VMEM_SPEC],
        out_specs=VMEM_SPEC,
        out_shape=jax.ShapeDtypeStruct(buf.shape, F32),
        scratch_shapes=[pltpu.VMEM((N_DEV,) + buf.shape, F32), pltpu.SemaphoreType.DMA((N_DEV - 1,)),
                        pltpu.SemaphoreType.DMA((N_DEV - 1,))],
    )(buf)


def _adamw(w, g, m, v):
    r, c = w.shape
    tr = _tile(r, 256, 8)

    def body(w_ref, g_ref, m_ref, v_ref, d_ref, mo_ref, vo_ref):
        gg = g_ref[...]
        mm = ADAM_B1 * m_ref[...] + (1.0 - ADAM_B1) * gg
        vv = ADAM_B2 * v_ref[...] + (1.0 - ADAM_B2) * (gg * gg)
        m_hat = mm / (1.0 - ADAM_B1 ** ADAM_STEP)
        v_hat = vv / (1.0 - ADAM_B2 ** ADAM_STEP)
        d_ref[...] = -ADAM_LR * (m_hat / (jnp.sqrt(v_hat) + ADAM_EPS) + ADAM_WD * w_ref[...])
        mo_ref[...] = mm
        vo_ref[...] = vv

    spec = pl.BlockSpec((tr, c), lambda i: (i, 0))
    return pl.pallas_call(
        body,
        name="adamw",
        grid=(r // tr,),
        in_specs=[spec] * 4,
        out_specs=[spec] * 3,
        out_shape=[jax.ShapeDtypeStruct((r, c), F32)] * 3,
        compiler_params=_cparams(("parallel",)),
    )(w, g, m, v)


def _adamw_halves(w, own, other, m, v, c, name, by_columns=False):
    r, cols = w.shape
    if by_columns:
        half_shape = (r, cols // 2)
        whole = pl.BlockSpec(half_shape, lambda h, i, c_ref: (0, h))
        part = pl.BlockSpec(half_shape, lambda h, i, c_ref: (0, 0))
        nt = 1
    else:
        half = r // 2
        tr = _tile(half, 256, 8)
        nt = half // tr
        whole = pl.BlockSpec((tr, cols), lambda h, i, c_ref: (h * nt + i, 0))
        part = pl.BlockSpec((tr, cols), lambda h, i, c_ref: (i, 0))

    def body(c_ref, w_ref, own_ref, other_ref, m_ref, v_ref, g_ref, d_ref, mo_ref, vo_ref):
        gg = jnp.where(pl.program_id(0) == c_ref[0], own_ref[...], other_ref[...])
        g_ref[...] = gg
        mm = ADAM_B1 * m_ref[...] + (1.0 - ADAM_B1) * gg
        vv = ADAM_B2 * v_ref[...] + (1.0 - ADAM_B2) * (gg * gg)
        m_hat = mm / (1.0 - ADAM_B1 ** ADAM_STEP)
        v_hat = vv / (1.0 - ADAM_B2 ** ADAM_STEP)
        d_ref[...] = -ADAM_LR * (m_hat / (jnp.sqrt(v_hat) + ADAM_EPS) + ADAM_WD * w_ref[...])
        mo_ref[...] = mm
        vo_ref[...] = vv

    return pl.pallas_call(
        body,
        name="adamw_" + name,
        grid_spec=pltpu.PrefetchScalarGridSpec(
            num_scalar_prefetch=1, grid=(2, nt),
            in_specs=[whole, part, part, whole, whole], out_specs=[whole] * 4),
        out_shape=[jax.ShapeDtypeStruct((r, cols), F32)] * 4,
        compiler_params=_cparams(("parallel", "parallel")),
    )(c, w, own, other, m, v)


GATHER_FIRST = ("ffn1_w_in", "ffn1_w_out")
GATHER_PROJ = ("w_in",)
GATHER_LATE = ("w_branch_a", "w_branch_b", "w_out", "ffn2_w_in", "ffn2_w_out")


class _Comm:
    def __init__(self, shards, c_arr, mine_arr):
        self.shards, self.c, self.mine = shards, c_arr, mine_arr
        self.groups, self.halves = {}, {}
        self.by_name = {entry[0]: entry for entry in BIG}

    def gather(self, names):
        table = [self.by_name[n] + (BF16, True) for n in names]
        return _gather_exchange([self.shards[n] for n in names], table)

    def gathered(self, names, outs):
        return [o.transpose(1, 0, 2).reshape(D_MODEL, W_IN_COLS) if n == STACKED else o for n, o in zip(names, outs)]

    def scatter(self, tag, grads):
        entries = [self.by_name[n] for n in grads]
        arrays = [g.reshape(D_MODEL, N_CHIPS, W_IN_COLS // N_CHIPS).transpose(1, 0, 2) if n == STACKED else g
                  for n, g in grads.items()]
        views = [_halves_view(*entry) for entry in entries]
        received = _exchange_halves(arrays, entries, tag)
        partials = [_add_sibling(g.reshape(v[0]), r, self.c, name)
                    for g, v, r, (name, _, _) in zip(arrays, views, received, entries)]
        self.groups[tag] = (entries, partials)
        return _scatter_exchange(partials, entries)

    def received(self, tag, pieces):
        entries, partials = self.groups[tag]
        for p, r, (name, _, axis) in zip(partials, pieces, entries):
            self.halves[name] = _add_chips(p, r, self.mine, name, axis)

    def finish(self):
        names = [n for n, _, _ in BIG]
        own = [self.halves[n] for n in names]
        return dict(zip(names, zip(own, _share_with_sibling(own))))


def kernel(x, meta_tokens, ffn1_norm, ffn1_w_in, ffn1_w_out, mix_norm, w_in, b_forget, attn_sinks, w_branch_a, w_branch_b, w_out, ffn2_norm, ffn2_w_in, ffn2_w_out, final_norm, loss_target, m_meta_tokens, m_ffn1_norm, m_ffn1_w_in, m_ffn1_w_out, m_mix_norm, m_w_in, m_b_forget, m_attn_sinks, m_w_branch_a, m_w_branch_b, m_w_out, m_ffn2_norm, m_ffn2_w_in, m_ffn2_w_out, m_final_norm, v_meta_tokens, v_ffn1_norm, v_ffn1_w_in, v_ffn1_w_out, v_mix_norm, v_w_in, v_b_forget, v_attn_sinks, v_w_branch_a, v_w_branch_b, v_w_out, v_ffn2_norm, v_ffn2_w_in, v_ffn2_w_out, v_final_norm):
    given = dict(locals())
    names = ["meta_tokens", "ffn1_norm", "ffn1_w_in", "ffn1_w_out", "mix_norm", "w_in", "b_forget", "attn_sinks",
             "w_branch_a", "w_branch_b", "w_out", "ffn2_norm", "ffn2_w_in", "ffn2_w_out", "final_norm"]
    big_names = [n for n, _, _ in BIG]
    cx, cy, cc = _coords()
    c_arr = cc.reshape(1).astype(jnp.int32)
    mine_arr = (2 * cx + cy).reshape(1).astype(jnp.int32)

    comm = _Comm({n: given[n][0].astype(BF16) for n in big_names}, c_arr, mine_arr)
    table = [comm.by_name[n] + (BF16, True) for n in GATHER_FIRST] + [("meta_tokens", (N_META, D_MODEL), 1, F32, False)]
    first = _gather_exchange([comm.shards[n] for n in GATHER_FIRST] + [meta_tokens], table)
    w1i, w1o, meta_full = _run_exchange(first, "gather_first")
    norms = (ffn1_norm, mix_norm, ffn2_norm, final_norm.reshape(1, D_MODEL))
    loss, grad_x, small, big = _local_step(x, loss_target, meta_full, norms, b_forget, attn_sinks, (w1i, w1o), comm)

    last = comm.scatter("ffn1", dict(ffn1_w_in=big["ffn1_w_in"], ffn1_w_out=big["ffn1_w_out"]))
    comm.received("ffn1", _run_exchange(last, "scatter_chip_sums"))
    grad_halves = comm.finish()
    grads = {}

    pad_lanes = lambda a: jnp.concatenate([a, jnp.zeros((1, LANES - a.shape[1]), F32)], axis=1)
    buf = jnp.concatenate([
        small["meta_tokens"].reshape(128, LANES),
        small["ffn1_norm"].reshape(8, LANES), small["mix_norm"].reshape(8, LANES),
        small["ffn2_norm"].reshape(8, LANES), small["final_norm"].reshape(8, LANES),
        loss, pad_lanes(small["b_forget"]), pad_lanes(small["attn_sinks"]),
        jnp.zeros((SMALL_ROWS - 163, LANES), F32)], axis=0)
    red = _all_reduce_small(buf)
    meta_cols = red[:128].reshape(N_META, D_MODEL)
    grads["meta_tokens"] = lax.dynamic_slice_in_dim(meta_cols, (2 * cx + cy) * (D_MODEL // N_CHIPS),
                                                    D_MODEL // N_CHIPS, axis=1)
    grads["ffn1_norm"] = red[128:136].reshape(1, D_MODEL)
    grads["mix_norm"] = red[136:144].reshape(1, D_MODEL)
    grads["ffn2_norm"] = red[144:152].reshape(1, D_MODEL)
    grads["final_norm"] = red[152:160].reshape(1, D_MODEL)
    loss_out = red[160, 0]
    grads["b_forget"] = red[161:162, :B_HEADS]
    grads["attn_sinks"] = red[162:163, :A_HEADS]

    out_g, out_d, out_m, out_v = [], [], [], []
    for n in names:
        w_full = given[n]
        shape = w_full.shape
        two_d = (lambda a: a.reshape(shape[-2], shape[-1])) if len(shape) >= 2 else (lambda a: a.reshape(1, shape[0]))
        if n == STACKED:
            own, other = grad_halves[n]
            tr_in = lambda a: two_d(a).T
            g2, d2, m2, v2 = [a.T for a in _adamw_halves(tr_in(w_full), own.T, other.T, tr_in(given["m_" + n]),
                                                          tr_in(given["v_" + n]), c_arr, n, by_columns=True)]
        elif n in grad_halves:
            own, other = grad_halves[n]
            g2, d2, m2, v2 = _adamw_halves(two_d(w_full), own, other, two_d(given["m_" + n]),
                                           two_d(given["v_" + n]), c_arr, n)
        else:
            g2 = two_d(grads[n])
            d2, m2, v2 = _adamw(two_d(w_full), g2, two_d(given["m_" + n]), two_d(given["v_" + n]))
        out_g.append(g2.reshape(shape))
        out_d.append(d2.reshape(shape))
        out_m.append(m2.reshape(shape))
        out_v.append(v2.reshape(shape))
    return (loss_out, grad_x, *out_g, *out_d, *out_m, *out_v)
```

```python
import jax
import jax.numpy as jnp
from jax import lax
from jax.experimental import pallas as pl
from jax.experimental.pallas import tpu as pltpu

F32 = jnp.float32
BF16 = jnp.bfloat16

D_MODEL = 1024
N_META = 16
BLOCK = 128
LANES = 128
PREFIX = BLOCK
N_PAD = PREFIX - N_META
HEAD_DIM = 64
A_HEADS = 8
A_KV_HEADS = 2
A_GROUP = 4
B_HEADS = 8
B_PAIRS = B_HEADS // 2
A_WIDTH = A_HEADS * HEAD_DIM
A_KV_WIDTH = A_KV_HEADS * HEAD_DIM
B_WIDTH = B_HEADS * HEAD_DIM
W_IN_COLS = A_WIDTH + 2 * A_KV_WIDTH + 3 * B_WIDTH + B_HEADS + 2 * D_MODEL
SRC_KA = A_WIDTH
SRC_VA = SRC_KA + A_KV_WIDTH
SRC_QB = SRC_VA + A_KV_WIDTH
SRC_KB = SRC_QB + B_WIDTH
SRC_VB = SRC_KB + B_WIDTH
SRC_F = SRC_VB + B_WIDTH
SRC_GA = SRC_F + B_HEADS
SRC_GB = SRC_GA + D_MODEL
A_PAD_WIDTH = A_HEADS * LANES
B_PAD_WIDTH = B_HEADS * LANES
F_COLS = LANES
OFF_QA = 0
OFF_KA = OFF_QA + A_PAD_WIDTH
OFF_VA = OFF_KA + A_KV_WIDTH
OFF_QB = OFF_VA + A_KV_WIDTH
OFF_KB = OFF_QB + B_WIDTH
OFF_VB = OFF_KB + B_PAD_WIDTH
OFF_GA = OFF_VB + B_PAD_WIDTH
OFF_GB = OFF_GA + D_MODEL
OFF_F = OFF_GB + D_MODEL
TAIL_COLS = 128
P_COLS = OFF_F + F_COLS + TAIL_COLS
EPS = 1e-6
NEG = -1e30
SCALE = HEAD_DIM ** -0.5
KEY_BLOCKS = 4

ADAM_LR = 0.001
ADAM_B1 = 0.9
ADAM_B2 = 0.999
ADAM_EPS = 1e-08
ADAM_WD = 0.01
ADAM_STEP = 10

N_CHIPS = 4
N_DEV = 8
VMEM_LIMIT = 56 * 1024 * 1024

NT_DIMS = (((1,), (1,)), ((), ()))
TN_DIMS = (((0,), (0,)), ((), ()))
MESH_ID = pl.DeviceIdType.MESH
HBM_SPEC = pl.BlockSpec(memory_space=pltpu.HBM)
VMEM_SPEC = pl.BlockSpec(memory_space=pltpu.VMEM)


def _tile(n, target, mult=16):
    best = None
    for t in range(mult, min(n, target) + 1, mult):
        if n % t == 0:
            best = t
    return best if best is not None else n


def _cparams(sem):
    return pltpu.CompilerParams(dimension_semantics=sem, vmem_limit_bytes=VMEM_LIMIT)


def _rms_scale(h):
    return lax.rsqrt(jnp.mean(h * h, axis=-1, keepdims=True) + EPS)


def _rms_bwd(dn, h, w):
    r = _rms_scale(h)
    dw = jnp.sum(dn * (h * r), axis=0, keepdims=True)
    z = dn * w
    dh = r * z - h * ((r * r * r) * jnp.mean(z * h, axis=-1, keepdims=True))
    return dh, dw


def _ffn_fwd(h, norm_w, w_in, w_out, exchange=None):
    t, d = h.shape
    f = w_out.shape[0]
    tm = _tile(t, 272)
    tc = _tile(f, 256, 128)
    nj = f // tc
    ni = t // tm
    n_x = len(exchange.ins) if exchange else 0

    def body(*refs):
        h_ref, nw_ref, wi_ref, wo_ref = refs[:4]
        hout_ref, g_ref, u_ref = refs[4 + n_x:7 + n_x]
        a_scr = refs[7 + 2 * n_x]
        i = pl.program_id(0)
        if exchange:
            _host_exchange(exchange, refs[4:4 + n_x], refs[7 + n_x:7 + 2 * n_x], refs[8 + 2 * n_x:],
                           i == 0, i == ni - 2, i == ni - 1)
        hh = h_ref[...]
        n = ((hh * _rms_scale(hh)) * nw_ref[...]).astype(BF16)
        for j in range(nj):
            cols = slice(j * tc, (j + 1) * tc)
            g = jnp.dot(n, wi_ref[:, j * tc:(j + 1) * tc], preferred_element_type=F32)
            u = jnp.dot(n, wi_ref[:, f + j * tc:f + (j + 1) * tc], preferred_element_type=F32)
            g_ref[:, cols] = g
            u_ref[:, cols] = u
            a_scr[:, cols] = ((g * jax.nn.sigmoid(g)) * u).astype(BF16)
        hout_ref[...] = hh + 0.5 * jnp.dot(a_scr[...], wo_ref[...], preferred_element_type=F32)

    resident = lambda a: pl.BlockSpec(a.shape, lambda i: (0, 0), pipeline_mode=pl.Buffered(1))
    row = lambda w: pl.BlockSpec((tm, w), lambda i: (i, 0))
    outs = pl.pallas_call(
        body,
        name="ffn_fwd",
        grid=(ni,),
        in_specs=[row(d), pl.BlockSpec((1, d), lambda i: (0, 0)), resident(w_in), resident(w_out)] + [HBM_SPEC] * n_x,
        out_specs=[row(d), row(f), row(f)] + [HBM_SPEC] * n_x,
        out_shape=[
            jax.ShapeDtypeStruct((t, d), F32),
            jax.ShapeDtypeStruct((t, f), F32),
            jax.ShapeDtypeStruct((t, f), F32),
        ] + (exchange.out_shape if exchange else []),
        scratch_shapes=[pltpu.VMEM((tm, f), BF16)] + (exchange.scratch if exchange else []),
        compiler_params=_cparams(("arbitrary",) if exchange else ("parallel",)),
    )(h, norm_w, w_in, w_out, *(exchange.ins if exchange else []))
    return outs[:3], outs[3:]


def _ffn_bwd(dh_out, h, norm_w, g, u, w_in, w_out, exchange=None):
    t, d = h.shape
    f = w_out.shape[0]
    tm = _tile(t, 272)
    tc = _tile(f, 256, 128)
    nj = f // tc
    ni = t // tm
    n_x = len(exchange.ins) if exchange else 0

    def body(*refs):
        dho_ref, h_ref, nw_ref, g_ref, u_ref, wi_ref, wo_ref = refs[:7]
        dhin_ref, n_ref, a_ref, dgu_ref, df_ref, dnw_ref = refs[7 + n_x:13 + n_x]
        i = pl.program_id(0)
        if exchange:
            _host_exchange(exchange, refs[7:7 + n_x], refs[13 + n_x:13 + 2 * n_x], refs[13 + 2 * n_x:],
                           i == 0, i == ni - 1, i == ni - 1)
        hh = h_ref[...]
        nw = nw_ref[...]
        n_ref[...] = ((hh * _rms_scale(hh)) * nw).astype(BF16)
        dho = dho_ref[...]
        df = (0.5 * dho).astype(BF16)
        df_ref[...] = df
        for j in range(nj):
            cols = slice(j * tc, (j + 1) * tc)
            da = lax.dot_general(df, wo_ref[cols, :], NT_DIMS, preferred_element_type=F32)
            gg = g_ref[:, cols]
            uu = u_ref[:, cols]
            sig = jax.nn.sigmoid(gg)
            sl = gg * sig
            a_ref[:, cols] = (sl * uu).astype(BF16)
            dgu_ref[0, :, cols] = ((da * uu) * (sig * (1.0 + gg * (1.0 - sig)))).astype(BF16)
            dgu_ref[1, :, cols] = (da * sl).astype(BF16)
        dn = (lax.dot_general(dgu_ref[0], wi_ref[:, :f], NT_DIMS, preferred_element_type=F32)
              + lax.dot_general(dgu_ref[1], wi_ref[:, f:], NT_DIMS, preferred_element_type=F32))
        dh, dw = _rms_bwd(dn, hh, nw)
        dhin_ref[...] = dho + dh
        dnw_ref[0] = dw

    resident = lambda a: pl.BlockSpec(a.shape, lambda i: (0, 0), pipeline_mode=pl.Buffered(1))
    row = lambda w: pl.BlockSpec((tm, w), lambda i: (i, 0))
    outs = pl.pallas_call(
        body,
        name="ffn_bwd",
        grid=(ni,),
        in_specs=[row(d), row(d), pl.BlockSpec((1, d), lambda i: (0, 0)), row(f), row(f),
                  resident(w_in), resident(w_out)] + [HBM_SPEC] * n_x,
        out_specs=[row(d), row(d), row(f), pl.BlockSpec((2, tm, f), lambda i: (0, i, 0)), row(d),
                   pl.BlockSpec((1, 1, d), lambda i: (i, 0, 0))] + [HBM_SPEC] * n_x,
        out_shape=[
            jax.ShapeDtypeStruct((t, d), F32),
            jax.ShapeDtypeStruct((t, d), BF16),
            jax.ShapeDtypeStruct((t, f), BF16),
            jax.ShapeDtypeStruct((2, t, f), BF16),
            jax.ShapeDtypeStruct((t, d), BF16),
            jax.ShapeDtypeStruct((ni, 1, d), F32),
        ] + (exchange.out_shape if exchange else []),
        scratch_shapes=exchange.scratch if exchange else [],
        compiler_params=_cparams(("arbitrary",) if exchange else ("parallel",)),
    )(dh_out, h, norm_w, g, u, w_in, w_out, *(exchange.ins if exchange else []))
    return outs[:6], outs[6:]


def _tn_matmul(a, b, name):
    t, k = a.shape
    split = b.ndim == 3
    n = 2 * b.shape[2] if split else b.shape[1]
    tk = _tile(k, 512, 128)
    tn = _tile(b.shape[-1], 1408, 128)
    per_half = b.shape[-1] // tn

    def body(a_ref, b_ref, o_ref):
        o_ref[...] = lax.dot_general(a_ref[...], b_ref[...], TN_DIMS, preferred_element_type=F32)

    if split:
        b_spec = pl.BlockSpec((None, t, tn), lambda i, j: (j // per_half, 0, j % per_half))
    else:
        b_spec = pl.BlockSpec((t, tn), lambda i, j: (0, j))
    return pl.pallas_call(
        body,
        name=name,
        grid=(k // tk, n // tn),
        in_specs=[pl.BlockSpec((t, tk), lambda i, j: (0, i)), b_spec],
        out_specs=pl.BlockSpec((tk, tn), lambda i, j: (i, j)),
        out_shape=jax.ShapeDtypeStruct((k, n), F32),
        compiler_params=_cparams(("parallel", "parallel")),
    )(a, b)


PROJ_PARTS = (
    (OFF_QA, A_PAD_WIDTH, True), (OFF_KA, A_KV_WIDTH, True), (OFF_VA, A_KV_WIDTH, True),
    (OFF_QB, B_WIDTH, True), (OFF_KB, B_PAD_WIDTH, True), (OFF_VB, B_PAD_WIDTH, True),
    (OFF_GA, D_MODEL, False), (OFF_GB, D_MODEL, False), (OFF_F, F_COLS, False),
)


def _proj_fwd(h, norm_w, w_p):
    t, d = h.shape
    tm = _tile(t, 272)

    def body(h_ref, nw_ref, w_ref, u_ref, *part_refs):
        hh = h_ref[...]
        un = ((hh * _rms_scale(hh)) * nw_ref[...]).astype(BF16)
        u_ref[...] = un
        for (off, width, _), p_ref in zip(PROJ_PARTS, part_refs):
            p_ref[...] = jnp.dot(un, w_ref[:, off:off + width], preferred_element_type=F32).astype(p_ref.dtype)

    row = lambda w: pl.BlockSpec((tm, w), lambda i: (i, 0))
    return pl.pallas_call(
        body,
        name="proj_fwd",
        grid=(t // tm,),
        in_specs=[row(d), pl.BlockSpec((1, d), lambda i: (0, 0)), pl.BlockSpec(w_p.shape, lambda i: (0, 0))],
        out_specs=[row(d)] + [row(width) for _, width, _ in PROJ_PARTS],
        out_shape=[jax.ShapeDtypeStruct((t, d), BF16)]
        + [jax.ShapeDtypeStruct((t, width), BF16 if is_bf else F32) for _, width, is_bf in PROJ_PARTS],
        compiler_params=_cparams(("parallel",)),
    )(h, norm_w, w_p)


def _proj_bwd(dh_out, h, norm_w, dproj, w_p):
    t, d = h.shape
    n = w_p.shape[1]
    tm = _tile(t, 272)
    ni = t // tm

    def body(dho_ref, h_ref, nw_ref, dp_ref, w_ref, dhin_ref, dnw_ref):
        dn = lax.dot_general(dp_ref[...], w_ref[...], NT_DIMS, preferred_element_type=F32)
        dh, dw = _rms_bwd(dn, h_ref[...], nw_ref[...])
        dhin_ref[...] = dho_ref[...] + dh
        dnw_ref[0] = dw

    row = lambda w: pl.BlockSpec((tm, w), lambda i: (i, 0))
    return pl.pallas_call(
        body,
        name="proj_bwd",
        grid=(ni,),
        in_specs=[row(d), row(d), pl.BlockSpec((1, d), lambda i: (0, 0)), row(n),
                  pl.BlockSpec(w_p.shape, lambda i: (0, 0), pipeline_mode=pl.Buffered(1))],
        out_specs=[row(d), pl.BlockSpec((1, 1, d), lambda i: (i, 0, 0))],
        out_shape=[jax.ShapeDtypeStruct((t, d), F32), jax.ShapeDtypeStruct((ni, 1, d), F32)],
        compiler_params=_cparams(("parallel",)),
    )(dh_out, h, norm_w, dproj, w_p)


def _merge_fwd(h, oa, ob, ga, gb, wa, wb, wo):
    t, d = h.shape
    tm = _tile(t, 544)

    def body(h_ref, oa_ref, ob_ref, ga_ref, gb_ref, wa_ref, wb_ref, wo_ref, hout_ref, mix_ref):
        ya = jnp.dot(oa_ref[...], wa_ref[...], preferred_element_type=F32)
        yb = jnp.dot(ob_ref[...], wb_ref[...], preferred_element_type=F32)
        mixed = (jax.nn.sigmoid(ga_ref[...]) * ya + jax.nn.sigmoid(gb_ref[...]) * yb).astype(BF16)
        mix_ref[...] = mixed
        hout_ref[...] = h_ref[...] + jnp.dot(mixed, wo_ref[...], preferred_element_type=F32)

    row = lambda w: pl.BlockSpec((tm, w), lambda i: (i, 0))
    full = lambda a: pl.BlockSpec(a.shape, lambda i: (0, 0))
    return pl.pallas_call(
        body,
        name="merge_fwd",
        grid=(t // tm,),
        in_specs=[row(d), row(oa.shape[1]), row(ob.shape[1]), row(d), row(d), full(wa), full(wb), full(wo)],
        out_specs=[row(d), row(d)],
        out_shape=[jax.ShapeDtypeStruct((t, d), F32), jax.ShapeDtypeStruct((t, d), BF16)],
        compiler_params=_cparams(("parallel",)),
    )(h, oa, ob, ga, gb, wa, wb, wo)


def _merge_bwd(dh, oa, ob, ga, gb, wa, wb, wo):
    t, d = dh.shape
    tm = _tile(t, 544)

    def body(dh_ref, oa_ref, ob_ref, ga_ref, gb_ref, wa_ref, wb_ref, wo_ref,
             dya_ref, dyb_ref, doa_ref, dob_ref, dga_ref, dgb_ref, dhb_ref):
        dhb = dh_ref[...].astype(BF16)
        dhb_ref[...] = dhb
        dmix = lax.dot_general(dhb, wo_ref[...], NT_DIMS, preferred_element_type=F32)
        for o_ref, g_ref, w_ref, dy_ref, do_ref, dg_ref in (
                (oa_ref, ga_ref, wa_ref, dya_ref, doa_ref, dga_ref),
                (ob_ref, gb_ref, wb_ref, dyb_ref, dob_ref, dgb_ref)):
            y = jnp.dot(o_ref[...], w_ref[...], preferred_element_type=F32)
            s = jax.nn.sigmoid(g_ref[...])
            dy = (dmix * s).astype(BF16)
            dy_ref[...] = dy
            dg_ref[...] = ((dmix * y) * (s * (1.0 - s))).astype(BF16)
            do_ref[...] = lax.dot_general(dy, w_ref[...], NT_DIMS, preferred_element_type=F32).astype(BF16)

    row = lambda w: pl.BlockSpec((tm, w), lambda i: (i, 0))
    full = lambda a: pl.BlockSpec(a.shape, lambda i: (0, 0))
    wa_w, wb_w = oa.shape[1], ob.shape[1]
    return pl.pallas_call(
        body,
        name="merge_bwd",
        grid=(t // tm,),
        in_specs=[row(d), row(wa_w), row(wb_w), row(d), row(d), full(wa), full(wb), full(wo)],
        out_specs=[row(d), row(d), row(wa_w), row(wb_w), row(d), row(d), row(d)],
        out_shape=[
            jax.ShapeDtypeStruct((t, d), BF16), jax.ShapeDtypeStruct((t, d), BF16),
            jax.ShapeDtypeStruct((t, wa_w), BF16), jax.ShapeDtypeStruct((t, wb_w), BF16),
            jax.ShapeDtypeStruct((t, d), BF16), jax.ShapeDtypeStruct((t, d), BF16),
            jax.ShapeDtypeStruct((t, d), BF16),
        ],
        compiler_params=_cparams(("parallel",)),
    )(dh, oa, ob, ga, gb, wa, wb, wo)


def _tri_dot(tri, x):
    hi = x.astype(BF16)
    r1 = x - hi.astype(F32)
    mid = r1.astype(BF16)
    lo = (r1 - mid.astype(F32)).astype(BF16)
    return (jnp.dot(tri, hi, preferred_element_type=F32)
            + jnp.dot(tri, mid, preferred_element_type=F32)
            + jnp.dot(tri, lo, preferred_element_type=F32))


def _forget_cumsum(f_logit, b_pad, nb):
    t, w = f_logit.shape
    bsz = t // (nb * BLOCK)

    def body(f_ref, b_ref, c_ref, carry):
        n = pl.program_id(1)

        @pl.when(n == 0)
        def _():
            carry[...] = jnp.zeros_like(carry)

        x = jax.nn.log_sigmoid(f_ref[...] + b_ref[...])
        rows = lax.broadcasted_iota(jnp.int32, (BLOCK, BLOCK), 0)
        cols = lax.broadcasted_iota(jnp.int32, (BLOCK, BLOCK), 1)
        tri = (cols <= rows).astype(BF16)
        c = _tri_dot(tri, x) + carry[...]
        c_ref[...] = c
        carry[...] = c[BLOCK - 1:BLOCK, :]

    return pl.pallas_call(
        body,
        name="forget_cumsum",
        grid=(bsz, nb),
        in_specs=[pl.BlockSpec((BLOCK, w), lambda b, n: (b * nb + n, 0)),
                  pl.BlockSpec((1, w), lambda b, n: (0, 0))],
        out_specs=pl.BlockSpec((BLOCK, w), lambda b, n: (b * nb + n, 0)),
        out_shape=jax.ShapeDtypeStruct((t, w), F32),
        scratch_shapes=[pltpu.VMEM((1, w), F32)],
        compiler_params=_cparams(("parallel", "arbitrary")),
    )(f_logit, b_pad)


def _forget_cumsum_bwd(dc, f_logit, b_pad, nb):
    t, w = f_logit.shape
    bsz = t // (nb * BLOCK)

    def body(dc_ref, f_ref, b_ref, df_ref, db_ref, carry):
        n = pl.program_id(1)

        @pl.when(n == 0)
        def _():
            carry[...] = jnp.zeros_like(carry)
            db_ref[...] = jnp.zeros_like(db_ref)

        rows = lax.broadcasted_iota(jnp.int32, (BLOCK, BLOCK), 0)
        cols = lax.broadcasted_iota(jnp.int32, (BLOCK, BLOCK), 1)
        tri = (cols >= rows).astype(BF16)
        dlf = _tri_dot(tri, dc_ref[...]) + carry[...]
        carry[...] = dlf[0:1, :]
        df = dlf * jax.nn.sigmoid(-(f_ref[...] + b_ref[...]))
        df_ref[...] = df.astype(BF16)
        db_ref[0] += jnp.sum(df, axis=0, keepdims=True)

    rev = lambda b, n: (b * nb + (nb - 1 - n), 0)
    return pl.pallas_call(
        body,
        name="forget_cumsum_bwd",
        grid=(bsz, nb),
        in_specs=[pl.BlockSpec((BLOCK, w), rev),
                  pl.BlockSpec((BLOCK, w), rev),
                  pl.BlockSpec((1, w), lambda b, n: (0, 0))],
        out_specs=[pl.BlockSpec((BLOCK, w), rev),
                   pl.BlockSpec((1, 1, w), lambda b, n: (b, 0, 0))],
        out_shape=[jax.ShapeDtypeStruct((t, w), BF16), jax.ShapeDtypeStruct((bsz, 1, w), F32)],
        scratch_shapes=[pltpu.VMEM((1, w), F32)],
        compiler_params=_cparams(("parallel", "arbitrary")),
    )(dc, f_logit, b_pad)


GROUP_ROWS = A_GROUP * BLOCK


def _stack_heads(ref):
    return jnp.concatenate([ref[:, i * LANES:(i + 1) * LANES] for i in range(A_GROUP)], axis=0)


def _unstack_heads(ref, x):
    for i in range(A_GROUP):
        ref[:, i * LANES:(i + 1) * LANES] = x[i * BLOCK:(i + 1) * BLOCK].astype(ref.dtype)


def _swa_logits(q, km, kp, kc, slope, n):
    qi = lax.broadcasted_iota(jnp.int32, (GROUP_ROWS, BLOCK), 0) & (BLOCK - 1)
    kj = lax.broadcasted_iota(jnp.int32, (GROUP_ROWS, BLOCK), 1)
    out = []
    for kk, dist, ok in (
            (km, n * BLOCK + qi - kj, (kj >= N_PAD) & (n * BLOCK + qi - kj >= 0)),
            (kp, BLOCK + qi - kj, (kj > qi) & (n >= 2)),
            (kc, qi - kj, (kj <= qi) & (n >= 1))):
        s = lax.dot_general(q, kk, NT_DIMS, preferred_element_type=F32) * SCALE
        s = s - slope * dist.astype(F32)
        out.append(jnp.where(ok, s, NEG))
    return out


def _swa_specs(nb):
    row = lambda b, n: b * nb + n
    qspec = pl.BlockSpec((BLOCK, A_GROUP * LANES), lambda b, g, n: (row(b, n), g))
    kv_m = pl.BlockSpec((BLOCK, LANES), lambda b, g, n: (row(b, 0), 0))
    kv_p = pl.BlockSpec((BLOCK, LANES), lambda b, g, n: (row(b, jnp.maximum(n - 1, 0)), 0))
    kv_c = pl.BlockSpec((BLOCK, LANES), lambda b, g, n: (row(b, n), 0))
    rowspec = pl.BlockSpec((1, GROUP_ROWS, 1), lambda b, g, n: (g, 0, 0))
    lsespec = pl.BlockSpec((1, 1, GROUP_ROWS, 1), lambda b, g, n: (row(b, n), g, 0, 0))
    return qspec, kv_m, kv_p, kv_c, rowspec, lsespec


def _swa_fwd(q, k, v, sink_rows, slope_rows, nb):
    t = q.shape[0]
    bsz = t // (nb * BLOCK)

    def body(q_ref, km_ref, kp_ref, kc_ref, vm_ref, vp_ref, vc_ref, sink_ref, slope_ref, o_ref, lse_ref):
        g = pl.program_id(1)
        n = pl.program_id(2)
        qq = _stack_heads(q_ref)
        sink = sink_ref[0]
        s_m, s_p, s_c = _swa_logits(qq, km_ref[...], kp_ref[...], kc_ref[...], slope_ref[0], n)
        m = jnp.maximum(jnp.maximum(jnp.max(s_m, axis=-1, keepdims=True), jnp.max(s_p, axis=-1, keepdims=True)),
                        jnp.maximum(jnp.max(s_c, axis=-1, keepdims=True), sink))
        e_m = jnp.exp(s_m - m)
        e_p = jnp.exp(s_p - m)
        e_c = jnp.exp(s_c - m)
        z = (jnp.sum(e_m, axis=-1, keepdims=True) + jnp.sum(e_p, axis=-1, keepdims=True)
             + jnp.sum(e_c, axis=-1, keepdims=True) + jnp.exp(sink - m))
        inv = 1.0 / z
        o = (jnp.dot((e_m * inv).astype(BF16), vm_ref[...], preferred_element_type=F32)
             + jnp.dot((e_p * inv).astype(BF16), vp_ref[...], preferred_element_type=F32)
             + jnp.dot((e_c * inv).astype(BF16), vc_ref[...], preferred_element_type=F32))
        lane_group = lax.broadcasted_iota(jnp.int32, (GROUP_ROWS, LANES), 1) // HEAD_DIM
        _unstack_heads(o_ref, jnp.where(lane_group == g, o, 0.0))
        lse_ref[0, 0] = m + jnp.log(z)

    qspec, kv_m, kv_p, kv_c, rowspec, lsespec = _swa_specs(nb)
    return pl.pallas_call(
        body,
        name="swa_fwd",
        grid=(bsz, A_KV_HEADS, nb),
        in_specs=[qspec, kv_m, kv_p, kv_c, kv_m, kv_p, kv_c, rowspec, rowspec],
        out_specs=[qspec, lsespec],
        out_shape=[jax.ShapeDtypeStruct((t, A_PAD_WIDTH), BF16),
                   jax.ShapeDtypeStruct((t // BLOCK, A_KV_HEADS, GROUP_ROWS, 1), F32)],
        compiler_params=_cparams(("parallel", "parallel", "arbitrary")),
    )(q, k, k, k, v, v, v, sink_rows, slope_rows)


def _swa_bwd(q, k, v, do, lse, sink_rows, slope_rows, nb):
    t = q.shape[0]
    l = nb * BLOCK
    bsz = t // l

    def body(q_ref, km_ref, kp_ref, kc_ref, vm_ref, vp_ref, vc_ref, do_ref, lse_ref, sink_ref, slope_ref,
             dq_ref, dk_ref, dv_ref, dsink_ref, dk_acc, dv_acc):
        g = pl.program_id(1)
        n = pl.program_id(2)

        @pl.when((g == 0) & (n == 0))
        def _():
            dk_acc[...] = jnp.zeros_like(dk_acc)
            dv_acc[...] = jnp.zeros_like(dv_acc)

        @pl.when(n == 0)
        def _():
            dsink_ref[...] = jnp.zeros_like(dsink_ref)

        qq = _stack_heads(q_ref)
        dob = _stack_heads(do_ref)
        lse = lse_ref[0, 0]
        logits = _swa_logits(qq, km_ref[...], kp_ref[...], kc_ref[...], slope_ref[0], n)
        probs = [jnp.exp(s - lse) for s in logits]
        dps = [lax.dot_general(dob, v_ref[...], NT_DIMS, preferred_element_type=F32)
               for v_ref in (vm_ref, vp_ref, vc_ref)]
        delta = sum(jnp.sum(p * dp, axis=-1, keepdims=True) for p, dp in zip(probs, dps))
        prev = jnp.maximum(n - 1, 0)
        dq = jnp.zeros((GROUP_ROWS, LANES), F32)
        for p, dp, k_ref, start in ((probs[0], dps[0], km_ref, 0),
                                    (probs[1], dps[1], kp_ref, prev * BLOCK),
                                    (probs[2], dps[2], kc_ref, n * BLOCK)):
            ds = (p * (dp - delta)).astype(BF16)
            dq = dq + jnp.dot(ds, k_ref[...], preferred_element_type=F32)
            rows = pl.ds(pl.multiple_of(start, BLOCK), BLOCK)
            dk_acc[rows, :] += lax.dot_general(ds, qq, TN_DIMS, preferred_element_type=F32) * SCALE
            dv_acc[rows, :] += lax.dot_general(p.astype(BF16), dob, TN_DIMS, preferred_element_type=F32)
        _unstack_heads(dq_ref, dq * SCALE)
        dsink_ref[0, 0] += -(jnp.exp(sink_ref[0] - lse) * delta)

        @pl.when((g == A_KV_HEADS - 1) & (n == nb - 1))
        def _():
            dk_ref[...] = dk_acc[...].astype(BF16)
            dv_ref[...] = dv_acc[...].astype(BF16)

    qspec, kv_m, kv_p, kv_c, rowspec, lsespec = _swa_specs(nb)
    kv_all = pl.BlockSpec((l, LANES), lambda b, g, n: (b, 0))
    return pl.pallas_call(
        body,
        name="swa_bwd",
        grid=(bsz, A_KV_HEADS, nb),
        in_specs=[qspec, kv_m, kv_p, kv_c, kv_m, kv_p, kv_c, qspec, lsespec, rowspec, rowspec],
        out_specs=[qspec, kv_all, kv_all,
                   pl.BlockSpec((1, 1, GROUP_ROWS, 1), lambda b, g, n: (b, g, 0, 0))],
        out_shape=[jax.ShapeDtypeStruct((t, A_PAD_WIDTH), BF16),
                   jax.ShapeDtypeStruct((t, LANES), BF16),
                   jax.ShapeDtypeStruct((t, LANES), BF16),
                   jax.ShapeDtypeStruct((bsz, A_KV_HEADS, GROUP_ROWS, 1), F32)],
        scratch_shapes=[pltpu.VMEM((l, LANES), F32), pltpu.VMEM((l, LANES), F32)],
        compiler_params=_cparams(("parallel", "arbitrary", "arbitrary")),
    )(q, k, k, k, v, v, v, do, lse, sink_rows, slope_rows)


CHUNK = KEY_BLOCKS * BLOCK


def _fox_chunk(qb, ci):
    sb = jnp.maximum(jnp.minimum(KEY_BLOCKS * ci, qb + 1 - KEY_BLOCKS), 0)
    lo = jnp.maximum(ci * CHUNK, N_PAD)
    return sb, lo, pl.ds(pl.multiple_of(sb * BLOCK, BLOCK), CHUNK)


def _fox_logits(s_ref, cr_ref, e, j, sb, lo, qb):
    q_pos = qb * BLOCK + lax.broadcasted_iota(jnp.int32, (BLOCK, BLOCK), 0)
    k_pos = (sb + j) * BLOCK + lax.broadcasted_iota(jnp.int32, (BLOCK, BLOCK), 1)
    s = s_ref[e, :, j * BLOCK:(j + 1) * BLOCK] - cr_ref[e, sb + j]
    return jnp.where((k_pos <= q_pos) & (k_pos >= lo), s, NEG)


def _fox_specs(nb):
    l = nb * BLOCK
    q_spec = pl.BlockSpec((BLOCK, LANES), lambda b, p, i: (b * nb + i, p))
    kv_spec = pl.BlockSpec((l, 2 * LANES), lambda b, p, i: (b, p))
    cc_spec = pl.BlockSpec((2, BLOCK, 1), lambda b, p, i: (b * B_PAIRS + p, i, 0))
    cr_spec = pl.BlockSpec((2, nb, 1, BLOCK), lambda b, p, i: (b * B_PAIRS + p, 0, 0, 0))
    return q_spec, kv_spec, cc_spec, cr_spec


def _fox_fwd(q, k, v, c_row, nb, exchange=None):
    t = q.shape[0]
    bsz = t // (nb * BLOCK)
    assert nb >= KEY_BLOCKS

    n_x = len(exchange.ins) if exchange else 0

    def body(*refs):
        q_ref, k_ref, v_ref, cr_ref = refs[:4]
        o_ref, ox_ref, lse_ref = refs[4 + n_x:7 + n_x]
        s_scr, hi_scr, lo_scr = refs[7 + 2 * n_x:10 + 2 * n_x]
        qb = pl.program_id(2)
        if exchange:
            first = (pl.program_id(0) == 0) & (pl.program_id(1) == 0)
            last = (pl.program_id(0) == bsz - 1) & (pl.program_id(1) == B_PAIRS - 1)
            _host_exchange(exchange, refs[4:4 + n_x], refs[7 + n_x:7 + 2 * n_x], refs[10 + 2 * n_x:],
                           first & (qb == 0), last & (qb == 0), last & (qb == nb - 1))
        qs = q_ref[...] * SCALE
        first_half = lax.broadcasted_iota(jnp.int32, (BLOCK, LANES), 1) < HEAD_DIM

        def step(ci, carry):
            stats, acc, acc_lo = carry[:4], carry[4], carry[5]
            sb, lo, krows = _fox_chunk(qb, ci)
            new_stats, alphas = [], []
            pv = jnp.zeros((BLOCK, LANES), F32)
            pv_lo = jnp.zeros((BLOCK, LANES), F32)
            for e in range(2):
                m, z = stats[2 * e], stats[2 * e + 1]
                tile = slice(e * LANES, (e + 1) * LANES)
                s_scr[e] = lax.dot_general(qs, k_ref[krows, tile], NT_DIMS, preferred_element_type=F32)
                m_new = m
                for j in range(KEY_BLOCKS):
                    s = _fox_logits(s_scr, cr_ref, e, j, sb, lo, qb)
                    s_scr[e, :, j * BLOCK:(j + 1) * BLOCK] = s
                    m_new = jnp.maximum(m_new, jnp.max(s, axis=-1, keepdims=True))
                alpha = jnp.exp(m - m_new)
                z = alpha * z
                for j in range(KEY_BLOCKS):
                    cols = slice(j * BLOCK, (j + 1) * BLOCK)
                    p = jnp.exp(s_scr[e, :, cols] - m_new)
                    z = z + jnp.sum(p, axis=-1, keepdims=True)
                    hi = p.astype(BF16)
                    hi_scr[e, :, cols] = hi
                    lo_scr[e, :, cols] = (p - hi.astype(F32)).astype(BF16)
                vv = v_ref[krows, tile]
                pv = pv + jnp.dot(hi_scr[e], vv, preferred_element_type=F32)
                pv_lo = pv_lo + jnp.dot(lo_scr[e], vv, preferred_element_type=F32)
                new_stats += [m_new, z]
                alphas.append(alpha)
            alpha = jnp.where(first_half, alphas[0], alphas[1])
            return (*new_stats, alpha * acc + pv, alpha * acc_lo + pv_lo)

        col = lambda val: jnp.full((BLOCK, 1), val, F32)
        m0, z0, m1, z1, acc, acc_lo = lax.fori_loop(
            0, (qb + KEY_BLOCKS) // KEY_BLOCKS, step,
            (col(NEG), col(0.0), col(NEG), col(0.0), jnp.zeros((BLOCK, LANES), F32), jnp.zeros((BLOCK, LANES), F32)))
        inv = 1.0 / jnp.where(first_half, z0, z1)
        o_ref[...] = (acc * inv).astype(BF16)
        ox_ref[...] = (acc + acc_lo) * inv
        lse_ref[0] = m0 + jnp.log(z0)
        lse_ref[1] = m1 + jnp.log(z1)

    q_spec, kv_spec, cc_spec, cr_spec = _fox_specs(nb)
    outs = pl.pallas_call(
        body,
        name="fox_fwd",
        grid=(bsz, B_PAIRS, nb),
        in_specs=[q_spec, kv_spec, kv_spec, cr_spec] + [HBM_SPEC] * n_x,
        out_specs=[q_spec, q_spec, cc_spec] + [HBM_SPEC] * n_x,
        out_shape=[jax.ShapeDtypeStruct((t, B_WIDTH), BF16), jax.ShapeDtypeStruct((t, B_WIDTH), F32),
                   jax.ShapeDtypeStruct((bsz * B_HEADS, nb * BLOCK, 1), F32)] + (exchange.out_shape if exchange else []),
        scratch_shapes=[pltpu.VMEM((2, BLOCK, CHUNK), F32), pltpu.VMEM((2, BLOCK, CHUNK), BF16),
                        pltpu.VMEM((2, BLOCK, CHUNK), BF16)] + (exchange.scratch if exchange else []),
        compiler_params=_cparams(("arbitrary",) * 3 if exchange else ("parallel", "parallel", "arbitrary")),
    )(q, k, v, c_row, *(exchange.ins if exchange else []))
    return outs[:3], outs[3:]


def _fox_bwd(q, k, v, o_exact, do, lse, c_row, nb, exchange=None):
    t = q.shape[0]
    l = nb * BLOCK
    bsz = t // l

    n_x = len(exchange.ins) if exchange else 0

    def body(*refs):
        q_ref, k_ref, v_ref, ox_ref, do_ref, lse_ref, cr_ref = refs[:7]
        dq_ref, dk_ref, dv_ref, dc_ref = refs[7 + n_x:11 + n_x]
        dk_acc, dv_acc, s_scr, dp_scr, p_scr, ds_scr = refs[11 + 2 * n_x:17 + 2 * n_x]
        qb = pl.program_id(2)
        if exchange:
            first = (pl.program_id(0) == 0) & (pl.program_id(1) == 0)
            last = (pl.program_id(0) == bsz - 1) & (pl.program_id(1) == B_PAIRS - 1)
            _host_exchange(exchange, refs[7:7 + n_x], refs[11 + n_x:11 + 2 * n_x], refs[17 + 2 * n_x:],
                           first & (qb == 0), last & (qb == 0), last & (qb == nb - 1))

        @pl.when(qb == 0)
        def _():
            dk_acc[...] = jnp.zeros_like(dk_acc)
            dv_acc[...] = jnp.zeros_like(dv_acc)
            dc_ref[...] = jnp.zeros_like(dc_ref)

        qs = q_ref[...] * SCALE
        dob = do_ref[...]
        top_half = lax.broadcasted_iota(jnp.int32, (LANES, BLOCK), 0) < HEAD_DIM
        pair_t = lambda x: jnp.concatenate([jnp.where(top_half, x.T, 0), jnp.where(top_half, 0, x.T)], axis=1)
        qs_t = pair_t(qs)
        dob_t = pair_t(dob)
        first_half = lax.broadcasted_iota(jnp.int32, (BLOCK, LANES), 1) < HEAD_DIM
        weighted = dob.astype(F32) * ox_ref[...]
        deltas = (jnp.sum(jnp.where(first_half, weighted, 0.0), axis=-1, keepdims=True),
                  jnp.sum(jnp.where(first_half, 0.0, weighted), axis=-1, keepdims=True))

        def step(ci, dq):
            sb, lo, krows = _fox_chunk(qb, ci)
            for e in range(2):
                tile = slice(e * LANES, (e + 1) * LANES)
                kk = k_ref[krows, tile]
                s_scr[e] = lax.dot_general(qs, kk, NT_DIMS, preferred_element_type=F32)
                dp_scr[e] = lax.dot_general(dob, v_ref[krows, tile], NT_DIMS, preferred_element_type=F32)
                lse_e = lse_ref[e]
                for j in range(KEY_BLOCKS):
                    cols = slice(j * BLOCK, (j + 1) * BLOCK)
                    p = jnp.exp(_fox_logits(s_scr, cr_ref, e, j, sb, lo, qb) - lse_e)
                    ds = p * (dp_scr[e, :, cols] - deltas[e])
                    dc_ref[e, sb + j] -= jnp.sum(ds, axis=0, keepdims=True)
                    p_scr[e, :, cols] = p.astype(BF16)
                    ds_scr[e, :, cols] = ds.astype(BF16)
                dq = dq + jnp.dot(ds_scr[e], kk, preferred_element_type=F32)
            dk_t = jnp.dot(qs_t, ds_scr[...].reshape(2 * BLOCK, CHUNK), preferred_element_type=F32)
            dv_t = jnp.dot(dob_t, p_scr[...].reshape(2 * BLOCK, CHUNK), preferred_element_type=F32)
            for j in range(KEY_BLOCKS):
                cols = slice(j * BLOCK, (j + 1) * BLOCK)
                dk_acc[sb + j] += dk_t[:, cols]
                dv_acc[sb + j] += dv_t[:, cols]
            return dq

        dq = lax.fori_loop(0, (qb + KEY_BLOCKS) // KEY_BLOCKS, step, jnp.zeros((BLOCK, LANES), F32))
        dq_ref[...] = (dq * SCALE).astype(BF16)

        @pl.when(qb == nb - 1)
        def _():
            for kb in range(nb):
                rows = slice(kb * BLOCK, (kb + 1) * BLOCK)
                for acc, out_ref in ((dk_acc, dk_ref), (dv_acc, dv_ref)):
                    pair = acc[kb].T
                    out_ref[rows, 0:LANES] = jnp.where(first_half, pair, 0.0).astype(BF16)
                    out_ref[rows, LANES:2 * LANES] = jnp.where(first_half, 0.0, pair).astype(BF16)

    q_spec, kv_spec, cc_spec, cr_spec = _fox_specs(nb)
    outs = pl.pallas_call(
        body,
        name="fox_bwd",
        grid=(bsz, B_PAIRS, nb),
        in_specs=[q_spec, kv_spec, kv_spec, q_spec, q_spec, cc_spec, cr_spec] + [HBM_SPEC] * n_x,
        out_specs=[q_spec, kv_spec, kv_spec, cr_spec] + [HBM_SPEC] * n_x,
        out_shape=[jax.ShapeDtypeStruct((t, B_WIDTH), BF16), jax.ShapeDtypeStruct((t, B_PAD_WIDTH), BF16),
                   jax.ShapeDtypeStruct((t, B_PAD_WIDTH), BF16),
                   jax.ShapeDtypeStruct((bsz * B_HEADS, nb, 1, BLOCK), F32)] + (exchange.out_shape if exchange else []),
        scratch_shapes=[pltpu.VMEM((nb, LANES, BLOCK), F32), pltpu.VMEM((nb, LANES, BLOCK), F32),
                        pltpu.VMEM((2, BLOCK, CHUNK), F32), pltpu.VMEM((2, BLOCK, CHUNK), F32),
                        pltpu.VMEM((2, BLOCK, CHUNK), BF16), pltpu.VMEM((2, BLOCK, CHUNK), BF16)]
        + (exchange.scratch if exchange else []),
        compiler_params=_cparams(("arbitrary",) * 3 if exchange else ("parallel", "parallel", "arbitrary")),
    )(q, k, v, o_exact, do, lse, c_row, *(exchange.ins if exchange else []))
    return outs[:4], outs[4:]


def _loss_head(h, final_w, target):
    bsz, l, d = h.shape
    nb = l // BLOCK

    def body(h_ref, w_ref, t_ref, loss_ref, dh_ref, dw_ref):
        b = pl.program_id(0)
        n = pl.program_id(1)

        @pl.when((b == 0) & (n == 0))
        def _():
            loss_ref[...] = jnp.zeros_like(loss_ref)
            dw_ref[...] = jnp.zeros_like(dw_ref)

        @pl.when(n == 0)
        def _():
            dh_ref[...] = jnp.zeros_like(dh_ref)

        @pl.when(n > 0)
        def _():
            hh = h_ref[0]
            w = w_ref[...]
            r = _rms_scale(hh)
            err = (hh * r) * w - t_ref[0]
            loss_ref[...] += 0.5 * jnp.sum(jnp.mean(err * err, axis=-1, keepdims=True), axis=0, keepdims=True)
            dy = err * (1.0 / d)
            dh, dw = _rms_bwd(dy, hh, w)
            dh_ref[0] = dh
            dw_ref[...] += dw

    return pl.pallas_call(
        body,
        name="loss_head",
        grid=(bsz, nb),
        in_specs=[
            pl.BlockSpec((1, BLOCK, d), lambda b, n: (b, n, 0)),
            pl.BlockSpec((1, d), lambda b, n: (0, 0)),
            pl.BlockSpec((1, BLOCK, d), lambda b, n: (b, jnp.maximum(n - 1, 0), 0)),
        ],
        out_specs=[
            pl.BlockSpec((1, 128), lambda b, n: (0, 0)),
            pl.BlockSpec((1, BLOCK, d), lambda b, n: (b, n, 0)),
            pl.BlockSpec((1, d), lambda b, n: (0, 0)),
        ],
        out_shape=[jax.ShapeDtypeStruct((1, 128), F32), jax.ShapeDtypeStruct((bsz, l, d), F32),
                   jax.ShapeDtypeStruct((1, d), F32)],
        compiler_params=_cparams(("arbitrary", "arbitrary")),
    )(h, final_w, target)


def _pad_tiles(w, src, heads, lane_slot, axis):
    pieces = []
    for h in range(heads):
        x = lax.slice_in_dim(w, src + HEAD_DIM * h, src + HEAD_DIM * (h + 1), axis=axis)
        z = jnp.zeros_like(x)
        pieces += [x, z] if lane_slot(h) == 0 else [z, x]
    return pieces


def _unpad_tiles(g, off, heads, lane_slot, axis):
    return [lax.slice_in_dim(g, off + LANES * h + HEAD_DIM * lane_slot(h),
                             off + LANES * h + HEAD_DIM * (lane_slot(h) + 1), axis=axis) for h in range(heads)]


A_SLOT = lambda h: h // A_GROUP
B_SLOT = lambda h: h % 2


def _layout_w_in(w):
    pad_f = jnp.zeros((w.shape[0], F_COLS - B_HEADS + TAIL_COLS), w.dtype)
    return jnp.concatenate(
        _pad_tiles(w, 0, A_HEADS, A_SLOT, 1) + [w[:, SRC_KA:SRC_KB]]
        + _pad_tiles(w, SRC_KB, B_HEADS, B_SLOT, 1) + _pad_tiles(w, SRC_VB, B_HEADS, B_SLOT, 1)
        + [w[:, SRC_GA:], w[:, SRC_F:SRC_GA], pad_f], axis=1)


def _unlayout_w_in(g):
    return jnp.concatenate(
        _unpad_tiles(g, OFF_QA, A_HEADS, A_SLOT, 1) + [g[:, OFF_KA:OFF_KB]]
        + _unpad_tiles(g, OFF_KB, B_HEADS, B_SLOT, 1) + _unpad_tiles(g, OFF_VB, B_HEADS, B_SLOT, 1)
        + [g[:, OFF_F:OFF_F + B_HEADS], g[:, OFF_GA:OFF_F]], axis=1)


def _local_step(x, target, meta, norms, b_forget, sinks, w, comm=None):
    n1, nmix, n2, nfin = norms
    w1i, w1o = w[:2]
    bsz, seq, d = x.shape
    l = PREFIX + seq
    nb = l // BLOCK
    t = bsz * l

    h0 = jnp.concatenate([jnp.zeros((bsz, N_PAD, d), F32),
                          jnp.broadcast_to(meta[None], (bsz, N_META, d)), x], axis=1).reshape(t, d)

    if comm is None:
        (h1, g1, u1), _ = _ffn_fwd(h0, n1, w1i, w1o)
        w_in, wa, wb, wo, w2i, w2o = w[2:]
    else:
        (h1, g1, u1), gathered = _ffn_fwd(h0, n1, w1i, w1o, comm.gather(GATHER_PROJ))
        w_in, = comm.gathered(GATHER_PROJ, gathered)
    wp = _layout_w_in(w_in)
    un, qa, ka, va, qb, kb, vb, ga, gb, f_logit = _proj_fwd(h1, nmix, wp)
    b_pad = jnp.concatenate([b_forget, jnp.zeros((1, F_COLS - B_HEADS), F32)], axis=1)
    c = _forget_cumsum(f_logit, b_pad, nb)
    c_heads = c[:, :B_HEADS].reshape(bsz, l, B_HEADS).transpose(0, 2, 1).reshape(bsz * B_HEADS, l)
    c_row = c_heads.reshape(bsz * B_HEADS, nb, 1, BLOCK)

    slopes = jnp.exp2(-8.0 * jnp.arange(1, A_HEADS + 1, dtype=F32) / A_HEADS)
    slope_rows = jnp.repeat(slopes.reshape(A_KV_HEADS, A_GROUP), BLOCK, axis=1)[:, :, None]
    sink_rows = jnp.repeat(sinks.reshape(A_KV_HEADS, A_GROUP), BLOCK, axis=1)[:, :, None]

    oa, lse_a = _swa_fwd(qa, ka, va, sink_rows, slope_rows, nb)
    if comm is None:
        (ob, ob_exact, lse_b), _ = _fox_fwd(qb, kb, vb, c_row, nb)
    else:
        (ob, ob_exact, lse_b), gathered = _fox_fwd(qb, kb, vb, c_row, nb, comm.gather(GATHER_LATE))
        wa, wb, wo, w2i, w2o = comm.gathered(GATHER_LATE, gathered)
    wa_p = jnp.concatenate(_pad_tiles(wa, 0, A_HEADS, A_SLOT, 0), axis=0)
    h2, mixed = _merge_fwd(h1, oa, ob, ga, gb, wa_p, wb, wo)
    (h3, g2, u2), _ = _ffn_fwd(h2, n2, w2i, w2o)
    loss, dh3, d_nfin = _loss_head(h3.reshape(bsz, l, d), nfin, target)

    (dh2, n2b, a2, dgu2, df2, dn2_parts), _ = _ffn_bwd(dh3.reshape(t, d), h2, n2, g2, u2, w2i, w2o)
    g_w2o = _tn_matmul(a2, df2, "grad_ffn2_w_out")
    g_w2i = _tn_matmul(n2b, dgu2, "grad_ffn2_w_in")

    dya, dyb, doa, dob, dga, dgb, dh2b = _merge_bwd(dh2, oa, ob, ga, gb, wa_p, wb, wo)
    g_wo = _tn_matmul(mixed, dh2b, "grad_w_out")
    g_wa = jnp.concatenate(_unpad_tiles(_tn_matmul(oa, dya, "grad_w_branch_a"), 0, A_HEADS, A_SLOT, 0), axis=0)
    g_wb = _tn_matmul(ob, dyb, "grad_w_branch_b")

    dqa, dka, dva, dsink_rows = _swa_bwd(qa, ka, va, doa, lse_a, sink_rows, slope_rows, nb)
    hosted = comm.scatter("ffn2", dict(ffn2_w_in=g_w2i, ffn2_w_out=g_w2o)) if comm else None
    (dqb, dkb, dvb, dc_row), pieces = _fox_bwd(qb, kb, vb, ob_exact, dob, lse_b, c_row, nb, hosted)
    if comm:
        comm.received("ffn2", pieces)
    dc = dc_row.reshape(bsz, B_HEADS, l).transpose(0, 2, 1).reshape(t, B_HEADS)
    dc = jnp.concatenate([dc, jnp.zeros((t, F_COLS - B_HEADS), F32)], axis=1)
    df_logit, db_parts = _forget_cumsum_bwd(dc, f_logit, b_pad, nb)

    dproj = jnp.concatenate([dqa, dka, dva, dqb, dkb, dvb, dga, dgb, df_logit,
                             jnp.zeros((t, TAIL_COLS), BF16)], axis=1)
    dh1, dnmix_parts = _proj_bwd(dh2, h1, nmix, dproj, wp)
    g_win = _unlayout_w_in(_tn_matmul(un, dproj, "grad_w_in"))

    hosted = comm.scatter("mixer", dict(w_in=g_win, w_branch_a=g_wa, w_branch_b=g_wb, w_out=g_wo)) if comm else None
    (dh0, n1b, a1, dgu1, df1, dn1_parts), pieces = _ffn_bwd(dh1, h0, n1, g1, u1, w1i, w1o, hosted)
    if comm:
        comm.received("mixer", pieces)
    g_w1o = _tn_matmul(a1, df1, "grad_ffn1_w_out")
    g_w1i = _tn_matmul(n1b, dgu1, "grad_ffn1_w_in")

    dh0 = dh0.reshape(bsz, l, d)
    grad_x = dh0[:, PREFIX:]
    small = dict(
        meta_tokens=jnp.sum(dh0[:, N_PAD:PREFIX], axis=0),
        ffn1_norm=jnp.sum(dn1_parts, axis=0),
        mix_norm=jnp.sum(dnmix_parts, axis=0),
        ffn2_norm=jnp.sum(dn2_parts, axis=0),
        final_norm=d_nfin,
        b_forget=jnp.sum(db_parts, axis=0)[:, :B_HEADS],
        attn_sinks=jnp.sum(dsink_rows.reshape(bsz, A_HEADS, BLOCK), axis=(0, 2)).reshape(1, A_HEADS),
    )
    big = dict(ffn1_w_in=g_w1i, ffn1_w_out=g_w1o, w_in=g_win, w_branch_a=g_wa, w_branch_b=g_wb,
               w_out=g_wo, ffn2_w_in=g_w2i, ffn2_w_out=g_w2o)
    return loss, grad_x, small, big


BIG = (
    ("ffn1_w_in", (D_MODEL, 5632), 1),
    ("ffn1_w_out", (2816, D_MODEL), 0),
    ("w_in", (D_MODEL, W_IN_COLS), 1),
    ("w_branch_a", (A_WIDTH, D_MODEL), 1),
    ("w_branch_b", (B_WIDTH, D_MODEL), 1),
    ("w_out", (D_MODEL, D_MODEL), 0),
    ("ffn2_w_in", (D_MODEL, 5632), 1),
    ("ffn2_w_out", (2816, D_MODEL), 0),
)
STACKED = "w_in"


def _coords():
    return lax.axis_index("x"), lax.axis_index("y"), lax.axis_index("c")


def _other_chips(x, y):
    return ((1 - x, y), (x, 1 - y), (1 - x, 1 - y))


def _chip_part(ref, name, shape, axis, k):
    if name == STACKED:
        return ref.at[k]
    size = shape[axis] // N_CHIPS
    start = pl.multiple_of(k * size, size)
    return ref.at[pl.ds(start, size), :] if axis == 0 else ref.at[:, pl.ds(start, size)]


def _full_shape(name, shape):
    return (N_CHIPS, shape[0], shape[1] // N_CHIPS) if name == STACKED else shape


class _Exchange:
    def __init__(self, ins, out_shape, n_sems, ops):
        self.ins, self.out_shape, self.n_sems, self.ops = list(ins), list(out_shape), n_sems, ops

    @property
    def scratch(self):
        return [pltpu.SemaphoreType.DMA((self.n_sems,)), pltpu.SemaphoreType.DMA((self.n_sems,))]


SEMS_PER_GATHER = 7


def _gather_exchange(shards, table):
    n = len(table)

    def ops(ins, outs, send_sems, recv_sems):
        x, y, c = _coords()
        mine = 2 * x + y
        sibling = (x, y, 1 - c)
        chips = _other_chips(x, y)
        slots = [2 * chip[0] + chip[1] for chip in chips]

        def part(i, k):
            name, shape, axis = table[i][:3]
            return _chip_part(outs[i], name, shape, axis, k)

        def half(ref, h):
            rows = ref.shape[0] // 2
            return ref.at[pl.ds(pl.multiple_of(h * rows, rows), rows), :]

        def own(i):
            sem = SEMS_PER_GATHER * i
            return pltpu.make_async_remote_copy(ins[i], part(i, mine), send_sems.at[sem], recv_sems.at[sem],
                                                device_id=sibling, device_id_type=MESH_ID)

        def fetch(i, j, slot):
            sem = SEMS_PER_GATHER * i + 1 + j
            if table[i][4]:
                src, dst = half(ins[i], c), half(part(i, slot), c)
            else:
                src, dst = ins[i], part(i, slot)
            return pltpu.make_async_remote_copy(src, dst, send_sems.at[sem], recv_sems.at[sem],
                                                device_id=(chips[j][0], chips[j][1], c), device_id_type=MESH_ID)

        def forward(i, j, h):
            sem = SEMS_PER_GATHER * i + 4 + j
            region = half(part(i, slots[j]), h)
            return pltpu.make_async_remote_copy(region, region, send_sems.at[sem], recv_sems.at[sem],
                                                device_id=sibling, device_id_type=MESH_ID)

        def start():
            for i in range(n):
                for j in range(3):
                    fetch(i, j, mine).start()
            for i in range(n):
                own(i).start()

        def relay():
            for i in range(n):
                for j in range(3):
                    fetch(i, j, slots[j]).wait_recv()
                    if table[i][4]:
                        forward(i, j, c).start()

        def finish():
            for i in range(n):
                own(i).wait()
                for j in range(3):
                    if table[i][4]:
                        forward(i, j, 1 - c).wait_recv()
                        forward(i, j, c).wait_send()
                    fetch(i, j, mine).wait_send()

        return start, relay, finish

    out_shape = [jax.ShapeDtypeStruct(_full_shape(name, shape), dtype) for name, shape, _, dtype, _ in table]
    return _Exchange(shards, out_shape, SEMS_PER_GATHER * n, ops)


def _run_exchange(exchange, name):
    n = len(exchange.ins)

    def body(*refs):
        start, relay, finish = exchange.ops(refs[:n], refs[n:2 * n], *refs[2 * n:])
        start()
        relay()
        finish()

    return pl.pallas_call(
        body,
        name=name,
        in_specs=[HBM_SPEC] * n,
        out_specs=[HBM_SPEC] * n,
        out_shape=exchange.out_shape,
        scratch_shapes=exchange.scratch,
    )(*exchange.ins)


def _host_exchange(exchange, in_refs, out_refs, sem_refs, first, middle, last):
    start, relay, finish = exchange.ops(in_refs, out_refs, *sem_refs)
    pl.when(first)(start)
    pl.when(middle)(relay)
    pl.when(last)(finish)


def _halves_view(name, shape, axis):
    r, c = shape
    if name == STACKED:
        return (N_CHIPS, 2, r // 2, c // N_CHIPS), lambda ref, h: ref.at[:, h]
    if axis == 1:
        return (2, r // 2, c), lambda ref, h: ref.at[h]
    return (N_CHIPS, 2, r // N_CHIPS // 2, c), lambda ref, h: ref.at[:, h]


def _exchange_halves(grads, entries, tag):
    n_w = len(entries)
    views = [_halves_view(*entry) for entry in entries]

    def body(*refs):
        ins, outs = refs[:n_w], refs[n_w:2 * n_w]
        send_sems, recv_sems = refs[2 * n_w:]
        x, y, c = _coords()
        copies = [pltpu.make_async_remote_copy(views[i][1](ins[i], 1 - c), outs[i], send_sems.at[i], recv_sems.at[i],
                                               device_id=(x, y, 1 - c), device_id_type=MESH_ID) for i in range(n_w)]
        for cp in copies:
            cp.start()
        for cp in copies:
            cp.wait()

    half_shape = lambda v: tuple(d for i, d in enumerate(v) if i != (1 if len(v) == 4 else 0))
    return pl.pallas_call(
        body,
        name="exchange_halves_" + tag,
        in_specs=[HBM_SPEC] * n_w,
        out_specs=[HBM_SPEC] * n_w,
        out_shape=[jax.ShapeDtypeStruct(half_shape(v[0]), F32) for v in views],
        scratch_shapes=[pltpu.SemaphoreType.DMA((n_w,)), pltpu.SemaphoreType.DMA((n_w,))],
    )(*[g.reshape(v[0]) for g, v in zip(grads, views)])


def _add_sibling(g_view, recv, c, name):
    shape = recv.shape
    if len(shape) == 2:
        tr = _tile(shape[0], 128, 16)
        grid = (shape[0] // tr,)
        g_spec = pl.BlockSpec((None, tr, shape[1]), lambda i, c_ref: (c_ref[0], i, 0))
        r_spec = pl.BlockSpec((tr, shape[1]), lambda i, c_ref: (i, 0))
    else:
        tr = _tile(shape[1], 256, 16)
        grid = (N_CHIPS, shape[1] // tr)
        g_spec = pl.BlockSpec((None, None, tr, shape[2]), lambda k, i, c_ref: (k, c_ref[0], i, 0))
        r_spec = pl.BlockSpec((None, tr, shape[2]), lambda k, i, c_ref: (k, i, 0))

    def body(c_ref, g_ref, r_ref, o_ref):
        o_ref[...] = (g_ref[...] + r_ref[...]).astype(BF16)

    return pl.pallas_call(
        body,
        name="add_sibling_" + name,
        grid_spec=pltpu.PrefetchScalarGridSpec(num_scalar_prefetch=1, grid=grid, in_specs=[g_spec, r_spec],
                                               out_specs=r_spec),
        out_shape=jax.ShapeDtypeStruct(shape, BF16),
        compiler_params=_cparams(("parallel",) * len(grid)),
    )(c, g_view, recv)


def _piece_of(ref, name, axis, k):
    if name == STACKED or axis == 0:
        return ref.at[k]
    size = ref.shape[1] // N_CHIPS
    return ref.at[:, pl.ds(pl.multiple_of(k * size, size), size)]


def _piece_shape(name, shape, axis):
    r, c = shape
    return (r // 2, c // N_CHIPS) if (axis == 1) else (r // N_CHIPS // 2, c)


def _scatter_exchange(partials, entries):
    n_w = len(entries)

    def ops(ins, outs, send_sems, recv_sems):
        x, y, c = _coords()
        chips = _other_chips(x, y)
        copies = []
        for i, (name, _, axis) in enumerate(entries):
            for j, chip in enumerate(chips):
                sem = 3 * i + j
                copies.append(pltpu.make_async_remote_copy(
                    _piece_of(ins[i], name, axis, 2 * chip[0] + chip[1]), outs[i].at[j], send_sems.at[sem],
                    recv_sems.at[sem], device_id=(chip[0], chip[1], c), device_id_type=MESH_ID))

        def start():
            for cp in copies:
                cp.start()

        def finish():
            for cp in copies:
                cp.wait()

        return start, lambda: None, finish

    out_shape = [jax.ShapeDtypeStruct((3,) + _piece_shape(*entry), BF16) for entry in entries]
    return _Exchange(partials, out_shape, 3 * n_w, ops)


def _add_chips(partial, recv, mine, name, axis):
    rows, cols = recv.shape[1:]
    tr = _tile(rows, 256, 16)
    if name == STACKED or axis == 0:
        p_spec = pl.BlockSpec((None, tr, cols), lambda i, k_ref: (k_ref[0], i, 0))
    else:
        p_spec = pl.BlockSpec((tr, cols), lambda i, k_ref: (i, k_ref[0]))

    def body(k_ref, p_ref, r_ref, o_ref):
        f32 = lambda a: a.astype(F32)
        o_ref[...] = ((f32(p_ref[...]) + f32(r_ref[0])) + f32(r_ref[1])) + f32(r_ref[2])

    return pl.pallas_call(
        body,
        name="add_chips_" + name,
        grid_spec=pltpu.PrefetchScalarGridSpec(
            num_scalar_prefetch=1, grid=(rows // tr,),
            in_specs=[p_spec, pl.BlockSpec((3, tr, cols), lambda i, k_ref: (0, i, 0))],
            out_specs=pl.BlockSpec((tr, cols), lambda i, k_ref: (i, 0))),
        out_shape=jax.ShapeDtypeStruct((rows, cols), F32),
        compiler_params=_cparams(("parallel",)),
    )(mine, partial, recv)


def _share_with_sibling(halves):
    n_w = len(halves)

    def body(*refs):
        ins, outs = refs[:n_w], refs[n_w:2 * n_w]
        send_sems, recv_sems = refs[2 * n_w:]
        x, y, c = _coords()
        copies = [pltpu.make_async_remote_copy(ins[i], outs[i], send_sems.at[i], recv_sems.at[i],
                                               device_id=(x, y, 1 - c), device_id_type=MESH_ID) for i in range(n_w)]
        for cp in copies:
            cp.start()
        for cp in copies:
            cp.wait()

    return pl.pallas_call(
        body,
        name="share_with_sibling",
        in_specs=[HBM_SPEC] * n_w,
        out_specs=[HBM_SPEC] * n_w,
        out_shape=[jax.ShapeDtypeStruct(h.shape, F32) for h in halves],
        scratch_shapes=[pltpu.SemaphoreType.DMA((n_w,)), pltpu.SemaphoreType.DMA((n_w,))],
    )(*halves)


SMALL_ROWS = 168


def _all_reduce_small(buf):
    def body(b_ref, out_ref, gathered, send_sems, recv_sems):
        x, y, c = _coords()
        me = 4 * x + 2 * y + c
        peers = [(x ^ fx, y ^ fy, c ^ fc) for fx in (0, 1) for fy in (0, 1) for fc in (0, 1)][1:]

        def copy(j, slot, dev):
            return pltpu.make_async_remote_copy(b_ref, gathered.at[slot], send_sems.at[j], recv_sems.at[j],
                                                device_id=dev, device_id_type=MESH_ID)

        for j, dev in enumerate(peers):
            copy(j, me, dev).start()
        gathered[me] = b_ref[...]
        for j, dev in enumerate(peers):
            copy(j, 4 * dev[0] + 2 * dev[1] + dev[2], dev).wait()
        acc = gathered[0]
        for d in range(1, N_DEV):
            acc = acc + gathered[d]
        out_ref[...] = acc

    return pl.pallas_call(
        body,
        name="all_reduce_small",
        in_specs=[VMEM_SPEC],
        out_specs=VMEM_SPEC,
        out_shape=jax.ShapeDtypeStruct(buf.shape, F32),
        scratch_shapes=[pltpu.VMEM((N_DEV,) + buf.shape, F32), pltpu.SemaphoreType.DMA((N_DEV - 1,)),
                        pltpu.SemaphoreType.DMA((N_DEV - 1,))],
    )(buf)


def _adamw(w, g, m, v):
    r, c = w.shape
    tr = _tile(r, max(8, (5 << 19) // (4 * c)), 8)

    def body(w_ref, g_ref, m_ref, v_ref, d_ref, mo_ref, vo_ref):
        gg = g_ref[...]
        mm = ADAM_B1 * m_ref[...] + (1.0 - ADAM_B1) * gg
        vv = ADAM_B2 * v_ref[...] + (1.0 - ADAM_B2) * (gg * gg)
        m_hat = mm / (1.0 - ADAM_B1 ** ADAM_STEP)
        v_hat = vv / (1.0 - ADAM_B2 ** ADAM_STEP)
        d_ref[...] = -ADAM_LR * (m_hat / (jnp.sqrt(v_hat) + ADAM_EPS) + ADAM_WD * w_ref[...])
        mo_ref[...] = mm
        vo_ref[...] = vv

    spec = pl.BlockSpec((tr, c), lambda i: (i, 0))
    return pl.pallas_call(
        body,
        name="adamw",
        grid=(r // tr,),
        in_specs=[spec] * 4,
        out_specs=[spec] * 3,
        out_shape=[jax.ShapeDtypeStruct((r, c), F32)] * 3,
        compiler_params=_cparams(("parallel",)),
    )(w, g, m, v)


def _adamw_halves(w, own, other, m, v, c, name):
    r, cols = w.shape
    half = r // 2
    tr = _tile(half, 256, 8)
    nt = half // tr
    whole = pl.BlockSpec((tr, cols), lambda h, i, c_ref: (h * nt + i, 0))
    part = pl.BlockSpec((tr, cols), lambda h, i, c_ref: (i, 0))

    def body(c_ref, w_ref, own_ref, other_ref, m_ref, v_ref, g_ref, d_ref, mo_ref, vo_ref):
        gg = jnp.where(pl.program_id(0) == c_ref[0], own_ref[...], other_ref[...])
        g_ref[...] = gg
        mm = ADAM_B1 * m_ref[...] + (1.0 - ADAM_B1) * gg
        vv = ADAM_B2 * v_ref[...] + (1.0 - ADAM_B2) * (gg * gg)
        m_hat = mm / (1.0 - ADAM_B1 ** ADAM_STEP)
        v_hat = vv / (1.0 - ADAM_B2 ** ADAM_STEP)
        d_ref[...] = -ADAM_LR * (m_hat / (jnp.sqrt(v_hat) + ADAM_EPS) + ADAM_WD * w_ref[...])
        mo_ref[...] = mm
        vo_ref[...] = vv

    return pl.pallas_call(
        body,
        name="adamw_" + name,
        grid_spec=pltpu.PrefetchScalarGridSpec(
            num_scalar_prefetch=1, grid=(2, nt),
            in_specs=[whole, part, part, whole, whole], out_specs=[whole] * 4),
        out_shape=[jax.ShapeDtypeStruct((r, cols), F32)] * 4,
        compiler_params=_cparams(("parallel", "parallel")),
    )(c, w, own, other, m, v)


GATHER_FIRST = ("ffn1_w_in", "ffn1_w_out")
GATHER_PROJ = ("w_in",)
GATHER_LATE = ("w_branch_a", "w_branch_b", "w_out", "ffn2_w_in", "ffn2_w_out")


class _Comm:
    def __init__(self, shards, c_arr, mine_arr):
        self.shards, self.c, self.mine = shards, c_arr, mine_arr
        self.groups, self.halves = {}, {}
        self.by_name = {entry[0]: entry for entry in BIG}

    def gather(self, names):
        table = [self.by_name[n] + (BF16, True) for n in names]
        return _gather_exchange([self.shards[n] for n in names], table)

    def gathered(self, names, outs):
        return [o.transpose(1, 0, 2).reshape(D_MODEL, W_IN_COLS) if n == STACKED else o for n, o in zip(names, outs)]

    def scatter(self, tag, grads):
        entries = [self.by_name[n] for n in grads]
        arrays = [g.reshape(D_MODEL, N_CHIPS, W_IN_COLS // N_CHIPS).transpose(1, 0, 2) if n == STACKED else g
                  for n, g in grads.items()]
        views = [_halves_view(*entry) for entry in entries]
        received = _exchange_halves(arrays, entries, tag)
        partials = [_add_sibling(g.reshape(v[0]), r, self.c, name)
                    for g, v, r, (name, _, _) in zip(arrays, views, received, entries)]
        self.groups[tag] = (entries, partials)
        return _scatter_exchange(partials, entries)

    def received(self, tag, pieces):
        entries, partials = self.groups[tag]
        for p, r, (name, _, axis) in zip(partials, pieces, entries):
            self.halves[name] = _add_chips(p, r, self.mine, name, axis)

    def finish(self):
        names = [n for n, _, _ in BIG]
        own = [self.halves[n] for n in names]
        return dict(zip(names, zip(own, _share_with_sibling(own))))


def kernel(x, meta_tokens, ffn1_norm, ffn1_w_in, ffn1_w_out, mix_norm, w_in, b_forget, attn_sinks, w_branch_a, w_branch_b, w_out, ffn2_norm, ffn2_w_in, ffn2_w_out, final_norm, loss_target, m_meta_tokens, m_ffn1_norm, m_ffn1_w_in, m_ffn1_w_out, m_mix_norm, m_w_in, m_b_forget, m_attn_sinks, m_w_branch_a, m_w_branch_b, m_w_out, m_ffn2_norm, m_ffn2_w_in, m_ffn2_w_out, m_final_norm, v_meta_tokens, v_ffn1_norm, v_ffn1_w_in, v_ffn1_w_out, v_mix_norm, v_w_in, v_b_forget, v_attn_sinks, v_w_branch_a, v_w_branch_b, v_w_out, v_ffn2_norm, v_ffn2_w_in, v_ffn2_w_out, v_final_norm):
    given = dict(locals())
    names = ["meta_tokens", "ffn1_norm", "ffn1_w_in", "ffn1_w_out", "mix_norm", "w_in", "b_forget", "attn_sinks",
             "w_branch_a", "w_branch_b", "w_out", "ffn2_norm", "ffn2_w_in", "ffn2_w_out", "final_norm"]
    big_names = [n for n, _, _ in BIG]
    cx, cy, cc = _coords()
    c_arr = cc.reshape(1).astype(jnp.int32)
    mine_arr = (2 * cx + cy).reshape(1).astype(jnp.int32)

    comm = _Comm({n: given[n][0].astype(BF16) for n in big_names}, c_arr, mine_arr)
    table = [comm.by_name[n] + (BF16, True) for n in GATHER_FIRST] + [("meta_tokens", (N_META, D_MODEL), 1, F32, False)]
    first = _gather_exchange([comm.shards[n] for n in GATHER_FIRST] + [meta_tokens], table)
    w1i, w1o, meta_full = _run_exchange(first, "gather_first")
    norms = (ffn1_norm, mix_norm, ffn2_norm, final_norm.reshape(1, D_MODEL))
    loss, grad_x, small, big = _local_step(x, loss_target, meta_full, norms, b_forget, attn_sinks, (w1i, w1o), comm)

    last = comm.scatter("ffn1", dict(ffn1_w_in=big["ffn1_w_in"], ffn1_w_out=big["ffn1_w_out"]))
    comm.received("ffn1", _run_exchange(last, "scatter_chip_sums"))
    grad_halves = comm.finish()
    grads = {}

    pad_lanes = lambda a: jnp.concatenate([a, jnp.zeros((1, LANES - a.shape[1]), F32)], axis=1)
    buf = jnp.concatenate([
        small["meta_tokens"].reshape(128, LANES),
        small["ffn1_norm"].reshape(8, LANES), small["mix_norm"].reshape(8, LANES),
        small["ffn2_norm"].reshape(8, LANES), small["final_norm"].reshape(8, LANES),
        loss, pad_lanes(small["b_forget"]), pad_lanes(small["attn_sinks"]),
        jnp.zeros((SMALL_ROWS - 163, LANES), F32)], axis=0)
    red = _all_reduce_small(buf)
    meta_cols = red[:128].reshape(N_META, D_MODEL)
    grads["meta_tokens"] = lax.dynamic_slice_in_dim(meta_cols, (2 * cx + cy) * (D_MODEL // N_CHIPS),
                                                    D_MODEL // N_CHIPS, axis=1)
    grads["ffn1_norm"] = red[128:136].reshape(1, D_MODEL)
    grads["mix_norm"] = red[136:144].reshape(1, D_MODEL)
    grads["ffn2_norm"] = red[144:152].reshape(1, D_MODEL)
    grads["final_norm"] = red[152:160].reshape(1, D_MODEL)
    loss_out = red[160, 0]
    grads["b_forget"] = red[161:162, :B_HEADS]
    grads["attn_sinks"] = red[162:163, :A_HEADS]

    out_g, out_d, out_m, out_v = [], [], [], []
    for n in names:
        w_full = given[n]
        shape = w_full.shape
        two_d = (lambda a: a.reshape(shape[-2], shape[-1])) if len(shape) >= 2 else (lambda a: a.reshape(1, shape[0]))
        if n in grad_halves:
            own, other = grad_halves[n]
            g2, d2, m2, v2 = _adamw_halves(two_d(w_full), own, other, two_d(given["m_" + n]),
                                           two_d(given["v_" + n]), c_arr, n)
        else:
            g2 = two_d(grads[n])
            d2, m2, v2 = _adamw(two_d(w_full), g2, two_d(given["m_" + n]), two_d(given["v_" + n]))
        out_g.append(g2.reshape(shape))
        out_d.append(d2.reshape(shape))
        out_m.append(m2.reshape(shape))
        out_v.append(v2.reshape(shape))
    return (loss_out, grad_x, *out_g, *out_d, *out_m, *out_v)
```

```python
import jax
import jax.numpy as jnp
from jax import lax
from jax.experimental import pallas as pl
from jax.experimental.pallas import tpu as pltpu

F32 = jnp.float32
BF16 = jnp.bfloat16

D_MODEL = 1024
N_META = 16
BLOCK = 128
LANES = 128
PREFIX = BLOCK
N_PAD = PREFIX - N_META
HEAD_DIM = 64
A_HEADS = 8
A_KV_HEADS = 2
A_GROUP = 4
B_HEADS = 8
B_PAIRS = B_HEADS // 2
A_WIDTH = A_HEADS * HEAD_DIM
A_KV_WIDTH = A_KV_HEADS * HEAD_DIM
B_WIDTH = B_HEADS * HEAD_DIM
W_IN_COLS = A_WIDTH + 2 * A_KV_WIDTH + 3 * B_WIDTH + B_HEADS + 2 * D_MODEL
SRC_KA = A_WIDTH
SRC_VA = SRC_KA + A_KV_WIDTH
SRC_QB = SRC_VA + A_KV_WIDTH
SRC_KB = SRC_QB + B_WIDTH
SRC_VB = SRC_KB + B_WIDTH
SRC_F = SRC_VB + B_WIDTH
SRC_GA = SRC_F + B_HEADS
SRC_GB = SRC_GA + D_MODEL
A_PAD_WIDTH = A_HEADS * LANES
B_PAD_WIDTH = B_HEADS * LANES
F_COLS = LANES
OFF_QA = 0
OFF_KA = OFF_QA + A_PAD_WIDTH
OFF_VA = OFF_KA + A_KV_WIDTH
OFF_QB = OFF_VA + A_KV_WIDTH
OFF_KB = OFF_QB + B_WIDTH
OFF_VB = OFF_KB + B_PAD_WIDTH
OFF_GA = OFF_VB + B_PAD_WIDTH
OFF_GB = OFF_GA + D_MODEL
OFF_F = OFF_GB + D_MODEL
TAIL_COLS = 128
P_COLS = OFF_F + F_COLS + TAIL_COLS
EPS = 1e-6
NEG = -1e30
SCALE = HEAD_DIM ** -0.5
KEY_BLOCKS = 4

ADAM_LR = 0.001
ADAM_B1 = 0.9
ADAM_B2 = 0.999
ADAM_EPS = 1e-08
ADAM_WD = 0.01
ADAM_STEP = 10

N_CHIPS = 4
N_DEV = 8
VMEM_LIMIT = 56 * 1024 * 1024

NT_DIMS = (((1,), (1,)), ((), ()))
TN_DIMS = (((0,), (0,)), ((), ()))
MESH_ID = pl.DeviceIdType.MESH
HBM_SPEC = pl.BlockSpec(memory_space=pltpu.HBM)
VMEM_SPEC = pl.BlockSpec(memory_space=pltpu.VMEM)


def _tile(n, target, mult=16):
    best = None
    for t in range(mult, min(n, target) + 1, mult):
        if n % t == 0:
            best = t
    return best if best is not None else n


def _cparams(sem):
    return pltpu.CompilerParams(dimension_semantics=sem, vmem_limit_bytes=VMEM_LIMIT)


def _rms_scale(h):
    return lax.rsqrt(jnp.mean(h * h, axis=-1, keepdims=True) + EPS)


def _rms_bwd(dn, h, w):
    r = _rms_scale(h)
    dw = jnp.sum(dn * (h * r), axis=0, keepdims=True)
    z = dn * w
    dh = r * z - h * ((r * r * r) * jnp.mean(z * h, axis=-1, keepdims=True))
    return dh, dw


def _ffn_fwd(h, norm_w, w_in, w_out, exchange=None):
    t, d = h.shape
    f = w_out.shape[0]
    tm = _tile(t, 272)
    tc = _tile(f, 256, 128)
    nj = f // tc
    ni = t // tm
    n_x = len(exchange.ins) if exchange else 0

    def body(*refs):
        h_ref, nw_ref, wi_ref, wo_ref = refs[:4]
        hout_ref, g_ref, u_ref = refs[4 + n_x:7 + n_x]
        a_scr = refs[7 + 2 * n_x]
        i = pl.program_id(0)
        if exchange:
            _host_exchange(exchange, refs[4:4 + n_x], refs[7 + n_x:7 + 2 * n_x], refs[8 + 2 * n_x:],
                           i == 0, i == ni - 2, i == ni - 1)
        hh = h_ref[...]
        n = ((hh * _rms_scale(hh)) * nw_ref[...]).astype(BF16)
        for j in range(nj):
            cols = slice(j * tc, (j + 1) * tc)
            g = jnp.dot(n, wi_ref[:, j * tc:(j + 1) * tc], preferred_element_type=F32)
            u = jnp.dot(n, wi_ref[:, f + j * tc:f + (j + 1) * tc], preferred_element_type=F32)
            g_ref[:, cols] = g
            u_ref[:, cols] = u
            a_scr[:, cols] = ((g * jax.nn.sigmoid(g)) * u).astype(BF16)
        hout_ref[...] = hh + 0.5 * jnp.dot(a_scr[...], wo_ref[...], preferred_element_type=F32)

    resident = lambda a: pl.BlockSpec(a.shape, lambda i: (0, 0), pipeline_mode=pl.Buffered(1))
    row = lambda w: pl.BlockSpec((tm, w), lambda i: (i, 0))
    outs = pl.pallas_call(
        body,
        name="ffn_fwd",
        grid=(ni,),
        in_specs=[row(d), pl.BlockSpec((1, d), lambda i: (0, 0)), resident(w_in), resident(w_out)] + [HBM_SPEC] * n_x,
        out_specs=[row(d), row(f), row(f)] + [HBM_SPEC] * n_x,
        out_shape=[
            jax.ShapeDtypeStruct((t, d), F32),
            jax.ShapeDtypeStruct((t, f), F32),
            jax.ShapeDtypeStruct((t, f), F32),
        ] + (exchange.out_shape if exchange else []),
        scratch_shapes=[pltpu.VMEM((tm, f), BF16)] + (exchange.scratch if exchange else []),
        compiler_params=_cparams(("arbitrary",) if exchange else ("parallel",)),
    )(h, norm_w, w_in, w_out, *(exchange.ins if exchange else []))
    return outs[:3], outs[3:]


def _ffn_bwd(dh_out, h, norm_w, g, u, w_in, w_out, exchange=None):
    t, d = h.shape
    f = w_out.shape[0]
    tm = _tile(t, 272)
    tc = _tile(f, 256, 128)
    nj = f // tc
    ni = t // tm
    n_x = len(exchange.ins) if exchange else 0

    def body(*refs):
        dho_ref, h_ref, nw_ref, g_ref, u_ref, wi_ref, wo_ref = refs[:7]
        dhin_ref, n_ref, a_ref, dgu_ref, df_ref, dnw_ref = refs[7 + n_x:13 + n_x]
        i = pl.program_id(0)
        if exchange:
            _host_exchange(exchange, refs[7:7 + n_x], refs[13 + n_x:13 + 2 * n_x], refs[13 + 2 * n_x:],
                           i == 0, i == ni - 1, i == ni - 1)
        hh = h_ref[...]
        nw = nw_ref[...]
        n_ref[...] = ((hh * _rms_scale(hh)) * nw).astype(BF16)
        dho = dho_ref[...]
        df = (0.5 * dho).astype(BF16)
        df_ref[...] = df
        for j in range(nj):
            cols = slice(j * tc, (j + 1) * tc)
            da = lax.dot_general(df, wo_ref[cols, :], NT_DIMS, preferred_element_type=F32)
            gg = g_ref[:, cols]
            uu = u_ref[:, cols]
            sig = jax.nn.sigmoid(gg)
            sl = gg * sig
            a_ref[:, cols] = (sl * uu).astype(BF16)
            dgu_ref[0, :, cols] = ((da * uu) * (sig * (1.0 + gg * (1.0 - sig)))).astype(BF16)
            dgu_ref[1, :, cols] = (da * sl).astype(BF16)
        dn = (lax.dot_general(dgu_ref[0], wi_ref[:, :f], NT_DIMS, preferred_element_type=F32)
              + lax.dot_general(dgu_ref[1], wi_ref[:, f:], NT_DIMS, preferred_element_type=F32))
        dh, dw = _rms_bwd(dn, hh, nw)
        dhin_ref[...] = dho + dh
        dnw_ref[0] = dw

    resident = lambda a: pl.BlockSpec(a.shape, lambda i: (0, 0), pipeline_mode=pl.Buffered(1))
    row = lambda w: pl.BlockSpec((tm, w), lambda i: (i, 0))
    outs = pl.pallas_call(
        body,
        name="ffn_bwd",
        grid=(ni,),
        in_specs=[row(d), row(d), pl.BlockSpec((1, d), lambda i: (0, 0)), row(f), row(f),
                  resident(w_in), resident(w_out)] + [HBM_SPEC] * n_x,
        out_specs=[row(d), row(d), row(f), pl.BlockSpec((2, tm, f), lambda i: (0, i, 0)), row(d),
                   pl.BlockSpec((1, 1, d), lambda i: (i, 0, 0))] + [HBM_SPEC] * n_x,
        out_shape=[
            jax.ShapeDtypeStruct((t, d), F32),
            jax.ShapeDtypeStruct((t, d), BF16),
            jax.ShapeDtypeStruct((t, f), BF16),
            jax.ShapeDtypeStruct((2, t, f), BF16),
            jax.ShapeDtypeStruct((t, d), BF16),
            jax.ShapeDtypeStruct((ni, 1, d), F32),
        ] + (exchange.out_shape if exchange else []),
        scratch_shapes=exchange.scratch if exchange else [],
        compiler_params=_cparams(("arbitrary",) if exchange else ("parallel",)),
    )(dh_out, h, norm_w, g, u, w_in, w_out, *(exchange.ins if exchange else []))
    return outs[:6], outs[6:]


def _tn_matmul(a, b, name):
    t, k = a.shape
    split = b.ndim == 3
    n = 2 * b.shape[2] if split else b.shape[1]
    tk = _tile(k, 512, 128)
    tn = _tile(b.shape[-1], 1408, 128)
    per_half = b.shape[-1] // tn

    def body(a_ref, b_ref, o_ref):
        o_ref[...] = lax.dot_general(a_ref[...], b_ref[...], TN_DIMS, preferred_element_type=F32)

    if split:
        b_spec = pl.BlockSpec((None, t, tn), lambda i, j: (j // per_half, 0, j % per_half))
    else:
        b_spec = pl.BlockSpec((t, tn), lambda i, j: (0, j))
    return pl.pallas_call(
        body,
        name=name,
        grid=(k // tk, n // tn),
        in_specs=[pl.BlockSpec((t, tk), lambda i, j: (0, i)), b_spec],
        out_specs=pl.BlockSpec((tk, tn), lambda i, j: (i, j)),
        out_shape=jax.ShapeDtypeStruct((k, n), F32),
        compiler_params=_cparams(("parallel", "parallel")),
    )(a, b)


PROJ_PARTS = (
    (OFF_QA, A_PAD_WIDTH, True), (OFF_KA, A_KV_WIDTH, True), (OFF_VA, A_KV_WIDTH, True),
    (OFF_QB, B_WIDTH, True), (OFF_KB, B_PAD_WIDTH, True), (OFF_VB, B_PAD_WIDTH, True),
    (OFF_GA, D_MODEL, False), (OFF_GB, D_MODEL, False), (OFF_F, F_COLS, False),
)


def _proj_fwd(h, norm_w, w_p):
    t, d = h.shape
    tm = _tile(t, 272)

    def body(h_ref, nw_ref, w_ref, u_ref, *part_refs):
        hh = h_ref[...]
        un = ((hh * _rms_scale(hh)) * nw_ref[...]).astype(BF16)
        u_ref[...] = un
        for (off, width, _), p_ref in zip(PROJ_PARTS, part_refs):
            p_ref[...] = jnp.dot(un, w_ref[:, off:off + width], preferred_element_type=F32).astype(p_ref.dtype)

    row = lambda w: pl.BlockSpec((tm, w), lambda i: (i, 0))
    return pl.pallas_call(
        body,
        name="proj_fwd",
        grid=(t // tm,),
        in_specs=[row(d), pl.BlockSpec((1, d), lambda i: (0, 0)), pl.BlockSpec(w_p.shape, lambda i: (0, 0))],
        out_specs=[row(d)] + [row(width) for _, width, _ in PROJ_PARTS],
        out_shape=[jax.ShapeDtypeStruct((t, d), BF16)]
        + [jax.ShapeDtypeStruct((t, width), BF16 if is_bf else F32) for _, width, is_bf in PROJ_PARTS],
        compiler_params=_cparams(("parallel",)),
    )(h, norm_w, w_p)


def _proj_bwd(dh_out, h, norm_w, dproj, w_p, exchange=None):
    t, d = h.shape
    n = w_p.shape[1]
    tm = _tile(t, 272)
    ni = t // tm
    n_x = len(exchange.ins) if exchange else 0

    def body(*refs):
        dho_ref, h_ref, nw_ref, dp_ref, w_ref = refs[:5]
        dhin_ref, dnw_ref = refs[5 + n_x:7 + n_x]
        if exchange:
            i = pl.program_id(0)
            _host_exchange(exchange, refs[5:5 + n_x], refs[7 + n_x:7 + 2 * n_x], refs[7 + 2 * n_x:],
                           i == 0, i == ni - 1, i == ni - 1)
        dn = lax.dot_general(dp_ref[...], w_ref[...], NT_DIMS, preferred_element_type=F32)
        dh, dw = _rms_bwd(dn, h_ref[...], nw_ref[...])
        dhin_ref[...] = dho_ref[...] + dh
        dnw_ref[0] = dw

    row = lambda w: pl.BlockSpec((tm, w), lambda i: (i, 0))
    outs = pl.pallas_call(
        body,
        name="proj_bwd",
        grid=(ni,),
        in_specs=[row(d), row(d), pl.BlockSpec((1, d), lambda i: (0, 0)), row(n),
                  pl.BlockSpec(w_p.shape, lambda i: (0, 0), pipeline_mode=pl.Buffered(1))] + [HBM_SPEC] * n_x,
        out_specs=[row(d), pl.BlockSpec((1, 1, d), lambda i: (i, 0, 0))] + [HBM_SPEC] * n_x,
        out_shape=[jax.ShapeDtypeStruct((t, d), F32), jax.ShapeDtypeStruct((ni, 1, d), F32)]
        + (exchange.out_shape if exchange else []),
        scratch_shapes=exchange.scratch if exchange else [],
        compiler_params=_cparams(("arbitrary",) if exchange else ("parallel",)),
    )(dh_out, h, norm_w, dproj, w_p, *(exchange.ins if exchange else []))
    return outs[:2], outs[2:]


def _merge_fwd(h, oa, ob, ga, gb, wa, wb, wo):
    t, d = h.shape
    tm = _tile(t, 544)

    def body(h_ref, oa_ref, ob_ref, ga_ref, gb_ref, wa_ref, wb_ref, wo_ref, hout_ref, mix_ref):
        ya = jnp.dot(oa_ref[...], wa_ref[...], preferred_element_type=F32)
        yb = jnp.dot(ob_ref[...], wb_ref[...], preferred_element_type=F32)
        mixed = (jax.nn.sigmoid(ga_ref[...]) * ya + jax.nn.sigmoid(gb_ref[...]) * yb).astype(BF16)
        mix_ref[...] = mixed
        hout_ref[...] = h_ref[...] + jnp.dot(mixed, wo_ref[...], preferred_element_type=F32)

    row = lambda w: pl.BlockSpec((tm, w), lambda i: (i, 0))
    full = lambda a: pl.BlockSpec(a.shape, lambda i: (0, 0))
    return pl.pallas_call(
        body,
        name="merge_fwd",
        grid=(t // tm,),
        in_specs=[row(d), row(oa.shape[1]), row(ob.shape[1]), row(d), row(d), full(wa), full(wb), full(wo)],
        out_specs=[row(d), row(d)],
        out_shape=[jax.ShapeDtypeStruct((t, d), F32), jax.ShapeDtypeStruct((t, d), BF16)],
        compiler_params=_cparams(("parallel",)),
    )(h, oa, ob, ga, gb, wa, wb, wo)


def _merge_bwd(dh, oa, ob, ga, gb, wa, wb, wo, exchange=None):
    t, d = dh.shape
    tm = _tile(t, 544)
    ni = t // tm
    n_x = len(exchange.ins) if exchange else 0

    def body(*refs):
        dh_ref, oa_ref, ob_ref, ga_ref, gb_ref, wa_ref, wb_ref, wo_ref = refs[:8]
        dya_ref, dyb_ref, doa_ref, dob_ref, dga_ref, dgb_ref, dhb_ref = refs[8 + n_x:15 + n_x]
        if exchange:
            i = pl.program_id(0)
            _host_exchange(exchange, refs[8:8 + n_x], refs[15 + n_x:15 + 2 * n_x], refs[15 + 2 * n_x:],
                           i == 0, i == ni - 1, i == ni - 1)
        dhb = dh_ref[...].astype(BF16)
        dhb_ref[...] = dhb
        dmix = lax.dot_general(dhb, wo_ref[...], NT_DIMS, preferred_element_type=F32)
        for o_ref, g_ref, w_ref, dy_ref, do_ref, dg_ref in (
                (oa_ref, ga_ref, wa_ref, dya_ref, doa_ref, dga_ref),
                (ob_ref, gb_ref, wb_ref, dyb_ref, dob_ref, dgb_ref)):
            y = jnp.dot(o_ref[...], w_ref[...], preferred_element_type=F32)
            s = jax.nn.sigmoid(g_ref[...])
            dy = (dmix * s).astype(BF16)
            dy_ref[...] = dy
            dg_ref[...] = ((dmix * y) * (s * (1.0 - s))).astype(BF16)
            do_ref[...] = lax.dot_general(dy, w_ref[...], NT_DIMS, preferred_element_type=F32).astype(BF16)

    row = lambda w: pl.BlockSpec((tm, w), lambda i: (i, 0))
    full = lambda a: pl.BlockSpec(a.shape, lambda i: (0, 0))
    wa_w, wb_w = oa.shape[1], ob.shape[1]
    outs = pl.pallas_call(
        body,
        name="merge_bwd",
        grid=(ni,),
        in_specs=[row(d), row(wa_w), row(wb_w), row(d), row(d), full(wa), full(wb), full(wo)] + [HBM_SPEC] * n_x,
        out_specs=[row(d), row(d), row(wa_w), row(wb_w), row(d), row(d), row(d)] + [HBM_SPEC] * n_x,
        out_shape=[
            jax.ShapeDtypeStruct((t, d), BF16), jax.ShapeDtypeStruct((t, d), BF16),
            jax.ShapeDtypeStruct((t, wa_w), BF16), jax.ShapeDtypeStruct((t, wb_w), BF16),
            jax.ShapeDtypeStruct((t, d), BF16), jax.ShapeDtypeStruct((t, d), BF16),
            jax.ShapeDtypeStruct((t, d), BF16),
        ] + (exchange.out_shape if exchange else []),
        scratch_shapes=exchange.scratch if exchange else [],
        compiler_params=_cparams(("arbitrary",) if exchange else ("parallel",)),
    )(dh, oa, ob, ga, gb, wa, wb, wo, *(exchange.ins if exchange else []))
    return outs[:7], outs[7:]


def _tri_dot(tri, x):
    hi = x.astype(BF16)
    r1 = x - hi.astype(F32)
    mid = r1.astype(BF16)
    lo = (r1 - mid.astype(F32)).astype(BF16)
    return (jnp.dot(tri, hi, preferred_element_type=F32)
            + jnp.dot(tri, mid, preferred_element_type=F32)
            + jnp.dot(tri, lo, preferred_element_type=F32))


def _forget_cumsum(f_logit, b_pad, nb):
    t, w = f_logit.shape
    bsz = t // (nb * BLOCK)

    def body(f_ref, b_ref, c_ref, carry):
        n = pl.program_id(1)

        @pl.when(n == 0)
        def _():
            carry[...] = jnp.zeros_like(carry)

        x = jax.nn.log_sigmoid(f_ref[...] + b_ref[...])
        rows = lax.broadcasted_iota(jnp.int32, (BLOCK, BLOCK), 0)
        cols = lax.broadcasted_iota(jnp.int32, (BLOCK, BLOCK), 1)
        tri = (cols <= rows).astype(BF16)
        c = _tri_dot(tri, x) + carry[...]
        c_ref[...] = c
        carry[...] = c[BLOCK - 1:BLOCK, :]

    return pl.pallas_call(
        body,
        name="forget_cumsum",
        grid=(bsz, nb),
        in_specs=[pl.BlockSpec((BLOCK, w), lambda b, n: (b * nb + n, 0)),
                  pl.BlockSpec((1, w), lambda b, n: (0, 0))],
        out_specs=pl.BlockSpec((BLOCK, w), lambda b, n: (b * nb + n, 0)),
        out_shape=jax.ShapeDtypeStruct((t, w), F32),
        scratch_shapes=[pltpu.VMEM((1, w), F32)],
        compiler_params=_cparams(("parallel", "arbitrary")),
    )(f_logit, b_pad)


def _forget_cumsum_bwd(dc, f_logit, b_pad, nb):
    t, w = f_logit.shape
    bsz = t // (nb * BLOCK)

    def body(dc_ref, f_ref, b_ref, df_ref, db_ref, carry):
        n = pl.program_id(1)

        @pl.when(n == 0)
        def _():
            carry[...] = jnp.zeros_like(carry)
            db_ref[...] = jnp.zeros_like(db_ref)

        rows = lax.broadcasted_iota(jnp.int32, (BLOCK, BLOCK), 0)
        cols = lax.broadcasted_iota(jnp.int32, (BLOCK, BLOCK), 1)
        tri = (cols >= rows).astype(BF16)
        dlf = _tri_dot(tri, dc_ref[...]) + carry[...]
        carry[...] = dlf[0:1, :]
        df = dlf * jax.nn.sigmoid(-(f_ref[...] + b_ref[...]))
        df_ref[...] = df.astype(BF16)
        db_ref[0] += jnp.sum(df, axis=0, keepdims=True)

    rev = lambda b, n: (b * nb + (nb - 1 - n), 0)
    return pl.pallas_call(
        body,
        name="forget_cumsum_bwd",
        grid=(bsz, nb),
        in_specs=[pl.BlockSpec((BLOCK, w), rev),
                  pl.BlockSpec((BLOCK, w), rev),
                  pl.BlockSpec((1, w), lambda b, n: (0, 0))],
        out_specs=[pl.BlockSpec((BLOCK, w), rev),
                   pl.BlockSpec((1, 1, w), lambda b, n: (b, 0, 0))],
        out_shape=[jax.ShapeDtypeStruct((t, w), BF16), jax.ShapeDtypeStruct((bsz, 1, w), F32)],
        scratch_shapes=[pltpu.VMEM((1, w), F32)],
        compiler_params=_cparams(("parallel", "arbitrary")),
    )(dc, f_logit, b_pad)


GROUP_ROWS = A_GROUP * BLOCK


def _stack_heads(ref):
    return jnp.concatenate([ref[:, i * LANES:(i + 1) * LANES] for i in range(A_GROUP)], axis=0)


def _unstack_heads(ref, x):
    for i in range(A_GROUP):
        ref[:, i * LANES:(i + 1) * LANES] = x[i * BLOCK:(i + 1) * BLOCK].astype(ref.dtype)


def _swa_logits(q, km, kp, kc, slope, n):
    qi = lax.broadcasted_iota(jnp.int32, (GROUP_ROWS, BLOCK), 0) & (BLOCK - 1)
    kj = lax.broadcasted_iota(jnp.int32, (GROUP_ROWS, BLOCK), 1)
    out = []
    for kk, dist, ok in (
            (km, n * BLOCK + qi - kj, (kj >= N_PAD) & (n * BLOCK + qi - kj >= 0)),
            (kp, BLOCK + qi - kj, (kj > qi) & (n >= 2)),
            (kc, qi - kj, (kj <= qi) & (n >= 1))):
        s = lax.dot_general(q, kk, NT_DIMS, preferred_element_type=F32) * SCALE
        s = s - slope * dist.astype(F32)
        out.append(jnp.where(ok, s, NEG))
    return out


def _swa_specs(nb):
    row = lambda b, n: b * nb + n
    qspec = pl.BlockSpec((BLOCK, A_GROUP * LANES), lambda b, g, n: (row(b, n), g))
    kv_m = pl.BlockSpec((BLOCK, LANES), lambda b, g, n: (row(b, 0), 0))
    kv_p = pl.BlockSpec((BLOCK, LANES), lambda b, g, n: (row(b, jnp.maximum(n - 1, 0)), 0))
    kv_c = pl.BlockSpec((BLOCK, LANES), lambda b, g, n: (row(b, n), 0))
    rowspec = pl.BlockSpec((1, GROUP_ROWS, 1), lambda b, g, n: (g, 0, 0))
    lsespec = pl.BlockSpec((1, 1, GROUP_ROWS, 1), lambda b, g, n: (row(b, n), g, 0, 0))
    return qspec, kv_m, kv_p, kv_c, rowspec, lsespec


def _swa_fwd(q, k, v, sink_rows, slope_rows, nb):
    t = q.shape[0]
    bsz = t // (nb * BLOCK)

    def body(q_ref, km_ref, kp_ref, kc_ref, vm_ref, vp_ref, vc_ref, sink_ref, slope_ref, o_ref, lse_ref):
        g = pl.program_id(1)
        n = pl.program_id(2)
        qq = _stack_heads(q_ref)
        sink = sink_ref[0]
        s_m, s_p, s_c = _swa_logits(qq, km_ref[...], kp_ref[...], kc_ref[...], slope_ref[0], n)
        m = jnp.maximum(jnp.maximum(jnp.max(s_m, axis=-1, keepdims=True), jnp.max(s_p, axis=-1, keepdims=True)),
                        jnp.maximum(jnp.max(s_c, axis=-1, keepdims=True), sink))
        e_m = jnp.exp(s_m - m)
        e_p = jnp.exp(s_p - m)
        e_c = jnp.exp(s_c - m)
        z = (jnp.sum(e_m, axis=-1, keepdims=True) + jnp.sum(e_p, axis=-1, keepdims=True)
             + jnp.sum(e_c, axis=-1, keepdims=True) + jnp.exp(sink - m))
        inv = 1.0 / z
        o = (jnp.dot((e_m * inv).astype(BF16), vm_ref[...], preferred_element_type=F32)
             + jnp.dot((e_p * inv).astype(BF16), vp_ref[...], preferred_element_type=F32)
             + jnp.dot((e_c * inv).astype(BF16), vc_ref[...], preferred_element_type=F32))
        lane_group = lax.broadcasted_iota(jnp.int32, (GROUP_ROWS, LANES), 1) // HEAD_DIM
        _unstack_heads(o_ref, jnp.where(lane_group == g, o, 0.0))
        lse_ref[0, 0] = m + jnp.log(z)

    qspec, kv_m, kv_p, kv_c, rowspec, lsespec = _swa_specs(nb)
    return pl.pallas_call(
        body,
        name="swa_fwd",
        grid=(bsz, A_KV_HEADS, nb),
        in_specs=[qspec, kv_m, kv_p, kv_c, kv_m, kv_p, kv_c, rowspec, rowspec],
        out_specs=[qspec, lsespec],
        out_shape=[jax.ShapeDtypeStruct((t, A_PAD_WIDTH), BF16),
                   jax.ShapeDtypeStruct((t // BLOCK, A_KV_HEADS, GROUP_ROWS, 1), F32)],
        compiler_params=_cparams(("parallel", "parallel", "arbitrary")),
    )(q, k, k, k, v, v, v, sink_rows, slope_rows)


def _swa_bwd(q, k, v, do, lse, sink_rows, slope_rows, nb):
    t = q.shape[0]
    l = nb * BLOCK
    bsz = t // l

    def body(q_ref, km_ref, kp_ref, kc_ref, vm_ref, vp_ref, vc_ref, do_ref, lse_ref, sink_ref, slope_ref,
             dq_ref, dk_ref, dv_ref, dsink_ref, dk_acc, dv_acc):
        g = pl.program_id(1)
        n = pl.program_id(2)

        @pl.when((g == 0) & (n == 0))
        def _():
            dk_acc[...] = jnp.zeros_like(dk_acc)
            dv_acc[...] = jnp.zeros_like(dv_acc)

        @pl.when(n == 0)
        def _():
            dsink_ref[...] = jnp.zeros_like(dsink_ref)

        qq = _stack_heads(q_ref)
        dob = _stack_heads(do_ref)
        lse = lse_ref[0, 0]
        logits = _swa_logits(qq, km_ref[...], kp_ref[...], kc_ref[...], slope_ref[0], n)
        probs = [jnp.exp(s - lse) for s in logits]
        dps = [lax.dot_general(dob, v_ref[...], NT_DIMS, preferred_element_type=F32)
               for v_ref in (vm_ref, vp_ref, vc_ref)]
        delta = sum(jnp.sum(p * dp, axis=-1, keepdims=True) for p, dp in zip(probs, dps))
        prev = jnp.maximum(n - 1, 0)
        dq = jnp.zeros((GROUP_ROWS, LANES), F32)
        for p, dp, k_ref, start in ((probs[0], dps[0], km_ref, 0),
                                    (probs[1], dps[1], kp_ref, prev * BLOCK),
                                    (probs[2], dps[2], kc_ref, n * BLOCK)):
            ds = (p * (dp - delta)).astype(BF16)
            dq = dq + jnp.dot(ds, k_ref[...], preferred_element_type=F32)
            rows = pl.ds(pl.multiple_of(start, BLOCK), BLOCK)
            dk_acc[rows, :] += lax.dot_general(ds, qq, TN_DIMS, preferred_element_type=F32) * SCALE
            dv_acc[rows, :] += lax.dot_general(p.astype(BF16), dob, TN_DIMS, preferred_element_type=F32)
        _unstack_heads(dq_ref, dq * SCALE)
        dsink_ref[0, 0] += -(jnp.exp(sink_ref[0] - lse) * delta)

        @pl.when((g == A_KV_HEADS - 1) & (n == nb - 1))
        def _():
            dk_ref[...] = dk_acc[...].astype(BF16)
            dv_ref[...] = dv_acc[...].astype(BF16)

    qspec, kv_m, kv_p, kv_c, rowspec, lsespec = _swa_specs(nb)
    kv_all = pl.BlockSpec((l, LANES), lambda b, g, n: (b, 0))
    return pl.pallas_call(
        body,
        name="swa_bwd",
        grid=(bsz, A_KV_HEADS, nb),
        in_specs=[qspec, kv_m, kv_p, kv_c, kv_m, kv_p, kv_c, qspec, lsespec, rowspec, rowspec],
        out_specs=[qspec, kv_all, kv_all,
                   pl.BlockSpec((1, 1, GROUP_ROWS, 1), lambda b, g, n: (b, g, 0, 0))],
        out_shape=[jax.ShapeDtypeStruct((t, A_PAD_WIDTH), BF16),
                   jax.ShapeDtypeStruct((t, LANES), BF16),
                   jax.ShapeDtypeStruct((t, LANES), BF16),
                   jax.ShapeDtypeStruct((bsz, A_KV_HEADS, GROUP_ROWS, 1), F32)],
        scratch_shapes=[pltpu.VMEM((l, LANES), F32), pltpu.VMEM((l, LANES), F32)],
        compiler_params=_cparams(("parallel", "arbitrary", "arbitrary")),
    )(q, k, k, k, v, v, v, do, lse, sink_rows, slope_rows)


CHUNK = KEY_BLOCKS * BLOCK


def _fox_chunk(qb, ci):
    sb = jnp.maximum(jnp.minimum(KEY_BLOCKS * ci, qb + 1 - KEY_BLOCKS), 0)
    lo = jnp.maximum(ci * CHUNK, N_PAD)
    return sb, lo, pl.ds(pl.multiple_of(sb * BLOCK, BLOCK), CHUNK)


def _fox_logits(s_ref, cr_ref, e, j, sb, lo, qb):
    lane = lax.broadcasted_iota(jnp.int32, (BLOCK, BLOCK), 1)
    ahead = lane - lax.broadcasted_iota(jnp.int32, (BLOCK, BLOCK), 0)
    first = (sb + j) * BLOCK
    s = s_ref[e, :, j * BLOCK:(j + 1) * BLOCK] - cr_ref[e, sb + j]
    return jnp.where((ahead <= qb * BLOCK - first) & (lane >= lo - first), s, NEG)


def _fox_specs(nb):
    l = nb * BLOCK
    q_spec = pl.BlockSpec((BLOCK, LANES), lambda b, p, i: (b * nb + i, p))
    kv_spec = pl.BlockSpec((l, 2 * LANES), lambda b, p, i: (b, p))
    cc_spec = pl.BlockSpec((2, BLOCK, 1), lambda b, p, i: (b * B_PAIRS + p, i, 0))
    cr_spec = pl.BlockSpec((2, nb, 1, BLOCK), lambda b, p, i: (b * B_PAIRS + p, 0, 0, 0))
    return q_spec, kv_spec, cc_spec, cr_spec


def _fox_fwd(q, k, v, c_row, nb, exchange=None):
    t = q.shape[0]
    bsz = t // (nb * BLOCK)
    assert nb >= KEY_BLOCKS

    n_x = len(exchange.ins) if exchange else 0

    def body(*refs):
        q_ref, k_ref, v_ref, cr_ref = refs[:4]
        o_ref, ox_ref, lse_ref = refs[4 + n_x:7 + n_x]
        s_scr, hi_scr, lo_scr = refs[7 + 2 * n_x:10 + 2 * n_x]
        qb = pl.program_id(2)
        if exchange:
            first = (pl.program_id(0) == 0) & (pl.program_id(1) == 0)
            last = (pl.program_id(0) == bsz - 1) & (pl.program_id(1) == B_PAIRS - 1)
            _host_exchange(exchange, refs[4:4 + n_x], refs[7 + n_x:7 + 2 * n_x], refs[10 + 2 * n_x:],
                           first & (qb == 0), last & (qb == 0), last & (qb == nb - 1))
        qs = q_ref[...] * SCALE
        first_half = lax.broadcasted_iota(jnp.int32, (BLOCK, LANES), 1) < HEAD_DIM

        def step(ci, carry):
            stats, acc, acc_lo = carry[:4], carry[4], carry[5]
            sb, lo, krows = _fox_chunk(qb, ci)
            new_stats, alphas = [], []
            pv = jnp.zeros((BLOCK, LANES), F32)
            pv_lo = jnp.zeros((BLOCK, LANES), F32)
            for e in range(2):
                m, z = stats[2 * e], stats[2 * e + 1]
                tile = slice(e * LANES, (e + 1) * LANES)
                s_scr[e] = lax.dot_general(qs, k_ref[krows, tile], NT_DIMS, preferred_element_type=F32)
                top = None
                for j in range(KEY_BLOCKS):
                    s = _fox_logits(s_scr, cr_ref, e, j, sb, lo, qb)
                    s_scr[e, :, j * BLOCK:(j + 1) * BLOCK] = s
                    top = s if top is None else jnp.maximum(top, s)
                m_new = jnp.maximum(m, jnp.max(top, axis=-1, keepdims=True))
                alpha = jnp.exp(m - m_new)
                m_wide = jnp.broadcast_to(m_new, (BLOCK, BLOCK))
                total = None
                for j in range(KEY_BLOCKS):
                    cols = slice(j * BLOCK, (j + 1) * BLOCK)
                    p = jnp.exp(s_scr[e, :, cols] - m_wide)
                    total = p if total is None else total + p
                    hi = p.astype(BF16)
                    hi_scr[e, :, cols] = hi
                    lo_scr[e, :, cols] = (p - hi.astype(F32)).astype(BF16)
                z = alpha * z + jnp.sum(total, axis=-1, keepdims=True)
                vv = v_ref[krows, tile]
                pv = pv + jnp.dot(hi_scr[e], vv, preferred_element_type=F32)
                pv_lo = pv_lo + jnp.dot(lo_scr[e], vv, preferred_element_type=F32)
                new_stats += [m_new, z]
                alphas.append(alpha)
            alpha = jnp.where(first_half, alphas[0], alphas[1])
            return (*new_stats, alpha * acc + pv, alpha * acc_lo + pv_lo)

        col = lambda val: jnp.full((BLOCK, 1), val, F32)
        m0, z0, m1, z1, acc, acc_lo = lax.fori_loop(
            0, (qb + KEY_BLOCKS) // KEY_BLOCKS, step,
            (col(NEG), col(0.0), col(NEG), col(0.0), jnp.zeros((BLOCK, LANES), F32), jnp.zeros((BLOCK, LANES), F32)))
        inv = 1.0 / jnp.where(first_half, z0, z1)
        o_ref[...] = (acc * inv).astype(BF16)
        ox_ref[...] = (acc + acc_lo) * inv
        lse_ref[0] = m0 + jnp.log(z0)
        lse_ref[1] = m1 + jnp.log(z1)

    q_spec, kv_spec, cc_spec, cr_spec = _fox_specs(nb)
    outs = pl.pallas_call(
        body,
        name="fox_fwd",
        grid=(bsz, B_PAIRS, nb),
        in_specs=[q_spec, kv_spec, kv_spec, cr_spec] + [HBM_SPEC] * n_x,
        out_specs=[q_spec, q_spec, cc_spec] + [HBM_SPEC] * n_x,
        out_shape=[jax.ShapeDtypeStruct((t, B_WIDTH), BF16), jax.ShapeDtypeStruct((t, B_WIDTH), F32),
                   jax.ShapeDtypeStruct((bsz * B_HEADS, nb * BLOCK, 1), F32)] + (exchange.out_shape if exchange else []),
        scratch_shapes=[pltpu.VMEM((2, BLOCK, CHUNK), F32), pltpu.VMEM((2, BLOCK, CHUNK), BF16),
                        pltpu.VMEM((2, BLOCK, CHUNK), BF16)] + (exchange.scratch if exchange else []),
        compiler_params=_cparams(("arbitrary",) * 3 if exchange else ("parallel", "parallel", "arbitrary")),
    )(q, k, v, c_row, *(exchange.ins if exchange else []))
    return outs[:3], outs[3:]


def _fox_bwd(q, k, v, o_exact, do, lse, c_row, nb, exchange=None):
    t = q.shape[0]
    l = nb * BLOCK
    bsz = t // l

    n_x = len(exchange.ins) if exchange else 0

    def body(*refs):
        q_ref, k_ref, v_ref, ox_ref, do_ref, lse_ref, cr_ref = refs[:7]
        dq_ref, dk_ref, dv_ref, dc_ref = refs[7 + n_x:11 + n_x]
        dk_acc, dv_acc, s_scr, dp_scr, p_scr, ds_scr = refs[11 + 2 * n_x:17 + 2 * n_x]
        qb = pl.program_id(2)
        if exchange:
            first = (pl.program_id(0) == 0) & (pl.program_id(1) == 0)
            last = (pl.program_id(0) == bsz - 1) & (pl.program_id(1) == B_PAIRS - 1)
            _host_exchange(exchange, refs[7:7 + n_x], refs[11 + n_x:11 + 2 * n_x], refs[17 + 2 * n_x:],
                           first & (qb == 0), last & (qb == 0), last & (qb == nb - 1))

        @pl.when(qb == 0)
        def _():
            dk_acc[...] = jnp.zeros_like(dk_acc)
            dv_acc[...] = jnp.zeros_like(dv_acc)
            dc_ref[...] = jnp.zeros_like(dc_ref)

        qs = q_ref[...] * SCALE
        dob = do_ref[...]
        top_half = lax.broadcasted_iota(jnp.int32, (LANES, BLOCK), 0) < HEAD_DIM
        pair_t = lambda x: jnp.concatenate([jnp.where(top_half, x.T, 0), jnp.where(top_half, 0, x.T)], axis=1)
        qs_t = pair_t(qs)
        dob_t = pair_t(dob)
        first_half = lax.broadcasted_iota(jnp.int32, (BLOCK, LANES), 1) < HEAD_DIM
        weighted = dob.astype(F32) * ox_ref[...]
        wide = lambda col: jnp.broadcast_to(col, (BLOCK, BLOCK))
        deltas = (wide(jnp.sum(jnp.where(first_half, weighted, 0.0), axis=-1, keepdims=True)),
                  wide(jnp.sum(jnp.where(first_half, 0.0, weighted), axis=-1, keepdims=True)))
        lses = (wide(lse_ref[0]), wide(lse_ref[1]))

        def step(ci, dq):
            sb, lo, krows = _fox_chunk(qb, ci)
            for e in range(2):
                tile = slice(e * LANES, (e + 1) * LANES)
                kk = k_ref[krows, tile]
                s_scr[e] = lax.dot_general(qs, kk, NT_DIMS, preferred_element_type=F32)
                dp_scr[e] = lax.dot_general(dob, v_ref[krows, tile], NT_DIMS, preferred_element_type=F32)
                for j in range(KEY_BLOCKS):
                    cols = slice(j * BLOCK, (j + 1) * BLOCK)
                    p = jnp.exp(_fox_logits(s_scr, cr_ref, e, j, sb, lo, qb) - lses[e])
                    ds = p * (dp_scr[e, :, cols] - deltas[e])
                    dc_ref[e, sb + j] -= jnp.sum(ds, axis=0, keepdims=True)
                    p_scr[e, :, cols] = p.astype(BF16)
                    ds_scr[e, :, cols] = ds.astype(BF16)
                dq = dq + jnp.dot(ds_scr[e], kk, preferred_element_type=F32)
            dk_t = jnp.dot(qs_t, ds_scr[...].reshape(2 * BLOCK, CHUNK), preferred_element_type=F32)
            dv_t = jnp.dot(dob_t, p_scr[...].reshape(2 * BLOCK, CHUNK), preferred_element_type=F32)
            for j in range(KEY_BLOCKS):
                cols = slice(j * BLOCK, (j + 1) * BLOCK)
                dk_acc[sb + j] += dk_t[:, cols]
                dv_acc[sb + j] += dv_t[:, cols]
            return dq

        dq = lax.fori_loop(0, (qb + KEY_BLOCKS) // KEY_BLOCKS, step, jnp.zeros((BLOCK, LANES), F32))
        dq_ref[...] = (dq * SCALE).astype(BF16)

        @pl.when(qb == nb - 1)
        def _():
            for kb in range(nb):
                rows = slice(kb * BLOCK, (kb + 1) * BLOCK)
                for acc, out_ref in ((dk_acc, dk_ref), (dv_acc, dv_ref)):
                    pair = acc[kb].T
                    out_ref[rows, 0:LANES] = jnp.where(first_half, pair, 0.0).astype(BF16)
                    out_ref[rows, LANES:2 * LANES] = jnp.where(first_half, 0.0, pair).astype(BF16)

    q_spec, kv_spec, cc_spec, cr_spec = _fox_specs(nb)
    outs = pl.pallas_call(
        body,
        name="fox_bwd",
        grid=(bsz, B_PAIRS, nb),
        in_specs=[q_spec, kv_spec, kv_spec, q_spec, q_spec, cc_spec, cr_spec] + [HBM_SPEC] * n_x,
        out_specs=[q_spec, kv_spec, kv_spec, cr_spec] + [HBM_SPEC] * n_x,
        out_shape=[jax.ShapeDtypeStruct((t, B_WIDTH), BF16), jax.ShapeDtypeStruct((t, B_PAD_WIDTH), BF16),
                   jax.ShapeDtypeStruct((t, B_PAD_WIDTH), BF16),
                   jax.ShapeDtypeStruct((bsz * B_HEADS, nb, 1, BLOCK), F32)] + (exchange.out_shape if exchange else []),
        scratch_shapes=[pltpu.VMEM((nb, LANES, BLOCK), F32), pltpu.VMEM((nb, LANES, BLOCK), F32),
                        pltpu.VMEM((2, BLOCK, CHUNK), F32), pltpu.VMEM((2, BLOCK, CHUNK), F32),
                        pltpu.VMEM((2, BLOCK, CHUNK), BF16), pltpu.VMEM((2, BLOCK, CHUNK), BF16)]
        + (exchange.scratch if exchange else []),
        compiler_params=_cparams(("arbitrary",) * 3 if exchange else ("parallel", "parallel", "arbitrary")),
    )(q, k, v, o_exact, do, lse, c_row, *(exchange.ins if exchange else []))
    return outs[:4], outs[4:]


def _loss_head(h, final_w, target):
    bsz, l, d = h.shape
    nb = l // BLOCK

    def body(h_ref, w_ref, t_ref, loss_ref, dh_ref, dw_ref):
        b = pl.program_id(0)
        n = pl.program_id(1)

        @pl.when((b == 0) & (n == 0))
        def _():
            loss_ref[...] = jnp.zeros_like(loss_ref)
            dw_ref[...] = jnp.zeros_like(dw_ref)

        @pl.when(n == 0)
        def _():
            dh_ref[...] = jnp.zeros_like(dh_ref)

        @pl.when(n > 0)
        def _():
            hh = h_ref[0]
            w = w_ref[...]
            r = _rms_scale(hh)
            err = (hh * r) * w - t_ref[0]
            loss_ref[...] += 0.5 * jnp.sum(jnp.mean(err * err, axis=-1, keepdims=True), axis=0, keepdims=True)
            dy = err * (1.0 / d)
            dh, dw = _rms_bwd(dy, hh, w)
            dh_ref[0] = dh
            dw_ref[...] += dw

    return pl.pallas_call(
        body,
        name="loss_head",
        grid=(bsz, nb),
        in_specs=[
            pl.BlockSpec((1, BLOCK, d), lambda b, n: (b, n, 0)),
            pl.BlockSpec((1, d), lambda b, n: (0, 0)),
            pl.BlockSpec((1, BLOCK, d), lambda b, n: (b, jnp.maximum(n - 1, 0), 0)),
        ],
        out_specs=[
            pl.BlockSpec((1, 128), lambda b, n: (0, 0)),
            pl.BlockSpec((1, BLOCK, d), lambda b, n: (b, n, 0)),
            pl.BlockSpec((1, d), lambda b, n: (0, 0)),
        ],
        out_shape=[jax.ShapeDtypeStruct((1, 128), F32), jax.ShapeDtypeStruct((bsz, l, d), F32),
                   jax.ShapeDtypeStruct((1, d), F32)],
        compiler_params=_cparams(("arbitrary", "arbitrary")),
    )(h, final_w, target)


def _pad_tiles(w, src, heads, lane_slot, axis):
    pieces = []
    for h in range(heads):
        x = lax.slice_in_dim(w, src + HEAD_DIM * h, src + HEAD_DIM * (h + 1), axis=axis)
        z = jnp.zeros_like(x)
        pieces += [x, z] if lane_slot(h) == 0 else [z, x]
    return pieces


def _unpad_tiles(g, off, heads, lane_slot, axis):
    return [lax.slice_in_dim(g, off + LANES * h + HEAD_DIM * lane_slot(h),
                             off + LANES * h + HEAD_DIM * (lane_slot(h) + 1), axis=axis) for h in range(heads)]


A_SLOT = lambda h: h // A_GROUP
B_SLOT = lambda h: h % 2


def _layout_w_in(w):
    pad_f = jnp.zeros((w.shape[0], F_COLS - B_HEADS + TAIL_COLS), w.dtype)
    return jnp.concatenate(
        _pad_tiles(w, 0, A_HEADS, A_SLOT, 1) + [w[:, SRC_KA:SRC_KB]]
        + _pad_tiles(w, SRC_KB, B_HEADS, B_SLOT, 1) + _pad_tiles(w, SRC_VB, B_HEADS, B_SLOT, 1)
        + [w[:, SRC_GA:], w[:, SRC_F:SRC_GA], pad_f], axis=1)


def _unlayout_w_in(g):
    return jnp.concatenate(
        _unpad_tiles(g, OFF_QA, A_HEADS, A_SLOT, 1) + [g[:, OFF_KA:OFF_KB]]
        + _unpad_tiles(g, OFF_KB, B_HEADS, B_SLOT, 1) + _unpad_tiles(g, OFF_VB, B_HEADS, B_SLOT, 1)
        + [g[:, OFF_F:OFF_F + B_HEADS], g[:, OFF_GA:OFF_F]], axis=1)


def _local_step(x, target, meta, norms, b_forget, sinks, w, comm=None):
    n1, nmix, n2, nfin = norms
    w1i, w1o = w[:2]
    bsz, seq, d = x.shape
    l = PREFIX + seq
    nb = l // BLOCK
    t = bsz * l

    h0 = jnp.concatenate([jnp.zeros((bsz, N_PAD, d), F32),
                          jnp.broadcast_to(meta[None], (bsz, N_META, d)), x], axis=1).reshape(t, d)

    if comm is None:
        (h1, g1, u1), _ = _ffn_fwd(h0, n1, w1i, w1o)
        w_in, wa, wb, wo, w2i, w2o = w[2:]
    else:
        (h1, g1, u1), gathered = _ffn_fwd(h0, n1, w1i, w1o, comm.gather(GATHER_PROJ))
        w_in, = comm.gathered(GATHER_PROJ, gathered)
    wp = _layout_w_in(w_in)
    un, qa, ka, va, qb, kb, vb, ga, gb, f_logit = _proj_fwd(h1, nmix, wp)
    b_pad = jnp.concatenate([b_forget, jnp.zeros((1, F_COLS - B_HEADS), F32)], axis=1)
    c = _forget_cumsum(f_logit, b_pad, nb)
    c_heads = c[:, :B_HEADS].reshape(bsz, l, B_HEADS).transpose(0, 2, 1).reshape(bsz * B_HEADS, l)
    c_row = c_heads.reshape(bsz * B_HEADS, nb, 1, BLOCK)

    slopes = jnp.exp2(-8.0 * jnp.arange(1, A_HEADS + 1, dtype=F32) / A_HEADS)
    slope_rows = jnp.repeat(slopes.reshape(A_KV_HEADS, A_GROUP), BLOCK, axis=1)[:, :, None]
    sink_rows = jnp.repeat(sinks.reshape(A_KV_HEADS, A_GROUP), BLOCK, axis=1)[:, :, None]

    oa, lse_a = _swa_fwd(qa, ka, va, sink_rows, slope_rows, nb)
    if comm is None:
        (ob, ob_exact, lse_b), _ = _fox_fwd(qb, kb, vb, c_row, nb)
    else:
        (ob, ob_exact, lse_b), gathered = _fox_fwd(qb, kb, vb, c_row, nb, comm.gather(GATHER_LATE))
        wa, wb, wo, w2i, w2o = comm.gathered(GATHER_LATE, gathered)
    wa_p = jnp.concatenate(_pad_tiles(wa, 0, A_HEADS, A_SLOT, 0), axis=0)
    h2, mixed = _merge_fwd(h1, oa, ob, ga, gb, wa_p, wb, wo)
    (h3, g2, u2), _ = _ffn_fwd(h2, n2, w2i, w2o)
    loss, dh3, d_nfin = _loss_head(h3.reshape(bsz, l, d), nfin, target)

    (dh2, n2b, a2, dgu2, df2, dn2_parts), _ = _ffn_bwd(dh3.reshape(t, d), h2, n2, g2, u2, w2i, w2o)
    g_w2o = _tn_matmul(a2, df2, "grad_ffn2_w_out")
    g_w2i = _tn_matmul(n2b, dgu2, "grad_ffn2_w_in")

    hosted = comm.swap("ffn2", dict(ffn2_w_in=g_w2i, ffn2_w_out=g_w2o)) if comm else None
    (dya, dyb, doa, dob, dga, dgb, dh2b), swapped = _merge_bwd(dh2, oa, ob, ga, gb, wa_p, wb, wo, hosted)
    g_wo = _tn_matmul(mixed, dh2b, "grad_w_out")
    g_wa = jnp.concatenate(_unpad_tiles(_tn_matmul(oa, dya, "grad_w_branch_a"), 0, A_HEADS, A_SLOT, 0), axis=0)
    g_wb = _tn_matmul(ob, dyb, "grad_w_branch_b")

    dqa, dka, dva, dsink_rows = _swa_bwd(qa, ka, va, doa, lse_a, sink_rows, slope_rows, nb)
    hosted = comm.scatter("ffn2", swapped) if comm else None
    (dqb, dkb, dvb, dc_row), pieces = _fox_bwd(qb, kb, vb, ob_exact, dob, lse_b, c_row, nb, hosted)
    if comm:
        comm.received("ffn2", pieces)
    dc = dc_row.reshape(bsz, B_HEADS, l).transpose(0, 2, 1).reshape(t, B_HEADS)
    dc = jnp.concatenate([dc, jnp.zeros((t, F_COLS - B_HEADS), F32)], axis=1)
    df_logit, db_parts = _forget_cumsum_bwd(dc, f_logit, b_pad, nb)

    dproj = jnp.concatenate([dqa, dka, dva, dqb, dkb, dvb, dga, dgb, df_logit,
                             jnp.zeros((t, TAIL_COLS), BF16)], axis=1)
    g_win = _unlayout_w_in(_tn_matmul(un, dproj, "grad_w_in"))
    hosted = comm.swap("mixer", dict(w_in=g_win, w_branch_a=g_wa, w_branch_b=g_wb, w_out=g_wo)) if comm else None
    (dh1, dnmix_parts), swapped = _proj_bwd(dh2, h1, nmix, dproj, wp, hosted)
    hosted = comm.scatter("mixer", swapped) if comm else None
    (dh0, n1b, a1, dgu1, df1, dn1_parts), pieces = _ffn_bwd(dh1, h0, n1, g1, u1, w1i, w1o, hosted)
    if comm:
        comm.received("mixer", pieces)
    g_w1o = _tn_matmul(a1, df1, "grad_ffn1_w_out")
    g_w1i = _tn_matmul(n1b, dgu1, "grad_ffn1_w_in")

    dh0 = dh0.reshape(bsz, l, d)
    grad_x = dh0[:, PREFIX:]
    small = dict(
        meta_tokens=jnp.sum(dh0[:, N_PAD:PREFIX], axis=0),
        ffn1_norm=jnp.sum(dn1_parts, axis=0),
        mix_norm=jnp.sum(dnmix_parts, axis=0),
        ffn2_norm=jnp.sum(dn2_parts, axis=0),
        final_norm=d_nfin,
        b_forget=jnp.sum(db_parts, axis=0)[:, :B_HEADS],
        attn_sinks=jnp.sum(dsink_rows.reshape(bsz, A_HEADS, BLOCK), axis=(0, 2)).reshape(1, A_HEADS),
    )
    big = dict(ffn1_w_in=g_w1i, ffn1_w_out=g_w1o, w_in=g_win, w_branch_a=g_wa, w_branch_b=g_wb,
               w_out=g_wo, ffn2_w_in=g_w2i, ffn2_w_out=g_w2o)
    return loss, grad_x, small, big


BIG = (
    ("ffn1_w_in", (D_MODEL, 5632), 1),
    ("ffn1_w_out", (2816, D_MODEL), 0),
    ("w_in", (D_MODEL, W_IN_COLS), 1),
    ("w_branch_a", (A_WIDTH, D_MODEL), 1),
    ("w_branch_b", (B_WIDTH, D_MODEL), 1),
    ("w_out", (D_MODEL, D_MODEL), 0),
    ("ffn2_w_in", (D_MODEL, 5632), 1),
    ("ffn2_w_out", (2816, D_MODEL), 0),
)
STACKED = "w_in"


def _coords():
    return lax.axis_index("x"), lax.axis_index("y"), lax.axis_index("c")


def _other_chips(x, y):
    return ((1 - x, y), (x, 1 - y), (1 - x, 1 - y))


def _chip_part(ref, name, shape, axis, k):
    if name == STACKED:
        return ref.at[k]
    size = shape[axis] // N_CHIPS
    start = pl.multiple_of(k * size, size)
    return ref.at[pl.ds(start, size), :] if axis == 0 else ref.at[:, pl.ds(start, size)]


def _full_shape(name, shape):
    return (N_CHIPS, shape[0], shape[1] // N_CHIPS) if name == STACKED else shape


class _Exchange:
    def __init__(self, ins, out_shape, n_sems, ops):
        self.ins, self.out_shape, self.n_sems, self.ops = list(ins), list(out_shape), n_sems, ops

    @property
    def scratch(self):
        return [pltpu.SemaphoreType.DMA((self.n_sems,)), pltpu.SemaphoreType.DMA((self.n_sems,))]


SEMS_PER_GATHER = 7


def _gather_exchange(shards, table):
    n = len(table)

    def ops(ins, outs, send_sems, recv_sems):
        x, y, c = _coords()
        mine = 2 * x + y
        sibling = (x, y, 1 - c)
        chips = _other_chips(x, y)
        slots = [2 * chip[0] + chip[1] for chip in chips]

        def part(i, k):
            name, shape, axis = table[i][:3]
            return _chip_part(outs[i], name, shape, axis, k)

        def half(ref, h):
            rows = ref.shape[0] // 2
            return ref.at[pl.ds(pl.multiple_of(h * rows, rows), rows), :]

        def own(i):
            sem = SEMS_PER_GATHER * i
            return pltpu.make_async_remote_copy(ins[i], part(i, mine), send_sems.at[sem], recv_sems.at[sem],
                                                device_id=sibling, device_id_type=MESH_ID)

        def fetch(i, j, slot):
            sem = SEMS_PER_GATHER * i + 1 + j
            if table[i][4]:
                src, dst = half(ins[i], c), half(part(i, slot), c)
            else:
                src, dst = ins[i], part(i, slot)
            return pltpu.make_async_remote_copy(src, dst, send_sems.at[sem], recv_sems.at[sem],
                                                device_id=(chips[j][0], chips[j][1], c), device_id_type=MESH_ID)

        def forward(i, j, h):
            sem = SEMS_PER_GATHER * i + 4 + j
            region = half(part(i, slots[j]), h)
            return pltpu.make_async_remote_copy(region, region, send_sems.at[sem], recv_sems.at[sem],
                                                device_id=sibling, device_id_type=MESH_ID)

        def start():
            for i in range(n):
                for j in range(3):
                    fetch(i, j, mine).start()
            for i in range(n):
                own(i).start()

        def relay():
            for i in range(n):
                for j in range(3):
                    fetch(i, j, slots[j]).wait_recv()
                    if table[i][4]:
                        forward(i, j, c).start()

        def finish():
            for i in range(n):
                own(i).wait()
                for j in range(3):
                    if table[i][4]:
                        forward(i, j, 1 - c).wait_recv()
                        forward(i, j, c).wait_send()
                    fetch(i, j, mine).wait_send()

        return start, relay, finish

    out_shape = [jax.ShapeDtypeStruct(_full_shape(name, shape), dtype) for name, shape, _, dtype, _ in table]
    return _Exchange(shards, out_shape, SEMS_PER_GATHER * n, ops)


def _run_exchange(exchange, name):
    n = len(exchange.ins)

    def body(*refs):
        start, relay, finish = exchange.ops(refs[:n], refs[n:2 * n], *refs[2 * n:])
        start()
        relay()
        finish()

    return pl.pallas_call(
        body,
        name=name,
        in_specs=[HBM_SPEC] * n,
        out_specs=[HBM_SPEC] * n,
        out_shape=exchange.out_shape,
        scratch_shapes=exchange.scratch,
    )(*exchange.ins)


def _host_exchange(exchange, in_refs, out_refs, sem_refs, first, middle, last):
    start, relay, finish = exchange.ops(in_refs, out_refs, *sem_refs)
    pl.when(first)(start)
    pl.when(middle)(relay)
    pl.when(last)(finish)


def _halves_view(name, shape, axis):
    r, c = shape
    if name == STACKED:
        return (N_CHIPS, 2, r // 2, c // N_CHIPS), lambda ref, h: ref.at[:, h]
    if axis == 1:
        return (2, r // 2, c), lambda ref, h: ref.at[h]
    return (N_CHIPS, 2, r // N_CHIPS // 2, c), lambda ref, h: ref.at[:, h]


def _halves_exchange(grads, entries):
    n_w = len(entries)
    views = [_halves_view(*entry) for entry in entries]

    def ops(ins, outs, send_sems, recv_sems):
        x, y, c = _coords()
        copies = [pltpu.make_async_remote_copy(views[i][1](ins[i], 1 - c), outs[i], send_sems.at[i], recv_sems.at[i],
                                               device_id=(x, y, 1 - c), device_id_type=MESH_ID) for i in range(n_w)]

        def start():
            for cp in copies:
                cp.start()

        def finish():
            for cp in copies:
                cp.wait()

        return start, lambda: None, finish

    half_shape = lambda v: tuple(d for i, d in enumerate(v) if i != (1 if len(v) == 4 else 0))
    out_shape = [jax.ShapeDtypeStruct(half_shape(v[0]), F32) for v in views]
    return _Exchange([g.reshape(v[0]) for g, v in zip(grads, views)], out_shape, n_w, ops)


def _add_sibling(g_view, recv, c, name):
    shape = recv.shape
    if len(shape) == 2:
        tr = _tile(shape[0], 128, 16)
        grid = (shape[0] // tr,)
        g_spec = pl.BlockSpec((None, tr, shape[1]), lambda i, c_ref: (c_ref[0], i, 0))
        r_spec = pl.BlockSpec((tr, shape[1]), lambda i, c_ref: (i, 0))
    else:
        tr = _tile(shape[1], 256, 16)
        grid = (N_CHIPS, shape[1] // tr)
        g_spec = pl.BlockSpec((None, None, tr, shape[2]), lambda k, i, c_ref: (k, c_ref[0], i, 0))
        r_spec = pl.BlockSpec((None, tr, shape[2]), lambda k, i, c_ref: (k, i, 0))

    def body(c_ref, g_ref, r_ref, o_ref):
        o_ref[...] = (g_ref[...] + r_ref[...]).astype(BF16)

    return pl.pallas_call(
        body,
        name="add_sibling_" + name,
        grid_spec=pltpu.PrefetchScalarGridSpec(num_scalar_prefetch=1, grid=grid, in_specs=[g_spec, r_spec],
                                               out_specs=r_spec),
        out_shape=jax.ShapeDtypeStruct(shape, BF16),
        compiler_params=_cparams(("parallel",) * len(grid)),
    )(c, g_view, recv)


def _piece_of(ref, name, axis, k):
    if name == STACKED or axis == 0:
        return ref.at[k]
    size = ref.shape[1] // N_CHIPS
    return ref.at[:, pl.ds(pl.multiple_of(k * size, size), size)]


def _piece_shape(name, shape, axis):
    r, c = shape
    return (r // 2, c // N_CHIPS) if (axis == 1) else (r // N_CHIPS // 2, c)


def _scatter_exchange(partials, entries):
    n_w = len(entries)

    def ops(ins, outs, send_sems, recv_sems):
        x, y, c = _coords()
        chips = _other_chips(x, y)
        copies = []
        for i, (name, _, axis) in enumerate(entries):
            for j, chip in enumerate(chips):
                sem = 3 * i + j
                copies.append(pltpu.make_async_remote_copy(
                    _piece_of(ins[i], name, axis, 2 * chip[0] + chip[1]), outs[i].at[j], send_sems.at[sem],
                    recv_sems.at[sem], device_id=(chip[0], chip[1], c), device_id_type=MESH_ID))

        def start():
            for cp in copies:
                cp.start()

        def finish():
            for cp in copies:
                cp.wait()

        return start, lambda: None, finish

    out_shape = [jax.ShapeDtypeStruct((3,) + _piece_shape(*entry), BF16) for entry in entries]
    return _Exchange(partials, out_shape, 3 * n_w, ops)


def _add_chips(partial, recv, mine, name, axis):
    rows, cols = recv.shape[1:]
    tr = _tile(rows, 256, 16)
    if name == STACKED or axis == 0:
        p_spec = pl.BlockSpec((None, tr, cols), lambda i, k_ref: (k_ref[0], i, 0))
    else:
        p_spec = pl.BlockSpec((tr, cols), lambda i, k_ref: (i, k_ref[0]))

    def body(k_ref, p_ref, r_ref, o_ref):
        f32 = lambda a: a.astype(F32)
        o_ref[...] = ((f32(p_ref[...]) + f32(r_ref[0])) + f32(r_ref[1])) + f32(r_ref[2])

    return pl.pallas_call(
        body,
        name="add_chips_" + name,
        grid_spec=pltpu.PrefetchScalarGridSpec(
            num_scalar_prefetch=1, grid=(rows // tr,),
            in_specs=[p_spec, pl.BlockSpec((3, tr, cols), lambda i, k_ref: (0, i, 0))],
            out_specs=pl.BlockSpec((tr, cols), lambda i, k_ref: (i, 0))),
        out_shape=jax.ShapeDtypeStruct((rows, cols), F32),
        compiler_params=_cparams(("parallel",)),
    )(mine, partial, recv)


def _share_with_sibling(halves):
    n_w = len(halves)

    def body(*refs):
        ins, outs = refs[:n_w], refs[n_w:2 * n_w]
        send_sems, recv_sems = refs[2 * n_w:]
        x, y, c = _coords()
        copies = [pltpu.make_async_remote_copy(ins[i], outs[i], send_sems.at[i], recv_sems.at[i],
                                               device_id=(x, y, 1 - c), device_id_type=MESH_ID) for i in range(n_w)]
        for cp in copies:
            cp.start()
        for cp in copies:
            cp.wait()

    return pl.pallas_call(
        body,
        name="share_with_sibling",
        in_specs=[HBM_SPEC] * n_w,
        out_specs=[HBM_SPEC] * n_w,
        out_shape=[jax.ShapeDtypeStruct(h.shape, F32) for h in halves],
        scratch_shapes=[pltpu.SemaphoreType.DMA((n_w,)), pltpu.SemaphoreType.DMA((n_w,))],
    )(*halves)


SMALL_ROWS = 168


def _all_reduce_small(buf):
    def body(b_ref, out_ref, gathered, send_sems, recv_sems):
        x, y, c = _coords()
        me = 4 * x + 2 * y + c
        peers = [(x ^ fx, y ^ fy, c ^ fc) for fx in (0, 1) for fy in (0, 1) for fc in (0, 1)][1:]

        def copy(j, slot, dev):
            return pltpu.make_async_remote_copy(b_ref, gathered.at[slot], send_sems.at[j], recv_sems.at[j],
                                                device_id=dev, device_id_type=MESH_ID)

        for j, dev in enumerate(peers):
            copy(j, me, dev).start()
        gathered[me] = b_ref[...]
        for j, dev in enumerate(peers):
            copy(j, 4 * dev[0] + 2 * dev[1] + dev[2], dev).wait()
        acc = gathered[0]
        for d in range(1, N_DEV):
            acc = acc + gathered[d]
        out_ref[...] = acc

    return pl.pallas_call(
        body,
        name="all_reduce_small",
        in_specs=[VMEM_SPEC],
        out_specs=VMEM_SPEC,
        out_shape=jax.ShapeDtypeStruct(buf.shape, F32),
        scratch_shapes=[pltpu.VMEM((N_DEV,) + buf.shape, F32), pltpu.SemaphoreType.DMA((N_DEV - 1,)),
                        pltpu.SemaphoreType.DMA((N_DEV - 1,))],
    )(buf)


def _adamw(w, g, m, v):
    r, c = w.shape
    tr = _tile(r, max(8, (5 << 19) // (4 * c)), 8)

    def body(w_ref, g_ref, m_ref, v_ref, d_ref, mo_ref, vo_ref):
        gg = g_ref[...]
        mm = ADAM_B1 * m_ref[...] + (1.0 - ADAM_B1) * gg
        vv = ADAM_B2 * v_ref[...] + (1.0 - ADAM_B2) * (gg * gg)
        m_hat = mm / (1.0 - ADAM_B1 ** ADAM_STEP)
        v_hat = vv / (1.0 - ADAM_B2 ** ADAM_STEP)
        d_ref[...] = -ADAM_LR * (m_hat / (jnp.sqrt(v_hat) + ADAM_EPS) + ADAM_WD * w_ref[...])
        mo_ref[...] = mm
        vo_ref[...] = vv

    spec = pl.BlockSpec((tr, c), lambda i: (i, 0))
    return pl.pallas_call(
        body,
        name="adamw",
        grid=(r // tr,),
        in_specs=[spec] * 4,
        out_specs=[spec] * 3,
        out_shape=[jax.ShapeDtypeStruct((r, c), F32)] * 3,
        compiler_params=_cparams(("parallel",)),
    )(w, g, m, v)


def _adamw_halves(w, own, other, m, v, c, name):
    r, cols = w.shape
    half = r // 2
    tr = _tile(half, 256, 8)
    nt = half // tr
    whole = pl.BlockSpec((tr, cols), lambda h, i, c_ref: (h * nt + i, 0))
    part = pl.BlockSpec((tr, cols), lambda h, i, c_ref: (i, 0))

    def body(c_ref, w_ref, own_ref, other_ref, m_ref, v_ref, g_ref, d_ref, mo_ref, vo_ref):
        gg = jnp.where(pl.program_id(0) == c_ref[0], own_ref[...], other_ref[...])
        g_ref[...] = gg
        mm = ADAM_B1 * m_ref[...] + (1.0 - ADAM_B1) * gg
        vv = ADAM_B2 * v_ref[...] + (1.0 - ADAM_B2) * (gg * gg)
        m_hat = mm / (1.0 - ADAM_B1 ** ADAM_STEP)
        v_hat = vv / (1.0 - ADAM_B2 ** ADAM_STEP)
        d_ref[...] = -ADAM_LR * (m_hat / (jnp.sqrt(v_hat) + ADAM_EPS) + ADAM_WD * w_ref[...])
        mo_ref[...] = mm
        vo_ref[...] = vv

    return pl.pallas_call(
        body,
        name="adamw_" + name,
        grid_spec=pltpu.PrefetchScalarGridSpec(
            num_scalar_prefetch=1, grid=(2, nt),
            in_specs=[whole, part, part, whole, whole], out_specs=[whole] * 4),
        out_shape=[jax.ShapeDtypeStruct((r, cols), F32)] * 4,
        compiler_params=_cparams(("parallel", "parallel")),
    )(c, w, own, other, m, v)


GATHER_FIRST = ("ffn1_w_in", "ffn1_w_out")
GATHER_PROJ = ("w_in",)
GATHER_LATE = ("w_branch_a", "w_branch_b", "w_out", "ffn2_w_in", "ffn2_w_out")


class _Comm:
    def __init__(self, shards, c_arr, mine_arr):
        self.shards, self.c, self.mine = shards, c_arr, mine_arr
        self.groups, self.halves = {}, {}
        self.by_name = {entry[0]: entry for entry in BIG}

    def gather(self, names):
        table = [self.by_name[n] + (BF16, True) for n in names]
        return _gather_exchange([self.shards[n] for n in names], table)

    def gathered(self, names, outs):
        return [o.transpose(1, 0, 2).reshape(D_MODEL, W_IN_COLS) if n == STACKED else o for n, o in zip(names, outs)]

    def swap(self, tag, grads):
        entries = [self.by_name[n] for n in grads]
        arrays = [g.reshape(D_MODEL, N_CHIPS, W_IN_COLS // N_CHIPS).transpose(1, 0, 2) if n == STACKED else g
                  for n, g in grads.items()]
        self.groups[tag] = (entries, arrays)
        return _halves_exchange(arrays, entries)

    def scatter(self, tag, received):
        entries, arrays = self.groups[tag]
        views = [_halves_view(*entry) for entry in entries]
        partials = [_add_sibling(g.reshape(v[0]), r, self.c, name)
                    for g, v, r, (name, _, _) in zip(arrays, views, received, entries)]
        self.groups[tag] = (entries, partials)
        return _scatter_exchange(partials, entries)

    def received(self, tag, pieces):
        entries, partials = self.groups[tag]
        for p, r, (name, _, axis) in zip(partials, pieces, entries):
            self.halves[name] = _add_chips(p, r, self.mine, name, axis)

    def finish(self):
        names = [n for n, _, _ in BIG]
        own = [self.halves[n] for n in names]
        return dict(zip(names, zip(own, _share_with_sibling(own))))


def kernel(x, meta_tokens, ffn1_norm, ffn1_w_in, ffn1_w_out, mix_norm, w_in, b_forget, attn_sinks, w_branch_a, w_branch_b, w_out, ffn2_norm, ffn2_w_in, ffn2_w_out, final_norm, loss_target, m_meta_tokens, m_ffn1_norm, m_ffn1_w_in, m_ffn1_w_out, m_mix_norm, m_w_in, m_b_forget, m_attn_sinks, m_w_branch_a, m_w_branch_b, m_w_out, m_ffn2_norm, m_ffn2_w_in, m_ffn2_w_out, m_final_norm, v_meta_tokens, v_ffn1_norm, v_ffn1_w_in, v_ffn1_w_out, v_mix_norm, v_w_in, v_b_forget, v_attn_sinks, v_w_branch_a, v_w_branch_b, v_w_out, v_ffn2_norm, v_ffn2_w_in, v_ffn2_w_out, v_final_norm):
    given = dict(locals())
    names = ["meta_tokens", "ffn1_norm", "ffn1_w_in", "ffn1_w_out", "mix_norm", "w_in", "b_forget", "attn_sinks",
             "w_branch_a", "w_branch_b", "w_out", "ffn2_norm", "ffn2_w_in", "ffn2_w_out", "final_norm"]
    big_names = [n for n, _, _ in BIG]
    cx, cy, cc = _coords()
    c_arr = cc.reshape(1).astype(jnp.int32)
    mine_arr = (2 * cx + cy).reshape(1).astype(jnp.int32)

    comm = _Comm({n: given[n][0].astype(BF16) for n in big_names}, c_arr, mine_arr)
    table = [comm.by_name[n] + (BF16, True) for n in GATHER_FIRST] + [("meta_tokens", (N_META, D_MODEL), 1, F32, False)]
    first = _gather_exchange([comm.shards[n] for n in GATHER_FIRST] + [meta_tokens], table)
    w1i, w1o, meta_full = _run_exchange(first, "gather_first")
    norms = (ffn1_norm, mix_norm, ffn2_norm, final_norm.reshape(1, D_MODEL))
    loss, grad_x, small, big = _local_step(x, loss_target, meta_full, norms, b_forget, attn_sinks, (w1i, w1o), comm)

    swap = comm.swap("ffn1", dict(ffn1_w_in=big["ffn1_w_in"], ffn1_w_out=big["ffn1_w_out"]))
    last = comm.scatter("ffn1", _run_exchange(swap, "exchange_halves"))
    comm.received("ffn1", _run_exchange(last, "scatter_chip_sums"))
    grad_halves = comm.finish()
    grads = {}

    pad_lanes = lambda a: jnp.concatenate([a, jnp.zeros((1, LANES - a.shape[1]), F32)], axis=1)
    buf = jnp.concatenate([
        small["meta_tokens"].reshape(128, LANES),
        small["ffn1_norm"].reshape(8, LANES), small["mix_norm"].reshape(8, LANES),
        small["ffn2_norm"].reshape(8, LANES), small["final_norm"].reshape(8, LANES),
        loss, pad_lanes(small["b_forget"]), pad_lanes(small["attn_sinks"]),
        jnp.zeros((SMALL_ROWS - 163, LANES), F32)], axis=0)
    red = _all_reduce_small(buf)
    meta_cols = red[:128].reshape(N_META, D_MODEL)
    grads["meta_tokens"] = lax.dynamic_slice_in_dim(meta_cols, (2 * cx + cy) * (D_MODEL // N_CHIPS),
                                                    D_MODEL // N_CHIPS, axis=1)
    grads["ffn1_norm"] = red[128:136].reshape(1, D_MODEL)
    grads["mix_norm"] = red[136:144].reshape(1, D_MODEL)
    grads["ffn2_norm"] = red[144:152].reshape(1, D_MODEL)
    grads["final_norm"] = red[152:160].reshape(1, D_MODEL)
    loss_out = red[160, 0]
    grads["b_forget"] = red[161:162, :B_HEADS]
    grads["attn_sinks"] = red[162:163, :A_HEADS]

    out_g, out_d, out_m, out_v = [], [], [], []
    for n in names:
        w_full = given[n]
        shape = w_full.shape
        two_d = (lambda a: a.reshape(shape[-2], shape[-1])) if len(shape) >= 2 else (lambda a: a.reshape(1, shape[0]))
        if n in grad_halves:
            own, other = grad_halves[n]
            g2, d2, m2, v2 = _adamw_halves(two_d(w_full), own, other, two_d(given["m_" + n]),
                                           two_d(given["v_" + n]), c_arr, n)
        else:
            g2 = two_d(grads[n])
            d2, m2, v2 = _adamw(two_d(w_full), g2, two_d(given["m_" + n]), two_d(given["v_" + n]))
        out_g.append(g2.reshape(shape))
        out_d.append(d2.reshape(shape))
        out_m.append(m2.reshape(shape))
        out_v.append(v2.reshape(shape))
    return (loss_out, grad_x, *out_g, *out_d, *out_m, *out_v)
```

```python
import jax
import jax.numpy as jnp
from jax import lax
from jax.experimental import pallas as pl
from jax.experimental.pallas import tpu as pltpu

F32 = jnp.float32
BF16 = jnp.bfloat16

D_MODEL = 1024
N_META = 16
BLOCK = 128
LANES = 128
PREFIX = BLOCK
N_PAD = PREFIX - N_META
HEAD_DIM = 64
A_HEADS = 8
A_KV_HEADS = 2
A_GROUP = 4
B_HEADS = 8
B_PAIRS = B_HEADS // 2
A_WIDTH = A_HEADS * HEAD_DIM
A_KV_WIDTH = A_KV_HEADS * HEAD_DIM
B_WIDTH = B_HEADS * HEAD_DIM
W_IN_COLS = A_WIDTH + 2 * A_KV_WIDTH + 3 * B_WIDTH + B_HEADS + 2 * D_MODEL
SRC_KA = A_WIDTH
SRC_VA = SRC_KA + A_KV_WIDTH
SRC_QB = SRC_VA + A_KV_WIDTH
SRC_KB = SRC_QB + B_WIDTH
SRC_VB = SRC_KB + B_WIDTH
SRC_F = SRC_VB + B_WIDTH
SRC_GA = SRC_F + B_HEADS
SRC_GB = SRC_GA + D_MODEL
A_PAD_WIDTH = A_HEADS * LANES
B_PAD_WIDTH = B_HEADS * LANES
F_COLS = LANES
OFF_QA = 0
OFF_KA = OFF_QA + A_PAD_WIDTH
OFF_VA = OFF_KA + A_KV_WIDTH
OFF_QB = OFF_VA + A_KV_WIDTH
OFF_KB = OFF_QB + B_WIDTH
OFF_VB = OFF_KB + B_PAD_WIDTH
OFF_GA = OFF_VB + B_PAD_WIDTH
OFF_GB = OFF_GA + D_MODEL
OFF_F = OFF_GB + D_MODEL
TAIL_COLS = 128
P_COLS = OFF_F + F_COLS + TAIL_COLS
EPS = 1e-6
NEG = -1e30
SCALE = HEAD_DIM ** -0.5
KEY_BLOCKS = 4

ADAM_LR = 0.001
ADAM_B1 = 0.9
ADAM_B2 = 0.999
ADAM_EPS = 1e-08
ADAM_WD = 0.01
ADAM_STEP = 10

N_CHIPS = 4
N_DEV = 8
VMEM_LIMIT = 56 * 1024 * 1024

NT_DIMS = (((1,), (1,)), ((), ()))
TN_DIMS = (((0,), (0,)), ((), ()))
MESH_ID = pl.DeviceIdType.MESH
HBM_SPEC = pl.BlockSpec(memory_space=pltpu.HBM)
VMEM_SPEC = pl.BlockSpec(memory_space=pltpu.VMEM)


def _tile(n, target, mult=16):
    best = None
    for t in range(mult, min(n, target) + 1, mult):
        if n % t == 0:
            best = t
    return best if best is not None else n


def _cparams(sem):
    return pltpu.CompilerParams(dimension_semantics=sem, vmem_limit_bytes=VMEM_LIMIT)


def _rms_scale(h):
    return lax.rsqrt(jnp.mean(h * h, axis=-1, keepdims=True) + EPS)


def _rms_bwd(dn, h, w):
    r = _rms_scale(h)
    dw = jnp.sum(dn * (h * r), axis=0, keepdims=True)
    z = dn * w
    dh = r * z - h * ((r * r * r) * jnp.mean(z * h, axis=-1, keepdims=True))
    return dh, dw


def _ffn_fwd(h, norm_w, w_in, w_out, exchange=None):
    t, d = h.shape
    f = w_out.shape[0]
    tm = _tile(t, 272)
    tc = _tile(f, 256, 128)
    nj = f // tc
    ni = t // tm
    n_x = len(exchange.ins) if exchange else 0

    def body(*refs):
        h_ref, nw_ref, wi_ref, wo_ref = refs[:4]
        hout_ref, g_ref, u_ref = refs[4 + n_x:7 + n_x]
        a_scr = refs[7 + 2 * n_x]
        i = pl.program_id(0)
        if exchange:
            _host_exchange(exchange, refs[4:4 + n_x], refs[7 + n_x:7 + 2 * n_x], refs[8 + 2 * n_x:],
                           i == 0, i == ni - 2, i == ni - 1)
        hh = h_ref[...]
        n = ((hh * _rms_scale(hh)) * nw_ref[...]).astype(BF16)
        for j in range(nj):
            cols = slice(j * tc, (j + 1) * tc)
            g = jnp.dot(n, wi_ref[:, j * tc:(j + 1) * tc], preferred_element_type=F32)
            u = jnp.dot(n, wi_ref[:, f + j * tc:f + (j + 1) * tc], preferred_element_type=F32)
            g_ref[:, cols] = g
            u_ref[:, cols] = u
            a_scr[:, cols] = ((g * jax.nn.sigmoid(g)) * u).astype(BF16)
        hout_ref[...] = hh + 0.5 * jnp.dot(a_scr[...], wo_ref[...], preferred_element_type=F32)

    resident = lambda a: pl.BlockSpec(a.shape, lambda i: (0, 0), pipeline_mode=pl.Buffered(1))
    row = lambda w: pl.BlockSpec((tm, w), lambda i: (i, 0))
    outs = pl.pallas_call(
        body,
        name="ffn_fwd",
        grid=(ni,),
        in_specs=[row(d), pl.BlockSpec((1, d), lambda i: (0, 0)), resident(w_in), resident(w_out)] + [HBM_SPEC] * n_x,
        out_specs=[row(d), row(f), row(f)] + [HBM_SPEC] * n_x,
        out_shape=[
            jax.ShapeDtypeStruct((t, d), F32),
            jax.ShapeDtypeStruct((t, f), F32),
            jax.ShapeDtypeStruct((t, f), F32),
        ] + (exchange.out_shape if exchange else []),
        scratch_shapes=[pltpu.VMEM((tm, f), BF16)] + (exchange.scratch if exchange else []),
        compiler_params=_cparams(("arbitrary",) if exchange else ("parallel",)),
    )(h, norm_w, w_in, w_out, *(exchange.ins if exchange else []))
    return outs[:3], outs[3:]


def _ffn_bwd(dh_out, h, norm_w, g, u, w_in, w_out, exchange=None):
    t, d = h.shape
    f = w_out.shape[0]
    tm = _tile(t, 272)
    tc = _tile(f, 256, 128)
    nj = f // tc
    ni = t // tm
    n_x = len(exchange.ins) if exchange else 0

    def body(*refs):
        dho_ref, h_ref, nw_ref, g_ref, u_ref, wi_ref, wo_ref = refs[:7]
        dhin_ref, n_ref, a_ref, dgu_ref, df_ref, dnw_ref = refs[7 + n_x:13 + n_x]
        i = pl.program_id(0)
        if exchange:
            _host_exchange(exchange, refs[7:7 + n_x], refs[13 + n_x:13 + 2 * n_x], refs[13 + 2 * n_x:],
                           i == 0, i == ni - 1, i == ni - 1)
        hh = h_ref[...]
        nw = nw_ref[...]
        n_ref[...] = ((hh * _rms_scale(hh)) * nw).astype(BF16)
        dho = dho_ref[...]
        df = (0.5 * dho).astype(BF16)
        df_ref[...] = df
        for j in range(nj):
            cols = slice(j * tc, (j + 1) * tc)
            da = lax.dot_general(df, wo_ref[cols, :], NT_DIMS, preferred_element_type=F32)
            gg = g_ref[:, cols]
            uu = u_ref[:, cols]
            sig = jax.nn.sigmoid(gg)
            sl = gg * sig
            a_ref[:, cols] = (sl * uu).astype(BF16)
            dgu_ref[0, :, cols] = ((da * uu) * (sig * (1.0 + gg * (1.0 - sig)))).astype(BF16)
            dgu_ref[1, :, cols] = (da * sl).astype(BF16)
        dn = (lax.dot_general(dgu_ref[0], wi_ref[:, :f], NT_DIMS, preferred_element_type=F32)
              + lax.dot_general(dgu_ref[1], wi_ref[:, f:], NT_DIMS, preferred_element_type=F32))
        dh, dw = _rms_bwd(dn, hh, nw)
        dhin_ref[...] = dho + dh
        dnw_ref[0] = dw

    resident = lambda a: pl.BlockSpec(a.shape, lambda i: (0, 0), pipeline_mode=pl.Buffered(1))
    row = lambda w: pl.BlockSpec((tm, w), lambda i: (i, 0))
    outs = pl.pallas_call(
        body,
        name="ffn_bwd",
        grid=(ni,),
        in_specs=[row(d), row(d), pl.BlockSpec((1, d), lambda i: (0, 0)), row(f), row(f),
                  resident(w_in), resident(w_out)] + [HBM_SPEC] * n_x,
        out_specs=[row(d), row(d), row(f), pl.BlockSpec((2, tm, f), lambda i: (0, i, 0)), row(d),
                   pl.BlockSpec((1, 1, d), lambda i: (i, 0, 0))] + [HBM_SPEC] * n_x,
        out_shape=[
            jax.ShapeDtypeStruct((t, d), F32),
            jax.ShapeDtypeStruct((t, d), BF16),
            jax.ShapeDtypeStruct((t, f), BF16),
            jax.ShapeDtypeStruct((2, t, f), BF16),
            jax.ShapeDtypeStruct((t, d), BF16),
            jax.ShapeDtypeStruct((ni, 1, d), F32),
        ] + (exchange.out_shape if exchange else []),
        scratch_shapes=exchange.scratch if exchange else [],
        compiler_params=_cparams(("arbitrary",) if exchange else ("parallel",)),
    )(dh_out, h, norm_w, g, u, w_in, w_out, *(exchange.ins if exchange else []))
    return outs[:6], outs[6:]


def _tn_matmul(a, b, name):
    t, k = a.shape
    split = b.ndim == 3
    n = 2 * b.shape[2] if split else b.shape[1]
    tk = _tile(k, 512, 128)
    tn = _tile(b.shape[-1], 1408, 128)
    per_half = b.shape[-1] // tn

    def body(a_ref, b_ref, o_ref):
        o_ref[...] = lax.dot_general(a_ref[...], b_ref[...], TN_DIMS, preferred_element_type=F32)

    if split:
        b_spec = pl.BlockSpec((None, t, tn), lambda i, j: (j // per_half, 0, j % per_half))
    else:
        b_spec = pl.BlockSpec((t, tn), lambda i, j: (0, j))
    return pl.pallas_call(
        body,
        name=name,
        grid=(k // tk, n // tn),
        in_specs=[pl.BlockSpec((t, tk), lambda i, j: (0, i)), b_spec],
        out_specs=pl.BlockSpec((tk, tn), lambda i, j: (i, j)),
        out_shape=jax.ShapeDtypeStruct((k, n), F32),
        compiler_params=_cparams(("parallel", "parallel")),
    )(a, b)


PROJ_PARTS = (
    (OFF_QA, A_PAD_WIDTH, True), (OFF_KA, A_KV_WIDTH, True), (OFF_VA, A_KV_WIDTH, True),
    (OFF_QB, B_WIDTH, True), (OFF_KB, B_PAD_WIDTH, True), (OFF_VB, B_PAD_WIDTH, True),
    (OFF_GA, D_MODEL, False), (OFF_GB, D_MODEL, False), (OFF_F, F_COLS, False),
)


def _proj_fwd(h, norm_w, w_p):
    t, d = h.shape
    tm = _tile(t, 272)

    def body(h_ref, nw_ref, w_ref, u_ref, *part_refs):
        hh = h_ref[...]
        un = ((hh * _rms_scale(hh)) * nw_ref[...]).astype(BF16)
        u_ref[...] = un
        for (off, width, _), p_ref in zip(PROJ_PARTS, part_refs):
            p_ref[...] = jnp.dot(un, w_ref[:, off:off + width], preferred_element_type=F32).astype(p_ref.dtype)

    row = lambda w: pl.BlockSpec((tm, w), lambda i: (i, 0))
    return pl.pallas_call(
        body,
        name="proj_fwd",
        grid=(t // tm,),
        in_specs=[row(d), pl.BlockSpec((1, d), lambda i: (0, 0)), pl.BlockSpec(w_p.shape, lambda i: (0, 0))],
        out_specs=[row(d)] + [row(width) for _, width, _ in PROJ_PARTS],
        out_shape=[jax.ShapeDtypeStruct((t, d), BF16)]
        + [jax.ShapeDtypeStruct((t, width), BF16 if is_bf else F32) for _, width, is_bf in PROJ_PARTS],
        compiler_params=_cparams(("parallel",)),
    )(h, norm_w, w_p)


def _proj_bwd(dh_out, h, norm_w, dproj, w_p, exchange=None):
    t, d = h.shape
    n = w_p.shape[1]
    tm = _tile(t, 272)
    ni = t // tm
    n_x = len(exchange.ins) if exchange else 0

    def body(*refs):
        dho_ref, h_ref, nw_ref, dp_ref, w_ref = refs[:5]
        dhin_ref, dnw_ref = refs[5 + n_x:7 + n_x]
        if exchange:
            i = pl.program_id(0)
            _host_exchange(exchange, refs[5:5 + n_x], refs[7 + n_x:7 + 2 * n_x], refs[7 + 2 * n_x:],
                           i == 0, i == ni - 1, i == ni - 1)
        dn = lax.dot_general(dp_ref[...], w_ref[...], NT_DIMS, preferred_element_type=F32)
        dh, dw = _rms_bwd(dn, h_ref[...], nw_ref[...])
        dhin_ref[...] = dho_ref[...] + dh
        dnw_ref[0] = dw

    row = lambda w: pl.BlockSpec((tm, w), lambda i: (i, 0))
    outs = pl.pallas_call(
        body,
        name="proj_bwd",
        grid=(ni,),
        in_specs=[row(d), row(d), pl.BlockSpec((1, d), lambda i: (0, 0)), row(n),
                  pl.BlockSpec(w_p.shape, lambda i: (0, 0), pipeline_mode=pl.Buffered(1))] + [HBM_SPEC] * n_x,
        out_specs=[row(d), pl.BlockSpec((1, 1, d), lambda i: (i, 0, 0))] + [HBM_SPEC] * n_x,
        out_shape=[jax.ShapeDtypeStruct((t, d), F32), jax.ShapeDtypeStruct((ni, 1, d), F32)]
        + (exchange.out_shape if exchange else []),
        scratch_shapes=exchange.scratch if exchange else [],
        compiler_params=_cparams(("arbitrary",) if exchange else ("parallel",)),
    )(dh_out, h, norm_w, dproj, w_p, *(exchange.ins if exchange else []))
    return outs[:2], outs[2:]


def _merge_fwd(h, oa, ob, ga, gb, wa, wb, wo):
    t, d = h.shape
    tm = _tile(t, 544)

    def body(h_ref, oa_ref, ob_ref, ga_ref, gb_ref, wa_ref, wb_ref, wo_ref, hout_ref, mix_ref):
        ya = jnp.dot(oa_ref[...], wa_ref[...], preferred_element_type=F32)
        yb = jnp.dot(ob_ref[...], wb_ref[...], preferred_element_type=F32)
        mixed = (jax.nn.sigmoid(ga_ref[...]) * ya + jax.nn.sigmoid(gb_ref[...]) * yb).astype(BF16)
        mix_ref[...] = mixed
        hout_ref[...] = h_ref[...] + jnp.dot(mixed, wo_ref[...], preferred_element_type=F32)

    row = lambda w: pl.BlockSpec((tm, w), lambda i: (i, 0))
    full = lambda a: pl.BlockSpec(a.shape, lambda i: (0, 0))
    return pl.pallas_call(
        body,
        name="merge_fwd",
        grid=(t // tm,),
        in_specs=[row(d), row(oa.shape[1]), row(ob.shape[1]), row(d), row(d), full(wa), full(wb), full(wo)],
        out_specs=[row(d), row(d)],
        out_shape=[jax.ShapeDtypeStruct((t, d), F32), jax.ShapeDtypeStruct((t, d), BF16)],
        compiler_params=_cparams(("parallel",)),
    )(h, oa, ob, ga, gb, wa, wb, wo)


def _merge_bwd(dh, oa, ob, ga, gb, wa, wb, wo, exchange=None):
    t, d = dh.shape
    tm = _tile(t, 544)
    ni = t // tm
    n_x = len(exchange.ins) if exchange else 0

    def body(*refs):
        dh_ref, oa_ref, ob_ref, ga_ref, gb_ref, wa_ref, wb_ref, wo_ref = refs[:8]
        dya_ref, dyb_ref, doa_ref, dob_ref, dga_ref, dgb_ref, dhb_ref = refs[8 + n_x:15 + n_x]
        if exchange:
            i = pl.program_id(0)
            _host_exchange(exchange, refs[8:8 + n_x], refs[15 + n_x:15 + 2 * n_x], refs[15 + 2 * n_x:],
                           i == 0, i == ni - 1, i == ni - 1)
        dhb = dh_ref[...].astype(BF16)
        dhb_ref[...] = dhb
        dmix = lax.dot_general(dhb, wo_ref[...], NT_DIMS, preferred_element_type=F32)
        for o_ref, g_ref, w_ref, dy_ref, do_ref, dg_ref in (
                (oa_ref, ga_ref, wa_ref, dya_ref, doa_ref, dga_ref),
                (ob_ref, gb_ref, wb_ref, dyb_ref, dob_ref, dgb_ref)):
            y = jnp.dot(o_ref[...], w_ref[...], preferred_element_type=F32)
            s = jax.nn.sigmoid(g_ref[...])
            dy = (dmix * s).astype(BF16)
            dy_ref[...] = dy
            dg_ref[...] = ((dmix * y) * (s * (1.0 - s))).astype(BF16)
            do_ref[...] = lax.dot_general(dy, w_ref[...], NT_DIMS, preferred_element_type=F32).astype(BF16)

    row = lambda w: pl.BlockSpec((tm, w), lambda i: (i, 0))
    full = lambda a: pl.BlockSpec(a.shape, lambda i: (0, 0))
    wa_w, wb_w = oa.shape[1], ob.shape[1]
    outs = pl.pallas_call(
        body,
        name="merge_bwd",
        grid=(ni,),
        in_specs=[row(d), row(wa_w), row(wb_w), row(d), row(d), full(wa), full(wb), full(wo)] + [HBM_SPEC] * n_x,
        out_specs=[row(d), row(d), row(wa_w), row(wb_w), row(d), row(d), row(d)] + [HBM_SPEC] * n_x,
        out_shape=[
            jax.ShapeDtypeStruct((t, d), BF16), jax.ShapeDtypeStruct((t, d), BF16),
            jax.ShapeDtypeStruct((t, wa_w), BF16), jax.ShapeDtypeStruct((t, wb_w), BF16),
            jax.ShapeDtypeStruct((t, d), BF16), jax.ShapeDtypeStruct((t, d), BF16),
            jax.ShapeDtypeStruct((t, d), BF16),
        ] + (exchange.out_shape if exchange else []),
        scratch_shapes=exchange.scratch if exchange else [],
        compiler_params=_cparams(("arbitrary",) if exchange else ("parallel",)),
    )(dh, oa, ob, ga, gb, wa, wb, wo, *(exchange.ins if exchange else []))
    return outs[:7], outs[7:]


def _tri_dot(tri, x):
    hi = x.astype(BF16)
    r1 = x - hi.astype(F32)
    mid = r1.astype(BF16)
    lo = (r1 - mid.astype(F32)).astype(BF16)
    return (jnp.dot(tri, hi, preferred_element_type=F32)
            + jnp.dot(tri, mid, preferred_element_type=F32)
            + jnp.dot(tri, lo, preferred_element_type=F32))


def _forget_cumsum(f_logit, b_pad, nb):
    t, w = f_logit.shape
    bsz = t // (nb * BLOCK)

    def body(f_ref, b_ref, c_ref, carry):
        n = pl.program_id(1)

        @pl.when(n == 0)
        def _():
            carry[...] = jnp.zeros_like(carry)

        x = jax.nn.log_sigmoid(f_ref[...] + b_ref[...])
        rows = lax.broadcasted_iota(jnp.int32, (BLOCK, BLOCK), 0)
        cols = lax.broadcasted_iota(jnp.int32, (BLOCK, BLOCK), 1)
        tri = (cols <= rows).astype(BF16)
        c = _tri_dot(tri, x) + carry[...]
        c_ref[...] = c
        carry[...] = c[BLOCK - 1:BLOCK, :]

    return pl.pallas_call(
        body,
        name="forget_cumsum",
        grid=(bsz, nb),
        in_specs=[pl.BlockSpec((BLOCK, w), lambda b, n: (b * nb + n, 0)),
                  pl.BlockSpec((1, w), lambda b, n: (0, 0))],
        out_specs=pl.BlockSpec((BLOCK, w), lambda b, n: (b * nb + n, 0)),
        out_shape=jax.ShapeDtypeStruct((t, w), F32),
        scratch_shapes=[pltpu.VMEM((1, w), F32)],
        compiler_params=_cparams(("parallel", "arbitrary")),
    )(f_logit, b_pad)


def _forget_cumsum_bwd(dc, f_logit, b_pad, nb):
    t, w = f_logit.shape
    bsz = t // (nb * BLOCK)

    def body(dc_ref, f_ref, b_ref, df_ref, db_ref, carry):
        n = pl.program_id(1)

        @pl.when(n == 0)
        def _():
            carry[...] = jnp.zeros_like(carry)
            db_ref[...] = jnp.zeros_like(db_ref)

        rows = lax.broadcasted_iota(jnp.int32, (BLOCK, BLOCK), 0)
        cols = lax.broadcasted_iota(jnp.int32, (BLOCK, BLOCK), 1)
        tri = (cols >= rows).astype(BF16)
        dlf = _tri_dot(tri, dc_ref[...]) + carry[...]
        carry[...] = dlf[0:1, :]
        df = dlf * jax.nn.sigmoid(-(f_ref[...] + b_ref[...]))
        df_ref[...] = df.astype(BF16)
        db_ref[0] += jnp.sum(df, axis=0, keepdims=True)

    rev = lambda b, n: (b * nb + (nb - 1 - n), 0)
    return pl.pallas_call(
        body,
        name="forget_cumsum_bwd",
        grid=(bsz, nb),
        in_specs=[pl.BlockSpec((BLOCK, w), rev),
                  pl.BlockSpec((BLOCK, w), rev),
                  pl.BlockSpec((1, w), lambda b, n: (0, 0))],
        out_specs=[pl.BlockSpec((BLOCK, w), rev),
                   pl.BlockSpec((1, 1, w), lambda b, n: (b, 0, 0))],
        out_shape=[jax.ShapeDtypeStruct((t, w), BF16), jax.ShapeDtypeStruct((bsz, 1, w), F32)],
        scratch_shapes=[pltpu.VMEM((1, w), F32)],
        compiler_params=_cparams(("parallel", "arbitrary")),
    )(dc, f_logit, b_pad)


GROUP_ROWS = A_GROUP * BLOCK


def _stack_heads(ref):
    return jnp.concatenate([ref[:, i * LANES:(i + 1) * LANES] for i in range(A_GROUP)], axis=0)


def _unstack_heads(ref, x):
    for i in range(A_GROUP):
        ref[:, i * LANES:(i + 1) * LANES] = x[i * BLOCK:(i + 1) * BLOCK].astype(ref.dtype)


def _swa_logits(q, keys, slope, n):
    qi = lax.broadcasted_iota(jnp.int32, (GROUP_ROWS, BLOCK), 0) & (BLOCK - 1)
    kj = lax.broadcasted_iota(jnp.int32, (GROUP_ROWS, BLOCK), 1)
    s_all = lax.dot_general(q, keys, NT_DIMS, preferred_element_type=F32) * SCALE
    out = []
    for i, (dist, ok) in enumerate((
            (n * BLOCK + qi - kj, (kj >= N_PAD) & (n * BLOCK + qi - kj >= 0)),
            (BLOCK + qi - kj, (kj > qi) & (n >= 2)),
            (qi - kj, (kj <= qi) & (n >= 1)))):
        s = s_all[:, i * BLOCK:(i + 1) * BLOCK] - slope * dist.astype(F32)
        out.append(jnp.where(ok, s, NEG))
    return out


def _three_blocks(m_ref, p_ref, c_ref):
    return jnp.concatenate([m_ref[...], p_ref[...], c_ref[...]], axis=0)


def _swa_specs(nb):
    row = lambda b, n: b * nb + n
    qspec = pl.BlockSpec((BLOCK, A_GROUP * LANES), lambda b, g, n: (row(b, n), g))
    kv_m = pl.BlockSpec((BLOCK, LANES), lambda b, g, n: (row(b, 0), 0))
    kv_p = pl.BlockSpec((BLOCK, LANES), lambda b, g, n: (row(b, jnp.maximum(n - 1, 0)), 0))
    kv_c = pl.BlockSpec((BLOCK, LANES), lambda b, g, n: (row(b, n), 0))
    rowspec = pl.BlockSpec((1, GROUP_ROWS, 1), lambda b, g, n: (g, 0, 0))
    lsespec = pl.BlockSpec((1, 1, GROUP_ROWS, 1), lambda b, g, n: (row(b, n), g, 0, 0))
    return qspec, kv_m, kv_p, kv_c, rowspec, lsespec


def _swa_fwd(q, k, v, sink_rows, slope_rows, nb):
    t = q.shape[0]
    bsz = t // (nb * BLOCK)

    def body(q_ref, km_ref, kp_ref, kc_ref, vm_ref, vp_ref, vc_ref, sink_ref, slope_ref, o_ref, lse_ref):
        g = pl.program_id(1)
        n = pl.program_id(2)
        qq = _stack_heads(q_ref)
        sink = sink_ref[0]
        s_m, s_p, s_c = _swa_logits(qq, _three_blocks(km_ref, kp_ref, kc_ref), slope_ref[0], n)
        m = jnp.maximum(jnp.max(jnp.maximum(jnp.maximum(s_m, s_p), s_c), axis=-1, keepdims=True), sink)
        m_wide = jnp.broadcast_to(m, (GROUP_ROWS, BLOCK))
        e_m = jnp.exp(s_m - m_wide)
        e_p = jnp.exp(s_p - m_wide)
        e_c = jnp.exp(s_c - m_wide)
        z = jnp.sum((e_m + e_p) + e_c, axis=-1, keepdims=True) + jnp.exp(sink - m)
        inv = jnp.broadcast_to(1.0 / z, (GROUP_ROWS, BLOCK))
        probs = jnp.concatenate([(e_m * inv).astype(BF16), (e_p * inv).astype(BF16), (e_c * inv).astype(BF16)], axis=1)
        o = jnp.dot(probs, _three_blocks(vm_ref, vp_ref, vc_ref), preferred_element_type=F32)
        lane_group = lax.broadcasted_iota(jnp.int32, (GROUP_ROWS, LANES), 1) // HEAD_DIM
        _unstack_heads(o_ref, jnp.where(lane_group == g, o, 0.0))
        lse_ref[0, 0] = m + jnp.log(z)

    qspec, kv_m, kv_p, kv_c, rowspec, lsespec = _swa_specs(nb)
    return pl.pallas_call(
        body,
        name="swa_fwd",
        grid=(bsz, A_KV_HEADS, nb),
        in_specs=[qspec, kv_m, kv_p, kv_c, kv_m, kv_p, kv_c, rowspec, rowspec],
        out_specs=[qspec, lsespec],
        out_shape=[jax.ShapeDtypeStruct((t, A_PAD_WIDTH), BF16),
                   jax.ShapeDtypeStruct((t // BLOCK, A_KV_HEADS, GROUP_ROWS, 1), F32)],
        compiler_params=_cparams(("parallel", "parallel", "arbitrary")),
    )(q, k, k, k, v, v, v, sink_rows, slope_rows)


def _swa_bwd(q, k, v, do, lse, sink_rows, slope_rows, nb):
    t = q.shape[0]
    l = nb * BLOCK
    bsz = t // l

    def body(q_ref, km_ref, kp_ref, kc_ref, vm_ref, vp_ref, vc_ref, do_ref, lse_ref, sink_ref, slope_ref,
             dq_ref, dk_ref, dv_ref, dsink_ref, dk_acc, dv_acc):
        g = pl.program_id(1)
        n = pl.program_id(2)

        @pl.when((g == 0) & (n == 0))
        def _():
            dk_acc[...] = jnp.zeros_like(dk_acc)
            dv_acc[...] = jnp.zeros_like(dv_acc)

        @pl.when(n == 0)
        def _():
            dsink_ref[...] = jnp.zeros_like(dsink_ref)

        qq = _stack_heads(q_ref)
        dob = _stack_heads(do_ref)
        lse = lse_ref[0, 0]
        keys = _three_blocks(km_ref, kp_ref, kc_ref)
        lse_wide = jnp.broadcast_to(lse, (GROUP_ROWS, BLOCK))
        probs = [jnp.exp(s - lse_wide) for s in _swa_logits(qq, keys, slope_ref[0], n)]
        dp_all = lax.dot_general(dob, _three_blocks(vm_ref, vp_ref, vc_ref), NT_DIMS, preferred_element_type=F32)
        dps = [dp_all[:, i * BLOCK:(i + 1) * BLOCK] for i in range(3)]
        delta = jnp.sum((probs[0] * dps[0] + probs[1] * dps[1]) + probs[2] * dps[2], axis=-1, keepdims=True)
        delta_wide = jnp.broadcast_to(delta, (GROUP_ROWS, BLOCK))
        ds = jnp.concatenate([(p * (dp - delta_wide)).astype(BF16) for p, dp in zip(probs, dps)], axis=1)
        pb = jnp.concatenate([p.astype(BF16) for p in probs], axis=1)
        dq = jnp.dot(ds, keys, preferred_element_type=F32)
        dk_all = lax.dot_general(ds, qq, TN_DIMS, preferred_element_type=F32) * SCALE
        dv_all = lax.dot_general(pb, dob, TN_DIMS, preferred_element_type=F32)
        prev = jnp.maximum(n - 1, 0)
        for i, start in enumerate((0, prev * BLOCK, n * BLOCK)):
            rows = pl.ds(pl.multiple_of(start, BLOCK), BLOCK)
            dk_acc[rows, :] += dk_all[i * BLOCK:(i + 1) * BLOCK]
            dv_acc[rows, :] += dv_all[i * BLOCK:(i + 1) * BLOCK]
        _unstack_heads(dq_ref, dq * SCALE)
        dsink_ref[0, 0] += -(jnp.exp(sink_ref[0] - lse) * delta)

        @pl.when((g == A_KV_HEADS - 1) & (n == nb - 1))
        def _():
            dk_ref[...] = dk_acc[...].astype(BF16)
            dv_ref[...] = dv_acc[...].astype(BF16)

    qspec, kv_m, kv_p, kv_c, rowspec, lsespec = _swa_specs(nb)
    kv_all = pl.BlockSpec((l, LANES), lambda b, g, n: (b, 0))
    return pl.pallas_call(
        body,
        name="swa_bwd",
        grid=(bsz, A_KV_HEADS, nb),
        in_specs=[qspec, kv_m, kv_p, kv_c, kv_m, kv_p, kv_c, qspec, lsespec, rowspec, rowspec],
        out_specs=[qspec, kv_all, kv_all,
                   pl.BlockSpec((1, 1, GROUP_ROWS, 1), lambda b, g, n: (b, g, 0, 0))],
        out_shape=[jax.ShapeDtypeStruct((t, A_PAD_WIDTH), BF16),
                   jax.ShapeDtypeStruct((t, LANES), BF16),
                   jax.ShapeDtypeStruct((t, LANES), BF16),
                   jax.ShapeDtypeStruct((bsz, A_KV_HEADS, GROUP_ROWS, 1), F32)],
        scratch_shapes=[pltpu.VMEM((l, LANES), F32), pltpu.VMEM((l, LANES), F32)],
        compiler_params=_cparams(("parallel", "arbitrary", "arbitrary")),
    )(q, k, k, k, v, v, v, do, lse, sink_rows, slope_rows)


CHUNK = KEY_BLOCKS * BLOCK


def _fox_chunk(qb, ci):
    sb = jnp.maximum(jnp.minimum(KEY_BLOCKS * ci, qb + 1 - KEY_BLOCKS), 0)
    lo = jnp.maximum(ci * CHUNK, N_PAD)
    return sb, lo, pl.ds(pl.multiple_of(sb * BLOCK, BLOCK), CHUNK)


def _fox_logits(s_ref, cr_ref, e, j, sb, lo, qb):
    lane = lax.broadcasted_iota(jnp.int32, (BLOCK, BLOCK), 1)
    ahead = lane - lax.broadcasted_iota(jnp.int32, (BLOCK, BLOCK), 0)
    first = (sb + j) * BLOCK
    s = s_ref[e, :, j * BLOCK:(j + 1) * BLOCK] - cr_ref[e, sb + j]
    return jnp.where((ahead <= qb * BLOCK - first) & (lane >= lo - first), s, NEG)


FOX_PAIRS = 2
FOX_HEADS = 2 * FOX_PAIRS
FOX_STEPS = B_PAIRS // FOX_PAIRS


def _fox_specs(nb):
    l = nb * BLOCK
    q_spec = pl.BlockSpec((BLOCK, FOX_PAIRS * LANES), lambda b, p, i: (b * nb + i, p))
    kv_spec = pl.BlockSpec((l, FOX_HEADS * LANES), lambda b, p, i: (b, p))
    cc_spec = pl.BlockSpec((FOX_HEADS, BLOCK, 1), lambda b, p, i: (b * FOX_STEPS + p, i, 0))
    cr_spec = pl.BlockSpec((FOX_HEADS, nb, 1, BLOCK), lambda b, p, i: (b * FOX_STEPS + p, 0, 0, 0))
    return q_spec, kv_spec, cc_spec, cr_spec


def _fox_fwd(q, k, v, c_row, nb, exchange=None):
    t = q.shape[0]
    bsz = t // (nb * BLOCK)
    assert nb >= KEY_BLOCKS

    n_x = len(exchange.ins) if exchange else 0

    def body(*refs):
        q_ref, k_ref, v_ref, cr_ref = refs[:4]
        o_ref, ox_ref, lse_ref = refs[4 + n_x:7 + n_x]
        s_scr, hi_scr, lo_scr = refs[7 + 2 * n_x:10 + 2 * n_x]
        qb = pl.program_id(2)
        if exchange:
            first = (pl.program_id(0) == 0) & (pl.program_id(1) == 0)
            last = (pl.program_id(0) == bsz - 1) & (pl.program_id(1) == FOX_STEPS - 1)
            _host_exchange(exchange, refs[4:4 + n_x], refs[7 + n_x:7 + 2 * n_x], refs[10 + 2 * n_x:],
                           first & (qb == 0), last & (qb == 0), last & (qb == nb - 1))
        qs = [q_ref[:, a * LANES:(a + 1) * LANES] * SCALE for a in range(FOX_PAIRS)]
        first_half = lax.broadcasted_iota(jnp.int32, (BLOCK, LANES), 1) < HEAD_DIM

        def step(ci, carry):
            stats, accs = carry[:2 * FOX_HEADS], carry[2 * FOX_HEADS:]
            sb, lo, krows = _fox_chunk(qb, ci)
            new_stats, new_accs = [], []
            for a in range(FOX_PAIRS):
                alphas = []
                pv = jnp.zeros((BLOCK, LANES), F32)
                pv_lo = jnp.zeros((BLOCK, LANES), F32)
                for e in (2 * a, 2 * a + 1):
                    m, z = stats[2 * e], stats[2 * e + 1]
                    tile = slice(e * LANES, (e + 1) * LANES)
                    s_scr[e] = lax.dot_general(qs[a], k_ref[krows, tile], NT_DIMS, preferred_element_type=F32)
                    top = None
                    for j in range(KEY_BLOCKS):
                        s = _fox_logits(s_scr, cr_ref, e, j, sb, lo, qb)
                        s_scr[e, :, j * BLOCK:(j + 1) * BLOCK] = s
                        top = s if top is None else jnp.maximum(top, s)
                    m_new = jnp.maximum(m, jnp.max(top, axis=-1, keepdims=True))
                    alpha = jnp.exp(m - m_new)
                    m_wide = jnp.broadcast_to(m_new, (BLOCK, BLOCK))
                    total = None
                    for j in range(KEY_BLOCKS):
                        cols = slice(j * BLOCK, (j + 1) * BLOCK)
                        p = jnp.exp(s_scr[e, :, cols] - m_wide)
                        total = p if total is None else total + p
                        hi = p.astype(BF16)
                        hi_scr[e, :, cols] = hi
                        lo_scr[e, :, cols] = (p - hi.astype(F32)).astype(BF16)
                    z = alpha * z + jnp.sum(total, axis=-1, keepdims=True)
                    vv = v_ref[krows, tile]
                    pv = pv + jnp.dot(hi_scr[e], vv, preferred_element_type=F32)
                    pv_lo = pv_lo + jnp.dot(lo_scr[e], vv, preferred_element_type=F32)
                    new_stats += [m_new, z]
                    alphas.append(alpha)
                alpha = jnp.where(first_half, alphas[0], alphas[1])
                new_accs += [alpha * accs[2 * a] + pv, alpha * accs[2 * a + 1] + pv_lo]
            return (*new_stats, *new_accs)

        col = lambda val: jnp.full((BLOCK, 1), val, F32)
        done = lax.fori_loop(
            0, (qb + KEY_BLOCKS) // KEY_BLOCKS, step,
            (col(NEG), col(0.0)) * FOX_HEADS + (jnp.zeros((BLOCK, LANES), F32),) * (2 * FOX_PAIRS))
        for a in range(FOX_PAIRS):
            m0, z0, m1, z1 = done[4 * a:4 * a + 4]
            acc, acc_lo = done[2 * FOX_HEADS + 2 * a:2 * FOX_HEADS + 2 * a + 2]
            inv = 1.0 / jnp.where(first_half, z0, z1)
            tile = slice(a * LANES, (a + 1) * LANES)
            o_ref[:, tile] = (acc * inv).astype(BF16)
            ox_ref[:, tile] = (acc + acc_lo) * inv
            lse_ref[2 * a] = m0 + jnp.log(z0)
            lse_ref[2 * a + 1] = m1 + jnp.log(z1)

    q_spec, kv_spec, cc_spec, cr_spec = _fox_specs(nb)
    outs = pl.pallas_call(
        body,
        name="fox_fwd",
        grid=(bsz, FOX_STEPS, nb),
        in_specs=[q_spec, kv_spec, kv_spec, cr_spec] + [HBM_SPEC] * n_x,
        out_specs=[q_spec, q_spec, cc_spec] + [HBM_SPEC] * n_x,
        out_shape=[jax.ShapeDtypeStruct((t, B_WIDTH), BF16), jax.ShapeDtypeStruct((t, B_WIDTH), F32),
                   jax.ShapeDtypeStruct((bsz * B_HEADS, nb * BLOCK, 1), F32)] + (exchange.out_shape if exchange else []),
        scratch_shapes=[pltpu.VMEM((FOX_HEADS, BLOCK, CHUNK), F32), pltpu.VMEM((FOX_HEADS, BLOCK, CHUNK), BF16),
                        pltpu.VMEM((FOX_HEADS, BLOCK, CHUNK), BF16)] + (exchange.scratch if exchange else []),
        compiler_params=_cparams(("arbitrary",) * 3 if exchange else ("parallel", "parallel", "arbitrary")),
    )(q, k, v, c_row, *(exchange.ins if exchange else []))
    return outs[:3], outs[3:]


def _fox_bwd(q, k, v, o_exact, do, lse, c_row, nb, exchange=None):
    t = q.shape[0]
    l = nb * BLOCK
    bsz = t // l

    n_x = len(exchange.ins) if exchange else 0

    def body(*refs):
        q_ref, k_ref, v_ref, ox_ref, do_ref, lse_ref, cr_ref = refs[:7]
        dq_ref, dk_ref, dv_ref, dc_ref = refs[7 + n_x:11 + n_x]
        dk_acc, dv_acc, s_scr, dp_scr, p_scr, ds_scr = refs[11 + 2 * n_x:17 + 2 * n_x]
        qb = pl.program_id(2)
        if exchange:
            first = (pl.program_id(0) == 0) & (pl.program_id(1) == 0)
            last = (pl.program_id(0) == bsz - 1) & (pl.program_id(1) == FOX_STEPS - 1)
            _host_exchange(exchange, refs[7:7 + n_x], refs[11 + n_x:11 + 2 * n_x], refs[17 + 2 * n_x:],
                           first & (qb == 0), last & (qb == 0), last & (qb == nb - 1))

        @pl.when(qb == 0)
        def _():
            dk_acc[...] = jnp.zeros_like(dk_acc)
            dv_acc[...] = jnp.zeros_like(dv_acc)
            dc_ref[...] = jnp.zeros_like(dc_ref)

        top_half = lax.broadcasted_iota(jnp.int32, (LANES, BLOCK), 0) < HEAD_DIM
        pair_t = lambda x: jnp.concatenate([jnp.where(top_half, x.T, 0), jnp.where(top_half, 0, x.T)], axis=1)
        first_half = lax.broadcasted_iota(jnp.int32, (BLOCK, LANES), 1) < HEAD_DIM
        wide = lambda col: jnp.broadcast_to(col, (BLOCK, BLOCK))
        qs, dobs, qs_t, dob_t, deltas = [], [], [], [], []
        for a in range(FOX_PAIRS):
            tile = slice(a * LANES, (a + 1) * LANES)
            qs.append(q_ref[:, tile] * SCALE)
            dobs.append(do_ref[:, tile])
            qs_t.append(pair_t(qs[a]))
            dob_t.append(pair_t(dobs[a]))
            weighted = dobs[a].astype(F32) * ox_ref[:, tile]
            deltas += [wide(jnp.sum(jnp.where(first_half, weighted, 0.0), axis=-1, keepdims=True)),
                       wide(jnp.sum(jnp.where(first_half, 0.0, weighted), axis=-1, keepdims=True))]
        lses = [wide(lse_ref[e]) for e in range(FOX_HEADS)]

        def step(ci, dqs):
            sb, lo, krows = _fox_chunk(qb, ci)
            dqs = list(dqs)
            for a in range(FOX_PAIRS):
                for e in (2 * a, 2 * a + 1):
                    tile = slice(e * LANES, (e + 1) * LANES)
                    kk = k_ref[krows, tile]
                    s_scr[e] = lax.dot_general(qs[a], kk, NT_DIMS, preferred_element_type=F32)
                    dp_scr[e] = lax.dot_general(dobs[a], v_ref[krows, tile], NT_DIMS, preferred_element_type=F32)
                    for j in range(KEY_BLOCKS):
                        cols = slice(j * BLOCK, (j + 1) * BLOCK)
                        p = jnp.exp(_fox_logits(s_scr, cr_ref, e, j, sb, lo, qb) - lses[e])
                        ds = p * (dp_scr[e, :, cols] - deltas[e])
                        dc_ref[e, sb + j] -= jnp.sum(ds, axis=0, keepdims=True)
                        p_scr[e, :, cols] = p.astype(BF16)
                        ds_scr[e, :, cols] = ds.astype(BF16)
                    dqs[a] = dqs[a] + jnp.dot(ds_scr[e], kk, preferred_element_type=F32)
                both = slice(2 * a, 2 * a + 2)
                dk_t = jnp.dot(qs_t[a], ds_scr[both].reshape(2 * BLOCK, CHUNK), preferred_element_type=F32)
                dv_t = jnp.dot(dob_t[a], p_scr[both].reshape(2 * BLOCK, CHUNK), preferred_element_type=F32)
                for j in range(KEY_BLOCKS):
                    cols = slice(j * BLOCK, (j + 1) * BLOCK)
                    dk_acc[a * nb + sb + j] += dk_t[:, cols]
                    dv_acc[a * nb + sb + j] += dv_t[:, cols]
            return tuple(dqs)

        dqs = lax.fori_loop(0, (qb + KEY_BLOCKS) // KEY_BLOCKS, step,
                            (jnp.zeros((BLOCK, LANES), F32),) * FOX_PAIRS)
        for a in range(FOX_PAIRS):
            dq_ref[:, a * LANES:(a + 1) * LANES] = (dqs[a] * SCALE).astype(BF16)

        @pl.when(qb == nb - 1)
        def _():
            for a in range(FOX_PAIRS):
                for kb in range(nb):
                    rows = slice(kb * BLOCK, (kb + 1) * BLOCK)
                    for acc, out_ref in ((dk_acc, dk_ref), (dv_acc, dv_ref)):
                        pair = acc[a * nb + kb].T
                        out_ref[rows, 2 * a * LANES:(2 * a + 1) * LANES] = jnp.where(first_half, pair, 0.0).astype(BF16)
                        out_ref[rows, (2 * a + 1) * LANES:(2 * a + 2) * LANES] = (
                            jnp.where(first_half, 0.0, pair).astype(BF16))

    q_spec, kv_spec, cc_spec, cr_spec = _fox_specs(nb)
    outs = pl.pallas_call(
        body,
        name="fox_bwd",
        grid=(bsz, FOX_STEPS, nb),
        in_specs=[q_spec, kv_spec, kv_spec, q_spec, q_spec, cc_spec, cr_spec] + [HBM_SPEC] * n_x,
        out_specs=[q_spec, kv_spec, kv_spec, cr_spec] + [HBM_SPEC] * n_x,
        out_shape=[jax.ShapeDtypeStruct((t, B_WIDTH), BF16), jax.ShapeDtypeStruct((t, B_PAD_WIDTH), BF16),
                   jax.ShapeDtypeStruct((t, B_PAD_WIDTH), BF16),
                   jax.ShapeDtypeStruct((bsz * B_HEADS, nb, 1, BLOCK), F32)] + (exchange.out_shape if exchange else []),
        scratch_shapes=[pltpu.VMEM((FOX_PAIRS * nb, LANES, BLOCK), F32), pltpu.VMEM((FOX_PAIRS * nb, LANES, BLOCK), F32),
                        pltpu.VMEM((FOX_HEADS, BLOCK, CHUNK), F32), pltpu.VMEM((FOX_HEADS, BLOCK, CHUNK), F32),
                        pltpu.VMEM((FOX_HEADS, BLOCK, CHUNK), BF16), pltpu.VMEM((FOX_HEADS, BLOCK, CHUNK), BF16)]
        + (exchange.scratch if exchange else []),
        compiler_params=_cparams(("arbitrary",) * 3 if exchange else ("parallel", "parallel", "arbitrary")),
    )(q, k, v, o_exact, do, lse, c_row, *(exchange.ins if exchange else []))
    return outs[:4], outs[4:]


def _loss_head(h, final_w, target):
    bsz, l, d = h.shape
    nb = l // BLOCK

    def body(h_ref, w_ref, t_ref, loss_ref, dh_ref, dw_ref):
        b = pl.program_id(0)
        n = pl.program_id(1)

        @pl.when((b == 0) & (n == 0))
        def _():
            loss_ref[...] = jnp.zeros_like(loss_ref)
            dw_ref[...] = jnp.zeros_like(dw_ref)

        @pl.when(n == 0)
        def _():
            dh_ref[...] = jnp.zeros_like(dh_ref)

        @pl.when(n > 0)
        def _():
            hh = h_ref[0]
            w = w_ref[...]
            r = _rms_scale(hh)
            err = (hh * r) * w - t_ref[0]
            loss_ref[...] += 0.5 * jnp.sum(jnp.mean(err * err, axis=-1, keepdims=True), axis=0, keepdims=True)
            dy = err * (1.0 / d)
            dh, dw = _rms_bwd(dy, hh, w)
            dh_ref[0] = dh
            dw_ref[...] += dw

    return pl.pallas_call(
        body,
        name="loss_head",
        grid=(bsz, nb),
        in_specs=[
            pl.BlockSpec((1, BLOCK, d), lambda b, n: (b, n, 0)),
            pl.BlockSpec((1, d), lambda b, n: (0, 0)),
            pl.BlockSpec((1, BLOCK, d), lambda b, n: (b, jnp.maximum(n - 1, 0), 0)),
        ],
        out_specs=[
            pl.BlockSpec((1, 128), lambda b, n: (0, 0)),
            pl.BlockSpec((1, BLOCK, d), lambda b, n: (b, n, 0)),
            pl.BlockSpec((1, d), lambda b, n: (0, 0)),
        ],
        out_shape=[jax.ShapeDtypeStruct((1, 128), F32), jax.ShapeDtypeStruct((bsz, l, d), F32),
                   jax.ShapeDtypeStruct((1, d), F32)],
        compiler_params=_cparams(("arbitrary", "arbitrary")),
    )(h, final_w, target)


def _pad_tiles(w, src, heads, lane_slot, axis):
    pieces = []
    for h in range(heads):
        x = lax.slice_in_dim(w, src + HEAD_DIM * h, src + HEAD_DIM * (h + 1), axis=axis)
        z = jnp.zeros_like(x)
        pieces += [x, z] if lane_slot(h) == 0 else [z, x]
    return pieces


def _unpad_tiles(g, off, heads, lane_slot, axis):
    return [lax.slice_in_dim(g, off + LANES * h + HEAD_DIM * lane_slot(h),
                             off + LANES * h + HEAD_DIM * (lane_slot(h) + 1), axis=axis) for h in range(heads)]


A_SLOT = lambda h: h // A_GROUP
B_SLOT = lambda h: h % 2


def _layout_w_in(w):
    pad_f = jnp.zeros((w.shape[0], F_COLS - B_HEADS + TAIL_COLS), w.dtype)
    return jnp.concatenate(
        _pad_tiles(w, 0, A_HEADS, A_SLOT, 1) + [w[:, SRC_KA:SRC_KB]]
        + _pad_tiles(w, SRC_KB, B_HEADS, B_SLOT, 1) + _pad_tiles(w, SRC_VB, B_HEADS, B_SLOT, 1)
        + [w[:, SRC_GA:], w[:, SRC_F:SRC_GA], pad_f], axis=1)


def _unlayout_w_in(g):
    return jnp.concatenate(
        _unpad_tiles(g, OFF_QA, A_HEADS, A_SLOT, 1) + [g[:, OFF_KA:OFF_KB]]
        + _unpad_tiles(g, OFF_KB, B_HEADS, B_SLOT, 1) + _unpad_tiles(g, OFF_VB, B_HEADS, B_SLOT, 1)
        + [g[:, OFF_F:OFF_F + B_HEADS], g[:, OFF_GA:OFF_F]], axis=1)


def _local_step(x, target, meta, norms, b_forget, sinks, w, comm=None):
    n1, nmix, n2, nfin = norms
    w1i, w1o = w[:2]
    bsz, seq, d = x.shape
    l = PREFIX + seq
    nb = l // BLOCK
    t = bsz * l

    h0 = jnp.concatenate([jnp.zeros((bsz, N_PAD, d), F32),
                          jnp.broadcast_to(meta[None], (bsz, N_META, d)), x], axis=1).reshape(t, d)

    if comm is None:
        (h1, g1, u1), _ = _ffn_fwd(h0, n1, w1i, w1o)
        w_in, wa, wb, wo, w2i, w2o = w[2:]
    else:
        (h1, g1, u1), gathered = _ffn_fwd(h0, n1, w1i, w1o, comm.gather(GATHER_PROJ))
        w_in, = comm.gathered(GATHER_PROJ, gathered)
    wp = _layout_w_in(w_in)
    un, qa, ka, va, qb, kb, vb, ga, gb, f_logit = _proj_fwd(h1, nmix, wp)
    b_pad = jnp.concatenate([b_forget, jnp.zeros((1, F_COLS - B_HEADS), F32)], axis=1)
    c = _forget_cumsum(f_logit, b_pad, nb)
    c_heads = c[:, :B_HEADS].reshape(bsz, l, B_HEADS).transpose(0, 2, 1).reshape(bsz * B_HEADS, l)
    c_row = c_heads.reshape(bsz * B_HEADS, nb, 1, BLOCK)

    slopes = jnp.exp2(-8.0 * jnp.arange(1, A_HEADS + 1, dtype=F32) / A_HEADS)
    slope_rows = jnp.repeat(slopes.reshape(A_KV_HEADS, A_GROUP), BLOCK, axis=1)[:, :, None]
    sink_rows = jnp.repeat(sinks.reshape(A_KV_HEADS, A_GROUP), BLOCK, axis=1)[:, :, None]

    oa, lse_a = _swa_fwd(qa, ka, va, sink_rows, slope_rows, nb)
    if comm is None:
        (ob, ob_exact, lse_b), _ = _fox_fwd(qb, kb, vb, c_row, nb)
    else:
        (ob, ob_exact, lse_b), gathered = _fox_fwd(qb, kb, vb, c_row, nb, comm.gather(GATHER_LATE))
        wa, wb, wo, w2i, w2o = comm.gathered(GATHER_LATE, gathered)
    wa_p = jnp.concatenate(_pad_tiles(wa, 0, A_HEADS, A_SLOT, 0), axis=0)
    h2, mixed = _merge_fwd(h1, oa, ob, ga, gb, wa_p, wb, wo)
    (h3, g2, u2), _ = _ffn_fwd(h2, n2, w2i, w2o)
    loss, dh3, d_nfin = _loss_head(h3.reshape(bsz, l, d), nfin, target)

    (dh2, n2b, a2, dgu2, df2, dn2_parts), _ = _ffn_bwd(dh3.reshape(t, d), h2, n2, g2, u2, w2i, w2o)
    g_w2o = _tn_matmul(a2, df2, "grad_ffn2_w_out")
    g_w2i = _tn_matmul(n2b, dgu2, "grad_ffn2_w_in")

    hosted = comm.swap("ffn2", dict(ffn2_w_in=g_w2i, ffn2_w_out=g_w2o)) if comm else None
    (dya, dyb, doa, dob, dga, dgb, dh2b), swapped = _merge_bwd(dh2, oa, ob, ga, gb, wa_p, wb, wo, hosted)
    g_wo = _tn_matmul(mixed, dh2b, "grad_w_out")
    g_wa = jnp.concatenate(_unpad_tiles(_tn_matmul(oa, dya, "grad_w_branch_a"), 0, A_HEADS, A_SLOT, 0), axis=0)
    g_wb = _tn_matmul(ob, dyb, "grad_w_branch_b")

    dqa, dka, dva, dsink_rows = _swa_bwd(qa, ka, va, doa, lse_a, sink_rows, slope_rows, nb)
    hosted = comm.scatter("ffn2", swapped) if comm else None
    (dqb, dkb, dvb, dc_row), pieces = _fox_bwd(qb, kb, vb, ob_exact, dob, lse_b, c_row, nb, hosted)
    if comm:
        comm.received("ffn2", pieces)
    dc = dc_row.reshape(bsz, B_HEADS, l).transpose(0, 2, 1).reshape(t, B_HEADS)
    dc = jnp.concatenate([dc, jnp.zeros((t, F_COLS - B_HEADS), F32)], axis=1)
    df_logit, db_parts = _forget_cumsum_bwd(dc, f_logit, b_pad, nb)

    dproj = jnp.concatenate([dqa, dka, dva, dqb, dkb, dvb, dga, dgb, df_logit,
                             jnp.zeros((t, TAIL_COLS), BF16)], axis=1)
    g_win = _unlayout_w_in(_tn_matmul(un, dproj, "grad_w_in"))
    hosted = comm.swap("mixer", dict(w_in=g_win, w_branch_a=g_wa, w_branch_b=g_wb, w_out=g_wo)) if comm else None
    (dh1, dnmix_parts), swapped = _proj_bwd(dh2, h1, nmix, dproj, wp, hosted)
    hosted = comm.scatter("mixer", swapped) if comm else None
    (dh0, n1b, a1, dgu1, df1, dn1_parts), pieces = _ffn_bwd(dh1, h0, n1, g1, u1, w1i, w1o, hosted)
    if comm:
        comm.received("mixer", pieces)
    g_w1o = _tn_matmul(a1, df1, "grad_ffn1_w_out")
    g_w1i = _tn_matmul(n1b, dgu1, "grad_ffn1_w_in")

    dh0 = dh0.reshape(bsz, l, d)
    grad_x = dh0[:, PREFIX:]
    small = dict(
        meta_tokens=jnp.sum(dh0[:, N_PAD:PREFIX], axis=0),
        ffn1_norm=jnp.sum(dn1_parts, axis=0),
        mix_norm=jnp.sum(dnmix_parts, axis=0),
        ffn2_norm=jnp.sum(dn2_parts, axis=0),
        final_norm=d_nfin,
        b_forget=jnp.sum(db_parts, axis=0)[:, :B_HEADS],
        attn_sinks=jnp.sum(dsink_rows.reshape(bsz, A_HEADS, BLOCK), axis=(0, 2)).reshape(1, A_HEADS),
    )
    big = dict(ffn1_w_in=g_w1i, ffn1_w_out=g_w1o, w_in=g_win, w_branch_a=g_wa, w_branch_b=g_wb,
               w_out=g_wo, ffn2_w_in=g_w2i, ffn2_w_out=g_w2o)
    return loss, grad_x, small, big


BIG = (
    ("ffn1_w_in", (D_MODEL, 5632), 1),
    ("ffn1_w_out", (2816, D_MODEL), 0),
    ("w_in", (D_MODEL, W_IN_COLS), 1),
    ("w_branch_a", (A_WIDTH, D_MODEL), 1),
    ("w_branch_b", (B_WIDTH, D_MODEL), 1),
    ("w_out", (D_MODEL, D_MODEL), 0),
    ("ffn2_w_in", (D_MODEL, 5632), 1),
    ("ffn2_w_out", (2816, D_MODEL), 0),
)
STACKED = "w_in"


def _coords():
    return lax.axis_index("x"), lax.axis_index("y"), lax.axis_index("c")


def _other_chips(x, y):
    return ((1 - x, y), (x, 1 - y), (1 - x, 1 - y))


def _chip_part(ref, name, shape, axis, k):
    if name == STACKED:
        return ref.at[k]
    size = shape[axis] // N_CHIPS
    start = pl.multiple_of(k * size, size)
    return ref.at[pl.ds(start, size), :] if axis == 0 else ref.at[:, pl.ds(start, size)]


def _full_shape(name, shape):
    return (N_CHIPS, shape[0], shape[1] // N_CHIPS) if name == STACKED else shape


class _Exchange:
    def __init__(self, ins, out_shape, n_sems, ops):
        self.ins, self.out_shape, self.n_sems, self.ops = list(ins), list(out_shape), n_sems, ops

    @property
    def scratch(self):
        return [pltpu.SemaphoreType.DMA((self.n_sems,)), pltpu.SemaphoreType.DMA((self.n_sems,))]


SEMS_PER_GATHER = 7


def _gather_exchange(shards, table):
    n = len(table)

    def ops(ins, outs, send_sems, recv_sems):
        x, y, c = _coords()
        mine = 2 * x + y
        sibling = (x, y, 1 - c)
        chips = _other_chips(x, y)
        slots = [2 * chip[0] + chip[1] for chip in chips]

        def part(i, k):
            name, shape, axis = table[i][:3]
            return _chip_part(outs[i], name, shape, axis, k)

        def half(ref, h):
            rows = ref.shape[0] // 2
            return ref.at[pl.ds(pl.multiple_of(h * rows, rows), rows), :]

        def own(i):
            sem = SEMS_PER_GATHER * i
            return pltpu.make_async_remote_copy(ins[i], part(i, mine), send_sems.at[sem], recv_sems.at[sem],
                                                device_id=sibling, device_id_type=MESH_ID)

        def fetch(i, j, slot):
            sem = SEMS_PER_GATHER * i + 1 + j
            if table[i][4]:
                src, dst = half(ins[i], c), half(part(i, slot), c)
            else:
                src, dst = ins[i], part(i, slot)
            return pltpu.make_async_remote_copy(src, dst, send_sems.at[sem], recv_sems.at[sem],
                                                device_id=(chips[j][0], chips[j][1], c), device_id_type=MESH_ID)

        def forward(i, j, h):
            sem = SEMS_PER_GATHER * i + 4 + j
            region = half(part(i, slots[j]), h)
            return pltpu.make_async_remote_copy(region, region, send_sems.at[sem], recv_sems.at[sem],
                                                device_id=sibling, device_id_type=MESH_ID)

        def start():
            for i in range(n):
                for j in range(3):
                    fetch(i, j, mine).start()
            for i in range(n):
                own(i).start()

        def relay():
            for i in range(n):
                for j in range(3):
                    fetch(i, j, slots[j]).wait_recv()
                    if table[i][4]:
                        forward(i, j, c).start()

        def finish():
            for i in range(n):
                own(i).wait()
                for j in range(3):
                    if table[i][4]:
                        forward(i, j, 1 - c).wait_recv()
                        forward(i, j, c).wait_send()
                    fetch(i, j, mine).wait_send()

        return start, relay, finish

    out_shape = [jax.ShapeDtypeStruct(_full_shape(name, shape), dtype) for name, shape, _, dtype, _ in table]
    return _Exchange(shards, out_shape, SEMS_PER_GATHER * n, ops)


def _run_exchange(exchange, name):
    n = len(exchange.ins)

    def body(*refs):
        start, relay, finish = exchange.ops(refs[:n], refs[n:2 * n], *refs[2 * n:])
        start()
        relay()
        finish()

    return pl.pallas_call(
        body,
        name=name,
        in_specs=[HBM_SPEC] * n,
        out_specs=[HBM_SPEC] * n,
        out_shape=exchange.out_shape,
        scratch_shapes=exchange.scratch,
    )(*exchange.ins)


def _host_exchange(exchange, in_refs, out_refs, sem_refs, first, middle, last):
    start, relay, finish = exchange.ops(in_refs, out_refs, *sem_refs)
    pl.when(first)(start)
    pl.when(middle)(relay)
    pl.when(last)(finish)


def _halves_view(name, shape, axis):
    r, c = shape
    if name == STACKED:
        return (N_CHIPS, 2, r // 2, c // N_CHIPS), lambda ref, h: ref.at[:, h]
    if axis == 1:
        return (2, r // 2, c), lambda ref, h: ref.at[h]
    return (N_CHIPS, 2, r // N_CHIPS // 2, c), lambda ref, h: ref.at[:, h]


def _halves_exchange(grads, entries):
    n_w = len(entries)
    views = [_halves_view(*entry) for entry in entries]

    def ops(ins, outs, send_sems, recv_sems):
        x, y, c = _coords()
        copies = [pltpu.make_async_remote_copy(views[i][1](ins[i], 1 - c), outs[i], send_sems.at[i], recv_sems.at[i],
                                               device_id=(x, y, 1 - c), device_id_type=MESH_ID) for i in range(n_w)]

        def start():
            for cp in copies:
                cp.start()

        def finish():
            for cp in copies:
                cp.wait()

        return start, lambda: None, finish

    half_shape = lambda v: tuple(d for i, d in enumerate(v) if i != (1 if len(v) == 4 else 0))
    out_shape = [jax.ShapeDtypeStruct(half_shape(v[0]), F32) for v in views]
    return _Exchange([g.reshape(v[0]) for g, v in zip(grads, views)], out_shape, n_w, ops)


def _add_sibling(g_view, recv, c, name):
    shape = recv.shape
    if len(shape) == 2:
        tr = _tile(shape[0], 128, 16)
        grid = (shape[0] // tr,)
        g_spec = pl.BlockSpec((None, tr, shape[1]), lambda i, c_ref: (c_ref[0], i, 0))
        r_spec = pl.BlockSpec((tr, shape[1]), lambda i, c_ref: (i, 0))
    else:
        tr = _tile(shape[1], 256, 16)
        grid = (N_CHIPS, shape[1] // tr)
        g_spec = pl.BlockSpec((None, None, tr, shape[2]), lambda k, i, c_ref: (k, c_ref[0], i, 0))
        r_spec = pl.BlockSpec((None, tr, shape[2]), lambda k, i, c_ref: (k, i, 0))

    def body(c_ref, g_ref, r_ref, o_ref):
        o_ref[...] = (g_ref[...] + r_ref[...]).astype(BF16)

    return pl.pallas_call(
        body,
        name="add_sibling_" + name,
        grid_spec=pltpu.PrefetchScalarGridSpec(num_scalar_prefetch=1, grid=grid, in_specs=[g_spec, r_spec],
                                               out_specs=r_spec),
        out_shape=jax.ShapeDtypeStruct(shape, BF16),
        compiler_params=_cparams(("parallel",) * len(grid)),
    )(c, g_view, recv)


def _piece_of(ref, name, axis, k):
    if name == STACKED or axis == 0:
        return ref.at[k]
    size = ref.shape[1] // N_CHIPS
    return ref.at[:, pl.ds(pl.multiple_of(k * size, size), size)]


def _piece_shape(name, shape, axis):
    r, c = shape
    return (r // 2, c // N_CHIPS) if (axis == 1) else (r // N_CHIPS // 2, c)


def _scatter_exchange(partials, entries):
    n_w = len(entries)

    def ops(ins, outs, send_sems, recv_sems):
        x, y, c = _coords()
        chips = _other_chips(x, y)
        copies = []
        for i, (name, _, axis) in enumerate(entries):
            for j, chip in enumerate(chips):
                sem = 3 * i + j
                copies.append(pltpu.make_async_remote_copy(
                    _piece_of(ins[i], name, axis, 2 * chip[0] + chip[1]), outs[i].at[j], send_sems.at[sem],
                    recv_sems.at[sem], device_id=(chip[0], chip[1], c), device_id_type=MESH_ID))

        def start():
            for cp in copies:
                cp.start()

        def finish():
            for cp in copies:
                cp.wait()

        return start, lambda: None, finish

    out_shape = [jax.ShapeDtypeStruct((3,) + _piece_shape(*entry), BF16) for entry in entries]
    return _Exchange(partials, out_shape, 3 * n_w, ops)


def _add_chips(partial, recv, mine, name, axis):
    rows, cols = recv.shape[1:]
    tr = _tile(rows, 256, 16)
    if name == STACKED or axis == 0:
        p_spec = pl.BlockSpec((None, tr, cols), lambda i, k_ref: (k_ref[0], i, 0))
    else:
        p_spec = pl.BlockSpec((tr, cols), lambda i, k_ref: (i, k_ref[0]))

    def body(k_ref, p_ref, r_ref, o_ref):
        f32 = lambda a: a.astype(F32)
        o_ref[...] = ((f32(p_ref[...]) + f32(r_ref[0])) + f32(r_ref[1])) + f32(r_ref[2])

    return pl.pallas_call(
        body,
        name="add_chips_" + name,
        grid_spec=pltpu.PrefetchScalarGridSpec(
            num_scalar_prefetch=1, grid=(rows // tr,),
            in_specs=[p_spec, pl.BlockSpec((3, tr, cols), lambda i, k_ref: (0, i, 0))],
            out_specs=pl.BlockSpec((tr, cols), lambda i, k_ref: (i, 0))),
        out_shape=jax.ShapeDtypeStruct((rows, cols), F32),
        compiler_params=_cparams(("parallel",)),
    )(mine, partial, recv)


def _share_with_sibling(halves):
    n_w = len(halves)

    def body(*refs):
        ins, outs = refs[:n_w], refs[n_w:2 * n_w]
        send_sems, recv_sems = refs[2 * n_w:]
        x, y, c = _coords()
        copies = [pltpu.make_async_remote_copy(ins[i], outs[i], send_sems.at[i], recv_sems.at[i],
                                               device_id=(x, y, 1 - c), device_id_type=MESH_ID) for i in range(n_w)]
        for cp in copies:
            cp.start()
        for cp in copies:
            cp.wait()

    return pl.pallas_call(
        body,
        name="share_with_sibling",
        in_specs=[HBM_SPEC] * n_w,
        out_specs=[HBM_SPEC] * n_w,
        out_shape=[jax.ShapeDtypeStruct(h.shape, F32) for h in halves],
        scratch_shapes=[pltpu.SemaphoreType.DMA((n_w,)), pltpu.SemaphoreType.DMA((n_w,))],
    )(*halves)


SMALL_ROWS = 168


def _all_reduce_small(buf):
    def body(b_ref, out_ref, gathered, send_sems, recv_sems):
        x, y, c = _coords()
        me = 4 * x + 2 * y + c
        peers = [(x ^ fx, y ^ fy, c ^ fc) for fx in (0, 1) for fy in (0, 1) for fc in (0, 1)][1:]

        def copy(j, slot, dev):
            return pltpu.make_async_remote_copy(b_ref, gathered.at[slot], send_sems.at[j], recv_sems.at[j],
                                                device_id=dev, device_id_type=MESH_ID)

        for j, dev in enumerate(peers):
            copy(j, me, dev).start()
        gathered[me] = b_ref[...]
        for j, dev in enumerate(peers):
            copy(j, 4 * dev[0] + 2 * dev[1] + dev[2], dev).wait()
        acc = gathered[0]
        for d in range(1, N_DEV):
            acc = acc + gathered[d]
        out_ref[...] = acc

    return pl.pallas_call(
        body,
        name="all_reduce_small",
        in_specs=[VMEM_SPEC],
        out_specs=VMEM_SPEC,
        out_shape=jax.ShapeDtypeStruct(buf.shape, F32),
        scratch_shapes=[pltpu.VMEM((N_DEV,) + buf.shape, F32), pltpu.SemaphoreType.DMA((N_DEV - 1,)),
                        pltpu.SemaphoreType.DMA((N_DEV - 1,))],
    )(buf)


def _adamw(w, g, m, v):
    r, c = w.shape
    tr = _tile(r, max(8, (5 << 19) // (4 * c)), 8)

    def body(w_ref, g_ref, m_ref, v_ref, d_ref, mo_ref, vo_ref):
        gg = g_ref[...]
        mm = ADAM_B1 * m_ref[...] + (1.0 - ADAM_B1) * gg
        vv = ADAM_B2 * v_ref[...] + (1.0 - ADAM_B2) * (gg * gg)
        m_hat = mm / (1.0 - ADAM_B1 ** ADAM_STEP)
        v_hat = vv / (1.0 - ADAM_B2 ** ADAM_STEP)
        d_ref[...] = -ADAM_LR * (m_hat / (jnp.sqrt(v_hat) + ADAM_EPS) + ADAM_WD * w_ref[...])
        mo_ref[...] = mm
        vo_ref[...] = vv

    spec = pl.BlockSpec((tr, c), lambda i: (i, 0))
    return pl.pallas_call(
        body,
        name="adamw",
        grid=(r // tr,),
        in_specs=[spec] * 4,
        out_specs=[spec] * 3,
        out_shape=[jax.ShapeDtypeStruct((r, c), F32)] * 3,
        compiler_params=_cparams(("parallel",)),
    )(w, g, m, v)


def _adamw_halves(w, own, other, m, v, c, name):
    r, cols = w.shape
    half = r // 2
    tr = _tile(half, 256, 8)
    nt = half // tr
    whole = pl.BlockSpec((tr, cols), lambda h, i, c_ref: (h * nt + i, 0))
    part = pl.BlockSpec((tr, cols), lambda h, i, c_ref: (i, 0))

    def body(c_ref, w_ref, own_ref, other_ref, m_ref, v_ref, g_ref, d_ref, mo_ref, vo_ref):
        gg = jnp.where(pl.program_id(0) == c_ref[0], own_ref[...], other_ref[...])
        g_ref[...] = gg
        mm = ADAM_B1 * m_ref[...] + (1.0 - ADAM_B1) * gg
        vv = ADAM_B2 * v_ref[...] + (1.0 - ADAM_B2) * (gg * gg)
        m_hat = mm / (1.0 - ADAM_B1 ** ADAM_STEP)
        v_hat = vv / (1.0 - ADAM_B2 ** ADAM_STEP)
        d_ref[...] = -ADAM_LR * (m_hat / (jnp.sqrt(v_hat) + ADAM_EPS) + ADAM_WD * w_ref[...])
        mo_ref[...] = mm
        vo_ref[...] = vv

    return pl.pallas_call(
        body,
        name="adamw_" + name,
        grid_spec=pltpu.PrefetchScalarGridSpec(
            num_scalar_prefetch=1, grid=(2, nt),
            in_specs=[whole, part, part, whole, whole], out_specs=[whole] * 4),
        out_shape=[jax.ShapeDtypeStruct((r, cols), F32)] * 4,
        compiler_params=_cparams(("parallel", "parallel")),
    )(c, w, own, other, m, v)


GATHER_FIRST = ("ffn1_w_in", "ffn1_w_out")
GATHER_PROJ = ("w_in",)
GATHER_LATE = ("w_branch_a", "w_branch_b", "w_out", "ffn2_w_in", "ffn2_w_out")


class _Comm:
    def __init__(self, shards, c_arr, mine_arr):
        self.shards, self.c, self.mine = shards, c_arr, mine_arr
        self.groups, self.halves = {}, {}
        self.by_name = {entry[0]: entry for entry in BIG}

    def gather(self, names):
        table = [self.by_name[n] + (BF16, True) for n in names]
        return _gather_exchange([self.shards[n] for n in names], table)

    def gathered(self, names, outs):
        return [o.transpose(1, 0, 2).reshape(D_MODEL, W_IN_COLS) if n == STACKED else o for n, o in zip(names, outs)]

    def swap(self, tag, grads):
        entries = [self.by_name[n] for n in grads]
        arrays = [g.reshape(D_MODEL, N_CHIPS, W_IN_COLS // N_CHIPS).transpose(1, 0, 2) if n == STACKED else g
                  for n, g in grads.items()]
        self.groups[tag] = (entries, arrays)
        return _halves_exchange(arrays, entries)

    def scatter(self, tag, received):
        entries, arrays = self.groups[tag]
        views = [_halves_view(*entry) for entry in entries]
        partials = [_add_sibling(g.reshape(v[0]), r, self.c, name)
                    for g, v, r, (name, _, _) in zip(arrays, views, received, entries)]
        self.groups[tag] = (entries, partials)
        return _scatter_exchange(partials, entries)

    def received(self, tag, pieces):
        entries, partials = self.groups[tag]
        for p, r, (name, _, axis) in zip(partials, pieces, entries):
            self.halves[name] = _add_chips(p, r, self.mine, name, axis)

    def finish(self):
        names = [n for n, _, _ in BIG]
        own = [self.halves[n] for n in names]
        return dict(zip(names, zip(own, _share_with_sibling(own))))


def kernel(x, meta_tokens, ffn1_norm, ffn1_w_in, ffn1_w_out, mix_norm, w_in, b_forget, attn_sinks, w_branch_a, w_branch_b, w_out, ffn2_norm, ffn2_w_in, ffn2_w_out, final_norm, loss_target, m_meta_tokens, m_ffn1_norm, m_ffn1_w_in, m_ffn1_w_out, m_mix_norm, m_w_in, m_b_forget, m_attn_sinks, m_w_branch_a, m_w_branch_b, m_w_out, m_ffn2_norm, m_ffn2_w_in, m_ffn2_w_out, m_final_norm, v_meta_tokens, v_ffn1_norm, v_ffn1_w_in, v_ffn1_w_out, v_mix_norm, v_w_in, v_b_forget, v_attn_sinks, v_w_branch_a, v_w_branch_b, v_w_out, v_ffn2_norm, v_ffn2_w_in, v_ffn2_w_out, v_final_norm):
    given = dict(locals())
    names = ["meta_tokens", "ffn1_norm", "ffn1_w_in", "ffn1_w_out", "mix_norm", "w_in", "b_forget", "attn_sinks",
             "w_branch_a", "w_branch_b", "w_out", "ffn2_norm", "ffn2_w_in", "ffn2_w_out", "final_norm"]
    big_names = [n for n, _, _ in BIG]
    cx, cy, cc = _coords()
    c_arr = cc.reshape(1).astype(jnp.int32)
    mine_arr = (2 * cx + cy).reshape(1).astype(jnp.int32)

    comm = _Comm({n: given[n][0].astype(BF16) for n in big_names}, c_arr, mine_arr)
    table = [comm.by_name[n] + (BF16, True) for n in GATHER_FIRST] + [("meta_tokens", (N_META, D_MODEL), 1, F32, False)]
    first = _gather_exchange([comm.shards[n] for n in GATHER_FIRST] + [meta_tokens], table)
    w1i, w1o, meta_full = _run_exchange(first, "gather_first")
    norms = (ffn1_norm, mix_norm, ffn2_norm, final_norm.reshape(1, D_MODEL))
    loss, grad_x, small, big = _local_step(x, loss_target, meta_full, norms, b_forget, attn_sinks, (w1i, w1o), comm)

    swap = comm.swap("ffn1", dict(ffn1_w_in=big["ffn1_w_in"], ffn1_w_out=big["ffn1_w_out"]))
    last = comm.scatter("ffn1", _run_exchange(swap, "exchange_halves"))
    comm.received("ffn1", _run_exchange(last, "scatter_chip_sums"))
    grad_halves = comm.finish()
    grads = {}

    pad_lanes = lambda a: jnp.concatenate([a, jnp.zeros((1, LANES - a.shape[1]), F32)], axis=1)
    buf = jnp.concatenate([
        small["meta_tokens"].reshape(128, LANES),
        small["ffn1_norm"].reshape(8, LANES), small["mix_norm"].reshape(8, LANES),
        small["ffn2_norm"].reshape(8, LANES), small["final_norm"].reshape(8, LANES),
        loss, pad_lanes(small["b_forget"]), pad_lanes(small["attn_sinks"]),
        jnp.zeros((SMALL_ROWS - 163, LANES), F32)], axis=0)
    red = _all_reduce_small(buf)
    meta_cols = red[:128].reshape(N_META, D_MODEL)
    grads["meta_tokens"] = lax.dynamic_slice_in_dim(meta_cols, (2 * cx + cy) * (D_MODEL // N_CHIPS),
                                                    D_MODEL // N_CHIPS, axis=1)
    grads["ffn1_norm"] = red[128:136].reshape(1, D_MODEL)
    grads["mix_norm"] = red[136:144].reshape(1, D_MODEL)
    grads["ffn2_norm"] = red[144:152].reshape(1, D_MODEL)
    grads["final_norm"] = red[152:160].reshape(1, D_MODEL)
    loss_out = red[160, 0]
    grads["b_forget"] = red[161:162, :B_HEADS]
    grads["attn_sinks"] = red[162:163, :A_HEADS]

    out_g, out_d, out_m, out_v = [], [], [], []
    for n in names:
        w_full = given[n]
        shape = w_full.shape
        two_d = (lambda a: a.reshape(shape[-2], shape[-1])) if len(shape) >= 2 else (lambda a: a.reshape(1, shape[0]))
        if n in grad_halves:
            own, other = grad_halves[n]
            g2, d2, m2, v2 = _adamw_halves(two_d(w_full), own, other, two_d(given["m_" + n]),
                                           two_d(given["v_" + n]), c_arr, n)
        else:
            g2 = two_d(grads[n])
            d2, m2, v2 = _adamw(two_d(w_full), g2, two_d(given["m_" + n]), two_d(given["v_" + n]))
        out_g.append(g2.reshape(shape))
        out_d.append(d2.reshape(shape))
        out_m.append(m2.reshape(shape))
        out_v.append(v2.reshape(shape))
    return (loss_out, grad_x, *out_g, *out_d, *out_m, *out_v)
```

```python
import jax
import jax.numpy as jnp
from jax import lax
from jax.experimental import pallas as pl
from jax.experimental.pallas import tpu as pltpu

F32 = jnp.float32
BF16 = jnp.bfloat16

D_MODEL = 1024
N_META = 16
BLOCK = 128
LANES = 128
PREFIX = BLOCK
N_PAD = PREFIX - N_META
HEAD_DIM = 64
A_HEADS = 8
A_KV_HEADS = 2
A_GROUP = 4
B_HEADS = 8
B_PAIRS = B_HEADS // 2
A_WIDTH = A_HEADS * HEAD_DIM
A_KV_WIDTH = A_KV_HEADS * HEAD_DIM
B_WIDTH = B_HEADS * HEAD_DIM
W_IN_COLS = A_WIDTH + 2 * A_KV_WIDTH + 3 * B_WIDTH + B_HEADS + 2 * D_MODEL
SRC_KA = A_WIDTH
SRC_VA = SRC_KA + A_KV_WIDTH
SRC_QB = SRC_VA + A_KV_WIDTH
SRC_KB = SRC_QB + B_WIDTH
SRC_VB = SRC_KB + B_WIDTH
SRC_F = SRC_VB + B_WIDTH
SRC_GA = SRC_F + B_HEADS
SRC_GB = SRC_GA + D_MODEL
A_PAD_WIDTH = A_HEADS * LANES
B_PAD_WIDTH = B_HEADS * LANES
F_COLS = LANES
OFF_QA = 0
OFF_KA = OFF_QA + A_PAD_WIDTH
OFF_VA = OFF_KA + A_KV_WIDTH
OFF_QB = OFF_VA + A_KV_WIDTH
OFF_KB = OFF_QB + B_WIDTH
OFF_VB = OFF_KB + B_PAD_WIDTH
OFF_GA = OFF_VB + B_PAD_WIDTH
OFF_GB = OFF_GA + D_MODEL
OFF_F = OFF_GB + D_MODEL
TAIL_COLS = 128
P_COLS = OFF_F + F_COLS + TAIL_COLS
EPS = 1e-6
NEG = -1e30
SCALE = HEAD_DIM ** -0.5
KEY_BLOCKS = 4

ADAM_LR = 0.001
ADAM_B1 = 0.9
ADAM_B2 = 0.999
ADAM_EPS = 1e-08
ADAM_WD = 0.01
ADAM_STEP = 10

N_CHIPS = 4
N_DEV = 8
VMEM_LIMIT = 56 * 1024 * 1024

NT_DIMS = (((1,), (1,)), ((), ()))
TN_DIMS = (((0,), (0,)), ((), ()))
MESH_ID = pl.DeviceIdType.MESH
HBM_SPEC = pl.BlockSpec(memory_space=pltpu.HBM)
VMEM_SPEC = pl.BlockSpec(memory_space=pltpu.VMEM)


def _tile(n, target, mult=16):
    best = None
    for t in range(mult, min(n, target) + 1, mult):
        if n % t == 0:
            best = t
    return best if best is not None else n


def _cparams(sem):
    return pltpu.CompilerParams(dimension_semantics=sem, vmem_limit_bytes=VMEM_LIMIT)


def _rms_scale(h):
    return lax.rsqrt(jnp.mean(h * h, axis=-1, keepdims=True) + EPS)


def _rms_bwd(dn, h, w):
    r = _rms_scale(h)
    dw = jnp.sum(dn * (h * r), axis=0, keepdims=True)
    z = dn * w
    dh = r * z - h * ((r * r * r) * jnp.mean(z * h, axis=-1, keepdims=True))
    return dh, dw


def _ffn_fwd(h, norm_w, w_in, w_out, exchange=None):
    t, d = h.shape
    f = w_out.shape[0]
    tm = _tile(t, 272)
    tc = _tile(f, 256, 128)
    nj = f // tc
    ni = t // tm
    n_x = len(exchange.ins) if exchange else 0

    def body(*refs):
        h_ref, nw_ref, wi_ref, wo_ref = refs[:4]
        hout_ref, g_ref, u_ref = refs[4 + n_x:7 + n_x]
        a_scr = refs[7 + 2 * n_x]
        i = pl.program_id(0)
        if exchange:
            _host_exchange(exchange, refs[4:4 + n_x], refs[7 + n_x:7 + 2 * n_x], refs[8 + 2 * n_x:],
                           i == 0, i == ni - 2, i == ni - 1)
        hh = h_ref[...]
        n = ((hh * _rms_scale(hh)) * nw_ref[...]).astype(BF16)
        for j in range(nj):
            cols = slice(j * tc, (j + 1) * tc)
            g = jnp.dot(n, wi_ref[:, j * tc:(j + 1) * tc], preferred_element_type=F32)
            u = jnp.dot(n, wi_ref[:, f + j * tc:f + (j + 1) * tc], preferred_element_type=F32)
            g_ref[:, cols] = g
            u_ref[:, cols] = u
            a_scr[:, cols] = ((g * jax.nn.sigmoid(g)) * u).astype(BF16)
        hout_ref[...] = hh + 0.5 * jnp.dot(a_scr[...], wo_ref[...], preferred_element_type=F32)

    resident = lambda a: pl.BlockSpec(a.shape, lambda i: (0, 0), pipeline_mode=pl.Buffered(1))
    row = lambda w: pl.BlockSpec((tm, w), lambda i: (i, 0))
    outs = pl.pallas_call(
        body,
        name="ffn_fwd",
        grid=(ni,),
        in_specs=[row(d), pl.BlockSpec((1, d), lambda i: (0, 0)), resident(w_in), resident(w_out)] + [HBM_SPEC] * n_x,
        out_specs=[row(d), row(f), row(f)] + [HBM_SPEC] * n_x,
        out_shape=[
            jax.ShapeDtypeStruct((t, d), F32),
            jax.ShapeDtypeStruct((t, f), F32),
            jax.ShapeDtypeStruct((t, f), F32),
        ] + (exchange.out_shape if exchange else []),
        scratch_shapes=[pltpu.VMEM((tm, f), BF16)] + (exchange.scratch if exchange else []),
        compiler_params=_cparams(("arbitrary",) if exchange else ("parallel",)),
    )(h, norm_w, w_in, w_out, *(exchange.ins if exchange else []))
    return outs[:3], outs[3:]


def _ffn_bwd(dh_out, h, norm_w, g, u, w_in, w_out, exchange=None):
    t, d = h.shape
    f = w_out.shape[0]
    tm = _tile(t, 272)
    tc = _tile(f, 256, 128)
    nj = f // tc
    ni = t // tm
    n_x = len(exchange.ins) if exchange else 0

    def body(*refs):
        dho_ref, h_ref, nw_ref, g_ref, u_ref, wi_ref, wo_ref = refs[:7]
        dhin_ref, n_ref, a_ref, dgu_ref, df_ref, dnw_ref = refs[7 + n_x:13 + n_x]
        i = pl.program_id(0)
        if exchange:
            _host_exchange(exchange, refs[7:7 + n_x], refs[13 + n_x:13 + 2 * n_x], refs[13 + 2 * n_x:],
                           i == 0, i == ni - 1, i == ni - 1)
        hh = h_ref[...]
        nw = nw_ref[...]
        n_ref[...] = ((hh * _rms_scale(hh)) * nw).astype(BF16)
        dho = dho_ref[...]
        df = (0.5 * dho).astype(BF16)
        df_ref[...] = df
        for j in range(nj):
            cols = slice(j * tc, (j + 1) * tc)
            da = lax.dot_general(df, wo_ref[cols, :], NT_DIMS, preferred_element_type=F32)
            gg = g_ref[:, cols]
            uu = u_ref[:, cols]
            sig = jax.nn.sigmoid(gg)
            sl = gg * sig
            a_ref[:, cols] = (sl * uu).astype(BF16)
            dgu_ref[0, :, cols] = ((da * uu) * (sig * (1.0 + gg * (1.0 - sig)))).astype(BF16)
            dgu_ref[1, :, cols] = (da * sl).astype(BF16)
        dn = (lax.dot_general(dgu_ref[0], wi_ref[:, :f], NT_DIMS, preferred_element_type=F32)
              + lax.dot_general(dgu_ref[1], wi_ref[:, f:], NT_DIMS, preferred_element_type=F32))
        dh, dw = _rms_bwd(dn, hh, nw)
        dhin_ref[...] = dho + dh
        dnw_ref[0] = dw

    resident = lambda a: pl.BlockSpec(a.shape, lambda i: (0, 0), pipeline_mode=pl.Buffered(1))
    row = lambda w: pl.BlockSpec((tm, w), lambda i: (i, 0))
    outs = pl.pallas_call(
        body,
        name="ffn_bwd",
        grid=(ni,),
        in_specs=[row(d), row(d), pl.BlockSpec((1, d), lambda i: (0, 0)), row(f), row(f),
                  resident(w_in), resident(w_out)] + [HBM_SPEC] * n_x,
        out_specs=[row(d), row(d), row(f), pl.BlockSpec((2, tm, f), lambda i: (0, i, 0)), row(d),
                   pl.BlockSpec((1, 1, d), lambda i: (i, 0, 0))] + [HBM_SPEC] * n_x,
        out_shape=[
            jax.ShapeDtypeStruct((t, d), F32),
            jax.ShapeDtypeStruct((t, d), BF16),
            jax.ShapeDtypeStruct((t, f), BF16),
            jax.ShapeDtypeStruct((2, t, f), BF16),
            jax.ShapeDtypeStruct((t, d), BF16),
            jax.ShapeDtypeStruct((ni, 1, d), F32),
        ] + (exchange.out_shape if exchange else []),
        scratch_shapes=exchange.scratch if exchange else [],
        compiler_params=_cparams(("arbitrary",) if exchange else ("parallel",)),
    )(dh_out, h, norm_w, g, u, w_in, w_out, *(exchange.ins if exchange else []))
    return outs[:6], outs[6:]


def _tn_matmul(a, b, name):
    t, k = a.shape
    split = b.ndim == 3
    n = 2 * b.shape[2] if split else b.shape[1]
    tk = _tile(k, 512, 128)
    tn = _tile(b.shape[-1], 1408, 128)
    per_half = b.shape[-1] // tn

    def body(a_ref, b_ref, o_ref):
        o_ref[...] = lax.dot_general(a_ref[...], b_ref[...], TN_DIMS, preferred_element_type=F32)

    if split:
        b_spec = pl.BlockSpec((None, t, tn), lambda i, j: (j // per_half, 0, j % per_half))
    else:
        b_spec = pl.BlockSpec((t, tn), lambda i, j: (0, j))
    return pl.pallas_call(
        body,
        name=name,
        grid=(k // tk, n // tn),
        in_specs=[pl.BlockSpec((t, tk), lambda i, j: (0, i)), b_spec],
        out_specs=pl.BlockSpec((tk, tn), lambda i, j: (i, j)),
        out_shape=jax.ShapeDtypeStruct((k, n), F32),
        compiler_params=_cparams(("parallel", "parallel")),
    )(a, b)


PROJ_PARTS = (
    (OFF_QA, A_PAD_WIDTH, True), (OFF_KA, A_KV_WIDTH, True), (OFF_VA, A_KV_WIDTH, True),
    (OFF_QB, B_WIDTH, True), (OFF_KB, B_PAD_WIDTH, True), (OFF_VB, B_PAD_WIDTH, True),
    (OFF_GA, D_MODEL, False), (OFF_GB, D_MODEL, False), (OFF_F, F_COLS, False),
)


def _proj_fwd(h, norm_w, w_p):
    t, d = h.shape
    tm = _tile(t, 272)

    def body(h_ref, nw_ref, w_ref, u_ref, *part_refs):
        hh = h_ref[...]
        un = ((hh * _rms_scale(hh)) * nw_ref[...]).astype(BF16)
        u_ref[...] = un
        for (off, width, _), p_ref in zip(PROJ_PARTS, part_refs):
            p_ref[...] = jnp.dot(un, w_ref[:, off:off + width], preferred_element_type=F32).astype(p_ref.dtype)

    row = lambda w: pl.BlockSpec((tm, w), lambda i: (i, 0))
    return pl.pallas_call(
        body,
        name="proj_fwd",
        grid=(t // tm,),
        in_specs=[row(d), pl.BlockSpec((1, d), lambda i: (0, 0)), pl.BlockSpec(w_p.shape, lambda i: (0, 0))],
        out_specs=[row(d)] + [row(width) for _, width, _ in PROJ_PARTS],
        out_shape=[jax.ShapeDtypeStruct((t, d), BF16)]
        + [jax.ShapeDtypeStruct((t, width), BF16 if is_bf else F32) for _, width, is_bf in PROJ_PARTS],
        compiler_params=_cparams(("parallel",)),
    )(h, norm_w, w_p)


def _proj_bwd(dh_out, h, norm_w, dproj, w_p, exchange=None):
    t, d = h.shape
    n = w_p.shape[1]
    tm = _tile(t, 272)
    ni = t // tm
    n_x = len(exchange.ins) if exchange else 0

    def body(*refs):
        dho_ref, h_ref, nw_ref, dp_ref, w_ref = refs[:5]
        dhin_ref, dnw_ref = refs[5 + n_x:7 + n_x]
        if exchange:
            i = pl.program_id(0)
            _host_exchange(exchange, refs[5:5 + n_x], refs[7 + n_x:7 + 2 * n_x], refs[7 + 2 * n_x:],
                           i == 0, i == ni - 1, i == ni - 1)
        dn = lax.dot_general(dp_ref[...], w_ref[...], NT_DIMS, preferred_element_type=F32)
        dh, dw = _rms_bwd(dn, h_ref[...], nw_ref[...])
        dhin_ref[...] = dho_ref[...] + dh
        dnw_ref[0] = dw

    row = lambda w: pl.BlockSpec((tm, w), lambda i: (i, 0))
    outs = pl.pallas_call(
        body,
        name="proj_bwd",
        grid=(ni,),
        in_specs=[row(d), row(d), pl.BlockSpec((1, d), lambda i: (0, 0)), row(n),
                  pl.BlockSpec(w_p.shape, lambda i: (0, 0), pipeline_mode=pl.Buffered(1))] + [HBM_SPEC] * n_x,
        out_specs=[row(d), pl.BlockSpec((1, 1, d), lambda i: (i, 0, 0))] + [HBM_SPEC] * n_x,
        out_shape=[jax.ShapeDtypeStruct((t, d), F32), jax.ShapeDtypeStruct((ni, 1, d), F32)]
        + (exchange.out_shape if exchange else []),
        scratch_shapes=exchange.scratch if exchange else [],
        compiler_params=_cparams(("arbitrary",) if exchange else ("parallel",)),
    )(dh_out, h, norm_w, dproj, w_p, *(exchange.ins if exchange else []))
    return outs[:2], outs[2:]


def _merge_fwd(h, oa, ob, ga, gb, wa, wb, wo):
    t, d = h.shape
    tm = _tile(t, 544)

    def body(h_ref, oa_ref, ob_ref, ga_ref, gb_ref, wa_ref, wb_ref, wo_ref, hout_ref, mix_ref):
        ya = jnp.dot(oa_ref[...], wa_ref[...], preferred_element_type=F32)
        yb = jnp.dot(ob_ref[...], wb_ref[...], preferred_element_type=F32)
        mixed = (jax.nn.sigmoid(ga_ref[...]) * ya + jax.nn.sigmoid(gb_ref[...]) * yb).astype(BF16)
        mix_ref[...] = mixed
        hout_ref[...] = h_ref[...] + jnp.dot(mixed, wo_ref[...], preferred_element_type=F32)

    row = lambda w: pl.BlockSpec((tm, w), lambda i: (i, 0))
    full = lambda a: pl.BlockSpec(a.shape, lambda i: (0, 0))
    return pl.pallas_call(
        body,
        name="merge_fwd",
        grid=(t // tm,),
        in_specs=[row(d), row(oa.shape[1]), row(ob.shape[1]), row(d), row(d), full(wa), full(wb), full(wo)],
        out_specs=[row(d), row(d)],
        out_shape=[jax.ShapeDtypeStruct((t, d), F32), jax.ShapeDtypeStruct((t, d), BF16)],
        compiler_params=_cparams(("parallel",)),
    )(h, oa, ob, ga, gb, wa, wb, wo)


def _merge_bwd(dh, oa, ob, ga, gb, wa, wb, wo, exchange=None):
    t, d = dh.shape
    tm = _tile(t, 544)
    ni = t // tm
    n_x = len(exchange.ins) if exchange else 0

    def body(*refs):
        dh_ref, oa_ref, ob_ref, ga_ref, gb_ref, wa_ref, wb_ref, wo_ref = refs[:8]
        dya_ref, dyb_ref, doa_ref, dob_ref, dga_ref, dgb_ref, dhb_ref = refs[8 + n_x:15 + n_x]
        if exchange:
            i = pl.program_id(0)
            _host_exchange(exchange, refs[8:8 + n_x], refs[15 + n_x:15 + 2 * n_x], refs[15 + 2 * n_x:],
                           i == 0, i == ni - 1, i == ni - 1)
        dhb = dh_ref[...].astype(BF16)
        dhb_ref[...] = dhb
        dmix = lax.dot_general(dhb, wo_ref[...], NT_DIMS, preferred_element_type=F32)
        for o_ref, g_ref, w_ref, dy_ref, do_ref, dg_ref in (
                (oa_ref, ga_ref, wa_ref, dya_ref, doa_ref, dga_ref),
                (ob_ref, gb_ref, wb_ref, dyb_ref, dob_ref, dgb_ref)):
            y = jnp.dot(o_ref[...], w_ref[...], preferred_element_type=F32)
            s = jax.nn.sigmoid(g_ref[...])
            dy = (dmix * s).astype(BF16)
            dy_ref[...] = dy
            dg_ref[...] = ((dmix * y) * (s * (1.0 - s))).astype(BF16)
            do_ref[...] = lax.dot_general(dy, w_ref[...], NT_DIMS, preferred_element_type=F32).astype(BF16)

    row = lambda w: pl.BlockSpec((tm, w), lambda i: (i, 0))
    full = lambda a: pl.BlockSpec(a.shape, lambda i: (0, 0))
    wa_w, wb_w = oa.shape[1], ob.shape[1]
    outs = pl.pallas_call(
        body,
        name="merge_bwd",
        grid=(ni,),
        in_specs=[row(d), row(wa_w), row(wb_w), row(d), row(d), full(wa), full(wb), full(wo)] + [HBM_SPEC] * n_x,
        out_specs=[row(d), row(d), row(wa_w), row(wb_w), row(d), row(d), row(d)] + [HBM_SPEC] * n_x,
        out_shape=[
            jax.ShapeDtypeStruct((t, d), BF16), jax.ShapeDtypeStruct((t, d), BF16),
            jax.ShapeDtypeStruct((t, wa_w), BF16), jax.ShapeDtypeStruct((t, wb_w), BF16),
            jax.ShapeDtypeStruct((t, d), BF16), jax.ShapeDtypeStruct((t, d), BF16),
            jax.ShapeDtypeStruct((t, d), BF16),
        ] + (exchange.out_shape if exchange else []),
        scratch_shapes=exchange.scratch if exchange else [],
        compiler_params=_cparams(("arbitrary",) if exchange else ("parallel",)),
    )(dh, oa, ob, ga, gb, wa, wb, wo, *(exchange.ins if exchange else []))
    return outs[:7], outs[7:]


def _tri_dot(tri, x):
    hi = x.astype(BF16)
    r1 = x - hi.astype(F32)
    mid = r1.astype(BF16)
    lo = (r1 - mid.astype(F32)).astype(BF16)
    return (jnp.dot(tri, hi, preferred_element_type=F32)
            + jnp.dot(tri, mid, preferred_element_type=F32)
            + jnp.dot(tri, lo, preferred_element_type=F32))


def _forget_cumsum(f_logit, b_pad, nb):
    t, w = f_logit.shape
    bsz = t // (nb * BLOCK)

    def body(f_ref, b_ref, c_ref, carry):
        n = pl.program_id(1)

        @pl.when(n == 0)
        def _():
            carry[...] = jnp.zeros_like(carry)

        x = jax.nn.log_sigmoid(f_ref[...] + b_ref[...])
        rows = lax.broadcasted_iota(jnp.int32, (BLOCK, BLOCK), 0)
        cols = lax.broadcasted_iota(jnp.int32, (BLOCK, BLOCK), 1)
        tri = (cols <= rows).astype(BF16)
        c = _tri_dot(tri, x) + carry[...]
        c_ref[...] = c
        carry[...] = c[BLOCK - 1:BLOCK, :]

    return pl.pallas_call(
        body,
        name="forget_cumsum",
        grid=(bsz, nb),
        in_specs=[pl.BlockSpec((BLOCK, w), lambda b, n: (b * nb + n, 0)),
                  pl.BlockSpec((1, w), lambda b, n: (0, 0))],
        out_specs=pl.BlockSpec((BLOCK, w), lambda b, n: (b * nb + n, 0)),
        out_shape=jax.ShapeDtypeStruct((t, w), F32),
        scratch_shapes=[pltpu.VMEM((1, w), F32)],
        compiler_params=_cparams(("parallel", "arbitrary")),
    )(f_logit, b_pad)


def _forget_cumsum_bwd(dc, f_logit, b_pad, nb):
    t, w = f_logit.shape
    bsz = t // (nb * BLOCK)

    def body(dc_ref, f_ref, b_ref, df_ref, db_ref, carry):
        n = pl.program_id(1)

        @pl.when(n == 0)
        def _():
            carry[...] = jnp.zeros_like(carry)
            db_ref[...] = jnp.zeros_like(db_ref)

        rows = lax.broadcasted_iota(jnp.int32, (BLOCK, BLOCK), 0)
        cols = lax.broadcasted_iota(jnp.int32, (BLOCK, BLOCK), 1)
        tri = (cols >= rows).astype(BF16)
        dlf = _tri_dot(tri, dc_ref[...]) + carry[...]
        carry[...] = dlf[0:1, :]
        df = dlf * jax.nn.sigmoid(-(f_ref[...] + b_ref[...]))
        df_ref[...] = df.astype(BF16)
        db_ref[0] += jnp.sum(df, axis=0, keepdims=True)

    rev = lambda b, n: (b * nb + (nb - 1 - n), 0)
    return pl.pallas_call(
        body,
        name="forget_cumsum_bwd",
        grid=(bsz, nb),
        in_specs=[pl.BlockSpec((BLOCK, w), rev),
                  pl.BlockSpec((BLOCK, w), rev),
                  pl.BlockSpec((1, w), lambda b, n: (0, 0))],
        out_specs=[pl.BlockSpec((BLOCK, w), rev),
                   pl.BlockSpec((1, 1, w), lambda b, n: (b, 0, 0))],
        out_shape=[jax.ShapeDtypeStruct((t, w), BF16), jax.ShapeDtypeStruct((bsz, 1, w), F32)],
        scratch_shapes=[pltpu.VMEM((1, w), F32)],
        compiler_params=_cparams(("parallel", "arbitrary")),
    )(dc, f_logit, b_pad)


GROUP_ROWS = A_GROUP * BLOCK


def _stack_heads(ref):
    return jnp.concatenate([ref[:, i * LANES:(i + 1) * LANES] for i in range(A_GROUP)], axis=0)


def _unstack_heads(ref, x):
    for i in range(A_GROUP):
        ref[:, i * LANES:(i + 1) * LANES] = x[i * BLOCK:(i + 1) * BLOCK].astype(ref.dtype)


def _swa_logits(q, keys, slope, n):
    qi = lax.broadcasted_iota(jnp.int32, (GROUP_ROWS, BLOCK), 0) & (BLOCK - 1)
    kj = lax.broadcasted_iota(jnp.int32, (GROUP_ROWS, BLOCK), 1)
    s_all = lax.dot_general(q, keys, NT_DIMS, preferred_element_type=F32) * SCALE
    out = []
    for i, (dist, ok) in enumerate((
            (n * BLOCK + qi - kj, (kj >= N_PAD) & (n * BLOCK + qi - kj >= 0)),
            (BLOCK + qi - kj, (kj > qi) & (n >= 2)),
            (qi - kj, (kj <= qi) & (n >= 1)))):
        s = s_all[:, i * BLOCK:(i + 1) * BLOCK] - slope * dist.astype(F32)
        out.append(jnp.where(ok, s, NEG))
    return out


def _three_blocks(m_ref, p_ref, c_ref):
    return jnp.concatenate([m_ref[...], p_ref[...], c_ref[...]], axis=0)


def _swa_specs(nb):
    row = lambda b, n: b * nb + n
    qspec = pl.BlockSpec((BLOCK, A_GROUP * LANES), lambda b, g, n: (row(b, n), g))
    kv_m = pl.BlockSpec((BLOCK, LANES), lambda b, g, n: (row(b, 0), 0))
    kv_p = pl.BlockSpec((BLOCK, LANES), lambda b, g, n: (row(b, jnp.maximum(n - 1, 0)), 0))
    kv_c = pl.BlockSpec((BLOCK, LANES), lambda b, g, n: (row(b, n), 0))
    rowspec = pl.BlockSpec((1, GROUP_ROWS, 1), lambda b, g, n: (g, 0, 0))
    lsespec = pl.BlockSpec((1, 1, GROUP_ROWS, 1), lambda b, g, n: (row(b, n), g, 0, 0))
    return qspec, kv_m, kv_p, kv_c, rowspec, lsespec


def _swa_fwd(q, k, v, sink_rows, slope_rows, nb):
    t = q.shape[0]
    bsz = t // (nb * BLOCK)

    def body(q_ref, km_ref, kp_ref, kc_ref, vm_ref, vp_ref, vc_ref, sink_ref, slope_ref, o_ref, lse_ref):
        g = pl.program_id(1)
        n = pl.program_id(2)
        qq = _stack_heads(q_ref)
        sink = sink_ref[0]
        s_m, s_p, s_c = _swa_logits(qq, _three_blocks(km_ref, kp_ref, kc_ref), slope_ref[0], n)
        m = jnp.maximum(jnp.max(jnp.maximum(jnp.maximum(s_m, s_p), s_c), axis=-1, keepdims=True), sink)
        m_wide = jnp.broadcast_to(m, (GROUP_ROWS, BLOCK))
        e_m = jnp.exp(s_m - m_wide)
        e_p = jnp.exp(s_p - m_wide)
        e_c = jnp.exp(s_c - m_wide)
        z = jnp.sum((e_m + e_p) + e_c, axis=-1, keepdims=True) + jnp.exp(sink - m)
        inv = jnp.broadcast_to(1.0 / z, (GROUP_ROWS, BLOCK))
        probs = jnp.concatenate([(e_m * inv).astype(BF16), (e_p * inv).astype(BF16), (e_c * inv).astype(BF16)], axis=1)
        o = jnp.dot(probs, _three_blocks(vm_ref, vp_ref, vc_ref), preferred_element_type=F32)
        lane_group = lax.broadcasted_iota(jnp.int32, (GROUP_ROWS, LANES), 1) // HEAD_DIM
        _unstack_heads(o_ref, jnp.where(lane_group == g, o, 0.0))
        lse_ref[0, 0] = m + jnp.log(z)

    qspec, kv_m, kv_p, kv_c, rowspec, lsespec = _swa_specs(nb)
    return pl.pallas_call(
        body,
        name="swa_fwd",
        grid=(bsz, A_KV_HEADS, nb),
        in_specs=[qspec, kv_m, kv_p, kv_c, kv_m, kv_p, kv_c, rowspec, rowspec],
        out_specs=[qspec, lsespec],
        out_shape=[jax.ShapeDtypeStruct((t, A_PAD_WIDTH), BF16),
                   jax.ShapeDtypeStruct((t // BLOCK, A_KV_HEADS, GROUP_ROWS, 1), F32)],
        compiler_params=_cparams(("parallel", "parallel", "arbitrary")),
    )(q, k, k, k, v, v, v, sink_rows, slope_rows)


def _swa_bwd(q, k, v, do, lse, sink_rows, slope_rows, nb):
    t = q.shape[0]
    l = nb * BLOCK
    bsz = t // l

    def body(q_ref, km_ref, kp_ref, kc_ref, vm_ref, vp_ref, vc_ref, do_ref, lse_ref, sink_ref, slope_ref,
             dq_ref, dk_ref, dv_ref, dsink_ref, dk_acc, dv_acc):
        g = pl.program_id(1)
        n = pl.program_id(2)

        @pl.when((g == 0) & (n == 0))
        def _():
            dk_acc[...] = jnp.zeros_like(dk_acc)
            dv_acc[...] = jnp.zeros_like(dv_acc)

        @pl.when(n == 0)
        def _():
            dsink_ref[...] = jnp.zeros_like(dsink_ref)

        qq = _stack_heads(q_ref)
        dob = _stack_heads(do_ref)
        lse = lse_ref[0, 0]
        keys = _three_blocks(km_ref, kp_ref, kc_ref)
        lse_wide = jnp.broadcast_to(lse, (GROUP_ROWS, BLOCK))
        probs = [jnp.exp(s - lse_wide) for s in _swa_logits(qq, keys, slope_ref[0], n)]
        dp_all = lax.dot_general(dob, _three_blocks(vm_ref, vp_ref, vc_ref), NT_DIMS, preferred_element_type=F32)
        dps = [dp_all[:, i * BLOCK:(i + 1) * BLOCK] for i in range(3)]
        delta = jnp.sum((probs[0] * dps[0] + probs[1] * dps[1]) + probs[2] * dps[2], axis=-1, keepdims=True)
        delta_wide = jnp.broadcast_to(delta, (GROUP_ROWS, BLOCK))
        ds = jnp.concatenate([(p * (dp - delta_wide)).astype(BF16) for p, dp in zip(probs, dps)], axis=1)
        pb = jnp.concatenate([p.astype(BF16) for p in probs], axis=1)
        dq = jnp.dot(ds, keys, preferred_element_type=F32)
        dk_all = lax.dot_general(ds, qq, TN_DIMS, preferred_element_type=F32) * SCALE
        dv_all = lax.dot_general(pb, dob, TN_DIMS, preferred_element_type=F32)
        prev = jnp.maximum(n - 1, 0)
        for i, start in enumerate((0, prev * BLOCK, n * BLOCK)):
            rows = pl.ds(pl.multiple_of(start, BLOCK), BLOCK)
            dk_acc[rows, :] += dk_all[i * BLOCK:(i + 1) * BLOCK]
            dv_acc[rows, :] += dv_all[i * BLOCK:(i + 1) * BLOCK]
        _unstack_heads(dq_ref, dq * SCALE)
        dsink_ref[0, 0] += -(jnp.exp(sink_ref[0] - lse) * delta)

        @pl.when((g == A_KV_HEADS - 1) & (n == nb - 1))
        def _():
            dk_ref[...] = dk_acc[...].astype(BF16)
            dv_ref[...] = dv_acc[...].astype(BF16)

    qspec, kv_m, kv_p, kv_c, rowspec, lsespec = _swa_specs(nb)
    kv_all = pl.BlockSpec((l, LANES), lambda b, g, n: (b, 0))
    return pl.pallas_call(
        body,
        name="swa_bwd",
        grid=(bsz, A_KV_HEADS, nb),
        in_specs=[qspec, kv_m, kv_p, kv_c, kv_m, kv_p, kv_c, qspec, lsespec, rowspec, rowspec],
        out_specs=[qspec, kv_all, kv_all,
                   pl.BlockSpec((1, 1, GROUP_ROWS, 1), lambda b, g, n: (b, g, 0, 0))],
        out_shape=[jax.ShapeDtypeStruct((t, A_PAD_WIDTH), BF16),
                   jax.ShapeDtypeStruct((t, LANES), BF16),
                   jax.ShapeDtypeStruct((t, LANES), BF16),
                   jax.ShapeDtypeStruct((bsz, A_KV_HEADS, GROUP_ROWS, 1), F32)],
        scratch_shapes=[pltpu.VMEM((l, LANES), F32), pltpu.VMEM((l, LANES), F32)],
        compiler_params=_cparams(("parallel", "arbitrary", "arbitrary")),
    )(q, k, k, k, v, v, v, do, lse, sink_rows, slope_rows)


CHUNK = KEY_BLOCKS * BLOCK


def _fox_chunk(qb, ci):
    sb = jnp.maximum(jnp.minimum(KEY_BLOCKS * ci, qb + 1 - KEY_BLOCKS), 0)
    lo = jnp.maximum(ci * CHUNK, N_PAD)
    return sb, lo, pl.ds(pl.multiple_of(sb * BLOCK, BLOCK), CHUNK)


def _fox_logits(s_ref, cr_ref, e, j, sb, lo, qb):
    lane = lax.broadcasted_iota(jnp.int32, (BLOCK, BLOCK), 1)
    ahead = lane - lax.broadcasted_iota(jnp.int32, (BLOCK, BLOCK), 0)
    first = (sb + j) * BLOCK
    s = s_ref[e, :, j * BLOCK:(j + 1) * BLOCK] - cr_ref[e, sb + j]
    return jnp.where((ahead <= qb * BLOCK - first) & (lane >= lo - first), s, NEG)


FOX_PAIRS = 4
FOX_HEADS = 2 * FOX_PAIRS
FOX_STEPS = B_PAIRS // FOX_PAIRS


def _fox_specs(nb):
    l = nb * BLOCK
    q_spec = pl.BlockSpec((BLOCK, FOX_PAIRS * LANES), lambda b, p, i: (b * nb + i, p))
    kv_spec = pl.BlockSpec((l, FOX_HEADS * LANES), lambda b, p, i: (b, p))
    cc_spec = pl.BlockSpec((FOX_HEADS, BLOCK, 1), lambda b, p, i: (b * FOX_STEPS + p, i, 0))
    cr_spec = pl.BlockSpec((FOX_HEADS, nb, 1, BLOCK), lambda b, p, i: (b * FOX_STEPS + p, 0, 0, 0))
    return q_spec, kv_spec, cc_spec, cr_spec


def _fox_fwd(q, k, v, c_row, nb, exchange=None):
    t = q.shape[0]
    bsz = t // (nb * BLOCK)
    assert nb >= KEY_BLOCKS

    n_x = len(exchange.ins) if exchange else 0

    def body(*refs):
        q_ref, k_ref, v_ref, cr_ref = refs[:4]
        o_ref, ox_ref, lse_ref = refs[4 + n_x:7 + n_x]
        s_scr, hi_scr, lo_scr = refs[7 + 2 * n_x:10 + 2 * n_x]
        qb = pl.program_id(2)
        if exchange:
            first = (pl.program_id(0) == 0) & (pl.program_id(1) == 0)
            last = (pl.program_id(0) == bsz - 1) & (pl.program_id(1) == FOX_STEPS - 1)
            _host_exchange(exchange, refs[4:4 + n_x], refs[7 + n_x:7 + 2 * n_x], refs[10 + 2 * n_x:],
                           first & (qb == 0), last & (qb == 0), last & (qb == nb - 1))
        qs = [q_ref[:, a * LANES:(a + 1) * LANES] * SCALE for a in range(FOX_PAIRS)]
        first_half = lax.broadcasted_iota(jnp.int32, (BLOCK, LANES), 1) < HEAD_DIM

        def step(ci, carry):
            stats, accs = carry[:2 * FOX_HEADS], carry[2 * FOX_HEADS:]
            sb, lo, krows = _fox_chunk(qb, ci)
            new_stats, new_accs = [], []
            for a in range(FOX_PAIRS):
                alphas = []
                pv = jnp.zeros((BLOCK, LANES), F32)
                pv_lo = jnp.zeros((BLOCK, LANES), F32)
                for e in (2 * a, 2 * a + 1):
                    m, z = stats[2 * e], stats[2 * e + 1]
                    tile = slice(e * LANES, (e + 1) * LANES)
                    s_scr[e] = lax.dot_general(qs[a], k_ref[krows, tile], NT_DIMS, preferred_element_type=F32)
                    top = None
                    for j in range(KEY_BLOCKS):
                        s = _fox_logits(s_scr, cr_ref, e, j, sb, lo, qb)
                        s_scr[e, :, j * BLOCK:(j + 1) * BLOCK] = s
                        top = s if top is None else jnp.maximum(top, s)
                    m_new = jnp.maximum(m, jnp.max(top, axis=-1, keepdims=True))
                    alpha = jnp.exp(m - m_new)
                    m_wide = jnp.broadcast_to(m_new, (BLOCK, BLOCK))
                    total = None
                    for j in range(KEY_BLOCKS):
                        cols = slice(j * BLOCK, (j + 1) * BLOCK)
                        p = jnp.exp(s_scr[e, :, cols] - m_wide)
                        total = p if total is None else total + p
                        hi = p.astype(BF16)
                        hi_scr[e, :, cols] = hi
                        lo_scr[e, :, cols] = (p - hi.astype(F32)).astype(BF16)
                    z = alpha * z + jnp.sum(total, axis=-1, keepdims=True)
                    vv = v_ref[krows, tile]
                    pv = pv + jnp.dot(hi_scr[e], vv, preferred_element_type=F32)
                    pv_lo = pv_lo + jnp.dot(lo_scr[e], vv, preferred_element_type=F32)
                    new_stats += [m_new, z]
                    alphas.append(alpha)
                alpha = jnp.where(first_half, alphas[0], alphas[1])
                new_accs += [alpha * accs[2 * a] + pv, alpha * accs[2 * a + 1] + pv_lo]
            return (*new_stats, *new_accs)

        col = lambda val: jnp.full((BLOCK, 1), val, F32)
        done = lax.fori_loop(
            0, (qb + KEY_BLOCKS) // KEY_BLOCKS, step,
            (col(NEG), col(0.0)) * FOX_HEADS + (jnp.zeros((BLOCK, LANES), F32),) * (2 * FOX_PAIRS))
        for a in range(FOX_PAIRS):
            m0, z0, m1, z1 = done[4 * a:4 * a + 4]
            acc, acc_lo = done[2 * FOX_HEADS + 2 * a:2 * FOX_HEADS + 2 * a + 2]
            inv = 1.0 / jnp.where(first_half, z0, z1)
            tile = slice(a * LANES, (a + 1) * LANES)
            o_ref[:, tile] = (acc * inv).astype(BF16)
            ox_ref[:, tile] = (acc + acc_lo) * inv
            lse_ref[2 * a] = m0 + jnp.log(z0)
            lse_ref[2 * a + 1] = m1 + jnp.log(z1)

    q_spec, kv_spec, cc_spec, cr_spec = _fox_specs(nb)
    outs = pl.pallas_call(
        body,
        name="fox_fwd",
        grid=(bsz, FOX_STEPS, nb),
        in_specs=[q_spec, kv_spec, kv_spec, cr_spec] + [HBM_SPEC] * n_x,
        out_specs=[q_spec, q_spec, cc_spec] + [HBM_SPEC] * n_x,
        out_shape=[jax.ShapeDtypeStruct((t, B_WIDTH), BF16), jax.ShapeDtypeStruct((t, B_WIDTH), F32),
                   jax.ShapeDtypeStruct((bsz * B_HEADS, nb * BLOCK, 1), F32)] + (exchange.out_shape if exchange else []),
        scratch_shapes=[pltpu.VMEM((FOX_HEADS, BLOCK, CHUNK), F32), pltpu.VMEM((FOX_HEADS, BLOCK, CHUNK), BF16),
                        pltpu.VMEM((FOX_HEADS, BLOCK, CHUNK), BF16)] + (exchange.scratch if exchange else []),
        compiler_params=_cparams(("arbitrary",) * 3 if exchange else ("parallel", "parallel", "arbitrary")),
    )(q, k, v, c_row, *(exchange.ins if exchange else []))
    return outs[:3], outs[3:]


def _fox_bwd(q, k, v, o_exact, do, lse, c_row, nb, exchange=None):
    t = q.shape[0]
    l = nb * BLOCK
    bsz = t // l

    n_x = len(exchange.ins) if exchange else 0

    def body(*refs):
        q_ref, k_ref, v_ref, ox_ref, do_ref, lse_ref, cr_ref = refs[:7]
        dq_ref, dk_ref, dv_ref, dc_ref = refs[7 + n_x:11 + n_x]
        dk_acc, dv_acc, s_scr, dp_scr, p_scr, ds_scr = refs[11 + 2 * n_x:17 + 2 * n_x]
        qb = pl.program_id(2)
        if exchange:
            first = (pl.program_id(0) == 0) & (pl.program_id(1) == 0)
            last = (pl.program_id(0) == bsz - 1) & (pl.program_id(1) == FOX_STEPS - 1)
            _host_exchange(exchange, refs[7:7 + n_x], refs[11 + n_x:11 + 2 * n_x], refs[17 + 2 * n_x:],
                           first & (qb == 0), last & (qb == 0), last & (qb == nb - 1))

        @pl.when(qb == 0)
        def _():
            dk_acc[...] = jnp.zeros_like(dk_acc)
            dv_acc[...] = jnp.zeros_like(dv_acc)
            dc_ref[...] = jnp.zeros_like(dc_ref)

        top_half = lax.broadcasted_iota(jnp.int32, (LANES, BLOCK), 0) < HEAD_DIM
        pair_t = lambda x: jnp.concatenate([jnp.where(top_half, x.T, 0), jnp.where(top_half, 0, x.T)], axis=1)
        first_half = lax.broadcasted_iota(jnp.int32, (BLOCK, LANES), 1) < HEAD_DIM
        wide = lambda col: jnp.broadcast_to(col, (BLOCK, BLOCK))
        qs, dobs, qs_t, dob_t, deltas = [], [], [], [], []
        for a in range(FOX_PAIRS):
            tile = slice(a * LANES, (a + 1) * LANES)
            qs.append(q_ref[:, tile] * SCALE)
            dobs.append(do_ref[:, tile])
            qs_t.append(pair_t(qs[a]))
            dob_t.append(pair_t(dobs[a]))
            weighted = dobs[a].astype(F32) * ox_ref[:, tile]
            deltas += [wide(jnp.sum(jnp.where(first_half, weighted, 0.0), axis=-1, keepdims=True)),
                       wide(jnp.sum(jnp.where(first_half, 0.0, weighted), axis=-1, keepdims=True))]
        lses = [wide(lse_ref[e]) for e in range(FOX_HEADS)]

        def step(ci, dqs):
            sb, lo, krows = _fox_chunk(qb, ci)
            dqs = list(dqs)
            for a in range(FOX_PAIRS):
                for e in (2 * a, 2 * a + 1):
                    tile = slice(e * LANES, (e + 1) * LANES)
                    kk = k_ref[krows, tile]
                    s_scr[e] = lax.dot_general(qs[a], kk, NT_DIMS, preferred_element_type=F32)
                    dp_scr[e] = lax.dot_general(dobs[a], v_ref[krows, tile], NT_DIMS, preferred_element_type=F32)
                    for j in range(KEY_BLOCKS):
                        cols = slice(j * BLOCK, (j + 1) * BLOCK)
                        p = jnp.exp(_fox_logits(s_scr, cr_ref, e, j, sb, lo, qb) - lses[e])
                        ds = p * (dp_scr[e, :, cols] - deltas[e])
                        dc_ref[e, sb + j] -= jnp.sum(ds, axis=0, keepdims=True)
                        p_scr[e, :, cols] = p.astype(BF16)
                        ds_scr[e, :, cols] = ds.astype(BF16)
                    dqs[a] = dqs[a] + jnp.dot(ds_scr[e], kk, preferred_element_type=F32)
                both = slice(2 * a, 2 * a + 2)
                dk_t = jnp.dot(qs_t[a], ds_scr[both].reshape(2 * BLOCK, CHUNK), preferred_element_type=F32)
                dv_t = jnp.dot(dob_t[a], p_scr[both].reshape(2 * BLOCK, CHUNK), preferred_element_type=F32)
                for j in range(KEY_BLOCKS):
                    cols = slice(j * BLOCK, (j + 1) * BLOCK)
                    dk_acc[a * nb + sb + j] += dk_t[:, cols]
                    dv_acc[a * nb + sb + j] += dv_t[:, cols]
            return tuple(dqs)

        dqs = lax.fori_loop(0, (qb + KEY_BLOCKS) // KEY_BLOCKS, step,
                            (jnp.zeros((BLOCK, LANES), F32),) * FOX_PAIRS)
        for a in range(FOX_PAIRS):
            dq_ref[:, a * LANES:(a + 1) * LANES] = (dqs[a] * SCALE).astype(BF16)

        @pl.when(qb == nb - 1)
        def _():
            for a in range(FOX_PAIRS):
                for kb in range(nb):
                    rows = slice(kb * BLOCK, (kb + 1) * BLOCK)
                    for acc, out_ref in ((dk_acc, dk_ref), (dv_acc, dv_ref)):
                        pair = acc[a * nb + kb].T
                        out_ref[rows, 2 * a * LANES:(2 * a + 1) * LANES] = jnp.where(first_half, pair, 0.0).astype(BF16)
                        out_ref[rows, (2 * a + 1) * LANES:(2 * a + 2) * LANES] = (
                            jnp.where(first_half, 0.0, pair).astype(BF16))

    q_spec, kv_spec, cc_spec, cr_spec = _fox_specs(nb)
    outs = pl.pallas_call(
        body,
        name="fox_bwd",
        grid=(bsz, FOX_STEPS, nb),
        in_specs=[q_spec, kv_spec, kv_spec, q_spec, q_spec, cc_spec, cr_spec] + [HBM_SPEC] * n_x,
        out_specs=[q_spec, kv_spec, kv_spec, cr_spec] + [HBM_SPEC] * n_x,
        out_shape=[jax.ShapeDtypeStruct((t, B_WIDTH), BF16), jax.ShapeDtypeStruct((t, B_PAD_WIDTH), BF16),
                   jax.ShapeDtypeStruct((t, B_PAD_WIDTH), BF16),
                   jax.ShapeDtypeStruct((bsz * B_HEADS, nb, 1, BLOCK), F32)] + (exchange.out_shape if exchange else []),
        scratch_shapes=[pltpu.VMEM((FOX_PAIRS * nb, LANES, BLOCK), F32), pltpu.VMEM((FOX_PAIRS * nb, LANES, BLOCK), F32),
                        pltpu.VMEM((FOX_HEADS, BLOCK, CHUNK), F32), pltpu.VMEM((FOX_HEADS, BLOCK, CHUNK), F32),
                        pltpu.VMEM((FOX_HEADS, BLOCK, CHUNK), BF16), pltpu.VMEM((FOX_HEADS, BLOCK, CHUNK), BF16)]
        + (exchange.scratch if exchange else []),
        compiler_params=_cparams(("arbitrary",) * 3 if exchange else ("parallel", "parallel", "arbitrary")),
    )(q, k, v, o_exact, do, lse, c_row, *(exchange.ins if exchange else []))
    return outs[:4], outs[4:]


def _loss_head(h, final_w, target):
    bsz, l, d = h.shape
    nb = l // BLOCK

    def body(h_ref, w_ref, t_ref, loss_ref, dh_ref, dw_ref):
        b = pl.program_id(0)
        n = pl.program_id(1)

        @pl.when((b == 0) & (n == 0))
        def _():
            loss_ref[...] = jnp.zeros_like(loss_ref)
            dw_ref[...] = jnp.zeros_like(dw_ref)

        @pl.when(n == 0)
        def _():
            dh_ref[...] = jnp.zeros_like(dh_ref)

        @pl.when(n > 0)
        def _():
            hh = h_ref[0]
            w = w_ref[...]
            r = _rms_scale(hh)
            err = (hh * r) * w - t_ref[0]
            loss_ref[...] += 0.5 * jnp.sum(jnp.mean(err * err, axis=-1, keepdims=True), axis=0, keepdims=True)
            dy = err * (1.0 / d)
            dh, dw = _rms_bwd(dy, hh, w)
            dh_ref[0] = dh
            dw_ref[...] += dw

    return pl.pallas_call(
        body,
        name="loss_head",
        grid=(bsz, nb),
        in_specs=[
            pl.BlockSpec((1, BLOCK, d), lambda b, n: (b, n, 0)),
            pl.BlockSpec((1, d), lambda b, n: (0, 0)),
            pl.BlockSpec((1, BLOCK, d), lambda b, n: (b, jnp.maximum(n - 1, 0), 0)),
        ],
        out_specs=[
            pl.BlockSpec((1, 128), lambda b, n: (0, 0)),
            pl.BlockSpec((1, BLOCK, d), lambda b, n: (b, n, 0)),
            pl.BlockSpec((1, d), lambda b, n: (0, 0)),
        ],
        out_shape=[jax.ShapeDtypeStruct((1, 128), F32), jax.ShapeDtypeStruct((bsz, l, d), F32),
                   jax.ShapeDtypeStruct((1, d), F32)],
        compiler_params=_cparams(("arbitrary", "arbitrary")),
    )(h, final_w, target)


def _pad_tiles(w, src, heads, lane_slot, axis):
    pieces = []
    for h in range(heads):
        x = lax.slice_in_dim(w, src + HEAD_DIM * h, src + HEAD_DIM * (h + 1), axis=axis)
        z = jnp.zeros_like(x)
        pieces += [x, z] if lane_slot(h) == 0 else [z, x]
    return pieces


def _unpad_tiles(g, off, heads, lane_slot, axis):
    return [lax.slice_in_dim(g, off + LANES * h + HEAD_DIM * lane_slot(h),
                             off + LANES * h + HEAD_DIM * (lane_slot(h) + 1), axis=axis) for h in range(heads)]


A_SLOT = lambda h: h // A_GROUP
B_SLOT = lambda h: h % 2


def _layout_w_in(w):
    pad_f = jnp.zeros((w.shape[0], F_COLS - B_HEADS + TAIL_COLS), w.dtype)
    return jnp.concatenate(
        _pad_tiles(w, 0, A_HEADS, A_SLOT, 1) + [w[:, SRC_KA:SRC_KB]]
        + _pad_tiles(w, SRC_KB, B_HEADS, B_SLOT, 1) + _pad_tiles(w, SRC_VB, B_HEADS, B_SLOT, 1)
        + [w[:, SRC_GA:], w[:, SRC_F:SRC_GA], pad_f], axis=1)


def _unlayout_w_in(g):
    return jnp.concatenate(
        _unpad_tiles(g, OFF_QA, A_HEADS, A_SLOT, 1) + [g[:, OFF_KA:OFF_KB]]
        + _unpad_tiles(g, OFF_KB, B_HEADS, B_SLOT, 1) + _unpad_tiles(g, OFF_VB, B_HEADS, B_SLOT, 1)
        + [g[:, OFF_F:OFF_F + B_HEADS], g[:, OFF_GA:OFF_F]], axis=1)


def _local_step(x, target, meta, norms, b_forget, sinks, w, comm=None):
    n1, nmix, n2, nfin = norms
    w1i, w1o = w[:2]
    bsz, seq, d = x.shape
    l = PREFIX + seq
    nb = l // BLOCK
    t = bsz * l

    h0 = jnp.concatenate([jnp.zeros((bsz, N_PAD, d), F32),
                          jnp.broadcast_to(meta[None], (bsz, N_META, d)), x], axis=1).reshape(t, d)

    if comm is None:
        (h1, g1, u1), _ = _ffn_fwd(h0, n1, w1i, w1o)
        w_in, wa, wb, wo, w2i, w2o = w[2:]
    else:
        (h1, g1, u1), gathered = _ffn_fwd(h0, n1, w1i, w1o, comm.gather(GATHER_PROJ))
        w_in, = comm.gathered(GATHER_PROJ, gathered)
    wp = _layout_w_in(w_in)
    un, qa, ka, va, qb, kb, vb, ga, gb, f_logit = _proj_fwd(h1, nmix, wp)
    b_pad = jnp.concatenate([b_forget, jnp.zeros((1, F_COLS - B_HEADS), F32)], axis=1)
    c = _forget_cumsum(f_logit, b_pad, nb)
    c_heads = c[:, :B_HEADS].reshape(bsz, l, B_HEADS).transpose(0, 2, 1).reshape(bsz * B_HEADS, l)
    c_row = c_heads.reshape(bsz * B_HEADS, nb, 1, BLOCK)

    slopes = jnp.exp2(-8.0 * jnp.arange(1, A_HEADS + 1, dtype=F32) / A_HEADS)
    slope_rows = jnp.repeat(slopes.reshape(A_KV_HEADS, A_GROUP), BLOCK, axis=1)[:, :, None]
    sink_rows = jnp.repeat(sinks.reshape(A_KV_HEADS, A_GROUP), BLOCK, axis=1)[:, :, None]

    oa, lse_a = _swa_fwd(qa, ka, va, sink_rows, slope_rows, nb)
    if comm is None:
        (ob, ob_exact, lse_b), _ = _fox_fwd(qb, kb, vb, c_row, nb)
    else:
        (ob, ob_exact, lse_b), gathered = _fox_fwd(qb, kb, vb, c_row, nb, comm.gather(GATHER_LATE))
        wa, wb, wo, w2i, w2o = comm.gathered(GATHER_LATE, gathered)
    wa_p = jnp.concatenate(_pad_tiles(wa, 0, A_HEADS, A_SLOT, 0), axis=0)
    h2, mixed = _merge_fwd(h1, oa, ob, ga, gb, wa_p, wb, wo)
    (h3, g2, u2), _ = _ffn_fwd(h2, n2, w2i, w2o)
    loss, dh3, d_nfin = _loss_head(h3.reshape(bsz, l, d), nfin, target)

    (dh2, n2b, a2, dgu2, df2, dn2_parts), _ = _ffn_bwd(dh3.reshape(t, d), h2, n2, g2, u2, w2i, w2o)
    g_w2o = _tn_matmul(a2, df2, "grad_ffn2_w_out")
    g_w2i = _tn_matmul(n2b, dgu2, "grad_ffn2_w_in")

    hosted = comm.swap("ffn2", dict(ffn2_w_in=g_w2i, ffn2_w_out=g_w2o)) if comm else None
    (dya, dyb, doa, dob, dga, dgb, dh2b), swapped = _merge_bwd(dh2, oa, ob, ga, gb, wa_p, wb, wo, hosted)
    g_wo = _tn_matmul(mixed, dh2b, "grad_w_out")
    g_wa = jnp.concatenate(_unpad_tiles(_tn_matmul(oa, dya, "grad_w_branch_a"), 0, A_HEADS, A_SLOT, 0), axis=0)
    g_wb = _tn_matmul(ob, dyb, "grad_w_branch_b")

    dqa, dka, dva, dsink_rows = _swa_bwd(qa, ka, va, doa, lse_a, sink_rows, slope_rows, nb)
    hosted = comm.scatter("ffn2", swapped) if comm else None
    (dqb, dkb, dvb, dc_row), pieces = _fox_bwd(qb, kb, vb, ob_exact, dob, lse_b, c_row, nb, hosted)
    if comm:
        comm.received("ffn2", pieces)
    dc = dc_row.reshape(bsz, B_HEADS, l).transpose(0, 2, 1).reshape(t, B_HEADS)
    dc = jnp.concatenate([dc, jnp.zeros((t, F_COLS - B_HEADS), F32)], axis=1)
    df_logit, db_parts = _forget_cumsum_bwd(dc, f_logit, b_pad, nb)

    dproj = jnp.concatenate([dqa, dka, dva, dqb, dkb, dvb, dga, dgb, df_logit,
                             jnp.zeros((t, TAIL_COLS), BF16)], axis=1)
    g_win = _unlayout_w_in(_tn_matmul(un, dproj, "grad_w_in"))
    hosted = comm.swap("mixer", dict(w_in=g_win, w_branch_a=g_wa, w_branch_b=g_wb, w_out=g_wo)) if comm else None
    (dh1, dnmix_parts), swapped = _proj_bwd(dh2, h1, nmix, dproj, wp, hosted)
    hosted = comm.scatter("mixer", swapped) if comm else None
    (dh0, n1b, a1, dgu1, df1, dn1_parts), pieces = _ffn_bwd(dh1, h0, n1, g1, u1, w1i, w1o, hosted)
    if comm:
        comm.received("mixer", pieces)
    g_w1o = _tn_matmul(a1, df1, "grad_ffn1_w_out")
    g_w1i = _tn_matmul(n1b, dgu1, "grad_ffn1_w_in")

    dh0 = dh0.reshape(bsz, l, d)
    grad_x = dh0[:, PREFIX:]
    small = dict(
        meta_tokens=jnp.sum(dh0[:, N_PAD:PREFIX], axis=0),
        ffn1_norm=jnp.sum(dn1_parts, axis=0),
        mix_norm=jnp.sum(dnmix_parts, axis=0),
        ffn2_norm=jnp.sum(dn2_parts, axis=0),
        final_norm=d_nfin,
        b_forget=jnp.sum(db_parts, axis=0)[:, :B_HEADS],
        attn_sinks=jnp.sum(dsink_rows.reshape(bsz, A_HEADS, BLOCK), axis=(0, 2)).reshape(1, A_HEADS),
    )
    big = dict(ffn1_w_in=g_w1i, ffn1_w_out=g_w1o, w_in=g_win, w_branch_a=g_wa, w_branch_b=g_wb,
               w_out=g_wo, ffn2_w_in=g_w2i, ffn2_w_out=g_w2o)
    return loss, grad_x, small, big


BIG = (
    ("ffn1_w_in", (D_MODEL, 5632), 1),
    ("ffn1_w_out", (2816, D_MODEL), 0),
    ("w_in", (D_MODEL, W_IN_COLS), 1),
    ("w_branch_a", (A_WIDTH, D_MODEL), 1),
    ("w_branch_b", (B_WIDTH, D_MODEL), 1),
    ("w_out", (D_MODEL, D_MODEL), 0),
    ("ffn2_w_in", (D_MODEL, 5632), 1),
    ("ffn2_w_out", (2816, D_MODEL), 0),
)
STACKED = "w_in"


def _coords():
    return lax.axis_index("x"), lax.axis_index("y"), lax.axis_index("c")


def _other_chips(x, y):
    return ((1 - x, y), (x, 1 - y), (1 - x, 1 - y))


def _chip_part(ref, name, shape, axis, k):
    if name == STACKED:
        return ref.at[k]
    size = shape[axis] // N_CHIPS
    start = pl.multiple_of(k * size, size)
    return ref.at[pl.ds(start, size), :] if axis == 0 else ref.at[:, pl.ds(start, size)]


def _full_shape(name, shape):
    return (N_CHIPS, shape[0], shape[1] // N_CHIPS) if name == STACKED else shape


class _Exchange:
    def __init__(self, ins, out_shape, n_sems, ops):
        self.ins, self.out_shape, self.n_sems, self.ops = list(ins), list(out_shape), n_sems, ops

    @property
    def scratch(self):
        return [pltpu.SemaphoreType.DMA((self.n_sems,)), pltpu.SemaphoreType.DMA((self.n_sems,))]


SEMS_PER_GATHER = 7


def _gather_exchange(shards, table):
    n = len(table)

    def ops(ins, outs, send_sems, recv_sems):
        x, y, c = _coords()
        mine = 2 * x + y
        sibling = (x, y, 1 - c)
        chips = _other_chips(x, y)
        slots = [2 * chip[0] + chip[1] for chip in chips]

        def part(i, k):
            name, shape, axis = table[i][:3]
            return _chip_part(outs[i], name, shape, axis, k)

        def half(ref, h):
            rows = ref.shape[0] // 2
            return ref.at[pl.ds(pl.multiple_of(h * rows, rows), rows), :]

        def own(i):
            sem = SEMS_PER_GATHER * i
            return pltpu.make_async_remote_copy(ins[i], part(i, mine), send_sems.at[sem], recv_sems.at[sem],
                                                device_id=sibling, device_id_type=MESH_ID)

        def fetch(i, j, slot):
            sem = SEMS_PER_GATHER * i + 1 + j
            if table[i][4]:
                src, dst = half(ins[i], c), half(part(i, slot), c)
            else:
                src, dst = ins[i], part(i, slot)
            return pltpu.make_async_remote_copy(src, dst, send_sems.at[sem], recv_sems.at[sem],
                                                device_id=(chips[j][0], chips[j][1], c), device_id_type=MESH_ID)

        def forward(i, j, h):
            sem = SEMS_PER_GATHER * i + 4 + j
            region = half(part(i, slots[j]), h)
            return pltpu.make_async_remote_copy(region, region, send_sems.at[sem], recv_sems.at[sem],
                                                device_id=sibling, device_id_type=MESH_ID)

        def start():
            for i in range(n):
                for j in range(3):
                    fetch(i, j, mine).start()
            for i in range(n):
                own(i).start()

        def relay():
            for i in range(n):
                for j in range(3):
                    fetch(i, j, slots[j]).wait_recv()
                    if table[i][4]:
                        forward(i, j, c).start()

        def finish():
            for i in range(n):
                own(i).wait()
                for j in range(3):
                    if table[i][4]:
                        forward(i, j, 1 - c).wait_recv()
                        forward(i, j, c).wait_send()
                    fetch(i, j, mine).wait_send()

        return start, relay, finish

    out_shape = [jax.ShapeDtypeStruct(_full_shape(name, shape), dtype) for name, shape, _, dtype, _ in table]
    return _Exchange(shards, out_shape, SEMS_PER_GATHER * n, ops)


def _run_exchange(exchange, name):
    n = len(exchange.ins)

    def body(*refs):
        start, relay, finish = exchange.ops(refs[:n], refs[n:2 * n], *refs[2 * n:])
        start()
        relay()
        finish()

    return pl.pallas_call(
        body,
        name=name,
        in_specs=[HBM_SPEC] * n,
        out_specs=[HBM_SPEC] * n,
        out_shape=exchange.out_shape,
        scratch_shapes=exchange.scratch,
    )(*exchange.ins)


def _host_exchange(exchange, in_refs, out_refs, sem_refs, first, middle, last):
    start, relay, finish = exchange.ops(in_refs, out_refs, *sem_refs)
    pl.when(first)(start)
    pl.when(middle)(relay)
    pl.when(last)(finish)


def _halves_view(name, shape, axis):
    r, c = shape
    if name == STACKED:
        return (N_CHIPS, 2, r // 2, c // N_CHIPS), lambda ref, h: ref.at[:, h]
    if axis == 1:
        return (2, r // 2, c), lambda ref, h: ref.at[h]
    return (N_CHIPS, 2, r // N_CHIPS // 2, c), lambda ref, h: ref.at[:, h]


def _halves_exchange(grads, entries):
    n_w = len(entries)
    views = [_halves_view(*entry) for entry in entries]

    def ops(ins, outs, send_sems, recv_sems):
        x, y, c = _coords()
        copies = [pltpu.make_async_remote_copy(views[i][1](ins[i], 1 - c), outs[i], send_sems.at[i], recv_sems.at[i],
                                               device_id=(x, y, 1 - c), device_id_type=MESH_ID) for i in range(n_w)]

        def start():
            for cp in copies:
                cp.start()

        def finish():
            for cp in copies:
                cp.wait()

        return start, lambda: None, finish

    half_shape = lambda v: tuple(d for i, d in enumerate(v) if i != (1 if len(v) == 4 else 0))
    out_shape = [jax.ShapeDtypeStruct(half_shape(v[0]), F32) for v in views]
    return _Exchange([g.reshape(v[0]) for g, v in zip(grads, views)], out_shape, n_w, ops)


def _add_sibling(g_view, recv, c, name):
    shape = recv.shape
    if len(shape) == 2:
        tr = _tile(shape[0], 128, 16)
        grid = (shape[0] // tr,)
        g_spec = pl.BlockSpec((None, tr, shape[1]), lambda i, c_ref: (c_ref[0], i, 0))
        r_spec = pl.BlockSpec((tr, shape[1]), lambda i, c_ref: (i, 0))
    else:
        tr = _tile(shape[1], 256, 16)
        grid = (N_CHIPS, shape[1] // tr)
        g_spec = pl.BlockSpec((None, None, tr, shape[2]), lambda k, i, c_ref: (k, c_ref[0], i, 0))
        r_spec = pl.BlockSpec((None, tr, shape[2]), lambda k, i, c_ref: (k, i, 0))

    def body(c_ref, g_ref, r_ref, o_ref):
        o_ref[...] = (g_ref[...] + r_ref[...]).astype(BF16)

    return pl.pallas_call(
        body,
        name="add_sibling_" + name,
        grid_spec=pltpu.PrefetchScalarGridSpec(num_scalar_prefetch=1, grid=grid, in_specs=[g_spec, r_spec],
                                               out_specs=r_spec),
        out_shape=jax.ShapeDtypeStruct(shape, BF16),
        compiler_params=_cparams(("parallel",) * len(grid)),
    )(c, g_view, recv)


def _piece_of(ref, name, axis, k):
    if name == STACKED or axis == 0:
        return ref.at[k]
    size = ref.shape[1] // N_CHIPS
    return ref.at[:, pl.ds(pl.multiple_of(k * size, size), size)]


def _piece_shape(name, shape, axis):
    r, c = shape
    return (r // 2, c // N_CHIPS) if (axis == 1) else (r // N_CHIPS // 2, c)


def _scatter_exchange(partials, entries):
    n_w = len(entries)

    def ops(ins, outs, send_sems, recv_sems):
        x, y, c = _coords()
        chips = _other_chips(x, y)
        copies = []
        for i, (name, _, axis) in enumerate(entries):
            for j, chip in enumerate(chips):
                sem = 3 * i + j
                copies.append(pltpu.make_async_remote_copy(
                    _piece_of(ins[i], name, axis, 2 * chip[0] + chip[1]), outs[i].at[j], send_sems.at[sem],
                    recv_sems.at[sem], device_id=(chip[0], chip[1], c), device_id_type=MESH_ID))

        def start():
            for cp in copies:
                cp.start()

        def finish():
            for cp in copies:
                cp.wait()

        return start, lambda: None, finish

    out_shape = [jax.ShapeDtypeStruct((3,) + _piece_shape(*entry), BF16) for entry in entries]
    return _Exchange(partials, out_shape, 3 * n_w, ops)


def _add_chips(partial, recv, mine, name, axis):
    rows, cols = recv.shape[1:]
    tr = _tile(rows, 256, 16)
    if name == STACKED or axis == 0:
        p_spec = pl.BlockSpec((None, tr, cols), lambda i, k_ref: (k_ref[0], i, 0))
    else:
        p_spec = pl.BlockSpec((tr, cols), lambda i, k_ref: (i, k_ref[0]))

    def body(k_ref, p_ref, r_ref, o_ref):
        f32 = lambda a: a.astype(F32)
        o_ref[...] = ((f32(p_ref[...]) + f32(r_ref[0])) + f32(r_ref[1])) + f32(r_ref[2])

    return pl.pallas_call(
        body,
        name="add_chips_" + name,
        grid_spec=pltpu.PrefetchScalarGridSpec(
            num_scalar_prefetch=1, grid=(rows // tr,),
            in_specs=[p_spec, pl.BlockSpec((3, tr, cols), lambda i, k_ref: (0, i, 0))],
            out_specs=pl.BlockSpec((tr, cols), lambda i, k_ref: (i, 0))),
        out_shape=jax.ShapeDtypeStruct((rows, cols), F32),
        compiler_params=_cparams(("parallel",)),
    )(mine, partial, recv)


def _share_with_sibling(halves):
    n_w = len(halves)

    def body(*refs):
        ins, outs = refs[:n_w], refs[n_w:2 * n_w]
        send_sems, recv_sems = refs[2 * n_w:]
        x, y, c = _coords()
        copies = [pltpu.make_async_remote_copy(ins[i], outs[i], send_sems.at[i], recv_sems.at[i],
                                               device_id=(x, y, 1 - c), device_id_type=MESH_ID) for i in range(n_w)]
        for cp in copies:
            cp.start()
        for cp in copies:
            cp.wait()

    return pl.pallas_call(
        body,
        name="share_with_sibling",
        in_specs=[HBM_SPEC] * n_w,
        out_specs=[HBM_SPEC] * n_w,
        out_shape=[jax.ShapeDtypeStruct(h.shape, F32) for h in halves],
        scratch_shapes=[pltpu.SemaphoreType.DMA((n_w,)), pltpu.SemaphoreType.DMA((n_w,))],
    )(*halves)


SMALL_ROWS = 168


def _all_reduce_small(buf):
    def body(b_ref, out_ref, gathered, send_sems, recv_sems):
        x, y, c = _coords()
        me = 4 * x + 2 * y + c
        peers = [(x ^ fx, y ^ fy, c ^ fc) for fx in (0, 1) for fy in (0, 1) for fc in (0, 1)][1:]

        def copy(j, slot, dev):
            return pltpu.make_async_remote_copy(b_ref, gathered.at[slot], send_sems.at[j], recv_sems.at[j],
                                                device_id=dev, device_id_type=MESH_ID)

        for j, dev in enumerate(peers):
            copy(j, me, dev).start()
        gathered[me] = b_ref[...]
        for j, dev in enumerate(peers):
            copy(j, 4 * dev[0] + 2 * dev[1] + dev[2], dev).wait()
        acc = gathered[0]
        for d in range(1, N_DEV):
            acc = acc + gathered[d]
        out_ref[...] = acc

    return pl.pallas_call(
        body,
        name="all_reduce_small",
        in_specs=[VMEM_SPEC],
        out_specs=VMEM_SPEC,
        out_shape=jax.ShapeDtypeStruct(buf.shape, F32),
        scratch_shapes=[pltpu.VMEM((N_DEV,) + buf.shape, F32), pltpu.SemaphoreType.DMA((N_DEV - 1,)),
                        pltpu.SemaphoreType.DMA((N_DEV - 1,))],
    )(buf)


def _adamw(w, g, m, v):
    r, rest = w.shape[0], w.shape[1:]
    per_row = 1
    for dim in rest:
        per_row *= dim
    tr = _tile(r, max(8, (5 << 19) // (4 * per_row)), 8 if len(rest) == 1 else 1)

    def body(w_ref, g_ref, m_ref, v_ref, d_ref, mo_ref, vo_ref):
        gg = g_ref[...]
        mm = ADAM_B1 * m_ref[...] + (1.0 - ADAM_B1) * gg
        vv = ADAM_B2 * v_ref[...] + (1.0 - ADAM_B2) * (gg * gg)
        m_hat = mm / (1.0 - ADAM_B1 ** ADAM_STEP)
        v_hat = vv / (1.0 - ADAM_B2 ** ADAM_STEP)
        d_ref[...] = -ADAM_LR * (m_hat / (jnp.sqrt(v_hat) + ADAM_EPS) + ADAM_WD * w_ref[...])
        mo_ref[...] = mm
        vo_ref[...] = vv

    spec = pl.BlockSpec((tr,) + rest, lambda i: (i,) + (0,) * len(rest))
    return pl.pallas_call(
        body,
        name="adamw",
        grid=(r // tr,),
        in_specs=[spec] * 4,
        out_specs=[spec] * 3,
        out_shape=[jax.ShapeDtypeStruct(w.shape, F32)] * 3,
        compiler_params=_cparams(("parallel",)),
    )(w, g, m, v)


def _adamw_halves(w, own, other, m, v, c, name):
    r, cols = w.shape
    half = r // 2
    tr = _tile(half, 256, 8)
    nt = half // tr
    whole = pl.BlockSpec((tr, cols), lambda h, i, c_ref: (h * nt + i, 0))
    part = pl.BlockSpec((tr, cols), lambda h, i, c_ref: (i, 0))

    def body(c_ref, w_ref, own_ref, other_ref, m_ref, v_ref, g_ref, d_ref, mo_ref, vo_ref):
        gg = jnp.where(pl.program_id(0) == c_ref[0], own_ref[...], other_ref[...])
        g_ref[...] = gg
        mm = ADAM_B1 * m_ref[...] + (1.0 - ADAM_B1) * gg
        vv = ADAM_B2 * v_ref[...] + (1.0 - ADAM_B2) * (gg * gg)
        m_hat = mm / (1.0 - ADAM_B1 ** ADAM_STEP)
        v_hat = vv / (1.0 - ADAM_B2 ** ADAM_STEP)
        d_ref[...] = -ADAM_LR * (m_hat / (jnp.sqrt(v_hat) + ADAM_EPS) + ADAM_WD * w_ref[...])
        mo_ref[...] = mm
        vo_ref[...] = vv

    return pl.pallas_call(
        body,
        name="adamw_" + name,
        grid_spec=pltpu.PrefetchScalarGridSpec(
            num_scalar_prefetch=1, grid=(2, nt),
            in_specs=[whole, part, part, whole, whole], out_specs=[whole] * 4),
        out_shape=[jax.ShapeDtypeStruct((r, cols), F32)] * 4,
        compiler_params=_cparams(("parallel", "parallel")),
    )(c, w, own, other, m, v)


GATHER_FIRST = ("ffn1_w_in", "ffn1_w_out")
GATHER_PROJ = ("w_in",)
GATHER_LATE = ("w_branch_a", "w_branch_b", "w_out", "ffn2_w_in", "ffn2_w_out")


class _Comm:
    def __init__(self, shards, c_arr, mine_arr):
        self.shards, self.c, self.mine = shards, c_arr, mine_arr
        self.groups, self.halves = {}, {}
        self.by_name = {entry[0]: entry for entry in BIG}

    def gather(self, names):
        table = [self.by_name[n] + (BF16, True) for n in names]
        return _gather_exchange([self.shards[n] for n in names], table)

    def gathered(self, names, outs):
        return [o.transpose(1, 0, 2).reshape(D_MODEL, W_IN_COLS) if n == STACKED else o for n, o in zip(names, outs)]

    def swap(self, tag, grads):
        entries = [self.by_name[n] for n in grads]
        arrays = [g.reshape(D_MODEL, N_CHIPS, W_IN_COLS // N_CHIPS).transpose(1, 0, 2) if n == STACKED else g
                  for n, g in grads.items()]
        self.groups[tag] = (entries, arrays)
        return _halves_exchange(arrays, entries)

    def scatter(self, tag, received):
        entries, arrays = self.groups[tag]
        views = [_halves_view(*entry) for entry in entries]
        partials = [_add_sibling(g.reshape(v[0]), r, self.c, name)
                    for g, v, r, (name, _, _) in zip(arrays, views, received, entries)]
        self.groups[tag] = (entries, partials)
        return _scatter_exchange(partials, entries)

    def received(self, tag, pieces):
        entries, partials = self.groups[tag]
        for p, r, (name, _, axis) in zip(partials, pieces, entries):
            self.halves[name] = _add_chips(p, r, self.mine, name, axis)

    def finish(self):
        names = [n for n, _, _ in BIG]
        own = [self.halves[n] for n in names]
        return dict(zip(names, zip(own, _share_with_sibling(own))))


def kernel(x, meta_tokens, ffn1_norm, ffn1_w_in, ffn1_w_out, mix_norm, w_in, b_forget, attn_sinks, w_branch_a, w_branch_b, w_out, ffn2_norm, ffn2_w_in, ffn2_w_out, final_norm, loss_target, m_meta_tokens, m_ffn1_norm, m_ffn1_w_in, m_ffn1_w_out, m_mix_norm, m_w_in, m_b_forget, m_attn_sinks, m_w_branch_a, m_w_branch_b, m_w_out, m_ffn2_norm, m_ffn2_w_in, m_ffn2_w_out, m_final_norm, v_meta_tokens, v_ffn1_norm, v_ffn1_w_in, v_ffn1_w_out, v_mix_norm, v_w_in, v_b_forget, v_attn_sinks, v_w_branch_a, v_w_branch_b, v_w_out, v_ffn2_norm, v_ffn2_w_in, v_ffn2_w_out, v_final_norm):
    given = dict(locals())
    names = ["meta_tokens", "ffn1_norm", "ffn1_w_in", "ffn1_w_out", "mix_norm", "w_in", "b_forget", "attn_sinks",
             "w_branch_a", "w_branch_b", "w_out", "ffn2_norm", "ffn2_w_in", "ffn2_w_out", "final_norm"]
    big_names = [n for n, _, _ in BIG]
    cx, cy, cc = _coords()
    c_arr = cc.reshape(1).astype(jnp.int32)
    mine_arr = (2 * cx + cy).reshape(1).astype(jnp.int32)

    comm = _Comm({n: given[n][0].astype(BF16) for n in big_names}, c_arr, mine_arr)
    table = [comm.by_name[n] + (BF16, True) for n in GATHER_FIRST] + [("meta_tokens", (N_META, D_MODEL), 1, F32, False)]
    first = _gather_exchange([comm.shards[n] for n in GATHER_FIRST] + [meta_tokens], table)
    w1i, w1o, meta_full = _run_exchange(first, "gather_first")
    norms = (ffn1_norm, mix_norm, ffn2_norm, final_norm.reshape(1, D_MODEL))
    loss, grad_x, small, big = _local_step(x, loss_target, meta_full, norms, b_forget, attn_sinks, (w1i, w1o), comm)

    swap = comm.swap("ffn1", dict(ffn1_w_in=big["ffn1_w_in"], ffn1_w_out=big["ffn1_w_out"]))
    last = comm.scatter("ffn1", _run_exchange(swap, "exchange_halves"))
    comm.received("ffn1", _run_exchange(last, "scatter_chip_sums"))
    grad_halves = comm.finish()
    grads = {}

    pad_lanes = lambda a: jnp.concatenate([a, jnp.zeros((1, LANES - a.shape[1]), F32)], axis=1)
    buf = jnp.concatenate([
        small["meta_tokens"].reshape(128, LANES),
        small["ffn1_norm"].reshape(8, LANES), small["mix_norm"].reshape(8, LANES),
        small["ffn2_norm"].reshape(8, LANES), small["final_norm"].reshape(8, LANES),
        loss, pad_lanes(small["b_forget"]), pad_lanes(small["attn_sinks"]),
        jnp.zeros((SMALL_ROWS - 163, LANES), F32)], axis=0)
    red = _all_reduce_small(buf)
    meta_cols = red[:128].reshape(N_META, D_MODEL)
    grads["meta_tokens"] = lax.dynamic_slice_in_dim(meta_cols, (2 * cx + cy) * (D_MODEL // N_CHIPS),
                                                    D_MODEL // N_CHIPS, axis=1)
    grads["ffn1_norm"] = red[128:136].reshape(1, D_MODEL)
    grads["mix_norm"] = red[136:144].reshape(1, D_MODEL)
    grads["ffn2_norm"] = red[144:152].reshape(1, D_MODEL)
    grads["final_norm"] = red[152:160].reshape(1, D_MODEL)
    loss_out = red[160, 0]
    grads["b_forget"] = red[161:162, :B_HEADS]
    grads["attn_sinks"] = red[162:163, :A_HEADS]

    out_g, out_d, out_m, out_v = [], [], [], []
    for n in names:
        w_full = given[n]
        shape = w_full.shape
        two_d = (lambda a: a.reshape(shape[-2], shape[-1])) if len(shape) >= 2 else (lambda a: a.reshape(1, shape[0]))
        if n == STACKED:
            own, other = grad_halves[n]
            g_t = jnp.concatenate([jnp.where(cc == 0, own, other), jnp.where(cc == 0, other, own)], axis=0).T
            tiles = lambda a: a.reshape(shape[-1], shape[-2] // LANES, LANES)
            untile = lambda a: a.reshape(shape[-1], shape[-2]).T
            d2, m2, v2 = [untile(a) for a in _adamw(tiles(two_d(w_full).T), tiles(g_t), tiles(two_d(given["m_" + n]).T),
                                                     tiles(two_d(given["v_" + n]).T))]
            g2 = g_t.T
        elif n in grad_halves:
            own, other = grad_halves[n]
            g2, d2, m2, v2 = _adamw_halves(two_d(w_full), own, other, two_d(given["m_" + n]),
                                           two_d(given["v_" + n]), c_arr, n)
        else:
            g2 = two_d(grads[n])
            d2, m2, v2 = _adamw(two_d(w_full), g2, two_d(given["m_" + n]), two_d(given["v_" + n]))
        out_g.append(g2.reshape(shape))
        out_d.append(d2.reshape(shape))
        out_m.append(m2.reshape(shape))
        out_v.append(v2.reshape(shape))
    return (loss_out, grad_x, *out_g, *out_d, *out_m, *out_v)
```

```python
import jax
import jax.numpy as jnp
from jax import lax
from jax.experimental import pallas as pl
from jax.experimental.pallas import tpu as pltpu

F32 = jnp.float32
BF16 = jnp.bfloat16

D_MODEL = 1024
N_META = 16
BLOCK = 128
LANES = 128
PREFIX = BLOCK
N_PAD = PREFIX - N_META
HEAD_DIM = 64
A_HEADS = 8
A_KV_HEADS = 2
A_GROUP = 4
B_HEADS = 8
B_PAIRS = B_HEADS // 2
A_WIDTH = A_HEADS * HEAD_DIM
A_KV_WIDTH = A_KV_HEADS * HEAD_DIM
B_WIDTH = B_HEADS * HEAD_DIM
W_IN_COLS = A_WIDTH + 2 * A_KV_WIDTH + 3 * B_WIDTH + B_HEADS + 2 * D_MODEL
SRC_KA = A_WIDTH
SRC_VA = SRC_KA + A_KV_WIDTH
SRC_QB = SRC_VA + A_KV_WIDTH
SRC_KB = SRC_QB + B_WIDTH
SRC_VB = SRC_KB + B_WIDTH
SRC_F = SRC_VB + B_WIDTH
SRC_GA = SRC_F + B_HEADS
SRC_GB = SRC_GA + D_MODEL
A_PAD_WIDTH = A_HEADS * LANES
B_PAD_WIDTH = B_HEADS * LANES
F_COLS = LANES
OFF_QA = 0
OFF_KA = SRC_KA
OFF_VA = SRC_VA
OFF_QB = SRC_QB
OFF_KB = SRC_KB
OFF_VB = SRC_VB
OFF_GA = SRC_F
OFF_GB = OFF_GA + D_MODEL
OFF_F = OFF_GB + D_MODEL
P_COLS = OFF_F + F_COLS
EPS = 1e-6
NEG = -1e30
SCALE = HEAD_DIM ** -0.5
KEY_BLOCKS = 4

ADAM_LR = 0.001
ADAM_B1 = 0.9
ADAM_B2 = 0.999
ADAM_EPS = 1e-08
ADAM_WD = 0.01
ADAM_STEP = 10

N_CHIPS = 4
N_DEV = 8
VMEM_LIMIT = 56 * 1024 * 1024

NT_DIMS = (((1,), (1,)), ((), ()))
TN_DIMS = (((0,), (0,)), ((), ()))
MESH_ID = pl.DeviceIdType.MESH
HBM_SPEC = pl.BlockSpec(memory_space=pltpu.HBM)
VMEM_SPEC = pl.BlockSpec(memory_space=pltpu.VMEM)


def _tile(n, target, mult=16):
    best = None
    for t in range(mult, min(n, target) + 1, mult):
        if n % t == 0:
            best = t
    return best if best is not None else n


def _cparams(sem):
    return pltpu.CompilerParams(dimension_semantics=sem, vmem_limit_bytes=VMEM_LIMIT)


def _rms_scale(h):
    return lax.rsqrt(jnp.mean(h * h, axis=-1, keepdims=True) + EPS)


def _rms_bwd(dn, h, w):
    r = _rms_scale(h)
    dw = jnp.sum(dn * (h * r), axis=0, keepdims=True)
    z = dn * w
    dh = r * z - h * ((r * r * r) * jnp.mean(z * h, axis=-1, keepdims=True))
    return dh, dw


def _ffn_fwd(h, norm_w, w_in, w_out, exchange=None):
    t, d = h.shape
    f = w_out.shape[0]
    tm = _tile(t, 272)
    tc = _tile(f, 256, 128)
    nj = f // tc
    ni = t // tm
    n_x = len(exchange.ins) if exchange else 0

    def body(*refs):
        h_ref, nw_ref, wi_ref, wo_ref = refs[:4]
        hout_ref, g_ref, u_ref = refs[4 + n_x:7 + n_x]
        a_scr = refs[7 + 2 * n_x]
        i = pl.program_id(0)
        if exchange:
            _host_exchange(exchange, refs[4:4 + n_x], refs[7 + n_x:7 + 2 * n_x], refs[8 + 2 * n_x:],
                           i == 0, i == ni - 2, i == ni - 1)
        hh = h_ref[...]
        n = ((hh * _rms_scale(hh)) * nw_ref[...]).astype(BF16)
        for j in range(nj):
            cols = slice(j * tc, (j + 1) * tc)
            g = jnp.dot(n, wi_ref[:, j * tc:(j + 1) * tc], preferred_element_type=F32)
            u = jnp.dot(n, wi_ref[:, f + j * tc:f + (j + 1) * tc], preferred_element_type=F32)
            g_ref[:, cols] = g
            u_ref[:, cols] = u
            a_scr[:, cols] = ((g * jax.nn.sigmoid(g)) * u).astype(BF16)
        hout_ref[...] = hh + 0.5 * jnp.dot(a_scr[...], wo_ref[...], preferred_element_type=F32)

    resident = lambda a: pl.BlockSpec(a.shape, lambda i: (0, 0), pipeline_mode=pl.Buffered(1))
    row = lambda w: pl.BlockSpec((tm, w), lambda i: (i, 0))
    outs = pl.pallas_call(
        body,
        name="ffn_fwd",
        grid=(ni,),
        in_specs=[row(d), pl.BlockSpec((1, d), lambda i: (0, 0)), resident(w_in), resident(w_out)] + [HBM_SPEC] * n_x,
        out_specs=[row(d), row(f), row(f)] + [HBM_SPEC] * n_x,
        out_shape=[
            jax.ShapeDtypeStruct((t, d), F32),
            jax.ShapeDtypeStruct((t, f), F32),
            jax.ShapeDtypeStruct((t, f), F32),
        ] + (exchange.out_shape if exchange else []),
        scratch_shapes=[pltpu.VMEM((tm, f), BF16)] + (exchange.scratch if exchange else []),
        compiler_params=_cparams(("arbitrary",) if exchange else ("parallel",)),
    )(h, norm_w, w_in, w_out, *(exchange.ins if exchange else []))
    return outs[:3], outs[3:]


def _ffn_bwd(dh_out, h, norm_w, g, u, w_in, w_out, exchange=None):
    t, d = h.shape
    f = w_out.shape[0]
    tm = _tile(t, 272)
    tc = _tile(f, 256, 128)
    nj = f // tc
    ni = t // tm
    n_x = len(exchange.ins) if exchange else 0

    def body(*refs):
        dho_ref, h_ref, nw_ref, g_ref, u_ref, wi_ref, wo_ref = refs[:7]
        dhin_ref, n_ref, a_ref, dgu_ref, df_ref, dnw_ref = refs[7 + n_x:13 + n_x]
        i = pl.program_id(0)
        if exchange:
            _host_exchange(exchange, refs[7:7 + n_x], refs[13 + n_x:13 + 2 * n_x], refs[13 + 2 * n_x:],
                           i == 0, i == ni - 1, i == ni - 1)
        hh = h_ref[...]
        nw = nw_ref[...]
        n_ref[...] = ((hh * _rms_scale(hh)) * nw).astype(BF16)
        dho = dho_ref[...]
        df = (0.5 * dho).astype(BF16)
        df_ref[...] = df
        for j in range(nj):
            cols = slice(j * tc, (j + 1) * tc)
            da = lax.dot_general(df, wo_ref[cols, :], NT_DIMS, preferred_element_type=F32)
            gg = g_ref[:, cols]
            uu = u_ref[:, cols]
            sig = jax.nn.sigmoid(gg)
            sl = gg * sig
            a_ref[:, cols] = (sl * uu).astype(BF16)
            dgu_ref[0, :, cols] = ((da * uu) * (sig * (1.0 + gg * (1.0 - sig)))).astype(BF16)
            dgu_ref[1, :, cols] = (da * sl).astype(BF16)
        dn = (lax.dot_general(dgu_ref[0], wi_ref[:, :f], NT_DIMS, preferred_element_type=F32)
              + lax.dot_general(dgu_ref[1], wi_ref[:, f:], NT_DIMS, preferred_element_type=F32))
        dh, dw = _rms_bwd(dn, hh, nw)
        dhin_ref[...] = dho + dh
        dnw_ref[0] = dw

    resident = lambda a: pl.BlockSpec(a.shape, lambda i: (0, 0), pipeline_mode=pl.Buffered(1))
    row = lambda w: pl.BlockSpec((tm, w), lambda i: (i, 0))
    outs = pl.pallas_call(
        body,
        name="ffn_bwd",
        grid=(ni,),
        in_specs=[row(d), row(d), pl.BlockSpec((1, d), lambda i: (0, 0)), row(f), row(f),
                  resident(w_in), resident(w_out)] + [HBM_SPEC] * n_x,
        out_specs=[row(d), row(d), row(f), pl.BlockSpec((2, tm, f), lambda i: (0, i, 0)), row(d),
                   pl.BlockSpec((1, 1, d), lambda i: (i, 0, 0))] + [HBM_SPEC] * n_x,
        out_shape=[
            jax.ShapeDtypeStruct((t, d), F32),
            jax.ShapeDtypeStruct((t, d), BF16),
            jax.ShapeDtypeStruct((t, f), BF16),
            jax.ShapeDtypeStruct((2, t, f), BF16),
            jax.ShapeDtypeStruct((t, d), BF16),
            jax.ShapeDtypeStruct((ni, 1, d), F32),
        ] + (exchange.out_shape if exchange else []),
        scratch_shapes=exchange.scratch if exchange else [],
        compiler_params=_cparams(("arbitrary",) if exchange else ("parallel",)),
    )(dh_out, h, norm_w, g, u, w_in, w_out, *(exchange.ins if exchange else []))
    return outs[:6], outs[6:]


def _tn_matmul(a, b, name):
    t, k = a.shape
    split = b.ndim == 3
    n = 2 * b.shape[2] if split else b.shape[1]
    tk = _tile(k, 512, 128)
    tn = _tile(b.shape[-1], 1408, 128)
    per_half = b.shape[-1] // tn

    def body(a_ref, b_ref, o_ref):
        o_ref[...] = lax.dot_general(a_ref[...], b_ref[...], TN_DIMS, preferred_element_type=F32)

    if split:
        b_spec = pl.BlockSpec((None, t, tn), lambda i, j: (j // per_half, 0, j % per_half))
    else:
        b_spec = pl.BlockSpec((t, tn), lambda i, j: (0, j))
    return pl.pallas_call(
        body,
        name=name,
        grid=(k // tk, n // tn),
        in_specs=[pl.BlockSpec((t, tk), lambda i, j: (0, i)), b_spec],
        out_specs=pl.BlockSpec((tk, tn), lambda i, j: (i, j)),
        out_shape=jax.ShapeDtypeStruct((k, n), F32),
        compiler_params=_cparams(("parallel", "parallel")),
    )(a, b)


A_SLOT = lambda h: h // A_GROUP
B_SLOT = lambda h: h % 2

PROJ_PARTS = (
    (OFF_QA, A_WIDTH, A_PAD_WIDTH, True, A_SLOT), (OFF_KA, A_KV_WIDTH, A_KV_WIDTH, True, None),
    (OFF_VA, A_KV_WIDTH, A_KV_WIDTH, True, None), (OFF_QB, B_WIDTH, B_WIDTH, True, None),
    (OFF_KB, B_WIDTH, B_PAD_WIDTH, True, B_SLOT), (OFF_VB, B_WIDTH, B_PAD_WIDTH, True, B_SLOT),
    (OFF_GA, D_MODEL, D_MODEL, False, None), (OFF_GB, D_MODEL, D_MODEL, False, None), (OFF_F, F_COLS, F_COLS, False, None),
)


def _head_tile(pair, head, slot):
    lane_slot = lax.broadcasted_iota(jnp.int32, pair.shape, 1) // HEAD_DIM
    moved = pair if head % 2 == slot else pltpu.roll(pair, HEAD_DIM, 1)
    return jnp.where(lane_slot == slot, moved, 0.0)


def _proj_fwd(h, norm_w, w_p):
    t, d = h.shape
    tm = _tile(t, 272)

    def body(h_ref, nw_ref, w_ref, u_ref, *part_refs):
        hh = h_ref[...]
        un = ((hh * _rms_scale(hh)) * nw_ref[...]).astype(BF16)
        u_ref[...] = un
        for (off, width, _, _, slot), p_ref in zip(PROJ_PARTS, part_refs):
            if slot is None:
                p_ref[...] = jnp.dot(un, w_ref[:, off:off + width], preferred_element_type=F32).astype(p_ref.dtype)
                continue
            for pair in range(width // LANES):
                x = jnp.dot(un, w_ref[:, off + pair * LANES:off + (pair + 1) * LANES], preferred_element_type=F32)
                for head in (2 * pair, 2 * pair + 1):
                    p_ref[:, head * LANES:(head + 1) * LANES] = _head_tile(x, head, slot(head)).astype(p_ref.dtype)

    row = lambda w: pl.BlockSpec((tm, w), lambda i: (i, 0))
    return pl.pallas_call(
        body,
        name="proj_fwd",
        grid=(t // tm,),
        in_specs=[row(d), pl.BlockSpec((1, d), lambda i: (0, 0)),
                  pl.BlockSpec(w_p.shape, lambda i: (0, 0), pipeline_mode=pl.Buffered(1))],
        out_specs=[row(d)] + [row(width) for _, _, width, _, _ in PROJ_PARTS],
        out_shape=[jax.ShapeDtypeStruct((t, d), BF16)]
        + [jax.ShapeDtypeStruct((t, width), BF16 if is_bf else F32) for _, _, width, is_bf, _ in PROJ_PARTS],
        compiler_params=_cparams(("parallel",)),
    )(h, norm_w, w_p)


def _proj_bwd(dh_out, h, norm_w, dproj, w_p, exchange=None):
    t, d = h.shape
    n = w_p.shape[1]
    tm = _tile(t, 272)
    ni = t // tm
    n_x = len(exchange.ins) if exchange else 0

    def body(*refs):
        dho_ref, h_ref, nw_ref, dp_ref, w_ref = refs[:5]
        dhin_ref, dnw_ref = refs[5 + n_x:7 + n_x]
        if exchange:
            i = pl.program_id(0)
            _host_exchange(exchange, refs[5:5 + n_x], refs[7 + n_x:7 + 2 * n_x], refs[7 + 2 * n_x:],
                           i == 0, i == ni - 1, i == ni - 1)
        dn = lax.dot_general(dp_ref[...], w_ref[...], NT_DIMS, preferred_element_type=F32)
        dh, dw = _rms_bwd(dn, h_ref[...], nw_ref[...])
        dhin_ref[...] = dho_ref[...] + dh
        dnw_ref[0] = dw

    row = lambda w: pl.BlockSpec((tm, w), lambda i: (i, 0))
    outs = pl.pallas_call(
        body,
        name="proj_bwd",
        grid=(ni,),
        in_specs=[row(d), row(d), pl.BlockSpec((1, d), lambda i: (0, 0)), row(n),
                  pl.BlockSpec(w_p.shape, lambda i: (0, 0), pipeline_mode=pl.Buffered(1))] + [HBM_SPEC] * n_x,
        out_specs=[row(d), pl.BlockSpec((1, 1, d), lambda i: (i, 0, 0))] + [HBM_SPEC] * n_x,
        out_shape=[jax.ShapeDtypeStruct((t, d), F32), jax.ShapeDtypeStruct((ni, 1, d), F32)]
        + (exchange.out_shape if exchange else []),
        scratch_shapes=exchange.scratch if exchange else [],
        compiler_params=_cparams(("arbitrary",) if exchange else ("parallel",)),
    )(dh_out, h, norm_w, dproj, w_p, *(exchange.ins if exchange else []))
    return outs[:2], outs[2:]


def _merge_fwd(h, oa, ob, ga, gb, wa, wb, wo):
    t, d = h.shape
    tm = _tile(t, 544)

    def body(h_ref, oa_ref, ob_ref, ga_ref, gb_ref, wa_ref, wb_ref, wo_ref, hout_ref, mix_ref):
        ya = jnp.dot(oa_ref[...], wa_ref[...], preferred_element_type=F32)
        yb = jnp.dot(ob_ref[...], wb_ref[...], preferred_element_type=F32)
        mixed = (jax.nn.sigmoid(ga_ref[...]) * ya + jax.nn.sigmoid(gb_ref[...]) * yb).astype(BF16)
        mix_ref[...] = mixed
        hout_ref[...] = h_ref[...] + jnp.dot(mixed, wo_ref[...], preferred_element_type=F32)

    row = lambda w: pl.BlockSpec((tm, w), lambda i: (i, 0))
    full = lambda a: pl.BlockSpec(a.shape, lambda i: (0, 0))
    return pl.pallas_call(
        body,
        name="merge_fwd",
        grid=(t // tm,),
        in_specs=[row(d), row(oa.shape[1]), row(ob.shape[1]), row(d), row(d), full(wa), full(wb), full(wo)],
        out_specs=[row(d), row(d)],
        out_shape=[jax.ShapeDtypeStruct((t, d), F32), jax.ShapeDtypeStruct((t, d), BF16)],
        compiler_params=_cparams(("parallel",)),
    )(h, oa, ob, ga, gb, wa, wb, wo)


def _merge_bwd(dh, oa, ob, ga, gb, wa, wb, wo, exchange=None):
    t, d = dh.shape
    tm = _tile(t, 544)
    ni = t // tm
    n_x = len(exchange.ins) if exchange else 0

    def body(*refs):
        dh_ref, oa_ref, ob_ref, ga_ref, gb_ref, wa_ref, wb_ref, wo_ref = refs[:8]
        dya_ref, dyb_ref, doa_ref, dob_ref, dga_ref, dgb_ref, dhb_ref = refs[8 + n_x:15 + n_x]
        if exchange:
            i = pl.program_id(0)
            _host_exchange(exchange, refs[8:8 + n_x], refs[15 + n_x:15 + 2 * n_x], refs[15 + 2 * n_x:],
                           i == 0, i == ni - 1, i == ni - 1)
        dhb = dh_ref[...].astype(BF16)
        dhb_ref[...] = dhb
        dmix = lax.dot_general(dhb, wo_ref[...], NT_DIMS, preferred_element_type=F32)
        for o_ref, g_ref, w_ref, dy_ref, do_ref, dg_ref in (
                (oa_ref, ga_ref, wa_ref, dya_ref, doa_ref, dga_ref),
                (ob_ref, gb_ref, wb_ref, dyb_ref, dob_ref, dgb_ref)):
            y = jnp.dot(o_ref[...], w_ref[...], preferred_element_type=F32)
            s = jax.nn.sigmoid(g_ref[...])
            dy = (dmix * s).astype(BF16)
            dy_ref[...] = dy
            dg_ref[...] = ((dmix * y) * (s * (1.0 - s))).astype(BF16)
            do_ref[...] = lax.dot_general(dy, w_ref[...], NT_DIMS, preferred_element_type=F32).astype(BF16)

    row = lambda w: pl.BlockSpec((tm, w), lambda i: (i, 0))
    full = lambda a: pl.BlockSpec(a.shape, lambda i: (0, 0))
    wa_w, wb_w = oa.shape[1], ob.shape[1]
    outs = pl.pallas_call(
        body,
        name="merge_bwd",
        grid=(ni,),
        in_specs=[row(d), row(wa_w), row(wb_w), row(d), row(d), full(wa), full(wb), full(wo)] + [HBM_SPEC] * n_x,
        out_specs=[row(d), row(d), row(wa_w), row(wb_w), row(d), row(d), row(d)] + [HBM_SPEC] * n_x,
        out_shape=[
            jax.ShapeDtypeStruct((t, d), BF16), jax.ShapeDtypeStruct((t, d), BF16),
            jax.ShapeDtypeStruct((t, wa_w), BF16), jax.ShapeDtypeStruct((t, wb_w), BF16),
            jax.ShapeDtypeStruct((t, d), BF16), jax.ShapeDtypeStruct((t, d), BF16),
            jax.ShapeDtypeStruct((t, d), BF16),
        ] + (exchange.out_shape if exchange else []),
        scratch_shapes=exchange.scratch if exchange else [],
        compiler_params=_cparams(("arbitrary",) if exchange else ("parallel",)),
    )(dh, oa, ob, ga, gb, wa, wb, wo, *(exchange.ins if exchange else []))
    return outs[:7], outs[7:]


def _tri_dot(tri, x):
    hi = x.astype(BF16)
    r1 = x - hi.astype(F32)
    mid = r1.astype(BF16)
    lo = (r1 - mid.astype(F32)).astype(BF16)
    return (jnp.dot(tri, hi, preferred_element_type=F32)
            + jnp.dot(tri, mid, preferred_element_type=F32)
            + jnp.dot(tri, lo, preferred_element_type=F32))


def _forget_cumsum(f_logit, b_pad, nb):
    t, w = f_logit.shape
    bsz = t // (nb * BLOCK)

    def body(f_ref, b_ref, c_ref, carry):
        n = pl.program_id(1)

        @pl.when(n == 0)
        def _():
            carry[...] = jnp.zeros_like(carry)

        x = jax.nn.log_sigmoid(f_ref[...] + b_ref[...])
        rows = lax.broadcasted_iota(jnp.int32, (BLOCK, BLOCK), 0)
        cols = lax.broadcasted_iota(jnp.int32, (BLOCK, BLOCK), 1)
        tri = (cols <= rows).astype(BF16)
        c = _tri_dot(tri, x) + carry[...]
        c_ref[...] = c
        carry[...] = c[BLOCK - 1:BLOCK, :]

    return pl.pallas_call(
        body,
        name="forget_cumsum",
        grid=(bsz, nb),
        in_specs=[pl.BlockSpec((BLOCK, w), lambda b, n: (b * nb + n, 0)),
                  pl.BlockSpec((1, w), lambda b, n: (0, 0))],
        out_specs=pl.BlockSpec((BLOCK, w), lambda b, n: (b * nb + n, 0)),
        out_shape=jax.ShapeDtypeStruct((t, w), F32),
        scratch_shapes=[pltpu.VMEM((1, w), F32)],
        compiler_params=_cparams(("parallel", "arbitrary")),
    )(f_logit, b_pad)


def _forget_cumsum_bwd(dc, f_logit, b_pad, nb):
    t, w = f_logit.shape
    bsz = t // (nb * BLOCK)

    def body(dc_ref, f_ref, b_ref, df_ref, db_ref, carry):
        n = pl.program_id(1)

        @pl.when(n == 0)
        def _():
            carry[...] = jnp.zeros_like(carry)
            db_ref[...] = jnp.zeros_like(db_ref)

        rows = lax.broadcasted_iota(jnp.int32, (BLOCK, BLOCK), 0)
        cols = lax.broadcasted_iota(jnp.int32, (BLOCK, BLOCK), 1)
        tri = (cols >= rows).astype(BF16)
        dlf = _tri_dot(tri, dc_ref[...]) + carry[...]
        carry[...] = dlf[0:1, :]
        df = dlf * jax.nn.sigmoid(-(f_ref[...] + b_ref[...]))
        df_ref[...] = df.astype(BF16)
        db_ref[0] += jnp.sum(df, axis=0, keepdims=True)

    rev = lambda b, n: (b * nb + (nb - 1 - n), 0)
    return pl.pallas_call(
        body,
        name="forget_cumsum_bwd",
        grid=(bsz, nb),
        in_specs=[pl.BlockSpec((BLOCK, w), rev),
                  pl.BlockSpec((BLOCK, w), rev),
                  pl.BlockSpec((1, w), lambda b, n: (0, 0))],
        out_specs=[pl.BlockSpec((BLOCK, w), rev),
                   pl.BlockSpec((1, 1, w), lambda b, n: (b, 0, 0))],
        out_shape=[jax.ShapeDtypeStruct((t, w), BF16), jax.ShapeDtypeStruct((bsz, 1, w), F32)],
        scratch_shapes=[pltpu.VMEM((1, w), F32)],
        compiler_params=_cparams(("parallel", "arbitrary")),
    )(dc, f_logit, b_pad)


GROUP_ROWS = A_GROUP * BLOCK


def _stack_heads(ref):
    return jnp.concatenate([ref[:, i * LANES:(i + 1) * LANES] for i in range(A_GROUP)], axis=0)


def _unstack_heads(ref, x):
    for i in range(A_GROUP):
        ref[:, i * LANES:(i + 1) * LANES] = x[i * BLOCK:(i + 1) * BLOCK].astype(ref.dtype)


def _swa_logits(q, keys, slope, n):
    qi = lax.broadcasted_iota(jnp.int32, (GROUP_ROWS, BLOCK), 0) & (BLOCK - 1)
    kj = lax.broadcasted_iota(jnp.int32, (GROUP_ROWS, BLOCK), 1)
    s_all = lax.dot_general(q, keys, NT_DIMS, preferred_element_type=F32) * SCALE
    out = []
    for i, (dist, ok) in enumerate((
            (n * BLOCK + qi - kj, (kj >= N_PAD) & (n * BLOCK + qi - kj >= 0)),
            (BLOCK + qi - kj, (kj > qi) & (n >= 2)),
            (qi - kj, (kj <= qi) & (n >= 1)))):
        s = s_all[:, i * BLOCK:(i + 1) * BLOCK] - slope * dist.astype(F32)
        out.append(jnp.where(ok, s, NEG))
    return out


def _three_blocks(m_ref, p_ref, c_ref):
    return jnp.concatenate([m_ref[...], p_ref[...], c_ref[...]], axis=0)


def _swa_specs(nb):
    row = lambda b, n: b * nb + n
    qspec = pl.BlockSpec((BLOCK, A_GROUP * LANES), lambda b, g, n: (row(b, n), g))
    kv_m = pl.BlockSpec((BLOCK, LANES), lambda b, g, n: (row(b, 0), 0))
    kv_p = pl.BlockSpec((BLOCK, LANES), lambda b, g, n: (row(b, jnp.maximum(n - 1, 0)), 0))
    kv_c = pl.BlockSpec((BLOCK, LANES), lambda b, g, n: (row(b, n), 0))
    rowspec = pl.BlockSpec((1, GROUP_ROWS, 1), lambda b, g, n: (g, 0, 0))
    lsespec = pl.BlockSpec((1, 1, GROUP_ROWS, 1), lambda b, g, n: (row(b, n), g, 0, 0))
    return qspec, kv_m, kv_p, kv_c, rowspec, lsespec


def _swa_fwd(q, k, v, sink_rows, slope_rows, nb):
    t = q.shape[0]
    bsz = t // (nb * BLOCK)

    def body(q_ref, km_ref, kp_ref, kc_ref, vm_ref, vp_ref, vc_ref, sink_ref, slope_ref, o_ref, lse_ref):
        g = pl.program_id(1)
        n = pl.program_id(2)
        qq = _stack_heads(q_ref)
        sink = sink_ref[0]
        s_m, s_p, s_c = _swa_logits(qq, _three_blocks(km_ref, kp_ref, kc_ref), slope_ref[0], n)
        m = jnp.maximum(jnp.max(jnp.maximum(jnp.maximum(s_m, s_p), s_c), axis=-1, keepdims=True), sink)
        m_wide = jnp.broadcast_to(m, (GROUP_ROWS, BLOCK))
        e_m = jnp.exp(s_m - m_wide)
        e_p = jnp.exp(s_p - m_wide)
        e_c = jnp.exp(s_c - m_wide)
        z = jnp.sum((e_m + e_p) + e_c, axis=-1, keepdims=True) + jnp.exp(sink - m)
        inv = jnp.broadcast_to(1.0 / z, (GROUP_ROWS, BLOCK))
        probs = jnp.concatenate([(e_m * inv).astype(BF16), (e_p * inv).astype(BF16), (e_c * inv).astype(BF16)], axis=1)
        o = jnp.dot(probs, _three_blocks(vm_ref, vp_ref, vc_ref), preferred_element_type=F32)
        lane_group = lax.broadcasted_iota(jnp.int32, (GROUP_ROWS, LANES), 1) // HEAD_DIM
        _unstack_heads(o_ref, jnp.where(lane_group == g, o, 0.0))
        lse_ref[0, 0] = m + jnp.log(z)

    qspec, kv_m, kv_p, kv_c, rowspec, lsespec = _swa_specs(nb)
    return pl.pallas_call(
        body,
        name="swa_fwd",
        grid=(bsz, A_KV_HEADS, nb),
        in_specs=[qspec, kv_m, kv_p, kv_c, kv_m, kv_p, kv_c, rowspec, rowspec],
        out_specs=[qspec, lsespec],
        out_shape=[jax.ShapeDtypeStruct((t, A_PAD_WIDTH), BF16),
                   jax.ShapeDtypeStruct((t // BLOCK, A_KV_HEADS, GROUP_ROWS, 1), F32)],
        compiler_params=_cparams(("parallel", "parallel", "arbitrary")),
    )(q, k, k, k, v, v, v, sink_rows, slope_rows)


def _swa_bwd(q, k, v, do, lse, sink_rows, slope_rows, nb):
    t = q.shape[0]
    l = nb * BLOCK
    bsz = t // l

    def body(q_ref, km_ref, kp_ref, kc_ref, vm_ref, vp_ref, vc_ref, do_ref, lse_ref, sink_ref, slope_ref,
             dq_ref, dk_ref, dv_ref, dsink_ref, dk_acc, dv_acc):
        g = pl.program_id(1)
        n = pl.program_id(2)

        @pl.when((g == 0) & (n == 0))
        def _():
            dk_acc[...] = jnp.zeros_like(dk_acc)
            dv_acc[...] = jnp.zeros_like(dv_acc)

        @pl.when(n == 0)
        def _():
            dsink_ref[...] = jnp.zeros_like(dsink_ref)

        qq = _stack_heads(q_ref)
        dob = _stack_heads(do_ref)
        lse = lse_ref[0, 0]
        keys = _three_blocks(km_ref, kp_ref, kc_ref)
        lse_wide = jnp.broadcast_to(lse, (GROUP_ROWS, BLOCK))
        probs = [jnp.exp(s - lse_wide) for s in _swa_logits(qq, keys, slope_ref[0], n)]
        dp_all = lax.dot_general(dob, _three_blocks(vm_ref, vp_ref, vc_ref), NT_DIMS, preferred_element_type=F32)
        dps = [dp_all[:, i * BLOCK:(i + 1) * BLOCK] for i in range(3)]
        delta = jnp.sum((probs[0] * dps[0] + probs[1] * dps[1]) + probs[2] * dps[2], axis=-1, keepdims=True)
        delta_wide = jnp.broadcast_to(delta, (GROUP_ROWS, BLOCK))
        ds = jnp.concatenate([(p * (dp - delta_wide)).astype(BF16) for p, dp in zip(probs, dps)], axis=1)
        pb = jnp.concatenate([p.astype(BF16) for p in probs], axis=1)
        dq = jnp.dot(ds, keys, preferred_element_type=F32)
        dk_all = lax.dot_general(ds, qq, TN_DIMS, preferred_element_type=F32) * SCALE
        dv_all = lax.dot_general(pb, dob, TN_DIMS, preferred_element_type=F32)
        prev = jnp.maximum(n - 1, 0)
        for i, start in enumerate((0, prev * BLOCK, n * BLOCK)):
            rows = pl.ds(pl.multiple_of(start, BLOCK), BLOCK)
            dk_acc[rows, :] += dk_all[i * BLOCK:(i + 1) * BLOCK]
            dv_acc[rows, :] += dv_all[i * BLOCK:(i + 1) * BLOCK]
        dq = dq * SCALE
        first_half = lax.broadcasted_iota(jnp.int32, (BLOCK, LANES), 1) < HEAD_DIM
        for pair in range(A_GROUP // 2):
            even = dq[2 * pair * BLOCK:(2 * pair + 1) * BLOCK]
            odd = dq[(2 * pair + 1) * BLOCK:(2 * pair + 2) * BLOCK]
            left = jnp.where(g == 0, even, pltpu.roll(even, HEAD_DIM, 1))
            right = jnp.where(g == 0, pltpu.roll(odd, HEAD_DIM, 1), odd)
            dq_ref[:, pair * LANES:(pair + 1) * LANES] = jnp.where(first_half, left, right).astype(BF16)
        dsink_ref[0, 0] += -(jnp.exp(sink_ref[0] - lse) * delta)

        @pl.when((g == A_KV_HEADS - 1) & (n == nb - 1))
        def _():
            dk_ref[...] = dk_acc[...].astype(BF16)
            dv_ref[...] = dv_acc[...].astype(BF16)

    qspec, kv_m, kv_p, kv_c, rowspec, lsespec = _swa_specs(nb)
    kv_all = pl.BlockSpec((l, LANES), lambda b, g, n: (b, 0))
    return pl.pallas_call(
        body,
        name="swa_bwd",
        grid=(bsz, A_KV_HEADS, nb),
        in_specs=[qspec, kv_m, kv_p, kv_c, kv_m, kv_p, kv_c, qspec, lsespec, rowspec, rowspec],
        out_specs=[pl.BlockSpec((BLOCK, A_GROUP * HEAD_DIM), lambda b, g, n: (b * nb + n, g)), kv_all, kv_all,
                   pl.BlockSpec((1, 1, GROUP_ROWS, 1), lambda b, g, n: (b, g, 0, 0))],
        out_shape=[jax.ShapeDtypeStruct((t, A_WIDTH), BF16),
                   jax.ShapeDtypeStruct((t, LANES), BF16),
                   jax.ShapeDtypeStruct((t, LANES), BF16),
                   jax.ShapeDtypeStruct((bsz, A_KV_HEADS, GROUP_ROWS, 1), F32)],
        scratch_shapes=[pltpu.VMEM((l, LANES), F32), pltpu.VMEM((l, LANES), F32)],
        compiler_params=_cparams(("parallel", "arbitrary", "arbitrary")),
    )(q, k, k, k, v, v, v, do, lse, sink_rows, slope_rows)


CHUNK = KEY_BLOCKS * BLOCK


def _fox_chunk(qb, ci):
    sb = jnp.maximum(jnp.minimum(KEY_BLOCKS * ci, qb + 1 - KEY_BLOCKS), 0)
    lo = jnp.maximum(ci * CHUNK, N_PAD)
    return sb, lo, pl.ds(pl.multiple_of(sb * BLOCK, BLOCK), CHUNK)


def _fox_logits(s_ref, cr_ref, e, j, sb, lo, qb):
    lane = lax.broadcasted_iota(jnp.int32, (BLOCK, BLOCK), 1)
    ahead = lane - lax.broadcasted_iota(jnp.int32, (BLOCK, BLOCK), 0)
    first = (sb + j) * BLOCK
    s = s_ref[e, :, j * BLOCK:(j + 1) * BLOCK] - cr_ref[e, sb + j]
    return jnp.where((ahead <= qb * BLOCK - first) & (lane >= lo - first), s, NEG)


FOX_PAIRS = 4
FOX_HEADS = 2 * FOX_PAIRS
FOX_STEPS = B_PAIRS // FOX_PAIRS


def _fox_specs(nb):
    l = nb * BLOCK
    q_spec = pl.BlockSpec((BLOCK, FOX_PAIRS * LANES), lambda b, p, i: (b * nb + i, p))
    kv_spec = pl.BlockSpec((l, FOX_HEADS * LANES), lambda b, p, i: (b, p))
    cc_spec = pl.BlockSpec((FOX_HEADS, BLOCK, 1), lambda b, p, i: (b * FOX_STEPS + p, i, 0))
    cr_spec = pl.BlockSpec((FOX_HEADS, nb, 1, BLOCK), lambda b, p, i: (b * FOX_STEPS + p, 0, 0, 0))
    return q_spec, kv_spec, cc_spec, cr_spec


def _fox_fwd(q, k, v, c_row, nb, exchange=None):
    t = q.shape[0]
    bsz = t // (nb * BLOCK)
    assert nb >= KEY_BLOCKS

    n_x = len(exchange.ins) if exchange else 0

    def body(*refs):
        q_ref, k_ref, v_ref, cr_ref = refs[:4]
        o_ref, ox_ref, lse_ref = refs[4 + n_x:7 + n_x]
        s_scr, hi_scr, lo_scr = refs[7 + 2 * n_x:10 + 2 * n_x]
        qb = pl.program_id(2)
        if exchange:
            first = (pl.program_id(0) == 0) & (pl.program_id(1) == 0)
            last = (pl.program_id(0) == bsz - 1) & (pl.program_id(1) == FOX_STEPS - 1)
            _host_exchange(exchange, refs[4:4 + n_x], refs[7 + n_x:7 + 2 * n_x], refs[10 + 2 * n_x:],
                           first & (qb == 0), last & (qb == 0), last & (qb == nb - 1))
        qs = [q_ref[:, a * LANES:(a + 1) * LANES] * SCALE for a in range(FOX_PAIRS)]
        first_half = lax.broadcasted_iota(jnp.int32, (BLOCK, LANES), 1) < HEAD_DIM

        def step(ci, carry):
            stats, accs = carry[:2 * FOX_HEADS], carry[2 * FOX_HEADS:]
            sb, lo, krows = _fox_chunk(qb, ci)
            new_stats, new_accs = [], []
            for a in range(FOX_PAIRS):
                alphas = []
                pv = jnp.zeros((BLOCK, LANES), F32)
                pv_lo = jnp.zeros((BLOCK, LANES), F32)
                for e in (2 * a, 2 * a + 1):
                    m, z = stats[2 * e], stats[2 * e + 1]
                    tile = slice(e * LANES, (e + 1) * LANES)
                    s_scr[e] = lax.dot_general(qs[a], k_ref[krows, tile], NT_DIMS, preferred_element_type=F32)
                    top = None
                    for j in range(KEY_BLOCKS):
                        s = _fox_logits(s_scr, cr_ref, e, j, sb, lo, qb)
                        s_scr[e, :, j * BLOCK:(j + 1) * BLOCK] = s
                        top = s if top is None else jnp.maximum(top, s)
                    m_new = jnp.maximum(m, jnp.max(top, axis=-1, keepdims=True))
                    alpha = jnp.exp(m - m_new)
                    m_wide = jnp.broadcast_to(m_new, (BLOCK, BLOCK))
                    total = None
                    for j in range(KEY_BLOCKS):
                        cols = slice(j * BLOCK, (j + 1) * BLOCK)
                        p = jnp.exp(s_scr[e, :, cols] - m_wide)
                        total = p if total is None else total + p
                        hi = p.astype(BF16)
                        hi_scr[e, :, cols] = hi
                        lo_scr[e, :, cols] = (p - hi.astype(F32)).astype(BF16)
                    z = alpha * z + jnp.sum(total, axis=-1, keepdims=True)
                    vv = v_ref[krows, tile]
                    pv = pv + jnp.dot(hi_scr[e], vv, preferred_element_type=F32)
                    pv_lo = pv_lo + jnp.dot(lo_scr[e], vv, preferred_element_type=F32)
                    new_stats += [m_new, z]
                    alphas.append(alpha)
                alpha = jnp.where(first_half, alphas[0], alphas[1])
                new_accs += [alpha * accs[2 * a] + pv, alpha * accs[2 * a + 1] + pv_lo]
            return (*new_stats, *new_accs)

        col = lambda val: jnp.full((BLOCK, 1), val, F32)
        done = lax.fori_loop(
            0, (qb + KEY_BLOCKS) // KEY_BLOCKS, step,
            (col(NEG), col(0.0)) * FOX_HEADS + (jnp.zeros((BLOCK, LANES), F32),) * (2 * FOX_PAIRS))
        for a in range(FOX_PAIRS):
            m0, z0, m1, z1 = done[4 * a:4 * a + 4]
            acc, acc_lo = done[2 * FOX_HEADS + 2 * a:2 * FOX_HEADS + 2 * a + 2]
            inv = 1.0 / jnp.where(first_half, z0, z1)
            tile = slice(a * LANES, (a + 1) * LANES)
            o_ref[:, tile] = (acc * inv).astype(BF16)
            ox_ref[:, tile] = (acc + acc_lo) * inv
            lse_ref[2 * a] = m0 + jnp.log(z0)
            lse_ref[2 * a + 1] = m1 + jnp.log(z1)

    q_spec, kv_spec, cc_spec, cr_spec = _fox_specs(nb)
    outs = pl.pallas_call(
        body,
        name="fox_fwd",
        grid=(bsz, FOX_STEPS, nb),
        in_specs=[q_spec, kv_spec, kv_spec, cr_spec] + [HBM_SPEC] * n_x,
        out_specs=[q_spec, q_spec, cc_spec] + [HBM_SPEC] * n_x,
        out_shape=[jax.ShapeDtypeStruct((t, B_WIDTH), BF16), jax.ShapeDtypeStruct((t, B_WIDTH), F32),
                   jax.ShapeDtypeStruct((bsz * B_HEADS, nb * BLOCK, 1), F32)] + (exchange.out_shape if exchange else []),
        scratch_shapes=[pltpu.VMEM((FOX_HEADS, BLOCK, CHUNK), F32), pltpu.VMEM((FOX_HEADS, BLOCK, CHUNK), BF16),
                        pltpu.VMEM((FOX_HEADS, BLOCK, CHUNK), BF16)] + (exchange.scratch if exchange else []),
        compiler_params=_cparams(("arbitrary",) * 3 if exchange else ("parallel", "parallel", "arbitrary")),
    )(q, k, v, c_row, *(exchange.ins if exchange else []))
    return outs[:3], outs[3:]


def _fox_bwd(q, k, v, o_exact, do, lse, c_row, nb, exchange=None):
    t = q.shape[0]
    l = nb * BLOCK
    bsz = t // l

    n_x = len(exchange.ins) if exchange else 0

    def body(*refs):
        q_ref, k_ref, v_ref, ox_ref, do_ref, lse_ref, cr_ref = refs[:7]
        dq_ref, dk_ref, dv_ref, dc_ref = refs[7 + n_x:11 + n_x]
        dk_acc, dv_acc, s_scr, dp_scr, p_scr, ds_scr = refs[11 + 2 * n_x:17 + 2 * n_x]
        qb = pl.program_id(2)
        if exchange:
            first = (pl.program_id(0) == 0) & (pl.program_id(1) == 0)
            last = (pl.program_id(0) == bsz - 1) & (pl.program_id(1) == FOX_STEPS - 1)
            _host_exchange(exchange, refs[7:7 + n_x], refs[11 + n_x:11 + 2 * n_x], refs[17 + 2 * n_x:],
                           first & (qb == 0), last & (qb == 0), last & (qb == nb - 1))

        @pl.when(qb == 0)
        def _():
            dk_acc[...] = jnp.zeros_like(dk_acc)
            dv_acc[...] = jnp.zeros_like(dv_acc)
            dc_ref[...] = jnp.zeros_like(dc_ref)

        top_half = lax.broadcasted_iota(jnp.int32, (LANES, BLOCK), 0) < HEAD_DIM
        pair_t = lambda x: jnp.concatenate([jnp.where(top_half, x.T, 0), jnp.where(top_half, 0, x.T)], axis=1)
        first_half = lax.broadcasted_iota(jnp.int32, (BLOCK, LANES), 1) < HEAD_DIM
        wide = lambda col: jnp.broadcast_to(col, (BLOCK, BLOCK))
        qs, dobs, qs_t, dob_t, deltas = [], [], [], [], []
        for a in range(FOX_PAIRS):
            tile = slice(a * LANES, (a + 1) * LANES)
            qs.append(q_ref[:, tile] * SCALE)
            dobs.append(do_ref[:, tile])
            qs_t.append(pair_t(qs[a]))
            dob_t.append(pair_t(dobs[a]))
            weighted = dobs[a].astype(F32) * ox_ref[:, tile]
            deltas += [wide(jnp.sum(jnp.where(first_half, weighted, 0.0), axis=-1, keepdims=True)),
                       wide(jnp.sum(jnp.where(first_half, 0.0, weighted), axis=-1, keepdims=True))]
        lses = [wide(lse_ref[e]) for e in range(FOX_HEADS)]

        def step(ci, dqs):
            sb, lo, krows = _fox_chunk(qb, ci)
            dqs = list(dqs)
            for a in range(FOX_PAIRS):
                for e in (2 * a, 2 * a + 1):
                    tile = slice(e * LANES, (e + 1) * LANES)
                    kk = k_ref[krows, tile]
                    s_scr[e] = lax.dot_general(qs[a], kk, NT_DIMS, preferred_element_type=F32)
                    dp_scr[e] = lax.dot_general(dobs[a], v_ref[krows, tile], NT_DIMS, preferred_element_type=F32)
                    for j in range(KEY_BLOCKS):
                        cols = slice(j * BLOCK, (j + 1) * BLOCK)
                        p = jnp.exp(_fox_logits(s_scr, cr_ref, e, j, sb, lo, qb) - lses[e])
                        ds = p * (dp_scr[e, :, cols] - deltas[e])
                        dc_ref[e, sb + j] -= jnp.sum(ds, axis=0, keepdims=True)
                        p_scr[e, :, cols] = p.astype(BF16)
                        ds_scr[e, :, cols] = ds.astype(BF16)
                    dqs[a] = dqs[a] + jnp.dot(ds_scr[e], kk, preferred_element_type=F32)
                both = slice(2 * a, 2 * a + 2)
                dk_t = jnp.dot(qs_t[a], ds_scr[both].reshape(2 * BLOCK, CHUNK), preferred_element_type=F32)
                dv_t = jnp.dot(dob_t[a], p_scr[both].reshape(2 * BLOCK, CHUNK), preferred_element_type=F32)
                for j in range(KEY_BLOCKS):
                    cols = slice(j * BLOCK, (j + 1) * BLOCK)
                    dk_acc[a * nb + sb + j] += dk_t[:, cols]
                    dv_acc[a * nb + sb + j] += dv_t[:, cols]
            return tuple(dqs)

        dqs = lax.fori_loop(0, (qb + KEY_BLOCKS) // KEY_BLOCKS, step,
                            (jnp.zeros((BLOCK, LANES), F32),) * FOX_PAIRS)
        for a in range(FOX_PAIRS):
            dq_ref[:, a * LANES:(a + 1) * LANES] = (dqs[a] * SCALE).astype(BF16)

        @pl.when(qb == nb - 1)
        def _():
            for a in range(FOX_PAIRS):
                for kb in range(nb):
                    rows = slice(kb * BLOCK, (kb + 1) * BLOCK)
                    for acc, out_ref in ((dk_acc, dk_ref), (dv_acc, dv_ref)):
                        out_ref[rows, a * LANES:(a + 1) * LANES] = acc[a * nb + kb].T.astype(BF16)

    q_spec, kv_spec, cc_spec, cr_spec = _fox_specs(nb)
    dkv_spec = pl.BlockSpec((l, FOX_PAIRS * LANES), lambda b, p, i: (b, p))
    outs = pl.pallas_call(
        body,
        name="fox_bwd",
        grid=(bsz, FOX_STEPS, nb),
        in_specs=[q_spec, kv_spec, kv_spec, q_spec, q_spec, cc_spec, cr_spec] + [HBM_SPEC] * n_x,
        out_specs=[q_spec, dkv_spec, dkv_spec, cr_spec] + [HBM_SPEC] * n_x,
        out_shape=[jax.ShapeDtypeStruct((t, B_WIDTH), BF16), jax.ShapeDtypeStruct((t, B_WIDTH), BF16),
                   jax.ShapeDtypeStruct((t, B_WIDTH), BF16),
                   jax.ShapeDtypeStruct((bsz * B_HEADS, nb, 1, BLOCK), F32)] + (exchange.out_shape if exchange else []),
        scratch_shapes=[pltpu.VMEM((FOX_PAIRS * nb, LANES, BLOCK), F32), pltpu.VMEM((FOX_PAIRS * nb, LANES, BLOCK), F32),
                        pltpu.VMEM((FOX_HEADS, BLOCK, CHUNK), F32), pltpu.VMEM((FOX_HEADS, BLOCK, CHUNK), F32),
                        pltpu.VMEM((FOX_HEADS, BLOCK, CHUNK), BF16), pltpu.VMEM((FOX_HEADS, BLOCK, CHUNK), BF16)]
        + (exchange.scratch if exchange else []),
        compiler_params=_cparams(("arbitrary",) * 3 if exchange else ("parallel", "parallel", "arbitrary")),
    )(q, k, v, o_exact, do, lse, c_row, *(exchange.ins if exchange else []))
    return outs[:4], outs[4:]


def _loss_head(h, final_w, target):
    bsz, l, d = h.shape
    nb = l // BLOCK

    def body(h_ref, w_ref, t_ref, loss_ref, dh_ref, dw_ref):
        b = pl.program_id(0)
        n = pl.program_id(1)

        @pl.when((b == 0) & (n == 0))
        def _():
            loss_ref[...] = jnp.zeros_like(loss_ref)
            dw_ref[...] = jnp.zeros_like(dw_ref)

        @pl.when(n == 0)
        def _():
            dh_ref[...] = jnp.zeros_like(dh_ref)

        @pl.when(n > 0)
        def _():
            hh = h_ref[0]
            w = w_ref[...]
            r = _rms_scale(hh)
            err = (hh * r) * w - t_ref[0]
            loss_ref[...] += 0.5 * jnp.sum(jnp.mean(err * err, axis=-1, keepdims=True), axis=0, keepdims=True)
            dy = err * (1.0 / d)
            dh, dw = _rms_bwd(dy, hh, w)
            dh_ref[0] = dh
            dw_ref[...] += dw

    return pl.pallas_call(
        body,
        name="loss_head",
        grid=(bsz, nb),
        in_specs=[
            pl.BlockSpec((1, BLOCK, d), lambda b, n: (b, n, 0)),
            pl.BlockSpec((1, d), lambda b, n: (0, 0)),
            pl.BlockSpec((1, BLOCK, d), lambda b, n: (b, jnp.maximum(n - 1, 0), 0)),
        ],
        out_specs=[
            pl.BlockSpec((1, 128), lambda b, n: (0, 0)),
            pl.BlockSpec((1, BLOCK, d), lambda b, n: (b, n, 0)),
            pl.BlockSpec((1, d), lambda b, n: (0, 0)),
        ],
        out_shape=[jax.ShapeDtypeStruct((1, 128), F32), jax.ShapeDtypeStruct((bsz, l, d), F32),
                   jax.ShapeDtypeStruct((1, d), F32)],
        compiler_params=_cparams(("arbitrary", "arbitrary")),
    )(h, final_w, target)


def _pad_tiles(w, src, heads, lane_slot, axis):
    pieces = []
    for h in range(heads):
        x = lax.slice_in_dim(w, src + HEAD_DIM * h, src + HEAD_DIM * (h + 1), axis=axis)
        z = jnp.zeros_like(x)
        pieces += [x, z] if lane_slot(h) == 0 else [z, x]
    return pieces


def _unpad_tiles(g, off, heads, lane_slot, axis):
    return [lax.slice_in_dim(g, off + LANES * h + HEAD_DIM * lane_slot(h),
                             off + LANES * h + HEAD_DIM * (lane_slot(h) + 1), axis=axis) for h in range(heads)]


def _layout_w_in(w):
    pad_f = jnp.zeros((w.shape[0], F_COLS - B_HEADS), w.dtype)
    return jnp.concatenate([w[:, :SRC_F], w[:, SRC_GA:], w[:, SRC_F:SRC_GA], pad_f], axis=1)


def _unlayout_w_in(g):
    return jnp.concatenate([g[:, :OFF_GA], g[:, OFF_F:OFF_F + B_HEADS], g[:, OFF_GA:OFF_F]], axis=1)


def _local_step(x, target, meta, norms, b_forget, sinks, w, comm=None):
    n1, nmix, n2, nfin = norms
    w1i, w1o = w[:2]
    bsz, seq, d = x.shape
    l = PREFIX + seq
    nb = l // BLOCK
    t = bsz * l

    h0 = jnp.concatenate([jnp.zeros((bsz, N_PAD, d), F32),
                          jnp.broadcast_to(meta[None], (bsz, N_META, d)), x], axis=1).reshape(t, d)

    if comm is None:
        (h1, g1, u1), _ = _ffn_fwd(h0, n1, w1i, w1o)
        w_in, wa, wb, wo, w2i, w2o = w[2:]
    else:
        (h1, g1, u1), gathered = _ffn_fwd(h0, n1, w1i, w1o, comm.gather(GATHER_PROJ))
        w_in, = comm.gathered(GATHER_PROJ, gathered)
    wp = _layout_w_in(w_in)
    un, qa, ka, va, qb, kb, vb, ga, gb, f_logit = _proj_fwd(h1, nmix, wp)
    b_pad = jnp.concatenate([b_forget, jnp.zeros((1, F_COLS - B_HEADS), F32)], axis=1)
    c = _forget_cumsum(f_logit, b_pad, nb)
    c_heads = c[:, :B_HEADS].reshape(bsz, l, B_HEADS).transpose(0, 2, 1).reshape(bsz * B_HEADS, l)
    c_row = c_heads.reshape(bsz * B_HEADS, nb, 1, BLOCK)

    slopes = jnp.exp2(-8.0 * jnp.arange(1, A_HEADS + 1, dtype=F32) / A_HEADS)
    slope_rows = jnp.repeat(slopes.reshape(A_KV_HEADS, A_GROUP), BLOCK, axis=1)[:, :, None]
    sink_rows = jnp.repeat(sinks.reshape(A_KV_HEADS, A_GROUP), BLOCK, axis=1)[:, :, None]

    oa, lse_a = _swa_fwd(qa, ka, va, sink_rows, slope_rows, nb)
    if comm is None:
        (ob, ob_exact, lse_b), _ = _fox_fwd(qb, kb, vb, c_row, nb)
    else:
        (ob, ob_exact, lse_b), gathered = _fox_fwd(qb, kb, vb, c_row, nb, comm.gather(GATHER_LATE))
        wa, wb, wo, w2i, w2o = comm.gathered(GATHER_LATE, gathered)
    wa_p = jnp.concatenate(_pad_tiles(wa, 0, A_HEADS, A_SLOT, 0), axis=0)
    h2, mixed = _merge_fwd(h1, oa, ob, ga, gb, wa_p, wb, wo)
    (h3, g2, u2), _ = _ffn_fwd(h2, n2, w2i, w2o)
    loss, dh3, d_nfin = _loss_head(h3.reshape(bsz, l, d), nfin, target)

    (dh2, n2b, a2, dgu2, df2, dn2_parts), _ = _ffn_bwd(dh3.reshape(t, d), h2, n2, g2, u2, w2i, w2o)
    g_w2o = _tn_matmul(a2, df2, "grad_ffn2_w_out")
    g_w2i = _tn_matmul(n2b, dgu2, "grad_ffn2_w_in")

    hosted = comm.swap("ffn2", dict(ffn2_w_in=g_w2i, ffn2_w_out=g_w2o)) if comm else None
    (dya, dyb, doa, dob, dga, dgb, dh2b), swapped = _merge_bwd(dh2, oa, ob, ga, gb, wa_p, wb, wo, hosted)
    g_wo = _tn_matmul(mixed, dh2b, "grad_w_out")
    g_wa = jnp.concatenate(_unpad_tiles(_tn_matmul(oa, dya, "grad_w_branch_a"), 0, A_HEADS, A_SLOT, 0), axis=0)
    g_wb = _tn_matmul(ob, dyb, "grad_w_branch_b")

    dqa, dka, dva, dsink_rows = _swa_bwd(qa, ka, va, doa, lse_a, sink_rows, slope_rows, nb)
    hosted = comm.scatter("ffn2", swapped) if comm else None
    (dqb, dkb, dvb, dc_row), pieces = _fox_bwd(qb, kb, vb, ob_exact, dob, lse_b, c_row, nb, hosted)
    if comm:
        comm.received("ffn2", pieces)
    dc = dc_row.reshape(bsz, B_HEADS, l).transpose(0, 2, 1).reshape(t, B_HEADS)
    dc = jnp.concatenate([dc, jnp.zeros((t, F_COLS - B_HEADS), F32)], axis=1)
    df_logit, db_parts = _forget_cumsum_bwd(dc, f_logit, b_pad, nb)

    dproj = jnp.concatenate([dqa, dka, dva, dqb, dkb, dvb, dga, dgb, df_logit], axis=1)
    g_win = _unlayout_w_in(_tn_matmul(un, dproj, "grad_w_in"))
    hosted = comm.swap("mixer", dict(w_in=g_win, w_branch_a=g_wa, w_branch_b=g_wb, w_out=g_wo)) if comm else None
    (dh1, dnmix_parts), swapped = _proj_bwd(dh2, h1, nmix, dproj, wp, hosted)
    hosted = comm.scatter("mixer", swapped) if comm else None
    (dh0, n1b, a1, dgu1, df1, dn1_parts), pieces = _ffn_bwd(dh1, h0, n1, g1, u1, w1i, w1o, hosted)
    if comm:
        comm.received("mixer", pieces)
    g_w1o = _tn_matmul(a1, df1, "grad_ffn1_w_out")
    g_w1i = _tn_matmul(n1b, dgu1, "grad_ffn1_w_in")

    dh0 = dh0.reshape(bsz, l, d)
    grad_x = dh0[:, PREFIX:]
    small = dict(
        meta_tokens=jnp.sum(dh0[:, N_PAD:PREFIX], axis=0),
        ffn1_norm=jnp.sum(dn1_parts, axis=0),
        mix_norm=jnp.sum(dnmix_parts, axis=0),
        ffn2_norm=jnp.sum(dn2_parts, axis=0),
        final_norm=d_nfin,
        b_forget=jnp.sum(db_parts, axis=0)[:, :B_HEADS],
        attn_sinks=jnp.sum(dsink_rows.reshape(bsz, A_HEADS, BLOCK), axis=(0, 2)).reshape(1, A_HEADS),
    )
    big = dict(ffn1_w_in=g_w1i, ffn1_w_out=g_w1o, w_in=g_win, w_branch_a=g_wa, w_branch_b=g_wb,
               w_out=g_wo, ffn2_w_in=g_w2i, ffn2_w_out=g_w2o)
    return loss, grad_x, small, big


BIG = (
    ("ffn1_w_in", (D_MODEL, 5632), 1),
    ("ffn1_w_out", (2816, D_MODEL), 0),
    ("w_in", (D_MODEL, W_IN_COLS), 1),
    ("w_branch_a", (A_WIDTH, D_MODEL), 1),
    ("w_branch_b", (B_WIDTH, D_MODEL), 1),
    ("w_out", (D_MODEL, D_MODEL), 0),
    ("ffn2_w_in", (D_MODEL, 5632), 1),
    ("ffn2_w_out", (2816, D_MODEL), 0),
)
STACKED = "w_in"


def _coords():
    return lax.axis_index("x"), lax.axis_index("y"), lax.axis_index("c")


def _other_chips(x, y):
    return ((1 - x, y), (x, 1 - y), (1 - x, 1 - y))


def _chip_part(ref, name, shape, axis, k):
    if name == STACKED:
        return ref.at[k]
    size = shape[axis] // N_CHIPS
    start = pl.multiple_of(k * size, size)
    return ref.at[pl.ds(start, size), :] if axis == 0 else ref.at[:, pl.ds(start, size)]


def _full_shape(name, shape):
    return (N_CHIPS, shape[0], shape[1] // N_CHIPS) if name == STACKED else shape


class _Exchange:
    def __init__(self, ins, out_shape, n_sems, ops):
        self.ins, self.out_shape, self.n_sems, self.ops = list(ins), list(out_shape), n_sems, ops

    @property
    def scratch(self):
        return [pltpu.SemaphoreType.DMA((self.n_sems,)), pltpu.SemaphoreType.DMA((self.n_sems,))]


SEMS_PER_GATHER = 7


def _gather_exchange(shards, table):
    n = len(table)

    def ops(ins, outs, send_sems, recv_sems):
        x, y, c = _coords()
        mine = 2 * x + y
        sibling = (x, y, 1 - c)
        chips = _other_chips(x, y)
        slots = [2 * chip[0] + chip[1] for chip in chips]

        def part(i, k):
            name, shape, axis = table[i][:3]
            return _chip_part(outs[i], name, shape, axis, k)

        def half(ref, h):
            rows = ref.shape[0] // 2
            return ref.at[pl.ds(pl.multiple_of(h * rows, rows), rows), :]

        def own(i):
            sem = SEMS_PER_GATHER * i
            return pltpu.make_async_remote_copy(ins[i], part(i, mine), send_sems.at[sem], recv_sems.at[sem],
                                                device_id=sibling, device_id_type=MESH_ID)

        def fetch(i, j, slot):
            sem = SEMS_PER_GATHER * i + 1 + j
            if table[i][4]:
                src, dst = half(ins[i], c), half(part(i, slot), c)
            else:
                src, dst = ins[i], part(i, slot)
            return pltpu.make_async_remote_copy(src, dst, send_sems.at[sem], recv_sems.at[sem],
                                                device_id=(chips[j][0], chips[j][1], c), device_id_type=MESH_ID)

        def forward(i, j, h):
            sem = SEMS_PER_GATHER * i + 4 + j
            region = half(part(i, slots[j]), h)
            return pltpu.make_async_remote_copy(region, region, send_sems.at[sem], recv_sems.at[sem],
                                                device_id=sibling, device_id_type=MESH_ID)

        def start():
            for i in range(n):
                for j in range(3):
                    fetch(i, j, mine).start()
            for i in range(n):
                own(i).start()

        def relay():
            for i in range(n):
                for j in range(3):
                    fetch(i, j, slots[j]).wait_recv()
                    if table[i][4]:
                        forward(i, j, c).start()

        def finish():
            for i in range(n):
                own(i).wait()
                for j in range(3):
                    if table[i][4]:
                        forward(i, j, 1 - c).wait_recv()
                        forward(i, j, c).wait_send()
                    fetch(i, j, mine).wait_send()

        return start, relay, finish

    out_shape = [jax.ShapeDtypeStruct(_full_shape(name, shape), dtype) for name, shape, _, dtype, _ in table]
    return _Exchange(shards, out_shape, SEMS_PER_GATHER * n, ops)


def _run_exchange(exchange, name):
    n = len(exchange.ins)

    def body(*refs):
        start, relay, finish = exchange.ops(refs[:n], refs[n:2 * n], *refs[2 * n:])
        start()
        relay()
        finish()

    return pl.pallas_call(
        body,
        name=name,
        in_specs=[HBM_SPEC] * n,
        out_specs=[HBM_SPEC] * n,
        out_shape=exchange.out_shape,
        scratch_shapes=exchange.scratch,
    )(*exchange.ins)


def _host_exchange(exchange, in_refs, out_refs, sem_refs, first, middle, last):
    start, relay, finish = exchange.ops(in_refs, out_refs, *sem_refs)
    pl.when(first)(start)
    pl.when(middle)(relay)
    pl.when(last)(finish)


def _halves_view(name, shape, axis):
    r, c = shape
    if name == STACKED:
        return (N_CHIPS, 2, r // 2, c // N_CHIPS), lambda ref, h: ref.at[:, h]
    if axis == 1:
        return (2, r // 2, c), lambda ref, h: ref.at[h]
    return (N_CHIPS, 2, r // N_CHIPS // 2, c), lambda ref, h: ref.at[:, h]


def _halves_exchange(grads, entries):
    n_w = len(entries)
    views = [_halves_view(*entry) for entry in entries]

    def ops(ins, outs, send_sems, recv_sems):
        x, y, c = _coords()
        copies = [pltpu.make_async_remote_copy(views[i][1](ins[i], 1 - c), outs[i], send_sems.at[i], recv_sems.at[i],
                                               device_id=(x, y, 1 - c), device_id_type=MESH_ID) for i in range(n_w)]

        def start():
            for cp in copies:
                cp.start()

        def finish():
            for cp in copies:
                cp.wait()

        return start, lambda: None, finish

    half_shape = lambda v: tuple(d for i, d in enumerate(v) if i != (1 if len(v) == 4 else 0))
    out_shape = [jax.ShapeDtypeStruct(half_shape(v[0]), F32) for v in views]
    return _Exchange([g.reshape(v[0]) for g, v in zip(grads, views)], out_shape, n_w, ops)


def _add_sibling(g_view, recv, c, name):
    shape = recv.shape
    if len(shape) == 2:
        tr = _tile(shape[0], 128, 16)
        grid = (shape[0] // tr,)
        g_spec = pl.BlockSpec((None, tr, shape[1]), lambda i, c_ref: (c_ref[0], i, 0))
        r_spec = pl.BlockSpec((tr, shape[1]), lambda i, c_ref: (i, 0))
    else:
        tr = _tile(shape[1], 256, 16)
        grid = (N_CHIPS, shape[1] // tr)
        g_spec = pl.BlockSpec((None, None, tr, shape[2]), lambda k, i, c_ref: (k, c_ref[0], i, 0))
        r_spec = pl.BlockSpec((None, tr, shape[2]), lambda k, i, c_ref: (k, i, 0))

    def body(c_ref, g_ref, r_ref, o_ref):
        o_ref[...] = (g_ref[...] + r_ref[...]).astype(BF16)

    return pl.pallas_call(
        body,
        name="add_sibling_" + name,
        grid_spec=pltpu.PrefetchScalarGridSpec(num_scalar_prefetch=1, grid=grid, in_specs=[g_spec, r_spec],
                                               out_specs=r_spec),
        out_shape=jax.ShapeDtypeStruct(shape, BF16),
        compiler_params=_cparams(("parallel",) * len(grid)),
    )(c, g_view, recv)


def _piece_of(ref, name, axis, k):
    if name == STACKED or axis == 0:
        return ref.at[k]
    size = ref.shape[1] // N_CHIPS
    return ref.at[:, pl.ds(pl.multiple_of(k * size, size), size)]


def _piece_shape(name, shape, axis):
    r, c = shape
    return (r // 2, c // N_CHIPS) if (axis == 1) else (r // N_CHIPS // 2, c)


def _scatter_exchange(partials, entries):
    n_w = len(entries)

    def ops(ins, outs, send_sems, recv_sems):
        x, y, c = _coords()
        chips = _other_chips(x, y)
        copies = []
        for i, (name, _, axis) in enumerate(entries):
            for j, chip in enumerate(chips):
                sem = 3 * i + j
                copies.append(pltpu.make_async_remote_copy(
                    _piece_of(ins[i], name, axis, 2 * chip[0] + chip[1]), outs[i].at[j], send_sems.at[sem],
                    recv_sems.at[sem], device_id=(chip[0], chip[1], c), device_id_type=MESH_ID))

        def start():
            for cp in copies:
                cp.start()

        def finish():
            for cp in copies:
                cp.wait()

        return start, lambda: None, finish

    out_shape = [jax.ShapeDtypeStruct((3,) + _piece_shape(*entry), BF16) for entry in entries]
    return _Exchange(partials, out_shape, 3 * n_w, ops)


def _add_chips(partial, recv, mine, name, axis):
    rows, cols = recv.shape[1:]
    tr = _tile(rows, 256, 16)
    if name == STACKED or axis == 0:
        p_spec = pl.BlockSpec((None, tr, cols), lambda i, k_ref: (k_ref[0], i, 0))
    else:
        p_spec = pl.BlockSpec((tr, cols), lambda i, k_ref: (i, k_ref[0]))

    def body(k_ref, p_ref, r_ref, o_ref):
        f32 = lambda a: a.astype(F32)
        o_ref[...] = ((f32(p_ref[...]) + f32(r_ref[0])) + f32(r_ref[1])) + f32(r_ref[2])

    return pl.pallas_call(
        body,
        name="add_chips_" + name,
        grid_spec=pltpu.PrefetchScalarGridSpec(
            num_scalar_prefetch=1, grid=(rows // tr,),
            in_specs=[p_spec, pl.BlockSpec((3, tr, cols), lambda i, k_ref: (0, i, 0))],
            out_specs=pl.BlockSpec((tr, cols), lambda i, k_ref: (i, 0))),
        out_shape=jax.ShapeDtypeStruct((rows, cols), F32),
        compiler_params=_cparams(("parallel",)),
    )(mine, partial, recv)


def _share_with_sibling(halves):
    n_w = len(halves)

    def body(*refs):
        ins, outs = refs[:n_w], refs[n_w:2 * n_w]
        send_sems, recv_sems = refs[2 * n_w:]
        x, y, c = _coords()
        copies = [pltpu.make_async_remote_copy(ins[i], outs[i], send_sems.at[i], recv_sems.at[i],
                                               device_id=(x, y, 1 - c), device_id_type=MESH_ID) for i in range(n_w)]
        for cp in copies:
            cp.start()
        for cp in copies:
            cp.wait()

    return pl.pallas_call(
        body,
        name="share_with_sibling",
        in_specs=[HBM_SPEC] * n_w,
        out_specs=[HBM_SPEC] * n_w,
        out_shape=[jax.ShapeDtypeStruct(h.shape, F32) for h in halves],
        scratch_shapes=[pltpu.SemaphoreType.DMA((n_w,)), pltpu.SemaphoreType.DMA((n_w,))],
    )(*halves)


SMALL_ROWS = 168


def _all_reduce_small(buf):
    def body(b_ref, out_ref, gathered, send_sems, recv_sems):
        x, y, c = _coords()
        me = 4 * x + 2 * y + c
        peers = [(x ^ fx, y ^ fy, c ^ fc) for fx in (0, 1) for fy in (0, 1) for fc in (0, 1)][1:]

        def copy(j, slot, dev):
            return pltpu.make_async_remote_copy(b_ref, gathered.at[slot], send_sems.at[j], recv_sems.at[j],
                                                device_id=dev, device_id_type=MESH_ID)

        for j, dev in enumerate(peers):
            copy(j, me, dev).start()
        gathered[me] = b_ref[...]
        for j, dev in enumerate(peers):
            copy(j, 4 * dev[0] + 2 * dev[1] + dev[2], dev).wait()
        acc = gathered[0]
        for d in range(1, N_DEV):
            acc = acc + gathered[d]
        out_ref[...] = acc

    return pl.pallas_call(
        body,
        name="all_reduce_small",
        in_specs=[VMEM_SPEC],
        out_specs=VMEM_SPEC,
        out_shape=jax.ShapeDtypeStruct(buf.shape, F32),
        scratch_shapes=[pltpu.VMEM((N_DEV,) + buf.shape, F32), pltpu.SemaphoreType.DMA((N_DEV - 1,)),
                        pltpu.SemaphoreType.DMA((N_DEV - 1,))],
    )(buf)


def _adamw(w, g, m, v):
    r, rest = w.shape[0], w.shape[1:]
    per_row = 1
    for dim in rest:
        per_row *= dim
    tr = _tile(r, max(8, (5 << 19) // (4 * per_row)), 8 if len(rest) == 1 else 1)

    def body(w_ref, g_ref, m_ref, v_ref, d_ref, mo_ref, vo_ref):
        gg = g_ref[...]
        mm = ADAM_B1 * m_ref[...] + (1.0 - ADAM_B1) * gg
        vv = ADAM_B2 * v_ref[...] + (1.0 - ADAM_B2) * (gg * gg)
        m_hat = mm / (1.0 - ADAM_B1 ** ADAM_STEP)
        v_hat = vv / (1.0 - ADAM_B2 ** ADAM_STEP)
        d_ref[...] = -ADAM_LR * (m_hat / (jnp.sqrt(v_hat) + ADAM_EPS) + ADAM_WD * w_ref[...])
        mo_ref[...] = mm
        vo_ref[...] = vv

    spec = pl.BlockSpec((tr,) + rest, lambda i: (i,) + (0,) * len(rest))
    return pl.pallas_call(
        body,
        name="adamw",
        grid=(r // tr,),
        in_specs=[spec] * 4,
        out_specs=[spec] * 3,
        out_shape=[jax.ShapeDtypeStruct(w.shape, F32)] * 3,
        compiler_params=_cparams(("parallel",)),
    )(w, g, m, v)


def _adamw_halves(w, own, other, m, v, c, name):
    r, cols = w.shape
    half = r // 2
    tr = _tile(half, 256, 8)
    nt = half // tr
    whole = pl.BlockSpec((tr, cols), lambda h, i, c_ref: (h * nt + i, 0))
    part = pl.BlockSpec((tr, cols), lambda h, i, c_ref: (i, 0))

    def body(c_ref, w_ref, own_ref, other_ref, m_ref, v_ref, g_ref, d_ref, mo_ref, vo_ref):
        gg = jnp.where(pl.program_id(0) == c_ref[0], own_ref[...], other_ref[...])
        g_ref[...] = gg
        mm = ADAM_B1 * m_ref[...] + (1.0 - ADAM_B1) * gg
        vv = ADAM_B2 * v_ref[...] + (1.0 - ADAM_B2) * (gg * gg)
        m_hat = mm / (1.0 - ADAM_B1 ** ADAM_STEP)
        v_hat = vv / (1.0 - ADAM_B2 ** ADAM_STEP)
        d_ref[...] = -ADAM_LR * (m_hat / (jnp.sqrt(v_hat) + ADAM_EPS) + ADAM_WD * w_ref[...])
        mo_ref[...] = mm
        vo_ref[...] = vv

    return pl.pallas_call(
        body,
        name="adamw_" + name,
        grid_spec=pltpu.PrefetchScalarGridSpec(
            num_scalar_prefetch=1, grid=(2, nt),
            in_specs=[whole, part, part, whole, whole], out_specs=[whole] * 4),
        out_shape=[jax.ShapeDtypeStruct((r, cols), F32)] * 4,
        compiler_params=_cparams(("parallel", "parallel")),
    )(c, w, own, other, m, v)


GATHER_FIRST = ("ffn1_w_in", "ffn1_w_out")
GATHER_PROJ = ("w_in",)
GATHER_LATE = ("w_branch_a", "w_branch_b", "w_out", "ffn2_w_in", "ffn2_w_out")


class _Comm:
    def __init__(self, shards, c_arr, mine_arr):
        self.shards, self.c, self.mine = shards, c_arr, mine_arr
        self.groups, self.halves = {}, {}
        self.by_name = {entry[0]: entry for entry in BIG}

    def gather(self, names):
        table = [self.by_name[n] + (BF16, True) for n in names]
        return _gather_exchange([self.shards[n] for n in names], table)

    def gathered(self, names, outs):
        return [o.transpose(1, 0, 2).reshape(D_MODEL, W_IN_COLS) if n == STACKED else o for n, o in zip(names, outs)]

    def swap(self, tag, grads):
        entries = [self.by_name[n] for n in grads]
        arrays = [g.reshape(D_MODEL, N_CHIPS, W_IN_COLS // N_CHIPS).transpose(1, 0, 2) if n == STACKED else g
                  for n, g in grads.items()]
        self.groups[tag] = (entries, arrays)
        return _halves_exchange(arrays, entries)

    def scatter(self, tag, received):
        entries, arrays = self.groups[tag]
        views = [_halves_view(*entry) for entry in entries]
        partials = [_add_sibling(g.reshape(v[0]), r, self.c, name)
                    for g, v, r, (name, _, _) in zip(arrays, views, received, entries)]
        self.groups[tag] = (entries, partials)
        return _scatter_exchange(partials, entries)

    def received(self, tag, pieces):
        entries, partials = self.groups[tag]
        for p, r, (name, _, axis) in zip(partials, pieces, entries):
            self.halves[name] = _add_chips(p, r, self.mine, name, axis)

    def finish(self):
        names = [n for n, _, _ in BIG]
        own = [self.halves[n] for n in names]
        return dict(zip(names, zip(own, _share_with_sibling(own))))


def kernel(x, meta_tokens, ffn1_norm, ffn1_w_in, ffn1_w_out, mix_norm, w_in, b_forget, attn_sinks, w_branch_a, w_branch_b, w_out, ffn2_norm, ffn2_w_in, ffn2_w_out, final_norm, loss_target, m_meta_tokens, m_ffn1_norm, m_ffn1_w_in, m_ffn1_w_out, m_mix_norm, m_w_in, m_b_forget, m_attn_sinks, m_w_branch_a, m_w_branch_b, m_w_out, m_ffn2_norm, m_ffn2_w_in, m_ffn2_w_out, m_final_norm, v_meta_tokens, v_ffn1_norm, v_ffn1_w_in, v_ffn1_w_out, v_mix_norm, v_w_in, v_b_forget, v_attn_sinks, v_w_branch_a, v_w_branch_b, v_w_out, v_ffn2_norm, v_ffn2_w_in, v_ffn2_w_out, v_final_norm):
    given = dict(locals())
    names = ["meta_tokens", "ffn1_norm", "ffn1_w_in", "ffn1_w_out", "mix_norm", "w_in", "b_forget", "attn_sinks",
             "w_branch_a", "w_branch_b", "w_out", "ffn2_norm", "ffn2_w_in", "ffn2_w_out", "final_norm"]
    big_names = [n for n, _, _ in BIG]
    cx, cy, cc = _coords()
    c_arr = cc.reshape(1).astype(jnp.int32)
    mine_arr = (2 * cx + cy).reshape(1).astype(jnp.int32)

    comm = _Comm({n: given[n][0].astype(BF16) for n in big_names}, c_arr, mine_arr)
    table = [comm.by_name[n] + (BF16, True) for n in GATHER_FIRST] + [("meta_tokens", (N_META, D_MODEL), 1, F32, False)]
    first = _gather_exchange([comm.shards[n] for n in GATHER_FIRST] + [meta_tokens], table)
    w1i, w1o, meta_full = _run_exchange(first, "gather_first")
    norms = (ffn1_norm, mix_norm, ffn2_norm, final_norm.reshape(1, D_MODEL))
    loss, grad_x, small, big = _local_step(x, loss_target, meta_full, norms, b_forget, attn_sinks, (w1i, w1o), comm)

    swap = comm.swap("ffn1", dict(ffn1_w_in=big["ffn1_w_in"], ffn1_w_out=big["ffn1_w_out"]))
    last = comm.scatter("ffn1", _run_exchange(swap, "exchange_halves"))
    comm.received("ffn1", _run_exchange(last, "scatter_chip_sums"))
    grad_halves = comm.finish()
    grads = {}

    pad_lanes = lambda a: jnp.concatenate([a, jnp.zeros((1, LANES - a.shape[1]), F32)], axis=1)
    buf = jnp.concatenate([
        small["meta_tokens"].reshape(128, LANES),
        small["ffn1_norm"].reshape(8, LANES), small["mix_norm"].reshape(8, LANES),
        small["ffn2_norm"].reshape(8, LANES), small["final_norm"].reshape(8, LANES),
        loss, pad_lanes(small["b_forget"]), pad_lanes(small["attn_sinks"]),
        jnp.zeros((SMALL_ROWS - 163, LANES), F32)], axis=0)
    red = _all_reduce_small(buf)
    meta_cols = red[:128].reshape(N_META, D_MODEL)
    grads["meta_tokens"] = lax.dynamic_slice_in_dim(meta_cols, (2 * cx + cy) * (D_MODEL // N_CHIPS),
                                                    D_MODEL // N_CHIPS, axis=1)
    grads["ffn1_norm"] = red[128:136].reshape(1, D_MODEL)
    grads["mix_norm"] = red[136:144].reshape(1, D_MODEL)
    grads["ffn2_norm"] = red[144:152].reshape(1, D_MODEL)
    grads["final_norm"] = red[152:160].reshape(1, D_MODEL)
    loss_out = red[160, 0]
    grads["b_forget"] = red[161:162, :B_HEADS]
    grads["attn_sinks"] = red[162:163, :A_HEADS]

    out_g, out_d, out_m, out_v = [], [], [], []
    for n in names:
        w_full = given[n]
        shape = w_full.shape
        two_d = (lambda a: a.reshape(shape[-2], shape[-1])) if len(shape) >= 2 else (lambda a: a.reshape(1, shape[0]))
        if n == STACKED:
            own, other = grad_halves[n]
            g_t = jnp.concatenate([jnp.where(cc == 0, own, other), jnp.where(cc == 0, other, own)], axis=0).T
            tiles = lambda a: a.reshape(shape[-1], shape[-2] // LANES, LANES)
            untile = lambda a: a.reshape(shape[-1], shape[-2]).T
            d2, m2, v2 = [untile(a) for a in _adamw(tiles(two_d(w_full).T), tiles(g_t), tiles(two_d(given["m_" + n]).T),
                                                     tiles(two_d(given["v_" + n]).T))]
            g2 = g_t.T
        elif n in grad_halves:
            own, other = grad_halves[n]
            g2, d2, m2, v2 = _adamw_halves(two_d(w_full), own, other, two_d(given["m_" + n]),
                                           two_d(given["v_" + n]), c_arr, n)
        else:
            g2 = two_d(grads[n])
            d2, m2, v2 = _adamw(two_d(w_full), g2, two_d(given["m_" + n]), two_d(given["v_" + n]))
        out_g.append(g2.reshape(shape))
        out_d.append(d2.reshape(shape))
        out_m.append(m2.reshape(shape))
        out_v.append(v2.reshape(shape))
    return (loss_out, grad_x, *out_g, *out_d, *out_m, *out_v)
```

```python
import jax
import jax.numpy as jnp
from jax import lax
from jax.experimental import pallas as pl
from jax.experimental.pallas import tpu as pltpu

F32 = jnp.float32
BF16 = jnp.bfloat16

D_MODEL = 1024
N_META = 16
BLOCK = 128
LANES = 128
PREFIX = BLOCK
N_PAD = PREFIX - N_META
HEAD_DIM = 64
A_HEADS = 8
A_KV_HEADS = 2
A_GROUP = 4
B_HEADS = 8
B_PAIRS = B_HEADS // 2
A_WIDTH = A_HEADS * HEAD_DIM
A_KV_WIDTH = A_KV_HEADS * HEAD_DIM
B_WIDTH = B_HEADS * HEAD_DIM
W_IN_COLS = A_WIDTH + 2 * A_KV_WIDTH + 3 * B_WIDTH + B_HEADS + 2 * D_MODEL
SRC_KA = A_WIDTH
SRC_VA = SRC_KA + A_KV_WIDTH
SRC_QB = SRC_VA + A_KV_WIDTH
SRC_KB = SRC_QB + B_WIDTH
SRC_VB = SRC_KB + B_WIDTH
SRC_F = SRC_VB + B_WIDTH
SRC_GA = SRC_F + B_HEADS
SRC_GB = SRC_GA + D_MODEL
A_PAD_WIDTH = A_HEADS * LANES
B_PAD_WIDTH = B_HEADS * LANES
F_COLS = LANES
OFF_QA = 0
OFF_KA = SRC_KA
OFF_VA = SRC_VA
OFF_QB = SRC_QB
OFF_KB = SRC_KB
OFF_VB = SRC_VB
OFF_GA = SRC_F
OFF_GB = OFF_GA + D_MODEL
OFF_F = OFF_GB + D_MODEL
P_COLS = OFF_F + F_COLS
EPS = 1e-6
NEG = -1e30
SCALE = HEAD_DIM ** -0.5
KEY_BLOCKS = 4

ADAM_LR = 0.001
ADAM_B1 = 0.9
ADAM_B2 = 0.999
ADAM_EPS = 1e-08
ADAM_WD = 0.01
ADAM_STEP = 10

N_CHIPS = 4
N_DEV = 8
VMEM_LIMIT = 56 * 1024 * 1024

NT_DIMS = (((1,), (1,)), ((), ()))
TN_DIMS = (((0,), (0,)), ((), ()))
MESH_ID = pl.DeviceIdType.MESH
HBM_SPEC = pl.BlockSpec(memory_space=pltpu.HBM)
VMEM_SPEC = pl.BlockSpec(memory_space=pltpu.VMEM)


def _tile(n, target, mult=16):
    best = None
    for t in range(mult, min(n, target) + 1, mult):
        if n % t == 0:
            best = t
    return best if best is not None else n


def _cparams(sem):
    return pltpu.CompilerParams(dimension_semantics=sem, vmem_limit_bytes=VMEM_LIMIT)


def _rms_scale(h):
    return lax.rsqrt(jnp.mean(h * h, axis=-1, keepdims=True) + EPS)


def _rms_bwd(dn, h, w):
    r = _rms_scale(h)
    dw = jnp.sum(dn * (h * r), axis=0, keepdims=True)
    z = dn * w
    dh = r * z - h * ((r * r * r) * jnp.mean(z * h, axis=-1, keepdims=True))
    return dh, dw


def _ffn_fwd(h, norm_w, w_in, w_out, exchange=None):
    t, d = h.shape
    f = w_out.shape[0]
    tm = _tile(t, 272)
    tc = _tile(f, 256, 128)
    nj = f // tc
    ni = t // tm
    n_x = len(exchange.ins) if exchange else 0

    def body(*refs):
        h_ref, nw_ref, wi_ref, wo_ref = refs[:4]
        hout_ref, g_ref, u_ref = refs[4 + n_x:7 + n_x]
        a_scr = refs[7 + 2 * n_x]
        i = pl.program_id(0)
        if exchange:
            _host_exchange(exchange, refs[4:4 + n_x], refs[7 + n_x:7 + 2 * n_x], refs[8 + 2 * n_x:],
                           i == 0, i == ni - 2, i == ni - 1)
        hh = h_ref[...]
        n = ((hh * _rms_scale(hh)) * nw_ref[...]).astype(BF16)
        for j in range(nj):
            cols = slice(j * tc, (j + 1) * tc)
            g = jnp.dot(n, wi_ref[:, j * tc:(j + 1) * tc], preferred_element_type=F32)
            u = jnp.dot(n, wi_ref[:, f + j * tc:f + (j + 1) * tc], preferred_element_type=F32)
            g_ref[:, cols] = g
            u_ref[:, cols] = u
            a_scr[:, cols] = ((g * jax.nn.sigmoid(g)) * u).astype(BF16)
        hout_ref[...] = hh + 0.5 * jnp.dot(a_scr[...], wo_ref[...], preferred_element_type=F32)

    resident = lambda a: pl.BlockSpec(a.shape, lambda i: (0, 0), pipeline_mode=pl.Buffered(1))
    row = lambda w: pl.BlockSpec((tm, w), lambda i: (i, 0))
    outs = pl.pallas_call(
        body,
        name="ffn_fwd",
        grid=(ni,),
        in_specs=[row(d), pl.BlockSpec((1, d), lambda i: (0, 0)), resident(w_in), resident(w_out)] + [HBM_SPEC] * n_x,
        out_specs=[row(d), row(f), row(f)] + [HBM_SPEC] * n_x,
        out_shape=[
            jax.ShapeDtypeStruct((t, d), F32),
            jax.ShapeDtypeStruct((t, f), F32),
            jax.ShapeDtypeStruct((t, f), F32),
        ] + (exchange.out_shape if exchange else []),
        scratch_shapes=[pltpu.VMEM((tm, f), BF16)] + (exchange.scratch if exchange else []),
        compiler_params=_cparams(("arbitrary",) if exchange else ("parallel",)),
    )(h, norm_w, w_in, w_out, *(exchange.ins if exchange else []))
    return outs[:3], outs[3:]


def _ffn_bwd(dh_out, h, norm_w, g, u, w_in, w_out, exchange=None):
    t, d = h.shape
    f = w_out.shape[0]
    tm = _tile(t, 272)
    tc = _tile(f, 256, 128)
    nj = f // tc
    ni = t // tm
    n_x = len(exchange.ins) if exchange else 0

    def body(*refs):
        dho_ref, h_ref, nw_ref, g_ref, u_ref, wi_ref, wo_ref = refs[:7]
        dhin_ref, n_ref, a_ref, dgu_ref, df_ref, dnw_ref = refs[7 + n_x:13 + n_x]
        i = pl.program_id(0)
        if exchange:
            _host_exchange(exchange, refs[7:7 + n_x], refs[13 + n_x:13 + 2 * n_x], refs[13 + 2 * n_x:],
                           i == 0, i == ni - 1, i == ni - 1)
        hh = h_ref[...]
        nw = nw_ref[...]
        n_ref[...] = ((hh * _rms_scale(hh)) * nw).astype(BF16)
        dho = dho_ref[...]
        df = (0.5 * dho).astype(BF16)
        df_ref[...] = df
        for j in range(nj):
            cols = slice(j * tc, (j + 1) * tc)
            da = lax.dot_general(df, wo_ref[cols, :], NT_DIMS, preferred_element_type=F32)
            gg = g_ref[:, cols]
            uu = u_ref[:, cols]
            sig = jax.nn.sigmoid(gg)
            sl = gg * sig
            a_ref[:, cols] = (sl * uu).astype(BF16)
            dgu_ref[0, :, cols] = ((da * uu) * (sig * (1.0 + gg * (1.0 - sig)))).astype(BF16)
            dgu_ref[1, :, cols] = (da * sl).astype(BF16)
        dn = (lax.dot_general(dgu_ref[0], wi_ref[:, :f], NT_DIMS, preferred_element_type=F32)
              + lax.dot_general(dgu_ref[1], wi_ref[:, f:], NT_DIMS, preferred_element_type=F32))
        dh, dw = _rms_bwd(dn, hh, nw)
        dhin_ref[...] = dho + dh
        dnw_ref[0] = dw

    resident = lambda a: pl.BlockSpec(a.shape, lambda i: (0, 0), pipeline_mode=pl.Buffered(1))
    row = lambda w: pl.BlockSpec((tm, w), lambda i: (i, 0))
    outs = pl.pallas_call(
        body,
        name="ffn_bwd",
        grid=(ni,),
        in_specs=[row(d), row(d), pl.BlockSpec((1, d), lambda i: (0, 0)), row(f), row(f),
                  resident(w_in), resident(w_out)] + [HBM_SPEC] * n_x,
        out_specs=[row(d), row(d), row(f), pl.BlockSpec((2, tm, f), lambda i: (0, i, 0)), row(d),
                   pl.BlockSpec((1, 1, d), lambda i: (i, 0, 0))] + [HBM_SPEC] * n_x,
        out_shape=[
            jax.ShapeDtypeStruct((t, d), F32),
            jax.ShapeDtypeStruct((t, d), BF16),
            jax.ShapeDtypeStruct((t, f), BF16),
            jax.ShapeDtypeStruct((2, t, f), BF16),
            jax.ShapeDtypeStruct((t, d), BF16),
            jax.ShapeDtypeStruct((ni, 1, d), F32),
        ] + (exchange.out_shape if exchange else []),
        scratch_shapes=exchange.scratch if exchange else [],
        compiler_params=_cparams(("arbitrary",) if exchange else ("parallel",)),
    )(dh_out, h, norm_w, g, u, w_in, w_out, *(exchange.ins if exchange else []))
    return outs[:6], outs[6:]


def _tn_matmul(a, b, name):
    t, k = a.shape
    split = b.ndim == 3
    n = 2 * b.shape[2] if split else b.shape[1]
    tk = _tile(k, 512, 128)
    tn = _tile(b.shape[-1], 1408, 128)
    per_half = b.shape[-1] // tn

    def body(a_ref, b_ref, o_ref):
        o_ref[...] = lax.dot_general(a_ref[...], b_ref[...], TN_DIMS, preferred_element_type=F32)

    if split:
        b_spec = pl.BlockSpec((None, t, tn), lambda i, j: (j // per_half, 0, j % per_half))
    else:
        b_spec = pl.BlockSpec((t, tn), lambda i, j: (0, j))
    return pl.pallas_call(
        body,
        name=name,
        grid=(k // tk, n // tn),
        in_specs=[pl.BlockSpec((t, tk), lambda i, j: (0, i)), b_spec],
        out_specs=pl.BlockSpec((tk, tn), lambda i, j: (i, j)),
        out_shape=jax.ShapeDtypeStruct((k, n), F32),
        compiler_params=_cparams(("parallel", "parallel")),
    )(a, b)


A_SLOT = lambda h: h // A_GROUP
B_SLOT = lambda h: h % 2

PROJ_PARTS = (
    (OFF_QA, A_WIDTH, A_PAD_WIDTH, True, A_SLOT), (OFF_KA, A_KV_WIDTH, A_KV_WIDTH, True, None),
    (OFF_VA, A_KV_WIDTH, A_KV_WIDTH, True, None), (OFF_QB, B_WIDTH, B_WIDTH, True, None),
    (OFF_KB, B_WIDTH, B_PAD_WIDTH, True, B_SLOT), (OFF_VB, B_WIDTH, B_PAD_WIDTH, True, B_SLOT),
    (OFF_GA, D_MODEL, D_MODEL, False, None), (OFF_GB, D_MODEL, D_MODEL, False, None), (OFF_F, F_COLS, F_COLS, False, None),
)


def _head_tile(pair, head, slot):
    lane_slot = lax.broadcasted_iota(jnp.int32, pair.shape, 1) // HEAD_DIM
    moved = pair if head % 2 == slot else pltpu.roll(pair, HEAD_DIM, 1)
    return jnp.where(lane_slot == slot, moved, 0.0)


def _proj_fwd(h, norm_w, w_p):
    t, d = h.shape
    tm = _tile(t, 272)

    def body(h_ref, nw_ref, w_ref, u_ref, *part_refs):
        hh = h_ref[...]
        un = ((hh * _rms_scale(hh)) * nw_ref[...]).astype(BF16)
        u_ref[...] = un
        for (off, width, _, _, slot), p_ref in zip(PROJ_PARTS, part_refs):
            if slot is None:
                p_ref[...] = jnp.dot(un, w_ref[:, off:off + width], preferred_element_type=F32).astype(p_ref.dtype)
                continue
            for pair in range(width // LANES):
                x = jnp.dot(un, w_ref[:, off + pair * LANES:off + (pair + 1) * LANES], preferred_element_type=F32)
                for head in (2 * pair, 2 * pair + 1):
                    p_ref[:, head * LANES:(head + 1) * LANES] = _head_tile(x, head, slot(head)).astype(p_ref.dtype)

    row = lambda w: pl.BlockSpec((tm, w), lambda i: (i, 0))
    return pl.pallas_call(
        body,
        name="proj_fwd",
        grid=(t // tm,),
        in_specs=[row(d), pl.BlockSpec((1, d), lambda i: (0, 0)),
                  pl.BlockSpec(w_p.shape, lambda i: (0, 0), pipeline_mode=pl.Buffered(1))],
        out_specs=[row(d)] + [row(width) for _, _, width, _, _ in PROJ_PARTS],
        out_shape=[jax.ShapeDtypeStruct((t, d), BF16)]
        + [jax.ShapeDtypeStruct((t, width), BF16 if is_bf else F32) for _, _, width, is_bf, _ in PROJ_PARTS],
        compiler_params=_cparams(("parallel",)),
    )(h, norm_w, w_p)


def _proj_bwd(dh_out, h, norm_w, dproj, w_p, exchange=None):
    t, d = h.shape
    n = w_p.shape[1]
    tm = _tile(t, 272)
    ni = t // tm
    n_x = len(exchange.ins) if exchange else 0

    def body(*refs):
        dho_ref, h_ref, nw_ref, dp_ref, w_ref = refs[:5]
        dhin_ref, dnw_ref = refs[5 + n_x:7 + n_x]
        if exchange:
            i = pl.program_id(0)
            _host_exchange(exchange, refs[5:5 + n_x], refs[7 + n_x:7 + 2 * n_x], refs[7 + 2 * n_x:],
                           i == 0, i == ni - 1, i == ni - 1)
        dn = lax.dot_general(dp_ref[...], w_ref[...], NT_DIMS, preferred_element_type=F32)
        dh, dw = _rms_bwd(dn, h_ref[...], nw_ref[...])
        dhin_ref[...] = dho_ref[...] + dh
        dnw_ref[0] = dw

    row = lambda w: pl.BlockSpec((tm, w), lambda i: (i, 0))
    outs = pl.pallas_call(
        body,
        name="proj_bwd",
        grid=(ni,),
        in_specs=[row(d), row(d), pl.BlockSpec((1, d), lambda i: (0, 0)), row(n),
                  pl.BlockSpec(w_p.shape, lambda i: (0, 0), pipeline_mode=pl.Buffered(1))] + [HBM_SPEC] * n_x,
        out_specs=[row(d), pl.BlockSpec((1, 1, d), lambda i: (i, 0, 0))] + [HBM_SPEC] * n_x,
        out_shape=[jax.ShapeDtypeStruct((t, d), F32), jax.ShapeDtypeStruct((ni, 1, d), F32)]
        + (exchange.out_shape if exchange else []),
        scratch_shapes=exchange.scratch if exchange else [],
        compiler_params=_cparams(("arbitrary",) if exchange else ("parallel",)),
    )(dh_out, h, norm_w, dproj, w_p, *(exchange.ins if exchange else []))
    return outs[:2], outs[2:]


def _merge_fwd(h, oa, ob, ga, gb, wa, wb, wo):
    t, d = h.shape
    tm = _tile(t, 544)

    def body(h_ref, oa_ref, ob_ref, ga_ref, gb_ref, wa_ref, wb_ref, wo_ref, hout_ref, mix_ref):
        ya = jnp.dot(oa_ref[...], wa_ref[...], preferred_element_type=F32)
        yb = jnp.dot(ob_ref[...], wb_ref[...], preferred_element_type=F32)
        mixed = (jax.nn.sigmoid(ga_ref[...]) * ya + jax.nn.sigmoid(gb_ref[...]) * yb).astype(BF16)
        mix_ref[...] = mixed
        hout_ref[...] = h_ref[...] + jnp.dot(mixed, wo_ref[...], preferred_element_type=F32)

    row = lambda w: pl.BlockSpec((tm, w), lambda i: (i, 0))
    full = lambda a: pl.BlockSpec(a.shape, lambda i: (0, 0))
    return pl.pallas_call(
        body,
        name="merge_fwd",
        grid=(t // tm,),
        in_specs=[row(d), row(oa.shape[1]), row(ob.shape[1]), row(d), row(d), full(wa), full(wb), full(wo)],
        out_specs=[row(d), row(d)],
        out_shape=[jax.ShapeDtypeStruct((t, d), F32), jax.ShapeDtypeStruct((t, d), BF16)],
        compiler_params=_cparams(("parallel",)),
    )(h, oa, ob, ga, gb, wa, wb, wo)


def _merge_bwd(dh, oa, ob, ga, gb, wa, wb, wo, exchange=None):
    t, d = dh.shape
    tm = _tile(t, 544)
    ni = t // tm
    n_x = len(exchange.ins) if exchange else 0

    def body(*refs):
        dh_ref, oa_ref, ob_ref, ga_ref, gb_ref, wa_ref, wb_ref, wo_ref = refs[:8]
        dya_ref, dyb_ref, doa_ref, dob_ref, dga_ref, dgb_ref, dhb_ref = refs[8 + n_x:15 + n_x]
        if exchange:
            i = pl.program_id(0)
            _host_exchange(exchange, refs[8:8 + n_x], refs[15 + n_x:15 + 2 * n_x], refs[15 + 2 * n_x:],
                           i == 0, i == ni - 1, i == ni - 1)
        dhb = dh_ref[...].astype(BF16)
        dhb_ref[...] = dhb
        dmix = lax.dot_general(dhb, wo_ref[...], NT_DIMS, preferred_element_type=F32)
        for o_ref, g_ref, w_ref, dy_ref, do_ref, dg_ref in (
                (oa_ref, ga_ref, wa_ref, dya_ref, doa_ref, dga_ref),
                (ob_ref, gb_ref, wb_ref, dyb_ref, dob_ref, dgb_ref)):
            y = jnp.dot(o_ref[...], w_ref[...], preferred_element_type=F32)
            s = jax.nn.sigmoid(g_ref[...])
            dy = (dmix * s).astype(BF16)
            dy_ref[...] = dy
            dg_ref[...] = ((dmix * y) * (s * (1.0 - s))).astype(BF16)
            do_ref[...] = lax.dot_general(dy, w_ref[...], NT_DIMS, preferred_element_type=F32).astype(BF16)

    row = lambda w: pl.BlockSpec((tm, w), lambda i: (i, 0))
    full = lambda a: pl.BlockSpec(a.shape, lambda i: (0, 0))
    wa_w, wb_w = oa.shape[1], ob.shape[1]
    outs = pl.pallas_call(
        body,
        name="merge_bwd",
        grid=(ni,),
        in_specs=[row(d), row(wa_w), row(wb_w), row(d), row(d), full(wa), full(wb), full(wo)] + [HBM_SPEC] * n_x,
        out_specs=[row(d), row(d), row(wa_w), row(wb_w), row(d), row(d), row(d)] + [HBM_SPEC] * n_x,
        out_shape=[
            jax.ShapeDtypeStruct((t, d), BF16), jax.ShapeDtypeStruct((t, d), BF16),
            jax.ShapeDtypeStruct((t, wa_w), BF16), jax.ShapeDtypeStruct((t, wb_w), BF16),
            jax.ShapeDtypeStruct((t, d), BF16), jax.ShapeDtypeStruct((t, d), BF16),
            jax.ShapeDtypeStruct((t, d), BF16),
        ] + (exchange.out_shape if exchange else []),
        scratch_shapes=exchange.scratch if exchange else [],
        compiler_params=_cparams(("arbitrary",) if exchange else ("parallel",)),
    )(dh, oa, ob, ga, gb, wa, wb, wo, *(exchange.ins if exchange else []))
    return outs[:7], outs[7:]


def _tri_dot(tri, x):
    hi = x.astype(BF16)
    r1 = x - hi.astype(F32)
    mid = r1.astype(BF16)
    lo = (r1 - mid.astype(F32)).astype(BF16)
    return (jnp.dot(tri, hi, preferred_element_type=F32)
            + jnp.dot(tri, mid, preferred_element_type=F32)
            + jnp.dot(tri, lo, preferred_element_type=F32))


def _forget_cumsum(f_logit, b_pad, nb):
    t, w = f_logit.shape
    bsz = t // (nb * BLOCK)

    def body(f_ref, b_ref, c_ref, carry):
        n = pl.program_id(1)

        @pl.when(n == 0)
        def _():
            carry[...] = jnp.zeros_like(carry)

        x = jax.nn.log_sigmoid(f_ref[...] + b_ref[...])
        rows = lax.broadcasted_iota(jnp.int32, (BLOCK, BLOCK), 0)
        cols = lax.broadcasted_iota(jnp.int32, (BLOCK, BLOCK), 1)
        tri = (cols <= rows).astype(BF16)
        c = _tri_dot(tri, x) + carry[...]
        c_ref[...] = c
        carry[...] = c[BLOCK - 1:BLOCK, :]

    return pl.pallas_call(
        body,
        name="forget_cumsum",
        grid=(bsz, nb),
        in_specs=[pl.BlockSpec((BLOCK, w), lambda b, n: (b * nb + n, 0)),
                  pl.BlockSpec((1, w), lambda b, n: (0, 0))],
        out_specs=pl.BlockSpec((BLOCK, w), lambda b, n: (b * nb + n, 0)),
        out_shape=jax.ShapeDtypeStruct((t, w), F32),
        scratch_shapes=[pltpu.VMEM((1, w), F32)],
        compiler_params=_cparams(("parallel", "arbitrary")),
    )(f_logit, b_pad)


def _forget_cumsum_bwd(dc, f_logit, b_pad, nb):
    t, w = f_logit.shape
    bsz = t // (nb * BLOCK)

    def body(dc_ref, f_ref, b_ref, df_ref, db_ref, carry):
        n = pl.program_id(1)

        @pl.when(n == 0)
        def _():
            carry[...] = jnp.zeros_like(carry)
            db_ref[...] = jnp.zeros_like(db_ref)

        rows = lax.broadcasted_iota(jnp.int32, (BLOCK, BLOCK), 0)
        cols = lax.broadcasted_iota(jnp.int32, (BLOCK, BLOCK), 1)
        tri = (cols >= rows).astype(BF16)
        dlf = _tri_dot(tri, dc_ref[...]) + carry[...]
        carry[...] = dlf[0:1, :]
        df = dlf * jax.nn.sigmoid(-(f_ref[...] + b_ref[...]))
        df_ref[...] = df.astype(BF16)
        db_ref[0] += jnp.sum(df, axis=0, keepdims=True)

    rev = lambda b, n: (b * nb + (nb - 1 - n), 0)
    return pl.pallas_call(
        body,
        name="forget_cumsum_bwd",
        grid=(bsz, nb),
        in_specs=[pl.BlockSpec((BLOCK, w), rev),
                  pl.BlockSpec((BLOCK, w), rev),
                  pl.BlockSpec((1, w), lambda b, n: (0, 0))],
        out_specs=[pl.BlockSpec((BLOCK, w), rev),
                   pl.BlockSpec((1, 1, w), lambda b, n: (b, 0, 0))],
        out_shape=[jax.ShapeDtypeStruct((t, w), BF16), jax.ShapeDtypeStruct((bsz, 1, w), F32)],
        scratch_shapes=[pltpu.VMEM((1, w), F32)],
        compiler_params=_cparams(("parallel", "arbitrary")),
    )(dc, f_logit, b_pad)


GROUP_ROWS = A_GROUP * BLOCK


def _stack_heads(ref, g):
    return jnp.concatenate([ref[:, (A_GROUP * g + i) * LANES:(A_GROUP * g + i + 1) * LANES] for i in range(A_GROUP)],
                           axis=0)


def _unstack_heads(ref, g, x):
    for i in range(A_GROUP):
        ref[:, (A_GROUP * g + i) * LANES:(A_GROUP * g + i + 1) * LANES] = x[i * BLOCK:(i + 1) * BLOCK].astype(ref.dtype)


def _swa_logits(q, keys, slope, n):
    qi = lax.broadcasted_iota(jnp.int32, (GROUP_ROWS, BLOCK), 0) & (BLOCK - 1)
    kj = lax.broadcasted_iota(jnp.int32, (GROUP_ROWS, BLOCK), 1)
    s_all = lax.dot_general(q, keys, NT_DIMS, preferred_element_type=F32) * SCALE
    out = []
    for i, (dist, ok) in enumerate((
            (n * BLOCK + qi - kj, (kj >= N_PAD) & (n * BLOCK + qi - kj >= 0)),
            (BLOCK + qi - kj, (kj > qi) & (n >= 2)),
            (qi - kj, (kj <= qi) & (n >= 1)))):
        s = s_all[:, i * BLOCK:(i + 1) * BLOCK] - slope * dist.astype(F32)
        out.append(jnp.where(ok, s, NEG))
    return out


def _three_blocks(m_ref, p_ref, c_ref):
    return jnp.concatenate([m_ref[...], p_ref[...], c_ref[...]], axis=0)


def _swa_specs(nb):
    row = lambda b, n: b * nb + n
    qspec = pl.BlockSpec((BLOCK, A_PAD_WIDTH), lambda b, n: (row(b, n), 0))
    kv_m = pl.BlockSpec((BLOCK, LANES), lambda b, n: (row(b, 0), 0))
    kv_p = pl.BlockSpec((BLOCK, LANES), lambda b, n: (row(b, jnp.maximum(n - 1, 0)), 0))
    kv_c = pl.BlockSpec((BLOCK, LANES), lambda b, n: (row(b, n), 0))
    rowspec = pl.BlockSpec((A_KV_HEADS, GROUP_ROWS, 1), lambda b, n: (0, 0, 0))
    lsespec = pl.BlockSpec((1, A_KV_HEADS, GROUP_ROWS, 1), lambda b, n: (row(b, n), 0, 0, 0))
    return qspec, kv_m, kv_p, kv_c, rowspec, lsespec


def _swa_fwd(q, k, v, sink_rows, slope_rows, nb):
    t = q.shape[0]
    bsz = t // (nb * BLOCK)

    def body(q_ref, km_ref, kp_ref, kc_ref, vm_ref, vp_ref, vc_ref, sink_ref, slope_ref, o_ref, lse_ref):
        n = pl.program_id(1)
        keys = _three_blocks(km_ref, kp_ref, kc_ref)
        values = _three_blocks(vm_ref, vp_ref, vc_ref)
        lane_group = lax.broadcasted_iota(jnp.int32, (GROUP_ROWS, LANES), 1) // HEAD_DIM
        for g in range(A_KV_HEADS):
            qq = _stack_heads(q_ref, g)
            sink = sink_ref[g]
            s_m, s_p, s_c = _swa_logits(qq, keys, slope_ref[g], n)
            m = jnp.maximum(jnp.max(jnp.maximum(jnp.maximum(s_m, s_p), s_c), axis=-1, keepdims=True), sink)
            m_wide = jnp.broadcast_to(m, (GROUP_ROWS, BLOCK))
            e_m = jnp.exp(s_m - m_wide)
            e_p = jnp.exp(s_p - m_wide)
            e_c = jnp.exp(s_c - m_wide)
            z = jnp.sum((e_m + e_p) + e_c, axis=-1, keepdims=True) + jnp.exp(sink - m)
            inv = jnp.broadcast_to(1.0 / z, (GROUP_ROWS, BLOCK))
            probs = jnp.concatenate([(e_m * inv).astype(BF16), (e_p * inv).astype(BF16), (e_c * inv).astype(BF16)],
                                    axis=1)
            o = jnp.dot(probs, values, preferred_element_type=F32)
            _unstack_heads(o_ref, g, jnp.where(lane_group == g, o, 0.0))
            lse_ref[0, g] = m + jnp.log(z)

    qspec, kv_m, kv_p, kv_c, rowspec, lsespec = _swa_specs(nb)
    return pl.pallas_call(
        body,
        name="swa_fwd",
        grid=(bsz, nb),
        in_specs=[qspec, kv_m, kv_p, kv_c, kv_m, kv_p, kv_c, rowspec, rowspec],
        out_specs=[qspec, lsespec],
        out_shape=[jax.ShapeDtypeStruct((t, A_PAD_WIDTH), BF16),
                   jax.ShapeDtypeStruct((t // BLOCK, A_KV_HEADS, GROUP_ROWS, 1), F32)],
        compiler_params=_cparams(("parallel", "arbitrary")),
    )(q, k, k, k, v, v, v, sink_rows, slope_rows)


def _swa_bwd(q, k, v, do, lse, sink_rows, slope_rows, nb):
    t = q.shape[0]
    l = nb * BLOCK
    bsz = t // l

    def body(q_ref, km_ref, kp_ref, kc_ref, vm_ref, vp_ref, vc_ref, do_ref, lse_ref, sink_ref, slope_ref,
             dq_ref, dk_ref, dv_ref, dsink_ref, dk_acc, dv_acc):
        n = pl.program_id(1)

        @pl.when(n == 0)
        def _():
            dk_acc[...] = jnp.zeros_like(dk_acc)
            dv_acc[...] = jnp.zeros_like(dv_acc)
            dsink_ref[...] = jnp.zeros_like(dsink_ref)

        keys = _three_blocks(km_ref, kp_ref, kc_ref)
        values = _three_blocks(vm_ref, vp_ref, vc_ref)
        first_half = lax.broadcasted_iota(jnp.int32, (BLOCK, LANES), 1) < HEAD_DIM
        prev = jnp.maximum(n - 1, 0)
        for g in range(A_KV_HEADS):
            qq = _stack_heads(q_ref, g)
            dob = _stack_heads(do_ref, g)
            lse = lse_ref[0, g]
            lse_wide = jnp.broadcast_to(lse, (GROUP_ROWS, BLOCK))
            probs = [jnp.exp(s - lse_wide) for s in _swa_logits(qq, keys, slope_ref[g], n)]
            dp_all = lax.dot_general(dob, values, NT_DIMS, preferred_element_type=F32)
            dps = [dp_all[:, i * BLOCK:(i + 1) * BLOCK] for i in range(3)]
            delta = jnp.sum((probs[0] * dps[0] + probs[1] * dps[1]) + probs[2] * dps[2], axis=-1, keepdims=True)
            delta_wide = jnp.broadcast_to(delta, (GROUP_ROWS, BLOCK))
            ds = jnp.concatenate([(p * (dp - delta_wide)).astype(BF16) for p, dp in zip(probs, dps)], axis=1)
            pb = jnp.concatenate([p.astype(BF16) for p in probs], axis=1)
            dq = jnp.dot(ds, keys, preferred_element_type=F32) * SCALE
            dk_all = lax.dot_general(ds, qq, TN_DIMS, preferred_element_type=F32) * SCALE
            dv_all = lax.dot_general(pb, dob, TN_DIMS, preferred_element_type=F32)
            for i, start in enumerate((0, prev * BLOCK, n * BLOCK)):
                rows = pl.ds(pl.multiple_of(start, BLOCK), BLOCK)
                dk_acc[rows, :] += dk_all[i * BLOCK:(i + 1) * BLOCK]
                dv_acc[rows, :] += dv_all[i * BLOCK:(i + 1) * BLOCK]
            for pair in range(A_GROUP // 2):
                even = dq[2 * pair * BLOCK:(2 * pair + 1) * BLOCK]
                odd = dq[(2 * pair + 1) * BLOCK:(2 * pair + 2) * BLOCK]
                left = even if g == 0 else pltpu.roll(even, HEAD_DIM, 1)
                right = pltpu.roll(odd, HEAD_DIM, 1) if g == 0 else odd
                tile = (A_GROUP // 2) * g + pair
                dq_ref[:, tile * LANES:(tile + 1) * LANES] = jnp.where(first_half, left, right).astype(BF16)
            dsink_ref[0, g] += -(jnp.exp(sink_ref[g] - lse) * delta)

        @pl.when(n == nb - 1)
        def _():
            dk_ref[...] = dk_acc[...].astype(BF16)
            dv_ref[...] = dv_acc[...].astype(BF16)

    qspec, kv_m, kv_p, kv_c, rowspec, lsespec = _swa_specs(nb)
    kv_all = pl.BlockSpec((l, LANES), lambda b, n: (b, 0))
    return pl.pallas_call(
        body,
        name="swa_bwd",
        grid=(bsz, nb),
        in_specs=[qspec, kv_m, kv_p, kv_c, kv_m, kv_p, kv_c, qspec, lsespec, rowspec, rowspec],
        out_specs=[pl.BlockSpec((BLOCK, A_WIDTH), lambda b, n: (b * nb + n, 0)), kv_all, kv_all,
                   pl.BlockSpec((1, A_KV_HEADS, GROUP_ROWS, 1), lambda b, n: (b, 0, 0, 0))],
        out_shape=[jax.ShapeDtypeStruct((t, A_WIDTH), BF16),
                   jax.ShapeDtypeStruct((t, LANES), BF16),
                   jax.ShapeDtypeStruct((t, LANES), BF16),
                   jax.ShapeDtypeStruct((bsz, A_KV_HEADS, GROUP_ROWS, 1), F32)],
        scratch_shapes=[pltpu.VMEM((l, LANES), F32), pltpu.VMEM((l, LANES), F32)],
        compiler_params=_cparams(("parallel", "arbitrary")),
    )(q, k, k, k, v, v, v, do, lse, sink_rows, slope_rows)


CHUNK = KEY_BLOCKS * BLOCK


def _fox_chunk(qb, ci):
    sb = jnp.maximum(jnp.minimum(KEY_BLOCKS * ci, qb + 1 - KEY_BLOCKS), 0)
    lo = jnp.maximum(ci * CHUNK, N_PAD)
    return sb, lo, pl.ds(pl.multiple_of(sb * BLOCK, BLOCK), CHUNK)


def _fox_logits(s_ref, cr_ref, e, j, sb, lo, qb):
    lane = lax.broadcasted_iota(jnp.int32, (BLOCK, BLOCK), 1)
    ahead = lane - lax.broadcasted_iota(jnp.int32, (BLOCK, BLOCK), 0)
    first = (sb + j) * BLOCK
    s = s_ref[e, :, j * BLOCK:(j + 1) * BLOCK] - cr_ref[e, sb + j]
    return jnp.where((ahead <= qb * BLOCK - first) & (lane >= lo - first), s, NEG)


FOX_PAIRS = 4
FOX_HEADS = 2 * FOX_PAIRS
FOX_STEPS = B_PAIRS // FOX_PAIRS


def _fox_specs(nb):
    l = nb * BLOCK
    q_spec = pl.BlockSpec((BLOCK, FOX_PAIRS * LANES), lambda b, p, i: (b * nb + i, p))
    kv_spec = pl.BlockSpec((l, FOX_HEADS * LANES), lambda b, p, i: (b, p))
    cc_spec = pl.BlockSpec((FOX_HEADS, BLOCK, 1), lambda b, p, i: (b * FOX_STEPS + p, i, 0))
    cr_spec = pl.BlockSpec((FOX_HEADS, nb, 1, BLOCK), lambda b, p, i: (b * FOX_STEPS + p, 0, 0, 0))
    return q_spec, kv_spec, cc_spec, cr_spec


def _fox_fwd(q, k, v, c_row, nb, exchange=None):
    t = q.shape[0]
    bsz = t // (nb * BLOCK)
    assert nb >= KEY_BLOCKS

    n_x = len(exchange.ins) if exchange else 0

    def body(*refs):
        q_ref, k_ref, v_ref, cr_ref = refs[:4]
        o_ref, ox_ref, lse_ref = refs[4 + n_x:7 + n_x]
        s_scr, hi_scr, lo_scr = refs[7 + 2 * n_x:10 + 2 * n_x]
        qb = pl.program_id(2)
        if exchange:
            first = (pl.program_id(0) == 0) & (pl.program_id(1) == 0)
            last = (pl.program_id(0) == bsz - 1) & (pl.program_id(1) == FOX_STEPS - 1)
            _host_exchange(exchange, refs[4:4 + n_x], refs[7 + n_x:7 + 2 * n_x], refs[10 + 2 * n_x:],
                           first & (qb == 0), last & (qb == 0), last & (qb == nb - 1))
        qs = [q_ref[:, a * LANES:(a + 1) * LANES] * SCALE for a in range(FOX_PAIRS)]
        first_half = lax.broadcasted_iota(jnp.int32, (BLOCK, LANES), 1) < HEAD_DIM

        def step(ci, carry):
            stats, accs = carry[:2 * FOX_HEADS], carry[2 * FOX_HEADS:]
            sb, lo, krows = _fox_chunk(qb, ci)
            new_stats, new_accs = [], []
            for a in range(FOX_PAIRS):
                alphas = []
                pv = jnp.zeros((BLOCK, LANES), F32)
                pv_lo = jnp.zeros((BLOCK, LANES), F32)
                for e in (2 * a, 2 * a + 1):
                    m, z = stats[2 * e], stats[2 * e + 1]
                    tile = slice(e * LANES, (e + 1) * LANES)
                    s_scr[e] = lax.dot_general(qs[a], k_ref[krows, tile], NT_DIMS, preferred_element_type=F32)
                    top = None
                    for j in range(KEY_BLOCKS):
                        s = _fox_logits(s_scr, cr_ref, e, j, sb, lo, qb)
                        s_scr[e, :, j * BLOCK:(j + 1) * BLOCK] = s
                        top = s if top is None else jnp.maximum(top, s)
                    m_new = jnp.maximum(m, jnp.max(top, axis=-1, keepdims=True))
                    alpha = jnp.exp(m - m_new)
                    m_wide = jnp.broadcast_to(m_new, (BLOCK, BLOCK))
                    total = None
                    for j in range(KEY_BLOCKS):
                        cols = slice(j * BLOCK, (j + 1) * BLOCK)
                        p = jnp.exp(s_scr[e, :, cols] - m_wide)
                        total = p if total is None else total + p
                        hi = p.astype(BF16)
                        hi_scr[e, :, cols] = hi
                        lo_scr[e, :, cols] = (p - hi.astype(F32)).astype(BF16)
                    z = alpha * z + jnp.sum(total, axis=-1, keepdims=True)
                    vv = v_ref[krows, tile]
                    pv = pv + jnp.dot(hi_scr[e], vv, preferred_element_type=F32)
                    pv_lo = pv_lo + jnp.dot(lo_scr[e], vv, preferred_element_type=F32)
                    new_stats += [m_new, z]
                    alphas.append(alpha)
                alpha = jnp.where(first_half, alphas[0], alphas[1])
                new_accs += [alpha * accs[2 * a] + pv, alpha * accs[2 * a + 1] + pv_lo]
            return (*new_stats, *new_accs)

        col = lambda val: jnp.full((BLOCK, 1), val, F32)
        done = lax.fori_loop(
            0, (qb + KEY_BLOCKS) // KEY_BLOCKS, step,
            (col(NEG), col(0.0)) * FOX_HEADS + (jnp.zeros((BLOCK, LANES), F32),) * (2 * FOX_PAIRS))
        for a in range(FOX_PAIRS):
            m0, z0, m1, z1 = done[4 * a:4 * a + 4]
            acc, acc_lo = done[2 * FOX_HEADS + 2 * a:2 * FOX_HEADS + 2 * a + 2]
            inv = 1.0 / jnp.where(first_half, z0, z1)
            tile = slice(a * LANES, (a + 1) * LANES)
            o_ref[:, tile] = (acc * inv).astype(BF16)
            ox_ref[:, tile] = (acc + acc_lo) * inv
            lse_ref[2 * a] = m0 + jnp.log(z0)
            lse_ref[2 * a + 1] = m1 + jnp.log(z1)

    q_spec, kv_spec, cc_spec, cr_spec = _fox_specs(nb)
    outs = pl.pallas_call(
        body,
        name="fox_fwd",
        grid=(bsz, FOX_STEPS, nb),
        in_specs=[q_spec, kv_spec, kv_spec, cr_spec] + [HBM_SPEC] * n_x,
        out_specs=[q_spec, q_spec, cc_spec] + [HBM_SPEC] * n_x,
        out_shape=[jax.ShapeDtypeStruct((t, B_WIDTH), BF16), jax.ShapeDtypeStruct((t, B_WIDTH), F32),
                   jax.ShapeDtypeStruct((bsz * B_HEADS, nb * BLOCK, 1), F32)] + (exchange.out_shape if exchange else []),
        scratch_shapes=[pltpu.VMEM((FOX_HEADS, BLOCK, CHUNK), F32), pltpu.VMEM((FOX_HEADS, BLOCK, CHUNK), BF16),
                        pltpu.VMEM((FOX_HEADS, BLOCK, CHUNK), BF16)] + (exchange.scratch if exchange else []),
        compiler_params=_cparams(("arbitrary",) * 3 if exchange else ("parallel", "parallel", "arbitrary")),
    )(q, k, v, c_row, *(exchange.ins if exchange else []))
    return outs[:3], outs[3:]


def _fox_bwd(q, k, v, o_exact, do, lse, c_row, nb, exchange=None):
    t = q.shape[0]
    l = nb * BLOCK
    bsz = t // l

    n_x = len(exchange.ins) if exchange else 0

    def body(*refs):
        q_ref, k_ref, v_ref, ox_ref, do_ref, lse_ref, cr_ref = refs[:7]
        dq_ref, dk_ref, dv_ref, dc_ref = refs[7 + n_x:11 + n_x]
        dk_acc, dv_acc, s_scr, dp_scr, p_scr, ds_scr = refs[11 + 2 * n_x:17 + 2 * n_x]
        qb = pl.program_id(2)
        if exchange:
            first = (pl.program_id(0) == 0) & (pl.program_id(1) == 0)
            last = (pl.program_id(0) == bsz - 1) & (pl.program_id(1) == FOX_STEPS - 1)
            _host_exchange(exchange, refs[7:7 + n_x], refs[11 + n_x:11 + 2 * n_x], refs[17 + 2 * n_x:],
                           first & (qb == 0), last & (qb == 0), last & (qb == nb - 1))

        @pl.when(qb == 0)
        def _():
            dk_acc[...] = jnp.zeros_like(dk_acc)
            dv_acc[...] = jnp.zeros_like(dv_acc)
            dc_ref[...] = jnp.zeros_like(dc_ref)

        top_half = lax.broadcasted_iota(jnp.int32, (LANES, BLOCK), 0) < HEAD_DIM
        pair_t = lambda x: jnp.concatenate([jnp.where(top_half, x.T, 0), jnp.where(top_half, 0, x.T)], axis=1)
        first_half = lax.broadcasted_iota(jnp.int32, (BLOCK, LANES), 1) < HEAD_DIM
        wide = lambda col: jnp.broadcast_to(col, (BLOCK, BLOCK))
        qs, dobs, qs_t, dob_t, deltas = [], [], [], [], []
        for a in range(FOX_PAIRS):
            tile = slice(a * LANES, (a + 1) * LANES)
            qs.append(q_ref[:, tile] * SCALE)
            dobs.append(do_ref[:, tile])
            qs_t.append(pair_t(qs[a]))
            dob_t.append(pair_t(dobs[a]))
            weighted = dobs[a].astype(F32) * ox_ref[:, tile]
            deltas += [wide(jnp.sum(jnp.where(first_half, weighted, 0.0), axis=-1, keepdims=True)),
                       wide(jnp.sum(jnp.where(first_half, 0.0, weighted), axis=-1, keepdims=True))]
        lses = [wide(lse_ref[e]) for e in range(FOX_HEADS)]

        def step(ci, dqs):
            sb, lo, krows = _fox_chunk(qb, ci)
            dqs = list(dqs)
            for a in range(FOX_PAIRS):
                for e in (2 * a, 2 * a + 1):
                    tile = slice(e * LANES, (e + 1) * LANES)
                    kk = k_ref[krows, tile]
                    s_scr[e] = lax.dot_general(qs[a], kk, NT_DIMS, preferred_element_type=F32)
                    dp_scr[e] = lax.dot_general(dobs[a], v_ref[krows, tile], NT_DIMS, preferred_element_type=F32)
                    for j in range(KEY_BLOCKS):
                        cols = slice(j * BLOCK, (j + 1) * BLOCK)
                        p = jnp.exp(_fox_logits(s_scr, cr_ref, e, j, sb, lo, qb) - lses[e])
                        ds = p * (dp_scr[e, :, cols] - deltas[e])
                        dc_ref[e, sb + j] -= jnp.sum(ds, axis=0, keepdims=True)
                        p_scr[e, :, cols] = p.astype(BF16)
                        ds_scr[e, :, cols] = ds.astype(BF16)
                    dqs[a] = dqs[a] + jnp.dot(ds_scr[e], kk, preferred_element_type=F32)
                both = slice(2 * a, 2 * a + 2)
                dk_t = jnp.dot(qs_t[a], ds_scr[both].reshape(2 * BLOCK, CHUNK), preferred_element_type=F32)
                dv_t = jnp.dot(dob_t[a], p_scr[both].reshape(2 * BLOCK, CHUNK), preferred_element_type=F32)
                for j in range(KEY_BLOCKS):
                    cols = slice(j * BLOCK, (j + 1) * BLOCK)
                    dk_acc[a * nb + sb + j] += dk_t[:, cols]
                    dv_acc[a * nb + sb + j] += dv_t[:, cols]
            return tuple(dqs)

        dqs = lax.fori_loop(0, (qb + KEY_BLOCKS) // KEY_BLOCKS, step,
                            (jnp.zeros((BLOCK, LANES), F32),) * FOX_PAIRS)
        for a in range(FOX_PAIRS):
            dq_ref[:, a * LANES:(a + 1) * LANES] = (dqs[a] * SCALE).astype(BF16)

        @pl.when(qb == nb - 1)
        def _():
            for a in range(FOX_PAIRS):
                for kb in range(nb):
                    rows = slice(kb * BLOCK, (kb + 1) * BLOCK)
                    for acc, out_ref in ((dk_acc, dk_ref), (dv_acc, dv_ref)):
                        out_ref[rows, a * LANES:(a + 1) * LANES] = acc[a * nb + kb].T.astype(BF16)

    q_spec, kv_spec, cc_spec, cr_spec = _fox_specs(nb)
    dkv_spec = pl.BlockSpec((l, FOX_PAIRS * LANES), lambda b, p, i: (b, p))
    outs = pl.pallas_call(
        body,
        name="fox_bwd",
        grid=(bsz, FOX_STEPS, nb),
        in_specs=[q_spec, kv_spec, kv_spec, q_spec, q_spec, cc_spec, cr_spec] + [HBM_SPEC] * n_x,
        out_specs=[q_spec, dkv_spec, dkv_spec, cr_spec] + [HBM_SPEC] * n_x,
        out_shape=[jax.ShapeDtypeStruct((t, B_WIDTH), BF16), jax.ShapeDtypeStruct((t, B_WIDTH), BF16),
                   jax.ShapeDtypeStruct((t, B_WIDTH), BF16),
                   jax.ShapeDtypeStruct((bsz * B_HEADS, nb, 1, BLOCK), F32)] + (exchange.out_shape if exchange else []),
        scratch_shapes=[pltpu.VMEM((FOX_PAIRS * nb, LANES, BLOCK), F32), pltpu.VMEM((FOX_PAIRS * nb, LANES, BLOCK), F32),
                        pltpu.VMEM((FOX_HEADS, BLOCK, CHUNK), F32), pltpu.VMEM((FOX_HEADS, BLOCK, CHUNK), F32),
                        pltpu.VMEM((FOX_HEADS, BLOCK, CHUNK), BF16), pltpu.VMEM((FOX_HEADS, BLOCK, CHUNK), BF16)]
        + (exchange.scratch if exchange else []),
        compiler_params=_cparams(("arbitrary",) * 3 if exchange else ("parallel", "parallel", "arbitrary")),
    )(q, k, v, o_exact, do, lse, c_row, *(exchange.ins if exchange else []))
    return outs[:4], outs[4:]


def _loss_head(h, final_w, target):
    bsz, l, d = h.shape
    nb = l // BLOCK

    def body(h_ref, w_ref, t_ref, loss_ref, dh_ref, dw_ref):
        b = pl.program_id(0)
        n = pl.program_id(1)

        @pl.when((b == 0) & (n == 0))
        def _():
            loss_ref[...] = jnp.zeros_like(loss_ref)
            dw_ref[...] = jnp.zeros_like(dw_ref)

        @pl.when(n == 0)
        def _():
            dh_ref[...] = jnp.zeros_like(dh_ref)

        @pl.when(n > 0)
        def _():
            hh = h_ref[0]
            w = w_ref[...]
            r = _rms_scale(hh)
            err = (hh * r) * w - t_ref[0]
            loss_ref[...] += 0.5 * jnp.sum(jnp.mean(err * err, axis=-1, keepdims=True), axis=0, keepdims=True)
            dy = err * (1.0 / d)
            dh, dw = _rms_bwd(dy, hh, w)
            dh_ref[0] = dh
            dw_ref[...] += dw

    return pl.pallas_call(
        body,
        name="loss_head",
        grid=(bsz, nb),
        in_specs=[
            pl.BlockSpec((1, BLOCK, d), lambda b, n: (b, n, 0)),
            pl.BlockSpec((1, d), lambda b, n: (0, 0)),
            pl.BlockSpec((1, BLOCK, d), lambda b, n: (b, jnp.maximum(n - 1, 0), 0)),
        ],
        out_specs=[
            pl.BlockSpec((1, 128), lambda b, n: (0, 0)),
            pl.BlockSpec((1, BLOCK, d), lambda b, n: (b, n, 0)),
            pl.BlockSpec((1, d), lambda b, n: (0, 0)),
        ],
        out_shape=[jax.ShapeDtypeStruct((1, 128), F32), jax.ShapeDtypeStruct((bsz, l, d), F32),
                   jax.ShapeDtypeStruct((1, d), F32)],
        compiler_params=_cparams(("arbitrary", "arbitrary")),
    )(h, final_w, target)


def _pad_tiles(w, src, heads, lane_slot, axis):
    pieces = []
    for h in range(heads):
        x = lax.slice_in_dim(w, src + HEAD_DIM * h, src + HEAD_DIM * (h + 1), axis=axis)
        z = jnp.zeros_like(x)
        pieces += [x, z] if lane_slot(h) == 0 else [z, x]
    return pieces


def _unpad_tiles(g, off, heads, lane_slot, axis):
    return [lax.slice_in_dim(g, off + LANES * h + HEAD_DIM * lane_slot(h),
                             off + LANES * h + HEAD_DIM * (lane_slot(h) + 1), axis=axis) for h in range(heads)]


def _layout_w_in(w):
    pad_f = jnp.zeros((w.shape[0], F_COLS - B_HEADS), w.dtype)
    return jnp.concatenate([w[:, :SRC_F], w[:, SRC_GA:], w[:, SRC_F:SRC_GA], pad_f], axis=1)


def _unlayout_w_in(g):
    return jnp.concatenate([g[:, :OFF_GA], g[:, OFF_F:OFF_F + B_HEADS], g[:, OFF_GA:OFF_F]], axis=1)


def _local_step(x, target, meta, norms, b_forget, sinks, w, comm=None):
    n1, nmix, n2, nfin = norms
    w1i, w1o = w[:2]
    bsz, seq, d = x.shape
    l = PREFIX + seq
    nb = l // BLOCK
    t = bsz * l

    h0 = jnp.concatenate([jnp.zeros((bsz, N_PAD, d), F32),
                          jnp.broadcast_to(meta[None], (bsz, N_META, d)), x], axis=1).reshape(t, d)

    if comm is None:
        (h1, g1, u1), _ = _ffn_fwd(h0, n1, w1i, w1o)
        w_in, wa, wb, wo, w2i, w2o = w[2:]
    else:
        (h1, g1, u1), gathered = _ffn_fwd(h0, n1, w1i, w1o, comm.gather(GATHER_PROJ))
        w_in, = comm.gathered(GATHER_PROJ, gathered)
    wp = _layout_w_in(w_in)
    un, qa, ka, va, qb, kb, vb, ga, gb, f_logit = _proj_fwd(h1, nmix, wp)
    b_pad = jnp.concatenate([b_forget, jnp.zeros((1, F_COLS - B_HEADS), F32)], axis=1)
    c = _forget_cumsum(f_logit, b_pad, nb)
    c_heads = c[:, :B_HEADS].reshape(bsz, l, B_HEADS).transpose(0, 2, 1).reshape(bsz * B_HEADS, l)
    c_row = c_heads.reshape(bsz * B_HEADS, nb, 1, BLOCK)

    slopes = jnp.exp2(-8.0 * jnp.arange(1, A_HEADS + 1, dtype=F32) / A_HEADS)
    slope_rows = jnp.repeat(slopes.reshape(A_KV_HEADS, A_GROUP), BLOCK, axis=1)[:, :, None]
    sink_rows = jnp.repeat(sinks.reshape(A_KV_HEADS, A_GROUP), BLOCK, axis=1)[:, :, None]

    oa, lse_a = _swa_fwd(qa, ka, va, sink_rows, slope_rows, nb)
    if comm is None:
        (ob, ob_exact, lse_b), _ = _fox_fwd(qb, kb, vb, c_row, nb)
    else:
        (ob, ob_exact, lse_b), gathered = _fox_fwd(qb, kb, vb, c_row, nb, comm.gather(GATHER_LATE))
        wa, wb, wo, w2i, w2o = comm.gathered(GATHER_LATE, gathered)
    wa_p = jnp.concatenate(_pad_tiles(wa, 0, A_HEADS, A_SLOT, 0), axis=0)
    h2, mixed = _merge_fwd(h1, oa, ob, ga, gb, wa_p, wb, wo)
    (h3, g2, u2), _ = _ffn_fwd(h2, n2, w2i, w2o)
    loss, dh3, d_nfin = _loss_head(h3.reshape(bsz, l, d), nfin, target)

    (dh2, n2b, a2, dgu2, df2, dn2_parts), _ = _ffn_bwd(dh3.reshape(t, d), h2, n2, g2, u2, w2i, w2o)
    g_w2o = _tn_matmul(a2, df2, "grad_ffn2_w_out")
    g_w2i = _tn_matmul(n2b, dgu2, "grad_ffn2_w_in")

    hosted = comm.swap("ffn2", dict(ffn2_w_in=g_w2i, ffn2_w_out=g_w2o)) if comm else None
    (dya, dyb, doa, dob, dga, dgb, dh2b), swapped = _merge_bwd(dh2, oa, ob, ga, gb, wa_p, wb, wo, hosted)
    g_wo = _tn_matmul(mixed, dh2b, "grad_w_out")
    g_wa = jnp.concatenate(_unpad_tiles(_tn_matmul(oa, dya, "grad_w_branch_a"), 0, A_HEADS, A_SLOT, 0), axis=0)
    g_wb = _tn_matmul(ob, dyb, "grad_w_branch_b")

    dqa, dka, dva, dsink_rows = _swa_bwd(qa, ka, va, doa, lse_a, sink_rows, slope_rows, nb)
    hosted = comm.scatter("ffn2", swapped) if comm else None
    (dqb, dkb, dvb, dc_row), pieces = _fox_bwd(qb, kb, vb, ob_exact, dob, lse_b, c_row, nb, hosted)
    if comm:
        comm.received("ffn2", pieces)
    dc = dc_row.reshape(bsz, B_HEADS, l).transpose(0, 2, 1).reshape(t, B_HEADS)
    dc = jnp.concatenate([dc, jnp.zeros((t, F_COLS - B_HEADS), F32)], axis=1)
    df_logit, db_parts = _forget_cumsum_bwd(dc, f_logit, b_pad, nb)

    dproj = jnp.concatenate([dqa, dka, dva, dqb, dkb, dvb, dga, dgb, df_logit], axis=1)
    g_win = _unlayout_w_in(_tn_matmul(un, dproj, "grad_w_in"))
    hosted = comm.swap("mixer", dict(w_in=g_win, w_branch_a=g_wa, w_branch_b=g_wb, w_out=g_wo)) if comm else None
    (dh1, dnmix_parts), swapped = _proj_bwd(dh2, h1, nmix, dproj, wp, hosted)
    hosted = comm.scatter("mixer", swapped) if comm else None
    (dh0, n1b, a1, dgu1, df1, dn1_parts), pieces = _ffn_bwd(dh1, h0, n1, g1, u1, w1i, w1o, hosted)
    if comm:
        comm.received("mixer", pieces)
    g_w1o = _tn_matmul(a1, df1, "grad_ffn1_w_out")
    g_w1i = _tn_matmul(n1b, dgu1, "grad_ffn1_w_in")

    dh0 = dh0.reshape(bsz, l, d)
    grad_x = dh0[:, PREFIX:]
    small = dict(
        meta_tokens=jnp.sum(dh0[:, N_PAD:PREFIX], axis=0),
        ffn1_norm=jnp.sum(dn1_parts, axis=0),
        mix_norm=jnp.sum(dnmix_parts, axis=0),
        ffn2_norm=jnp.sum(dn2_parts, axis=0),
        final_norm=d_nfin,
        b_forget=jnp.sum(db_parts, axis=0)[:, :B_HEADS],
        attn_sinks=jnp.sum(dsink_rows.reshape(bsz, A_HEADS, BLOCK), axis=(0, 2)).reshape(1, A_HEADS),
    )
    big = dict(ffn1_w_in=g_w1i, ffn1_w_out=g_w1o, w_in=g_win, w_branch_a=g_wa, w_branch_b=g_wb,
               w_out=g_wo, ffn2_w_in=g_w2i, ffn2_w_out=g_w2o)
    return loss, grad_x, small, big


BIG = (
    ("ffn1_w_in", (D_MODEL, 5632), 1),
    ("ffn1_w_out", (2816, D_MODEL), 0),
    ("w_in", (D_MODEL, W_IN_COLS), 1),
    ("w_branch_a", (A_WIDTH, D_MODEL), 1),
    ("w_branch_b", (B_WIDTH, D_MODEL), 1),
    ("w_out", (D_MODEL, D_MODEL), 0),
    ("ffn2_w_in", (D_MODEL, 5632), 1),
    ("ffn2_w_out", (2816, D_MODEL), 0),
)
STACKED = "w_in"


def _coords():
    return lax.axis_index("x"), lax.axis_index("y"), lax.axis_index("c")


def _other_chips(x, y):
    return ((1 - x, y), (x, 1 - y), (1 - x, 1 - y))


def _chip_part(ref, name, shape, axis, k):
    if name == STACKED:
        return ref.at[k]
    size = shape[axis] // N_CHIPS
    start = pl.multiple_of(k * size, size)
    return ref.at[pl.ds(start, size), :] if axis == 0 else ref.at[:, pl.ds(start, size)]


def _full_shape(name, shape):
    return (N_CHIPS, shape[0], shape[1] // N_CHIPS) if name == STACKED else shape


class _Exchange:
    def __init__(self, ins, out_shape, n_sems, ops):
        self.ins, self.out_shape, self.n_sems, self.ops = list(ins), list(out_shape), n_sems, ops

    @property
    def scratch(self):
        return [pltpu.SemaphoreType.DMA((self.n_sems,)), pltpu.SemaphoreType.DMA((self.n_sems,))]


SEMS_PER_GATHER = 7


def _gather_exchange(shards, table):
    n = len(table)

    def ops(ins, outs, send_sems, recv_sems):
        x, y, c = _coords()
        mine = 2 * x + y
        sibling = (x, y, 1 - c)
        chips = _other_chips(x, y)
        slots = [2 * chip[0] + chip[1] for chip in chips]

        def part(i, k):
            name, shape, axis = table[i][:3]
            return _chip_part(outs[i], name, shape, axis, k)

        def half(ref, h):
            rows = ref.shape[0] // 2
            return ref.at[pl.ds(pl.multiple_of(h * rows, rows), rows), :]

        def own(i):
            sem = SEMS_PER_GATHER * i
            return pltpu.make_async_remote_copy(ins[i], part(i, mine), send_sems.at[sem], recv_sems.at[sem],
                                                device_id=sibling, device_id_type=MESH_ID)

        def fetch(i, j, slot):
            sem = SEMS_PER_GATHER * i + 1 + j
            if table[i][4]:
                src, dst = half(ins[i], c), half(part(i, slot), c)
            else:
                src, dst = ins[i], part(i, slot)
            return pltpu.make_async_remote_copy(src, dst, send_sems.at[sem], recv_sems.at[sem],
                                                device_id=(chips[j][0], chips[j][1], c), device_id_type=MESH_ID)

        def forward(i, j, h):
            sem = SEMS_PER_GATHER * i + 4 + j
            region = half(part(i, slots[j]), h)
            return pltpu.make_async_remote_copy(region, region, send_sems.at[sem], recv_sems.at[sem],
                                                device_id=sibling, device_id_type=MESH_ID)

        def start():
            for i in range(n):
                for j in range(3):
                    fetch(i, j, mine).start()
            for i in range(n):
                own(i).start()

        def relay():
            for i in range(n):
                for j in range(3):
                    fetch(i, j, slots[j]).wait_recv()
                    if table[i][4]:
                        forward(i, j, c).start()

        def finish():
            for i in range(n):
                own(i).wait()
                for j in range(3):
                    if table[i][4]:
                        forward(i, j, 1 - c).wait_recv()
                        forward(i, j, c).wait_send()
                    fetch(i, j, mine).wait_send()

        return start, relay, finish

    out_shape = [jax.ShapeDtypeStruct(_full_shape(name, shape), dtype) for name, shape, _, dtype, _ in table]
    return _Exchange(shards, out_shape, SEMS_PER_GATHER * n, ops)


def _run_exchange(exchange, name):
    n = len(exchange.ins)

    def body(*refs):
        start, relay, finish = exchange.ops(refs[:n], refs[n:2 * n], *refs[2 * n:])
        start()
        relay()
        finish()

    return pl.pallas_call(
        body,
        name=name,
        in_specs=[HBM_SPEC] * n,
        out_specs=[HBM_SPEC] * n,
        out_shape=exchange.out_shape,
        scratch_shapes=exchange.scratch,
    )(*exchange.ins)


def _host_exchange(exchange, in_refs, out_refs, sem_refs, first, middle, last):
    start, relay, finish = exchange.ops(in_refs, out_refs, *sem_refs)
    pl.when(first)(start)
    pl.when(middle)(relay)
    pl.when(last)(finish)


def _halves_view(name, shape, axis):
    r, c = shape
    if name == STACKED:
        return (N_CHIPS, 2, r // 2, c // N_CHIPS), lambda ref, h: ref.at[:, h]
    if axis == 1:
        return (2, r // 2, c), lambda ref, h: ref.at[h]
    return (N_CHIPS, 2, r // N_CHIPS // 2, c), lambda ref, h: ref.at[:, h]


def _halves_exchange(grads, entries):
    n_w = len(entries)
    views = [_halves_view(*entry) for entry in entries]

    def ops(ins, outs, send_sems, recv_sems):
        x, y, c = _coords()
        copies = [pltpu.make_async_remote_copy(views[i][1](ins[i], 1 - c), outs[i], send_sems.at[i], recv_sems.at[i],
                                               device_id=(x, y, 1 - c), device_id_type=MESH_ID) for i in range(n_w)]

        def start():
            for cp in copies:
                cp.start()

        def finish():
            for cp in copies:
                cp.wait()

        return start, lambda: None, finish

    half_shape = lambda v: tuple(d for i, d in enumerate(v) if i != (1 if len(v) == 4 else 0))
    out_shape = [jax.ShapeDtypeStruct(half_shape(v[0]), F32) for v in views]
    return _Exchange([g.reshape(v[0]) for g, v in zip(grads, views)], out_shape, n_w, ops)


def _add_sibling(g_view, recv, c, name):
    shape = recv.shape
    if len(shape) == 2:
        tr = _tile(shape[0], 128, 16)
        grid = (shape[0] // tr,)
        g_spec = pl.BlockSpec((None, tr, shape[1]), lambda i, c_ref: (c_ref[0], i, 0))
        r_spec = pl.BlockSpec((tr, shape[1]), lambda i, c_ref: (i, 0))
    else:
        tr = _tile(shape[1], 256, 16)
        grid = (N_CHIPS, shape[1] // tr)
        g_spec = pl.BlockSpec((None, None, tr, shape[2]), lambda k, i, c_ref: (k, c_ref[0], i, 0))
        r_spec = pl.BlockSpec((None, tr, shape[2]), lambda k, i, c_ref: (k, i, 0))

    def body(c_ref, g_ref, r_ref, o_ref):
        o_ref[...] = (g_ref[...] + r_ref[...]).astype(BF16)

    return pl.pallas_call(
        body,
        name="add_sibling_" + name,
        grid_spec=pltpu.PrefetchScalarGridSpec(num_scalar_prefetch=1, grid=grid, in_specs=[g_spec, r_spec],
                                               out_specs=r_spec),
        out_shape=jax.ShapeDtypeStruct(shape, BF16),
        compiler_params=_cparams(("parallel",) * len(grid)),
    )(c, g_view, recv)


def _piece_of(ref, name, axis, k):
    if name == STACKED or axis == 0:
        return ref.at[k]
    size = ref.shape[1] // N_CHIPS
    return ref.at[:, pl.ds(pl.multiple_of(k * size, size), size)]


def _piece_shape(name, shape, axis):
    r, c = shape
    return (r // 2, c // N_CHIPS) if (axis == 1) else (r // N_CHIPS // 2, c)


def _scatter_exchange(partials, entries):
    n_w = len(entries)

    def ops(ins, outs, send_sems, recv_sems):
        x, y, c = _coords()
        chips = _other_chips(x, y)
        copies = []
        for i, (name, _, axis) in enumerate(entries):
            for j, chip in enumerate(chips):
                sem = 3 * i + j
                copies.append(pltpu.make_async_remote_copy(
                    _piece_of(ins[i], name, axis, 2 * chip[0] + chip[1]), outs[i].at[j], send_sems.at[sem],
                    recv_sems.at[sem], device_id=(chip[0], chip[1], c), device_id_type=MESH_ID))

        def start():
            for cp in copies:
                cp.start()

        def finish():
            for cp in copies:
                cp.wait()

        return start, lambda: None, finish

    out_shape = [jax.ShapeDtypeStruct((3,) + _piece_shape(*entry), BF16) for entry in entries]
    return _Exchange(partials, out_shape, 3 * n_w, ops)


def _add_chips(partial, recv, mine, name, axis):
    rows, cols = recv.shape[1:]
    tr = _tile(rows, 256, 16)
    if name == STACKED or axis == 0:
        p_spec = pl.BlockSpec((None, tr, cols), lambda i, k_ref: (k_ref[0], i, 0))
    else:
        p_spec = pl.BlockSpec((tr, cols), lambda i, k_ref: (i, k_ref[0]))

    def body(k_ref, p_ref, r_ref, o_ref):
        f32 = lambda a: a.astype(F32)
        o_ref[...] = ((f32(p_ref[...]) + f32(r_ref[0])) + f32(r_ref[1])) + f32(r_ref[2])

    return pl.pallas_call(
        body,
        name="add_chips_" + name,
        grid_spec=pltpu.PrefetchScalarGridSpec(
            num_scalar_prefetch=1, grid=(rows // tr,),
            in_specs=[p_spec, pl.BlockSpec((3, tr, cols), lambda i, k_ref: (0, i, 0))],
            out_specs=pl.BlockSpec((tr, cols), lambda i, k_ref: (i, 0))),
        out_shape=jax.ShapeDtypeStruct((rows, cols), F32),
        compiler_params=_cparams(("parallel",)),
    )(mine, partial, recv)


def _share_with_sibling(halves):
    n_w = len(halves)

    def body(*refs):
        ins, outs = refs[:n_w], refs[n_w:2 * n_w]
        send_sems, recv_sems = refs[2 * n_w:]
        x, y, c = _coords()
        copies = [pltpu.make_async_remote_copy(ins[i], outs[i], send_sems.at[i], recv_sems.at[i],
                                               device_id=(x, y, 1 - c), device_id_type=MESH_ID) for i in range(n_w)]
        for cp in copies:
            cp.start()
        for cp in copies:
            cp.wait()

    return pl.pallas_call(
        body,
        name="share_with_sibling",
        in_specs=[HBM_SPEC] * n_w,
        out_specs=[HBM_SPEC] * n_w,
        out_shape=[jax.ShapeDtypeStruct(h.shape, F32) for h in halves],
        scratch_shapes=[pltpu.SemaphoreType.DMA((n_w,)), pltpu.SemaphoreType.DMA((n_w,))],
    )(*halves)


SMALL_ROWS = 168


def _all_reduce_small(buf):
    def body(b_ref, out_ref, gathered, send_sems, recv_sems):
        x, y, c = _coords()
        me = 4 * x + 2 * y + c
        peers = [(x ^ fx, y ^ fy, c ^ fc) for fx in (0, 1) for fy in (0, 1) for fc in (0, 1)][1:]

        def copy(j, slot, dev):
            return pltpu.make_async_remote_copy(b_ref, gathered.at[slot], send_sems.at[j], recv_sems.at[j],
                                                device_id=dev, device_id_type=MESH_ID)

        for j, dev in enumerate(peers):
            copy(j, me, dev).start()
        gathered[me] = b_ref[...]
        for j, dev in enumerate(peers):
            copy(j, 4 * dev[0] + 2 * dev[1] + dev[2], dev).wait()
        acc = gathered[0]
        for d in range(1, N_DEV):
            acc = acc + gathered[d]
        out_ref[...] = acc

    return pl.pallas_call(
        body,
        name="all_reduce_small",
        in_specs=[VMEM_SPEC],
        out_specs=VMEM_SPEC,
        out_shape=jax.ShapeDtypeStruct(buf.shape, F32),
        scratch_shapes=[pltpu.VMEM((N_DEV,) + buf.shape, F32), pltpu.SemaphoreType.DMA((N_DEV - 1,)),
                        pltpu.SemaphoreType.DMA((N_DEV - 1,))],
    )(buf)


def _adamw(w, g, m, v):
    r, rest = w.shape[0], w.shape[1:]
    per_row = 1
    for dim in rest:
        per_row *= dim
    tr = _tile(r, max(8, (5 << 19) // (4 * per_row)), 8 if len(rest) == 1 else 1)

    def body(w_ref, g_ref, m_ref, v_ref, d_ref, mo_ref, vo_ref):
        gg = g_ref[...]
        mm = ADAM_B1 * m_ref[...] + (1.0 - ADAM_B1) * gg
        vv = ADAM_B2 * v_ref[...] + (1.0 - ADAM_B2) * (gg * gg)
        m_hat = mm / (1.0 - ADAM_B1 ** ADAM_STEP)
        v_hat = vv / (1.0 - ADAM_B2 ** ADAM_STEP)
        d_ref[...] = -ADAM_LR * (m_hat / (jnp.sqrt(v_hat) + ADAM_EPS) + ADAM_WD * w_ref[...])
        mo_ref[...] = mm
        vo_ref[...] = vv

    spec = pl.BlockSpec((tr,) + rest, lambda i: (i,) + (0,) * len(rest))
    return pl.pallas_call(
        body,
        name="adamw",
        grid=(r // tr,),
        in_specs=[spec] * 4,
        out_specs=[spec] * 3,
        out_shape=[jax.ShapeDtypeStruct(w.shape, F32)] * 3,
        compiler_params=_cparams(("parallel",)),
    )(w, g, m, v)


def _adamw_halves(w, own, other, m, v, c, name):
    r, cols = w.shape
    half = r // 2
    tr = _tile(half, 256, 8)
    nt = half // tr
    whole = pl.BlockSpec((tr, cols), lambda h, i, c_ref: (h * nt + i, 0))
    part = pl.BlockSpec((tr, cols), lambda h, i, c_ref: (i, 0))

    def body(c_ref, w_ref, own_ref, other_ref, m_ref, v_ref, g_ref, d_ref, mo_ref, vo_ref):
        gg = jnp.where(pl.program_id(0) == c_ref[0], own_ref[...], other_ref[...])
        g_ref[...] = gg
        mm = ADAM_B1 * m_ref[...] + (1.0 - ADAM_B1) * gg
        vv = ADAM_B2 * v_ref[...] + (1.0 - ADAM_B2) * (gg * gg)
        m_hat = mm / (1.0 - ADAM_B1 ** ADAM_STEP)
        v_hat = vv / (1.0 - ADAM_B2 ** ADAM_STEP)
        d_ref[...] = -ADAM_LR * (m_hat / (jnp.sqrt(v_hat) + ADAM_EPS) + ADAM_WD * w_ref[...])
        mo_ref[...] = mm
        vo_ref[...] = vv

    return pl.pallas_call(
        body,
        name="adamw_" + name,
        grid_spec=pltpu.PrefetchScalarGridSpec(
            num_scalar_prefetch=1, grid=(2, nt),
            in_specs=[whole, part, part, whole, whole], out_specs=[whole] * 4),
        out_shape=[jax.ShapeDtypeStruct((r, cols), F32)] * 4,
        compiler_params=_cparams(("parallel", "parallel")),
    )(c, w, own, other, m, v)


GATHER_FIRST = ("ffn1_w_in", "ffn1_w_out")
GATHER_PROJ = ("w_in",)
GATHER_LATE = ("w_branch_a", "w_branch_b", "w_out", "ffn2_w_in", "ffn2_w_out")


class _Comm:
    def __init__(self, shards, c_arr, mine_arr):
        self.shards, self.c, self.mine = shards, c_arr, mine_arr
        self.groups, self.halves = {}, {}
        self.by_name = {entry[0]: entry for entry in BIG}

    def gather(self, names):
        table = [self.by_name[n] + (BF16, True) for n in names]
        return _gather_exchange([self.shards[n] for n in names], table)

    def gathered(self, names, outs):
        return [o.transpose(1, 0, 2).reshape(D_MODEL, W_IN_COLS) if n == STACKED else o for n, o in zip(names, outs)]

    def swap(self, tag, grads):
        entries = [self.by_name[n] for n in grads]
        arrays = [g.reshape(D_MODEL, N_CHIPS, W_IN_COLS // N_CHIPS).transpose(1, 0, 2) if n == STACKED else g
                  for n, g in grads.items()]
        self.groups[tag] = (entries, arrays)
        return _halves_exchange(arrays, entries)

    def scatter(self, tag, received):
        entries, arrays = self.groups[tag]
        views = [_halves_view(*entry) for entry in entries]
        partials = [_add_sibling(g.reshape(v[0]), r, self.c, name)
                    for g, v, r, (name, _, _) in zip(arrays, views, received, entries)]
        self.groups[tag] = (entries, partials)
        return _scatter_exchange(partials, entries)

    def received(self, tag, pieces):
        entries, partials = self.groups[tag]
        for p, r, (name, _, axis) in zip(partials, pieces, entries):
            self.halves[name] = _add_chips(p, r, self.mine, name, axis)

    def finish(self):
        names = [n for n, _, _ in BIG]
        own = [self.halves[n] for n in names]
        return dict(zip(names, zip(own, _share_with_sibling(own))))


def kernel(x, meta_tokens, ffn1_norm, ffn1_w_in, ffn1_w_out, mix_norm, w_in, b_forget, attn_sinks, w_branch_a, w_branch_b, w_out, ffn2_norm, ffn2_w_in, ffn2_w_out, final_norm, loss_target, m_meta_tokens, m_ffn1_norm, m_ffn1_w_in, m_ffn1_w_out, m_mix_norm, m_w_in, m_b_forget, m_attn_sinks, m_w_branch_a, m_w_branch_b, m_w_out, m_ffn2_norm, m_ffn2_w_in, m_ffn2_w_out, m_final_norm, v_meta_tokens, v_ffn1_norm, v_ffn1_w_in, v_ffn1_w_out, v_mix_norm, v_w_in, v_b_forget, v_attn_sinks, v_w_branch_a, v_w_branch_b, v_w_out, v_ffn2_norm, v_ffn2_w_in, v_ffn2_w_out, v_final_norm):
    given = dict(locals())
    names = ["meta_tokens", "ffn1_norm", "ffn1_w_in", "ffn1_w_out", "mix_norm", "w_in", "b_forget", "attn_sinks",
             "w_branch_a", "w_branch_b", "w_out", "ffn2_norm", "ffn2_w_in", "ffn2_w_out", "final_norm"]
    big_names = [n for n, _, _ in BIG]
    cx, cy, cc = _coords()
    c_arr = cc.reshape(1).astype(jnp.int32)
    mine_arr = (2 * cx + cy).reshape(1).astype(jnp.int32)

    comm = _Comm({n: given[n][0].astype(BF16) for n in big_names}, c_arr, mine_arr)
    table = [comm.by_name[n] + (BF16, True) for n in GATHER_FIRST] + [("meta_tokens", (N_META, D_MODEL), 1, F32, False)]
    first = _gather_exchange([comm.shards[n] for n in GATHER_FIRST] + [meta_tokens], table)
    w1i, w1o, meta_full = _run_exchange(first, "gather_first")
    norms = (ffn1_norm, mix_norm, ffn2_norm, final_norm.reshape(1, D_MODEL))
    loss, grad_x, small, big = _local_step(x, loss_target, meta_full, norms, b_forget, attn_sinks, (w1i, w1o), comm)

    swap = comm.swap("ffn1", dict(ffn1_w_in=big["ffn1_w_in"], ffn1_w_out=big["ffn1_w_out"]))
    last = comm.scatter("ffn1", _run_exchange(swap, "exchange_halves"))
    comm.received("ffn1", _run_exchange(last, "scatter_chip_sums"))
    grad_halves = comm.finish()
    grads = {}

    pad_lanes = lambda a: jnp.concatenate([a, jnp.zeros((1, LANES - a.shape[1]), F32)], axis=1)
    buf = jnp.concatenate([
        small["meta_tokens"].reshape(128, LANES),
        small["ffn1_norm"].reshape(8, LANES), small["mix_norm"].reshape(8, LANES),
        small["ffn2_norm"].reshape(8, LANES), small["final_norm"].reshape(8, LANES),
        loss, pad_lanes(small["b_forget"]), pad_lanes(small["attn_sinks"]),
        jnp.zeros((SMALL_ROWS - 163, LANES), F32)], axis=0)
    red = _all_reduce_small(buf)
    meta_cols = red[:128].reshape(N_META, D_MODEL)
    grads["meta_tokens"] = lax.dynamic_slice_in_dim(meta_cols, (2 * cx + cy) * (D_MODEL // N_CHIPS),
                                                    D_MODEL // N_CHIPS, axis=1)
    grads["ffn1_norm"] = red[128:136].reshape(1, D_MODEL)
    grads["mix_norm"] = red[136:144].reshape(1, D_MODEL)
    grads["ffn2_norm"] = red[144:152].reshape(1, D_MODEL)
    grads["final_norm"] = red[152:160].reshape(1, D_MODEL)
    loss_out = red[160, 0]
    grads["b_forget"] = red[161:162, :B_HEADS]
    grads["attn_sinks"] = red[162:163, :A_HEADS]

    out_g, out_d, out_m, out_v = [], [], [], []
    for n in names:
        w_full = given[n]
        shape = w_full.shape
        two_d = (lambda a: a.reshape(shape[-2], shape[-1])) if len(shape) >= 2 else (lambda a: a.reshape(1, shape[0]))
        if n == STACKED:
            own, other = grad_halves[n]
            g_t = jnp.concatenate([jnp.where(cc == 0, own, other), jnp.where(cc == 0, other, own)], axis=0).T
            tiles = lambda a: a.reshape(shape[-1], shape[-2] // LANES, LANES)
            untile = lambda a: a.reshape(shape[-1], shape[-2]).T
            d2, m2, v2 = [untile(a) for a in _adamw(tiles(two_d(w_full).T), tiles(g_t), tiles(two_d(given["m_" + n]).T),
                                                     tiles(two_d(given["v_" + n]).T))]
            g2 = g_t.T
        elif n in grad_halves:
            own, other = grad_halves[n]
            g2, d2, m2, v2 = _adamw_halves(two_d(w_full), own, other, two_d(given["m_" + n]),
                                           two_d(given["v_" + n]), c_arr, n)
        else:
            g2 = two_d(grads[n])
            d2, m2, v2 = _adamw(two_d(w_full), g2, two_d(given["m_" + n]), two_d(given["v_" + n]))
        out_g.append(g2.reshape(shape))
        out_d.append(d2.reshape(shape))
        out_m.append(m2.reshape(shape))
        out_v.append(v2.reshape(shape))
    return (loss_out, grad_x, *out_g, *out_d, *out_m, *out_v)
```

```python
import jax
import jax.numpy as jnp
from jax import lax
from jax.experimental import pallas as pl
from jax.experimental.pallas import tpu as pltpu

F32 = jnp.float32
BF16 = jnp.bfloat16

D_MODEL = 1024
N_META = 16
BLOCK = 128
LANES = 128
PREFIX = BLOCK
N_PAD = PREFIX - N_META
HEAD_DIM = 64
A_HEADS = 8
A_KV_HEADS = 2
A_GROUP = 4
B_HEADS = 8
B_PAIRS = B_HEADS // 2
A_WIDTH = A_HEADS * HEAD_DIM
A_KV_WIDTH = A_KV_HEADS * HEAD_DIM
B_WIDTH = B_HEADS * HEAD_DIM
W_IN_COLS = A_WIDTH + 2 * A_KV_WIDTH + 3 * B_WIDTH + B_HEADS + 2 * D_MODEL
SRC_KA = A_WIDTH
SRC_VA = SRC_KA + A_KV_WIDTH
SRC_QB = SRC_VA + A_KV_WIDTH
SRC_KB = SRC_QB + B_WIDTH
SRC_VB = SRC_KB + B_WIDTH
SRC_F = SRC_VB + B_WIDTH
SRC_GA = SRC_F + B_HEADS
SRC_GB = SRC_GA + D_MODEL
A_PAD_WIDTH = A_HEADS * LANES
B_PAD_WIDTH = B_HEADS * LANES
F_COLS = LANES
OFF_QA = 0
OFF_KA = SRC_KA
OFF_VA = SRC_VA
OFF_QB = SRC_QB
OFF_KB = SRC_KB
OFF_VB = SRC_VB
OFF_GA = SRC_F
OFF_GB = OFF_GA + D_MODEL
OFF_F = OFF_GB + D_MODEL
P_COLS = OFF_F + F_COLS
EPS = 1e-6
NEG = -1e30
SCALE = HEAD_DIM ** -0.5
KEY_BLOCKS = 4

ADAM_LR = 0.001
ADAM_B1 = 0.9
ADAM_B2 = 0.999
ADAM_EPS = 1e-08
ADAM_WD = 0.01
ADAM_STEP = 10

N_CHIPS = 4
N_DEV = 8
VMEM_LIMIT = 56 * 1024 * 1024

NT_DIMS = (((1,), (1,)), ((), ()))
TN_DIMS = (((0,), (0,)), ((), ()))
MESH_ID = pl.DeviceIdType.MESH
HBM_SPEC = pl.BlockSpec(memory_space=pltpu.HBM)
VMEM_SPEC = pl.BlockSpec(memory_space=pltpu.VMEM)


def _tile(n, target, mult=16):
    best = None
    for t in range(mult, min(n, target) + 1, mult):
        if n % t == 0:
            best = t
    return best if best is not None else n


def _cparams(sem):
    return pltpu.CompilerParams(dimension_semantics=sem, vmem_limit_bytes=VMEM_LIMIT)


def _rms_scale(h):
    return lax.rsqrt(jnp.mean(h * h, axis=-1, keepdims=True) + EPS)


def _rms_bwd(dn, h, w):
    r = _rms_scale(h)
    dw = jnp.sum(dn * (h * r), axis=0, keepdims=True)
    z = dn * w
    dh = r * z - h * ((r * r * r) * jnp.mean(z * h, axis=-1, keepdims=True))
    return dh, dw


def _ffn_fwd(h, norm_w, w_in, w_out, exchange=None):
    t, d = h.shape
    f = w_out.shape[0]
    tm = _tile(t, 272)
    tc = _tile(f, 256, 128)
    nj = f // tc
    ni = t // tm
    n_x = len(exchange.ins) if exchange else 0

    def body(*refs):
        h_ref, nw_ref, wi_ref, wo_ref = refs[:4]
        hout_ref, g_ref, u_ref = refs[4 + n_x:7 + n_x]
        a_scr = refs[7 + 2 * n_x]
        i = pl.program_id(0)
        if exchange:
            _host_exchange(exchange, refs[4:4 + n_x], refs[7 + n_x:7 + 2 * n_x], refs[8 + 2 * n_x:],
                           i == 0, i == ni - 2, i == ni - 1)
        hh = h_ref[...]
        n = ((hh * _rms_scale(hh)) * nw_ref[...]).astype(BF16)
        for j in range(nj):
            cols = slice(j * tc, (j + 1) * tc)
            g = jnp.dot(n, wi_ref[:, j * tc:(j + 1) * tc], preferred_element_type=F32)
            u = jnp.dot(n, wi_ref[:, f + j * tc:f + (j + 1) * tc], preferred_element_type=F32)
            g_ref[:, cols] = g
            u_ref[:, cols] = u
            a_scr[:, cols] = ((g * jax.nn.sigmoid(g)) * u).astype(BF16)
        hout_ref[...] = hh + 0.5 * jnp.dot(a_scr[...], wo_ref[...], preferred_element_type=F32)

    resident = lambda a: pl.BlockSpec(a.shape, lambda i: (0, 0), pipeline_mode=pl.Buffered(1))
    row = lambda w: pl.BlockSpec((tm, w), lambda i: (i, 0))
    outs = pl.pallas_call(
        body,
        name="ffn_fwd",
        grid=(ni,),
        in_specs=[row(d), pl.BlockSpec((1, d), lambda i: (0, 0)), resident(w_in), resident(w_out)] + [HBM_SPEC] * n_x,
        out_specs=[row(d), row(f), row(f)] + [HBM_SPEC] * n_x,
        out_shape=[
            jax.ShapeDtypeStruct((t, d), F32),
            jax.ShapeDtypeStruct((t, f), F32),
            jax.ShapeDtypeStruct((t, f), F32),
        ] + (exchange.out_shape if exchange else []),
        scratch_shapes=[pltpu.VMEM((tm, f), BF16)] + (exchange.scratch if exchange else []),
        compiler_params=_cparams(("arbitrary",) if exchange else ("parallel",)),
    )(h, norm_w, w_in, w_out, *(exchange.ins if exchange else []))
    return outs[:3], outs[3:]


def _ffn_bwd(dh_out, h, norm_w, g, u, w_in, w_out, exchange=None):
    t, d = h.shape
    f = w_out.shape[0]
    tm = _tile(t, 272)
    tc = _tile(f, 256, 128)
    nj = f // tc
    ni = t // tm
    n_x = len(exchange.ins) if exchange else 0

    def body(*refs):
        dho_ref, h_ref, nw_ref, g_ref, u_ref, wi_ref, wo_ref = refs[:7]
        dhin_ref, n_ref, a_ref, dgu_ref, df_ref, dnw_ref = refs[7 + n_x:13 + n_x]
        i = pl.program_id(0)
        if exchange:
            _host_exchange(exchange, refs[7:7 + n_x], refs[13 + n_x:13 + 2 * n_x], refs[13 + 2 * n_x:],
                           i == 0, i == ni - 1, i == ni - 1)
        hh = h_ref[...]
        nw = nw_ref[...]
        n_ref[...] = ((hh * _rms_scale(hh)) * nw).astype(BF16)
        dho = dho_ref[...]
        df = (0.5 * dho).astype(BF16)
        df_ref[...] = df
        for j in range(nj):
            cols = slice(j * tc, (j + 1) * tc)
            da = lax.dot_general(df, wo_ref[cols, :], NT_DIMS, preferred_element_type=F32)
            gg = g_ref[:, cols]
            uu = u_ref[:, cols]
            sig = jax.nn.sigmoid(gg)
            sl = gg * sig
            a_ref[:, cols] = (sl * uu).astype(BF16)
            dgu_ref[0, :, cols] = ((da * uu) * (sig * (1.0 + gg * (1.0 - sig)))).astype(BF16)
            dgu_ref[1, :, cols] = (da * sl).astype(BF16)
        dn = (lax.dot_general(dgu_ref[0], wi_ref[:, :f], NT_DIMS, preferred_element_type=F32)
              + lax.dot_general(dgu_ref[1], wi_ref[:, f:], NT_DIMS, preferred_element_type=F32))
        dh, dw = _rms_bwd(dn, hh, nw)
        dhin_ref[...] = dho + dh
        dnw_ref[0] = dw

    resident = lambda a: pl.BlockSpec(a.shape, lambda i: (0, 0), pipeline_mode=pl.Buffered(1))
    row = lambda w: pl.BlockSpec((tm, w), lambda i: (i, 0))
    outs = pl.pallas_call(
        body,
        name="ffn_bwd",
        grid=(ni,),
        in_specs=[row(d), row(d), pl.BlockSpec((1, d), lambda i: (0, 0)), row(f), row(f),
                  resident(w_in), resident(w_out)] + [HBM_SPEC] * n_x,
        out_specs=[row(d), row(d), row(f), pl.BlockSpec((2, tm, f), lambda i: (0, i, 0)), row(d),
                   pl.BlockSpec((1, 1, d), lambda i: (i, 0, 0))] + [HBM_SPEC] * n_x,
        out_shape=[
            jax.ShapeDtypeStruct((t, d), F32),
            jax.ShapeDtypeStruct((t, d), BF16),
            jax.ShapeDtypeStruct((t, f), BF16),
            jax.ShapeDtypeStruct((2, t, f), BF16),
            jax.ShapeDtypeStruct((t, d), BF16),
            jax.ShapeDtypeStruct((ni, 1, d), F32),
        ] + (exchange.out_shape if exchange else []),
        scratch_shapes=exchange.scratch if exchange else [],
        compiler_params=_cparams(("arbitrary",) if exchange else ("parallel",)),
    )(dh_out, h, norm_w, g, u, w_in, w_out, *(exchange.ins if exchange else []))
    return outs[:6], outs[6:]


def _tn_matmul(a, b, name, exchange=None):
    t, k = a.shape
    split = b.ndim == 3
    n = 2 * b.shape[2] if split else b.shape[1]
    tk = _tile(k, 512, 128)
    tn = _tile(b.shape[-1], 1408, 128)
    per_half = b.shape[-1] // tn
    ni, nj = k // tk, n // tn
    n_x = len(exchange.ins) if exchange else 0

    def body(*refs):
        a_ref, b_ref, o_ref = refs[0], refs[1], refs[2 + n_x]
        if exchange:
            i, j = pl.program_id(0), pl.program_id(1)
            at_end = (i == ni - 1) & (j == nj - 1)
            _host_exchange(exchange, refs[2:2 + n_x], refs[3 + n_x:3 + 2 * n_x], refs[3 + 2 * n_x:],
                           (i == 0) & (j == 0), at_end, at_end)
        o_ref[...] = lax.dot_general(a_ref[...], b_ref[...], TN_DIMS, preferred_element_type=F32)

    if split:
        b_spec = pl.BlockSpec((None, t, tn), lambda i, j: (j // per_half, 0, j % per_half))
    else:
        b_spec = pl.BlockSpec((t, tn), lambda i, j: (0, j))
    outs = pl.pallas_call(
        body,
        name=name,
        grid=(ni, nj),
        in_specs=[pl.BlockSpec((t, tk), lambda i, j: (0, i)), b_spec] + [HBM_SPEC] * n_x,
        out_specs=[pl.BlockSpec((tk, tn), lambda i, j: (i, j))] + [HBM_SPEC] * n_x,
        out_shape=[jax.ShapeDtypeStruct((k, n), F32)] + (exchange.out_shape if exchange else []),
        scratch_shapes=exchange.scratch if exchange else [],
        compiler_params=_cparams(("arbitrary", "arbitrary") if exchange else ("parallel", "parallel")),
    )(a, b, *(exchange.ins if exchange else []))
    return (outs[0], outs[1:]) if exchange else outs[0]


A_SLOT = lambda h: h // A_GROUP
B_SLOT = lambda h: h % 2

PROJ_PARTS = (
    (OFF_QA, A_WIDTH, A_PAD_WIDTH, True, A_SLOT), (OFF_KA, A_KV_WIDTH, A_KV_WIDTH, True, None),
    (OFF_VA, A_KV_WIDTH, A_KV_WIDTH, True, None), (OFF_QB, B_WIDTH, B_WIDTH, True, None),
    (OFF_KB, B_WIDTH, B_PAD_WIDTH, True, B_SLOT), (OFF_VB, B_WIDTH, B_PAD_WIDTH, True, B_SLOT),
    (OFF_GA, D_MODEL, D_MODEL, False, None), (OFF_GB, D_MODEL, D_MODEL, False, None), (OFF_F, F_COLS, F_COLS, False, None),
)


def _head_tile(pair, head, slot):
    lane_slot = lax.broadcasted_iota(jnp.int32, pair.shape, 1) // HEAD_DIM
    moved = pair if head % 2 == slot else pltpu.roll(pair, HEAD_DIM, 1)
    return jnp.where(lane_slot == slot, moved, 0.0)


def _proj_fwd(h, norm_w, w_p):
    t, d = h.shape
    tm = _tile(t, 272)

    def body(h_ref, nw_ref, w_ref, u_ref, *part_refs):
        hh = h_ref[...]
        un = ((hh * _rms_scale(hh)) * nw_ref[...]).astype(BF16)
        u_ref[...] = un
        for (off, width, _, _, slot), p_ref in zip(PROJ_PARTS, part_refs):
            if slot is None:
                p_ref[...] = jnp.dot(un, w_ref[:, off:off + width], preferred_element_type=F32).astype(p_ref.dtype)
                continue
            for pair in range(width // LANES):
                x = jnp.dot(un, w_ref[:, off + pair * LANES:off + (pair + 1) * LANES], preferred_element_type=F32)
                for head in (2 * pair, 2 * pair + 1):
                    p_ref[:, head * LANES:(head + 1) * LANES] = _head_tile(x, head, slot(head)).astype(p_ref.dtype)

    row = lambda w: pl.BlockSpec((tm, w), lambda i: (i, 0))
    return pl.pallas_call(
        body,
        name="proj_fwd",
        grid=(t // tm,),
        in_specs=[row(d), pl.BlockSpec((1, d), lambda i: (0, 0)),
                  pl.BlockSpec(w_p.shape, lambda i: (0, 0), pipeline_mode=pl.Buffered(1))],
        out_specs=[row(d)] + [row(width) for _, _, width, _, _ in PROJ_PARTS],
        out_shape=[jax.ShapeDtypeStruct((t, d), BF16)]
        + [jax.ShapeDtypeStruct((t, width), BF16 if is_bf else F32) for _, _, width, is_bf, _ in PROJ_PARTS],
        compiler_params=_cparams(("parallel",)),
    )(h, norm_w, w_p)


def _proj_bwd(dh_out, h, norm_w, dproj, w_p, exchange=None):
    t, d = h.shape
    n = w_p.shape[1]
    tm = _tile(t, 272)
    ni = t // tm
    n_x = len(exchange.ins) if exchange else 0

    def body(*refs):
        dho_ref, h_ref, nw_ref, dp_ref, w_ref = refs[:5]
        dhin_ref, dnw_ref = refs[5 + n_x:7 + n_x]
        if exchange:
            i = pl.program_id(0)
            _host_exchange(exchange, refs[5:5 + n_x], refs[7 + n_x:7 + 2 * n_x], refs[7 + 2 * n_x:],
                           i == 0, i == ni - 1, i == ni - 1)
        dn = lax.dot_general(dp_ref[...], w_ref[...], NT_DIMS, preferred_element_type=F32)
        dh, dw = _rms_bwd(dn, h_ref[...], nw_ref[...])
        dhin_ref[...] = dho_ref[...] + dh
        dnw_ref[0] = dw

    row = lambda w: pl.BlockSpec((tm, w), lambda i: (i, 0))
    outs = pl.pallas_call(
        body,
        name="proj_bwd",
        grid=(ni,),
        in_specs=[row(d), row(d), pl.BlockSpec((1, d), lambda i: (0, 0)), row(n),
                  pl.BlockSpec(w_p.shape, lambda i: (0, 0), pipeline_mode=pl.Buffered(1))] + [HBM_SPEC] * n_x,
        out_specs=[row(d), pl.BlockSpec((1, 1, d), lambda i: (i, 0, 0))] + [HBM_SPEC] * n_x,
        out_shape=[jax.ShapeDtypeStruct((t, d), F32), jax.ShapeDtypeStruct((ni, 1, d), F32)]
        + (exchange.out_shape if exchange else []),
        scratch_shapes=exchange.scratch if exchange else [],
        compiler_params=_cparams(("arbitrary",) if exchange else ("parallel",)),
    )(dh_out, h, norm_w, dproj, w_p, *(exchange.ins if exchange else []))
    return outs[:2], outs[2:]


def _merge_fwd(h, oa, ob, ga, gb, wa, wb, wo):
    t, d = h.shape
    tm = _tile(t, 544)

    def body(h_ref, oa_ref, ob_ref, ga_ref, gb_ref, wa_ref, wb_ref, wo_ref, hout_ref, mix_ref):
        ya = jnp.dot(oa_ref[...], wa_ref[...], preferred_element_type=F32)
        yb = jnp.dot(ob_ref[...], wb_ref[...], preferred_element_type=F32)
        mixed = (jax.nn.sigmoid(ga_ref[...]) * ya + jax.nn.sigmoid(gb_ref[...]) * yb).astype(BF16)
        mix_ref[...] = mixed
        hout_ref[...] = h_ref[...] + jnp.dot(mixed, wo_ref[...], preferred_element_type=F32)

    row = lambda w: pl.BlockSpec((tm, w), lambda i: (i, 0))
    full = lambda a: pl.BlockSpec(a.shape, lambda i: (0, 0))
    return pl.pallas_call(
        body,
        name="merge_fwd",
        grid=(t // tm,),
        in_specs=[row(d), row(oa.shape[1]), row(ob.shape[1]), row(d), row(d), full(wa), full(wb), full(wo)],
        out_specs=[row(d), row(d)],
        out_shape=[jax.ShapeDtypeStruct((t, d), F32), jax.ShapeDtypeStruct((t, d), BF16)],
        compiler_params=_cparams(("parallel",)),
    )(h, oa, ob, ga, gb, wa, wb, wo)


def _merge_bwd(dh, oa, ob, ga, gb, wa, wb, wo, exchange=None):
    t, d = dh.shape
    tm = _tile(t, 544)
    ni = t // tm
    n_x = len(exchange.ins) if exchange else 0

    def body(*refs):
        dh_ref, oa_ref, ob_ref, ga_ref, gb_ref, wa_ref, wb_ref, wo_ref = refs[:8]
        dya_ref, dyb_ref, doa_ref, dob_ref, dga_ref, dgb_ref, dhb_ref = refs[8 + n_x:15 + n_x]
        if exchange:
            i = pl.program_id(0)
            _host_exchange(exchange, refs[8:8 + n_x], refs[15 + n_x:15 + 2 * n_x], refs[15 + 2 * n_x:],
                           i == 0, i == ni - 1, i == ni - 1)
        dhb = dh_ref[...].astype(BF16)
        dhb_ref[...] = dhb
        dmix = lax.dot_general(dhb, wo_ref[...], NT_DIMS, preferred_element_type=F32)
        for o_ref, g_ref, w_ref, dy_ref, do_ref, dg_ref in (
                (oa_ref, ga_ref, wa_ref, dya_ref, doa_ref, dga_ref),
                (ob_ref, gb_ref, wb_ref, dyb_ref, dob_ref, dgb_ref)):
            y = jnp.dot(o_ref[...], w_ref[...], preferred_element_type=F32)
            s = jax.nn.sigmoid(g_ref[...])
            dy = (dmix * s).astype(BF16)
            dy_ref[...] = dy
            dg_ref[...] = ((dmix * y) * (s * (1.0 - s))).astype(BF16)
            do_ref[...] = lax.dot_general(dy, w_ref[...], NT_DIMS, preferred_element_type=F32).astype(BF16)

    row = lambda w: pl.BlockSpec((tm, w), lambda i: (i, 0))
    full = lambda a: pl.BlockSpec(a.shape, lambda i: (0, 0))
    wa_w, wb_w = oa.shape[1], ob.shape[1]
    outs = pl.pallas_call(
        body,
        name="merge_bwd",
        grid=(ni,),
        in_specs=[row(d), row(wa_w), row(wb_w), row(d), row(d), full(wa), full(wb), full(wo)] + [HBM_SPEC] * n_x,
        out_specs=[row(d), row(d), row(wa_w), row(wb_w), row(d), row(d), row(d)] + [HBM_SPEC] * n_x,
        out_shape=[
            jax.ShapeDtypeStruct((t, d), BF16), jax.ShapeDtypeStruct((t, d), BF16),
            jax.ShapeDtypeStruct((t, wa_w), BF16), jax.ShapeDtypeStruct((t, wb_w), BF16),
            jax.ShapeDtypeStruct((t, d), BF16), jax.ShapeDtypeStruct((t, d), BF16),
            jax.ShapeDtypeStruct((t, d), BF16),
        ] + (exchange.out_shape if exchange else []),
        scratch_shapes=exchange.scratch if exchange else [],
        compiler_params=_cparams(("arbitrary",) if exchange else ("parallel",)),
    )(dh, oa, ob, ga, gb, wa, wb, wo, *(exchange.ins if exchange else []))
    return outs[:7], outs[7:]


def _tri_dot(tri, x):
    hi = x.astype(BF16)
    r1 = x - hi.astype(F32)
    mid = r1.astype(BF16)
    lo = (r1 - mid.astype(F32)).astype(BF16)
    return (jnp.dot(tri, hi, preferred_element_type=F32)
            + jnp.dot(tri, mid, preferred_element_type=F32)
            + jnp.dot(tri, lo, preferred_element_type=F32))


def _forget_cumsum(f_logit, b_pad, nb):
    t, w = f_logit.shape
    bsz = t // (nb * BLOCK)

    def body(f_ref, b_ref, c_ref, carry):
        n = pl.program_id(1)

        @pl.when(n == 0)
        def _():
            carry[...] = jnp.zeros_like(carry)

        x = jax.nn.log_sigmoid(f_ref[...] + b_ref[...])
        rows = lax.broadcasted_iota(jnp.int32, (BLOCK, BLOCK), 0)
        cols = lax.broadcasted_iota(jnp.int32, (BLOCK, BLOCK), 1)
        tri = (cols <= rows).astype(BF16)
        c = _tri_dot(tri, x) + carry[...]
        c_ref[...] = c
        carry[...] = c[BLOCK - 1:BLOCK, :]

    return pl.pallas_call(
        body,
        name="forget_cumsum",
        grid=(bsz, nb),
        in_specs=[pl.BlockSpec((BLOCK, w), lambda b, n: (b * nb + n, 0)),
                  pl.BlockSpec((1, w), lambda b, n: (0, 0))],
        out_specs=pl.BlockSpec((BLOCK, w), lambda b, n: (b * nb + n, 0)),
        out_shape=jax.ShapeDtypeStruct((t, w), F32),
        scratch_shapes=[pltpu.VMEM((1, w), F32)],
        compiler_params=_cparams(("parallel", "arbitrary")),
    )(f_logit, b_pad)


def _forget_cumsum_bwd(dc, f_logit, b_pad, nb):
    t, w = f_logit.shape
    bsz = t // (nb * BLOCK)

    def body(dc_ref, f_ref, b_ref, df_ref, db_ref, carry):
        n = pl.program_id(1)

        @pl.when(n == 0)
        def _():
            carry[...] = jnp.zeros_like(carry)
            db_ref[...] = jnp.zeros_like(db_ref)

        rows = lax.broadcasted_iota(jnp.int32, (BLOCK, BLOCK), 0)
        cols = lax.broadcasted_iota(jnp.int32, (BLOCK, BLOCK), 1)
        tri = (cols >= rows).astype(BF16)
        dlf = _tri_dot(tri, dc_ref[...]) + carry[...]
        carry[...] = dlf[0:1, :]
        df = dlf * jax.nn.sigmoid(-(f_ref[...] + b_ref[...]))
        df_ref[...] = df.astype(BF16)
        db_ref[0] += jnp.sum(df, axis=0, keepdims=True)

    rev = lambda b, n: (b * nb + (nb - 1 - n), 0)
    return pl.pallas_call(
        body,
        name="forget_cumsum_bwd",
        grid=(bsz, nb),
        in_specs=[pl.BlockSpec((BLOCK, w), rev),
                  pl.BlockSpec((BLOCK, w), rev),
                  pl.BlockSpec((1, w), lambda b, n: (0, 0))],
        out_specs=[pl.BlockSpec((BLOCK, w), rev),
                   pl.BlockSpec((1, 1, w), lambda b, n: (b, 0, 0))],
        out_shape=[jax.ShapeDtypeStruct((t, w), BF16), jax.ShapeDtypeStruct((bsz, 1, w), F32)],
        scratch_shapes=[pltpu.VMEM((1, w), F32)],
        compiler_params=_cparams(("parallel", "arbitrary")),
    )(dc, f_logit, b_pad)


GROUP_ROWS = A_GROUP * BLOCK


def _stack_heads(ref, g):
    return jnp.concatenate([ref[:, (A_GROUP * g + i) * LANES:(A_GROUP * g + i + 1) * LANES] for i in range(A_GROUP)],
                           axis=0)


def _unstack_heads(ref, g, x):
    for i in range(A_GROUP):
        ref[:, (A_GROUP * g + i) * LANES:(A_GROUP * g + i + 1) * LANES] = x[i * BLOCK:(i + 1) * BLOCK].astype(ref.dtype)


def _swa_logits(q, keys, slope, n):
    qi = lax.broadcasted_iota(jnp.int32, (GROUP_ROWS, BLOCK), 0) & (BLOCK - 1)
    kj = lax.broadcasted_iota(jnp.int32, (GROUP_ROWS, BLOCK), 1)
    s_all = lax.dot_general(q, keys, NT_DIMS, preferred_element_type=F32) * SCALE
    out = []
    for i, (dist, ok) in enumerate((
            (n * BLOCK + qi - kj, (kj >= N_PAD) & (n * BLOCK + qi - kj >= 0)),
            (BLOCK + qi - kj, (kj > qi) & (n >= 2)),
            (qi - kj, (kj <= qi) & (n >= 1)))):
        s = s_all[:, i * BLOCK:(i + 1) * BLOCK] - slope * dist.astype(F32)
        out.append(jnp.where(ok, s, NEG))
    return out


def _three_blocks(m_ref, p_ref, c_ref):
    return jnp.concatenate([m_ref[...], p_ref[...], c_ref[...]], axis=0)


def _swa_specs(nb):
    row = lambda b, n: b * nb + n
    qspec = pl.BlockSpec((BLOCK, A_PAD_WIDTH), lambda b, n: (row(b, n), 0))
    kv_m = pl.BlockSpec((BLOCK, LANES), lambda b, n: (row(b, 0), 0))
    kv_p = pl.BlockSpec((BLOCK, LANES), lambda b, n: (row(b, jnp.maximum(n - 1, 0)), 0))
    kv_c = pl.BlockSpec((BLOCK, LANES), lambda b, n: (row(b, n), 0))
    rowspec = pl.BlockSpec((A_KV_HEADS, GROUP_ROWS, 1), lambda b, n: (0, 0, 0))
    lsespec = pl.BlockSpec((1, A_KV_HEADS, GROUP_ROWS, 1), lambda b, n: (row(b, n), 0, 0, 0))
    return qspec, kv_m, kv_p, kv_c, rowspec, lsespec


def _swa_fwd(q, k, v, sink_rows, slope_rows, nb):
    t = q.shape[0]
    bsz = t // (nb * BLOCK)

    def body(q_ref, km_ref, kp_ref, kc_ref, vm_ref, vp_ref, vc_ref, sink_ref, slope_ref, o_ref, lse_ref):
        n = pl.program_id(1)
        keys = _three_blocks(km_ref, kp_ref, kc_ref)
        values = _three_blocks(vm_ref, vp_ref, vc_ref)
        lane_group = lax.broadcasted_iota(jnp.int32, (GROUP_ROWS, LANES), 1) // HEAD_DIM
        for g in range(A_KV_HEADS):
            qq = _stack_heads(q_ref, g)
            sink = sink_ref[g]
            s_m, s_p, s_c = _swa_logits(qq, keys, slope_ref[g], n)
            m = jnp.maximum(jnp.max(jnp.maximum(jnp.maximum(s_m, s_p), s_c), axis=-1, keepdims=True), sink)
            m_wide = jnp.broadcast_to(m, (GROUP_ROWS, BLOCK))
            e_m = jnp.exp(s_m - m_wide)
            e_p = jnp.exp(s_p - m_wide)
            e_c = jnp.exp(s_c - m_wide)
            z = jnp.sum((e_m + e_p) + e_c, axis=-1, keepdims=True) + jnp.exp(sink - m)
            inv = jnp.broadcast_to(1.0 / z, (GROUP_ROWS, BLOCK))
            probs = jnp.concatenate([(e_m * inv).astype(BF16), (e_p * inv).astype(BF16), (e_c * inv).astype(BF16)],
                                    axis=1)
            o = jnp.dot(probs, values, preferred_element_type=F32)
            _unstack_heads(o_ref, g, jnp.where(lane_group == g, o, 0.0))
            lse_ref[0, g] = m + jnp.log(z)

    qspec, kv_m, kv_p, kv_c, rowspec, lsespec = _swa_specs(nb)
    return pl.pallas_call(
        body,
        name="swa_fwd",
        grid=(bsz, nb),
        in_specs=[qspec, kv_m, kv_p, kv_c, kv_m, kv_p, kv_c, rowspec, rowspec],
        out_specs=[qspec, lsespec],
        out_shape=[jax.ShapeDtypeStruct((t, A_PAD_WIDTH), BF16),
                   jax.ShapeDtypeStruct((t // BLOCK, A_KV_HEADS, GROUP_ROWS, 1), F32)],
        compiler_params=_cparams(("parallel", "arbitrary")),
    )(q, k, k, k, v, v, v, sink_rows, slope_rows)


def _swa_bwd(q, k, v, do, lse, sink_rows, slope_rows, nb):
    t = q.shape[0]
    l = nb * BLOCK
    bsz = t // l

    def body(q_ref, km_ref, kp_ref, kc_ref, vm_ref, vp_ref, vc_ref, do_ref, lse_ref, sink_ref, slope_ref,
             dq_ref, dk_ref, dv_ref, dsink_ref, dk_acc, dv_acc):
        n = pl.program_id(1)

        @pl.when(n == 0)
        def _():
            dk_acc[...] = jnp.zeros_like(dk_acc)
            dv_acc[...] = jnp.zeros_like(dv_acc)
            dsink_ref[...] = jnp.zeros_like(dsink_ref)

        keys = _three_blocks(km_ref, kp_ref, kc_ref)
        values = _three_blocks(vm_ref, vp_ref, vc_ref)
        first_half = lax.broadcasted_iota(jnp.int32, (BLOCK, LANES), 1) < HEAD_DIM
        prev = jnp.maximum(n - 1, 0)
        for g in range(A_KV_HEADS):
            qq = _stack_heads(q_ref, g)
            dob = _stack_heads(do_ref, g)
            lse = lse_ref[0, g]
            lse_wide = jnp.broadcast_to(lse, (GROUP_ROWS, BLOCK))
            probs = [jnp.exp(s - lse_wide) for s in _swa_logits(qq, keys, slope_ref[g], n)]
            dp_all = lax.dot_general(dob, values, NT_DIMS, preferred_element_type=F32)
            dps = [dp_all[:, i * BLOCK:(i + 1) * BLOCK] for i in range(3)]
            delta = jnp.sum((probs[0] * dps[0] + probs[1] * dps[1]) + probs[2] * dps[2], axis=-1, keepdims=True)
            delta_wide = jnp.broadcast_to(delta, (GROUP_ROWS, BLOCK))
            ds = jnp.concatenate([(p * (dp - delta_wide)).astype(BF16) for p, dp in zip(probs, dps)], axis=1)
            pb = jnp.concatenate([p.astype(BF16) for p in probs], axis=1)
            dq = jnp.dot(ds, keys, preferred_element_type=F32) * SCALE
            dk_all = lax.dot_general(ds, qq, TN_DIMS, preferred_element_type=F32) * SCALE
            dv_all = lax.dot_general(pb, dob, TN_DIMS, preferred_element_type=F32)
            for i, start in enumerate((0, prev * BLOCK, n * BLOCK)):
                rows = pl.ds(pl.multiple_of(start, BLOCK), BLOCK)
                dk_acc[rows, :] += dk_all[i * BLOCK:(i + 1) * BLOCK]
                dv_acc[rows, :] += dv_all[i * BLOCK:(i + 1) * BLOCK]
            for pair in range(A_GROUP // 2):
                even = dq[2 * pair * BLOCK:(2 * pair + 1) * BLOCK]
                odd = dq[(2 * pair + 1) * BLOCK:(2 * pair + 2) * BLOCK]
                left = even if g == 0 else pltpu.roll(even, HEAD_DIM, 1)
                right = pltpu.roll(odd, HEAD_DIM, 1) if g == 0 else odd
                tile = (A_GROUP // 2) * g + pair
                dq_ref[:, tile * LANES:(tile + 1) * LANES] = jnp.where(first_half, left, right).astype(BF16)
            dsink_ref[0, g] += -(jnp.exp(sink_ref[g] - lse) * delta)

        @pl.when(n == nb - 1)
        def _():
            dk_ref[...] = dk_acc[...].astype(BF16)
            dv_ref[...] = dv_acc[...].astype(BF16)

    qspec, kv_m, kv_p, kv_c, rowspec, lsespec = _swa_specs(nb)
    kv_all = pl.BlockSpec((l, LANES), lambda b, n: (b, 0))
    return pl.pallas_call(
        body,
        name="swa_bwd",
        grid=(bsz, nb),
        in_specs=[qspec, kv_m, kv_p, kv_c, kv_m, kv_p, kv_c, qspec, lsespec, rowspec, rowspec],
        out_specs=[pl.BlockSpec((BLOCK, A_WIDTH), lambda b, n: (b * nb + n, 0)), kv_all, kv_all,
                   pl.BlockSpec((1, A_KV_HEADS, GROUP_ROWS, 1), lambda b, n: (b, 0, 0, 0))],
        out_shape=[jax.ShapeDtypeStruct((t, A_WIDTH), BF16),
                   jax.ShapeDtypeStruct((t, LANES), BF16),
                   jax.ShapeDtypeStruct((t, LANES), BF16),
                   jax.ShapeDtypeStruct((bsz, A_KV_HEADS, GROUP_ROWS, 1), F32)],
        scratch_shapes=[pltpu.VMEM((l, LANES), F32), pltpu.VMEM((l, LANES), F32)],
        compiler_params=_cparams(("parallel", "arbitrary")),
    )(q, k, k, k, v, v, v, do, lse, sink_rows, slope_rows)


CHUNK = KEY_BLOCKS * BLOCK


def _fox_chunk(qb, ci):
    sb = jnp.maximum(jnp.minimum(KEY_BLOCKS * ci, qb + 1 - KEY_BLOCKS), 0)
    lo = jnp.maximum(ci * CHUNK, N_PAD)
    return sb, lo, pl.ds(pl.multiple_of(sb * BLOCK, BLOCK), CHUNK)


def _fox_logits(s_ref, cr_ref, e, j, sb, lo, qb):
    lane = lax.broadcasted_iota(jnp.int32, (BLOCK, BLOCK), 1)
    ahead = lane - lax.broadcasted_iota(jnp.int32, (BLOCK, BLOCK), 0)
    first = (sb + j) * BLOCK
    s = s_ref[e, :, j * BLOCK:(j + 1) * BLOCK] - cr_ref[e, sb + j]
    return jnp.where((ahead <= qb * BLOCK - first) & (lane >= lo - first), s, NEG)


FOX_PAIRS = 4
FOX_HEADS = 2 * FOX_PAIRS
FOX_STEPS = B_PAIRS // FOX_PAIRS


def _fox_specs(nb):
    l = nb * BLOCK
    q_spec = pl.BlockSpec((BLOCK, FOX_PAIRS * LANES), lambda b, p, i: (b * nb + i, p))
    kv_spec = pl.BlockSpec((l, FOX_HEADS * LANES), lambda b, p, i: (b, p))
    cc_spec = pl.BlockSpec((FOX_HEADS, BLOCK, 1), lambda b, p, i: (b * FOX_STEPS + p, i, 0))
    cr_spec = pl.BlockSpec((FOX_HEADS, nb, 1, BLOCK), lambda b, p, i: (b * FOX_STEPS + p, 0, 0, 0))
    return q_spec, kv_spec, cc_spec, cr_spec


def _fox_fwd(q, k, v, c_row, nb, exchange=None):
    t = q.shape[0]
    bsz = t // (nb * BLOCK)
    assert nb >= KEY_BLOCKS

    n_x = len(exchange.ins) if exchange else 0

    def body(*refs):
        q_ref, k_ref, v_ref, cr_ref = refs[:4]
        o_ref, ox_ref, lse_ref = refs[4 + n_x:7 + n_x]
        s_scr, hi_scr, lo_scr = refs[7 + 2 * n_x:10 + 2 * n_x]
        qb = pl.program_id(2)
        if exchange:
            first = (pl.program_id(0) == 0) & (pl.program_id(1) == 0)
            last = (pl.program_id(0) == bsz - 1) & (pl.program_id(1) == FOX_STEPS - 1)
            _host_exchange(exchange, refs[4:4 + n_x], refs[7 + n_x:7 + 2 * n_x], refs[10 + 2 * n_x:],
                           first & (qb == 0), last & (qb == 0), last & (qb == nb - 1))
        qs = [q_ref[:, a * LANES:(a + 1) * LANES] * SCALE for a in range(FOX_PAIRS)]
        first_half = lax.broadcasted_iota(jnp.int32, (BLOCK, LANES), 1) < HEAD_DIM

        def step(ci, carry):
            stats, accs = carry[:2 * FOX_HEADS], carry[2 * FOX_HEADS:]
            sb, lo, krows = _fox_chunk(qb, ci)
            new_stats, new_accs = [], []
            for a in range(FOX_PAIRS):
                alphas = []
                pv = jnp.zeros((BLOCK, LANES), F32)
                pv_lo = jnp.zeros((BLOCK, LANES), F32)
                for e in (2 * a, 2 * a + 1):
                    m, z = stats[2 * e], stats[2 * e + 1]
                    tile = slice(e * LANES, (e + 1) * LANES)
                    s_scr[e] = lax.dot_general(qs[a], k_ref[krows, tile], NT_DIMS, preferred_element_type=F32)
                    top = None
                    for j in range(KEY_BLOCKS):
                        s = _fox_logits(s_scr, cr_ref, e, j, sb, lo, qb)
                        s_scr[e, :, j * BLOCK:(j + 1) * BLOCK] = s
                        top = s if top is None else jnp.maximum(top, s)
                    m_new = jnp.maximum(m, jnp.max(top, axis=-1, keepdims=True))
                    alpha = jnp.exp(m - m_new)
                    m_wide = jnp.broadcast_to(m_new, (BLOCK, BLOCK))
                    total = None
                    for j in range(KEY_BLOCKS):
                        cols = slice(j * BLOCK, (j + 1) * BLOCK)
                        p = jnp.exp(s_scr[e, :, cols] - m_wide)
                        total = p if total is None else total + p
                        hi = p.astype(BF16)
                        hi_scr[e, :, cols] = hi
                        lo_scr[e, :, cols] = (p - hi.astype(F32)).astype(BF16)
                    z = alpha * z + jnp.sum(total, axis=-1, keepdims=True)
                    vv = v_ref[krows, tile]
                    pv = pv + jnp.dot(hi_scr[e], vv, preferred_element_type=F32)
                    pv_lo = pv_lo + jnp.dot(lo_scr[e], vv, preferred_element_type=F32)
                    new_stats += [m_new, z]
                    alphas.append(alpha)
                alpha = jnp.where(first_half, alphas[0], alphas[1])
                new_accs += [alpha * accs[2 * a] + pv, alpha * accs[2 * a + 1] + pv_lo]
            return (*new_stats, *new_accs)

        col = lambda val: jnp.full((BLOCK, 1), val, F32)
        done = lax.fori_loop(
            0, (qb + KEY_BLOCKS) // KEY_BLOCKS, step,
            (col(NEG), col(0.0)) * FOX_HEADS + (jnp.zeros((BLOCK, LANES), F32),) * (2 * FOX_PAIRS))
        for a in range(FOX_PAIRS):
            m0, z0, m1, z1 = done[4 * a:4 * a + 4]
            acc, acc_lo = done[2 * FOX_HEADS + 2 * a:2 * FOX_HEADS + 2 * a + 2]
            inv = 1.0 / jnp.where(first_half, z0, z1)
            tile = slice(a * LANES, (a + 1) * LANES)
            o_ref[:, tile] = (acc * inv).astype(BF16)
            ox_ref[:, tile] = (acc + acc_lo) * inv
            lse_ref[2 * a] = m0 + jnp.log(z0)
            lse_ref[2 * a + 1] = m1 + jnp.log(z1)

    q_spec, kv_spec, cc_spec, cr_spec = _fox_specs(nb)
    outs = pl.pallas_call(
        body,
        name="fox_fwd",
        grid=(bsz, FOX_STEPS, nb),
        in_specs=[q_spec, kv_spec, kv_spec, cr_spec] + [HBM_SPEC] * n_x,
        out_specs=[q_spec, q_spec, cc_spec] + [HBM_SPEC] * n_x,
        out_shape=[jax.ShapeDtypeStruct((t, B_WIDTH), BF16), jax.ShapeDtypeStruct((t, B_WIDTH), F32),
                   jax.ShapeDtypeStruct((bsz * B_HEADS, nb * BLOCK, 1), F32)] + (exchange.out_shape if exchange else []),
        scratch_shapes=[pltpu.VMEM((FOX_HEADS, BLOCK, CHUNK), F32), pltpu.VMEM((FOX_HEADS, BLOCK, CHUNK), BF16),
                        pltpu.VMEM((FOX_HEADS, BLOCK, CHUNK), BF16)] + (exchange.scratch if exchange else []),
        compiler_params=_cparams(("arbitrary",) * 3 if exchange else ("parallel", "parallel", "arbitrary")),
    )(q, k, v, c_row, *(exchange.ins if exchange else []))
    return outs[:3], outs[3:]


def _fox_bwd(q, k, v, o_exact, do, lse, c_row, nb, exchange=None):
    t = q.shape[0]
    l = nb * BLOCK
    bsz = t // l

    n_x = len(exchange.ins) if exchange else 0

    def body(*refs):
        q_ref, k_ref, v_ref, ox_ref, do_ref, lse_ref, cr_ref = refs[:7]
        dq_ref, dk_ref, dv_ref, dc_ref = refs[7 + n_x:11 + n_x]
        dk_acc, dv_acc, s_scr, dp_scr, p_scr, ds_scr = refs[11 + 2 * n_x:17 + 2 * n_x]
        qb = pl.program_id(2)
        if exchange:
            first = (pl.program_id(0) == 0) & (pl.program_id(1) == 0)
            last = (pl.program_id(0) == bsz - 1) & (pl.program_id(1) == FOX_STEPS - 1)
            _host_exchange(exchange, refs[7:7 + n_x], refs[11 + n_x:11 + 2 * n_x], refs[17 + 2 * n_x:],
                           first & (qb == 0), last & (qb == 0), last & (qb == nb - 1))

        @pl.when(qb == 0)
        def _():
            dk_acc[...] = jnp.zeros_like(dk_acc)
            dv_acc[...] = jnp.zeros_like(dv_acc)
            dc_ref[...] = jnp.zeros_like(dc_ref)

        top_half = lax.broadcasted_iota(jnp.int32, (LANES, BLOCK), 0) < HEAD_DIM
        pair_t = lambda x: jnp.concatenate([jnp.where(top_half, x.T, 0), jnp.where(top_half, 0, x.T)], axis=1)
        first_half = lax.broadcasted_iota(jnp.int32, (BLOCK, LANES), 1) < HEAD_DIM
        wide = lambda col: jnp.broadcast_to(col, (BLOCK, BLOCK))
        qs, dobs, qs_t, dob_t, deltas = [], [], [], [], []
        for a in range(FOX_PAIRS):
            tile = slice(a * LANES, (a + 1) * LANES)
            qs.append(q_ref[:, tile] * SCALE)
            dobs.append(do_ref[:, tile])
            qs_t.append(pair_t(qs[a]))
            dob_t.append(pair_t(dobs[a]))
            weighted = dobs[a].astype(F32) * ox_ref[:, tile]
            deltas += [wide(jnp.sum(jnp.where(first_half, weighted, 0.0), axis=-1, keepdims=True)),
                       wide(jnp.sum(jnp.where(first_half, 0.0, weighted), axis=-1, keepdims=True))]
        lses = [wide(lse_ref[e]) for e in range(FOX_HEADS)]

        def step(ci, dqs):
            sb, lo, krows = _fox_chunk(qb, ci)
            dqs = list(dqs)
            for a in range(FOX_PAIRS):
                for e in (2 * a, 2 * a + 1):
                    tile = slice(e * LANES, (e + 1) * LANES)
                    kk = k_ref[krows, tile]
                    s_scr[e] = lax.dot_general(qs[a], kk, NT_DIMS, preferred_element_type=F32)
                    dp_scr[e] = lax.dot_general(dobs[a], v_ref[krows, tile], NT_DIMS, preferred_element_type=F32)
                    for j in range(KEY_BLOCKS):
                        cols = slice(j * BLOCK, (j + 1) * BLOCK)
                        p = jnp.exp(_fox_logits(s_scr, cr_ref, e, j, sb, lo, qb) - lses[e])
                        ds = p * (dp_scr[e, :, cols] - deltas[e])
                        dc_ref[e, sb + j] -= jnp.sum(ds, axis=0, keepdims=True)
                        p_scr[e, :, cols] = p.astype(BF16)
                        ds_scr[e, :, cols] = ds.astype(BF16)
                    dqs[a] = dqs[a] + jnp.dot(ds_scr[e], kk, preferred_element_type=F32)
                both = slice(2 * a, 2 * a + 2)
                dk_t = jnp.dot(qs_t[a], ds_scr[both].reshape(2 * BLOCK, CHUNK), preferred_element_type=F32)
                dv_t = jnp.dot(dob_t[a], p_scr[both].reshape(2 * BLOCK, CHUNK), preferred_element_type=F32)
                for j in range(KEY_BLOCKS):
                    cols = slice(j * BLOCK, (j + 1) * BLOCK)
                    dk_acc[a * nb + sb + j] += dk_t[:, cols]
                    dv_acc[a * nb + sb + j] += dv_t[:, cols]
            return tuple(dqs)

        dqs = lax.fori_loop(0, (qb + KEY_BLOCKS) // KEY_BLOCKS, step,
                            (jnp.zeros((BLOCK, LANES), F32),) * FOX_PAIRS)
        for a in range(FOX_PAIRS):
            dq_ref[:, a * LANES:(a + 1) * LANES] = (dqs[a] * SCALE).astype(BF16)

        @pl.when(qb == nb - 1)
        def _():
            for a in range(FOX_PAIRS):
                for kb in range(nb):
                    rows = slice(kb * BLOCK, (kb + 1) * BLOCK)
                    for acc, out_ref in ((dk_acc, dk_ref), (dv_acc, dv_ref)):
                        out_ref[rows, a * LANES:(a + 1) * LANES] = acc[a * nb + kb].T.astype(BF16)

    q_spec, kv_spec, cc_spec, cr_spec = _fox_specs(nb)
    dkv_spec = pl.BlockSpec((l, FOX_PAIRS * LANES), lambda b, p, i: (b, p))
    outs = pl.pallas_call(
        body,
        name="fox_bwd",
        grid=(bsz, FOX_STEPS, nb),
        in_specs=[q_spec, kv_spec, kv_spec, q_spec, q_spec, cc_spec, cr_spec] + [HBM_SPEC] * n_x,
        out_specs=[q_spec, dkv_spec, dkv_spec, cr_spec] + [HBM_SPEC] * n_x,
        out_shape=[jax.ShapeDtypeStruct((t, B_WIDTH), BF16), jax.ShapeDtypeStruct((t, B_WIDTH), BF16),
                   jax.ShapeDtypeStruct((t, B_WIDTH), BF16),
                   jax.ShapeDtypeStruct((bsz * B_HEADS, nb, 1, BLOCK), F32)] + (exchange.out_shape if exchange else []),
        scratch_shapes=[pltpu.VMEM((FOX_PAIRS * nb, LANES, BLOCK), F32), pltpu.VMEM((FOX_PAIRS * nb, LANES, BLOCK), F32),
                        pltpu.VMEM((FOX_HEADS, BLOCK, CHUNK), F32), pltpu.VMEM((FOX_HEADS, BLOCK, CHUNK), F32),
                        pltpu.VMEM((FOX_HEADS, BLOCK, CHUNK), BF16), pltpu.VMEM((FOX_HEADS, BLOCK, CHUNK), BF16)]
        + (exchange.scratch if exchange else []),
        compiler_params=_cparams(("arbitrary",) * 3 if exchange else ("parallel", "parallel", "arbitrary")),
    )(q, k, v, o_exact, do, lse, c_row, *(exchange.ins if exchange else []))
    return outs[:4], outs[4:]


def _loss_head(h, final_w, target):
    bsz, l, d = h.shape
    nb = l // BLOCK

    def body(h_ref, w_ref, t_ref, loss_ref, dh_ref, dw_ref):
        b = pl.program_id(0)
        n = pl.program_id(1)

        @pl.when((b == 0) & (n == 0))
        def _():
            loss_ref[...] = jnp.zeros_like(loss_ref)
            dw_ref[...] = jnp.zeros_like(dw_ref)

        @pl.when(n == 0)
        def _():
            dh_ref[...] = jnp.zeros_like(dh_ref)

        @pl.when(n > 0)
        def _():
            hh = h_ref[0]
            w = w_ref[...]
            r = _rms_scale(hh)
            err = (hh * r) * w - t_ref[0]
            loss_ref[...] += 0.5 * jnp.sum(jnp.mean(err * err, axis=-1, keepdims=True), axis=0, keepdims=True)
            dy = err * (1.0 / d)
            dh, dw = _rms_bwd(dy, hh, w)
            dh_ref[0] = dh
            dw_ref[...] += dw

    return pl.pallas_call(
        body,
        name="loss_head",
        grid=(bsz, nb),
        in_specs=[
            pl.BlockSpec((1, BLOCK, d), lambda b, n: (b, n, 0)),
            pl.BlockSpec((1, d), lambda b, n: (0, 0)),
            pl.BlockSpec((1, BLOCK, d), lambda b, n: (b, jnp.maximum(n - 1, 0), 0)),
        ],
        out_specs=[
            pl.BlockSpec((1, 128), lambda b, n: (0, 0)),
            pl.BlockSpec((1, BLOCK, d), lambda b, n: (b, n, 0)),
            pl.BlockSpec((1, d), lambda b, n: (0, 0)),
        ],
        out_shape=[jax.ShapeDtypeStruct((1, 128), F32), jax.ShapeDtypeStruct((bsz, l, d), F32),
                   jax.ShapeDtypeStruct((1, d), F32)],
        compiler_params=_cparams(("arbitrary", "arbitrary")),
    )(h, final_w, target)


def _pad_tiles(w, src, heads, lane_slot, axis):
    pieces = []
    for h in range(heads):
        x = lax.slice_in_dim(w, src + HEAD_DIM * h, src + HEAD_DIM * (h + 1), axis=axis)
        z = jnp.zeros_like(x)
        pieces += [x, z] if lane_slot(h) == 0 else [z, x]
    return pieces


def _unpad_tiles(g, off, heads, lane_slot, axis):
    return [lax.slice_in_dim(g, off + LANES * h + HEAD_DIM * lane_slot(h),
                             off + LANES * h + HEAD_DIM * (lane_slot(h) + 1), axis=axis) for h in range(heads)]


def _layout_w_in(w):
    pad_f = jnp.zeros((w.shape[0], F_COLS - B_HEADS), w.dtype)
    return jnp.concatenate([w[:, :SRC_F], w[:, SRC_GA:], w[:, SRC_F:SRC_GA], pad_f], axis=1)


def _unlayout_w_in(g):
    return jnp.concatenate([g[:, :OFF_GA], g[:, OFF_F:OFF_F + B_HEADS], g[:, OFF_GA:OFF_F]], axis=1)


def _local_step(x, target, meta, norms, b_forget, sinks, w, comm=None):
    n1, nmix, n2, nfin = norms
    w1i, w1o = w[:2]
    bsz, seq, d = x.shape
    l = PREFIX + seq
    nb = l // BLOCK
    t = bsz * l

    h0 = jnp.concatenate([jnp.zeros((bsz, N_PAD, d), F32),
                          jnp.broadcast_to(meta[None], (bsz, N_META, d)), x], axis=1).reshape(t, d)

    if comm is None:
        (h1, g1, u1), _ = _ffn_fwd(h0, n1, w1i, w1o)
        w_in, wa, wb, wo, w2i, w2o = w[2:]
    else:
        (h1, g1, u1), gathered = _ffn_fwd(h0, n1, w1i, w1o, comm.gather(GATHER_PROJ))
        w_in, = comm.gathered(GATHER_PROJ, gathered)
    wp = _layout_w_in(w_in)
    un, qa, ka, va, qb, kb, vb, ga, gb, f_logit = _proj_fwd(h1, nmix, wp)
    b_pad = jnp.concatenate([b_forget, jnp.zeros((1, F_COLS - B_HEADS), F32)], axis=1)
    c = _forget_cumsum(f_logit, b_pad, nb)
    c_heads = c[:, :B_HEADS].reshape(bsz, l, B_HEADS).transpose(0, 2, 1).reshape(bsz * B_HEADS, l)
    c_row = c_heads.reshape(bsz * B_HEADS, nb, 1, BLOCK)

    slopes = jnp.exp2(-8.0 * jnp.arange(1, A_HEADS + 1, dtype=F32) / A_HEADS)
    slope_rows = jnp.repeat(slopes.reshape(A_KV_HEADS, A_GROUP), BLOCK, axis=1)[:, :, None]
    sink_rows = jnp.repeat(sinks.reshape(A_KV_HEADS, A_GROUP), BLOCK, axis=1)[:, :, None]

    oa, lse_a = _swa_fwd(qa, ka, va, sink_rows, slope_rows, nb)
    if comm is None:
        (ob, ob_exact, lse_b), _ = _fox_fwd(qb, kb, vb, c_row, nb)
    else:
        (ob, ob_exact, lse_b), gathered = _fox_fwd(qb, kb, vb, c_row, nb, comm.gather(GATHER_LATE))
        wa, wb, wo, w2i, w2o = comm.gathered(GATHER_LATE, gathered)
    wa_p = jnp.concatenate(_pad_tiles(wa, 0, A_HEADS, A_SLOT, 0), axis=0)
    h2, mixed = _merge_fwd(h1, oa, ob, ga, gb, wa_p, wb, wo)
    (h3, g2, u2), _ = _ffn_fwd(h2, n2, w2i, w2o)
    loss, dh3, d_nfin = _loss_head(h3.reshape(bsz, l, d), nfin, target)

    (dh2, n2b, a2, dgu2, df2, dn2_parts), _ = _ffn_bwd(dh3.reshape(t, d), h2, n2, g2, u2, w2i, w2o)
    g_w2o = _tn_matmul(a2, df2, "grad_ffn2_w_out")
    g_w2i = _tn_matmul(n2b, dgu2, "grad_ffn2_w_in")

    hosted = comm.swap("ffn2", dict(ffn2_w_in=g_w2i, ffn2_w_out=g_w2o)) if comm else None
    (dya, dyb, doa, dob, dga, dgb, dh2b), swapped = _merge_bwd(dh2, oa, ob, ga, gb, wa_p, wb, wo, hosted)
    g_wo = _tn_matmul(mixed, dh2b, "grad_w_out")
    g_wa = jnp.concatenate(_unpad_tiles(_tn_matmul(oa, dya, "grad_w_branch_a"), 0, A_HEADS, A_SLOT, 0), axis=0)
    g_wb = _tn_matmul(ob, dyb, "grad_w_branch_b")

    dqa, dka, dva, dsink_rows = _swa_bwd(qa, ka, va, doa, lse_a, sink_rows, slope_rows, nb)
    hosted = comm.scatter("ffn2", swapped) if comm else None
    (dqb, dkb, dvb, dc_row), pieces = _fox_bwd(qb, kb, vb, ob_exact, dob, lse_b, c_row, nb, hosted)
    if comm:
        comm.received("ffn2", pieces)
    dc = dc_row.reshape(bsz, B_HEADS, l).transpose(0, 2, 1).reshape(t, B_HEADS)
    dc = jnp.concatenate([dc, jnp.zeros((t, F_COLS - B_HEADS), F32)], axis=1)
    df_logit, db_parts = _forget_cumsum_bwd(dc, f_logit, b_pad, nb)

    dproj = jnp.concatenate([dqa, dka, dva, dqb, dkb, dvb, dga, dgb, df_logit], axis=1)
    g_win = _unlayout_w_in(_tn_matmul(un, dproj, "grad_w_in"))
    hosted = comm.swap("mixer", dict(w_in=g_win, w_branch_a=g_wa, w_branch_b=g_wb, w_out=g_wo)) if comm else None
    (dh1, dnmix_parts), swapped = _proj_bwd(dh2, h1, nmix, dproj, wp, hosted)
    hosted = comm.scatter("mixer", swapped) if comm else None
    (dh0, n1b, a1, dgu1, df1, dn1_parts), pieces = _ffn_bwd(dh1, h0, n1, g1, u1, w1i, w1o, hosted)
    dh0 = dh0.reshape(bsz, l, d)
    grad_x = dh0[:, PREFIX:]
    small = dict(
        meta_tokens=jnp.sum(dh0[:, N_PAD:PREFIX], axis=0),
        ffn1_norm=jnp.sum(dn1_parts, axis=0),
        mix_norm=jnp.sum(dnmix_parts, axis=0),
        ffn2_norm=jnp.sum(dn2_parts, axis=0),
        final_norm=d_nfin,
        b_forget=jnp.sum(db_parts, axis=0)[:, :B_HEADS],
        attn_sinks=jnp.sum(dsink_rows.reshape(bsz, A_HEADS, BLOCK), axis=(0, 2)).reshape(1, A_HEADS),
    )
    if comm is None:
        g_w1o = _tn_matmul(a1, df1, "grad_ffn1_w_out")
        g_w1i = _tn_matmul(n1b, dgu1, "grad_ffn1_w_in")
    else:
        comm.received("mixer", pieces)
        g_w1o, gathered = _tn_matmul(a1, df1, "grad_ffn1_w_out", comm.small_gather(loss, small))
        comm.small_gathered(gathered)
        swapped = _run_exchange(comm.swap("ffn1_out", dict(ffn1_w_out=g_w1o)), "exchange_halves_ffn1_out")
        g_w1i, pieces = _tn_matmul(n1b, dgu1, "grad_ffn1_w_in", comm.scatter("ffn1_out", swapped))
        comm.received("ffn1_out", pieces)
    big = dict(ffn1_w_in=g_w1i, ffn1_w_out=g_w1o, w_in=g_win, w_branch_a=g_wa, w_branch_b=g_wb,
               w_out=g_wo, ffn2_w_in=g_w2i, ffn2_w_out=g_w2o)
    return loss, grad_x, small, big


BIG = (
    ("ffn1_w_in", (D_MODEL, 5632), 1),
    ("ffn1_w_out", (2816, D_MODEL), 0),
    ("w_in", (D_MODEL, W_IN_COLS), 1),
    ("w_branch_a", (A_WIDTH, D_MODEL), 1),
    ("w_branch_b", (B_WIDTH, D_MODEL), 1),
    ("w_out", (D_MODEL, D_MODEL), 0),
    ("ffn2_w_in", (D_MODEL, 5632), 1),
    ("ffn2_w_out", (2816, D_MODEL), 0),
)
STACKED = "w_in"


def _coords():
    return lax.axis_index("x"), lax.axis_index("y"), lax.axis_index("c")


def _other_chips(x, y):
    return ((1 - x, y), (x, 1 - y), (1 - x, 1 - y))


def _chip_part(ref, name, shape, axis, k):
    if name == STACKED:
        return ref.at[k]
    size = shape[axis] // N_CHIPS
    start = pl.multiple_of(k * size, size)
    return ref.at[pl.ds(start, size), :] if axis == 0 else ref.at[:, pl.ds(start, size)]


def _full_shape(name, shape):
    return (N_CHIPS, shape[0], shape[1] // N_CHIPS) if name == STACKED else shape


class _Exchange:
    def __init__(self, ins, out_shape, n_sems, ops):
        self.ins, self.out_shape, self.n_sems, self.ops = list(ins), list(out_shape), n_sems, ops

    @property
    def scratch(self):
        return [pltpu.SemaphoreType.DMA((self.n_sems,)), pltpu.SemaphoreType.DMA((self.n_sems,))]


SEMS_PER_GATHER = 7


def _gather_exchange(shards, table):
    n = len(table)

    def ops(ins, outs, send_sems, recv_sems):
        x, y, c = _coords()
        mine = 2 * x + y
        sibling = (x, y, 1 - c)
        chips = _other_chips(x, y)
        slots = [2 * chip[0] + chip[1] for chip in chips]

        def part(i, k):
            name, shape, axis = table[i][:3]
            return _chip_part(outs[i], name, shape, axis, k)

        def half(ref, h):
            rows = ref.shape[0] // 2
            return ref.at[pl.ds(pl.multiple_of(h * rows, rows), rows), :]

        def own(i):
            sem = SEMS_PER_GATHER * i
            return pltpu.make_async_remote_copy(ins[i], part(i, mine), send_sems.at[sem], recv_sems.at[sem],
                                                device_id=sibling, device_id_type=MESH_ID)

        def fetch(i, j, slot):
            sem = SEMS_PER_GATHER * i + 1 + j
            if table[i][4]:
                src, dst = half(ins[i], c), half(part(i, slot), c)
            else:
                src, dst = ins[i], part(i, slot)
            return pltpu.make_async_remote_copy(src, dst, send_sems.at[sem], recv_sems.at[sem],
                                                device_id=(chips[j][0], chips[j][1], c), device_id_type=MESH_ID)

        def forward(i, j, h):
            sem = SEMS_PER_GATHER * i + 4 + j
            region = half(part(i, slots[j]), h)
            return pltpu.make_async_remote_copy(region, region, send_sems.at[sem], recv_sems.at[sem],
                                                device_id=sibling, device_id_type=MESH_ID)

        def start():
            for i in range(n):
                for j in range(3):
                    fetch(i, j, mine).start()
            for i in range(n):
                own(i).start()

        def relay():
            for i in range(n):
                for j in range(3):
                    fetch(i, j, slots[j]).wait_recv()
                    if table[i][4]:
                        forward(i, j, c).start()

        def finish():
            for i in range(n):
                own(i).wait()
                for j in range(3):
                    if table[i][4]:
                        forward(i, j, 1 - c).wait_recv()
                        forward(i, j, c).wait_send()
                    fetch(i, j, mine).wait_send()

        return start, relay, finish

    out_shape = [jax.ShapeDtypeStruct(_full_shape(name, shape), dtype) for name, shape, _, dtype, _ in table]
    return _Exchange(shards, out_shape, SEMS_PER_GATHER * n, ops)


def _run_exchange(exchange, name):
    n = len(exchange.ins)

    def body(*refs):
        start, relay, finish = exchange.ops(refs[:n], refs[n:2 * n], *refs[2 * n:])
        start()
        relay()
        finish()

    return pl.pallas_call(
        body,
        name=name,
        in_specs=[HBM_SPEC] * n,
        out_specs=[HBM_SPEC] * n,
        out_shape=exchange.out_shape,
        scratch_shapes=exchange.scratch,
    )(*exchange.ins)


def _host_exchange(exchange, in_refs, out_refs, sem_refs, first, middle, last):
    start, relay, finish = exchange.ops(in_refs, out_refs, *sem_refs)
    pl.when(first)(start)
    pl.when(middle)(relay)
    pl.when(last)(finish)


def _halves_view(name, shape, axis):
    r, c = shape
    if name == STACKED:
        return (N_CHIPS, 2, r // 2, c // N_CHIPS), lambda ref, h: ref.at[:, h]
    if axis == 1:
        return (2, r // 2, c), lambda ref, h: ref.at[h]
    return (N_CHIPS, 2, r // N_CHIPS // 2, c), lambda ref, h: ref.at[:, h]


def _halves_exchange(grads, entries):
    n_w = len(entries)
    views = [_halves_view(*entry) for entry in entries]

    def ops(ins, outs, send_sems, recv_sems):
        x, y, c = _coords()
        copies = [pltpu.make_async_remote_copy(views[i][1](ins[i], 1 - c), outs[i], send_sems.at[i], recv_sems.at[i],
                                               device_id=(x, y, 1 - c), device_id_type=MESH_ID) for i in range(n_w)]

        def start():
            for cp in copies:
                cp.start()

        def finish():
            for cp in copies:
                cp.wait()

        return start, lambda: None, finish

    half_shape = lambda v: tuple(d for i, d in enumerate(v) if i != (1 if len(v) == 4 else 0))
    out_shape = [jax.ShapeDtypeStruct(half_shape(v[0]), F32) for v in views]
    return _Exchange([g.reshape(v[0]) for g, v in zip(grads, views)], out_shape, n_w, ops)


def _add_sibling(g_view, recv, c, name):
    shape = recv.shape
    if len(shape) == 2:
        tr = _tile(shape[0], 128, 16)
        grid = (shape[0] // tr,)
        g_spec = pl.BlockSpec((None, tr, shape[1]), lambda i, c_ref: (c_ref[0], i, 0))
        r_spec = pl.BlockSpec((tr, shape[1]), lambda i, c_ref: (i, 0))
    else:
        tr = _tile(shape[1], 256, 16)
        grid = (N_CHIPS, shape[1] // tr)
        g_spec = pl.BlockSpec((None, None, tr, shape[2]), lambda k, i, c_ref: (k, c_ref[0], i, 0))
        r_spec = pl.BlockSpec((None, tr, shape[2]), lambda k, i, c_ref: (k, i, 0))

    def body(c_ref, g_ref, r_ref, o_ref):
        o_ref[...] = (g_ref[...] + r_ref[...]).astype(BF16)

    return pl.pallas_call(
        body,
        name="add_sibling_" + name,
        grid_spec=pltpu.PrefetchScalarGridSpec(num_scalar_prefetch=1, grid=grid, in_specs=[g_spec, r_spec],
                                               out_specs=r_spec),
        out_shape=jax.ShapeDtypeStruct(shape, BF16),
        compiler_params=_cparams(("parallel",) * len(grid)),
    )(c, g_view, recv)


def _piece_of(ref, name, axis, k):
    if name == STACKED or axis == 0:
        return ref.at[k]
    size = ref.shape[1] // N_CHIPS
    return ref.at[:, pl.ds(pl.multiple_of(k * size, size), size)]


def _piece_shape(name, shape, axis):
    r, c = shape
    return (r // 2, c // N_CHIPS) if (axis == 1) else (r // N_CHIPS // 2, c)


def _scatter_exchange(partials, entries):
    n_w = len(entries)

    def ops(ins, outs, send_sems, recv_sems):
        x, y, c = _coords()
        chips = _other_chips(x, y)
        copies = []
        for i, (name, _, axis) in enumerate(entries):
            for j, chip in enumerate(chips):
                sem = 3 * i + j
                copies.append(pltpu.make_async_remote_copy(
                    _piece_of(ins[i], name, axis, 2 * chip[0] + chip[1]), outs[i].at[j], send_sems.at[sem],
                    recv_sems.at[sem], device_id=(chip[0], chip[1], c), device_id_type=MESH_ID))

        def start():
            for cp in copies:
                cp.start()

        def finish():
            for cp in copies:
                cp.wait()

        return start, lambda: None, finish

    out_shape = [jax.ShapeDtypeStruct((3,) + _piece_shape(*entry), BF16) for entry in entries]
    return _Exchange(partials, out_shape, 3 * n_w, ops)


def _add_chips(partial, recv, mine, name, axis):
    rows, cols = recv.shape[1:]
    tr = _tile(rows, 256, 16)
    if name == STACKED or axis == 0:
        p_spec = pl.BlockSpec((None, tr, cols), lambda i, k_ref: (k_ref[0], i, 0))
    else:
        p_spec = pl.BlockSpec((tr, cols), lambda i, k_ref: (i, k_ref[0]))

    def body(k_ref, p_ref, r_ref, o_ref):
        f32 = lambda a: a.astype(F32)
        o_ref[...] = ((f32(p_ref[...]) + f32(r_ref[0])) + f32(r_ref[1])) + f32(r_ref[2])

    return pl.pallas_call(
        body,
        name="add_chips_" + name,
        grid_spec=pltpu.PrefetchScalarGridSpec(
            num_scalar_prefetch=1, grid=(rows // tr,),
            in_specs=[p_spec, pl.BlockSpec((3, tr, cols), lambda i, k_ref: (0, i, 0))],
            out_specs=pl.BlockSpec((tr, cols), lambda i, k_ref: (i, 0))),
        out_shape=jax.ShapeDtypeStruct((rows, cols), F32),
        compiler_params=_cparams(("parallel",)),
    )(mine, partial, recv)


def _share_with_sibling(halves):
    n_w = len(halves)

    def body(*refs):
        ins, outs = refs[:n_w], refs[n_w:2 * n_w]
        send_sems, recv_sems = refs[2 * n_w:]
        x, y, c = _coords()
        copies = [pltpu.make_async_remote_copy(ins[i], outs[i], send_sems.at[i], recv_sems.at[i],
                                               device_id=(x, y, 1 - c), device_id_type=MESH_ID) for i in range(n_w)]
        for cp in copies:
            cp.start()
        for cp in copies:
            cp.wait()

    return pl.pallas_call(
        body,
        name="share_with_sibling",
        in_specs=[HBM_SPEC] * n_w,
        out_specs=[HBM_SPEC] * n_w,
        out_shape=[jax.ShapeDtypeStruct(h.shape, F32) for h in halves],
        scratch_shapes=[pltpu.SemaphoreType.DMA((n_w,)), pltpu.SemaphoreType.DMA((n_w,))],
    )(*halves)


SMALL_ROWS = 168


def _small_exchange(buf):
    def ops(ins, outs, send_sems, recv_sems):
        x, y, c = _coords()
        me = 4 * x + 2 * y + c
        peers = [(x ^ fx, y ^ fy, c ^ fc) for fx in (0, 1) for fy in (0, 1) for fc in (0, 1)][1:]

        def copy(j, slot, dev):
            return pltpu.make_async_remote_copy(ins[0], outs[0].at[slot], send_sems.at[j], recv_sems.at[j],
                                                device_id=dev, device_id_type=MESH_ID)

        own = pltpu.make_async_copy(ins[0], outs[0].at[me], send_sems.at[N_DEV - 1])

        def start():
            own.start()
            for j, dev in enumerate(peers):
                copy(j, me, dev).start()

        def finish():
            for j, dev in enumerate(peers):
                copy(j, 4 * dev[0] + 2 * dev[1] + dev[2], dev).wait()
            own.wait()

        return start, lambda: None, finish

    return _Exchange([buf], [jax.ShapeDtypeStruct((N_DEV,) + buf.shape, F32)], N_DEV, ops)


def _sum_devices(gathered):
    def body(g_ref, out_ref):
        acc = g_ref[0]
        for d in range(1, N_DEV):
            acc = acc + g_ref[d]
        out_ref[...] = acc

    return pl.pallas_call(
        body,
        name="sum_devices",
        in_specs=[VMEM_SPEC],
        out_specs=VMEM_SPEC,
        out_shape=jax.ShapeDtypeStruct(gathered.shape[1:], F32),
    )(gathered)


def _adamw(w, g, m, v):
    r, rest = w.shape[0], w.shape[1:]
    per_row = 1
    for dim in rest:
        per_row *= dim
    tr = _tile(r, max(8, (5 << 19) // (4 * per_row)), 8 if len(rest) == 1 else 1)

    def body(w_ref, g_ref, m_ref, v_ref, d_ref, mo_ref, vo_ref):
        gg = g_ref[...]
        mm = ADAM_B1 * m_ref[...] + (1.0 - ADAM_B1) * gg
        vv = ADAM_B2 * v_ref[...] + (1.0 - ADAM_B2) * (gg * gg)
        m_hat = mm / (1.0 - ADAM_B1 ** ADAM_STEP)
        v_hat = vv / (1.0 - ADAM_B2 ** ADAM_STEP)
        d_ref[...] = -ADAM_LR * (m_hat / (jnp.sqrt(v_hat) + ADAM_EPS) + ADAM_WD * w_ref[...])
        mo_ref[...] = mm
        vo_ref[...] = vv

    spec = pl.BlockSpec((tr,) + rest, lambda i: (i,) + (0,) * len(rest))
    return pl.pallas_call(
        body,
        name="adamw",
        grid=(r // tr,),
        in_specs=[spec] * 4,
        out_specs=[spec] * 3,
        out_shape=[jax.ShapeDtypeStruct(w.shape, F32)] * 3,
        compiler_params=_cparams(("parallel",)),
    )(w, g, m, v)


def _adamw_halves(w, own, other, m, v, c, name):
    r, cols = w.shape
    half = r // 2
    tr = _tile(half, 256, 8)
    nt = half // tr
    whole = pl.BlockSpec((tr, cols), lambda h, i, c_ref: (h * nt + i, 0))
    part = pl.BlockSpec((tr, cols), lambda h, i, c_ref: (i, 0))

    def body(c_ref, w_ref, own_ref, other_ref, m_ref, v_ref, g_ref, d_ref, mo_ref, vo_ref):
        gg = jnp.where(pl.program_id(0) == c_ref[0], own_ref[...], other_ref[...])
        g_ref[...] = gg
        mm = ADAM_B1 * m_ref[...] + (1.0 - ADAM_B1) * gg
        vv = ADAM_B2 * v_ref[...] + (1.0 - ADAM_B2) * (gg * gg)
        m_hat = mm / (1.0 - ADAM_B1 ** ADAM_STEP)
        v_hat = vv / (1.0 - ADAM_B2 ** ADAM_STEP)
        d_ref[...] = -ADAM_LR * (m_hat / (jnp.sqrt(v_hat) + ADAM_EPS) + ADAM_WD * w_ref[...])
        mo_ref[...] = mm
        vo_ref[...] = vv

    return pl.pallas_call(
        body,
        name="adamw_" + name,
        grid_spec=pltpu.PrefetchScalarGridSpec(
            num_scalar_prefetch=1, grid=(2, nt),
            in_specs=[whole, part, part, whole, whole], out_specs=[whole] * 4),
        out_shape=[jax.ShapeDtypeStruct((r, cols), F32)] * 4,
        compiler_params=_cparams(("parallel", "parallel")),
    )(c, w, own, other, m, v)


GATHER_FIRST = ("ffn1_w_in", "ffn1_w_out")
GATHER_PROJ = ("w_in",)
GATHER_LATE = ("w_branch_a", "w_branch_b", "w_out", "ffn2_w_in", "ffn2_w_out")


class _Comm:
    def __init__(self, shards, c_arr, mine_arr):
        self.shards, self.c, self.mine = shards, c_arr, mine_arr
        self.groups, self.halves = {}, {}
        self.by_name = {entry[0]: entry for entry in BIG}

    def small_gather(self, loss, small):
        pad_lanes = lambda a: jnp.concatenate([a, jnp.zeros((1, LANES - a.shape[1]), F32)], axis=1)
        buf = jnp.concatenate([
            small["meta_tokens"].reshape(128, LANES),
            small["ffn1_norm"].reshape(8, LANES), small["mix_norm"].reshape(8, LANES),
            small["ffn2_norm"].reshape(8, LANES), small["final_norm"].reshape(8, LANES),
            loss, pad_lanes(small["b_forget"]), pad_lanes(small["attn_sinks"]),
            jnp.zeros((SMALL_ROWS - 163, LANES), F32)], axis=0)
        return _small_exchange(buf)

    def small_gathered(self, outs):
        self.reduced = _sum_devices(outs[0])

    def gather(self, names):
        table = [self.by_name[n] + (BF16, True) for n in names]
        return _gather_exchange([self.shards[n] for n in names], table)

    def gathered(self, names, outs):
        return [o.transpose(1, 0, 2).reshape(D_MODEL, W_IN_COLS) if n == STACKED else o for n, o in zip(names, outs)]

    def swap(self, tag, grads):
        entries = [self.by_name[n] for n in grads]
        arrays = [g.reshape(D_MODEL, N_CHIPS, W_IN_COLS // N_CHIPS).transpose(1, 0, 2) if n == STACKED else g
                  for n, g in grads.items()]
        self.groups[tag] = (entries, arrays)
        return _halves_exchange(arrays, entries)

    def scatter(self, tag, received):
        entries, arrays = self.groups[tag]
        views = [_halves_view(*entry) for entry in entries]
        partials = [_add_sibling(g.reshape(v[0]), r, self.c, name)
                    for g, v, r, (name, _, _) in zip(arrays, views, received, entries)]
        self.groups[tag] = (entries, partials)
        return _scatter_exchange(partials, entries)

    def received(self, tag, pieces):
        entries, partials = self.groups[tag]
        for p, r, (name, _, axis) in zip(partials, pieces, entries):
            self.halves[name] = _add_chips(p, r, self.mine, name, axis)

    def finish(self):
        names = [n for n, _, _ in BIG]
        own = [self.halves[n] for n in names]
        return dict(zip(names, zip(own, _share_with_sibling(own))))


def kernel(x, meta_tokens, ffn1_norm, ffn1_w_in, ffn1_w_out, mix_norm, w_in, b_forget, attn_sinks, w_branch_a, w_branch_b, w_out, ffn2_norm, ffn2_w_in, ffn2_w_out, final_norm, loss_target, m_meta_tokens, m_ffn1_norm, m_ffn1_w_in, m_ffn1_w_out, m_mix_norm, m_w_in, m_b_forget, m_attn_sinks, m_w_branch_a, m_w_branch_b, m_w_out, m_ffn2_norm, m_ffn2_w_in, m_ffn2_w_out, m_final_norm, v_meta_tokens, v_ffn1_norm, v_ffn1_w_in, v_ffn1_w_out, v_mix_norm, v_w_in, v_b_forget, v_attn_sinks, v_w_branch_a, v_w_branch_b, v_w_out, v_ffn2_norm, v_ffn2_w_in, v_ffn2_w_out, v_final_norm):
    given = dict(locals())
    names = ["meta_tokens", "ffn1_norm", "ffn1_w_in", "ffn1_w_out", "mix_norm", "w_in", "b_forget", "attn_sinks",
             "w_branch_a", "w_branch_b", "w_out", "ffn2_norm", "ffn2_w_in", "ffn2_w_out", "final_norm"]
    big_names = [n for n, _, _ in BIG]
    cx, cy, cc = _coords()
    c_arr = cc.reshape(1).astype(jnp.int32)
    mine_arr = (2 * cx + cy).reshape(1).astype(jnp.int32)

    comm = _Comm({n: given[n][0].astype(BF16) for n in big_names}, c_arr, mine_arr)
    table = [comm.by_name[n] + (BF16, True) for n in GATHER_FIRST] + [("meta_tokens", (N_META, D_MODEL), 1, F32, False)]
    first = _gather_exchange([comm.shards[n] for n in GATHER_FIRST] + [meta_tokens], table)
    w1i, w1o, meta_full = _run_exchange(first, "gather_first")
    norms = (ffn1_norm, mix_norm, ffn2_norm, final_norm.reshape(1, D_MODEL))
    loss, grad_x, small, big = _local_step(x, loss_target, meta_full, norms, b_forget, attn_sinks, (w1i, w1o), comm)

    swap = comm.swap("ffn1_in", dict(ffn1_w_in=big["ffn1_w_in"]))
    last = comm.scatter("ffn1_in", _run_exchange(swap, "exchange_halves_ffn1_in"))
    comm.received("ffn1_in", _run_exchange(last, "scatter_chip_sums"))
    grad_halves = comm.finish()
    grads = {}

    red = comm.reduced
    meta_cols = red[:128].reshape(N_META, D_MODEL)
    grads["meta_tokens"] = lax.dynamic_slice_in_dim(meta_cols, (2 * cx + cy) * (D_MODEL // N_CHIPS),
                                                    D_MODEL // N_CHIPS, axis=1)
    grads["ffn1_norm"] = red[128:136].reshape(1, D_MODEL)
    grads["mix_norm"] = red[136:144].reshape(1, D_MODEL)
    grads["ffn2_norm"] = red[144:152].reshape(1, D_MODEL)
    grads["final_norm"] = red[152:160].reshape(1, D_MODEL)
    loss_out = red[160, 0]
    grads["b_forget"] = red[161:162, :B_HEADS]
    grads["attn_sinks"] = red[162:163, :A_HEADS]

    out_g, out_d, out_m, out_v = [], [], [], []
    for n in names:
        w_full = given[n]
        shape = w_full.shape
        two_d = (lambda a: a.reshape(shape[-2], shape[-1])) if len(shape) >= 2 else (lambda a: a.reshape(1, shape[0]))
        if n == STACKED:
            own, other = grad_halves[n]
            g_t = jnp.concatenate([jnp.where(cc == 0, own, other), jnp.where(cc == 0, other, own)], axis=0).T
            tiles = lambda a: a.reshape(shape[-1], shape[-2] // LANES, LANES)
            untile = lambda a: a.reshape(shape[-1], shape[-2]).T
            d2, m2, v2 = [untile(a) for a in _adamw(tiles(two_d(w_full).T), tiles(g_t), tiles(two_d(given["m_" + n]).T),
                                                     tiles(two_d(given["v_" + n]).T))]
            g2 = g_t.T
        elif n in grad_halves:
            own, other = grad_halves[n]
            g2, d2, m2, v2 = _adamw_halves(two_d(w_full), own, other, two_d(given["m_" + n]),
                                           two_d(given["v_" + n]), c_arr, n)
        else:
            g2 = two_d(grads[n])
            d2, m2, v2 = _adamw(two_d(w_full), g2, two_d(given["m_" + n]), two_d(given["v_" + n]))
        out_g.append(g2.reshape(shape))
        out_d.append(d2.reshape(shape))
        out_m.append(m2.reshape(shape))
        out_v.append(v2.reshape(shape))
    return (loss_out, grad_x, *out_g, *out_d, *out_m, *out_v)
```

```python
import jax
import jax.numpy as jnp
from jax import lax
from jax.experimental import pallas as pl
from jax.experimental.pallas import tpu as pltpu

F32 = jnp.float32
BF16 = jnp.bfloat16

D_MODEL = 1024
N_META = 16
BLOCK = 128
LANES = 128
PREFIX = BLOCK
N_PAD = PREFIX - N_META
HEAD_DIM = 64
A_HEADS = 8
A_KV_HEADS = 2
A_GROUP = 4
B_HEADS = 8
B_PAIRS = B_HEADS // 2
A_WIDTH = A_HEADS * HEAD_DIM
A_KV_WIDTH = A_KV_HEADS * HEAD_DIM
B_WIDTH = B_HEADS * HEAD_DIM
W_IN_COLS = A_WIDTH + 2 * A_KV_WIDTH + 3 * B_WIDTH + B_HEADS + 2 * D_MODEL
SRC_KA = A_WIDTH
SRC_VA = SRC_KA + A_KV_WIDTH
SRC_QB = SRC_VA + A_KV_WIDTH
SRC_KB = SRC_QB + B_WIDTH
SRC_VB = SRC_KB + B_WIDTH
SRC_F = SRC_VB + B_WIDTH
SRC_GA = SRC_F + B_HEADS
SRC_GB = SRC_GA + D_MODEL
A_PAD_WIDTH = A_HEADS * LANES
B_PAD_WIDTH = B_HEADS * LANES
F_COLS = LANES
OFF_QA = 0
OFF_KA = SRC_KA
OFF_VA = SRC_VA
OFF_QB = SRC_QB
OFF_KB = SRC_KB
OFF_VB = SRC_VB
OFF_GA = SRC_F
OFF_GB = OFF_GA + D_MODEL
OFF_F = OFF_GB + D_MODEL
P_COLS = OFF_F + F_COLS
EPS = 1e-6
NEG = -1e30
SCALE = HEAD_DIM ** -0.5
KEY_BLOCKS = 4

ADAM_LR = 0.001
ADAM_B1 = 0.9
ADAM_B2 = 0.999
ADAM_EPS = 1e-08
ADAM_WD = 0.01
ADAM_STEP = 10

N_CHIPS = 4
N_DEV = 8
VMEM_LIMIT = 56 * 1024 * 1024

NT_DIMS = (((1,), (1,)), ((), ()))
TN_DIMS = (((0,), (0,)), ((), ()))
MESH_ID = pl.DeviceIdType.MESH
HBM_SPEC = pl.BlockSpec(memory_space=pltpu.HBM)
VMEM_SPEC = pl.BlockSpec(memory_space=pltpu.VMEM)


def _tile(n, target, mult=16):
    best = None
    for t in range(mult, min(n, target) + 1, mult):
        if n % t == 0:
            best = t
    return best if best is not None else n


def _cparams(sem):
    return pltpu.CompilerParams(dimension_semantics=sem, vmem_limit_bytes=VMEM_LIMIT)


def _rms_scale(h):
    return lax.rsqrt(jnp.mean(h * h, axis=-1, keepdims=True) + EPS)


def _rms_bwd(dn, h, w):
    r = _rms_scale(h)
    dw = jnp.sum(dn * (h * r), axis=0, keepdims=True)
    z = dn * w
    dh = r * z - h * ((r * r * r) * jnp.mean(z * h, axis=-1, keepdims=True))
    return dh, dw


def _ffn_fwd(h, norm_w, w_in, w_out, exchange=None):
    t, d = h.shape
    f = w_out.shape[0]
    tm = _tile(t, 272)
    tc = _tile(f, 256, 128)
    nj = f // tc
    ni = t // tm
    n_x = len(exchange.ins) if exchange else 0

    def body(*refs):
        h_ref, nw_ref, wi_ref, wo_ref = refs[:4]
        hout_ref, g_ref, u_ref = refs[4 + n_x:7 + n_x]
        a_scr = refs[7 + 2 * n_x]
        i = pl.program_id(0)
        if exchange:
            _host_exchange(exchange, refs[4:4 + n_x], refs[7 + n_x:7 + 2 * n_x], refs[8 + 2 * n_x:],
                           i == 0, i == ni - 2, i == ni - 1)
        hh = h_ref[...]
        n = ((hh * _rms_scale(hh)) * nw_ref[...]).astype(BF16)
        for j in range(nj):
            cols = slice(j * tc, (j + 1) * tc)
            g = jnp.dot(n, wi_ref[:, j * tc:(j + 1) * tc], preferred_element_type=F32)
            u = jnp.dot(n, wi_ref[:, f + j * tc:f + (j + 1) * tc], preferred_element_type=F32)
            g_ref[:, cols] = g
            u_ref[:, cols] = u
            a_scr[:, cols] = ((g * jax.nn.sigmoid(g)) * u).astype(BF16)
        hout_ref[...] = hh + 0.5 * jnp.dot(a_scr[...], wo_ref[...], preferred_element_type=F32)

    resident = lambda a: pl.BlockSpec(a.shape, lambda i: (0, 0), pipeline_mode=pl.Buffered(1))
    row = lambda w: pl.BlockSpec((tm, w), lambda i: (i, 0))
    outs = pl.pallas_call(
        body,
        name="ffn_fwd",
        grid=(ni,),
        in_specs=[row(d), pl.BlockSpec((1, d), lambda i: (0, 0)), resident(w_in), resident(w_out)] + [HBM_SPEC] * n_x,
        out_specs=[row(d), row(f), row(f)] + [HBM_SPEC] * n_x,
        out_shape=[
            jax.ShapeDtypeStruct((t, d), F32),
            jax.ShapeDtypeStruct((t, f), F32),
            jax.ShapeDtypeStruct((t, f), F32),
        ] + (exchange.out_shape if exchange else []),
        scratch_shapes=[pltpu.VMEM((tm, f), BF16)] + (exchange.scratch if exchange else []),
        compiler_params=_cparams(("arbitrary",) if exchange else ("parallel",)),
    )(h, norm_w, w_in, w_out, *(exchange.ins if exchange else []))
    return outs[:3], outs[3:]


def _ffn_bwd(dh_out, h, norm_w, g, u, w_in, w_out, exchange=None):
    t, d = h.shape
    f = w_out.shape[0]
    tm = _tile(t, 272)
    tc = _tile(f, 256, 128)
    nj = f // tc
    ni = t // tm
    n_x = len(exchange.ins) if exchange else 0

    def body(*refs):
        dho_ref, h_ref, nw_ref, g_ref, u_ref, wi_ref, wo_ref = refs[:7]
        dhin_ref, n_ref, a_ref, dgu_ref, df_ref, dnw_ref = refs[7 + n_x:13 + n_x]
        i = pl.program_id(0)
        if exchange:
            _host_exchange(exchange, refs[7:7 + n_x], refs[13 + n_x:13 + 2 * n_x], refs[13 + 2 * n_x:],
                           i == 0, i == ni - 1, i == ni - 1)
        hh = h_ref[...]
        nw = nw_ref[...]
        n_ref[...] = ((hh * _rms_scale(hh)) * nw).astype(BF16)
        dho = dho_ref[...]
        df = (0.5 * dho).astype(BF16)
        df_ref[...] = df
        for j in range(nj):
            cols = slice(j * tc, (j + 1) * tc)
            da = lax.dot_general(df, wo_ref[cols, :], NT_DIMS, preferred_element_type=F32)
            gg = g_ref[:, cols]
            uu = u_ref[:, cols]
            sig = jax.nn.sigmoid(gg)
            sl = gg * sig
            a_ref[:, cols] = (sl * uu).astype(BF16)
            dgu_ref[0, :, cols] = ((da * uu) * (sig * (1.0 + gg * (1.0 - sig)))).astype(BF16)
            dgu_ref[1, :, cols] = (da * sl).astype(BF16)
        dn = (lax.dot_general(dgu_ref[0], wi_ref[:, :f], NT_DIMS, preferred_element_type=F32)
              + lax.dot_general(dgu_ref[1], wi_ref[:, f:], NT_DIMS, preferred_element_type=F32))
        dh, dw = _rms_bwd(dn, hh, nw)
        dhin_ref[...] = dho + dh
        dnw_ref[0] = dw

    resident = lambda a: pl.BlockSpec(a.shape, lambda i: (0, 0), pipeline_mode=pl.Buffered(1))
    row = lambda w: pl.BlockSpec((tm, w), lambda i: (i, 0))
    outs = pl.pallas_call(
        body,
        name="ffn_bwd",
        grid=(ni,),
        in_specs=[row(d), row(d), pl.BlockSpec((1, d), lambda i: (0, 0)), row(f), row(f),
                  resident(w_in), resident(w_out)] + [HBM_SPEC] * n_x,
        out_specs=[row(d), row(d), row(f), pl.BlockSpec((2, tm, f), lambda i: (0, i, 0)), row(d),
                   pl.BlockSpec((1, 1, d), lambda i: (i, 0, 0))] + [HBM_SPEC] * n_x,
        out_shape=[
            jax.ShapeDtypeStruct((t, d), F32),
            jax.ShapeDtypeStruct((t, d), BF16),
            jax.ShapeDtypeStruct((t, f), BF16),
            jax.ShapeDtypeStruct((2, t, f), BF16),
            jax.ShapeDtypeStruct((t, d), BF16),
            jax.ShapeDtypeStruct((ni, 1, d), F32),
        ] + (exchange.out_shape if exchange else []),
        scratch_shapes=exchange.scratch if exchange else [],
        compiler_params=_cparams(("arbitrary",) if exchange else ("parallel",)),
    )(dh_out, h, norm_w, g, u, w_in, w_out, *(exchange.ins if exchange else []))
    return outs[:6], outs[6:]


def _tn_matmul(a, b, name, exchange=None):
    t, k = a.shape
    split = b.ndim == 3
    n = 2 * b.shape[2] if split else b.shape[1]
    tk = _tile(k, 512, 128)
    tn = _tile(b.shape[-1], 1408, 128)
    per_half = b.shape[-1] // tn
    ni, nj = k // tk, n // tn
    n_x = len(exchange.ins) if exchange else 0

    def body(*refs):
        a_ref, b_ref, o_ref = refs[0], refs[1], refs[2 + n_x]
        if exchange:
            i, j = pl.program_id(0), pl.program_id(1)
            at_end = (i == ni - 1) & (j == nj - 1)
            _host_exchange(exchange, refs[2:2 + n_x], refs[3 + n_x:3 + 2 * n_x], refs[3 + 2 * n_x:],
                           (i == 0) & (j == 0), at_end, at_end)
        o_ref[...] = lax.dot_general(a_ref[...], b_ref[...], TN_DIMS, preferred_element_type=F32)

    if split:
        b_spec = pl.BlockSpec((None, t, tn), lambda i, j: (j // per_half, 0, j % per_half))
    else:
        b_spec = pl.BlockSpec((t, tn), lambda i, j: (0, j))
    outs = pl.pallas_call(
        body,
        name=name,
        grid=(ni, nj),
        in_specs=[pl.BlockSpec((t, tk), lambda i, j: (0, i)), b_spec] + [HBM_SPEC] * n_x,
        out_specs=[pl.BlockSpec((tk, tn), lambda i, j: (i, j))] + [HBM_SPEC] * n_x,
        out_shape=[jax.ShapeDtypeStruct((k, n), F32)] + (exchange.out_shape if exchange else []),
        scratch_shapes=exchange.scratch if exchange else [],
        compiler_params=_cparams(("arbitrary", "arbitrary") if exchange else ("parallel", "parallel")),
    )(a, b, *(exchange.ins if exchange else []))
    return (outs[0], outs[1:]) if exchange else outs[0]


A_SLOT = lambda h: h // A_GROUP
B_SLOT = lambda h: h % 2

PROJ_PARTS = (
    (OFF_QA, A_WIDTH, A_PAD_WIDTH, True, A_SLOT), (OFF_KA, A_KV_WIDTH, A_KV_WIDTH, True, None),
    (OFF_VA, A_KV_WIDTH, A_KV_WIDTH, True, None), (OFF_QB, B_WIDTH, B_WIDTH, True, None),
    (OFF_KB, B_WIDTH, B_PAD_WIDTH, True, B_SLOT), (OFF_VB, B_WIDTH, B_PAD_WIDTH, True, B_SLOT),
    (OFF_GA, D_MODEL, D_MODEL, False, None), (OFF_GB, D_MODEL, D_MODEL, False, None), (OFF_F, F_COLS, F_COLS, False, None),
)


def _head_tile(pair, head, slot):
    lane_slot = lax.broadcasted_iota(jnp.int32, pair.shape, 1) // HEAD_DIM
    moved = pair if head % 2 == slot else pltpu.roll(pair, HEAD_DIM, 1)
    return jnp.where(lane_slot == slot, moved, 0.0)


def _proj_fwd(h, norm_w, w_p):
    t, d = h.shape
    tm = _tile(t, 272)

    def body(h_ref, nw_ref, w_ref, u_ref, *part_refs):
        hh = h_ref[...]
        un = ((hh * _rms_scale(hh)) * nw_ref[...]).astype(BF16)
        u_ref[...] = un
        for (off, width, _, _, slot), p_ref in zip(PROJ_PARTS, part_refs):
            if slot is None:
                p_ref[...] = jnp.dot(un, w_ref[:, off:off + width], preferred_element_type=F32).astype(p_ref.dtype)
                continue
            for pair in range(width // LANES):
                x = jnp.dot(un, w_ref[:, off + pair * LANES:off + (pair + 1) * LANES], preferred_element_type=F32)
                for head in (2 * pair, 2 * pair + 1):
                    p_ref[:, head * LANES:(head + 1) * LANES] = _head_tile(x, head, slot(head)).astype(p_ref.dtype)

    row = lambda w: pl.BlockSpec((tm, w), lambda i: (i, 0))
    return pl.pallas_call(
        body,
        name="proj_fwd",
        grid=(t // tm,),
        in_specs=[row(d), pl.BlockSpec((1, d), lambda i: (0, 0)),
                  pl.BlockSpec(w_p.shape, lambda i: (0, 0), pipeline_mode=pl.Buffered(1))],
        out_specs=[row(d)] + [row(width) for _, _, width, _, _ in PROJ_PARTS],
        out_shape=[jax.ShapeDtypeStruct((t, d), BF16)]
        + [jax.ShapeDtypeStruct((t, width), BF16 if is_bf else F32) for _, _, width, is_bf, _ in PROJ_PARTS],
        compiler_params=_cparams(("parallel",)),
    )(h, norm_w, w_p)


def _proj_bwd(dh_out, h, norm_w, dproj, w_p, exchange=None):
    t, d = h.shape
    n = w_p.shape[1]
    tm = _tile(t, 272)
    ni = t // tm
    n_x = len(exchange.ins) if exchange else 0

    def body(*refs):
        dho_ref, h_ref, nw_ref, dp_ref, w_ref = refs[:5]
        dhin_ref, dnw_ref = refs[5 + n_x:7 + n_x]
        if exchange:
            i = pl.program_id(0)
            _host_exchange(exchange, refs[5:5 + n_x], refs[7 + n_x:7 + 2 * n_x], refs[7 + 2 * n_x:],
                           i == 0, i == ni - 1, i == ni - 1)
        dn = lax.dot_general(dp_ref[...], w_ref[...], NT_DIMS, preferred_element_type=F32)
        dh, dw = _rms_bwd(dn, h_ref[...], nw_ref[...])
        dhin_ref[...] = dho_ref[...] + dh
        dnw_ref[0] = dw

    row = lambda w: pl.BlockSpec((tm, w), lambda i: (i, 0))
    outs = pl.pallas_call(
        body,
        name="proj_bwd",
        grid=(ni,),
        in_specs=[row(d), row(d), pl.BlockSpec((1, d), lambda i: (0, 0)), row(n),
                  pl.BlockSpec(w_p.shape, lambda i: (0, 0), pipeline_mode=pl.Buffered(1))] + [HBM_SPEC] * n_x,
        out_specs=[row(d), pl.BlockSpec((1, 1, d), lambda i: (i, 0, 0))] + [HBM_SPEC] * n_x,
        out_shape=[jax.ShapeDtypeStruct((t, d), F32), jax.ShapeDtypeStruct((ni, 1, d), F32)]
        + (exchange.out_shape if exchange else []),
        scratch_shapes=exchange.scratch if exchange else [],
        compiler_params=_cparams(("arbitrary",) if exchange else ("parallel",)),
    )(dh_out, h, norm_w, dproj, w_p, *(exchange.ins if exchange else []))
    return outs[:2], outs[2:]


def _merge_fwd(h, oa, ob, ga, gb, wa, wb, wo):
    t, d = h.shape
    tm = _tile(t, 544)

    def body(h_ref, oa_ref, ob_ref, ga_ref, gb_ref, wa_ref, wb_ref, wo_ref, hout_ref, mix_ref):
        ya = jnp.dot(oa_ref[...], wa_ref[...], preferred_element_type=F32)
        yb = jnp.dot(ob_ref[...], wb_ref[...], preferred_element_type=F32)
        mixed = (jax.nn.sigmoid(ga_ref[...]) * ya + jax.nn.sigmoid(gb_ref[...]) * yb).astype(BF16)
        mix_ref[...] = mixed
        hout_ref[...] = h_ref[...] + jnp.dot(mixed, wo_ref[...], preferred_element_type=F32)

    row = lambda w: pl.BlockSpec((tm, w), lambda i: (i, 0))
    full = lambda a: pl.BlockSpec(a.shape, lambda i: (0, 0))
    return pl.pallas_call(
        body,
        name="merge_fwd",
        grid=(t // tm,),
        in_specs=[row(d), row(oa.shape[1]), row(ob.shape[1]), row(d), row(d), full(wa), full(wb), full(wo)],
        out_specs=[row(d), row(d)],
        out_shape=[jax.ShapeDtypeStruct((t, d), F32), jax.ShapeDtypeStruct((t, d), BF16)],
        compiler_params=_cparams(("parallel",)),
    )(h, oa, ob, ga, gb, wa, wb, wo)


def _merge_bwd(dh, oa, ob, ga, gb, wa, wb, wo, exchange=None):
    t, d = dh.shape
    tm = _tile(t, 544)
    ni = t // tm
    n_x = len(exchange.ins) if exchange else 0

    def body(*refs):
        dh_ref, oa_ref, ob_ref, ga_ref, gb_ref, wa_ref, wb_ref, wo_ref = refs[:8]
        dya_ref, dyb_ref, doa_ref, dob_ref, dga_ref, dgb_ref, dhb_ref = refs[8 + n_x:15 + n_x]
        if exchange:
            i = pl.program_id(0)
            _host_exchange(exchange, refs[8:8 + n_x], refs[15 + n_x:15 + 2 * n_x], refs[15 + 2 * n_x:],
                           i == 0, i == ni - 1, i == ni - 1)
        dhb = dh_ref[...].astype(BF16)
        dhb_ref[...] = dhb
        dmix = lax.dot_general(dhb, wo_ref[...], NT_DIMS, preferred_element_type=F32)
        for o_ref, g_ref, w_ref, dy_ref, do_ref, dg_ref in (
                (oa_ref, ga_ref, wa_ref, dya_ref, doa_ref, dga_ref),
                (ob_ref, gb_ref, wb_ref, dyb_ref, dob_ref, dgb_ref)):
            y = jnp.dot(o_ref[...], w_ref[...], preferred_element_type=F32)
            s = jax.nn.sigmoid(g_ref[...])
            dy = (dmix * s).astype(BF16)
            dy_ref[...] = dy
            dg_ref[...] = ((dmix * y) * (s * (1.0 - s))).astype(BF16)
            do_ref[...] = lax.dot_general(dy, w_ref[...], NT_DIMS, preferred_element_type=F32).astype(BF16)

    row = lambda w: pl.BlockSpec((tm, w), lambda i: (i, 0))
    full = lambda a: pl.BlockSpec(a.shape, lambda i: (0, 0))
    wa_w, wb_w = oa.shape[1], ob.shape[1]
    outs = pl.pallas_call(
        body,
        name="merge_bwd",
        grid=(ni,),
        in_specs=[row(d), row(wa_w), row(wb_w), row(d), row(d), full(wa), full(wb), full(wo)] + [HBM_SPEC] * n_x,
        out_specs=[row(d), row(d), row(wa_w), row(wb_w), row(d), row(d), row(d)] + [HBM_SPEC] * n_x,
        out_shape=[
            jax.ShapeDtypeStruct((t, d), BF16), jax.ShapeDtypeStruct((t, d), BF16),
            jax.ShapeDtypeStruct((t, wa_w), BF16), jax.ShapeDtypeStruct((t, wb_w), BF16),
            jax.ShapeDtypeStruct((t, d), BF16), jax.ShapeDtypeStruct((t, d), BF16),
            jax.ShapeDtypeStruct((t, d), BF16),
        ] + (exchange.out_shape if exchange else []),
        scratch_shapes=exchange.scratch if exchange else [],
        compiler_params=_cparams(("arbitrary",) if exchange else ("parallel",)),
    )(dh, oa, ob, ga, gb, wa, wb, wo, *(exchange.ins if exchange else []))
    return outs[:7], outs[7:]


def _tri_dot(tri, x):
    hi = x.astype(BF16)
    r1 = x - hi.astype(F32)
    mid = r1.astype(BF16)
    lo = (r1 - mid.astype(F32)).astype(BF16)
    return (jnp.dot(tri, hi, preferred_element_type=F32)
            + jnp.dot(tri, mid, preferred_element_type=F32)
            + jnp.dot(tri, lo, preferred_element_type=F32))


def _forget_cumsum(f_logit, b_pad, nb):
    t, w = f_logit.shape
    bsz = t // (nb * BLOCK)

    def body(f_ref, b_ref, c_ref, carry):
        n = pl.program_id(1)

        @pl.when(n == 0)
        def _():
            carry[...] = jnp.zeros_like(carry)

        x = jax.nn.log_sigmoid(f_ref[...] + b_ref[...])
        rows = lax.broadcasted_iota(jnp.int32, (BLOCK, BLOCK), 0)
        cols = lax.broadcasted_iota(jnp.int32, (BLOCK, BLOCK), 1)
        tri = (cols <= rows).astype(BF16)
        c = _tri_dot(tri, x) + carry[...]
        c_ref[...] = c
        carry[...] = c[BLOCK - 1:BLOCK, :]

    return pl.pallas_call(
        body,
        name="forget_cumsum",
        grid=(bsz, nb),
        in_specs=[pl.BlockSpec((BLOCK, w), lambda b, n: (b * nb + n, 0)),
                  pl.BlockSpec((1, w), lambda b, n: (0, 0))],
        out_specs=pl.BlockSpec((BLOCK, w), lambda b, n: (b * nb + n, 0)),
        out_shape=jax.ShapeDtypeStruct((t, w), F32),
        scratch_shapes=[pltpu.VMEM((1, w), F32)],
        compiler_params=_cparams(("parallel", "arbitrary")),
    )(f_logit, b_pad)


def _forget_cumsum_bwd(dc, f_logit, b_pad, nb):
    t, w = f_logit.shape
    bsz = t // (nb * BLOCK)

    def body(dc_ref, f_ref, b_ref, df_ref, db_ref, carry):
        n = pl.program_id(1)

        @pl.when(n == 0)
        def _():
            carry[...] = jnp.zeros_like(carry)
            db_ref[...] = jnp.zeros_like(db_ref)

        rows = lax.broadcasted_iota(jnp.int32, (BLOCK, BLOCK), 0)
        cols = lax.broadcasted_iota(jnp.int32, (BLOCK, BLOCK), 1)
        tri = (cols >= rows).astype(BF16)
        dlf = _tri_dot(tri, dc_ref[...]) + carry[...]
        carry[...] = dlf[0:1, :]
        df = dlf * jax.nn.sigmoid(-(f_ref[...] + b_ref[...]))
        df_ref[...] = df.astype(BF16)
        db_ref[0] += jnp.sum(df, axis=0, keepdims=True)

    rev = lambda b, n: (b * nb + (nb - 1 - n), 0)
    return pl.pallas_call(
        body,
        name="forget_cumsum_bwd",
        grid=(bsz, nb),
        in_specs=[pl.BlockSpec((BLOCK, w), rev),
                  pl.BlockSpec((BLOCK, w), rev),
                  pl.BlockSpec((1, w), lambda b, n: (0, 0))],
        out_specs=[pl.BlockSpec((BLOCK, w), rev),
                   pl.BlockSpec((1, 1, w), lambda b, n: (b, 0, 0))],
        out_shape=[jax.ShapeDtypeStruct((t, w), BF16), jax.ShapeDtypeStruct((bsz, 1, w), F32)],
        scratch_shapes=[pltpu.VMEM((1, w), F32)],
        compiler_params=_cparams(("parallel", "arbitrary")),
    )(dc, f_logit, b_pad)


GROUP_ROWS = A_GROUP * BLOCK


def _stack_heads(ref, g):
    return jnp.concatenate([ref[:, (A_GROUP * g + i) * LANES:(A_GROUP * g + i + 1) * LANES] for i in range(A_GROUP)],
                           axis=0)


def _unstack_heads(ref, g, x):
    for i in range(A_GROUP):
        ref[:, (A_GROUP * g + i) * LANES:(A_GROUP * g + i + 1) * LANES] = x[i * BLOCK:(i + 1) * BLOCK].astype(ref.dtype)


def _swa_logits(q, keys, slope, n):
    qi = lax.broadcasted_iota(jnp.int32, (GROUP_ROWS, BLOCK), 0) & (BLOCK - 1)
    kj = lax.broadcasted_iota(jnp.int32, (GROUP_ROWS, BLOCK), 1)
    s_all = lax.dot_general(q, keys, NT_DIMS, preferred_element_type=F32) * SCALE
    out = []
    for i, (dist, ok) in enumerate((
            (n * BLOCK + qi - kj, (kj >= N_PAD) & (n * BLOCK + qi - kj >= 0)),
            (BLOCK + qi - kj, (kj > qi) & (n >= 2)),
            (qi - kj, (kj <= qi) & (n >= 1)))):
        s = s_all[:, i * BLOCK:(i + 1) * BLOCK] - slope * dist.astype(F32)
        out.append(jnp.where(ok, s, NEG))
    return out


def _three_blocks(m_ref, p_ref, c_ref):
    return jnp.concatenate([m_ref[...], p_ref[...], c_ref[...]], axis=0)


def _swa_specs(nb):
    row = lambda b, n: b * nb + n
    qspec = pl.BlockSpec((BLOCK, A_PAD_WIDTH), lambda b, n: (row(b, n), 0))
    kv_m = pl.BlockSpec((BLOCK, LANES), lambda b, n: (row(b, 0), 0))
    kv_p = pl.BlockSpec((BLOCK, LANES), lambda b, n: (row(b, jnp.maximum(n - 1, 0)), 0))
    kv_c = pl.BlockSpec((BLOCK, LANES), lambda b, n: (row(b, n), 0))
    rowspec = pl.BlockSpec((A_KV_HEADS, GROUP_ROWS, 1), lambda b, n: (0, 0, 0))
    lsespec = pl.BlockSpec((1, A_KV_HEADS, GROUP_ROWS, 1), lambda b, n: (row(b, n), 0, 0, 0))
    return qspec, kv_m, kv_p, kv_c, rowspec, lsespec


def _swa_fwd(q, k, v, sink_rows, slope_rows, nb):
    t = q.shape[0]
    bsz = t // (nb * BLOCK)

    def body(q_ref, km_ref, kp_ref, kc_ref, vm_ref, vp_ref, vc_ref, sink_ref, slope_ref, o_ref, lse_ref):
        n = pl.program_id(1)
        keys = _three_blocks(km_ref, kp_ref, kc_ref)
        values = _three_blocks(vm_ref, vp_ref, vc_ref)
        lane_group = lax.broadcasted_iota(jnp.int32, (GROUP_ROWS, LANES), 1) // HEAD_DIM
        for g in range(A_KV_HEADS):
            qq = _stack_heads(q_ref, g)
            sink = sink_ref[g]
            s_m, s_p, s_c = _swa_logits(qq, keys, slope_ref[g], n)
            m = jnp.maximum(jnp.max(jnp.maximum(jnp.maximum(s_m, s_p), s_c), axis=-1, keepdims=True), sink)
            m_wide = jnp.broadcast_to(m, (GROUP_ROWS, BLOCK))
            e_m = jnp.exp(s_m - m_wide)
            e_p = jnp.exp(s_p - m_wide)
            e_c = jnp.exp(s_c - m_wide)
            z = jnp.sum((e_m + e_p) + e_c, axis=-1, keepdims=True) + jnp.exp(sink - m)
            inv = jnp.broadcast_to(1.0 / z, (GROUP_ROWS, BLOCK))
            probs = jnp.concatenate([(e_m * inv).astype(BF16), (e_p * inv).astype(BF16), (e_c * inv).astype(BF16)],
                                    axis=1)
            o = jnp.dot(probs, values, preferred_element_type=F32)
            _unstack_heads(o_ref, g, jnp.where(lane_group == g, o, 0.0))
            lse_ref[0, g] = m + jnp.log(z)

    qspec, kv_m, kv_p, kv_c, rowspec, lsespec = _swa_specs(nb)
    return pl.pallas_call(
        body,
        name="swa_fwd",
        grid=(bsz, nb),
        in_specs=[qspec, kv_m, kv_p, kv_c, kv_m, kv_p, kv_c, rowspec, rowspec],
        out_specs=[qspec, lsespec],
        out_shape=[jax.ShapeDtypeStruct((t, A_PAD_WIDTH), BF16),
                   jax.ShapeDtypeStruct((t // BLOCK, A_KV_HEADS, GROUP_ROWS, 1), F32)],
        compiler_params=_cparams(("parallel", "arbitrary")),
    )(q, k, k, k, v, v, v, sink_rows, slope_rows)


def _swa_bwd(q, k, v, do, lse, sink_rows, slope_rows, nb):
    t = q.shape[0]
    l = nb * BLOCK
    bsz = t // l

    def body(q_ref, km_ref, kp_ref, kc_ref, vm_ref, vp_ref, vc_ref, do_ref, lse_ref, sink_ref, slope_ref,
             dq_ref, dk_ref, dv_ref, dsink_ref, dk_acc, dv_acc):
        n = pl.program_id(1)

        @pl.when(n == 0)
        def _():
            dk_acc[...] = jnp.zeros_like(dk_acc)
            dv_acc[...] = jnp.zeros_like(dv_acc)
            dsink_ref[...] = jnp.zeros_like(dsink_ref)

        keys = _three_blocks(km_ref, kp_ref, kc_ref)
        values = _three_blocks(vm_ref, vp_ref, vc_ref)
        first_half = lax.broadcasted_iota(jnp.int32, (BLOCK, LANES), 1) < HEAD_DIM
        prev = jnp.maximum(n - 1, 0)
        for g in range(A_KV_HEADS):
            qq = _stack_heads(q_ref, g)
            dob = _stack_heads(do_ref, g)
            lse = lse_ref[0, g]
            lse_wide = jnp.broadcast_to(lse, (GROUP_ROWS, BLOCK))
            probs = [jnp.exp(s - lse_wide) for s in _swa_logits(qq, keys, slope_ref[g], n)]
            dp_all = lax.dot_general(dob, values, NT_DIMS, preferred_element_type=F32)
            dps = [dp_all[:, i * BLOCK:(i + 1) * BLOCK] for i in range(3)]
            delta = jnp.sum((probs[0] * dps[0] + probs[1] * dps[1]) + probs[2] * dps[2], axis=-1, keepdims=True)
            delta_wide = jnp.broadcast_to(delta, (GROUP_ROWS, BLOCK))
            ds = jnp.concatenate([(p * (dp - delta_wide)).astype(BF16) for p, dp in zip(probs, dps)], axis=1)
            pb = jnp.concatenate([p.astype(BF16) for p in probs], axis=1)
            dq = jnp.dot(ds, keys, preferred_element_type=F32) * SCALE
            dk_all = lax.dot_general(ds, qq, TN_DIMS, preferred_element_type=F32) * SCALE
            dv_all = lax.dot_general(pb, dob, TN_DIMS, preferred_element_type=F32)
            for i, start in enumerate((0, prev * BLOCK, n * BLOCK)):
                rows = pl.ds(pl.multiple_of(start, BLOCK), BLOCK)
                dk_acc[rows, :] += dk_all[i * BLOCK:(i + 1) * BLOCK]
                dv_acc[rows, :] += dv_all[i * BLOCK:(i + 1) * BLOCK]
            for pair in range(A_GROUP // 2):
                even = dq[2 * pair * BLOCK:(2 * pair + 1) * BLOCK]
                odd = dq[(2 * pair + 1) * BLOCK:(2 * pair + 2) * BLOCK]
                left = even if g == 0 else pltpu.roll(even, HEAD_DIM, 1)
                right = pltpu.roll(odd, HEAD_DIM, 1) if g == 0 else odd
                tile = (A_GROUP // 2) * g + pair
                dq_ref[:, tile * LANES:(tile + 1) * LANES] = jnp.where(first_half, left, right).astype(BF16)
            dsink_ref[0, g] += -(jnp.exp(sink_ref[g] - lse) * delta)

        @pl.when(n == nb - 1)
        def _():
            dk_ref[...] = dk_acc[...].astype(BF16)
            dv_ref[...] = dv_acc[...].astype(BF16)

    qspec, kv_m, kv_p, kv_c, rowspec, lsespec = _swa_specs(nb)
    kv_all = pl.BlockSpec((l, LANES), lambda b, n: (b, 0))
    return pl.pallas_call(
        body,
        name="swa_bwd",
        grid=(bsz, nb),
        in_specs=[qspec, kv_m, kv_p, kv_c, kv_m, kv_p, kv_c, qspec, lsespec, rowspec, rowspec],
        out_specs=[pl.BlockSpec((BLOCK, A_WIDTH), lambda b, n: (b * nb + n, 0)), kv_all, kv_all,
                   pl.BlockSpec((1, A_KV_HEADS, GROUP_ROWS, 1), lambda b, n: (b, 0, 0, 0))],
        out_shape=[jax.ShapeDtypeStruct((t, A_WIDTH), BF16),
                   jax.ShapeDtypeStruct((t, LANES), BF16),
                   jax.ShapeDtypeStruct((t, LANES), BF16),
                   jax.ShapeDtypeStruct((bsz, A_KV_HEADS, GROUP_ROWS, 1), F32)],
        scratch_shapes=[pltpu.VMEM((l, LANES), F32), pltpu.VMEM((l, LANES), F32)],
        compiler_params=_cparams(("parallel", "arbitrary")),
    )(q, k, k, k, v, v, v, do, lse, sink_rows, slope_rows)


CHUNK = KEY_BLOCKS * BLOCK


def _fox_chunk(qb, ci):
    sb = jnp.maximum(jnp.minimum(KEY_BLOCKS * ci, qb + 1 - KEY_BLOCKS), 0)
    lo = jnp.maximum(ci * CHUNK, N_PAD)
    return sb, lo, pl.ds(pl.multiple_of(sb * BLOCK, BLOCK), CHUNK)


def _fox_logits(s_ref, cr_ref, e, j, sb, lo, qb):
    lane = lax.broadcasted_iota(jnp.int32, (BLOCK, BLOCK), 1)
    ahead = lane - lax.broadcasted_iota(jnp.int32, (BLOCK, BLOCK), 0)
    first = (sb + j) * BLOCK
    s = s_ref[e, :, j * BLOCK:(j + 1) * BLOCK] - cr_ref[e, sb + j]
    return jnp.where((ahead <= qb * BLOCK - first) & (lane >= lo - first), s, NEG)


FOX_PAIRS = 4
FOX_HEADS = 2 * FOX_PAIRS
FOX_STEPS = B_PAIRS // FOX_PAIRS


def _fox_specs(nb):
    l = nb * BLOCK
    q_spec = pl.BlockSpec((BLOCK, FOX_PAIRS * LANES), lambda b, p, i: (b * nb + i, p))
    kv_spec = pl.BlockSpec((l, FOX_HEADS * LANES), lambda b, p, i: (b, p))
    cc_spec = pl.BlockSpec((FOX_HEADS, BLOCK, 1), lambda b, p, i: (b * FOX_STEPS + p, i, 0))
    cr_spec = pl.BlockSpec((FOX_HEADS, nb, 1, BLOCK), lambda b, p, i: (b * FOX_STEPS + p, 0, 0, 0))
    return q_spec, kv_spec, cc_spec, cr_spec


def _fox_fwd(q, k, v, c_row, nb, exchange=None):
    t = q.shape[0]
    bsz = t // (nb * BLOCK)
    assert nb >= KEY_BLOCKS

    n_x = len(exchange.ins) if exchange else 0

    def body(*refs):
        q_ref, k_ref, v_ref, cr_ref = refs[:4]
        o_ref, ox_ref, lse_ref = refs[4 + n_x:7 + n_x]
        s_scr, hi_scr, lo_scr = refs[7 + 2 * n_x:10 + 2 * n_x]
        qb = pl.program_id(2)
        if exchange:
            first = (pl.program_id(0) == 0) & (pl.program_id(1) == 0)
            last = (pl.program_id(0) == bsz - 1) & (pl.program_id(1) == FOX_STEPS - 1)
            _host_exchange(exchange, refs[4:4 + n_x], refs[7 + n_x:7 + 2 * n_x], refs[10 + 2 * n_x:],
                           first & (qb == 0), last & (qb == 0), last & (qb == nb - 1))
        qs = [q_ref[:, a * LANES:(a + 1) * LANES] * SCALE for a in range(FOX_PAIRS)]
        first_half = lax.broadcasted_iota(jnp.int32, (BLOCK, LANES), 1) < HEAD_DIM

        def step(ci, carry):
            stats, accs = carry[:2 * FOX_HEADS], carry[2 * FOX_HEADS:]
            sb, lo, krows = _fox_chunk(qb, ci)
            new_stats, new_accs = [], []
            for a in range(FOX_PAIRS):
                alphas = []
                pv = jnp.zeros((BLOCK, LANES), F32)
                pv_lo = jnp.zeros((BLOCK, LANES), F32)
                for e in (2 * a, 2 * a + 1):
                    m, z = stats[2 * e], stats[2 * e + 1]
                    tile = slice(e * LANES, (e + 1) * LANES)
                    s_scr[e] = lax.dot_general(qs[a], k_ref[krows, tile], NT_DIMS, preferred_element_type=F32)
                    top = None
                    for j in range(KEY_BLOCKS):
                        s = _fox_logits(s_scr, cr_ref, e, j, sb, lo, qb)
                        s_scr[e, :, j * BLOCK:(j + 1) * BLOCK] = s
                        top = s if top is None else jnp.maximum(top, s)
                    m_new = jnp.maximum(m, jnp.max(top, axis=-1, keepdims=True))
                    alpha = jnp.exp(m - m_new)
                    m_wide = jnp.broadcast_to(m_new, (BLOCK, BLOCK))
                    total = None
                    for j in range(KEY_BLOCKS):
                        cols = slice(j * BLOCK, (j + 1) * BLOCK)
                        p = jnp.exp(s_scr[e, :, cols] - m_wide)
                        total = p if total is None else total + p
                        hi = p.astype(BF16)
                        hi_scr[e, :, cols] = hi
                        lo_scr[e, :, cols] = (p - hi.astype(F32)).astype(BF16)
                    z = alpha * z + jnp.sum(total, axis=-1, keepdims=True)
                    vv = v_ref[krows, tile]
                    pv = pv + jnp.dot(hi_scr[e], vv, preferred_element_type=F32)
                    pv_lo = pv_lo + jnp.dot(lo_scr[e], vv, preferred_element_type=F32)
                    new_stats += [m_new, z]
                    alphas.append(alpha)
                alpha = jnp.where(first_half, alphas[0], alphas[1])
                new_accs += [alpha * accs[2 * a] + pv, alpha * accs[2 * a + 1] + pv_lo]
            return (*new_stats, *new_accs)

        col = lambda val: jnp.full((BLOCK, 1), val, F32)
        done = lax.fori_loop(
            0, (qb + KEY_BLOCKS) // KEY_BLOCKS, step,
            (col(NEG), col(0.0)) * FOX_HEADS + (jnp.zeros((BLOCK, LANES), F32),) * (2 * FOX_PAIRS))
        for a in range(FOX_PAIRS):
            m0, z0, m1, z1 = done[4 * a:4 * a + 4]
            acc, acc_lo = done[2 * FOX_HEADS + 2 * a:2 * FOX_HEADS + 2 * a + 2]
            inv = 1.0 / jnp.where(first_half, z0, z1)
            tile = slice(a * LANES, (a + 1) * LANES)
            o_ref[:, tile] = (acc * inv).astype(BF16)
            ox_ref[:, tile] = (acc + acc_lo) * inv
            lse_ref[2 * a] = m0 + jnp.log(z0)
            lse_ref[2 * a + 1] = m1 + jnp.log(z1)

    q_spec, kv_spec, cc_spec, cr_spec = _fox_specs(nb)
    outs = pl.pallas_call(
        body,
        name="fox_fwd",
        grid=(bsz, FOX_STEPS, nb),
        in_specs=[q_spec, kv_spec, kv_spec, cr_spec] + [HBM_SPEC] * n_x,
        out_specs=[q_spec, q_spec, cc_spec] + [HBM_SPEC] * n_x,
        out_shape=[jax.ShapeDtypeStruct((t, B_WIDTH), BF16), jax.ShapeDtypeStruct((t, B_WIDTH), F32),
                   jax.ShapeDtypeStruct((bsz * B_HEADS, nb * BLOCK, 1), F32)] + (exchange.out_shape if exchange else []),
        scratch_shapes=[pltpu.VMEM((FOX_HEADS, BLOCK, CHUNK), F32), pltpu.VMEM((FOX_HEADS, BLOCK, CHUNK), BF16),
                        pltpu.VMEM((FOX_HEADS, BLOCK, CHUNK), BF16)] + (exchange.scratch if exchange else []),
        compiler_params=_cparams(("arbitrary",) * 3 if exchange else ("parallel", "parallel", "arbitrary")),
    )(q, k, v, c_row, *(exchange.ins if exchange else []))
    return outs[:3], outs[3:]


def _fox_bwd(q, k, v, o_exact, do, lse, c_row, nb, exchange=None):
    t = q.shape[0]
    l = nb * BLOCK
    bsz = t // l

    n_x = len(exchange.ins) if exchange else 0

    def body(*refs):
        q_ref, k_ref, v_ref, ox_ref, do_ref, lse_ref, cr_ref = refs[:7]
        dq_ref, dk_ref, dv_ref, dc_ref = refs[7 + n_x:11 + n_x]
        dk_acc, dv_acc, s_scr, dp_scr, p_scr, ds_scr = refs[11 + 2 * n_x:17 + 2 * n_x]
        qb = pl.program_id(2)
        if exchange:
            first = (pl.program_id(0) == 0) & (pl.program_id(1) == 0)
            last = (pl.program_id(0) == bsz - 1) & (pl.program_id(1) == FOX_STEPS - 1)
            _host_exchange(exchange, refs[7:7 + n_x], refs[11 + n_x:11 + 2 * n_x], refs[17 + 2 * n_x:],
                           first & (qb == 0), last & (qb == 0), last & (qb == nb - 1))

        @pl.when(qb == 0)
        def _():
            dk_acc[...] = jnp.zeros_like(dk_acc)
            dv_acc[...] = jnp.zeros_like(dv_acc)
            dc_ref[...] = jnp.zeros_like(dc_ref)

        top_half = lax.broadcasted_iota(jnp.int32, (LANES, BLOCK), 0) < HEAD_DIM
        pair_t = lambda x: jnp.concatenate([jnp.where(top_half, x.T, 0), jnp.where(top_half, 0, x.T)], axis=1)
        first_half = lax.broadcasted_iota(jnp.int32, (BLOCK, LANES), 1) < HEAD_DIM
        wide = lambda col: jnp.broadcast_to(col, (BLOCK, BLOCK))
        qs, dobs, qs_t, dob_t, deltas = [], [], [], [], []
        for a in range(FOX_PAIRS):
            tile = slice(a * LANES, (a + 1) * LANES)
            qs.append(q_ref[:, tile] * SCALE)
            dobs.append(do_ref[:, tile])
            qs_t.append(pair_t(qs[a]))
            dob_t.append(pair_t(dobs[a]))
            weighted = dobs[a].astype(F32) * ox_ref[:, tile]
            deltas += [wide(jnp.sum(jnp.where(first_half, weighted, 0.0), axis=-1, keepdims=True)),
                       wide(jnp.sum(jnp.where(first_half, 0.0, weighted), axis=-1, keepdims=True))]
        lses = [wide(lse_ref[e]) for e in range(FOX_HEADS)]

        def step(ci, dqs):
            sb, lo, krows = _fox_chunk(qb, ci)
            dqs = list(dqs)
            for a in range(FOX_PAIRS):
                for e in (2 * a, 2 * a + 1):
                    tile = slice(e * LANES, (e + 1) * LANES)
                    kk = k_ref[krows, tile]
                    s_scr[e] = lax.dot_general(qs[a], kk, NT_DIMS, preferred_element_type=F32)
                    dp_scr[e] = lax.dot_general(dobs[a], v_ref[krows, tile], NT_DIMS, preferred_element_type=F32)
                    for j in range(KEY_BLOCKS):
                        cols = slice(j * BLOCK, (j + 1) * BLOCK)
                        p = jnp.exp(_fox_logits(s_scr, cr_ref, e, j, sb, lo, qb) - lses[e])
                        ds = p * (dp_scr[e, :, cols] - deltas[e])
                        dc_ref[e, sb + j] -= jnp.sum(ds, axis=0, keepdims=True)
                        p_scr[e, :, cols] = p.astype(BF16)
                        ds_scr[e, :, cols] = ds.astype(BF16)
                    dqs[a] = dqs[a] + jnp.dot(ds_scr[e], kk, preferred_element_type=F32)
                both = slice(2 * a, 2 * a + 2)
                dk_t = jnp.dot(qs_t[a], ds_scr[both].reshape(2 * BLOCK, CHUNK), preferred_element_type=F32)
                dv_t = jnp.dot(dob_t[a], p_scr[both].reshape(2 * BLOCK, CHUNK), preferred_element_type=F32)
                for j in range(KEY_BLOCKS):
                    cols = slice(j * BLOCK, (j + 1) * BLOCK)
                    dk_acc[a * nb + sb + j] += dk_t[:, cols]
                    dv_acc[a * nb + sb + j] += dv_t[:, cols]
            return tuple(dqs)

        dqs = lax.fori_loop(0, (qb + KEY_BLOCKS) // KEY_BLOCKS, step,
                            (jnp.zeros((BLOCK, LANES), F32),) * FOX_PAIRS)
        for a in range(FOX_PAIRS):
            dq_ref[:, a * LANES:(a + 1) * LANES] = (dqs[a] * SCALE).astype(BF16)

        @pl.when(qb == nb - 1)
        def _():
            for a in range(FOX_PAIRS):
                for kb in range(nb):
                    rows = slice(kb * BLOCK, (kb + 1) * BLOCK)
                    for acc, out_ref in ((dk_acc, dk_ref), (dv_acc, dv_ref)):
                        out_ref[rows, a * LANES:(a + 1) * LANES] = acc[a * nb + kb].T.astype(BF16)

    q_spec, kv_spec, cc_spec, cr_spec = _fox_specs(nb)
    dkv_spec = pl.BlockSpec((l, FOX_PAIRS * LANES), lambda b, p, i: (b, p))
    outs = pl.pallas_call(
        body,
        name="fox_bwd",
        grid=(bsz, FOX_STEPS, nb),
        in_specs=[q_spec, kv_spec, kv_spec, q_spec, q_spec, cc_spec, cr_spec] + [HBM_SPEC] * n_x,
        out_specs=[q_spec, dkv_spec, dkv_spec, cr_spec] + [HBM_SPEC] * n_x,
        out_shape=[jax.ShapeDtypeStruct((t, B_WIDTH), BF16), jax.ShapeDtypeStruct((t, B_WIDTH), BF16),
                   jax.ShapeDtypeStruct((t, B_WIDTH), BF16),
                   jax.ShapeDtypeStruct((bsz * B_HEADS, nb, 1, BLOCK), F32)] + (exchange.out_shape if exchange else []),
        scratch_shapes=[pltpu.VMEM((FOX_PAIRS * nb, LANES, BLOCK), F32), pltpu.VMEM((FOX_PAIRS * nb, LANES, BLOCK), F32),
                        pltpu.VMEM((FOX_HEADS, BLOCK, CHUNK), F32), pltpu.VMEM((FOX_HEADS, BLOCK, CHUNK), F32),
                        pltpu.VMEM((FOX_HEADS, BLOCK, CHUNK), BF16), pltpu.VMEM((FOX_HEADS, BLOCK, CHUNK), BF16)]
        + (exchange.scratch if exchange else []),
        compiler_params=_cparams(("arbitrary",) * 3 if exchange else ("parallel", "parallel", "arbitrary")),
    )(q, k, v, o_exact, do, lse, c_row, *(exchange.ins if exchange else []))
    return outs[:4], outs[4:]


def _loss_head(h, final_w, target):
    bsz, l, d = h.shape
    nb = l // BLOCK

    def body(h_ref, w_ref, t_ref, loss_ref, dh_ref, dw_ref):
        b = pl.program_id(0)
        n = pl.program_id(1)

        @pl.when((b == 0) & (n == 0))
        def _():
            loss_ref[...] = jnp.zeros_like(loss_ref)
            dw_ref[...] = jnp.zeros_like(dw_ref)

        @pl.when(n == 0)
        def _():
            dh_ref[...] = jnp.zeros_like(dh_ref)

        @pl.when(n > 0)
        def _():
            hh = h_ref[0]
            w = w_ref[...]
            r = _rms_scale(hh)
            err = (hh * r) * w - t_ref[0]
            loss_ref[...] += 0.5 * jnp.sum(jnp.mean(err * err, axis=-1, keepdims=True), axis=0, keepdims=True)
            dy = err * (1.0 / d)
            dh, dw = _rms_bwd(dy, hh, w)
            dh_ref[0] = dh
            dw_ref[...] += dw

    return pl.pallas_call(
        body,
        name="loss_head",
        grid=(bsz, nb),
        in_specs=[
            pl.BlockSpec((1, BLOCK, d), lambda b, n: (b, n, 0)),
            pl.BlockSpec((1, d), lambda b, n: (0, 0)),
            pl.BlockSpec((1, BLOCK, d), lambda b, n: (b, jnp.maximum(n - 1, 0), 0)),
        ],
        out_specs=[
            pl.BlockSpec((1, 128), lambda b, n: (0, 0)),
            pl.BlockSpec((1, BLOCK, d), lambda b, n: (b, n, 0)),
            pl.BlockSpec((1, d), lambda b, n: (0, 0)),
        ],
        out_shape=[jax.ShapeDtypeStruct((1, 128), F32), jax.ShapeDtypeStruct((bsz, l, d), F32),
                   jax.ShapeDtypeStruct((1, d), F32)],
        compiler_params=_cparams(("arbitrary", "arbitrary")),
    )(h, final_w, target)


def _pad_tiles(w, src, heads, lane_slot, axis):
    pieces = []
    for h in range(heads):
        x = lax.slice_in_dim(w, src + HEAD_DIM * h, src + HEAD_DIM * (h + 1), axis=axis)
        z = jnp.zeros_like(x)
        pieces += [x, z] if lane_slot(h) == 0 else [z, x]
    return pieces


def _unpad_tiles(g, off, heads, lane_slot, axis):
    return [lax.slice_in_dim(g, off + LANES * h + HEAD_DIM * lane_slot(h),
                             off + LANES * h + HEAD_DIM * (lane_slot(h) + 1), axis=axis) for h in range(heads)]


def _layout_w_in(w):
    pad_f = jnp.zeros((w.shape[0], F_COLS - B_HEADS), w.dtype)
    return jnp.concatenate([w[:, :SRC_F], w[:, SRC_GA:], w[:, SRC_F:SRC_GA], pad_f], axis=1)


def _unlayout_w_in(g):
    return jnp.concatenate([g[:, :OFF_GA], g[:, OFF_F:OFF_F + B_HEADS], g[:, OFF_GA:OFF_F]], axis=1)


def _local_step(x, target, meta, norms, b_forget, sinks, w, comm=None):
    n1, nmix, n2, nfin = norms
    w1i, w1o = w[:2]
    bsz, seq, d = x.shape
    l = PREFIX + seq
    nb = l // BLOCK
    t = bsz * l

    h0 = jnp.concatenate([jnp.zeros((bsz, N_PAD, d), F32),
                          jnp.broadcast_to(meta[None], (bsz, N_META, d)), x], axis=1).reshape(t, d)

    if comm is None:
        (h1, g1, u1), _ = _ffn_fwd(h0, n1, w1i, w1o)
        w_in, wa, wb, wo, w2i, w2o = w[2:]
    else:
        (h1, g1, u1), gathered = _ffn_fwd(h0, n1, w1i, w1o, comm.gather(GATHER_PROJ))
        w_in, = comm.gathered(GATHER_PROJ, gathered)
    wp = _layout_w_in(w_in)
    un, qa, ka, va, qb, kb, vb, ga, gb, f_logit = _proj_fwd(h1, nmix, wp)
    b_pad = jnp.concatenate([b_forget, jnp.zeros((1, F_COLS - B_HEADS), F32)], axis=1)
    c = _forget_cumsum(f_logit, b_pad, nb)
    c_heads = c[:, :B_HEADS].reshape(bsz, l, B_HEADS).transpose(0, 2, 1).reshape(bsz * B_HEADS, l)
    c_row = c_heads.reshape(bsz * B_HEADS, nb, 1, BLOCK)

    slopes = jnp.exp2(-8.0 * jnp.arange(1, A_HEADS + 1, dtype=F32) / A_HEADS)
    slope_rows = jnp.repeat(slopes.reshape(A_KV_HEADS, A_GROUP), BLOCK, axis=1)[:, :, None]
    sink_rows = jnp.repeat(sinks.reshape(A_KV_HEADS, A_GROUP), BLOCK, axis=1)[:, :, None]

    oa, lse_a = _swa_fwd(qa, ka, va, sink_rows, slope_rows, nb)
    if comm is None:
        (ob, ob_exact, lse_b), _ = _fox_fwd(qb, kb, vb, c_row, nb)
    else:
        (ob, ob_exact, lse_b), gathered = _fox_fwd(qb, kb, vb, c_row, nb, comm.gather(GATHER_LATE))
        wa, wb, wo, w2i, w2o = comm.gathered(GATHER_LATE, gathered)
    wa_p = jnp.concatenate(_pad_tiles(wa, 0, A_HEADS, A_SLOT, 0), axis=0)
    h2, mixed = _merge_fwd(h1, oa, ob, ga, gb, wa_p, wb, wo)
    (h3, g2, u2), _ = _ffn_fwd(h2, n2, w2i, w2o)
    loss, dh3, d_nfin = _loss_head(h3.reshape(bsz, l, d), nfin, target)

    (dh2, n2b, a2, dgu2, df2, dn2_parts), _ = _ffn_bwd(dh3.reshape(t, d), h2, n2, g2, u2, w2i, w2o)
    g_w2o = _tn_matmul(a2, df2, "grad_ffn2_w_out")
    g_w2i = _tn_matmul(n2b, dgu2, "grad_ffn2_w_in")

    hosted = comm.swap("ffn2", dict(ffn2_w_in=g_w2i, ffn2_w_out=g_w2o)) if comm else None
    (dya, dyb, doa, dob, dga, dgb, dh2b), swapped = _merge_bwd(dh2, oa, ob, ga, gb, wa_p, wb, wo, hosted)
    g_wo = _tn_matmul(mixed, dh2b, "grad_w_out")
    g_wa = jnp.concatenate(_unpad_tiles(_tn_matmul(oa, dya, "grad_w_branch_a"), 0, A_HEADS, A_SLOT, 0), axis=0)
    g_wb = _tn_matmul(ob, dyb, "grad_w_branch_b")

    dqa, dka, dva, dsink_rows = _swa_bwd(qa, ka, va, doa, lse_a, sink_rows, slope_rows, nb)
    hosted = comm.scatter("ffn2", swapped) if comm else None
    (dqb, dkb, dvb, dc_row), pieces = _fox_bwd(qb, kb, vb, ob_exact, dob, lse_b, c_row, nb, hosted)
    if comm:
        comm.received("ffn2", pieces)
    dc = dc_row.reshape(bsz, B_HEADS, l).transpose(0, 2, 1).reshape(t, B_HEADS)
    dc = jnp.concatenate([dc, jnp.zeros((t, F_COLS - B_HEADS), F32)], axis=1)
    df_logit, db_parts = _forget_cumsum_bwd(dc, f_logit, b_pad, nb)

    dproj = jnp.concatenate([dqa, dka, dva, dqb, dkb, dvb, dga, dgb, df_logit], axis=1)
    g_win = _unlayout_w_in(_tn_matmul(un, dproj, "grad_w_in"))
    hosted = comm.swap("mixer", dict(w_in=g_win, w_branch_a=g_wa, w_branch_b=g_wb, w_out=g_wo)) if comm else None
    (dh1, dnmix_parts), swapped = _proj_bwd(dh2, h1, nmix, dproj, wp, hosted)
    hosted = comm.scatter("mixer", swapped) if comm else None
    (dh0, n1b, a1, dgu1, df1, dn1_parts), pieces = _ffn_bwd(dh1, h0, n1, g1, u1, w1i, w1o, hosted)
    dh0 = dh0.reshape(bsz, l, d)
    grad_x = dh0[:, PREFIX:]
    small = dict(
        meta_tokens=jnp.sum(dh0[:, N_PAD:PREFIX], axis=0),
        ffn1_norm=jnp.sum(dn1_parts, axis=0),
        mix_norm=jnp.sum(dnmix_parts, axis=0),
        ffn2_norm=jnp.sum(dn2_parts, axis=0),
        final_norm=d_nfin,
        b_forget=jnp.sum(db_parts, axis=0)[:, :B_HEADS],
        attn_sinks=jnp.sum(dsink_rows.reshape(bsz, A_HEADS, BLOCK), axis=(0, 2)).reshape(1, A_HEADS),
    )
    if comm is None:
        g_w1o = _tn_matmul(a1, df1, "grad_ffn1_w_out")
        g_w1i = _tn_matmul(n1b, dgu1, "grad_ffn1_w_in")
    else:
        comm.received("mixer", pieces)
        g_w1o, gathered = _tn_matmul(a1, df1, "grad_ffn1_w_out", comm.small_gather(loss, small))
        comm.small_gathered(gathered)
        swapped = _run_exchange(comm.swap("ffn1_out", dict(ffn1_w_out=g_w1o)), "exchange_halves_ffn1_out")
        g_w1i, pieces = _tn_matmul(n1b, dgu1, "grad_ffn1_w_in", comm.scatter("ffn1_out", swapped))
        comm.received("ffn1_out", pieces)
    big = dict(ffn1_w_in=g_w1i, ffn1_w_out=g_w1o, w_in=g_win, w_branch_a=g_wa, w_branch_b=g_wb,
               w_out=g_wo, ffn2_w_in=g_w2i, ffn2_w_out=g_w2o)
    return loss, grad_x, small, big


BIG = (
    ("ffn1_w_in", (D_MODEL, 5632), 1),
    ("ffn1_w_out", (2816, D_MODEL), 0),
    ("w_in", (D_MODEL, W_IN_COLS), 1),
    ("w_branch_a", (A_WIDTH, D_MODEL), 1),
    ("w_branch_b", (B_WIDTH, D_MODEL), 1),
    ("w_out", (D_MODEL, D_MODEL), 0),
    ("ffn2_w_in", (D_MODEL, 5632), 1),
    ("ffn2_w_out", (2816, D_MODEL), 0),
)
STACKED = "w_in"


def _coords():
    return lax.axis_index("x"), lax.axis_index("y"), lax.axis_index("c")


def _other_chips(x, y):
    return ((1 - x, y), (x, 1 - y), (1 - x, 1 - y))


def _chip_part(ref, name, shape, axis, k):
    if name == STACKED:
        return ref.at[k]
    size = shape[axis] // N_CHIPS
    start = pl.multiple_of(k * size, size)
    return ref.at[pl.ds(start, size), :] if axis == 0 else ref.at[:, pl.ds(start, size)]


def _full_shape(name, shape):
    return (N_CHIPS, shape[0], shape[1] // N_CHIPS) if name == STACKED else shape


class _Exchange:
    def __init__(self, ins, out_shape, n_sems, ops):
        self.ins, self.out_shape, self.n_sems, self.ops = list(ins), list(out_shape), n_sems, ops

    @property
    def scratch(self):
        return [pltpu.SemaphoreType.DMA((self.n_sems,)), pltpu.SemaphoreType.DMA((self.n_sems,))]


SEMS_PER_GATHER = 9


def _gather_exchange(shards, table):
    n = len(table)
    x_nbr, y_nbr, diagonal = 0, 1, 2

    def ops(ins, outs, send_sems, recv_sems):
        x, y, c = _coords()
        mine = 2 * x + y
        sibling = (x, y, 1 - c)
        chips = _other_chips(x, y)
        slots = [2 * chip[0] + chip[1] for chip in chips]

        def part(i, k):
            name, shape, axis = table[i][:3]
            return _chip_part(outs[i], name, shape, axis, k)

        def half(ref, h):
            rows = ref.shape[0] // 2
            return ref.at[pl.ds(pl.multiple_of(h * rows, rows), rows), :]

        def remote(i, sem, src, dst, device):
            sem = SEMS_PER_GATHER * i + sem
            return pltpu.make_async_remote_copy(src, dst, send_sems.at[sem], recv_sems.at[sem],
                                                device_id=device, device_id_type=MESH_ID)

        def own(i):
            return remote(i, 0, ins[i], part(i, mine), sibling)

        def fetch(i, j, slot):
            if table[i][4]:
                src, dst = half(ins[i], c), half(part(i, slot), c)
            else:
                src, dst = ins[i], part(i, slot)
            return remote(i, 1 + j, src, dst, (chips[j][0], chips[j][1], c))

        def relayed(i, via, of):
            region = half(half(part(i, slots[of]), c), via)
            return remote(i, 4 + via, region, region, (chips[via][0], chips[via][1], c))

        def forward(i, j, h):
            region = half(part(i, slots[j]), h)
            return remote(i, 6 + j, region, region, sibling)

        def start():
            for i in range(n):
                for j in (x_nbr, y_nbr) if table[i][4] else (x_nbr, y_nbr, diagonal):
                    fetch(i, j, mine).start()
            for i in range(n):
                own(i).start()

        def relay():
            for i in range(n):
                if not table[i][4]:
                    for j in range(3):
                        fetch(i, j, slots[j]).wait_recv()
                    continue
                fetch(i, y_nbr, slots[y_nbr]).wait_recv()
                relayed(i, x_nbr, y_nbr).start()
                forward(i, y_nbr, c).start()
                fetch(i, x_nbr, slots[x_nbr]).wait_recv()
                relayed(i, y_nbr, x_nbr).start()
                forward(i, x_nbr, c).start()
            for i in range(n):
                if table[i][4]:
                    relayed(i, x_nbr, diagonal).wait_recv()
                    relayed(i, y_nbr, diagonal).wait_recv()
                    forward(i, diagonal, c).start()

        def finish():
            for i in range(n):
                own(i).wait()
                for j in range(3):
                    if table[i][4]:
                        forward(i, j, 1 - c).wait_recv()
                        forward(i, j, c).wait_send()
                    if j != diagonal or not table[i][4]:
                        fetch(i, j, mine).wait_send()
                if table[i][4]:
                    relayed(i, x_nbr, y_nbr).wait_send()
                    relayed(i, y_nbr, x_nbr).wait_send()

        return start, relay, finish

    out_shape = [jax.ShapeDtypeStruct(_full_shape(name, shape), dtype) for name, shape, _, dtype, _ in table]
    return _Exchange(shards, out_shape, SEMS_PER_GATHER * n, ops)


def _run_exchange(exchange, name):
    n = len(exchange.ins)

    def body(*refs):
        start, relay, finish = exchange.ops(refs[:n], refs[n:2 * n], *refs[2 * n:])
        start()
        relay()
        finish()

    return pl.pallas_call(
        body,
        name=name,
        in_specs=[HBM_SPEC] * n,
        out_specs=[HBM_SPEC] * n,
        out_shape=exchange.out_shape,
        scratch_shapes=exchange.scratch,
    )(*exchange.ins)


def _host_exchange(exchange, in_refs, out_refs, sem_refs, first, middle, last):
    start, relay, finish = exchange.ops(in_refs, out_refs, *sem_refs)
    pl.when(first)(start)
    pl.when(middle)(relay)
    pl.when(last)(finish)


def _halves_view(name, shape, axis):
    r, c = shape
    if name == STACKED:
        return (N_CHIPS, 2, r // 2, c // N_CHIPS), lambda ref, h: ref.at[:, h]
    if axis == 1:
        return (2, r // 2, c), lambda ref, h: ref.at[h]
    return (N_CHIPS, 2, r // N_CHIPS // 2, c), lambda ref, h: ref.at[:, h]


def _halves_exchange(grads, entries):
    n_w = len(entries)
    views = [_halves_view(*entry) for entry in entries]

    def ops(ins, outs, send_sems, recv_sems):
        x, y, c = _coords()
        copies = [pltpu.make_async_remote_copy(views[i][1](ins[i], 1 - c), outs[i], send_sems.at[i], recv_sems.at[i],
                                               device_id=(x, y, 1 - c), device_id_type=MESH_ID) for i in range(n_w)]

        def start():
            for cp in copies:
                cp.start()

        def finish():
            for cp in copies:
                cp.wait()

        return start, lambda: None, finish

    half_shape = lambda v: tuple(d for i, d in enumerate(v) if i != (1 if len(v) == 4 else 0))
    out_shape = [jax.ShapeDtypeStruct(half_shape(v[0]), F32) for v in views]
    return _Exchange([g.reshape(v[0]) for g, v in zip(grads, views)], out_shape, n_w, ops)


def _add_sibling(g_view, recv, c, name):
    shape = recv.shape
    if len(shape) == 2:
        tr = _tile(shape[0], 128, 16)
        grid = (shape[0] // tr,)
        g_spec = pl.BlockSpec((None, tr, shape[1]), lambda i, c_ref: (c_ref[0], i, 0))
        r_spec = pl.BlockSpec((tr, shape[1]), lambda i, c_ref: (i, 0))
    else:
        tr = _tile(shape[1], 256, 16)
        grid = (N_CHIPS, shape[1] // tr)
        g_spec = pl.BlockSpec((None, None, tr, shape[2]), lambda k, i, c_ref: (k, c_ref[0], i, 0))
        r_spec = pl.BlockSpec((None, tr, shape[2]), lambda k, i, c_ref: (k, i, 0))

    def body(c_ref, g_ref, r_ref, o_ref):
        o_ref[...] = (g_ref[...] + r_ref[...]).astype(BF16)

    return pl.pallas_call(
        body,
        name="add_sibling_" + name,
        grid_spec=pltpu.PrefetchScalarGridSpec(num_scalar_prefetch=1, grid=grid, in_specs=[g_spec, r_spec],
                                               out_specs=r_spec),
        out_shape=jax.ShapeDtypeStruct(shape, BF16),
        compiler_params=_cparams(("parallel",) * len(grid)),
    )(c, g_view, recv)


def _piece_of(ref, name, axis, k):
    if name == STACKED or axis == 0:
        return ref.at[k]
    size = ref.shape[1] // N_CHIPS
    return ref.at[:, pl.ds(pl.multiple_of(k * size, size), size)]


def _piece_shape(name, shape, axis):
    r, c = shape
    return (r // 2, c // N_CHIPS) if (axis == 1) else (r // N_CHIPS // 2, c)


def _scatter_exchange(partials, entries):
    n_w = len(entries)

    def ops(ins, outs, send_sems, recv_sems):
        x, y, c = _coords()
        chips = _other_chips(x, y)
        copies = []
        for i, (name, _, axis) in enumerate(entries):
            for j, chip in enumerate(chips):
                sem = 3 * i + j
                copies.append(pltpu.make_async_remote_copy(
                    _piece_of(ins[i], name, axis, 2 * chip[0] + chip[1]), outs[i].at[j], send_sems.at[sem],
                    recv_sems.at[sem], device_id=(chip[0], chip[1], c), device_id_type=MESH_ID))

        def start():
            for cp in copies:
                cp.start()

        def finish():
            for cp in copies:
                cp.wait()

        return start, lambda: None, finish

    out_shape = [jax.ShapeDtypeStruct((3,) + _piece_shape(*entry), BF16) for entry in entries]
    return _Exchange(partials, out_shape, 3 * n_w, ops)


def _add_chips(partial, recv, mine, name, axis):
    rows, cols = recv.shape[1:]
    tr = _tile(rows, 256, 16)
    if name == STACKED or axis == 0:
        p_spec = pl.BlockSpec((None, tr, cols), lambda i, k_ref: (k_ref[0], i, 0))
    else:
        p_spec = pl.BlockSpec((tr, cols), lambda i, k_ref: (i, k_ref[0]))

    def body(k_ref, p_ref, r_ref, o_ref):
        f32 = lambda a: a.astype(F32)
        o_ref[...] = ((f32(p_ref[...]) + f32(r_ref[0])) + f32(r_ref[1])) + f32(r_ref[2])

    return pl.pallas_call(
        body,
        name="add_chips_" + name,
        grid_spec=pltpu.PrefetchScalarGridSpec(
            num_scalar_prefetch=1, grid=(rows // tr,),
            in_specs=[p_spec, pl.BlockSpec((3, tr, cols), lambda i, k_ref: (0, i, 0))],
            out_specs=pl.BlockSpec((tr, cols), lambda i, k_ref: (i, 0))),
        out_shape=jax.ShapeDtypeStruct((rows, cols), F32),
        compiler_params=_cparams(("parallel",)),
    )(mine, partial, recv)


def _share_with_sibling(halves):
    n_w = len(halves)

    def body(*refs):
        ins, outs = refs[:n_w], refs[n_w:2 * n_w]
        send_sems, recv_sems = refs[2 * n_w:]
        x, y, c = _coords()
        copies = [pltpu.make_async_remote_copy(ins[i], outs[i], send_sems.at[i], recv_sems.at[i],
                                               device_id=(x, y, 1 - c), device_id_type=MESH_ID) for i in range(n_w)]
        for cp in copies:
            cp.start()
        for cp in copies:
            cp.wait()

    return pl.pallas_call(
        body,
        name="share_with_sibling",
        in_specs=[HBM_SPEC] * n_w,
        out_specs=[HBM_SPEC] * n_w,
        out_shape=[jax.ShapeDtypeStruct(h.shape, F32) for h in halves],
        scratch_shapes=[pltpu.SemaphoreType.DMA((n_w,)), pltpu.SemaphoreType.DMA((n_w,))],
    )(*halves)


SMALL_ROWS = 168


def _small_exchange(buf):
    def ops(ins, outs, send_sems, recv_sems):
        x, y, c = _coords()
        me = 4 * x + 2 * y + c
        peers = [(x ^ fx, y ^ fy, c ^ fc) for fx in (0, 1) for fy in (0, 1) for fc in (0, 1)][1:]

        def copy(j, slot, dev):
            return pltpu.make_async_remote_copy(ins[0], outs[0].at[slot], send_sems.at[j], recv_sems.at[j],
                                                device_id=dev, device_id_type=MESH_ID)

        own = pltpu.make_async_copy(ins[0], outs[0].at[me], send_sems.at[N_DEV - 1])

        def start():
            own.start()
            for j, dev in enumerate(peers):
                copy(j, me, dev).start()

        def finish():
            for j, dev in enumerate(peers):
                copy(j, 4 * dev[0] + 2 * dev[1] + dev[2], dev).wait()
            own.wait()

        return start, lambda: None, finish

    return _Exchange([buf], [jax.ShapeDtypeStruct((N_DEV,) + buf.shape, F32)], N_DEV, ops)


def _sum_devices(gathered):
    def body(g_ref, out_ref):
        acc = g_ref[0]
        for d in range(1, N_DEV):
            acc = acc + g_ref[d]
        out_ref[...] = acc

    return pl.pallas_call(
        body,
        name="sum_devices",
        in_specs=[VMEM_SPEC],
        out_specs=VMEM_SPEC,
        out_shape=jax.ShapeDtypeStruct(gathered.shape[1:], F32),
    )(gathered)


def _adamw(w, g, m, v):
    r, rest = w.shape[0], w.shape[1:]
    per_row = 1
    for dim in rest:
        per_row *= dim
    tr = _tile(r, max(8, (5 << 19) // (4 * per_row)), 8 if len(rest) == 1 else 1)

    def body(w_ref, g_ref, m_ref, v_ref, d_ref, mo_ref, vo_ref):
        gg = g_ref[...]
        mm = ADAM_B1 * m_ref[...] + (1.0 - ADAM_B1) * gg
        vv = ADAM_B2 * v_ref[...] + (1.0 - ADAM_B2) * (gg * gg)
        m_hat = mm / (1.0 - ADAM_B1 ** ADAM_STEP)
        v_hat = vv / (1.0 - ADAM_B2 ** ADAM_STEP)
        d_ref[...] = -ADAM_LR * (m_hat / (jnp.sqrt(v_hat) + ADAM_EPS) + ADAM_WD * w_ref[...])
        mo_ref[...] = mm
        vo_ref[...] = vv

    spec = pl.BlockSpec((tr,) + rest, lambda i: (i,) + (0,) * len(rest))
    return pl.pallas_call(
        body,
        name="adamw",
        grid=(r // tr,),
        in_specs=[spec] * 4,
        out_specs=[spec] * 3,
        out_shape=[jax.ShapeDtypeStruct(w.shape, F32)] * 3,
        compiler_params=_cparams(("parallel",)),
    )(w, g, m, v)


def _adamw_halves(w, own, other, m, v, c, name):
    r, cols = w.shape
    half = r // 2
    tr = _tile(half, 256, 8)
    nt = half // tr
    whole = pl.BlockSpec((tr, cols), lambda h, i, c_ref: (h * nt + i, 0))
    part = pl.BlockSpec((tr, cols), lambda h, i, c_ref: (i, 0))

    def body(c_ref, w_ref, own_ref, other_ref, m_ref, v_ref, g_ref, d_ref, mo_ref, vo_ref):
        gg = jnp.where(pl.program_id(0) == c_ref[0], own_ref[...], other_ref[...])
        g_ref[...] = gg
        mm = ADAM_B1 * m_ref[...] + (1.0 - ADAM_B1) * gg
        vv = ADAM_B2 * v_ref[...] + (1.0 - ADAM_B2) * (gg * gg)
        m_hat = mm / (1.0 - ADAM_B1 ** ADAM_STEP)
        v_hat = vv / (1.0 - ADAM_B2 ** ADAM_STEP)
        d_ref[...] = -ADAM_LR * (m_hat / (jnp.sqrt(v_hat) + ADAM_EPS) + ADAM_WD * w_ref[...])
        mo_ref[...] = mm
        vo_ref[...] = vv

    return pl.pallas_call(
        body,
        name="adamw_" + name,
        grid_spec=pltpu.PrefetchScalarGridSpec(
            num_scalar_prefetch=1, grid=(2, nt),
            in_specs=[whole, part, part, whole, whole], out_specs=[whole] * 4),
        out_shape=[jax.ShapeDtypeStruct((r, cols), F32)] * 4,
        compiler_params=_cparams(("parallel", "parallel")),
    )(c, w, own, other, m, v)


GATHER_FIRST = ("ffn1_w_in", "ffn1_w_out")
GATHER_PROJ = ("w_in",)
GATHER_LATE = ("w_branch_a", "w_branch_b", "w_out", "ffn2_w_in", "ffn2_w_out")


class _Comm:
    def __init__(self, shards, c_arr, mine_arr):
        self.shards, self.c, self.mine = shards, c_arr, mine_arr
        self.groups, self.halves = {}, {}
        self.by_name = {entry[0]: entry for entry in BIG}

    def small_gather(self, loss, small):
        pad_lanes = lambda a: jnp.concatenate([a, jnp.zeros((1, LANES - a.shape[1]), F32)], axis=1)
        buf = jnp.concatenate([
            small["meta_tokens"].reshape(128, LANES),
            small["ffn1_norm"].reshape(8, LANES), small["mix_norm"].reshape(8, LANES),
            small["ffn2_norm"].reshape(8, LANES), small["final_norm"].reshape(8, LANES),
            loss, pad_lanes(small["b_forget"]), pad_lanes(small["attn_sinks"]),
            jnp.zeros((SMALL_ROWS - 163, LANES), F32)], axis=0)
        return _small_exchange(buf)

    def small_gathered(self, outs):
        self.reduced = _sum_devices(outs[0])

    def gather(self, names):
        table = [self.by_name[n] + (BF16, True) for n in names]
        return _gather_exchange([self.shards[n] for n in names], table)

    def gathered(self, names, outs):
        return [o.transpose(1, 0, 2).reshape(D_MODEL, W_IN_COLS) if n == STACKED else o for n, o in zip(names, outs)]

    def swap(self, tag, grads):
        entries = [self.by_name[n] for n in grads]
        arrays = [g.reshape(D_MODEL, N_CHIPS, W_IN_COLS // N_CHIPS).transpose(1, 0, 2) if n == STACKED else g
                  for n, g in grads.items()]
        self.groups[tag] = (entries, arrays)
        return _halves_exchange(arrays, entries)

    def scatter(self, tag, received):
        entries, arrays = self.groups[tag]
        views = [_halves_view(*entry) for entry in entries]
        partials = [_add_sibling(g.reshape(v[0]), r, self.c, name)
                    for g, v, r, (name, _, _) in zip(arrays, views, received, entries)]
        self.groups[tag] = (entries, partials)
        return _scatter_exchange(partials, entries)

    def received(self, tag, pieces):
        entries, partials = self.groups[tag]
        for p, r, (name, _, axis) in zip(partials, pieces, entries):
            self.halves[name] = _add_chips(p, r, self.mine, name, axis)

    def finish(self):
        names = [n for n, _, _ in BIG]
        own = [self.halves[n] for n in names]
        return dict(zip(names, zip(own, _share_with_sibling(own))))


def kernel(x, meta_tokens, ffn1_norm, ffn1_w_in, ffn1_w_out, mix_norm, w_in, b_forget, attn_sinks, w_branch_a, w_branch_b, w_out, ffn2_norm, ffn2_w_in, ffn2_w_out, final_norm, loss_target, m_meta_tokens, m_ffn1_norm, m_ffn1_w_in, m_ffn1_w_out, m_mix_norm, m_w_in, m_b_forget, m_attn_sinks, m_w_branch_a, m_w_branch_b, m_w_out, m_ffn2_norm, m_ffn2_w_in, m_ffn2_w_out, m_final_norm, v_meta_tokens, v_ffn1_norm, v_ffn1_w_in, v_ffn1_w_out, v_mix_norm, v_w_in, v_b_forget, v_attn_sinks, v_w_branch_a, v_w_branch_b, v_w_out, v_ffn2_norm, v_ffn2_w_in, v_ffn2_w_out, v_final_norm):
    given = dict(locals())
    names = ["meta_tokens", "ffn1_norm", "ffn1_w_in", "ffn1_w_out", "mix_norm", "w_in", "b_forget", "attn_sinks",
             "w_branch_a", "w_branch_b", "w_out", "ffn2_norm", "ffn2_w_in", "ffn2_w_out", "final_norm"]
    big_names = [n for n, _, _ in BIG]
    cx, cy, cc = _coords()
    c_arr = cc.reshape(1).astype(jnp.int32)
    mine_arr = (2 * cx + cy).reshape(1).astype(jnp.int32)

    comm = _Comm({n: given[n][0].astype(BF16) for n in big_names}, c_arr, mine_arr)
    table = [comm.by_name[n] + (BF16, True) for n in GATHER_FIRST] + [("meta_tokens", (N_META, D_MODEL), 1, F32, False)]
    first = _gather_exchange([comm.shards[n] for n in GATHER_FIRST] + [meta_tokens], table)
    w1i, w1o, meta_full = _run_exchange(first, "gather_first")
    norms = (ffn1_norm, mix_norm, ffn2_norm, final_norm.reshape(1, D_MODEL))
    loss, grad_x, small, big = _local_step(x, loss_target, meta_full, norms, b_forget, attn_sinks, (w1i, w1o), comm)

    swap = comm.swap("ffn1_in", dict(ffn1_w_in=big["ffn1_w_in"]))
    last = comm.scatter("ffn1_in", _run_exchange(swap, "exchange_halves_ffn1_in"))
    comm.received("ffn1_in", _run_exchange(last, "scatter_chip_sums"))
    grad_halves = comm.finish()
    grads = {}

    red = comm.reduced
    meta_cols = red[:128].reshape(N_META, D_MODEL)
    grads["meta_tokens"] = lax.dynamic_slice_in_dim(meta_cols, (2 * cx + cy) * (D_MODEL // N_CHIPS),
                                                    D_MODEL // N_CHIPS, axis=1)
    grads["ffn1_norm"] = red[128:136].reshape(1, D_MODEL)
    grads["mix_norm"] = red[136:144].reshape(1, D_MODEL)
    grads["ffn2_norm"] = red[144:152].reshape(1, D_MODEL)
    grads["final_norm"] = red[152:160].reshape(1, D_MODEL)
    loss_out = red[160, 0]
    grads["b_forget"] = red[161:162, :B_HEADS]
    grads["attn_sinks"] = red[162:163, :A_HEADS]

    out_g, out_d, out_m, out_v = [], [], [], []
    for n in names:
        w_full = given[n]
        shape = w_full.shape
        two_d = (lambda a: a.reshape(shape[-2], shape[-1])) if len(shape) >= 2 else (lambda a: a.reshape(1, shape[0]))
        if n == STACKED:
            own, other = grad_halves[n]
            g_t = jnp.concatenate([jnp.where(cc == 0, own, other), jnp.where(cc == 0, other, own)], axis=0).T
            tiles = lambda a: a.reshape(shape[-1], shape[-2] // LANES, LANES)
            untile = lambda a: a.reshape(shape[-1], shape[-2]).T
            d2, m2, v2 = [untile(a) for a in _adamw(tiles(two_d(w_full).T), tiles(g_t), tiles(two_d(given["m_" + n]).T),
                                                     tiles(two_d(given["v_" + n]).T))]
            g2 = g_t.T
        elif n in grad_halves:
            own, other = grad_halves[n]
            g2, d2, m2, v2 = _adamw_halves(two_d(w_full), own, other, two_d(given["m_" + n]),
                                           two_d(given["v_" + n]), c_arr, n)
        else:
            g2 = two_d(grads[n])
            d2, m2, v2 = _adamw(two_d(w_full), g2, two_d(given["m_" + n]), two_d(given["v_" + n]))
        out_g.append(g2.reshape(shape))
        out_d.append(d2.reshape(shape))
        out_m.append(m2.reshape(shape))
        out_v.append(v2.reshape(shape))
    return (loss_out, grad_x, *out_g, *out_d, *out_m, *out_v)
```

```python
import jax
import jax.numpy as jnp
from jax import lax
from jax.experimental import pallas as pl
from jax.experimental.pallas import tpu as pltpu

F32 = jnp.float32
BF16 = jnp.bfloat16

D_MODEL = 1024
N_META = 16
BLOCK = 128
LANES = 128
PREFIX = BLOCK
N_PAD = PREFIX - N_META
HEAD_DIM = 64
A_HEADS = 8
A_KV_HEADS = 2
A_GROUP = 4
B_HEADS = 8
B_PAIRS = B_HEADS // 2
A_WIDTH = A_HEADS * HEAD_DIM
A_KV_WIDTH = A_KV_HEADS * HEAD_DIM
B_WIDTH = B_HEADS * HEAD_DIM
W_IN_COLS = A_WIDTH + 2 * A_KV_WIDTH + 3 * B_WIDTH + B_HEADS + 2 * D_MODEL
SRC_KA = A_WIDTH
SRC_VA = SRC_KA + A_KV_WIDTH
SRC_QB = SRC_VA + A_KV_WIDTH
SRC_KB = SRC_QB + B_WIDTH
SRC_VB = SRC_KB + B_WIDTH
SRC_F = SRC_VB + B_WIDTH
SRC_GA = SRC_F + B_HEADS
SRC_GB = SRC_GA + D_MODEL
A_PAD_WIDTH = A_HEADS * LANES
B_PAD_WIDTH = B_HEADS * LANES
F_COLS = LANES
OFF_QA = 0
OFF_KA = SRC_KA
OFF_VA = SRC_VA
OFF_QB = SRC_QB
OFF_KB = SRC_KB
OFF_VB = SRC_VB
OFF_GA = SRC_F
OFF_GB = OFF_GA + D_MODEL
OFF_F = OFF_GB + D_MODEL
P_COLS = OFF_F + F_COLS
EPS = 1e-6
NEG = -1e30
SCALE = HEAD_DIM ** -0.5
KEY_BLOCKS = 4

ADAM_LR = 0.001
ADAM_B1 = 0.9
ADAM_B2 = 0.999
ADAM_EPS = 1e-08
ADAM_WD = 0.01
ADAM_STEP = 10

N_CHIPS = 4
N_DEV = 8
VMEM_LIMIT = 56 * 1024 * 1024

NT_DIMS = (((1,), (1,)), ((), ()))
TN_DIMS = (((0,), (0,)), ((), ()))
MESH_ID = pl.DeviceIdType.MESH
HBM_SPEC = pl.BlockSpec(memory_space=pltpu.HBM)
VMEM_SPEC = pl.BlockSpec(memory_space=pltpu.VMEM)


def _tile(n, target, mult=16):
    best = None
    for t in range(mult, min(n, target) + 1, mult):
        if n % t == 0:
            best = t
    return best if best is not None else n


def _cparams(sem):
    return pltpu.CompilerParams(dimension_semantics=sem, vmem_limit_bytes=VMEM_LIMIT)


def _rms_scale(h):
    return lax.rsqrt(jnp.mean(h * h, axis=-1, keepdims=True) + EPS)


def _rms_bwd(dn, h, w):
    r = _rms_scale(h)
    dw = jnp.sum(dn * (h * r), axis=0, keepdims=True)
    z = dn * w
    dh = r * z - h * ((r * r * r) * jnp.mean(z * h, axis=-1, keepdims=True))
    return dh, dw


def _ffn_fwd(h, norm_w, w_in, w_out, exchange=None):
    t, d = h.shape
    f = w_out.shape[0]
    tm = _tile(t, 272)
    tc = _tile(f, 256, 128)
    nj = f // tc
    ni = t // tm
    n_x = len(exchange.ins) if exchange else 0

    def body(*refs):
        h_ref, nw_ref, wi_ref, wo_ref = refs[:4]
        hout_ref, g_ref, u_ref = refs[4 + n_x:7 + n_x]
        a_scr = refs[7 + 2 * n_x]
        i = pl.program_id(0)
        if exchange:
            _host_exchange(exchange, refs[4:4 + n_x], refs[7 + n_x:7 + 2 * n_x], refs[8 + 2 * n_x:],
                           i == 0, i == ni // 3, i == ni - 1)
        hh = h_ref[...]
        n = ((hh * _rms_scale(hh)) * nw_ref[...]).astype(BF16)
        for j in range(nj):
            cols = slice(j * tc, (j + 1) * tc)
            g = jnp.dot(n, wi_ref[:, j * tc:(j + 1) * tc], preferred_element_type=F32)
            u = jnp.dot(n, wi_ref[:, f + j * tc:f + (j + 1) * tc], preferred_element_type=F32)
            g_ref[:, cols] = g
            u_ref[:, cols] = u
            a_scr[:, cols] = ((g * jax.nn.sigmoid(g)) * u).astype(BF16)
        hout_ref[...] = hh + 0.5 * jnp.dot(a_scr[...], wo_ref[...], preferred_element_type=F32)

    resident = lambda a: pl.BlockSpec(a.shape, lambda i: (0, 0), pipeline_mode=pl.Buffered(1))
    row = lambda w: pl.BlockSpec((tm, w), lambda i: (i, 0))
    outs = pl.pallas_call(
        body,
        name="ffn_fwd",
        grid=(ni,),
        in_specs=[row(d), pl.BlockSpec((1, d), lambda i: (0, 0)), resident(w_in), resident(w_out)] + [HBM_SPEC] * n_x,
        out_specs=[row(d), row(f), row(f)] + [HBM_SPEC] * n_x,
        out_shape=[
            jax.ShapeDtypeStruct((t, d), F32),
            jax.ShapeDtypeStruct((t, f), F32),
            jax.ShapeDtypeStruct((t, f), F32),
        ] + (exchange.out_shape if exchange else []),
        scratch_shapes=[pltpu.VMEM((tm, f), BF16)] + (exchange.scratch if exchange else []),
        compiler_params=_cparams(("arbitrary",) if exchange else ("parallel",)),
    )(h, norm_w, w_in, w_out, *(exchange.ins if exchange else []))
    return outs[:3], outs[3:]


def _ffn_bwd(dh_out, h, norm_w, g, u, w_in, w_out, exchange=None):
    t, d = h.shape
    f = w_out.shape[0]
    tm = _tile(t, 272)
    tc = _tile(f, 256, 128)
    nj = f // tc
    ni = t // tm
    n_x = len(exchange.ins) if exchange else 0

    def body(*refs):
        dho_ref, h_ref, nw_ref, g_ref, u_ref, wi_ref, wo_ref = refs[:7]
        dhin_ref, n_ref, a_ref, dgu_ref, df_ref, dnw_ref = refs[7 + n_x:13 + n_x]
        i = pl.program_id(0)
        if exchange:
            _host_exchange(exchange, refs[7:7 + n_x], refs[13 + n_x:13 + 2 * n_x], refs[13 + 2 * n_x:],
                           i == 0, i == ni - 1, i == ni - 1)
        hh = h_ref[...]
        nw = nw_ref[...]
        n_ref[...] = ((hh * _rms_scale(hh)) * nw).astype(BF16)
        dho = dho_ref[...]
        df = (0.5 * dho).astype(BF16)
        df_ref[...] = df
        for j in range(nj):
            cols = slice(j * tc, (j + 1) * tc)
            da = lax.dot_general(df, wo_ref[cols, :], NT_DIMS, preferred_element_type=F32)
            gg = g_ref[:, cols]
            uu = u_ref[:, cols]
            sig = jax.nn.sigmoid(gg)
            sl = gg * sig
            a_ref[:, cols] = (sl * uu).astype(BF16)
            dgu_ref[0, :, cols] = ((da * uu) * (sig * (1.0 + gg * (1.0 - sig)))).astype(BF16)
            dgu_ref[1, :, cols] = (da * sl).astype(BF16)
        dn = (lax.dot_general(dgu_ref[0], wi_ref[:, :f], NT_DIMS, preferred_element_type=F32)
              + lax.dot_general(dgu_ref[1], wi_ref[:, f:], NT_DIMS, preferred_element_type=F32))
        dh, dw = _rms_bwd(dn, hh, nw)
        dhin_ref[...] = dho + dh
        dnw_ref[0] = dw

    resident = lambda a: pl.BlockSpec(a.shape, lambda i: (0, 0), pipeline_mode=pl.Buffered(1))
    row = lambda w: pl.BlockSpec((tm, w), lambda i: (i, 0))
    outs = pl.pallas_call(
        body,
        name="ffn_bwd",
        grid=(ni,),
        in_specs=[row(d), row(d), pl.BlockSpec((1, d), lambda i: (0, 0)), row(f), row(f),
                  resident(w_in), resident(w_out)] + [HBM_SPEC] * n_x,
        out_specs=[row(d), row(d), row(f), pl.BlockSpec((2, tm, f), lambda i: (0, i, 0)), row(d),
                   pl.BlockSpec((1, 1, d), lambda i: (i, 0, 0))] + [HBM_SPEC] * n_x,
        out_shape=[
            jax.ShapeDtypeStruct((t, d), F32),
            jax.ShapeDtypeStruct((t, d), BF16),
            jax.ShapeDtypeStruct((t, f), BF16),
            jax.ShapeDtypeStruct((2, t, f), BF16),
            jax.ShapeDtypeStruct((t, d), BF16),
            jax.ShapeDtypeStruct((ni, 1, d), F32),
        ] + (exchange.out_shape if exchange else []),
        scratch_shapes=exchange.scratch if exchange else [],
        compiler_params=_cparams(("arbitrary",) if exchange else ("parallel",)),
    )(dh_out, h, norm_w, g, u, w_in, w_out, *(exchange.ins if exchange else []))
    return outs[:6], outs[6:]


def _tn_matmul(a, b, name, exchange=None):
    t, k = a.shape
    split = b.ndim == 3
    n = 2 * b.shape[2] if split else b.shape[1]
    tk = _tile(k, 512, 128)
    tn = _tile(b.shape[-1], 1408, 128)
    per_half = b.shape[-1] // tn
    ni, nj = k // tk, n // tn
    n_x = len(exchange.ins) if exchange else 0

    def body(*refs):
        a_ref, b_ref, o_ref = refs[0], refs[1], refs[2 + n_x]
        if exchange:
            i, j = pl.program_id(0), pl.program_id(1)
            at_end = (i == ni - 1) & (j == nj - 1)
            _host_exchange(exchange, refs[2:2 + n_x], refs[3 + n_x:3 + 2 * n_x], refs[3 + 2 * n_x:],
                           (i == 0) & (j == 0), at_end, at_end)
        o_ref[...] = lax.dot_general(a_ref[...], b_ref[...], TN_DIMS, preferred_element_type=F32)

    if split:
        b_spec = pl.BlockSpec((None, t, tn), lambda i, j: (j // per_half, 0, j % per_half))
    else:
        b_spec = pl.BlockSpec((t, tn), lambda i, j: (0, j))
    outs = pl.pallas_call(
        body,
        name=name,
        grid=(ni, nj),
        in_specs=[pl.BlockSpec((t, tk), lambda i, j: (0, i)), b_spec] + [HBM_SPEC] * n_x,
        out_specs=[pl.BlockSpec((tk, tn), lambda i, j: (i, j))] + [HBM_SPEC] * n_x,
        out_shape=[jax.ShapeDtypeStruct((k, n), F32)] + (exchange.out_shape if exchange else []),
        scratch_shapes=exchange.scratch if exchange else [],
        compiler_params=_cparams(("arbitrary", "arbitrary") if exchange else ("parallel", "parallel")),
    )(a, b, *(exchange.ins if exchange else []))
    return (outs[0], outs[1:]) if exchange else outs[0]


A_SLOT = lambda h: h // A_GROUP
B_SLOT = lambda h: h % 2

PROJ_PARTS = (
    (OFF_QA, A_WIDTH, A_PAD_WIDTH, True, A_SLOT), (OFF_KA, A_KV_WIDTH, A_KV_WIDTH, True, None),
    (OFF_VA, A_KV_WIDTH, A_KV_WIDTH, True, None), (OFF_QB, B_WIDTH, B_WIDTH, True, None),
    (OFF_KB, B_WIDTH, B_PAD_WIDTH, True, B_SLOT), (OFF_VB, B_WIDTH, B_PAD_WIDTH, True, B_SLOT),
    (OFF_GA, D_MODEL, D_MODEL, False, None), (OFF_GB, D_MODEL, D_MODEL, False, None), (OFF_F, F_COLS, F_COLS, False, None),
)


def _head_tile(pair, head, slot):
    lane_slot = lax.broadcasted_iota(jnp.int32, pair.shape, 1) // HEAD_DIM
    moved = pair if head % 2 == slot else pltpu.roll(pair, HEAD_DIM, 1)
    return jnp.where(lane_slot == slot, moved, 0.0)


def _proj_fwd(h, norm_w, w_p):
    t, d = h.shape
    tm = _tile(t, 272)

    def body(h_ref, nw_ref, w_ref, u_ref, *part_refs):
        hh = h_ref[...]
        un = ((hh * _rms_scale(hh)) * nw_ref[...]).astype(BF16)
        u_ref[...] = un
        for (off, width, _, _, slot), p_ref in zip(PROJ_PARTS, part_refs):
            if slot is None:
                p_ref[...] = jnp.dot(un, w_ref[:, off:off + width], preferred_element_type=F32).astype(p_ref.dtype)
                continue
            for pair in range(width // LANES):
                x = jnp.dot(un, w_ref[:, off + pair * LANES:off + (pair + 1) * LANES], preferred_element_type=F32)
                for head in (2 * pair, 2 * pair + 1):
                    p_ref[:, head * LANES:(head + 1) * LANES] = _head_tile(x, head, slot(head)).astype(p_ref.dtype)

    row = lambda w: pl.BlockSpec((tm, w), lambda i: (i, 0))
    return pl.pallas_call(
        body,
        name="proj_fwd",
        grid=(t // tm,),
        in_specs=[row(d), pl.BlockSpec((1, d), lambda i: (0, 0)),
                  pl.BlockSpec(w_p.shape, lambda i: (0, 0), pipeline_mode=pl.Buffered(1))],
        out_specs=[row(d)] + [row(width) for _, _, width, _, _ in PROJ_PARTS],
        out_shape=[jax.ShapeDtypeStruct((t, d), BF16)]
        + [jax.ShapeDtypeStruct((t, width), BF16 if is_bf else F32) for _, _, width, is_bf, _ in PROJ_PARTS],
        compiler_params=_cparams(("parallel",)),
    )(h, norm_w, w_p)


def _proj_bwd(dh_out, h, norm_w, dproj, w_p, exchange=None):
    t, d = h.shape
    n = w_p.shape[1]
    tm = _tile(t, 272)
    ni = t // tm
    n_x = len(exchange.ins) if exchange else 0

    def body(*refs):
        dho_ref, h_ref, nw_ref, dp_ref, w_ref = refs[:5]
        dhin_ref, dnw_ref = refs[5 + n_x:7 + n_x]
        if exchange:
            i = pl.program_id(0)
            _host_exchange(exchange, refs[5:5 + n_x], refs[7 + n_x:7 + 2 * n_x], refs[7 + 2 * n_x:],
                           i == 0, i == ni - 1, i == ni - 1)
        dn = lax.dot_general(dp_ref[...], w_ref[...], NT_DIMS, preferred_element_type=F32)
        dh, dw = _rms_bwd(dn, h_ref[...], nw_ref[...])
        dhin_ref[...] = dho_ref[...] + dh
        dnw_ref[0] = dw

    row = lambda w: pl.BlockSpec((tm, w), lambda i: (i, 0))
    outs = pl.pallas_call(
        body,
        name="proj_bwd",
        grid=(ni,),
        in_specs=[row(d), row(d), pl.BlockSpec((1, d), lambda i: (0, 0)), row(n),
                  pl.BlockSpec(w_p.shape, lambda i: (0, 0), pipeline_mode=pl.Buffered(1))] + [HBM_SPEC] * n_x,
        out_specs=[row(d), pl.BlockSpec((1, 1, d), lambda i: (i, 0, 0))] + [HBM_SPEC] * n_x,
        out_shape=[jax.ShapeDtypeStruct((t, d), F32), jax.ShapeDtypeStruct((ni, 1, d), F32)]
        + (exchange.out_shape if exchange else []),
        scratch_shapes=exchange.scratch if exchange else [],
        compiler_params=_cparams(("arbitrary",) if exchange else ("parallel",)),
    )(dh_out, h, norm_w, dproj, w_p, *(exchange.ins if exchange else []))
    return outs[:2], outs[2:]


def _merge_fwd(h, oa, ob, ga, gb, wa, wb, wo):
    t, d = h.shape
    tm = _tile(t, 544)

    def body(h_ref, oa_ref, ob_ref, ga_ref, gb_ref, wa_ref, wb_ref, wo_ref, hout_ref, mix_ref):
        ya = jnp.dot(oa_ref[...], wa_ref[...], preferred_element_type=F32)
        yb = jnp.dot(ob_ref[...], wb_ref[...], preferred_element_type=F32)
        mixed = (jax.nn.sigmoid(ga_ref[...]) * ya + jax.nn.sigmoid(gb_ref[...]) * yb).astype(BF16)
        mix_ref[...] = mixed
        hout_ref[...] = h_ref[...] + jnp.dot(mixed, wo_ref[...], preferred_element_type=F32)

    row = lambda w: pl.BlockSpec((tm, w), lambda i: (i, 0))
    full = lambda a: pl.BlockSpec(a.shape, lambda i: (0, 0))
    return pl.pallas_call(
        body,
        name="merge_fwd",
        grid=(t // tm,),
        in_specs=[row(d), row(oa.shape[1]), row(ob.shape[1]), row(d), row(d), full(wa), full(wb), full(wo)],
        out_specs=[row(d), row(d)],
        out_shape=[jax.ShapeDtypeStruct((t, d), F32), jax.ShapeDtypeStruct((t, d), BF16)],
        compiler_params=_cparams(("parallel",)),
    )(h, oa, ob, ga, gb, wa, wb, wo)


def _merge_bwd(dh, oa, ob, ga, gb, wa, wb, wo, exchange=None):
    t, d = dh.shape
    tm = _tile(t, 544)
    ni = t // tm
    n_x = len(exchange.ins) if exchange else 0

    def body(*refs):
        dh_ref, oa_ref, ob_ref, ga_ref, gb_ref, wa_ref, wb_ref, wo_ref = refs[:8]
        dya_ref, dyb_ref, doa_ref, dob_ref, dga_ref, dgb_ref, dhb_ref = refs[8 + n_x:15 + n_x]
        if exchange:
            i = pl.program_id(0)
            _host_exchange(exchange, refs[8:8 + n_x], refs[15 + n_x:15 + 2 * n_x], refs[15 + 2 * n_x:],
                           i == 0, i == ni - 1, i == ni - 1)
        dhb = dh_ref[...].astype(BF16)
        dhb_ref[...] = dhb
        dmix = lax.dot_general(dhb, wo_ref[...], NT_DIMS, preferred_element_type=F32)
        for o_ref, g_ref, w_ref, dy_ref, do_ref, dg_ref in (
                (oa_ref, ga_ref, wa_ref, dya_ref, doa_ref, dga_ref),
                (ob_ref, gb_ref, wb_ref, dyb_ref, dob_ref, dgb_ref)):
            y = jnp.dot(o_ref[...], w_ref[...], preferred_element_type=F32)
            s = jax.nn.sigmoid(g_ref[...])
            dy = (dmix * s).astype(BF16)
            dy_ref[...] = dy
            dg_ref[...] = ((dmix * y) * (s * (1.0 - s))).astype(BF16)
            do_ref[...] = lax.dot_general(dy, w_ref[...], NT_DIMS, preferred_element_type=F32).astype(BF16)

    row = lambda w: pl.BlockSpec((tm, w), lambda i: (i, 0))
    full = lambda a: pl.BlockSpec(a.shape, lambda i: (0, 0))
    wa_w, wb_w = oa.shape[1], ob.shape[1]
    outs = pl.pallas_call(
        body,
        name="merge_bwd",
        grid=(ni,),
        in_specs=[row(d), row(wa_w), row(wb_w), row(d), row(d), full(wa), full(wb), full(wo)] + [HBM_SPEC] * n_x,
        out_specs=[row(d), row(d), row(wa_w), row(wb_w), row(d), row(d), row(d)] + [HBM_SPEC] * n_x,
        out_shape=[
            jax.ShapeDtypeStruct((t, d), BF16), jax.ShapeDtypeStruct((t, d), BF16),
            jax.ShapeDtypeStruct((t, wa_w), BF16), jax.ShapeDtypeStruct((t, wb_w), BF16),
            jax.ShapeDtypeStruct((t, d), BF16), jax.ShapeDtypeStruct((t, d), BF16),
            jax.ShapeDtypeStruct((t, d), BF16),
        ] + (exchange.out_shape if exchange else []),
        scratch_shapes=exchange.scratch if exchange else [],
        compiler_params=_cparams(("arbitrary",) if exchange else ("parallel",)),
    )(dh, oa, ob, ga, gb, wa, wb, wo, *(exchange.ins if exchange else []))
    return outs[:7], outs[7:]


def _tri_dot(tri, x):
    hi = x.astype(BF16)
    r1 = x - hi.astype(F32)
    mid = r1.astype(BF16)
    lo = (r1 - mid.astype(F32)).astype(BF16)
    return (jnp.dot(tri, hi, preferred_element_type=F32)
            + jnp.dot(tri, mid, preferred_element_type=F32)
            + jnp.dot(tri, lo, preferred_element_type=F32))


def _forget_cumsum(f_logit, b_pad, nb):
    t, w = f_logit.shape
    bsz = t // (nb * BLOCK)

    def body(f_ref, b_ref, c_ref, carry):
        n = pl.program_id(1)

        @pl.when(n == 0)
        def _():
            carry[...] = jnp.zeros_like(carry)

        x = jax.nn.log_sigmoid(f_ref[...] + b_ref[...])
        rows = lax.broadcasted_iota(jnp.int32, (BLOCK, BLOCK), 0)
        cols = lax.broadcasted_iota(jnp.int32, (BLOCK, BLOCK), 1)
        tri = (cols <= rows).astype(BF16)
        c = _tri_dot(tri, x) + carry[...]
        c_ref[...] = c
        carry[...] = c[BLOCK - 1:BLOCK, :]

    return pl.pallas_call(
        body,
        name="forget_cumsum",
        grid=(bsz, nb),
        in_specs=[pl.BlockSpec((BLOCK, w), lambda b, n: (b * nb + n, 0)),
                  pl.BlockSpec((1, w), lambda b, n: (0, 0))],
        out_specs=pl.BlockSpec((BLOCK, w), lambda b, n: (b * nb + n, 0)),
        out_shape=jax.ShapeDtypeStruct((t, w), F32),
        scratch_shapes=[pltpu.VMEM((1, w), F32)],
        compiler_params=_cparams(("parallel", "arbitrary")),
    )(f_logit, b_pad)


def _forget_cumsum_bwd(dc, f_logit, b_pad, nb):
    t, w = f_logit.shape
    bsz = t // (nb * BLOCK)

    def body(dc_ref, f_ref, b_ref, df_ref, db_ref, carry):
        n = pl.program_id(1)

        @pl.when(n == 0)
        def _():
            carry[...] = jnp.zeros_like(carry)
            db_ref[...] = jnp.zeros_like(db_ref)

        rows = lax.broadcasted_iota(jnp.int32, (BLOCK, BLOCK), 0)
        cols = lax.broadcasted_iota(jnp.int32, (BLOCK, BLOCK), 1)
        tri = (cols >= rows).astype(BF16)
        dlf = _tri_dot(tri, dc_ref[...]) + carry[...]
        carry[...] = dlf[0:1, :]
        df = dlf * jax.nn.sigmoid(-(f_ref[...] + b_ref[...]))
        df_ref[...] = df.astype(BF16)
        db_ref[0] += jnp.sum(df, axis=0, keepdims=True)

    rev = lambda b, n: (b * nb + (nb - 1 - n), 0)
    return pl.pallas_call(
        body,
        name="forget_cumsum_bwd",
        grid=(bsz, nb),
        in_specs=[pl.BlockSpec((BLOCK, w), rev),
                  pl.BlockSpec((BLOCK, w), rev),
                  pl.BlockSpec((1, w), lambda b, n: (0, 0))],
        out_specs=[pl.BlockSpec((BLOCK, w), rev),
                   pl.BlockSpec((1, 1, w), lambda b, n: (b, 0, 0))],
        out_shape=[jax.ShapeDtypeStruct((t, w), BF16), jax.ShapeDtypeStruct((bsz, 1, w), F32)],
        scratch_shapes=[pltpu.VMEM((1, w), F32)],
        compiler_params=_cparams(("parallel", "arbitrary")),
    )(dc, f_logit, b_pad)


GROUP_ROWS = A_GROUP * BLOCK


def _stack_heads(ref, g):
    return jnp.concatenate([ref[:, (A_GROUP * g + i) * LANES:(A_GROUP * g + i + 1) * LANES] for i in range(A_GROUP)],
                           axis=0)


def _unstack_heads(ref, g, x):
    for i in range(A_GROUP):
        ref[:, (A_GROUP * g + i) * LANES:(A_GROUP * g + i + 1) * LANES] = x[i * BLOCK:(i + 1) * BLOCK].astype(ref.dtype)


def _swa_logits(q, keys, slope, n):
    qi = lax.broadcasted_iota(jnp.int32, (GROUP_ROWS, BLOCK), 0) & (BLOCK - 1)
    kj = lax.broadcasted_iota(jnp.int32, (GROUP_ROWS, BLOCK), 1)
    s_all = lax.dot_general(q, keys, NT_DIMS, preferred_element_type=F32) * SCALE
    out = []
    for i, (dist, ok) in enumerate((
            (n * BLOCK + qi - kj, (kj >= N_PAD) & (n * BLOCK + qi - kj >= 0)),
            (BLOCK + qi - kj, (kj > qi) & (n >= 2)),
            (qi - kj, (kj <= qi) & (n >= 1)))):
        s = s_all[:, i * BLOCK:(i + 1) * BLOCK] - slope * dist.astype(F32)
        out.append(jnp.where(ok, s, NEG))
    return out


def _three_blocks(m_ref, p_ref, c_ref):
    return jnp.concatenate([m_ref[...], p_ref[...], c_ref[...]], axis=0)


def _swa_specs(nb):
    row = lambda b, n: b * nb + n
    qspec = pl.BlockSpec((BLOCK, A_PAD_WIDTH), lambda b, n: (row(b, n), 0))
    kv_m = pl.BlockSpec((BLOCK, LANES), lambda b, n: (row(b, 0), 0))
    kv_p = pl.BlockSpec((BLOCK, LANES), lambda b, n: (row(b, jnp.maximum(n - 1, 0)), 0))
    kv_c = pl.BlockSpec((BLOCK, LANES), lambda b, n: (row(b, n), 0))
    rowspec = pl.BlockSpec((A_KV_HEADS, GROUP_ROWS, 1), lambda b, n: (0, 0, 0))
    lsespec = pl.BlockSpec((1, A_KV_HEADS, GROUP_ROWS, 1), lambda b, n: (row(b, n), 0, 0, 0))
    return qspec, kv_m, kv_p, kv_c, rowspec, lsespec


def _swa_fwd(q, k, v, sink_rows, slope_rows, nb):
    t = q.shape[0]
    bsz = t // (nb * BLOCK)

    def body(q_ref, km_ref, kp_ref, kc_ref, vm_ref, vp_ref, vc_ref, sink_ref, slope_ref, o_ref, lse_ref):
        n = pl.program_id(1)
        keys = _three_blocks(km_ref, kp_ref, kc_ref)
        values = _three_blocks(vm_ref, vp_ref, vc_ref)
        lane_group = lax.broadcasted_iota(jnp.int32, (GROUP_ROWS, LANES), 1) // HEAD_DIM
        for g in range(A_KV_HEADS):
            qq = _stack_heads(q_ref, g)
            sink = sink_ref[g]
            s_m, s_p, s_c = _swa_logits(qq, keys, slope_ref[g], n)
            m = jnp.maximum(jnp.max(jnp.maximum(jnp.maximum(s_m, s_p), s_c), axis=-1, keepdims=True), sink)
            m_wide = jnp.broadcast_to(m, (GROUP_ROWS, BLOCK))
            e_m = jnp.exp(s_m - m_wide)
            e_p = jnp.exp(s_p - m_wide)
            e_c = jnp.exp(s_c - m_wide)
            z = jnp.sum((e_m + e_p) + e_c, axis=-1, keepdims=True) + jnp.exp(sink - m)
            inv = jnp.broadcast_to(1.0 / z, (GROUP_ROWS, BLOCK))
            probs = jnp.concatenate([(e_m * inv).astype(BF16), (e_p * inv).astype(BF16), (e_c * inv).astype(BF16)],
                                    axis=1)
            o = jnp.dot(probs, values, preferred_element_type=F32)
            _unstack_heads(o_ref, g, jnp.where(lane_group == g, o, 0.0))
            lse_ref[0, g] = m + jnp.log(z)

    qspec, kv_m, kv_p, kv_c, rowspec, lsespec = _swa_specs(nb)
    return pl.pallas_call(
        body,
        name="swa_fwd",
        grid=(bsz, nb),
        in_specs=[qspec, kv_m, kv_p, kv_c, kv_m, kv_p, kv_c, rowspec, rowspec],
        out_specs=[qspec, lsespec],
        out_shape=[jax.ShapeDtypeStruct((t, A_PAD_WIDTH), BF16),
                   jax.ShapeDtypeStruct((t // BLOCK, A_KV_HEADS, GROUP_ROWS, 1), F32)],
        compiler_params=_cparams(("parallel", "arbitrary")),
    )(q, k, k, k, v, v, v, sink_rows, slope_rows)


def _swa_bwd(q, k, v, do, lse, sink_rows, slope_rows, nb):
    t = q.shape[0]
    l = nb * BLOCK
    bsz = t // l

    def body(q_ref, km_ref, kp_ref, kc_ref, vm_ref, vp_ref, vc_ref, do_ref, lse_ref, sink_ref, slope_ref,
             dq_ref, dk_ref, dv_ref, dsink_ref, dk_acc, dv_acc):
        n = pl.program_id(1)

        @pl.when(n == 0)
        def _():
            dk_acc[...] = jnp.zeros_like(dk_acc)
            dv_acc[...] = jnp.zeros_like(dv_acc)
            dsink_ref[...] = jnp.zeros_like(dsink_ref)

        keys = _three_blocks(km_ref, kp_ref, kc_ref)
        values = _three_blocks(vm_ref, vp_ref, vc_ref)
        first_half = lax.broadcasted_iota(jnp.int32, (BLOCK, LANES), 1) < HEAD_DIM
        prev = jnp.maximum(n - 1, 0)
        for g in range(A_KV_HEADS):
            qq = _stack_heads(q_ref, g)
            dob = _stack_heads(do_ref, g)
            lse = lse_ref[0, g]
            lse_wide = jnp.broadcast_to(lse, (GROUP_ROWS, BLOCK))
            probs = [jnp.exp(s - lse_wide) for s in _swa_logits(qq, keys, slope_ref[g], n)]
            dp_all = lax.dot_general(dob, values, NT_DIMS, preferred_element_type=F32)
            dps = [dp_all[:, i * BLOCK:(i + 1) * BLOCK] for i in range(3)]
            delta = jnp.sum((probs[0] * dps[0] + probs[1] * dps[1]) + probs[2] * dps[2], axis=-1, keepdims=True)
            delta_wide = jnp.broadcast_to(delta, (GROUP_ROWS, BLOCK))
            ds = jnp.concatenate([(p * (dp - delta_wide)).astype(BF16) for p, dp in zip(probs, dps)], axis=1)
            pb = jnp.concatenate([p.astype(BF16) for p in probs], axis=1)
            dq = jnp.dot(ds, keys, preferred_element_type=F32) * SCALE
            dk_all = lax.dot_general(ds, qq, TN_DIMS, preferred_element_type=F32) * SCALE
            dv_all = lax.dot_general(pb, dob, TN_DIMS, preferred_element_type=F32)
            for i, start in enumerate((0, prev * BLOCK, n * BLOCK)):
                rows = pl.ds(pl.multiple_of(start, BLOCK), BLOCK)
                dk_acc[rows, :] += dk_all[i * BLOCK:(i + 1) * BLOCK]
                dv_acc[rows, :] += dv_all[i * BLOCK:(i + 1) * BLOCK]
            for pair in range(A_GROUP // 2):
                even = dq[2 * pair * BLOCK:(2 * pair + 1) * BLOCK]
                odd = dq[(2 * pair + 1) * BLOCK:(2 * pair + 2) * BLOCK]
                left = even if g == 0 else pltpu.roll(even, HEAD_DIM, 1)
                right = pltpu.roll(odd, HEAD_DIM, 1) if g == 0 else odd
                tile = (A_GROUP // 2) * g + pair
                dq_ref[:, tile * LANES:(tile + 1) * LANES] = jnp.where(first_half, left, right).astype(BF16)
            dsink_ref[0, g] += -(jnp.exp(sink_ref[g] - lse) * delta)

        @pl.when(n == nb - 1)
        def _():
            dk_ref[...] = dk_acc[...].astype(BF16)
            dv_ref[...] = dv_acc[...].astype(BF16)

    qspec, kv_m, kv_p, kv_c, rowspec, lsespec = _swa_specs(nb)
    kv_all = pl.BlockSpec((l, LANES), lambda b, n: (b, 0))
    return pl.pallas_call(
        body,
        name="swa_bwd",
        grid=(bsz, nb),
        in_specs=[qspec, kv_m, kv_p, kv_c, kv_m, kv_p, kv_c, qspec, lsespec, rowspec, rowspec],
        out_specs=[pl.BlockSpec((BLOCK, A_WIDTH), lambda b, n: (b * nb + n, 0)), kv_all, kv_all,
                   pl.BlockSpec((1, A_KV_HEADS, GROUP_ROWS, 1), lambda b, n: (b, 0, 0, 0))],
        out_shape=[jax.ShapeDtypeStruct((t, A_WIDTH), BF16),
                   jax.ShapeDtypeStruct((t, LANES), BF16),
                   jax.ShapeDtypeStruct((t, LANES), BF16),
                   jax.ShapeDtypeStruct((bsz, A_KV_HEADS, GROUP_ROWS, 1), F32)],
        scratch_shapes=[pltpu.VMEM((l, LANES), F32), pltpu.VMEM((l, LANES), F32)],
        compiler_params=_cparams(("parallel", "arbitrary")),
    )(q, k, k, k, v, v, v, do, lse, sink_rows, slope_rows)


CHUNK = KEY_BLOCKS * BLOCK


def _fox_chunk(qb, ci):
    sb = jnp.maximum(jnp.minimum(KEY_BLOCKS * ci, qb + 1 - KEY_BLOCKS), 0)
    lo = jnp.maximum(ci * CHUNK, N_PAD)
    return sb, lo, pl.ds(pl.multiple_of(sb * BLOCK, BLOCK), CHUNK)


def _fox_logits(s_ref, cr_ref, e, j, sb, lo, qb):
    lane = lax.broadcasted_iota(jnp.int32, (BLOCK, BLOCK), 1)
    ahead = lane - lax.broadcasted_iota(jnp.int32, (BLOCK, BLOCK), 0)
    first = (sb + j) * BLOCK
    s = s_ref[e, :, j * BLOCK:(j + 1) * BLOCK] - cr_ref[e, sb + j]
    return jnp.where((ahead <= qb * BLOCK - first) & (lane >= lo - first), s, NEG)


FOX_PAIRS = 4
FOX_HEADS = 2 * FOX_PAIRS
FOX_STEPS = B_PAIRS // FOX_PAIRS


def _fox_specs(nb):
    l = nb * BLOCK
    q_spec = pl.BlockSpec((BLOCK, FOX_PAIRS * LANES), lambda b, p, i: (b * nb + i, p))
    kv_spec = pl.BlockSpec((l, FOX_HEADS * LANES), lambda b, p, i: (b, p))
    cc_spec = pl.BlockSpec((FOX_HEADS, BLOCK, 1), lambda b, p, i: (b * FOX_STEPS + p, i, 0))
    cr_spec = pl.BlockSpec((FOX_HEADS, nb, 1, BLOCK), lambda b, p, i: (b * FOX_STEPS + p, 0, 0, 0))
    return q_spec, kv_spec, cc_spec, cr_spec


def _fox_fwd(q, k, v, c_row, nb, exchange=None):
    t = q.shape[0]
    bsz = t // (nb * BLOCK)
    assert nb >= KEY_BLOCKS

    n_x = len(exchange.ins) if exchange else 0

    def body(*refs):
        q_ref, k_ref, v_ref, cr_ref = refs[:4]
        o_ref, ox_ref, lse_ref = refs[4 + n_x:7 + n_x]
        s_scr, hi_scr, lo_scr = refs[7 + 2 * n_x:10 + 2 * n_x]
        qb = pl.program_id(2)
        if exchange:
            first = (pl.program_id(0) == 0) & (pl.program_id(1) == 0)
            last = (pl.program_id(0) == bsz - 1) & (pl.program_id(1) == FOX_STEPS - 1)
            _host_exchange(exchange, refs[4:4 + n_x], refs[7 + n_x:7 + 2 * n_x], refs[10 + 2 * n_x:],
                           first & (qb == 0), first & (qb == 2 * nb // 3), last & (qb == nb - 1))
        qs = [q_ref[:, a * LANES:(a + 1) * LANES] * SCALE for a in range(FOX_PAIRS)]
        first_half = lax.broadcasted_iota(jnp.int32, (BLOCK, LANES), 1) < HEAD_DIM

        def step(ci, carry):
            stats, accs = carry[:2 * FOX_HEADS], carry[2 * FOX_HEADS:]
            sb, lo, krows = _fox_chunk(qb, ci)
            new_stats, new_accs = [], []
            for a in range(FOX_PAIRS):
                alphas = []
                pv = jnp.zeros((BLOCK, LANES), F32)
                pv_lo = jnp.zeros((BLOCK, LANES), F32)
                for e in (2 * a, 2 * a + 1):
                    m, z = stats[2 * e], stats[2 * e + 1]
                    tile = slice(e * LANES, (e + 1) * LANES)
                    s_scr[e] = lax.dot_general(qs[a], k_ref[krows, tile], NT_DIMS, preferred_element_type=F32)
                    top = None
                    for j in range(KEY_BLOCKS):
                        s = _fox_logits(s_scr, cr_ref, e, j, sb, lo, qb)
                        s_scr[e, :, j * BLOCK:(j + 1) * BLOCK] = s
                        top = s if top is None else jnp.maximum(top, s)
                    m_new = jnp.maximum(m, jnp.max(top, axis=-1, keepdims=True))
                    alpha = jnp.exp(m - m_new)
                    m_wide = jnp.broadcast_to(m_new, (BLOCK, BLOCK))
                    total = None
                    for j in range(KEY_BLOCKS):
                        cols = slice(j * BLOCK, (j + 1) * BLOCK)
                        p = jnp.exp(s_scr[e, :, cols] - m_wide)
                        total = p if total is None else total + p
                        hi = p.astype(BF16)
                        hi_scr[e, :, cols] = hi
                        lo_scr[e, :, cols] = (p - hi.astype(F32)).astype(BF16)
                    z = alpha * z + jnp.sum(total, axis=-1, keepdims=True)
                    vv = v_ref[krows, tile]
                    pv = pv + jnp.dot(hi_scr[e], vv, preferred_element_type=F32)
                    pv_lo = pv_lo + jnp.dot(lo_scr[e], vv, preferred_element_type=F32)
                    new_stats += [m_new, z]
                    alphas.append(alpha)
                alpha = jnp.where(first_half, alphas[0], alphas[1])
                new_accs += [alpha * accs[2 * a] + pv, alpha * accs[2 * a + 1] + pv_lo]
            return (*new_stats, *new_accs)

        col = lambda val: jnp.full((BLOCK, 1), val, F32)
        done = lax.fori_loop(
            0, (qb + KEY_BLOCKS) // KEY_BLOCKS, step,
            (col(NEG), col(0.0)) * FOX_HEADS + (jnp.zeros((BLOCK, LANES), F32),) * (2 * FOX_PAIRS))
        for a in range(FOX_PAIRS):
            m0, z0, m1, z1 = done[4 * a:4 * a + 4]
            acc, acc_lo = done[2 * FOX_HEADS + 2 * a:2 * FOX_HEADS + 2 * a + 2]
            inv = 1.0 / jnp.where(first_half, z0, z1)
            tile = slice(a * LANES, (a + 1) * LANES)
            o_ref[:, tile] = (acc * inv).astype(BF16)
            ox_ref[:, tile] = (acc + acc_lo) * inv
            lse_ref[2 * a] = m0 + jnp.log(z0)
            lse_ref[2 * a + 1] = m1 + jnp.log(z1)

    q_spec, kv_spec, cc_spec, cr_spec = _fox_specs(nb)
    outs = pl.pallas_call(
        body,
        name="fox_fwd",
        grid=(bsz, FOX_STEPS, nb),
        in_specs=[q_spec, kv_spec, kv_spec, cr_spec] + [HBM_SPEC] * n_x,
        out_specs=[q_spec, q_spec, cc_spec] + [HBM_SPEC] * n_x,
        out_shape=[jax.ShapeDtypeStruct((t, B_WIDTH), BF16), jax.ShapeDtypeStruct((t, B_WIDTH), F32),
                   jax.ShapeDtypeStruct((bsz * B_HEADS, nb * BLOCK, 1), F32)] + (exchange.out_shape if exchange else []),
        scratch_shapes=[pltpu.VMEM((FOX_HEADS, BLOCK, CHUNK), F32), pltpu.VMEM((FOX_HEADS, BLOCK, CHUNK), BF16),
                        pltpu.VMEM((FOX_HEADS, BLOCK, CHUNK), BF16)] + (exchange.scratch if exchange else []),
        compiler_params=_cparams(("arbitrary",) * 3 if exchange else ("parallel", "parallel", "arbitrary")),
    )(q, k, v, c_row, *(exchange.ins if exchange else []))
    return outs[:3], outs[3:]


def _fox_bwd(q, k, v, o_exact, do, lse, c_row, nb, exchange=None):
    t = q.shape[0]
    l = nb * BLOCK
    bsz = t // l

    n_x = len(exchange.ins) if exchange else 0

    def body(*refs):
        q_ref, k_ref, v_ref, ox_ref, do_ref, lse_ref, cr_ref = refs[:7]
        dq_ref, dk_ref, dv_ref, dc_ref = refs[7 + n_x:11 + n_x]
        dk_acc, dv_acc, s_scr, dp_scr, p_scr, ds_scr = refs[11 + 2 * n_x:17 + 2 * n_x]
        qb = pl.program_id(2)
        if exchange:
            first = (pl.program_id(0) == 0) & (pl.program_id(1) == 0)
            last = (pl.program_id(0) == bsz - 1) & (pl.program_id(1) == FOX_STEPS - 1)
            _host_exchange(exchange, refs[7:7 + n_x], refs[11 + n_x:11 + 2 * n_x], refs[17 + 2 * n_x:],
                           first & (qb == 0), last & (qb == 0), last & (qb == nb - 1))

        @pl.when(qb == 0)
        def _():
            dk_acc[...] = jnp.zeros_like(dk_acc)
            dv_acc[...] = jnp.zeros_like(dv_acc)
            dc_ref[...] = jnp.zeros_like(dc_ref)

        top_half = lax.broadcasted_iota(jnp.int32, (LANES, BLOCK), 0) < HEAD_DIM
        pair_t = lambda x: jnp.concatenate([jnp.where(top_half, x.T, 0), jnp.where(top_half, 0, x.T)], axis=1)
        first_half = lax.broadcasted_iota(jnp.int32, (BLOCK, LANES), 1) < HEAD_DIM
        wide = lambda col: jnp.broadcast_to(col, (BLOCK, BLOCK))
        qs, dobs, qs_t, dob_t, deltas = [], [], [], [], []
        for a in range(FOX_PAIRS):
            tile = slice(a * LANES, (a + 1) * LANES)
            qs.append(q_ref[:, tile] * SCALE)
            dobs.append(do_ref[:, tile])
            qs_t.append(pair_t(qs[a]))
            dob_t.append(pair_t(dobs[a]))
            weighted = dobs[a].astype(F32) * ox_ref[:, tile]
            deltas += [wide(jnp.sum(jnp.where(first_half, weighted, 0.0), axis=-1, keepdims=True)),
                       wide(jnp.sum(jnp.where(first_half, 0.0, weighted), axis=-1, keepdims=True))]
        lses = [wide(lse_ref[e]) for e in range(FOX_HEADS)]

        def step(ci, dqs):
            sb, lo, krows = _fox_chunk(qb, ci)
            dqs = list(dqs)
            for a in range(FOX_PAIRS):
                for e in (2 * a, 2 * a + 1):
                    tile = slice(e * LANES, (e + 1) * LANES)
                    kk = k_ref[krows, tile]
                    s_scr[e] = lax.dot_general(qs[a], kk, NT_DIMS, preferred_element_type=F32)
                    dp_scr[e] = lax.dot_general(dobs[a], v_ref[krows, tile], NT_DIMS, preferred_element_type=F32)
                    for j in range(KEY_BLOCKS):
                        cols = slice(j * BLOCK, (j + 1) * BLOCK)
                        p = jnp.exp(_fox_logits(s_scr, cr_ref, e, j, sb, lo, qb) - lses[e])
                        ds = p * (dp_scr[e, :, cols] - deltas[e])
                        dc_ref[e, sb + j] -= jnp.sum(ds, axis=0, keepdims=True)
                        p_scr[e, :, cols] = p.astype(BF16)
                        ds_scr[e, :, cols] = ds.astype(BF16)
                    dqs[a] = dqs[a] + jnp.dot(ds_scr[e], kk, preferred_element_type=F32)
                both = slice(2 * a, 2 * a + 2)
                dk_t = jnp.dot(qs_t[a], ds_scr[both].reshape(2 * BLOCK, CHUNK), preferred_element_type=F32)
                dv_t = jnp.dot(dob_t[a], p_scr[both].reshape(2 * BLOCK, CHUNK), preferred_element_type=F32)
                for j in range(KEY_BLOCKS):
                    cols = slice(j * BLOCK, (j + 1) * BLOCK)
                    dk_acc[a * nb + sb + j] += dk_t[:, cols]
                    dv_acc[a * nb + sb + j] += dv_t[:, cols]
            return tuple(dqs)

        dqs = lax.fori_loop(0, (qb + KEY_BLOCKS) // KEY_BLOCKS, step,
                            (jnp.zeros((BLOCK, LANES), F32),) * FOX_PAIRS)
        for a in range(FOX_PAIRS):
            dq_ref[:, a * LANES:(a + 1) * LANES] = (dqs[a] * SCALE).astype(BF16)

        @pl.when(qb == nb - 1)
        def _():
            for a in range(FOX_PAIRS):
                for kb in range(nb):
                    rows = slice(kb * BLOCK, (kb + 1) * BLOCK)
                    for acc, out_ref in ((dk_acc, dk_ref), (dv_acc, dv_ref)):
                        out_ref[rows, a * LANES:(a + 1) * LANES] = acc[a * nb + kb].T.astype(BF16)

    q_spec, kv_spec, cc_spec, cr_spec = _fox_specs(nb)
    dkv_spec = pl.BlockSpec((l, FOX_PAIRS * LANES), lambda b, p, i: (b, p))
    outs = pl.pallas_call(
        body,
        name="fox_bwd",
        grid=(bsz, FOX_STEPS, nb),
        in_specs=[q_spec, kv_spec, kv_spec, q_spec, q_spec, cc_spec, cr_spec] + [HBM_SPEC] * n_x,
        out_specs=[q_spec, dkv_spec, dkv_spec, cr_spec] + [HBM_SPEC] * n_x,
        out_shape=[jax.ShapeDtypeStruct((t, B_WIDTH), BF16), jax.ShapeDtypeStruct((t, B_WIDTH), BF16),
                   jax.ShapeDtypeStruct((t, B_WIDTH), BF16),
                   jax.ShapeDtypeStruct((bsz * B_HEADS, nb, 1, BLOCK), F32)] + (exchange.out_shape if exchange else []),
        scratch_shapes=[pltpu.VMEM((FOX_PAIRS * nb, LANES, BLOCK), F32), pltpu.VMEM((FOX_PAIRS * nb, LANES, BLOCK), F32),
                        pltpu.VMEM((FOX_HEADS, BLOCK, CHUNK), F32), pltpu.VMEM((FOX_HEADS, BLOCK, CHUNK), F32),
                        pltpu.VMEM((FOX_HEADS, BLOCK, CHUNK), BF16), pltpu.VMEM((FOX_HEADS, BLOCK, CHUNK), BF16)]
        + (exchange.scratch if exchange else []),
        compiler_params=_cparams(("arbitrary",) * 3 if exchange else ("parallel", "parallel", "arbitrary")),
    )(q, k, v, o_exact, do, lse, c_row, *(exchange.ins if exchange else []))
    return outs[:4], outs[4:]


def _loss_head(h, final_w, target):
    bsz, l, d = h.shape
    nb = l // BLOCK

    def body(h_ref, w_ref, t_ref, loss_ref, dh_ref, dw_ref):
        b = pl.program_id(0)
        n = pl.program_id(1)

        @pl.when((b == 0) & (n == 0))
        def _():
            loss_ref[...] = jnp.zeros_like(loss_ref)
            dw_ref[...] = jnp.zeros_like(dw_ref)

        @pl.when(n == 0)
        def _():
            dh_ref[...] = jnp.zeros_like(dh_ref)

        @pl.when(n > 0)
        def _():
            hh = h_ref[0]
            w = w_ref[...]
            r = _rms_scale(hh)
            err = (hh * r) * w - t_ref[0]
            loss_ref[...] += 0.5 * jnp.sum(jnp.mean(err * err, axis=-1, keepdims=True), axis=0, keepdims=True)
            dy = err * (1.0 / d)
            dh, dw = _rms_bwd(dy, hh, w)
            dh_ref[0] = dh
            dw_ref[...] += dw

    return pl.pallas_call(
        body,
        name="loss_head",
        grid=(bsz, nb),
        in_specs=[
            pl.BlockSpec((1, BLOCK, d), lambda b, n: (b, n, 0)),
            pl.BlockSpec((1, d), lambda b, n: (0, 0)),
            pl.BlockSpec((1, BLOCK, d), lambda b, n: (b, jnp.maximum(n - 1, 0), 0)),
        ],
        out_specs=[
            pl.BlockSpec((1, 128), lambda b, n: (0, 0)),
            pl.BlockSpec((1, BLOCK, d), lambda b, n: (b, n, 0)),
            pl.BlockSpec((1, d), lambda b, n: (0, 0)),
        ],
        out_shape=[jax.ShapeDtypeStruct((1, 128), F32), jax.ShapeDtypeStruct((bsz, l, d), F32),
                   jax.ShapeDtypeStruct((1, d), F32)],
        compiler_params=_cparams(("arbitrary", "arbitrary")),
    )(h, final_w, target)


def _pad_tiles(w, src, heads, lane_slot, axis):
    pieces = []
    for h in range(heads):
        x = lax.slice_in_dim(w, src + HEAD_DIM * h, src + HEAD_DIM * (h + 1), axis=axis)
        z = jnp.zeros_like(x)
        pieces += [x, z] if lane_slot(h) == 0 else [z, x]
    return pieces


def _unpad_tiles(g, off, heads, lane_slot, axis):
    return [lax.slice_in_dim(g, off + LANES * h + HEAD_DIM * lane_slot(h),
                             off + LANES * h + HEAD_DIM * (lane_slot(h) + 1), axis=axis) for h in range(heads)]


def _layout_w_in(w):
    pad_f = jnp.zeros((w.shape[0], F_COLS - B_HEADS), w.dtype)
    return jnp.concatenate([w[:, :SRC_F], w[:, SRC_GA:], w[:, SRC_F:SRC_GA], pad_f], axis=1)


def _unlayout_w_in(g):
    return jnp.concatenate([g[:, :OFF_GA], g[:, OFF_F:OFF_F + B_HEADS], g[:, OFF_GA:OFF_F]], axis=1)


def _local_step(x, target, meta, norms, b_forget, sinks, w, comm=None):
    n1, nmix, n2, nfin = norms
    w1i, w1o = w[:2]
    bsz, seq, d = x.shape
    l = PREFIX + seq
    nb = l // BLOCK
    t = bsz * l

    h0 = jnp.concatenate([jnp.zeros((bsz, N_PAD, d), F32),
                          jnp.broadcast_to(meta[None], (bsz, N_META, d)), x], axis=1).reshape(t, d)

    if comm is None:
        (h1, g1, u1), _ = _ffn_fwd(h0, n1, w1i, w1o)
        w_in, wa, wb, wo, w2i, w2o = w[2:]
    else:
        (h1, g1, u1), gathered = _ffn_fwd(h0, n1, w1i, w1o, comm.gather(GATHER_PROJ))
        w_in, = comm.gathered(GATHER_PROJ, gathered)
    wp = _layout_w_in(w_in)
    un, qa, ka, va, qb, kb, vb, ga, gb, f_logit = _proj_fwd(h1, nmix, wp)
    b_pad = jnp.concatenate([b_forget, jnp.zeros((1, F_COLS - B_HEADS), F32)], axis=1)
    c = _forget_cumsum(f_logit, b_pad, nb)
    c_heads = c[:, :B_HEADS].reshape(bsz, l, B_HEADS).transpose(0, 2, 1).reshape(bsz * B_HEADS, l)
    c_row = c_heads.reshape(bsz * B_HEADS, nb, 1, BLOCK)

    slopes = jnp.exp2(-8.0 * jnp.arange(1, A_HEADS + 1, dtype=F32) / A_HEADS)
    slope_rows = jnp.repeat(slopes.reshape(A_KV_HEADS, A_GROUP), BLOCK, axis=1)[:, :, None]
    sink_rows = jnp.repeat(sinks.reshape(A_KV_HEADS, A_GROUP), BLOCK, axis=1)[:, :, None]

    oa, lse_a = _swa_fwd(qa, ka, va, sink_rows, slope_rows, nb)
    if comm is None:
        (ob, ob_exact, lse_b), _ = _fox_fwd(qb, kb, vb, c_row, nb)
    else:
        (ob, ob_exact, lse_b), gathered = _fox_fwd(qb, kb, vb, c_row, nb, comm.gather(GATHER_LATE))
        wa, wb, wo, w2i, w2o = comm.gathered(GATHER_LATE, gathered)
    wa_p = jnp.concatenate(_pad_tiles(wa, 0, A_HEADS, A_SLOT, 0), axis=0)
    h2, mixed = _merge_fwd(h1, oa, ob, ga, gb, wa_p, wb, wo)
    (h3, g2, u2), _ = _ffn_fwd(h2, n2, w2i, w2o)
    loss, dh3, d_nfin = _loss_head(h3.reshape(bsz, l, d), nfin, target)

    (dh2, n2b, a2, dgu2, df2, dn2_parts), _ = _ffn_bwd(dh3.reshape(t, d), h2, n2, g2, u2, w2i, w2o)
    g_w2o = _tn_matmul(a2, df2, "grad_ffn2_w_out")
    g_w2i = _tn_matmul(n2b, dgu2, "grad_ffn2_w_in")

    hosted = comm.swap("ffn2", dict(ffn2_w_in=g_w2i, ffn2_w_out=g_w2o)) if comm else None
    (dya, dyb, doa, dob, dga, dgb, dh2b), swapped = _merge_bwd(dh2, oa, ob, ga, gb, wa_p, wb, wo, hosted)
    g_wo = _tn_matmul(mixed, dh2b, "grad_w_out")
    g_wa = jnp.concatenate(_unpad_tiles(_tn_matmul(oa, dya, "grad_w_branch_a"), 0, A_HEADS, A_SLOT, 0), axis=0)
    g_wb = _tn_matmul(ob, dyb, "grad_w_branch_b")

    dqa, dka, dva, dsink_rows = _swa_bwd(qa, ka, va, doa, lse_a, sink_rows, slope_rows, nb)
    hosted = comm.scatter("ffn2", swapped) if comm else None
    (dqb, dkb, dvb, dc_row), pieces = _fox_bwd(qb, kb, vb, ob_exact, dob, lse_b, c_row, nb, hosted)
    if comm:
        comm.received("ffn2", pieces)
    dc = dc_row.reshape(bsz, B_HEADS, l).transpose(0, 2, 1).reshape(t, B_HEADS)
    dc = jnp.concatenate([dc, jnp.zeros((t, F_COLS - B_HEADS), F32)], axis=1)
    df_logit, db_parts = _forget_cumsum_bwd(dc, f_logit, b_pad, nb)

    dproj = jnp.concatenate([dqa, dka, dva, dqb, dkb, dvb, dga, dgb, df_logit], axis=1)
    g_win = _unlayout_w_in(_tn_matmul(un, dproj, "grad_w_in"))
    hosted = comm.swap("mixer", dict(w_in=g_win, w_branch_a=g_wa, w_branch_b=g_wb, w_out=g_wo)) if comm else None
    (dh1, dnmix_parts), swapped = _proj_bwd(dh2, h1, nmix, dproj, wp, hosted)
    hosted = comm.scatter("mixer", swapped) if comm else None
    (dh0, n1b, a1, dgu1, df1, dn1_parts), pieces = _ffn_bwd(dh1, h0, n1, g1, u1, w1i, w1o, hosted)
    dh0 = dh0.reshape(bsz, l, d)
    grad_x = dh0[:, PREFIX:]
    small = dict(
        meta_tokens=jnp.sum(dh0[:, N_PAD:PREFIX], axis=0),
        ffn1_norm=jnp.sum(dn1_parts, axis=0),
        mix_norm=jnp.sum(dnmix_parts, axis=0),
        ffn2_norm=jnp.sum(dn2_parts, axis=0),
        final_norm=d_nfin,
        b_forget=jnp.sum(db_parts, axis=0)[:, :B_HEADS],
        attn_sinks=jnp.sum(dsink_rows.reshape(bsz, A_HEADS, BLOCK), axis=(0, 2)).reshape(1, A_HEADS),
    )
    if comm is None:
        g_w1o = _tn_matmul(a1, df1, "grad_ffn1_w_out")
        g_w1i = _tn_matmul(n1b, dgu1, "grad_ffn1_w_in")
    else:
        comm.received("mixer", pieces)
        g_w1o, gathered = _tn_matmul(a1, df1, "grad_ffn1_w_out", comm.small_gather(loss, small))
        comm.small_gathered(gathered)
        swapped = _run_exchange(comm.swap("ffn1_out", dict(ffn1_w_out=g_w1o)), "exchange_halves_ffn1_out")
        g_w1i, pieces = _tn_matmul(n1b, dgu1, "grad_ffn1_w_in", comm.scatter("ffn1_out", swapped))
        comm.received("ffn1_out", pieces)
    big = dict(ffn1_w_in=g_w1i, ffn1_w_out=g_w1o, w_in=g_win, w_branch_a=g_wa, w_branch_b=g_wb,
               w_out=g_wo, ffn2_w_in=g_w2i, ffn2_w_out=g_w2o)
    return loss, grad_x, small, big


BIG = (
    ("ffn1_w_in", (D_MODEL, 5632), 1),
    ("ffn1_w_out", (2816, D_MODEL), 0),
    ("w_in", (D_MODEL, W_IN_COLS), 1),
    ("w_branch_a", (A_WIDTH, D_MODEL), 1),
    ("w_branch_b", (B_WIDTH, D_MODEL), 1),
    ("w_out", (D_MODEL, D_MODEL), 0),
    ("ffn2_w_in", (D_MODEL, 5632), 1),
    ("ffn2_w_out", (2816, D_MODEL), 0),
)
STACKED = "w_in"


def _coords():
    return lax.axis_index("x"), lax.axis_index("y"), lax.axis_index("c")


def _other_chips(x, y):
    return ((1 - x, y), (x, 1 - y), (1 - x, 1 - y))


def _chip_part(ref, name, shape, axis, k):
    if name == STACKED:
        return ref.at[k]
    size = shape[axis] // N_CHIPS
    start = pl.multiple_of(k * size, size)
    return ref.at[pl.ds(start, size), :] if axis == 0 else ref.at[:, pl.ds(start, size)]


def _full_shape(name, shape):
    return (N_CHIPS, shape[0], shape[1] // N_CHIPS) if name == STACKED else shape


class _Exchange:
    def __init__(self, ins, out_shape, n_sems, ops):
        self.ins, self.out_shape, self.n_sems, self.ops = list(ins), list(out_shape), n_sems, ops

    @property
    def scratch(self):
        return [pltpu.SemaphoreType.DMA((self.n_sems,)), pltpu.SemaphoreType.DMA((self.n_sems,))]


SEMS_PER_GATHER = 9


def _gather_exchange(shards, table):
    n = len(table)
    x_nbr, y_nbr, diagonal = 0, 1, 2

    def ops(ins, outs, send_sems, recv_sems):
        x, y, c = _coords()
        mine = 2 * x + y
        sibling = (x, y, 1 - c)
        chips = _other_chips(x, y)
        slots = [2 * chip[0] + chip[1] for chip in chips]

        def part(i, k):
            name, shape, axis = table[i][:3]
            return _chip_part(outs[i], name, shape, axis, k)

        def half(ref, h):
            rows = ref.shape[0] // 2
            return ref.at[pl.ds(pl.multiple_of(h * rows, rows), rows), :]

        def remote(i, sem, src, dst, device):
            sem = SEMS_PER_GATHER * i + sem
            return pltpu.make_async_remote_copy(src, dst, send_sems.at[sem], recv_sems.at[sem],
                                                device_id=device, device_id_type=MESH_ID)

        def own(i):
            return remote(i, 0, ins[i], part(i, mine), sibling)

        def fetch(i, j, slot):
            if table[i][4]:
                src, dst = half(ins[i], c), half(part(i, slot), c)
            else:
                src, dst = ins[i], part(i, slot)
            return remote(i, 1 + j, src, dst, (chips[j][0], chips[j][1], c))

        def relayed(i, via, of):
            region = half(half(part(i, slots[of]), c), via)
            return remote(i, 4 + via, region, region, (chips[via][0], chips[via][1], c))

        def forward(i, j, h):
            region = half(part(i, slots[j]), h)
            return remote(i, 6 + j, region, region, sibling)

        def start():
            for i in range(n):
                for j in (x_nbr, y_nbr) if table[i][4] else (x_nbr, y_nbr, diagonal):
                    fetch(i, j, mine).start()
            for i in range(n):
                own(i).start()

        def relay():
            for i in range(n):
                if not table[i][4]:
                    for j in range(3):
                        fetch(i, j, slots[j]).wait_recv()
                    continue
                fetch(i, y_nbr, slots[y_nbr]).wait_recv()
                relayed(i, x_nbr, y_nbr).start()
                forward(i, y_nbr, c).start()
                fetch(i, x_nbr, slots[x_nbr]).wait_recv()
                relayed(i, y_nbr, x_nbr).start()
                forward(i, x_nbr, c).start()
            for i in range(n):
                if table[i][4]:
                    relayed(i, x_nbr, diagonal).wait_recv()
                    relayed(i, y_nbr, diagonal).wait_recv()
                    forward(i, diagonal, c).start()

        def finish():
            for i in range(n):
                own(i).wait()
                for j in range(3):
                    if table[i][4]:
                        forward(i, j, 1 - c).wait_recv()
                        forward(i, j, c).wait_send()
                    if j != diagonal or not table[i][4]:
                        fetch(i, j, mine).wait_send()
                if table[i][4]:
                    relayed(i, x_nbr, y_nbr).wait_send()
                    relayed(i, y_nbr, x_nbr).wait_send()

        return start, relay, finish

    out_shape = [jax.ShapeDtypeStruct(_full_shape(name, shape), dtype) for name, shape, _, dtype, _ in table]
    return _Exchange(shards, out_shape, SEMS_PER_GATHER * n, ops)


def _run_exchange(exchange, name):
    n = len(exchange.ins)

    def body(*refs):
        start, relay, finish = exchange.ops(refs[:n], refs[n:2 * n], *refs[2 * n:])
        start()
        relay()
        finish()

    return pl.pallas_call(
        body,
        name=name,
        in_specs=[HBM_SPEC] * n,
        out_specs=[HBM_SPEC] * n,
        out_shape=exchange.out_shape,
        scratch_shapes=exchange.scratch,
    )(*exchange.ins)


def _host_exchange(exchange, in_refs, out_refs, sem_refs, first, middle, last):
    start, relay, finish = exchange.ops(in_refs, out_refs, *sem_refs)
    pl.when(first)(start)
    pl.when(middle)(relay)
    pl.when(last)(finish)


def _halves_view(name, shape, axis):
    r, c = shape
    if name == STACKED:
        return (N_CHIPS, 2, r // 2, c // N_CHIPS), lambda ref, h: ref.at[:, h]
    if axis == 1:
        return (2, r // 2, c), lambda ref, h: ref.at[h]
    return (N_CHIPS, 2, r // N_CHIPS // 2, c), lambda ref, h: ref.at[:, h]


def _halves_exchange(grads, entries):
    n_w = len(entries)
    views = [_halves_view(*entry) for entry in entries]

    def ops(ins, outs, send_sems, recv_sems):
        x, y, c = _coords()
        copies = [pltpu.make_async_remote_copy(views[i][1](ins[i], 1 - c), outs[i], send_sems.at[i], recv_sems.at[i],
                                               device_id=(x, y, 1 - c), device_id_type=MESH_ID) for i in range(n_w)]

        def start():
            for cp in copies:
                cp.start()

        def finish():
            for cp in copies:
                cp.wait()

        return start, lambda: None, finish

    half_shape = lambda v: tuple(d for i, d in enumerate(v) if i != (1 if len(v) == 4 else 0))
    out_shape = [jax.ShapeDtypeStruct(half_shape(v[0]), F32) for v in views]
    return _Exchange([g.reshape(v[0]) for g, v in zip(grads, views)], out_shape, n_w, ops)


def _add_sibling(g_view, recv, c, name):
    shape = recv.shape
    if len(shape) == 2:
        tr = _tile(shape[0], 128, 16)
        grid = (shape[0] // tr,)
        g_spec = pl.BlockSpec((None, tr, shape[1]), lambda i, c_ref: (c_ref[0], i, 0))
        r_spec = pl.BlockSpec((tr, shape[1]), lambda i, c_ref: (i, 0))
    else:
        tr = _tile(shape[1], 256, 16)
        grid = (N_CHIPS, shape[1] // tr)
        g_spec = pl.BlockSpec((None, None, tr, shape[2]), lambda k, i, c_ref: (k, c_ref[0], i, 0))
        r_spec = pl.BlockSpec((None, tr, shape[2]), lambda k, i, c_ref: (k, i, 0))

    def body(c_ref, g_ref, r_ref, o_ref):
        o_ref[...] = (g_ref[...] + r_ref[...]).astype(BF16)

    return pl.pallas_call(
        body,
        name="add_sibling_" + name,
        grid_spec=pltpu.PrefetchScalarGridSpec(num_scalar_prefetch=1, grid=grid, in_specs=[g_spec, r_spec],
                                               out_specs=r_spec),
        out_shape=jax.ShapeDtypeStruct(shape, BF16),
        compiler_params=_cparams(("parallel",) * len(grid)),
    )(c, g_view, recv)


def _piece_of(ref, name, axis, k):
    if name == STACKED or axis == 0:
        return ref.at[k]
    size = ref.shape[1] // N_CHIPS
    return ref.at[:, pl.ds(pl.multiple_of(k * size, size), size)]


def _piece_shape(name, shape, axis):
    r, c = shape
    return (r // 2, c // N_CHIPS) if (axis == 1) else (r // N_CHIPS // 2, c)


def _scatter_exchange(partials, entries):
    n_w = len(entries)

    def ops(ins, outs, send_sems, recv_sems):
        x, y, c = _coords()
        chips = _other_chips(x, y)
        copies = []
        for i, (name, _, axis) in enumerate(entries):
            for j, chip in enumerate(chips):
                sem = 3 * i + j
                copies.append(pltpu.make_async_remote_copy(
                    _piece_of(ins[i], name, axis, 2 * chip[0] + chip[1]), outs[i].at[j], send_sems.at[sem],
                    recv_sems.at[sem], device_id=(chip[0], chip[1], c), device_id_type=MESH_ID))

        def start():
            for cp in copies:
                cp.start()

        def finish():
            for cp in copies:
                cp.wait()

        return start, lambda: None, finish

    out_shape = [jax.ShapeDtypeStruct((3,) + _piece_shape(*entry), BF16) for entry in entries]
    return _Exchange(partials, out_shape, 3 * n_w, ops)


def _add_chips(partial, recv, mine, name, axis):
    rows, cols = recv.shape[1:]
    tr = _tile(rows, 256, 16)
    if name == STACKED or axis == 0:
        p_spec = pl.BlockSpec((None, tr, cols), lambda i, k_ref: (k_ref[0], i, 0))
    else:
        p_spec = pl.BlockSpec((tr, cols), lambda i, k_ref: (i, k_ref[0]))

    def body(k_ref, p_ref, r_ref, o_ref):
        f32 = lambda a: a.astype(F32)
        o_ref[...] = ((f32(p_ref[...]) + f32(r_ref[0])) + f32(r_ref[1])) + f32(r_ref[2])

    return pl.pallas_call(
        body,
        name="add_chips_" + name,
        grid_spec=pltpu.PrefetchScalarGridSpec(
            num_scalar_prefetch=1, grid=(rows // tr,),
            in_specs=[p_spec, pl.BlockSpec((3, tr, cols), lambda i, k_ref: (0, i, 0))],
            out_specs=pl.BlockSpec((tr, cols), lambda i, k_ref: (i, 0))),
        out_shape=jax.ShapeDtypeStruct((rows, cols), F32),
        compiler_params=_cparams(("parallel",)),
    )(mine, partial, recv)


def _share_with_sibling(halves):
    n_w = len(halves)

    def body(*refs):
        ins, outs = refs[:n_w], refs[n_w:2 * n_w]
        send_sems, recv_sems = refs[2 * n_w:]
        x, y, c = _coords()
        copies = [pltpu.make_async_remote_copy(ins[i], outs[i], send_sems.at[i], recv_sems.at[i],
                                               device_id=(x, y, 1 - c), device_id_type=MESH_ID) for i in range(n_w)]
        for cp in copies:
            cp.start()
        for cp in copies:
            cp.wait()

    return pl.pallas_call(
        body,
        name="share_with_sibling",
        in_specs=[HBM_SPEC] * n_w,
        out_specs=[HBM_SPEC] * n_w,
        out_shape=[jax.ShapeDtypeStruct(h.shape, F32) for h in halves],
        scratch_shapes=[pltpu.SemaphoreType.DMA((n_w,)), pltpu.SemaphoreType.DMA((n_w,))],
    )(*halves)


SMALL_ROWS = 168


def _small_exchange(buf):
    def ops(ins, outs, send_sems, recv_sems):
        x, y, c = _coords()
        me = 4 * x + 2 * y + c
        peers = [(x ^ fx, y ^ fy, c ^ fc) for fx in (0, 1) for fy in (0, 1) for fc in (0, 1)][1:]

        def copy(j, slot, dev):
            return pltpu.make_async_remote_copy(ins[0], outs[0].at[slot], send_sems.at[j], recv_sems.at[j],
                                                device_id=dev, device_id_type=MESH_ID)

        own = pltpu.make_async_copy(ins[0], outs[0].at[me], send_sems.at[N_DEV - 1])

        def start():
            own.start()
            for j, dev in enumerate(peers):
                copy(j, me, dev).start()

        def finish():
            for j, dev in enumerate(peers):
                copy(j, 4 * dev[0] + 2 * dev[1] + dev[2], dev).wait()
            own.wait()

        return start, lambda: None, finish

    return _Exchange([buf], [jax.ShapeDtypeStruct((N_DEV,) + buf.shape, F32)], N_DEV, ops)


def _sum_devices(gathered):
    def body(g_ref, out_ref):
        acc = g_ref[0]
        for d in range(1, N_DEV):
            acc = acc + g_ref[d]
        out_ref[...] = acc

    return pl.pallas_call(
        body,
        name="sum_devices",
        in_specs=[VMEM_SPEC],
        out_specs=VMEM_SPEC,
        out_shape=jax.ShapeDtypeStruct(gathered.shape[1:], F32),
    )(gathered)


def _adamw(w, g, m, v):
    r, rest = w.shape[0], w.shape[1:]
    per_row = 1
    for dim in rest:
        per_row *= dim
    tr = _tile(r, max(8, (5 << 19) // (4 * per_row)), 8 if len(rest) == 1 else 1)

    def body(w_ref, g_ref, m_ref, v_ref, d_ref, mo_ref, vo_ref):
        gg = g_ref[...]
        mm = ADAM_B1 * m_ref[...] + (1.0 - ADAM_B1) * gg
        vv = ADAM_B2 * v_ref[...] + (1.0 - ADAM_B2) * (gg * gg)
        m_hat = mm / (1.0 - ADAM_B1 ** ADAM_STEP)
        v_hat = vv / (1.0 - ADAM_B2 ** ADAM_STEP)
        d_ref[...] = -ADAM_LR * (m_hat / (jnp.sqrt(v_hat) + ADAM_EPS) + ADAM_WD * w_ref[...])
        mo_ref[...] = mm
        vo_ref[...] = vv

    spec = pl.BlockSpec((tr,) + rest, lambda i: (i,) + (0,) * len(rest))
    return pl.pallas_call(
        body,
        name="adamw",
        grid=(r // tr,),
        in_specs=[spec] * 4,
        out_specs=[spec] * 3,
        out_shape=[jax.ShapeDtypeStruct(w.shape, F32)] * 3,
        compiler_params=_cparams(("parallel",)),
    )(w, g, m, v)


def _adamw_halves(w, own, other, m, v, c, name):
    r, cols = w.shape
    half = r // 2
    tr = _tile(half, 256, 8)
    nt = half // tr
    whole = pl.BlockSpec((tr, cols), lambda h, i, c_ref: (h * nt + i, 0))
    part = pl.BlockSpec((tr, cols), lambda h, i, c_ref: (i, 0))

    def body(c_ref, w_ref, own_ref, other_ref, m_ref, v_ref, g_ref, d_ref, mo_ref, vo_ref):
        gg = jnp.where(pl.program_id(0) == c_ref[0], own_ref[...], other_ref[...])
        g_ref[...] = gg
        mm = ADAM_B1 * m_ref[...] + (1.0 - ADAM_B1) * gg
        vv = ADAM_B2 * v_ref[...] + (1.0 - ADAM_B2) * (gg * gg)
        m_hat = mm / (1.0 - ADAM_B1 ** ADAM_STEP)
        v_hat = vv / (1.0 - ADAM_B2 ** ADAM_STEP)
        d_ref[...] = -ADAM_LR * (m_hat / (jnp.sqrt(v_hat) + ADAM_EPS) + ADAM_WD * w_ref[...])
        mo_ref[...] = mm
        vo_ref[...] = vv

    return pl.pallas_call(
        body,
        name="adamw_" + name,
        grid_spec=pltpu.PrefetchScalarGridSpec(
            num_scalar_prefetch=1, grid=(2, nt),
            in_specs=[whole, part, part, whole, whole], out_specs=[whole] * 4),
        out_shape=[jax.ShapeDtypeStruct((r, cols), F32)] * 4,
        compiler_params=_cparams(("parallel", "parallel")),
    )(c, w, own, other, m, v)


GATHER_FIRST = ("ffn1_w_in", "ffn1_w_out")
GATHER_PROJ = ("w_in",)
GATHER_LATE = ("w_branch_a", "w_branch_b", "w_out", "ffn2_w_in", "ffn2_w_out")


class _Comm:
    def __init__(self, shards, c_arr, mine_arr):
        self.shards, self.c, self.mine = shards, c_arr, mine_arr
        self.groups, self.halves = {}, {}
        self.by_name = {entry[0]: entry for entry in BIG}

    def small_gather(self, loss, small):
        pad_lanes = lambda a: jnp.concatenate([a, jnp.zeros((1, LANES - a.shape[1]), F32)], axis=1)
        buf = jnp.concatenate([
            small["meta_tokens"].reshape(128, LANES),
            small["ffn1_norm"].reshape(8, LANES), small["mix_norm"].reshape(8, LANES),
            small["ffn2_norm"].reshape(8, LANES), small["final_norm"].reshape(8, LANES),
            loss, pad_lanes(small["b_forget"]), pad_lanes(small["attn_sinks"]),
            jnp.zeros((SMALL_ROWS - 163, LANES), F32)], axis=0)
        return _small_exchange(buf)

    def small_gathered(self, outs):
        self.reduced = _sum_devices(outs[0])

    def gather(self, names):
        table = [self.by_name[n] + (BF16, True) for n in names]
        return _gather_exchange([self.shards[n] for n in names], table)

    def gathered(self, names, outs):
        return [o.transpose(1, 0, 2).reshape(D_MODEL, W_IN_COLS) if n == STACKED else o for n, o in zip(names, outs)]

    def swap(self, tag, grads):
        entries = [self.by_name[n] for n in grads]
        arrays = [g.reshape(D_MODEL, N_CHIPS, W_IN_COLS // N_CHIPS).transpose(1, 0, 2) if n == STACKED else g
                  for n, g in grads.items()]
        self.groups[tag] = (entries, arrays)
        return _halves_exchange(arrays, entries)

    def scatter(self, tag, received):
        entries, arrays = self.groups[tag]
        views = [_halves_view(*entry) for entry in entries]
        partials = [_add_sibling(g.reshape(v[0]), r, self.c, name)
                    for g, v, r, (name, _, _) in zip(arrays, views, received, entries)]
        self.groups[tag] = (entries, partials)
        return _scatter_exchange(partials, entries)

    def received(self, tag, pieces):
        entries, partials = self.groups[tag]
        for p, r, (name, _, axis) in zip(partials, pieces, entries):
            self.halves[name] = _add_chips(p, r, self.mine, name, axis)

    def finish(self):
        names = [n for n, _, _ in BIG]
        own = [self.halves[n] for n in names]
        return dict(zip(names, zip(own, _share_with_sibling(own))))


def kernel(x, meta_tokens, ffn1_norm, ffn1_w_in, ffn1_w_out, mix_norm, w_in, b_forget, attn_sinks, w_branch_a, w_branch_b, w_out, ffn2_norm, ffn2_w_in, ffn2_w_out, final_norm, loss_target, m_meta_tokens, m_ffn1_norm, m_ffn1_w_in, m_ffn1_w_out, m_mix_norm, m_w_in, m_b_forget, m_attn_sinks, m_w_branch_a, m_w_branch_b, m_w_out, m_ffn2_norm, m_ffn2_w_in, m_ffn2_w_out, m_final_norm, v_meta_tokens, v_ffn1_norm, v_ffn1_w_in, v_ffn1_w_out, v_mix_norm, v_w_in, v_b_forget, v_attn_sinks, v_w_branch_a, v_w_branch_b, v_w_out, v_ffn2_norm, v_ffn2_w_in, v_ffn2_w_out, v_final_norm):
    given = dict(locals())
    names = ["meta_tokens", "ffn1_norm", "ffn1_w_in", "ffn1_w_out", "mix_norm", "w_in", "b_forget", "attn_sinks",
             "w_branch_a", "w_branch_b", "w_out", "ffn2_norm", "ffn2_w_in", "ffn2_w_out", "final_norm"]
    big_names = [n for n, _, _ in BIG]
    cx, cy, cc = _coords()
    c_arr = cc.reshape(1).astype(jnp.int32)
    mine_arr = (2 * cx + cy).reshape(1).astype(jnp.int32)

    comm = _Comm({n: given[n][0].astype(BF16) for n in big_names}, c_arr, mine_arr)
    table = [comm.by_name[n] + (BF16, True) for n in GATHER_FIRST] + [("meta_tokens", (N_META, D_MODEL), 1, F32, False)]
    first = _gather_exchange([comm.shards[n] for n in GATHER_FIRST] + [meta_tokens], table)
    w1i, w1o, meta_full = _run_exchange(first, "gather_first")
    norms = (ffn1_norm, mix_norm, ffn2_norm, final_norm.reshape(1, D_MODEL))
    loss, grad_x, small, big = _local_step(x, loss_target, meta_full, norms, b_forget, attn_sinks, (w1i, w1o), comm)

    swap = comm.swap("ffn1_in", dict(ffn1_w_in=big["ffn1_w_in"]))
    last = comm.scatter("ffn1_in", _run_exchange(swap, "exchange_halves_ffn1_in"))
    comm.received("ffn1_in", _run_exchange(last, "scatter_chip_sums"))
    grad_halves = comm.finish()
    grads = {}

    red = comm.reduced
    meta_cols = red[:128].reshape(N_META, D_MODEL)
    grads["meta_tokens"] = lax.dynamic_slice_in_dim(meta_cols, (2 * cx + cy) * (D_MODEL // N_CHIPS),
                                                    D_MODEL // N_CHIPS, axis=1)
    grads["ffn1_norm"] = red[128:136].reshape(1, D_MODEL)
    grads["mix_norm"] = red[136:144].reshape(1, D_MODEL)
    grads["ffn2_norm"] = red[144:152].reshape(1, D_MODEL)
    grads["final_norm"] = red[152:160].reshape(1, D_MODEL)
    loss_out = red[160, 0]
    grads["b_forget"] = red[161:162, :B_HEADS]
    grads["attn_sinks"] = red[162:163, :A_HEADS]

    out_g, out_d, out_m, out_v = [], [], [], []
    for n in names:
        w_full = given[n]
        shape = w_full.shape
        two_d = (lambda a: a.reshape(shape[-2], shape[-1])) if len(shape) >= 2 else (lambda a: a.reshape(1, shape[0]))
        if n == STACKED:
            own, other = grad_halves[n]
            g_t = jnp.concatenate([jnp.where(cc == 0, own, other), jnp.where(cc == 0, other, own)], axis=0).T
            tiles = lambda a: a.reshape(shape[-1], shape[-2] // LANES, LANES)
            untile = lambda a: a.reshape(shape[-1], shape[-2]).T
            d2, m2, v2 = [untile(a) for a in _adamw(tiles(two_d(w_full).T), tiles(g_t), tiles(two_d(given["m_" + n]).T),
                                                     tiles(two_d(given["v_" + n]).T))]
            g2 = g_t.T
        elif n in grad_halves:
            own, other = grad_halves[n]
            g2, d2, m2, v2 = _adamw_halves(two_d(w_full), own, other, two_d(given["m_" + n]),
                                           two_d(given["v_" + n]), c_arr, n)
        else:
            g2 = two_d(grads[n])
            d2, m2, v2 = _adamw(two_d(w_full), g2, two_d(given["m_" + n]), two_d(given["v_" + n]))
        out_g.append(g2.reshape(shape))
        out_d.append(d2.reshape(shape))
        out_m.append(m2.reshape(shape))
        out_v.append(v2.reshape(shape))
    return (loss_out, grad_x, *out_g, *out_d, *out_m, *out_v)
```

```python
import jax
import jax.numpy as jnp
from jax import lax
from jax.experimental import pallas as pl
from jax.experimental.pallas import tpu as pltpu

F32 = jnp.float32
BF16 = jnp.bfloat16

D_MODEL = 1024
N_META = 16
BLOCK = 128
LANES = 128
PREFIX = BLOCK
N_PAD = PREFIX - N_META
HEAD_DIM = 64
A_HEADS = 8
A_KV_HEADS = 2
A_GROUP = 4
B_HEADS = 8
B_PAIRS = B_HEADS // 2
A_WIDTH = A_HEADS * HEAD_DIM
A_KV_WIDTH = A_KV_HEADS * HEAD_DIM
B_WIDTH = B_HEADS * HEAD_DIM
W_IN_COLS = A_WIDTH + 2 * A_KV_WIDTH + 3 * B_WIDTH + B_HEADS + 2 * D_MODEL
SRC_KA = A_WIDTH
SRC_VA = SRC_KA + A_KV_WIDTH
SRC_QB = SRC_VA + A_KV_WIDTH
SRC_KB = SRC_QB + B_WIDTH
SRC_VB = SRC_KB + B_WIDTH
SRC_F = SRC_VB + B_WIDTH
SRC_GA = SRC_F + B_HEADS
SRC_GB = SRC_GA + D_MODEL
A_PAD_WIDTH = A_HEADS * LANES
B_PAD_WIDTH = B_HEADS * LANES
F_COLS = LANES
OFF_QA = 0
OFF_KA = SRC_KA
OFF_VA = SRC_VA
OFF_QB = SRC_QB
OFF_KB = SRC_KB
OFF_VB = SRC_VB
OFF_GA = SRC_F
OFF_GB = OFF_GA + D_MODEL
OFF_F = OFF_GB + D_MODEL
P_COLS = OFF_F + F_COLS
EPS = 1e-6
NEG = -1e30
SCALE = HEAD_DIM ** -0.5
KEY_BLOCKS = 4

ADAM_LR = 0.001
ADAM_B1 = 0.9
ADAM_B2 = 0.999
ADAM_EPS = 1e-08
ADAM_WD = 0.01
ADAM_STEP = 10

N_CHIPS = 4
N_DEV = 8
VMEM_LIMIT = 56 * 1024 * 1024

NT_DIMS = (((1,), (1,)), ((), ()))
TN_DIMS = (((0,), (0,)), ((), ()))
MESH_ID = pl.DeviceIdType.MESH
HBM_SPEC = pl.BlockSpec(memory_space=pltpu.HBM)
VMEM_SPEC = pl.BlockSpec(memory_space=pltpu.VMEM)


def _tile(n, target, mult=16):
    best = None
    for t in range(mult, min(n, target) + 1, mult):
        if n % t == 0:
            best = t
    return best if best is not None else n


def _cparams(sem):
    return pltpu.CompilerParams(dimension_semantics=sem, vmem_limit_bytes=VMEM_LIMIT)


def _rms_scale(h):
    return lax.rsqrt(jnp.mean(h * h, axis=-1, keepdims=True) + EPS)


def _rms_bwd(dn, h, w):
    r = _rms_scale(h)
    dw = jnp.sum(dn * (h * r), axis=0, keepdims=True)
    z = dn * w
    dh = r * z - h * ((r * r * r) * jnp.mean(z * h, axis=-1, keepdims=True))
    return dh, dw


def _ffn_fwd(h, norm_w, w_in, w_out, exchange=None):
    t, d = h.shape
    f = w_out.shape[0]
    tm = _tile(t, 272)
    tc = _tile(f, 256, 128)
    nj = f // tc
    ni = t // tm
    n_x = len(exchange.ins) if exchange else 0

    def body(*refs):
        h_ref, nw_ref, wi_ref, wo_ref = refs[:4]
        hout_ref, g_ref, u_ref = refs[4 + n_x:7 + n_x]
        a_scr = refs[7 + 2 * n_x]
        i = pl.program_id(0)
        if exchange:
            _host_exchange(exchange, refs[4:4 + n_x], refs[7 + n_x:7 + 2 * n_x], refs[8 + 2 * n_x:],
                           i == 0, i == ni // 3, i == ni - 1, late=i == 2 * ni // 3)
        hh = h_ref[...]
        n = ((hh * _rms_scale(hh)) * nw_ref[...]).astype(BF16)
        for j in range(nj):
            cols = slice(j * tc, (j + 1) * tc)
            g = jnp.dot(n, wi_ref[:, j * tc:(j + 1) * tc], preferred_element_type=F32)
            u = jnp.dot(n, wi_ref[:, f + j * tc:f + (j + 1) * tc], preferred_element_type=F32)
            g_ref[:, cols] = g
            u_ref[:, cols] = u
            a_scr[:, cols] = ((g * jax.nn.sigmoid(g)) * u).astype(BF16)
        hout_ref[...] = hh + 0.5 * jnp.dot(a_scr[...], wo_ref[...], preferred_element_type=F32)

    resident = lambda a: pl.BlockSpec(a.shape, lambda i: (0, 0), pipeline_mode=pl.Buffered(1))
    row = lambda w: pl.BlockSpec((tm, w), lambda i: (i, 0))
    outs = pl.pallas_call(
        body,
        name="ffn_fwd",
        grid=(ni,),
        in_specs=[row(d), pl.BlockSpec((1, d), lambda i: (0, 0)), resident(w_in), resident(w_out)] + [HBM_SPEC] * n_x,
        out_specs=[row(d), row(f), row(f)] + [HBM_SPEC] * n_x,
        out_shape=[
            jax.ShapeDtypeStruct((t, d), F32),
            jax.ShapeDtypeStruct((t, f), F32),
            jax.ShapeDtypeStruct((t, f), F32),
        ] + (exchange.out_shape if exchange else []),
        scratch_shapes=[pltpu.VMEM((tm, f), BF16)] + (exchange.scratch if exchange else []),
        compiler_params=_cparams(("arbitrary",) if exchange else ("parallel",)),
    )(h, norm_w, w_in, w_out, *(exchange.ins if exchange else []))
    return outs[:3], outs[3:]


def _ffn_bwd(dh_out, h, norm_w, g, u, w_in, w_out, exchange=None):
    t, d = h.shape
    f = w_out.shape[0]
    tm = _tile(t, 272)
    tc = _tile(f, 256, 128)
    nj = f // tc
    ni = t // tm
    n_x = len(exchange.ins) if exchange else 0

    def body(*refs):
        dho_ref, h_ref, nw_ref, g_ref, u_ref, wi_ref, wo_ref = refs[:7]
        dhin_ref, n_ref, a_ref, dgu_ref, df_ref, dnw_ref = refs[7 + n_x:13 + n_x]
        i = pl.program_id(0)
        if exchange:
            _host_exchange(exchange, refs[7:7 + n_x], refs[13 + n_x:13 + 2 * n_x], refs[13 + 2 * n_x:],
                           i == 0, i == ni - 1, i == ni - 1)
        hh = h_ref[...]
        nw = nw_ref[...]
        n_ref[...] = ((hh * _rms_scale(hh)) * nw).astype(BF16)
        dho = dho_ref[...]
        df = (0.5 * dho).astype(BF16)
        df_ref[...] = df
        for j in range(nj):
            cols = slice(j * tc, (j + 1) * tc)
            da = lax.dot_general(df, wo_ref[cols, :], NT_DIMS, preferred_element_type=F32)
            gg = g_ref[:, cols]
            uu = u_ref[:, cols]
            sig = jax.nn.sigmoid(gg)
            sl = gg * sig
            a_ref[:, cols] = (sl * uu).astype(BF16)
            dgu_ref[0, :, cols] = ((da * uu) * (sig * (1.0 + gg * (1.0 - sig)))).astype(BF16)
            dgu_ref[1, :, cols] = (da * sl).astype(BF16)
        dn = (lax.dot_general(dgu_ref[0], wi_ref[:, :f], NT_DIMS, preferred_element_type=F32)
              + lax.dot_general(dgu_ref[1], wi_ref[:, f:], NT_DIMS, preferred_element_type=F32))
        dh, dw = _rms_bwd(dn, hh, nw)
        dhin_ref[...] = dho + dh
        dnw_ref[0] = dw

    resident = lambda a: pl.BlockSpec(a.shape, lambda i: (0, 0), pipeline_mode=pl.Buffered(1))
    row = lambda w: pl.BlockSpec((tm, w), lambda i: (i, 0))
    outs = pl.pallas_call(
        body,
        name="ffn_bwd",
        grid=(ni,),
        in_specs=[row(d), row(d), pl.BlockSpec((1, d), lambda i: (0, 0)), row(f), row(f),
                  resident(w_in), resident(w_out)] + [HBM_SPEC] * n_x,
        out_specs=[row(d), row(d), row(f), pl.BlockSpec((2, tm, f), lambda i: (0, i, 0)), row(d),
                   pl.BlockSpec((1, 1, d), lambda i: (i, 0, 0))] + [HBM_SPEC] * n_x,
        out_shape=[
            jax.ShapeDtypeStruct((t, d), F32),
            jax.ShapeDtypeStruct((t, d), BF16),
            jax.ShapeDtypeStruct((t, f), BF16),
            jax.ShapeDtypeStruct((2, t, f), BF16),
            jax.ShapeDtypeStruct((t, d), BF16),
            jax.ShapeDtypeStruct((ni, 1, d), F32),
        ] + (exchange.out_shape if exchange else []),
        scratch_shapes=exchange.scratch if exchange else [],
        compiler_params=_cparams(("arbitrary",) if exchange else ("parallel",)),
    )(dh_out, h, norm_w, g, u, w_in, w_out, *(exchange.ins if exchange else []))
    return outs[:6], outs[6:]


def _tn_matmul(a, b, name, exchange=None):
    t, k = a.shape
    split = b.ndim == 3
    n = 2 * b.shape[2] if split else b.shape[1]
    tk = _tile(k, 512, 128)
    tn = _tile(b.shape[-1], 1408, 128)
    per_half = b.shape[-1] // tn
    ni, nj = k // tk, n // tn
    n_x = len(exchange.ins) if exchange else 0

    def body(*refs):
        a_ref, b_ref, o_ref = refs[0], refs[1], refs[2 + n_x]
        if exchange:
            i, j = pl.program_id(0), pl.program_id(1)
            at_end = (i == ni - 1) & (j == nj - 1)
            _host_exchange(exchange, refs[2:2 + n_x], refs[3 + n_x:3 + 2 * n_x], refs[3 + 2 * n_x:],
                           (i == 0) & (j == 0), at_end, at_end)
        o_ref[...] = lax.dot_general(a_ref[...], b_ref[...], TN_DIMS, preferred_element_type=F32)

    if split:
        b_spec = pl.BlockSpec((None, t, tn), lambda i, j: (j // per_half, 0, j % per_half))
    else:
        b_spec = pl.BlockSpec((t, tn), lambda i, j: (0, j))
    outs = pl.pallas_call(
        body,
        name=name,
        grid=(ni, nj),
        in_specs=[pl.BlockSpec((t, tk), lambda i, j: (0, i)), b_spec] + [HBM_SPEC] * n_x,
        out_specs=[pl.BlockSpec((tk, tn), lambda i, j: (i, j))] + [HBM_SPEC] * n_x,
        out_shape=[jax.ShapeDtypeStruct((k, n), F32)] + (exchange.out_shape if exchange else []),
        scratch_shapes=exchange.scratch if exchange else [],
        compiler_params=_cparams(("arbitrary", "arbitrary") if exchange else ("parallel", "parallel")),
    )(a, b, *(exchange.ins if exchange else []))
    return (outs[0], outs[1:]) if exchange else outs[0]


A_SLOT = lambda h: h // A_GROUP
B_SLOT = lambda h: h % 2

PROJ_PARTS = (
    (OFF_QA, A_WIDTH, A_PAD_WIDTH, True, A_SLOT), (OFF_KA, A_KV_WIDTH, A_KV_WIDTH, True, None),
    (OFF_VA, A_KV_WIDTH, A_KV_WIDTH, True, None), (OFF_QB, B_WIDTH, B_WIDTH, True, None),
    (OFF_KB, B_WIDTH, B_PAD_WIDTH, True, B_SLOT), (OFF_VB, B_WIDTH, B_PAD_WIDTH, True, B_SLOT),
    (OFF_GA, D_MODEL, D_MODEL, False, None), (OFF_GB, D_MODEL, D_MODEL, False, None), (OFF_F, F_COLS, F_COLS, False, None),
)


def _head_tile(pair, head, slot):
    lane_slot = lax.broadcasted_iota(jnp.int32, pair.shape, 1) // HEAD_DIM
    moved = pair if head % 2 == slot else pltpu.roll(pair, HEAD_DIM, 1)
    return jnp.where(lane_slot == slot, moved, 0.0)


def _proj_fwd(h, norm_w, w_p):
    t, d = h.shape
    tm = _tile(t, 272)

    def body(h_ref, nw_ref, w_ref, u_ref, *part_refs):
        hh = h_ref[...]
        un = ((hh * _rms_scale(hh)) * nw_ref[...]).astype(BF16)
        u_ref[...] = un
        for (off, width, _, _, slot), p_ref in zip(PROJ_PARTS, part_refs):
            if slot is None:
                p_ref[...] = jnp.dot(un, w_ref[:, off:off + width], preferred_element_type=F32).astype(p_ref.dtype)
                continue
            for pair in range(width // LANES):
                x = jnp.dot(un, w_ref[:, off + pair * LANES:off + (pair + 1) * LANES], preferred_element_type=F32)
                for head in (2 * pair, 2 * pair + 1):
                    p_ref[:, head * LANES:(head + 1) * LANES] = _head_tile(x, head, slot(head)).astype(p_ref.dtype)

    row = lambda w: pl.BlockSpec((tm, w), lambda i: (i, 0))
    return pl.pallas_call(
        body,
        name="proj_fwd",
        grid=(t // tm,),
        in_specs=[row(d), pl.BlockSpec((1, d), lambda i: (0, 0)),
                  pl.BlockSpec(w_p.shape, lambda i: (0, 0), pipeline_mode=pl.Buffered(1))],
        out_specs=[row(d)] + [row(width) for _, _, width, _, _ in PROJ_PARTS],
        out_shape=[jax.ShapeDtypeStruct((t, d), BF16)]
        + [jax.ShapeDtypeStruct((t, width), BF16 if is_bf else F32) for _, _, width, is_bf, _ in PROJ_PARTS],
        compiler_params=_cparams(("parallel",)),
    )(h, norm_w, w_p)


def _proj_bwd(dh_out, h, norm_w, dproj, w_p, exchange=None):
    t, d = h.shape
    n = w_p.shape[1]
    tm = _tile(t, 272)
    ni = t // tm
    n_x = len(exchange.ins) if exchange else 0

    def body(*refs):
        dho_ref, h_ref, nw_ref, dp_ref, w_ref = refs[:5]
        dhin_ref, dnw_ref = refs[5 + n_x:7 + n_x]
        if exchange:
            i = pl.program_id(0)
            _host_exchange(exchange, refs[5:5 + n_x], refs[7 + n_x:7 + 2 * n_x], refs[7 + 2 * n_x:],
                           i == 0, i == ni - 1, i == ni - 1)
        dn = lax.dot_general(dp_ref[...], w_ref[...], NT_DIMS, preferred_element_type=F32)
        dh, dw = _rms_bwd(dn, h_ref[...], nw_ref[...])
        dhin_ref[...] = dho_ref[...] + dh
        dnw_ref[0] = dw

    row = lambda w: pl.BlockSpec((tm, w), lambda i: (i, 0))
    outs = pl.pallas_call(
        body,
        name="proj_bwd",
        grid=(ni,),
        in_specs=[row(d), row(d), pl.BlockSpec((1, d), lambda i: (0, 0)), row(n),
                  pl.BlockSpec(w_p.shape, lambda i: (0, 0), pipeline_mode=pl.Buffered(1))] + [HBM_SPEC] * n_x,
        out_specs=[row(d), pl.BlockSpec((1, 1, d), lambda i: (i, 0, 0))] + [HBM_SPEC] * n_x,
        out_shape=[jax.ShapeDtypeStruct((t, d), F32), jax.ShapeDtypeStruct((ni, 1, d), F32)]
        + (exchange.out_shape if exchange else []),
        scratch_shapes=exchange.scratch if exchange else [],
        compiler_params=_cparams(("arbitrary",) if exchange else ("parallel",)),
    )(dh_out, h, norm_w, dproj, w_p, *(exchange.ins if exchange else []))
    return outs[:2], outs[2:]


def _merge_fwd(h, oa, ob, ga, gb, wa, wb, wo):
    t, d = h.shape
    tm = _tile(t, 544)

    def body(h_ref, oa_ref, ob_ref, ga_ref, gb_ref, wa_ref, wb_ref, wo_ref, hout_ref, mix_ref):
        ya = jnp.dot(oa_ref[...], wa_ref[...], preferred_element_type=F32)
        yb = jnp.dot(ob_ref[...], wb_ref[...], preferred_element_type=F32)
        mixed = (jax.nn.sigmoid(ga_ref[...]) * ya + jax.nn.sigmoid(gb_ref[...]) * yb).astype(BF16)
        mix_ref[...] = mixed
        hout_ref[...] = h_ref[...] + jnp.dot(mixed, wo_ref[...], preferred_element_type=F32)

    row = lambda w: pl.BlockSpec((tm, w), lambda i: (i, 0))
    full = lambda a: pl.BlockSpec(a.shape, lambda i: (0, 0))
    return pl.pallas_call(
        body,
        name="merge_fwd",
        grid=(t // tm,),
        in_specs=[row(d), row(oa.shape[1]), row(ob.shape[1]), row(d), row(d), full(wa), full(wb), full(wo)],
        out_specs=[row(d), row(d)],
        out_shape=[jax.ShapeDtypeStruct((t, d), F32), jax.ShapeDtypeStruct((t, d), BF16)],
        compiler_params=_cparams(("parallel",)),
    )(h, oa, ob, ga, gb, wa, wb, wo)


def _merge_bwd(dh, oa, ob, ga, gb, wa, wb, wo, exchange=None):
    t, d = dh.shape
    tm = _tile(t, 544)
    ni = t // tm
    n_x = len(exchange.ins) if exchange else 0

    def body(*refs):
        dh_ref, oa_ref, ob_ref, ga_ref, gb_ref, wa_ref, wb_ref, wo_ref = refs[:8]
        dya_ref, dyb_ref, doa_ref, dob_ref, dga_ref, dgb_ref, dhb_ref = refs[8 + n_x:15 + n_x]
        if exchange:
            i = pl.program_id(0)
            _host_exchange(exchange, refs[8:8 + n_x], refs[15 + n_x:15 + 2 * n_x], refs[15 + 2 * n_x:],
                           i == 0, i == ni - 1, i == ni - 1)
        dhb = dh_ref[...].astype(BF16)
        dhb_ref[...] = dhb
        dmix = lax.dot_general(dhb, wo_ref[...], NT_DIMS, preferred_element_type=F32)
        for o_ref, g_ref, w_ref, dy_ref, do_ref, dg_ref in (
                (oa_ref, ga_ref, wa_ref, dya_ref, doa_ref, dga_ref),
                (ob_ref, gb_ref, wb_ref, dyb_ref, dob_ref, dgb_ref)):
            y = jnp.dot(o_ref[...], w_ref[...], preferred_element_type=F32)
            s = jax.nn.sigmoid(g_ref[...])
            dy = (dmix * s).astype(BF16)
            dy_ref[...] = dy
            dg_ref[...] = ((dmix * y) * (s * (1.0 - s))).astype(BF16)
            do_ref[...] = lax.dot_general(dy, w_ref[...], NT_DIMS, preferred_element_type=F32).astype(BF16)

    row = lambda w: pl.BlockSpec((tm, w), lambda i: (i, 0))
    full = lambda a: pl.BlockSpec(a.shape, lambda i: (0, 0))
    wa_w, wb_w = oa.shape[1], ob.shape[1]
    outs = pl.pallas_call(
        body,
        name="merge_bwd",
        grid=(ni,),
        in_specs=[row(d), row(wa_w), row(wb_w), row(d), row(d), full(wa), full(wb), full(wo)] + [HBM_SPEC] * n_x,
        out_specs=[row(d), row(d), row(wa_w), row(wb_w), row(d), row(d), row(d)] + [HBM_SPEC] * n_x,
        out_shape=[
            jax.ShapeDtypeStruct((t, d), BF16), jax.ShapeDtypeStruct((t, d), BF16),
            jax.ShapeDtypeStruct((t, wa_w), BF16), jax.ShapeDtypeStruct((t, wb_w), BF16),
            jax.ShapeDtypeStruct((t, d), BF16), jax.ShapeDtypeStruct((t, d), BF16),
            jax.ShapeDtypeStruct((t, d), BF16),
        ] + (exchange.out_shape if exchange else []),
        scratch_shapes=exchange.scratch if exchange else [],
        compiler_params=_cparams(("arbitrary",) if exchange else ("parallel",)),
    )(dh, oa, ob, ga, gb, wa, wb, wo, *(exchange.ins if exchange else []))
    return outs[:7], outs[7:]


def _tri_dot(tri, x):
    hi = x.astype(BF16)
    r1 = x - hi.astype(F32)
    mid = r1.astype(BF16)
    lo = (r1 - mid.astype(F32)).astype(BF16)
    return (jnp.dot(tri, hi, preferred_element_type=F32)
            + jnp.dot(tri, mid, preferred_element_type=F32)
            + jnp.dot(tri, lo, preferred_element_type=F32))


def _forget_cumsum(f_logit, b_pad, nb):
    t, w = f_logit.shape
    bsz = t // (nb * BLOCK)

    def body(f_ref, b_ref, c_ref, carry):
        n = pl.program_id(1)

        @pl.when(n == 0)
        def _():
            carry[...] = jnp.zeros_like(carry)

        x = jax.nn.log_sigmoid(f_ref[...] + b_ref[...])
        rows = lax.broadcasted_iota(jnp.int32, (BLOCK, BLOCK), 0)
        cols = lax.broadcasted_iota(jnp.int32, (BLOCK, BLOCK), 1)
        tri = (cols <= rows).astype(BF16)
        c = _tri_dot(tri, x) + carry[...]
        c_ref[...] = c
        carry[...] = c[BLOCK - 1:BLOCK, :]

    return pl.pallas_call(
        body,
        name="forget_cumsum",
        grid=(bsz, nb),
        in_specs=[pl.BlockSpec((BLOCK, w), lambda b, n: (b * nb + n, 0)),
                  pl.BlockSpec((1, w), lambda b, n: (0, 0))],
        out_specs=pl.BlockSpec((BLOCK, w), lambda b, n: (b * nb + n, 0)),
        out_shape=jax.ShapeDtypeStruct((t, w), F32),
        scratch_shapes=[pltpu.VMEM((1, w), F32)],
        compiler_params=_cparams(("parallel", "arbitrary")),
    )(f_logit, b_pad)


def _forget_cumsum_bwd(dc, f_logit, b_pad, nb):
    t, w = f_logit.shape
    bsz = t // (nb * BLOCK)

    def body(dc_ref, f_ref, b_ref, df_ref, db_ref, carry):
        n = pl.program_id(1)

        @pl.when(n == 0)
        def _():
            carry[...] = jnp.zeros_like(carry)
            db_ref[...] = jnp.zeros_like(db_ref)

        rows = lax.broadcasted_iota(jnp.int32, (BLOCK, BLOCK), 0)
        cols = lax.broadcasted_iota(jnp.int32, (BLOCK, BLOCK), 1)
        tri = (cols >= rows).astype(BF16)
        dlf = _tri_dot(tri, dc_ref[...]) + carry[...]
        carry[...] = dlf[0:1, :]
        df = dlf * jax.nn.sigmoid(-(f_ref[...] + b_ref[...]))
        df_ref[...] = df.astype(BF16)
        db_ref[0] += jnp.sum(df, axis=0, keepdims=True)

    rev = lambda b, n: (b * nb + (nb - 1 - n), 0)
    return pl.pallas_call(
        body,
        name="forget_cumsum_bwd",
        grid=(bsz, nb),
        in_specs=[pl.BlockSpec((BLOCK, w), rev),
                  pl.BlockSpec((BLOCK, w), rev),
                  pl.BlockSpec((1, w), lambda b, n: (0, 0))],
        out_specs=[pl.BlockSpec((BLOCK, w), rev),
                   pl.BlockSpec((1, 1, w), lambda b, n: (b, 0, 0))],
        out_shape=[jax.ShapeDtypeStruct((t, w), BF16), jax.ShapeDtypeStruct((bsz, 1, w), F32)],
        scratch_shapes=[pltpu.VMEM((1, w), F32)],
        compiler_params=_cparams(("parallel", "arbitrary")),
    )(dc, f_logit, b_pad)


GROUP_ROWS = A_GROUP * BLOCK


def _stack_heads(ref, g):
    return jnp.concatenate([ref[:, (A_GROUP * g + i) * LANES:(A_GROUP * g + i + 1) * LANES] for i in range(A_GROUP)],
                           axis=0)


def _unstack_heads(ref, g, x):
    for i in range(A_GROUP):
        ref[:, (A_GROUP * g + i) * LANES:(A_GROUP * g + i + 1) * LANES] = x[i * BLOCK:(i + 1) * BLOCK].astype(ref.dtype)


def _swa_logits(q, keys, slope, n):
    qi = lax.broadcasted_iota(jnp.int32, (GROUP_ROWS, BLOCK), 0) & (BLOCK - 1)
    kj = lax.broadcasted_iota(jnp.int32, (GROUP_ROWS, BLOCK), 1)
    s_all = lax.dot_general(q, keys, NT_DIMS, preferred_element_type=F32) * SCALE
    out = []
    for i, (dist, ok) in enumerate((
            (n * BLOCK + qi - kj, (kj >= N_PAD) & (n * BLOCK + qi - kj >= 0)),
            (BLOCK + qi - kj, (kj > qi) & (n >= 2)),
            (qi - kj, (kj <= qi) & (n >= 1)))):
        s = s_all[:, i * BLOCK:(i + 1) * BLOCK] - slope * dist.astype(F32)
        out.append(jnp.where(ok, s, NEG))
    return out


def _three_blocks(m_ref, p_ref, c_ref):
    return jnp.concatenate([m_ref[...], p_ref[...], c_ref[...]], axis=0)


def _swa_specs(nb):
    row = lambda b, n: b * nb + n
    qspec = pl.BlockSpec((BLOCK, A_PAD_WIDTH), lambda b, n: (row(b, n), 0))
    kv_m = pl.BlockSpec((BLOCK, LANES), lambda b, n: (row(b, 0), 0))
    kv_p = pl.BlockSpec((BLOCK, LANES), lambda b, n: (row(b, jnp.maximum(n - 1, 0)), 0))
    kv_c = pl.BlockSpec((BLOCK, LANES), lambda b, n: (row(b, n), 0))
    rowspec = pl.BlockSpec((A_KV_HEADS, GROUP_ROWS, 1), lambda b, n: (0, 0, 0))
    lsespec = pl.BlockSpec((1, A_KV_HEADS, GROUP_ROWS, 1), lambda b, n: (row(b, n), 0, 0, 0))
    return qspec, kv_m, kv_p, kv_c, rowspec, lsespec


def _swa_fwd(q, k, v, sink_rows, slope_rows, nb):
    t = q.shape[0]
    bsz = t // (nb * BLOCK)

    def body(q_ref, km_ref, kp_ref, kc_ref, vm_ref, vp_ref, vc_ref, sink_ref, slope_ref, o_ref, lse_ref):
        n = pl.program_id(1)
        keys = _three_blocks(km_ref, kp_ref, kc_ref)
        values = _three_blocks(vm_ref, vp_ref, vc_ref)
        lane_group = lax.broadcasted_iota(jnp.int32, (GROUP_ROWS, LANES), 1) // HEAD_DIM
        for g in range(A_KV_HEADS):
            qq = _stack_heads(q_ref, g)
            sink = sink_ref[g]
            s_m, s_p, s_c = _swa_logits(qq, keys, slope_ref[g], n)
            m = jnp.maximum(jnp.max(jnp.maximum(jnp.maximum(s_m, s_p), s_c), axis=-1, keepdims=True), sink)
            m_wide = jnp.broadcast_to(m, (GROUP_ROWS, BLOCK))
            e_m = jnp.exp(s_m - m_wide)
            e_p = jnp.exp(s_p - m_wide)
            e_c = jnp.exp(s_c - m_wide)
            z = jnp.sum((e_m + e_p) + e_c, axis=-1, keepdims=True) + jnp.exp(sink - m)
            inv = jnp.broadcast_to(1.0 / z, (GROUP_ROWS, BLOCK))
            probs = jnp.concatenate([(e_m * inv).astype(BF16), (e_p * inv).astype(BF16), (e_c * inv).astype(BF16)],
                                    axis=1)
            o = jnp.dot(probs, values, preferred_element_type=F32)
            _unstack_heads(o_ref, g, jnp.where(lane_group == g, o, 0.0))
            lse_ref[0, g] = m + jnp.log(z)

    qspec, kv_m, kv_p, kv_c, rowspec, lsespec = _swa_specs(nb)
    return pl.pallas_call(
        body,
        name="swa_fwd",
        grid=(bsz, nb),
        in_specs=[qspec, kv_m, kv_p, kv_c, kv_m, kv_p, kv_c, rowspec, rowspec],
        out_specs=[qspec, lsespec],
        out_shape=[jax.ShapeDtypeStruct((t, A_PAD_WIDTH), BF16),
                   jax.ShapeDtypeStruct((t // BLOCK, A_KV_HEADS, GROUP_ROWS, 1), F32)],
        compiler_params=_cparams(("parallel", "arbitrary")),
    )(q, k, k, k, v, v, v, sink_rows, slope_rows)


def _swa_bwd(q, k, v, do, lse, sink_rows, slope_rows, nb):
    t = q.shape[0]
    l = nb * BLOCK
    bsz = t // l

    def body(q_ref, km_ref, kp_ref, kc_ref, vm_ref, vp_ref, vc_ref, do_ref, lse_ref, sink_ref, slope_ref,
             dq_ref, dk_ref, dv_ref, dsink_ref, dk_acc, dv_acc):
        n = pl.program_id(1)

        @pl.when(n == 0)
        def _():
            dk_acc[...] = jnp.zeros_like(dk_acc)
            dv_acc[...] = jnp.zeros_like(dv_acc)
            dsink_ref[...] = jnp.zeros_like(dsink_ref)

        keys = _three_blocks(km_ref, kp_ref, kc_ref)
        values = _three_blocks(vm_ref, vp_ref, vc_ref)
        first_half = lax.broadcasted_iota(jnp.int32, (BLOCK, LANES), 1) < HEAD_DIM
        prev = jnp.maximum(n - 1, 0)
        for g in range(A_KV_HEADS):
            qq = _stack_heads(q_ref, g)
            dob = _stack_heads(do_ref, g)
            lse = lse_ref[0, g]
            lse_wide = jnp.broadcast_to(lse, (GROUP_ROWS, BLOCK))
            probs = [jnp.exp(s - lse_wide) for s in _swa_logits(qq, keys, slope_ref[g], n)]
            dp_all = lax.dot_general(dob, values, NT_DIMS, preferred_element_type=F32)
            dps = [dp_all[:, i * BLOCK:(i + 1) * BLOCK] for i in range(3)]
            delta = jnp.sum((probs[0] * dps[0] + probs[1] * dps[1]) + probs[2] * dps[2], axis=-1, keepdims=True)
            delta_wide = jnp.broadcast_to(delta, (GROUP_ROWS, BLOCK))
            ds = jnp.concatenate([(p * (dp - delta_wide)).astype(BF16) for p, dp in zip(probs, dps)], axis=1)
            pb = jnp.concatenate([p.astype(BF16) for p in probs], axis=1)
            dq = jnp.dot(ds, keys, preferred_element_type=F32) * SCALE
            dk_all = lax.dot_general(ds, qq, TN_DIMS, preferred_element_type=F32) * SCALE
            dv_all = lax.dot_general(pb, dob, TN_DIMS, preferred_element_type=F32)
            for i, start in enumerate((0, prev * BLOCK, n * BLOCK)):
                rows = pl.ds(pl.multiple_of(start, BLOCK), BLOCK)
                dk_acc[rows, :] += dk_all[i * BLOCK:(i + 1) * BLOCK]
                dv_acc[rows, :] += dv_all[i * BLOCK:(i + 1) * BLOCK]
            for pair in range(A_GROUP // 2):
                even = dq[2 * pair * BLOCK:(2 * pair + 1) * BLOCK]
                odd = dq[(2 * pair + 1) * BLOCK:(2 * pair + 2) * BLOCK]
                left = even if g == 0 else pltpu.roll(even, HEAD_DIM, 1)
                right = pltpu.roll(odd, HEAD_DIM, 1) if g == 0 else odd
                tile = (A_GROUP // 2) * g + pair
                dq_ref[:, tile * LANES:(tile + 1) * LANES] = jnp.where(first_half, left, right).astype(BF16)
            dsink_ref[0, g] += -(jnp.exp(sink_ref[g] - lse) * delta)

        @pl.when(n == nb - 1)
        def _():
            dk_ref[...] = dk_acc[...].astype(BF16)
            dv_ref[...] = dv_acc[...].astype(BF16)

    qspec, kv_m, kv_p, kv_c, rowspec, lsespec = _swa_specs(nb)
    kv_all = pl.BlockSpec((l, LANES), lambda b, n: (b, 0))
    return pl.pallas_call(
        body,
        name="swa_bwd",
        grid=(bsz, nb),
        in_specs=[qspec, kv_m, kv_p, kv_c, kv_m, kv_p, kv_c, qspec, lsespec, rowspec, rowspec],
        out_specs=[pl.BlockSpec((BLOCK, A_WIDTH), lambda b, n: (b * nb + n, 0)), kv_all, kv_all,
                   pl.BlockSpec((1, A_KV_HEADS, GROUP_ROWS, 1), lambda b, n: (b, 0, 0, 0))],
        out_shape=[jax.ShapeDtypeStruct((t, A_WIDTH), BF16),
                   jax.ShapeDtypeStruct((t, LANES), BF16),
                   jax.ShapeDtypeStruct((t, LANES), BF16),
                   jax.ShapeDtypeStruct((bsz, A_KV_HEADS, GROUP_ROWS, 1), F32)],
        scratch_shapes=[pltpu.VMEM((l, LANES), F32), pltpu.VMEM((l, LANES), F32)],
        compiler_params=_cparams(("parallel", "arbitrary")),
    )(q, k, k, k, v, v, v, do, lse, sink_rows, slope_rows)


CHUNK = KEY_BLOCKS * BLOCK


def _fox_chunk(qb, ci):
    sb = jnp.maximum(jnp.minimum(KEY_BLOCKS * ci, qb + 1 - KEY_BLOCKS), 0)
    lo = jnp.maximum(ci * CHUNK, N_PAD)
    return sb, lo, pl.ds(pl.multiple_of(sb * BLOCK, BLOCK), CHUNK)


def _fox_logits(s_ref, cr_ref, e, j, sb, lo, qb):
    lane = lax.broadcasted_iota(jnp.int32, (BLOCK, BLOCK), 1)
    ahead = lane - lax.broadcasted_iota(jnp.int32, (BLOCK, BLOCK), 0)
    first = (sb + j) * BLOCK
    s = s_ref[e, :, j * BLOCK:(j + 1) * BLOCK] - cr_ref[e, sb + j]
    return jnp.where((ahead <= qb * BLOCK - first) & (lane >= lo - first), s, NEG)


FOX_PAIRS = 4
FOX_HEADS = 2 * FOX_PAIRS
FOX_STEPS = B_PAIRS // FOX_PAIRS


def _fox_specs(nb):
    l = nb * BLOCK
    q_spec = pl.BlockSpec((BLOCK, FOX_PAIRS * LANES), lambda b, p, i: (b * nb + i, p))
    kv_spec = pl.BlockSpec((l, FOX_HEADS * LANES), lambda b, p, i: (b, p))
    cc_spec = pl.BlockSpec((FOX_HEADS, BLOCK, 1), lambda b, p, i: (b * FOX_STEPS + p, i, 0))
    cr_spec = pl.BlockSpec((FOX_HEADS, nb, 1, BLOCK), lambda b, p, i: (b * FOX_STEPS + p, 0, 0, 0))
    return q_spec, kv_spec, cc_spec, cr_spec


def _fox_fwd(q, k, v, c_row, nb, exchange=None):
    t = q.shape[0]
    bsz = t // (nb * BLOCK)
    assert nb >= KEY_BLOCKS

    n_x = len(exchange.ins) if exchange else 0

    def body(*refs):
        q_ref, k_ref, v_ref, cr_ref = refs[:4]
        o_ref, ox_ref, lse_ref = refs[4 + n_x:7 + n_x]
        s_scr, hi_scr, lo_scr = refs[7 + 2 * n_x:10 + 2 * n_x]
        qb = pl.program_id(2)
        if exchange:
            first = (pl.program_id(0) == 0) & (pl.program_id(1) == 0)
            last = (pl.program_id(0) == bsz - 1) & (pl.program_id(1) == FOX_STEPS - 1)
            _host_exchange(exchange, refs[4:4 + n_x], refs[7 + n_x:7 + 2 * n_x], refs[10 + 2 * n_x:],
                           first & (qb == 0), first & (qb == 2 * nb // 3), last & (qb == nb - 1),
                           late=last & (qb == 0))
        qs = [q_ref[:, a * LANES:(a + 1) * LANES] * SCALE for a in range(FOX_PAIRS)]
        first_half = lax.broadcasted_iota(jnp.int32, (BLOCK, LANES), 1) < HEAD_DIM

        def step(ci, carry):
            stats, accs = carry[:2 * FOX_HEADS], carry[2 * FOX_HEADS:]
            sb, lo, krows = _fox_chunk(qb, ci)
            new_stats, new_accs = [], []
            for a in range(FOX_PAIRS):
                alphas = []
                pv = jnp.zeros((BLOCK, LANES), F32)
                pv_lo = jnp.zeros((BLOCK, LANES), F32)
                for e in (2 * a, 2 * a + 1):
                    m, z = stats[2 * e], stats[2 * e + 1]
                    tile = slice(e * LANES, (e + 1) * LANES)
                    s_scr[e] = lax.dot_general(qs[a], k_ref[krows, tile], NT_DIMS, preferred_element_type=F32)
                    top = None
                    for j in range(KEY_BLOCKS):
                        s = _fox_logits(s_scr, cr_ref, e, j, sb, lo, qb)
                        s_scr[e, :, j * BLOCK:(j + 1) * BLOCK] = s
                        top = s if top is None else jnp.maximum(top, s)
                    m_new = jnp.maximum(m, jnp.max(top, axis=-1, keepdims=True))
                    alpha = jnp.exp(m - m_new)
                    m_wide = jnp.broadcast_to(m_new, (BLOCK, BLOCK))
                    total = None
                    for j in range(KEY_BLOCKS):
                        cols = slice(j * BLOCK, (j + 1) * BLOCK)
                        p = jnp.exp(s_scr[e, :, cols] - m_wide)
                        total = p if total is None else total + p
                        hi = p.astype(BF16)
                        hi_scr[e, :, cols] = hi
                        lo_scr[e, :, cols] = (p - hi.astype(F32)).astype(BF16)
                    z = alpha * z + jnp.sum(total, axis=-1, keepdims=True)
                    vv = v_ref[krows, tile]
                    pv = pv + jnp.dot(hi_scr[e], vv, preferred_element_type=F32)
                    pv_lo = pv_lo + jnp.dot(lo_scr[e], vv, preferred_element_type=F32)
                    new_stats += [m_new, z]
                    alphas.append(alpha)
                alpha = jnp.where(first_half, alphas[0], alphas[1])
                new_accs += [alpha * accs[2 * a] + pv, alpha * accs[2 * a + 1] + pv_lo]
            return (*new_stats, *new_accs)

        col = lambda val: jnp.full((BLOCK, 1), val, F32)
        done = lax.fori_loop(
            0, (qb + KEY_BLOCKS) // KEY_BLOCKS, step,
            (col(NEG), col(0.0)) * FOX_HEADS + (jnp.zeros((BLOCK, LANES), F32),) * (2 * FOX_PAIRS))
        for a in range(FOX_PAIRS):
            m0, z0, m1, z1 = done[4 * a:4 * a + 4]
            acc, acc_lo = done[2 * FOX_HEADS + 2 * a:2 * FOX_HEADS + 2 * a + 2]
            inv = 1.0 / jnp.where(first_half, z0, z1)
            tile = slice(a * LANES, (a + 1) * LANES)
            o_ref[:, tile] = (acc * inv).astype(BF16)
            ox_ref[:, tile] = (acc + acc_lo) * inv
            lse_ref[2 * a] = m0 + jnp.log(z0)
            lse_ref[2 * a + 1] = m1 + jnp.log(z1)

    q_spec, kv_spec, cc_spec, cr_spec = _fox_specs(nb)
    outs = pl.pallas_call(
        body,
        name="fox_fwd",
        grid=(bsz, FOX_STEPS, nb),
        in_specs=[q_spec, kv_spec, kv_spec, cr_spec] + [HBM_SPEC] * n_x,
        out_specs=[q_spec, q_spec, cc_spec] + [HBM_SPEC] * n_x,
        out_shape=[jax.ShapeDtypeStruct((t, B_WIDTH), BF16), jax.ShapeDtypeStruct((t, B_WIDTH), F32),
                   jax.ShapeDtypeStruct((bsz * B_HEADS, nb * BLOCK, 1), F32)] + (exchange.out_shape if exchange else []),
        scratch_shapes=[pltpu.VMEM((FOX_HEADS, BLOCK, CHUNK), F32), pltpu.VMEM((FOX_HEADS, BLOCK, CHUNK), BF16),
                        pltpu.VMEM((FOX_HEADS, BLOCK, CHUNK), BF16)] + (exchange.scratch if exchange else []),
        compiler_params=_cparams(("arbitrary",) * 3 if exchange else ("parallel", "parallel", "arbitrary")),
    )(q, k, v, c_row, *(exchange.ins if exchange else []))
    return outs[:3], outs[3:]


def _fox_bwd(q, k, v, o_exact, do, lse, c_row, nb, exchange=None):
    t = q.shape[0]
    l = nb * BLOCK
    bsz = t // l

    n_x = len(exchange.ins) if exchange else 0

    def body(*refs):
        q_ref, k_ref, v_ref, ox_ref, do_ref, lse_ref, cr_ref = refs[:7]
        dq_ref, dk_ref, dv_ref, dc_ref = refs[7 + n_x:11 + n_x]
        dk_acc, dv_acc, s_scr, dp_scr, p_scr, ds_scr = refs[11 + 2 * n_x:17 + 2 * n_x]
        qb = pl.program_id(2)
        if exchange:
            first = (pl.program_id(0) == 0) & (pl.program_id(1) == 0)
            last = (pl.program_id(0) == bsz - 1) & (pl.program_id(1) == FOX_STEPS - 1)
            _host_exchange(exchange, refs[7:7 + n_x], refs[11 + n_x:11 + 2 * n_x], refs[17 + 2 * n_x:],
                           first & (qb == 0), last & (qb == 0), last & (qb == nb - 1))

        @pl.when(qb == 0)
        def _():
            dk_acc[...] = jnp.zeros_like(dk_acc)
            dv_acc[...] = jnp.zeros_like(dv_acc)
            dc_ref[...] = jnp.zeros_like(dc_ref)

        top_half = lax.broadcasted_iota(jnp.int32, (LANES, BLOCK), 0) < HEAD_DIM
        pair_t = lambda x: jnp.concatenate([jnp.where(top_half, x.T, 0), jnp.where(top_half, 0, x.T)], axis=1)
        first_half = lax.broadcasted_iota(jnp.int32, (BLOCK, LANES), 1) < HEAD_DIM
        wide = lambda col: jnp.broadcast_to(col, (BLOCK, BLOCK))
        qs, dobs, qs_t, dob_t, deltas = [], [], [], [], []
        for a in range(FOX_PAIRS):
            tile = slice(a * LANES, (a + 1) * LANES)
            qs.append(q_ref[:, tile] * SCALE)
            dobs.append(do_ref[:, tile])
            qs_t.append(pair_t(qs[a]))
            dob_t.append(pair_t(dobs[a]))
            weighted = dobs[a].astype(F32) * ox_ref[:, tile]
            deltas += [wide(jnp.sum(jnp.where(first_half, weighted, 0.0), axis=-1, keepdims=True)),
                       wide(jnp.sum(jnp.where(first_half, 0.0, weighted), axis=-1, keepdims=True))]
        lses = [wide(lse_ref[e]) for e in range(FOX_HEADS)]

        def step(ci, dqs):
            sb, lo, krows = _fox_chunk(qb, ci)
            dqs = list(dqs)
            for a in range(FOX_PAIRS):
                for e in (2 * a, 2 * a + 1):
                    tile = slice(e * LANES, (e + 1) * LANES)
                    kk = k_ref[krows, tile]
                    s_scr[e] = lax.dot_general(qs[a], kk, NT_DIMS, preferred_element_type=F32)
                    dp_scr[e] = lax.dot_general(dobs[a], v_ref[krows, tile], NT_DIMS, preferred_element_type=F32)
                    for j in range(KEY_BLOCKS):
                        cols = slice(j * BLOCK, (j + 1) * BLOCK)
                        p = jnp.exp(_fox_logits(s_scr, cr_ref, e, j, sb, lo, qb) - lses[e])
                        ds = p * (dp_scr[e, :, cols] - deltas[e])
                        dc_ref[e, sb + j] -= jnp.sum(ds, axis=0, keepdims=True)
                        p_scr[e, :, cols] = p.astype(BF16)
                        ds_scr[e, :, cols] = ds.astype(BF16)
                    dqs[a] = dqs[a] + jnp.dot(ds_scr[e], kk, preferred_element_type=F32)
                both = slice(2 * a, 2 * a + 2)
                dk_t = jnp.dot(qs_t[a], ds_scr[both].reshape(2 * BLOCK, CHUNK), preferred_element_type=F32)
                dv_t = jnp.dot(dob_t[a], p_scr[both].reshape(2 * BLOCK, CHUNK), preferred_element_type=F32)
                for j in range(KEY_BLOCKS):
                    cols = slice(j * BLOCK, (j + 1) * BLOCK)
                    dk_acc[a * nb + sb + j] += dk_t[:, cols]
                    dv_acc[a * nb + sb + j] += dv_t[:, cols]
            return tuple(dqs)

        dqs = lax.fori_loop(0, (qb + KEY_BLOCKS) // KEY_BLOCKS, step,
                            (jnp.zeros((BLOCK, LANES), F32),) * FOX_PAIRS)
        for a in range(FOX_PAIRS):
            dq_ref[:, a * LANES:(a + 1) * LANES] = (dqs[a] * SCALE).astype(BF16)

        @pl.when(qb == nb - 1)
        def _():
            for a in range(FOX_PAIRS):
                for kb in range(nb):
                    rows = slice(kb * BLOCK, (kb + 1) * BLOCK)
                    for acc, out_ref in ((dk_acc, dk_ref), (dv_acc, dv_ref)):
                        out_ref[rows, a * LANES:(a + 1) * LANES] = acc[a * nb + kb].T.astype(BF16)

    q_spec, kv_spec, cc_spec, cr_spec = _fox_specs(nb)
    dkv_spec = pl.BlockSpec((l, FOX_PAIRS * LANES), lambda b, p, i: (b, p))
    outs = pl.pallas_call(
        body,
        name="fox_bwd",
        grid=(bsz, FOX_STEPS, nb),
        in_specs=[q_spec, kv_spec, kv_spec, q_spec, q_spec, cc_spec, cr_spec] + [HBM_SPEC] * n_x,
        out_specs=[q_spec, dkv_spec, dkv_spec, cr_spec] + [HBM_SPEC] * n_x,
        out_shape=[jax.ShapeDtypeStruct((t, B_WIDTH), BF16), jax.ShapeDtypeStruct((t, B_WIDTH), BF16),
                   jax.ShapeDtypeStruct((t, B_WIDTH), BF16),
                   jax.ShapeDtypeStruct((bsz * B_HEADS, nb, 1, BLOCK), F32)] + (exchange.out_shape if exchange else []),
        scratch_shapes=[pltpu.VMEM((FOX_PAIRS * nb, LANES, BLOCK), F32), pltpu.VMEM((FOX_PAIRS * nb, LANES, BLOCK), F32),
                        pltpu.VMEM((FOX_HEADS, BLOCK, CHUNK), F32), pltpu.VMEM((FOX_HEADS, BLOCK, CHUNK), F32),
                        pltpu.VMEM((FOX_HEADS, BLOCK, CHUNK), BF16), pltpu.VMEM((FOX_HEADS, BLOCK, CHUNK), BF16)]
        + (exchange.scratch if exchange else []),
        compiler_params=_cparams(("arbitrary",) * 3 if exchange else ("parallel", "parallel", "arbitrary")),
    )(q, k, v, o_exact, do, lse, c_row, *(exchange.ins if exchange else []))
    return outs[:4], outs[4:]


def _loss_head(h, final_w, target):
    bsz, l, d = h.shape
    nb = l // BLOCK

    def body(h_ref, w_ref, t_ref, loss_ref, dh_ref, dw_ref):
        b = pl.program_id(0)
        n = pl.program_id(1)

        @pl.when((b == 0) & (n == 0))
        def _():
            loss_ref[...] = jnp.zeros_like(loss_ref)
            dw_ref[...] = jnp.zeros_like(dw_ref)

        @pl.when(n == 0)
        def _():
            dh_ref[...] = jnp.zeros_like(dh_ref)

        @pl.when(n > 0)
        def _():
            hh = h_ref[0]
            w = w_ref[...]
            r = _rms_scale(hh)
            err = (hh * r) * w - t_ref[0]
            loss_ref[...] += 0.5 * jnp.sum(jnp.mean(err * err, axis=-1, keepdims=True), axis=0, keepdims=True)
            dy = err * (1.0 / d)
            dh, dw = _rms_bwd(dy, hh, w)
            dh_ref[0] = dh
            dw_ref[...] += dw

    return pl.pallas_call(
        body,
        name="loss_head",
        grid=(bsz, nb),
        in_specs=[
            pl.BlockSpec((1, BLOCK, d), lambda b, n: (b, n, 0)),
            pl.BlockSpec((1, d), lambda b, n: (0, 0)),
            pl.BlockSpec((1, BLOCK, d), lambda b, n: (b, jnp.maximum(n - 1, 0), 0)),
        ],
        out_specs=[
            pl.BlockSpec((1, 128), lambda b, n: (0, 0)),
            pl.BlockSpec((1, BLOCK, d), lambda b, n: (b, n, 0)),
            pl.BlockSpec((1, d), lambda b, n: (0, 0)),
        ],
        out_shape=[jax.ShapeDtypeStruct((1, 128), F32), jax.ShapeDtypeStruct((bsz, l, d), F32),
                   jax.ShapeDtypeStruct((1, d), F32)],
        compiler_params=_cparams(("arbitrary", "arbitrary")),
    )(h, final_w, target)


def _pad_tiles(w, src, heads, lane_slot, axis):
    pieces = []
    for h in range(heads):
        x = lax.slice_in_dim(w, src + HEAD_DIM * h, src + HEAD_DIM * (h + 1), axis=axis)
        z = jnp.zeros_like(x)
        pieces += [x, z] if lane_slot(h) == 0 else [z, x]
    return pieces


def _unpad_tiles(g, off, heads, lane_slot, axis):
    return [lax.slice_in_dim(g, off + LANES * h + HEAD_DIM * lane_slot(h),
                             off + LANES * h + HEAD_DIM * (lane_slot(h) + 1), axis=axis) for h in range(heads)]


def _layout_w_in(w):
    pad_f = jnp.zeros((w.shape[0], F_COLS - B_HEADS), w.dtype)
    return jnp.concatenate([w[:, :SRC_F], w[:, SRC_GA:], w[:, SRC_F:SRC_GA], pad_f], axis=1)


def _unlayout_w_in(g):
    return jnp.concatenate([g[:, :OFF_GA], g[:, OFF_F:OFF_F + B_HEADS], g[:, OFF_GA:OFF_F]], axis=1)


def _local_step(x, target, meta, norms, b_forget, sinks, w, comm=None):
    n1, nmix, n2, nfin = norms
    w1i, w1o = w[:2]
    bsz, seq, d = x.shape
    l = PREFIX + seq
    nb = l // BLOCK
    t = bsz * l

    h0 = jnp.concatenate([jnp.zeros((bsz, N_PAD, d), F32),
                          jnp.broadcast_to(meta[None], (bsz, N_META, d)), x], axis=1).reshape(t, d)

    if comm is None:
        (h1, g1, u1), _ = _ffn_fwd(h0, n1, w1i, w1o)
        w_in, wa, wb, wo, w2i, w2o = w[2:]
    else:
        (h1, g1, u1), gathered = _ffn_fwd(h0, n1, w1i, w1o, comm.gather(GATHER_PROJ))
        w_in, = comm.gathered(GATHER_PROJ, gathered)
    wp = _layout_w_in(w_in)
    un, qa, ka, va, qb, kb, vb, ga, gb, f_logit = _proj_fwd(h1, nmix, wp)
    b_pad = jnp.concatenate([b_forget, jnp.zeros((1, F_COLS - B_HEADS), F32)], axis=1)
    c = _forget_cumsum(f_logit, b_pad, nb)
    c_heads = c[:, :B_HEADS].reshape(bsz, l, B_HEADS).transpose(0, 2, 1).reshape(bsz * B_HEADS, l)
    c_row = c_heads.reshape(bsz * B_HEADS, nb, 1, BLOCK)

    slopes = jnp.exp2(-8.0 * jnp.arange(1, A_HEADS + 1, dtype=F32) / A_HEADS)
    slope_rows = jnp.repeat(slopes.reshape(A_KV_HEADS, A_GROUP), BLOCK, axis=1)[:, :, None]
    sink_rows = jnp.repeat(sinks.reshape(A_KV_HEADS, A_GROUP), BLOCK, axis=1)[:, :, None]

    oa, lse_a = _swa_fwd(qa, ka, va, sink_rows, slope_rows, nb)
    if comm is None:
        (ob, ob_exact, lse_b), _ = _fox_fwd(qb, kb, vb, c_row, nb)
    else:
        (ob, ob_exact, lse_b), gathered = _fox_fwd(qb, kb, vb, c_row, nb, comm.gather(GATHER_LATE))
        wa, wb, wo, w2i, w2o = comm.gathered(GATHER_LATE, gathered)
    wa_p = jnp.concatenate(_pad_tiles(wa, 0, A_HEADS, A_SLOT, 0), axis=0)
    h2, mixed = _merge_fwd(h1, oa, ob, ga, gb, wa_p, wb, wo)
    (h3, g2, u2), _ = _ffn_fwd(h2, n2, w2i, w2o)
    loss, dh3, d_nfin = _loss_head(h3.reshape(bsz, l, d), nfin, target)

    (dh2, n2b, a2, dgu2, df2, dn2_parts), _ = _ffn_bwd(dh3.reshape(t, d), h2, n2, g2, u2, w2i, w2o)
    g_w2o = _tn_matmul(a2, df2, "grad_ffn2_w_out")
    g_w2i = _tn_matmul(n2b, dgu2, "grad_ffn2_w_in")

    hosted = comm.swap("ffn2", dict(ffn2_w_in=g_w2i, ffn2_w_out=g_w2o)) if comm else None
    (dya, dyb, doa, dob, dga, dgb, dh2b), swapped = _merge_bwd(dh2, oa, ob, ga, gb, wa_p, wb, wo, hosted)
    g_wo = _tn_matmul(mixed, dh2b, "grad_w_out")
    g_wa = jnp.concatenate(_unpad_tiles(_tn_matmul(oa, dya, "grad_w_branch_a"), 0, A_HEADS, A_SLOT, 0), axis=0)
    g_wb = _tn_matmul(ob, dyb, "grad_w_branch_b")

    dqa, dka, dva, dsink_rows = _swa_bwd(qa, ka, va, doa, lse_a, sink_rows, slope_rows, nb)
    hosted = comm.scatter("ffn2", swapped) if comm else None
    (dqb, dkb, dvb, dc_row), pieces = _fox_bwd(qb, kb, vb, ob_exact, dob, lse_b, c_row, nb, hosted)
    if comm:
        comm.received("ffn2", pieces)
    dc = dc_row.reshape(bsz, B_HEADS, l).transpose(0, 2, 1).reshape(t, B_HEADS)
    dc = jnp.concatenate([dc, jnp.zeros((t, F_COLS - B_HEADS), F32)], axis=1)
    df_logit, db_parts = _forget_cumsum_bwd(dc, f_logit, b_pad, nb)

    dproj = jnp.concatenate([dqa, dka, dva, dqb, dkb, dvb, dga, dgb, df_logit], axis=1)
    g_win = _unlayout_w_in(_tn_matmul(un, dproj, "grad_w_in"))
    hosted = comm.swap("mixer", dict(w_in=g_win, w_branch_a=g_wa, w_branch_b=g_wb, w_out=g_wo)) if comm else None
    (dh1, dnmix_parts), swapped = _proj_bwd(dh2, h1, nmix, dproj, wp, hosted)
    hosted = comm.scatter("mixer", swapped) if comm else None
    (dh0, n1b, a1, dgu1, df1, dn1_parts), pieces = _ffn_bwd(dh1, h0, n1, g1, u1, w1i, w1o, hosted)
    dh0 = dh0.reshape(bsz, l, d)
    grad_x = dh0[:, PREFIX:]
    small = dict(
        meta_tokens=jnp.sum(dh0[:, N_PAD:PREFIX], axis=0),
        ffn1_norm=jnp.sum(dn1_parts, axis=0),
        mix_norm=jnp.sum(dnmix_parts, axis=0),
        ffn2_norm=jnp.sum(dn2_parts, axis=0),
        final_norm=d_nfin,
        b_forget=jnp.sum(db_parts, axis=0)[:, :B_HEADS],
        attn_sinks=jnp.sum(dsink_rows.reshape(bsz, A_HEADS, BLOCK), axis=(0, 2)).reshape(1, A_HEADS),
    )
    if comm is None:
        g_w1o = _tn_matmul(a1, df1, "grad_ffn1_w_out")
        g_w1i = _tn_matmul(n1b, dgu1, "grad_ffn1_w_in")
    else:
        comm.received("mixer", pieces)
        g_w1o, gathered = _tn_matmul(a1, df1, "grad_ffn1_w_out", comm.small_gather(loss, small))
        comm.small_gathered(gathered)
        swapped = _run_exchange(comm.swap("ffn1_out", dict(ffn1_w_out=g_w1o)), "exchange_halves_ffn1_out")
        g_w1i, pieces = _tn_matmul(n1b, dgu1, "grad_ffn1_w_in", comm.scatter("ffn1_out", swapped))
        comm.received("ffn1_out", pieces)
    big = dict(ffn1_w_in=g_w1i, ffn1_w_out=g_w1o, w_in=g_win, w_branch_a=g_wa, w_branch_b=g_wb,
               w_out=g_wo, ffn2_w_in=g_w2i, ffn2_w_out=g_w2o)
    return loss, grad_x, small, big


BIG = (
    ("ffn1_w_in", (D_MODEL, 5632), 1),
    ("ffn1_w_out", (2816, D_MODEL), 0),
    ("w_in", (D_MODEL, W_IN_COLS), 1),
    ("w_branch_a", (A_WIDTH, D_MODEL), 1),
    ("w_branch_b", (B_WIDTH, D_MODEL), 1),
    ("w_out", (D_MODEL, D_MODEL), 0),
    ("ffn2_w_in", (D_MODEL, 5632), 1),
    ("ffn2_w_out", (2816, D_MODEL), 0),
)
STACKED = "w_in"


def _coords():
    return lax.axis_index("x"), lax.axis_index("y"), lax.axis_index("c")


def _other_chips(x, y):
    return ((1 - x, y), (x, 1 - y), (1 - x, 1 - y))


def _chip_part(ref, name, shape, axis, k):
    if name == STACKED:
        return ref.at[k]
    size = shape[axis] // N_CHIPS
    start = pl.multiple_of(k * size, size)
    return ref.at[pl.ds(start, size), :] if axis == 0 else ref.at[:, pl.ds(start, size)]


def _full_shape(name, shape):
    return (N_CHIPS, shape[0], shape[1] // N_CHIPS) if name == STACKED else shape


class _Exchange:
    def __init__(self, ins, out_shape, n_sems, ops):
        self.ins, self.out_shape, self.n_sems, self.ops = list(ins), list(out_shape), n_sems, ops

    @property
    def scratch(self):
        return [pltpu.SemaphoreType.DMA((self.n_sems,)), pltpu.SemaphoreType.DMA((self.n_sems,))]


SEMS_PER_GATHER = 9


def _gather_exchange(shards, table):
    n = len(table)
    x_nbr, y_nbr, diagonal = 0, 1, 2

    def ops(ins, outs, send_sems, recv_sems):
        x, y, c = _coords()
        mine = 2 * x + y
        sibling = (x, y, 1 - c)
        chips = _other_chips(x, y)
        slots = [2 * chip[0] + chip[1] for chip in chips]

        def part(i, k):
            name, shape, axis = table[i][:3]
            return _chip_part(outs[i], name, shape, axis, k)

        def half(ref, h):
            rows = ref.shape[0] // 2
            return ref.at[pl.ds(pl.multiple_of(h * rows, rows), rows), :]

        def remote(i, sem, src, dst, device):
            sem = SEMS_PER_GATHER * i + sem
            return pltpu.make_async_remote_copy(src, dst, send_sems.at[sem], recv_sems.at[sem],
                                                device_id=device, device_id_type=MESH_ID)

        def own(i):
            return remote(i, 0, ins[i], part(i, mine), sibling)

        def fetch(i, j, slot):
            if table[i][4]:
                src, dst = half(ins[i], c), half(part(i, slot), c)
            else:
                src, dst = ins[i], part(i, slot)
            return remote(i, 1 + j, src, dst, (chips[j][0], chips[j][1], c))

        def relayed(i, via, of):
            region = half(half(part(i, slots[of]), c), via)
            return remote(i, 4 + via, region, region, (chips[via][0], chips[via][1], c))

        def forward(i, j, h):
            region = half(part(i, slots[j]), h)
            return remote(i, 6 + j, region, region, sibling)

        def start():
            for i in range(n):
                for j in (x_nbr, y_nbr) if table[i][4] else (x_nbr, y_nbr, diagonal):
                    fetch(i, j, mine).start()
            for i in range(n):
                own(i).start()

        def relay():
            for i in range(n):
                if not table[i][4]:
                    for j in range(3):
                        fetch(i, j, slots[j]).wait_recv()
                    continue
                fetch(i, y_nbr, slots[y_nbr]).wait_recv()
                relayed(i, x_nbr, y_nbr).start()
                forward(i, y_nbr, c).start()
                fetch(i, x_nbr, slots[x_nbr]).wait_recv()
                relayed(i, y_nbr, x_nbr).start()
                forward(i, x_nbr, c).start()

        def relay_diagonal():
            for i in range(n):
                if table[i][4]:
                    relayed(i, x_nbr, diagonal).wait_recv()
                    relayed(i, y_nbr, diagonal).wait_recv()
                    forward(i, diagonal, c).start()

        def finish():
            for i in range(n):
                own(i).wait()
                for j in range(3):
                    if table[i][4]:
                        forward(i, j, 1 - c).wait_recv()
                        forward(i, j, c).wait_send()
                    if j != diagonal or not table[i][4]:
                        fetch(i, j, mine).wait_send()
                if table[i][4]:
                    relayed(i, x_nbr, y_nbr).wait_send()
                    relayed(i, y_nbr, x_nbr).wait_send()

        return start, relay, relay_diagonal, finish

    out_shape = [jax.ShapeDtypeStruct(_full_shape(name, shape), dtype) for name, shape, _, dtype, _ in table]
    return _Exchange(shards, out_shape, SEMS_PER_GATHER * n, ops)


def _run_exchange(exchange, name):
    n = len(exchange.ins)

    def body(*refs):
        for phase in exchange.ops(refs[:n], refs[n:2 * n], *refs[2 * n:]):
            phase()

    return pl.pallas_call(
        body,
        name=name,
        in_specs=[HBM_SPEC] * n,
        out_specs=[HBM_SPEC] * n,
        out_shape=exchange.out_shape,
        scratch_shapes=exchange.scratch,
    )(*exchange.ins)


def _host_exchange(exchange, in_refs, out_refs, sem_refs, first, middle, last, late=None):
    start, *relays, finish = exchange.ops(in_refs, out_refs, *sem_refs)
    pl.when(first)(start)
    pl.when(middle)(relays[0])
    if len(relays) > 1:
        pl.when(last if late is None else late)(relays[1])
    pl.when(last)(finish)


def _halves_view(name, shape, axis):
    r, c = shape
    if name == STACKED:
        return (N_CHIPS, 2, r // 2, c // N_CHIPS), lambda ref, h: ref.at[:, h]
    if axis == 1:
        return (2, r // 2, c), lambda ref, h: ref.at[h]
    return (N_CHIPS, 2, r // N_CHIPS // 2, c), lambda ref, h: ref.at[:, h]


def _halves_exchange(grads, entries):
    n_w = len(entries)
    views = [_halves_view(*entry) for entry in entries]

    def ops(ins, outs, send_sems, recv_sems):
        x, y, c = _coords()
        copies = [pltpu.make_async_remote_copy(views[i][1](ins[i], 1 - c), outs[i], send_sems.at[i], recv_sems.at[i],
                                               device_id=(x, y, 1 - c), device_id_type=MESH_ID) for i in range(n_w)]

        def start():
            for cp in copies:
                cp.start()

        def finish():
            for cp in copies:
                cp.wait()

        return start, lambda: None, finish

    half_shape = lambda v: tuple(d for i, d in enumerate(v) if i != (1 if len(v) == 4 else 0))
    out_shape = [jax.ShapeDtypeStruct(half_shape(v[0]), F32) for v in views]
    return _Exchange([g.reshape(v[0]) for g, v in zip(grads, views)], out_shape, n_w, ops)


def _add_sibling(g_view, recv, c, name):
    shape = recv.shape
    if len(shape) == 2:
        tr = _tile(shape[0], 128, 16)
        grid = (shape[0] // tr,)
        g_spec = pl.BlockSpec((None, tr, shape[1]), lambda i, c_ref: (c_ref[0], i, 0))
        r_spec = pl.BlockSpec((tr, shape[1]), lambda i, c_ref: (i, 0))
    else:
        tr = _tile(shape[1], 256, 16)
        grid = (N_CHIPS, shape[1] // tr)
        g_spec = pl.BlockSpec((None, None, tr, shape[2]), lambda k, i, c_ref: (k, c_ref[0], i, 0))
        r_spec = pl.BlockSpec((None, tr, shape[2]), lambda k, i, c_ref: (k, i, 0))

    def body(c_ref, g_ref, r_ref, o_ref):
        o_ref[...] = (g_ref[...] + r_ref[...]).astype(BF16)

    return pl.pallas_call(
        body,
        name="add_sibling_" + name,
        grid_spec=pltpu.PrefetchScalarGridSpec(num_scalar_prefetch=1, grid=grid, in_specs=[g_spec, r_spec],
                                               out_specs=r_spec),
        out_shape=jax.ShapeDtypeStruct(shape, BF16),
        compiler_params=_cparams(("parallel",) * len(grid)),
    )(c, g_view, recv)


def _piece_of(ref, name, axis, k):
    if name == STACKED or axis == 0:
        return ref.at[k]
    size = ref.shape[1] // N_CHIPS
    return ref.at[:, pl.ds(pl.multiple_of(k * size, size), size)]


def _piece_shape(name, shape, axis):
    r, c = shape
    return (r // 2, c // N_CHIPS) if (axis == 1) else (r // N_CHIPS // 2, c)


def _scatter_exchange(partials, entries):
    n_w = len(entries)

    def ops(ins, outs, send_sems, recv_sems):
        x, y, c = _coords()
        chips = _other_chips(x, y)
        copies = []
        for i, (name, _, axis) in enumerate(entries):
            for j, chip in enumerate(chips):
                sem = 3 * i + j
                copies.append(pltpu.make_async_remote_copy(
                    _piece_of(ins[i], name, axis, 2 * chip[0] + chip[1]), outs[i].at[j], send_sems.at[sem],
                    recv_sems.at[sem], device_id=(chip[0], chip[1], c), device_id_type=MESH_ID))

        def start():
            for cp in copies:
                cp.start()

        def finish():
            for cp in copies:
                cp.wait()

        return start, lambda: None, finish

    out_shape = [jax.ShapeDtypeStruct((3,) + _piece_shape(*entry), BF16) for entry in entries]
    return _Exchange(partials, out_shape, 3 * n_w, ops)


def _add_chips(partial, recv, mine, name, axis):
    rows, cols = recv.shape[1:]
    tr = _tile(rows, 256, 16)
    if name == STACKED or axis == 0:
        p_spec = pl.BlockSpec((None, tr, cols), lambda i, k_ref: (k_ref[0], i, 0))
    else:
        p_spec = pl.BlockSpec((tr, cols), lambda i, k_ref: (i, k_ref[0]))

    def body(k_ref, p_ref, r_ref, o_ref):
        f32 = lambda a: a.astype(F32)
        o_ref[...] = ((f32(p_ref[...]) + f32(r_ref[0])) + f32(r_ref[1])) + f32(r_ref[2])

    return pl.pallas_call(
        body,
        name="add_chips_" + name,
        grid_spec=pltpu.PrefetchScalarGridSpec(
            num_scalar_prefetch=1, grid=(rows // tr,),
            in_specs=[p_spec, pl.BlockSpec((3, tr, cols), lambda i, k_ref: (0, i, 0))],
            out_specs=pl.BlockSpec((tr, cols), lambda i, k_ref: (i, 0))),
        out_shape=jax.ShapeDtypeStruct((rows, cols), F32),
        compiler_params=_cparams(("parallel",)),
    )(mine, partial, recv)


def _share_with_sibling(halves):
    n_w = len(halves)

    def body(*refs):
        ins, outs = refs[:n_w], refs[n_w:2 * n_w]
        send_sems, recv_sems = refs[2 * n_w:]
        x, y, c = _coords()
        copies = [pltpu.make_async_remote_copy(ins[i], outs[i], send_sems.at[i], recv_sems.at[i],
                                               device_id=(x, y, 1 - c), device_id_type=MESH_ID) for i in range(n_w)]
        for cp in copies:
            cp.start()
        for cp in copies:
            cp.wait()

    return pl.pallas_call(
        body,
        name="share_with_sibling",
        in_specs=[HBM_SPEC] * n_w,
        out_specs=[HBM_SPEC] * n_w,
        out_shape=[jax.ShapeDtypeStruct(h.shape, F32) for h in halves],
        scratch_shapes=[pltpu.SemaphoreType.DMA((n_w,)), pltpu.SemaphoreType.DMA((n_w,))],
    )(*halves)


SMALL_ROWS = 168


def _small_exchange(buf):
    def ops(ins, outs, send_sems, recv_sems):
        x, y, c = _coords()
        me = 4 * x + 2 * y + c
        peers = [(x ^ fx, y ^ fy, c ^ fc) for fx in (0, 1) for fy in (0, 1) for fc in (0, 1)][1:]

        def copy(j, slot, dev):
            return pltpu.make_async_remote_copy(ins[0], outs[0].at[slot], send_sems.at[j], recv_sems.at[j],
                                                device_id=dev, device_id_type=MESH_ID)

        own = pltpu.make_async_copy(ins[0], outs[0].at[me], send_sems.at[N_DEV - 1])

        def start():
            own.start()
            for j, dev in enumerate(peers):
                copy(j, me, dev).start()

        def finish():
            for j, dev in enumerate(peers):
                copy(j, 4 * dev[0] + 2 * dev[1] + dev[2], dev).wait()
            own.wait()

        return start, lambda: None, finish

    return _Exchange([buf], [jax.ShapeDtypeStruct((N_DEV,) + buf.shape, F32)], N_DEV, ops)


def _sum_devices(gathered):
    def body(g_ref, out_ref):
        acc = g_ref[0]
        for d in range(1, N_DEV):
            acc = acc + g_ref[d]
        out_ref[...] = acc

    return pl.pallas_call(
        body,
        name="sum_devices",
        in_specs=[VMEM_SPEC],
        out_specs=VMEM_SPEC,
        out_shape=jax.ShapeDtypeStruct(gathered.shape[1:], F32),
    )(gathered)


def _adamw(w, g, m, v):
    r, rest = w.shape[0], w.shape[1:]
    per_row = 1
    for dim in rest:
        per_row *= dim
    tr = _tile(r, max(8, (5 << 19) // (4 * per_row)), 8 if len(rest) == 1 else 1)

    def body(w_ref, g_ref, m_ref, v_ref, d_ref, mo_ref, vo_ref):
        gg = g_ref[...]
        mm = ADAM_B1 * m_ref[...] + (1.0 - ADAM_B1) * gg
        vv = ADAM_B2 * v_ref[...] + (1.0 - ADAM_B2) * (gg * gg)
        m_hat = mm / (1.0 - ADAM_B1 ** ADAM_STEP)
        v_hat = vv / (1.0 - ADAM_B2 ** ADAM_STEP)
        d_ref[...] = -ADAM_LR * (m_hat / (jnp.sqrt(v_hat) + ADAM_EPS) + ADAM_WD * w_ref[...])
        mo_ref[...] = mm
        vo_ref[...] = vv

    spec = pl.BlockSpec((tr,) + rest, lambda i: (i,) + (0,) * len(rest))
    return pl.pallas_call(
        body,
        name="adamw",
        grid=(r // tr,),
        in_specs=[spec] * 4,
        out_specs=[spec] * 3,
        out_shape=[jax.ShapeDtypeStruct(w.shape, F32)] * 3,
        compiler_params=_cparams(("parallel",)),
    )(w, g, m, v)


def _adamw_halves(w, own, other, m, v, c, name):
    r, cols = w.shape
    half = r // 2
    tr = _tile(half, 256, 8)
    nt = half // tr
    whole = pl.BlockSpec((tr, cols), lambda h, i, c_ref: (h * nt + i, 0))
    part = pl.BlockSpec((tr, cols), lambda h, i, c_ref: (i, 0))

    def body(c_ref, w_ref, own_ref, other_ref, m_ref, v_ref, g_ref, d_ref, mo_ref, vo_ref):
        gg = jnp.where(pl.program_id(0) == c_ref[0], own_ref[...], other_ref[...])
        g_ref[...] = gg
        mm = ADAM_B1 * m_ref[...] + (1.0 - ADAM_B1) * gg
        vv = ADAM_B2 * v_ref[...] + (1.0 - ADAM_B2) * (gg * gg)
        m_hat = mm / (1.0 - ADAM_B1 ** ADAM_STEP)
        v_hat = vv / (1.0 - ADAM_B2 ** ADAM_STEP)
        d_ref[...] = -ADAM_LR * (m_hat / (jnp.sqrt(v_hat) + ADAM_EPS) + ADAM_WD * w_ref[...])
        mo_ref[...] = mm
        vo_ref[...] = vv

    return pl.pallas_call(
        body,
        name="adamw_" + name,
        grid_spec=pltpu.PrefetchScalarGridSpec(
            num_scalar_prefetch=1, grid=(2, nt),
            in_specs=[whole, part, part, whole, whole], out_specs=[whole] * 4),
        out_shape=[jax.ShapeDtypeStruct((r, cols), F32)] * 4,
        compiler_params=_cparams(("parallel", "parallel")),
    )(c, w, own, other, m, v)


GATHER_FIRST = ("ffn1_w_in", "ffn1_w_out")
GATHER_PROJ = ("w_in",)
GATHER_LATE = ("w_branch_a", "w_branch_b", "w_out", "ffn2_w_in", "ffn2_w_out")


class _Comm:
    def __init__(self, shards, c_arr, mine_arr):
        self.shards, self.c, self.mine = shards, c_arr, mine_arr
        self.groups, self.halves = {}, {}
        self.by_name = {entry[0]: entry for entry in BIG}

    def small_gather(self, loss, small):
        pad_lanes = lambda a: jnp.concatenate([a, jnp.zeros((1, LANES - a.shape[1]), F32)], axis=1)
        buf = jnp.concatenate([
            small["meta_tokens"].reshape(128, LANES),
            small["ffn1_norm"].reshape(8, LANES), small["mix_norm"].reshape(8, LANES),
            small["ffn2_norm"].reshape(8, LANES), small["final_norm"].reshape(8, LANES),
            loss, pad_lanes(small["b_forget"]), pad_lanes(small["attn_sinks"]),
            jnp.zeros((SMALL_ROWS - 163, LANES), F32)], axis=0)
        return _small_exchange(buf)

    def small_gathered(self, outs):
        self.reduced = _sum_devices(outs[0])

    def gather(self, names):
        table = [self.by_name[n] + (BF16, True) for n in names]
        return _gather_exchange([self.shards[n] for n in names], table)

    def gathered(self, names, outs):
        return [o.transpose(1, 0, 2).reshape(D_MODEL, W_IN_COLS) if n == STACKED else o for n, o in zip(names, outs)]

    def swap(self, tag, grads):
        entries = [self.by_name[n] for n in grads]
        arrays = [g.reshape(D_MODEL, N_CHIPS, W_IN_COLS // N_CHIPS).transpose(1, 0, 2) if n == STACKED else g
                  for n, g in grads.items()]
        self.groups[tag] = (entries, arrays)
        return _halves_exchange(arrays, entries)

    def scatter(self, tag, received):
        entries, arrays = self.groups[tag]
        views = [_halves_view(*entry) for entry in entries]
        partials = [_add_sibling(g.reshape(v[0]), r, self.c, name)
                    for g, v, r, (name, _, _) in zip(arrays, views, received, entries)]
        self.groups[tag] = (entries, partials)
        return _scatter_exchange(partials, entries)

    def received(self, tag, pieces):
        entries, partials = self.groups[tag]
        for p, r, (name, _, axis) in zip(partials, pieces, entries):
            self.halves[name] = _add_chips(p, r, self.mine, name, axis)

    def finish(self):
        names = [n for n, _, _ in BIG]
        own = [self.halves[n] for n in names]
        return dict(zip(names, zip(own, _share_with_sibling(own))))


def kernel(x, meta_tokens, ffn1_norm, ffn1_w_in, ffn1_w_out, mix_norm, w_in, b_forget, attn_sinks, w_branch_a, w_branch_b, w_out, ffn2_norm, ffn2_w_in, ffn2_w_out, final_norm, loss_target, m_meta_tokens, m_ffn1_norm, m_ffn1_w_in, m_ffn1_w_out, m_mix_norm, m_w_in, m_b_forget, m_attn_sinks, m_w_branch_a, m_w_branch_b, m_w_out, m_ffn2_norm, m_ffn2_w_in, m_ffn2_w_out, m_final_norm, v_meta_tokens, v_ffn1_norm, v_ffn1_w_in, v_ffn1_w_out, v_mix_norm, v_w_in, v_b_forget, v_attn_sinks, v_w_branch_a, v_w_branch_b, v_w_out, v_ffn2_norm, v_ffn2_w_in, v_ffn2_w_out, v_final_norm):
    given = dict(locals())
    names = ["meta_tokens", "ffn1_norm", "ffn1_w_in", "ffn1_w_out", "mix_norm", "w_in", "b_forget", "attn_sinks",
             "w_branch_a", "w_branch_b", "w_out", "ffn2_norm", "ffn2_w_in", "ffn2_w_out", "final_norm"]
    big_names = [n for n, _, _ in BIG]
    cx, cy, cc = _coords()
    c_arr = cc.reshape(1).astype(jnp.int32)
    mine_arr = (2 * cx + cy).reshape(1).astype(jnp.int32)

    comm = _Comm({n: given[n][0].astype(BF16) for n in big_names}, c_arr, mine_arr)
    table = [comm.by_name[n] + (BF16, True) for n in GATHER_FIRST] + [("meta_tokens", (N_META, D_MODEL), 1, F32, False)]
    first = _gather_exchange([comm.shards[n] for n in GATHER_FIRST] + [meta_tokens], table)
    w1i, w1o, meta_full = _run_exchange(first, "gather_first")
    norms = (ffn1_norm, mix_norm, ffn2_norm, final_norm.reshape(1, D_MODEL))
    loss, grad_x, small, big = _local_step(x, loss_target, meta_full, norms, b_forget, attn_sinks, (w1i, w1o), comm)

    swap = comm.swap("ffn1_in", dict(ffn1_w_in=big["ffn1_w_in"]))
    last = comm.scatter("ffn1_in", _run_exchange(swap, "exchange_halves_ffn1_in"))
    comm.received("ffn1_in", _run_exchange(last, "scatter_chip_sums"))
    grad_halves = comm.finish()
    grads = {}

    red = comm.reduced
    meta_cols = red[:128].reshape(N_META, D_MODEL)
    grads["meta_tokens"] = lax.dynamic_slice_in_dim(meta_cols, (2 * cx + cy) * (D_MODEL // N_CHIPS),
                                                    D_MODEL // N_CHIPS, axis=1)
    grads["ffn1_norm"] = red[128:136].reshape(1, D_MODEL)
    grads["mix_norm"] = red[136:144].reshape(1, D_MODEL)
    grads["ffn2_norm"] = red[144:152].reshape(1, D_MODEL)
    grads["final_norm"] = red[152:160].reshape(1, D_MODEL)
    loss_out = red[160, 0]
    grads["b_forget"] = red[161:162, :B_HEADS]
    grads["attn_sinks"] = red[162:163, :A_HEADS]

    out_g, out_d, out_m, out_v = [], [], [], []
    for n in names:
        w_full = given[n]
        shape = w_full.shape
        two_d = (lambda a: a.reshape(shape[-2], shape[-1])) if len(shape) >= 2 else (lambda a: a.reshape(1, shape[0]))
        if n == STACKED:
            own, other = grad_halves[n]
            g_t = jnp.concatenate([jnp.where(cc == 0, own, other), jnp.where(cc == 0, other, own)], axis=0).T
            tiles = lambda a: a.reshape(shape[-1], shape[-2] // LANES, LANES)
            untile = lambda a: a.reshape(shape[-1], shape[-2]).T
            d2, m2, v2 = [untile(a) for a in _adamw(tiles(two_d(w_full).T), tiles(g_t), tiles(two_d(given["m_" + n]).T),
                                                     tiles(two_d(given["v_" + n]).T))]
            g2 = g_t.T
        elif n in grad_halves:
            own, other = grad_halves[n]
            g2, d2, m2, v2 = _adamw_halves(two_d(w_full), own, other, two_d(given["m_" + n]),
                                           two_d(given["v_" + n]), c_arr, n)
        else:
            g2 = two_d(grads[n])
            d2, m2, v2 = _adamw(two_d(w_full), g2, two_d(given["m_" + n]), two_d(given["v_" + n]))
        out_g.append(g2.reshape(shape))
        out_d.append(d2.reshape(shape))
        out_m.append(m2.reshape(shape))
        out_v.append(v2.reshape(shape))
    return (loss_out, grad_x, *out_g, *out_d, *out_m, *out_v)
```

```python
import jax
import jax.numpy as jnp
from jax import lax
from jax.experimental import pallas as pl
from jax.experimental.pallas import tpu as pltpu

F32 = jnp.float32
BF16 = jnp.bfloat16

D_MODEL = 1024
N_META = 16
BLOCK = 128
LANES = 128
PREFIX = BLOCK
N_PAD = PREFIX - N_META
HEAD_DIM = 64
A_HEADS = 8
A_KV_HEADS = 2
A_GROUP = 4
B_HEADS = 8
B_PAIRS = B_HEADS // 2
A_WIDTH = A_HEADS * HEAD_DIM
A_KV_WIDTH = A_KV_HEADS * HEAD_DIM
B_WIDTH = B_HEADS * HEAD_DIM
W_IN_COLS = A_WIDTH + 2 * A_KV_WIDTH + 3 * B_WIDTH + B_HEADS + 2 * D_MODEL
SRC_KA = A_WIDTH
SRC_VA = SRC_KA + A_KV_WIDTH
SRC_QB = SRC_VA + A_KV_WIDTH
SRC_KB = SRC_QB + B_WIDTH
SRC_VB = SRC_KB + B_WIDTH
SRC_F = SRC_VB + B_WIDTH
SRC_GA = SRC_F + B_HEADS
SRC_GB = SRC_GA + D_MODEL
A_PAD_WIDTH = A_HEADS * LANES
B_PAD_WIDTH = B_HEADS * LANES
F_COLS = LANES
OFF_QA = 0
OFF_KA = SRC_KA
OFF_VA = SRC_VA
OFF_QB = SRC_QB
OFF_KB = SRC_KB
OFF_VB = SRC_VB
OFF_GA = SRC_F
OFF_GB = OFF_GA + D_MODEL
OFF_F = OFF_GB + D_MODEL
P_COLS = OFF_F + F_COLS
EPS = 1e-6
NEG = -1e30
SCALE = HEAD_DIM ** -0.5
KEY_BLOCKS = 4

ADAM_LR = 0.001
ADAM_B1 = 0.9
ADAM_B2 = 0.999
ADAM_EPS = 1e-08
ADAM_WD = 0.01
ADAM_STEP = 10

N_CHIPS = 4
N_DEV = 8
VMEM_LIMIT = 56 * 1024 * 1024

NT_DIMS = (((1,), (1,)), ((), ()))
TN_DIMS = (((0,), (0,)), ((), ()))
MESH_ID = pl.DeviceIdType.MESH
HBM_SPEC = pl.BlockSpec(memory_space=pltpu.HBM)
VMEM_SPEC = pl.BlockSpec(memory_space=pltpu.VMEM)


def _tile(n, target, mult=16):
    best = None
    for t in range(mult, min(n, target) + 1, mult):
        if n % t == 0:
            best = t
    return best if best is not None else n


def _cparams(sem):
    return pltpu.CompilerParams(dimension_semantics=sem, vmem_limit_bytes=VMEM_LIMIT)


def _rms_scale(h):
    return lax.rsqrt(jnp.mean(h * h, axis=-1, keepdims=True) + EPS)


def _rms_bwd(dn, h, w):
    r = _rms_scale(h)
    dw = jnp.sum(dn * (h * r), axis=0, keepdims=True)
    z = dn * w
    dh = r * z - h * ((r * r * r) * jnp.mean(z * h, axis=-1, keepdims=True))
    return dh, dw


def _ffn_fwd(h, norm_w, w_in, w_out, exchange=None):
    t, d = h.shape
    f = w_out.shape[0]
    tm = _tile(t, 272)
    tc = _tile(f, 256, 128)
    nj = f // tc
    ni = t // tm
    n_x = len(exchange.ins) if exchange else 0

    def body(*refs):
        h_ref, nw_ref, wi_ref, wo_ref = refs[:4]
        hout_ref, g_ref, u_ref = refs[4 + n_x:7 + n_x]
        a_scr = refs[7 + 2 * n_x]
        i = pl.program_id(0)
        if exchange:
            _host_exchange(exchange, refs[4:4 + n_x], refs[7 + n_x:7 + 2 * n_x], refs[8 + 2 * n_x:],
                           i == 0, i == ni // 3, i == ni - 1, late=i == 2 * ni // 3)
        hh = h_ref[...]
        n = ((hh * _rms_scale(hh)) * nw_ref[...]).astype(BF16)
        for j in range(nj):
            cols = slice(j * tc, (j + 1) * tc)
            g = jnp.dot(n, wi_ref[:, j * tc:(j + 1) * tc], preferred_element_type=F32)
            u = jnp.dot(n, wi_ref[:, f + j * tc:f + (j + 1) * tc], preferred_element_type=F32)
            g_ref[:, cols] = g
            u_ref[:, cols] = u
            a_scr[:, cols] = ((g * jax.nn.sigmoid(g)) * u).astype(BF16)
        hout_ref[...] = hh + 0.5 * jnp.dot(a_scr[...], wo_ref[...], preferred_element_type=F32)

    resident = lambda a: pl.BlockSpec(a.shape, lambda i: (0, 0), pipeline_mode=pl.Buffered(1))
    row = lambda w: pl.BlockSpec((tm, w), lambda i: (i, 0))
    outs = pl.pallas_call(
        body,
        name="ffn_fwd",
        grid=(ni,),
        in_specs=[row(d), pl.BlockSpec((1, d), lambda i: (0, 0)), resident(w_in), resident(w_out)] + [HBM_SPEC] * n_x,
        out_specs=[row(d), row(f), row(f)] + [HBM_SPEC] * n_x,
        out_shape=[
            jax.ShapeDtypeStruct((t, d), F32),
            jax.ShapeDtypeStruct((t, f), F32),
            jax.ShapeDtypeStruct((t, f), F32),
        ] + (exchange.out_shape if exchange else []),
        scratch_shapes=[pltpu.VMEM((tm, f), BF16)] + (exchange.scratch if exchange else []),
        compiler_params=_cparams(("arbitrary",) if exchange else ("parallel",)),
    )(h, norm_w, w_in, w_out, *(exchange.ins if exchange else []))
    return outs[:3], outs[3:]


def _ffn_bwd(dh_out, h, norm_w, g, u, w_in, w_out, exchange=None):
    t, d = h.shape
    f = w_out.shape[0]
    tm = _tile(t, 272)
    tc = _tile(f, 256, 128)
    nj = f // tc
    ni = t // tm
    n_x = len(exchange.ins) if exchange else 0

    def body(*refs):
        dho_ref, h_ref, nw_ref, g_ref, u_ref, wi_ref, wo_ref = refs[:7]
        dhin_ref, n_ref, a_ref, dgu_ref, df_ref, dnw_ref = refs[7 + n_x:13 + n_x]
        i = pl.program_id(0)
        if exchange:
            _host_exchange(exchange, refs[7:7 + n_x], refs[13 + n_x:13 + 2 * n_x], refs[13 + 2 * n_x:],
                           i == 0, i == ni - 1, i == ni - 1)
        hh = h_ref[...]
        nw = nw_ref[...]
        n_ref[...] = ((hh * _rms_scale(hh)) * nw).astype(BF16)
        dho = dho_ref[...]
        df = (0.5 * dho).astype(BF16)
        df_ref[...] = df
        for j in range(nj):
            cols = slice(j * tc, (j + 1) * tc)
            da = lax.dot_general(df, wo_ref[cols, :], NT_DIMS, preferred_element_type=F32)
            gg = g_ref[:, cols]
            uu = u_ref[:, cols]
            sig = jax.nn.sigmoid(gg)
            sl = gg * sig
            a_ref[:, cols] = (sl * uu).astype(BF16)
            dgu_ref[0, :, cols] = ((da * uu) * (sig * (1.0 + gg * (1.0 - sig)))).astype(BF16)
            dgu_ref[1, :, cols] = (da * sl).astype(BF16)
        dn = (lax.dot_general(dgu_ref[0], wi_ref[:, :f], NT_DIMS, preferred_element_type=F32)
              + lax.dot_general(dgu_ref[1], wi_ref[:, f:], NT_DIMS, preferred_element_type=F32))
        dh, dw = _rms_bwd(dn, hh, nw)
        dhin_ref[...] = dho + dh
        dnw_ref[0] = dw

    resident = lambda a: pl.BlockSpec(a.shape, lambda i: (0, 0), pipeline_mode=pl.Buffered(1))
    row = lambda w: pl.BlockSpec((tm, w), lambda i: (i, 0))
    outs = pl.pallas_call(
        body,
        name="ffn_bwd",
        grid=(ni,),
        in_specs=[row(d), row(d), pl.BlockSpec((1, d), lambda i: (0, 0)), row(f), row(f),
                  resident(w_in), resident(w_out)] + [HBM_SPEC] * n_x,
        out_specs=[row(d), row(d), row(f), pl.BlockSpec((2, tm, f), lambda i: (0, i, 0)), row(d),
                   pl.BlockSpec((1, 1, d), lambda i: (i, 0, 0))] + [HBM_SPEC] * n_x,
        out_shape=[
            jax.ShapeDtypeStruct((t, d), F32),
            jax.ShapeDtypeStruct((t, d), BF16),
            jax.ShapeDtypeStruct((t, f), BF16),
            jax.ShapeDtypeStruct((2, t, f), BF16),
            jax.ShapeDtypeStruct((t, d), BF16),
            jax.ShapeDtypeStruct((ni, 1, d), F32),
        ] + (exchange.out_shape if exchange else []),
        scratch_shapes=exchange.scratch if exchange else [],
        compiler_params=_cparams(("arbitrary",) if exchange else ("parallel",)),
    )(dh_out, h, norm_w, g, u, w_in, w_out, *(exchange.ins if exchange else []))
    return outs[:6], outs[6:]


def _tn_matmul(a, b, name, exchange=None):
    t, k = a.shape
    split = b.ndim == 3
    n = 2 * b.shape[2] if split else b.shape[1]
    tk = _tile(k, 512, 128)
    tn = _tile(b.shape[-1], 1408, 128)
    per_half = b.shape[-1] // tn
    ni, nj = k // tk, n // tn
    n_x = len(exchange.ins) if exchange else 0

    def body(*refs):
        a_ref, b_ref, o_ref = refs[0], refs[1], refs[2 + n_x]
        if exchange:
            i, j = pl.program_id(0), pl.program_id(1)
            at_end = (i == ni - 1) & (j == nj - 1)
            _host_exchange(exchange, refs[2:2 + n_x], refs[3 + n_x:3 + 2 * n_x], refs[3 + 2 * n_x:],
                           (i == 0) & (j == 0), at_end, at_end)
        o_ref[...] = lax.dot_general(a_ref[...], b_ref[...], TN_DIMS, preferred_element_type=F32)

    if split:
        b_spec = pl.BlockSpec((None, t, tn), lambda i, j: (j // per_half, 0, j % per_half))
    else:
        b_spec = pl.BlockSpec((t, tn), lambda i, j: (0, j))
    outs = pl.pallas_call(
        body,
        name=name,
        grid=(ni, nj),
        in_specs=[pl.BlockSpec((t, tk), lambda i, j: (0, i)), b_spec] + [HBM_SPEC] * n_x,
        out_specs=[pl.BlockSpec((tk, tn), lambda i, j: (i, j))] + [HBM_SPEC] * n_x,
        out_shape=[jax.ShapeDtypeStruct((k, n), F32)] + (exchange.out_shape if exchange else []),
        scratch_shapes=exchange.scratch if exchange else [],
        compiler_params=_cparams(("arbitrary", "arbitrary") if exchange else ("parallel", "parallel")),
    )(a, b, *(exchange.ins if exchange else []))
    return (outs[0], outs[1:]) if exchange else outs[0]


A_SLOT = lambda h: h // A_GROUP
B_SLOT = lambda h: h % 2

PROJ_PARTS = (
    (OFF_QA, A_WIDTH, A_PAD_WIDTH, True, A_SLOT), (OFF_KA, A_KV_WIDTH, A_KV_WIDTH, True, None),
    (OFF_VA, A_KV_WIDTH, A_KV_WIDTH, True, None), (OFF_QB, B_WIDTH, B_WIDTH, True, None),
    (OFF_KB, B_WIDTH, B_PAD_WIDTH, True, B_SLOT), (OFF_VB, B_WIDTH, B_PAD_WIDTH, True, B_SLOT),
    (OFF_GA, D_MODEL, D_MODEL, False, None), (OFF_GB, D_MODEL, D_MODEL, False, None), (OFF_F, F_COLS, F_COLS, False, None),
)


def _head_tile(pair, head, slot):
    lane_slot = lax.broadcasted_iota(jnp.int32, pair.shape, 1) // HEAD_DIM
    moved = pair if head % 2 == slot else pltpu.roll(pair, HEAD_DIM, 1)
    return jnp.where(lane_slot == slot, moved, 0.0)


def _proj_fwd(h, norm_w, w_p):
    t, d = h.shape
    tm = _tile(t, 272)

    def body(h_ref, nw_ref, w_ref, u_ref, *part_refs):
        hh = h_ref[...]
        un = ((hh * _rms_scale(hh)) * nw_ref[...]).astype(BF16)
        u_ref[...] = un
        for (off, width, _, _, slot), p_ref in zip(PROJ_PARTS, part_refs):
            if slot is None:
                p_ref[...] = jnp.dot(un, w_ref[:, off:off + width], preferred_element_type=F32).astype(p_ref.dtype)
                continue
            for pair in range(width // LANES):
                x = jnp.dot(un, w_ref[:, off + pair * LANES:off + (pair + 1) * LANES], preferred_element_type=F32)
                for head in (2 * pair, 2 * pair + 1):
                    p_ref[:, head * LANES:(head + 1) * LANES] = _head_tile(x, head, slot(head)).astype(p_ref.dtype)

    row = lambda w: pl.BlockSpec((tm, w), lambda i: (i, 0))
    return pl.pallas_call(
        body,
        name="proj_fwd",
        grid=(t // tm,),
        in_specs=[row(d), pl.BlockSpec((1, d), lambda i: (0, 0)),
                  pl.BlockSpec(w_p.shape, lambda i: (0, 0), pipeline_mode=pl.Buffered(1))],
        out_specs=[row(d)] + [row(width) for _, _, width, _, _ in PROJ_PARTS],
        out_shape=[jax.ShapeDtypeStruct((t, d), BF16)]
        + [jax.ShapeDtypeStruct((t, width), BF16 if is_bf else F32) for _, _, width, is_bf, _ in PROJ_PARTS],
        compiler_params=_cparams(("parallel",)),
    )(h, norm_w, w_p)


def _proj_bwd(dh_out, h, norm_w, dproj, w_p, exchange=None):
    t, d = h.shape
    n = w_p.shape[1]
    tm = _tile(t, 272)
    ni = t // tm
    n_x = len(exchange.ins) if exchange else 0

    def body(*refs):
        dho_ref, h_ref, nw_ref, dp_ref, w_ref = refs[:5]
        dhin_ref, dnw_ref = refs[5 + n_x:7 + n_x]
        if exchange:
            i = pl.program_id(0)
            _host_exchange(exchange, refs[5:5 + n_x], refs[7 + n_x:7 + 2 * n_x], refs[7 + 2 * n_x:],
                           i == 0, i == ni - 1, i == ni - 1)
        dn = lax.dot_general(dp_ref[...], w_ref[...], NT_DIMS, preferred_element_type=F32)
        dh, dw = _rms_bwd(dn, h_ref[...], nw_ref[...])
        dhin_ref[...] = dho_ref[...] + dh
        dnw_ref[0] = dw

    row = lambda w: pl.BlockSpec((tm, w), lambda i: (i, 0))
    outs = pl.pallas_call(
        body,
        name="proj_bwd",
        grid=(ni,),
        in_specs=[row(d), row(d), pl.BlockSpec((1, d), lambda i: (0, 0)), row(n),
                  pl.BlockSpec(w_p.shape, lambda i: (0, 0), pipeline_mode=pl.Buffered(1))] + [HBM_SPEC] * n_x,
        out_specs=[row(d), pl.BlockSpec((1, 1, d), lambda i: (i, 0, 0))] + [HBM_SPEC] * n_x,
        out_shape=[jax.ShapeDtypeStruct((t, d), F32), jax.ShapeDtypeStruct((ni, 1, d), F32)]
        + (exchange.out_shape if exchange else []),
        scratch_shapes=exchange.scratch if exchange else [],
        compiler_params=_cparams(("arbitrary",) if exchange else ("parallel",)),
    )(dh_out, h, norm_w, dproj, w_p, *(exchange.ins if exchange else []))
    return outs[:2], outs[2:]


def _merge_fwd(h, oa, ob, ga, gb, wa, wb, wo):
    t, d = h.shape
    tm = _tile(t, 544)

    def body(h_ref, oa_ref, ob_ref, ga_ref, gb_ref, wa_ref, wb_ref, wo_ref, hout_ref, mix_ref):
        ya = jnp.dot(oa_ref[...], wa_ref[...], preferred_element_type=F32)
        yb = jnp.dot(ob_ref[...], wb_ref[...], preferred_element_type=F32)
        mixed = (jax.nn.sigmoid(ga_ref[...]) * ya + jax.nn.sigmoid(gb_ref[...]) * yb).astype(BF16)
        mix_ref[...] = mixed
        hout_ref[...] = h_ref[...] + jnp.dot(mixed, wo_ref[...], preferred_element_type=F32)

    row = lambda w: pl.BlockSpec((tm, w), lambda i: (i, 0))
    full = lambda a: pl.BlockSpec(a.shape, lambda i: (0, 0))
    return pl.pallas_call(
        body,
        name="merge_fwd",
        grid=(t // tm,),
        in_specs=[row(d), row(oa.shape[1]), row(ob.shape[1]), row(d), row(d), full(wa), full(wb), full(wo)],
        out_specs=[row(d), row(d)],
        out_shape=[jax.ShapeDtypeStruct((t, d), F32), jax.ShapeDtypeStruct((t, d), BF16)],
        compiler_params=_cparams(("parallel",)),
    )(h, oa, ob, ga, gb, wa, wb, wo)


def _merge_bwd(dh, oa, ob, ga, gb, wa, wb, wo, exchange=None):
    t, d = dh.shape
    tm = _tile(t, 544)
    ni = t // tm
    n_x = len(exchange.ins) if exchange else 0

    def body(*refs):
        dh_ref, oa_ref, ob_ref, ga_ref, gb_ref, wa_ref, wb_ref, wo_ref = refs[:8]
        dya_ref, dyb_ref, doa_ref, dob_ref, dga_ref, dgb_ref, dhb_ref = refs[8 + n_x:15 + n_x]
        if exchange:
            i = pl.program_id(0)
            _host_exchange(exchange, refs[8:8 + n_x], refs[15 + n_x:15 + 2 * n_x], refs[15 + 2 * n_x:],
                           i == 0, i == ni - 1, i == ni - 1)
        dhb = dh_ref[...].astype(BF16)
        dhb_ref[...] = dhb
        dmix = lax.dot_general(dhb, wo_ref[...], NT_DIMS, preferred_element_type=F32)
        for o_ref, g_ref, w_ref, dy_ref, do_ref, dg_ref in (
                (oa_ref, ga_ref, wa_ref, dya_ref, doa_ref, dga_ref),
                (ob_ref, gb_ref, wb_ref, dyb_ref, dob_ref, dgb_ref)):
            y = jnp.dot(o_ref[...], w_ref[...], preferred_element_type=F32)
            s = jax.nn.sigmoid(g_ref[...])
            dy = (dmix * s).astype(BF16)
            dy_ref[...] = dy
            dg_ref[...] = ((dmix * y) * (s * (1.0 - s))).astype(BF16)
            do_ref[...] = lax.dot_general(dy, w_ref[...], NT_DIMS, preferred_element_type=F32).astype(BF16)

    row = lambda w: pl.BlockSpec((tm, w), lambda i: (i, 0))
    full = lambda a: pl.BlockSpec(a.shape, lambda i: (0, 0))
    wa_w, wb_w = oa.shape[1], ob.shape[1]
    outs = pl.pallas_call(
        body,
        name="merge_bwd",
        grid=(ni,),
        in_specs=[row(d), row(wa_w), row(wb_w), row(d), row(d), full(wa), full(wb), full(wo)] + [HBM_SPEC] * n_x,
        out_specs=[row(d), row(d), row(wa_w), row(wb_w), row(d), row(d), row(d)] + [HBM_SPEC] * n_x,
        out_shape=[
            jax.ShapeDtypeStruct((t, d), BF16), jax.ShapeDtypeStruct((t, d), BF16),
            jax.ShapeDtypeStruct((t, wa_w), BF16), jax.ShapeDtypeStruct((t, wb_w), BF16),
            jax.ShapeDtypeStruct((t, d), BF16), jax.ShapeDtypeStruct((t, d), BF16),
            jax.ShapeDtypeStruct((t, d), BF16),
        ] + (exchange.out_shape if exchange else []),
        scratch_shapes=exchange.scratch if exchange else [],
        compiler_params=_cparams(("arbitrary",) if exchange else ("parallel",)),
    )(dh, oa, ob, ga, gb, wa, wb, wo, *(exchange.ins if exchange else []))
    return outs[:7], outs[7:]


def _tri_dot(tri, x):
    hi = x.astype(BF16)
    r1 = x - hi.astype(F32)
    mid = r1.astype(BF16)
    lo = (r1 - mid.astype(F32)).astype(BF16)
    return (jnp.dot(tri, hi, preferred_element_type=F32)
            + jnp.dot(tri, mid, preferred_element_type=F32)
            + jnp.dot(tri, lo, preferred_element_type=F32))


def _forget_cumsum(f_logit, b_pad, nb):
    t, w = f_logit.shape
    bsz = t // (nb * BLOCK)

    def body(f_ref, b_ref, c_ref, carry):
        n = pl.program_id(1)

        @pl.when(n == 0)
        def _():
            carry[...] = jnp.zeros_like(carry)

        x = jax.nn.log_sigmoid(f_ref[...] + b_ref[...])
        rows = lax.broadcasted_iota(jnp.int32, (BLOCK, BLOCK), 0)
        cols = lax.broadcasted_iota(jnp.int32, (BLOCK, BLOCK), 1)
        tri = (cols <= rows).astype(BF16)
        c = _tri_dot(tri, x) + carry[...]
        c_ref[...] = c
        carry[...] = c[BLOCK - 1:BLOCK, :]

    return pl.pallas_call(
        body,
        name="forget_cumsum",
        grid=(bsz, nb),
        in_specs=[pl.BlockSpec((BLOCK, w), lambda b, n: (b * nb + n, 0)),
                  pl.BlockSpec((1, w), lambda b, n: (0, 0))],
        out_specs=pl.BlockSpec((BLOCK, w), lambda b, n: (b * nb + n, 0)),
        out_shape=jax.ShapeDtypeStruct((t, w), F32),
        scratch_shapes=[pltpu.VMEM((1, w), F32)],
        compiler_params=_cparams(("parallel", "arbitrary")),
    )(f_logit, b_pad)


def _forget_cumsum_bwd(dc, f_logit, b_pad, nb):
    t, w = f_logit.shape
    bsz = t // (nb * BLOCK)

    def body(dc_ref, f_ref, b_ref, df_ref, db_ref, carry):
        n = pl.program_id(1)

        @pl.when(n == 0)
        def _():
            carry[...] = jnp.zeros_like(carry)
            db_ref[...] = jnp.zeros_like(db_ref)

        rows = lax.broadcasted_iota(jnp.int32, (BLOCK, BLOCK), 0)
        cols = lax.broadcasted_iota(jnp.int32, (BLOCK, BLOCK), 1)
        tri = (cols >= rows).astype(BF16)
        dlf = _tri_dot(tri, dc_ref[...]) + carry[...]
        carry[...] = dlf[0:1, :]
        df = dlf * jax.nn.sigmoid(-(f_ref[...] + b_ref[...]))
        df_ref[...] = df.astype(BF16)
        db_ref[0] += jnp.sum(df, axis=0, keepdims=True)

    rev = lambda b, n: (b * nb + (nb - 1 - n), 0)
    return pl.pallas_call(
        body,
        name="forget_cumsum_bwd",
        grid=(bsz, nb),
        in_specs=[pl.BlockSpec((BLOCK, w), rev),
                  pl.BlockSpec((BLOCK, w), rev),
                  pl.BlockSpec((1, w), lambda b, n: (0, 0))],
        out_specs=[pl.BlockSpec((BLOCK, w), rev),
                   pl.BlockSpec((1, 1, w), lambda b, n: (b, 0, 0))],
        out_shape=[jax.ShapeDtypeStruct((t, w), BF16), jax.ShapeDtypeStruct((bsz, 1, w), F32)],
        scratch_shapes=[pltpu.VMEM((1, w), F32)],
        compiler_params=_cparams(("parallel", "arbitrary")),
    )(dc, f_logit, b_pad)


GROUP_ROWS = A_GROUP * BLOCK


def _stack_heads(ref, g):
    return jnp.concatenate([ref[:, (A_GROUP * g + i) * LANES:(A_GROUP * g + i + 1) * LANES] for i in range(A_GROUP)],
                           axis=0)


def _unstack_heads(ref, g, x):
    for i in range(A_GROUP):
        ref[:, (A_GROUP * g + i) * LANES:(A_GROUP * g + i + 1) * LANES] = x[i * BLOCK:(i + 1) * BLOCK].astype(ref.dtype)


def _swa_logits(q, keys, slope, n):
    qi = lax.broadcasted_iota(jnp.int32, (GROUP_ROWS, BLOCK), 0) & (BLOCK - 1)
    kj = lax.broadcasted_iota(jnp.int32, (GROUP_ROWS, BLOCK), 1)
    s_all = lax.dot_general(q, keys, NT_DIMS, preferred_element_type=F32) * SCALE
    out = []
    for i, (dist, ok) in enumerate((
            (n * BLOCK + qi - kj, (kj >= N_PAD) & (n * BLOCK + qi - kj >= 0)),
            (BLOCK + qi - kj, (kj > qi) & (n >= 2)),
            (qi - kj, (kj <= qi) & (n >= 1)))):
        s = s_all[:, i * BLOCK:(i + 1) * BLOCK] - slope * dist.astype(F32)
        out.append(jnp.where(ok, s, NEG))
    return out


def _three_blocks(m_ref, p_ref, c_ref):
    return jnp.concatenate([m_ref[...], p_ref[...], c_ref[...]], axis=0)


def _swa_specs(bsz, nb):
    qspec = pl.BlockSpec((bsz, BLOCK, A_PAD_WIDTH), lambda n: (0, n, 0))
    kv_m = pl.BlockSpec((bsz, BLOCK, LANES), lambda n: (0, 0, 0))
    kv_p = pl.BlockSpec((bsz, BLOCK, LANES), lambda n: (0, jnp.maximum(n - 1, 0), 0))
    kv_c = pl.BlockSpec((bsz, BLOCK, LANES), lambda n: (0, n, 0))
    rowspec = pl.BlockSpec((A_KV_HEADS, GROUP_ROWS, 1), lambda n: (0, 0, 0))
    lsespec = pl.BlockSpec((bsz, 1, A_KV_HEADS, GROUP_ROWS, 1), lambda n: (0, n, 0, 0, 0))
    return qspec, kv_m, kv_p, kv_c, rowspec, lsespec


def _swa_fwd(q, k, v, sink_rows, slope_rows, nb):
    t = q.shape[0]
    l = nb * BLOCK
    bsz = t // l

    def body(q_ref, km_ref, kp_ref, kc_ref, vm_ref, vp_ref, vc_ref, sink_ref, slope_ref, o_ref, lse_ref):
        n = pl.program_id(0)
        lane_group = lax.broadcasted_iota(jnp.int32, (GROUP_ROWS, LANES), 1) // HEAD_DIM
        for b in range(bsz):
            keys = _three_blocks(km_ref.at[b], kp_ref.at[b], kc_ref.at[b])
            values = _three_blocks(vm_ref.at[b], vp_ref.at[b], vc_ref.at[b])
            for g in range(A_KV_HEADS):
                qq = _stack_heads(q_ref.at[b], g)
                sink = sink_ref[g]
                s_m, s_p, s_c = _swa_logits(qq, keys, slope_ref[g], n)
                m = jnp.maximum(jnp.max(jnp.maximum(jnp.maximum(s_m, s_p), s_c), axis=-1, keepdims=True), sink)
                m_wide = jnp.broadcast_to(m, (GROUP_ROWS, BLOCK))
                e_m = jnp.exp(s_m - m_wide)
                e_p = jnp.exp(s_p - m_wide)
                e_c = jnp.exp(s_c - m_wide)
                z = jnp.sum((e_m + e_p) + e_c, axis=-1, keepdims=True) + jnp.exp(sink - m)
                inv = jnp.broadcast_to(1.0 / z, (GROUP_ROWS, BLOCK))
                probs = jnp.concatenate([(e_m * inv).astype(BF16), (e_p * inv).astype(BF16),
                                         (e_c * inv).astype(BF16)], axis=1)
                o = jnp.dot(probs, values, preferred_element_type=F32)
                _unstack_heads(o_ref.at[b], g, jnp.where(lane_group == g, o, 0.0))
                lse_ref[b, 0, g] = m + jnp.log(z)

    qspec, kv_m, kv_p, kv_c, rowspec, lsespec = _swa_specs(bsz, nb)
    by_example = lambda a: a.reshape(bsz, l, a.shape[1])
    q3, k3, v3 = by_example(q), by_example(k), by_example(v)
    o, lse = pl.pallas_call(
        body,
        name="swa_fwd",
        grid=(nb,),
        in_specs=[qspec, kv_m, kv_p, kv_c, kv_m, kv_p, kv_c, rowspec, rowspec],
        out_specs=[qspec, lsespec],
        out_shape=[jax.ShapeDtypeStruct((bsz, l, A_PAD_WIDTH), BF16),
                   jax.ShapeDtypeStruct((bsz, nb, A_KV_HEADS, GROUP_ROWS, 1), F32)],
        compiler_params=_cparams(("arbitrary",)),
    )(q3, k3, k3, k3, v3, v3, v3, sink_rows, slope_rows)
    return o.reshape(t, A_PAD_WIDTH), lse


def _swa_bwd(q, k, v, do, lse, sink_rows, slope_rows, nb):
    t = q.shape[0]
    l = nb * BLOCK
    bsz = t // l

    def body(q_ref, km_ref, kp_ref, kc_ref, vm_ref, vp_ref, vc_ref, do_ref, lse_ref, sink_ref, slope_ref,
             dq_ref, dk_ref, dv_ref, dsink_ref, dk_acc, dv_acc):
        n = pl.program_id(0)

        @pl.when(n == 0)
        def _():
            dk_acc[...] = jnp.zeros_like(dk_acc)
            dv_acc[...] = jnp.zeros_like(dv_acc)
            dsink_ref[...] = jnp.zeros_like(dsink_ref)

        first_half = lax.broadcasted_iota(jnp.int32, (BLOCK, LANES), 1) < HEAD_DIM
        prev = jnp.maximum(n - 1, 0)
        for b in range(bsz):
            keys = _three_blocks(km_ref.at[b], kp_ref.at[b], kc_ref.at[b])
            values = _three_blocks(vm_ref.at[b], vp_ref.at[b], vc_ref.at[b])
            for g in range(A_KV_HEADS):
                qq = _stack_heads(q_ref.at[b], g)
                dob = _stack_heads(do_ref.at[b], g)
                lse_g = lse_ref[b, 0, g]
                lse_wide = jnp.broadcast_to(lse_g, (GROUP_ROWS, BLOCK))
                probs = [jnp.exp(s - lse_wide) for s in _swa_logits(qq, keys, slope_ref[g], n)]
                dp_all = lax.dot_general(dob, values, NT_DIMS, preferred_element_type=F32)
                dps = [dp_all[:, i * BLOCK:(i + 1) * BLOCK] for i in range(3)]
                delta = jnp.sum((probs[0] * dps[0] + probs[1] * dps[1]) + probs[2] * dps[2], axis=-1, keepdims=True)
                delta_wide = jnp.broadcast_to(delta, (GROUP_ROWS, BLOCK))
                ds = jnp.concatenate([(p * (dp - delta_wide)).astype(BF16) for p, dp in zip(probs, dps)], axis=1)
                pb = jnp.concatenate([p.astype(BF16) for p in probs], axis=1)
                dq = jnp.dot(ds, keys, preferred_element_type=F32) * SCALE
                dk_all = lax.dot_general(ds, qq, TN_DIMS, preferred_element_type=F32) * SCALE
                dv_all = lax.dot_general(pb, dob, TN_DIMS, preferred_element_type=F32)
                for i, start in enumerate((0, prev * BLOCK, n * BLOCK)):
                    rows = pl.ds(pl.multiple_of(start, BLOCK), BLOCK)
                    dk_acc[b, rows, :] += dk_all[i * BLOCK:(i + 1) * BLOCK]
                    dv_acc[b, rows, :] += dv_all[i * BLOCK:(i + 1) * BLOCK]
                for pair in range(A_GROUP // 2):
                    even = dq[2 * pair * BLOCK:(2 * pair + 1) * BLOCK]
                    odd = dq[(2 * pair + 1) * BLOCK:(2 * pair + 2) * BLOCK]
                    left = even if g == 0 else pltpu.roll(even, HEAD_DIM, 1)
                    right = pltpu.roll(odd, HEAD_DIM, 1) if g == 0 else odd
                    tile = (A_GROUP // 2) * g + pair
                    dq_ref[b, :, tile * LANES:(tile + 1) * LANES] = jnp.where(first_half, left, right).astype(BF16)
                dsink_ref[b, g] += -(jnp.exp(sink_ref[g] - lse_g) * delta)

        @pl.when(n == nb - 1)
        def _():
            dk_ref[...] = dk_acc[...].astype(BF16)
            dv_ref[...] = dv_acc[...].astype(BF16)

    qspec, kv_m, kv_p, kv_c, rowspec, lsespec = _swa_specs(bsz, nb)
    kv_all = pl.BlockSpec((bsz, l, LANES), lambda n: (0, 0, 0))
    by_example = lambda a: a.reshape(bsz, l, a.shape[1])
    q3, k3, v3 = by_example(q), by_example(k), by_example(v)
    dq, dk, dv, dsink = pl.pallas_call(
        body,
        name="swa_bwd",
        grid=(nb,),
        in_specs=[qspec, kv_m, kv_p, kv_c, kv_m, kv_p, kv_c, qspec, lsespec, rowspec, rowspec],
        out_specs=[pl.BlockSpec((bsz, BLOCK, A_WIDTH), lambda n: (0, n, 0)), kv_all, kv_all,
                   pl.BlockSpec((bsz, A_KV_HEADS, GROUP_ROWS, 1), lambda n: (0, 0, 0, 0))],
        out_shape=[jax.ShapeDtypeStruct((bsz, l, A_WIDTH), BF16),
                   jax.ShapeDtypeStruct((bsz, l, LANES), BF16),
                   jax.ShapeDtypeStruct((bsz, l, LANES), BF16),
                   jax.ShapeDtypeStruct((bsz, A_KV_HEADS, GROUP_ROWS, 1), F32)],
        scratch_shapes=[pltpu.VMEM((bsz, l, LANES), F32), pltpu.VMEM((bsz, l, LANES), F32)],
        compiler_params=_cparams(("arbitrary",)),
    )(q3, k3, k3, k3, v3, v3, v3, by_example(do), lse, sink_rows, slope_rows)
    return dq.reshape(t, A_WIDTH), dk.reshape(t, LANES), dv.reshape(t, LANES), dsink


CHUNK = KEY_BLOCKS * BLOCK


def _fox_chunk(qb, ci):
    sb = jnp.maximum(jnp.minimum(KEY_BLOCKS * ci, qb + 1 - KEY_BLOCKS), 0)
    lo = jnp.maximum(ci * CHUNK, N_PAD)
    return sb, lo, pl.ds(pl.multiple_of(sb * BLOCK, BLOCK), CHUNK)


def _fox_logits(s_ref, cr_ref, e, j, sb, lo, qb):
    lane = lax.broadcasted_iota(jnp.int32, (BLOCK, BLOCK), 1)
    ahead = lane - lax.broadcasted_iota(jnp.int32, (BLOCK, BLOCK), 0)
    first = (sb + j) * BLOCK
    s = s_ref[e, :, j * BLOCK:(j + 1) * BLOCK] - cr_ref[e, sb + j]
    return jnp.where((ahead <= qb * BLOCK - first) & (lane >= lo - first), s, NEG)


FOX_PAIRS = 4
FOX_HEADS = 2 * FOX_PAIRS
FOX_STEPS = B_PAIRS // FOX_PAIRS


def _fox_specs(nb):
    l = nb * BLOCK
    q_spec = pl.BlockSpec((BLOCK, FOX_PAIRS * LANES), lambda b, p, i: (b * nb + i, p))
    kv_spec = pl.BlockSpec((l, FOX_HEADS * LANES), lambda b, p, i: (b, p))
    cc_spec = pl.BlockSpec((FOX_HEADS, BLOCK, 1), lambda b, p, i: (b * FOX_STEPS + p, i, 0))
    cr_spec = pl.BlockSpec((FOX_HEADS, nb, 1, BLOCK), lambda b, p, i: (b * FOX_STEPS + p, 0, 0, 0))
    return q_spec, kv_spec, cc_spec, cr_spec


def _fox_fwd(q, k, v, c_row, nb, exchange=None):
    t = q.shape[0]
    bsz = t // (nb * BLOCK)
    assert nb >= KEY_BLOCKS

    n_x = len(exchange.ins) if exchange else 0

    def body(*refs):
        q_ref, k_ref, v_ref, cr_ref = refs[:4]
        o_ref, ox_ref, lse_ref = refs[4 + n_x:7 + n_x]
        s_scr, hi_scr, lo_scr = refs[7 + 2 * n_x:10 + 2 * n_x]
        qb = pl.program_id(2)
        if exchange:
            first = (pl.program_id(0) == 0) & (pl.program_id(1) == 0)
            last = (pl.program_id(0) == bsz - 1) & (pl.program_id(1) == FOX_STEPS - 1)
            _host_exchange(exchange, refs[4:4 + n_x], refs[7 + n_x:7 + 2 * n_x], refs[10 + 2 * n_x:],
                           first & (qb == 0), first & (qb == 2 * nb // 3), last & (qb == nb - 1),
                           late=last & (qb == 0))
        qs = [q_ref[:, a * LANES:(a + 1) * LANES] * SCALE for a in range(FOX_PAIRS)]
        first_half = lax.broadcasted_iota(jnp.int32, (BLOCK, LANES), 1) < HEAD_DIM

        def step(ci, carry):
            stats, accs = carry[:2 * FOX_HEADS], carry[2 * FOX_HEADS:]
            sb, lo, krows = _fox_chunk(qb, ci)
            new_stats, new_accs = [], []
            for a in range(FOX_PAIRS):
                alphas = []
                pv = jnp.zeros((BLOCK, LANES), F32)
                pv_lo = jnp.zeros((BLOCK, LANES), F32)
                for e in (2 * a, 2 * a + 1):
                    m, z = stats[2 * e], stats[2 * e + 1]
                    tile = slice(e * LANES, (e + 1) * LANES)
                    s_scr[e] = lax.dot_general(qs[a], k_ref[krows, tile], NT_DIMS, preferred_element_type=F32)
                    top = None
                    for j in range(KEY_BLOCKS):
                        s = _fox_logits(s_scr, cr_ref, e, j, sb, lo, qb)
                        s_scr[e, :, j * BLOCK:(j + 1) * BLOCK] = s
                        top = s if top is None else jnp.maximum(top, s)
                    m_new = jnp.maximum(m, jnp.max(top, axis=-1, keepdims=True))
                    alpha = jnp.exp(m - m_new)
                    m_wide = jnp.broadcast_to(m_new, (BLOCK, BLOCK))
                    total = None
                    for j in range(KEY_BLOCKS):
                        cols = slice(j * BLOCK, (j + 1) * BLOCK)
                        p = jnp.exp(s_scr[e, :, cols] - m_wide)
                        total = p if total is None else total + p
                        hi = p.astype(BF16)
                        hi_scr[e, :, cols] = hi
                        lo_scr[e, :, cols] = (p - hi.astype(F32)).astype(BF16)
                    z = alpha * z + jnp.sum(total, axis=-1, keepdims=True)
                    vv = v_ref[krows, tile]
                    pv = pv + jnp.dot(hi_scr[e], vv, preferred_element_type=F32)
                    pv_lo = pv_lo + jnp.dot(lo_scr[e], vv, preferred_element_type=F32)
                    new_stats += [m_new, z]
                    alphas.append(alpha)
                alpha = jnp.where(first_half, alphas[0], alphas[1])
                new_accs += [alpha * accs[2 * a] + pv, alpha * accs[2 * a + 1] + pv_lo]
            return (*new_stats, *new_accs)

        col = lambda val: jnp.full((BLOCK, 1), val, F32)
        done = lax.fori_loop(
            0, (qb + KEY_BLOCKS) // KEY_BLOCKS, step,
            (col(NEG), col(0.0)) * FOX_HEADS + (jnp.zeros((BLOCK, LANES), F32),) * (2 * FOX_PAIRS))
        for a in range(FOX_PAIRS):
            m0, z0, m1, z1 = done[4 * a:4 * a + 4]
            acc, acc_lo = done[2 * FOX_HEADS + 2 * a:2 * FOX_HEADS + 2 * a + 2]
            inv = 1.0 / jnp.where(first_half, z0, z1)
            tile = slice(a * LANES, (a + 1) * LANES)
            o_ref[:, tile] = (acc * inv).astype(BF16)
            ox_ref[:, tile] = (acc + acc_lo) * inv
            lse_ref[2 * a] = m0 + jnp.log(z0)
            lse_ref[2 * a + 1] = m1 + jnp.log(z1)

    q_spec, kv_spec, cc_spec, cr_spec = _fox_specs(nb)
    outs = pl.pallas_call(
        body,
        name="fox_fwd",
        grid=(bsz, FOX_STEPS, nb),
        in_specs=[q_spec, kv_spec, kv_spec, cr_spec] + [HBM_SPEC] * n_x,
        out_specs=[q_spec, q_spec, cc_spec] + [HBM_SPEC] * n_x,
        out_shape=[jax.ShapeDtypeStruct((t, B_WIDTH), BF16), jax.ShapeDtypeStruct((t, B_WIDTH), F32),
                   jax.ShapeDtypeStruct((bsz * B_HEADS, nb * BLOCK, 1), F32)] + (exchange.out_shape if exchange else []),
        scratch_shapes=[pltpu.VMEM((FOX_HEADS, BLOCK, CHUNK), F32), pltpu.VMEM((FOX_HEADS, BLOCK, CHUNK), BF16),
                        pltpu.VMEM((FOX_HEADS, BLOCK, CHUNK), BF16)] + (exchange.scratch if exchange else []),
        compiler_params=_cparams(("arbitrary",) * 3 if exchange else ("parallel", "parallel", "arbitrary")),
    )(q, k, v, c_row, *(exchange.ins if exchange else []))
    return outs[:3], outs[3:]


def _fox_bwd(q, k, v, o_exact, do, lse, c_row, nb, exchange=None):
    t = q.shape[0]
    l = nb * BLOCK
    bsz = t // l

    n_x = len(exchange.ins) if exchange else 0

    def body(*refs):
        q_ref, k_ref, v_ref, ox_ref, do_ref, lse_ref, cr_ref = refs[:7]
        dq_ref, dk_ref, dv_ref, dc_ref = refs[7 + n_x:11 + n_x]
        dk_acc, dv_acc, s_scr, dp_scr, p_scr, ds_scr = refs[11 + 2 * n_x:17 + 2 * n_x]
        qb = pl.program_id(2)
        if exchange:
            first = (pl.program_id(0) == 0) & (pl.program_id(1) == 0)
            last = (pl.program_id(0) == bsz - 1) & (pl.program_id(1) == FOX_STEPS - 1)
            _host_exchange(exchange, refs[7:7 + n_x], refs[11 + n_x:11 + 2 * n_x], refs[17 + 2 * n_x:],
                           first & (qb == 0), last & (qb == 0), last & (qb == nb - 1))

        @pl.when(qb == 0)
        def _():
            dk_acc[...] = jnp.zeros_like(dk_acc)
            dv_acc[...] = jnp.zeros_like(dv_acc)
            dc_ref[...] = jnp.zeros_like(dc_ref)

        top_half = lax.broadcasted_iota(jnp.int32, (LANES, BLOCK), 0) < HEAD_DIM
        pair_t = lambda x: jnp.concatenate([jnp.where(top_half, x.T, 0), jnp.where(top_half, 0, x.T)], axis=1)
        first_half = lax.broadcasted_iota(jnp.int32, (BLOCK, LANES), 1) < HEAD_DIM
        wide = lambda col: jnp.broadcast_to(col, (BLOCK, BLOCK))
        qs, dobs, qs_t, dob_t, deltas = [], [], [], [], []
        for a in range(FOX_PAIRS):
            tile = slice(a * LANES, (a + 1) * LANES)
            qs.append(q_ref[:, tile] * SCALE)
            dobs.append(do_ref[:, tile])
            qs_t.append(pair_t(qs[a]))
            dob_t.append(pair_t(dobs[a]))
            weighted = dobs[a].astype(F32) * ox_ref[:, tile]
            deltas += [wide(jnp.sum(jnp.where(first_half, weighted, 0.0), axis=-1, keepdims=True)),
                       wide(jnp.sum(jnp.where(first_half, 0.0, weighted), axis=-1, keepdims=True))]
        lses = [wide(lse_ref[e]) for e in range(FOX_HEADS)]

        def step(ci, dqs):
            sb, lo, krows = _fox_chunk(qb, ci)
            dqs = list(dqs)
            for a in range(FOX_PAIRS):
                for e in (2 * a, 2 * a + 1):
                    tile = slice(e * LANES, (e + 1) * LANES)
                    kk = k_ref[krows, tile]
                    s_scr[e] = lax.dot_general(qs[a], kk, NT_DIMS, preferred_element_type=F32)
                    dp_scr[e] = lax.dot_general(dobs[a], v_ref[krows, tile], NT_DIMS, preferred_element_type=F32)
                    for j in range(KEY_BLOCKS):
                        cols = slice(j * BLOCK, (j + 1) * BLOCK)
                        p = jnp.exp(_fox_logits(s_scr, cr_ref, e, j, sb, lo, qb) - lses[e])
                        ds = p * (dp_scr[e, :, cols] - deltas[e])
                        dc_ref[e, sb + j] -= jnp.sum(ds, axis=0, keepdims=True)
                        p_scr[e, :, cols] = p.astype(BF16)
                        ds_scr[e, :, cols] = ds.astype(BF16)
                    dqs[a] = dqs[a] + jnp.dot(ds_scr[e], kk, preferred_element_type=F32)
                both = slice(2 * a, 2 * a + 2)
                dk_t = jnp.dot(qs_t[a], ds_scr[both].reshape(2 * BLOCK, CHUNK), preferred_element_type=F32)
                dv_t = jnp.dot(dob_t[a], p_scr[both].reshape(2 * BLOCK, CHUNK), preferred_element_type=F32)
                for j in range(KEY_BLOCKS):
                    cols = slice(j * BLOCK, (j + 1) * BLOCK)
                    dk_acc[a * nb + sb + j] += dk_t[:, cols]
                    dv_acc[a * nb + sb + j] += dv_t[:, cols]
            return tuple(dqs)

        dqs = lax.fori_loop(0, (qb + KEY_BLOCKS) // KEY_BLOCKS, step,
                            (jnp.zeros((BLOCK, LANES), F32),) * FOX_PAIRS)
        for a in range(FOX_PAIRS):
            dq_ref[:, a * LANES:(a + 1) * LANES] = (dqs[a] * SCALE).astype(BF16)

        @pl.when(qb == nb - 1)
        def _():
            for a in range(FOX_PAIRS):
                for kb in range(nb):
                    rows = slice(kb * BLOCK, (kb + 1) * BLOCK)
                    for acc, out_ref in ((dk_acc, dk_ref), (dv_acc, dv_ref)):
                        out_ref[rows, a * LANES:(a + 1) * LANES] = acc[a * nb + kb].T.astype(BF16)

    q_spec, kv_spec, cc_spec, cr_spec = _fox_specs(nb)
    dkv_spec = pl.BlockSpec((l, FOX_PAIRS * LANES), lambda b, p, i: (b, p))
    outs = pl.pallas_call(
        body,
        name="fox_bwd",
        grid=(bsz, FOX_STEPS, nb),
        in_specs=[q_spec, kv_spec, kv_spec, q_spec, q_spec, cc_spec, cr_spec] + [HBM_SPEC] * n_x,
        out_specs=[q_spec, dkv_spec, dkv_spec, cr_spec] + [HBM_SPEC] * n_x,
        out_shape=[jax.ShapeDtypeStruct((t, B_WIDTH), BF16), jax.ShapeDtypeStruct((t, B_WIDTH), BF16),
                   jax.ShapeDtypeStruct((t, B_WIDTH), BF16),
                   jax.ShapeDtypeStruct((bsz * B_HEADS, nb, 1, BLOCK), F32)] + (exchange.out_shape if exchange else []),
        scratch_shapes=[pltpu.VMEM((FOX_PAIRS * nb, LANES, BLOCK), F32), pltpu.VMEM((FOX_PAIRS * nb, LANES, BLOCK), F32),
                        pltpu.VMEM((FOX_HEADS, BLOCK, CHUNK), F32), pltpu.VMEM((FOX_HEADS, BLOCK, CHUNK), F32),
                        pltpu.VMEM((FOX_HEADS, BLOCK, CHUNK), BF16), pltpu.VMEM((FOX_HEADS, BLOCK, CHUNK), BF16)]
        + (exchange.scratch if exchange else []),
        compiler_params=_cparams(("arbitrary",) * 3 if exchange else ("parallel", "parallel", "arbitrary")),
    )(q, k, v, o_exact, do, lse, c_row, *(exchange.ins if exchange else []))
    return outs[:4], outs[4:]


def _loss_head(h, final_w, target):
    bsz, l, d = h.shape
    nb = l // BLOCK

    def body(h_ref, w_ref, t_ref, loss_ref, dh_ref, dw_ref):
        b = pl.program_id(0)
        n = pl.program_id(1)

        @pl.when((b == 0) & (n == 0))
        def _():
            loss_ref[...] = jnp.zeros_like(loss_ref)
            dw_ref[...] = jnp.zeros_like(dw_ref)

        @pl.when(n == 0)
        def _():
            dh_ref[...] = jnp.zeros_like(dh_ref)

        @pl.when(n > 0)
        def _():
            hh = h_ref[0]
            w = w_ref[...]
            r = _rms_scale(hh)
            err = (hh * r) * w - t_ref[0]
            loss_ref[...] += 0.5 * jnp.sum(jnp.mean(err * err, axis=-1, keepdims=True), axis=0, keepdims=True)
            dy = err * (1.0 / d)
            dh, dw = _rms_bwd(dy, hh, w)
            dh_ref[0] = dh
            dw_ref[...] += dw

    return pl.pallas_call(
        body,
        name="loss_head",
        grid=(bsz, nb),
        in_specs=[
            pl.BlockSpec((1, BLOCK, d), lambda b, n: (b, n, 0)),
            pl.BlockSpec((1, d), lambda b, n: (0, 0)),
            pl.BlockSpec((1, BLOCK, d), lambda b, n: (b, jnp.maximum(n - 1, 0), 0)),
        ],
        out_specs=[
            pl.BlockSpec((1, 128), lambda b, n: (0, 0)),
            pl.BlockSpec((1, BLOCK, d), lambda b, n: (b, n, 0)),
            pl.BlockSpec((1, d), lambda b, n: (0, 0)),
        ],
        out_shape=[jax.ShapeDtypeStruct((1, 128), F32), jax.ShapeDtypeStruct((bsz, l, d), F32),
                   jax.ShapeDtypeStruct((1, d), F32)],
        compiler_params=_cparams(("arbitrary", "arbitrary")),
    )(h, final_w, target)


def _pad_tiles(w, src, heads, lane_slot, axis):
    pieces = []
    for h in range(heads):
        x = lax.slice_in_dim(w, src + HEAD_DIM * h, src + HEAD_DIM * (h + 1), axis=axis)
        z = jnp.zeros_like(x)
        pieces += [x, z] if lane_slot(h) == 0 else [z, x]
    return pieces


def _unpad_tiles(g, off, heads, lane_slot, axis):
    return [lax.slice_in_dim(g, off + LANES * h + HEAD_DIM * lane_slot(h),
                             off + LANES * h + HEAD_DIM * (lane_slot(h) + 1), axis=axis) for h in range(heads)]


def _layout_w_in(w):
    pad_f = jnp.zeros((w.shape[0], F_COLS - B_HEADS), w.dtype)
    return jnp.concatenate([w[:, :SRC_F], w[:, SRC_GA:], w[:, SRC_F:SRC_GA], pad_f], axis=1)


def _unlayout_w_in(g):
    return jnp.concatenate([g[:, :OFF_GA], g[:, OFF_F:OFF_F + B_HEADS], g[:, OFF_GA:OFF_F]], axis=1)


def _local_step(x, target, meta, norms, b_forget, sinks, w, comm=None):
    n1, nmix, n2, nfin = norms
    w1i, w1o = w[:2]
    bsz, seq, d = x.shape
    l = PREFIX + seq
    nb = l // BLOCK
    t = bsz * l

    h0 = jnp.concatenate([jnp.zeros((bsz, N_PAD, d), F32),
                          jnp.broadcast_to(meta[None], (bsz, N_META, d)), x], axis=1).reshape(t, d)

    if comm is None:
        (h1, g1, u1), _ = _ffn_fwd(h0, n1, w1i, w1o)
        w_in, wa, wb, wo, w2i, w2o = w[2:]
    else:
        (h1, g1, u1), gathered = _ffn_fwd(h0, n1, w1i, w1o, comm.gather(GATHER_PROJ))
        w_in, = comm.gathered(GATHER_PROJ, gathered)
    wp = _layout_w_in(w_in)
    un, qa, ka, va, qb, kb, vb, ga, gb, f_logit = _proj_fwd(h1, nmix, wp)
    b_pad = jnp.concatenate([b_forget, jnp.zeros((1, F_COLS - B_HEADS), F32)], axis=1)
    c = _forget_cumsum(f_logit, b_pad, nb)
    c_heads = c[:, :B_HEADS].reshape(bsz, l, B_HEADS).transpose(0, 2, 1).reshape(bsz * B_HEADS, l)
    c_row = c_heads.reshape(bsz * B_HEADS, nb, 1, BLOCK)

    slopes = jnp.exp2(-8.0 * jnp.arange(1, A_HEADS + 1, dtype=F32) / A_HEADS)
    slope_rows = jnp.repeat(slopes.reshape(A_KV_HEADS, A_GROUP), BLOCK, axis=1)[:, :, None]
    sink_rows = jnp.repeat(sinks.reshape(A_KV_HEADS, A_GROUP), BLOCK, axis=1)[:, :, None]

    oa, lse_a = _swa_fwd(qa, ka, va, sink_rows, slope_rows, nb)
    if comm is None:
        (ob, ob_exact, lse_b), _ = _fox_fwd(qb, kb, vb, c_row, nb)
    else:
        (ob, ob_exact, lse_b), gathered = _fox_fwd(qb, kb, vb, c_row, nb, comm.gather(GATHER_LATE))
        wa, wb, wo, w2i, w2o = comm.gathered(GATHER_LATE, gathered)
    wa_p = jnp.concatenate(_pad_tiles(wa, 0, A_HEADS, A_SLOT, 0), axis=0)
    h2, mixed = _merge_fwd(h1, oa, ob, ga, gb, wa_p, wb, wo)
    (h3, g2, u2), _ = _ffn_fwd(h2, n2, w2i, w2o)
    loss, dh3, d_nfin = _loss_head(h3.reshape(bsz, l, d), nfin, target)

    (dh2, n2b, a2, dgu2, df2, dn2_parts), _ = _ffn_bwd(dh3.reshape(t, d), h2, n2, g2, u2, w2i, w2o)
    g_w2o = _tn_matmul(a2, df2, "grad_ffn2_w_out")
    g_w2i = _tn_matmul(n2b, dgu2, "grad_ffn2_w_in")

    hosted = comm.swap("ffn2", dict(ffn2_w_in=g_w2i, ffn2_w_out=g_w2o)) if comm else None
    (dya, dyb, doa, dob, dga, dgb, dh2b), swapped = _merge_bwd(dh2, oa, ob, ga, gb, wa_p, wb, wo, hosted)
    g_wo = _tn_matmul(mixed, dh2b, "grad_w_out")
    g_wa = jnp.concatenate(_unpad_tiles(_tn_matmul(oa, dya, "grad_w_branch_a"), 0, A_HEADS, A_SLOT, 0), axis=0)
    g_wb = _tn_matmul(ob, dyb, "grad_w_branch_b")

    dqa, dka, dva, dsink_rows = _swa_bwd(qa, ka, va, doa, lse_a, sink_rows, slope_rows, nb)
    hosted = comm.scatter("ffn2", swapped) if comm else None
    (dqb, dkb, dvb, dc_row), pieces = _fox_bwd(qb, kb, vb, ob_exact, dob, lse_b, c_row, nb, hosted)
    if comm:
        comm.received("ffn2", pieces)
    dc = dc_row.reshape(bsz, B_HEADS, l).transpose(0, 2, 1).reshape(t, B_HEADS)
    dc = jnp.concatenate([dc, jnp.zeros((t, F_COLS - B_HEADS), F32)], axis=1)
    df_logit, db_parts = _forget_cumsum_bwd(dc, f_logit, b_pad, nb)

    dproj = jnp.concatenate([dqa, dka, dva, dqb, dkb, dvb, dga, dgb, df_logit], axis=1)
    g_win = _unlayout_w_in(_tn_matmul(un, dproj, "grad_w_in"))
    hosted = comm.swap("mixer", dict(w_in=g_win, w_branch_a=g_wa, w_branch_b=g_wb, w_out=g_wo)) if comm else None
    (dh1, dnmix_parts), swapped = _proj_bwd(dh2, h1, nmix, dproj, wp, hosted)
    hosted = comm.scatter("mixer", swapped) if comm else None
    (dh0, n1b, a1, dgu1, df1, dn1_parts), pieces = _ffn_bwd(dh1, h0, n1, g1, u1, w1i, w1o, hosted)
    dh0 = dh0.reshape(bsz, l, d)
    grad_x = dh0[:, PREFIX:]
    small = dict(
        meta_tokens=jnp.sum(dh0[:, N_PAD:PREFIX], axis=0),
        ffn1_norm=jnp.sum(dn1_parts, axis=0),
        mix_norm=jnp.sum(dnmix_parts, axis=0),
        ffn2_norm=jnp.sum(dn2_parts, axis=0),
        final_norm=d_nfin,
        b_forget=jnp.sum(db_parts, axis=0)[:, :B_HEADS],
        attn_sinks=jnp.sum(dsink_rows.reshape(bsz, A_HEADS, BLOCK), axis=(0, 2)).reshape(1, A_HEADS),
    )
    if comm is None:
        g_w1o = _tn_matmul(a1, df1, "grad_ffn1_w_out")
        g_w1i = _tn_matmul(n1b, dgu1, "grad_ffn1_w_in")
    else:
        comm.received("mixer", pieces)
        g_w1o, gathered = _tn_matmul(a1, df1, "grad_ffn1_w_out", comm.small_gather(loss, small))
        comm.small_gathered(gathered)
        swapped = _run_exchange(comm.swap("ffn1_out", dict(ffn1_w_out=g_w1o)), "exchange_halves_ffn1_out")
        g_w1i, pieces = _tn_matmul(n1b, dgu1, "grad_ffn1_w_in", comm.scatter("ffn1_out", swapped))
        comm.received("ffn1_out", pieces)
    big = dict(ffn1_w_in=g_w1i, ffn1_w_out=g_w1o, w_in=g_win, w_branch_a=g_wa, w_branch_b=g_wb,
               w_out=g_wo, ffn2_w_in=g_w2i, ffn2_w_out=g_w2o)
    return loss, grad_x, small, big


BIG = (
    ("ffn1_w_in", (D_MODEL, 5632), 1),
    ("ffn1_w_out", (2816, D_MODEL), 0),
    ("w_in", (D_MODEL, W_IN_COLS), 1),
    ("w_branch_a", (A_WIDTH, D_MODEL), 1),
    ("w_branch_b", (B_WIDTH, D_MODEL), 1),
    ("w_out", (D_MODEL, D_MODEL), 0),
    ("ffn2_w_in", (D_MODEL, 5632), 1),
    ("ffn2_w_out", (2816, D_MODEL), 0),
)
STACKED = "w_in"


def _coords():
    return lax.axis_index("x"), lax.axis_index("y"), lax.axis_index("c")


def _other_chips(x, y):
    return ((1 - x, y), (x, 1 - y), (1 - x, 1 - y))


def _chip_part(ref, name, shape, axis, k):
    if name == STACKED:
        return ref.at[k]
    size = shape[axis] // N_CHIPS
    start = pl.multiple_of(k * size, size)
    return ref.at[pl.ds(start, size), :] if axis == 0 else ref.at[:, pl.ds(start, size)]


def _full_shape(name, shape):
    return (N_CHIPS, shape[0], shape[1] // N_CHIPS) if name == STACKED else shape


class _Exchange:
    def __init__(self, ins, out_shape, n_sems, ops):
        self.ins, self.out_shape, self.n_sems, self.ops = list(ins), list(out_shape), n_sems, ops

    @property
    def scratch(self):
        return [pltpu.SemaphoreType.DMA((self.n_sems,)), pltpu.SemaphoreType.DMA((self.n_sems,))]


SEMS_PER_GATHER = 9


def _gather_exchange(shards, table):
    n = len(table)
    x_nbr, y_nbr, diagonal = 0, 1, 2

    def ops(ins, outs, send_sems, recv_sems):
        x, y, c = _coords()
        mine = 2 * x + y
        sibling = (x, y, 1 - c)
        chips = _other_chips(x, y)
        slots = [2 * chip[0] + chip[1] for chip in chips]

        def part(i, k):
            name, shape, axis = table[i][:3]
            return _chip_part(outs[i], name, shape, axis, k)

        def half(ref, h):
            rows = ref.shape[0] // 2
            return ref.at[pl.ds(pl.multiple_of(h * rows, rows), rows), :]

        def remote(i, sem, src, dst, device):
            sem = SEMS_PER_GATHER * i + sem
            return pltpu.make_async_remote_copy(src, dst, send_sems.at[sem], recv_sems.at[sem],
                                                device_id=device, device_id_type=MESH_ID)

        def own(i):
            return remote(i, 0, ins[i], part(i, mine), sibling)

        def fetch(i, j, slot):
            if table[i][4]:
                src, dst = half(ins[i], c), half(part(i, slot), c)
            else:
                src, dst = ins[i], part(i, slot)
            return remote(i, 1 + j, src, dst, (chips[j][0], chips[j][1], c))

        def relayed(i, via, of):
            region = half(half(part(i, slots[of]), c), via)
            return remote(i, 4 + via, region, region, (chips[via][0], chips[via][1], c))

        def forward(i, j, h):
            region = half(part(i, slots[j]), h)
            return remote(i, 6 + j, region, region, sibling)

        def start():
            for i in range(n):
                for j in (x_nbr, y_nbr) if table[i][4] else (x_nbr, y_nbr, diagonal):
                    fetch(i, j, mine).start()
            for i in range(n):
                own(i).start()

        def relay():
            for i in range(n):
                if not table[i][4]:
                    for j in range(3):
                        fetch(i, j, slots[j]).wait_recv()
                    continue
                fetch(i, y_nbr, slots[y_nbr]).wait_recv()
                relayed(i, x_nbr, y_nbr).start()
                forward(i, y_nbr, c).start()
                fetch(i, x_nbr, slots[x_nbr]).wait_recv()
                relayed(i, y_nbr, x_nbr).start()
                forward(i, x_nbr, c).start()

        def relay_diagonal():
            for i in range(n):
                if table[i][4]:
                    relayed(i, x_nbr, diagonal).wait_recv()
                    relayed(i, y_nbr, diagonal).wait_recv()
                    forward(i, diagonal, c).start()

        def finish():
            for i in range(n):
                own(i).wait()
                for j in range(3):
                    if table[i][4]:
                        forward(i, j, 1 - c).wait_recv()
                        forward(i, j, c).wait_send()
                    if j != diagonal or not table[i][4]:
                        fetch(i, j, mine).wait_send()
                if table[i][4]:
                    relayed(i, x_nbr, y_nbr).wait_send()
                    relayed(i, y_nbr, x_nbr).wait_send()

        return start, relay, relay_diagonal, finish

    out_shape = [jax.ShapeDtypeStruct(_full_shape(name, shape), dtype) for name, shape, _, dtype, _ in table]
    return _Exchange(shards, out_shape, SEMS_PER_GATHER * n, ops)


def _run_exchange(exchange, name):
    n = len(exchange.ins)

    def body(*refs):
        for phase in exchange.ops(refs[:n], refs[n:2 * n], *refs[2 * n:]):
            phase()

    return pl.pallas_call(
        body,
        name=name,
        in_specs=[HBM_SPEC] * n,
        out_specs=[HBM_SPEC] * n,
        out_shape=exchange.out_shape,
        scratch_shapes=exchange.scratch,
    )(*exchange.ins)


def _host_exchange(exchange, in_refs, out_refs, sem_refs, first, middle, last, late=None):
    start, *relays, finish = exchange.ops(in_refs, out_refs, *sem_refs)
    pl.when(first)(start)
    pl.when(middle)(relays[0])
    if len(relays) > 1:
        pl.when(last if late is None else late)(relays[1])
    pl.when(last)(finish)


def _halves_view(name, shape, axis):
    r, c = shape
    if name == STACKED:
        return (N_CHIPS, 2, r // 2, c // N_CHIPS), lambda ref, h: ref.at[:, h]
    if axis == 1:
        return (2, r // 2, c), lambda ref, h: ref.at[h]
    return (N_CHIPS, 2, r // N_CHIPS // 2, c), lambda ref, h: ref.at[:, h]


def _halves_exchange(grads, entries):
    n_w = len(entries)
    views = [_halves_view(*entry) for entry in entries]

    def ops(ins, outs, send_sems, recv_sems):
        x, y, c = _coords()
        copies = [pltpu.make_async_remote_copy(views[i][1](ins[i], 1 - c), outs[i], send_sems.at[i], recv_sems.at[i],
                                               device_id=(x, y, 1 - c), device_id_type=MESH_ID) for i in range(n_w)]

        def start():
            for cp in copies:
                cp.start()

        def finish():
            for cp in copies:
                cp.wait()

        return start, lambda: None, finish

    half_shape = lambda v: tuple(d for i, d in enumerate(v) if i != (1 if len(v) == 4 else 0))
    out_shape = [jax.ShapeDtypeStruct(half_shape(v[0]), F32) for v in views]
    return _Exchange([g.reshape(v[0]) for g, v in zip(grads, views)], out_shape, n_w, ops)


def _add_sibling(g_view, recv, c, name):
    shape = recv.shape
    if len(shape) == 2:
        tr = _tile(shape[0], 128, 16)
        grid = (shape[0] // tr,)
        g_spec = pl.BlockSpec((None, tr, shape[1]), lambda i, c_ref: (c_ref[0], i, 0))
        r_spec = pl.BlockSpec((tr, shape[1]), lambda i, c_ref: (i, 0))
    else:
        tr = _tile(shape[1], 256, 16)
        grid = (N_CHIPS, shape[1] // tr)
        g_spec = pl.BlockSpec((None, None, tr, shape[2]), lambda k, i, c_ref: (k, c_ref[0], i, 0))
        r_spec = pl.BlockSpec((None, tr, shape[2]), lambda k, i, c_ref: (k, i, 0))

    def body(c_ref, g_ref, r_ref, o_ref):
        o_ref[...] = (g_ref[...] + r_ref[...]).astype(BF16)

    return pl.pallas_call(
        body,
        name="add_sibling_" + name,
        grid_spec=pltpu.PrefetchScalarGridSpec(num_scalar_prefetch=1, grid=grid, in_specs=[g_spec, r_spec],
                                               out_specs=r_spec),
        out_shape=jax.ShapeDtypeStruct(shape, BF16),
        compiler_params=_cparams(("parallel",) * len(grid)),
    )(c, g_view, recv)


def _piece_of(ref, name, axis, k):
    if name == STACKED or axis == 0:
        return ref.at[k]
    size = ref.shape[1] // N_CHIPS
    return ref.at[:, pl.ds(pl.multiple_of(k * size, size), size)]


def _piece_shape(name, shape, axis):
    r, c = shape
    return (r // 2, c // N_CHIPS) if (axis == 1) else (r // N_CHIPS // 2, c)


def _scatter_exchange(partials, entries):
    n_w = len(entries)

    def ops(ins, outs, send_sems, recv_sems):
        x, y, c = _coords()
        chips = _other_chips(x, y)
        copies = []
        for i, (name, _, axis) in enumerate(entries):
            for j, chip in enumerate(chips):
                sem = 3 * i + j
                copies.append(pltpu.make_async_remote_copy(
                    _piece_of(ins[i], name, axis, 2 * chip[0] + chip[1]), outs[i].at[j], send_sems.at[sem],
                    recv_sems.at[sem], device_id=(chip[0], chip[1], c), device_id_type=MESH_ID))

        def start():
            for cp in copies:
                cp.start()

        def finish():
            for cp in copies:
                cp.wait()

        return start, lambda: None, finish

    out_shape = [jax.ShapeDtypeStruct((3,) + _piece_shape(*entry), BF16) for entry in entries]
    return _Exchange(partials, out_shape, 3 * n_w, ops)


def _add_chips(partial, recv, mine, name, axis):
    rows, cols = recv.shape[1:]
    tr = _tile(rows, 256, 16)
    if name == STACKED or axis == 0:
        p_spec = pl.BlockSpec((None, tr, cols), lambda i, k_ref: (k_ref[0], i, 0))
    else:
        p_spec = pl.BlockSpec((tr, cols), lambda i, k_ref: (i, k_ref[0]))

    def body(k_ref, p_ref, r_ref, o_ref):
        f32 = lambda a: a.astype(F32)
        o_ref[...] = ((f32(p_ref[...]) + f32(r_ref[0])) + f32(r_ref[1])) + f32(r_ref[2])

    return pl.pallas_call(
        body,
        name="add_chips_" + name,
        grid_spec=pltpu.PrefetchScalarGridSpec(
            num_scalar_prefetch=1, grid=(rows // tr,),
            in_specs=[p_spec, pl.BlockSpec((3, tr, cols), lambda i, k_ref: (0, i, 0))],
            out_specs=pl.BlockSpec((tr, cols), lambda i, k_ref: (i, 0))),
        out_shape=jax.ShapeDtypeStruct((rows, cols), F32),
        compiler_params=_cparams(("parallel",)),
    )(mine, partial, recv)


def _share_with_sibling(halves):
    n_w = len(halves)

    def body(*refs):
        ins, outs = refs[:n_w], refs[n_w:2 * n_w]
        send_sems, recv_sems = refs[2 * n_w:]
        x, y, c = _coords()
        copies = [pltpu.make_async_remote_copy(ins[i], outs[i], send_sems.at[i], recv_sems.at[i],
                                               device_id=(x, y, 1 - c), device_id_type=MESH_ID) for i in range(n_w)]
        for cp in copies:
            cp.start()
        for cp in copies:
            cp.wait()

    return pl.pallas_call(
        body,
        name="share_with_sibling",
        in_specs=[HBM_SPEC] * n_w,
        out_specs=[HBM_SPEC] * n_w,
        out_shape=[jax.ShapeDtypeStruct(h.shape, F32) for h in halves],
        scratch_shapes=[pltpu.SemaphoreType.DMA((n_w,)), pltpu.SemaphoreType.DMA((n_w,))],
    )(*halves)


SMALL_ROWS = 168


def _small_exchange(buf):
    def ops(ins, outs, send_sems, recv_sems):
        x, y, c = _coords()
        me = 4 * x + 2 * y + c
        peers = [(x ^ fx, y ^ fy, c ^ fc) for fx in (0, 1) for fy in (0, 1) for fc in (0, 1)][1:]

        def copy(j, slot, dev):
            return pltpu.make_async_remote_copy(ins[0], outs[0].at[slot], send_sems.at[j], recv_sems.at[j],
                                                device_id=dev, device_id_type=MESH_ID)

        own = pltpu.make_async_copy(ins[0], outs[0].at[me], send_sems.at[N_DEV - 1])

        def start():
            own.start()
            for j, dev in enumerate(peers):
                copy(j, me, dev).start()

        def finish():
            for j, dev in enumerate(peers):
                copy(j, 4 * dev[0] + 2 * dev[1] + dev[2], dev).wait()
            own.wait()

        return start, lambda: None, finish

    return _Exchange([buf], [jax.ShapeDtypeStruct((N_DEV,) + buf.shape, F32)], N_DEV, ops)


def _sum_devices(gathered):
    def body(g_ref, out_ref):
        acc = g_ref[0]
        for d in range(1, N_DEV):
            acc = acc + g_ref[d]
        out_ref[...] = acc

    return pl.pallas_call(
        body,
        name="sum_devices",
        in_specs=[VMEM_SPEC],
        out_specs=VMEM_SPEC,
        out_shape=jax.ShapeDtypeStruct(gathered.shape[1:], F32),
    )(gathered)


def _adamw(w, g, m, v):
    r, rest = w.shape[0], w.shape[1:]
    per_row = 1
    for dim in rest:
        per_row *= dim
    tr = _tile(r, max(8, (5 << 19) // (4 * per_row)), 8 if len(rest) == 1 else 1)

    def body(w_ref, g_ref, m_ref, v_ref, d_ref, mo_ref, vo_ref):
        gg = g_ref[...]
        mm = ADAM_B1 * m_ref[...] + (1.0 - ADAM_B1) * gg
        vv = ADAM_B2 * v_ref[...] + (1.0 - ADAM_B2) * (gg * gg)
        m_hat = mm / (1.0 - ADAM_B1 ** ADAM_STEP)
        v_hat = vv / (1.0 - ADAM_B2 ** ADAM_STEP)
        d_ref[...] = -ADAM_LR * (m_hat / (jnp.sqrt(v_hat) + ADAM_EPS) + ADAM_WD * w_ref[...])
        mo_ref[...] = mm
        vo_ref[...] = vv

    spec = pl.BlockSpec((tr,) + rest, lambda i: (i,) + (0,) * len(rest))
    return pl.pallas_call(
        body,
        name="adamw",
        grid=(r // tr,),
        in_specs=[spec] * 4,
        out_specs=[spec] * 3,
        out_shape=[jax.ShapeDtypeStruct(w.shape, F32)] * 3,
        compiler_params=_cparams(("parallel",)),
    )(w, g, m, v)


def _adamw_halves(w, own, other, m, v, c, name):
    r, cols = w.shape
    half = r // 2
    tr = _tile(half, 256, 8)
    nt = half // tr
    whole = pl.BlockSpec((tr, cols), lambda h, i, c_ref: (h * nt + i, 0))
    part = pl.BlockSpec((tr, cols), lambda h, i, c_ref: (i, 0))

    def body(c_ref, w_ref, own_ref, other_ref, m_ref, v_ref, g_ref, d_ref, mo_ref, vo_ref):
        gg = jnp.where(pl.program_id(0) == c_ref[0], own_ref[...], other_ref[...])
        g_ref[...] = gg
        mm = ADAM_B1 * m_ref[...] + (1.0 - ADAM_B1) * gg
        vv = ADAM_B2 * v_ref[...] + (1.0 - ADAM_B2) * (gg * gg)
        m_hat = mm / (1.0 - ADAM_B1 ** ADAM_STEP)
        v_hat = vv / (1.0 - ADAM_B2 ** ADAM_STEP)
        d_ref[...] = -ADAM_LR * (m_hat / (jnp.sqrt(v_hat) + ADAM_EPS) + ADAM_WD * w_ref[...])
        mo_ref[...] = mm
        vo_ref[...] = vv

    return pl.pallas_call(
        body,
        name="adamw_" + name,
        grid_spec=pltpu.PrefetchScalarGridSpec(
            num_scalar_prefetch=1, grid=(2, nt),
            in_specs=[whole, part, part, whole, whole], out_specs=[whole] * 4),
        out_shape=[jax.ShapeDtypeStruct((r, cols), F32)] * 4,
        compiler_params=_cparams(("parallel", "parallel")),
    )(c, w, own, other, m, v)


GATHER_FIRST = ("ffn1_w_in", "ffn1_w_out")
GATHER_PROJ = ("w_in",)
GATHER_LATE = ("w_branch_a", "w_branch_b", "w_out", "ffn2_w_in", "ffn2_w_out")


class _Comm:
    def __init__(self, shards, c_arr, mine_arr):
        self.shards, self.c, self.mine = shards, c_arr, mine_arr
        self.groups, self.halves = {}, {}
        self.by_name = {entry[0]: entry for entry in BIG}

    def small_gather(self, loss, small):
        pad_lanes = lambda a: jnp.concatenate([a, jnp.zeros((1, LANES - a.shape[1]), F32)], axis=1)
        buf = jnp.concatenate([
            small["meta_tokens"].reshape(128, LANES),
            small["ffn1_norm"].reshape(8, LANES), small["mix_norm"].reshape(8, LANES),
            small["ffn2_norm"].reshape(8, LANES), small["final_norm"].reshape(8, LANES),
            loss, pad_lanes(small["b_forget"]), pad_lanes(small["attn_sinks"]),
            jnp.zeros((SMALL_ROWS - 163, LANES), F32)], axis=0)
        return _small_exchange(buf)

    def small_gathered(self, outs):
        self.reduced = _sum_devices(outs[0])

    def gather(self, names):
        table = [self.by_name[n] + (BF16, True) for n in names]
        return _gather_exchange([self.shards[n] for n in names], table)

    def gathered(self, names, outs):
        return [o.transpose(1, 0, 2).reshape(D_MODEL, W_IN_COLS) if n == STACKED else o for n, o in zip(names, outs)]

    def swap(self, tag, grads):
        entries = [self.by_name[n] for n in grads]
        arrays = [g.reshape(D_MODEL, N_CHIPS, W_IN_COLS // N_CHIPS).transpose(1, 0, 2) if n == STACKED else g
                  for n, g in grads.items()]
        self.groups[tag] = (entries, arrays)
        return _halves_exchange(arrays, entries)

    def scatter(self, tag, received):
        entries, arrays = self.groups[tag]
        views = [_halves_view(*entry) for entry in entries]
        partials = [_add_sibling(g.reshape(v[0]), r, self.c, name)
                    for g, v, r, (name, _, _) in zip(arrays, views, received, entries)]
        self.groups[tag] = (entries, partials)
        return _scatter_exchange(partials, entries)

    def received(self, tag, pieces):
        entries, partials = self.groups[tag]
        for p, r, (name, _, axis) in zip(partials, pieces, entries):
            self.halves[name] = _add_chips(p, r, self.mine, name, axis)

    def finish(self):
        names = [n for n, _, _ in BIG]
        own = [self.halves[n] for n in names]
        return dict(zip(names, zip(own, _share_with_sibling(own))))


def kernel(x, meta_tokens, ffn1_norm, ffn1_w_in, ffn1_w_out, mix_norm, w_in, b_forget, attn_sinks, w_branch_a, w_branch_b, w_out, ffn2_norm, ffn2_w_in, ffn2_w_out, final_norm, loss_target, m_meta_tokens, m_ffn1_norm, m_ffn1_w_in, m_ffn1_w_out, m_mix_norm, m_w_in, m_b_forget, m_attn_sinks, m_w_branch_a, m_w_branch_b, m_w_out, m_ffn2_norm, m_ffn2_w_in, m_ffn2_w_out, m_final_norm, v_meta_tokens, v_ffn1_norm, v_ffn1_w_in, v_ffn1_w_out, v_mix_norm, v_w_in, v_b_forget, v_attn_sinks, v_w_branch_a, v_w_branch_b, v_w_out, v_ffn2_norm, v_ffn2_w_in, v_ffn2_w_out, v_final_norm):
    given = dict(locals())
    names = ["meta_tokens", "ffn1_norm", "ffn1_w_in", "ffn1_w_out", "mix_norm", "w_in", "b_forget", "attn_sinks",
             "w_branch_a", "w_branch_b", "w_out", "ffn2_norm", "ffn2_w_in", "ffn2_w_out", "final_norm"]
    big_names = [n for n, _, _ in BIG]
    cx, cy, cc = _coords()
    c_arr = cc.reshape(1).astype(jnp.int32)
    mine_arr = (2 * cx + cy).reshape(1).astype(jnp.int32)

    comm = _Comm({n: given[n][0].astype(BF16) for n in big_names}, c_arr, mine_arr)
    table = [comm.by_name[n] + (BF16, True) for n in GATHER_FIRST] + [("meta_tokens", (N_META, D_MODEL), 1, F32, False)]
    first = _gather_exchange([comm.shards[n] for n in GATHER_FIRST] + [meta_tokens], table)
    w1i, w1o, meta_full = _run_exchange(first, "gather_first")
    norms = (ffn1_norm, mix_norm, ffn2_norm, final_norm.reshape(1, D_MODEL))
    loss, grad_x, small, big = _local_step(x, loss_target, meta_full, norms, b_forget, attn_sinks, (w1i, w1o), comm)

    swap = comm.swap("ffn1_in", dict(ffn1_w_in=big["ffn1_w_in"]))
    last = comm.scatter("ffn1_in", _run_exchange(swap, "exchange_halves_ffn1_in"))
    comm.received("ffn1_in", _run_exchange(last, "scatter_chip_sums"))
    grad_halves = comm.finish()
    grads = {}

    red = comm.reduced
    meta_cols = red[:128].reshape(N_META, D_MODEL)
    grads["meta_tokens"] = lax.dynamic_slice_in_dim(meta_cols, (2 * cx + cy) * (D_MODEL // N_CHIPS),
                                                    D_MODEL // N_CHIPS, axis=1)
    grads["ffn1_norm"] = red[128:136].reshape(1, D_MODEL)
    grads["mix_norm"] = red[136:144].reshape(1, D_MODEL)
    grads["ffn2_norm"] = red[144:152].reshape(1, D_MODEL)
    grads["final_norm"] = red[152:160].reshape(1, D_MODEL)
    loss_out = red[160, 0]
    grads["b_forget"] = red[161:162, :B_HEADS]
    grads["attn_sinks"] = red[162:163, :A_HEADS]

    out_g, out_d, out_m, out_v = [], [], [], []
    for n in names:
        w_full = given[n]
        shape = w_full.shape
        two_d = (lambda a: a.reshape(shape[-2], shape[-1])) if len(shape) >= 2 else (lambda a: a.reshape(1, shape[0]))
        if n == STACKED:
            own, other = grad_halves[n]
            g_t = jnp.concatenate([jnp.where(cc == 0, own, other), jnp.where(cc == 0, other, own)], axis=0).T
            tiles = lambda a: a.reshape(shape[-1], shape[-2] // LANES, LANES)
            untile = lambda a: a.reshape(shape[-1], shape[-2]).T
            d2, m2, v2 = [untile(a) for a in _adamw(tiles(two_d(w_full).T), tiles(g_t), tiles(two_d(given["m_" + n]).T),
                                                     tiles(two_d(given["v_" + n]).T))]
            g2 = g_t.T
        elif n in grad_halves:
            own, other = grad_halves[n]
            g2, d2, m2, v2 = _adamw_halves(two_d(w_full), own, other, two_d(given["m_" + n]),
                                           two_d(given["v_" + n]), c_arr, n)
        else:
            g2 = two_d(grads[n])
            d2, m2, v2 = _adamw(two_d(w_full), g2, two_d(given["m_" + n]), two_d(given["v_" + n]))
        out_g.append(g2.reshape(shape))
        out_d.append(d2.reshape(shape))
        out_m.append(m2.reshape(shape))
        out_v.append(v2.reshape(shape))
    return (loss_out, grad_x, *out_g, *out_d, *out_m, *out_v)
```

```python
import jax
import jax.numpy as jnp
from jax import lax
from jax.experimental import pallas as pl
from jax.experimental.pallas import tpu as pltpu

F32 = jnp.float32
BF16 = jnp.bfloat16

D_MODEL = 1024
N_META = 16
BLOCK = 128
LANES = 128
PREFIX = BLOCK
N_PAD = PREFIX - N_META
HEAD_DIM = 64
A_HEADS = 8
A_KV_HEADS = 2
A_GROUP = 4
B_HEADS = 8
B_PAIRS = B_HEADS // 2
A_WIDTH = A_HEADS * HEAD_DIM
A_KV_WIDTH = A_KV_HEADS * HEAD_DIM
B_WIDTH = B_HEADS * HEAD_DIM
W_IN_COLS = A_WIDTH + 2 * A_KV_WIDTH + 3 * B_WIDTH + B_HEADS + 2 * D_MODEL
SRC_KA = A_WIDTH
SRC_VA = SRC_KA + A_KV_WIDTH
SRC_QB = SRC_VA + A_KV_WIDTH
SRC_KB = SRC_QB + B_WIDTH
SRC_VB = SRC_KB + B_WIDTH
SRC_F = SRC_VB + B_WIDTH
SRC_GA = SRC_F + B_HEADS
SRC_GB = SRC_GA + D_MODEL
A_PAD_WIDTH = A_HEADS * LANES
B_PAD_WIDTH = B_HEADS * LANES
F_COLS = LANES
OFF_QA = 0
OFF_KA = SRC_KA
OFF_VA = SRC_VA
OFF_QB = SRC_QB
OFF_KB = SRC_KB
OFF_VB = SRC_VB
OFF_GA = SRC_F
OFF_GB = OFF_GA + D_MODEL
OFF_F = OFF_GB + D_MODEL
P_COLS = OFF_F + F_COLS
EPS = 1e-6
NEG = -1e30
SCALE = HEAD_DIM ** -0.5
KEY_BLOCKS = 4

ADAM_LR = 0.001
ADAM_B1 = 0.9
ADAM_B2 = 0.999
ADAM_EPS = 1e-08
ADAM_WD = 0.01
ADAM_STEP = 10

N_CHIPS = 4
N_DEV = 8
VMEM_LIMIT = 56 * 1024 * 1024

NT_DIMS = (((1,), (1,)), ((), ()))
TN_DIMS = (((0,), (0,)), ((), ()))
MESH_ID = pl.DeviceIdType.MESH
HBM_SPEC = pl.BlockSpec(memory_space=pltpu.HBM)
VMEM_SPEC = pl.BlockSpec(memory_space=pltpu.VMEM)


def _tile(n, target, mult=16):
    best = None
    for t in range(mult, min(n, target) + 1, mult):
        if n % t == 0:
            best = t
    return best if best is not None else n


def _cparams(sem):
    return pltpu.CompilerParams(dimension_semantics=sem, vmem_limit_bytes=VMEM_LIMIT)


def _rms_scale(h):
    return lax.rsqrt(jnp.mean(h * h, axis=-1, keepdims=True) + EPS)


def _rms_bwd(dn, h, w):
    r = _rms_scale(h)
    dw = jnp.sum(dn * (h * r), axis=0, keepdims=True)
    z = dn * w
    dh = r * z - h * ((r * r * r) * jnp.mean(z * h, axis=-1, keepdims=True))
    return dh, dw


def _ffn_fwd(h, norm_w, w_in, w_out, exchange=None):
    t, d = h.shape
    f = w_out.shape[0]
    tm = _tile(t, 272)
    tc = _tile(f, 256, 128)
    nj = f // tc
    ni = t // tm
    n_x = len(exchange.ins) if exchange else 0

    def body(*refs):
        h_ref, nw_ref, wi_ref, wo_ref = refs[:4]
        hout_ref, g_ref, u_ref = refs[4 + n_x:7 + n_x]
        a_scr = refs[7 + 2 * n_x]
        i = pl.program_id(0)
        if exchange:
            _host_exchange(exchange, refs[4:4 + n_x], refs[7 + n_x:7 + 2 * n_x], refs[8 + 2 * n_x:],
                           i == 0, i == ni // 3, i == ni - 1, late=i == 2 * ni // 3)
        hh = h_ref[...]
        n = ((hh * _rms_scale(hh)) * nw_ref[...]).astype(BF16)
        for j in range(nj):
            cols = slice(j * tc, (j + 1) * tc)
            g = jnp.dot(n, wi_ref[:, j * tc:(j + 1) * tc], preferred_element_type=F32)
            u = jnp.dot(n, wi_ref[:, f + j * tc:f + (j + 1) * tc], preferred_element_type=F32)
            g_ref[:, cols] = g
            u_ref[:, cols] = u
            a_scr[:, cols] = ((g * jax.nn.sigmoid(g)) * u).astype(BF16)
        hout_ref[...] = hh + 0.5 * jnp.dot(a_scr[...], wo_ref[...], preferred_element_type=F32)

    resident = lambda a: pl.BlockSpec(a.shape, lambda i: (0, 0), pipeline_mode=pl.Buffered(1))
    row = lambda w: pl.BlockSpec((tm, w), lambda i: (i, 0))
    outs = pl.pallas_call(
        body,
        name="ffn_fwd",
        grid=(ni,),
        in_specs=[row(d), pl.BlockSpec((1, d), lambda i: (0, 0)), resident(w_in), resident(w_out)] + [HBM_SPEC] * n_x,
        out_specs=[row(d), row(f), row(f)] + [HBM_SPEC] * n_x,
        out_shape=[
            jax.ShapeDtypeStruct((t, d), F32),
            jax.ShapeDtypeStruct((t, f), F32),
            jax.ShapeDtypeStruct((t, f), F32),
        ] + (exchange.out_shape if exchange else []),
        scratch_shapes=[pltpu.VMEM((tm, f), BF16)] + (exchange.scratch if exchange else []),
        compiler_params=_cparams(("arbitrary",) if exchange else ("parallel",)),
    )(h, norm_w, w_in, w_out, *(exchange.ins if exchange else []))
    return outs[:3], outs[3:]


def _ffn_bwd(dh_out, h, norm_w, g, u, w_in, w_out, exchange=None):
    t, d = h.shape
    f = w_out.shape[0]
    tm = _tile(t, 272)
    tc = _tile(f, 256, 128)
    nj = f // tc
    ni = t // tm
    n_x = len(exchange.ins) if exchange else 0

    def body(*refs):
        dho_ref, h_ref, nw_ref, g_ref, u_ref, wi_ref, wo_ref = refs[:7]
        dhin_ref, n_ref, a_ref, dgu_ref, df_ref, dnw_ref = refs[7 + n_x:13 + n_x]
        i = pl.program_id(0)
        if exchange:
            _host_exchange(exchange, refs[7:7 + n_x], refs[13 + n_x:13 + 2 * n_x], refs[13 + 2 * n_x:],
                           i == 0, i == ni - 1, i == ni - 1)
        hh = h_ref[...]
        nw = nw_ref[...]
        n_ref[...] = ((hh * _rms_scale(hh)) * nw).astype(BF16)
        dho = dho_ref[...]
        df = (0.5 * dho).astype(BF16)
        df_ref[...] = df
        for j in range(nj):
            cols = slice(j * tc, (j + 1) * tc)
            da = lax.dot_general(df, wo_ref[cols, :], NT_DIMS, preferred_element_type=F32)
            gg = g_ref[:, cols]
            uu = u_ref[:, cols]
            sig = jax.nn.sigmoid(gg)
            sl = gg * sig
            a_ref[:, cols] = (sl * uu).astype(BF16)
            dgu_ref[0, :, cols] = ((da * uu) * (sig * (1.0 + gg * (1.0 - sig)))).astype(BF16)
            dgu_ref[1, :, cols] = (da * sl).astype(BF16)
        dn = (lax.dot_general(dgu_ref[0], wi_ref[:, :f], NT_DIMS, preferred_element_type=F32)
              + lax.dot_general(dgu_ref[1], wi_ref[:, f:], NT_DIMS, preferred_element_type=F32))
        dh, dw = _rms_bwd(dn, hh, nw)
        dhin_ref[...] = dho + dh
        dnw_ref[0] = dw

    resident = lambda a: pl.BlockSpec(a.shape, lambda i: (0, 0), pipeline_mode=pl.Buffered(1))
    row = lambda w: pl.BlockSpec((tm, w), lambda i: (i, 0))
    outs = pl.pallas_call(
        body,
        name="ffn_bwd",
        grid=(ni,),
        in_specs=[row(d), row(d), pl.BlockSpec((1, d), lambda i: (0, 0)), row(f), row(f),
                  resident(w_in), resident(w_out)] + [HBM_SPEC] * n_x,
        out_specs=[row(d), row(d), row(f), pl.BlockSpec((2, tm, f), lambda i: (0, i, 0)), row(d),
                   pl.BlockSpec((1, 1, d), lambda i: (i, 0, 0))] + [HBM_SPEC] * n_x,
        out_shape=[
            jax.ShapeDtypeStruct((t, d), F32),
            jax.ShapeDtypeStruct((t, d), BF16),
            jax.ShapeDtypeStruct((t, f), BF16),
            jax.ShapeDtypeStruct((2, t, f), BF16),
            jax.ShapeDtypeStruct((t, d), BF16),
            jax.ShapeDtypeStruct((ni, 1, d), F32),
        ] + (exchange.out_shape if exchange else []),
        scratch_shapes=exchange.scratch if exchange else [],
        compiler_params=_cparams(("arbitrary",) if exchange else ("parallel",)),
    )(dh_out, h, norm_w, g, u, w_in, w_out, *(exchange.ins if exchange else []))
    return outs[:6], outs[6:]


def _tn_matmul(a, b, name, exchange=None):
    t, k = a.shape
    split = b.ndim == 3
    n = 2 * b.shape[2] if split else b.shape[1]
    tk = _tile(k, 512, 128)
    tn = _tile(b.shape[-1], 1408, 128)
    per_half = b.shape[-1] // tn
    ni, nj = k // tk, n // tn
    n_x = len(exchange.ins) if exchange else 0

    def body(*refs):
        a_ref, b_ref, o_ref = refs[0], refs[1], refs[2 + n_x]
        if exchange:
            i, j = pl.program_id(0), pl.program_id(1)
            at_end = (i == ni - 1) & (j == nj - 1)
            _host_exchange(exchange, refs[2:2 + n_x], refs[3 + n_x:3 + 2 * n_x], refs[3 + 2 * n_x:],
                           (i == 0) & (j == 0), at_end, at_end)
        o_ref[...] = lax.dot_general(a_ref[...], b_ref[...], TN_DIMS, preferred_element_type=F32)

    if split:
        b_spec = pl.BlockSpec((None, t, tn), lambda i, j: (j // per_half, 0, j % per_half))
    else:
        b_spec = pl.BlockSpec((t, tn), lambda i, j: (0, j))
    outs = pl.pallas_call(
        body,
        name=name,
        grid=(ni, nj),
        in_specs=[pl.BlockSpec((t, tk), lambda i, j: (0, i)), b_spec] + [HBM_SPEC] * n_x,
        out_specs=[pl.BlockSpec((tk, tn), lambda i, j: (i, j))] + [HBM_SPEC] * n_x,
        out_shape=[jax.ShapeDtypeStruct((k, n), F32)] + (exchange.out_shape if exchange else []),
        scratch_shapes=exchange.scratch if exchange else [],
        compiler_params=_cparams(("arbitrary", "arbitrary") if exchange else ("parallel", "parallel")),
    )(a, b, *(exchange.ins if exchange else []))
    return (outs[0], outs[1:]) if exchange else outs[0]


A_SLOT = lambda h: h // A_GROUP
B_SLOT = lambda h: h % 2

PROJ_PARTS = (
    (OFF_QA, A_WIDTH, A_PAD_WIDTH, True, A_SLOT), (OFF_KA, A_KV_WIDTH, A_KV_WIDTH, True, None),
    (OFF_VA, A_KV_WIDTH, A_KV_WIDTH, True, None), (OFF_QB, B_WIDTH, B_WIDTH, True, None),
    (OFF_KB, B_WIDTH, B_PAD_WIDTH, True, B_SLOT), (OFF_VB, B_WIDTH, B_PAD_WIDTH, True, B_SLOT),
    (OFF_GA, D_MODEL, D_MODEL, False, None), (OFF_GB, D_MODEL, D_MODEL, False, None), (OFF_F, F_COLS, F_COLS, False, None),
)


def _head_tile(pair, head, slot):
    lane_slot = lax.broadcasted_iota(jnp.int32, pair.shape, 1) // HEAD_DIM
    moved = pair if head % 2 == slot else pltpu.roll(pair, HEAD_DIM, 1)
    return jnp.where(lane_slot == slot, moved, 0.0)


def _proj_fwd(h, norm_w, w_p):
    t, d = h.shape
    tm = _tile(t, 272)

    def body(h_ref, nw_ref, w_ref, u_ref, *part_refs):
        hh = h_ref[...]
        un = ((hh * _rms_scale(hh)) * nw_ref[...]).astype(BF16)
        u_ref[...] = un
        for (off, width, _, _, slot), p_ref in zip(PROJ_PARTS, part_refs):
            if slot is None:
                p_ref[...] = jnp.dot(un, w_ref[:, off:off + width], preferred_element_type=F32).astype(p_ref.dtype)
                continue
            for pair in range(width // LANES):
                x = jnp.dot(un, w_ref[:, off + pair * LANES:off + (pair + 1) * LANES], preferred_element_type=F32)
                for head in (2 * pair, 2 * pair + 1):
                    p_ref[:, head * LANES:(head + 1) * LANES] = _head_tile(x, head, slot(head)).astype(p_ref.dtype)

    row = lambda w: pl.BlockSpec((tm, w), lambda i: (i, 0))
    return pl.pallas_call(
        body,
        name="proj_fwd",
        grid=(t // tm,),
        in_specs=[row(d), pl.BlockSpec((1, d), lambda i: (0, 0)),
                  pl.BlockSpec(w_p.shape, lambda i: (0, 0), pipeline_mode=pl.Buffered(1))],
        out_specs=[row(d)] + [row(width) for _, _, width, _, _ in PROJ_PARTS],
        out_shape=[jax.ShapeDtypeStruct((t, d), BF16)]
        + [jax.ShapeDtypeStruct((t, width), BF16 if is_bf else F32) for _, _, width, is_bf, _ in PROJ_PARTS],
        compiler_params=_cparams(("parallel",)),
    )(h, norm_w, w_p)


def _proj_bwd(dh_out, h, norm_w, dproj, w_p, exchange=None):
    t, d = h.shape
    n = w_p.shape[1]
    tm = _tile(t, 272)
    ni = t // tm
    n_x = len(exchange.ins) if exchange else 0

    def body(*refs):
        dho_ref, h_ref, nw_ref, dp_ref, w_ref = refs[:5]
        dhin_ref, dnw_ref = refs[5 + n_x:7 + n_x]
        if exchange:
            i = pl.program_id(0)
            _host_exchange(exchange, refs[5:5 + n_x], refs[7 + n_x:7 + 2 * n_x], refs[7 + 2 * n_x:],
                           i == 0, i == ni - 1, i == ni - 1)
        dn = lax.dot_general(dp_ref[...], w_ref[...], NT_DIMS, preferred_element_type=F32)
        dh, dw = _rms_bwd(dn, h_ref[...], nw_ref[...])
        dhin_ref[...] = dho_ref[...] + dh
        dnw_ref[0] = dw

    row = lambda w: pl.BlockSpec((tm, w), lambda i: (i, 0))
    outs = pl.pallas_call(
        body,
        name="proj_bwd",
        grid=(ni,),
        in_specs=[row(d), row(d), pl.BlockSpec((1, d), lambda i: (0, 0)), row(n),
                  pl.BlockSpec(w_p.shape, lambda i: (0, 0), pipeline_mode=pl.Buffered(1))] + [HBM_SPEC] * n_x,
        out_specs=[row(d), pl.BlockSpec((1, 1, d), lambda i: (i, 0, 0))] + [HBM_SPEC] * n_x,
        out_shape=[jax.ShapeDtypeStruct((t, d), F32), jax.ShapeDtypeStruct((ni, 1, d), F32)]
        + (exchange.out_shape if exchange else []),
        scratch_shapes=exchange.scratch if exchange else [],
        compiler_params=_cparams(("arbitrary",) if exchange else ("parallel",)),
    )(dh_out, h, norm_w, dproj, w_p, *(exchange.ins if exchange else []))
    return outs[:2], outs[2:]


def _merge_fwd(h, oa, ob, ga, gb, wa, wb, wo):
    t, d = h.shape
    tm = _tile(t, 544)

    def body(h_ref, oa_ref, ob_ref, ga_ref, gb_ref, wa_ref, wb_ref, wo_ref, hout_ref, mix_ref):
        ya = jnp.dot(oa_ref[...], wa_ref[...], preferred_element_type=F32)
        yb = jnp.dot(ob_ref[...], wb_ref[...], preferred_element_type=F32)
        mixed = (jax.nn.sigmoid(ga_ref[...]) * ya + jax.nn.sigmoid(gb_ref[...]) * yb).astype(BF16)
        mix_ref[...] = mixed
        hout_ref[...] = h_ref[...] + jnp.dot(mixed, wo_ref[...], preferred_element_type=F32)

    row = lambda w: pl.BlockSpec((tm, w), lambda i: (i, 0))
    full = lambda a: pl.BlockSpec(a.shape, lambda i: (0, 0))
    return pl.pallas_call(
        body,
        name="merge_fwd",
        grid=(t // tm,),
        in_specs=[row(d), row(oa.shape[1]), row(ob.shape[1]), row(d), row(d), full(wa), full(wb), full(wo)],
        out_specs=[row(d), row(d)],
        out_shape=[jax.ShapeDtypeStruct((t, d), F32), jax.ShapeDtypeStruct((t, d), BF16)],
        compiler_params=_cparams(("parallel",)),
    )(h, oa, ob, ga, gb, wa, wb, wo)


def _merge_bwd(dh, oa, ob, ga, gb, wa, wb, wo, exchange=None):
    t, d = dh.shape
    tm = _tile(t, 544)
    ni = t // tm
    n_x = len(exchange.ins) if exchange else 0

    def body(*refs):
        dh_ref, oa_ref, ob_ref, ga_ref, gb_ref, wa_ref, wb_ref, wo_ref = refs[:8]
        dya_ref, dyb_ref, doa_ref, dob_ref, dga_ref, dgb_ref, dhb_ref = refs[8 + n_x:15 + n_x]
        if exchange:
            i = pl.program_id(0)
            _host_exchange(exchange, refs[8:8 + n_x], refs[15 + n_x:15 + 2 * n_x], refs[15 + 2 * n_x:],
                           i == 0, i == ni - 1, i == ni - 1)
        dhb = dh_ref[...].astype(BF16)
        dhb_ref[...] = dhb
        dmix = lax.dot_general(dhb, wo_ref[...], NT_DIMS, preferred_element_type=F32)
        for o_ref, g_ref, w_ref, dy_ref, do_ref, dg_ref in (
                (oa_ref, ga_ref, wa_ref, dya_ref, doa_ref, dga_ref),
                (ob_ref, gb_ref, wb_ref, dyb_ref, dob_ref, dgb_ref)):
            y = jnp.dot(o_ref[...], w_ref[...], preferred_element_type=F32)
            s = jax.nn.sigmoid(g_ref[...])
            dy = (dmix * s).astype(BF16)
            dy_ref[...] = dy
            dg_ref[...] = ((dmix * y) * (s * (1.0 - s))).astype(BF16)
            do_ref[...] = lax.dot_general(dy, w_ref[...], NT_DIMS, preferred_element_type=F32).astype(BF16)

    row = lambda w: pl.BlockSpec((tm, w), lambda i: (i, 0))
    full = lambda a: pl.BlockSpec(a.shape, lambda i: (0, 0))
    wa_w, wb_w = oa.shape[1], ob.shape[1]
    outs = pl.pallas_call(
        body,
        name="merge_bwd",
        grid=(ni,),
        in_specs=[row(d), row(wa_w), row(wb_w), row(d), row(d), full(wa), full(wb), full(wo)] + [HBM_SPEC] * n_x,
        out_specs=[row(d), row(d), row(wa_w), row(wb_w), row(d), row(d), row(d)] + [HBM_SPEC] * n_x,
        out_shape=[
            jax.ShapeDtypeStruct((t, d), BF16), jax.ShapeDtypeStruct((t, d), BF16),
            jax.ShapeDtypeStruct((t, wa_w), BF16), jax.ShapeDtypeStruct((t, wb_w), BF16),
            jax.ShapeDtypeStruct((t, d), BF16), jax.ShapeDtypeStruct((t, d), BF16),
            jax.ShapeDtypeStruct((t, d), BF16),
        ] + (exchange.out_shape if exchange else []),
        scratch_shapes=exchange.scratch if exchange else [],
        compiler_params=_cparams(("arbitrary",) if exchange else ("parallel",)),
    )(dh, oa, ob, ga, gb, wa, wb, wo, *(exchange.ins if exchange else []))
    return outs[:7], outs[7:]


def _tri_dot(tri, x):
    hi = x.astype(BF16)
    r1 = x - hi.astype(F32)
    mid = r1.astype(BF16)
    lo = (r1 - mid.astype(F32)).astype(BF16)
    return (jnp.dot(tri, hi, preferred_element_type=F32)
            + jnp.dot(tri, mid, preferred_element_type=F32)
            + jnp.dot(tri, lo, preferred_element_type=F32))


def _forget_cumsum(f_logit, b_pad, nb):
    t, w = f_logit.shape
    bsz = t // (nb * BLOCK)

    def body(f_ref, b_ref, c_ref, carry):
        n = pl.program_id(1)

        @pl.when(n == 0)
        def _():
            carry[...] = jnp.zeros_like(carry)

        x = jax.nn.log_sigmoid(f_ref[...] + b_ref[...])
        rows = lax.broadcasted_iota(jnp.int32, (BLOCK, BLOCK), 0)
        cols = lax.broadcasted_iota(jnp.int32, (BLOCK, BLOCK), 1)
        tri = (cols <= rows).astype(BF16)
        c = _tri_dot(tri, x) + carry[...]
        c_ref[...] = c
        carry[...] = c[BLOCK - 1:BLOCK, :]

    return pl.pallas_call(
        body,
        name="forget_cumsum",
        grid=(bsz, nb),
        in_specs=[pl.BlockSpec((BLOCK, w), lambda b, n: (b * nb + n, 0)),
                  pl.BlockSpec((1, w), lambda b, n: (0, 0))],
        out_specs=pl.BlockSpec((BLOCK, w), lambda b, n: (b * nb + n, 0)),
        out_shape=jax.ShapeDtypeStruct((t, w), F32),
        scratch_shapes=[pltpu.VMEM((1, w), F32)],
        compiler_params=_cparams(("parallel", "arbitrary")),
    )(f_logit, b_pad)


def _forget_cumsum_bwd(dc, f_logit, b_pad, nb):
    t, w = f_logit.shape
    bsz = t // (nb * BLOCK)

    def body(dc_ref, f_ref, b_ref, df_ref, db_ref, carry):
        n = pl.program_id(1)

        @pl.when(n == 0)
        def _():
            carry[...] = jnp.zeros_like(carry)
            db_ref[...] = jnp.zeros_like(db_ref)

        rows = lax.broadcasted_iota(jnp.int32, (BLOCK, BLOCK), 0)
        cols = lax.broadcasted_iota(jnp.int32, (BLOCK, BLOCK), 1)
        tri = (cols >= rows).astype(BF16)
        dlf = _tri_dot(tri, dc_ref[...]) + carry[...]
        carry[...] = dlf[0:1, :]
        df = dlf * jax.nn.sigmoid(-(f_ref[...] + b_ref[...]))
        df_ref[...] = df.astype(BF16)
        db_ref[0] += jnp.sum(df, axis=0, keepdims=True)

    rev = lambda b, n: (b * nb + (nb - 1 - n), 0)
    return pl.pallas_call(
        body,
        name="forget_cumsum_bwd",
        grid=(bsz, nb),
        in_specs=[pl.BlockSpec((BLOCK, w), rev),
                  pl.BlockSpec((BLOCK, w), rev),
                  pl.BlockSpec((1, w), lambda b, n: (0, 0))],
        out_specs=[pl.BlockSpec((BLOCK, w), rev),
                   pl.BlockSpec((1, 1, w), lambda b, n: (b, 0, 0))],
        out_shape=[jax.ShapeDtypeStruct((t, w), BF16), jax.ShapeDtypeStruct((bsz, 1, w), F32)],
        scratch_shapes=[pltpu.VMEM((1, w), F32)],
        compiler_params=_cparams(("parallel", "arbitrary")),
    )(dc, f_logit, b_pad)


GROUP_ROWS = A_GROUP * BLOCK


def _stack_heads(ref, g):
    return jnp.concatenate([ref[:, (A_GROUP * g + i) * LANES:(A_GROUP * g + i + 1) * LANES] for i in range(A_GROUP)],
                           axis=0)


def _unstack_heads(ref, g, x):
    for i in range(A_GROUP):
        ref[:, (A_GROUP * g + i) * LANES:(A_GROUP * g + i + 1) * LANES] = x[i * BLOCK:(i + 1) * BLOCK].astype(ref.dtype)


def _swa_logits(qk, slope, n):
    qi = lax.broadcasted_iota(jnp.int32, (GROUP_ROWS, BLOCK), 0) & (BLOCK - 1)
    kj = lax.broadcasted_iota(jnp.int32, (GROUP_ROWS, BLOCK), 1)
    s_all = qk * SCALE
    out = []
    for i, (dist, ok) in enumerate((
            (n * BLOCK + qi - kj, (kj >= N_PAD) & (n * BLOCK + qi - kj >= 0)),
            (BLOCK + qi - kj, (kj > qi) & (n >= 2)),
            (qi - kj, (kj <= qi) & (n >= 1)))):
        s = s_all[:, i * BLOCK:(i + 1) * BLOCK] - slope * dist.astype(F32)
        out.append(jnp.where(ok, s, NEG))
    return out


def _three_blocks(m_ref, p_ref, c_ref):
    return jnp.concatenate([m_ref[...], p_ref[...], c_ref[...]], axis=0)


def _swa_specs(bsz, nb):
    qspec = pl.BlockSpec((bsz, BLOCK, A_PAD_WIDTH), lambda n: (0, n, 0))
    kv_m = pl.BlockSpec((bsz, BLOCK, LANES), lambda n: (0, 0, 0))
    kv_p = pl.BlockSpec((bsz, BLOCK, LANES), lambda n: (0, jnp.maximum(n - 1, 0), 0))
    kv_c = pl.BlockSpec((bsz, BLOCK, LANES), lambda n: (0, n, 0))
    rowspec = pl.BlockSpec((A_KV_HEADS, GROUP_ROWS, 1), lambda n: (0, 0, 0))
    lsespec = pl.BlockSpec((bsz, 1, A_KV_HEADS, GROUP_ROWS, 1), lambda n: (0, n, 0, 0, 0))
    return qspec, kv_m, kv_p, kv_c, rowspec, lsespec


def _swa_fwd(q, k, v, sink_rows, slope_rows, nb):
    t = q.shape[0]
    l = nb * BLOCK
    bsz = t // l

    def body(q_ref, km_ref, kp_ref, kc_ref, vm_ref, vp_ref, vc_ref, sink_ref, slope_ref, o_ref, lse_ref):
        n = pl.program_id(0)
        lane_group = lax.broadcasted_iota(jnp.int32, (GROUP_ROWS, LANES), 1) // HEAD_DIM
        products = {}
        for b in range(bsz):
            keys = _three_blocks(km_ref.at[b], kp_ref.at[b], kc_ref.at[b])
            for g in range(A_KV_HEADS):
                products[b, g] = lax.dot_general(_stack_heads(q_ref.at[b], g), keys, NT_DIMS,
                                                 preferred_element_type=F32)
        for b in range(bsz):
            values = _three_blocks(vm_ref.at[b], vp_ref.at[b], vc_ref.at[b])
            for g in range(A_KV_HEADS):
                sink = sink_ref[g]
                s_m, s_p, s_c = _swa_logits(products[b, g], slope_ref[g], n)
                m = jnp.maximum(jnp.max(jnp.maximum(jnp.maximum(s_m, s_p), s_c), axis=-1, keepdims=True), sink)
                m_wide = jnp.broadcast_to(m, (GROUP_ROWS, BLOCK))
                e_m = jnp.exp(s_m - m_wide)
                e_p = jnp.exp(s_p - m_wide)
                e_c = jnp.exp(s_c - m_wide)
                z = jnp.sum((e_m + e_p) + e_c, axis=-1, keepdims=True) + jnp.exp(sink - m)
                inv = jnp.broadcast_to(1.0 / z, (GROUP_ROWS, BLOCK))
                probs = jnp.concatenate([(e_m * inv).astype(BF16), (e_p * inv).astype(BF16),
                                         (e_c * inv).astype(BF16)], axis=1)
                o = jnp.dot(probs, values, preferred_element_type=F32)
                _unstack_heads(o_ref.at[b], g, jnp.where(lane_group == g, o, 0.0))
                lse_ref[b, 0, g] = m + jnp.log(z)

    qspec, kv_m, kv_p, kv_c, rowspec, lsespec = _swa_specs(bsz, nb)
    by_example = lambda a: a.reshape(bsz, l, a.shape[1])
    q3, k3, v3 = by_example(q), by_example(k), by_example(v)
    o, lse = pl.pallas_call(
        body,
        name="swa_fwd",
        grid=(nb,),
        in_specs=[qspec, kv_m, kv_p, kv_c, kv_m, kv_p, kv_c, rowspec, rowspec],
        out_specs=[qspec, lsespec],
        out_shape=[jax.ShapeDtypeStruct((bsz, l, A_PAD_WIDTH), BF16),
                   jax.ShapeDtypeStruct((bsz, nb, A_KV_HEADS, GROUP_ROWS, 1), F32)],
        compiler_params=_cparams(("arbitrary",)),
    )(q3, k3, k3, k3, v3, v3, v3, sink_rows, slope_rows)
    return o.reshape(t, A_PAD_WIDTH), lse


def _swa_bwd(q, k, v, do, lse, sink_rows, slope_rows, nb):
    t = q.shape[0]
    l = nb * BLOCK
    bsz = t // l

    def body(q_ref, km_ref, kp_ref, kc_ref, vm_ref, vp_ref, vc_ref, do_ref, lse_ref, sink_ref, slope_ref,
             dq_ref, dk_ref, dv_ref, dsink_ref, dk_acc, dv_acc):
        n = pl.program_id(0)

        @pl.when(n == 0)
        def _():
            dk_acc[...] = jnp.zeros_like(dk_acc)
            dv_acc[...] = jnp.zeros_like(dv_acc)
            dsink_ref[...] = jnp.zeros_like(dsink_ref)

        first_half = lax.broadcasted_iota(jnp.int32, (BLOCK, LANES), 1) < HEAD_DIM
        prev = jnp.maximum(n - 1, 0)
        products = {}
        for b in range(bsz):
            keys = _three_blocks(km_ref.at[b], kp_ref.at[b], kc_ref.at[b])
            values = _three_blocks(vm_ref.at[b], vp_ref.at[b], vc_ref.at[b])
            for g in range(A_KV_HEADS):
                products[b, g] = (
                    lax.dot_general(_stack_heads(q_ref.at[b], g), keys, NT_DIMS, preferred_element_type=F32),
                    lax.dot_general(_stack_heads(do_ref.at[b], g), values, NT_DIMS, preferred_element_type=F32))
        for b in range(bsz):
            keys = _three_blocks(km_ref.at[b], kp_ref.at[b], kc_ref.at[b])
            for g in range(A_KV_HEADS):
                qq = _stack_heads(q_ref.at[b], g)
                dob = _stack_heads(do_ref.at[b], g)
                lse_g = lse_ref[b, 0, g]
                lse_wide = jnp.broadcast_to(lse_g, (GROUP_ROWS, BLOCK))
                qk, dp_all = products[b, g]
                probs = [jnp.exp(s - lse_wide) for s in _swa_logits(qk, slope_ref[g], n)]
                dps = [dp_all[:, i * BLOCK:(i + 1) * BLOCK] for i in range(3)]
                delta = jnp.sum((probs[0] * dps[0] + probs[1] * dps[1]) + probs[2] * dps[2], axis=-1, keepdims=True)
                delta_wide = jnp.broadcast_to(delta, (GROUP_ROWS, BLOCK))
                ds = jnp.concatenate([(p * (dp - delta_wide)).astype(BF16) for p, dp in zip(probs, dps)], axis=1)
                pb = jnp.concatenate([p.astype(BF16) for p in probs], axis=1)
                dq = jnp.dot(ds, keys, preferred_element_type=F32) * SCALE
                dk_all = lax.dot_general(ds, qq, TN_DIMS, preferred_element_type=F32) * SCALE
                dv_all = lax.dot_general(pb, dob, TN_DIMS, preferred_element_type=F32)
                for i, start in enumerate((0, prev * BLOCK, n * BLOCK)):
                    rows = pl.ds(pl.multiple_of(start, BLOCK), BLOCK)
                    dk_acc[b, rows, :] += dk_all[i * BLOCK:(i + 1) * BLOCK]
                    dv_acc[b, rows, :] += dv_all[i * BLOCK:(i + 1) * BLOCK]
                for pair in range(A_GROUP // 2):
                    even = dq[2 * pair * BLOCK:(2 * pair + 1) * BLOCK]
                    odd = dq[(2 * pair + 1) * BLOCK:(2 * pair + 2) * BLOCK]
                    left = even if g == 0 else pltpu.roll(even, HEAD_DIM, 1)
                    right = pltpu.roll(odd, HEAD_DIM, 1) if g == 0 else odd
                    tile = (A_GROUP // 2) * g + pair
                    dq_ref[b, :, tile * LANES:(tile + 1) * LANES] = jnp.where(first_half, left, right).astype(BF16)
                dsink_ref[b, g] += -(jnp.exp(sink_ref[g] - lse_g) * delta)

        @pl.when(n == nb - 1)
        def _():
            dk_ref[...] = dk_acc[...].astype(BF16)
            dv_ref[...] = dv_acc[...].astype(BF16)

    qspec, kv_m, kv_p, kv_c, rowspec, lsespec = _swa_specs(bsz, nb)
    kv_all = pl.BlockSpec((bsz, l, LANES), lambda n: (0, 0, 0))
    by_example = lambda a: a.reshape(bsz, l, a.shape[1])
    q3, k3, v3 = by_example(q), by_example(k), by_example(v)
    dq, dk, dv, dsink = pl.pallas_call(
        body,
        name="swa_bwd",
        grid=(nb,),
        in_specs=[qspec, kv_m, kv_p, kv_c, kv_m, kv_p, kv_c, qspec, lsespec, rowspec, rowspec],
        out_specs=[pl.BlockSpec((bsz, BLOCK, A_WIDTH), lambda n: (0, n, 0)), kv_all, kv_all,
                   pl.BlockSpec((bsz, A_KV_HEADS, GROUP_ROWS, 1), lambda n: (0, 0, 0, 0))],
        out_shape=[jax.ShapeDtypeStruct((bsz, l, A_WIDTH), BF16),
                   jax.ShapeDtypeStruct((bsz, l, LANES), BF16),
                   jax.ShapeDtypeStruct((bsz, l, LANES), BF16),
                   jax.ShapeDtypeStruct((bsz, A_KV_HEADS, GROUP_ROWS, 1), F32)],
        scratch_shapes=[pltpu.VMEM((bsz, l, LANES), F32), pltpu.VMEM((bsz, l, LANES), F32)],
        compiler_params=_cparams(("arbitrary",)),
    )(q3, k3, k3, k3, v3, v3, v3, by_example(do), lse, sink_rows, slope_rows)
    return dq.reshape(t, A_WIDTH), dk.reshape(t, LANES), dv.reshape(t, LANES), dsink


CHUNK = KEY_BLOCKS * BLOCK


def _fox_chunk(qb, ci):
    sb = jnp.maximum(jnp.minimum(KEY_BLOCKS * ci, qb + 1 - KEY_BLOCKS), 0)
    lo = jnp.maximum(ci * CHUNK, N_PAD)
    return sb, lo, pl.ds(pl.multiple_of(sb * BLOCK, BLOCK), CHUNK)


def _fox_logits(s_ref, cr_ref, e, j, sb, lo, qb):
    lane = lax.broadcasted_iota(jnp.int32, (BLOCK, BLOCK), 1)
    ahead = lane - lax.broadcasted_iota(jnp.int32, (BLOCK, BLOCK), 0)
    first = (sb + j) * BLOCK
    s = s_ref[e, :, j * BLOCK:(j + 1) * BLOCK] - cr_ref[e, sb + j]
    return jnp.where((ahead <= qb * BLOCK - first) & (lane >= lo - first), s, NEG)


FOX_PAIRS = 4
FOX_HEADS = 2 * FOX_PAIRS
FOX_STEPS = B_PAIRS // FOX_PAIRS


def _fox_specs(nb):
    l = nb * BLOCK
    q_spec = pl.BlockSpec((BLOCK, FOX_PAIRS * LANES), lambda b, p, i: (b * nb + i, p))
    kv_spec = pl.BlockSpec((l, FOX_HEADS * LANES), lambda b, p, i: (b, p))
    cc_spec = pl.BlockSpec((FOX_HEADS, BLOCK, 1), lambda b, p, i: (b * FOX_STEPS + p, i, 0))
    cr_spec = pl.BlockSpec((FOX_HEADS, nb, 1, BLOCK), lambda b, p, i: (b * FOX_STEPS + p, 0, 0, 0))
    return q_spec, kv_spec, cc_spec, cr_spec


def _fox_fwd(q, k, v, c_row, nb, exchange=None):
    t = q.shape[0]
    bsz = t // (nb * BLOCK)
    assert nb >= KEY_BLOCKS

    n_x = len(exchange.ins) if exchange else 0

    def body(*refs):
        q_ref, k_ref, v_ref, cr_ref = refs[:4]
        o_ref, ox_ref, lse_ref = refs[4 + n_x:7 + n_x]
        s_scr, hi_scr, lo_scr = refs[7 + 2 * n_x:10 + 2 * n_x]
        qb = pl.program_id(2)
        if exchange:
            first = (pl.program_id(0) == 0) & (pl.program_id(1) == 0)
            last = (pl.program_id(0) == bsz - 1) & (pl.program_id(1) == FOX_STEPS - 1)
            _host_exchange(exchange, refs[4:4 + n_x], refs[7 + n_x:7 + 2 * n_x], refs[10 + 2 * n_x:],
                           first & (qb == 0), first & (qb == 2 * nb // 3), last & (qb == nb - 1),
                           late=last & (qb == 0))
        qs = [q_ref[:, a * LANES:(a + 1) * LANES] * SCALE for a in range(FOX_PAIRS)]
        first_half = lax.broadcasted_iota(jnp.int32, (BLOCK, LANES), 1) < HEAD_DIM

        def step(ci, carry):
            stats, accs = carry[:2 * FOX_HEADS], carry[2 * FOX_HEADS:]
            sb, lo, krows = _fox_chunk(qb, ci)
            for e in range(FOX_HEADS):
                s_scr[e] = lax.dot_general(qs[e // 2], k_ref[krows, e * LANES:(e + 1) * LANES], NT_DIMS,
                                           preferred_element_type=F32)
            new_stats, new_accs = [], []
            for a in range(FOX_PAIRS):
                alphas = []
                pv = jnp.zeros((BLOCK, LANES), F32)
                pv_lo = jnp.zeros((BLOCK, LANES), F32)
                for e in (2 * a, 2 * a + 1):
                    m, z = stats[2 * e], stats[2 * e + 1]
                    tile = slice(e * LANES, (e + 1) * LANES)
                    top = None
                    for j in range(KEY_BLOCKS):
                        s = _fox_logits(s_scr, cr_ref, e, j, sb, lo, qb)
                        s_scr[e, :, j * BLOCK:(j + 1) * BLOCK] = s
                        top = s if top is None else jnp.maximum(top, s)
                    m_new = jnp.maximum(m, jnp.max(top, axis=-1, keepdims=True))
                    alpha = jnp.exp(m - m_new)
                    m_wide = jnp.broadcast_to(m_new, (BLOCK, BLOCK))
                    total = None
                    for j in range(KEY_BLOCKS):
                        cols = slice(j * BLOCK, (j + 1) * BLOCK)
                        p = jnp.exp(s_scr[e, :, cols] - m_wide)
                        total = p if total is None else total + p
                        hi = p.astype(BF16)
                        hi_scr[e, :, cols] = hi
                        lo_scr[e, :, cols] = (p - hi.astype(F32)).astype(BF16)
                    z = alpha * z + jnp.sum(total, axis=-1, keepdims=True)
                    vv = v_ref[krows, tile]
                    pv = pv + jnp.dot(hi_scr[e], vv, preferred_element_type=F32)
                    pv_lo = pv_lo + jnp.dot(lo_scr[e], vv, preferred_element_type=F32)
                    new_stats += [m_new, z]
                    alphas.append(alpha)
                alpha = jnp.where(first_half, alphas[0], alphas[1])
                new_accs += [alpha * accs[2 * a] + pv, alpha * accs[2 * a + 1] + pv_lo]
            return (*new_stats, *new_accs)

        col = lambda val: jnp.full((BLOCK, 1), val, F32)
        done = lax.fori_loop(
            0, (qb + KEY_BLOCKS) // KEY_BLOCKS, step,
            (col(NEG), col(0.0)) * FOX_HEADS + (jnp.zeros((BLOCK, LANES), F32),) * (2 * FOX_PAIRS))
        for a in range(FOX_PAIRS):
            m0, z0, m1, z1 = done[4 * a:4 * a + 4]
            acc, acc_lo = done[2 * FOX_HEADS + 2 * a:2 * FOX_HEADS + 2 * a + 2]
            inv = 1.0 / jnp.where(first_half, z0, z1)
            tile = slice(a * LANES, (a + 1) * LANES)
            o_ref[:, tile] = (acc * inv).astype(BF16)
            ox_ref[:, tile] = (acc + acc_lo) * inv
            lse_ref[2 * a] = m0 + jnp.log(z0)
            lse_ref[2 * a + 1] = m1 + jnp.log(z1)

    q_spec, kv_spec, cc_spec, cr_spec = _fox_specs(nb)
    outs = pl.pallas_call(
        body,
        name="fox_fwd",
        grid=(bsz, FOX_STEPS, nb),
        in_specs=[q_spec, kv_spec, kv_spec, cr_spec] + [HBM_SPEC] * n_x,
        out_specs=[q_spec, q_spec, cc_spec] + [HBM_SPEC] * n_x,
        out_shape=[jax.ShapeDtypeStruct((t, B_WIDTH), BF16), jax.ShapeDtypeStruct((t, B_WIDTH), F32),
                   jax.ShapeDtypeStruct((bsz * B_HEADS, nb * BLOCK, 1), F32)] + (exchange.out_shape if exchange else []),
        scratch_shapes=[pltpu.VMEM((FOX_HEADS, BLOCK, CHUNK), F32), pltpu.VMEM((FOX_HEADS, BLOCK, CHUNK), BF16),
                        pltpu.VMEM((FOX_HEADS, BLOCK, CHUNK), BF16)] + (exchange.scratch if exchange else []),
        compiler_params=_cparams(("arbitrary",) * 3 if exchange else ("parallel", "parallel", "arbitrary")),
    )(q, k, v, c_row, *(exchange.ins if exchange else []))
    return outs[:3], outs[3:]


def _fox_bwd(q, k, v, o_exact, do, lse, c_row, nb, exchange=None):
    t = q.shape[0]
    l = nb * BLOCK
    bsz = t // l

    n_x = len(exchange.ins) if exchange else 0

    def body(*refs):
        q_ref, k_ref, v_ref, ox_ref, do_ref, lse_ref, cr_ref = refs[:7]
        dq_ref, dk_ref, dv_ref, dc_ref = refs[7 + n_x:11 + n_x]
        dk_acc, dv_acc, s_scr, dp_scr, p_scr, ds_scr = refs[11 + 2 * n_x:17 + 2 * n_x]
        qb = pl.program_id(2)
        if exchange:
            first = (pl.program_id(0) == 0) & (pl.program_id(1) == 0)
            last = (pl.program_id(0) == bsz - 1) & (pl.program_id(1) == FOX_STEPS - 1)
            _host_exchange(exchange, refs[7:7 + n_x], refs[11 + n_x:11 + 2 * n_x], refs[17 + 2 * n_x:],
                           first & (qb == 0), last & (qb == 0), last & (qb == nb - 1))

        @pl.when(qb == 0)
        def _():
            dk_acc[...] = jnp.zeros_like(dk_acc)
            dv_acc[...] = jnp.zeros_like(dv_acc)
            dc_ref[...] = jnp.zeros_like(dc_ref)

        top_half = lax.broadcasted_iota(jnp.int32, (LANES, BLOCK), 0) < HEAD_DIM
        pair_t = lambda x: jnp.concatenate([jnp.where(top_half, x.T, 0), jnp.where(top_half, 0, x.T)], axis=1)
        first_half = lax.broadcasted_iota(jnp.int32, (BLOCK, LANES), 1) < HEAD_DIM
        wide = lambda col: jnp.broadcast_to(col, (BLOCK, BLOCK))
        qs, dobs, qs_t, dob_t, deltas = [], [], [], [], []
        for a in range(FOX_PAIRS):
            tile = slice(a * LANES, (a + 1) * LANES)
            qs.append(q_ref[:, tile] * SCALE)
            dobs.append(do_ref[:, tile])
            qs_t.append(pair_t(qs[a]))
            dob_t.append(pair_t(dobs[a]))
            weighted = dobs[a].astype(F32) * ox_ref[:, tile]
            deltas += [wide(jnp.sum(jnp.where(first_half, weighted, 0.0), axis=-1, keepdims=True)),
                       wide(jnp.sum(jnp.where(first_half, 0.0, weighted), axis=-1, keepdims=True))]
        lses = [wide(lse_ref[e]) for e in range(FOX_HEADS)]

        def step(ci, dqs):
            sb, lo, krows = _fox_chunk(qb, ci)
            dqs = list(dqs)
            for e in range(FOX_HEADS):
                tile = slice(e * LANES, (e + 1) * LANES)
                s_scr[e] = lax.dot_general(qs[e // 2], k_ref[krows, tile], NT_DIMS, preferred_element_type=F32)
                dp_scr[e] = lax.dot_general(dobs[e // 2], v_ref[krows, tile], NT_DIMS, preferred_element_type=F32)
            for a in range(FOX_PAIRS):
                for e in (2 * a, 2 * a + 1):
                    tile = slice(e * LANES, (e + 1) * LANES)
                    kk = k_ref[krows, tile]
                    for j in range(KEY_BLOCKS):
                        cols = slice(j * BLOCK, (j + 1) * BLOCK)
                        p = jnp.exp(_fox_logits(s_scr, cr_ref, e, j, sb, lo, qb) - lses[e])
                        ds = p * (dp_scr[e, :, cols] - deltas[e])
                        dc_ref[e, sb + j] -= jnp.sum(ds, axis=0, keepdims=True)
                        p_scr[e, :, cols] = p.astype(BF16)
                        ds_scr[e, :, cols] = ds.astype(BF16)
                    dqs[a] = dqs[a] + jnp.dot(ds_scr[e], kk, preferred_element_type=F32)
                both = slice(2 * a, 2 * a + 2)
                dk_t = jnp.dot(qs_t[a], ds_scr[both].reshape(2 * BLOCK, CHUNK), preferred_element_type=F32)
                dv_t = jnp.dot(dob_t[a], p_scr[both].reshape(2 * BLOCK, CHUNK), preferred_element_type=F32)
                for j in range(KEY_BLOCKS):
                    cols = slice(j * BLOCK, (j + 1) * BLOCK)
                    dk_acc[a * nb + sb + j] += dk_t[:, cols]
                    dv_acc[a * nb + sb + j] += dv_t[:, cols]
            return tuple(dqs)

        dqs = lax.fori_loop(0, (qb + KEY_BLOCKS) // KEY_BLOCKS, step,
                            (jnp.zeros((BLOCK, LANES), F32),) * FOX_PAIRS)
        for a in range(FOX_PAIRS):
            dq_ref[:, a * LANES:(a + 1) * LANES] = (dqs[a] * SCALE).astype(BF16)

        @pl.when(qb == nb - 1)
        def _():
            for a in range(FOX_PAIRS):
                for kb in range(nb):
                    rows = slice(kb * BLOCK, (kb + 1) * BLOCK)
                    for acc, out_ref in ((dk_acc, dk_ref), (dv_acc, dv_ref)):
                        out_ref[rows, a * LANES:(a + 1) * LANES] = acc[a * nb + kb].T.astype(BF16)

    q_spec, kv_spec, cc_spec, cr_spec = _fox_specs(nb)
    dkv_spec = pl.BlockSpec((l, FOX_PAIRS * LANES), lambda b, p, i: (b, p))
    outs = pl.pallas_call(
        body,
        name="fox_bwd",
        grid=(bsz, FOX_STEPS, nb),
        in_specs=[q_spec, kv_spec, kv_spec, q_spec, q_spec, cc_spec, cr_spec] + [HBM_SPEC] * n_x,
        out_specs=[q_spec, dkv_spec, dkv_spec, cr_spec] + [HBM_SPEC] * n_x,
        out_shape=[jax.ShapeDtypeStruct((t, B_WIDTH), BF16), jax.ShapeDtypeStruct((t, B_WIDTH), BF16),
                   jax.ShapeDtypeStruct((t, B_WIDTH), BF16),
                   jax.ShapeDtypeStruct((bsz * B_HEADS, nb, 1, BLOCK), F32)] + (exchange.out_shape if exchange else []),
        scratch_shapes=[pltpu.VMEM((FOX_PAIRS * nb, LANES, BLOCK), F32), pltpu.VMEM((FOX_PAIRS * nb, LANES, BLOCK), F32),
                        pltpu.VMEM((FOX_HEADS, BLOCK, CHUNK), F32), pltpu.VMEM((FOX_HEADS, BLOCK, CHUNK), F32),
                        pltpu.VMEM((FOX_HEADS, BLOCK, CHUNK), BF16), pltpu.VMEM((FOX_HEADS, BLOCK, CHUNK), BF16)]
        + (exchange.scratch if exchange else []),
        compiler_params=_cparams(("arbitrary",) * 3 if exchange else ("parallel", "parallel", "arbitrary")),
    )(q, k, v, o_exact, do, lse, c_row, *(exchange.ins if exchange else []))
    return outs[:4], outs[4:]


def _loss_head(h, final_w, target):
    bsz, l, d = h.shape
    nb = l // BLOCK

    def body(h_ref, w_ref, t_ref, loss_ref, dh_ref, dw_ref):
        b = pl.program_id(0)
        n = pl.program_id(1)

        @pl.when((b == 0) & (n == 0))
        def _():
            loss_ref[...] = jnp.zeros_like(loss_ref)
            dw_ref[...] = jnp.zeros_like(dw_ref)

        @pl.when(n == 0)
        def _():
            dh_ref[...] = jnp.zeros_like(dh_ref)

        @pl.when(n > 0)
        def _():
            hh = h_ref[0]
            w = w_ref[...]
            r = _rms_scale(hh)
            err = (hh * r) * w - t_ref[0]
            loss_ref[...] += 0.5 * jnp.sum(jnp.mean(err * err, axis=-1, keepdims=True), axis=0, keepdims=True)
            dy = err * (1.0 / d)
            dh, dw = _rms_bwd(dy, hh, w)
            dh_ref[0] = dh
            dw_ref[...] += dw

    return pl.pallas_call(
        body,
        name="loss_head",
        grid=(bsz, nb),
        in_specs=[
            pl.BlockSpec((1, BLOCK, d), lambda b, n: (b, n, 0)),
            pl.BlockSpec((1, d), lambda b, n: (0, 0)),
            pl.BlockSpec((1, BLOCK, d), lambda b, n: (b, jnp.maximum(n - 1, 0), 0)),
        ],
        out_specs=[
            pl.BlockSpec((1, 128), lambda b, n: (0, 0)),
            pl.BlockSpec((1, BLOCK, d), lambda b, n: (b, n, 0)),
            pl.BlockSpec((1, d), lambda b, n: (0, 0)),
        ],
        out_shape=[jax.ShapeDtypeStruct((1, 128), F32), jax.ShapeDtypeStruct((bsz, l, d), F32),
                   jax.ShapeDtypeStruct((1, d), F32)],
        compiler_params=_cparams(("arbitrary", "arbitrary")),
    )(h, final_w, target)


def _pad_tiles(w, src, heads, lane_slot, axis):
    pieces = []
    for h in range(heads):
        x = lax.slice_in_dim(w, src + HEAD_DIM * h, src + HEAD_DIM * (h + 1), axis=axis)
        z = jnp.zeros_like(x)
        pieces += [x, z] if lane_slot(h) == 0 else [z, x]
    return pieces


def _unpad_tiles(g, off, heads, lane_slot, axis):
    return [lax.slice_in_dim(g, off + LANES * h + HEAD_DIM * lane_slot(h),
                             off + LANES * h + HEAD_DIM * (lane_slot(h) + 1), axis=axis) for h in range(heads)]


def _layout_w_in(w):
    pad_f = jnp.zeros((w.shape[0], F_COLS - B_HEADS), w.dtype)
    return jnp.concatenate([w[:, :SRC_F], w[:, SRC_GA:], w[:, SRC_F:SRC_GA], pad_f], axis=1)


def _unlayout_w_in(g):
    return jnp.concatenate([g[:, :OFF_GA], g[:, OFF_F:OFF_F + B_HEADS], g[:, OFF_GA:OFF_F]], axis=1)


def _local_step(x, target, meta, norms, b_forget, sinks, w, comm=None):
    n1, nmix, n2, nfin = norms
    w1i, w1o = w[:2]
    bsz, seq, d = x.shape
    l = PREFIX + seq
    nb = l // BLOCK
    t = bsz * l

    h0 = jnp.concatenate([jnp.zeros((bsz, N_PAD, d), F32),
                          jnp.broadcast_to(meta[None], (bsz, N_META, d)), x], axis=1).reshape(t, d)

    if comm is None:
        (h1, g1, u1), _ = _ffn_fwd(h0, n1, w1i, w1o)
        w_in, wa, wb, wo, w2i, w2o = w[2:]
    else:
        (h1, g1, u1), gathered = _ffn_fwd(h0, n1, w1i, w1o, comm.gather(GATHER_PROJ))
        w_in, = comm.gathered(GATHER_PROJ, gathered)
    wp = _layout_w_in(w_in)
    un, qa, ka, va, qb, kb, vb, ga, gb, f_logit = _proj_fwd(h1, nmix, wp)
    b_pad = jnp.concatenate([b_forget, jnp.zeros((1, F_COLS - B_HEADS), F32)], axis=1)
    c = _forget_cumsum(f_logit, b_pad, nb)
    c_heads = c[:, :B_HEADS].reshape(bsz, l, B_HEADS).transpose(0, 2, 1).reshape(bsz * B_HEADS, l)
    c_row = c_heads.reshape(bsz * B_HEADS, nb, 1, BLOCK)

    slopes = jnp.exp2(-8.0 * jnp.arange(1, A_HEADS + 1, dtype=F32) / A_HEADS)
    slope_rows = jnp.repeat(slopes.reshape(A_KV_HEADS, A_GROUP), BLOCK, axis=1)[:, :, None]
    sink_rows = jnp.repeat(sinks.reshape(A_KV_HEADS, A_GROUP), BLOCK, axis=1)[:, :, None]

    oa, lse_a = _swa_fwd(qa, ka, va, sink_rows, slope_rows, nb)
    if comm is None:
        (ob, ob_exact, lse_b), _ = _fox_fwd(qb, kb, vb, c_row, nb)
    else:
        (ob, ob_exact, lse_b), gathered = _fox_fwd(qb, kb, vb, c_row, nb, comm.gather(GATHER_LATE))
        wa, wb, wo, w2i, w2o = comm.gathered(GATHER_LATE, gathered)
    wa_p = jnp.concatenate(_pad_tiles(wa, 0, A_HEADS, A_SLOT, 0), axis=0)
    h2, mixed = _merge_fwd(h1, oa, ob, ga, gb, wa_p, wb, wo)
    (h3, g2, u2), _ = _ffn_fwd(h2, n2, w2i, w2o)
    loss, dh3, d_nfin = _loss_head(h3.reshape(bsz, l, d), nfin, target)

    (dh2, n2b, a2, dgu2, df2, dn2_parts), _ = _ffn_bwd(dh3.reshape(t, d), h2, n2, g2, u2, w2i, w2o)
    g_w2o = _tn_matmul(a2, df2, "grad_ffn2_w_out")
    g_w2i = _tn_matmul(n2b, dgu2, "grad_ffn2_w_in")

    hosted = comm.swap("ffn2", dict(ffn2_w_in=g_w2i, ffn2_w_out=g_w2o)) if comm else None
    (dya, dyb, doa, dob, dga, dgb, dh2b), swapped = _merge_bwd(dh2, oa, ob, ga, gb, wa_p, wb, wo, hosted)
    g_wo = _tn_matmul(mixed, dh2b, "grad_w_out")
    g_wa = jnp.concatenate(_unpad_tiles(_tn_matmul(oa, dya, "grad_w_branch_a"), 0, A_HEADS, A_SLOT, 0), axis=0)
    g_wb = _tn_matmul(ob, dyb, "grad_w_branch_b")

    dqa, dka, dva, dsink_rows = _swa_bwd(qa, ka, va, doa, lse_a, sink_rows, slope_rows, nb)
    hosted = comm.scatter("ffn2", swapped) if comm else None
    (dqb, dkb, dvb, dc_row), pieces = _fox_bwd(qb, kb, vb, ob_exact, dob, lse_b, c_row, nb, hosted)
    if comm:
        comm.received("ffn2", pieces)
    dc = dc_row.reshape(bsz, B_HEADS, l).transpose(0, 2, 1).reshape(t, B_HEADS)
    dc = jnp.concatenate([dc, jnp.zeros((t, F_COLS - B_HEADS), F32)], axis=1)
    df_logit, db_parts = _forget_cumsum_bwd(dc, f_logit, b_pad, nb)

    dproj = jnp.concatenate([dqa, dka, dva, dqb, dkb, dvb, dga, dgb, df_logit], axis=1)
    g_win = _unlayout_w_in(_tn_matmul(un, dproj, "grad_w_in"))
    hosted = comm.swap("mixer", dict(w_in=g_win, w_branch_a=g_wa, w_branch_b=g_wb, w_out=g_wo)) if comm else None
    (dh1, dnmix_parts), swapped = _proj_bwd(dh2, h1, nmix, dproj, wp, hosted)
    hosted = comm.scatter("mixer", swapped) if comm else None
    (dh0, n1b, a1, dgu1, df1, dn1_parts), pieces = _ffn_bwd(dh1, h0, n1, g1, u1, w1i, w1o, hosted)
    dh0 = dh0.reshape(bsz, l, d)
    grad_x = dh0[:, PREFIX:]
    small = dict(
        meta_tokens=jnp.sum(dh0[:, N_PAD:PREFIX], axis=0),
        ffn1_norm=jnp.sum(dn1_parts, axis=0),
        mix_norm=jnp.sum(dnmix_parts, axis=0),
        ffn2_norm=jnp.sum(dn2_parts, axis=0),
        final_norm=d_nfin,
        b_forget=jnp.sum(db_parts, axis=0)[:, :B_HEADS],
        attn_sinks=jnp.sum(dsink_rows.reshape(bsz, A_HEADS, BLOCK), axis=(0, 2)).reshape(1, A_HEADS),
    )
    if comm is None:
        g_w1o = _tn_matmul(a1, df1, "grad_ffn1_w_out")
        g_w1i = _tn_matmul(n1b, dgu1, "grad_ffn1_w_in")
    else:
        comm.received("mixer", pieces)
        g_w1o, gathered = _tn_matmul(a1, df1, "grad_ffn1_w_out", comm.small_gather(loss, small))
        comm.small_gathered(gathered)
        swapped = _run_exchange(comm.swap("ffn1_out", dict(ffn1_w_out=g_w1o)), "exchange_halves_ffn1_out")
        g_w1i, pieces = _tn_matmul(n1b, dgu1, "grad_ffn1_w_in", comm.scatter("ffn1_out", swapped))
        comm.received("ffn1_out", pieces)
    big = dict(ffn1_w_in=g_w1i, ffn1_w_out=g_w1o, w_in=g_win, w_branch_a=g_wa, w_branch_b=g_wb,
               w_out=g_wo, ffn2_w_in=g_w2i, ffn2_w_out=g_w2o)
    return loss, grad_x, small, big


BIG = (
    ("ffn1_w_in", (D_MODEL, 5632), 1),
    ("ffn1_w_out", (2816, D_MODEL), 0),
    ("w_in", (D_MODEL, W_IN_COLS), 1),
    ("w_branch_a", (A_WIDTH, D_MODEL), 1),
    ("w_branch_b", (B_WIDTH, D_MODEL), 1),
    ("w_out", (D_MODEL, D_MODEL), 0),
    ("ffn2_w_in", (D_MODEL, 5632), 1),
    ("ffn2_w_out", (2816, D_MODEL), 0),
)
STACKED = "w_in"


def _coords():
    return lax.axis_index("x"), lax.axis_index("y"), lax.axis_index("c")


def _other_chips(x, y):
    return ((1 - x, y), (x, 1 - y), (1 - x, 1 - y))


def _chip_part(ref, name, shape, axis, k):
    if name == STACKED:
        return ref.at[k]
    size = shape[axis] // N_CHIPS
    start = pl.multiple_of(k * size, size)
    return ref.at[pl.ds(start, size), :] if axis == 0 else ref.at[:, pl.ds(start, size)]


def _full_shape(name, shape):
    return (N_CHIPS, shape[0], shape[1] // N_CHIPS) if name == STACKED else shape


class _Exchange:
    def __init__(self, ins, out_shape, n_sems, ops):
        self.ins, self.out_shape, self.n_sems, self.ops = list(ins), list(out_shape), n_sems, ops

    @property
    def scratch(self):
        return [pltpu.SemaphoreType.DMA((self.n_sems,)), pltpu.SemaphoreType.DMA((self.n_sems,))]


SEMS_PER_GATHER = 9


def _gather_exchange(shards, table):
    n = len(table)
    x_nbr, y_nbr, diagonal = 0, 1, 2

    def ops(ins, outs, send_sems, recv_sems):
        x, y, c = _coords()
        mine = 2 * x + y
        sibling = (x, y, 1 - c)
        chips = _other_chips(x, y)
        slots = [2 * chip[0] + chip[1] for chip in chips]

        def part(i, k):
            name, shape, axis = table[i][:3]
            return _chip_part(outs[i], name, shape, axis, k)

        def half(ref, h):
            rows = ref.shape[0] // 2
            return ref.at[pl.ds(pl.multiple_of(h * rows, rows), rows), :]

        def remote(i, sem, src, dst, device):
            sem = SEMS_PER_GATHER * i + sem
            return pltpu.make_async_remote_copy(src, dst, send_sems.at[sem], recv_sems.at[sem],
                                                device_id=device, device_id_type=MESH_ID)

        def own(i):
            return remote(i, 0, ins[i], part(i, mine), sibling)

        def fetch(i, j, slot):
            if table[i][4]:
                src, dst = half(ins[i], c), half(part(i, slot), c)
            else:
                src, dst = ins[i], part(i, slot)
            return remote(i, 1 + j, src, dst, (chips[j][0], chips[j][1], c))

        def relayed(i, via, of):
            region = half(half(part(i, slots[of]), c), via)
            return remote(i, 4 + via, region, region, (chips[via][0], chips[via][1], c))

        def forward(i, j, h):
            region = half(part(i, slots[j]), h)
            return remote(i, 6 + j, region, region, sibling)

        def start():
            for i in range(n):
                for j in (x_nbr, y_nbr) if table[i][4] else (x_nbr, y_nbr, diagonal):
                    fetch(i, j, mine).start()
            for i in range(n):
                own(i).start()

        def relay():
            for i in range(n):
                if not table[i][4]:
                    for j in range(3):
                        fetch(i, j, slots[j]).wait_recv()
                    continue
                fetch(i, y_nbr, slots[y_nbr]).wait_recv()
                relayed(i, x_nbr, y_nbr).start()
                forward(i, y_nbr, c).start()
                fetch(i, x_nbr, slots[x_nbr]).wait_recv()
                relayed(i, y_nbr, x_nbr).start()
                forward(i, x_nbr, c).start()

        def relay_diagonal():
            for i in range(n):
                if table[i][4]:
                    relayed(i, x_nbr, diagonal).wait_recv()
                    relayed(i, y_nbr, diagonal).wait_recv()
                    forward(i, diagonal, c).start()

        def finish():
            for i in range(n):
                own(i).wait()
                for j in range(3):
                    if table[i][4]:
                        forward(i, j, 1 - c).wait_recv()
                        forward(i, j, c).wait_send()
                    if j != diagonal or not table[i][4]:
                        fetch(i, j, mine).wait_send()
                if table[i][4]:
                    relayed(i, x_nbr, y_nbr).wait_send()
                    relayed(i, y_nbr, x_nbr).wait_send()

        return start, relay, relay_diagonal, finish

    out_shape = [jax.ShapeDtypeStruct(_full_shape(name, shape), dtype) for name, shape, _, dtype, _ in table]
    return _Exchange(shards, out_shape, SEMS_PER_GATHER * n, ops)


def _run_exchange(exchange, name):
    n = len(exchange.ins)

    def body(*refs):
        for phase in exchange.ops(refs[:n], refs[n:2 * n], *refs[2 * n:]):
            phase()

    return pl.pallas_call(
        body,
        name=name,
        in_specs=[HBM_SPEC] * n,
        out_specs=[HBM_SPEC] * n,
        out_shape=exchange.out_shape,
        scratch_shapes=exchange.scratch,
    )(*exchange.ins)


def _host_exchange(exchange, in_refs, out_refs, sem_refs, first, middle, last, late=None):
    start, *relays, finish = exchange.ops(in_refs, out_refs, *sem_refs)
    pl.when(first)(start)
    pl.when(middle)(relays[0])
    if len(relays) > 1:
        pl.when(last if late is None else late)(relays[1])
    pl.when(last)(finish)


def _halves_view(name, shape, axis):
    r, c = shape
    if name == STACKED:
        return (N_CHIPS, 2, r // 2, c // N_CHIPS), lambda ref, h: ref.at[:, h]
    if axis == 1:
        return (2, r // 2, c), lambda ref, h: ref.at[h]
    return (N_CHIPS, 2, r // N_CHIPS // 2, c), lambda ref, h: ref.at[:, h]


def _halves_exchange(grads, entries):
    n_w = len(entries)
    views = [_halves_view(*entry) for entry in entries]

    def ops(ins, outs, send_sems, recv_sems):
        x, y, c = _coords()
        copies = [pltpu.make_async_remote_copy(views[i][1](ins[i], 1 - c), outs[i], send_sems.at[i], recv_sems.at[i],
                                               device_id=(x, y, 1 - c), device_id_type=MESH_ID) for i in range(n_w)]

        def start():
            for cp in copies:
                cp.start()

        def finish():
            for cp in copies:
                cp.wait()

        return start, lambda: None, finish

    half_shape = lambda v: tuple(d for i, d in enumerate(v) if i != (1 if len(v) == 4 else 0))
    out_shape = [jax.ShapeDtypeStruct(half_shape(v[0]), F32) for v in views]
    return _Exchange([g.reshape(v[0]) for g, v in zip(grads, views)], out_shape, n_w, ops)


def _add_sibling(g_view, recv, c, name):
    shape = recv.shape
    if len(shape) == 2:
        tr = _tile(shape[0], 128, 16)
        grid = (shape[0] // tr,)
        g_spec = pl.BlockSpec((None, tr, shape[1]), lambda i, c_ref: (c_ref[0], i, 0))
        r_spec = pl.BlockSpec((tr, shape[1]), lambda i, c_ref: (i, 0))
    else:
        tr = _tile(shape[1], 256, 16)
        grid = (N_CHIPS, shape[1] // tr)
        g_spec = pl.BlockSpec((None, None, tr, shape[2]), lambda k, i, c_ref: (k, c_ref[0], i, 0))
        r_spec = pl.BlockSpec((None, tr, shape[2]), lambda k, i, c_ref: (k, i, 0))

    def body(c_ref, g_ref, r_ref, o_ref):
        o_ref[...] = (g_ref[...] + r_ref[...]).astype(BF16)

    return pl.pallas_call(
        body,
        name="add_sibling_" + name,
        grid_spec=pltpu.PrefetchScalarGridSpec(num_scalar_prefetch=1, grid=grid, in_specs=[g_spec, r_spec],
                                               out_specs=r_spec),
        out_shape=jax.ShapeDtypeStruct(shape, BF16),
        compiler_params=_cparams(("parallel",) * len(grid)),
    )(c, g_view, recv)


def _piece_of(ref, name, axis, k):
    if name == STACKED or axis == 0:
        return ref.at[k]
    size = ref.shape[1] // N_CHIPS
    return ref.at[:, pl.ds(pl.multiple_of(k * size, size), size)]


def _piece_shape(name, shape, axis):
    r, c = shape
    return (r // 2, c // N_CHIPS) if (axis == 1) else (r // N_CHIPS // 2, c)


def _scatter_exchange(partials, entries):
    n_w = len(entries)

    def ops(ins, outs, send_sems, recv_sems):
        x, y, c = _coords()
        chips = _other_chips(x, y)
        copies = []
        for i, (name, _, axis) in enumerate(entries):
            for j, chip in enumerate(chips):
                sem = 3 * i + j
                copies.append(pltpu.make_async_remote_copy(
                    _piece_of(ins[i], name, axis, 2 * chip[0] + chip[1]), outs[i].at[j], send_sems.at[sem],
                    recv_sems.at[sem], device_id=(chip[0], chip[1], c), device_id_type=MESH_ID))

        def start():
            for cp in copies:
                cp.start()

        def finish():
            for cp in copies:
                cp.wait()

        return start, lambda: None, finish

    out_shape = [jax.ShapeDtypeStruct((3,) + _piece_shape(*entry), BF16) for entry in entries]
    return _Exchange(partials, out_shape, 3 * n_w, ops)


def _add_chips(partial, recv, mine, name, axis):
    rows, cols = recv.shape[1:]
    tr = _tile(rows, 256, 16)
    if name == STACKED or axis == 0:
        p_spec = pl.BlockSpec((None, tr, cols), lambda i, k_ref: (k_ref[0], i, 0))
    else:
        p_spec = pl.BlockSpec((tr, cols), lambda i, k_ref: (i, k_ref[0]))

    def body(k_ref, p_ref, r_ref, o_ref):
        f32 = lambda a: a.astype(F32)
        o_ref[...] = ((f32(p_ref[...]) + f32(r_ref[0])) + f32(r_ref[1])) + f32(r_ref[2])

    return pl.pallas_call(
        body,
        name="add_chips_" + name,
        grid_spec=pltpu.PrefetchScalarGridSpec(
            num_scalar_prefetch=1, grid=(rows // tr,),
            in_specs=[p_spec, pl.BlockSpec((3, tr, cols), lambda i, k_ref: (0, i, 0))],
            out_specs=pl.BlockSpec((tr, cols), lambda i, k_ref: (i, 0))),
        out_shape=jax.ShapeDtypeStruct((rows, cols), F32),
        compiler_params=_cparams(("parallel",)),
    )(mine, partial, recv)


def _share_with_sibling(halves):
    n_w = len(halves)

    def body(*refs):
        ins, outs = refs[:n_w], refs[n_w:2 * n_w]
        send_sems, recv_sems = refs[2 * n_w:]
        x, y, c = _coords()
        copies = [pltpu.make_async_remote_copy(ins[i], outs[i], send_sems.at[i], recv_sems.at[i],
                                               device_id=(x, y, 1 - c), device_id_type=MESH_ID) for i in range(n_w)]
        for cp in copies:
            cp.start()
        for cp in copies:
            cp.wait()

    return pl.pallas_call(
        body,
        name="share_with_sibling",
        in_specs=[HBM_SPEC] * n_w,
        out_specs=[HBM_SPEC] * n_w,
        out_shape=[jax.ShapeDtypeStruct(h.shape, F32) for h in halves],
        scratch_shapes=[pltpu.SemaphoreType.DMA((n_w,)), pltpu.SemaphoreType.DMA((n_w,))],
    )(*halves)


SMALL_ROWS = 168


def _small_exchange(buf):
    def ops(ins, outs, send_sems, recv_sems):
        x, y, c = _coords()
        me = 4 * x + 2 * y + c
        peers = [(x ^ fx, y ^ fy, c ^ fc) for fx in (0, 1) for fy in (0, 1) for fc in (0, 1)][1:]

        def copy(j, slot, dev):
            return pltpu.make_async_remote_copy(ins[0], outs[0].at[slot], send_sems.at[j], recv_sems.at[j],
                                                device_id=dev, device_id_type=MESH_ID)

        own = pltpu.make_async_copy(ins[0], outs[0].at[me], send_sems.at[N_DEV - 1])

        def start():
            own.start()
            for j, dev in enumerate(peers):
                copy(j, me, dev).start()

        def finish():
            for j, dev in enumerate(peers):
                copy(j, 4 * dev[0] + 2 * dev[1] + dev[2], dev).wait()
            own.wait()

        return start, lambda: None, finish

    return _Exchange([buf], [jax.ShapeDtypeStruct((N_DEV,) + buf.shape, F32)], N_DEV, ops)


def _sum_devices(gathered):
    def body(g_ref, out_ref):
        acc = g_ref[0]
        for d in range(1, N_DEV):
            acc = acc + g_ref[d]
        out_ref[...] = acc

    return pl.pallas_call(
        body,
        name="sum_devices",
        in_specs=[VMEM_SPEC],
        out_specs=VMEM_SPEC,
        out_shape=jax.ShapeDtypeStruct(gathered.shape[1:], F32),
    )(gathered)


def _adamw(w, g, m, v):
    r, rest = w.shape[0], w.shape[1:]
    per_row = 1
    for dim in rest:
        per_row *= dim
    tr = _tile(r, max(8, (5 << 19) // (4 * per_row)), 8 if len(rest) == 1 else 1)

    def body(w_ref, g_ref, m_ref, v_ref, d_ref, mo_ref, vo_ref):
        gg = g_ref[...]
        mm = ADAM_B1 * m_ref[...] + (1.0 - ADAM_B1) * gg
        vv = ADAM_B2 * v_ref[...] + (1.0 - ADAM_B2) * (gg * gg)
        m_hat = mm / (1.0 - ADAM_B1 ** ADAM_STEP)
        v_hat = vv / (1.0 - ADAM_B2 ** ADAM_STEP)
        d_ref[...] = -ADAM_LR * (m_hat / (jnp.sqrt(v_hat) + ADAM_EPS) + ADAM_WD * w_ref[...])
        mo_ref[...] = mm
        vo_ref[...] = vv

    spec = pl.BlockSpec((tr,) + rest, lambda i: (i,) + (0,) * len(rest))
    return pl.pallas_call(
        body,
        name="adamw",
        grid=(r // tr,),
        in_specs=[spec] * 4,
        out_specs=[spec] * 3,
        out_shape=[jax.ShapeDtypeStruct(w.shape, F32)] * 3,
        compiler_params=_cparams(("parallel",)),
    )(w, g, m, v)


def _adamw_halves(w, own, other, m, v, c, name):
    r, cols = w.shape
    half = r // 2
    tr = _tile(half, 256, 8)
    nt = half // tr
    whole = pl.BlockSpec((tr, cols), lambda h, i, c_ref: (h * nt + i, 0))
    part = pl.BlockSpec((tr, cols), lambda h, i, c_ref: (i, 0))

    def body(c_ref, w_ref, own_ref, other_ref, m_ref, v_ref, g_ref, d_ref, mo_ref, vo_ref):
        gg = jnp.where(pl.program_id(0) == c_ref[0], own_ref[...], other_ref[...])
        g_ref[...] = gg
        mm = ADAM_B1 * m_ref[...] + (1.0 - ADAM_B1) * gg
        vv = ADAM_B2 * v_ref[...] + (1.0 - ADAM_B2) * (gg * gg)
        m_hat = mm / (1.0 - ADAM_B1 ** ADAM_STEP)
        v_hat = vv / (1.0 - ADAM_B2 ** ADAM_STEP)
        d_ref[...] = -ADAM_LR * (m_hat / (jnp.sqrt(v_hat) + ADAM_EPS) + ADAM_WD * w_ref[...])
        mo_ref[...] = mm
        vo_ref[...] = vv

    return pl.pallas_call(
        body,
        name="adamw_" + name,
        grid_spec=pltpu.PrefetchScalarGridSpec(
            num_scalar_prefetch=1, grid=(2, nt),
            in_specs=[whole, part, part, whole, whole], out_specs=[whole] * 4),
        out_shape=[jax.ShapeDtypeStruct((r, cols), F32)] * 4,
        compiler_params=_cparams(("parallel", "parallel")),
    )(c, w, own, other, m, v)


GATHER_FIRST = ("ffn1_w_in", "ffn1_w_out")
GATHER_PROJ = ("w_in",)
GATHER_LATE = ("w_branch_a", "w_branch_b", "w_out", "ffn2_w_in", "ffn2_w_out")


class _Comm:
    def __init__(self, shards, c_arr, mine_arr):
        self.shards, self.c, self.mine = shards, c_arr, mine_arr
        self.groups, self.halves = {}, {}
        self.by_name = {entry[0]: entry for entry in BIG}

    def small_gather(self, loss, small):
        pad_lanes = lambda a: jnp.concatenate([a, jnp.zeros((1, LANES - a.shape[1]), F32)], axis=1)
        buf = jnp.concatenate([
            small["meta_tokens"].reshape(128, LANES),
            small["ffn1_norm"].reshape(8, LANES), small["mix_norm"].reshape(8, LANES),
            small["ffn2_norm"].reshape(8, LANES), small["final_norm"].reshape(8, LANES),
            loss, pad_lanes(small["b_forget"]), pad_lanes(small["attn_sinks"]),
            jnp.zeros((SMALL_ROWS - 163, LANES), F32)], axis=0)
        return _small_exchange(buf)

    def small_gathered(self, outs):
        self.reduced = _sum_devices(outs[0])

    def gather(self, names):
        table = [self.by_name[n] + (BF16, True) for n in names]
        return _gather_exchange([self.shards[n] for n in names], table)

    def gathered(self, names, outs):
        return [o.transpose(1, 0, 2).reshape(D_MODEL, W_IN_COLS) if n == STACKED else o for n, o in zip(names, outs)]

    def swap(self, tag, grads):
        entries = [self.by_name[n] for n in grads]
        arrays = [g.reshape(D_MODEL, N_CHIPS, W_IN_COLS // N_CHIPS).transpose(1, 0, 2) if n == STACKED else g
                  for n, g in grads.items()]
        self.groups[tag] = (entries, arrays)
        return _halves_exchange(arrays, entries)

    def scatter(self, tag, received):
        entries, arrays = self.groups[tag]
        views = [_halves_view(*entry) for entry in entries]
        partials = [_add_sibling(g.reshape(v[0]), r, self.c, name)
                    for g, v, r, (name, _, _) in zip(arrays, views, received, entries)]
        self.groups[tag] = (entries, partials)
        return _scatter_exchange(partials, entries)

    def received(self, tag, pieces):
        entries, partials = self.groups[tag]
        for p, r, (name, _, axis) in zip(partials, pieces, entries):
            self.halves[name] = _add_chips(p, r, self.mine, name, axis)

    def finish(self):
        names = [n for n, _, _ in BIG]
        own = [self.halves[n] for n in names]
        return dict(zip(names, zip(own, _share_with_sibling(own))))


def kernel(x, meta_tokens, ffn1_norm, ffn1_w_in, ffn1_w_out, mix_norm, w_in, b_forget, attn_sinks, w_branch_a, w_branch_b, w_out, ffn2_norm, ffn2_w_in, ffn2_w_out, final_norm, loss_target, m_meta_tokens, m_ffn1_norm, m_ffn1_w_in, m_ffn1_w_out, m_mix_norm, m_w_in, m_b_forget, m_attn_sinks, m_w_branch_a, m_w_branch_b, m_w_out, m_ffn2_norm, m_ffn2_w_in, m_ffn2_w_out, m_final_norm, v_meta_tokens, v_ffn1_norm, v_ffn1_w_in, v_ffn1_w_out, v_mix_norm, v_w_in, v_b_forget, v_attn_sinks, v_w_branch_a, v_w_branch_b, v_w_out, v_ffn2_norm, v_ffn2_w_in, v_ffn2_w_out, v_final_norm):
    given = dict(locals())
    names = ["meta_tokens", "ffn1_norm", "ffn1_w_in", "ffn1_w_out", "mix_norm", "w_in", "b_forget", "attn_sinks",
             "w_branch_a", "w_branch_b", "w_out", "ffn2_norm", "ffn2_w_in", "ffn2_w_out", "final_norm"]
    big_names = [n for n, _, _ in BIG]
    cx, cy, cc = _coords()
    c_arr = cc.reshape(1).astype(jnp.int32)
    mine_arr = (2 * cx + cy).reshape(1).astype(jnp.int32)

    comm = _Comm({n: given[n][0].astype(BF16) for n in big_names}, c_arr, mine_arr)
    table = [comm.by_name[n] + (BF16, True) for n in GATHER_FIRST] + [("meta_tokens", (N_META, D_MODEL), 1, F32, False)]
    first = _gather_exchange([comm.shards[n] for n in GATHER_FIRST] + [meta_tokens], table)
    w1i, w1o, meta_full = _run_exchange(first, "gather_first")
    norms = (ffn1_norm, mix_norm, ffn2_norm, final_norm.reshape(1, D_MODEL))
    loss, grad_x, small, big = _local_step(x, loss_target, meta_full, norms, b_forget, attn_sinks, (w1i, w1o), comm)

    swap = comm.swap("ffn1_in", dict(ffn1_w_in=big["ffn1_w_in"]))
    last = comm.scatter("ffn1_in", _run_exchange(swap, "exchange_halves_ffn1_in"))
    comm.received("ffn1_in", _run_exchange(last, "scatter_chip_sums"))
    grad_halves = comm.finish()
    grads = {}

    red = comm.reduced
    meta_cols = red[:128].reshape(N_META, D_MODEL)
    grads["meta_tokens"] = lax.dynamic_slice_in_dim(meta_cols, (2 * cx + cy) * (D_MODEL // N_CHIPS),
                                                    D_MODEL // N_CHIPS, axis=1)
    grads["ffn1_norm"] = red[128:136].reshape(1, D_MODEL)
    grads["mix_norm"] = red[136:144].reshape(1, D_MODEL)
    grads["ffn2_norm"] = red[144:152].reshape(1, D_MODEL)
    grads["final_norm"] = red[152:160].reshape(1, D_MODEL)
    loss_out = red[160, 0]
    grads["b_forget"] = red[161:162, :B_HEADS]
    grads["attn_sinks"] = red[162:163, :A_HEADS]

    out_g, out_d, out_m, out_v = [], [], [], []
    for n in names:
        w_full = given[n]
        shape = w_full.shape
        two_d = (lambda a: a.reshape(shape[-2], shape[-1])) if len(shape) >= 2 else (lambda a: a.reshape(1, shape[0]))
        if n == STACKED:
            own, other = grad_halves[n]
            g_t = jnp.concatenate([jnp.where(cc == 0, own, other), jnp.where(cc == 0, other, own)], axis=0).T
            tiles = lambda a: a.reshape(shape[-1], shape[-2] // LANES, LANES)
            untile = lambda a: a.reshape(shape[-1], shape[-2]).T
            d2, m2, v2 = [untile(a) for a in _adamw(tiles(two_d(w_full).T), tiles(g_t), tiles(two_d(given["m_" + n]).T),
                                                     tiles(two_d(given["v_" + n]).T))]
            g2 = g_t.T
        elif n in grad_halves:
            own, other = grad_halves[n]
            g2, d2, m2, v2 = _adamw_halves(two_d(w_full), own, other, two_d(given["m_" + n]),
                                           two_d(given["v_" + n]), c_arr, n)
        else:
            g2 = two_d(grads[n])
            d2, m2, v2 = _adamw(two_d(w_full), g2, two_d(given["m_" + n]), two_d(given["v_" + n]))
        out_g.append(g2.reshape(shape))
        out_d.append(d2.reshape(shape))
        out_m.append(m2.reshape(shape))
        out_v.append(v2.reshape(shape))
    return (loss_out, grad_x, *out_g, *out_d, *out_m, *out_v)
```

```python
import jax
import jax.numpy as jnp
from jax import lax
from jax.experimental import pallas as pl
from jax.experimental.pallas import tpu as pltpu

F32 = jnp.float32
BF16 = jnp.bfloat16

D_MODEL = 1024
N_META = 16
BLOCK = 128
LANES = 128
PREFIX = BLOCK
N_PAD = PREFIX - N_META
HEAD_DIM = 64
A_HEADS = 8
A_KV_HEADS = 2
A_GROUP = 4
B_HEADS = 8
B_PAIRS = B_HEADS // 2
A_WIDTH = A_HEADS * HEAD_DIM
A_KV_WIDTH = A_KV_HEADS * HEAD_DIM
B_WIDTH = B_HEADS * HEAD_DIM
W_IN_COLS = A_WIDTH + 2 * A_KV_WIDTH + 3 * B_WIDTH + B_HEADS + 2 * D_MODEL
SRC_KA = A_WIDTH
SRC_VA = SRC_KA + A_KV_WIDTH
SRC_QB = SRC_VA + A_KV_WIDTH
SRC_KB = SRC_QB + B_WIDTH
SRC_VB = SRC_KB + B_WIDTH
SRC_F = SRC_VB + B_WIDTH
SRC_GA = SRC_F + B_HEADS
SRC_GB = SRC_GA + D_MODEL
A_PAD_WIDTH = A_HEADS * LANES
B_PAD_WIDTH = B_HEADS * LANES
F_COLS = LANES
OFF_QA = 0
OFF_KA = SRC_KA
OFF_VA = SRC_VA
OFF_QB = SRC_QB
OFF_KB = SRC_KB
OFF_VB = SRC_VB
OFF_GA = SRC_F
OFF_GB = OFF_GA + D_MODEL
OFF_F = OFF_GB + D_MODEL
P_COLS = OFF_F + F_COLS
EPS = 1e-6
NEG = -1e30
SCALE = HEAD_DIM ** -0.5
KEY_BLOCKS = 4

ADAM_LR = 0.001
ADAM_B1 = 0.9
ADAM_B2 = 0.999
ADAM_EPS = 1e-08
ADAM_WD = 0.01
ADAM_STEP = 10

N_CHIPS = 4
N_DEV = 8
VMEM_LIMIT = 56 * 1024 * 1024

NT_DIMS = (((1,), (1,)), ((), ()))
TN_DIMS = (((0,), (0,)), ((), ()))
MESH_ID = pl.DeviceIdType.MESH
HBM_SPEC = pl.BlockSpec(memory_space=pltpu.HBM)
VMEM_SPEC = pl.BlockSpec(memory_space=pltpu.VMEM)


def _tile(n, target, mult=16):
    best = None
    for t in range(mult, min(n, target) + 1, mult):
        if n % t == 0:
            best = t
    return best if best is not None else n


def _cparams(sem):
    return pltpu.CompilerParams(dimension_semantics=sem, vmem_limit_bytes=VMEM_LIMIT)


def _rms_scale(h):
    return lax.rsqrt(jnp.mean(h * h, axis=-1, keepdims=True) + EPS)


def _rms_bwd(dn, h, w):
    r = _rms_scale(h)
    dw = jnp.sum(dn * (h * r), axis=0, keepdims=True)
    z = dn * w
    dh = r * z - h * ((r * r * r) * jnp.mean(z * h, axis=-1, keepdims=True))
    return dh, dw


def _ffn_fwd(h, norm_w, w_in, w_out, exchange=None):
    t, d = h.shape
    f = w_out.shape[0]
    tm = _tile(t, 272)
    tc = _tile(f, 256, 128)
    nj = f // tc
    ni = t // tm
    n_x = len(exchange.ins) if exchange else 0

    def body(*refs):
        h_ref, nw_ref, wi_ref, wo_ref = refs[:4]
        hout_ref, g_ref, u_ref = refs[4 + n_x:7 + n_x]
        a_scr = refs[7 + 2 * n_x]
        i = pl.program_id(0)
        if exchange:
            _host_exchange(exchange, refs[4:4 + n_x], refs[7 + n_x:7 + 2 * n_x], refs[8 + 2 * n_x:],
                           i == 0, i == ni // 3, i == ni - 1, late=i == 2 * ni // 3)
        hh = h_ref[...]
        n = ((hh * _rms_scale(hh)) * nw_ref[...]).astype(BF16)
        for j in range(nj):
            cols = slice(j * tc, (j + 1) * tc)
            g = jnp.dot(n, wi_ref[:, j * tc:(j + 1) * tc], preferred_element_type=F32)
            u = jnp.dot(n, wi_ref[:, f + j * tc:f + (j + 1) * tc], preferred_element_type=F32)
            g_ref[:, cols] = g
            u_ref[:, cols] = u
            a_scr[:, cols] = ((g * jax.nn.sigmoid(g)) * u).astype(BF16)
        hout_ref[...] = hh + 0.5 * jnp.dot(a_scr[...], wo_ref[...], preferred_element_type=F32)

    resident = lambda a: pl.BlockSpec(a.shape, lambda i: (0, 0), pipeline_mode=pl.Buffered(1))
    row = lambda w: pl.BlockSpec((tm, w), lambda i: (i, 0))
    outs = pl.pallas_call(
        body,
        name="ffn_fwd",
        grid=(ni,),
        in_specs=[row(d), pl.BlockSpec((1, d), lambda i: (0, 0)), resident(w_in), resident(w_out)] + [HBM_SPEC] * n_x,
        out_specs=[row(d), row(f), row(f)] + [HBM_SPEC] * n_x,
        out_shape=[
            jax.ShapeDtypeStruct((t, d), F32),
            jax.ShapeDtypeStruct((t, f), F32),
            jax.ShapeDtypeStruct((t, f), F32),
        ] + (exchange.out_shape if exchange else []),
        scratch_shapes=[pltpu.VMEM((tm, f), BF16)] + (exchange.scratch if exchange else []),
        compiler_params=_cparams(("arbitrary",) if exchange else ("parallel",)),
    )(h, norm_w, w_in, w_out, *(exchange.ins if exchange else []))
    return outs[:3], outs[3:]


def _ffn_bwd(dh_out, h, norm_w, g, u, w_in, w_out, exchange=None):
    t, d = h.shape
    f = w_out.shape[0]
    tm = _tile(t, 272)
    tc = _tile(f, 256, 128)
    nj = f // tc
    ni = t // tm
    n_x = len(exchange.ins) if exchange else 0

    def body(*refs):
        dho_ref, h_ref, nw_ref, g_ref, u_ref, wi_ref, wo_ref = refs[:7]
        dhin_ref, n_ref, a_ref, dgu_ref, df_ref, dnw_ref = refs[7 + n_x:13 + n_x]
        i = pl.program_id(0)
        if exchange:
            _host_exchange(exchange, refs[7:7 + n_x], refs[13 + n_x:13 + 2 * n_x], refs[13 + 2 * n_x:],
                           i == 0, i == ni - 1, i == ni - 1)
        hh = h_ref[...]
        nw = nw_ref[...]
        n_ref[...] = ((hh * _rms_scale(hh)) * nw).astype(BF16)
        dho = dho_ref[...]
        df = (0.5 * dho).astype(BF16)
        df_ref[...] = df
        for j in range(nj):
            cols = slice(j * tc, (j + 1) * tc)
            da = lax.dot_general(df, wo_ref[cols, :], NT_DIMS, preferred_element_type=F32)
            gg = g_ref[:, cols]
            uu = u_ref[:, cols]
            sig = jax.nn.sigmoid(gg)
            sl = gg * sig
            a_ref[:, cols] = (sl * uu).astype(BF16)
            dgu_ref[0, :, cols] = ((da * uu) * (sig * (1.0 + gg * (1.0 - sig)))).astype(BF16)
            dgu_ref[1, :, cols] = (da * sl).astype(BF16)
        dn = (lax.dot_general(dgu_ref[0], wi_ref[:, :f], NT_DIMS, preferred_element_type=F32)
              + lax.dot_general(dgu_ref[1], wi_ref[:, f:], NT_DIMS, preferred_element_type=F32))
        dh, dw = _rms_bwd(dn, hh, nw)
        dhin_ref[...] = dho + dh
        dnw_ref[0] = dw

    resident = lambda a: pl.BlockSpec(a.shape, lambda i: (0, 0), pipeline_mode=pl.Buffered(1))
    row = lambda w: pl.BlockSpec((tm, w), lambda i: (i, 0))
    outs = pl.pallas_call(
        body,
        name="ffn_bwd",
        grid=(ni,),
        in_specs=[row(d), row(d), pl.BlockSpec((1, d), lambda i: (0, 0)), row(f), row(f),
                  resident(w_in), resident(w_out)] + [HBM_SPEC] * n_x,
        out_specs=[row(d), row(d), row(f), pl.BlockSpec((2, tm, f), lambda i: (0, i, 0)), row(d),
                   pl.BlockSpec((1, 1, d), lambda i: (i, 0, 0))] + [HBM_SPEC] * n_x,
        out_shape=[
            jax.ShapeDtypeStruct((t, d), F32),
            jax.ShapeDtypeStruct((t, d), BF16),
            jax.ShapeDtypeStruct((t, f), BF16),
            jax.ShapeDtypeStruct((2, t, f), BF16),
            jax.ShapeDtypeStruct((t, d), BF16),
            jax.ShapeDtypeStruct((ni, 1, d), F32),
        ] + (exchange.out_shape if exchange else []),
        scratch_shapes=exchange.scratch if exchange else [],
        compiler_params=_cparams(("arbitrary",) if exchange else ("parallel",)),
    )(dh_out, h, norm_w, g, u, w_in, w_out, *(exchange.ins if exchange else []))
    return outs[:6], outs[6:]


def _tn_matmul(a, b, name, exchange=None):
    t, k = a.shape
    split = b.ndim == 3
    n = 2 * b.shape[2] if split else b.shape[1]
    tk = _tile(k, 512, 128)
    tn = _tile(b.shape[-1], 1408, 128)
    per_half = b.shape[-1] // tn
    ni, nj = k // tk, n // tn
    n_x = len(exchange.ins) if exchange else 0

    def body(*refs):
        a_ref, b_ref, o_ref = refs[0], refs[1], refs[2 + n_x]
        if exchange:
            i, j = pl.program_id(0), pl.program_id(1)
            at_end = (i == ni - 1) & (j == nj - 1)
            _host_exchange(exchange, refs[2:2 + n_x], refs[3 + n_x:3 + 2 * n_x], refs[3 + 2 * n_x:],
                           (i == 0) & (j == 0), at_end, at_end)
        o_ref[...] = lax.dot_general(a_ref[...], b_ref[...], TN_DIMS, preferred_element_type=F32)

    if split:
        b_spec = pl.BlockSpec((None, t, tn), lambda i, j: (j // per_half, 0, j % per_half))
    else:
        b_spec = pl.BlockSpec((t, tn), lambda i, j: (0, j))
    outs = pl.pallas_call(
        body,
        name=name,
        grid=(ni, nj),
        in_specs=[pl.BlockSpec((t, tk), lambda i, j: (0, i)), b_spec] + [HBM_SPEC] * n_x,
        out_specs=[pl.BlockSpec((tk, tn), lambda i, j: (i, j))] + [HBM_SPEC] * n_x,
        out_shape=[jax.ShapeDtypeStruct((k, n), F32)] + (exchange.out_shape if exchange else []),
        scratch_shapes=exchange.scratch if exchange else [],
        compiler_params=_cparams(("arbitrary", "arbitrary") if exchange else ("parallel", "parallel")),
    )(a, b, *(exchange.ins if exchange else []))
    return (outs[0], outs[1:]) if exchange else outs[0]


A_SLOT = lambda h: h // A_GROUP
B_SLOT = lambda h: h % 2

PROJ_PARTS = (
    (OFF_QA, A_WIDTH, A_PAD_WIDTH, True, A_SLOT), (OFF_KA, A_KV_WIDTH, A_KV_WIDTH, True, None),
    (OFF_VA, A_KV_WIDTH, A_KV_WIDTH, True, None), (OFF_QB, B_WIDTH, B_WIDTH, True, None),
    (OFF_KB, B_WIDTH, B_PAD_WIDTH, True, B_SLOT), (OFF_VB, B_WIDTH, B_PAD_WIDTH, True, B_SLOT),
    (OFF_GA, D_MODEL, D_MODEL, False, None), (OFF_GB, D_MODEL, D_MODEL, False, None), (OFF_F, F_COLS, F_COLS, False, None),
)


def _head_tile(pair, head, slot):
    lane_slot = lax.broadcasted_iota(jnp.int32, pair.shape, 1) // HEAD_DIM
    moved = pair if head % 2 == slot else pltpu.roll(pair, HEAD_DIM, 1)
    return jnp.where(lane_slot == slot, moved, 0.0)


def _proj_fwd(h, norm_w, w_p):
    t, d = h.shape
    tm = _tile(t, 272)

    def body(h_ref, nw_ref, w_ref, u_ref, *part_refs):
        hh = h_ref[...]
        un = ((hh * _rms_scale(hh)) * nw_ref[...]).astype(BF16)
        u_ref[...] = un
        for (off, width, _, _, slot), p_ref in zip(PROJ_PARTS, part_refs):
            if slot is None:
                p_ref[...] = jnp.dot(un, w_ref[:, off:off + width], preferred_element_type=F32).astype(p_ref.dtype)
                continue
            for pair in range(width // LANES):
                x = jnp.dot(un, w_ref[:, off + pair * LANES:off + (pair + 1) * LANES], preferred_element_type=F32)
                for head in (2 * pair, 2 * pair + 1):
                    p_ref[:, head * LANES:(head + 1) * LANES] = _head_tile(x, head, slot(head)).astype(p_ref.dtype)

    row = lambda w: pl.BlockSpec((tm, w), lambda i: (i, 0))
    return pl.pallas_call(
        body,
        name="proj_fwd",
        grid=(t // tm,),
        in_specs=[row(d), pl.BlockSpec((1, d), lambda i: (0, 0)),
                  pl.BlockSpec(w_p.shape, lambda i: (0, 0), pipeline_mode=pl.Buffered(1))],
        out_specs=[row(d)] + [row(width) for _, _, width, _, _ in PROJ_PARTS],
        out_shape=[jax.ShapeDtypeStruct((t, d), BF16)]
        + [jax.ShapeDtypeStruct((t, width), BF16 if is_bf else F32) for _, _, width, is_bf, _ in PROJ_PARTS],
        compiler_params=_cparams(("parallel",)),
    )(h, norm_w, w_p)


def _proj_bwd(dh_out, h, norm_w, dproj, w_p, exchange=None):
    t, d = h.shape
    n = w_p.shape[1]
    tm = _tile(t, 272)
    ni = t // tm
    n_x = len(exchange.ins) if exchange else 0

    def body(*refs):
        dho_ref, h_ref, nw_ref, dp_ref, w_ref = refs[:5]
        dhin_ref, dnw_ref = refs[5 + n_x:7 + n_x]
        if exchange:
            i = pl.program_id(0)
            _host_exchange(exchange, refs[5:5 + n_x], refs[7 + n_x:7 + 2 * n_x], refs[7 + 2 * n_x:],
                           i == 0, i == ni - 1, i == ni - 1)
        dn = lax.dot_general(dp_ref[...], w_ref[...], NT_DIMS, preferred_element_type=F32)
        dh, dw = _rms_bwd(dn, h_ref[...], nw_ref[...])
        dhin_ref[...] = dho_ref[...] + dh
        dnw_ref[0] = dw

    row = lambda w: pl.BlockSpec((tm, w), lambda i: (i, 0))
    outs = pl.pallas_call(
        body,
        name="proj_bwd",
        grid=(ni,),
        in_specs=[row(d), row(d), pl.BlockSpec((1, d), lambda i: (0, 0)), row(n),
                  pl.BlockSpec(w_p.shape, lambda i: (0, 0), pipeline_mode=pl.Buffered(1))] + [HBM_SPEC] * n_x,
        out_specs=[row(d), pl.BlockSpec((1, 1, d), lambda i: (i, 0, 0))] + [HBM_SPEC] * n_x,
        out_shape=[jax.ShapeDtypeStruct((t, d), F32), jax.ShapeDtypeStruct((ni, 1, d), F32)]
        + (exchange.out_shape if exchange else []),
        scratch_shapes=exchange.scratch if exchange else [],
        compiler_params=_cparams(("arbitrary",) if exchange else ("parallel",)),
    )(dh_out, h, norm_w, dproj, w_p, *(exchange.ins if exchange else []))
    return outs[:2], outs[2:]


def _merge_fwd(h, oa, ob, ga, gb, wa, wb, wo):
    t, d = h.shape
    tm = _tile(t, 544)

    def body(h_ref, oa_ref, ob_ref, ga_ref, gb_ref, wa_ref, wb_ref, wo_ref, hout_ref, mix_ref):
        ya = jnp.dot(oa_ref[...], wa_ref[...], preferred_element_type=F32)
        yb = jnp.dot(ob_ref[...], wb_ref[...], preferred_element_type=F32)
        mixed = (jax.nn.sigmoid(ga_ref[...]) * ya + jax.nn.sigmoid(gb_ref[...]) * yb).astype(BF16)
        mix_ref[...] = mixed
        hout_ref[...] = h_ref[...] + jnp.dot(mixed, wo_ref[...], preferred_element_type=F32)

    row = lambda w: pl.BlockSpec((tm, w), lambda i: (i, 0))
    full = lambda a: pl.BlockSpec(a.shape, lambda i: (0, 0))
    return pl.pallas_call(
        body,
        name="merge_fwd",
        grid=(t // tm,),
        in_specs=[row(d), row(oa.shape[1]), row(ob.shape[1]), row(d), row(d), full(wa), full(wb), full(wo)],
        out_specs=[row(d), row(d)],
        out_shape=[jax.ShapeDtypeStruct((t, d), F32), jax.ShapeDtypeStruct((t, d), BF16)],
        compiler_params=_cparams(("parallel",)),
    )(h, oa, ob, ga, gb, wa, wb, wo)


def _merge_bwd(dh, oa, ob, ga, gb, wa, wb, wo, exchange=None):
    t, d = dh.shape
    tm = _tile(t, 544)
    ni = t // tm
    n_x = len(exchange.ins) if exchange else 0

    def body(*refs):
        dh_ref, oa_ref, ob_ref, ga_ref, gb_ref, wa_ref, wb_ref, wo_ref = refs[:8]
        dya_ref, dyb_ref, doa_ref, dob_ref, dga_ref, dgb_ref, dhb_ref = refs[8 + n_x:15 + n_x]
        if exchange:
            i = pl.program_id(0)
            _host_exchange(exchange, refs[8:8 + n_x], refs[15 + n_x:15 + 2 * n_x], refs[15 + 2 * n_x:],
                           i == 0, i == ni - 1, i == ni - 1)
        dhb = dh_ref[...].astype(BF16)
        dhb_ref[...] = dhb
        dmix = lax.dot_general(dhb, wo_ref[...], NT_DIMS, preferred_element_type=F32)
        for o_ref, g_ref, w_ref, dy_ref, do_ref, dg_ref in (
                (oa_ref, ga_ref, wa_ref, dya_ref, doa_ref, dga_ref),
                (ob_ref, gb_ref, wb_ref, dyb_ref, dob_ref, dgb_ref)):
            y = jnp.dot(o_ref[...], w_ref[...], preferred_element_type=F32)
            s = jax.nn.sigmoid(g_ref[...])
            dy = (dmix * s).astype(BF16)
            dy_ref[...] = dy
            dg_ref[...] = ((dmix * y) * (s * (1.0 - s))).astype(BF16)
            do_ref[...] = lax.dot_general(dy, w_ref[...], NT_DIMS, preferred_element_type=F32).astype(BF16)

    row = lambda w: pl.BlockSpec((tm, w), lambda i: (i, 0))
    full = lambda a: pl.BlockSpec(a.shape, lambda i: (0, 0))
    wa_w, wb_w = oa.shape[1], ob.shape[1]
    outs = pl.pallas_call(
        body,
        name="merge_bwd",
        grid=(ni,),
        in_specs=[row(d), row(wa_w), row(wb_w), row(d), row(d), full(wa), full(wb), full(wo)] + [HBM_SPEC] * n_x,
        out_specs=[row(d), row(d), row(wa_w), row(wb_w), row(d), row(d), row(d)] + [HBM_SPEC] * n_x,
        out_shape=[
            jax.ShapeDtypeStruct((t, d), BF16), jax.ShapeDtypeStruct((t, d), BF16),
            jax.ShapeDtypeStruct((t, wa_w), BF16), jax.ShapeDtypeStruct((t, wb_w), BF16),
            jax.ShapeDtypeStruct((t, d), BF16), jax.ShapeDtypeStruct((t, d), BF16),
            jax.ShapeDtypeStruct((t, d), BF16),
        ] + (exchange.out_shape if exchange else []),
        scratch_shapes=exchange.scratch if exchange else [],
        compiler_params=_cparams(("arbitrary",) if exchange else ("parallel",)),
    )(dh, oa, ob, ga, gb, wa, wb, wo, *(exchange.ins if exchange else []))
    return outs[:7], outs[7:]


def _tri_dot(tri, x):
    hi = x.astype(BF16)
    r1 = x - hi.astype(F32)
    mid = r1.astype(BF16)
    lo = (r1 - mid.astype(F32)).astype(BF16)
    return (jnp.dot(tri, hi, preferred_element_type=F32)
            + jnp.dot(tri, mid, preferred_element_type=F32)
            + jnp.dot(tri, lo, preferred_element_type=F32))


def _forget_cumsum(f_logit, b_pad, nb):
    t, w = f_logit.shape
    bsz = t // (nb * BLOCK)

    def body(f_ref, b_ref, c_ref, carry):
        n = pl.program_id(1)

        @pl.when(n == 0)
        def _():
            carry[...] = jnp.zeros_like(carry)

        x = jax.nn.log_sigmoid(f_ref[...] + b_ref[...])
        rows = lax.broadcasted_iota(jnp.int32, (BLOCK, BLOCK), 0)
        cols = lax.broadcasted_iota(jnp.int32, (BLOCK, BLOCK), 1)
        tri = (cols <= rows).astype(BF16)
        c = _tri_dot(tri, x) + carry[...]
        c_ref[...] = c
        carry[...] = c[BLOCK - 1:BLOCK, :]

    return pl.pallas_call(
        body,
        name="forget_cumsum",
        grid=(bsz, nb),
        in_specs=[pl.BlockSpec((BLOCK, w), lambda b, n: (b * nb + n, 0)),
                  pl.BlockSpec((1, w), lambda b, n: (0, 0))],
        out_specs=pl.BlockSpec((BLOCK, w), lambda b, n: (b * nb + n, 0)),
        out_shape=jax.ShapeDtypeStruct((t, w), F32),
        scratch_shapes=[pltpu.VMEM((1, w), F32)],
        compiler_params=_cparams(("parallel", "arbitrary")),
    )(f_logit, b_pad)


def _forget_cumsum_bwd(dc, f_logit, b_pad, nb):
    t, w = f_logit.shape
    bsz = t // (nb * BLOCK)

    def body(dc_ref, f_ref, b_ref, df_ref, db_ref, carry):
        n = pl.program_id(1)

        @pl.when(n == 0)
        def _():
            carry[...] = jnp.zeros_like(carry)
            db_ref[...] = jnp.zeros_like(db_ref)

        rows = lax.broadcasted_iota(jnp.int32, (BLOCK, BLOCK), 0)
        cols = lax.broadcasted_iota(jnp.int32, (BLOCK, BLOCK), 1)
        tri = (cols >= rows).astype(BF16)
        dlf = _tri_dot(tri, dc_ref[...]) + carry[...]
        carry[...] = dlf[0:1, :]
        df = dlf * jax.nn.sigmoid(-(f_ref[...] + b_ref[...]))
        df_ref[...] = df.astype(BF16)
        db_ref[0] += jnp.sum(df, axis=0, keepdims=True)

    rev = lambda b, n: (b * nb + (nb - 1 - n), 0)
    return pl.pallas_call(
        body,
        name="forget_cumsum_bwd",
        grid=(bsz, nb),
        in_specs=[pl.BlockSpec((BLOCK, w), rev),
                  pl.BlockSpec((BLOCK, w), rev),
                  pl.BlockSpec((1, w), lambda b, n: (0, 0))],
        out_specs=[pl.BlockSpec((BLOCK, w), rev),
                   pl.BlockSpec((1, 1, w), lambda b, n: (b, 0, 0))],
        out_shape=[jax.ShapeDtypeStruct((t, w), BF16), jax.ShapeDtypeStruct((bsz, 1, w), F32)],
        scratch_shapes=[pltpu.VMEM((1, w), F32)],
        compiler_params=_cparams(("parallel", "arbitrary")),
    )(dc, f_logit, b_pad)


GROUP_ROWS = A_GROUP * BLOCK


def _stack_heads(ref, g):
    return jnp.concatenate([ref[:, (A_GROUP * g + i) * LANES:(A_GROUP * g + i + 1) * LANES] for i in range(A_GROUP)],
                           axis=0)


def _unstack_heads(ref, g, x):
    for i in range(A_GROUP):
        ref[:, (A_GROUP * g + i) * LANES:(A_GROUP * g + i + 1) * LANES] = x[i * BLOCK:(i + 1) * BLOCK].astype(ref.dtype)


def _swa_logits(qk, slope, n):
    qi = lax.broadcasted_iota(jnp.int32, (GROUP_ROWS, BLOCK), 0) & (BLOCK - 1)
    kj = lax.broadcasted_iota(jnp.int32, (GROUP_ROWS, BLOCK), 1)
    s_all = qk * SCALE
    out = []
    for i, (dist, ok) in enumerate((
            (n * BLOCK + qi - kj, (kj >= N_PAD) & (n * BLOCK + qi - kj >= 0)),
            (BLOCK + qi - kj, (kj > qi) & (n >= 2)),
            (qi - kj, (kj <= qi) & (n >= 1)))):
        s = s_all[:, i * BLOCK:(i + 1) * BLOCK] - slope * dist.astype(F32)
        out.append(jnp.where(ok, s, NEG))
    return out


def _three_blocks(m_ref, p_ref, c_ref):
    return jnp.concatenate([m_ref[...], p_ref[...], c_ref[...]], axis=0)


def _swa_specs(bsz, nb):
    qspec = pl.BlockSpec((bsz, BLOCK, A_PAD_WIDTH), lambda n: (0, n, 0))
    kv_m = pl.BlockSpec((bsz, BLOCK, LANES), lambda n: (0, 0, 0))
    kv_p = pl.BlockSpec((bsz, BLOCK, LANES), lambda n: (0, jnp.maximum(n - 1, 0), 0))
    kv_c = pl.BlockSpec((bsz, BLOCK, LANES), lambda n: (0, n, 0))
    rowspec = pl.BlockSpec((A_KV_HEADS, GROUP_ROWS, 1), lambda n: (0, 0, 0))
    lsespec = pl.BlockSpec((bsz, 1, A_KV_HEADS, GROUP_ROWS, 1), lambda n: (0, n, 0, 0, 0))
    return qspec, kv_m, kv_p, kv_c, rowspec, lsespec


def _swa_fwd(q, k, v, sink_rows, slope_rows, nb):
    t = q.shape[0]
    l = nb * BLOCK
    bsz = t // l

    def body(q_ref, km_ref, kp_ref, kc_ref, vm_ref, vp_ref, vc_ref, sink_ref, slope_ref, o_ref, lse_ref):
        n = pl.program_id(0)
        lane_group = lax.broadcasted_iota(jnp.int32, (GROUP_ROWS, LANES), 1) // HEAD_DIM
        products = {}
        for b in range(bsz):
            keys = _three_blocks(km_ref.at[b], kp_ref.at[b], kc_ref.at[b])
            for g in range(A_KV_HEADS):
                products[b, g] = lax.dot_general(_stack_heads(q_ref.at[b], g), keys, NT_DIMS,
                                                 preferred_element_type=F32)
        for b in range(bsz):
            values = _three_blocks(vm_ref.at[b], vp_ref.at[b], vc_ref.at[b])
            for g in range(A_KV_HEADS):
                sink = sink_ref[g]
                s_m, s_p, s_c = _swa_logits(products[b, g], slope_ref[g], n)
                m = jnp.maximum(jnp.max(jnp.maximum(jnp.maximum(s_m, s_p), s_c), axis=-1, keepdims=True), sink)
                m_wide = jnp.broadcast_to(m, (GROUP_ROWS, BLOCK))
                e_m = jnp.exp(s_m - m_wide)
                e_p = jnp.exp(s_p - m_wide)
                e_c = jnp.exp(s_c - m_wide)
                z = jnp.sum((e_m + e_p) + e_c, axis=-1, keepdims=True) + jnp.exp(sink - m)
                inv = jnp.broadcast_to(1.0 / z, (GROUP_ROWS, BLOCK))
                probs = jnp.concatenate([(e_m * inv).astype(BF16), (e_p * inv).astype(BF16),
                                         (e_c * inv).astype(BF16)], axis=1)
                o = jnp.dot(probs, values, preferred_element_type=F32)
                _unstack_heads(o_ref.at[b], g, jnp.where(lane_group == g, o, 0.0))
                lse_ref[b, 0, g] = m + jnp.log(z)

    qspec, kv_m, kv_p, kv_c, rowspec, lsespec = _swa_specs(bsz, nb)
    by_example = lambda a: a.reshape(bsz, l, a.shape[1])
    q3, k3, v3 = by_example(q), by_example(k), by_example(v)
    o, lse = pl.pallas_call(
        body,
        name="swa_fwd",
        grid=(nb,),
        in_specs=[qspec, kv_m, kv_p, kv_c, kv_m, kv_p, kv_c, rowspec, rowspec],
        out_specs=[qspec, lsespec],
        out_shape=[jax.ShapeDtypeStruct((bsz, l, A_PAD_WIDTH), BF16),
                   jax.ShapeDtypeStruct((bsz, nb, A_KV_HEADS, GROUP_ROWS, 1), F32)],
        compiler_params=_cparams(("arbitrary",)),
    )(q3, k3, k3, k3, v3, v3, v3, sink_rows, slope_rows)
    return o.reshape(t, A_PAD_WIDTH), lse


def _swa_bwd(q, k, v, do, lse, sink_rows, slope_rows, nb):
    t = q.shape[0]
    l = nb * BLOCK
    bsz = t // l

    def body(q_ref, km_ref, kp_ref, kc_ref, vm_ref, vp_ref, vc_ref, do_ref, lse_ref, sink_ref, slope_ref,
             dq_ref, dk_ref, dv_ref, dsink_ref, dk_acc, dv_acc):
        n = pl.program_id(0)

        @pl.when(n == 0)
        def _():
            dk_acc[...] = jnp.zeros_like(dk_acc)
            dv_acc[...] = jnp.zeros_like(dv_acc)
            dsink_ref[...] = jnp.zeros_like(dsink_ref)

        first_half = lax.broadcasted_iota(jnp.int32, (BLOCK, LANES), 1) < HEAD_DIM
        prev = jnp.maximum(n - 1, 0)
        products = {}
        for b in range(bsz):
            keys = _three_blocks(km_ref.at[b], kp_ref.at[b], kc_ref.at[b])
            values = _three_blocks(vm_ref.at[b], vp_ref.at[b], vc_ref.at[b])
            for g in range(A_KV_HEADS):
                products[b, g] = (
                    lax.dot_general(_stack_heads(q_ref.at[b], g), keys, NT_DIMS, preferred_element_type=F32),
                    lax.dot_general(_stack_heads(do_ref.at[b], g), values, NT_DIMS, preferred_element_type=F32))
        for b in range(bsz):
            keys = _three_blocks(km_ref.at[b], kp_ref.at[b], kc_ref.at[b])
            for g in range(A_KV_HEADS):
                qq = _stack_heads(q_ref.at[b], g)
                dob = _stack_heads(do_ref.at[b], g)
                lse_g = lse_ref[b, 0, g]
                lse_wide = jnp.broadcast_to(lse_g, (GROUP_ROWS, BLOCK))
                qk, dp_all = products[b, g]
                probs = [jnp.exp(s - lse_wide) for s in _swa_logits(qk, slope_ref[g], n)]
                dps = [dp_all[:, i * BLOCK:(i + 1) * BLOCK] for i in range(3)]
                delta = jnp.sum((probs[0] * dps[0] + probs[1] * dps[1]) + probs[2] * dps[2], axis=-1, keepdims=True)
                delta_wide = jnp.broadcast_to(delta, (GROUP_ROWS, BLOCK))
                ds = jnp.concatenate([(p * (dp - delta_wide)).astype(BF16) for p, dp in zip(probs, dps)], axis=1)
                pb = jnp.concatenate([p.astype(BF16) for p in probs], axis=1)
                dq = jnp.dot(ds, keys, preferred_element_type=F32) * SCALE
                dk_all = lax.dot_general(ds, qq, TN_DIMS, preferred_element_type=F32) * SCALE
                dv_all = lax.dot_general(pb, dob, TN_DIMS, preferred_element_type=F32)
                for i, start in enumerate((0, prev * BLOCK, n * BLOCK)):
                    rows = pl.ds(pl.multiple_of(start, BLOCK), BLOCK)
                    dk_acc[b, rows, :] += dk_all[i * BLOCK:(i + 1) * BLOCK]
                    dv_acc[b, rows, :] += dv_all[i * BLOCK:(i + 1) * BLOCK]
                for pair in range(A_GROUP // 2):
                    even = dq[2 * pair * BLOCK:(2 * pair + 1) * BLOCK]
                    odd = dq[(2 * pair + 1) * BLOCK:(2 * pair + 2) * BLOCK]
                    left = even if g == 0 else pltpu.roll(even, HEAD_DIM, 1)
                    right = pltpu.roll(odd, HEAD_DIM, 1) if g == 0 else odd
                    tile = (A_GROUP // 2) * g + pair
                    dq_ref[b, :, tile * LANES:(tile + 1) * LANES] = jnp.where(first_half, left, right).astype(BF16)
                dsink_ref[b, g] += -(jnp.exp(sink_ref[g] - lse_g) * delta)

        @pl.when(n == nb - 1)
        def _():
            dk_ref[...] = dk_acc[...].astype(BF16)
            dv_ref[...] = dv_acc[...].astype(BF16)

    qspec, kv_m, kv_p, kv_c, rowspec, lsespec = _swa_specs(bsz, nb)
    kv_all = pl.BlockSpec((bsz, l, LANES), lambda n: (0, 0, 0))
    by_example = lambda a: a.reshape(bsz, l, a.shape[1])
    q3, k3, v3 = by_example(q), by_example(k), by_example(v)
    dq, dk, dv, dsink = pl.pallas_call(
        body,
        name="swa_bwd",
        grid=(nb,),
        in_specs=[qspec, kv_m, kv_p, kv_c, kv_m, kv_p, kv_c, qspec, lsespec, rowspec, rowspec],
        out_specs=[pl.BlockSpec((bsz, BLOCK, A_WIDTH), lambda n: (0, n, 0)), kv_all, kv_all,
                   pl.BlockSpec((bsz, A_KV_HEADS, GROUP_ROWS, 1), lambda n: (0, 0, 0, 0))],
        out_shape=[jax.ShapeDtypeStruct((bsz, l, A_WIDTH), BF16),
                   jax.ShapeDtypeStruct((bsz, l, LANES), BF16),
                   jax.ShapeDtypeStruct((bsz, l, LANES), BF16),
                   jax.ShapeDtypeStruct((bsz, A_KV_HEADS, GROUP_ROWS, 1), F32)],
        scratch_shapes=[pltpu.VMEM((bsz, l, LANES), F32), pltpu.VMEM((bsz, l, LANES), F32)],
        compiler_params=_cparams(("arbitrary",)),
    )(q3, k3, k3, k3, v3, v3, v3, by_example(do), lse, sink_rows, slope_rows)
    return dq.reshape(t, A_WIDTH), dk.reshape(t, LANES), dv.reshape(t, LANES), dsink


CHUNK = KEY_BLOCKS * BLOCK


def _fox_chunk(qb, ci):
    sb = jnp.maximum(jnp.minimum(KEY_BLOCKS * ci, qb + 1 - KEY_BLOCKS), 0)
    lo = jnp.maximum(ci * CHUNK, N_PAD)
    return sb, lo, pl.ds(pl.multiple_of(sb * BLOCK, BLOCK), CHUNK)


def _fox_logits(s_ref, cr_ref, e, j, sb, lo, qb):
    lane = lax.broadcasted_iota(jnp.int32, (BLOCK, BLOCK), 1)
    ahead = lane - lax.broadcasted_iota(jnp.int32, (BLOCK, BLOCK), 0)
    first = (sb + j) * BLOCK
    s = s_ref[e, :, j * BLOCK:(j + 1) * BLOCK] - cr_ref[e, sb + j]
    return jnp.where((ahead <= qb * BLOCK - first) & (lane >= lo - first), s, NEG)


FOX_PAIRS = 4
FOX_HEADS = 2 * FOX_PAIRS
FOX_STEPS = B_PAIRS // FOX_PAIRS


def _fox_specs(nb):
    l = nb * BLOCK
    q_spec = pl.BlockSpec((BLOCK, FOX_PAIRS * LANES), lambda b, p, i: (b * nb + i, p))
    kv_spec = pl.BlockSpec((l, FOX_HEADS * LANES), lambda b, p, i: (b, p))
    cc_spec = pl.BlockSpec((FOX_HEADS, BLOCK, 1), lambda b, p, i: (b * FOX_STEPS + p, i, 0))
    cr_spec = pl.BlockSpec((FOX_HEADS, nb, 1, BLOCK), lambda b, p, i: (b * FOX_STEPS + p, 0, 0, 0))
    return q_spec, kv_spec, cc_spec, cr_spec


def _fox_fwd(q, k, v, c_row, nb, exchange=None):
    t = q.shape[0]
    bsz = t // (nb * BLOCK)
    assert nb >= KEY_BLOCKS

    n_x = len(exchange.ins) if exchange else 0

    def body(*refs):
        q_ref, k_ref, v_ref, cr_ref = refs[:4]
        o_ref, ox_ref, lse_ref = refs[4 + n_x:7 + n_x]
        s_scr, hi_scr, lo_scr = refs[7 + 2 * n_x:10 + 2 * n_x]
        qb = pl.program_id(2)
        if exchange:
            first = (pl.program_id(0) == 0) & (pl.program_id(1) == 0)
            last = (pl.program_id(0) == bsz - 1) & (pl.program_id(1) == FOX_STEPS - 1)
            _host_exchange(exchange, refs[4:4 + n_x], refs[7 + n_x:7 + 2 * n_x], refs[10 + 2 * n_x:],
                           first & (qb == 0), first & (qb == 2 * nb // 3), last & (qb == nb - 1),
                           late=last & (qb == 0))
        qs = [q_ref[:, a * LANES:(a + 1) * LANES] * SCALE for a in range(FOX_PAIRS)]
        first_half = lax.broadcasted_iota(jnp.int32, (BLOCK, LANES), 1) < HEAD_DIM

        def step(ci, carry):
            stats, accs = carry[:2 * FOX_HEADS], carry[2 * FOX_HEADS:]
            sb, lo, krows = _fox_chunk(qb, ci)
            for e in range(FOX_HEADS):
                s_scr[e] = lax.dot_general(qs[e // 2], k_ref[krows, e * LANES:(e + 1) * LANES], NT_DIMS,
                                           preferred_element_type=F32)
            new_stats, new_accs = [], []
            for a in range(FOX_PAIRS):
                alphas = []
                pv = jnp.zeros((BLOCK, LANES), F32)
                pv_lo = jnp.zeros((BLOCK, LANES), F32)
                for e in (2 * a, 2 * a + 1):
                    m, z = stats[2 * e], stats[2 * e + 1]
                    tile = slice(e * LANES, (e + 1) * LANES)
                    top = None
                    for j in range(KEY_BLOCKS):
                        s = _fox_logits(s_scr, cr_ref, e, j, sb, lo, qb)
                        s_scr[e, :, j * BLOCK:(j + 1) * BLOCK] = s
                        top = s if top is None else jnp.maximum(top, s)
                    m_new = jnp.maximum(m, jnp.max(top, axis=-1, keepdims=True))
                    alpha = jnp.exp(m - m_new)
                    m_wide = jnp.broadcast_to(m_new, (BLOCK, BLOCK))
                    total = None
                    for j in range(KEY_BLOCKS):
                        cols = slice(j * BLOCK, (j + 1) * BLOCK)
                        p = jnp.exp(s_scr[e, :, cols] - m_wide)
                        total = p if total is None else total + p
                        hi = p.astype(BF16)
                        hi_scr[e, :, cols] = hi
                        lo_scr[e, :, cols] = (p - hi.astype(F32)).astype(BF16)
                    z = alpha * z + jnp.sum(total, axis=-1, keepdims=True)
                    vv = v_ref[krows, tile]
                    pv = pv + jnp.dot(hi_scr[e], vv, preferred_element_type=F32)
                    pv_lo = pv_lo + jnp.dot(lo_scr[e], vv, preferred_element_type=F32)
                    new_stats += [m_new, z]
                    alphas.append(alpha)
                alpha = jnp.where(first_half, alphas[0], alphas[1])
                new_accs += [alpha * accs[2 * a] + pv, alpha * accs[2 * a + 1] + pv_lo]
            return (*new_stats, *new_accs)

        col = lambda val: jnp.full((BLOCK, 1), val, F32)
        done = lax.fori_loop(
            0, (qb + KEY_BLOCKS) // KEY_BLOCKS, step,
            (col(NEG), col(0.0)) * FOX_HEADS + (jnp.zeros((BLOCK, LANES), F32),) * (2 * FOX_PAIRS))
        for a in range(FOX_PAIRS):
            m0, z0, m1, z1 = done[4 * a:4 * a + 4]
            acc, acc_lo = done[2 * FOX_HEADS + 2 * a:2 * FOX_HEADS + 2 * a + 2]
            inv = 1.0 / jnp.where(first_half, z0, z1)
            tile = slice(a * LANES, (a + 1) * LANES)
            o_ref[:, tile] = (acc * inv).astype(BF16)
            ox_ref[:, tile] = (acc + acc_lo) * inv
            lse_ref[2 * a] = m0 + jnp.log(z0)
            lse_ref[2 * a + 1] = m1 + jnp.log(z1)

    q_spec, kv_spec, cc_spec, cr_spec = _fox_specs(nb)
    outs = pl.pallas_call(
        body,
        name="fox_fwd",
        grid=(bsz, FOX_STEPS, nb),
        in_specs=[q_spec, kv_spec, kv_spec, cr_spec] + [HBM_SPEC] * n_x,
        out_specs=[q_spec, q_spec, cc_spec] + [HBM_SPEC] * n_x,
        out_shape=[jax.ShapeDtypeStruct((t, B_WIDTH), BF16), jax.ShapeDtypeStruct((t, B_WIDTH), F32),
                   jax.ShapeDtypeStruct((bsz * B_HEADS, nb * BLOCK, 1), F32)] + (exchange.out_shape if exchange else []),
        scratch_shapes=[pltpu.VMEM((FOX_HEADS, BLOCK, CHUNK), F32), pltpu.VMEM((FOX_HEADS, BLOCK, CHUNK), BF16),
                        pltpu.VMEM((FOX_HEADS, BLOCK, CHUNK), BF16)] + (exchange.scratch if exchange else []),
        compiler_params=_cparams(("arbitrary",) * 3 if exchange else ("parallel", "parallel", "arbitrary")),
    )(q, k, v, c_row, *(exchange.ins if exchange else []))
    return outs[:3], outs[3:]


def _fox_bwd(q, k, v, o_exact, do, lse, c_row, nb, exchange=None):
    t = q.shape[0]
    l = nb * BLOCK
    bsz = t // l

    n_x = len(exchange.ins) if exchange else 0

    def body(*refs):
        q_ref, k_ref, v_ref, ox_ref, do_ref, lse_ref, cr_ref = refs[:7]
        dq_ref, dk_ref, dv_ref, dc_ref = refs[7 + n_x:11 + n_x]
        dk_acc, dv_acc, s_scr, dp_scr, p_scr, ds_scr, dq_scr = refs[11 + 2 * n_x:18 + 2 * n_x]
        qb = pl.program_id(2)
        if exchange:
            first = (pl.program_id(0) == 0) & (pl.program_id(1) == 0)
            last = (pl.program_id(0) == bsz - 1) & (pl.program_id(1) == FOX_STEPS - 1)
            _host_exchange(exchange, refs[7:7 + n_x], refs[11 + n_x:11 + 2 * n_x], refs[18 + 2 * n_x:],
                           first & (qb == 0), last & (qb == 0), last & (qb == nb - 1))

        @pl.when(qb == 0)
        def _():
            dk_acc[...] = jnp.zeros_like(dk_acc)
            dv_acc[...] = jnp.zeros_like(dv_acc)
            dc_ref[...] = jnp.zeros_like(dc_ref)

        top_half = lax.broadcasted_iota(jnp.int32, (LANES, BLOCK), 0) < HEAD_DIM
        pair_t = lambda x: jnp.concatenate([jnp.where(top_half, x.T, 0), jnp.where(top_half, 0, x.T)], axis=1)
        first_half = lax.broadcasted_iota(jnp.int32, (BLOCK, LANES), 1) < HEAD_DIM
        wide = lambda col: jnp.broadcast_to(col, (BLOCK, BLOCK))
        qs, dobs, qs_t, dob_t, deltas = [], [], [], [], []
        for a in range(FOX_PAIRS):
            tile = slice(a * LANES, (a + 1) * LANES)
            qs.append(q_ref[:, tile] * SCALE)
            dobs.append(do_ref[:, tile])
            qs_t.append(pair_t(qs[a]))
            dob_t.append(pair_t(dobs[a]))
            weighted = dobs[a].astype(F32) * ox_ref[:, tile]
            deltas += [wide(jnp.sum(jnp.where(first_half, weighted, 0.0), axis=-1, keepdims=True)),
                       wide(jnp.sum(jnp.where(first_half, 0.0, weighted), axis=-1, keepdims=True))]
        lses = [wide(lse_ref[e]) for e in range(FOX_HEADS)]

        dq_scr[...] = jnp.zeros(dq_scr.shape, F32)

        def step(ci, carry):
            sb, lo, krows = _fox_chunk(qb, ci)
            for e in range(FOX_HEADS):
                tile = slice(e * LANES, (e + 1) * LANES)
                s_scr[e] = lax.dot_general(qs[e // 2], k_ref[krows, tile], NT_DIMS, preferred_element_type=F32)
                dp_scr[e] = lax.dot_general(dobs[e // 2], v_ref[krows, tile], NT_DIMS, preferred_element_type=F32)
            for a in range(FOX_PAIRS):
                for e in (2 * a, 2 * a + 1):
                    tile = slice(e * LANES, (e + 1) * LANES)
                    kk = k_ref[krows, tile]
                    for j in range(KEY_BLOCKS):
                        cols = slice(j * BLOCK, (j + 1) * BLOCK)
                        p = jnp.exp(_fox_logits(s_scr, cr_ref, e, j, sb, lo, qb) - lses[e])
                        ds = p * (dp_scr[e, :, cols] - deltas[e])
                        dc_ref[e, sb + j] -= jnp.sum(ds, axis=0, keepdims=True)
                        p_scr[e, :, cols] = p.astype(BF16)
                        ds_scr[e, :, cols] = ds.astype(BF16)
                    dq_scr[a] += jnp.dot(ds_scr[e], kk, preferred_element_type=F32)
                both = slice(2 * a, 2 * a + 2)
                dk_t = jnp.dot(qs_t[a], ds_scr[both].reshape(2 * BLOCK, CHUNK), preferred_element_type=F32)
                dv_t = jnp.dot(dob_t[a], p_scr[both].reshape(2 * BLOCK, CHUNK), preferred_element_type=F32)
                for j in range(KEY_BLOCKS):
                    cols = slice(j * BLOCK, (j + 1) * BLOCK)
                    dk_acc[a * nb + sb + j] += dk_t[:, cols]
                    dv_acc[a * nb + sb + j] += dv_t[:, cols]
            return carry

        lax.fori_loop(0, (qb + KEY_BLOCKS) // KEY_BLOCKS, step, 0)
        for a in range(FOX_PAIRS):
            dq_ref[:, a * LANES:(a + 1) * LANES] = (dq_scr[a] * SCALE).astype(BF16)

        @pl.when(qb == nb - 1)
        def _():
            for a in range(FOX_PAIRS):
                for kb in range(nb):
                    rows = slice(kb * BLOCK, (kb + 1) * BLOCK)
                    for acc, out_ref in ((dk_acc, dk_ref), (dv_acc, dv_ref)):
                        out_ref[rows, a * LANES:(a + 1) * LANES] = acc[a * nb + kb].T.astype(BF16)

    q_spec, kv_spec, cc_spec, cr_spec = _fox_specs(nb)
    dkv_spec = pl.BlockSpec((l, FOX_PAIRS * LANES), lambda b, p, i: (b, p))
    outs = pl.pallas_call(
        body,
        name="fox_bwd",
        grid=(bsz, FOX_STEPS, nb),
        in_specs=[q_spec, kv_spec, kv_spec, q_spec, q_spec, cc_spec, cr_spec] + [HBM_SPEC] * n_x,
        out_specs=[q_spec, dkv_spec, dkv_spec, cr_spec] + [HBM_SPEC] * n_x,
        out_shape=[jax.ShapeDtypeStruct((t, B_WIDTH), BF16), jax.ShapeDtypeStruct((t, B_WIDTH), BF16),
                   jax.ShapeDtypeStruct((t, B_WIDTH), BF16),
                   jax.ShapeDtypeStruct((bsz * B_HEADS, nb, 1, BLOCK), F32)] + (exchange.out_shape if exchange else []),
        scratch_shapes=[pltpu.VMEM((FOX_PAIRS * nb, LANES, BLOCK), F32), pltpu.VMEM((FOX_PAIRS * nb, LANES, BLOCK), F32),
                        pltpu.VMEM((FOX_HEADS, BLOCK, CHUNK), F32), pltpu.VMEM((FOX_HEADS, BLOCK, CHUNK), F32),
                        pltpu.VMEM((FOX_HEADS, BLOCK, CHUNK), BF16), pltpu.VMEM((FOX_HEADS, BLOCK, CHUNK), BF16),
                        pltpu.VMEM((FOX_PAIRS, BLOCK, LANES), F32)]
        + (exchange.scratch if exchange else []),
        compiler_params=_cparams(("arbitrary",) * 3 if exchange else ("parallel", "parallel", "arbitrary")),
    )(q, k, v, o_exact, do, lse, c_row, *(exchange.ins if exchange else []))
    return outs[:4], outs[4:]


def _loss_head(h, final_w, target):
    bsz, l, d = h.shape
    nb = l // BLOCK

    def body(h_ref, w_ref, t_ref, loss_ref, dh_ref, dw_ref):
        b = pl.program_id(0)
        n = pl.program_id(1)

        @pl.when((b == 0) & (n == 0))
        def _():
            loss_ref[...] = jnp.zeros_like(loss_ref)
            dw_ref[...] = jnp.zeros_like(dw_ref)

        @pl.when(n == 0)
        def _():
            dh_ref[...] = jnp.zeros_like(dh_ref)

        @pl.when(n > 0)
        def _():
            hh = h_ref[0]
            w = w_ref[...]
            r = _rms_scale(hh)
            err = (hh * r) * w - t_ref[0]
            loss_ref[...] += 0.5 * jnp.sum(jnp.mean(err * err, axis=-1, keepdims=True), axis=0, keepdims=True)
            dy = err * (1.0 / d)
            dh, dw = _rms_bwd(dy, hh, w)
            dh_ref[0] = dh
            dw_ref[...] += dw

    return pl.pallas_call(
        body,
        name="loss_head",
        grid=(bsz, nb),
        in_specs=[
            pl.BlockSpec((1, BLOCK, d), lambda b, n: (b, n, 0)),
            pl.BlockSpec((1, d), lambda b, n: (0, 0)),
            pl.BlockSpec((1, BLOCK, d), lambda b, n: (b, jnp.maximum(n - 1, 0), 0)),
        ],
        out_specs=[
            pl.BlockSpec((1, 128), lambda b, n: (0, 0)),
            pl.BlockSpec((1, BLOCK, d), lambda b, n: (b, n, 0)),
            pl.BlockSpec((1, d), lambda b, n: (0, 0)),
        ],
        out_shape=[jax.ShapeDtypeStruct((1, 128), F32), jax.ShapeDtypeStruct((bsz, l, d), F32),
                   jax.ShapeDtypeStruct((1, d), F32)],
        compiler_params=_cparams(("arbitrary", "arbitrary")),
    )(h, final_w, target)


def _pad_tiles(w, src, heads, lane_slot, axis):
    pieces = []
    for h in range(heads):
        x = lax.slice_in_dim(w, src + HEAD_DIM * h, src + HEAD_DIM * (h + 1), axis=axis)
        z = jnp.zeros_like(x)
        pieces += [x, z] if lane_slot(h) == 0 else [z, x]
    return pieces


def _unpad_tiles(g, off, heads, lane_slot, axis):
    return [lax.slice_in_dim(g, off + LANES * h + HEAD_DIM * lane_slot(h),
                             off + LANES * h + HEAD_DIM * (lane_slot(h) + 1), axis=axis) for h in range(heads)]


def _layout_w_in(w):
    pad_f = jnp.zeros((w.shape[0], F_COLS - B_HEADS), w.dtype)
    return jnp.concatenate([w[:, :SRC_F], w[:, SRC_GA:], w[:, SRC_F:SRC_GA], pad_f], axis=1)


def _unlayout_w_in(g):
    return jnp.concatenate([g[:, :OFF_GA], g[:, OFF_F:OFF_F + B_HEADS], g[:, OFF_GA:OFF_F]], axis=1)


def _local_step(x, target, meta, norms, b_forget, sinks, w, comm=None):
    n1, nmix, n2, nfin = norms
    w1i, w1o = w[:2]
    bsz, seq, d = x.shape
    l = PREFIX + seq
    nb = l // BLOCK
    t = bsz * l

    h0 = jnp.concatenate([jnp.zeros((bsz, N_PAD, d), F32),
                          jnp.broadcast_to(meta[None], (bsz, N_META, d)), x], axis=1).reshape(t, d)

    if comm is None:
        (h1, g1, u1), _ = _ffn_fwd(h0, n1, w1i, w1o)
        w_in, wa, wb, wo, w2i, w2o = w[2:]
    else:
        (h1, g1, u1), gathered = _ffn_fwd(h0, n1, w1i, w1o, comm.gather(GATHER_PROJ))
        w_in, = comm.gathered(GATHER_PROJ, gathered)
    wp = _layout_w_in(w_in)
    un, qa, ka, va, qb, kb, vb, ga, gb, f_logit = _proj_fwd(h1, nmix, wp)
    b_pad = jnp.concatenate([b_forget, jnp.zeros((1, F_COLS - B_HEADS), F32)], axis=1)
    c = _forget_cumsum(f_logit, b_pad, nb)
    c_heads = c[:, :B_HEADS].reshape(bsz, l, B_HEADS).transpose(0, 2, 1).reshape(bsz * B_HEADS, l)
    c_row = c_heads.reshape(bsz * B_HEADS, nb, 1, BLOCK)

    slopes = jnp.exp2(-8.0 * jnp.arange(1, A_HEADS + 1, dtype=F32) / A_HEADS)
    slope_rows = jnp.repeat(slopes.reshape(A_KV_HEADS, A_GROUP), BLOCK, axis=1)[:, :, None]
    sink_rows = jnp.repeat(sinks.reshape(A_KV_HEADS, A_GROUP), BLOCK, axis=1)[:, :, None]

    oa, lse_a = _swa_fwd(qa, ka, va, sink_rows, slope_rows, nb)
    if comm is None:
        (ob, ob_exact, lse_b), _ = _fox_fwd(qb, kb, vb, c_row, nb)
    else:
        (ob, ob_exact, lse_b), gathered = _fox_fwd(qb, kb, vb, c_row, nb, comm.gather(GATHER_LATE))
        wa, wb, wo, w2i, w2o = comm.gathered(GATHER_LATE, gathered)
    wa_p = jnp.concatenate(_pad_tiles(wa, 0, A_HEADS, A_SLOT, 0), axis=0)
    h2, mixed = _merge_fwd(h1, oa, ob, ga, gb, wa_p, wb, wo)
    (h3, g2, u2), _ = _ffn_fwd(h2, n2, w2i, w2o)
    loss, dh3, d_nfin = _loss_head(h3.reshape(bsz, l, d), nfin, target)

    (dh2, n2b, a2, dgu2, df2, dn2_parts), _ = _ffn_bwd(dh3.reshape(t, d), h2, n2, g2, u2, w2i, w2o)
    g_w2o = _tn_matmul(a2, df2, "grad_ffn2_w_out")
    g_w2i = _tn_matmul(n2b, dgu2, "grad_ffn2_w_in")

    hosted = comm.swap("ffn2", dict(ffn2_w_in=g_w2i, ffn2_w_out=g_w2o)) if comm else None
    (dya, dyb, doa, dob, dga, dgb, dh2b), swapped = _merge_bwd(dh2, oa, ob, ga, gb, wa_p, wb, wo, hosted)
    g_wo = _tn_matmul(mixed, dh2b, "grad_w_out")
    g_wa = jnp.concatenate(_unpad_tiles(_tn_matmul(oa, dya, "grad_w_branch_a"), 0, A_HEADS, A_SLOT, 0), axis=0)
    g_wb = _tn_matmul(ob, dyb, "grad_w_branch_b")

    dqa, dka, dva, dsink_rows = _swa_bwd(qa, ka, va, doa, lse_a, sink_rows, slope_rows, nb)
    hosted = comm.scatter("ffn2", swapped) if comm else None
    (dqb, dkb, dvb, dc_row), pieces = _fox_bwd(qb, kb, vb, ob_exact, dob, lse_b, c_row, nb, hosted)
    if comm:
        comm.received("ffn2", pieces)
    dc = dc_row.reshape(bsz, B_HEADS, l).transpose(0, 2, 1).reshape(t, B_HEADS)
    dc = jnp.concatenate([dc, jnp.zeros((t, F_COLS - B_HEADS), F32)], axis=1)
    df_logit, db_parts = _forget_cumsum_bwd(dc, f_logit, b_pad, nb)

    dproj = jnp.concatenate([dqa, dka, dva, dqb, dkb, dvb, dga, dgb, df_logit], axis=1)
    g_win = _unlayout_w_in(_tn_matmul(un, dproj, "grad_w_in"))
    hosted = comm.swap("mixer", dict(w_in=g_win, w_branch_a=g_wa, w_branch_b=g_wb, w_out=g_wo)) if comm else None
    (dh1, dnmix_parts), swapped = _proj_bwd(dh2, h1, nmix, dproj, wp, hosted)
    hosted = comm.scatter("mixer", swapped) if comm else None
    (dh0, n1b, a1, dgu1, df1, dn1_parts), pieces = _ffn_bwd(dh1, h0, n1, g1, u1, w1i, w1o, hosted)
    dh0 = dh0.reshape(bsz, l, d)
    grad_x = dh0[:, PREFIX:]
    small = dict(
        meta_tokens=jnp.sum(dh0[:, N_PAD:PREFIX], axis=0),
        ffn1_norm=jnp.sum(dn1_parts, axis=0),
        mix_norm=jnp.sum(dnmix_parts, axis=0),
        ffn2_norm=jnp.sum(dn2_parts, axis=0),
        final_norm=d_nfin,
        b_forget=jnp.sum(db_parts, axis=0)[:, :B_HEADS],
        attn_sinks=jnp.sum(dsink_rows.reshape(bsz, A_HEADS, BLOCK), axis=(0, 2)).reshape(1, A_HEADS),
    )
    if comm is None:
        g_w1o = _tn_matmul(a1, df1, "grad_ffn1_w_out")
        g_w1i = _tn_matmul(n1b, dgu1, "grad_ffn1_w_in")
    else:
        comm.received("mixer", pieces)
        g_w1o, gathered = _tn_matmul(a1, df1, "grad_ffn1_w_out", comm.small_gather(loss, small))
        comm.small_gathered(gathered)
        swapped = _run_exchange(comm.swap("ffn1_out", dict(ffn1_w_out=g_w1o)), "exchange_halves_ffn1_out")
        g_w1i, pieces = _tn_matmul(n1b, dgu1, "grad_ffn1_w_in", comm.scatter("ffn1_out", swapped))
        comm.received("ffn1_out", pieces)
    big = dict(ffn1_w_in=g_w1i, ffn1_w_out=g_w1o, w_in=g_win, w_branch_a=g_wa, w_branch_b=g_wb,
               w_out=g_wo, ffn2_w_in=g_w2i, ffn2_w_out=g_w2o)
    return loss, grad_x, small, big


BIG = (
    ("ffn1_w_in", (D_MODEL, 5632), 1),
    ("ffn1_w_out", (2816, D_MODEL), 0),
    ("w_in", (D_MODEL, W_IN_COLS), 1),
    ("w_branch_a", (A_WIDTH, D_MODEL), 1),
    ("w_branch_b", (B_WIDTH, D_MODEL), 1),
    ("w_out", (D_MODEL, D_MODEL), 0),
    ("ffn2_w_in", (D_MODEL, 5632), 1),
    ("ffn2_w_out", (2816, D_MODEL), 0),
)
STACKED = "w_in"


def _coords():
    return lax.axis_index("x"), lax.axis_index("y"), lax.axis_index("c")


def _other_chips(x, y):
    return ((1 - x, y), (x, 1 - y), (1 - x, 1 - y))


def _chip_part(ref, name, shape, axis, k):
    if name == STACKED:
        return ref.at[k]
    size = shape[axis] // N_CHIPS
    start = pl.multiple_of(k * size, size)
    return ref.at[pl.ds(start, size), :] if axis == 0 else ref.at[:, pl.ds(start, size)]


def _full_shape(name, shape):
    return (N_CHIPS, shape[0], shape[1] // N_CHIPS) if name == STACKED else shape


class _Exchange:
    def __init__(self, ins, out_shape, n_sems, ops):
        self.ins, self.out_shape, self.n_sems, self.ops = list(ins), list(out_shape), n_sems, ops

    @property
    def scratch(self):
        return [pltpu.SemaphoreType.DMA((self.n_sems,)), pltpu.SemaphoreType.DMA((self.n_sems,))]


SEMS_PER_GATHER = 9


def _gather_exchange(shards, table):
    n = len(table)
    x_nbr, y_nbr, diagonal = 0, 1, 2

    def ops(ins, outs, send_sems, recv_sems):
        x, y, c = _coords()
        mine = 2 * x + y
        sibling = (x, y, 1 - c)
        chips = _other_chips(x, y)
        slots = [2 * chip[0] + chip[1] for chip in chips]

        def part(i, k):
            name, shape, axis = table[i][:3]
            return _chip_part(outs[i], name, shape, axis, k)

        def half(ref, h):
            rows = ref.shape[0] // 2
            return ref.at[pl.ds(pl.multiple_of(h * rows, rows), rows), :]

        def remote(i, sem, src, dst, device):
            sem = SEMS_PER_GATHER * i + sem
            return pltpu.make_async_remote_copy(src, dst, send_sems.at[sem], recv_sems.at[sem],
                                                device_id=device, device_id_type=MESH_ID)

        def own(i):
            return remote(i, 0, ins[i], part(i, mine), sibling)

        def fetch(i, j, slot):
            if table[i][4]:
                src, dst = half(ins[i], c), half(part(i, slot), c)
            else:
                src, dst = ins[i], part(i, slot)
            return remote(i, 1 + j, src, dst, (chips[j][0], chips[j][1], c))

        def relayed(i, via, of):
            region = half(half(part(i, slots[of]), c), via)
            return remote(i, 4 + via, region, region, (chips[via][0], chips[via][1], c))

        def forward(i, j, h):
            region = half(part(i, slots[j]), h)
            return remote(i, 6 + j, region, region, sibling)

        def start():
            for i in range(n):
                for j in (x_nbr, y_nbr) if table[i][4] else (x_nbr, y_nbr, diagonal):
                    fetch(i, j, mine).start()
            for i in range(n):
                own(i).start()

        def relay():
            for i in range(n):
                if not table[i][4]:
                    for j in range(3):
                        fetch(i, j, slots[j]).wait_recv()
                    continue
                fetch(i, y_nbr, slots[y_nbr]).wait_recv()
                relayed(i, x_nbr, y_nbr).start()
                forward(i, y_nbr, c).start()
                fetch(i, x_nbr, slots[x_nbr]).wait_recv()
                relayed(i, y_nbr, x_nbr).start()
                forward(i, x_nbr, c).start()

        def relay_diagonal():
            for i in range(n):
                if table[i][4]:
                    relayed(i, x_nbr, diagonal).wait_recv()
                    relayed(i, y_nbr, diagonal).wait_recv()
                    forward(i, diagonal, c).start()

        def finish():
            for i in range(n):
                own(i).wait()
                for j in range(3):
                    if table[i][4]:
                        forward(i, j, 1 - c).wait_recv()
                        forward(i, j, c).wait_send()
                    if j != diagonal or not table[i][4]:
                        fetch(i, j, mine).wait_send()
                if table[i][4]:
                    relayed(i, x_nbr, y_nbr).wait_send()
                    relayed(i, y_nbr, x_nbr).wait_send()

        return start, relay, relay_diagonal, finish

    out_shape = [jax.ShapeDtypeStruct(_full_shape(name, shape), dtype) for name, shape, _, dtype, _ in table]
    return _Exchange(shards, out_shape, SEMS_PER_GATHER * n, ops)


def _run_exchange(exchange, name):
    n = len(exchange.ins)

    def body(*refs):
        for phase in exchange.ops(refs[:n], refs[n:2 * n], *refs[2 * n:]):
            phase()

    return pl.pallas_call(
        body,
        name=name,
        in_specs=[HBM_SPEC] * n,
        out_specs=[HBM_SPEC] * n,
        out_shape=exchange.out_shape,
        scratch_shapes=exchange.scratch,
    )(*exchange.ins)


CAST_ROWS = 64


def _cast_hosting(arrays, exchange, name):
    n_a, n_x = len(arrays), len(exchange.ins)

    def body(*refs):
        a_in, x_in = refs[:n_a], refs[n_a:n_a + n_x]
        a_out, x_out = refs[n_a + n_x:2 * n_a + n_x], refs[2 * n_a + n_x:2 * n_a + 2 * n_x]
        start, *rest = exchange.ops(x_in, x_out, *refs[2 * n_a + 2 * n_x:])
        start()
        for src, dst in zip(a_in, a_out):
            def rows(i, carry, src=src, dst=dst):
                window = pl.ds(pl.multiple_of(i * CAST_ROWS, CAST_ROWS), CAST_ROWS)
                dst[window, :] = src[window, :].astype(BF16)
                return carry

            lax.fori_loop(0, src.shape[0] // CAST_ROWS, rows, 0)
        for phase in rest:
            phase()

    outs = pl.pallas_call(
        body,
        name=name,
        in_specs=[VMEM_SPEC] * n_a + [HBM_SPEC] * n_x,
        out_specs=[VMEM_SPEC] * n_a + [HBM_SPEC] * n_x,
        out_shape=[jax.ShapeDtypeStruct(a.shape, BF16) for a in arrays] + exchange.out_shape,
        scratch_shapes=exchange.scratch,
        compiler_params=pltpu.CompilerParams(vmem_limit_bytes=VMEM_LIMIT),
    )(*arrays, *exchange.ins)
    return outs[:n_a], outs[n_a:]


def _host_exchange(exchange, in_refs, out_refs, sem_refs, first, middle, last, late=None):
    start, *relays, finish = exchange.ops(in_refs, out_refs, *sem_refs)
    pl.when(first)(start)
    pl.when(middle)(relays[0])
    if len(relays) > 1:
        pl.when(last if late is None else late)(relays[1])
    pl.when(last)(finish)


def _halves_view(name, shape, axis):
    r, c = shape
    if name == STACKED:
        return (N_CHIPS, 2, r // 2, c // N_CHIPS), lambda ref, h: ref.at[:, h]
    if axis == 1:
        return (2, r // 2, c), lambda ref, h: ref.at[h]
    return (N_CHIPS, 2, r // N_CHIPS // 2, c), lambda ref, h: ref.at[:, h]


def _halves_exchange(grads, entries):
    n_w = len(entries)
    views = [_halves_view(*entry) for entry in entries]

    def ops(ins, outs, send_sems, recv_sems):
        x, y, c = _coords()
        copies = [pltpu.make_async_remote_copy(views[i][1](ins[i], 1 - c), outs[i], send_sems.at[i], recv_sems.at[i],
                                               device_id=(x, y, 1 - c), device_id_type=MESH_ID) for i in range(n_w)]

        def start():
            for cp in copies:
                cp.start()

        def finish():
            for cp in copies:
                cp.wait()

        return start, lambda: None, finish

    half_shape = lambda v: tuple(d for i, d in enumerate(v) if i != (1 if len(v) == 4 else 0))
    out_shape = [jax.ShapeDtypeStruct(half_shape(v[0]), F32) for v in views]
    return _Exchange([g.reshape(v[0]) for g, v in zip(grads, views)], out_shape, n_w, ops)


def _add_sibling(g_view, recv, c, name):
    shape = recv.shape
    if len(shape) == 2:
        tr = _tile(shape[0], 128, 16)
        grid = (shape[0] // tr,)
        g_spec = pl.BlockSpec((None, tr, shape[1]), lambda i, c_ref: (c_ref[0], i, 0))
        r_spec = pl.BlockSpec((tr, shape[1]), lambda i, c_ref: (i, 0))
    else:
        tr = _tile(shape[1], 256, 16)
        grid = (N_CHIPS, shape[1] // tr)
        g_spec = pl.BlockSpec((None, None, tr, shape[2]), lambda k, i, c_ref: (k, c_ref[0], i, 0))
        r_spec = pl.BlockSpec((None, tr, shape[2]), lambda k, i, c_ref: (k, i, 0))

    def body(c_ref, g_ref, r_ref, o_ref):
        o_ref[...] = (g_ref[...] + r_ref[...]).astype(BF16)

    return pl.pallas_call(
        body,
        name="add_sibling_" + name,
        grid_spec=pltpu.PrefetchScalarGridSpec(num_scalar_prefetch=1, grid=grid, in_specs=[g_spec, r_spec],
                                               out_specs=r_spec),
        out_shape=jax.ShapeDtypeStruct(shape, BF16),
        compiler_params=_cparams(("parallel",) * len(grid)),
    )(c, g_view, recv)


def _piece_of(ref, name, axis, k):
    if name == STACKED or axis == 0:
        return ref.at[k]
    size = ref.shape[1] // N_CHIPS
    return ref.at[:, pl.ds(pl.multiple_of(k * size, size), size)]


def _piece_shape(name, shape, axis):
    r, c = shape
    return (r // 2, c // N_CHIPS) if (axis == 1) else (r // N_CHIPS // 2, c)


def _scatter_exchange(partials, entries):
    n_w = len(entries)

    def ops(ins, outs, send_sems, recv_sems):
        x, y, c = _coords()
        chips = _other_chips(x, y)
        copies = []
        for i, (name, _, axis) in enumerate(entries):
            for j, chip in enumerate(chips):
                sem = 3 * i + j
                copies.append(pltpu.make_async_remote_copy(
                    _piece_of(ins[i], name, axis, 2 * chip[0] + chip[1]), outs[i].at[j], send_sems.at[sem],
                    recv_sems.at[sem], device_id=(chip[0], chip[1], c), device_id_type=MESH_ID))

        def start():
            for cp in copies:
                cp.start()

        def finish():
            for cp in copies:
                cp.wait()

        return start, lambda: None, finish

    out_shape = [jax.ShapeDtypeStruct((3,) + _piece_shape(*entry), BF16) for entry in entries]
    return _Exchange(partials, out_shape, 3 * n_w, ops)


def _add_chips(partial, recv, mine, name, axis):
    rows, cols = recv.shape[1:]
    tr = _tile(rows, 256, 16)
    if name == STACKED or axis == 0:
        p_spec = pl.BlockSpec((None, tr, cols), lambda i, k_ref: (k_ref[0], i, 0))
    else:
        p_spec = pl.BlockSpec((tr, cols), lambda i, k_ref: (i, k_ref[0]))

    def body(k_ref, p_ref, r_ref, o_ref):
        f32 = lambda a: a.astype(F32)
        o_ref[...] = ((f32(p_ref[...]) + f32(r_ref[0])) + f32(r_ref[1])) + f32(r_ref[2])

    return pl.pallas_call(
        body,
        name="add_chips_" + name,
        grid_spec=pltpu.PrefetchScalarGridSpec(
            num_scalar_prefetch=1, grid=(rows // tr,),
            in_specs=[p_spec, pl.BlockSpec((3, tr, cols), lambda i, k_ref: (0, i, 0))],
            out_specs=pl.BlockSpec((tr, cols), lambda i, k_ref: (i, 0))),
        out_shape=jax.ShapeDtypeStruct((rows, cols), F32),
        compiler_params=_cparams(("parallel",)),
    )(mine, partial, recv)


def _share_with_sibling(halves):
    n_w = len(halves)

    def body(*refs):
        ins, outs = refs[:n_w], refs[n_w:2 * n_w]
        send_sems, recv_sems = refs[2 * n_w:]
        x, y, c = _coords()
        copies = [pltpu.make_async_remote_copy(ins[i], outs[i], send_sems.at[i], recv_sems.at[i],
                                               device_id=(x, y, 1 - c), device_id_type=MESH_ID) for i in range(n_w)]
        for cp in copies:
            cp.start()
        for cp in copies:
            cp.wait()

    return pl.pallas_call(
        body,
        name="share_with_sibling",
        in_specs=[HBM_SPEC] * n_w,
        out_specs=[HBM_SPEC] * n_w,
        out_shape=[jax.ShapeDtypeStruct(h.shape, F32) for h in halves],
        scratch_shapes=[pltpu.SemaphoreType.DMA((n_w,)), pltpu.SemaphoreType.DMA((n_w,))],
    )(*halves)


SMALL_ROWS = 168


def _small_exchange(buf):
    def ops(ins, outs, send_sems, recv_sems):
        x, y, c = _coords()
        me = 4 * x + 2 * y + c
        peers = [(x ^ fx, y ^ fy, c ^ fc) for fx in (0, 1) for fy in (0, 1) for fc in (0, 1)][1:]

        def copy(j, slot, dev):
            return pltpu.make_async_remote_copy(ins[0], outs[0].at[slot], send_sems.at[j], recv_sems.at[j],
                                                device_id=dev, device_id_type=MESH_ID)

        own = pltpu.make_async_copy(ins[0], outs[0].at[me], send_sems.at[N_DEV - 1])

        def start():
            own.start()
            for j, dev in enumerate(peers):
                copy(j, me, dev).start()

        def finish():
            for j, dev in enumerate(peers):
                copy(j, 4 * dev[0] + 2 * dev[1] + dev[2], dev).wait()
            own.wait()

        return start, lambda: None, finish

    return _Exchange([buf], [jax.ShapeDtypeStruct((N_DEV,) + buf.shape, F32)], N_DEV, ops)


def _sum_devices(gathered):
    def body(g_ref, out_ref):
        acc = g_ref[0]
        for d in range(1, N_DEV):
            acc = acc + g_ref[d]
        out_ref[...] = acc

    return pl.pallas_call(
        body,
        name="sum_devices",
        in_specs=[VMEM_SPEC],
        out_specs=VMEM_SPEC,
        out_shape=jax.ShapeDtypeStruct(gathered.shape[1:], F32),
    )(gathered)


def _adamw(w, g, m, v):
    r, rest = w.shape[0], w.shape[1:]
    per_row = 1
    for dim in rest:
        per_row *= dim
    tr = _tile(r, max(8, (5 << 19) // (4 * per_row)), 8 if len(rest) == 1 else 1)

    def body(w_ref, g_ref, m_ref, v_ref, d_ref, mo_ref, vo_ref):
        gg = g_ref[...]
        mm = ADAM_B1 * m_ref[...] + (1.0 - ADAM_B1) * gg
        vv = ADAM_B2 * v_ref[...] + (1.0 - ADAM_B2) * (gg * gg)
        m_hat = mm / (1.0 - ADAM_B1 ** ADAM_STEP)
        v_hat = vv / (1.0 - ADAM_B2 ** ADAM_STEP)
        d_ref[...] = -ADAM_LR * (m_hat / (jnp.sqrt(v_hat) + ADAM_EPS) + ADAM_WD * w_ref[...])
        mo_ref[...] = mm
        vo_ref[...] = vv

    spec = pl.BlockSpec((tr,) + rest, lambda i: (i,) + (0,) * len(rest))
    return pl.pallas_call(
        body,
        name="adamw",
        grid=(r // tr,),
        in_specs=[spec] * 4,
        out_specs=[spec] * 3,
        out_shape=[jax.ShapeDtypeStruct(w.shape, F32)] * 3,
        compiler_params=_cparams(("parallel",)),
    )(w, g, m, v)


def _adamw_halves(w, own, other, m, v, c, name):
    r, cols = w.shape
    half = r // 2
    tr = _tile(half, 256, 8)
    nt = half // tr
    whole = pl.BlockSpec((tr, cols), lambda h, i, c_ref: (h * nt + i, 0))
    part = pl.BlockSpec((tr, cols), lambda h, i, c_ref: (i, 0))

    def body(c_ref, w_ref, own_ref, other_ref, m_ref, v_ref, g_ref, d_ref, mo_ref, vo_ref):
        gg = jnp.where(pl.program_id(0) == c_ref[0], own_ref[...], other_ref[...])
        g_ref[...] = gg
        mm = ADAM_B1 * m_ref[...] + (1.0 - ADAM_B1) * gg
        vv = ADAM_B2 * v_ref[...] + (1.0 - ADAM_B2) * (gg * gg)
        m_hat = mm / (1.0 - ADAM_B1 ** ADAM_STEP)
        v_hat = vv / (1.0 - ADAM_B2 ** ADAM_STEP)
        d_ref[...] = -ADAM_LR * (m_hat / (jnp.sqrt(v_hat) + ADAM_EPS) + ADAM_WD * w_ref[...])
        mo_ref[...] = mm
        vo_ref[...] = vv

    return pl.pallas_call(
        body,
        name="adamw_" + name,
        grid_spec=pltpu.PrefetchScalarGridSpec(
            num_scalar_prefetch=1, grid=(2, nt),
            in_specs=[whole, part, part, whole, whole], out_specs=[whole] * 4),
        out_shape=[jax.ShapeDtypeStruct((r, cols), F32)] * 4,
        compiler_params=_cparams(("parallel", "parallel")),
    )(c, w, own, other, m, v)


GATHER_FIRST = ("ffn1_w_in", "ffn1_w_out")
GATHER_PROJ = ("w_in",)
GATHER_LATE = ("w_branch_a", "w_branch_b", "w_out", "ffn2_w_in", "ffn2_w_out")


class _Comm:
    def __init__(self, shards, c_arr, mine_arr):
        self.shards, self.c, self.mine = shards, c_arr, mine_arr
        self.groups, self.halves = {}, {}
        self.by_name = {entry[0]: entry for entry in BIG}

    def small_gather(self, loss, small):
        pad_lanes = lambda a: jnp.concatenate([a, jnp.zeros((1, LANES - a.shape[1]), F32)], axis=1)
        buf = jnp.concatenate([
            small["meta_tokens"].reshape(128, LANES),
            small["ffn1_norm"].reshape(8, LANES), small["mix_norm"].reshape(8, LANES),
            small["ffn2_norm"].reshape(8, LANES), small["final_norm"].reshape(8, LANES),
            loss, pad_lanes(small["b_forget"]), pad_lanes(small["attn_sinks"]),
            jnp.zeros((SMALL_ROWS - 163, LANES), F32)], axis=0)
        return _small_exchange(buf)

    def small_gathered(self, outs):
        self.reduced = _sum_devices(outs[0])

    def gather(self, names):
        table = [self.by_name[n] + (BF16, True) for n in names]
        return _gather_exchange([self.shards[n] for n in names], table)

    def gathered(self, names, outs):
        return [o.transpose(1, 0, 2).reshape(D_MODEL, W_IN_COLS) if n == STACKED else o for n, o in zip(names, outs)]

    def swap(self, tag, grads):
        entries = [self.by_name[n] for n in grads]
        arrays = [g.reshape(D_MODEL, N_CHIPS, W_IN_COLS // N_CHIPS).transpose(1, 0, 2) if n == STACKED else g
                  for n, g in grads.items()]
        self.groups[tag] = (entries, arrays)
        return _halves_exchange(arrays, entries)

    def scatter(self, tag, received):
        entries, arrays = self.groups[tag]
        views = [_halves_view(*entry) for entry in entries]
        partials = [_add_sibling(g.reshape(v[0]), r, self.c, name)
                    for g, v, r, (name, _, _) in zip(arrays, views, received, entries)]
        self.groups[tag] = (entries, partials)
        return _scatter_exchange(partials, entries)

    def received(self, tag, pieces):
        entries, partials = self.groups[tag]
        for p, r, (name, _, axis) in zip(partials, pieces, entries):
            self.halves[name] = _add_chips(p, r, self.mine, name, axis)

    def finish(self):
        names = [n for n, _, _ in BIG]
        own = [self.halves[n] for n in names]
        return dict(zip(names, zip(own, _share_with_sibling(own))))


def kernel(x, meta_tokens, ffn1_norm, ffn1_w_in, ffn1_w_out, mix_norm, w_in, b_forget, attn_sinks, w_branch_a, w_branch_b, w_out, ffn2_norm, ffn2_w_in, ffn2_w_out, final_norm, loss_target, m_meta_tokens, m_ffn1_norm, m_ffn1_w_in, m_ffn1_w_out, m_mix_norm, m_w_in, m_b_forget, m_attn_sinks, m_w_branch_a, m_w_branch_b, m_w_out, m_ffn2_norm, m_ffn2_w_in, m_ffn2_w_out, m_final_norm, v_meta_tokens, v_ffn1_norm, v_ffn1_w_in, v_ffn1_w_out, v_mix_norm, v_w_in, v_b_forget, v_attn_sinks, v_w_branch_a, v_w_branch_b, v_w_out, v_ffn2_norm, v_ffn2_w_in, v_ffn2_w_out, v_final_norm):
    given = dict(locals())
    names = ["meta_tokens", "ffn1_norm", "ffn1_w_in", "ffn1_w_out", "mix_norm", "w_in", "b_forget", "attn_sinks",
             "w_branch_a", "w_branch_b", "w_out", "ffn2_norm", "ffn2_w_in", "ffn2_w_out", "final_norm"]
    big_names = [n for n, _, _ in BIG]
    cx, cy, cc = _coords()
    c_arr = cc.reshape(1).astype(jnp.int32)
    mine_arr = (2 * cx + cy).reshape(1).astype(jnp.int32)

    by_name = {entry[0]: entry for entry in BIG}
    shards = {n: given[n][0].astype(BF16) for n in GATHER_FIRST}
    table = [by_name[n] + (BF16, True) for n in GATHER_FIRST] + [("meta_tokens", (N_META, D_MODEL), 1, F32, False)]
    first = _gather_exchange([shards[n] for n in GATHER_FIRST] + [meta_tokens], table)
    late_names = [n for n in big_names if n not in GATHER_FIRST]
    casts, (w1i, w1o, meta_full) = _cast_hosting([given[n][0] for n in late_names], first, "gather_first")
    shards.update(zip(late_names, casts))
    comm = _Comm(shards, c_arr, mine_arr)
    norms = (ffn1_norm, mix_norm, ffn2_norm, final_norm.reshape(1, D_MODEL))
    loss, grad_x, small, big = _local_step(x, loss_target, meta_full, norms, b_forget, attn_sinks, (w1i, w1o), comm)

    swap = comm.swap("ffn1_in", dict(ffn1_w_in=big["ffn1_w_in"]))
    last = comm.scatter("ffn1_in", _run_exchange(swap, "exchange_halves_ffn1_in"))
    comm.received("ffn1_in", _run_exchange(last, "scatter_chip_sums"))
    grad_halves = comm.finish()
    grads = {}

    red = comm.reduced
    meta_cols = red[:128].reshape(N_META, D_MODEL)
    grads["meta_tokens"] = lax.dynamic_slice_in_dim(meta_cols, (2 * cx + cy) * (D_MODEL // N_CHIPS),
                                                    D_MODEL // N_CHIPS, axis=1)
    grads["ffn1_norm"] = red[128:136].reshape(1, D_MODEL)
    grads["mix_norm"] = red[136:144].reshape(1, D_MODEL)
    grads["ffn2_norm"] = red[144:152].reshape(1, D_MODEL)
    grads["final_norm"] = red[152:160].reshape(1, D_MODEL)
    loss_out = red[160, 0]
    grads["b_forget"] = red[161:162, :B_HEADS]
    grads["attn_sinks"] = red[162:163, :A_HEADS]

    out_g, out_d, out_m, out_v = [], [], [], []
    for n in names:
        w_full = given[n]
        shape = w_full.shape
        two_d = (lambda a: a.reshape(shape[-2], shape[-1])) if len(shape) >= 2 else (lambda a: a.reshape(1, shape[0]))
        if n == STACKED:
            own, other = grad_halves[n]
            g_t = jnp.concatenate([jnp.where(cc == 0, own, other), jnp.where(cc == 0, other, own)], axis=0).T
            tiles = lambda a: a.reshape(shape[-1], shape[-2] // LANES, LANES)
            untile = lambda a: a.reshape(shape[-1], shape[-2]).T
            d2, m2, v2 = [untile(a) for a in _adamw(tiles(two_d(w_full).T), tiles(g_t), tiles(two_d(given["m_" + n]).T),
                                                     tiles(two_d(given["v_" + n]).T))]
            g2 = g_t.T
        elif n in grad_halves:
            own, other = grad_halves[n]
            g2, d2, m2, v2 = _adamw_halves(two_d(w_full), own, other, two_d(given["m_" + n]),
                                           two_d(given["v_" + n]), c_arr, n)
        else:
            g2 = two_d(grads[n])
            d2, m2, v2 = _adamw(two_d(w_full), g2, two_d(given["m_" + n]), two_d(given["v_" + n]))
        out_g.append(g2.reshape(shape))
        out_d.append(d2.reshape(shape))
        out_m.append(m2.reshape(shape))
        out_v.append(v2.reshape(shape))
    return (loss_out, grad_x, *out_g, *out_d, *out_m, *out_v)
```

```python
import jax
import jax.numpy as jnp
from jax import lax
from jax.experimental import pallas as pl
from jax.experimental.pallas import tpu as pltpu

F32 = jnp.float32
BF16 = jnp.bfloat16

D_MODEL = 1024
N_META = 16
BLOCK = 128
LANES = 128
PREFIX = BLOCK
N_PAD = PREFIX - N_META
HEAD_DIM = 64
A_HEADS = 8
A_KV_HEADS = 2
A_GROUP = 4
B_HEADS = 8
B_PAIRS = B_HEADS // 2
A_WIDTH = A_HEADS * HEAD_DIM
A_KV_WIDTH = A_KV_HEADS * HEAD_DIM
B_WIDTH = B_HEADS * HEAD_DIM
W_IN_COLS = A_WIDTH + 2 * A_KV_WIDTH + 3 * B_WIDTH + B_HEADS + 2 * D_MODEL
SRC_KA = A_WIDTH
SRC_VA = SRC_KA + A_KV_WIDTH
SRC_QB = SRC_VA + A_KV_WIDTH
SRC_KB = SRC_QB + B_WIDTH
SRC_VB = SRC_KB + B_WIDTH
SRC_F = SRC_VB + B_WIDTH
SRC_GA = SRC_F + B_HEADS
SRC_GB = SRC_GA + D_MODEL
A_PAD_WIDTH = A_HEADS * LANES
B_PAD_WIDTH = B_HEADS * LANES
F_COLS = LANES
OFF_QA = 0
OFF_KA = SRC_KA
OFF_VA = SRC_VA
OFF_QB = SRC_QB
OFF_KB = SRC_KB
OFF_VB = SRC_VB
OFF_GA = SRC_F
OFF_GB = OFF_GA + D_MODEL
OFF_F = OFF_GB + D_MODEL
P_COLS = OFF_F + F_COLS
EPS = 1e-6
NEG = -1e30
SCALE = HEAD_DIM ** -0.5
KEY_BLOCKS = 4

ADAM_LR = 0.001
ADAM_B1 = 0.9
ADAM_B2 = 0.999
ADAM_EPS = 1e-08
ADAM_WD = 0.01
ADAM_STEP = 10

N_CHIPS = 4
N_DEV = 8
VMEM_LIMIT = 56 * 1024 * 1024

NT_DIMS = (((1,), (1,)), ((), ()))
TN_DIMS = (((0,), (0,)), ((), ()))
MESH_ID = pl.DeviceIdType.MESH
HBM_SPEC = pl.BlockSpec(memory_space=pltpu.HBM)
VMEM_SPEC = pl.BlockSpec(memory_space=pltpu.VMEM)


def _tile(n, target, mult=16):
    best = None
    for t in range(mult, min(n, target) + 1, mult):
        if n % t == 0:
            best = t
    return best if best is not None else n


def _cparams(sem):
    return pltpu.CompilerParams(dimension_semantics=sem, vmem_limit_bytes=VMEM_LIMIT)


def _rms_scale(h):
    return lax.rsqrt(jnp.mean(h * h, axis=-1, keepdims=True) + EPS)


def _rms_bwd(dn, h, w):
    r = _rms_scale(h)
    dw = jnp.sum(dn * (h * r), axis=0, keepdims=True)
    z = dn * w
    dh = r * z - h * ((r * r * r) * jnp.mean(z * h, axis=-1, keepdims=True))
    return dh, dw


def _ffn_fwd(h, norm_w, w_in, w_out, exchange=None):
    t, d = h.shape
    f = w_out.shape[0]
    tm = _tile(t, 272)
    tc = _tile(f, 256, 128)
    nj = f // tc
    ni = t // tm
    n_x = len(exchange.ins) if exchange else 0

    def body(*refs):
        h_ref, nw_ref, wi_ref, wo_ref = refs[:4]
        hout_ref, g_ref, u_ref = refs[4 + n_x:7 + n_x]
        a_scr = refs[7 + 2 * n_x]
        i = pl.program_id(0)
        if exchange:
            _host_exchange(exchange, refs[4:4 + n_x], refs[7 + n_x:7 + 2 * n_x], refs[8 + 2 * n_x:],
                           i == 0, i == ni // 3, i == ni - 1, late=i == 2 * ni // 3)
        hh = h_ref[...]
        n = ((hh * _rms_scale(hh)) * nw_ref[...]).astype(BF16)
        for j in range(nj):
            cols = slice(j * tc, (j + 1) * tc)
            g = jnp.dot(n, wi_ref[:, j * tc:(j + 1) * tc], preferred_element_type=F32)
            u = jnp.dot(n, wi_ref[:, f + j * tc:f + (j + 1) * tc], preferred_element_type=F32)
            g_ref[:, cols] = g
            u_ref[:, cols] = u
            a_scr[:, cols] = ((g * jax.nn.sigmoid(g)) * u).astype(BF16)
        hout_ref[...] = hh + 0.5 * jnp.dot(a_scr[...], wo_ref[...], preferred_element_type=F32)

    resident = lambda a: pl.BlockSpec(a.shape, lambda i: (0, 0), pipeline_mode=pl.Buffered(1))
    row = lambda w: pl.BlockSpec((tm, w), lambda i: (i, 0))
    outs = pl.pallas_call(
        body,
        name="ffn_fwd",
        grid=(ni,),
        in_specs=[row(d), pl.BlockSpec((1, d), lambda i: (0, 0)), resident(w_in), resident(w_out)] + [HBM_SPEC] * n_x,
        out_specs=[row(d), row(f), row(f)] + [HBM_SPEC] * n_x,
        out_shape=[
            jax.ShapeDtypeStruct((t, d), F32),
            jax.ShapeDtypeStruct((t, f), F32),
            jax.ShapeDtypeStruct((t, f), F32),
        ] + (exchange.out_shape if exchange else []),
        scratch_shapes=[pltpu.VMEM((tm, f), BF16)] + (exchange.scratch if exchange else []),
        compiler_params=_cparams(("arbitrary",) if exchange else ("parallel",)),
    )(h, norm_w, w_in, w_out, *(exchange.ins if exchange else []))
    return outs[:3], outs[3:]


def _ffn_bwd(dh_out, h, norm_w, g, u, w_in, w_out, exchange=None):
    t, d = h.shape
    f = w_out.shape[0]
    tm = _tile(t, 272)
    tc = _tile(f, 256, 128)
    nj = f // tc
    ni = t // tm
    n_x = len(exchange.ins) if exchange else 0

    def body(*refs):
        dho_ref, h_ref, nw_ref, g_ref, u_ref, wi_ref, wo_ref = refs[:7]
        dhin_ref, n_ref, a_ref, dgu_ref, df_ref, dnw_ref = refs[7 + n_x:13 + n_x]
        i = pl.program_id(0)
        if exchange:
            _host_exchange(exchange, refs[7:7 + n_x], refs[13 + n_x:13 + 2 * n_x], refs[13 + 2 * n_x:],
                           i == 0, i == ni - 1, i == ni - 1)
        hh = h_ref[...]
        nw = nw_ref[...]
        n_ref[...] = ((hh * _rms_scale(hh)) * nw).astype(BF16)
        dho = dho_ref[...]
        df = (0.5 * dho).astype(BF16)
        df_ref[...] = df
        for j in range(nj):
            cols = slice(j * tc, (j + 1) * tc)
            da = lax.dot_general(df, wo_ref[cols, :], NT_DIMS, preferred_element_type=F32)
            gg = g_ref[:, cols]
            uu = u_ref[:, cols]
            sig = jax.nn.sigmoid(gg)
            sl = gg * sig
            a_ref[:, cols] = (sl * uu).astype(BF16)
            dgu_ref[0, :, cols] = ((da * uu) * (sig * (1.0 + gg * (1.0 - sig)))).astype(BF16)
            dgu_ref[1, :, cols] = (da * sl).astype(BF16)
        dn = (lax.dot_general(dgu_ref[0], wi_ref[:, :f], NT_DIMS, preferred_element_type=F32)
              + lax.dot_general(dgu_ref[1], wi_ref[:, f:], NT_DIMS, preferred_element_type=F32))
        dh, dw = _rms_bwd(dn, hh, nw)
        dhin_ref[...] = dho + dh
        dnw_ref[0] = dw

    resident = lambda a: pl.BlockSpec(a.shape, lambda i: (0, 0), pipeline_mode=pl.Buffered(1))
    row = lambda w: pl.BlockSpec((tm, w), lambda i: (i, 0))
    outs = pl.pallas_call(
        body,
        name="ffn_bwd",
        grid=(ni,),
        in_specs=[row(d), row(d), pl.BlockSpec((1, d), lambda i: (0, 0)), row(f), row(f),
                  resident(w_in), resident(w_out)] + [HBM_SPEC] * n_x,
        out_specs=[row(d), row(d), row(f), pl.BlockSpec((2, tm, f), lambda i: (0, i, 0)), row(d),
                   pl.BlockSpec((1, 1, d), lambda i: (i, 0, 0))] + [HBM_SPEC] * n_x,
        out_shape=[
            jax.ShapeDtypeStruct((t, d), F32),
            jax.ShapeDtypeStruct((t, d), BF16),
            jax.ShapeDtypeStruct((t, f), BF16),
            jax.ShapeDtypeStruct((2, t, f), BF16),
            jax.ShapeDtypeStruct((t, d), BF16),
            jax.ShapeDtypeStruct((ni, 1, d), F32),
        ] + (exchange.out_shape if exchange else []),
        scratch_shapes=exchange.scratch if exchange else [],
        compiler_params=_cparams(("arbitrary",) if exchange else ("parallel",)),
    )(dh_out, h, norm_w, g, u, w_in, w_out, *(exchange.ins if exchange else []))
    return outs[:6], outs[6:]


def _tn_matmul(a, b, name, exchange=None):
    t, k = a.shape
    split = b.ndim == 3
    n = 2 * b.shape[2] if split else b.shape[1]
    tk = _tile(k, 512, 128)
    tn = _tile(b.shape[-1], 1408, 128)
    per_half = b.shape[-1] // tn
    ni, nj = k // tk, n // tn
    n_x = len(exchange.ins) if exchange else 0

    def body(*refs):
        a_ref, b_ref, o_ref = refs[0], refs[1], refs[2 + n_x]
        if exchange:
            i, j = pl.program_id(0), pl.program_id(1)
            at_end = (i == ni - 1) & (j == nj - 1)
            _host_exchange(exchange, refs[2:2 + n_x], refs[3 + n_x:3 + 2 * n_x], refs[3 + 2 * n_x:],
                           (i == 0) & (j == 0), at_end, at_end)
        o_ref[...] = lax.dot_general(a_ref[...], b_ref[...], TN_DIMS, preferred_element_type=F32)

    if split:
        b_spec = pl.BlockSpec((None, t, tn), lambda i, j: (j // per_half, 0, j % per_half))
    else:
        b_spec = pl.BlockSpec((t, tn), lambda i, j: (0, j))
    outs = pl.pallas_call(
        body,
        name=name,
        grid=(ni, nj),
        in_specs=[pl.BlockSpec((t, tk), lambda i, j: (0, i)), b_spec] + [HBM_SPEC] * n_x,
        out_specs=[pl.BlockSpec((tk, tn), lambda i, j: (i, j))] + [HBM_SPEC] * n_x,
        out_shape=[jax.ShapeDtypeStruct((k, n), F32)] + (exchange.out_shape if exchange else []),
        scratch_shapes=exchange.scratch if exchange else [],
        compiler_params=_cparams(("arbitrary", "arbitrary") if exchange else ("parallel", "parallel")),
    )(a, b, *(exchange.ins if exchange else []))
    return (outs[0], outs[1:]) if exchange else outs[0]


A_SLOT = lambda h: h // A_GROUP
B_SLOT = lambda h: h % 2

PROJ_PARTS = (
    (OFF_QA, A_WIDTH, A_PAD_WIDTH, True, A_SLOT), (OFF_KA, A_KV_WIDTH, A_KV_WIDTH, True, None),
    (OFF_VA, A_KV_WIDTH, A_KV_WIDTH, True, None), (OFF_QB, B_WIDTH, B_WIDTH, True, None),
    (OFF_KB, B_WIDTH, B_PAD_WIDTH, True, B_SLOT), (OFF_VB, B_WIDTH, B_PAD_WIDTH, True, B_SLOT),
    (OFF_GA, D_MODEL, D_MODEL, False, None), (OFF_GB, D_MODEL, D_MODEL, False, None), (OFF_F, F_COLS, F_COLS, False, None),
)


def _head_tile(pair, head, slot):
    lane_slot = lax.broadcasted_iota(jnp.int32, pair.shape, 1) // HEAD_DIM
    moved = pair if head % 2 == slot else pltpu.roll(pair, HEAD_DIM, 1)
    return jnp.where(lane_slot == slot, moved, 0.0)


def _proj_fwd(h, norm_w, w_p):
    t, d = h.shape
    tm = _tile(t, 272)

    def body(h_ref, nw_ref, w_ref, u_ref, *part_refs):
        hh = h_ref[...]
        un = ((hh * _rms_scale(hh)) * nw_ref[...]).astype(BF16)
        u_ref[...] = un
        for (off, width, _, _, slot), p_ref in zip(PROJ_PARTS, part_refs):
            if slot is None:
                p_ref[...] = jnp.dot(un, w_ref[:, off:off + width], preferred_element_type=F32).astype(p_ref.dtype)
                continue
            part = jnp.dot(un, w_ref[:, off:off + width], preferred_element_type=F32)
            for pair in range(width // LANES):
                x = part[:, pair * LANES:(pair + 1) * LANES]
                for head in (2 * pair, 2 * pair + 1):
                    p_ref[:, head * LANES:(head + 1) * LANES] = _head_tile(x, head, slot(head)).astype(p_ref.dtype)

    row = lambda w: pl.BlockSpec((tm, w), lambda i: (i, 0))
    return pl.pallas_call(
        body,
        name="proj_fwd",
        grid=(t // tm,),
        in_specs=[row(d), pl.BlockSpec((1, d), lambda i: (0, 0)),
                  pl.BlockSpec(w_p.shape, lambda i: (0, 0), pipeline_mode=pl.Buffered(1))],
        out_specs=[row(d)] + [row(width) for _, _, width, _, _ in PROJ_PARTS],
        out_shape=[jax.ShapeDtypeStruct((t, d), BF16)]
        + [jax.ShapeDtypeStruct((t, width), BF16 if is_bf else F32) for _, _, width, is_bf, _ in PROJ_PARTS],
        compiler_params=_cparams(("parallel",)),
    )(h, norm_w, w_p)


def _proj_bwd(dh_out, h, norm_w, dproj, w_p, exchange=None):
    t, d = h.shape
    n = w_p.shape[1]
    tm = _tile(t, 272)
    ni = t // tm
    n_x = len(exchange.ins) if exchange else 0

    def body(*refs):
        dho_ref, h_ref, nw_ref, dp_ref, w_ref = refs[:5]
        dhin_ref, dnw_ref = refs[5 + n_x:7 + n_x]
        if exchange:
            i = pl.program_id(0)
            _host_exchange(exchange, refs[5:5 + n_x], refs[7 + n_x:7 + 2 * n_x], refs[7 + 2 * n_x:],
                           i == 0, i == ni - 1, i == ni - 1)
        dn = lax.dot_general(dp_ref[...], w_ref[...], NT_DIMS, preferred_element_type=F32)
        dh, dw = _rms_bwd(dn, h_ref[...], nw_ref[...])
        dhin_ref[...] = dho_ref[...] + dh
        dnw_ref[0] = dw

    row = lambda w: pl.BlockSpec((tm, w), lambda i: (i, 0))
    outs = pl.pallas_call(
        body,
        name="proj_bwd",
        grid=(ni,),
        in_specs=[row(d), row(d), pl.BlockSpec((1, d), lambda i: (0, 0)), row(n),
                  pl.BlockSpec(w_p.shape, lambda i: (0, 0), pipeline_mode=pl.Buffered(1))] + [HBM_SPEC] * n_x,
        out_specs=[row(d), pl.BlockSpec((1, 1, d), lambda i: (i, 0, 0))] + [HBM_SPEC] * n_x,
        out_shape=[jax.ShapeDtypeStruct((t, d), F32), jax.ShapeDtypeStruct((ni, 1, d), F32)]
        + (exchange.out_shape if exchange else []),
        scratch_shapes=exchange.scratch if exchange else [],
        compiler_params=_cparams(("arbitrary",) if exchange else ("parallel",)),
    )(dh_out, h, norm_w, dproj, w_p, *(exchange.ins if exchange else []))
    return outs[:2], outs[2:]


def _merge_fwd(h, oa, ob, ga, gb, wa, wb, wo):
    t, d = h.shape
    tm = _tile(t, 544)

    def body(h_ref, oa_ref, ob_ref, ga_ref, gb_ref, wa_ref, wb_ref, wo_ref, hout_ref, mix_ref):
        ya = jnp.dot(oa_ref[...], wa_ref[...], preferred_element_type=F32)
        yb = jnp.dot(ob_ref[...], wb_ref[...], preferred_element_type=F32)
        mixed = (jax.nn.sigmoid(ga_ref[...]) * ya + jax.nn.sigmoid(gb_ref[...]) * yb).astype(BF16)
        mix_ref[...] = mixed
        hout_ref[...] = h_ref[...] + jnp.dot(mixed, wo_ref[...], preferred_element_type=F32)

    row = lambda w: pl.BlockSpec((tm, w), lambda i: (i, 0))
    full = lambda a: pl.BlockSpec(a.shape, lambda i: (0, 0))
    return pl.pallas_call(
        body,
        name="merge_fwd",
        grid=(t // tm,),
        in_specs=[row(d), row(oa.shape[1]), row(ob.shape[1]), row(d), row(d), full(wa), full(wb), full(wo)],
        out_specs=[row(d), row(d)],
        out_shape=[jax.ShapeDtypeStruct((t, d), F32), jax.ShapeDtypeStruct((t, d), BF16)],
        compiler_params=_cparams(("parallel",)),
    )(h, oa, ob, ga, gb, wa, wb, wo)


def _merge_bwd(dh, oa, ob, ga, gb, wa, wb, wo, exchange=None):
    t, d = dh.shape
    tm = _tile(t, 544)
    ni = t // tm
    n_x = len(exchange.ins) if exchange else 0

    def body(*refs):
        dh_ref, oa_ref, ob_ref, ga_ref, gb_ref, wa_ref, wb_ref, wo_ref = refs[:8]
        dya_ref, dyb_ref, doa_ref, dob_ref, dga_ref, dgb_ref, dhb_ref = refs[8 + n_x:15 + n_x]
        if exchange:
            i = pl.program_id(0)
            _host_exchange(exchange, refs[8:8 + n_x], refs[15 + n_x:15 + 2 * n_x], refs[15 + 2 * n_x:],
                           i == 0, i == ni - 1, i == ni - 1)
        dhb = dh_ref[...].astype(BF16)
        dhb_ref[...] = dhb
        dmix = lax.dot_general(dhb, wo_ref[...], NT_DIMS, preferred_element_type=F32)
        for o_ref, g_ref, w_ref, dy_ref, do_ref, dg_ref in (
                (oa_ref, ga_ref, wa_ref, dya_ref, doa_ref, dga_ref),
                (ob_ref, gb_ref, wb_ref, dyb_ref, dob_ref, dgb_ref)):
            y = jnp.dot(o_ref[...], w_ref[...], preferred_element_type=F32)
            s = jax.nn.sigmoid(g_ref[...])
            dy = (dmix * s).astype(BF16)
            dy_ref[...] = dy
            dg_ref[...] = ((dmix * y) * (s * (1.0 - s))).astype(BF16)
            do_ref[...] = lax.dot_general(dy, w_ref[...], NT_DIMS, preferred_element_type=F32).astype(BF16)

    row = lambda w: pl.BlockSpec((tm, w), lambda i: (i, 0))
    full = lambda a: pl.BlockSpec(a.shape, lambda i: (0, 0))
    wa_w, wb_w = oa.shape[1], ob.shape[1]
    outs = pl.pallas_call(
        body,
        name="merge_bwd",
        grid=(ni,),
        in_specs=[row(d), row(wa_w), row(wb_w), row(d), row(d), full(wa), full(wb), full(wo)] + [HBM_SPEC] * n_x,
        out_specs=[row(d), row(d), row(wa_w), row(wb_w), row(d), row(d), row(d)] + [HBM_SPEC] * n_x,
        out_shape=[
            jax.ShapeDtypeStruct((t, d), BF16), jax.ShapeDtypeStruct((t, d), BF16),
            jax.ShapeDtypeStruct((t, wa_w), BF16), jax.ShapeDtypeStruct((t, wb_w), BF16),
            jax.ShapeDtypeStruct((t, d), BF16), jax.ShapeDtypeStruct((t, d), BF16),
            jax.ShapeDtypeStruct((t, d), BF16),
        ] + (exchange.out_shape if exchange else []),
        scratch_shapes=exchange.scratch if exchange else [],
        compiler_params=_cparams(("arbitrary",) if exchange else ("parallel",)),
    )(dh, oa, ob, ga, gb, wa, wb, wo, *(exchange.ins if exchange else []))
    return outs[:7], outs[7:]


def _tri_dot(tri, x):
    hi = x.astype(BF16)
    r1 = x - hi.astype(F32)
    mid = r1.astype(BF16)
    lo = (r1 - mid.astype(F32)).astype(BF16)
    return (jnp.dot(tri, hi, preferred_element_type=F32)
            + jnp.dot(tri, mid, preferred_element_type=F32)
            + jnp.dot(tri, lo, preferred_element_type=F32))


def _forget_cumsum(f_logit, b_pad, nb):
    t, w = f_logit.shape
    bsz = t // (nb * BLOCK)

    def body(f_ref, b_ref, c_ref, carry):
        n = pl.program_id(1)

        @pl.when(n == 0)
        def _():
            carry[...] = jnp.zeros_like(carry)

        x = jax.nn.log_sigmoid(f_ref[...] + b_ref[...])
        rows = lax.broadcasted_iota(jnp.int32, (BLOCK, BLOCK), 0)
        cols = lax.broadcasted_iota(jnp.int32, (BLOCK, BLOCK), 1)
        tri = (cols <= rows).astype(BF16)
        c = _tri_dot(tri, x) + carry[...]
        c_ref[...] = c
        carry[...] = c[BLOCK - 1:BLOCK, :]

    return pl.pallas_call(
        body,
        name="forget_cumsum",
        grid=(bsz, nb),
        in_specs=[pl.BlockSpec((BLOCK, w), lambda b, n: (b * nb + n, 0)),
                  pl.BlockSpec((1, w), lambda b, n: (0, 0))],
        out_specs=pl.BlockSpec((BLOCK, w), lambda b, n: (b * nb + n, 0)),
        out_shape=jax.ShapeDtypeStruct((t, w), F32),
        scratch_shapes=[pltpu.VMEM((1, w), F32)],
        compiler_params=_cparams(("parallel", "arbitrary")),
    )(f_logit, b_pad)


def _forget_cumsum_bwd(dc, f_logit, b_pad, nb):
    t, w = f_logit.shape
    bsz = t // (nb * BLOCK)

    def body(dc_ref, f_ref, b_ref, df_ref, db_ref, carry):
        n = pl.program_id(1)

        @pl.when(n == 0)
        def _():
            carry[...] = jnp.zeros_like(carry)
            db_ref[...] = jnp.zeros_like(db_ref)

        rows = lax.broadcasted_iota(jnp.int32, (BLOCK, BLOCK), 0)
        cols = lax.broadcasted_iota(jnp.int32, (BLOCK, BLOCK), 1)
        tri = (cols >= rows).astype(BF16)
        dlf = _tri_dot(tri, dc_ref[...]) + carry[...]
        carry[...] = dlf[0:1, :]
        df = dlf * jax.nn.sigmoid(-(f_ref[...] + b_ref[...]))
        df_ref[...] = df.astype(BF16)
        db_ref[0] += jnp.sum(df, axis=0, keepdims=True)

    rev = lambda b, n: (b * nb + (nb - 1 - n), 0)
    return pl.pallas_call(
        body,
        name="forget_cumsum_bwd",
        grid=(bsz, nb),
        in_specs=[pl.BlockSpec((BLOCK, w), rev),
                  pl.BlockSpec((BLOCK, w), rev),
                  pl.BlockSpec((1, w), lambda b, n: (0, 0))],
        out_specs=[pl.BlockSpec((BLOCK, w), rev),
                   pl.BlockSpec((1, 1, w), lambda b, n: (b, 0, 0))],
        out_shape=[jax.ShapeDtypeStruct((t, w), BF16), jax.ShapeDtypeStruct((bsz, 1, w), F32)],
        scratch_shapes=[pltpu.VMEM((1, w), F32)],
        compiler_params=_cparams(("parallel", "arbitrary")),
    )(dc, f_logit, b_pad)


GROUP_ROWS = A_GROUP * BLOCK


def _stack_heads(ref, g):
    return jnp.concatenate([ref[:, (A_GROUP * g + i) * LANES:(A_GROUP * g + i + 1) * LANES] for i in range(A_GROUP)],
                           axis=0)


def _unstack_heads(ref, g, x):
    for i in range(A_GROUP):
        ref[:, (A_GROUP * g + i) * LANES:(A_GROUP * g + i + 1) * LANES] = x[i * BLOCK:(i + 1) * BLOCK].astype(ref.dtype)


def _swa_logits(qk, slope, n):
    qi = lax.broadcasted_iota(jnp.int32, (GROUP_ROWS, BLOCK), 0) & (BLOCK - 1)
    kj = lax.broadcasted_iota(jnp.int32, (GROUP_ROWS, BLOCK), 1)
    s_all = qk * SCALE
    out = []
    for i, (dist, ok) in enumerate((
            (n * BLOCK + qi - kj, (kj >= N_PAD) & (n * BLOCK + qi - kj >= 0)),
            (BLOCK + qi - kj, (kj > qi) & (n >= 2)),
            (qi - kj, (kj <= qi) & (n >= 1)))):
        s = s_all[:, i * BLOCK:(i + 1) * BLOCK] - slope * dist.astype(F32)
        out.append(jnp.where(ok, s, NEG))
    return out


def _three_blocks(m_ref, p_ref, c_ref):
    return jnp.concatenate([m_ref[...], p_ref[...], c_ref[...]], axis=0)


def _swa_specs(bsz, nb):
    qspec = pl.BlockSpec((bsz, BLOCK, A_PAD_WIDTH), lambda n: (0, n, 0))
    kv_m = pl.BlockSpec((bsz, BLOCK, LANES), lambda n: (0, 0, 0))
    kv_p = pl.BlockSpec((bsz, BLOCK, LANES), lambda n: (0, jnp.maximum(n - 1, 0), 0))
    kv_c = pl.BlockSpec((bsz, BLOCK, LANES), lambda n: (0, n, 0))
    rowspec = pl.BlockSpec((A_KV_HEADS, GROUP_ROWS, 1), lambda n: (0, 0, 0))
    lsespec = pl.BlockSpec((bsz, 1, A_KV_HEADS, GROUP_ROWS, 1), lambda n: (0, n, 0, 0, 0))
    return qspec, kv_m, kv_p, kv_c, rowspec, lsespec


def _swa_fwd(q, k, v, sink_rows, slope_rows, nb):
    t = q.shape[0]
    l = nb * BLOCK
    bsz = t // l

    def body(q_ref, km_ref, kp_ref, kc_ref, vm_ref, vp_ref, vc_ref, sink_ref, slope_ref, o_ref, lse_ref):
        n = pl.program_id(0)
        lane_group = lax.broadcasted_iota(jnp.int32, (GROUP_ROWS, LANES), 1) // HEAD_DIM
        products = {}
        for b in range(bsz):
            keys = _three_blocks(km_ref.at[b], kp_ref.at[b], kc_ref.at[b])
            for g in range(A_KV_HEADS):
                products[b, g] = lax.dot_general(_stack_heads(q_ref.at[b], g), keys, NT_DIMS,
                                                 preferred_element_type=F32)
        for b in range(bsz):
            values = _three_blocks(vm_ref.at[b], vp_ref.at[b], vc_ref.at[b])
            for g in range(A_KV_HEADS):
                sink = sink_ref[g]
                s_m, s_p, s_c = _swa_logits(products[b, g], slope_ref[g], n)
                m = jnp.maximum(jnp.max(jnp.maximum(jnp.maximum(s_m, s_p), s_c), axis=-1, keepdims=True), sink)
                m_wide = jnp.broadcast_to(m, (GROUP_ROWS, BLOCK))
                e_m = jnp.exp(s_m - m_wide)
                e_p = jnp.exp(s_p - m_wide)
                e_c = jnp.exp(s_c - m_wide)
                z = jnp.sum((e_m + e_p) + e_c, axis=-1, keepdims=True) + jnp.exp(sink - m)
                inv = jnp.broadcast_to(1.0 / z, (GROUP_ROWS, BLOCK))
                probs = jnp.concatenate([(e_m * inv).astype(BF16), (e_p * inv).astype(BF16),
                                         (e_c * inv).astype(BF16)], axis=1)
                o = jnp.dot(probs, values, preferred_element_type=F32)
                _unstack_heads(o_ref.at[b], g, jnp.where(lane_group == g, o, 0.0))
                lse_ref[b, 0, g] = m + jnp.log(z)

    qspec, kv_m, kv_p, kv_c, rowspec, lsespec = _swa_specs(bsz, nb)
    by_example = lambda a: a.reshape(bsz, l, a.shape[1])
    q3, k3, v3 = by_example(q), by_example(k), by_example(v)
    o, lse = pl.pallas_call(
        body,
        name="swa_fwd",
        grid=(nb,),
        in_specs=[qspec, kv_m, kv_p, kv_c, kv_m, kv_p, kv_c, rowspec, rowspec],
        out_specs=[qspec, lsespec],
        out_shape=[jax.ShapeDtypeStruct((bsz, l, A_PAD_WIDTH), BF16),
                   jax.ShapeDtypeStruct((bsz, nb, A_KV_HEADS, GROUP_ROWS, 1), F32)],
        compiler_params=_cparams(("arbitrary",)),
    )(q3, k3, k3, k3, v3, v3, v3, sink_rows, slope_rows)
    return o.reshape(t, A_PAD_WIDTH), lse


def _swa_bwd(q, k, v, do, lse, sink_rows, slope_rows, nb):
    t = q.shape[0]
    l = nb * BLOCK
    bsz = t // l

    def body(q_ref, km_ref, kp_ref, kc_ref, vm_ref, vp_ref, vc_ref, do_ref, lse_ref, sink_ref, slope_ref,
             dq_ref, dk_ref, dv_ref, dsink_ref, dk_acc, dv_acc):
        n = pl.program_id(0)

        @pl.when(n == 0)
        def _():
            dk_acc[...] = jnp.zeros_like(dk_acc)
            dv_acc[...] = jnp.zeros_like(dv_acc)
            dsink_ref[...] = jnp.zeros_like(dsink_ref)

        first_half = lax.broadcasted_iota(jnp.int32, (BLOCK, LANES), 1) < HEAD_DIM
        prev = jnp.maximum(n - 1, 0)
        products = {}
        for b in range(bsz):
            keys = _three_blocks(km_ref.at[b], kp_ref.at[b], kc_ref.at[b])
            values = _three_blocks(vm_ref.at[b], vp_ref.at[b], vc_ref.at[b])
            for g in range(A_KV_HEADS):
                products[b, g] = (
                    lax.dot_general(_stack_heads(q_ref.at[b], g), keys, NT_DIMS, preferred_element_type=F32),
                    lax.dot_general(_stack_heads(do_ref.at[b], g), values, NT_DIMS, preferred_element_type=F32))
        for b in range(bsz):
            keys = _three_blocks(km_ref.at[b], kp_ref.at[b], kc_ref.at[b])
            for g in range(A_KV_HEADS):
                qq = _stack_heads(q_ref.at[b], g)
                dob = _stack_heads(do_ref.at[b], g)
                lse_g = lse_ref[b, 0, g]
                lse_wide = jnp.broadcast_to(lse_g, (GROUP_ROWS, BLOCK))
                qk, dp_all = products[b, g]
                probs = [jnp.exp(s - lse_wide) for s in _swa_logits(qk, slope_ref[g], n)]
                dps = [dp_all[:, i * BLOCK:(i + 1) * BLOCK] for i in range(3)]
                delta = jnp.sum((probs[0] * dps[0] + probs[1] * dps[1]) + probs[2] * dps[2], axis=-1, keepdims=True)
                delta_wide = jnp.broadcast_to(delta, (GROUP_ROWS, BLOCK))
                ds = jnp.concatenate([(p * (dp - delta_wide)).astype(BF16) for p, dp in zip(probs, dps)], axis=1)
                pb = jnp.concatenate([p.astype(BF16) for p in probs], axis=1)
                dq = jnp.dot(ds, keys, preferred_element_type=F32) * SCALE
                dk_all = lax.dot_general(ds, qq, TN_DIMS, preferred_element_type=F32) * SCALE
                dv_all = lax.dot_general(pb, dob, TN_DIMS, preferred_element_type=F32)
                for i, start in enumerate((0, prev * BLOCK, n * BLOCK)):
                    rows = pl.ds(pl.multiple_of(start, BLOCK), BLOCK)
                    dk_acc[b, rows, :] += dk_all[i * BLOCK:(i + 1) * BLOCK]
                    dv_acc[b, rows, :] += dv_all[i * BLOCK:(i + 1) * BLOCK]
                for pair in range(A_GROUP // 2):
                    even = dq[2 * pair * BLOCK:(2 * pair + 1) * BLOCK]
                    odd = dq[(2 * pair + 1) * BLOCK:(2 * pair + 2) * BLOCK]
                    left = even if g == 0 else pltpu.roll(even, HEAD_DIM, 1)
                    right = pltpu.roll(odd, HEAD_DIM, 1) if g == 0 else odd
                    tile = (A_GROUP // 2) * g + pair
                    dq_ref[b, :, tile * LANES:(tile + 1) * LANES] = jnp.where(first_half, left, right).astype(BF16)
                dsink_ref[b, g] += -(jnp.exp(sink_ref[g] - lse_g) * delta)

        @pl.when(n == nb - 1)
        def _():
            dk_ref[...] = dk_acc[...].astype(BF16)
            dv_ref[...] = dv_acc[...].astype(BF16)

    qspec, kv_m, kv_p, kv_c, rowspec, lsespec = _swa_specs(bsz, nb)
    kv_all = pl.BlockSpec((bsz, l, LANES), lambda n: (0, 0, 0))
    by_example = lambda a: a.reshape(bsz, l, a.shape[1])
    q3, k3, v3 = by_example(q), by_example(k), by_example(v)
    dq, dk, dv, dsink = pl.pallas_call(
        body,
        name="swa_bwd",
        grid=(nb,),
        in_specs=[qspec, kv_m, kv_p, kv_c, kv_m, kv_p, kv_c, qspec, lsespec, rowspec, rowspec],
        out_specs=[pl.BlockSpec((bsz, BLOCK, A_WIDTH), lambda n: (0, n, 0)), kv_all, kv_all,
                   pl.BlockSpec((bsz, A_KV_HEADS, GROUP_ROWS, 1), lambda n: (0, 0, 0, 0))],
        out_shape=[jax.ShapeDtypeStruct((bsz, l, A_WIDTH), BF16),
                   jax.ShapeDtypeStruct((bsz, l, LANES), BF16),
                   jax.ShapeDtypeStruct((bsz, l, LANES), BF16),
                   jax.ShapeDtypeStruct((bsz, A_KV_HEADS, GROUP_ROWS, 1), F32)],
        scratch_shapes=[pltpu.VMEM((bsz, l, LANES), F32), pltpu.VMEM((bsz, l, LANES), F32)],
        compiler_params=_cparams(("arbitrary",)),
    )(q3, k3, k3, k3, v3, v3, v3, by_example(do), lse, sink_rows, slope_rows)
    return dq.reshape(t, A_WIDTH), dk.reshape(t, LANES), dv.reshape(t, LANES), dsink


CHUNK = KEY_BLOCKS * BLOCK


def _fox_chunk(qb, ci):
    sb = jnp.maximum(jnp.minimum(KEY_BLOCKS * ci, qb + 1 - KEY_BLOCKS), 0)
    lo = jnp.maximum(ci * CHUNK, N_PAD)
    return sb, lo, pl.ds(pl.multiple_of(sb * BLOCK, BLOCK), CHUNK)


def _fox_logits(s_ref, cr_ref, e, j, sb, lo, qb):
    lane = lax.broadcasted_iota(jnp.int32, (BLOCK, BLOCK), 1)
    ahead = lane - lax.broadcasted_iota(jnp.int32, (BLOCK, BLOCK), 0)
    first = (sb + j) * BLOCK
    s = s_ref[e, :, j * BLOCK:(j + 1) * BLOCK] - cr_ref[e, sb + j]
    return jnp.where((ahead <= qb * BLOCK - first) & (lane >= lo - first), s, NEG)


FOX_PAIRS = 4
FOX_HEADS = 2 * FOX_PAIRS
FOX_STEPS = B_PAIRS // FOX_PAIRS


def _fox_specs(nb):
    l = nb * BLOCK
    q_spec = pl.BlockSpec((BLOCK, FOX_PAIRS * LANES), lambda b, p, i: (b * nb + i, p))
    kv_spec = pl.BlockSpec((l, FOX_HEADS * LANES), lambda b, p, i: (b, p))
    cc_spec = pl.BlockSpec((FOX_HEADS, BLOCK, 1), lambda b, p, i: (b * FOX_STEPS + p, i, 0))
    cr_spec = pl.BlockSpec((FOX_HEADS, nb, 1, BLOCK), lambda b, p, i: (b * FOX_STEPS + p, 0, 0, 0))
    return q_spec, kv_spec, cc_spec, cr_spec


def _fox_fwd(q, k, v, c_row, nb, exchange=None):
    t = q.shape[0]
    bsz = t // (nb * BLOCK)
    assert nb >= KEY_BLOCKS

    n_x = len(exchange.ins) if exchange else 0

    def body(*refs):
        q_ref, k_ref, v_ref, cr_ref = refs[:4]
        o_ref, ox_ref, lse_ref = refs[4 + n_x:7 + n_x]
        s_scr, hi_scr, lo_scr = refs[7 + 2 * n_x:10 + 2 * n_x]
        qb = pl.program_id(2)
        if exchange:
            first = (pl.program_id(0) == 0) & (pl.program_id(1) == 0)
            last = (pl.program_id(0) == bsz - 1) & (pl.program_id(1) == FOX_STEPS - 1)
            _host_exchange(exchange, refs[4:4 + n_x], refs[7 + n_x:7 + 2 * n_x], refs[10 + 2 * n_x:],
                           first & (qb == 0), first & (qb == 2 * nb // 3), last & (qb == nb - 1),
                           late=last & (qb == 0))
        qs = [q_ref[:, a * LANES:(a + 1) * LANES] * SCALE for a in range(FOX_PAIRS)]
        first_half = lax.broadcasted_iota(jnp.int32, (BLOCK, LANES), 1) < HEAD_DIM

        def step(ci, carry):
            stats, accs = carry[:2 * FOX_HEADS], carry[2 * FOX_HEADS:]
            sb, lo, krows = _fox_chunk(qb, ci)
            for e in range(FOX_HEADS):
                s_scr[e] = lax.dot_general(qs[e // 2], k_ref[krows, e * LANES:(e + 1) * LANES], NT_DIMS,
                                           preferred_element_type=F32)
            new_stats, new_accs = [], []
            for a in range(FOX_PAIRS):
                alphas = []
                pv = jnp.zeros((BLOCK, LANES), F32)
                pv_lo = jnp.zeros((BLOCK, LANES), F32)
                for e in (2 * a, 2 * a + 1):
                    m, z = stats[2 * e], stats[2 * e + 1]
                    tile = slice(e * LANES, (e + 1) * LANES)
                    top = None
                    for j in range(KEY_BLOCKS):
                        s = _fox_logits(s_scr, cr_ref, e, j, sb, lo, qb)
                        s_scr[e, :, j * BLOCK:(j + 1) * BLOCK] = s
                        top = s if top is None else jnp.maximum(top, s)
                    m_new = jnp.maximum(m, jnp.max(top, axis=-1, keepdims=True))
                    alpha = jnp.exp(m - m_new)
                    m_wide = jnp.broadcast_to(m_new, (BLOCK, BLOCK))
                    total = None
                    for j in range(KEY_BLOCKS):
                        cols = slice(j * BLOCK, (j + 1) * BLOCK)
                        p = jnp.exp(s_scr[e, :, cols] - m_wide)
                        total = p if total is None else total + p
                        hi = p.astype(BF16)
                        hi_scr[e, :, cols] = hi
                        lo_scr[e, :, cols] = (p - hi.astype(F32)).astype(BF16)
                    z = alpha * z + jnp.sum(total, axis=-1, keepdims=True)
                    vv = v_ref[krows, tile]
                    pv = pv + jnp.dot(hi_scr[e], vv, preferred_element_type=F32)
                    pv_lo = pv_lo + jnp.dot(lo_scr[e], vv, preferred_element_type=F32)
                    new_stats += [m_new, z]
                    alphas.append(alpha)
                alpha = jnp.where(first_half, alphas[0], alphas[1])
                new_accs += [alpha * accs[2 * a] + pv, alpha * accs[2 * a + 1] + pv_lo]
            return (*new_stats, *new_accs)

        col = lambda val: jnp.full((BLOCK, 1), val, F32)
        done = lax.fori_loop(
            0, (qb + KEY_BLOCKS) // KEY_BLOCKS, step,
            (col(NEG), col(0.0)) * FOX_HEADS + (jnp.zeros((BLOCK, LANES), F32),) * (2 * FOX_PAIRS))
        for a in range(FOX_PAIRS):
            m0, z0, m1, z1 = done[4 * a:4 * a + 4]
            acc, acc_lo = done[2 * FOX_HEADS + 2 * a:2 * FOX_HEADS + 2 * a + 2]
            inv = 1.0 / jnp.where(first_half, z0, z1)
            tile = slice(a * LANES, (a + 1) * LANES)
            o_ref[:, tile] = (acc * inv).astype(BF16)
            ox_ref[:, tile] = (acc + acc_lo) * inv
            lse_ref[2 * a] = m0 + jnp.log(z0)
            lse_ref[2 * a + 1] = m1 + jnp.log(z1)

    q_spec, kv_spec, cc_spec, cr_spec = _fox_specs(nb)
    outs = pl.pallas_call(
        body,
        name="fox_fwd",
        grid=(bsz, FOX_STEPS, nb),
        in_specs=[q_spec, kv_spec, kv_spec, cr_spec] + [HBM_SPEC] * n_x,
        out_specs=[q_spec, q_spec, cc_spec] + [HBM_SPEC] * n_x,
        out_shape=[jax.ShapeDtypeStruct((t, B_WIDTH), BF16), jax.ShapeDtypeStruct((t, B_WIDTH), F32),
                   jax.ShapeDtypeStruct((bsz * B_HEADS, nb * BLOCK, 1), F32)] + (exchange.out_shape if exchange else []),
        scratch_shapes=[pltpu.VMEM((FOX_HEADS, BLOCK, CHUNK), F32), pltpu.VMEM((FOX_HEADS, BLOCK, CHUNK), BF16),
                        pltpu.VMEM((FOX_HEADS, BLOCK, CHUNK), BF16)] + (exchange.scratch if exchange else []),
        compiler_params=_cparams(("arbitrary",) * 3 if exchange else ("parallel", "parallel", "arbitrary")),
    )(q, k, v, c_row, *(exchange.ins if exchange else []))
    return outs[:3], outs[3:]


def _fox_bwd(q, k, v, o_exact, do, lse, c_row, nb, exchange=None):
    t = q.shape[0]
    l = nb * BLOCK
    bsz = t // l

    n_x = len(exchange.ins) if exchange else 0

    def body(*refs):
        q_ref, k_ref, v_ref, ox_ref, do_ref, lse_ref, cr_ref = refs[:7]
        dq_ref, dk_ref, dv_ref, dc_ref = refs[7 + n_x:11 + n_x]
        dk_acc, dv_acc, s_scr, dp_scr, p_scr, ds_scr, dq_scr = refs[11 + 2 * n_x:18 + 2 * n_x]
        qb = pl.program_id(2)
        if exchange:
            first = (pl.program_id(0) == 0) & (pl.program_id(1) == 0)
            last = (pl.program_id(0) == bsz - 1) & (pl.program_id(1) == FOX_STEPS - 1)
            _host_exchange(exchange, refs[7:7 + n_x], refs[11 + n_x:11 + 2 * n_x], refs[18 + 2 * n_x:],
                           first & (qb == 0), last & (qb == 0), last & (qb == nb - 1))

        @pl.when(qb == 0)
        def _():
            dk_acc[...] = jnp.zeros_like(dk_acc)
            dv_acc[...] = jnp.zeros_like(dv_acc)
            dc_ref[...] = jnp.zeros_like(dc_ref)

        top_half = lax.broadcasted_iota(jnp.int32, (LANES, BLOCK), 0) < HEAD_DIM
        pair_t = lambda x: jnp.concatenate([jnp.where(top_half, x.T, 0), jnp.where(top_half, 0, x.T)], axis=1)
        first_half = lax.broadcasted_iota(jnp.int32, (BLOCK, LANES), 1) < HEAD_DIM
        wide = lambda col: jnp.broadcast_to(col, (BLOCK, BLOCK))
        qs, dobs, qs_t, dob_t, deltas = [], [], [], [], []
        for a in range(FOX_PAIRS):
            tile = slice(a * LANES, (a + 1) * LANES)
            qs.append(q_ref[:, tile] * SCALE)
            dobs.append(do_ref[:, tile])
            qs_t.append(pair_t(qs[a]))
            dob_t.append(pair_t(dobs[a]))
            weighted = dobs[a].astype(F32) * ox_ref[:, tile]
            deltas += [wide(jnp.sum(jnp.where(first_half, weighted, 0.0), axis=-1, keepdims=True)),
                       wide(jnp.sum(jnp.where(first_half, 0.0, weighted), axis=-1, keepdims=True))]
        lses = [wide(lse_ref[e]) for e in range(FOX_HEADS)]

        dq_scr[...] = jnp.zeros(dq_scr.shape, F32)

        def step(ci, carry):
            sb, lo, krows = _fox_chunk(qb, ci)
            for e in range(FOX_HEADS):
                tile = slice(e * LANES, (e + 1) * LANES)
                s_scr[e] = lax.dot_general(qs[e // 2], k_ref[krows, tile], NT_DIMS, preferred_element_type=F32)
                dp_scr[e] = lax.dot_general(dobs[e // 2], v_ref[krows, tile], NT_DIMS, preferred_element_type=F32)
            for a in range(FOX_PAIRS):
                for e in (2 * a, 2 * a + 1):
                    tile = slice(e * LANES, (e + 1) * LANES)
                    kk = k_ref[krows, tile]
                    for j in range(KEY_BLOCKS):
                        cols = slice(j * BLOCK, (j + 1) * BLOCK)
                        p = jnp.exp(_fox_logits(s_scr, cr_ref, e, j, sb, lo, qb) - lses[e])
                        ds = p * (dp_scr[e, :, cols] - deltas[e])
                        dc_ref[e, sb + j] -= jnp.sum(ds, axis=0, keepdims=True)
                        p_scr[e, :, cols] = p.astype(BF16)
                        ds_scr[e, :, cols] = ds.astype(BF16)
                    dq_scr[a] += jnp.dot(ds_scr[e], kk, preferred_element_type=F32)
                both = slice(2 * a, 2 * a + 2)
                dk_t = jnp.dot(qs_t[a], ds_scr[both].reshape(2 * BLOCK, CHUNK), preferred_element_type=F32)
                dv_t = jnp.dot(dob_t[a], p_scr[both].reshape(2 * BLOCK, CHUNK), preferred_element_type=F32)
                for j in range(KEY_BLOCKS):
                    cols = slice(j * BLOCK, (j + 1) * BLOCK)
                    dk_acc[a * nb + sb + j] += dk_t[:, cols]
                    dv_acc[a * nb + sb + j] += dv_t[:, cols]
            return carry

        lax.fori_loop(0, (qb + KEY_BLOCKS) // KEY_BLOCKS, step, 0)
        for a in range(FOX_PAIRS):
            dq_ref[:, a * LANES:(a + 1) * LANES] = (dq_scr[a] * SCALE).astype(BF16)

        @pl.when(qb == nb - 1)
        def _():
            for a in range(FOX_PAIRS):
                for kb in range(nb):
                    rows = slice(kb * BLOCK, (kb + 1) * BLOCK)
                    for acc, out_ref in ((dk_acc, dk_ref), (dv_acc, dv_ref)):
                        out_ref[rows, a * LANES:(a + 1) * LANES] = acc[a * nb + kb].T.astype(BF16)

    q_spec, kv_spec, cc_spec, cr_spec = _fox_specs(nb)
    dkv_spec = pl.BlockSpec((l, FOX_PAIRS * LANES), lambda b, p, i: (b, p))
    outs = pl.pallas_call(
        body,
        name="fox_bwd",
        grid=(bsz, FOX_STEPS, nb),
        in_specs=[q_spec, kv_spec, kv_spec, q_spec, q_spec, cc_spec, cr_spec] + [HBM_SPEC] * n_x,
        out_specs=[q_spec, dkv_spec, dkv_spec, cr_spec] + [HBM_SPEC] * n_x,
        out_shape=[jax.ShapeDtypeStruct((t, B_WIDTH), BF16), jax.ShapeDtypeStruct((t, B_WIDTH), BF16),
                   jax.ShapeDtypeStruct((t, B_WIDTH), BF16),
                   jax.ShapeDtypeStruct((bsz * B_HEADS, nb, 1, BLOCK), F32)] + (exchange.out_shape if exchange else []),
        scratch_shapes=[pltpu.VMEM((FOX_PAIRS * nb, LANES, BLOCK), F32), pltpu.VMEM((FOX_PAIRS * nb, LANES, BLOCK), F32),
                        pltpu.VMEM((FOX_HEADS, BLOCK, CHUNK), F32), pltpu.VMEM((FOX_HEADS, BLOCK, CHUNK), F32),
                        pltpu.VMEM((FOX_HEADS, BLOCK, CHUNK), BF16), pltpu.VMEM((FOX_HEADS, BLOCK, CHUNK), BF16),
                        pltpu.VMEM((FOX_PAIRS, BLOCK, LANES), F32)]
        + (exchange.scratch if exchange else []),
        compiler_params=_cparams(("arbitrary",) * 3 if exchange else ("parallel", "parallel", "arbitrary")),
    )(q, k, v, o_exact, do, lse, c_row, *(exchange.ins if exchange else []))
    return outs[:4], outs[4:]


def _loss_head(h, final_w, target):
    bsz, l, d = h.shape
    nb = l // BLOCK

    def body(h_ref, w_ref, t_ref, loss_ref, dh_ref, dw_ref):
        n = pl.program_id(0)

        @pl.when(n == 0)
        def _():
            loss_ref[...] = jnp.zeros_like(loss_ref)
            dw_ref[...] = jnp.zeros_like(dw_ref)
            dh_ref[...] = jnp.zeros_like(dh_ref)

        @pl.when(n > 0)
        def _():
            w = w_ref[...]
            for b in range(bsz):
                hh = h_ref[b]
                r = _rms_scale(hh)
                err = (hh * r) * w - t_ref[b]
                loss_ref[...] += 0.5 * jnp.sum(jnp.mean(err * err, axis=-1, keepdims=True), axis=0, keepdims=True)
                dy = err * (1.0 / d)
                dh, dw = _rms_bwd(dy, hh, w)
                dh_ref[b] = dh
                dw_ref[...] += dw

    return pl.pallas_call(
        body,
        name="loss_head",
        grid=(nb,),
        in_specs=[
            pl.BlockSpec((bsz, BLOCK, d), lambda n: (0, n, 0)),
            pl.BlockSpec((1, d), lambda n: (0, 0)),
            pl.BlockSpec((bsz, BLOCK, d), lambda n: (0, jnp.maximum(n - 1, 0), 0)),
        ],
        out_specs=[
            pl.BlockSpec((1, 128), lambda n: (0, 0)),
            pl.BlockSpec((bsz, BLOCK, d), lambda n: (0, n, 0)),
            pl.BlockSpec((1, d), lambda n: (0, 0)),
        ],
        out_shape=[jax.ShapeDtypeStruct((1, 128), F32), jax.ShapeDtypeStruct((bsz, l, d), F32),
                   jax.ShapeDtypeStruct((1, d), F32)],
        compiler_params=_cparams(("arbitrary",)),
    )(h, final_w, target)


def _pad_tiles(w, src, heads, lane_slot, axis):
    pieces = []
    for h in range(heads):
        x = lax.slice_in_dim(w, src + HEAD_DIM * h, src + HEAD_DIM * (h + 1), axis=axis)
        z = jnp.zeros_like(x)
        pieces += [x, z] if lane_slot(h) == 0 else [z, x]
    return pieces


def _unpad_tiles(g, off, heads, lane_slot, axis):
    return [lax.slice_in_dim(g, off + LANES * h + HEAD_DIM * lane_slot(h),
                             off + LANES * h + HEAD_DIM * (lane_slot(h) + 1), axis=axis) for h in range(heads)]


def _layout_w_in(w):
    pad_f = jnp.zeros((w.shape[0], F_COLS - B_HEADS), w.dtype)
    return jnp.concatenate([w[:, :SRC_F], w[:, SRC_GA:], w[:, SRC_F:SRC_GA], pad_f], axis=1)


def _unlayout_w_in(g):
    return jnp.concatenate([g[:, :OFF_GA], g[:, OFF_F:OFF_F + B_HEADS], g[:, OFF_GA:OFF_F]], axis=1)


def _local_step(x, target, meta, norms, b_forget, sinks, w, comm=None):
    n1, nmix, n2, nfin = norms
    w1i, w1o = w[:2]
    bsz, seq, d = x.shape
    l = PREFIX + seq
    nb = l // BLOCK
    t = bsz * l

    h0 = jnp.concatenate([jnp.zeros((bsz, N_PAD, d), F32),
                          jnp.broadcast_to(meta[None], (bsz, N_META, d)), x], axis=1).reshape(t, d)

    if comm is None:
        (h1, g1, u1), _ = _ffn_fwd(h0, n1, w1i, w1o)
        w_in, wa, wb, wo, w2i, w2o = w[2:]
    else:
        (h1, g1, u1), gathered = _ffn_fwd(h0, n1, w1i, w1o, comm.gather(GATHER_PROJ))
        w_in, = comm.gathered(GATHER_PROJ, gathered)
    wp = _layout_w_in(w_in)
    un, qa, ka, va, qb, kb, vb, ga, gb, f_logit = _proj_fwd(h1, nmix, wp)
    b_pad = jnp.concatenate([b_forget, jnp.zeros((1, F_COLS - B_HEADS), F32)], axis=1)
    c = _forget_cumsum(f_logit, b_pad, nb)
    c_heads = c[:, :B_HEADS].reshape(bsz, l, B_HEADS).transpose(0, 2, 1).reshape(bsz * B_HEADS, l)
    c_row = c_heads.reshape(bsz * B_HEADS, nb, 1, BLOCK)

    slopes = jnp.exp2(-8.0 * jnp.arange(1, A_HEADS + 1, dtype=F32) / A_HEADS)
    slope_rows = jnp.repeat(slopes.reshape(A_KV_HEADS, A_GROUP), BLOCK, axis=1)[:, :, None]
    sink_rows = jnp.repeat(sinks.reshape(A_KV_HEADS, A_GROUP), BLOCK, axis=1)[:, :, None]

    oa, lse_a = _swa_fwd(qa, ka, va, sink_rows, slope_rows, nb)
    if comm is None:
        (ob, ob_exact, lse_b), _ = _fox_fwd(qb, kb, vb, c_row, nb)
    else:
        (ob, ob_exact, lse_b), gathered = _fox_fwd(qb, kb, vb, c_row, nb, comm.gather(GATHER_LATE))
        wa, wb, wo, w2i, w2o = comm.gathered(GATHER_LATE, gathered)
    wa_p = jnp.concatenate(_pad_tiles(wa, 0, A_HEADS, A_SLOT, 0), axis=0)
    h2, mixed = _merge_fwd(h1, oa, ob, ga, gb, wa_p, wb, wo)
    (h3, g2, u2), _ = _ffn_fwd(h2, n2, w2i, w2o)
    loss, dh3, d_nfin = _loss_head(h3.reshape(bsz, l, d), nfin, target)

    (dh2, n2b, a2, dgu2, df2, dn2_parts), _ = _ffn_bwd(dh3.reshape(t, d), h2, n2, g2, u2, w2i, w2o)
    g_w2o = _tn_matmul(a2, df2, "grad_ffn2_w_out")
    g_w2i = _tn_matmul(n2b, dgu2, "grad_ffn2_w_in")

    hosted = comm.swap("ffn2", dict(ffn2_w_in=g_w2i, ffn2_w_out=g_w2o)) if comm else None
    (dya, dyb, doa, dob, dga, dgb, dh2b), swapped = _merge_bwd(dh2, oa, ob, ga, gb, wa_p, wb, wo, hosted)
    g_wo = _tn_matmul(mixed, dh2b, "grad_w_out")
    g_wa = jnp.concatenate(_unpad_tiles(_tn_matmul(oa, dya, "grad_w_branch_a"), 0, A_HEADS, A_SLOT, 0), axis=0)
    g_wb = _tn_matmul(ob, dyb, "grad_w_branch_b")

    dqa, dka, dva, dsink_rows = _swa_bwd(qa, ka, va, doa, lse_a, sink_rows, slope_rows, nb)
    hosted = comm.scatter("ffn2", swapped) if comm else None
    (dqb, dkb, dvb, dc_row), pieces = _fox_bwd(qb, kb, vb, ob_exact, dob, lse_b, c_row, nb, hosted)
    if comm:
        comm.received("ffn2", pieces)
    dc = dc_row.reshape(bsz, B_HEADS, l).transpose(0, 2, 1).reshape(t, B_HEADS)
    dc = jnp.concatenate([dc, jnp.zeros((t, F_COLS - B_HEADS), F32)], axis=1)
    df_logit, db_parts = _forget_cumsum_bwd(dc, f_logit, b_pad, nb)

    dproj = jnp.concatenate([dqa, dka, dva, dqb, dkb, dvb, dga, dgb, df_logit], axis=1)
    g_win = _unlayout_w_in(_tn_matmul(un, dproj, "grad_w_in"))
    hosted = comm.swap("mixer", dict(w_in=g_win, w_branch_a=g_wa, w_branch_b=g_wb, w_out=g_wo)) if comm else None
    (dh1, dnmix_parts), swapped = _proj_bwd(dh2, h1, nmix, dproj, wp, hosted)
    hosted = comm.scatter("mixer", swapped) if comm else None
    (dh0, n1b, a1, dgu1, df1, dn1_parts), pieces = _ffn_bwd(dh1, h0, n1, g1, u1, w1i, w1o, hosted)
    dh0 = dh0.reshape(bsz, l, d)
    grad_x = dh0[:, PREFIX:]
    small = dict(
        meta_tokens=jnp.sum(dh0[:, N_PAD:PREFIX], axis=0),
        ffn1_norm=jnp.sum(dn1_parts, axis=0),
        mix_norm=jnp.sum(dnmix_parts, axis=0),
        ffn2_norm=jnp.sum(dn2_parts, axis=0),
        final_norm=d_nfin,
        b_forget=jnp.sum(db_parts, axis=0)[:, :B_HEADS],
        attn_sinks=jnp.sum(dsink_rows.reshape(bsz, A_HEADS, BLOCK), axis=(0, 2)).reshape(1, A_HEADS),
    )
    if comm is None:
        g_w1o = _tn_matmul(a1, df1, "grad_ffn1_w_out")
        g_w1i = _tn_matmul(n1b, dgu1, "grad_ffn1_w_in")
    else:
        comm.received("mixer", pieces)
        g_w1o, gathered = _tn_matmul(a1, df1, "grad_ffn1_w_out", comm.small_gather(loss, small))
        comm.small_gathered(gathered)
        swapped = _run_exchange(comm.swap("ffn1_out", dict(ffn1_w_out=g_w1o)), "exchange_halves_ffn1_out")
        g_w1i, pieces = _tn_matmul(n1b, dgu1, "grad_ffn1_w_in", comm.scatter("ffn1_out", swapped))
        comm.received("ffn1_out", pieces)
    big = dict(ffn1_w_in=g_w1i, ffn1_w_out=g_w1o, w_in=g_win, w_branch_a=g_wa, w_branch_b=g_wb,
               w_out=g_wo, ffn2_w_in=g_w2i, ffn2_w_out=g_w2o)
    return loss, grad_x, small, big


BIG = (
    ("ffn1_w_in", (D_MODEL, 5632), 1),
    ("ffn1_w_out", (2816, D_MODEL), 0),
    ("w_in", (D_MODEL, W_IN_COLS), 1),
    ("w_branch_a", (A_WIDTH, D_MODEL), 1),
    ("w_branch_b", (B_WIDTH, D_MODEL), 1),
    ("w_out", (D_MODEL, D_MODEL), 0),
    ("ffn2_w_in", (D_MODEL, 5632), 1),
    ("ffn2_w_out", (2816, D_MODEL), 0),
)
STACKED = "w_in"


def _coords():
    return lax.axis_index("x"), lax.axis_index("y"), lax.axis_index("c")


def _other_chips(x, y):
    return ((1 - x, y), (x, 1 - y), (1 - x, 1 - y))


def _chip_part(ref, name, shape, axis, k):
    if name == STACKED:
        return ref.at[k]
    size = shape[axis] // N_CHIPS
    start = pl.multiple_of(k * size, size)
    return ref.at[pl.ds(start, size), :] if axis == 0 else ref.at[:, pl.ds(start, size)]


def _full_shape(name, shape):
    return (N_CHIPS, shape[0], shape[1] // N_CHIPS) if name == STACKED else shape


class _Exchange:
    def __init__(self, ins, out_shape, n_sems, ops):
        self.ins, self.out_shape, self.n_sems, self.ops = list(ins), list(out_shape), n_sems, ops

    @property
    def scratch(self):
        return [pltpu.SemaphoreType.DMA((self.n_sems,)), pltpu.SemaphoreType.DMA((self.n_sems,))]


SEMS_PER_GATHER = 9


def _gather_exchange(shards, table):
    n = len(table)
    x_nbr, y_nbr, diagonal = 0, 1, 2

    def ops(ins, outs, send_sems, recv_sems):
        x, y, c = _coords()
        mine = 2 * x + y
        sibling = (x, y, 1 - c)
        chips = _other_chips(x, y)
        slots = [2 * chip[0] + chip[1] for chip in chips]

        def part(i, k):
            name, shape, axis = table[i][:3]
            return _chip_part(outs[i], name, shape, axis, k)

        def half(ref, h):
            rows = ref.shape[0] // 2
            return ref.at[pl.ds(pl.multiple_of(h * rows, rows), rows), :]

        def remote(i, sem, src, dst, device):
            sem = SEMS_PER_GATHER * i + sem
            return pltpu.make_async_remote_copy(src, dst, send_sems.at[sem], recv_sems.at[sem],
                                                device_id=device, device_id_type=MESH_ID)

        def own(i):
            return remote(i, 0, ins[i], part(i, mine), sibling)

        def fetch(i, j, slot):
            if table[i][4]:
                src, dst = half(ins[i], c), half(part(i, slot), c)
            else:
                src, dst = ins[i], part(i, slot)
            return remote(i, 1 + j, src, dst, (chips[j][0], chips[j][1], c))

        def relayed(i, via, of):
            region = half(half(part(i, slots[of]), c), via)
            return remote(i, 4 + via, region, region, (chips[via][0], chips[via][1], c))

        def forward(i, j, h):
            region = half(part(i, slots[j]), h)
            return remote(i, 6 + j, region, region, sibling)

        def start():
            for i in range(n):
                for j in (x_nbr, y_nbr) if table[i][4] else (x_nbr, y_nbr, diagonal):
                    fetch(i, j, mine).start()
            for i in range(n):
                own(i).start()

        def relay():
            for i in range(n):
                if not table[i][4]:
                    for j in range(3):
                        fetch(i, j, slots[j]).wait_recv()
                    continue
                fetch(i, y_nbr, slots[y_nbr]).wait_recv()
                relayed(i, x_nbr, y_nbr).start()
                forward(i, y_nbr, c).start()
                fetch(i, x_nbr, slots[x_nbr]).wait_recv()
                relayed(i, y_nbr, x_nbr).start()
                forward(i, x_nbr, c).start()

        def relay_diagonal():
            for i in range(n):
                if table[i][4]:
                    relayed(i, x_nbr, diagonal).wait_recv()
                    relayed(i, y_nbr, diagonal).wait_recv()
                    forward(i, diagonal, c).start()

        def finish():
            for i in range(n):
                own(i).wait()
                for j in range(3):
                    if table[i][4]:
                        forward(i, j, 1 - c).wait_recv()
                        forward(i, j, c).wait_send()
                    if j != diagonal or not table[i][4]:
                        fetch(i, j, mine).wait_send()
                if table[i][4]:
                    relayed(i, x_nbr, y_nbr).wait_send()
                    relayed(i, y_nbr, x_nbr).wait_send()

        return start, relay, relay_diagonal, finish

    out_shape = [jax.ShapeDtypeStruct(_full_shape(name, shape), dtype) for name, shape, _, dtype, _ in table]
    return _Exchange(shards, out_shape, SEMS_PER_GATHER * n, ops)


def _run_exchange(exchange, name):
    n = len(exchange.ins)

    def body(*refs):
        for phase in exchange.ops(refs[:n], refs[n:2 * n], *refs[2 * n:]):
            phase()

    return pl.pallas_call(
        body,
        name=name,
        in_specs=[HBM_SPEC] * n,
        out_specs=[HBM_SPEC] * n,
        out_shape=exchange.out_shape,
        scratch_shapes=exchange.scratch,
    )(*exchange.ins)


CAST_ROWS = 64


def _cast_hosting(arrays, exchange, name):
    n_a, n_x = len(arrays), len(exchange.ins)

    def body(*refs):
        a_in, x_in = refs[:n_a], refs[n_a:n_a + n_x]
        a_out, x_out = refs[n_a + n_x:2 * n_a + n_x], refs[2 * n_a + n_x:2 * n_a + 2 * n_x]
        start, *rest = exchange.ops(x_in, x_out, *refs[2 * n_a + 2 * n_x:])
        start()
        for src, dst in zip(a_in, a_out):
            def rows(i, carry, src=src, dst=dst):
                window = pl.ds(pl.multiple_of(i * CAST_ROWS, CAST_ROWS), CAST_ROWS)
                dst[window, :] = src[window, :].astype(BF16)
                return carry

            lax.fori_loop(0, src.shape[0] // CAST_ROWS, rows, 0)
        for phase in rest:
            phase()

    outs = pl.pallas_call(
        body,
        name=name,
        in_specs=[VMEM_SPEC] * n_a + [HBM_SPEC] * n_x,
        out_specs=[VMEM_SPEC] * n_a + [HBM_SPEC] * n_x,
        out_shape=[jax.ShapeDtypeStruct(a.shape, BF16) for a in arrays] + exchange.out_shape,
        scratch_shapes=exchange.scratch,
        compiler_params=pltpu.CompilerParams(vmem_limit_bytes=VMEM_LIMIT),
    )(*arrays, *exchange.ins)
    return outs[:n_a], outs[n_a:]


def _host_exchange(exchange, in_refs, out_refs, sem_refs, first, middle, last, late=None):
    start, *relays, finish = exchange.ops(in_refs, out_refs, *sem_refs)
    pl.when(first)(start)
    pl.when(middle)(relays[0])
    if len(relays) > 1:
        pl.when(last if late is None else late)(relays[1])
    pl.when(last)(finish)


def _halves_view(name, shape, axis):
    r, c = shape
    if name == STACKED:
        return (N_CHIPS, 2, r // 2, c // N_CHIPS), lambda ref, h: ref.at[:, h]
    if axis == 1:
        return (2, r // 2, c), lambda ref, h: ref.at[h]
    return (N_CHIPS, 2, r // N_CHIPS // 2, c), lambda ref, h: ref.at[:, h]


def _halves_exchange(grads, entries):
    n_w = len(entries)
    views = [_halves_view(*entry) for entry in entries]

    def ops(ins, outs, send_sems, recv_sems):
        x, y, c = _coords()
        copies = [pltpu.make_async_remote_copy(views[i][1](ins[i], 1 - c), outs[i], send_sems.at[i], recv_sems.at[i],
                                               device_id=(x, y, 1 - c), device_id_type=MESH_ID) for i in range(n_w)]

        def start():
            for cp in copies:
                cp.start()

        def finish():
            for cp in copies:
                cp.wait()

        return start, lambda: None, finish

    half_shape = lambda v: tuple(d for i, d in enumerate(v) if i != (1 if len(v) == 4 else 0))
    out_shape = [jax.ShapeDtypeStruct(half_shape(v[0]), F32) for v in views]
    return _Exchange([g.reshape(v[0]) for g, v in zip(grads, views)], out_shape, n_w, ops)


def _add_sibling(g_view, recv, c, name):
    shape = recv.shape
    if len(shape) == 2:
        tr = _tile(shape[0], 128, 16)
        grid = (shape[0] // tr,)
        g_spec = pl.BlockSpec((None, tr, shape[1]), lambda i, c_ref: (c_ref[0], i, 0))
        r_spec = pl.BlockSpec((tr, shape[1]), lambda i, c_ref: (i, 0))
    else:
        tr = _tile(shape[1], 256, 16)
        grid = (N_CHIPS, shape[1] // tr)
        g_spec = pl.BlockSpec((None, None, tr, shape[2]), lambda k, i, c_ref: (k, c_ref[0], i, 0))
        r_spec = pl.BlockSpec((None, tr, shape[2]), lambda k, i, c_ref: (k, i, 0))

    def body(c_ref, g_ref, r_ref, o_ref):
        o_ref[...] = (g_ref[...] + r_ref[...]).astype(BF16)

    return pl.pallas_call(
        body,
        name="add_sibling_" + name,
        grid_spec=pltpu.PrefetchScalarGridSpec(num_scalar_prefetch=1, grid=grid, in_specs=[g_spec, r_spec],
                                               out_specs=r_spec),
        out_shape=jax.ShapeDtypeStruct(shape, BF16),
        compiler_params=_cparams(("parallel",) * len(grid)),
    )(c, g_view, recv)


def _piece_of(ref, name, axis, k):
    if name == STACKED or axis == 0:
        return ref.at[k]
    size = ref.shape[1] // N_CHIPS
    return ref.at[:, pl.ds(pl.multiple_of(k * size, size), size)]


def _piece_shape(name, shape, axis):
    r, c = shape
    return (r // 2, c // N_CHIPS) if (axis == 1) else (r // N_CHIPS // 2, c)


def _scatter_exchange(partials, entries):
    n_w = len(entries)

    def ops(ins, outs, send_sems, recv_sems):
        x, y, c = _coords()
        chips = _other_chips(x, y)
        copies = []
        for i, (name, _, axis) in enumerate(entries):
            for j, chip in enumerate(chips):
                sem = 3 * i + j
                copies.append(pltpu.make_async_remote_copy(
                    _piece_of(ins[i], name, axis, 2 * chip[0] + chip[1]), outs[i].at[j], send_sems.at[sem],
                    recv_sems.at[sem], device_id=(chip[0], chip[1], c), device_id_type=MESH_ID))

        def start():
            for cp in copies:
                cp.start()

        def finish():
            for cp in copies:
                cp.wait()

        return start, lambda: None, finish

    out_shape = [jax.ShapeDtypeStruct((3,) + _piece_shape(*entry), BF16) for entry in entries]
    return _Exchange(partials, out_shape, 3 * n_w, ops)


def _add_chips(partial, recv, mine, name, axis):
    rows, cols = recv.shape[1:]
    tr = _tile(rows, 256, 16)
    if name == STACKED or axis == 0:
        p_spec = pl.BlockSpec((None, tr, cols), lambda i, k_ref: (k_ref[0], i, 0))
    else:
        p_spec = pl.BlockSpec((tr, cols), lambda i, k_ref: (i, k_ref[0]))

    def body(k_ref, p_ref, r_ref, o_ref):
        f32 = lambda a: a.astype(F32)
        o_ref[...] = ((f32(p_ref[...]) + f32(r_ref[0])) + f32(r_ref[1])) + f32(r_ref[2])

    return pl.pallas_call(
        body,
        name="add_chips_" + name,
        grid_spec=pltpu.PrefetchScalarGridSpec(
            num_scalar_prefetch=1, grid=(rows // tr,),
            in_specs=[p_spec, pl.BlockSpec((3, tr, cols), lambda i, k_ref: (0, i, 0))],
            out_specs=pl.BlockSpec((tr, cols), lambda i, k_ref: (i, 0))),
        out_shape=jax.ShapeDtypeStruct((rows, cols), F32),
        compiler_params=_cparams(("parallel",)),
    )(mine, partial, recv)


def _share_with_sibling(halves):
    n_w = len(halves)

    def body(*refs):
        ins, outs = refs[:n_w], refs[n_w:2 * n_w]
        send_sems, recv_sems = refs[2 * n_w:]
        x, y, c = _coords()
        copies = [pltpu.make_async_remote_copy(ins[i], outs[i], send_sems.at[i], recv_sems.at[i],
                                               device_id=(x, y, 1 - c), device_id_type=MESH_ID) for i in range(n_w)]
        for cp in copies:
            cp.start()
        for cp in copies:
            cp.wait()

    return pl.pallas_call(
        body,
        name="share_with_sibling",
        in_specs=[HBM_SPEC] * n_w,
        out_specs=[HBM_SPEC] * n_w,
        out_shape=[jax.ShapeDtypeStruct(h.shape, F32) for h in halves],
        scratch_shapes=[pltpu.SemaphoreType.DMA((n_w,)), pltpu.SemaphoreType.DMA((n_w,))],
    )(*halves)


SMALL_ROWS = 168


def _small_exchange(buf):
    def ops(ins, outs, send_sems, recv_sems):
        x, y, c = _coords()
        me = 4 * x + 2 * y + c
        peers = [(x ^ fx, y ^ fy, c ^ fc) for fx in (0, 1) for fy in (0, 1) for fc in (0, 1)][1:]

        def copy(j, slot, dev):
            return pltpu.make_async_remote_copy(ins[0], outs[0].at[slot], send_sems.at[j], recv_sems.at[j],
                                                device_id=dev, device_id_type=MESH_ID)

        own = pltpu.make_async_copy(ins[0], outs[0].at[me], send_sems.at[N_DEV - 1])

        def start():
            own.start()
            for j, dev in enumerate(peers):
                copy(j, me, dev).start()

        def finish():
            for j, dev in enumerate(peers):
                copy(j, 4 * dev[0] + 2 * dev[1] + dev[2], dev).wait()
            own.wait()

        return start, lambda: None, finish

    return _Exchange([buf], [jax.ShapeDtypeStruct((N_DEV,) + buf.shape, F32)], N_DEV, ops)


def _sum_devices(gathered):
    def body(g_ref, out_ref):
        acc = g_ref[0]
        for d in range(1, N_DEV):
            acc = acc + g_ref[d]
        out_ref[...] = acc

    return pl.pallas_call(
        body,
        name="sum_devices",
        in_specs=[VMEM_SPEC],
        out_specs=VMEM_SPEC,
        out_shape=jax.ShapeDtypeStruct(gathered.shape[1:], F32),
    )(gathered)


def _adamw(w, g, m, v):
    r, rest = w.shape[0], w.shape[1:]
    per_row = 1
    for dim in rest:
        per_row *= dim
    tr = _tile(r, max(8, (5 << 19) // (4 * per_row)), 8 if len(rest) == 1 else 1)

    def body(w_ref, g_ref, m_ref, v_ref, d_ref, mo_ref, vo_ref):
        gg = g_ref[...]
        mm = ADAM_B1 * m_ref[...] + (1.0 - ADAM_B1) * gg
        vv = ADAM_B2 * v_ref[...] + (1.0 - ADAM_B2) * (gg * gg)
        m_hat = mm / (1.0 - ADAM_B1 ** ADAM_STEP)
        v_hat = vv / (1.0 - ADAM_B2 ** ADAM_STEP)
        d_ref[...] = -ADAM_LR * (m_hat / (jnp.sqrt(v_hat) + ADAM_EPS) + ADAM_WD * w_ref[...])
        mo_ref[...] = mm
        vo_ref[...] = vv

    spec = pl.BlockSpec((tr,) + rest, lambda i: (i,) + (0,) * len(rest))
    return pl.pallas_call(
        body,
        name="adamw",
        grid=(r // tr,),
        in_specs=[spec] * 4,
        out_specs=[spec] * 3,
        out_shape=[jax.ShapeDtypeStruct(w.shape, F32)] * 3,
        compiler_params=_cparams(("parallel",)),
    )(w, g, m, v)


def _adamw_halves(w, own, other, m, v, c, name):
    r, cols = w.shape
    half = r // 2
    tr = _tile(half, 256, 8)
    nt = half // tr
    whole = pl.BlockSpec((tr, cols), lambda h, i, c_ref: (h * nt + i, 0))
    part = pl.BlockSpec((tr, cols), lambda h, i, c_ref: (i, 0))

    def body(c_ref, w_ref, own_ref, other_ref, m_ref, v_ref, g_ref, d_ref, mo_ref, vo_ref):
        gg = jnp.where(pl.program_id(0) == c_ref[0], own_ref[...], other_ref[...])
        g_ref[...] = gg
        mm = ADAM_B1 * m_ref[...] + (1.0 - ADAM_B1) * gg
        vv = ADAM_B2 * v_ref[...] + (1.0 - ADAM_B2) * (gg * gg)
        m_hat = mm / (1.0 - ADAM_B1 ** ADAM_STEP)
        v_hat = vv / (1.0 - ADAM_B2 ** ADAM_STEP)
        d_ref[...] = -ADAM_LR * (m_hat / (jnp.sqrt(v_hat) + ADAM_EPS) + ADAM_WD * w_ref[...])
        mo_ref[...] = mm
        vo_ref[...] = vv

    return pl.pallas_call(
        body,
        name="adamw_" + name,
        grid_spec=pltpu.PrefetchScalarGridSpec(
            num_scalar_prefetch=1, grid=(2, nt),
            in_specs=[whole, part, part, whole, whole], out_specs=[whole] * 4),
        out_shape=[jax.ShapeDtypeStruct((r, cols), F32)] * 4,
        compiler_params=_cparams(("parallel", "parallel")),
    )(c, w, own, other, m, v)


GATHER_FIRST = ("ffn1_w_in", "ffn1_w_out")
GATHER_PROJ = ("w_in",)
GATHER_LATE = ("w_branch_a", "w_branch_b", "w_out", "ffn2_w_in", "ffn2_w_out")


class _Comm:
    def __init__(self, shards, c_arr, mine_arr):
        self.shards, self.c, self.mine = shards, c_arr, mine_arr
        self.groups, self.halves = {}, {}
        self.by_name = {entry[0]: entry for entry in BIG}

    def small_gather(self, loss, small):
        pad_lanes = lambda a: jnp.concatenate([a, jnp.zeros((1, LANES - a.shape[1]), F32)], axis=1)
        buf = jnp.concatenate([
            small["meta_tokens"].reshape(128, LANES),
            small["ffn1_norm"].reshape(8, LANES), small["mix_norm"].reshape(8, LANES),
            small["ffn2_norm"].reshape(8, LANES), small["final_norm"].reshape(8, LANES),
            loss, pad_lanes(small["b_forget"]), pad_lanes(small["attn_sinks"]),
            jnp.zeros((SMALL_ROWS - 163, LANES), F32)], axis=0)
        return _small_exchange(buf)

    def small_gathered(self, outs):
        self.reduced = _sum_devices(outs[0])

    def gather(self, names):
        table = [self.by_name[n] + (BF16, True) for n in names]
        return _gather_exchange([self.shards[n] for n in names], table)

    def gathered(self, names, outs):
        return [o.transpose(1, 0, 2).reshape(D_MODEL, W_IN_COLS) if n == STACKED else o for n, o in zip(names, outs)]

    def swap(self, tag, grads):
        entries = [self.by_name[n] for n in grads]
        arrays = [g.reshape(D_MODEL, N_CHIPS, W_IN_COLS // N_CHIPS).transpose(1, 0, 2) if n == STACKED else g
                  for n, g in grads.items()]
        self.groups[tag] = (entries, arrays)
        return _halves_exchange(arrays, entries)

    def scatter(self, tag, received):
        entries, arrays = self.groups[tag]
        views = [_halves_view(*entry) for entry in entries]
        partials = [_add_sibling(g.reshape(v[0]), r, self.c, name)
                    for g, v, r, (name, _, _) in zip(arrays, views, received, entries)]
        self.groups[tag] = (entries, partials)
        return _scatter_exchange(partials, entries)

    def received(self, tag, pieces):
        entries, partials = self.groups[tag]
        for p, r, (name, _, axis) in zip(partials, pieces, entries):
            self.halves[name] = _add_chips(p, r, self.mine, name, axis)

    def finish(self):
        names = [n for n, _, _ in BIG]
        own = [self.halves[n] for n in names]
        return dict(zip(names, zip(own, _share_with_sibling(own))))


def kernel(x, meta_tokens, ffn1_norm, ffn1_w_in, ffn1_w_out, mix_norm, w_in, b_forget, attn_sinks, w_branch_a, w_branch_b, w_out, ffn2_norm, ffn2_w_in, ffn2_w_out, final_norm, loss_target, m_meta_tokens, m_ffn1_norm, m_ffn1_w_in, m_ffn1_w_out, m_mix_norm, m_w_in, m_b_forget, m_attn_sinks, m_w_branch_a, m_w_branch_b, m_w_out, m_ffn2_norm, m_ffn2_w_in, m_ffn2_w_out, m_final_norm, v_meta_tokens, v_ffn1_norm, v_ffn1_w_in, v_ffn1_w_out, v_mix_norm, v_w_in, v_b_forget, v_attn_sinks, v_w_branch_a, v_w_branch_b, v_w_out, v_ffn2_norm, v_ffn2_w_in, v_ffn2_w_out, v_final_norm):
    given = dict(locals())
    names = ["meta_tokens", "ffn1_norm", "ffn1_w_in", "ffn1_w_out", "mix_norm", "w_in", "b_forget", "attn_sinks",
             "w_branch_a", "w_branch_b", "w_out", "ffn2_norm", "ffn2_w_in", "ffn2_w_out", "final_norm"]
    big_names = [n for n, _, _ in BIG]
    cx, cy, cc = _coords()
    c_arr = cc.reshape(1).astype(jnp.int32)
    mine_arr = (2 * cx + cy).reshape(1).astype(jnp.int32)

    by_name = {entry[0]: entry for entry in BIG}
    shards = {n: given[n][0].astype(BF16) for n in GATHER_FIRST}
    table = [by_name[n] + (BF16, True) for n in GATHER_FIRST] + [("meta_tokens", (N_META, D_MODEL), 1, F32, False)]
    first = _gather_exchange([shards[n] for n in GATHER_FIRST] + [meta_tokens], table)
    late_names = [n for n in big_names if n not in GATHER_FIRST]
    casts, (w1i, w1o, meta_full) = _cast_hosting([given[n][0] for n in late_names], first, "gather_first")
    shards.update(zip(late_names, casts))
    comm = _Comm(shards, c_arr, mine_arr)
    norms = (ffn1_norm, mix_norm, ffn2_norm, final_norm.reshape(1, D_MODEL))
    loss, grad_x, small, big = _local_step(x, loss_target, meta_full, norms, b_forget, attn_sinks, (w1i, w1o), comm)

    swap = comm.swap("ffn1_in", dict(ffn1_w_in=big["ffn1_w_in"]))
    last = comm.scatter("ffn1_in", _run_exchange(swap, "exchange_halves_ffn1_in"))
    comm.received("ffn1_in", _run_exchange(last, "scatter_chip_sums"))
    grad_halves = comm.finish()
    grads = {}

    red = comm.reduced
    meta_cols = red[:128].reshape(N_META, D_MODEL)
    grads["meta_tokens"] = lax.dynamic_slice_in_dim(meta_cols, (2 * cx + cy) * (D_MODEL // N_CHIPS),
                                                    D_MODEL // N_CHIPS, axis=1)
    grads["ffn1_norm"] = red[128:136].reshape(1, D_MODEL)
    grads["mix_norm"] = red[136:144].reshape(1, D_MODEL)
    grads["ffn2_norm"] = red[144:152].reshape(1, D_MODEL)
    grads["final_norm"] = red[152:160].reshape(1, D_MODEL)
    loss_out = red[160, 0]
    grads["b_forget"] = red[161:162, :B_HEADS]
    grads["attn_sinks"] = red[162:163, :A_HEADS]

    out_g, out_d, out_m, out_v = [], [], [], []
    for n in names:
        w_full = given[n]
        shape = w_full.shape
        two_d = (lambda a: a.reshape(shape[-2], shape[-1])) if len(shape) >= 2 else (lambda a: a.reshape(1, shape[0]))
        if n == STACKED:
            own, other = grad_halves[n]
            g_t = jnp.concatenate([jnp.where(cc == 0, own, other), jnp.where(cc == 0, other, own)], axis=0).T
            tiles = lambda a: a.reshape(shape[-1], shape[-2] // LANES, LANES)
            untile = lambda a: a.reshape(shape[-1], shape[-2]).T
            d2, m2, v2 = [untile(a) for a in _adamw(tiles(two_d(w_full).T), tiles(g_t), tiles(two_d(given["m_" + n]).T),
                                                     tiles(two_d(given["v_" + n]).T))]
            g2 = g_t.T
        elif n in grad_halves:
            own, other = grad_halves[n]
            g2, d2, m2, v2 = _adamw_halves(two_d(w_full), own, other, two_d(given["m_" + n]),
                                           two_d(given["v_" + n]), c_arr, n)
        else:
            g2 = two_d(grads[n])
            d2, m2, v2 = _adamw(two_d(w_full), g2, two_d(given["m_" + n]), two_d(given["v_" + n]))
        out_g.append(g2.reshape(shape))
        out_d.append(d2.reshape(shape))
        out_m.append(m2.reshape(shape))
        out_v.append(v2.reshape(shape))
    return (loss_out, grad_x, *out_g, *out_d, *out_m, *out_v)
```

```python
import jax
import jax.numpy as jnp
from jax import lax
from jax.experimental import pallas as pl
from jax.experimental.pallas import tpu as pltpu

F32 = jnp.float32
BF16 = jnp.bfloat16

D_MODEL = 1024
N_META = 16
BLOCK = 128
LANES = 128
PREFIX = BLOCK
N_PAD = PREFIX - N_META
HEAD_DIM = 64
A_HEADS = 8
A_KV_HEADS = 2
A_GROUP = 4
B_HEADS = 8
B_PAIRS = B_HEADS // 2
A_WIDTH = A_HEADS * HEAD_DIM
A_KV_WIDTH = A_KV_HEADS * HEAD_DIM
B_WIDTH = B_HEADS * HEAD_DIM
W_IN_COLS = A_WIDTH + 2 * A_KV_WIDTH + 3 * B_WIDTH + B_HEADS + 2 * D_MODEL
SRC_KA = A_WIDTH
SRC_VA = SRC_KA + A_KV_WIDTH
SRC_QB = SRC_VA + A_KV_WIDTH
SRC_KB = SRC_QB + B_WIDTH
SRC_VB = SRC_KB + B_WIDTH
SRC_F = SRC_VB + B_WIDTH
SRC_GA = SRC_F + B_HEADS
SRC_GB = SRC_GA + D_MODEL
A_PAD_WIDTH = A_HEADS * LANES
B_PAD_WIDTH = B_HEADS * LANES
F_COLS = LANES
OFF_QA = 0
OFF_KA = SRC_KA
OFF_VA = SRC_VA
OFF_F = OFF_VA + A_KV_WIDTH
OFF_QB = OFF_F + F_COLS
OFF_KB = OFF_QB + B_WIDTH
OFF_VB = OFF_KB + B_WIDTH
OFF_GA = OFF_VB + B_WIDTH
OFF_GB = OFF_GA + D_MODEL
P_COLS = OFF_GB + D_MODEL
P_PIECES = ((0, OFF_QB), (OFF_QB, OFF_GA - OFF_QB), (OFF_GA, P_COLS - OFF_GA))
EPS = 1e-6
NEG = -1e30
SCALE = HEAD_DIM ** -0.5
KEY_BLOCKS = 4

ADAM_LR = 0.001
ADAM_B1 = 0.9
ADAM_B2 = 0.999
ADAM_EPS = 1e-08
ADAM_WD = 0.01
ADAM_STEP = 10

N_CHIPS = 4
N_DEV = 8
VMEM_LIMIT = 56 * 1024 * 1024

NT_DIMS = (((1,), (1,)), ((), ()))
TN_DIMS = (((0,), (0,)), ((), ()))
MESH_ID = pl.DeviceIdType.MESH
HBM_SPEC = pl.BlockSpec(memory_space=pltpu.HBM)
VMEM_SPEC = pl.BlockSpec(memory_space=pltpu.VMEM)


def _tile(n, target, mult=16):
    best = None
    for t in range(mult, min(n, target) + 1, mult):
        if n % t == 0:
            best = t
    return best if best is not None else n


def _cparams(sem):
    return pltpu.CompilerParams(dimension_semantics=sem, vmem_limit_bytes=VMEM_LIMIT)


def _rms_scale(h):
    return lax.rsqrt(jnp.mean(h * h, axis=-1, keepdims=True) + EPS)


def _rms_bwd(dn, h, w):
    r = _rms_scale(h)
    dw = jnp.sum(dn * (h * r), axis=0, keepdims=True)
    z = dn * w
    dh = r * z - h * ((r * r * r) * jnp.mean(z * h, axis=-1, keepdims=True))
    return dh, dw


def _ffn_fwd(h, norm_w, w_in, w_out, exchange=None):
    t, d = h.shape
    f = w_out.shape[0]
    tm = _tile(t, 272)
    tc = _tile(f, 256, 128)
    nj = f // tc
    ni = t // tm
    n_x = len(exchange.ins) if exchange else 0

    def body(*refs):
        h_ref, nw_ref, wi_ref, wo_ref = refs[:4]
        hout_ref, g_ref, u_ref = refs[4 + n_x:7 + n_x]
        a_scr = refs[7 + 2 * n_x]
        i = pl.program_id(0)
        if exchange:
            _host_exchange(exchange, refs[4:4 + n_x], refs[7 + n_x:7 + 2 * n_x], refs[8 + 2 * n_x:],
                           i == 0, i == ni // 3, i == ni - 1, late=i == 2 * ni // 3)
        hh = h_ref[...]
        n = ((hh * _rms_scale(hh)) * nw_ref[...]).astype(BF16)
        for j in range(nj):
            cols = slice(j * tc, (j + 1) * tc)
            g = jnp.dot(n, wi_ref[:, j * tc:(j + 1) * tc], preferred_element_type=F32)
            u = jnp.dot(n, wi_ref[:, f + j * tc:f + (j + 1) * tc], preferred_element_type=F32)
            g_ref[:, cols] = g
            u_ref[:, cols] = u
            a_scr[:, cols] = ((g * jax.nn.sigmoid(g)) * u).astype(BF16)
        hout_ref[...] = hh + 0.5 * jnp.dot(a_scr[...], wo_ref[...], preferred_element_type=F32)

    resident = lambda a: pl.BlockSpec(a.shape, lambda i: (0, 0), pipeline_mode=pl.Buffered(1))
    row = lambda w: pl.BlockSpec((tm, w), lambda i: (i, 0))
    outs = pl.pallas_call(
        body,
        name="ffn_fwd",
        grid=(ni,),
        in_specs=[row(d), pl.BlockSpec((1, d), lambda i: (0, 0)), resident(w_in), resident(w_out)] + [HBM_SPEC] * n_x,
        out_specs=[row(d), row(f), row(f)] + [HBM_SPEC] * n_x,
        out_shape=[
            jax.ShapeDtypeStruct((t, d), F32),
            jax.ShapeDtypeStruct((t, f), F32),
            jax.ShapeDtypeStruct((t, f), F32),
        ] + (exchange.out_shape if exchange else []),
        scratch_shapes=[pltpu.VMEM((tm, f), BF16)] + (exchange.scratch if exchange else []),
        compiler_params=_cparams(("arbitrary",) if exchange else ("parallel",)),
    )(h, norm_w, w_in, w_out, *(exchange.ins if exchange else []))
    return outs[:3], outs[3:]


def _ffn_bwd(dh_out, h, norm_w, g, u, w_in, w_out, exchange=None):
    t, d = h.shape
    f = w_out.shape[0]
    tm = _tile(t, 272)
    tc = _tile(f, 256, 128)
    nj = f // tc
    ni = t // tm
    n_x = len(exchange.ins) if exchange else 0

    def body(*refs):
        dho_ref, h_ref, nw_ref, g_ref, u_ref, wi_ref, wo_ref = refs[:7]
        dhin_ref, n_ref, a_ref, dgu_ref, df_ref, dnw_ref = refs[7 + n_x:13 + n_x]
        i = pl.program_id(0)
        if exchange:
            _host_exchange(exchange, refs[7:7 + n_x], refs[13 + n_x:13 + 2 * n_x], refs[13 + 2 * n_x:],
                           i == 0, i == ni - 1, i == ni - 1)
        hh = h_ref[...]
        nw = nw_ref[...]
        n_ref[...] = ((hh * _rms_scale(hh)) * nw).astype(BF16)
        dho = dho_ref[...]
        df = (0.5 * dho).astype(BF16)
        df_ref[...] = df
        for j in range(nj):
            cols = slice(j * tc, (j + 1) * tc)
            da = lax.dot_general(df, wo_ref[cols, :], NT_DIMS, preferred_element_type=F32)
            gg = g_ref[:, cols]
            uu = u_ref[:, cols]
            sig = jax.nn.sigmoid(gg)
            sl = gg * sig
            a_ref[:, cols] = (sl * uu).astype(BF16)
            dgu_ref[0, :, cols] = ((da * uu) * (sig * (1.0 + gg * (1.0 - sig)))).astype(BF16)
            dgu_ref[1, :, cols] = (da * sl).astype(BF16)
        dn = (lax.dot_general(dgu_ref[0], wi_ref[:, :f], NT_DIMS, preferred_element_type=F32)
              + lax.dot_general(dgu_ref[1], wi_ref[:, f:], NT_DIMS, preferred_element_type=F32))
        dh, dw = _rms_bwd(dn, hh, nw)
        dhin_ref[...] = dho + dh
        dnw_ref[0] = dw

    resident = lambda a: pl.BlockSpec(a.shape, lambda i: (0, 0), pipeline_mode=pl.Buffered(1))
    row = lambda w: pl.BlockSpec((tm, w), lambda i: (i, 0))
    outs = pl.pallas_call(
        body,
        name="ffn_bwd",
        grid=(ni,),
        in_specs=[row(d), row(d), pl.BlockSpec((1, d), lambda i: (0, 0)), row(f), row(f),
                  resident(w_in), resident(w_out)] + [HBM_SPEC] * n_x,
        out_specs=[row(d), row(d), row(f), pl.BlockSpec((2, tm, f), lambda i: (0, i, 0)), row(d),
                   pl.BlockSpec((1, 1, d), lambda i: (i, 0, 0))] + [HBM_SPEC] * n_x,
        out_shape=[
            jax.ShapeDtypeStruct((t, d), F32),
            jax.ShapeDtypeStruct((t, d), BF16),
            jax.ShapeDtypeStruct((t, f), BF16),
            jax.ShapeDtypeStruct((2, t, f), BF16),
            jax.ShapeDtypeStruct((t, d), BF16),
            jax.ShapeDtypeStruct((ni, 1, d), F32),
        ] + (exchange.out_shape if exchange else []),
        scratch_shapes=exchange.scratch if exchange else [],
        compiler_params=_cparams(("arbitrary",) if exchange else ("parallel",)),
    )(dh_out, h, norm_w, g, u, w_in, w_out, *(exchange.ins if exchange else []))
    return outs[:6], outs[6:]


def _tn_matmul(a, b, name, exchange=None):
    t, k = a.shape
    split = b.ndim == 3
    n = 2 * b.shape[2] if split else b.shape[1]
    tk = _tile(k, 512, 128)
    tn = _tile(b.shape[-1], 1408, 128)
    per_half = b.shape[-1] // tn
    ni, nj = k // tk, n // tn
    n_x = len(exchange.ins) if exchange else 0

    def body(*refs):
        a_ref, b_ref, o_ref = refs[0], refs[1], refs[2 + n_x]
        if exchange:
            i, j = pl.program_id(0), pl.program_id(1)
            at_end = (i == ni - 1) & (j == nj - 1)
            _host_exchange(exchange, refs[2:2 + n_x], refs[3 + n_x:3 + 2 * n_x], refs[3 + 2 * n_x:],
                           (i == 0) & (j == 0), at_end, at_end)
        o_ref[...] = lax.dot_general(a_ref[...], b_ref[...], TN_DIMS, preferred_element_type=F32)

    if split:
        b_spec = pl.BlockSpec((None, t, tn), lambda i, j: (j // per_half, 0, j % per_half))
    else:
        b_spec = pl.BlockSpec((t, tn), lambda i, j: (0, j))
    outs = pl.pallas_call(
        body,
        name=name,
        grid=(ni, nj),
        in_specs=[pl.BlockSpec((t, tk), lambda i, j: (0, i)), b_spec] + [HBM_SPEC] * n_x,
        out_specs=[pl.BlockSpec((tk, tn), lambda i, j: (i, j))] + [HBM_SPEC] * n_x,
        out_shape=[jax.ShapeDtypeStruct((k, n), F32)] + (exchange.out_shape if exchange else []),
        scratch_shapes=exchange.scratch if exchange else [],
        compiler_params=_cparams(("arbitrary", "arbitrary") if exchange else ("parallel", "parallel")),
    )(a, b, *(exchange.ins if exchange else []))
    return (outs[0], outs[1:]) if exchange else outs[0]


A_SLOT = lambda h: h // A_GROUP
B_SLOT = lambda h: h % 2

PROJ_PARTS = (
    (OFF_QA, A_WIDTH, A_PAD_WIDTH, True, A_SLOT), (OFF_KA, A_KV_WIDTH, A_KV_WIDTH, True, None),
    (OFF_VA, A_KV_WIDTH, A_KV_WIDTH, True, None), (OFF_QB, B_WIDTH, B_WIDTH, True, None),
    (OFF_KB, B_WIDTH, B_PAD_WIDTH, True, B_SLOT), (OFF_VB, B_WIDTH, B_PAD_WIDTH, True, B_SLOT),
    (OFF_GA, D_MODEL, D_MODEL, False, None), (OFF_GB, D_MODEL, D_MODEL, False, None), (OFF_F, F_COLS, F_COLS, False, None),
)


def _head_tile(pair, head, slot):
    lane_slot = lax.broadcasted_iota(jnp.int32, pair.shape, 1) // HEAD_DIM
    moved = pair if head % 2 == slot else pltpu.roll(pair, HEAD_DIM, 1)
    return jnp.where(lane_slot == slot, moved, 0.0)


def _proj_fwd(h, norm_w, w_p):
    t, d = h.shape
    tm = _tile(t, 272)

    def body(h_ref, nw_ref, w_ref, u_ref, *part_refs):
        hh = h_ref[...]
        un = ((hh * _rms_scale(hh)) * nw_ref[...]).astype(BF16)
        u_ref[...] = un
        for (off, width, _, _, slot), p_ref in zip(PROJ_PARTS, part_refs):
            if slot is None:
                p_ref[...] = jnp.dot(un, w_ref[:, off:off + width], preferred_element_type=F32).astype(p_ref.dtype)
                continue
            part = jnp.dot(un, w_ref[:, off:off + width], preferred_element_type=F32)
            for pair in range(width // LANES):
                x = part[:, pair * LANES:(pair + 1) * LANES]
                for head in (2 * pair, 2 * pair + 1):
                    p_ref[:, head * LANES:(head + 1) * LANES] = _head_tile(x, head, slot(head)).astype(p_ref.dtype)

    row = lambda w: pl.BlockSpec((tm, w), lambda i: (i, 0))
    return pl.pallas_call(
        body,
        name="proj_fwd",
        grid=(t // tm,),
        in_specs=[row(d), pl.BlockSpec((1, d), lambda i: (0, 0)),
                  pl.BlockSpec(w_p.shape, lambda i: (0, 0), pipeline_mode=pl.Buffered(1))],
        out_specs=[row(d)] + [row(width) for _, _, width, _, _ in PROJ_PARTS],
        out_shape=[jax.ShapeDtypeStruct((t, d), BF16)]
        + [jax.ShapeDtypeStruct((t, width), BF16 if is_bf else F32) for _, _, width, is_bf, _ in PROJ_PARTS],
        compiler_params=_cparams(("parallel",)),
    )(h, norm_w, w_p)


def _proj_bwd(dh_out, h, norm_w, dproj, w_p, exchange=None):
    t, d = h.shape
    tm = _tile(t, 272)
    ni = t // tm
    n_p = len(P_PIECES)
    n_in = 4 + n_p
    n_x = len(exchange.ins) if exchange else 0

    def body(*refs):
        dho_ref, h_ref, nw_ref = refs[:3]
        dp_refs, w_ref = refs[3:3 + n_p], refs[3 + n_p]
        dhin_ref, dnw_ref = refs[n_in + n_x:n_in + 2 + n_x]
        if exchange:
            i = pl.program_id(0)
            _host_exchange(exchange, refs[n_in:n_in + n_x], refs[n_in + 2 + n_x:n_in + 2 + 2 * n_x],
                           refs[n_in + 2 + 2 * n_x:], i == 0, i == ni - 1, i == ni - 1)
        dn = None
        for dp_ref, (off, width) in zip(dp_refs, P_PIECES):
            part = lax.dot_general(dp_ref[...], w_ref[:, off:off + width], NT_DIMS, preferred_element_type=F32)
            dn = part if dn is None else dn + part
        dh, dw = _rms_bwd(dn, h_ref[...], nw_ref[...])
        dhin_ref[...] = dho_ref[...] + dh
        dnw_ref[0] = dw

    row = lambda w: pl.BlockSpec((tm, w), lambda i: (i, 0))
    outs = pl.pallas_call(
        body,
        name="proj_bwd",
        grid=(ni,),
        in_specs=[row(d), row(d), pl.BlockSpec((1, d), lambda i: (0, 0))] + [row(width) for _, width in P_PIECES]
        + [pl.BlockSpec(w_p.shape, lambda i: (0, 0), pipeline_mode=pl.Buffered(1))] + [HBM_SPEC] * n_x,
        out_specs=[row(d), pl.BlockSpec((1, 1, d), lambda i: (i, 0, 0))] + [HBM_SPEC] * n_x,
        out_shape=[jax.ShapeDtypeStruct((t, d), F32), jax.ShapeDtypeStruct((ni, 1, d), F32)]
        + (exchange.out_shape if exchange else []),
        scratch_shapes=exchange.scratch if exchange else [],
        compiler_params=_cparams(("arbitrary",) if exchange else ("parallel",)),
    )(dh_out, h, norm_w, *dproj, w_p, *(exchange.ins if exchange else []))
    return outs[:2], outs[2:]


def _merge_fwd(h, oa, ob, ga, gb, wa, wb, wo):
    t, d = h.shape
    tm = _tile(t, 544)

    def body(h_ref, oa_ref, ob_ref, ga_ref, gb_ref, wa_ref, wb_ref, wo_ref, hout_ref, mix_ref):
        ya = jnp.dot(oa_ref[...], wa_ref[...], preferred_element_type=F32)
        yb = jnp.dot(ob_ref[...], wb_ref[...], preferred_element_type=F32)
        mixed = (jax.nn.sigmoid(ga_ref[...]) * ya + jax.nn.sigmoid(gb_ref[...]) * yb).astype(BF16)
        mix_ref[...] = mixed
        hout_ref[...] = h_ref[...] + jnp.dot(mixed, wo_ref[...], preferred_element_type=F32)

    row = lambda w: pl.BlockSpec((tm, w), lambda i: (i, 0))
    full = lambda a: pl.BlockSpec(a.shape, lambda i: (0, 0))
    return pl.pallas_call(
        body,
        name="merge_fwd",
        grid=(t // tm,),
        in_specs=[row(d), row(oa.shape[1]), row(ob.shape[1]), row(d), row(d), full(wa), full(wb), full(wo)],
        out_specs=[row(d), row(d)],
        out_shape=[jax.ShapeDtypeStruct((t, d), F32), jax.ShapeDtypeStruct((t, d), BF16)],
        compiler_params=_cparams(("parallel",)),
    )(h, oa, ob, ga, gb, wa, wb, wo)


def _merge_bwd(dh, oa, ob, ga, gb, wa, wb, wo, exchange=None):
    t, d = dh.shape
    tm = _tile(t, 544)
    ni = t // tm
    n_x = len(exchange.ins) if exchange else 0

    def body(*refs):
        dh_ref, oa_ref, ob_ref, ga_ref, gb_ref, wa_ref, wb_ref, wo_ref = refs[:8]
        dya_ref, dyb_ref, doa_ref, dob_ref, dg_ref, dhb_ref = refs[8 + n_x:14 + n_x]
        if exchange:
            i = pl.program_id(0)
            _host_exchange(exchange, refs[8:8 + n_x], refs[14 + n_x:14 + 2 * n_x], refs[14 + 2 * n_x:],
                           i == 0, i == ni - 1, i == ni - 1)
        dhb = dh_ref[...].astype(BF16)
        dhb_ref[...] = dhb
        dmix = lax.dot_general(dhb, wo_ref[...], NT_DIMS, preferred_element_type=F32)
        for branch, (o_ref, g_ref, w_ref, dy_ref, do_ref) in enumerate((
                (oa_ref, ga_ref, wa_ref, dya_ref, doa_ref),
                (ob_ref, gb_ref, wb_ref, dyb_ref, dob_ref))):
            y = jnp.dot(o_ref[...], w_ref[...], preferred_element_type=F32)
            s = jax.nn.sigmoid(g_ref[...])
            dy = (dmix * s).astype(BF16)
            dy_ref[...] = dy
            dg_ref[:, branch * d:(branch + 1) * d] = ((dmix * y) * (s * (1.0 - s))).astype(BF16)
            do_ref[...] = lax.dot_general(dy, w_ref[...], NT_DIMS, preferred_element_type=F32).astype(BF16)

    row = lambda w: pl.BlockSpec((tm, w), lambda i: (i, 0))
    full = lambda a: pl.BlockSpec(a.shape, lambda i: (0, 0))
    wa_w, wb_w = oa.shape[1], ob.shape[1]
    outs = pl.pallas_call(
        body,
        name="merge_bwd",
        grid=(ni,),
        in_specs=[row(d), row(wa_w), row(wb_w), row(d), row(d), full(wa), full(wb), full(wo)] + [HBM_SPEC] * n_x,
        out_specs=[row(d), row(d), row(wa_w), row(wb_w), row(2 * d), row(d)] + [HBM_SPEC] * n_x,
        out_shape=[
            jax.ShapeDtypeStruct((t, d), BF16), jax.ShapeDtypeStruct((t, d), BF16),
            jax.ShapeDtypeStruct((t, wa_w), BF16), jax.ShapeDtypeStruct((t, wb_w), BF16),
            jax.ShapeDtypeStruct((t, 2 * d), BF16), jax.ShapeDtypeStruct((t, d), BF16),
        ] + (exchange.out_shape if exchange else []),
        scratch_shapes=exchange.scratch if exchange else [],
        compiler_params=_cparams(("arbitrary",) if exchange else ("parallel",)),
    )(dh, oa, ob, ga, gb, wa, wb, wo, *(exchange.ins if exchange else []))
    return outs[:6], outs[6:]


def _tri_dot(tri, x):
    hi = x.astype(BF16)
    r1 = x - hi.astype(F32)
    mid = r1.astype(BF16)
    lo = (r1 - mid.astype(F32)).astype(BF16)
    return (jnp.dot(tri, hi, preferred_element_type=F32)
            + jnp.dot(tri, mid, preferred_element_type=F32)
            + jnp.dot(tri, lo, preferred_element_type=F32))


def _forget_cumsum(f_logit, b_pad, nb):
    t, w = f_logit.shape
    bsz = t // (nb * BLOCK)

    def body(f_ref, b_ref, c_ref, carry):
        n = pl.program_id(1)

        @pl.when(n == 0)
        def _():
            carry[...] = jnp.zeros_like(carry)

        x = jax.nn.log_sigmoid(f_ref[...] + b_ref[...])
        rows = lax.broadcasted_iota(jnp.int32, (BLOCK, BLOCK), 0)
        cols = lax.broadcasted_iota(jnp.int32, (BLOCK, BLOCK), 1)
        tri = (cols <= rows).astype(BF16)
        c = _tri_dot(tri, x) + carry[...]
        c_ref[...] = c
        carry[...] = c[BLOCK - 1:BLOCK, :]

    return pl.pallas_call(
        body,
        name="forget_cumsum",
        grid=(bsz, nb),
        in_specs=[pl.BlockSpec((BLOCK, w), lambda b, n: (b * nb + n, 0)),
                  pl.BlockSpec((1, w), lambda b, n: (0, 0))],
        out_specs=pl.BlockSpec((BLOCK, w), lambda b, n: (b * nb + n, 0)),
        out_shape=jax.ShapeDtypeStruct((t, w), F32),
        scratch_shapes=[pltpu.VMEM((1, w), F32)],
        compiler_params=_cparams(("parallel", "arbitrary")),
    )(f_logit, b_pad)


def _forget_cumsum_bwd(dc, f_logit, b_pad, nb):
    t, w = f_logit.shape
    bsz = t // (nb * BLOCK)

    def body(dc_ref, f_ref, b_ref, df_ref, db_ref, carry):
        n = pl.program_id(1)

        @pl.when(n == 0)
        def _():
            carry[...] = jnp.zeros_like(carry)
            db_ref[...] = jnp.zeros_like(db_ref)

        rows = lax.broadcasted_iota(jnp.int32, (BLOCK, BLOCK), 0)
        cols = lax.broadcasted_iota(jnp.int32, (BLOCK, BLOCK), 1)
        tri = (cols >= rows).astype(BF16)
        dlf = _tri_dot(tri, dc_ref[...]) + carry[...]
        carry[...] = dlf[0:1, :]
        df = dlf * jax.nn.sigmoid(-(f_ref[...] + b_ref[...]))
        df_ref[...] = df.astype(BF16)
        db_ref[0] += jnp.sum(df, axis=0, keepdims=True)

    rev = lambda b, n: (b * nb + (nb - 1 - n), 0)
    return pl.pallas_call(
        body,
        name="forget_cumsum_bwd",
        grid=(bsz, nb),
        in_specs=[pl.BlockSpec((BLOCK, w), rev),
                  pl.BlockSpec((BLOCK, w), rev),
                  pl.BlockSpec((1, w), lambda b, n: (0, 0))],
        out_specs=[pl.BlockSpec((BLOCK, w), rev),
                   pl.BlockSpec((1, 1, w), lambda b, n: (b, 0, 0))],
        out_shape=[jax.ShapeDtypeStruct((t, w), BF16), jax.ShapeDtypeStruct((bsz, 1, w), F32)],
        scratch_shapes=[pltpu.VMEM((1, w), F32)],
        compiler_params=_cparams(("parallel", "arbitrary")),
    )(dc, f_logit, b_pad)


GROUP_ROWS = A_GROUP * BLOCK


def _stack_heads(ref, g):
    return jnp.concatenate([ref[:, (A_GROUP * g + i) * LANES:(A_GROUP * g + i + 1) * LANES] for i in range(A_GROUP)],
                           axis=0)


def _unstack_heads(ref, g, x):
    for i in range(A_GROUP):
        ref[:, (A_GROUP * g + i) * LANES:(A_GROUP * g + i + 1) * LANES] = x[i * BLOCK:(i + 1) * BLOCK].astype(ref.dtype)


def _swa_logits(qk, slope, n):
    qi = lax.broadcasted_iota(jnp.int32, (GROUP_ROWS, BLOCK), 0) & (BLOCK - 1)
    kj = lax.broadcasted_iota(jnp.int32, (GROUP_ROWS, BLOCK), 1)
    s_all = qk * SCALE
    out = []
    for i, (dist, ok) in enumerate((
            (n * BLOCK + qi - kj, (kj >= N_PAD) & (n * BLOCK + qi - kj >= 0)),
            (BLOCK + qi - kj, (kj > qi) & (n >= 2)),
            (qi - kj, (kj <= qi) & (n >= 1)))):
        s = s_all[:, i * BLOCK:(i + 1) * BLOCK] - slope * dist.astype(F32)
        out.append(jnp.where(ok, s, NEG))
    return out


def _three_blocks(m_ref, p_ref, c_ref):
    return jnp.concatenate([m_ref[...], p_ref[...], c_ref[...]], axis=0)


def _swa_specs(bsz, nb):
    qspec = pl.BlockSpec((bsz, BLOCK, A_PAD_WIDTH), lambda n: (0, n, 0))
    kv_m = pl.BlockSpec((bsz, BLOCK, LANES), lambda n: (0, 0, 0))
    kv_p = pl.BlockSpec((bsz, BLOCK, LANES), lambda n: (0, jnp.maximum(n - 1, 0), 0))
    kv_c = pl.BlockSpec((bsz, BLOCK, LANES), lambda n: (0, n, 0))
    rowspec = pl.BlockSpec((A_KV_HEADS, GROUP_ROWS, 1), lambda n: (0, 0, 0))
    lsespec = pl.BlockSpec((bsz, 1, A_KV_HEADS, GROUP_ROWS, 1), lambda n: (0, n, 0, 0, 0))
    return qspec, kv_m, kv_p, kv_c, rowspec, lsespec


def _swa_fwd(q, k, v, sink_rows, slope_rows, nb):
    t = q.shape[0]
    l = nb * BLOCK
    bsz = t // l

    def body(q_ref, km_ref, kp_ref, kc_ref, vm_ref, vp_ref, vc_ref, sink_ref, slope_ref, o_ref, lse_ref):
        n = pl.program_id(0)
        lane_group = lax.broadcasted_iota(jnp.int32, (GROUP_ROWS, LANES), 1) // HEAD_DIM
        products = {}
        for b in range(bsz):
            keys = _three_blocks(km_ref.at[b], kp_ref.at[b], kc_ref.at[b])
            for g in range(A_KV_HEADS):
                products[b, g] = lax.dot_general(_stack_heads(q_ref.at[b], g), keys, NT_DIMS,
                                                 preferred_element_type=F32)
        for b in range(bsz):
            values = _three_blocks(vm_ref.at[b], vp_ref.at[b], vc_ref.at[b])
            for g in range(A_KV_HEADS):
                sink = sink_ref[g]
                s_m, s_p, s_c = _swa_logits(products[b, g], slope_ref[g], n)
                m = jnp.maximum(jnp.max(jnp.maximum(jnp.maximum(s_m, s_p), s_c), axis=-1, keepdims=True), sink)
                m_wide = jnp.broadcast_to(m, (GROUP_ROWS, BLOCK))
                e_m = jnp.exp(s_m - m_wide)
                e_p = jnp.exp(s_p - m_wide)
                e_c = jnp.exp(s_c - m_wide)
                z = jnp.sum((e_m + e_p) + e_c, axis=-1, keepdims=True) + jnp.exp(sink - m)
                inv = jnp.broadcast_to(1.0 / z, (GROUP_ROWS, BLOCK))
                probs = jnp.concatenate([(e_m * inv).astype(BF16), (e_p * inv).astype(BF16),
                                         (e_c * inv).astype(BF16)], axis=1)
                o = jnp.dot(probs, values, preferred_element_type=F32)
                _unstack_heads(o_ref.at[b], g, jnp.where(lane_group == g, o, 0.0))
                lse_ref[b, 0, g] = m + jnp.log(z)

    qspec, kv_m, kv_p, kv_c, rowspec, lsespec = _swa_specs(bsz, nb)
    by_example = lambda a: a.reshape(bsz, l, a.shape[1])
    q3, k3, v3 = by_example(q), by_example(k), by_example(v)
    o, lse = pl.pallas_call(
        body,
        name="swa_fwd",
        grid=(nb,),
        in_specs=[qspec, kv_m, kv_p, kv_c, kv_m, kv_p, kv_c, rowspec, rowspec],
        out_specs=[qspec, lsespec],
        out_shape=[jax.ShapeDtypeStruct((bsz, l, A_PAD_WIDTH), BF16),
                   jax.ShapeDtypeStruct((bsz, nb, A_KV_HEADS, GROUP_ROWS, 1), F32)],
        compiler_params=_cparams(("arbitrary",)),
    )(q3, k3, k3, k3, v3, v3, v3, sink_rows, slope_rows)
    return o.reshape(t, A_PAD_WIDTH), lse


def _swa_bwd(q, k, v, do, lse, sink_rows, slope_rows, nb):
    t = q.shape[0]
    l = nb * BLOCK
    bsz = t // l

    def body(q_ref, km_ref, kp_ref, kc_ref, vm_ref, vp_ref, vc_ref, do_ref, lse_ref, sink_ref, slope_ref,
             dq_ref, dk_ref, dv_ref, dsink_ref, dk_acc, dv_acc):
        n = pl.program_id(0)

        @pl.when(n == 0)
        def _():
            dk_acc[...] = jnp.zeros_like(dk_acc)
            dv_acc[...] = jnp.zeros_like(dv_acc)
            dsink_ref[...] = jnp.zeros_like(dsink_ref)

        first_half = lax.broadcasted_iota(jnp.int32, (BLOCK, LANES), 1) < HEAD_DIM
        prev = jnp.maximum(n - 1, 0)
        products = {}
        for b in range(bsz):
            keys = _three_blocks(km_ref.at[b], kp_ref.at[b], kc_ref.at[b])
            values = _three_blocks(vm_ref.at[b], vp_ref.at[b], vc_ref.at[b])
            for g in range(A_KV_HEADS):
                products[b, g] = (
                    lax.dot_general(_stack_heads(q_ref.at[b], g), keys, NT_DIMS, preferred_element_type=F32),
                    lax.dot_general(_stack_heads(do_ref.at[b], g), values, NT_DIMS, preferred_element_type=F32))
        for b in range(bsz):
            keys = _three_blocks(km_ref.at[b], kp_ref.at[b], kc_ref.at[b])
            for g in range(A_KV_HEADS):
                qq = _stack_heads(q_ref.at[b], g)
                dob = _stack_heads(do_ref.at[b], g)
                lse_g = lse_ref[b, 0, g]
                lse_wide = jnp.broadcast_to(lse_g, (GROUP_ROWS, BLOCK))
                qk, dp_all = products[b, g]
                probs = [jnp.exp(s - lse_wide) for s in _swa_logits(qk, slope_ref[g], n)]
                dps = [dp_all[:, i * BLOCK:(i + 1) * BLOCK] for i in range(3)]
                delta = jnp.sum((probs[0] * dps[0] + probs[1] * dps[1]) + probs[2] * dps[2], axis=-1, keepdims=True)
                delta_wide = jnp.broadcast_to(delta, (GROUP_ROWS, BLOCK))
                ds = jnp.concatenate([(p * (dp - delta_wide)).astype(BF16) for p, dp in zip(probs, dps)], axis=1)
                pb = jnp.concatenate([p.astype(BF16) for p in probs], axis=1)
                dq = jnp.dot(ds, keys, preferred_element_type=F32) * SCALE
                dk_all = lax.dot_general(ds, qq, TN_DIMS, preferred_element_type=F32) * SCALE
                dv_all = lax.dot_general(pb, dob, TN_DIMS, preferred_element_type=F32)
                for i, start in enumerate((0, prev * BLOCK, n * BLOCK)):
                    rows = pl.ds(pl.multiple_of(start, BLOCK), BLOCK)
                    dk_acc[b, rows, :] += dk_all[i * BLOCK:(i + 1) * BLOCK]
                    dv_acc[b, rows, :] += dv_all[i * BLOCK:(i + 1) * BLOCK]
                for pair in range(A_GROUP // 2):
                    even = dq[2 * pair * BLOCK:(2 * pair + 1) * BLOCK]
                    odd = dq[(2 * pair + 1) * BLOCK:(2 * pair + 2) * BLOCK]
                    left = even if g == 0 else pltpu.roll(even, HEAD_DIM, 1)
                    right = pltpu.roll(odd, HEAD_DIM, 1) if g == 0 else odd
                    tile = (A_GROUP // 2) * g + pair
                    dq_ref[b, :, tile * LANES:(tile + 1) * LANES] = jnp.where(first_half, left, right).astype(BF16)
                dsink_ref[b, g] += -(jnp.exp(sink_ref[g] - lse_g) * delta)

        @pl.when(n == nb - 1)
        def _():
            dk_ref[...] = dk_acc[...].astype(BF16)
            dv_ref[...] = dv_acc[...].astype(BF16)

    qspec, kv_m, kv_p, kv_c, rowspec, lsespec = _swa_specs(bsz, nb)
    kv_all = pl.BlockSpec((bsz, l, LANES), lambda n: (0, 0, 0))
    by_example = lambda a: a.reshape(bsz, l, a.shape[1])
    q3, k3, v3 = by_example(q), by_example(k), by_example(v)
    dq, dk, dv, dsink = pl.pallas_call(
        body,
        name="swa_bwd",
        grid=(nb,),
        in_specs=[qspec, kv_m, kv_p, kv_c, kv_m, kv_p, kv_c, qspec, lsespec, rowspec, rowspec],
        out_specs=[pl.BlockSpec((bsz, BLOCK, A_WIDTH), lambda n: (0, n, 0)), kv_all, kv_all,
                   pl.BlockSpec((bsz, A_KV_HEADS, GROUP_ROWS, 1), lambda n: (0, 0, 0, 0))],
        out_shape=[jax.ShapeDtypeStruct((bsz, l, A_WIDTH), BF16),
                   jax.ShapeDtypeStruct((bsz, l, LANES), BF16),
                   jax.ShapeDtypeStruct((bsz, l, LANES), BF16),
                   jax.ShapeDtypeStruct((bsz, A_KV_HEADS, GROUP_ROWS, 1), F32)],
        scratch_shapes=[pltpu.VMEM((bsz, l, LANES), F32), pltpu.VMEM((bsz, l, LANES), F32)],
        compiler_params=_cparams(("arbitrary",)),
    )(q3, k3, k3, k3, v3, v3, v3, by_example(do), lse, sink_rows, slope_rows)
    return dq.reshape(t, A_WIDTH), dk.reshape(t, LANES), dv.reshape(t, LANES), dsink


CHUNK = KEY_BLOCKS * BLOCK


def _fox_chunk(qb, ci):
    sb = jnp.maximum(jnp.minimum(KEY_BLOCKS * ci, qb + 1 - KEY_BLOCKS), 0)
    lo = jnp.maximum(ci * CHUNK, N_PAD)
    return sb, lo, pl.ds(pl.multiple_of(sb * BLOCK, BLOCK), CHUNK)


def _fox_logits(s_ref, cr_ref, e, j, sb, lo, qb):
    lane = lax.broadcasted_iota(jnp.int32, (BLOCK, BLOCK), 1)
    ahead = lane - lax.broadcasted_iota(jnp.int32, (BLOCK, BLOCK), 0)
    first = (sb + j) * BLOCK
    s = s_ref[e, :, j * BLOCK:(j + 1) * BLOCK] - cr_ref[e, sb + j]
    return jnp.where((ahead <= qb * BLOCK - first) & (lane >= lo - first), s, NEG)


FOX_PAIRS = 4
FOX_HEADS = 2 * FOX_PAIRS
FOX_STEPS = B_PAIRS // FOX_PAIRS


def _fox_specs(nb):
    l = nb * BLOCK
    q_spec = pl.BlockSpec((BLOCK, FOX_PAIRS * LANES), lambda b, p, i: (b * nb + i, p))
    kv_spec = pl.BlockSpec((l, FOX_HEADS * LANES), lambda b, p, i: (b, p))
    cc_spec = pl.BlockSpec((FOX_HEADS, BLOCK, 1), lambda b, p, i: (b * FOX_STEPS + p, i, 0))
    cr_spec = pl.BlockSpec((FOX_HEADS, nb, 1, BLOCK), lambda b, p, i: (b * FOX_STEPS + p, 0, 0, 0))
    return q_spec, kv_spec, cc_spec, cr_spec


def _fox_fwd(q, k, v, c_row, nb, exchange=None):
    t = q.shape[0]
    bsz = t // (nb * BLOCK)
    assert nb >= KEY_BLOCKS

    n_x = len(exchange.ins) if exchange else 0

    def body(*refs):
        q_ref, k_ref, v_ref, cr_ref = refs[:4]
        o_ref, ox_ref, lse_ref = refs[4 + n_x:7 + n_x]
        s_scr, hi_scr, lo_scr = refs[7 + 2 * n_x:10 + 2 * n_x]
        qb = pl.program_id(2)
        if exchange:
            first = (pl.program_id(0) == 0) & (pl.program_id(1) == 0)
            last = (pl.program_id(0) == bsz - 1) & (pl.program_id(1) == FOX_STEPS - 1)
            _host_exchange(exchange, refs[4:4 + n_x], refs[7 + n_x:7 + 2 * n_x], refs[10 + 2 * n_x:],
                           first & (qb == 0), first & (qb == 2 * nb // 3), last & (qb == nb - 1),
                           late=last & (qb == 0))
        qs = [q_ref[:, a * LANES:(a + 1) * LANES] * SCALE for a in range(FOX_PAIRS)]
        first_half = lax.broadcasted_iota(jnp.int32, (BLOCK, LANES), 1) < HEAD_DIM

        def step(ci, carry):
            stats, accs = carry[:2 * FOX_HEADS], carry[2 * FOX_HEADS:]
            sb, lo, krows = _fox_chunk(qb, ci)
            for e in range(FOX_HEADS):
                s_scr[e] = lax.dot_general(qs[e // 2], k_ref[krows, e * LANES:(e + 1) * LANES], NT_DIMS,
                                           preferred_element_type=F32)
            new_stats, new_accs = [], []
            for a in range(FOX_PAIRS):
                alphas = []
                pv = jnp.zeros((BLOCK, LANES), F32)
                pv_lo = jnp.zeros((BLOCK, LANES), F32)
                for e in (2 * a, 2 * a + 1):
                    m, z = stats[2 * e], stats[2 * e + 1]
                    tile = slice(e * LANES, (e + 1) * LANES)
                    top = None
                    for j in range(KEY_BLOCKS):
                        s = _fox_logits(s_scr, cr_ref, e, j, sb, lo, qb)
                        s_scr[e, :, j * BLOCK:(j + 1) * BLOCK] = s
                        top = s if top is None else jnp.maximum(top, s)
                    m_new = jnp.maximum(m, jnp.max(top, axis=-1, keepdims=True))
                    alpha = jnp.exp(m - m_new)
                    m_wide = jnp.broadcast_to(m_new, (BLOCK, BLOCK))
                    total = None
                    for j in range(KEY_BLOCKS):
                        cols = slice(j * BLOCK, (j + 1) * BLOCK)
                        p = jnp.exp(s_scr[e, :, cols] - m_wide)
                        total = p if total is None else total + p
                        hi = p.astype(BF16)
                        hi_scr[e, :, cols] = hi
                        lo_scr[e, :, cols] = (p - hi.astype(F32)).astype(BF16)
                    z = alpha * z + jnp.sum(total, axis=-1, keepdims=True)
                    vv = v_ref[krows, tile]
                    pv = pv + jnp.dot(hi_scr[e], vv, preferred_element_type=F32)
                    pv_lo = pv_lo + jnp.dot(lo_scr[e], vv, preferred_element_type=F32)
                    new_stats += [m_new, z]
                    alphas.append(alpha)
                alpha = jnp.where(first_half, alphas[0], alphas[1])
                new_accs += [alpha * accs[2 * a] + pv, alpha * accs[2 * a + 1] + pv_lo]
            return (*new_stats, *new_accs)

        col = lambda val: jnp.full((BLOCK, 1), val, F32)
        done = lax.fori_loop(
            0, (qb + KEY_BLOCKS) // KEY_BLOCKS, step,
            (col(NEG), col(0.0)) * FOX_HEADS + (jnp.zeros((BLOCK, LANES), F32),) * (2 * FOX_PAIRS))
        for a in range(FOX_PAIRS):
            m0, z0, m1, z1 = done[4 * a:4 * a + 4]
            acc, acc_lo = done[2 * FOX_HEADS + 2 * a:2 * FOX_HEADS + 2 * a + 2]
            inv = 1.0 / jnp.where(first_half, z0, z1)
            tile = slice(a * LANES, (a + 1) * LANES)
            o_ref[:, tile] = (acc * inv).astype(BF16)
            ox_ref[:, tile] = (acc + acc_lo) * inv
            lse_ref[2 * a] = m0 + jnp.log(z0)
            lse_ref[2 * a + 1] = m1 + jnp.log(z1)

    q_spec, kv_spec, cc_spec, cr_spec = _fox_specs(nb)
    outs = pl.pallas_call(
        body,
        name="fox_fwd",
        grid=(bsz, FOX_STEPS, nb),
        in_specs=[q_spec, kv_spec, kv_spec, cr_spec] + [HBM_SPEC] * n_x,
        out_specs=[q_spec, q_spec, cc_spec] + [HBM_SPEC] * n_x,
        out_shape=[jax.ShapeDtypeStruct((t, B_WIDTH), BF16), jax.ShapeDtypeStruct((t, B_WIDTH), F32),
                   jax.ShapeDtypeStruct((bsz * B_HEADS, nb * BLOCK, 1), F32)] + (exchange.out_shape if exchange else []),
        scratch_shapes=[pltpu.VMEM((FOX_HEADS, BLOCK, CHUNK), F32), pltpu.VMEM((FOX_HEADS, BLOCK, CHUNK), BF16),
                        pltpu.VMEM((FOX_HEADS, BLOCK, CHUNK), BF16)] + (exchange.scratch if exchange else []),
        compiler_params=_cparams(("arbitrary",) * 3 if exchange else ("parallel", "parallel", "arbitrary")),
    )(q, k, v, c_row, *(exchange.ins if exchange else []))
    return outs[:3], outs[3:]


def _fox_bwd(q, k, v, o_exact, do, lse, c_row, nb, exchange=None):
    t = q.shape[0]
    l = nb * BLOCK
    bsz = t // l

    n_x = len(exchange.ins) if exchange else 0

    def body(*refs):
        q_ref, k_ref, v_ref, ox_ref, do_ref, lse_ref, cr_ref = refs[:7]
        dq_ref, dk_ref, dv_ref, dc_ref = refs[7 + n_x:11 + n_x]
        dk_acc, dv_acc, s_scr, dp_scr, p_scr, ds_scr, dq_scr = refs[11 + 2 * n_x:18 + 2 * n_x]
        qb = pl.program_id(2)
        if exchange:
            first = (pl.program_id(0) == 0) & (pl.program_id(1) == 0)
            last = (pl.program_id(0) == bsz - 1) & (pl.program_id(1) == FOX_STEPS - 1)
            _host_exchange(exchange, refs[7:7 + n_x], refs[11 + n_x:11 + 2 * n_x], refs[18 + 2 * n_x:],
                           first & (qb == 0), last & (qb == 0), last & (qb == nb - 1))

        @pl.when(qb == 0)
        def _():
            dk_acc[...] = jnp.zeros_like(dk_acc)
            dv_acc[...] = jnp.zeros_like(dv_acc)
            dc_ref[...] = jnp.zeros_like(dc_ref)

        top_half = lax.broadcasted_iota(jnp.int32, (LANES, BLOCK), 0) < HEAD_DIM
        pair_t = lambda x: jnp.concatenate([jnp.where(top_half, x.T, 0), jnp.where(top_half, 0, x.T)], axis=1)
        first_half = lax.broadcasted_iota(jnp.int32, (BLOCK, LANES), 1) < HEAD_DIM
        wide = lambda col: jnp.broadcast_to(col, (BLOCK, BLOCK))
        qs, dobs, qs_t, dob_t, deltas = [], [], [], [], []
        for a in range(FOX_PAIRS):
            tile = slice(a * LANES, (a + 1) * LANES)
            qs.append(q_ref[:, tile] * SCALE)
            dobs.append(do_ref[:, tile])
            qs_t.append(pair_t(qs[a]))
            dob_t.append(pair_t(dobs[a]))
            weighted = dobs[a].astype(F32) * ox_ref[:, tile]
            deltas += [wide(jnp.sum(jnp.where(first_half, weighted, 0.0), axis=-1, keepdims=True)),
                       wide(jnp.sum(jnp.where(first_half, 0.0, weighted), axis=-1, keepdims=True))]
        lses = [wide(lse_ref[e]) for e in range(FOX_HEADS)]

        dq_scr[...] = jnp.zeros(dq_scr.shape, F32)

        def step(ci, carry):
            sb, lo, krows = _fox_chunk(qb, ci)
            for e in range(FOX_HEADS):
                tile = slice(e * LANES, (e + 1) * LANES)
                s_scr[e] = lax.dot_general(qs[e // 2], k_ref[krows, tile], NT_DIMS, preferred_element_type=F32)
                dp_scr[e] = lax.dot_general(dobs[e // 2], v_ref[krows, tile], NT_DIMS, preferred_element_type=F32)
            for a in range(FOX_PAIRS):
                for e in (2 * a, 2 * a + 1):
                    tile = slice(e * LANES, (e + 1) * LANES)
                    kk = k_ref[krows, tile]
                    for j in range(KEY_BLOCKS):
                        cols = slice(j * BLOCK, (j + 1) * BLOCK)
                        p = jnp.exp(_fox_logits(s_scr, cr_ref, e, j, sb, lo, qb) - lses[e])
                        ds = p * (dp_scr[e, :, cols] - deltas[e])
                        dc_ref[e, sb + j] -= jnp.sum(ds, axis=0, keepdims=True)
                        p_scr[e, :, cols] = p.astype(BF16)
                        ds_scr[e, :, cols] = ds.astype(BF16)
                    dq_scr[a] += jnp.dot(ds_scr[e], kk, preferred_element_type=F32)
                both = slice(2 * a, 2 * a + 2)
                dk_t = jnp.dot(qs_t[a], ds_scr[both].reshape(2 * BLOCK, CHUNK), preferred_element_type=F32)
                dv_t = jnp.dot(dob_t[a], p_scr[both].reshape(2 * BLOCK, CHUNK), preferred_element_type=F32)
                for j in range(KEY_BLOCKS):
                    cols = slice(j * BLOCK, (j + 1) * BLOCK)
                    dk_acc[a * nb + sb + j] += dk_t[:, cols]
                    dv_acc[a * nb + sb + j] += dv_t[:, cols]
            return carry

        lax.fori_loop(0, (qb + KEY_BLOCKS) // KEY_BLOCKS, step, 0)
        for a in range(FOX_PAIRS):
            dq_ref[:, a * LANES:(a + 1) * LANES] = (dq_scr[a] * SCALE).astype(BF16)

        @pl.when(qb == nb - 1)
        def _():
            for a in range(FOX_PAIRS):
                for kb in range(nb):
                    rows = slice(kb * BLOCK, (kb + 1) * BLOCK)
                    for acc, out_ref in ((dk_acc, dk_ref), (dv_acc, dv_ref)):
                        out_ref[rows, a * LANES:(a + 1) * LANES] = acc[a * nb + kb].T.astype(BF16)

    q_spec, kv_spec, cc_spec, cr_spec = _fox_specs(nb)
    dkv_spec = pl.BlockSpec((l, FOX_PAIRS * LANES), lambda b, p, i: (b, p))
    outs = pl.pallas_call(
        body,
        name="fox_bwd",
        grid=(bsz, FOX_STEPS, nb),
        in_specs=[q_spec, kv_spec, kv_spec, q_spec, q_spec, cc_spec, cr_spec] + [HBM_SPEC] * n_x,
        out_specs=[q_spec, dkv_spec, dkv_spec, cr_spec] + [HBM_SPEC] * n_x,
        out_shape=[jax.ShapeDtypeStruct((t, B_WIDTH), BF16), jax.ShapeDtypeStruct((t, B_WIDTH), BF16),
                   jax.ShapeDtypeStruct((t, B_WIDTH), BF16),
                   jax.ShapeDtypeStruct((bsz * B_HEADS, nb, 1, BLOCK), F32)] + (exchange.out_shape if exchange else []),
        scratch_shapes=[pltpu.VMEM((FOX_PAIRS * nb, LANES, BLOCK), F32), pltpu.VMEM((FOX_PAIRS * nb, LANES, BLOCK), F32),
                        pltpu.VMEM((FOX_HEADS, BLOCK, CHUNK), F32), pltpu.VMEM((FOX_HEADS, BLOCK, CHUNK), F32),
                        pltpu.VMEM((FOX_HEADS, BLOCK, CHUNK), BF16), pltpu.VMEM((FOX_HEADS, BLOCK, CHUNK), BF16),
                        pltpu.VMEM((FOX_PAIRS, BLOCK, LANES), F32)]
        + (exchange.scratch if exchange else []),
        compiler_params=_cparams(("arbitrary",) * 3 if exchange else ("parallel", "parallel", "arbitrary")),
    )(q, k, v, o_exact, do, lse, c_row, *(exchange.ins if exchange else []))
    return outs[:4], outs[4:]


def _loss_head(h, final_w, target):
    bsz, l, d = h.shape
    nb = l // BLOCK

    def body(h_ref, w_ref, t_ref, loss_ref, dh_ref, dw_ref):
        n = pl.program_id(0)

        @pl.when(n == 0)
        def _():
            loss_ref[...] = jnp.zeros_like(loss_ref)
            dw_ref[...] = jnp.zeros_like(dw_ref)
            dh_ref[...] = jnp.zeros_like(dh_ref)

        @pl.when(n > 0)
        def _():
            w = w_ref[...]
            for b in range(bsz):
                hh = h_ref[b]
                r = _rms_scale(hh)
                err = (hh * r) * w - t_ref[b]
                loss_ref[...] += 0.5 * jnp.sum(jnp.mean(err * err, axis=-1, keepdims=True), axis=0, keepdims=True)
                dy = err * (1.0 / d)
                dh, dw = _rms_bwd(dy, hh, w)
                dh_ref[b] = dh
                dw_ref[...] += dw

    return pl.pallas_call(
        body,
        name="loss_head",
        grid=(nb,),
        in_specs=[
            pl.BlockSpec((bsz, BLOCK, d), lambda n: (0, n, 0)),
            pl.BlockSpec((1, d), lambda n: (0, 0)),
            pl.BlockSpec((bsz, BLOCK, d), lambda n: (0, jnp.maximum(n - 1, 0), 0)),
        ],
        out_specs=[
            pl.BlockSpec((1, 128), lambda n: (0, 0)),
            pl.BlockSpec((bsz, BLOCK, d), lambda n: (0, n, 0)),
            pl.BlockSpec((1, d), lambda n: (0, 0)),
        ],
        out_shape=[jax.ShapeDtypeStruct((1, 128), F32), jax.ShapeDtypeStruct((bsz, l, d), F32),
                   jax.ShapeDtypeStruct((1, d), F32)],
        compiler_params=_cparams(("arbitrary",)),
    )(h, final_w, target)


def _pad_tiles(w, src, heads, lane_slot, axis):
    pieces = []
    for h in range(heads):
        x = lax.slice_in_dim(w, src + HEAD_DIM * h, src + HEAD_DIM * (h + 1), axis=axis)
        z = jnp.zeros_like(x)
        pieces += [x, z] if lane_slot(h) == 0 else [z, x]
    return pieces


def _unpad_tiles(g, off, heads, lane_slot, axis):
    return [lax.slice_in_dim(g, off + LANES * h + HEAD_DIM * lane_slot(h),
                             off + LANES * h + HEAD_DIM * (lane_slot(h) + 1), axis=axis) for h in range(heads)]


def _layout_w_in(w):
    pad_f = jnp.zeros((w.shape[0], F_COLS - B_HEADS), w.dtype)
    return jnp.concatenate([w[:, :SRC_QB], w[:, SRC_F:SRC_GA], pad_f, w[:, SRC_QB:SRC_F], w[:, SRC_GA:]], axis=1)


def _unlayout_w_in(g_a, g_b, g_gates):
    return jnp.concatenate([g_a[:, :OFF_F], g_b, g_a[:, OFF_F:OFF_F + B_HEADS], g_gates], axis=1)


def _local_step(x, target, meta, norms, b_forget, sinks, w, comm=None):
    n1, nmix, n2, nfin = norms
    w1i, w1o = w[:2]
    bsz, seq, d = x.shape
    l = PREFIX + seq
    nb = l // BLOCK
    t = bsz * l

    h0 = jnp.concatenate([jnp.zeros((bsz, N_PAD, d), F32),
                          jnp.broadcast_to(meta[None], (bsz, N_META, d)), x], axis=1).reshape(t, d)

    if comm is None:
        (h1, g1, u1), _ = _ffn_fwd(h0, n1, w1i, w1o)
        w_in, wa, wb, wo, w2i, w2o = w[2:]
    else:
        (h1, g1, u1), gathered = _ffn_fwd(h0, n1, w1i, w1o, comm.gather(GATHER_PROJ))
        w_in, = comm.gathered(GATHER_PROJ, gathered)
    wp = _layout_w_in(w_in)
    un, qa, ka, va, qb, kb, vb, ga, gb, f_logit = _proj_fwd(h1, nmix, wp)
    b_pad = jnp.concatenate([b_forget, jnp.zeros((1, F_COLS - B_HEADS), F32)], axis=1)
    c = _forget_cumsum(f_logit, b_pad, nb)
    c_heads = c[:, :B_HEADS].reshape(bsz, l, B_HEADS).transpose(0, 2, 1).reshape(bsz * B_HEADS, l)
    c_row = c_heads.reshape(bsz * B_HEADS, nb, 1, BLOCK)

    slopes = jnp.exp2(-8.0 * jnp.arange(1, A_HEADS + 1, dtype=F32) / A_HEADS)
    slope_rows = jnp.repeat(slopes.reshape(A_KV_HEADS, A_GROUP), BLOCK, axis=1)[:, :, None]
    sink_rows = jnp.repeat(sinks.reshape(A_KV_HEADS, A_GROUP), BLOCK, axis=1)[:, :, None]

    oa, lse_a = _swa_fwd(qa, ka, va, sink_rows, slope_rows, nb)
    if comm is None:
        (ob, ob_exact, lse_b), _ = _fox_fwd(qb, kb, vb, c_row, nb)
    else:
        (ob, ob_exact, lse_b), gathered = _fox_fwd(qb, kb, vb, c_row, nb, comm.gather(GATHER_LATE))
        wa, wb, wo, w2i, w2o = comm.gathered(GATHER_LATE, gathered)
    wa_p = jnp.concatenate(_pad_tiles(wa, 0, A_HEADS, A_SLOT, 0), axis=0)
    h2, mixed = _merge_fwd(h1, oa, ob, ga, gb, wa_p, wb, wo)
    (h3, g2, u2), _ = _ffn_fwd(h2, n2, w2i, w2o)
    loss, dh3, d_nfin = _loss_head(h3.reshape(bsz, l, d), nfin, target)

    (dh2, n2b, a2, dgu2, df2, dn2_parts), _ = _ffn_bwd(dh3.reshape(t, d), h2, n2, g2, u2, w2i, w2o)
    g_w2o = _tn_matmul(a2, df2, "grad_ffn2_w_out")
    g_w2i = _tn_matmul(n2b, dgu2, "grad_ffn2_w_in")

    hosted = comm.swap("ffn2", dict(ffn2_w_in=g_w2i, ffn2_w_out=g_w2o)) if comm else None
    (dya, dyb, doa, dob, dgates, dh2b), swapped = _merge_bwd(dh2, oa, ob, ga, gb, wa_p, wb, wo, hosted)
    g_wo = _tn_matmul(mixed, dh2b, "grad_w_out")
    g_wa = jnp.concatenate(_unpad_tiles(_tn_matmul(oa, dya, "grad_w_branch_a"), 0, A_HEADS, A_SLOT, 0), axis=0)
    g_wb = _tn_matmul(ob, dyb, "grad_w_branch_b")

    dqa, dka, dva, dsink_rows = _swa_bwd(qa, ka, va, doa, lse_a, sink_rows, slope_rows, nb)
    hosted = comm.scatter("ffn2", swapped) if comm else None
    (dqb, dkb, dvb, dc_row), pieces = _fox_bwd(qb, kb, vb, ob_exact, dob, lse_b, c_row, nb, hosted)
    if comm:
        comm.received("ffn2", pieces)
    dc = dc_row.reshape(bsz, B_HEADS, l).transpose(0, 2, 1).reshape(t, B_HEADS)
    dc = jnp.concatenate([dc, jnp.zeros((t, F_COLS - B_HEADS), F32)], axis=1)
    df_logit, db_parts = _forget_cumsum_bwd(dc, f_logit, b_pad, nb)

    dproj = (jnp.concatenate([dqa, dka, dva, df_logit], axis=1), jnp.concatenate([dqb, dkb, dvb], axis=1), dgates)
    g_win = _unlayout_w_in(*[_tn_matmul(un, piece, "grad_w_in_" + tag) for piece, tag in zip(dproj, ("a", "b", "gates"))])
    hosted = comm.swap("mixer", dict(w_in=g_win, w_branch_a=g_wa, w_branch_b=g_wb, w_out=g_wo)) if comm else None
    (dh1, dnmix_parts), swapped = _proj_bwd(dh2, h1, nmix, dproj, wp, hosted)
    hosted = comm.scatter("mixer", swapped) if comm else None
    (dh0, n1b, a1, dgu1, df1, dn1_parts), pieces = _ffn_bwd(dh1, h0, n1, g1, u1, w1i, w1o, hosted)
    dh0 = dh0.reshape(bsz, l, d)
    grad_x = dh0[:, PREFIX:]
    small = dict(
        meta_tokens=jnp.sum(dh0[:, N_PAD:PREFIX], axis=0),
        ffn1_norm=jnp.sum(dn1_parts, axis=0),
        mix_norm=jnp.sum(dnmix_parts, axis=0),
        ffn2_norm=jnp.sum(dn2_parts, axis=0),
        final_norm=d_nfin,
        b_forget=jnp.sum(db_parts, axis=0)[:, :B_HEADS],
        attn_sinks=jnp.sum(dsink_rows.reshape(bsz, A_HEADS, BLOCK), axis=(0, 2)).reshape(1, A_HEADS),
    )
    if comm is None:
        g_w1o = _tn_matmul(a1, df1, "grad_ffn1_w_out")
        g_w1i = _tn_matmul(n1b, dgu1, "grad_ffn1_w_in")
    else:
        comm.received("mixer", pieces)
        g_w1o, gathered = _tn_matmul(a1, df1, "grad_ffn1_w_out", comm.small_gather(loss, small))
        comm.small_gathered(gathered)
        swapped = _run_exchange(comm.swap("ffn1_out", dict(ffn1_w_out=g_w1o)), "exchange_halves_ffn1_out")
        g_w1i, pieces = _tn_matmul(n1b, dgu1, "grad_ffn1_w_in", comm.scatter("ffn1_out", swapped))
        comm.received("ffn1_out", pieces)
    big = dict(ffn1_w_in=g_w1i, ffn1_w_out=g_w1o, w_in=g_win, w_branch_a=g_wa, w_branch_b=g_wb,
               w_out=g_wo, ffn2_w_in=g_w2i, ffn2_w_out=g_w2o)
    return loss, grad_x, small, big


BIG = (
    ("ffn1_w_in", (D_MODEL, 5632), 1),
    ("ffn1_w_out", (2816, D_MODEL), 0),
    ("w_in", (D_MODEL, W_IN_COLS), 1),
    ("w_branch_a", (A_WIDTH, D_MODEL), 1),
    ("w_branch_b", (B_WIDTH, D_MODEL), 1),
    ("w_out", (D_MODEL, D_MODEL), 0),
    ("ffn2_w_in", (D_MODEL, 5632), 1),
    ("ffn2_w_out", (2816, D_MODEL), 0),
)
STACKED = "w_in"


def _coords():
    return lax.axis_index("x"), lax.axis_index("y"), lax.axis_index("c")


def _other_chips(x, y):
    return ((1 - x, y), (x, 1 - y), (1 - x, 1 - y))


def _chip_part(ref, name, shape, axis, k):
    if name == STACKED:
        return ref.at[k]
    size = shape[axis] // N_CHIPS
    start = pl.multiple_of(k * size, size)
    return ref.at[pl.ds(start, size), :] if axis == 0 else ref.at[:, pl.ds(start, size)]


def _full_shape(name, shape):
    return (N_CHIPS, shape[0], shape[1] // N_CHIPS) if name == STACKED else shape


class _Exchange:
    def __init__(self, ins, out_shape, n_sems, ops):
        self.ins, self.out_shape, self.n_sems, self.ops = list(ins), list(out_shape), n_sems, ops

    @property
    def scratch(self):
        return [pltpu.SemaphoreType.DMA((self.n_sems,)), pltpu.SemaphoreType.DMA((self.n_sems,))]


SEMS_PER_GATHER = 9


def _gather_exchange(shards, table):
    n = len(table)
    x_nbr, y_nbr, diagonal = 0, 1, 2

    def ops(ins, outs, send_sems, recv_sems):
        x, y, c = _coords()
        mine = 2 * x + y
        sibling = (x, y, 1 - c)
        chips = _other_chips(x, y)
        slots = [2 * chip[0] + chip[1] for chip in chips]

        def part(i, k):
            name, shape, axis = table[i][:3]
            return _chip_part(outs[i], name, shape, axis, k)

        def half(ref, h):
            rows = ref.shape[0] // 2
            return ref.at[pl.ds(pl.multiple_of(h * rows, rows), rows), :]

        def remote(i, sem, src, dst, device):
            sem = SEMS_PER_GATHER * i + sem
            return pltpu.make_async_remote_copy(src, dst, send_sems.at[sem], recv_sems.at[sem],
                                                device_id=device, device_id_type=MESH_ID)

        def own(i):
            return remote(i, 0, ins[i], part(i, mine), sibling)

        def fetch(i, j, slot):
            if table[i][4]:
                src, dst = half(ins[i], c), half(part(i, slot), c)
            else:
                src, dst = ins[i], part(i, slot)
            return remote(i, 1 + j, src, dst, (chips[j][0], chips[j][1], c))

        def relayed(i, via, of):
            region = half(half(part(i, slots[of]), c), via)
            return remote(i, 4 + via, region, region, (chips[via][0], chips[via][1], c))

        def forward(i, j, h):
            region = half(part(i, slots[j]), h)
            return remote(i, 6 + j, region, region, sibling)

        def start():
            for i in range(n):
                for j in (x_nbr, y_nbr) if table[i][4] else (x_nbr, y_nbr, diagonal):
                    fetch(i, j, mine).start()
            for i in range(n):
                own(i).start()

        def relay():
            for i in range(n):
                if not table[i][4]:
                    for j in range(3):
                        fetch(i, j, slots[j]).wait_recv()
                    continue
                fetch(i, y_nbr, slots[y_nbr]).wait_recv()
                relayed(i, x_nbr, y_nbr).start()
                forward(i, y_nbr, c).start()
                fetch(i, x_nbr, slots[x_nbr]).wait_recv()
                relayed(i, y_nbr, x_nbr).start()
                forward(i, x_nbr, c).start()

        def relay_diagonal():
            for i in range(n):
                if table[i][4]:
                    relayed(i, x_nbr, diagonal).wait_recv()
                    relayed(i, y_nbr, diagonal).wait_recv()
                    forward(i, diagonal, c).start()

        def finish():
            for i in range(n):
                own(i).wait()
                for j in range(3):
                    if table[i][4]:
                        forward(i, j, 1 - c).wait_recv()
                        forward(i, j, c).wait_send()
                    if j != diagonal or not table[i][4]:
                        fetch(i, j, mine).wait_send()
                if table[i][4]:
                    relayed(i, x_nbr, y_nbr).wait_send()
                    relayed(i, y_nbr, x_nbr).wait_send()

        return start, relay, relay_diagonal, finish

    out_shape = [jax.ShapeDtypeStruct(_full_shape(name, shape), dtype) for name, shape, _, dtype, _ in table]
    return _Exchange(shards, out_shape, SEMS_PER_GATHER * n, ops)


def _run_exchange(exchange, name):
    n = len(exchange.ins)

    def body(*refs):
        for phase in exchange.ops(refs[:n], refs[n:2 * n], *refs[2 * n:]):
            phase()

    return pl.pallas_call(
        body,
        name=name,
        in_specs=[HBM_SPEC] * n,
        out_specs=[HBM_SPEC] * n,
        out_shape=exchange.out_shape,
        scratch_shapes=exchange.scratch,
    )(*exchange.ins)


CAST_ROWS = 64


def _cast_hosting(arrays, exchange, name):
    n_a, n_x = len(arrays), len(exchange.ins)

    def body(*refs):
        a_in, x_in = refs[:n_a], refs[n_a:n_a + n_x]
        a_out, x_out = refs[n_a + n_x:2 * n_a + n_x], refs[2 * n_a + n_x:2 * n_a + 2 * n_x]
        start, *rest = exchange.ops(x_in, x_out, *refs[2 * n_a + 2 * n_x:])
        start()
        for src, dst in zip(a_in, a_out):
            def rows(i, carry, src=src, dst=dst):
                window = pl.ds(pl.multiple_of(i * CAST_ROWS, CAST_ROWS), CAST_ROWS)
                dst[window, :] = src[window, :].astype(BF16)
                return carry

            lax.fori_loop(0, src.shape[0] // CAST_ROWS, rows, 0)
        for phase in rest:
            phase()

    outs = pl.pallas_call(
        body,
        name=name,
        in_specs=[VMEM_SPEC] * n_a + [HBM_SPEC] * n_x,
        out_specs=[VMEM_SPEC] * n_a + [HBM_SPEC] * n_x,
        out_shape=[jax.ShapeDtypeStruct(a.shape, BF16) for a in arrays] + exchange.out_shape,
        scratch_shapes=exchange.scratch,
        compiler_params=pltpu.CompilerParams(vmem_limit_bytes=VMEM_LIMIT),
    )(*arrays, *exchange.ins)
    return outs[:n_a], outs[n_a:]


def _host_exchange(exchange, in_refs, out_refs, sem_refs, first, middle, last, late=None):
    start, *relays, finish = exchange.ops(in_refs, out_refs, *sem_refs)
    pl.when(first)(start)
    pl.when(middle)(relays[0])
    if len(relays) > 1:
        pl.when(last if late is None else late)(relays[1])
    pl.when(last)(finish)


def _halves_view(name, shape, axis):
    r, c = shape
    if name == STACKED:
        return (N_CHIPS, 2, r // 2, c // N_CHIPS), lambda ref, h: ref.at[:, h]
    if axis == 1:
        return (2, r // 2, c), lambda ref, h: ref.at[h]
    return (N_CHIPS, 2, r // N_CHIPS // 2, c), lambda ref, h: ref.at[:, h]


def _halves_exchange(grads, entries):
    n_w = len(entries)
    views = [_halves_view(*entry) for entry in entries]

    def ops(ins, outs, send_sems, recv_sems):
        x, y, c = _coords()
        copies = [pltpu.make_async_remote_copy(views[i][1](ins[i], 1 - c), outs[i], send_sems.at[i], recv_sems.at[i],
                                               device_id=(x, y, 1 - c), device_id_type=MESH_ID) for i in range(n_w)]

        def start():
            for cp in copies:
                cp.start()

        def finish():
            for cp in copies:
                cp.wait()

        return start, lambda: None, finish

    half_shape = lambda v: tuple(d for i, d in enumerate(v) if i != (1 if len(v) == 4 else 0))
    out_shape = [jax.ShapeDtypeStruct(half_shape(v[0]), F32) for v in views]
    return _Exchange([g.reshape(v[0]) for g, v in zip(grads, views)], out_shape, n_w, ops)


def _add_sibling(g_view, recv, c, name):
    shape = recv.shape
    if len(shape) == 2:
        tr = _tile(shape[0], 128, 16)
        grid = (shape[0] // tr,)
        g_spec = pl.BlockSpec((None, tr, shape[1]), lambda i, c_ref: (c_ref[0], i, 0))
        r_spec = pl.BlockSpec((tr, shape[1]), lambda i, c_ref: (i, 0))
    else:
        tr = _tile(shape[1], 256, 16)
        grid = (N_CHIPS, shape[1] // tr)
        g_spec = pl.BlockSpec((None, None, tr, shape[2]), lambda k, i, c_ref: (k, c_ref[0], i, 0))
        r_spec = pl.BlockSpec((None, tr, shape[2]), lambda k, i, c_ref: (k, i, 0))

    def body(c_ref, g_ref, r_ref, o_ref):
        o_ref[...] = (g_ref[...] + r_ref[...]).astype(BF16)

    return pl.pallas_call(
        body,
        name="add_sibling_" + name,
        grid_spec=pltpu.PrefetchScalarGridSpec(num_scalar_prefetch=1, grid=grid, in_specs=[g_spec, r_spec],
                                               out_specs=r_spec),
        out_shape=jax.ShapeDtypeStruct(shape, BF16),
        compiler_params=_cparams(("parallel",) * len(grid)),
    )(c, g_view, recv)


def _piece_of(ref, name, axis, k):
    if name == STACKED or axis == 0:
        return ref.at[k]
    size = ref.shape[1] // N_CHIPS
    return ref.at[:, pl.ds(pl.multiple_of(k * size, size), size)]


def _piece_shape(name, shape, axis):
    r, c = shape
    return (r // 2, c // N_CHIPS) if (axis == 1) else (r // N_CHIPS // 2, c)


def _scatter_exchange(partials, entries):
    n_w = len(entries)

    def ops(ins, outs, send_sems, recv_sems):
        x, y, c = _coords()
        chips = _other_chips(x, y)
        copies = []
        for i, (name, _, axis) in enumerate(entries):
            for j, chip in enumerate(chips):
                sem = 3 * i + j
                copies.append(pltpu.make_async_remote_copy(
                    _piece_of(ins[i], name, axis, 2 * chip[0] + chip[1]), outs[i].at[j], send_sems.at[sem],
                    recv_sems.at[sem], device_id=(chip[0], chip[1], c), device_id_type=MESH_ID))

        def start():
            for cp in copies:
                cp.start()

        def finish():
            for cp in copies:
                cp.wait()

        return start, lambda: None, finish

    out_shape = [jax.ShapeDtypeStruct((3,) + _piece_shape(*entry), BF16) for entry in entries]
    return _Exchange(partials, out_shape, 3 * n_w, ops)


def _add_chips(partial, recv, mine, name, axis):
    rows, cols = recv.shape[1:]
    tr = _tile(rows, 256, 16)
    if name == STACKED or axis == 0:
        p_spec = pl.BlockSpec((None, tr, cols), lambda i, k_ref: (k_ref[0], i, 0))
    else:
        p_spec = pl.BlockSpec((tr, cols), lambda i, k_ref: (i, k_ref[0]))

    def body(k_ref, p_ref, r_ref, o_ref):
        f32 = lambda a: a.astype(F32)
        o_ref[...] = ((f32(p_ref[...]) + f32(r_ref[0])) + f32(r_ref[1])) + f32(r_ref[2])

    return pl.pallas_call(
        body,
        name="add_chips_" + name,
        grid_spec=pltpu.PrefetchScalarGridSpec(
            num_scalar_prefetch=1, grid=(rows // tr,),
            in_specs=[p_spec, pl.BlockSpec((3, tr, cols), lambda i, k_ref: (0, i, 0))],
            out_specs=pl.BlockSpec((tr, cols), lambda i, k_ref: (i, 0))),
        out_shape=jax.ShapeDtypeStruct((rows, cols), F32),
        compiler_params=_cparams(("parallel",)),
    )(mine, partial, recv)


def _share_with_sibling(halves):
    n_w = len(halves)

    def body(*refs):
        ins, outs = refs[:n_w], refs[n_w:2 * n_w]
        send_sems, recv_sems = refs[2 * n_w:]
        x, y, c = _coords()
        copies = [pltpu.make_async_remote_copy(ins[i], outs[i], send_sems.at[i], recv_sems.at[i],
                                               device_id=(x, y, 1 - c), device_id_type=MESH_ID) for i in range(n_w)]
        for cp in copies:
            cp.start()
        for cp in copies:
            cp.wait()

    return pl.pallas_call(
        body,
        name="share_with_sibling",
        in_specs=[HBM_SPEC] * n_w,
        out_specs=[HBM_SPEC] * n_w,
        out_shape=[jax.ShapeDtypeStruct(h.shape, F32) for h in halves],
        scratch_shapes=[pltpu.SemaphoreType.DMA((n_w,)), pltpu.SemaphoreType.DMA((n_w,))],
    )(*halves)


SMALL_ROWS = 168


def _small_exchange(buf):
    def ops(ins, outs, send_sems, recv_sems):
        x, y, c = _coords()
        me = 4 * x + 2 * y + c
        peers = [(x ^ fx, y ^ fy, c ^ fc) for fx in (0, 1) for fy in (0, 1) for fc in (0, 1)][1:]

        def copy(j, slot, dev):
            return pltpu.make_async_remote_copy(ins[0], outs[0].at[slot], send_sems.at[j], recv_sems.at[j],
                                                device_id=dev, device_id_type=MESH_ID)

        own = pltpu.make_async_copy(ins[0], outs[0].at[me], send_sems.at[N_DEV - 1])

        def start():
            own.start()
            for j, dev in enumerate(peers):
                copy(j, me, dev).start()

        def finish():
            for j, dev in enumerate(peers):
                copy(j, 4 * dev[0] + 2 * dev[1] + dev[2], dev).wait()
            own.wait()

        return start, lambda: None, finish

    return _Exchange([buf], [jax.ShapeDtypeStruct((N_DEV,) + buf.shape, F32)], N_DEV, ops)


def _sum_devices(gathered):
    def body(g_ref, out_ref):
        acc = g_ref[0]
        for d in range(1, N_DEV):
            acc = acc + g_ref[d]
        out_ref[...] = acc

    return pl.pallas_call(
        body,
        name="sum_devices",
        in_specs=[VMEM_SPEC],
        out_specs=VMEM_SPEC,
        out_shape=jax.ShapeDtypeStruct(gathered.shape[1:], F32),
    )(gathered)


def _adamw(w, g, m, v, copy_g=False):
    r, rest = w.shape[0], w.shape[1:]
    per_row = 1
    for dim in rest:
        per_row *= dim
    tr = _tile(r, max(8, (5 << 19) // (4 * per_row)), 8 if len(rest) == 1 else 1)

    def body(w_ref, g_ref, m_ref, v_ref, *out_refs):
        d_ref, mo_ref, vo_ref = out_refs[-3:]
        gg = g_ref[...]
        if copy_g:
            out_refs[0][...] = gg
        mm = ADAM_B1 * m_ref[...] + (1.0 - ADAM_B1) * gg
        vv = ADAM_B2 * v_ref[...] + (1.0 - ADAM_B2) * (gg * gg)
        m_hat = mm / (1.0 - ADAM_B1 ** ADAM_STEP)
        v_hat = vv / (1.0 - ADAM_B2 ** ADAM_STEP)
        d_ref[...] = -ADAM_LR * (m_hat / (jnp.sqrt(v_hat) + ADAM_EPS) + ADAM_WD * w_ref[...])
        mo_ref[...] = mm
        vo_ref[...] = vv

    n_out = 4 if copy_g else 3
    spec = pl.BlockSpec((tr,) + rest, lambda i: (i,) + (0,) * len(rest))
    return pl.pallas_call(
        body,
        name="adamw",
        grid=(r // tr,),
        in_specs=[spec] * 4,
        out_specs=[spec] * n_out,
        out_shape=[jax.ShapeDtypeStruct(w.shape, F32)] * n_out,
        compiler_params=_cparams(("parallel",)),
    )(w, g, m, v)


def _adamw_halves(w, own, other, m, v, c, name):
    r, cols = w.shape
    half = r // 2
    tr = _tile(half, 256, 8)
    nt = half // tr
    whole = pl.BlockSpec((tr, cols), lambda h, i, c_ref: (h * nt + i, 0))
    part = pl.BlockSpec((tr, cols), lambda h, i, c_ref: (i, 0))

    def body(c_ref, w_ref, own_ref, other_ref, m_ref, v_ref, g_ref, d_ref, mo_ref, vo_ref):
        gg = jnp.where(pl.program_id(0) == c_ref[0], own_ref[...], other_ref[...])
        g_ref[...] = gg
        mm = ADAM_B1 * m_ref[...] + (1.0 - ADAM_B1) * gg
        vv = ADAM_B2 * v_ref[...] + (1.0 - ADAM_B2) * (gg * gg)
        m_hat = mm / (1.0 - ADAM_B1 ** ADAM_STEP)
        v_hat = vv / (1.0 - ADAM_B2 ** ADAM_STEP)
        d_ref[...] = -ADAM_LR * (m_hat / (jnp.sqrt(v_hat) + ADAM_EPS) + ADAM_WD * w_ref[...])
        mo_ref[...] = mm
        vo_ref[...] = vv

    return pl.pallas_call(
        body,
        name="adamw_" + name,
        grid_spec=pltpu.PrefetchScalarGridSpec(
            num_scalar_prefetch=1, grid=(2, nt),
            in_specs=[whole, part, part, whole, whole], out_specs=[whole] * 4),
        out_shape=[jax.ShapeDtypeStruct((r, cols), F32)] * 4,
        compiler_params=_cparams(("parallel", "parallel")),
    )(c, w, own, other, m, v)


GATHER_FIRST = ("ffn1_w_in", "ffn1_w_out")
GATHER_PROJ = ("w_in",)
GATHER_LATE = ("w_branch_a", "w_branch_b", "w_out", "ffn2_w_in", "ffn2_w_out")


class _Comm:
    def __init__(self, shards, c_arr, mine_arr):
        self.shards, self.c, self.mine = shards, c_arr, mine_arr
        self.groups, self.halves = {}, {}
        self.by_name = {entry[0]: entry for entry in BIG}

    def small_gather(self, loss, small):
        pad_lanes = lambda a: jnp.concatenate([a, jnp.zeros((1, LANES - a.shape[1]), F32)], axis=1)
        buf = jnp.concatenate([
            small["meta_tokens"].reshape(128, LANES),
            small["ffn1_norm"].reshape(8, LANES), small["mix_norm"].reshape(8, LANES),
            small["ffn2_norm"].reshape(8, LANES), small["final_norm"].reshape(8, LANES),
            loss, pad_lanes(small["b_forget"]), pad_lanes(small["attn_sinks"]),
            jnp.zeros((SMALL_ROWS - 163, LANES), F32)], axis=0)
        return _small_exchange(buf)

    def small_gathered(self, outs):
        self.reduced = _sum_devices(outs[0])

    def gather(self, names):
        table = [self.by_name[n] + (BF16, True) for n in names]
        return _gather_exchange([self.shards[n] for n in names], table)

    def gathered(self, names, outs):
        return [o.transpose(1, 0, 2).reshape(D_MODEL, W_IN_COLS) if n == STACKED else o for n, o in zip(names, outs)]

    def swap(self, tag, grads):
        entries = [self.by_name[n] for n in grads]
        arrays = [g.reshape(D_MODEL, N_CHIPS, W_IN_COLS // N_CHIPS).transpose(1, 0, 2) if n == STACKED else g
                  for n, g in grads.items()]
        self.groups[tag] = (entries, arrays)
        return _halves_exchange(arrays, entries)

    def scatter(self, tag, received):
        entries, arrays = self.groups[tag]
        views = [_halves_view(*entry) for entry in entries]
        partials = [_add_sibling(g.reshape(v[0]), r, self.c, name)
                    for g, v, r, (name, _, _) in zip(arrays, views, received, entries)]
        self.groups[tag] = (entries, partials)
        return _scatter_exchange(partials, entries)

    def received(self, tag, pieces):
        entries, partials = self.groups[tag]
        for p, r, (name, _, axis) in zip(partials, pieces, entries):
            self.halves[name] = _add_chips(p, r, self.mine, name, axis)

    def finish(self):
        names = [n for n, _, _ in BIG]
        own = [self.halves[n] for n in names]
        return dict(zip(names, zip(own, _share_with_sibling(own))))


def kernel(x, meta_tokens, ffn1_norm, ffn1_w_in, ffn1_w_out, mix_norm, w_in, b_forget, attn_sinks, w_branch_a, w_branch_b, w_out, ffn2_norm, ffn2_w_in, ffn2_w_out, final_norm, loss_target, m_meta_tokens, m_ffn1_norm, m_ffn1_w_in, m_ffn1_w_out, m_mix_norm, m_w_in, m_b_forget, m_attn_sinks, m_w_branch_a, m_w_branch_b, m_w_out, m_ffn2_norm, m_ffn2_w_in, m_ffn2_w_out, m_final_norm, v_meta_tokens, v_ffn1_norm, v_ffn1_w_in, v_ffn1_w_out, v_mix_norm, v_w_in, v_b_forget, v_attn_sinks, v_w_branch_a, v_w_branch_b, v_w_out, v_ffn2_norm, v_ffn2_w_in, v_ffn2_w_out, v_final_norm):
    given = dict(locals())
    names = ["meta_tokens", "ffn1_norm", "ffn1_w_in", "ffn1_w_out", "mix_norm", "w_in", "b_forget", "attn_sinks",
             "w_branch_a", "w_branch_b", "w_out", "ffn2_norm", "ffn2_w_in", "ffn2_w_out", "final_norm"]
    big_names = [n for n, _, _ in BIG]
    cx, cy, cc = _coords()
    c_arr = cc.reshape(1).astype(jnp.int32)
    mine_arr = (2 * cx + cy).reshape(1).astype(jnp.int32)

    by_name = {entry[0]: entry for entry in BIG}
    shards = {n: given[n][0].astype(BF16) for n in GATHER_FIRST}
    table = [by_name[n] + (BF16, True) for n in GATHER_FIRST] + [("meta_tokens", (N_META, D_MODEL), 1, F32, False)]
    first = _gather_exchange([shards[n] for n in GATHER_FIRST] + [meta_tokens], table)
    late_names = [n for n in big_names if n not in GATHER_FIRST]
    casts, (w1i, w1o, meta_full) = _cast_hosting([given[n][0] for n in late_names], first, "gather_first")
    shards.update(zip(late_names, casts))
    comm = _Comm(shards, c_arr, mine_arr)
    norms = (ffn1_norm, mix_norm, ffn2_norm, final_norm.reshape(1, D_MODEL))
    loss, grad_x, small, big = _local_step(x, loss_target, meta_full, norms, b_forget, attn_sinks, (w1i, w1o), comm)

    swap = comm.swap("ffn1_in", dict(ffn1_w_in=big["ffn1_w_in"]))
    last = comm.scatter("ffn1_in", _run_exchange(swap, "exchange_halves_ffn1_in"))
    comm.received("ffn1_in", _run_exchange(last, "scatter_chip_sums"))
    grad_halves = comm.finish()
    grads = {}

    red = comm.reduced
    meta_cols = red[:128].reshape(N_META, D_MODEL)
    grads["meta_tokens"] = lax.dynamic_slice_in_dim(meta_cols, (2 * cx + cy) * (D_MODEL // N_CHIPS),
                                                    D_MODEL // N_CHIPS, axis=1)
    grads["ffn1_norm"] = red[128:136].reshape(1, D_MODEL)
    grads["mix_norm"] = red[136:144].reshape(1, D_MODEL)
    grads["ffn2_norm"] = red[144:152].reshape(1, D_MODEL)
    grads["final_norm"] = red[152:160].reshape(1, D_MODEL)
    loss_out = red[160, 0]
    grads["b_forget"] = red[161:162, :B_HEADS]
    grads["attn_sinks"] = red[162:163, :A_HEADS]

    out_g, out_d, out_m, out_v = [], [], [], []
    for n in names:
        w_full = given[n]
        shape = w_full.shape
        two_d = (lambda a: a.reshape(shape[-2], shape[-1])) if len(shape) >= 2 else (lambda a: a.reshape(1, shape[0]))
        if n == STACKED:
            own, other = grad_halves[n]
            g_nat = jnp.concatenate([jnp.where(cc == 0, own, other), jnp.where(cc == 0, other, own)], axis=0)
            rows = lambda a: a.reshape(1, shape[-2], shape[-1]).transpose(2, 0, 1)
            unrows = lambda a: a.transpose(1, 2, 0)
            g2, d2, m2, v2 = [unrows(a) for a in _adamw(rows(w_full), rows(g_nat), rows(given["m_" + n]),
                                                         rows(given["v_" + n]), copy_g=True)]
        elif n in grad_halves:
            own, other = grad_halves[n]
            g2, d2, m2, v2 = _adamw_halves(two_d(w_full), own, other, two_d(given["m_" + n]),
                                           two_d(given["v_" + n]), c_arr, n)
        else:
            g2 = two_d(grads[n])
            d2, m2, v2 = _adamw(two_d(w_full), g2, two_d(given["m_" + n]), two_d(given["v_" + n]))
        out_g.append(g2.reshape(shape))
        out_d.append(d2.reshape(shape))
        out_m.append(m2.reshape(shape))
        out_v.append(v2.reshape(shape))
    return (loss_out, grad_x, *out_g, *out_d, *out_m, *out_v)
```

```python
import jax
import jax.numpy as jnp
from jax import lax
from jax.experimental import pallas as pl
from jax.experimental.pallas import tpu as pltpu

F32 = jnp.float32
BF16 = jnp.bfloat16

D_MODEL = 1024
N_META = 16
BLOCK = 128
LANES = 128
PREFIX = BLOCK
N_PAD = PREFIX - N_META
HEAD_DIM = 64
A_HEADS = 8
A_KV_HEADS = 2
A_GROUP = 4
B_HEADS = 8
B_PAIRS = B_HEADS // 2
A_WIDTH = A_HEADS * HEAD_DIM
A_KV_WIDTH = A_KV_HEADS * HEAD_DIM
B_WIDTH = B_HEADS * HEAD_DIM
W_IN_COLS = A_WIDTH + 2 * A_KV_WIDTH + 3 * B_WIDTH + B_HEADS + 2 * D_MODEL
SRC_KA = A_WIDTH
SRC_VA = SRC_KA + A_KV_WIDTH
SRC_QB = SRC_VA + A_KV_WIDTH
SRC_KB = SRC_QB + B_WIDTH
SRC_VB = SRC_KB + B_WIDTH
SRC_F = SRC_VB + B_WIDTH
SRC_GA = SRC_F + B_HEADS
SRC_GB = SRC_GA + D_MODEL
A_PAD_WIDTH = A_HEADS * LANES
B_PAD_WIDTH = B_HEADS * LANES
F_COLS = LANES
OFF_QA = 0
OFF_KA = SRC_KA
OFF_VA = SRC_VA
OFF_F = OFF_VA + A_KV_WIDTH
OFF_QB = OFF_F + F_COLS
OFF_KB = OFF_QB + B_WIDTH
OFF_VB = OFF_KB + B_WIDTH
OFF_GA = OFF_VB + B_WIDTH
OFF_GB = OFF_GA + D_MODEL
P_COLS = OFF_GB + D_MODEL
P_PIECES = ((0, OFF_QB), (OFF_QB, OFF_GA - OFF_QB), (OFF_GA, P_COLS - OFF_GA))
EPS = 1e-6
NEG = -1e30
SCALE = HEAD_DIM ** -0.5
KEY_BLOCKS = 4

ADAM_LR = 0.001
ADAM_B1 = 0.9
ADAM_B2 = 0.999
ADAM_EPS = 1e-08
ADAM_WD = 0.01
ADAM_STEP = 10

N_CHIPS = 4
N_DEV = 8
VMEM_LIMIT = 56 * 1024 * 1024

NT_DIMS = (((1,), (1,)), ((), ()))
TN_DIMS = (((0,), (0,)), ((), ()))
MESH_ID = pl.DeviceIdType.MESH
HBM_SPEC = pl.BlockSpec(memory_space=pltpu.HBM)
VMEM_SPEC = pl.BlockSpec(memory_space=pltpu.VMEM)


def _tile(n, target, mult=16):
    best = None
    for t in range(mult, min(n, target) + 1, mult):
        if n % t == 0:
            best = t
    return best if best is not None else n


def _cparams(sem):
    return pltpu.CompilerParams(dimension_semantics=sem, vmem_limit_bytes=VMEM_LIMIT)


def _rms_scale(h):
    return lax.rsqrt(jnp.mean(h * h, axis=-1, keepdims=True) + EPS)


def _rms_bwd(dn, h, w):
    r = _rms_scale(h)
    dw = jnp.sum(dn * (h * r), axis=0, keepdims=True)
    z = dn * w
    dh = r * z - h * ((r * r * r) * jnp.mean(z * h, axis=-1, keepdims=True))
    return dh, dw


def _ffn_fwd(h, norm_w, w_in, w_out, exchange=None):
    t, d = h.shape
    f = w_out.shape[0]
    tm = _tile(t, 272)
    tc = _tile(f, 256, 128)
    nj = f // tc
    ni = t // tm
    n_x = len(exchange.ins) if exchange else 0

    def body(*refs):
        h_ref, nw_ref, wi_ref, wo_ref = refs[:4]
        hout_ref, g_ref, u_ref = refs[4 + n_x:7 + n_x]
        a_scr = refs[7 + 2 * n_x]
        i = pl.program_id(0)
        if exchange:
            _host_exchange(exchange, refs[4:4 + n_x], refs[7 + n_x:7 + 2 * n_x], refs[8 + 2 * n_x:],
                           i == 0, i == ni // 3, i == ni - 1, late=i == 2 * ni // 3)
        hh = h_ref[...]
        n = ((hh * _rms_scale(hh)) * nw_ref[...]).astype(BF16)
        for j in range(nj):
            cols = slice(j * tc, (j + 1) * tc)
            g = jnp.dot(n, wi_ref[:, j * tc:(j + 1) * tc], preferred_element_type=F32)
            u = jnp.dot(n, wi_ref[:, f + j * tc:f + (j + 1) * tc], preferred_element_type=F32)
            g_ref[:, cols] = g
            u_ref[:, cols] = u
            a_scr[:, cols] = ((g * jax.nn.sigmoid(g)) * u).astype(BF16)
        hout_ref[...] = hh + 0.5 * jnp.dot(a_scr[...], wo_ref[...], preferred_element_type=F32)

    resident = lambda a: pl.BlockSpec(a.shape, lambda i: (0, 0), pipeline_mode=pl.Buffered(1))
    row = lambda w: pl.BlockSpec((tm, w), lambda i: (i, 0))
    outs = pl.pallas_call(
        body,
        name="ffn_fwd",
        grid=(ni,),
        in_specs=[row(d), pl.BlockSpec((1, d), lambda i: (0, 0)), resident(w_in), resident(w_out)] + [HBM_SPEC] * n_x,
        out_specs=[row(d), row(f), row(f)] + [HBM_SPEC] * n_x,
        out_shape=[
            jax.ShapeDtypeStruct((t, d), F32),
            jax.ShapeDtypeStruct((t, f), F32),
            jax.ShapeDtypeStruct((t, f), F32),
        ] + (exchange.out_shape if exchange else []),
        scratch_shapes=[pltpu.VMEM((tm, f), BF16)] + (exchange.scratch if exchange else []),
        compiler_params=_cparams(("arbitrary",) if exchange else ("parallel",)),
    )(h, norm_w, w_in, w_out, *(exchange.ins if exchange else []))
    return outs[:3], outs[3:]


def _ffn_bwd(dh_out, h, norm_w, g, u, w_in, w_out, exchange=None):
    t, d = h.shape
    f = w_out.shape[0]
    tm = _tile(t, 272)
    tc = _tile(f, 256, 128)
    nj = f // tc
    ni = t // tm
    n_x = len(exchange.ins) if exchange else 0

    def body(*refs):
        dho_ref, h_ref, nw_ref, g_ref, u_ref, wi_ref, wo_ref = refs[:7]
        dhin_ref, n_ref, a_ref, dgu_ref, df_ref, dnw_ref = refs[7 + n_x:13 + n_x]
        i = pl.program_id(0)
        if exchange:
            _host_exchange(exchange, refs[7:7 + n_x], refs[13 + n_x:13 + 2 * n_x], refs[13 + 2 * n_x:],
                           i == 0, i == ni - 1, i == ni - 1)
        hh = h_ref[...]
        nw = nw_ref[...]
        n_ref[...] = ((hh * _rms_scale(hh)) * nw).astype(BF16)
        dho = dho_ref[...]
        df = (0.5 * dho).astype(BF16)
        df_ref[...] = df
        for j in range(nj):
            cols = slice(j * tc, (j + 1) * tc)
            da = lax.dot_general(df, wo_ref[cols, :], NT_DIMS, preferred_element_type=F32)
            gg = g_ref[:, cols]
            uu = u_ref[:, cols]
            sig = jax.nn.sigmoid(gg)
            sl = gg * sig
            a_ref[:, cols] = (sl * uu).astype(BF16)
            dgu_ref[0, :, cols] = ((da * uu) * (sig * (1.0 + gg * (1.0 - sig)))).astype(BF16)
            dgu_ref[1, :, cols] = (da * sl).astype(BF16)
        dn = (lax.dot_general(dgu_ref[0], wi_ref[:, :f], NT_DIMS, preferred_element_type=F32)
              + lax.dot_general(dgu_ref[1], wi_ref[:, f:], NT_DIMS, preferred_element_type=F32))
        dh, dw = _rms_bwd(dn, hh, nw)
        dhin_ref[...] = dho + dh
        dnw_ref[0] = dw

    resident = lambda a: pl.BlockSpec(a.shape, lambda i: (0, 0), pipeline_mode=pl.Buffered(1))
    row = lambda w: pl.BlockSpec((tm, w), lambda i: (i, 0))
    outs = pl.pallas_call(
        body,
        name="ffn_bwd",
        grid=(ni,),
        in_specs=[row(d), row(d), pl.BlockSpec((1, d), lambda i: (0, 0)), row(f), row(f),
                  resident(w_in), resident(w_out)] + [HBM_SPEC] * n_x,
        out_specs=[row(d), row(d), row(f), pl.BlockSpec((2, tm, f), lambda i: (0, i, 0)), row(d),
                   pl.BlockSpec((1, 1, d), lambda i: (i, 0, 0))] + [HBM_SPEC] * n_x,
        out_shape=[
            jax.ShapeDtypeStruct((t, d), F32),
            jax.ShapeDtypeStruct((t, d), BF16),
            jax.ShapeDtypeStruct((t, f), BF16),
            jax.ShapeDtypeStruct((2, t, f), BF16),
            jax.ShapeDtypeStruct((t, d), BF16),
            jax.ShapeDtypeStruct((ni, 1, d), F32),
        ] + (exchange.out_shape if exchange else []),
        scratch_shapes=exchange.scratch if exchange else [],
        compiler_params=_cparams(("arbitrary",) if exchange else ("parallel",)),
    )(dh_out, h, norm_w, g, u, w_in, w_out, *(exchange.ins if exchange else []))
    return outs[:6], outs[6:]


def _tn_matmul(a, b, name, exchange=None):
    t, k = a.shape
    split = b.ndim == 3
    n = 2 * b.shape[2] if split else b.shape[1]
    tk = _tile(k, 512, 128)
    tn = _tile(b.shape[-1], 1408, 128)
    per_half = b.shape[-1] // tn
    ni, nj = k // tk, n // tn
    n_x = len(exchange.ins) if exchange else 0

    def body(*refs):
        a_ref, b_ref, o_ref = refs[0], refs[1], refs[2 + n_x]
        if exchange:
            i, j = pl.program_id(0), pl.program_id(1)
            at_end = (i == ni - 1) & (j == nj - 1)
            _host_exchange(exchange, refs[2:2 + n_x], refs[3 + n_x:3 + 2 * n_x], refs[3 + 2 * n_x:],
                           (i == 0) & (j == 0), at_end, at_end)
        o_ref[...] = lax.dot_general(a_ref[...], b_ref[...], TN_DIMS, preferred_element_type=F32)

    if split:
        b_spec = pl.BlockSpec((None, t, tn), lambda i, j: (j // per_half, 0, j % per_half))
    else:
        b_spec = pl.BlockSpec((t, tn), lambda i, j: (0, j))
    outs = pl.pallas_call(
        body,
        name=name,
        grid=(ni, nj),
        in_specs=[pl.BlockSpec((t, tk), lambda i, j: (0, i)), b_spec] + [HBM_SPEC] * n_x,
        out_specs=[pl.BlockSpec((tk, tn), lambda i, j: (i, j))] + [HBM_SPEC] * n_x,
        out_shape=[jax.ShapeDtypeStruct((k, n), F32)] + (exchange.out_shape if exchange else []),
        scratch_shapes=exchange.scratch if exchange else [],
        compiler_params=_cparams(("arbitrary", "arbitrary") if exchange else ("parallel", "parallel")),
    )(a, b, *(exchange.ins if exchange else []))
    return (outs[0], outs[1:]) if exchange else outs[0]


A_SLOT = lambda h: h // A_GROUP
B_SLOT = lambda h: h % 2

PROJ_PARTS = (
    (OFF_QA, A_WIDTH, A_PAD_WIDTH, True, A_SLOT), (OFF_KA, A_KV_WIDTH, A_KV_WIDTH, True, None),
    (OFF_VA, A_KV_WIDTH, A_KV_WIDTH, True, None), (OFF_QB, B_WIDTH, B_WIDTH, True, None),
    (OFF_KB, B_WIDTH, B_PAD_WIDTH, True, B_SLOT), (OFF_VB, B_WIDTH, B_PAD_WIDTH, True, B_SLOT),
    (OFF_GA, D_MODEL, D_MODEL, False, None), (OFF_GB, D_MODEL, D_MODEL, False, None), (OFF_F, F_COLS, F_COLS, False, None),
)


def _head_tile(pair, head, slot):
    lane_slot = lax.broadcasted_iota(jnp.int32, pair.shape, 1) // HEAD_DIM
    moved = pair if head % 2 == slot else pltpu.roll(pair, HEAD_DIM, 1)
    return jnp.where(lane_slot == slot, moved, 0.0)


def _proj_fwd(h, norm_w, w_p):
    t, d = h.shape
    tm = _tile(t, 272)

    def body(h_ref, nw_ref, w_ref, u_ref, *part_refs):
        hh = h_ref[...]
        un = ((hh * _rms_scale(hh)) * nw_ref[...]).astype(BF16)
        u_ref[...] = un
        for (off, width, _, _, slot), p_ref in zip(PROJ_PARTS, part_refs):
            if slot is None:
                p_ref[...] = jnp.dot(un, w_ref[:, off:off + width], preferred_element_type=F32).astype(p_ref.dtype)
                continue
            part = jnp.dot(un, w_ref[:, off:off + width], preferred_element_type=F32)
            for pair in range(width // LANES):
                x = part[:, pair * LANES:(pair + 1) * LANES]
                for head in (2 * pair, 2 * pair + 1):
                    p_ref[:, head * LANES:(head + 1) * LANES] = _head_tile(x, head, slot(head)).astype(p_ref.dtype)

    row = lambda w: pl.BlockSpec((tm, w), lambda i: (i, 0))
    return pl.pallas_call(
        body,
        name="proj_fwd",
        grid=(t // tm,),
        in_specs=[row(d), pl.BlockSpec((1, d), lambda i: (0, 0)),
                  pl.BlockSpec(w_p.shape, lambda i: (0, 0), pipeline_mode=pl.Buffered(1))],
        out_specs=[row(d)] + [row(width) for _, _, width, _, _ in PROJ_PARTS],
        out_shape=[jax.ShapeDtypeStruct((t, d), BF16)]
        + [jax.ShapeDtypeStruct((t, width), BF16 if is_bf else F32) for _, _, width, is_bf, _ in PROJ_PARTS],
        compiler_params=_cparams(("parallel",)),
    )(h, norm_w, w_p)


def _proj_bwd(dh_out, h, norm_w, dproj, w_p, exchange=None):
    t, d = h.shape
    tm = _tile(t, 272)
    ni = t // tm
    n_p = len(P_PIECES)
    n_in = 4 + n_p
    n_x = len(exchange.ins) if exchange else 0

    def body(*refs):
        dho_ref, h_ref, nw_ref = refs[:3]
        dp_refs, w_ref = refs[3:3 + n_p], refs[3 + n_p]
        dhin_ref, dnw_ref = refs[n_in + n_x:n_in + 2 + n_x]
        if exchange:
            i = pl.program_id(0)
            _host_exchange(exchange, refs[n_in:n_in + n_x], refs[n_in + 2 + n_x:n_in + 2 + 2 * n_x],
                           refs[n_in + 2 + 2 * n_x:], i == 0, i == ni - 1, i == ni - 1)
        dn = None
        for dp_ref, (off, width) in zip(dp_refs, P_PIECES):
            part = lax.dot_general(dp_ref[...], w_ref[:, off:off + width], NT_DIMS, preferred_element_type=F32)
            dn = part if dn is None else dn + part
        dh, dw = _rms_bwd(dn, h_ref[...], nw_ref[...])
        dhin_ref[...] = dho_ref[...] + dh
        dnw_ref[0] = dw

    row = lambda w: pl.BlockSpec((tm, w), lambda i: (i, 0))
    outs = pl.pallas_call(
        body,
        name="proj_bwd",
        grid=(ni,),
        in_specs=[row(d), row(d), pl.BlockSpec((1, d), lambda i: (0, 0))] + [row(width) for _, width in P_PIECES]
        + [pl.BlockSpec(w_p.shape, lambda i: (0, 0), pipeline_mode=pl.Buffered(1))] + [HBM_SPEC] * n_x,
        out_specs=[row(d), pl.BlockSpec((1, 1, d), lambda i: (i, 0, 0))] + [HBM_SPEC] * n_x,
        out_shape=[jax.ShapeDtypeStruct((t, d), F32), jax.ShapeDtypeStruct((ni, 1, d), F32)]
        + (exchange.out_shape if exchange else []),
        scratch_shapes=exchange.scratch if exchange else [],
        compiler_params=_cparams(("arbitrary",) if exchange else ("parallel",)),
    )(dh_out, h, norm_w, *dproj, w_p, *(exchange.ins if exchange else []))
    return outs[:2], outs[2:]


def _merge_fwd(h, oa, ob, ga, gb, wa, wb, wo):
    t, d = h.shape
    tm = _tile(t, 544)

    def body(h_ref, oa_ref, ob_ref, ga_ref, gb_ref, wa_ref, wb_ref, wo_ref, hout_ref, mix_ref):
        ya = jnp.dot(oa_ref[...], wa_ref[...], preferred_element_type=F32)
        yb = jnp.dot(ob_ref[...], wb_ref[...], preferred_element_type=F32)
        mixed = (jax.nn.sigmoid(ga_ref[...]) * ya + jax.nn.sigmoid(gb_ref[...]) * yb).astype(BF16)
        mix_ref[...] = mixed
        hout_ref[...] = h_ref[...] + jnp.dot(mixed, wo_ref[...], preferred_element_type=F32)

    row = lambda w: pl.BlockSpec((tm, w), lambda i: (i, 0))
    full = lambda a: pl.BlockSpec(a.shape, lambda i: (0, 0))
    return pl.pallas_call(
        body,
        name="merge_fwd",
        grid=(t // tm,),
        in_specs=[row(d), row(oa.shape[1]), row(ob.shape[1]), row(d), row(d), full(wa), full(wb), full(wo)],
        out_specs=[row(d), row(d)],
        out_shape=[jax.ShapeDtypeStruct((t, d), F32), jax.ShapeDtypeStruct((t, d), BF16)],
        compiler_params=_cparams(("parallel",)),
    )(h, oa, ob, ga, gb, wa, wb, wo)


def _merge_bwd(dh, oa, ob, ga, gb, wa, wb, wo, exchange=None):
    t, d = dh.shape
    tm = _tile(t, 544)
    ni = t // tm
    n_x = len(exchange.ins) if exchange else 0

    def body(*refs):
        dh_ref, oa_ref, ob_ref, ga_ref, gb_ref, wa_ref, wb_ref, wo_ref = refs[:8]
        dya_ref, dyb_ref, doa_ref, dob_ref, dg_ref, dhb_ref = refs[8 + n_x:14 + n_x]
        if exchange:
            i = pl.program_id(0)
            _host_exchange(exchange, refs[8:8 + n_x], refs[14 + n_x:14 + 2 * n_x], refs[14 + 2 * n_x:],
                           i == 0, i == ni - 1, i == ni - 1)
        dhb = dh_ref[...].astype(BF16)
        dhb_ref[...] = dhb
        dmix = lax.dot_general(dhb, wo_ref[...], NT_DIMS, preferred_element_type=F32)
        for branch, (o_ref, g_ref, w_ref, dy_ref, do_ref) in enumerate((
                (oa_ref, ga_ref, wa_ref, dya_ref, doa_ref),
                (ob_ref, gb_ref, wb_ref, dyb_ref, dob_ref))):
            y = jnp.dot(o_ref[...], w_ref[...], preferred_element_type=F32)
            s = jax.nn.sigmoid(g_ref[...])
            dy = (dmix * s).astype(BF16)
            dy_ref[...] = dy
            dg_ref[:, branch * d:(branch + 1) * d] = ((dmix * y) * (s * (1.0 - s))).astype(BF16)
            do_ref[...] = lax.dot_general(dy, w_ref[...], NT_DIMS, preferred_element_type=F32).astype(BF16)

    row = lambda w: pl.BlockSpec((tm, w), lambda i: (i, 0))
    full = lambda a: pl.BlockSpec(a.shape, lambda i: (0, 0))
    wa_w, wb_w = oa.shape[1], ob.shape[1]
    outs = pl.pallas_call(
        body,
        name="merge_bwd",
        grid=(ni,),
        in_specs=[row(d), row(wa_w), row(wb_w), row(d), row(d), full(wa), full(wb), full(wo)] + [HBM_SPEC] * n_x,
        out_specs=[row(d), row(d), row(wa_w), row(wb_w), row(2 * d), row(d)] + [HBM_SPEC] * n_x,
        out_shape=[
            jax.ShapeDtypeStruct((t, d), BF16), jax.ShapeDtypeStruct((t, d), BF16),
            jax.ShapeDtypeStruct((t, wa_w), BF16), jax.ShapeDtypeStruct((t, wb_w), BF16),
            jax.ShapeDtypeStruct((t, 2 * d), BF16), jax.ShapeDtypeStruct((t, d), BF16),
        ] + (exchange.out_shape if exchange else []),
        scratch_shapes=exchange.scratch if exchange else [],
        compiler_params=_cparams(("arbitrary",) if exchange else ("parallel",)),
    )(dh, oa, ob, ga, gb, wa, wb, wo, *(exchange.ins if exchange else []))
    return outs[:6], outs[6:]


def _tri_dot(tri, x):
    hi = x.astype(BF16)
    r1 = x - hi.astype(F32)
    mid = r1.astype(BF16)
    lo = (r1 - mid.astype(F32)).astype(BF16)
    return (jnp.dot(tri, hi, preferred_element_type=F32)
            + jnp.dot(tri, mid, preferred_element_type=F32)
            + jnp.dot(tri, lo, preferred_element_type=F32))


def _forget_cumsum(f_logit, b_pad, nb):
    t, w = f_logit.shape
    bsz = t // (nb * BLOCK)

    def body(f_ref, b_ref, c_ref, carry):
        n = pl.program_id(1)

        @pl.when(n == 0)
        def _():
            carry[...] = jnp.zeros_like(carry)

        x = jax.nn.log_sigmoid(f_ref[...] + b_ref[...])
        rows = lax.broadcasted_iota(jnp.int32, (BLOCK, BLOCK), 0)
        cols = lax.broadcasted_iota(jnp.int32, (BLOCK, BLOCK), 1)
        tri = (cols <= rows).astype(BF16)
        c = _tri_dot(tri, x) + carry[...]
        c_ref[...] = c
        carry[...] = c[BLOCK - 1:BLOCK, :]

    return pl.pallas_call(
        body,
        name="forget_cumsum",
        grid=(bsz, nb),
        in_specs=[pl.BlockSpec((BLOCK, w), lambda b, n: (b * nb + n, 0)),
                  pl.BlockSpec((1, w), lambda b, n: (0, 0))],
        out_specs=pl.BlockSpec((BLOCK, w), lambda b, n: (b * nb + n, 0)),
        out_shape=jax.ShapeDtypeStruct((t, w), F32),
        scratch_shapes=[pltpu.VMEM((1, w), F32)],
        compiler_params=_cparams(("parallel", "arbitrary")),
    )(f_logit, b_pad)


def _forget_cumsum_bwd(dc, f_logit, b_pad, nb):
    t, w = f_logit.shape
    bsz = t // (nb * BLOCK)

    def body(dc_ref, f_ref, b_ref, df_ref, db_ref, carry):
        n = pl.program_id(1)

        @pl.when(n == 0)
        def _():
            carry[...] = jnp.zeros_like(carry)
            db_ref[...] = jnp.zeros_like(db_ref)

        rows = lax.broadcasted_iota(jnp.int32, (BLOCK, BLOCK), 0)
        cols = lax.broadcasted_iota(jnp.int32, (BLOCK, BLOCK), 1)
        tri = (cols >= rows).astype(BF16)
        dlf = _tri_dot(tri, dc_ref[...]) + carry[...]
        carry[...] = dlf[0:1, :]
        df = dlf * jax.nn.sigmoid(-(f_ref[...] + b_ref[...]))
        df_ref[...] = df.astype(BF16)
        db_ref[0] += jnp.sum(df, axis=0, keepdims=True)

    rev = lambda b, n: (b * nb + (nb - 1 - n), 0)
    return pl.pallas_call(
        body,
        name="forget_cumsum_bwd",
        grid=(bsz, nb),
        in_specs=[pl.BlockSpec((BLOCK, w), rev),
                  pl.BlockSpec((BLOCK, w), rev),
                  pl.BlockSpec((1, w), lambda b, n: (0, 0))],
        out_specs=[pl.BlockSpec((BLOCK, w), rev),
                   pl.BlockSpec((1, 1, w), lambda b, n: (b, 0, 0))],
        out_shape=[jax.ShapeDtypeStruct((t, w), BF16), jax.ShapeDtypeStruct((bsz, 1, w), F32)],
        scratch_shapes=[pltpu.VMEM((1, w), F32)],
        compiler_params=_cparams(("parallel", "arbitrary")),
    )(dc, f_logit, b_pad)


GROUP_ROWS = A_GROUP * BLOCK


def _stack_heads(ref, g):
    return jnp.concatenate([ref[:, (A_GROUP * g + i) * LANES:(A_GROUP * g + i + 1) * LANES] for i in range(A_GROUP)],
                           axis=0)


def _unstack_heads(ref, g, x):
    for i in range(A_GROUP):
        ref[:, (A_GROUP * g + i) * LANES:(A_GROUP * g + i + 1) * LANES] = x[i * BLOCK:(i + 1) * BLOCK].astype(ref.dtype)


def _swa_logits(qk, slope, n):
    qi = lax.broadcasted_iota(jnp.int32, (GROUP_ROWS, BLOCK), 0) & (BLOCK - 1)
    kj = lax.broadcasted_iota(jnp.int32, (GROUP_ROWS, BLOCK), 1)
    s_all = qk * SCALE
    out = []
    for i, (dist, ok) in enumerate((
            (n * BLOCK + qi - kj, (kj >= N_PAD) & (n * BLOCK + qi - kj >= 0)),
            (BLOCK + qi - kj, (kj > qi) & (n >= 2)),
            (qi - kj, (kj <= qi) & (n >= 1)))):
        s = s_all[:, i * BLOCK:(i + 1) * BLOCK] - slope * dist.astype(F32)
        out.append(jnp.where(ok, s, NEG))
    return out


def _three_blocks(m_ref, p_ref, c_ref):
    return jnp.concatenate([m_ref[...], p_ref[...], c_ref[...]], axis=0)


def _swa_specs(bsz, nb):
    qspec = pl.BlockSpec((bsz, BLOCK, A_PAD_WIDTH), lambda n: (0, n, 0))
    kv_m = pl.BlockSpec((bsz, BLOCK, LANES), lambda n: (0, 0, 0))
    kv_p = pl.BlockSpec((bsz, BLOCK, LANES), lambda n: (0, jnp.maximum(n - 1, 0), 0))
    kv_c = pl.BlockSpec((bsz, BLOCK, LANES), lambda n: (0, n, 0))
    rowspec = pl.BlockSpec((A_KV_HEADS, GROUP_ROWS, 1), lambda n: (0, 0, 0))
    lsespec = pl.BlockSpec((bsz, 1, A_KV_HEADS, GROUP_ROWS, 1), lambda n: (0, n, 0, 0, 0))
    return qspec, kv_m, kv_p, kv_c, rowspec, lsespec


def _swa_fwd(q, k, v, sink_rows, slope_rows, nb):
    t = q.shape[0]
    l = nb * BLOCK
    bsz = t // l

    def body(q_ref, km_ref, kp_ref, kc_ref, vm_ref, vp_ref, vc_ref, sink_ref, slope_ref, o_ref, lse_ref):
        n = pl.program_id(0)
        lane_group = lax.broadcasted_iota(jnp.int32, (GROUP_ROWS, LANES), 1) // HEAD_DIM
        products = {}
        for b in range(bsz):
            keys = _three_blocks(km_ref.at[b], kp_ref.at[b], kc_ref.at[b])
            for g in range(A_KV_HEADS):
                products[b, g] = lax.dot_general(_stack_heads(q_ref.at[b], g), keys, NT_DIMS,
                                                 preferred_element_type=F32)
        for b in range(bsz):
            values = _three_blocks(vm_ref.at[b], vp_ref.at[b], vc_ref.at[b])
            for g in range(A_KV_HEADS):
                sink = sink_ref[g]
                s_m, s_p, s_c = _swa_logits(products[b, g], slope_ref[g], n)
                m = jnp.maximum(jnp.max(jnp.maximum(jnp.maximum(s_m, s_p), s_c), axis=-1, keepdims=True), sink)
                m_wide = jnp.broadcast_to(m, (GROUP_ROWS, BLOCK))
                e_m = jnp.exp(s_m - m_wide)
                e_p = jnp.exp(s_p - m_wide)
                e_c = jnp.exp(s_c - m_wide)
                z = jnp.sum((e_m + e_p) + e_c, axis=-1, keepdims=True) + jnp.exp(sink - m)
                inv = jnp.broadcast_to(1.0 / z, (GROUP_ROWS, BLOCK))
                probs = jnp.concatenate([(e_m * inv).astype(BF16), (e_p * inv).astype(BF16),
                                         (e_c * inv).astype(BF16)], axis=1)
                o = jnp.dot(probs, values, preferred_element_type=F32)
                _unstack_heads(o_ref.at[b], g, jnp.where(lane_group == g, o, 0.0))
                lse_ref[b, 0, g] = m + jnp.log(z)

    qspec, kv_m, kv_p, kv_c, rowspec, lsespec = _swa_specs(bsz, nb)
    by_example = lambda a: a.reshape(bsz, l, a.shape[1])
    q3, k3, v3 = by_example(q), by_example(k), by_example(v)
    o, lse = pl.pallas_call(
        body,
        name="swa_fwd",
        grid=(nb,),
        in_specs=[qspec, kv_m, kv_p, kv_c, kv_m, kv_p, kv_c, rowspec, rowspec],
        out_specs=[qspec, lsespec],
        out_shape=[jax.ShapeDtypeStruct((bsz, l, A_PAD_WIDTH), BF16),
                   jax.ShapeDtypeStruct((bsz, nb, A_KV_HEADS, GROUP_ROWS, 1), F32)],
        compiler_params=_cparams(("arbitrary",)),
    )(q3, k3, k3, k3, v3, v3, v3, sink_rows, slope_rows)
    return o.reshape(t, A_PAD_WIDTH), lse


def _swa_bwd(q, k, v, do, lse, sink_rows, slope_rows, nb):
    t = q.shape[0]
    l = nb * BLOCK
    bsz = t // l

    def body(q_ref, km_ref, kp_ref, kc_ref, vm_ref, vp_ref, vc_ref, do_ref, lse_ref, sink_ref, slope_ref,
             dq_ref, dk_ref, dv_ref, dsink_ref, dk_acc, dv_acc):
        n = pl.program_id(0)

        @pl.when(n == 0)
        def _():
            dk_acc[...] = jnp.zeros_like(dk_acc)
            dv_acc[...] = jnp.zeros_like(dv_acc)
            dsink_ref[...] = jnp.zeros_like(dsink_ref)

        first_half = lax.broadcasted_iota(jnp.int32, (BLOCK, LANES), 1) < HEAD_DIM
        prev = jnp.maximum(n - 1, 0)
        products = {}
        for b in range(bsz):
            keys = _three_blocks(km_ref.at[b], kp_ref.at[b], kc_ref.at[b])
            values = _three_blocks(vm_ref.at[b], vp_ref.at[b], vc_ref.at[b])
            for g in range(A_KV_HEADS):
                products[b, g] = (
                    lax.dot_general(_stack_heads(q_ref.at[b], g), keys, NT_DIMS, preferred_element_type=F32),
                    lax.dot_general(_stack_heads(do_ref.at[b], g), values, NT_DIMS, preferred_element_type=F32))
        for b in range(bsz):
            keys = _three_blocks(km_ref.at[b], kp_ref.at[b], kc_ref.at[b])
            for g in range(A_KV_HEADS):
                qq = _stack_heads(q_ref.at[b], g)
                dob = _stack_heads(do_ref.at[b], g)
                lse_g = lse_ref[b, 0, g]
                lse_wide = jnp.broadcast_to(lse_g, (GROUP_ROWS, BLOCK))
                qk, dp_all = products[b, g]
                probs = [jnp.exp(s - lse_wide) for s in _swa_logits(qk, slope_ref[g], n)]
                dps = [dp_all[:, i * BLOCK:(i + 1) * BLOCK] for i in range(3)]
                delta = jnp.sum((probs[0] * dps[0] + probs[1] * dps[1]) + probs[2] * dps[2], axis=-1, keepdims=True)
                delta_wide = jnp.broadcast_to(delta, (GROUP_ROWS, BLOCK))
                ds = jnp.concatenate([(p * (dp - delta_wide)).astype(BF16) for p, dp in zip(probs, dps)], axis=1)
                pb = jnp.concatenate([p.astype(BF16) for p in probs], axis=1)
                dq = jnp.dot(ds, keys, preferred_element_type=F32) * SCALE
                dk_all = lax.dot_general(ds, qq, TN_DIMS, preferred_element_type=F32) * SCALE
                dv_all = lax.dot_general(pb, dob, TN_DIMS, preferred_element_type=F32)
                for i, start in enumerate((0, prev * BLOCK, n * BLOCK)):
                    rows = pl.ds(pl.multiple_of(start, BLOCK), BLOCK)
                    dk_acc[b, rows, :] += dk_all[i * BLOCK:(i + 1) * BLOCK]
                    dv_acc[b, rows, :] += dv_all[i * BLOCK:(i + 1) * BLOCK]
                for pair in range(A_GROUP // 2):
                    even = dq[2 * pair * BLOCK:(2 * pair + 1) * BLOCK]
                    odd = dq[(2 * pair + 1) * BLOCK:(2 * pair + 2) * BLOCK]
                    left = even if g == 0 else pltpu.roll(even, HEAD_DIM, 1)
                    right = pltpu.roll(odd, HEAD_DIM, 1) if g == 0 else odd
                    tile = (A_GROUP // 2) * g + pair
                    dq_ref[b, :, tile * LANES:(tile + 1) * LANES] = jnp.where(first_half, left, right).astype(BF16)
                dsink_ref[b, g] += -(jnp.exp(sink_ref[g] - lse_g) * delta)

        @pl.when(n == nb - 1)
        def _():
            dk_ref[...] = dk_acc[...].astype(BF16)
            dv_ref[...] = dv_acc[...].astype(BF16)

    qspec, kv_m, kv_p, kv_c, rowspec, lsespec = _swa_specs(bsz, nb)
    kv_all = pl.BlockSpec((bsz, l, LANES), lambda n: (0, 0, 0))
    by_example = lambda a: a.reshape(bsz, l, a.shape[1])
    q3, k3, v3 = by_example(q), by_example(k), by_example(v)
    dq, dk, dv, dsink = pl.pallas_call(
        body,
        name="swa_bwd",
        grid=(nb,),
        in_specs=[qspec, kv_m, kv_p, kv_c, kv_m, kv_p, kv_c, qspec, lsespec, rowspec, rowspec],
        out_specs=[pl.BlockSpec((bsz, BLOCK, A_WIDTH), lambda n: (0, n, 0)), kv_all, kv_all,
                   pl.BlockSpec((bsz, A_KV_HEADS, GROUP_ROWS, 1), lambda n: (0, 0, 0, 0))],
        out_shape=[jax.ShapeDtypeStruct((bsz, l, A_WIDTH), BF16),
                   jax.ShapeDtypeStruct((bsz, l, LANES), BF16),
                   jax.ShapeDtypeStruct((bsz, l, LANES), BF16),
                   jax.ShapeDtypeStruct((bsz, A_KV_HEADS, GROUP_ROWS, 1), F32)],
        scratch_shapes=[pltpu.VMEM((bsz, l, LANES), F32), pltpu.VMEM((bsz, l, LANES), F32)],
        compiler_params=_cparams(("arbitrary",)),
    )(q3, k3, k3, k3, v3, v3, v3, by_example(do), lse, sink_rows, slope_rows)
    return dq.reshape(t, A_WIDTH), dk.reshape(t, LANES), dv.reshape(t, LANES), dsink


CHUNK = KEY_BLOCKS * BLOCK


def _fox_chunk(qb, ci):
    sb = jnp.maximum(jnp.minimum(KEY_BLOCKS * ci, qb + 1 - KEY_BLOCKS), 0)
    lo = jnp.maximum(ci * CHUNK, N_PAD)
    return sb, lo, pl.ds(pl.multiple_of(sb * BLOCK, BLOCK), CHUNK)


def _fox_logits(s_ref, cr_ref, e, j, sb, lo, qb):
    lane = lax.broadcasted_iota(jnp.int32, (BLOCK, BLOCK), 1)
    ahead = lane - lax.broadcasted_iota(jnp.int32, (BLOCK, BLOCK), 0)
    first = (sb + j) * BLOCK
    s = s_ref[e, :, j * BLOCK:(j + 1) * BLOCK] - cr_ref[e, sb + j]
    return jnp.where((ahead <= qb * BLOCK - first) & (lane >= lo - first), s, NEG)


FOX_PAIRS = 4
FOX_HEADS = 2 * FOX_PAIRS
FOX_STEPS = B_PAIRS // FOX_PAIRS


def _fox_specs(nb):
    l = nb * BLOCK
    q_spec = pl.BlockSpec((BLOCK, FOX_PAIRS * LANES), lambda b, p, i: (b * nb + i, p))
    kv_spec = pl.BlockSpec((l, FOX_HEADS * LANES), lambda b, p, i: (b, p))
    cc_spec = pl.BlockSpec((FOX_HEADS, BLOCK, 1), lambda b, p, i: (b * FOX_STEPS + p, i, 0))
    cr_spec = pl.BlockSpec((FOX_HEADS, nb, 1, BLOCK), lambda b, p, i: (b * FOX_STEPS + p, 0, 0, 0))
    return q_spec, kv_spec, cc_spec, cr_spec


def _fox_fwd(q, k, v, c_row, nb, exchange=None):
    t = q.shape[0]
    bsz = t // (nb * BLOCK)
    assert nb >= KEY_BLOCKS

    n_x = len(exchange.ins) if exchange else 0

    def body(*refs):
        q_ref, k_ref, v_ref, cr_ref = refs[:4]
        o_ref, ox_ref, lse_ref = refs[4 + n_x:7 + n_x]
        s_scr, hi_scr, lo_scr = refs[7 + 2 * n_x:10 + 2 * n_x]
        qb = pl.program_id(2)
        if exchange:
            first = (pl.program_id(0) == 0) & (pl.program_id(1) == 0)
            last = (pl.program_id(0) == bsz - 1) & (pl.program_id(1) == FOX_STEPS - 1)
            _host_exchange(exchange, refs[4:4 + n_x], refs[7 + n_x:7 + 2 * n_x], refs[10 + 2 * n_x:],
                           first & (qb == 0), first & (qb == 2 * nb // 3), last & (qb == nb - 1),
                           late=last & (qb == 0))
        qs = [q_ref[:, a * LANES:(a + 1) * LANES] * SCALE for a in range(FOX_PAIRS)]
        first_half = lax.broadcasted_iota(jnp.int32, (BLOCK, LANES), 1) < HEAD_DIM

        def step(ci, carry):
            stats, accs = carry[:2 * FOX_HEADS], carry[2 * FOX_HEADS:]
            sb, lo, krows = _fox_chunk(qb, ci)
            for e in range(FOX_HEADS):
                s_scr[e] = lax.dot_general(qs[e // 2], k_ref[krows, e * LANES:(e + 1) * LANES], NT_DIMS,
                                           preferred_element_type=F32)
            new_stats, new_accs = [], []
            for a in range(FOX_PAIRS):
                alphas = []
                pv = jnp.zeros((BLOCK, LANES), F32)
                pv_lo = jnp.zeros((BLOCK, LANES), F32)
                for e in (2 * a, 2 * a + 1):
                    m, z = stats[2 * e], stats[2 * e + 1]
                    tile = slice(e * LANES, (e + 1) * LANES)
                    top = None
                    for j in range(KEY_BLOCKS):
                        s = _fox_logits(s_scr, cr_ref, e, j, sb, lo, qb)
                        s_scr[e, :, j * BLOCK:(j + 1) * BLOCK] = s
                        top = s if top is None else jnp.maximum(top, s)
                    m_new = jnp.maximum(m, jnp.max(top, axis=-1, keepdims=True))
                    alpha = jnp.exp(m - m_new)
                    m_wide = jnp.broadcast_to(m_new, (BLOCK, BLOCK))
                    total = None
                    for j in range(KEY_BLOCKS):
                        cols = slice(j * BLOCK, (j + 1) * BLOCK)
                        p = jnp.exp(s_scr[e, :, cols] - m_wide)
                        total = p if total is None else total + p
                        hi = p.astype(BF16)
                        hi_scr[e, :, cols] = hi
                        lo_scr[e, :, cols] = (p - hi.astype(F32)).astype(BF16)
                    z = alpha * z + jnp.sum(total, axis=-1, keepdims=True)
                    vv = v_ref[krows, tile]
                    pv = pv + jnp.dot(hi_scr[e], vv, preferred_element_type=F32)
                    pv_lo = pv_lo + jnp.dot(lo_scr[e], vv, preferred_element_type=F32)
                    new_stats += [m_new, z]
                    alphas.append(alpha)
                alpha = jnp.where(first_half, alphas[0], alphas[1])
                new_accs += [alpha * accs[2 * a] + pv, alpha * accs[2 * a + 1] + pv_lo]
            return (*new_stats, *new_accs)

        col = lambda val: jnp.full((BLOCK, 1), val, F32)
        done = lax.fori_loop(
            0, (qb + KEY_BLOCKS) // KEY_BLOCKS, step,
            (col(NEG), col(0.0)) * FOX_HEADS + (jnp.zeros((BLOCK, LANES), F32),) * (2 * FOX_PAIRS))
        for a in range(FOX_PAIRS):
            m0, z0, m1, z1 = done[4 * a:4 * a + 4]
            acc, acc_lo = done[2 * FOX_HEADS + 2 * a:2 * FOX_HEADS + 2 * a + 2]
            inv = 1.0 / jnp.where(first_half, z0, z1)
            tile = slice(a * LANES, (a + 1) * LANES)
            o_ref[:, tile] = (acc * inv).astype(BF16)
            ox_ref[:, tile] = (acc + acc_lo) * inv
            lse_ref[2 * a] = m0 + jnp.log(z0)
            lse_ref[2 * a + 1] = m1 + jnp.log(z1)

    q_spec, kv_spec, cc_spec, cr_spec = _fox_specs(nb)
    outs = pl.pallas_call(
        body,
        name="fox_fwd",
        grid=(bsz, FOX_STEPS, nb),
        in_specs=[q_spec, kv_spec, kv_spec, cr_spec] + [HBM_SPEC] * n_x,
        out_specs=[q_spec, q_spec, cc_spec] + [HBM_SPEC] * n_x,
        out_shape=[jax.ShapeDtypeStruct((t, B_WIDTH), BF16), jax.ShapeDtypeStruct((t, B_WIDTH), F32),
                   jax.ShapeDtypeStruct((bsz * B_HEADS, nb * BLOCK, 1), F32)] + (exchange.out_shape if exchange else []),
        scratch_shapes=[pltpu.VMEM((FOX_HEADS, BLOCK, CHUNK), F32), pltpu.VMEM((FOX_HEADS, BLOCK, CHUNK), BF16),
                        pltpu.VMEM((FOX_HEADS, BLOCK, CHUNK), BF16)] + (exchange.scratch if exchange else []),
        compiler_params=_cparams(("arbitrary",) * 3 if exchange else ("parallel", "parallel", "arbitrary")),
    )(q, k, v, c_row, *(exchange.ins if exchange else []))
    return outs[:3], outs[3:]


def _fox_bwd(q, k, v, o_exact, do, lse, c_row, nb, exchange=None):
    t = q.shape[0]
    l = nb * BLOCK
    bsz = t // l

    n_x = len(exchange.ins) if exchange else 0

    def body(*refs):
        q_ref, k_ref, v_ref, ox_ref, do_ref, lse_ref, cr_ref = refs[:7]
        dq_ref, dk_ref, dv_ref, dc_ref = refs[7 + n_x:11 + n_x]
        dk_acc, dv_acc, s_scr, dp_scr, p_scr, ds_scr, dq_scr = refs[11 + 2 * n_x:18 + 2 * n_x]
        qb = pl.program_id(2)
        if exchange:
            first = (pl.program_id(0) == 0) & (pl.program_id(1) == 0)
            last = (pl.program_id(0) == bsz - 1) & (pl.program_id(1) == FOX_STEPS - 1)
            _host_exchange(exchange, refs[7:7 + n_x], refs[11 + n_x:11 + 2 * n_x], refs[18 + 2 * n_x:],
                           first & (qb == 0), last & (qb == 0), last & (qb == nb - 1))

        @pl.when(qb == 0)
        def _():
            dk_acc[...] = jnp.zeros_like(dk_acc)
            dv_acc[...] = jnp.zeros_like(dv_acc)
            dc_ref[...] = jnp.zeros_like(dc_ref)

        top_half = lax.broadcasted_iota(jnp.int32, (LANES, BLOCK), 0) < HEAD_DIM
        pair_t = lambda x: jnp.concatenate([jnp.where(top_half, x.T, 0), jnp.where(top_half, 0, x.T)], axis=1)
        first_half = lax.broadcasted_iota(jnp.int32, (BLOCK, LANES), 1) < HEAD_DIM
        wide = lambda col: jnp.broadcast_to(col, (BLOCK, BLOCK))
        qs, dobs, qs_t, dob_t, deltas = [], [], [], [], []
        for a in range(FOX_PAIRS):
            tile = slice(a * LANES, (a + 1) * LANES)
            qs.append(q_ref[:, tile] * SCALE)
            dobs.append(do_ref[:, tile])
            qs_t.append(pair_t(qs[a]))
            dob_t.append(pair_t(dobs[a]))
            weighted = dobs[a].astype(F32) * ox_ref[:, tile]
            deltas += [wide(jnp.sum(jnp.where(first_half, weighted, 0.0), axis=-1, keepdims=True)),
                       wide(jnp.sum(jnp.where(first_half, 0.0, weighted), axis=-1, keepdims=True))]
        lses = [wide(lse_ref[e]) for e in range(FOX_HEADS)]

        dq_scr[...] = jnp.zeros(dq_scr.shape, F32)

        def step(ci, carry):
            sb, lo, krows = _fox_chunk(qb, ci)
            for e in range(FOX_HEADS):
                tile = slice(e * LANES, (e + 1) * LANES)
                s_scr[e] = lax.dot_general(qs[e // 2], k_ref[krows, tile], NT_DIMS, preferred_element_type=F32)
                dp_scr[e] = lax.dot_general(dobs[e // 2], v_ref[krows, tile], NT_DIMS, preferred_element_type=F32)
            for a in range(FOX_PAIRS):
                for e in (2 * a, 2 * a + 1):
                    tile = slice(e * LANES, (e + 1) * LANES)
                    kk = k_ref[krows, tile]
                    for j in range(KEY_BLOCKS):
                        cols = slice(j * BLOCK, (j + 1) * BLOCK)
                        p = jnp.exp(_fox_logits(s_scr, cr_ref, e, j, sb, lo, qb) - lses[e])
                        ds = p * (dp_scr[e, :, cols] - deltas[e])
                        dc_ref[e, sb + j] -= jnp.sum(ds, axis=0, keepdims=True)
                        p_scr[e, :, cols] = p.astype(BF16)
                        ds_scr[e, :, cols] = ds.astype(BF16)
                    dq_scr[a] += jnp.dot(ds_scr[e], kk, preferred_element_type=F32)
                both = slice(2 * a, 2 * a + 2)
                dk_t = jnp.dot(qs_t[a], ds_scr[both].reshape(2 * BLOCK, CHUNK), preferred_element_type=F32)
                dv_t = jnp.dot(dob_t[a], p_scr[both].reshape(2 * BLOCK, CHUNK), preferred_element_type=F32)
                for j in range(KEY_BLOCKS):
                    cols = slice(j * BLOCK, (j + 1) * BLOCK)
                    dk_acc[a * nb + sb + j] += dk_t[:, cols]
                    dv_acc[a * nb + sb + j] += dv_t[:, cols]
            return carry

        lax.fori_loop(0, (qb + KEY_BLOCKS) // KEY_BLOCKS, step, 0)
        for a in range(FOX_PAIRS):
            dq_ref[:, a * LANES:(a + 1) * LANES] = (dq_scr[a] * SCALE).astype(BF16)

        @pl.when(qb == nb - 1)
        def _():
            for a in range(FOX_PAIRS):
                for kb in range(nb):
                    rows = slice(kb * BLOCK, (kb + 1) * BLOCK)
                    for acc, out_ref in ((dk_acc, dk_ref), (dv_acc, dv_ref)):
                        out_ref[rows, a * LANES:(a + 1) * LANES] = acc[a * nb + kb].T.astype(BF16)

    q_spec, kv_spec, cc_spec, cr_spec = _fox_specs(nb)
    dkv_spec = pl.BlockSpec((l, FOX_PAIRS * LANES), lambda b, p, i: (b, p))
    outs = pl.pallas_call(
        body,
        name="fox_bwd",
        grid=(bsz, FOX_STEPS, nb),
        in_specs=[q_spec, kv_spec, kv_spec, q_spec, q_spec, cc_spec, cr_spec] + [HBM_SPEC] * n_x,
        out_specs=[q_spec, dkv_spec, dkv_spec, cr_spec] + [HBM_SPEC] * n_x,
        out_shape=[jax.ShapeDtypeStruct((t, B_WIDTH), BF16), jax.ShapeDtypeStruct((t, B_WIDTH), BF16),
                   jax.ShapeDtypeStruct((t, B_WIDTH), BF16),
                   jax.ShapeDtypeStruct((bsz * B_HEADS, nb, 1, BLOCK), F32)] + (exchange.out_shape if exchange else []),
        scratch_shapes=[pltpu.VMEM((FOX_PAIRS * nb, LANES, BLOCK), F32), pltpu.VMEM((FOX_PAIRS * nb, LANES, BLOCK), F32),
                        pltpu.VMEM((FOX_HEADS, BLOCK, CHUNK), F32), pltpu.VMEM((FOX_HEADS, BLOCK, CHUNK), F32),
                        pltpu.VMEM((FOX_HEADS, BLOCK, CHUNK), BF16), pltpu.VMEM((FOX_HEADS, BLOCK, CHUNK), BF16),
                        pltpu.VMEM((FOX_PAIRS, BLOCK, LANES), F32)]
        + (exchange.scratch if exchange else []),
        compiler_params=_cparams(("arbitrary",) * 3 if exchange else ("parallel", "parallel", "arbitrary")),
    )(q, k, v, o_exact, do, lse, c_row, *(exchange.ins if exchange else []))
    return outs[:4], outs[4:]


def _loss_head(h, final_w, target):
    bsz, l, d = h.shape
    nb = l // BLOCK

    def body(h_ref, w_ref, t_ref, loss_ref, dh_ref, dw_ref):
        n = pl.program_id(0)

        @pl.when(n == 0)
        def _():
            loss_ref[...] = jnp.zeros_like(loss_ref)
            dw_ref[...] = jnp.zeros_like(dw_ref)
            dh_ref[...] = jnp.zeros_like(dh_ref)

        @pl.when(n > 0)
        def _():
            w = w_ref[...]
            for b in range(bsz):
                hh = h_ref[b]
                r = _rms_scale(hh)
                err = (hh * r) * w - t_ref[b]
                loss_ref[...] += 0.5 * jnp.sum(jnp.mean(err * err, axis=-1, keepdims=True), axis=0, keepdims=True)
                dy = err * (1.0 / d)
                dh, dw = _rms_bwd(dy, hh, w)
                dh_ref[b] = dh
                dw_ref[...] += dw

    return pl.pallas_call(
        body,
        name="loss_head",
        grid=(nb,),
        in_specs=[
            pl.BlockSpec((bsz, BLOCK, d), lambda n: (0, n, 0)),
            pl.BlockSpec((1, d), lambda n: (0, 0)),
            pl.BlockSpec((bsz, BLOCK, d), lambda n: (0, jnp.maximum(n - 1, 0), 0)),
        ],
        out_specs=[
            pl.BlockSpec((1, 128), lambda n: (0, 0)),
            pl.BlockSpec((bsz, BLOCK, d), lambda n: (0, n, 0)),
            pl.BlockSpec((1, d), lambda n: (0, 0)),
        ],
        out_shape=[jax.ShapeDtypeStruct((1, 128), F32), jax.ShapeDtypeStruct((bsz, l, d), F32),
                   jax.ShapeDtypeStruct((1, d), F32)],
        compiler_params=_cparams(("arbitrary",)),
    )(h, final_w, target)


def _pad_tiles(w, src, heads, lane_slot, axis):
    pieces = []
    for h in range(heads):
        x = lax.slice_in_dim(w, src + HEAD_DIM * h, src + HEAD_DIM * (h + 1), axis=axis)
        z = jnp.zeros_like(x)
        pieces += [x, z] if lane_slot(h) == 0 else [z, x]
    return pieces


def _unpad_tiles(g, off, heads, lane_slot, axis):
    return [lax.slice_in_dim(g, off + LANES * h + HEAD_DIM * lane_slot(h),
                             off + LANES * h + HEAD_DIM * (lane_slot(h) + 1), axis=axis) for h in range(heads)]


REF_RUNS = ((0, SRC_QB, 0, 0), (SRC_QB, SRC_F, 1, 0), (SRC_F, SRC_GA, 0, OFF_F), (SRC_GA, W_IN_COLS, 2, 0))
P_RUNS = ((0, OFF_F, 0), (OFF_F, OFF_F + B_HEADS, SRC_F), (OFF_QB, OFF_GA, SRC_QB), (OFF_GA, P_COLS, SRC_GA))
SHARD_COLS = W_IN_COLS // N_CHIPS
SHARD_PAD_COLS = -(-SHARD_COLS // LANES) * LANES


def _place(tile, lane, src_ref, src, dst, length):
    for t in range(src // LANES, (src + length - 1) // LANES + 1):
        x = src_ref[:, t * LANES:(t + 1) * LANES].astype(F32)
        shift = (dst - src) % LANES
        moved = pltpu.roll(x, shift, 1) if shift else x
        from_t = (lane >= max(dst, dst + t * LANES - src)) & (lane < min(dst + length, dst + (t + 1) * LANES - src))
        tile = jnp.where(from_t, moved, tile)
    return tile


def _layout_w_in(stacked):
    d = stacked.shape[1]
    rows = _tile(d, 256)

    def body(s_ref, o_ref):
        lane = lax.broadcasted_iota(jnp.int32, (rows, LANES), 1)
        for lo in range(0, P_COLS, LANES):
            tile = jnp.zeros((rows, LANES), F32)
            for first, end, ref_col in P_RUNS:
                start, stop = max(lo, first), min(lo + LANES, end)
                while start < stop:
                    k, src = divmod(ref_col + start - first, SHARD_COLS)
                    length = min(stop - start, SHARD_COLS - src)
                    tile = _place(tile, lane, s_ref.at[k], src, start - lo, length)
                    start += length
            o_ref[:, lo:lo + LANES] = tile.astype(o_ref.dtype)

    return pl.pallas_call(
        body,
        name="layout_w_in",
        grid=(d // rows,),
        in_specs=[pl.BlockSpec((N_CHIPS, rows, SHARD_PAD_COLS), lambda i: (0, i, 0))],
        out_specs=pl.BlockSpec((rows, P_COLS), lambda i: (i, 0)),
        out_shape=jax.ShapeDtypeStruct((d, P_COLS), stacked.dtype),
        compiler_params=_cparams(("parallel",)),
    )(stacked)


def _stack_w_in_grad(pieces):
    d = pieces[0].shape[0]
    rows = _tile(d, 256)

    def body(*refs):
        piece_refs, o_ref = refs[:-1], refs[-1]
        lane = lax.broadcasted_iota(jnp.int32, (rows, LANES), 1)
        for k in range(N_CHIPS):
            for j in range(0, SHARD_COLS, LANES):
                width = min(LANES, SHARD_COLS - j)
                lo = k * SHARD_COLS + j
                tile = jnp.zeros((rows, LANES), F32)
                for first, end, piece, at in REF_RUNS:
                    start, stop = max(lo, first), min(lo + width, end)
                    if start >= stop:
                        continue
                    tile = _place(tile, lane, piece_refs[piece], at + start - first, start - lo, stop - start)
                o_ref[k, :, j:j + width] = tile[:, :width]

    return pl.pallas_call(
        body,
        name="stack_w_in_grad",
        grid=(d // rows,),
        in_specs=[pl.BlockSpec((rows, p.shape[1]), lambda i: (i, 0)) for p in pieces],
        out_specs=pl.BlockSpec((N_CHIPS, rows, SHARD_COLS), lambda i: (0, i, 0)),
        out_shape=jax.ShapeDtypeStruct((N_CHIPS, d, SHARD_COLS), F32),
        compiler_params=_cparams(("parallel",)),
    )(*pieces)


def _local_step(x, target, meta, norms, b_forget, sinks, w, comm=None):
    n1, nmix, n2, nfin = norms
    w1i, w1o = w[:2]
    bsz, seq, d = x.shape
    l = PREFIX + seq
    nb = l // BLOCK
    t = bsz * l

    h0 = jnp.concatenate([jnp.zeros((bsz, N_PAD, d), F32),
                          jnp.broadcast_to(meta[None], (bsz, N_META, d)), x], axis=1).reshape(t, d)

    if comm is None:
        (h1, g1, u1), _ = _ffn_fwd(h0, n1, w1i, w1o)
        w_in, wa, wb, wo, w2i, w2o = w[2:]
        w_in = jnp.pad(w_in.reshape(d, N_CHIPS, SHARD_COLS).transpose(1, 0, 2),
                       ((0, 0), (0, 0), (0, SHARD_PAD_COLS - SHARD_COLS)))
    else:
        (h1, g1, u1), (w_in,) = _ffn_fwd(h0, n1, w1i, w1o, comm.gather(GATHER_PROJ))
    wp = _layout_w_in(w_in)
    un, qa, ka, va, qb, kb, vb, ga, gb, f_logit = _proj_fwd(h1, nmix, wp)
    b_pad = jnp.concatenate([b_forget, jnp.zeros((1, F_COLS - B_HEADS), F32)], axis=1)
    c = _forget_cumsum(f_logit, b_pad, nb)
    c_heads = c[:, :B_HEADS].reshape(bsz, l, B_HEADS).transpose(0, 2, 1).reshape(bsz * B_HEADS, l)
    c_row = c_heads.reshape(bsz * B_HEADS, nb, 1, BLOCK)

    slopes = jnp.exp2(-8.0 * jnp.arange(1, A_HEADS + 1, dtype=F32) / A_HEADS)
    slope_rows = jnp.repeat(slopes.reshape(A_KV_HEADS, A_GROUP), BLOCK, axis=1)[:, :, None]
    sink_rows = jnp.repeat(sinks.reshape(A_KV_HEADS, A_GROUP), BLOCK, axis=1)[:, :, None]

    oa, lse_a = _swa_fwd(qa, ka, va, sink_rows, slope_rows, nb)
    if comm is None:
        (ob, ob_exact, lse_b), _ = _fox_fwd(qb, kb, vb, c_row, nb)
    else:
        (ob, ob_exact, lse_b), (wa, wb, wo, w2i, w2o) = _fox_fwd(qb, kb, vb, c_row, nb, comm.gather(GATHER_LATE))
    wa_p = jnp.concatenate(_pad_tiles(wa, 0, A_HEADS, A_SLOT, 0), axis=0)
    h2, mixed = _merge_fwd(h1, oa, ob, ga, gb, wa_p, wb, wo)
    (h3, g2, u2), _ = _ffn_fwd(h2, n2, w2i, w2o)
    loss, dh3, d_nfin = _loss_head(h3.reshape(bsz, l, d), nfin, target)

    (dh2, n2b, a2, dgu2, df2, dn2_parts), _ = _ffn_bwd(dh3.reshape(t, d), h2, n2, g2, u2, w2i, w2o)
    g_w2o = _tn_matmul(a2, df2, "grad_ffn2_w_out")
    g_w2i = _tn_matmul(n2b, dgu2, "grad_ffn2_w_in")

    hosted = comm.swap("ffn2", dict(ffn2_w_in=g_w2i, ffn2_w_out=g_w2o)) if comm else None
    (dya, dyb, doa, dob, dgates, dh2b), swapped = _merge_bwd(dh2, oa, ob, ga, gb, wa_p, wb, wo, hosted)
    g_wo = _tn_matmul(mixed, dh2b, "grad_w_out")
    g_wa = jnp.concatenate(_unpad_tiles(_tn_matmul(oa, dya, "grad_w_branch_a"), 0, A_HEADS, A_SLOT, 0), axis=0)
    g_wb = _tn_matmul(ob, dyb, "grad_w_branch_b")

    dqa, dka, dva, dsink_rows = _swa_bwd(qa, ka, va, doa, lse_a, sink_rows, slope_rows, nb)
    hosted = comm.scatter("ffn2", swapped) if comm else None
    (dqb, dkb, dvb, dc_row), pieces = _fox_bwd(qb, kb, vb, ob_exact, dob, lse_b, c_row, nb, hosted)
    if comm:
        comm.received("ffn2", pieces)
    dc = dc_row.reshape(bsz, B_HEADS, l).transpose(0, 2, 1).reshape(t, B_HEADS)
    dc = jnp.concatenate([dc, jnp.zeros((t, F_COLS - B_HEADS), F32)], axis=1)
    df_logit, db_parts = _forget_cumsum_bwd(dc, f_logit, b_pad, nb)

    dproj = (jnp.concatenate([dqa, dka, dva, df_logit], axis=1), jnp.concatenate([dqb, dkb, dvb], axis=1), dgates)
    g_win = _stack_w_in_grad([_tn_matmul(un, piece, "grad_w_in_" + tag) for piece, tag in zip(dproj, ("a", "b", "gates"))])
    if comm is None:
        g_win = g_win.transpose(1, 0, 2).reshape(d, W_IN_COLS)
    hosted = comm.swap("mixer", dict(w_in=g_win, w_branch_a=g_wa, w_branch_b=g_wb, w_out=g_wo)) if comm else None
    (dh1, dnmix_parts), swapped = _proj_bwd(dh2, h1, nmix, dproj, wp, hosted)
    hosted = comm.scatter("mixer", swapped) if comm else None
    (dh0, n1b, a1, dgu1, df1, dn1_parts), pieces = _ffn_bwd(dh1, h0, n1, g1, u1, w1i, w1o, hosted)
    dh0 = dh0.reshape(bsz, l, d)
    grad_x = dh0[:, PREFIX:]
    small = dict(
        meta_tokens=jnp.sum(dh0[:, N_PAD:PREFIX], axis=0),
        ffn1_norm=jnp.sum(dn1_parts, axis=0),
        mix_norm=jnp.sum(dnmix_parts, axis=0),
        ffn2_norm=jnp.sum(dn2_parts, axis=0),
        final_norm=d_nfin,
        b_forget=jnp.sum(db_parts, axis=0)[:, :B_HEADS],
        attn_sinks=jnp.sum(dsink_rows.reshape(bsz, A_HEADS, BLOCK), axis=(0, 2)).reshape(1, A_HEADS),
    )
    if comm is None:
        g_w1o = _tn_matmul(a1, df1, "grad_ffn1_w_out")
        g_w1i = _tn_matmul(n1b, dgu1, "grad_ffn1_w_in")
    else:
        comm.received("mixer", pieces)
        g_w1o, gathered = _tn_matmul(a1, df1, "grad_ffn1_w_out", comm.small_gather(loss, small))
        comm.small_gathered(gathered)
        swapped = _run_exchange(comm.swap("ffn1_out", dict(ffn1_w_out=g_w1o)), "exchange_halves_ffn1_out")
        g_w1i, pieces = _tn_matmul(n1b, dgu1, "grad_ffn1_w_in", comm.scatter("ffn1_out", swapped))
        comm.received("ffn1_out", pieces)
    big = dict(ffn1_w_in=g_w1i, ffn1_w_out=g_w1o, w_in=g_win, w_branch_a=g_wa, w_branch_b=g_wb,
               w_out=g_wo, ffn2_w_in=g_w2i, ffn2_w_out=g_w2o)
    return loss, grad_x, small, big


BIG = (
    ("ffn1_w_in", (D_MODEL, 5632), 1),
    ("ffn1_w_out", (2816, D_MODEL), 0),
    ("w_in", (D_MODEL, W_IN_COLS), 1),
    ("w_branch_a", (A_WIDTH, D_MODEL), 1),
    ("w_branch_b", (B_WIDTH, D_MODEL), 1),
    ("w_out", (D_MODEL, D_MODEL), 0),
    ("ffn2_w_in", (D_MODEL, 5632), 1),
    ("ffn2_w_out", (2816, D_MODEL), 0),
)
STACKED = "w_in"


def _coords():
    return lax.axis_index("x"), lax.axis_index("y"), lax.axis_index("c")


def _other_chips(x, y):
    return ((1 - x, y), (x, 1 - y), (1 - x, 1 - y))


def _chip_part(ref, name, shape, axis, k):
    if name == STACKED:
        return ref.at[k]
    size = shape[axis] // N_CHIPS
    start = pl.multiple_of(k * size, size)
    return ref.at[pl.ds(start, size), :] if axis == 0 else ref.at[:, pl.ds(start, size)]


def _full_shape(name, shape):
    return (N_CHIPS, shape[0], shape[1] // N_CHIPS) if name == STACKED else shape


class _Exchange:
    def __init__(self, ins, out_shape, n_sems, ops):
        self.ins, self.out_shape, self.n_sems, self.ops = list(ins), list(out_shape), n_sems, ops

    @property
    def scratch(self):
        return [pltpu.SemaphoreType.DMA((self.n_sems,)), pltpu.SemaphoreType.DMA((self.n_sems,))]


SEMS_PER_GATHER = 9


def _gather_exchange(shards, table):
    n = len(table)
    x_nbr, y_nbr, diagonal = 0, 1, 2

    def ops(ins, outs, send_sems, recv_sems):
        x, y, c = _coords()
        mine = 2 * x + y
        sibling = (x, y, 1 - c)
        chips = _other_chips(x, y)
        slots = [2 * chip[0] + chip[1] for chip in chips]

        def part(i, k):
            name, shape, axis = table[i][:3]
            return _chip_part(outs[i], name, shape, axis, k)

        def half(ref, h):
            rows = ref.shape[0] // 2
            return ref.at[pl.ds(pl.multiple_of(h * rows, rows), rows), :]

        def remote(i, sem, src, dst, device):
            sem = SEMS_PER_GATHER * i + sem
            return pltpu.make_async_remote_copy(src, dst, send_sems.at[sem], recv_sems.at[sem],
                                                device_id=device, device_id_type=MESH_ID)

        def own(i):
            return remote(i, 0, ins[i], part(i, mine), sibling)

        def fetch(i, j, slot):
            if table[i][4]:
                src, dst = half(ins[i], c), half(part(i, slot), c)
            else:
                src, dst = ins[i], part(i, slot)
            return remote(i, 1 + j, src, dst, (chips[j][0], chips[j][1], c))

        def relayed(i, via, of):
            region = half(half(part(i, slots[of]), c), via)
            return remote(i, 4 + via, region, region, (chips[via][0], chips[via][1], c))

        def forward(i, j, h):
            region = half(part(i, slots[j]), h)
            return remote(i, 6 + j, region, region, sibling)

        def start():
            for i in range(n):
                for j in (x_nbr, y_nbr) if table[i][4] else (x_nbr, y_nbr, diagonal):
                    fetch(i, j, mine).start()
            for i in range(n):
                own(i).start()

        def relay():
            for i in range(n):
                if not table[i][4]:
                    for j in range(3):
                        fetch(i, j, slots[j]).wait_recv()
                    continue
                fetch(i, y_nbr, slots[y_nbr]).wait_recv()
                relayed(i, x_nbr, y_nbr).start()
                forward(i, y_nbr, c).start()
                fetch(i, x_nbr, slots[x_nbr]).wait_recv()
                relayed(i, y_nbr, x_nbr).start()
                forward(i, x_nbr, c).start()

        def relay_diagonal():
            for i in range(n):
                if table[i][4]:
                    relayed(i, x_nbr, diagonal).wait_recv()
                    relayed(i, y_nbr, diagonal).wait_recv()
                    forward(i, diagonal, c).start()

        def finish():
            for i in range(n):
                own(i).wait()
                for j in range(3):
                    if table[i][4]:
                        forward(i, j, 1 - c).wait_recv()
                        forward(i, j, c).wait_send()
                    if j != diagonal or not table[i][4]:
                        fetch(i, j, mine).wait_send()
                if table[i][4]:
                    relayed(i, x_nbr, y_nbr).wait_send()
                    relayed(i, y_nbr, x_nbr).wait_send()

        return start, relay, relay_diagonal, finish

    out_shape = [jax.ShapeDtypeStruct(_full_shape(name, shape), dtype) for name, shape, _, dtype, _ in table]
    return _Exchange(shards, out_shape, SEMS_PER_GATHER * n, ops)


def _run_exchange(exchange, name):
    n = len(exchange.ins)

    def body(*refs):
        for phase in exchange.ops(refs[:n], refs[n:2 * n], *refs[2 * n:]):
            phase()

    return pl.pallas_call(
        body,
        name=name,
        in_specs=[HBM_SPEC] * n,
        out_specs=[HBM_SPEC] * n,
        out_shape=exchange.out_shape,
        scratch_shapes=exchange.scratch,
    )(*exchange.ins)


CAST_ROWS = 64


def _cast_hosting(arrays, exchange, name):
    n_a, n_x = len(arrays), len(exchange.ins)

    def body(*refs):
        a_in, x_in = refs[:n_a], refs[n_a:n_a + n_x]
        a_out, x_out = refs[n_a + n_x:2 * n_a + n_x], refs[2 * n_a + n_x:2 * n_a + 2 * n_x]
        start, *rest = exchange.ops(x_in, x_out, *refs[2 * n_a + 2 * n_x:])
        start()
        for src, dst in zip(a_in, a_out):
            def rows(i, carry, src=src, dst=dst):
                window = pl.ds(pl.multiple_of(i * CAST_ROWS, CAST_ROWS), CAST_ROWS)
                cols = src.shape[1]
                if dst.shape[1] != cols:
                    dst[window, dst.shape[1] - LANES:] = jnp.zeros((CAST_ROWS, LANES), BF16)
                dst[window, :cols] = src[window, :].astype(BF16)
                return carry

            lax.fori_loop(0, src.shape[0] // CAST_ROWS, rows, 0)
        for phase in rest:
            phase()

    outs = pl.pallas_call(
        body,
        name=name,
        in_specs=[VMEM_SPEC] * n_a + [HBM_SPEC] * n_x,
        out_specs=[VMEM_SPEC] * n_a + [HBM_SPEC] * n_x,
        out_shape=[jax.ShapeDtypeStruct((a.shape[0], -(-a.shape[1] // LANES) * LANES), BF16) for a in arrays]
        + exchange.out_shape,
        scratch_shapes=exchange.scratch,
        compiler_params=pltpu.CompilerParams(vmem_limit_bytes=VMEM_LIMIT),
    )(*arrays, *exchange.ins)
    return outs[:n_a], outs[n_a:]


def _host_exchange(exchange, in_refs, out_refs, sem_refs, first, middle, last, late=None):
    start, *relays, finish = exchange.ops(in_refs, out_refs, *sem_refs)
    pl.when(first)(start)
    pl.when(middle)(relays[0])
    if len(relays) > 1:
        pl.when(last if late is None else late)(relays[1])
    pl.when(last)(finish)


def _halves_view(name, shape, axis):
    r, c = shape
    if name == STACKED:
        return (N_CHIPS, 2, r // 2, c // N_CHIPS), lambda ref, h: ref.at[:, h]
    if axis == 1:
        return (2, r // 2, c), lambda ref, h: ref.at[h]
    return (N_CHIPS, 2, r // N_CHIPS // 2, c), lambda ref, h: ref.at[:, h]


def _halves_exchange(grads, entries):
    n_w = len(entries)
    views = [_halves_view(*entry) for entry in entries]

    def ops(ins, outs, send_sems, recv_sems):
        x, y, c = _coords()
        copies = [pltpu.make_async_remote_copy(views[i][1](ins[i], 1 - c), outs[i], send_sems.at[i], recv_sems.at[i],
                                               device_id=(x, y, 1 - c), device_id_type=MESH_ID) for i in range(n_w)]

        def start():
            for cp in copies:
                cp.start()

        def finish():
            for cp in copies:
                cp.wait()

        return start, lambda: None, finish

    half_shape = lambda v: tuple(d for i, d in enumerate(v) if i != (1 if len(v) == 4 else 0))
    out_shape = [jax.ShapeDtypeStruct(half_shape(v[0]), F32) for v in views]
    return _Exchange([g.reshape(v[0]) for g, v in zip(grads, views)], out_shape, n_w, ops)


def _add_sibling(g_view, recv, c, name):
    shape = recv.shape
    if len(shape) == 2:
        tr = _tile(shape[0], 128, 16)
        grid = (shape[0] // tr,)
        g_spec = pl.BlockSpec((None, tr, shape[1]), lambda i, c_ref: (c_ref[0], i, 0))
        r_spec = pl.BlockSpec((tr, shape[1]), lambda i, c_ref: (i, 0))
    else:
        tr = _tile(shape[1], 256, 16)
        grid = (N_CHIPS, shape[1] // tr)
        g_spec = pl.BlockSpec((None, None, tr, shape[2]), lambda k, i, c_ref: (k, c_ref[0], i, 0))
        r_spec = pl.BlockSpec((None, tr, shape[2]), lambda k, i, c_ref: (k, i, 0))

    def body(c_ref, g_ref, r_ref, o_ref):
        o_ref[...] = (g_ref[...] + r_ref[...]).astype(BF16)

    return pl.pallas_call(
        body,
        name="add_sibling_" + name,
        grid_spec=pltpu.PrefetchScalarGridSpec(num_scalar_prefetch=1, grid=grid, in_specs=[g_spec, r_spec],
                                               out_specs=r_spec),
        out_shape=jax.ShapeDtypeStruct(shape, BF16),
        compiler_params=_cparams(("parallel",) * len(grid)),
    )(c, g_view, recv)


def _piece_of(ref, name, axis, k):
    if name == STACKED or axis == 0:
        return ref.at[k]
    size = ref.shape[1] // N_CHIPS
    return ref.at[:, pl.ds(pl.multiple_of(k * size, size), size)]


def _piece_shape(name, shape, axis):
    r, c = shape
    return (r // 2, c // N_CHIPS) if (axis == 1) else (r // N_CHIPS // 2, c)


def _scatter_exchange(partials, entries):
    n_w = len(entries)

    def ops(ins, outs, send_sems, recv_sems):
        x, y, c = _coords()
        chips = _other_chips(x, y)
        copies = []
        for i, (name, _, axis) in enumerate(entries):
            for j, chip in enumerate(chips):
                sem = 3 * i + j
                copies.append(pltpu.make_async_remote_copy(
                    _piece_of(ins[i], name, axis, 2 * chip[0] + chip[1]), outs[i].at[j], send_sems.at[sem],
                    recv_sems.at[sem], device_id=(chip[0], chip[1], c), device_id_type=MESH_ID))

        def start():
            for cp in copies:
                cp.start()

        def finish():
            for cp in copies:
                cp.wait()

        return start, lambda: None, finish

    out_shape = [jax.ShapeDtypeStruct((3,) + _piece_shape(*entry), BF16) for entry in entries]
    return _Exchange(partials, out_shape, 3 * n_w, ops)


def _add_chips(partial, recv, mine, name, axis):
    rows, cols = recv.shape[1:]
    tr = _tile(rows, 256, 16)
    if name == STACKED or axis == 0:
        p_spec = pl.BlockSpec((None, tr, cols), lambda i, k_ref: (k_ref[0], i, 0))
    else:
        p_spec = pl.BlockSpec((tr, cols), lambda i, k_ref: (i, k_ref[0]))

    def body(k_ref, p_ref, r_ref, o_ref):
        f32 = lambda a: a.astype(F32)
        o_ref[...] = ((f32(p_ref[...]) + f32(r_ref[0])) + f32(r_ref[1])) + f32(r_ref[2])

    return pl.pallas_call(
        body,
        name="add_chips_" + name,
        grid_spec=pltpu.PrefetchScalarGridSpec(
            num_scalar_prefetch=1, grid=(rows // tr,),
            in_specs=[p_spec, pl.BlockSpec((3, tr, cols), lambda i, k_ref: (0, i, 0))],
            out_specs=pl.BlockSpec((tr, cols), lambda i, k_ref: (i, 0))),
        out_shape=jax.ShapeDtypeStruct((rows, cols), F32),
        compiler_params=_cparams(("parallel",)),
    )(mine, partial, recv)


def _share_with_sibling(halves):
    n_w = len(halves)

    def body(*refs):
        ins, outs = refs[:n_w], refs[n_w:2 * n_w]
        send_sems, recv_sems = refs[2 * n_w:]
        x, y, c = _coords()
        copies = [pltpu.make_async_remote_copy(ins[i], outs[i], send_sems.at[i], recv_sems.at[i],
                                               device_id=(x, y, 1 - c), device_id_type=MESH_ID) for i in range(n_w)]
        for cp in copies:
            cp.start()
        for cp in copies:
            cp.wait()

    return pl.pallas_call(
        body,
        name="share_with_sibling",
        in_specs=[HBM_SPEC] * n_w,
        out_specs=[HBM_SPEC] * n_w,
        out_shape=[jax.ShapeDtypeStruct(h.shape, F32) for h in halves],
        scratch_shapes=[pltpu.SemaphoreType.DMA((n_w,)), pltpu.SemaphoreType.DMA((n_w,))],
    )(*halves)


SMALL_ROWS = 168


def _small_exchange(buf):
    def ops(ins, outs, send_sems, recv_sems):
        x, y, c = _coords()
        me = 4 * x + 2 * y + c
        peers = [(x ^ fx, y ^ fy, c ^ fc) for fx in (0, 1) for fy in (0, 1) for fc in (0, 1)][1:]

        def copy(j, slot, dev):
            return pltpu.make_async_remote_copy(ins[0], outs[0].at[slot], send_sems.at[j], recv_sems.at[j],
                                                device_id=dev, device_id_type=MESH_ID)

        own = pltpu.make_async_copy(ins[0], outs[0].at[me], send_sems.at[N_DEV - 1])

        def start():
            own.start()
            for j, dev in enumerate(peers):
                copy(j, me, dev).start()

        def finish():
            for j, dev in enumerate(peers):
                copy(j, 4 * dev[0] + 2 * dev[1] + dev[2], dev).wait()
            own.wait()

        return start, lambda: None, finish

    return _Exchange([buf], [jax.ShapeDtypeStruct((N_DEV,) + buf.shape, F32)], N_DEV, ops)


def _sum_devices(gathered):
    def body(g_ref, out_ref):
        acc = g_ref[0]
        for d in range(1, N_DEV):
            acc = acc + g_ref[d]
        out_ref[...] = acc

    return pl.pallas_call(
        body,
        name="sum_devices",
        in_specs=[VMEM_SPEC],
        out_specs=VMEM_SPEC,
        out_shape=jax.ShapeDtypeStruct(gathered.shape[1:], F32),
    )(gathered)


def _adamw(w, g, m, v, copy_g=False):
    r, rest = w.shape[0], w.shape[1:]
    per_row = 1
    for dim in rest:
        per_row *= dim
    tr = _tile(r, max(8, (5 << 19) // (4 * per_row)), 8 if len(rest) == 1 else 1)

    def body(w_ref, g_ref, m_ref, v_ref, *out_refs):
        d_ref, mo_ref, vo_ref = out_refs[-3:]
        gg = g_ref[...]
        if copy_g:
            out_refs[0][...] = gg
        mm = ADAM_B1 * m_ref[...] + (1.0 - ADAM_B1) * gg
        vv = ADAM_B2 * v_ref[...] + (1.0 - ADAM_B2) * (gg * gg)
        m_hat = mm / (1.0 - ADAM_B1 ** ADAM_STEP)
        v_hat = vv / (1.0 - ADAM_B2 ** ADAM_STEP)
        d_ref[...] = -ADAM_LR * (m_hat / (jnp.sqrt(v_hat) + ADAM_EPS) + ADAM_WD * w_ref[...])
        mo_ref[...] = mm
        vo_ref[...] = vv

    n_out = 4 if copy_g else 3
    spec = pl.BlockSpec((tr,) + rest, lambda i: (i,) + (0,) * len(rest))
    return pl.pallas_call(
        body,
        name="adamw",
        grid=(r // tr,),
        in_specs=[spec] * 4,
        out_specs=[spec] * n_out,
        out_shape=[jax.ShapeDtypeStruct(w.shape, F32)] * n_out,
        compiler_params=_cparams(("parallel",)),
    )(w, g, m, v)


def _adamw_halves(w, own, other, m, v, c, name):
    r, cols = w.shape
    half = r // 2
    tr = _tile(half, 256, 8)
    nt = half // tr
    whole = pl.BlockSpec((tr, cols), lambda h, i, c_ref: (h * nt + i, 0))
    part = pl.BlockSpec((tr, cols), lambda h, i, c_ref: (i, 0))

    def body(c_ref, w_ref, own_ref, other_ref, m_ref, v_ref, g_ref, d_ref, mo_ref, vo_ref):
        gg = jnp.where(pl.program_id(0) == c_ref[0], own_ref[...], other_ref[...])
        g_ref[...] = gg
        mm = ADAM_B1 * m_ref[...] + (1.0 - ADAM_B1) * gg
        vv = ADAM_B2 * v_ref[...] + (1.0 - ADAM_B2) * (gg * gg)
        m_hat = mm / (1.0 - ADAM_B1 ** ADAM_STEP)
        v_hat = vv / (1.0 - ADAM_B2 ** ADAM_STEP)
        d_ref[...] = -ADAM_LR * (m_hat / (jnp.sqrt(v_hat) + ADAM_EPS) + ADAM_WD * w_ref[...])
        mo_ref[...] = mm
        vo_ref[...] = vv

    return pl.pallas_call(
        body,
        name="adamw_" + name,
        grid_spec=pltpu.PrefetchScalarGridSpec(
            num_scalar_prefetch=1, grid=(2, nt),
            in_specs=[whole, part, part, whole, whole], out_specs=[whole] * 4),
        out_shape=[jax.ShapeDtypeStruct((r, cols), F32)] * 4,
        compiler_params=_cparams(("parallel", "parallel")),
    )(c, w, own, other, m, v)


GATHER_FIRST = ("ffn1_w_in", "ffn1_w_out")
GATHER_PROJ = ("w_in",)
GATHER_LATE = ("w_branch_a", "w_branch_b", "w_out", "ffn2_w_in", "ffn2_w_out")


class _Comm:
    def __init__(self, shards, c_arr, mine_arr):
        self.shards, self.c, self.mine = shards, c_arr, mine_arr
        self.groups, self.halves = {}, {}
        self.by_name = {entry[0]: entry for entry in BIG}

    def small_gather(self, loss, small):
        pad_lanes = lambda a: jnp.concatenate([a, jnp.zeros((1, LANES - a.shape[1]), F32)], axis=1)
        buf = jnp.concatenate([
            small["meta_tokens"].reshape(128, LANES),
            small["ffn1_norm"].reshape(8, LANES), small["mix_norm"].reshape(8, LANES),
            small["ffn2_norm"].reshape(8, LANES), small["final_norm"].reshape(8, LANES),
            loss, pad_lanes(small["b_forget"]), pad_lanes(small["attn_sinks"]),
            jnp.zeros((SMALL_ROWS - 163, LANES), F32)], axis=0)
        return _small_exchange(buf)

    def small_gathered(self, outs):
        self.reduced = _sum_devices(outs[0])

    def gather(self, names):
        padded = (STACKED, (D_MODEL, N_CHIPS * SHARD_PAD_COLS), 1)
        table = [(padded if n == STACKED else self.by_name[n]) + (BF16, True) for n in names]
        return _gather_exchange([self.shards[n] for n in names], table)

    def swap(self, tag, grads):
        entries = [self.by_name[n] for n in grads]
        arrays = list(grads.values())
        self.groups[tag] = (entries, arrays)
        return _halves_exchange(arrays, entries)

    def scatter(self, tag, received):
        entries, arrays = self.groups[tag]
        views = [_halves_view(*entry) for entry in entries]
        partials = [_add_sibling(g.reshape(v[0]), r, self.c, name)
                    for g, v, r, (name, _, _) in zip(arrays, views, received, entries)]
        self.groups[tag] = (entries, partials)
        return _scatter_exchange(partials, entries)

    def received(self, tag, pieces):
        entries, partials = self.groups[tag]
        for p, r, (name, _, axis) in zip(partials, pieces, entries):
            self.halves[name] = _add_chips(p, r, self.mine, name, axis)

    def finish(self):
        names = [n for n, _, _ in BIG]
        own = [self.halves[n] for n in names]
        return dict(zip(names, zip(own, _share_with_sibling(own))))


def kernel(x, meta_tokens, ffn1_norm, ffn1_w_in, ffn1_w_out, mix_norm, w_in, b_forget, attn_sinks, w_branch_a, w_branch_b, w_out, ffn2_norm, ffn2_w_in, ffn2_w_out, final_norm, loss_target, m_meta_tokens, m_ffn1_norm, m_ffn1_w_in, m_ffn1_w_out, m_mix_norm, m_w_in, m_b_forget, m_attn_sinks, m_w_branch_a, m_w_branch_b, m_w_out, m_ffn2_norm, m_ffn2_w_in, m_ffn2_w_out, m_final_norm, v_meta_tokens, v_ffn1_norm, v_ffn1_w_in, v_ffn1_w_out, v_mix_norm, v_w_in, v_b_forget, v_attn_sinks, v_w_branch_a, v_w_branch_b, v_w_out, v_ffn2_norm, v_ffn2_w_in, v_ffn2_w_out, v_final_norm):
    given = dict(locals())
    names = ["meta_tokens", "ffn1_norm", "ffn1_w_in", "ffn1_w_out", "mix_norm", "w_in", "b_forget", "attn_sinks",
             "w_branch_a", "w_branch_b", "w_out", "ffn2_norm", "ffn2_w_in", "ffn2_w_out", "final_norm"]
    big_names = [n for n, _, _ in BIG]
    cx, cy, cc = _coords()
    c_arr = cc.reshape(1).astype(jnp.int32)
    mine_arr = (2 * cx + cy).reshape(1).astype(jnp.int32)

    by_name = {entry[0]: entry for entry in BIG}
    shards = {n: given[n][0].astype(BF16) for n in GATHER_FIRST}
    table = [by_name[n] + (BF16, True) for n in GATHER_FIRST] + [("meta_tokens", (N_META, D_MODEL), 1, F32, False)]
    first = _gather_exchange([shards[n] for n in GATHER_FIRST] + [meta_tokens], table)
    late_names = [n for n in big_names if n not in GATHER_FIRST]
    casts, (w1i, w1o, meta_full) = _cast_hosting([given[n][0] for n in late_names], first, "gather_first")
    shards.update(zip(late_names, casts))
    comm = _Comm(shards, c_arr, mine_arr)
    norms = (ffn1_norm, mix_norm, ffn2_norm, final_norm.reshape(1, D_MODEL))
    loss, grad_x, small, big = _local_step(x, loss_target, meta_full, norms, b_forget, attn_sinks, (w1i, w1o), comm)

    swap = comm.swap("ffn1_in", dict(ffn1_w_in=big["ffn1_w_in"]))
    last = comm.scatter("ffn1_in", _run_exchange(swap, "exchange_halves_ffn1_in"))
    comm.received("ffn1_in", _run_exchange(last, "scatter_chip_sums"))
    grad_halves = comm.finish()
    grads = {}

    red = comm.reduced
    meta_cols = red[:128].reshape(N_META, D_MODEL)
    grads["meta_tokens"] = lax.dynamic_slice_in_dim(meta_cols, (2 * cx + cy) * (D_MODEL // N_CHIPS),
                                                    D_MODEL // N_CHIPS, axis=1)
    grads["ffn1_norm"] = red[128:136].reshape(1, D_MODEL)
    grads["mix_norm"] = red[136:144].reshape(1, D_MODEL)
    grads["ffn2_norm"] = red[144:152].reshape(1, D_MODEL)
    grads["final_norm"] = red[152:160].reshape(1, D_MODEL)
    loss_out = red[160, 0]
    grads["b_forget"] = red[161:162, :B_HEADS]
    grads["attn_sinks"] = red[162:163, :A_HEADS]

    out_g, out_d, out_m, out_v = [], [], [], []
    for n in names:
        w_full = given[n]
        shape = w_full.shape
        two_d = (lambda a: a.reshape(shape[-2], shape[-1])) if len(shape) >= 2 else (lambda a: a.reshape(1, shape[0]))
        if n == STACKED:
            own, other = grad_halves[n]
            g_nat = jnp.concatenate([jnp.where(cc == 0, own, other), jnp.where(cc == 0, other, own)], axis=0)
            rows = lambda a: a.reshape(1, shape[-2], shape[-1]).transpose(2, 0, 1)
            unrows = lambda a: a.transpose(1, 2, 0)
            g2, d2, m2, v2 = [unrows(a) for a in _adamw(rows(w_full), rows(g_nat), rows(given["m_" + n]),
                                                         rows(given["v_" + n]), copy_g=True)]
        elif n in grad_halves:
            own, other = grad_halves[n]
            g2, d2, m2, v2 = _adamw_halves(two_d(w_full), own, other, two_d(given["m_" + n]),
                                           two_d(given["v_" + n]), c_arr, n)
        else:
            g2 = two_d(grads[n])
            d2, m2, v2 = _adamw(two_d(w_full), g2, two_d(given["m_" + n]), two_d(given["v_" + n]))
        out_g.append(g2.reshape(shape))
        out_d.append(d2.reshape(shape))
        out_m.append(m2.reshape(shape))
        out_v.append(v2.reshape(shape))
    return (loss_out, grad_x, *out_g, *out_d, *out_m, *out_v)
```

```python
import jax
import jax.numpy as jnp
from jax import lax
from jax.experimental import pallas as pl
from jax.experimental.pallas import tpu as pltpu

F32 = jnp.float32
BF16 = jnp.bfloat16

D_MODEL = 1024
N_META = 16
BLOCK = 128
LANES = 128
PREFIX = BLOCK
N_PAD = PREFIX - N_META
HEAD_DIM = 64
A_HEADS = 8
A_KV_HEADS = 2
A_GROUP = 4
B_HEADS = 8
B_PAIRS = B_HEADS // 2
A_WIDTH = A_HEADS * HEAD_DIM
A_KV_WIDTH = A_KV_HEADS * HEAD_DIM
B_WIDTH = B_HEADS * HEAD_DIM
W_IN_COLS = A_WIDTH + 2 * A_KV_WIDTH + 3 * B_WIDTH + B_HEADS + 2 * D_MODEL
SRC_KA = A_WIDTH
SRC_VA = SRC_KA + A_KV_WIDTH
SRC_QB = SRC_VA + A_KV_WIDTH
SRC_KB = SRC_QB + B_WIDTH
SRC_VB = SRC_KB + B_WIDTH
SRC_F = SRC_VB + B_WIDTH
SRC_GA = SRC_F + B_HEADS
SRC_GB = SRC_GA + D_MODEL
A_PAD_WIDTH = A_HEADS * LANES
B_PAD_WIDTH = B_HEADS * LANES
F_COLS = LANES
OFF_QA = 0
OFF_KA = SRC_KA
OFF_VA = SRC_VA
OFF_F = OFF_VA + A_KV_WIDTH
OFF_QB = OFF_F + F_COLS
OFF_KB = OFF_QB + B_WIDTH
OFF_VB = OFF_KB + B_WIDTH
OFF_GA = OFF_VB + B_WIDTH
OFF_GB = OFF_GA + D_MODEL
P_COLS = OFF_GB + D_MODEL
P_PIECES = ((0, OFF_QB), (OFF_QB, OFF_GA - OFF_QB), (OFF_GA, P_COLS - OFF_GA))
EPS = 1e-6
NEG = -1e30
SCALE = HEAD_DIM ** -0.5
KEY_BLOCKS = 4

ADAM_LR = 0.001
ADAM_B1 = 0.9
ADAM_B2 = 0.999
ADAM_EPS = 1e-08
ADAM_WD = 0.01
ADAM_STEP = 10

N_CHIPS = 4
N_DEV = 8
VMEM_LIMIT = 56 * 1024 * 1024

NT_DIMS = (((1,), (1,)), ((), ()))
TN_DIMS = (((0,), (0,)), ((), ()))
MESH_ID = pl.DeviceIdType.MESH
HBM_SPEC = pl.BlockSpec(memory_space=pltpu.HBM)
VMEM_SPEC = pl.BlockSpec(memory_space=pltpu.VMEM)


def _tile(n, target, mult=16):
    best = None
    for t in range(mult, min(n, target) + 1, mult):
        if n % t == 0:
            best = t
    return best if best is not None else n


def _cparams(sem):
    return pltpu.CompilerParams(dimension_semantics=sem, vmem_limit_bytes=VMEM_LIMIT)


def _rms_scale(h):
    return lax.rsqrt(jnp.mean(h * h, axis=-1, keepdims=True) + EPS)


def _rms_bwd(dn, h, w):
    r = _rms_scale(h)
    dw = jnp.sum(dn * (h * r), axis=0, keepdims=True)
    z = dn * w
    dh = r * z - h * ((r * r * r) * jnp.mean(z * h, axis=-1, keepdims=True))
    return dh, dw


def _ffn_fwd(h, norm_w, w_in, w_out, exchange=None):
    t, d = h.shape
    f = w_out.shape[0]
    tm = _tile(t, 272)
    tc = _tile(f, 256, 128)
    nj = f // tc
    ni = t // tm
    n_x = len(exchange.ins) if exchange else 0

    def body(*refs):
        h_ref, nw_ref, wi_ref, wo_ref = refs[:4]
        hout_ref, g_ref, u_ref = refs[4 + n_x:7 + n_x]
        a_scr = refs[7 + 2 * n_x]
        i = pl.program_id(0)
        if exchange:
            _host_exchange(exchange, refs[4:4 + n_x], refs[7 + n_x:7 + 2 * n_x], refs[8 + 2 * n_x:],
                           i == 0, i == ni // 3, i == ni - 1, late=i == 2 * ni // 3)
        hh = h_ref[...]
        n = ((hh * _rms_scale(hh)) * nw_ref[...]).astype(BF16)
        for j in range(nj):
            cols = slice(j * tc, (j + 1) * tc)
            g = jnp.dot(n, wi_ref[:, j * tc:(j + 1) * tc], preferred_element_type=F32)
            u = jnp.dot(n, wi_ref[:, f + j * tc:f + (j + 1) * tc], preferred_element_type=F32)
            g_ref[:, cols] = g
            u_ref[:, cols] = u
            a_scr[:, cols] = ((g * jax.nn.sigmoid(g)) * u).astype(BF16)
        hout_ref[...] = hh + 0.5 * jnp.dot(a_scr[...], wo_ref[...], preferred_element_type=F32)

    resident = lambda a: pl.BlockSpec(a.shape, lambda i: (0, 0), pipeline_mode=pl.Buffered(1))
    row = lambda w: pl.BlockSpec((tm, w), lambda i: (i, 0))
    outs = pl.pallas_call(
        body,
        name="ffn_fwd",
        grid=(ni,),
        in_specs=[row(d), pl.BlockSpec((1, d), lambda i: (0, 0)), resident(w_in), resident(w_out)] + [HBM_SPEC] * n_x,
        out_specs=[row(d), row(f), row(f)] + [HBM_SPEC] * n_x,
        out_shape=[
            jax.ShapeDtypeStruct((t, d), F32),
            jax.ShapeDtypeStruct((t, f), F32),
            jax.ShapeDtypeStruct((t, f), F32),
        ] + (exchange.out_shape if exchange else []),
        scratch_shapes=[pltpu.VMEM((tm, f), BF16)] + (exchange.scratch if exchange else []),
        compiler_params=_cparams(("arbitrary",) if exchange else ("parallel",)),
    )(h, norm_w, w_in, w_out, *(exchange.ins if exchange else []))
    return outs[:3], outs[3:]


def _ffn_bwd(dh_out, h, norm_w, g, u, w_in, w_out, exchange=None):
    t, d = h.shape
    f = w_out.shape[0]
    tm = _tile(t, 272)
    tc = _tile(f, 256, 128)
    nj = f // tc
    ni = t // tm
    n_x = len(exchange.ins) if exchange else 0

    def body(*refs):
        dho_ref, h_ref, nw_ref, g_ref, u_ref, wi_ref, wo_ref = refs[:7]
        dhin_ref, n_ref, a_ref, dgu_ref, df_ref, dnw_ref = refs[7 + n_x:13 + n_x]
        i = pl.program_id(0)
        if exchange:
            _host_exchange(exchange, refs[7:7 + n_x], refs[13 + n_x:13 + 2 * n_x], refs[13 + 2 * n_x:],
                           i == 0, i == ni - 1, i == ni - 1)
        hh = h_ref[...]
        nw = nw_ref[...]
        n_ref[...] = ((hh * _rms_scale(hh)) * nw).astype(BF16)
        dho = dho_ref[...]
        df = (0.5 * dho).astype(BF16)
        df_ref[...] = df
        for j in range(nj):
            cols = slice(j * tc, (j + 1) * tc)
            da = lax.dot_general(df, wo_ref[cols, :], NT_DIMS, preferred_element_type=F32)
            gg = g_ref[:, cols]
            uu = u_ref[:, cols]
            sig = jax.nn.sigmoid(gg)
            sl = gg * sig
            a_ref[:, cols] = (sl * uu).astype(BF16)
            dgu_ref[0, :, cols] = ((da * uu) * (sig * (1.0 + gg * (1.0 - sig)))).astype(BF16)
            dgu_ref[1, :, cols] = (da * sl).astype(BF16)
        dn = (lax.dot_general(dgu_ref[0], wi_ref[:, :f], NT_DIMS, preferred_element_type=F32)
              + lax.dot_general(dgu_ref[1], wi_ref[:, f:], NT_DIMS, preferred_element_type=F32))
        dh, dw = _rms_bwd(dn, hh, nw)
        dhin_ref[...] = dho + dh
        dnw_ref[0] = dw

    resident = lambda a: pl.BlockSpec(a.shape, lambda i: (0, 0), pipeline_mode=pl.Buffered(1))
    row = lambda w: pl.BlockSpec((tm, w), lambda i: (i, 0))
    outs = pl.pallas_call(
        body,
        name="ffn_bwd",
        grid=(ni,),
        in_specs=[row(d), row(d), pl.BlockSpec((1, d), lambda i: (0, 0)), row(f), row(f),
                  resident(w_in), resident(w_out)] + [HBM_SPEC] * n_x,
        out_specs=[row(d), row(d), row(f), pl.BlockSpec((2, tm, f), lambda i: (0, i, 0)), row(d),
                   pl.BlockSpec((1, 1, d), lambda i: (i, 0, 0))] + [HBM_SPEC] * n_x,
        out_shape=[
            jax.ShapeDtypeStruct((t, d), F32),
            jax.ShapeDtypeStruct((t, d), BF16),
            jax.ShapeDtypeStruct((t, f), BF16),
            jax.ShapeDtypeStruct((2, t, f), BF16),
            jax.ShapeDtypeStruct((t, d), BF16),
            jax.ShapeDtypeStruct((ni, 1, d), F32),
        ] + (exchange.out_shape if exchange else []),
        scratch_shapes=exchange.scratch if exchange else [],
        compiler_params=_cparams(("arbitrary",) if exchange else ("parallel",)),
    )(dh_out, h, norm_w, g, u, w_in, w_out, *(exchange.ins if exchange else []))
    return outs[:6], outs[6:]


def _tn_matmul(a, b, name, exchange=None):
    t, k = a.shape
    split = b.ndim == 3
    n = 2 * b.shape[2] if split else b.shape[1]
    tk = _tile(k, 512, 128)
    tn = _tile(b.shape[-1], 1408, 128)
    per_half = b.shape[-1] // tn
    ni, nj = k // tk, n // tn
    n_x = len(exchange.ins) if exchange else 0

    def body(*refs):
        a_ref, b_ref, o_ref = refs[0], refs[1], refs[2 + n_x]
        if exchange:
            i, j = pl.program_id(0), pl.program_id(1)
            at_end = (i == ni - 1) & (j == nj - 1)
            _host_exchange(exchange, refs[2:2 + n_x], refs[3 + n_x:3 + 2 * n_x], refs[3 + 2 * n_x:],
                           (i == 0) & (j == 0), at_end, at_end)
        o_ref[...] = lax.dot_general(a_ref[...], b_ref[...], TN_DIMS, preferred_element_type=F32)

    if split:
        b_spec = pl.BlockSpec((None, t, tn), lambda i, j: (j // per_half, 0, j % per_half))
    else:
        b_spec = pl.BlockSpec((t, tn), lambda i, j: (0, j))
    outs = pl.pallas_call(
        body,
        name=name,
        grid=(ni, nj),
        in_specs=[pl.BlockSpec((t, tk), lambda i, j: (0, i)), b_spec] + [HBM_SPEC] * n_x,
        out_specs=[pl.BlockSpec((tk, tn), lambda i, j: (i, j))] + [HBM_SPEC] * n_x,
        out_shape=[jax.ShapeDtypeStruct((k, n), F32)] + (exchange.out_shape if exchange else []),
        scratch_shapes=exchange.scratch if exchange else [],
        compiler_params=_cparams(("arbitrary", "arbitrary") if exchange else ("parallel", "parallel")),
    )(a, b, *(exchange.ins if exchange else []))
    return (outs[0], outs[1:]) if exchange else outs[0]


A_SLOT = lambda h: h // A_GROUP
B_SLOT = lambda h: h % 2

PROJ_PARTS = (
    (OFF_QA, A_WIDTH, A_PAD_WIDTH, True, A_SLOT), (OFF_KA, A_KV_WIDTH, A_KV_WIDTH, True, None),
    (OFF_VA, A_KV_WIDTH, A_KV_WIDTH, True, None), (OFF_QB, B_WIDTH, B_WIDTH, True, None),
    (OFF_KB, B_WIDTH, B_PAD_WIDTH, True, B_SLOT), (OFF_VB, B_WIDTH, B_PAD_WIDTH, True, B_SLOT),
    (OFF_GA, D_MODEL, D_MODEL, False, None), (OFF_GB, D_MODEL, D_MODEL, False, None), (OFF_F, F_COLS, F_COLS, False, None),
)


def _head_tile(pair, head, slot):
    lane_slot = lax.broadcasted_iota(jnp.int32, pair.shape, 1) // HEAD_DIM
    moved = pair if head % 2 == slot else pltpu.roll(pair, HEAD_DIM, 1)
    return jnp.where(lane_slot == slot, moved, 0.0)


def _proj_fwd(h, norm_w, w_p):
    t, d = h.shape
    tm = _tile(t, 272)

    def body(h_ref, nw_ref, w_ref, u_ref, *part_refs):
        hh = h_ref[...]
        un = ((hh * _rms_scale(hh)) * nw_ref[...]).astype(BF16)
        u_ref[...] = un
        for (off, width, _, _, slot), p_ref in zip(PROJ_PARTS, part_refs):
            if slot is None:
                p_ref[...] = jnp.dot(un, w_ref[:, off:off + width], preferred_element_type=F32).astype(p_ref.dtype)
                continue
            part = jnp.dot(un, w_ref[:, off:off + width], preferred_element_type=F32)
            for pair in range(width // LANES):
                x = part[:, pair * LANES:(pair + 1) * LANES]
                for head in (2 * pair, 2 * pair + 1):
                    p_ref[:, head * LANES:(head + 1) * LANES] = _head_tile(x, head, slot(head)).astype(p_ref.dtype)

    row = lambda w: pl.BlockSpec((tm, w), lambda i: (i, 0))
    return pl.pallas_call(
        body,
        name="proj_fwd",
        grid=(t // tm,),
        in_specs=[row(d), pl.BlockSpec((1, d), lambda i: (0, 0)),
                  pl.BlockSpec(w_p.shape, lambda i: (0, 0), pipeline_mode=pl.Buffered(1))],
        out_specs=[row(d)] + [row(width) for _, _, width, _, _ in PROJ_PARTS],
        out_shape=[jax.ShapeDtypeStruct((t, d), BF16)]
        + [jax.ShapeDtypeStruct((t, width), BF16 if is_bf else F32) for _, _, width, is_bf, _ in PROJ_PARTS],
        compiler_params=_cparams(("parallel",)),
    )(h, norm_w, w_p)


def _proj_bwd(dh_out, h, norm_w, dproj, w_p, exchange=None):
    t, d = h.shape
    tm = _tile(t, 272)
    ni = t // tm
    n_p = len(P_PIECES)
    n_in = 4 + n_p
    n_x = len(exchange.ins) if exchange else 0

    def body(*refs):
        dho_ref, h_ref, nw_ref = refs[:3]
        dp_refs, w_ref = refs[3:3 + n_p], refs[3 + n_p]
        dhin_ref, dnw_ref = refs[n_in + n_x:n_in + 2 + n_x]
        if exchange:
            i = pl.program_id(0)
            _host_exchange(exchange, refs[n_in:n_in + n_x], refs[n_in + 2 + n_x:n_in + 2 + 2 * n_x],
                           refs[n_in + 2 + 2 * n_x:], i == 0, i == ni - 1, i == ni - 1)
        dn = None
        for dp_ref, (off, width) in zip(dp_refs, P_PIECES):
            part = lax.dot_general(dp_ref[...], w_ref[:, off:off + width], NT_DIMS, preferred_element_type=F32)
            dn = part if dn is None else dn + part
        dh, dw = _rms_bwd(dn, h_ref[...], nw_ref[...])
        dhin_ref[...] = dho_ref[...] + dh
        dnw_ref[0] = dw

    row = lambda w: pl.BlockSpec((tm, w), lambda i: (i, 0))
    outs = pl.pallas_call(
        body,
        name="proj_bwd",
        grid=(ni,),
        in_specs=[row(d), row(d), pl.BlockSpec((1, d), lambda i: (0, 0))] + [row(width) for _, width in P_PIECES]
        + [pl.BlockSpec(w_p.shape, lambda i: (0, 0), pipeline_mode=pl.Buffered(1))] + [HBM_SPEC] * n_x,
        out_specs=[row(d), pl.BlockSpec((1, 1, d), lambda i: (i, 0, 0))] + [HBM_SPEC] * n_x,
        out_shape=[jax.ShapeDtypeStruct((t, d), F32), jax.ShapeDtypeStruct((ni, 1, d), F32)]
        + (exchange.out_shape if exchange else []),
        scratch_shapes=exchange.scratch if exchange else [],
        compiler_params=_cparams(("arbitrary",) if exchange else ("parallel",)),
    )(dh_out, h, norm_w, *dproj, w_p, *(exchange.ins if exchange else []))
    return outs[:2], outs[2:]


def _merge_fwd(h, oa, ob, ga, gb, wa, wb, wo):
    t, d = h.shape
    tm = _tile(t, 544)

    def body(h_ref, oa_ref, ob_ref, ga_ref, gb_ref, wa_ref, wb_ref, wo_ref, hout_ref, mix_ref):
        ya = jnp.dot(oa_ref[...], wa_ref[...], preferred_element_type=F32)
        yb = jnp.dot(ob_ref[...], wb_ref[...], preferred_element_type=F32)
        mixed = (jax.nn.sigmoid(ga_ref[...]) * ya + jax.nn.sigmoid(gb_ref[...]) * yb).astype(BF16)
        mix_ref[...] = mixed
        hout_ref[...] = h_ref[...] + jnp.dot(mixed, wo_ref[...], preferred_element_type=F32)

    row = lambda w: pl.BlockSpec((tm, w), lambda i: (i, 0))
    full = lambda a: pl.BlockSpec(a.shape, lambda i: (0, 0))
    return pl.pallas_call(
        body,
        name="merge_fwd",
        grid=(t // tm,),
        in_specs=[row(d), row(oa.shape[1]), row(ob.shape[1]), row(d), row(d), full(wa), full(wb), full(wo)],
        out_specs=[row(d), row(d)],
        out_shape=[jax.ShapeDtypeStruct((t, d), F32), jax.ShapeDtypeStruct((t, d), BF16)],
        compiler_params=_cparams(("parallel",)),
    )(h, oa, ob, ga, gb, wa, wb, wo)


def _merge_bwd(dh, oa, ob, ga, gb, wa, wb, wo, exchange=None):
    t, d = dh.shape
    tm = _tile(t, 544)
    ni = t // tm
    n_x = len(exchange.ins) if exchange else 0

    def body(*refs):
        dh_ref, oa_ref, ob_ref, ga_ref, gb_ref, wa_ref, wb_ref, wo_ref = refs[:8]
        dya_ref, dyb_ref, doa_ref, dob_ref, dg_ref, dhb_ref = refs[8 + n_x:14 + n_x]
        if exchange:
            i = pl.program_id(0)
            _host_exchange(exchange, refs[8:8 + n_x], refs[14 + n_x:14 + 2 * n_x], refs[14 + 2 * n_x:],
                           i == 0, i == ni - 1, i == ni - 1)
        dhb = dh_ref[...].astype(BF16)
        dhb_ref[...] = dhb
        dmix = lax.dot_general(dhb, wo_ref[...], NT_DIMS, preferred_element_type=F32)
        for branch, (o_ref, g_ref, w_ref, dy_ref, do_ref) in enumerate((
                (oa_ref, ga_ref, wa_ref, dya_ref, doa_ref),
                (ob_ref, gb_ref, wb_ref, dyb_ref, dob_ref))):
            y = jnp.dot(o_ref[...], w_ref[...], preferred_element_type=F32)
            s = jax.nn.sigmoid(g_ref[...])
            dy = (dmix * s).astype(BF16)
            dy_ref[...] = dy
            dg_ref[:, branch * d:(branch + 1) * d] = ((dmix * y) * (s * (1.0 - s))).astype(BF16)
            do_ref[...] = lax.dot_general(dy, w_ref[...], NT_DIMS, preferred_element_type=F32).astype(BF16)

    row = lambda w: pl.BlockSpec((tm, w), lambda i: (i, 0))
    full = lambda a: pl.BlockSpec(a.shape, lambda i: (0, 0))
    wa_w, wb_w = oa.shape[1], ob.shape[1]
    outs = pl.pallas_call(
        body,
        name="merge_bwd",
        grid=(ni,),
        in_specs=[row(d), row(wa_w), row(wb_w), row(d), row(d), full(wa), full(wb), full(wo)] + [HBM_SPEC] * n_x,
        out_specs=[row(d), row(d), row(wa_w), row(wb_w), row(2 * d), row(d)] + [HBM_SPEC] * n_x,
        out_shape=[
            jax.ShapeDtypeStruct((t, d), BF16), jax.ShapeDtypeStruct((t, d), BF16),
            jax.ShapeDtypeStruct((t, wa_w), BF16), jax.ShapeDtypeStruct((t, wb_w), BF16),
            jax.ShapeDtypeStruct((t, 2 * d), BF16), jax.ShapeDtypeStruct((t, d), BF16),
        ] + (exchange.out_shape if exchange else []),
        scratch_shapes=exchange.scratch if exchange else [],
        compiler_params=_cparams(("arbitrary",) if exchange else ("parallel",)),
    )(dh, oa, ob, ga, gb, wa, wb, wo, *(exchange.ins if exchange else []))
    return outs[:6], outs[6:]


def _tri_dot(tri, x):
    hi = x.astype(BF16)
    r1 = x - hi.astype(F32)
    mid = r1.astype(BF16)
    lo = (r1 - mid.astype(F32)).astype(BF16)
    return (jnp.dot(tri, hi, preferred_element_type=F32)
            + jnp.dot(tri, mid, preferred_element_type=F32)
            + jnp.dot(tri, lo, preferred_element_type=F32))


def _forget_cumsum(f_logit, b_pad, nb):
    t, w = f_logit.shape
    bsz = t // (nb * BLOCK)

    def body(f_ref, b_ref, c_ref, carry):
        n = pl.program_id(1)

        @pl.when(n == 0)
        def _():
            carry[...] = jnp.zeros_like(carry)

        x = jax.nn.log_sigmoid(f_ref[...] + b_ref[...])
        rows = lax.broadcasted_iota(jnp.int32, (BLOCK, BLOCK), 0)
        cols = lax.broadcasted_iota(jnp.int32, (BLOCK, BLOCK), 1)
        tri = (cols <= rows).astype(BF16)
        c = _tri_dot(tri, x) + carry[...]
        c_ref[...] = c
        carry[...] = c[BLOCK - 1:BLOCK, :]

    return pl.pallas_call(
        body,
        name="forget_cumsum",
        grid=(bsz, nb),
        in_specs=[pl.BlockSpec((BLOCK, w), lambda b, n: (b * nb + n, 0)),
                  pl.BlockSpec((1, w), lambda b, n: (0, 0))],
        out_specs=pl.BlockSpec((BLOCK, w), lambda b, n: (b * nb + n, 0)),
        out_shape=jax.ShapeDtypeStruct((t, w), F32),
        scratch_shapes=[pltpu.VMEM((1, w), F32)],
        compiler_params=_cparams(("parallel", "arbitrary")),
    )(f_logit, b_pad)


def _forget_cumsum_bwd(dc, f_logit, b_pad, nb):
    t, w = f_logit.shape
    bsz = t // (nb * BLOCK)

    def body(dc_ref, f_ref, b_ref, df_ref, db_ref, carry):
        n = pl.program_id(1)

        @pl.when(n == 0)
        def _():
            carry[...] = jnp.zeros_like(carry)
            db_ref[...] = jnp.zeros_like(db_ref)

        rows = lax.broadcasted_iota(jnp.int32, (BLOCK, BLOCK), 0)
        cols = lax.broadcasted_iota(jnp.int32, (BLOCK, BLOCK), 1)
        tri = (cols >= rows).astype(BF16)
        dlf = _tri_dot(tri, dc_ref[...]) + carry[...]
        carry[...] = dlf[0:1, :]
        df = dlf * jax.nn.sigmoid(-(f_ref[...] + b_ref[...]))
        df_ref[...] = df.astype(BF16)
        db_ref[0] += jnp.sum(df, axis=0, keepdims=True)

    rev = lambda b, n: (b * nb + (nb - 1 - n), 0)
    return pl.pallas_call(
        body,
        name="forget_cumsum_bwd",
        grid=(bsz, nb),
        in_specs=[pl.BlockSpec((BLOCK, w), rev),
                  pl.BlockSpec((BLOCK, w), rev),
                  pl.BlockSpec((1, w), lambda b, n: (0, 0))],
        out_specs=[pl.BlockSpec((BLOCK, w), rev),
                   pl.BlockSpec((1, 1, w), lambda b, n: (b, 0, 0))],
        out_shape=[jax.ShapeDtypeStruct((t, w), BF16), jax.ShapeDtypeStruct((bsz, 1, w), F32)],
        scratch_shapes=[pltpu.VMEM((1, w), F32)],
        compiler_params=_cparams(("parallel", "arbitrary")),
    )(dc, f_logit, b_pad)


GROUP_ROWS = A_GROUP * BLOCK


def _stack_heads(ref, g):
    return jnp.concatenate([ref[:, (A_GROUP * g + i) * LANES:(A_GROUP * g + i + 1) * LANES] for i in range(A_GROUP)],
                           axis=0)


def _unstack_heads(ref, g, x):
    for i in range(A_GROUP):
        ref[:, (A_GROUP * g + i) * LANES:(A_GROUP * g + i + 1) * LANES] = x[i * BLOCK:(i + 1) * BLOCK].astype(ref.dtype)


def _swa_logits(qk, slope, n):
    qi = lax.broadcasted_iota(jnp.int32, (GROUP_ROWS, BLOCK), 0) & (BLOCK - 1)
    kj = lax.broadcasted_iota(jnp.int32, (GROUP_ROWS, BLOCK), 1)
    s_all = qk * SCALE
    out = []
    for i, (dist, ok) in enumerate((
            (n * BLOCK + qi - kj, (kj >= N_PAD) & (n * BLOCK + qi - kj >= 0)),
            (BLOCK + qi - kj, (kj > qi) & (n >= 2)),
            (qi - kj, (kj <= qi) & (n >= 1)))):
        s = s_all[:, i * BLOCK:(i + 1) * BLOCK] - slope * dist.astype(F32)
        out.append(jnp.where(ok, s, NEG))
    return out


def _three_blocks(m_ref, p_ref, c_ref):
    return jnp.concatenate([m_ref[...], p_ref[...], c_ref[...]], axis=0)


def _swa_specs(bsz, nb):
    qspec = pl.BlockSpec((bsz, BLOCK, A_PAD_WIDTH), lambda n: (0, n, 0))
    kv_m = pl.BlockSpec((bsz, BLOCK, LANES), lambda n: (0, 0, 0))
    kv_p = pl.BlockSpec((bsz, BLOCK, LANES), lambda n: (0, jnp.maximum(n - 1, 0), 0))
    kv_c = pl.BlockSpec((bsz, BLOCK, LANES), lambda n: (0, n, 0))
    rowspec = pl.BlockSpec((A_KV_HEADS, GROUP_ROWS, 1), lambda n: (0, 0, 0))
    lsespec = pl.BlockSpec((bsz, 1, A_KV_HEADS, GROUP_ROWS, 1), lambda n: (0, n, 0, 0, 0))
    return qspec, kv_m, kv_p, kv_c, rowspec, lsespec


def _swa_fwd(q, k, v, sink_rows, slope_rows, nb):
    t = q.shape[0]
    l = nb * BLOCK
    bsz = t // l

    def body(q_ref, km_ref, kp_ref, kc_ref, vm_ref, vp_ref, vc_ref, sink_ref, slope_ref, o_ref, lse_ref):
        n = pl.program_id(0)
        lane_group = lax.broadcasted_iota(jnp.int32, (GROUP_ROWS, LANES), 1) // HEAD_DIM
        products = {}
        for b in range(bsz):
            keys = _three_blocks(km_ref.at[b], kp_ref.at[b], kc_ref.at[b])
            for g in range(A_KV_HEADS):
                products[b, g] = lax.dot_general(_stack_heads(q_ref.at[b], g), keys, NT_DIMS,
                                                 preferred_element_type=F32)
        for b in range(bsz):
            values = _three_blocks(vm_ref.at[b], vp_ref.at[b], vc_ref.at[b])
            for g in range(A_KV_HEADS):
                sink = sink_ref[g]
                s_m, s_p, s_c = _swa_logits(products[b, g], slope_ref[g], n)
                m = jnp.maximum(jnp.max(jnp.maximum(jnp.maximum(s_m, s_p), s_c), axis=-1, keepdims=True), sink)
                m_wide = jnp.broadcast_to(m, (GROUP_ROWS, BLOCK))
                e_m = jnp.exp(s_m - m_wide)
                e_p = jnp.exp(s_p - m_wide)
                e_c = jnp.exp(s_c - m_wide)
                z = jnp.sum((e_m + e_p) + e_c, axis=-1, keepdims=True) + jnp.exp(sink - m)
                inv = jnp.broadcast_to(1.0 / z, (GROUP_ROWS, BLOCK))
                probs = jnp.concatenate([(e_m * inv).astype(BF16), (e_p * inv).astype(BF16),
                                         (e_c * inv).astype(BF16)], axis=1)
                o = jnp.dot(probs, values, preferred_element_type=F32)
                _unstack_heads(o_ref.at[b], g, jnp.where(lane_group == g, o, 0.0))
                lse_ref[b, 0, g] = m + jnp.log(z)

    qspec, kv_m, kv_p, kv_c, rowspec, lsespec = _swa_specs(bsz, nb)
    by_example = lambda a: a.reshape(bsz, l, a.shape[1])
    q3, k3, v3 = by_example(q), by_example(k), by_example(v)
    o, lse = pl.pallas_call(
        body,
        name="swa_fwd",
        grid=(nb,),
        in_specs=[qspec, kv_m, kv_p, kv_c, kv_m, kv_p, kv_c, rowspec, rowspec],
        out_specs=[qspec, lsespec],
        out_shape=[jax.ShapeDtypeStruct((bsz, l, A_PAD_WIDTH), BF16),
                   jax.ShapeDtypeStruct((bsz, nb, A_KV_HEADS, GROUP_ROWS, 1), F32)],
        compiler_params=_cparams(("arbitrary",)),
    )(q3, k3, k3, k3, v3, v3, v3, sink_rows, slope_rows)
    return o.reshape(t, A_PAD_WIDTH), lse


def _swa_bwd(q, k, v, do, lse, sink_rows, slope_rows, nb):
    t = q.shape[0]
    l = nb * BLOCK
    bsz = t // l

    def body(q_ref, km_ref, kp_ref, kc_ref, vm_ref, vp_ref, vc_ref, do_ref, lse_ref, sink_ref, slope_ref,
             dq_ref, dk_ref, dv_ref, dsink_ref, dk_acc, dv_acc):
        n = pl.program_id(0)

        @pl.when(n == 0)
        def _():
            dk_acc[...] = jnp.zeros_like(dk_acc)
            dv_acc[...] = jnp.zeros_like(dv_acc)
            dsink_ref[...] = jnp.zeros_like(dsink_ref)

        first_half = lax.broadcasted_iota(jnp.int32, (BLOCK, LANES), 1) < HEAD_DIM
        prev = jnp.maximum(n - 1, 0)
        products = {}
        for b in range(bsz):
            keys = _three_blocks(km_ref.at[b], kp_ref.at[b], kc_ref.at[b])
            values = _three_blocks(vm_ref.at[b], vp_ref.at[b], vc_ref.at[b])
            for g in range(A_KV_HEADS):
                products[b, g] = (
                    lax.dot_general(_stack_heads(q_ref.at[b], g), keys, NT_DIMS, preferred_element_type=F32),
                    lax.dot_general(_stack_heads(do_ref.at[b], g), values, NT_DIMS, preferred_element_type=F32))
        for b in range(bsz):
            keys = _three_blocks(km_ref.at[b], kp_ref.at[b], kc_ref.at[b])
            for g in range(A_KV_HEADS):
                qq = _stack_heads(q_ref.at[b], g)
                dob = _stack_heads(do_ref.at[b], g)
                lse_g = lse_ref[b, 0, g]
                lse_wide = jnp.broadcast_to(lse_g, (GROUP_ROWS, BLOCK))
                qk, dp_all = products[b, g]
                probs = [jnp.exp(s - lse_wide) for s in _swa_logits(qk, slope_ref[g], n)]
                dps = [dp_all[:, i * BLOCK:(i + 1) * BLOCK] for i in range(3)]
                delta = jnp.sum((probs[0] * dps[0] + probs[1] * dps[1]) + probs[2] * dps[2], axis=-1, keepdims=True)
                delta_wide = jnp.broadcast_to(delta, (GROUP_ROWS, BLOCK))
                ds = jnp.concatenate([(p * (dp - delta_wide)).astype(BF16) for p, dp in zip(probs, dps)], axis=1)
                pb = jnp.concatenate([p.astype(BF16) for p in probs], axis=1)
                dq = jnp.dot(ds, keys, preferred_element_type=F32) * SCALE
                dk_all = lax.dot_general(ds, qq, TN_DIMS, preferred_element_type=F32) * SCALE
                dv_all = lax.dot_general(pb, dob, TN_DIMS, preferred_element_type=F32)
                for i, start in enumerate((0, prev * BLOCK, n * BLOCK)):
                    rows = pl.ds(pl.multiple_of(start, BLOCK), BLOCK)
                    dk_acc[b, rows, :] += dk_all[i * BLOCK:(i + 1) * BLOCK]
                    dv_acc[b, rows, :] += dv_all[i * BLOCK:(i + 1) * BLOCK]
                for pair in range(A_GROUP // 2):
                    even = dq[2 * pair * BLOCK:(2 * pair + 1) * BLOCK]
                    odd = dq[(2 * pair + 1) * BLOCK:(2 * pair + 2) * BLOCK]
                    left = even if g == 0 else pltpu.roll(even, HEAD_DIM, 1)
                    right = pltpu.roll(odd, HEAD_DIM, 1) if g == 0 else odd
                    tile = (A_GROUP // 2) * g + pair
                    dq_ref[b, :, tile * LANES:(tile + 1) * LANES] = jnp.where(first_half, left, right).astype(BF16)
                dsink_ref[b, g] += -(jnp.exp(sink_ref[g] - lse_g) * delta)

        @pl.when(n == nb - 1)
        def _():
            dk_ref[...] = dk_acc[...].astype(BF16)
            dv_ref[...] = dv_acc[...].astype(BF16)

    qspec, kv_m, kv_p, kv_c, rowspec, lsespec = _swa_specs(bsz, nb)
    kv_all = pl.BlockSpec((bsz, l, LANES), lambda n: (0, 0, 0))
    by_example = lambda a: a.reshape(bsz, l, a.shape[1])
    q3, k3, v3 = by_example(q), by_example(k), by_example(v)
    dq, dk, dv, dsink = pl.pallas_call(
        body,
        name="swa_bwd",
        grid=(nb,),
        in_specs=[qspec, kv_m, kv_p, kv_c, kv_m, kv_p, kv_c, qspec, lsespec, rowspec, rowspec],
        out_specs=[pl.BlockSpec((bsz, BLOCK, A_WIDTH), lambda n: (0, n, 0)), kv_all, kv_all,
                   pl.BlockSpec((bsz, A_KV_HEADS, GROUP_ROWS, 1), lambda n: (0, 0, 0, 0))],
        out_shape=[jax.ShapeDtypeStruct((bsz, l, A_WIDTH), BF16),
                   jax.ShapeDtypeStruct((bsz, l, LANES), BF16),
                   jax.ShapeDtypeStruct((bsz, l, LANES), BF16),
                   jax.ShapeDtypeStruct((bsz, A_KV_HEADS, GROUP_ROWS, 1), F32)],
        scratch_shapes=[pltpu.VMEM((bsz, l, LANES), F32), pltpu.VMEM((bsz, l, LANES), F32)],
        compiler_params=_cparams(("arbitrary",)),
    )(q3, k3, k3, k3, v3, v3, v3, by_example(do), lse, sink_rows, slope_rows)
    return dq.reshape(t, A_WIDTH), dk.reshape(t, LANES), dv.reshape(t, LANES), dsink


CHUNK = KEY_BLOCKS * BLOCK


def _fox_chunk(qb, ci):
    sb = jnp.maximum(jnp.minimum(KEY_BLOCKS * ci, qb + 1 - KEY_BLOCKS), 0)
    lo = jnp.maximum(ci * CHUNK, N_PAD)
    return sb, lo, pl.ds(pl.multiple_of(sb * BLOCK, BLOCK), CHUNK)


def _fox_logits(s_ref, cr_ref, e, j, sb, lo, qb):
    lane = lax.broadcasted_iota(jnp.int32, (BLOCK, BLOCK), 1)
    ahead = lane - lax.broadcasted_iota(jnp.int32, (BLOCK, BLOCK), 0)
    first = (sb + j) * BLOCK
    s = s_ref[e, :, j * BLOCK:(j + 1) * BLOCK] - cr_ref[e, sb + j]
    return jnp.where((ahead <= qb * BLOCK - first) & (lane >= lo - first), s, NEG)


FOX_PAIRS = 4
FOX_HEADS = 2 * FOX_PAIRS
FOX_STEPS = B_PAIRS // FOX_PAIRS


def _fox_specs(nb):
    l = nb * BLOCK
    q_spec = pl.BlockSpec((BLOCK, FOX_PAIRS * LANES), lambda b, p, i: (b * nb + i, p))
    kv_spec = pl.BlockSpec((l, FOX_HEADS * LANES), lambda b, p, i: (b, p))
    cc_spec = pl.BlockSpec((FOX_HEADS, BLOCK, 1), lambda b, p, i: (b * FOX_STEPS + p, i, 0))
    cr_spec = pl.BlockSpec((FOX_HEADS, nb, 1, BLOCK), lambda b, p, i: (b * FOX_STEPS + p, 0, 0, 0))
    return q_spec, kv_spec, cc_spec, cr_spec


def _fox_fwd(q, k, v, c_row, nb, exchange=None):
    t = q.shape[0]
    bsz = t // (nb * BLOCK)
    assert nb >= KEY_BLOCKS

    n_x = len(exchange.ins) if exchange else 0

    def body(*refs):
        q_ref, k_ref, v_ref, cr_ref = refs[:4]
        o_ref, ox_ref, lse_ref = refs[4 + n_x:7 + n_x]
        s_scr, hi_scr, lo_scr = refs[7 + 2 * n_x:10 + 2 * n_x]
        qb = pl.program_id(2)
        if exchange:
            first = (pl.program_id(0) == 0) & (pl.program_id(1) == 0)
            last = (pl.program_id(0) == bsz - 1) & (pl.program_id(1) == FOX_STEPS - 1)
            _host_exchange(exchange, refs[4:4 + n_x], refs[7 + n_x:7 + 2 * n_x], refs[10 + 2 * n_x:],
                           first & (qb == 0), first & (qb == 2 * nb // 3), last & (qb == nb - 1),
                           late=last & (qb == 0))
        qs = [q_ref[:, a * LANES:(a + 1) * LANES] * SCALE for a in range(FOX_PAIRS)]
        first_half = lax.broadcasted_iota(jnp.int32, (BLOCK, LANES), 1) < HEAD_DIM

        def step(ci, carry):
            stats, accs = carry[:2 * FOX_HEADS], carry[2 * FOX_HEADS:]
            sb, lo, krows = _fox_chunk(qb, ci)
            for e in range(FOX_HEADS):
                s_scr[e] = lax.dot_general(qs[e // 2], k_ref[krows, e * LANES:(e + 1) * LANES], NT_DIMS,
                                           preferred_element_type=F32)
            new_stats, new_accs = [], []
            for a in range(FOX_PAIRS):
                alphas = []
                pv = jnp.zeros((BLOCK, LANES), F32)
                pv_lo = jnp.zeros((BLOCK, LANES), F32)
                for e in (2 * a, 2 * a + 1):
                    m, z = stats[2 * e], stats[2 * e + 1]
                    tile = slice(e * LANES, (e + 1) * LANES)
                    top = None
                    for j in range(KEY_BLOCKS):
                        s = _fox_logits(s_scr, cr_ref, e, j, sb, lo, qb)
                        s_scr[e, :, j * BLOCK:(j + 1) * BLOCK] = s
                        top = s if top is None else jnp.maximum(top, s)
                    m_new = jnp.maximum(m, jnp.max(top, axis=-1, keepdims=True))
                    alpha = jnp.exp(m - m_new)
                    m_wide = jnp.broadcast_to(m_new, (BLOCK, BLOCK))
                    total = None
                    for j in range(KEY_BLOCKS):
                        cols = slice(j * BLOCK, (j + 1) * BLOCK)
                        p = jnp.exp(s_scr[e, :, cols] - m_wide)
                        total = p if total is None else total + p
                        hi = p.astype(BF16)
                        hi_scr[e, :, cols] = hi
                        lo_scr[e, :, cols] = (p - hi.astype(F32)).astype(BF16)
                    z = alpha * z + jnp.sum(total, axis=-1, keepdims=True)
                    vv = v_ref[krows, tile]
                    pv = pv + jnp.dot(hi_scr[e], vv, preferred_element_type=F32)
                    pv_lo = pv_lo + jnp.dot(lo_scr[e], vv, preferred_element_type=F32)
                    new_stats += [m_new, z]
                    alphas.append(alpha)
                alpha = jnp.where(first_half, alphas[0], alphas[1])
                new_accs += [alpha * accs[2 * a] + pv, alpha * accs[2 * a + 1] + pv_lo]
            return (*new_stats, *new_accs)

        col = lambda val: jnp.full((BLOCK, 1), val, F32)
        done = lax.fori_loop(
            0, (qb + KEY_BLOCKS) // KEY_BLOCKS, step,
            (col(NEG), col(0.0)) * FOX_HEADS + (jnp.zeros((BLOCK, LANES), F32),) * (2 * FOX_PAIRS))
        for a in range(FOX_PAIRS):
            m0, z0, m1, z1 = done[4 * a:4 * a + 4]
            acc, acc_lo = done[2 * FOX_HEADS + 2 * a:2 * FOX_HEADS + 2 * a + 2]
            inv = 1.0 / jnp.where(first_half, z0, z1)
            tile = slice(a * LANES, (a + 1) * LANES)
            o_ref[:, tile] = (acc * inv).astype(BF16)
            ox_ref[:, tile] = (acc + acc_lo) * inv
            lse_ref[2 * a] = m0 + jnp.log(z0)
            lse_ref[2 * a + 1] = m1 + jnp.log(z1)

    q_spec, kv_spec, cc_spec, cr_spec = _fox_specs(nb)
    outs = pl.pallas_call(
        body,
        name="fox_fwd",
        grid=(bsz, FOX_STEPS, nb),
        in_specs=[q_spec, kv_spec, kv_spec, cr_spec] + [HBM_SPEC] * n_x,
        out_specs=[q_spec, q_spec, cc_spec] + [HBM_SPEC] * n_x,
        out_shape=[jax.ShapeDtypeStruct((t, B_WIDTH), BF16), jax.ShapeDtypeStruct((t, B_WIDTH), F32),
                   jax.ShapeDtypeStruct((bsz * B_HEADS, nb * BLOCK, 1), F32)] + (exchange.out_shape if exchange else []),
        scratch_shapes=[pltpu.VMEM((FOX_HEADS, BLOCK, CHUNK), F32), pltpu.VMEM((FOX_HEADS, BLOCK, CHUNK), BF16),
                        pltpu.VMEM((FOX_HEADS, BLOCK, CHUNK), BF16)] + (exchange.scratch if exchange else []),
        compiler_params=_cparams(("arbitrary",) * 3 if exchange else ("parallel", "parallel", "arbitrary")),
    )(q, k, v, c_row, *(exchange.ins if exchange else []))
    return outs[:3], outs[3:]


def _fox_bwd(q, k, v, o_exact, do, lse, c_row, nb, exchange=None):
    t = q.shape[0]
    l = nb * BLOCK
    bsz = t // l

    n_x = len(exchange.ins) if exchange else 0

    def body(*refs):
        q_ref, k_ref, v_ref, ox_ref, do_ref, lse_ref, cr_ref = refs[:7]
        dq_ref, dk_ref, dv_ref, dc_ref = refs[7 + n_x:11 + n_x]
        dk_acc, dv_acc, s_scr, dp_scr, p_scr, ds_scr, dq_scr = refs[11 + 2 * n_x:18 + 2 * n_x]
        qb = pl.program_id(2)
        if exchange:
            first = (pl.program_id(0) == 0) & (pl.program_id(1) == 0)
            last = (pl.program_id(0) == bsz - 1) & (pl.program_id(1) == FOX_STEPS - 1)
            _host_exchange(exchange, refs[7:7 + n_x], refs[11 + n_x:11 + 2 * n_x], refs[18 + 2 * n_x:],
                           first & (qb == 0), last & (qb == 0), last & (qb == nb - 1))

        @pl.when(qb == 0)
        def _():
            dk_acc[...] = jnp.zeros_like(dk_acc)
            dv_acc[...] = jnp.zeros_like(dv_acc)
            dc_ref[...] = jnp.zeros_like(dc_ref)

        top_half = lax.broadcasted_iota(jnp.int32, (LANES, BLOCK), 0) < HEAD_DIM
        pair_t = lambda x: jnp.concatenate([jnp.where(top_half, x.T, 0), jnp.where(top_half, 0, x.T)], axis=1)
        first_half = lax.broadcasted_iota(jnp.int32, (BLOCK, LANES), 1) < HEAD_DIM
        wide = lambda col: jnp.broadcast_to(col, (BLOCK, BLOCK))
        qs, dobs, qs_t, dob_t, deltas = [], [], [], [], []
        for a in range(FOX_PAIRS):
            tile = slice(a * LANES, (a + 1) * LANES)
            qs.append(q_ref[:, tile] * SCALE)
            dobs.append(do_ref[:, tile])
            qs_t.append(pair_t(qs[a]))
            dob_t.append(pair_t(dobs[a]))
            weighted = dobs[a].astype(F32) * ox_ref[:, tile]
            deltas += [wide(jnp.sum(jnp.where(first_half, weighted, 0.0), axis=-1, keepdims=True)),
                       wide(jnp.sum(jnp.where(first_half, 0.0, weighted), axis=-1, keepdims=True))]
        lses = [wide(lse_ref[e]) for e in range(FOX_HEADS)]

        dq_scr[...] = jnp.zeros(dq_scr.shape, F32)

        def step(ci, carry):
            sb, lo, krows = _fox_chunk(qb, ci)
            for e in range(FOX_HEADS):
                tile = slice(e * LANES, (e + 1) * LANES)
                s_scr[e] = lax.dot_general(qs[e // 2], k_ref[krows, tile], NT_DIMS, preferred_element_type=F32)
                dp_scr[e] = lax.dot_general(dobs[e // 2], v_ref[krows, tile], NT_DIMS, preferred_element_type=F32)
            for a in range(FOX_PAIRS):
                for e in (2 * a, 2 * a + 1):
                    tile = slice(e * LANES, (e + 1) * LANES)
                    kk = k_ref[krows, tile]
                    for j in range(KEY_BLOCKS):
                        cols = slice(j * BLOCK, (j + 1) * BLOCK)
                        p = jnp.exp(_fox_logits(s_scr, cr_ref, e, j, sb, lo, qb) - lses[e])
                        ds = p * (dp_scr[e, :, cols] - deltas[e])
                        dc_ref[e, sb + j] -= jnp.sum(ds, axis=0, keepdims=True)
                        p_scr[e, :, cols] = p.astype(BF16)
                        ds_scr[e, :, cols] = ds.astype(BF16)
                    dq_scr[a] += jnp.dot(ds_scr[e], kk, preferred_element_type=F32)
                both = slice(2 * a, 2 * a + 2)
                dk_t = jnp.dot(qs_t[a], ds_scr[both].reshape(2 * BLOCK, CHUNK), preferred_element_type=F32)
                dv_t = jnp.dot(dob_t[a], p_scr[both].reshape(2 * BLOCK, CHUNK), preferred_element_type=F32)
                for j in range(KEY_BLOCKS):
                    cols = slice(j * BLOCK, (j + 1) * BLOCK)
                    dk_acc[a * nb + sb + j] += dk_t[:, cols]
                    dv_acc[a * nb + sb + j] += dv_t[:, cols]
            return carry

        lax.fori_loop(0, (qb + KEY_BLOCKS) // KEY_BLOCKS, step, 0)
        for a in range(FOX_PAIRS):
            dq_ref[:, a * LANES:(a + 1) * LANES] = (dq_scr[a] * SCALE).astype(BF16)

        @pl.when(qb == nb - 1)
        def _():
            for a in range(FOX_PAIRS):
                for kb in range(nb):
                    rows = slice(kb * BLOCK, (kb + 1) * BLOCK)
                    for acc, out_ref in ((dk_acc, dk_ref), (dv_acc, dv_ref)):
                        out_ref[rows, a * LANES:(a + 1) * LANES] = acc[a * nb + kb].T.astype(BF16)

    q_spec, kv_spec, cc_spec, cr_spec = _fox_specs(nb)
    dkv_spec = pl.BlockSpec((l, FOX_PAIRS * LANES), lambda b, p, i: (b, p))
    outs = pl.pallas_call(
        body,
        name="fox_bwd",
        grid=(bsz, FOX_STEPS, nb),
        in_specs=[q_spec, kv_spec, kv_spec, q_spec, q_spec, cc_spec, cr_spec] + [HBM_SPEC] * n_x,
        out_specs=[q_spec, dkv_spec, dkv_spec, cr_spec] + [HBM_SPEC] * n_x,
        out_shape=[jax.ShapeDtypeStruct((t, B_WIDTH), BF16), jax.ShapeDtypeStruct((t, B_WIDTH), BF16),
                   jax.ShapeDtypeStruct((t, B_WIDTH), BF16),
                   jax.ShapeDtypeStruct((bsz * B_HEADS, nb, 1, BLOCK), F32)] + (exchange.out_shape if exchange else []),
        scratch_shapes=[pltpu.VMEM((FOX_PAIRS * nb, LANES, BLOCK), F32), pltpu.VMEM((FOX_PAIRS * nb, LANES, BLOCK), F32),
                        pltpu.VMEM((FOX_HEADS, BLOCK, CHUNK), F32), pltpu.VMEM((FOX_HEADS, BLOCK, CHUNK), F32),
                        pltpu.VMEM((FOX_HEADS, BLOCK, CHUNK), BF16), pltpu.VMEM((FOX_HEADS, BLOCK, CHUNK), BF16),
                        pltpu.VMEM((FOX_PAIRS, BLOCK, LANES), F32)]
        + (exchange.scratch if exchange else []),
        compiler_params=_cparams(("arbitrary",) * 3 if exchange else ("parallel", "parallel", "arbitrary")),
    )(q, k, v, o_exact, do, lse, c_row, *(exchange.ins if exchange else []))
    return outs[:4], outs[4:]


def _loss_head(h, final_w, target):
    bsz, l, d = h.shape
    nb = l // BLOCK

    def body(h_ref, w_ref, t_ref, loss_ref, dh_ref, dw_ref):
        n = pl.program_id(0)

        @pl.when(n == 0)
        def _():
            loss_ref[...] = jnp.zeros_like(loss_ref)
            dw_ref[...] = jnp.zeros_like(dw_ref)
            dh_ref[...] = jnp.zeros_like(dh_ref)

        @pl.when(n > 0)
        def _():
            w = w_ref[...]
            for b in range(bsz):
                hh = h_ref[b]
                r = _rms_scale(hh)
                err = (hh * r) * w - t_ref[b]
                loss_ref[...] += 0.5 * jnp.sum(jnp.mean(err * err, axis=-1, keepdims=True), axis=0, keepdims=True)
                dy = err * (1.0 / d)
                dh, dw = _rms_bwd(dy, hh, w)
                dh_ref[b] = dh
                dw_ref[...] += dw

    return pl.pallas_call(
        body,
        name="loss_head",
        grid=(nb,),
        in_specs=[
            pl.BlockSpec((bsz, BLOCK, d), lambda n: (0, n, 0)),
            pl.BlockSpec((1, d), lambda n: (0, 0)),
            pl.BlockSpec((bsz, BLOCK, d), lambda n: (0, jnp.maximum(n - 1, 0), 0)),
        ],
        out_specs=[
            pl.BlockSpec((1, 128), lambda n: (0, 0)),
            pl.BlockSpec((bsz, BLOCK, d), lambda n: (0, n, 0)),
            pl.BlockSpec((1, d), lambda n: (0, 0)),
        ],
        out_shape=[jax.ShapeDtypeStruct((1, 128), F32), jax.ShapeDtypeStruct((bsz, l, d), F32),
                   jax.ShapeDtypeStruct((1, d), F32)],
        compiler_params=_cparams(("arbitrary",)),
    )(h, final_w, target)


def _pad_tiles(w, src, heads, lane_slot, axis):
    pieces = []
    for h in range(heads):
        x = lax.slice_in_dim(w, src + HEAD_DIM * h, src + HEAD_DIM * (h + 1), axis=axis)
        z = jnp.zeros_like(x)
        pieces += [x, z] if lane_slot(h) == 0 else [z, x]
    return pieces


def _unpad_tiles(g, off, heads, lane_slot, axis):
    return [lax.slice_in_dim(g, off + LANES * h + HEAD_DIM * lane_slot(h),
                             off + LANES * h + HEAD_DIM * (lane_slot(h) + 1), axis=axis) for h in range(heads)]


REF_RUNS = ((0, SRC_QB, 0, 0), (SRC_QB, SRC_F, 1, 0), (SRC_F, SRC_GA, 0, OFF_F), (SRC_GA, W_IN_COLS, 2, 0))
P_RUNS = ((0, OFF_F, 0), (OFF_F, OFF_F + B_HEADS, SRC_F), (OFF_QB, OFF_GA, SRC_QB), (OFF_GA, P_COLS, SRC_GA))
SHARD_COLS = W_IN_COLS // N_CHIPS
SHARD_PAD_COLS = -(-SHARD_COLS // LANES) * LANES


def _place(tile, lane, src_ref, src, dst, length):
    for t in range(src // LANES, (src + length - 1) // LANES + 1):
        x = src_ref[:, t * LANES:(t + 1) * LANES].astype(F32)
        shift = (dst - src) % LANES
        moved = pltpu.roll(x, shift, 1) if shift else x
        from_t = (lane >= max(dst, dst + t * LANES - src)) & (lane < min(dst + length, dst + (t + 1) * LANES - src))
        tile = jnp.where(from_t, moved, tile)
    return tile


def _layout_w_in(stacked):
    d = stacked.shape[1]
    rows = _tile(d, 256)

    def body(s_ref, o_ref):
        lane = lax.broadcasted_iota(jnp.int32, (rows, LANES), 1)
        for lo in range(0, P_COLS, LANES):
            tile = jnp.zeros((rows, LANES), F32)
            for first, end, ref_col in P_RUNS:
                start, stop = max(lo, first), min(lo + LANES, end)
                while start < stop:
                    k, src = divmod(ref_col + start - first, SHARD_COLS)
                    length = min(stop - start, SHARD_COLS - src)
                    tile = _place(tile, lane, s_ref.at[k], src, start - lo, length)
                    start += length
            o_ref[:, lo:lo + LANES] = tile.astype(o_ref.dtype)

    return pl.pallas_call(
        body,
        name="layout_w_in",
        grid=(d // rows,),
        in_specs=[pl.BlockSpec((N_CHIPS, rows, SHARD_PAD_COLS), lambda i: (0, i, 0))],
        out_specs=pl.BlockSpec((rows, P_COLS), lambda i: (i, 0)),
        out_shape=jax.ShapeDtypeStruct((d, P_COLS), stacked.dtype),
        compiler_params=_cparams(("parallel",)),
    )(stacked)


def _stack_w_in_grad(pieces):
    d = pieces[0].shape[0]
    rows = _tile(d, 256)

    def body(*refs):
        piece_refs, o_ref = refs[:-1], refs[-1]
        lane = lax.broadcasted_iota(jnp.int32, (rows, LANES), 1)
        for k in range(N_CHIPS):
            for j in range(0, SHARD_COLS, LANES):
                width = min(LANES, SHARD_COLS - j)
                lo = k * SHARD_COLS + j
                tile = jnp.zeros((rows, LANES), F32)
                for first, end, piece, at in REF_RUNS:
                    start, stop = max(lo, first), min(lo + width, end)
                    if start >= stop:
                        continue
                    tile = _place(tile, lane, piece_refs[piece], at + start - first, start - lo, stop - start)
                o_ref[k, :, j:j + width] = tile[:, :width]

    return pl.pallas_call(
        body,
        name="stack_w_in_grad",
        grid=(d // rows,),
        in_specs=[pl.BlockSpec((rows, p.shape[1]), lambda i: (i, 0)) for p in pieces],
        out_specs=pl.BlockSpec((N_CHIPS, rows, SHARD_COLS), lambda i: (0, i, 0)),
        out_shape=jax.ShapeDtypeStruct((N_CHIPS, d, SHARD_COLS), F32),
        compiler_params=_cparams(("parallel",)),
    )(*pieces)


def _local_step(x, target, meta, norms, b_forget, sinks, w, comm=None):
    n1, nmix, n2, nfin = norms
    w1i, w1o = w[:2]
    if meta is not None:
        x = jnp.concatenate([jnp.zeros((x.shape[0], N_PAD, x.shape[2]), F32),
                             jnp.broadcast_to(meta[None], (x.shape[0], N_META, x.shape[2])), x], axis=1)
    bsz, l, d = x.shape
    nb = l // BLOCK
    t = bsz * l
    h0 = x.reshape(t, d)

    if comm is None:
        (h1, g1, u1), _ = _ffn_fwd(h0, n1, w1i, w1o)
        w_in, wa, wb, wo, w2i, w2o = w[2:]
        w_in = jnp.pad(w_in.reshape(d, N_CHIPS, SHARD_COLS).transpose(1, 0, 2),
                       ((0, 0), (0, 0), (0, SHARD_PAD_COLS - SHARD_COLS)))
    else:
        (h1, g1, u1), (w_in,) = _ffn_fwd(h0, n1, w1i, w1o, comm.gather(GATHER_PROJ))
    wp = _layout_w_in(w_in)
    un, qa, ka, va, qb, kb, vb, ga, gb, f_logit = _proj_fwd(h1, nmix, wp)
    b_pad = jnp.concatenate([b_forget, jnp.zeros((1, F_COLS - B_HEADS), F32)], axis=1)
    c = _forget_cumsum(f_logit, b_pad, nb)
    c_heads = c[:, :B_HEADS].reshape(bsz, l, B_HEADS).transpose(0, 2, 1).reshape(bsz * B_HEADS, l)
    c_row = c_heads.reshape(bsz * B_HEADS, nb, 1, BLOCK)

    slopes = jnp.exp2(-8.0 * jnp.arange(1, A_HEADS + 1, dtype=F32) / A_HEADS)
    slope_rows = jnp.repeat(slopes.reshape(A_KV_HEADS, A_GROUP), BLOCK, axis=1)[:, :, None]
    sink_rows = jnp.repeat(sinks.reshape(A_KV_HEADS, A_GROUP), BLOCK, axis=1)[:, :, None]

    oa, lse_a = _swa_fwd(qa, ka, va, sink_rows, slope_rows, nb)
    if comm is None:
        (ob, ob_exact, lse_b), _ = _fox_fwd(qb, kb, vb, c_row, nb)
    else:
        (ob, ob_exact, lse_b), (wa, wb, wo, w2i, w2o) = _fox_fwd(qb, kb, vb, c_row, nb, comm.gather(GATHER_LATE))
    wa_p = jnp.concatenate(_pad_tiles(wa, 0, A_HEADS, A_SLOT, 0), axis=0)
    h2, mixed = _merge_fwd(h1, oa, ob, ga, gb, wa_p, wb, wo)
    (h3, g2, u2), _ = _ffn_fwd(h2, n2, w2i, w2o)
    loss, dh3, d_nfin = _loss_head(h3.reshape(bsz, l, d), nfin, target)

    (dh2, n2b, a2, dgu2, df2, dn2_parts), _ = _ffn_bwd(dh3.reshape(t, d), h2, n2, g2, u2, w2i, w2o)
    g_w2o = _tn_matmul(a2, df2, "grad_ffn2_w_out")
    g_w2i = _tn_matmul(n2b, dgu2, "grad_ffn2_w_in")

    hosted = comm.swap("ffn2", dict(ffn2_w_in=g_w2i, ffn2_w_out=g_w2o)) if comm else None
    (dya, dyb, doa, dob, dgates, dh2b), swapped = _merge_bwd(dh2, oa, ob, ga, gb, wa_p, wb, wo, hosted)
    g_wo = _tn_matmul(mixed, dh2b, "grad_w_out")
    g_wa = jnp.concatenate(_unpad_tiles(_tn_matmul(oa, dya, "grad_w_branch_a"), 0, A_HEADS, A_SLOT, 0), axis=0)
    g_wb = _tn_matmul(ob, dyb, "grad_w_branch_b")

    dqa, dka, dva, dsink_rows = _swa_bwd(qa, ka, va, doa, lse_a, sink_rows, slope_rows, nb)
    hosted = comm.scatter("ffn2", swapped) if comm else None
    (dqb, dkb, dvb, dc_row), pieces = _fox_bwd(qb, kb, vb, ob_exact, dob, lse_b, c_row, nb, hosted)
    if comm:
        comm.received("ffn2", pieces)
    dc = dc_row.reshape(bsz, B_HEADS, l).transpose(0, 2, 1).reshape(t, B_HEADS)
    dc = jnp.concatenate([dc, jnp.zeros((t, F_COLS - B_HEADS), F32)], axis=1)
    df_logit, db_parts = _forget_cumsum_bwd(dc, f_logit, b_pad, nb)

    dproj = (jnp.concatenate([dqa, dka, dva, df_logit], axis=1), jnp.concatenate([dqb, dkb, dvb], axis=1), dgates)
    g_win = _stack_w_in_grad([_tn_matmul(un, piece, "grad_w_in_" + tag) for piece, tag in zip(dproj, ("a", "b", "gates"))])
    if comm is None:
        g_win = g_win.transpose(1, 0, 2).reshape(d, W_IN_COLS)
    hosted = comm.swap("mixer", dict(w_in=g_win, w_branch_a=g_wa, w_branch_b=g_wb, w_out=g_wo)) if comm else None
    (dh1, dnmix_parts), swapped = _proj_bwd(dh2, h1, nmix, dproj, wp, hosted)
    hosted = comm.scatter("mixer", swapped) if comm else None
    (dh0, n1b, a1, dgu1, df1, dn1_parts), pieces = _ffn_bwd(dh1, h0, n1, g1, u1, w1i, w1o, hosted)
    dh0 = dh0.reshape(bsz, l, d)
    grad_x = dh0 if comm else dh0[:, PREFIX:]
    small = dict(
        meta_tokens=jnp.sum(dh0[:, N_PAD:PREFIX], axis=0),
        ffn1_norm=jnp.sum(dn1_parts, axis=0),
        mix_norm=jnp.sum(dnmix_parts, axis=0),
        ffn2_norm=jnp.sum(dn2_parts, axis=0),
        final_norm=d_nfin,
        b_forget=jnp.sum(db_parts, axis=0)[:, :B_HEADS],
        attn_sinks=jnp.sum(dsink_rows.reshape(bsz, A_HEADS, BLOCK), axis=(0, 2)).reshape(1, A_HEADS),
    )
    if comm is None:
        g_w1o = _tn_matmul(a1, df1, "grad_ffn1_w_out")
        g_w1i = _tn_matmul(n1b, dgu1, "grad_ffn1_w_in")
    else:
        comm.received("mixer", pieces)
        g_w1o, gathered = _tn_matmul(a1, df1, "grad_ffn1_w_out", comm.small_gather(loss, small))
        comm.small_gathered(gathered)
        swapped = _run_exchange(comm.swap("ffn1_out", dict(ffn1_w_out=g_w1o)), "exchange_halves_ffn1_out")
        g_w1i, pieces = _tn_matmul(n1b, dgu1, "grad_ffn1_w_in", comm.scatter("ffn1_out", swapped))
        comm.received("ffn1_out", pieces)
    big = dict(ffn1_w_in=g_w1i, ffn1_w_out=g_w1o, w_in=g_win, w_branch_a=g_wa, w_branch_b=g_wb,
               w_out=g_wo, ffn2_w_in=g_w2i, ffn2_w_out=g_w2o)
    return loss, grad_x, small, big


BIG = (
    ("ffn1_w_in", (D_MODEL, 5632), 1),
    ("ffn1_w_out", (2816, D_MODEL), 0),
    ("w_in", (D_MODEL, W_IN_COLS), 1),
    ("w_branch_a", (A_WIDTH, D_MODEL), 1),
    ("w_branch_b", (B_WIDTH, D_MODEL), 1),
    ("w_out", (D_MODEL, D_MODEL), 0),
    ("ffn2_w_in", (D_MODEL, 5632), 1),
    ("ffn2_w_out", (2816, D_MODEL), 0),
)
STACKED = "w_in"


def _coords():
    return lax.axis_index("x"), lax.axis_index("y"), lax.axis_index("c")


def _other_chips(x, y):
    return ((1 - x, y), (x, 1 - y), (1 - x, 1 - y))


def _chip_part(ref, name, shape, axis, k):
    if name == STACKED:
        return ref.at[k]
    size = shape[axis] // N_CHIPS
    start = pl.multiple_of(k * size, size)
    return ref.at[pl.ds(start, size), :] if axis == 0 else ref.at[:, pl.ds(start, size)]


def _full_shape(name, shape):
    return (N_CHIPS, shape[0], shape[1] // N_CHIPS) if name == STACKED else shape


class _Exchange:
    def __init__(self, ins, out_shape, n_sems, ops):
        self.ins, self.out_shape, self.n_sems, self.ops = list(ins), list(out_shape), n_sems, ops

    @property
    def scratch(self):
        return [pltpu.SemaphoreType.DMA((self.n_sems,)), pltpu.SemaphoreType.DMA((self.n_sems,))]


SEMS_PER_GATHER = 9


def _gather_exchange(shards, table):
    n = len(table)
    x_nbr, y_nbr, diagonal = 0, 1, 2

    def ops(ins, outs, send_sems, recv_sems):
        x, y, c = _coords()
        mine = 2 * x + y
        sibling = (x, y, 1 - c)
        chips = _other_chips(x, y)
        slots = [2 * chip[0] + chip[1] for chip in chips]

        def part(i, k):
            name, shape, axis = table[i][:3]
            return _chip_part(outs[i], name, shape, axis, k)

        def half(ref, h):
            rows = ref.shape[0] // 2
            return ref.at[pl.ds(pl.multiple_of(h * rows, rows), rows), :]

        def remote(i, sem, src, dst, device):
            sem = SEMS_PER_GATHER * i + sem
            return pltpu.make_async_remote_copy(src, dst, send_sems.at[sem], recv_sems.at[sem],
                                                device_id=device, device_id_type=MESH_ID)

        def own(i):
            return remote(i, 0, ins[i], part(i, mine), sibling)

        def fetch(i, j, slot):
            if table[i][4]:
                src, dst = half(ins[i], c), half(part(i, slot), c)
            else:
                src, dst = ins[i], part(i, slot)
            return remote(i, 1 + j, src, dst, (chips[j][0], chips[j][1], c))

        def relayed(i, via, of):
            region = half(half(part(i, slots[of]), c), via)
            return remote(i, 4 + via, region, region, (chips[via][0], chips[via][1], c))

        def forward(i, j, h):
            region = half(part(i, slots[j]), h)
            return remote(i, 6 + j, region, region, sibling)

        def start():
            for i in range(n):
                for j in (x_nbr, y_nbr) if table[i][4] else (x_nbr, y_nbr, diagonal):
                    fetch(i, j, mine).start()
            for i in range(n):
                own(i).start()

        def relay():
            for i in range(n):
                if not table[i][4]:
                    for j in range(3):
                        fetch(i, j, slots[j]).wait_recv()
                    continue
                fetch(i, y_nbr, slots[y_nbr]).wait_recv()
                relayed(i, x_nbr, y_nbr).start()
                forward(i, y_nbr, c).start()
                fetch(i, x_nbr, slots[x_nbr]).wait_recv()
                relayed(i, y_nbr, x_nbr).start()
                forward(i, x_nbr, c).start()

        def relay_diagonal():
            for i in range(n):
                if table[i][4]:
                    relayed(i, x_nbr, diagonal).wait_recv()
                    relayed(i, y_nbr, diagonal).wait_recv()
                    forward(i, diagonal, c).start()

        def finish():
            for i in range(n):
                own(i).wait()
                for j in range(3):
                    if table[i][4]:
                        forward(i, j, 1 - c).wait_recv()
                        forward(i, j, c).wait_send()
                    if j != diagonal or not table[i][4]:
                        fetch(i, j, mine).wait_send()
                if table[i][4]:
                    relayed(i, x_nbr, y_nbr).wait_send()
                    relayed(i, y_nbr, x_nbr).wait_send()

        return start, relay, relay_diagonal, finish

    out_shape = [jax.ShapeDtypeStruct(_full_shape(name, shape), dtype) for name, shape, _, dtype, _ in table]
    return _Exchange(shards, out_shape, SEMS_PER_GATHER * n, ops)


STREAM_ROWS = 256


def _stream_rows(src, dst, src_first, dst_first, buf, sem_in, sem_out):
    bsz = dst.shape[0]
    n_rows = min(src.shape[1] - src_first, dst.shape[1] - dst_first)
    chunks = [(b, r) for b in range(bsz) for r in range(0, n_rows, STREAM_ROWS)]

    def load(i):
        b, r = chunks[i]
        return pltpu.make_async_copy(src.at[b, pl.ds(src_first + r, STREAM_ROWS), :], buf.at[i % 2], sem_in.at[i % 2])

    def store(i):
        b, r = chunks[i]
        return pltpu.make_async_copy(buf.at[i % 2], dst.at[b, pl.ds(dst_first + r, STREAM_ROWS), :],
                                     sem_out.at[i % 2])

    load(0).start()
    for i in range(len(chunks)):
        if i + 1 < len(chunks):
            if i >= 1:
                store(i - 1).wait()
            load(i + 1).start()
        load(i).wait()
        store(i).start()
    for i in range(max(len(chunks) - 2, 0), len(chunks)):
        store(i).wait()


def _run_exchange(exchange, name, tail_of=None):
    n = len(exchange.ins)
    k = 1 if tail_of else 0

    def body(*refs):
        sems = refs[2 * n + 2 * k:2 * n + 2 * k + 2]
        *phases, finish = exchange.ops(refs[:n], refs[n + k:2 * n + k], *sems)
        for phase in phases:
            phase()
        if tail_of:
            _stream_rows(refs[n], refs[2 * n + k], tail_of[1], 0, *refs[2 * n + 2 * k + 2:])
        finish()

    extra_in, extra_out, extra_scratch = [], [], []
    if tail_of:
        src, first = tail_of
        assert (src.shape[1] - first) % STREAM_ROWS == 0
        extra_in = [src]
        extra_out = [jax.ShapeDtypeStruct((src.shape[0], src.shape[1] - first, src.shape[2]), src.dtype)]
        extra_scratch = [pltpu.VMEM((2, STREAM_ROWS, src.shape[2]), src.dtype),
                         pltpu.SemaphoreType.DMA((2,)), pltpu.SemaphoreType.DMA((2,))]
    return pl.pallas_call(
        body,
        name=name,
        in_specs=[HBM_SPEC] * (n + k),
        out_specs=[HBM_SPEC] * (n + k),
        out_shape=exchange.out_shape + extra_out,
        scratch_shapes=exchange.scratch + extra_scratch,
    )(*exchange.ins, *extra_in)


CAST_ROWS = 64


def _cast_hosting(arrays, exchange, name, stream=None):
    n_a, n_x = len(arrays), len(exchange.ins)
    k = 1 if stream else 0

    def body(*refs):
        a_in, x_in = refs[:n_a], refs[n_a:n_a + n_x]
        outs = refs[n_a + n_x + k:]
        a_out, x_out = outs[:n_a], outs[n_a:n_a + n_x]
        scratch = outs[n_a + n_x + k:]
        start, *rest = exchange.ops(x_in, x_out, *scratch[:2])
        start()
        for src, dst in zip(a_in, a_out):
            def rows(i, carry, src=src, dst=dst):
                window = pl.ds(pl.multiple_of(i * CAST_ROWS, CAST_ROWS), CAST_ROWS)
                cols = src.shape[1]
                if dst.shape[1] != cols:
                    dst[window, dst.shape[1] - LANES:] = jnp.zeros((CAST_ROWS, LANES), BF16)
                dst[window, :cols] = src[window, :].astype(BF16)
                return carry

            lax.fori_loop(0, src.shape[0] // CAST_ROWS, rows, 0)
        if stream:
            x_ref, h_ref = refs[n_a + n_x], outs[n_a + n_x]
            buf, sem_in, sem_out = scratch[2:]
            _stream_rows(x_ref, h_ref, 0, PREFIX, buf, sem_in, sem_out)
        for phase in rest:
            phase()
        if stream:
            meta = pltpu.make_async_copy(x_out[stream[1]], buf.at[1, pl.ds(0, N_META), :], sem_in.at[0])
            meta.start()
            buf[0, 0:N_PAD, :] = jnp.zeros((N_PAD, buf.shape[2]), F32)
            meta.wait()
            buf[0, N_PAD:PREFIX, :] = buf[1, 0:N_META, :]
            puts = [pltpu.make_async_copy(buf.at[0, pl.ds(0, PREFIX), :], h_ref.at[b, pl.ds(0, PREFIX), :], sem_out.at[b])
                    for b in range(h_ref.shape[0])]
            for put in puts:
                put.start()
            for put in puts:
                put.wait()

    extra_in, extra_out, extra_scratch = [], [], []
    if stream:
        x = stream[0]
        assert x.shape[0] <= 2 and x.shape[1] % STREAM_ROWS == 0 and x.dtype == F32
        extra_in = [x]
        extra_out = [jax.ShapeDtypeStruct((x.shape[0], PREFIX + x.shape[1], x.shape[2]), F32)]
        extra_scratch = [pltpu.VMEM((2, STREAM_ROWS, x.shape[2]), F32),
                         pltpu.SemaphoreType.DMA((2,)), pltpu.SemaphoreType.DMA((2,))]
    outs = pl.pallas_call(
        body,
        name=name,
        in_specs=[VMEM_SPEC] * n_a + [HBM_SPEC] * (n_x + k),
        out_specs=[VMEM_SPEC] * n_a + [HBM_SPEC] * (n_x + k),
        out_shape=[jax.ShapeDtypeStruct((a.shape[0], -(-a.shape[1] // LANES) * LANES), BF16) for a in arrays]
        + exchange.out_shape + extra_out,
        scratch_shapes=exchange.scratch + extra_scratch,
        compiler_params=pltpu.CompilerParams(vmem_limit_bytes=VMEM_LIMIT),
    )(*arrays, *exchange.ins, *extra_in)
    return outs[:n_a], outs[n_a:]


def _host_exchange(exchange, in_refs, out_refs, sem_refs, first, middle, last, late=None):
    start, *relays, finish = exchange.ops(in_refs, out_refs, *sem_refs)
    pl.when(first)(start)
    pl.when(middle)(relays[0])
    if len(relays) > 1:
        pl.when(last if late is None else late)(relays[1])
    pl.when(last)(finish)


def _halves_view(name, shape, axis):
    r, c = shape
    if name == STACKED:
        return (N_CHIPS, 2, r // 2, c // N_CHIPS), lambda ref, h: ref.at[:, h]
    if axis == 1:
        return (2, r // 2, c), lambda ref, h: ref.at[h]
    return (N_CHIPS, 2, r // N_CHIPS // 2, c), lambda ref, h: ref.at[:, h]


def _halves_exchange(grads, entries):
    n_w = len(entries)
    views = [_halves_view(*entry) for entry in entries]

    def ops(ins, outs, send_sems, recv_sems):
        x, y, c = _coords()
        copies = [pltpu.make_async_remote_copy(views[i][1](ins[i], 1 - c), outs[i], send_sems.at[i], recv_sems.at[i],
                                               device_id=(x, y, 1 - c), device_id_type=MESH_ID) for i in range(n_w)]

        def start():
            for cp in copies:
                cp.start()

        def finish():
            for cp in copies:
                cp.wait()

        return start, lambda: None, finish

    half_shape = lambda v: tuple(d for i, d in enumerate(v) if i != (1 if len(v) == 4 else 0))
    out_shape = [jax.ShapeDtypeStruct(half_shape(v[0]), F32) for v in views]
    return _Exchange([g.reshape(v[0]) for g, v in zip(grads, views)], out_shape, n_w, ops)


def _add_sibling(g_view, recv, c, name):
    shape = recv.shape
    if len(shape) == 2:
        tr = _tile(shape[0], 128, 16)
        grid = (shape[0] // tr,)
        g_spec = pl.BlockSpec((None, tr, shape[1]), lambda i, c_ref: (c_ref[0], i, 0))
        r_spec = pl.BlockSpec((tr, shape[1]), lambda i, c_ref: (i, 0))
    else:
        tr = _tile(shape[1], 256, 16)
        grid = (N_CHIPS, shape[1] // tr)
        g_spec = pl.BlockSpec((None, None, tr, shape[2]), lambda k, i, c_ref: (k, c_ref[0], i, 0))
        r_spec = pl.BlockSpec((None, tr, shape[2]), lambda k, i, c_ref: (k, i, 0))

    def body(c_ref, g_ref, r_ref, o_ref):
        o_ref[...] = (g_ref[...] + r_ref[...]).astype(BF16)

    return pl.pallas_call(
        body,
        name="add_sibling_" + name,
        grid_spec=pltpu.PrefetchScalarGridSpec(num_scalar_prefetch=1, grid=grid, in_specs=[g_spec, r_spec],
                                               out_specs=r_spec),
        out_shape=jax.ShapeDtypeStruct(shape, BF16),
        compiler_params=_cparams(("parallel",) * len(grid)),
    )(c, g_view, recv)


def _piece_of(ref, name, axis, k):
    if name == STACKED or axis == 0:
        return ref.at[k]
    size = ref.shape[1] // N_CHIPS
    return ref.at[:, pl.ds(pl.multiple_of(k * size, size), size)]


def _piece_shape(name, shape, axis):
    r, c = shape
    return (r // 2, c // N_CHIPS) if (axis == 1) else (r // N_CHIPS // 2, c)


def _scatter_exchange(partials, entries):
    n_w = len(entries)

    def ops(ins, outs, send_sems, recv_sems):
        x, y, c = _coords()
        chips = _other_chips(x, y)
        copies = []
        for i, (name, _, axis) in enumerate(entries):
            for j, chip in enumerate(chips):
                sem = 3 * i + j
                copies.append(pltpu.make_async_remote_copy(
                    _piece_of(ins[i], name, axis, 2 * chip[0] + chip[1]), outs[i].at[j], send_sems.at[sem],
                    recv_sems.at[sem], device_id=(chip[0], chip[1], c), device_id_type=MESH_ID))

        def start():
            for cp in copies:
                cp.start()

        def finish():
            for cp in copies:
                cp.wait()

        return start, lambda: None, finish

    out_shape = [jax.ShapeDtypeStruct((3,) + _piece_shape(*entry), BF16) for entry in entries]
    return _Exchange(partials, out_shape, 3 * n_w, ops)


def _add_chips(partial, recv, mine, name, axis):
    rows, cols = recv.shape[1:]
    tr = _tile(rows, 256, 16)
    if name == STACKED or axis == 0:
        p_spec = pl.BlockSpec((None, tr, cols), lambda i, k_ref: (k_ref[0], i, 0))
    else:
        p_spec = pl.BlockSpec((tr, cols), lambda i, k_ref: (i, k_ref[0]))

    def body(k_ref, p_ref, r_ref, o_ref):
        f32 = lambda a: a.astype(F32)
        o_ref[...] = ((f32(p_ref[...]) + f32(r_ref[0])) + f32(r_ref[1])) + f32(r_ref[2])

    return pl.pallas_call(
        body,
        name="add_chips_" + name,
        grid_spec=pltpu.PrefetchScalarGridSpec(
            num_scalar_prefetch=1, grid=(rows // tr,),
            in_specs=[p_spec, pl.BlockSpec((3, tr, cols), lambda i, k_ref: (0, i, 0))],
            out_specs=pl.BlockSpec((tr, cols), lambda i, k_ref: (i, 0))),
        out_shape=jax.ShapeDtypeStruct((rows, cols), F32),
        compiler_params=_cparams(("parallel",)),
    )(mine, partial, recv)


def _share_with_sibling(halves):
    n_w = len(halves)

    def body(*refs):
        ins, outs = refs[:n_w], refs[n_w:2 * n_w]
        send_sems, recv_sems = refs[2 * n_w:]
        x, y, c = _coords()
        copies = [pltpu.make_async_remote_copy(ins[i], outs[i], send_sems.at[i], recv_sems.at[i],
                                               device_id=(x, y, 1 - c), device_id_type=MESH_ID) for i in range(n_w)]
        for cp in copies:
            cp.start()
        for cp in copies:
            cp.wait()

    return pl.pallas_call(
        body,
        name="share_with_sibling",
        in_specs=[HBM_SPEC] * n_w,
        out_specs=[HBM_SPEC] * n_w,
        out_shape=[jax.ShapeDtypeStruct(h.shape, F32) for h in halves],
        scratch_shapes=[pltpu.SemaphoreType.DMA((n_w,)), pltpu.SemaphoreType.DMA((n_w,))],
    )(*halves)


SMALL_ROWS = 168


def _small_exchange(buf):
    def ops(ins, outs, send_sems, recv_sems):
        x, y, c = _coords()
        me = 4 * x + 2 * y + c
        peers = [(x ^ fx, y ^ fy, c ^ fc) for fx in (0, 1) for fy in (0, 1) for fc in (0, 1)][1:]

        def copy(j, slot, dev):
            return pltpu.make_async_remote_copy(ins[0], outs[0].at[slot], send_sems.at[j], recv_sems.at[j],
                                                device_id=dev, device_id_type=MESH_ID)

        own = pltpu.make_async_copy(ins[0], outs[0].at[me], send_sems.at[N_DEV - 1])

        def start():
            own.start()
            for j, dev in enumerate(peers):
                copy(j, me, dev).start()

        def finish():
            for j, dev in enumerate(peers):
                copy(j, 4 * dev[0] + 2 * dev[1] + dev[2], dev).wait()
            own.wait()

        return start, lambda: None, finish

    return _Exchange([buf], [jax.ShapeDtypeStruct((N_DEV,) + buf.shape, F32)], N_DEV, ops)


def _sum_devices(gathered):
    def body(g_ref, out_ref):
        acc = g_ref[0]
        for d in range(1, N_DEV):
            acc = acc + g_ref[d]
        out_ref[...] = acc

    return pl.pallas_call(
        body,
        name="sum_devices",
        in_specs=[VMEM_SPEC],
        out_specs=VMEM_SPEC,
        out_shape=jax.ShapeDtypeStruct(gathered.shape[1:], F32),
    )(gathered)


def _adamw(w, g, m, v, copy_g=False):
    r, rest = w.shape[0], w.shape[1:]
    per_row = 1
    for dim in rest:
        per_row *= dim
    tr = _tile(r, max(8, (5 << 19) // (4 * per_row)), 8 if len(rest) == 1 else 1)

    def body(w_ref, g_ref, m_ref, v_ref, *out_refs):
        d_ref, mo_ref, vo_ref = out_refs[-3:]
        gg = g_ref[...]
        if copy_g:
            out_refs[0][...] = gg
        mm = ADAM_B1 * m_ref[...] + (1.0 - ADAM_B1) * gg
        vv = ADAM_B2 * v_ref[...] + (1.0 - ADAM_B2) * (gg * gg)
        m_hat = mm / (1.0 - ADAM_B1 ** ADAM_STEP)
        v_hat = vv / (1.0 - ADAM_B2 ** ADAM_STEP)
        d_ref[...] = -ADAM_LR * (m_hat / (jnp.sqrt(v_hat) + ADAM_EPS) + ADAM_WD * w_ref[...])
        mo_ref[...] = mm
        vo_ref[...] = vv

    n_out = 4 if copy_g else 3
    spec = pl.BlockSpec((tr,) + rest, lambda i: (i,) + (0,) * len(rest))
    return pl.pallas_call(
        body,
        name="adamw",
        grid=(r // tr,),
        in_specs=[spec] * 4,
        out_specs=[spec] * n_out,
        out_shape=[jax.ShapeDtypeStruct(w.shape, F32)] * n_out,
        compiler_params=_cparams(("parallel",)),
    )(w, g, m, v)


def _adamw_halves(w, own, other, m, v, c, name):
    r, cols = w.shape
    half = r // 2
    tr = _tile(half, 256, 8)
    nt = half // tr
    whole = pl.BlockSpec((tr, cols), lambda h, i, c_ref: (h * nt + i, 0))
    part = pl.BlockSpec((tr, cols), lambda h, i, c_ref: (i, 0))

    def body(c_ref, w_ref, own_ref, other_ref, m_ref, v_ref, g_ref, d_ref, mo_ref, vo_ref):
        gg = jnp.where(pl.program_id(0) == c_ref[0], own_ref[...], other_ref[...])
        g_ref[...] = gg
        mm = ADAM_B1 * m_ref[...] + (1.0 - ADAM_B1) * gg
        vv = ADAM_B2 * v_ref[...] + (1.0 - ADAM_B2) * (gg * gg)
        m_hat = mm / (1.0 - ADAM_B1 ** ADAM_STEP)
        v_hat = vv / (1.0 - ADAM_B2 ** ADAM_STEP)
        d_ref[...] = -ADAM_LR * (m_hat / (jnp.sqrt(v_hat) + ADAM_EPS) + ADAM_WD * w_ref[...])
        mo_ref[...] = mm
        vo_ref[...] = vv

    return pl.pallas_call(
        body,
        name="adamw_" + name,
        grid_spec=pltpu.PrefetchScalarGridSpec(
            num_scalar_prefetch=1, grid=(2, nt),
            in_specs=[whole, part, part, whole, whole], out_specs=[whole] * 4),
        out_shape=[jax.ShapeDtypeStruct((r, cols), F32)] * 4,
        compiler_params=_cparams(("parallel", "parallel")),
    )(c, w, own, other, m, v)


GATHER_FIRST = ("ffn1_w_in", "ffn1_w_out")
GATHER_PROJ = ("w_in",)
GATHER_LATE = ("w_branch_a", "w_branch_b", "w_out", "ffn2_w_in", "ffn2_w_out")


class _Comm:
    def __init__(self, shards, c_arr, mine_arr):
        self.shards, self.c, self.mine = shards, c_arr, mine_arr
        self.groups, self.halves = {}, {}
        self.by_name = {entry[0]: entry for entry in BIG}

    def small_gather(self, loss, small):
        pad_lanes = lambda a: jnp.concatenate([a, jnp.zeros((1, LANES - a.shape[1]), F32)], axis=1)
        buf = jnp.concatenate([
            small["meta_tokens"].reshape(128, LANES),
            small["ffn1_norm"].reshape(8, LANES), small["mix_norm"].reshape(8, LANES),
            small["ffn2_norm"].reshape(8, LANES), small["final_norm"].reshape(8, LANES),
            loss, pad_lanes(small["b_forget"]), pad_lanes(small["attn_sinks"]),
            jnp.zeros((SMALL_ROWS - 163, LANES), F32)], axis=0)
        return _small_exchange(buf)

    def small_gathered(self, outs):
        self.reduced = _sum_devices(outs[0])

    def gather(self, names):
        padded = (STACKED, (D_MODEL, N_CHIPS * SHARD_PAD_COLS), 1)
        table = [(padded if n == STACKED else self.by_name[n]) + (BF16, True) for n in names]
        return _gather_exchange([self.shards[n] for n in names], table)

    def swap(self, tag, grads):
        entries = [self.by_name[n] for n in grads]
        arrays = list(grads.values())
        self.groups[tag] = (entries, arrays)
        return _halves_exchange(arrays, entries)

    def scatter(self, tag, received):
        entries, arrays = self.groups[tag]
        views = [_halves_view(*entry) for entry in entries]
        partials = [_add_sibling(g.reshape(v[0]), r, self.c, name)
                    for g, v, r, (name, _, _) in zip(arrays, views, received, entries)]
        self.groups[tag] = (entries, partials)
        return _scatter_exchange(partials, entries)

    def received(self, tag, pieces):
        entries, partials = self.groups[tag]
        for p, r, (name, _, axis) in zip(partials, pieces, entries):
            self.halves[name] = _add_chips(p, r, self.mine, name, axis)

    def finish(self):
        names = [n for n, _, _ in BIG]
        own = [self.halves[n] for n in names]
        return dict(zip(names, zip(own, _share_with_sibling(own))))


def kernel(x, meta_tokens, ffn1_norm, ffn1_w_in, ffn1_w_out, mix_norm, w_in, b_forget, attn_sinks, w_branch_a, w_branch_b, w_out, ffn2_norm, ffn2_w_in, ffn2_w_out, final_norm, loss_target, m_meta_tokens, m_ffn1_norm, m_ffn1_w_in, m_ffn1_w_out, m_mix_norm, m_w_in, m_b_forget, m_attn_sinks, m_w_branch_a, m_w_branch_b, m_w_out, m_ffn2_norm, m_ffn2_w_in, m_ffn2_w_out, m_final_norm, v_meta_tokens, v_ffn1_norm, v_ffn1_w_in, v_ffn1_w_out, v_mix_norm, v_w_in, v_b_forget, v_attn_sinks, v_w_branch_a, v_w_branch_b, v_w_out, v_ffn2_norm, v_ffn2_w_in, v_ffn2_w_out, v_final_norm):
    given = dict(locals())
    names = ["meta_tokens", "ffn1_norm", "ffn1_w_in", "ffn1_w_out", "mix_norm", "w_in", "b_forget", "attn_sinks",
             "w_branch_a", "w_branch_b", "w_out", "ffn2_norm", "ffn2_w_in", "ffn2_w_out", "final_norm"]
    big_names = [n for n, _, _ in BIG]
    cx, cy, cc = _coords()
    c_arr = cc.reshape(1).astype(jnp.int32)
    mine_arr = (2 * cx + cy).reshape(1).astype(jnp.int32)

    by_name = {entry[0]: entry for entry in BIG}
    shards = {n: given[n][0].astype(BF16) for n in GATHER_FIRST}
    table = [by_name[n] + (BF16, True) for n in GATHER_FIRST] + [("meta_tokens", (N_META, D_MODEL), 1, F32, False)]
    first = _gather_exchange([shards[n] for n in GATHER_FIRST] + [meta_tokens], table)
    late_names = [n for n in big_names if n not in GATHER_FIRST]
    casts, (w1i, w1o, _, stream) = _cast_hosting([given[n][0] for n in late_names], first, "gather_first",
                                                 stream=(x, len(GATHER_FIRST)))
    shards.update(zip(late_names, casts))
    comm = _Comm(shards, c_arr, mine_arr)
    norms = (ffn1_norm, mix_norm, ffn2_norm, final_norm.reshape(1, D_MODEL))
    loss, dh0, small, big = _local_step(stream, loss_target, None, norms, b_forget, attn_sinks, (w1i, w1o), comm)

    swap = comm.swap("ffn1_in", dict(ffn1_w_in=big["ffn1_w_in"]))
    last = comm.scatter("ffn1_in", _run_exchange(swap, "exchange_halves_ffn1_in"))
    *pieces, grad_x = _run_exchange(last, "scatter_chip_sums", tail_of=(dh0, PREFIX))
    comm.received("ffn1_in", pieces)
    grad_halves = comm.finish()
    grads = {}

    red = comm.reduced
    meta_cols = red[:128].reshape(N_META, D_MODEL)
    grads["meta_tokens"] = lax.dynamic_slice_in_dim(meta_cols, (2 * cx + cy) * (D_MODEL // N_CHIPS),
                                                    D_MODEL // N_CHIPS, axis=1)
    grads["ffn1_norm"] = red[128:136].reshape(1, D_MODEL)
    grads["mix_norm"] = red[136:144].reshape(1, D_MODEL)
    grads["ffn2_norm"] = red[144:152].reshape(1, D_MODEL)
    grads["final_norm"] = red[152:160].reshape(1, D_MODEL)
    loss_out = red[160, 0]
    grads["b_forget"] = red[161:162, :B_HEADS]
    grads["attn_sinks"] = red[162:163, :A_HEADS]

    out_g, out_d, out_m, out_v = [], [], [], []
    for n in names:
        w_full = given[n]
        shape = w_full.shape
        two_d = (lambda a: a.reshape(shape[-2], shape[-1])) if len(shape) >= 2 else (lambda a: a.reshape(1, shape[0]))
        if n == STACKED:
            own, other = grad_halves[n]
            g_nat = jnp.concatenate([jnp.where(cc == 0, own, other), jnp.where(cc == 0, other, own)], axis=0)
            rows = lambda a: a.reshape(1, shape[-2], shape[-1]).transpose(2, 0, 1)
            unrows = lambda a: a.transpose(1, 2, 0)
            g2, d2, m2, v2 = [unrows(a) for a in _adamw(rows(w_full), rows(g_nat), rows(given["m_" + n]),
                                                         rows(given["v_" + n]), copy_g=True)]
        elif n in grad_halves:
            own, other = grad_halves[n]
            g2, d2, m2, v2 = _adamw_halves(two_d(w_full), own, other, two_d(given["m_" + n]),
                                           two_d(given["v_" + n]), c_arr, n)
        else:
            g2 = two_d(grads[n])
            d2, m2, v2 = _adamw(two_d(w_full), g2, two_d(given["m_" + n]), two_d(given["v_" + n]))
        out_g.append(g2.reshape(shape))
        out_d.append(d2.reshape(shape))
        out_m.append(m2.reshape(shape))
        out_v.append(v2.reshape(shape))
    return (loss_out, grad_x, *out_g, *out_d, *out_m, *out_v)
```

```python
import jax
import jax.numpy as jnp
from jax import lax
from jax.experimental import pallas as pl
from jax.experimental.pallas import tpu as pltpu

F32 = jnp.float32
BF16 = jnp.bfloat16

D_MODEL = 1024
N_META = 16
BLOCK = 128
LANES = 128
PREFIX = BLOCK
N_PAD = PREFIX - N_META
HEAD_DIM = 64
A_HEADS = 8
A_KV_HEADS = 2
A_GROUP = 4
B_HEADS = 8
B_PAIRS = B_HEADS // 2
A_WIDTH = A_HEADS * HEAD_DIM
A_KV_WIDTH = A_KV_HEADS * HEAD_DIM
B_WIDTH = B_HEADS * HEAD_DIM
W_IN_COLS = A_WIDTH + 2 * A_KV_WIDTH + 3 * B_WIDTH + B_HEADS + 2 * D_MODEL
SRC_KA = A_WIDTH
SRC_VA = SRC_KA + A_KV_WIDTH
SRC_QB = SRC_VA + A_KV_WIDTH
SRC_KB = SRC_QB + B_WIDTH
SRC_VB = SRC_KB + B_WIDTH
SRC_F = SRC_VB + B_WIDTH
SRC_GA = SRC_F + B_HEADS
SRC_GB = SRC_GA + D_MODEL
A_PAD_WIDTH = A_HEADS * LANES
B_PAD_WIDTH = B_HEADS * LANES
F_COLS = LANES
OFF_QA = 0
OFF_KA = SRC_KA
OFF_VA = SRC_VA
OFF_F = OFF_VA + A_KV_WIDTH
OFF_QB = OFF_F + F_COLS
OFF_KB = OFF_QB + B_WIDTH
OFF_VB = OFF_KB + B_WIDTH
OFF_GA = OFF_VB + B_WIDTH
OFF_GB = OFF_GA + D_MODEL
P_COLS = OFF_GB + D_MODEL
P_PIECES = ((0, OFF_QB), (OFF_QB, OFF_GA - OFF_QB), (OFF_GA, P_COLS - OFF_GA))
EPS = 1e-6
NEG = -1e30
SCALE = HEAD_DIM ** -0.5
KEY_BLOCKS = 4

ADAM_LR = 0.001
ADAM_B1 = 0.9
ADAM_B2 = 0.999
ADAM_EPS = 1e-08
ADAM_WD = 0.01
ADAM_STEP = 10

N_CHIPS = 4
N_DEV = 8
VMEM_LIMIT = 56 * 1024 * 1024

NT_DIMS = (((1,), (1,)), ((), ()))
TN_DIMS = (((0,), (0,)), ((), ()))
MESH_ID = pl.DeviceIdType.MESH
HBM_SPEC = pl.BlockSpec(memory_space=pltpu.HBM)
VMEM_SPEC = pl.BlockSpec(memory_space=pltpu.VMEM)


def _tile(n, target, mult=16):
    best = None
    for t in range(mult, min(n, target) + 1, mult):
        if n % t == 0:
            best = t
    return best if best is not None else n


def _cparams(sem):
    return pltpu.CompilerParams(dimension_semantics=sem, vmem_limit_bytes=VMEM_LIMIT)


def _rms_scale(h):
    return lax.rsqrt(jnp.mean(h * h, axis=-1, keepdims=True) + EPS)


def _rms_bwd(dn, h, w):
    r = _rms_scale(h)
    dw = jnp.sum(dn * (h * r), axis=0, keepdims=True)
    z = dn * w
    dh = r * z - h * ((r * r * r) * jnp.mean(z * h, axis=-1, keepdims=True))
    return dh, dw


def _ffn_fwd(h, norm_w, w_in, w_out, exchange=None):
    t, d = h.shape
    f = w_out.shape[0]
    tm = _tile(t, 272)
    tc = _tile(f, 256, 128)
    nj = f // tc
    ni = t // tm
    n_x = len(exchange.ins) if exchange else 0

    def body(*refs):
        h_ref, nw_ref, wi_ref, wo_ref = refs[:4]
        hout_ref, g_ref, u_ref = refs[4 + n_x:7 + n_x]
        a_scr = refs[7 + 2 * n_x]
        i = pl.program_id(0)
        if exchange:
            _host_exchange(exchange, refs[4:4 + n_x], refs[7 + n_x:7 + 2 * n_x], refs[8 + 2 * n_x:],
                           i == 0, i == ni // 3, i == ni - 1, late=i == 2 * ni // 3)
        hh = h_ref[...]
        n = ((hh * _rms_scale(hh)) * nw_ref[...]).astype(BF16)
        for j in range(nj):
            cols = slice(j * tc, (j + 1) * tc)
            g = jnp.dot(n, wi_ref[:, j * tc:(j + 1) * tc], preferred_element_type=F32)
            u = jnp.dot(n, wi_ref[:, f + j * tc:f + (j + 1) * tc], preferred_element_type=F32)
            g_ref[:, cols] = g
            u_ref[:, cols] = u
            a_scr[:, cols] = ((g * jax.nn.sigmoid(g)) * u).astype(BF16)
        hout_ref[...] = hh + 0.5 * jnp.dot(a_scr[...], wo_ref[...], preferred_element_type=F32)

    resident = lambda a: pl.BlockSpec(a.shape, lambda i: (0, 0), pipeline_mode=pl.Buffered(1))
    row = lambda w: pl.BlockSpec((tm, w), lambda i: (i, 0))
    outs = pl.pallas_call(
        body,
        name="ffn_fwd",
        grid=(ni,),
        in_specs=[row(d), pl.BlockSpec((1, d), lambda i: (0, 0)), resident(w_in), resident(w_out)] + [HBM_SPEC] * n_x,
        out_specs=[row(d), row(f), row(f)] + [HBM_SPEC] * n_x,
        out_shape=[
            jax.ShapeDtypeStruct((t, d), F32),
            jax.ShapeDtypeStruct((t, f), F32),
            jax.ShapeDtypeStruct((t, f), F32),
        ] + (exchange.out_shape if exchange else []),
        scratch_shapes=[pltpu.VMEM((tm, f), BF16)] + (exchange.scratch if exchange else []),
        compiler_params=_cparams(("arbitrary",) if exchange else ("parallel",)),
    )(h, norm_w, w_in, w_out, *(exchange.ins if exchange else []))
    return outs[:3], outs[3:]


def _ffn_bwd(dh_out, h, norm_w, g, u, w_in, w_out, exchange=None):
    t, d = h.shape
    f = w_out.shape[0]
    tm = _tile(t, 272)
    tc = _tile(f, 256, 128)
    nj = f // tc
    ni = t // tm
    n_x = len(exchange.ins) if exchange else 0

    def body(*refs):
        dho_ref, h_ref, nw_ref, g_ref, u_ref, wi_ref, wo_ref = refs[:7]
        dhin_ref, n_ref, a_ref, dgu_ref, df_ref, dnw_ref = refs[7 + n_x:13 + n_x]
        i = pl.program_id(0)
        if exchange:
            _host_exchange(exchange, refs[7:7 + n_x], refs[13 + n_x:13 + 2 * n_x], refs[13 + 2 * n_x:],
                           i == 0, i == ni - 1, i == ni - 1)
        hh = h_ref[...]
        nw = nw_ref[...]
        n_ref[...] = ((hh * _rms_scale(hh)) * nw).astype(BF16)
        dho = dho_ref[...]
        df = (0.5 * dho).astype(BF16)
        df_ref[...] = df
        for j in range(nj):
            cols = slice(j * tc, (j + 1) * tc)
            da = lax.dot_general(df, wo_ref[cols, :], NT_DIMS, preferred_element_type=F32)
            gg = g_ref[:, cols]
            uu = u_ref[:, cols]
            sig = jax.nn.sigmoid(gg)
            sl = gg * sig
            a_ref[:, cols] = (sl * uu).astype(BF16)
            dgu_ref[0, :, cols] = ((da * uu) * (sig * (1.0 + gg * (1.0 - sig)))).astype(BF16)
            dgu_ref[1, :, cols] = (da * sl).astype(BF16)
        dn = (lax.dot_general(dgu_ref[0], wi_ref[:, :f], NT_DIMS, preferred_element_type=F32)
              + lax.dot_general(dgu_ref[1], wi_ref[:, f:], NT_DIMS, preferred_element_type=F32))
        dh, dw = _rms_bwd(dn, hh, nw)
        dhin_ref[...] = dho + dh
        dnw_ref[0] = dw

    resident = lambda a: pl.BlockSpec(a.shape, lambda i: (0, 0), pipeline_mode=pl.Buffered(1))
    row = lambda w: pl.BlockSpec((tm, w), lambda i: (i, 0))
    outs = pl.pallas_call(
        body,
        name="ffn_bwd",
        grid=(ni,),
        in_specs=[row(d), row(d), pl.BlockSpec((1, d), lambda i: (0, 0)), row(f), row(f),
                  resident(w_in), resident(w_out)] + [HBM_SPEC] * n_x,
        out_specs=[row(d), row(d), row(f), pl.BlockSpec((2, tm, f), lambda i: (0, i, 0)), row(d),
                   pl.BlockSpec((1, 1, d), lambda i: (i, 0, 0))] + [HBM_SPEC] * n_x,
        out_shape=[
            jax.ShapeDtypeStruct((t, d), F32),
            jax.ShapeDtypeStruct((t, d), BF16),
            jax.ShapeDtypeStruct((t, f), BF16),
            jax.ShapeDtypeStruct((2, t, f), BF16),
            jax.ShapeDtypeStruct((t, d), BF16),
            jax.ShapeDtypeStruct((ni, 1, d), F32),
        ] + (exchange.out_shape if exchange else []),
        scratch_shapes=exchange.scratch if exchange else [],
        compiler_params=_cparams(("arbitrary",) if exchange else ("parallel",)),
    )(dh_out, h, norm_w, g, u, w_in, w_out, *(exchange.ins if exchange else []))
    return outs[:6], outs[6:]


def _tn_matmul(a, b, name, exchange=None):
    t, k = a.shape
    split = b.ndim == 3
    n = 2 * b.shape[2] if split else b.shape[1]
    tk = _tile(k, 512, 128)
    tn = _tile(b.shape[-1], 1408, 128)
    per_half = b.shape[-1] // tn
    ni, nj = k // tk, n // tn
    n_x = len(exchange.ins) if exchange else 0

    def body(*refs):
        a_ref, b_ref, o_ref = refs[0], refs[1], refs[2 + n_x]
        if exchange:
            i, j = pl.program_id(0), pl.program_id(1)
            at_end = (i == ni - 1) & (j == nj - 1)
            _host_exchange(exchange, refs[2:2 + n_x], refs[3 + n_x:3 + 2 * n_x], refs[3 + 2 * n_x:],
                           (i == 0) & (j == 0), at_end, at_end)
        o_ref[...] = lax.dot_general(a_ref[...], b_ref[...], TN_DIMS, preferred_element_type=F32)

    if split:
        b_spec = pl.BlockSpec((None, t, tn), lambda i, j: (j // per_half, 0, j % per_half))
    else:
        b_spec = pl.BlockSpec((t, tn), lambda i, j: (0, j))
    outs = pl.pallas_call(
        body,
        name=name,
        grid=(ni, nj),
        in_specs=[pl.BlockSpec((t, tk), lambda i, j: (0, i)), b_spec] + [HBM_SPEC] * n_x,
        out_specs=[pl.BlockSpec((tk, tn), lambda i, j: (i, j))] + [HBM_SPEC] * n_x,
        out_shape=[jax.ShapeDtypeStruct((k, n), F32)] + (exchange.out_shape if exchange else []),
        scratch_shapes=exchange.scratch if exchange else [],
        compiler_params=_cparams(("arbitrary", "arbitrary") if exchange else ("parallel", "parallel")),
    )(a, b, *(exchange.ins if exchange else []))
    return (outs[0], outs[1:]) if exchange else outs[0]


A_SLOT = lambda h: h // A_GROUP
B_SLOT = lambda h: h % 2

PROJ_PARTS = (
    (OFF_QA, A_WIDTH, A_PAD_WIDTH, True, A_SLOT), (OFF_KA, A_KV_WIDTH, A_KV_WIDTH, True, None),
    (OFF_VA, A_KV_WIDTH, A_KV_WIDTH, True, None), (OFF_QB, B_WIDTH, B_WIDTH, True, None),
    (OFF_KB, B_WIDTH, B_PAD_WIDTH, True, B_SLOT), (OFF_VB, B_WIDTH, B_PAD_WIDTH, True, B_SLOT),
    (OFF_GA, D_MODEL, D_MODEL, False, None), (OFF_GB, D_MODEL, D_MODEL, False, None), (OFF_F, F_COLS, F_COLS, False, None),
)


def _head_tile(pair, head, slot):
    lane_slot = lax.broadcasted_iota(jnp.int32, pair.shape, 1) // HEAD_DIM
    moved = pair if head % 2 == slot else pltpu.roll(pair, HEAD_DIM, 1)
    return jnp.where(lane_slot == slot, moved, 0.0)


def _proj_fwd(h, norm_w, w_p):
    t, d = h.shape
    tm = _tile(t, 272)

    def body(h_ref, nw_ref, w_ref, u_ref, *part_refs):
        hh = h_ref[...]
        un = ((hh * _rms_scale(hh)) * nw_ref[...]).astype(BF16)
        u_ref[...] = un
        for (off, width, _, _, slot), p_ref in zip(PROJ_PARTS, part_refs):
            if slot is None:
                p_ref[...] = jnp.dot(un, w_ref[:, off:off + width], preferred_element_type=F32).astype(p_ref.dtype)
                continue
            part = jnp.dot(un, w_ref[:, off:off + width], preferred_element_type=F32)
            for pair in range(width // LANES):
                x = part[:, pair * LANES:(pair + 1) * LANES]
                for head in (2 * pair, 2 * pair + 1):
                    p_ref[:, head * LANES:(head + 1) * LANES] = _head_tile(x, head, slot(head)).astype(p_ref.dtype)

    row = lambda w: pl.BlockSpec((tm, w), lambda i: (i, 0))
    return pl.pallas_call(
        body,
        name="proj_fwd",
        grid=(t // tm,),
        in_specs=[row(d), pl.BlockSpec((1, d), lambda i: (0, 0)),
                  pl.BlockSpec(w_p.shape, lambda i: (0, 0), pipeline_mode=pl.Buffered(1))],
        out_specs=[row(d)] + [row(width) for _, _, width, _, _ in PROJ_PARTS],
        out_shape=[jax.ShapeDtypeStruct((t, d), BF16)]
        + [jax.ShapeDtypeStruct((t, width), BF16 if is_bf else F32) for _, _, width, is_bf, _ in PROJ_PARTS],
        compiler_params=_cparams(("parallel",)),
    )(h, norm_w, w_p)


def _proj_bwd(dh_out, h, norm_w, dproj, w_p, exchange=None):
    t, d = h.shape
    tm = _tile(t, 272)
    ni = t // tm
    n_p = len(P_PIECES)
    n_in = 4 + n_p
    n_x = len(exchange.ins) if exchange else 0

    def body(*refs):
        dho_ref, h_ref, nw_ref = refs[:3]
        dp_refs, w_ref = refs[3:3 + n_p], refs[3 + n_p]
        dhin_ref, dnw_ref = refs[n_in + n_x:n_in + 2 + n_x]
        if exchange:
            i = pl.program_id(0)
            _host_exchange(exchange, refs[n_in:n_in + n_x], refs[n_in + 2 + n_x:n_in + 2 + 2 * n_x],
                           refs[n_in + 2 + 2 * n_x:], i == 0, i == ni - 1, i == ni - 1)
        dn = None
        for dp_ref, (off, width) in zip(dp_refs, P_PIECES):
            part = lax.dot_general(dp_ref[...], w_ref[:, off:off + width], NT_DIMS, preferred_element_type=F32)
            dn = part if dn is None else dn + part
        dh, dw = _rms_bwd(dn, h_ref[...], nw_ref[...])
        dhin_ref[...] = dho_ref[...] + dh
        dnw_ref[0] = dw

    row = lambda w: pl.BlockSpec((tm, w), lambda i: (i, 0))
    outs = pl.pallas_call(
        body,
        name="proj_bwd",
        grid=(ni,),
        in_specs=[row(d), row(d), pl.BlockSpec((1, d), lambda i: (0, 0))] + [row(width) for _, width in P_PIECES]
        + [pl.BlockSpec(w_p.shape, lambda i: (0, 0), pipeline_mode=pl.Buffered(1))] + [HBM_SPEC] * n_x,
        out_specs=[row(d), pl.BlockSpec((1, 1, d), lambda i: (i, 0, 0))] + [HBM_SPEC] * n_x,
        out_shape=[jax.ShapeDtypeStruct((t, d), F32), jax.ShapeDtypeStruct((ni, 1, d), F32)]
        + (exchange.out_shape if exchange else []),
        scratch_shapes=exchange.scratch if exchange else [],
        compiler_params=_cparams(("arbitrary",) if exchange else ("parallel",)),
    )(dh_out, h, norm_w, *dproj, w_p, *(exchange.ins if exchange else []))
    return outs[:2], outs[2:]


def _merge_fwd(h, oa, ob, ga, gb, wa, wb, wo):
    t, d = h.shape
    tm = _tile(t, 544)

    def body(h_ref, oa_ref, ob_ref, ga_ref, gb_ref, wa_ref, wb_ref, wo_ref, hout_ref, mix_ref):
        ya = jnp.dot(oa_ref[...], wa_ref[...], preferred_element_type=F32)
        yb = jnp.dot(ob_ref[...], wb_ref[...], preferred_element_type=F32)
        mixed = (jax.nn.sigmoid(ga_ref[...]) * ya + jax.nn.sigmoid(gb_ref[...]) * yb).astype(BF16)
        mix_ref[...] = mixed
        hout_ref[...] = h_ref[...] + jnp.dot(mixed, wo_ref[...], preferred_element_type=F32)

    row = lambda w: pl.BlockSpec((tm, w), lambda i: (i, 0))
    full = lambda a: pl.BlockSpec(a.shape, lambda i: (0, 0))
    return pl.pallas_call(
        body,
        name="merge_fwd",
        grid=(t // tm,),
        in_specs=[row(d), row(oa.shape[1]), row(ob.shape[1]), row(d), row(d), full(wa), full(wb), full(wo)],
        out_specs=[row(d), row(d)],
        out_shape=[jax.ShapeDtypeStruct((t, d), F32), jax.ShapeDtypeStruct((t, d), BF16)],
        compiler_params=_cparams(("parallel",)),
    )(h, oa, ob, ga, gb, wa, wb, wo)


def _merge_bwd(dh, oa, ob, ga, gb, wa, wb, wo, exchange=None):
    t, d = dh.shape
    tm = _tile(t, 544)
    ni = t // tm
    n_x = len(exchange.ins) if exchange else 0

    def body(*refs):
        dh_ref, oa_ref, ob_ref, ga_ref, gb_ref, wa_ref, wb_ref, wo_ref = refs[:8]
        dya_ref, dyb_ref, doa_ref, dob_ref, dg_ref, dhb_ref = refs[8 + n_x:14 + n_x]
        if exchange:
            i = pl.program_id(0)
            _host_exchange(exchange, refs[8:8 + n_x], refs[14 + n_x:14 + 2 * n_x], refs[14 + 2 * n_x:],
                           i == 0, i == ni - 1, i == ni - 1)
        dhb = dh_ref[...].astype(BF16)
        dhb_ref[...] = dhb
        dmix = lax.dot_general(dhb, wo_ref[...], NT_DIMS, preferred_element_type=F32)
        for branch, (o_ref, g_ref, w_ref, dy_ref, do_ref) in enumerate((
                (oa_ref, ga_ref, wa_ref, dya_ref, doa_ref),
                (ob_ref, gb_ref, wb_ref, dyb_ref, dob_ref))):
            y = jnp.dot(o_ref[...], w_ref[...], preferred_element_type=F32)
            s = jax.nn.sigmoid(g_ref[...])
            dy = (dmix * s).astype(BF16)
            dy_ref[...] = dy
            dg_ref[:, branch * d:(branch + 1) * d] = ((dmix * y) * (s * (1.0 - s))).astype(BF16)
            do_ref[...] = lax.dot_general(dy, w_ref[...], NT_DIMS, preferred_element_type=F32).astype(BF16)

    row = lambda w: pl.BlockSpec((tm, w), lambda i: (i, 0))
    full = lambda a: pl.BlockSpec(a.shape, lambda i: (0, 0))
    wa_w, wb_w = oa.shape[1], ob.shape[1]
    outs = pl.pallas_call(
        body,
        name="merge_bwd",
        grid=(ni,),
        in_specs=[row(d), row(wa_w), row(wb_w), row(d), row(d), full(wa), full(wb), full(wo)] + [HBM_SPEC] * n_x,
        out_specs=[row(d), row(d), row(wa_w), row(wb_w), row(2 * d), row(d)] + [HBM_SPEC] * n_x,
        out_shape=[
            jax.ShapeDtypeStruct((t, d), BF16), jax.ShapeDtypeStruct((t, d), BF16),
            jax.ShapeDtypeStruct((t, wa_w), BF16), jax.ShapeDtypeStruct((t, wb_w), BF16),
            jax.ShapeDtypeStruct((t, 2 * d), BF16), jax.ShapeDtypeStruct((t, d), BF16),
        ] + (exchange.out_shape if exchange else []),
        scratch_shapes=exchange.scratch if exchange else [],
        compiler_params=_cparams(("arbitrary",) if exchange else ("parallel",)),
    )(dh, oa, ob, ga, gb, wa, wb, wo, *(exchange.ins if exchange else []))
    return outs[:6], outs[6:]


def _tri_dot(tri, x):
    hi = x.astype(BF16)
    r1 = x - hi.astype(F32)
    mid = r1.astype(BF16)
    lo = (r1 - mid.astype(F32)).astype(BF16)
    return (jnp.dot(tri, hi, preferred_element_type=F32)
            + jnp.dot(tri, mid, preferred_element_type=F32)
            + jnp.dot(tri, lo, preferred_element_type=F32))


def _forget_cumsum(f_logit, b_pad, nb):
    t, w = f_logit.shape
    bsz = t // (nb * BLOCK)

    def body(f_ref, b_ref, c_ref, carry):
        @pl.when(pl.program_id(0) == 0)
        def _():
            carry[...] = jnp.zeros_like(carry)

        rows = lax.broadcasted_iota(jnp.int32, (BLOCK, BLOCK), 0)
        cols = lax.broadcasted_iota(jnp.int32, (BLOCK, BLOCK), 1)
        tri = (cols <= rows).astype(BF16)
        for b in range(bsz):
            x = jax.nn.log_sigmoid(f_ref[b] + b_ref[...])
            c = _tri_dot(tri, x) + carry[b]
            c_ref[b] = c
            carry[b] = c[BLOCK - 1:BLOCK, :]

    block = pl.BlockSpec((bsz, BLOCK, w), lambda n: (0, n, 0))
    return pl.pallas_call(
        body,
        name="forget_cumsum",
        grid=(nb,),
        in_specs=[block, pl.BlockSpec((1, w), lambda n: (0, 0))],
        out_specs=block,
        out_shape=jax.ShapeDtypeStruct((bsz, nb * BLOCK, w), F32),
        scratch_shapes=[pltpu.VMEM((bsz, 1, w), F32)],
        compiler_params=_cparams(("arbitrary",)),
    )(f_logit.reshape(bsz, nb * BLOCK, w), b_pad).reshape(t, w)


def _forget_cumsum_bwd(dc, f_logit, b_pad, nb):
    t, w = f_logit.shape
    bsz = t // (nb * BLOCK)

    def body(dc_ref, f_ref, b_ref, df_ref, db_ref, carry):
        @pl.when(pl.program_id(0) == 0)
        def _():
            carry[...] = jnp.zeros_like(carry)
            db_ref[...] = jnp.zeros_like(db_ref)

        rows = lax.broadcasted_iota(jnp.int32, (BLOCK, BLOCK), 0)
        cols = lax.broadcasted_iota(jnp.int32, (BLOCK, BLOCK), 1)
        tri = (cols >= rows).astype(BF16)
        for b in range(bsz):
            dlf = _tri_dot(tri, dc_ref[b]) + carry[b]
            carry[b] = dlf[0:1, :]
            df = dlf * jax.nn.sigmoid(-(f_ref[b] + b_ref[...]))
            df_ref[b] = df.astype(BF16)
            db_ref[b] += jnp.sum(df, axis=0, keepdims=True)

    l = nb * BLOCK
    rev = pl.BlockSpec((bsz, BLOCK, w), lambda n: (0, nb - 1 - n, 0))
    df, db = pl.pallas_call(
        body,
        name="forget_cumsum_bwd",
        grid=(nb,),
        in_specs=[rev, rev, pl.BlockSpec((1, w), lambda n: (0, 0))],
        out_specs=[rev, pl.BlockSpec((bsz, 1, w), lambda n: (0, 0, 0))],
        out_shape=[jax.ShapeDtypeStruct((bsz, l, w), BF16), jax.ShapeDtypeStruct((bsz, 1, w), F32)],
        scratch_shapes=[pltpu.VMEM((bsz, 1, w), F32)],
        compiler_params=_cparams(("arbitrary",)),
    )(dc.reshape(bsz, l, w), f_logit.reshape(bsz, l, w), b_pad)
    return df.reshape(t, w), db


GROUP_ROWS = A_GROUP * BLOCK


def _stack_heads(ref, g):
    return jnp.concatenate([ref[:, (A_GROUP * g + i) * LANES:(A_GROUP * g + i + 1) * LANES] for i in range(A_GROUP)],
                           axis=0)


def _unstack_heads(ref, g, x):
    for i in range(A_GROUP):
        ref[:, (A_GROUP * g + i) * LANES:(A_GROUP * g + i + 1) * LANES] = x[i * BLOCK:(i + 1) * BLOCK].astype(ref.dtype)


def _swa_logits(qk, slope, n):
    qi = lax.broadcasted_iota(jnp.int32, (GROUP_ROWS, BLOCK), 0) & (BLOCK - 1)
    kj = lax.broadcasted_iota(jnp.int32, (GROUP_ROWS, BLOCK), 1)
    s_all = qk * SCALE
    out = []
    for i, (dist, ok) in enumerate((
            (n * BLOCK + qi - kj, (kj >= N_PAD) & (n * BLOCK + qi - kj >= 0)),
            (BLOCK + qi - kj, (kj > qi) & (n >= 2)),
            (qi - kj, (kj <= qi) & (n >= 1)))):
        s = s_all[:, i * BLOCK:(i + 1) * BLOCK] - slope * dist.astype(F32)
        out.append(jnp.where(ok, s, NEG))
    return out


def _three_blocks(m_ref, p_ref, c_ref):
    return jnp.concatenate([m_ref[...], p_ref[...], c_ref[...]], axis=0)


def _swa_specs(bsz, nb):
    qspec = pl.BlockSpec((bsz, BLOCK, A_PAD_WIDTH), lambda n: (0, n, 0))
    kv_m = pl.BlockSpec((bsz, BLOCK, LANES), lambda n: (0, 0, 0))
    kv_p = pl.BlockSpec((bsz, BLOCK, LANES), lambda n: (0, jnp.maximum(n - 1, 0), 0))
    kv_c = pl.BlockSpec((bsz, BLOCK, LANES), lambda n: (0, n, 0))
    rowspec = pl.BlockSpec((A_KV_HEADS, GROUP_ROWS, 1), lambda n: (0, 0, 0))
    lsespec = pl.BlockSpec((bsz, 1, A_KV_HEADS, GROUP_ROWS, 1), lambda n: (0, n, 0, 0, 0))
    return qspec, kv_m, kv_p, kv_c, rowspec, lsespec


def _swa_fwd(q, k, v, sink_rows, slope_rows, nb):
    t = q.shape[0]
    l = nb * BLOCK
    bsz = t // l

    def body(q_ref, km_ref, kp_ref, kc_ref, vm_ref, vp_ref, vc_ref, sink_ref, slope_ref, o_ref, lse_ref):
        n = pl.program_id(0)
        lane_group = lax.broadcasted_iota(jnp.int32, (GROUP_ROWS, LANES), 1) // HEAD_DIM
        products = {}
        for b in range(bsz):
            keys = _three_blocks(km_ref.at[b], kp_ref.at[b], kc_ref.at[b])
            for g in range(A_KV_HEADS):
                products[b, g] = lax.dot_general(_stack_heads(q_ref.at[b], g), keys, NT_DIMS,
                                                 preferred_element_type=F32)
        for b in range(bsz):
            values = _three_blocks(vm_ref.at[b], vp_ref.at[b], vc_ref.at[b])
            for g in range(A_KV_HEADS):
                sink = sink_ref[g]
                s_m, s_p, s_c = _swa_logits(products[b, g], slope_ref[g], n)
                m = jnp.maximum(jnp.max(jnp.maximum(jnp.maximum(s_m, s_p), s_c), axis=-1, keepdims=True), sink)
                m_wide = jnp.broadcast_to(m, (GROUP_ROWS, BLOCK))
                e_m = jnp.exp(s_m - m_wide)
                e_p = jnp.exp(s_p - m_wide)
                e_c = jnp.exp(s_c - m_wide)
                z = jnp.sum((e_m + e_p) + e_c, axis=-1, keepdims=True) + jnp.exp(sink - m)
                inv = jnp.broadcast_to(1.0 / z, (GROUP_ROWS, BLOCK))
                probs = jnp.concatenate([(e_m * inv).astype(BF16), (e_p * inv).astype(BF16),
                                         (e_c * inv).astype(BF16)], axis=1)
                o = jnp.dot(probs, values, preferred_element_type=F32)
                _unstack_heads(o_ref.at[b], g, jnp.where(lane_group == g, o, 0.0))
                lse_ref[b, 0, g] = m + jnp.log(z)

    qspec, kv_m, kv_p, kv_c, rowspec, lsespec = _swa_specs(bsz, nb)
    by_example = lambda a: a.reshape(bsz, l, a.shape[1])
    q3, k3, v3 = by_example(q), by_example(k), by_example(v)
    o, lse = pl.pallas_call(
        body,
        name="swa_fwd",
        grid=(nb,),
        in_specs=[qspec, kv_m, kv_p, kv_c, kv_m, kv_p, kv_c, rowspec, rowspec],
        out_specs=[qspec, lsespec],
        out_shape=[jax.ShapeDtypeStruct((bsz, l, A_PAD_WIDTH), BF16),
                   jax.ShapeDtypeStruct((bsz, nb, A_KV_HEADS, GROUP_ROWS, 1), F32)],
        compiler_params=_cparams(("arbitrary",)),
    )(q3, k3, k3, k3, v3, v3, v3, sink_rows, slope_rows)
    return o.reshape(t, A_PAD_WIDTH), lse


def _swa_bwd(q, k, v, do, lse, sink_rows, slope_rows, nb):
    t = q.shape[0]
    l = nb * BLOCK
    bsz = t // l

    def body(q_ref, km_ref, kp_ref, kc_ref, vm_ref, vp_ref, vc_ref, do_ref, lse_ref, sink_ref, slope_ref,
             dq_ref, dk_ref, dv_ref, dsink_ref, dk_acc, dv_acc):
        n = pl.program_id(0)

        @pl.when(n == 0)
        def _():
            dk_acc[...] = jnp.zeros_like(dk_acc)
            dv_acc[...] = jnp.zeros_like(dv_acc)
            dsink_ref[...] = jnp.zeros_like(dsink_ref)

        first_half = lax.broadcasted_iota(jnp.int32, (BLOCK, LANES), 1) < HEAD_DIM
        prev = jnp.maximum(n - 1, 0)
        products = {}
        for b in range(bsz):
            keys = _three_blocks(km_ref.at[b], kp_ref.at[b], kc_ref.at[b])
            values = _three_blocks(vm_ref.at[b], vp_ref.at[b], vc_ref.at[b])
            for g in range(A_KV_HEADS):
                products[b, g] = (
                    lax.dot_general(_stack_heads(q_ref.at[b], g), keys, NT_DIMS, preferred_element_type=F32),
                    lax.dot_general(_stack_heads(do_ref.at[b], g), values, NT_DIMS, preferred_element_type=F32))
        for b in range(bsz):
            keys = _three_blocks(km_ref.at[b], kp_ref.at[b], kc_ref.at[b])
            for g in range(A_KV_HEADS):
                qq = _stack_heads(q_ref.at[b], g)
                dob = _stack_heads(do_ref.at[b], g)
                lse_g = lse_ref[b, 0, g]
                lse_wide = jnp.broadcast_to(lse_g, (GROUP_ROWS, BLOCK))
                qk, dp_all = products[b, g]
                probs = [jnp.exp(s - lse_wide) for s in _swa_logits(qk, slope_ref[g], n)]
                dps = [dp_all[:, i * BLOCK:(i + 1) * BLOCK] for i in range(3)]
                delta = jnp.sum((probs[0] * dps[0] + probs[1] * dps[1]) + probs[2] * dps[2], axis=-1, keepdims=True)
                delta_wide = jnp.broadcast_to(delta, (GROUP_ROWS, BLOCK))
                ds = jnp.concatenate([(p * (dp - delta_wide)).astype(BF16) for p, dp in zip(probs, dps)], axis=1)
                pb = jnp.concatenate([p.astype(BF16) for p in probs], axis=1)
                dq = jnp.dot(ds, keys, preferred_element_type=F32) * SCALE
                dk_all = lax.dot_general(ds, qq, TN_DIMS, preferred_element_type=F32) * SCALE
                dv_all = lax.dot_general(pb, dob, TN_DIMS, preferred_element_type=F32)
                for i, start in enumerate((0, prev * BLOCK, n * BLOCK)):
                    rows = pl.ds(pl.multiple_of(start, BLOCK), BLOCK)
                    dk_acc[b, rows, :] += dk_all[i * BLOCK:(i + 1) * BLOCK]
                    dv_acc[b, rows, :] += dv_all[i * BLOCK:(i + 1) * BLOCK]
                for pair in range(A_GROUP // 2):
                    even = dq[2 * pair * BLOCK:(2 * pair + 1) * BLOCK]
                    odd = dq[(2 * pair + 1) * BLOCK:(2 * pair + 2) * BLOCK]
                    left = even if g == 0 else pltpu.roll(even, HEAD_DIM, 1)
                    right = pltpu.roll(odd, HEAD_DIM, 1) if g == 0 else odd
                    tile = (A_GROUP // 2) * g + pair
                    dq_ref[b, :, tile * LANES:(tile + 1) * LANES] = jnp.where(first_half, left, right).astype(BF16)
                dsink_ref[b, g] += -(jnp.exp(sink_ref[g] - lse_g) * delta)

        @pl.when(n == nb - 1)
        def _():
            dk_ref[...] = dk_acc[...].astype(BF16)
            dv_ref[...] = dv_acc[...].astype(BF16)

    qspec, kv_m, kv_p, kv_c, rowspec, lsespec = _swa_specs(bsz, nb)
    kv_all = pl.BlockSpec((bsz, l, LANES), lambda n: (0, 0, 0))
    by_example = lambda a: a.reshape(bsz, l, a.shape[1])
    q3, k3, v3 = by_example(q), by_example(k), by_example(v)
    dq, dk, dv, dsink = pl.pallas_call(
        body,
        name="swa_bwd",
        grid=(nb,),
        in_specs=[qspec, kv_m, kv_p, kv_c, kv_m, kv_p, kv_c, qspec, lsespec, rowspec, rowspec],
        out_specs=[pl.BlockSpec((bsz, BLOCK, A_WIDTH), lambda n: (0, n, 0)), kv_all, kv_all,
                   pl.BlockSpec((bsz, A_KV_HEADS, GROUP_ROWS, 1), lambda n: (0, 0, 0, 0))],
        out_shape=[jax.ShapeDtypeStruct((bsz, l, A_WIDTH), BF16),
                   jax.ShapeDtypeStruct((bsz, l, LANES), BF16),
                   jax.ShapeDtypeStruct((bsz, l, LANES), BF16),
                   jax.ShapeDtypeStruct((bsz, A_KV_HEADS, GROUP_ROWS, 1), F32)],
        scratch_shapes=[pltpu.VMEM((bsz, l, LANES), F32), pltpu.VMEM((bsz, l, LANES), F32)],
        compiler_params=_cparams(("arbitrary",)),
    )(q3, k3, k3, k3, v3, v3, v3, by_example(do), lse, sink_rows, slope_rows)
    return dq.reshape(t, A_WIDTH), dk.reshape(t, LANES), dv.reshape(t, LANES), dsink


CHUNK = KEY_BLOCKS * BLOCK


def _fox_chunk(qb, ci):
    sb = jnp.maximum(jnp.minimum(KEY_BLOCKS * ci, qb + 1 - KEY_BLOCKS), 0)
    lo = jnp.maximum(ci * CHUNK, N_PAD)
    return sb, lo, pl.ds(pl.multiple_of(sb * BLOCK, BLOCK), CHUNK)


def _fox_logits(s_ref, cr_ref, e, j, sb, lo, qb):
    lane = lax.broadcasted_iota(jnp.int32, (BLOCK, BLOCK), 1)
    ahead = lane - lax.broadcasted_iota(jnp.int32, (BLOCK, BLOCK), 0)
    first = (sb + j) * BLOCK
    s = s_ref[e, :, j * BLOCK:(j + 1) * BLOCK] - cr_ref[e, sb + j]
    return jnp.where((ahead <= qb * BLOCK - first) & (lane >= lo - first), s, NEG)


FOX_PAIRS = 4
FOX_HEADS = 2 * FOX_PAIRS
FOX_STEPS = B_PAIRS // FOX_PAIRS


def _fox_specs(nb):
    l = nb * BLOCK
    q_spec = pl.BlockSpec((BLOCK, FOX_PAIRS * LANES), lambda b, p, i: (b * nb + i, p))
    kv_spec = pl.BlockSpec((l, FOX_HEADS * LANES), lambda b, p, i: (b, p))
    cc_spec = pl.BlockSpec((FOX_HEADS, BLOCK, 1), lambda b, p, i: (b * FOX_STEPS + p, i, 0))
    cr_spec = pl.BlockSpec((FOX_HEADS, nb, 1, BLOCK), lambda b, p, i: (b * FOX_STEPS + p, 0, 0, 0))
    return q_spec, kv_spec, cc_spec, cr_spec


def _fox_fwd(q, k, v, c_row, nb, exchange=None):
    t = q.shape[0]
    bsz = t // (nb * BLOCK)
    assert nb >= KEY_BLOCKS

    n_x = len(exchange.ins) if exchange else 0

    def body(*refs):
        q_ref, k_ref, v_ref, cr_ref = refs[:4]
        o_ref, ox_ref, lse_ref = refs[4 + n_x:7 + n_x]
        s_scr, hi_scr, lo_scr = refs[7 + 2 * n_x:10 + 2 * n_x]
        qb = pl.program_id(2)
        if exchange:
            first = (pl.program_id(0) == 0) & (pl.program_id(1) == 0)
            last = (pl.program_id(0) == bsz - 1) & (pl.program_id(1) == FOX_STEPS - 1)
            _host_exchange(exchange, refs[4:4 + n_x], refs[7 + n_x:7 + 2 * n_x], refs[10 + 2 * n_x:],
                           first & (qb == 0), first & (qb == 2 * nb // 3), last & (qb == nb - 1),
                           late=last & (qb == 0))
        qs = [q_ref[:, a * LANES:(a + 1) * LANES] * SCALE for a in range(FOX_PAIRS)]
        first_half = lax.broadcasted_iota(jnp.int32, (BLOCK, LANES), 1) < HEAD_DIM

        def step(ci, carry):
            stats, accs = carry[:2 * FOX_HEADS], carry[2 * FOX_HEADS:]
            sb, lo, krows = _fox_chunk(qb, ci)
            for e in range(FOX_HEADS):
                s_scr[e] = lax.dot_general(qs[e // 2], k_ref[krows, e * LANES:(e + 1) * LANES], NT_DIMS,
                                           preferred_element_type=F32)
            new_stats, new_accs = [], []
            for a in range(FOX_PAIRS):
                alphas = []
                pv = jnp.zeros((BLOCK, LANES), F32)
                pv_lo = jnp.zeros((BLOCK, LANES), F32)
                for e in (2 * a, 2 * a + 1):
                    m, z = stats[2 * e], stats[2 * e + 1]
                    tile = slice(e * LANES, (e + 1) * LANES)
                    top = None
                    for j in range(KEY_BLOCKS):
                        s = _fox_logits(s_scr, cr_ref, e, j, sb, lo, qb)
                        s_scr[e, :, j * BLOCK:(j + 1) * BLOCK] = s
                        top = s if top is None else jnp.maximum(top, s)
                    m_new = jnp.maximum(m, jnp.max(top, axis=-1, keepdims=True))
                    alpha = jnp.exp(m - m_new)
                    m_wide = jnp.broadcast_to(m_new, (BLOCK, BLOCK))
                    total = None
                    for j in range(KEY_BLOCKS):
                        cols = slice(j * BLOCK, (j + 1) * BLOCK)
                        p = jnp.exp(s_scr[e, :, cols] - m_wide)
                        total = p if total is None else total + p
                        hi = p.astype(BF16)
                        hi_scr[e, :, cols] = hi
                        lo_scr[e, :, cols] = (p - hi.astype(F32)).astype(BF16)
                    z = alpha * z + jnp.sum(total, axis=-1, keepdims=True)
                    vv = v_ref[krows, tile]
                    pv = pv + jnp.dot(hi_scr[e], vv, preferred_element_type=F32)
                    pv_lo = pv_lo + jnp.dot(lo_scr[e], vv, preferred_element_type=F32)
                    new_stats += [m_new, z]
                    alphas.append(alpha)
                alpha = jnp.where(first_half, alphas[0], alphas[1])
                new_accs += [alpha * accs[2 * a] + pv, alpha * accs[2 * a + 1] + pv_lo]
            return (*new_stats, *new_accs)

        col = lambda val: jnp.full((BLOCK, 1), val, F32)
        done = lax.fori_loop(
            0, (qb + KEY_BLOCKS) // KEY_BLOCKS, step,
            (col(NEG), col(0.0)) * FOX_HEADS + (jnp.zeros((BLOCK, LANES), F32),) * (2 * FOX_PAIRS))
        for a in range(FOX_PAIRS):
            m0, z0, m1, z1 = done[4 * a:4 * a + 4]
            acc, acc_lo = done[2 * FOX_HEADS + 2 * a:2 * FOX_HEADS + 2 * a + 2]
            inv = 1.0 / jnp.where(first_half, z0, z1)
            tile = slice(a * LANES, (a + 1) * LANES)
            o_ref[:, tile] = (acc * inv).astype(BF16)
            ox_ref[:, tile] = (acc + acc_lo) * inv
            lse_ref[2 * a] = m0 + jnp.log(z0)
            lse_ref[2 * a + 1] = m1 + jnp.log(z1)

    q_spec, kv_spec, cc_spec, cr_spec = _fox_specs(nb)
    outs = pl.pallas_call(
        body,
        name="fox_fwd",
        grid=(bsz, FOX_STEPS, nb),
        in_specs=[q_spec, kv_spec, kv_spec, cr_spec] + [HBM_SPEC] * n_x,
        out_specs=[q_spec, q_spec, cc_spec] + [HBM_SPEC] * n_x,
        out_shape=[jax.ShapeDtypeStruct((t, B_WIDTH), BF16), jax.ShapeDtypeStruct((t, B_WIDTH), F32),
                   jax.ShapeDtypeStruct((bsz * B_HEADS, nb * BLOCK, 1), F32)] + (exchange.out_shape if exchange else []),
        scratch_shapes=[pltpu.VMEM((FOX_HEADS, BLOCK, CHUNK), F32), pltpu.VMEM((FOX_HEADS, BLOCK, CHUNK), BF16),
                        pltpu.VMEM((FOX_HEADS, BLOCK, CHUNK), BF16)] + (exchange.scratch if exchange else []),
        compiler_params=_cparams(("arbitrary",) * 3 if exchange else ("parallel", "parallel", "arbitrary")),
    )(q, k, v, c_row, *(exchange.ins if exchange else []))
    return outs[:3], outs[3:]


def _fox_bwd(q, k, v, o_exact, do, lse, c_row, nb, exchange=None):
    t = q.shape[0]
    l = nb * BLOCK
    bsz = t // l

    n_x = len(exchange.ins) if exchange else 0

    def body(*refs):
        q_ref, k_ref, v_ref, ox_ref, do_ref, lse_ref, cr_ref = refs[:7]
        dq_ref, dk_ref, dv_ref, dc_ref = refs[7 + n_x:11 + n_x]
        dk_acc, dv_acc, s_scr, dp_scr, p_scr, ds_scr, dq_scr = refs[11 + 2 * n_x:18 + 2 * n_x]
        qb = pl.program_id(2)
        if exchange:
            first = (pl.program_id(0) == 0) & (pl.program_id(1) == 0)
            last = (pl.program_id(0) == bsz - 1) & (pl.program_id(1) == FOX_STEPS - 1)
            _host_exchange(exchange, refs[7:7 + n_x], refs[11 + n_x:11 + 2 * n_x], refs[18 + 2 * n_x:],
                           first & (qb == 0), last & (qb == 0), last & (qb == nb - 1))

        @pl.when(qb == 0)
        def _():
            dk_acc[...] = jnp.zeros_like(dk_acc)
            dv_acc[...] = jnp.zeros_like(dv_acc)
            dc_ref[...] = jnp.zeros_like(dc_ref)

        top_half = lax.broadcasted_iota(jnp.int32, (LANES, BLOCK), 0) < HEAD_DIM
        pair_t = lambda x: jnp.concatenate([jnp.where(top_half, x.T, 0), jnp.where(top_half, 0, x.T)], axis=1)
        first_half = lax.broadcasted_iota(jnp.int32, (BLOCK, LANES), 1) < HEAD_DIM
        wide = lambda col: jnp.broadcast_to(col, (BLOCK, BLOCK))
        qs, dobs, qs_t, dob_t, deltas = [], [], [], [], []
        for a in range(FOX_PAIRS):
            tile = slice(a * LANES, (a + 1) * LANES)
            qs.append(q_ref[:, tile] * SCALE)
            dobs.append(do_ref[:, tile])
            qs_t.append(pair_t(qs[a]))
            dob_t.append(pair_t(dobs[a]))
            weighted = dobs[a].astype(F32) * ox_ref[:, tile]
            deltas += [wide(jnp.sum(jnp.where(first_half, weighted, 0.0), axis=-1, keepdims=True)),
                       wide(jnp.sum(jnp.where(first_half, 0.0, weighted), axis=-1, keepdims=True))]
        lses = [wide(lse_ref[e]) for e in range(FOX_HEADS)]

        dq_scr[...] = jnp.zeros(dq_scr.shape, F32)

        def step(ci, carry):
            sb, lo, krows = _fox_chunk(qb, ci)
            for e in range(FOX_HEADS):
                tile = slice(e * LANES, (e + 1) * LANES)
                s_scr[e] = lax.dot_general(qs[e // 2], k_ref[krows, tile], NT_DIMS, preferred_element_type=F32)
                dp_scr[e] = lax.dot_general(dobs[e // 2], v_ref[krows, tile], NT_DIMS, preferred_element_type=F32)
            for a in range(FOX_PAIRS):
                for e in (2 * a, 2 * a + 1):
                    tile = slice(e * LANES, (e + 1) * LANES)
                    kk = k_ref[krows, tile]
                    for j in range(KEY_BLOCKS):
                        cols = slice(j * BLOCK, (j + 1) * BLOCK)
                        p = jnp.exp(_fox_logits(s_scr, cr_ref, e, j, sb, lo, qb) - lses[e])
                        ds = p * (dp_scr[e, :, cols] - deltas[e])
                        dc_ref[e, sb + j] -= jnp.sum(ds, axis=0, keepdims=True)
                        p_scr[e, :, cols] = p.astype(BF16)
                        ds_scr[e, :, cols] = ds.astype(BF16)
                    dq_scr[a] += jnp.dot(ds_scr[e], kk, preferred_element_type=F32)
                both = slice(2 * a, 2 * a + 2)
                dk_t = jnp.dot(qs_t[a], ds_scr[both].reshape(2 * BLOCK, CHUNK), preferred_element_type=F32)
                dv_t = jnp.dot(dob_t[a], p_scr[both].reshape(2 * BLOCK, CHUNK), preferred_element_type=F32)
                for j in range(KEY_BLOCKS):
                    cols = slice(j * BLOCK, (j + 1) * BLOCK)
                    dk_acc[a * nb + sb + j] += dk_t[:, cols]
                    dv_acc[a * nb + sb + j] += dv_t[:, cols]
            return carry

        lax.fori_loop(0, (qb + KEY_BLOCKS) // KEY_BLOCKS, step, 0)
        for a in range(FOX_PAIRS):
            dq_ref[:, a * LANES:(a + 1) * LANES] = (dq_scr[a] * SCALE).astype(BF16)

        @pl.when(qb == nb - 1)
        def _():
            for a in range(FOX_PAIRS):
                for kb in range(nb):
                    rows = slice(kb * BLOCK, (kb + 1) * BLOCK)
                    for acc, out_ref in ((dk_acc, dk_ref), (dv_acc, dv_ref)):
                        out_ref[rows, a * LANES:(a + 1) * LANES] = acc[a * nb + kb].T.astype(BF16)

    q_spec, kv_spec, cc_spec, cr_spec = _fox_specs(nb)
    dkv_spec = pl.BlockSpec((l, FOX_PAIRS * LANES), lambda b, p, i: (b, p))
    outs = pl.pallas_call(
        body,
        name="fox_bwd",
        grid=(bsz, FOX_STEPS, nb),
        in_specs=[q_spec, kv_spec, kv_spec, q_spec, q_spec, cc_spec, cr_spec] + [HBM_SPEC] * n_x,
        out_specs=[q_spec, dkv_spec, dkv_spec, cr_spec] + [HBM_SPEC] * n_x,
        out_shape=[jax.ShapeDtypeStruct((t, B_WIDTH), BF16), jax.ShapeDtypeStruct((t, B_WIDTH), BF16),
                   jax.ShapeDtypeStruct((t, B_WIDTH), BF16),
                   jax.ShapeDtypeStruct((bsz * B_HEADS, nb, 1, BLOCK), F32)] + (exchange.out_shape if exchange else []),
        scratch_shapes=[pltpu.VMEM((FOX_PAIRS * nb, LANES, BLOCK), F32), pltpu.VMEM((FOX_PAIRS * nb, LANES, BLOCK), F32),
                        pltpu.VMEM((FOX_HEADS, BLOCK, CHUNK), F32), pltpu.VMEM((FOX_HEADS, BLOCK, CHUNK), F32),
                        pltpu.VMEM((FOX_HEADS, BLOCK, CHUNK), BF16), pltpu.VMEM((FOX_HEADS, BLOCK, CHUNK), BF16),
                        pltpu.VMEM((FOX_PAIRS, BLOCK, LANES), F32)]
        + (exchange.scratch if exchange else []),
        compiler_params=_cparams(("arbitrary",) * 3 if exchange else ("parallel", "parallel", "arbitrary")),
    )(q, k, v, o_exact, do, lse, c_row, *(exchange.ins if exchange else []))
    return outs[:4], outs[4:]


def _loss_head(h, final_w, target):
    bsz, l, d = h.shape
    nb = l // BLOCK

    def body(h_ref, w_ref, t_ref, loss_ref, dh_ref, dw_ref):
        n = pl.program_id(0)

        @pl.when(n == 0)
        def _():
            loss_ref[...] = jnp.zeros_like(loss_ref)
            dw_ref[...] = jnp.zeros_like(dw_ref)
            dh_ref[...] = jnp.zeros_like(dh_ref)

        @pl.when(n > 0)
        def _():
            w = w_ref[...]
            for b in range(bsz):
                hh = h_ref[b]
                r = _rms_scale(hh)
                err = (hh * r) * w - t_ref[b]
                loss_ref[...] += 0.5 * jnp.sum(jnp.mean(err * err, axis=-1, keepdims=True), axis=0, keepdims=True)
                dy = err * (1.0 / d)
                dh, dw = _rms_bwd(dy, hh, w)
                dh_ref[b] = dh
                dw_ref[...] += dw

    return pl.pallas_call(
        body,
        name="loss_head",
        grid=(nb,),
        in_specs=[
            pl.BlockSpec((bsz, BLOCK, d), lambda n: (0, n, 0)),
            pl.BlockSpec((1, d), lambda n: (0, 0)),
            pl.BlockSpec((bsz, BLOCK, d), lambda n: (0, jnp.maximum(n - 1, 0), 0)),
        ],
        out_specs=[
            pl.BlockSpec((1, 128), lambda n: (0, 0)),
            pl.BlockSpec((bsz, BLOCK, d), lambda n: (0, n, 0)),
            pl.BlockSpec((1, d), lambda n: (0, 0)),
        ],
        out_shape=[jax.ShapeDtypeStruct((1, 128), F32), jax.ShapeDtypeStruct((bsz, l, d), F32),
                   jax.ShapeDtypeStruct((1, d), F32)],
        compiler_params=_cparams(("arbitrary",)),
    )(h, final_w, target)


def _pad_tiles(w, src, heads, lane_slot, axis):
    pieces = []
    for h in range(heads):
        x = lax.slice_in_dim(w, src + HEAD_DIM * h, src + HEAD_DIM * (h + 1), axis=axis)
        z = jnp.zeros_like(x)
        pieces += [x, z] if lane_slot(h) == 0 else [z, x]
    return pieces


def _unpad_tiles(g, off, heads, lane_slot, axis):
    return [lax.slice_in_dim(g, off + LANES * h + HEAD_DIM * lane_slot(h),
                             off + LANES * h + HEAD_DIM * (lane_slot(h) + 1), axis=axis) for h in range(heads)]


REF_RUNS = ((0, SRC_QB, 0, 0), (SRC_QB, SRC_F, 1, 0), (SRC_F, SRC_GA, 0, OFF_F), (SRC_GA, W_IN_COLS, 2, 0))
P_RUNS = ((0, OFF_F, 0), (OFF_F, OFF_F + B_HEADS, SRC_F), (OFF_QB, OFF_GA, SRC_QB), (OFF_GA, P_COLS, SRC_GA))
SHARD_COLS = W_IN_COLS // N_CHIPS
SHARD_PAD_COLS = -(-SHARD_COLS // LANES) * LANES


def _place(tile, lane, src_ref, src, dst, length):
    for t in range(src // LANES, (src + length - 1) // LANES + 1):
        x = src_ref[:, t * LANES:(t + 1) * LANES].astype(F32)
        shift = (dst - src) % LANES
        moved = pltpu.roll(x, shift, 1) if shift else x
        from_t = (lane >= max(dst, dst + t * LANES - src)) & (lane < min(dst + length, dst + (t + 1) * LANES - src))
        tile = jnp.where(from_t, moved, tile)
    return tile


def _layout_w_in(stacked):
    d = stacked.shape[1]
    rows = _tile(d, 256)

    def body(s_ref, o_ref):
        lane = lax.broadcasted_iota(jnp.int32, (rows, LANES), 1)
        for lo in range(0, P_COLS, LANES):
            tile = jnp.zeros((rows, LANES), F32)
            for first, end, ref_col in P_RUNS:
                start, stop = max(lo, first), min(lo + LANES, end)
                while start < stop:
                    k, src = divmod(ref_col + start - first, SHARD_COLS)
                    length = min(stop - start, SHARD_COLS - src)
                    tile = _place(tile, lane, s_ref.at[k], src, start - lo, length)
                    start += length
            o_ref[:, lo:lo + LANES] = tile.astype(o_ref.dtype)

    return pl.pallas_call(
        body,
        name="layout_w_in",
        grid=(d // rows,),
        in_specs=[pl.BlockSpec((N_CHIPS, rows, SHARD_PAD_COLS), lambda i: (0, i, 0))],
        out_specs=pl.BlockSpec((rows, P_COLS), lambda i: (i, 0)),
        out_shape=jax.ShapeDtypeStruct((d, P_COLS), stacked.dtype),
        compiler_params=_cparams(("parallel",)),
    )(stacked)


def _stack_w_in_grad(pieces):
    d = pieces[0].shape[0]
    rows = _tile(d, 256)

    def body(*refs):
        piece_refs, o_ref = refs[:-1], refs[-1]
        lane = lax.broadcasted_iota(jnp.int32, (rows, LANES), 1)
        for k in range(N_CHIPS):
            for j in range(0, SHARD_COLS, LANES):
                width = min(LANES, SHARD_COLS - j)
                lo = k * SHARD_COLS + j
                tile = jnp.zeros((rows, LANES), F32)
                for first, end, piece, at in REF_RUNS:
                    start, stop = max(lo, first), min(lo + width, end)
                    if start >= stop:
                        continue
                    tile = _place(tile, lane, piece_refs[piece], at + start - first, start - lo, stop - start)
                o_ref[k, :, j:j + width] = tile[:, :width]

    return pl.pallas_call(
        body,
        name="stack_w_in_grad",
        grid=(d // rows,),
        in_specs=[pl.BlockSpec((rows, p.shape[1]), lambda i: (i, 0)) for p in pieces],
        out_specs=pl.BlockSpec((N_CHIPS, rows, SHARD_COLS), lambda i: (0, i, 0)),
        out_shape=jax.ShapeDtypeStruct((N_CHIPS, d, SHARD_COLS), F32),
        compiler_params=_cparams(("parallel",)),
    )(*pieces)


def _local_step(x, target, meta, norms, b_forget, sinks, w, comm=None):
    n1, nmix, n2, nfin = norms
    w1i, w1o = w[:2]
    if meta is not None:
        x = jnp.concatenate([jnp.zeros((x.shape[0], N_PAD, x.shape[2]), F32),
                             jnp.broadcast_to(meta[None], (x.shape[0], N_META, x.shape[2])), x], axis=1)
    bsz, l, d = x.shape
    nb = l // BLOCK
    t = bsz * l
    h0 = x.reshape(t, d)

    if comm is None:
        (h1, g1, u1), _ = _ffn_fwd(h0, n1, w1i, w1o)
        w_in, wa, wb, wo, w2i, w2o = w[2:]
        w_in = jnp.pad(w_in.reshape(d, N_CHIPS, SHARD_COLS).transpose(1, 0, 2),
                       ((0, 0), (0, 0), (0, SHARD_PAD_COLS - SHARD_COLS)))
    else:
        (h1, g1, u1), (w_in,) = _ffn_fwd(h0, n1, w1i, w1o, comm.gather(GATHER_PROJ))
    wp = _layout_w_in(w_in)
    un, qa, ka, va, qb, kb, vb, ga, gb, f_logit = _proj_fwd(h1, nmix, wp)
    b_pad = jnp.concatenate([b_forget, jnp.zeros((1, F_COLS - B_HEADS), F32)], axis=1)
    c = _forget_cumsum(f_logit, b_pad, nb)
    c_heads = c[:, :B_HEADS].reshape(bsz, l, B_HEADS).transpose(0, 2, 1).reshape(bsz * B_HEADS, l)
    c_row = c_heads.reshape(bsz * B_HEADS, nb, 1, BLOCK)

    slopes = jnp.exp2(-8.0 * jnp.arange(1, A_HEADS + 1, dtype=F32) / A_HEADS)
    slope_rows = jnp.repeat(slopes.reshape(A_KV_HEADS, A_GROUP), BLOCK, axis=1)[:, :, None]
    sink_rows = jnp.repeat(sinks.reshape(A_KV_HEADS, A_GROUP), BLOCK, axis=1)[:, :, None]

    oa, lse_a = _swa_fwd(qa, ka, va, sink_rows, slope_rows, nb)
    if comm is None:
        (ob, ob_exact, lse_b), _ = _fox_fwd(qb, kb, vb, c_row, nb)
    else:
        (ob, ob_exact, lse_b), (wa, wb, wo, w2i, w2o) = _fox_fwd(qb, kb, vb, c_row, nb, comm.gather(GATHER_LATE))
    wa_p = jnp.concatenate(_pad_tiles(wa, 0, A_HEADS, A_SLOT, 0), axis=0)
    h2, mixed = _merge_fwd(h1, oa, ob, ga, gb, wa_p, wb, wo)
    (h3, g2, u2), _ = _ffn_fwd(h2, n2, w2i, w2o)
    loss, dh3, d_nfin = _loss_head(h3.reshape(bsz, l, d), nfin, target)

    (dh2, n2b, a2, dgu2, df2, dn2_parts), _ = _ffn_bwd(dh3.reshape(t, d), h2, n2, g2, u2, w2i, w2o)
    g_w2o = _tn_matmul(a2, df2, "grad_ffn2_w_out")
    g_w2i = _tn_matmul(n2b, dgu2, "grad_ffn2_w_in")

    hosted = comm.swap("ffn2", dict(ffn2_w_in=g_w2i, ffn2_w_out=g_w2o)) if comm else None
    (dya, dyb, doa, dob, dgates, dh2b), swapped = _merge_bwd(dh2, oa, ob, ga, gb, wa_p, wb, wo, hosted)
    g_wo = _tn_matmul(mixed, dh2b, "grad_w_out")
    g_wa = jnp.concatenate(_unpad_tiles(_tn_matmul(oa, dya, "grad_w_branch_a"), 0, A_HEADS, A_SLOT, 0), axis=0)
    g_wb = _tn_matmul(ob, dyb, "grad_w_branch_b")

    dqa, dka, dva, dsink_rows = _swa_bwd(qa, ka, va, doa, lse_a, sink_rows, slope_rows, nb)
    hosted = comm.scatter("ffn2", swapped) if comm else None
    (dqb, dkb, dvb, dc_row), pieces = _fox_bwd(qb, kb, vb, ob_exact, dob, lse_b, c_row, nb, hosted)
    if comm:
        comm.received("ffn2", pieces)
    dc = dc_row.reshape(bsz, B_HEADS, l).transpose(0, 2, 1).reshape(t, B_HEADS)
    dc = jnp.concatenate([dc, jnp.zeros((t, F_COLS - B_HEADS), F32)], axis=1)
    df_logit, db_parts = _forget_cumsum_bwd(dc, f_logit, b_pad, nb)

    dproj = (jnp.concatenate([dqa, dka, dva, df_logit], axis=1), jnp.concatenate([dqb, dkb, dvb], axis=1), dgates)
    g_win = _stack_w_in_grad([_tn_matmul(un, piece, "grad_w_in_" + tag) for piece, tag in zip(dproj, ("a", "b", "gates"))])
    if comm is None:
        g_win = g_win.transpose(1, 0, 2).reshape(d, W_IN_COLS)
    hosted = comm.swap("mixer", dict(w_in=g_win, w_branch_a=g_wa, w_branch_b=g_wb, w_out=g_wo)) if comm else None
    (dh1, dnmix_parts), swapped = _proj_bwd(dh2, h1, nmix, dproj, wp, hosted)
    hosted = comm.scatter("mixer", swapped) if comm else None
    (dh0, n1b, a1, dgu1, df1, dn1_parts), pieces = _ffn_bwd(dh1, h0, n1, g1, u1, w1i, w1o, hosted)
    dh0 = dh0.reshape(bsz, l, d)
    grad_x = dh0[:, PREFIX:]
    small = dict(
        meta_tokens=jnp.sum(dh0[:, N_PAD:PREFIX], axis=0),
        ffn1_norm=jnp.sum(dn1_parts, axis=0),
        mix_norm=jnp.sum(dnmix_parts, axis=0),
        ffn2_norm=jnp.sum(dn2_parts, axis=0),
        final_norm=d_nfin,
        b_forget=jnp.sum(db_parts, axis=0)[:, :B_HEADS],
        attn_sinks=jnp.sum(dsink_rows.reshape(bsz, A_HEADS, BLOCK), axis=(0, 2)).reshape(1, A_HEADS),
    )
    if comm is None:
        g_w1o = _tn_matmul(a1, df1, "grad_ffn1_w_out")
        g_w1i = _tn_matmul(n1b, dgu1, "grad_ffn1_w_in")
    else:
        comm.received("mixer", pieces)
        g_w1o, gathered = _tn_matmul(a1, df1, "grad_ffn1_w_out", comm.small_gather(loss, small))
        comm.small_gathered(gathered)
        swapped = _run_exchange(comm.swap("ffn1_out", dict(ffn1_w_out=g_w1o)), "exchange_halves_ffn1_out")
        g_w1i, pieces = _tn_matmul(n1b, dgu1, "grad_ffn1_w_in", comm.scatter("ffn1_out", swapped))
        comm.received("ffn1_out", pieces)
    big = dict(ffn1_w_in=g_w1i, ffn1_w_out=g_w1o, w_in=g_win, w_branch_a=g_wa, w_branch_b=g_wb,
               w_out=g_wo, ffn2_w_in=g_w2i, ffn2_w_out=g_w2o)
    return loss, grad_x, small, big


BIG = (
    ("ffn1_w_in", (D_MODEL, 5632), 1),
    ("ffn1_w_out", (2816, D_MODEL), 0),
    ("w_in", (D_MODEL, W_IN_COLS), 1),
    ("w_branch_a", (A_WIDTH, D_MODEL), 1),
    ("w_branch_b", (B_WIDTH, D_MODEL), 1),
    ("w_out", (D_MODEL, D_MODEL), 0),
    ("ffn2_w_in", (D_MODEL, 5632), 1),
    ("ffn2_w_out", (2816, D_MODEL), 0),
)
STACKED = "w_in"


def _coords():
    return lax.axis_index("x"), lax.axis_index("y"), lax.axis_index("c")


def _other_chips(x, y):
    return ((1 - x, y), (x, 1 - y), (1 - x, 1 - y))


def _chip_part(ref, name, shape, axis, k):
    if name == STACKED:
        return ref.at[k]
    size = shape[axis] // N_CHIPS
    start = pl.multiple_of(k * size, size)
    return ref.at[pl.ds(start, size), :] if axis == 0 else ref.at[:, pl.ds(start, size)]


def _full_shape(name, shape):
    return (N_CHIPS, shape[0], shape[1] // N_CHIPS) if name == STACKED else shape


class _Exchange:
    def __init__(self, ins, out_shape, n_sems, ops):
        self.ins, self.out_shape, self.n_sems, self.ops = list(ins), list(out_shape), n_sems, ops

    @property
    def scratch(self):
        return [pltpu.SemaphoreType.DMA((self.n_sems,)), pltpu.SemaphoreType.DMA((self.n_sems,))]


SEMS_PER_GATHER = 9


def _gather_exchange(shards, table):
    n = len(table)
    x_nbr, y_nbr, diagonal = 0, 1, 2

    def ops(ins, outs, send_sems, recv_sems):
        x, y, c = _coords()
        mine = 2 * x + y
        sibling = (x, y, 1 - c)
        chips = _other_chips(x, y)
        slots = [2 * chip[0] + chip[1] for chip in chips]

        def part(i, k):
            name, shape, axis = table[i][:3]
            return _chip_part(outs[i], name, shape, axis, k)

        def half(ref, h):
            rows = ref.shape[0] // 2
            return ref.at[pl.ds(pl.multiple_of(h * rows, rows), rows), :]

        def remote(i, sem, src, dst, device):
            sem = SEMS_PER_GATHER * i + sem
            return pltpu.make_async_remote_copy(src, dst, send_sems.at[sem], recv_sems.at[sem],
                                                device_id=device, device_id_type=MESH_ID)

        def own(i):
            return remote(i, 0, ins[i], part(i, mine), sibling)

        def fetch(i, j, slot):
            if table[i][4]:
                src, dst = half(ins[i], c), half(part(i, slot), c)
            else:
                src, dst = ins[i], part(i, slot)
            return remote(i, 1 + j, src, dst, (chips[j][0], chips[j][1], c))

        def relayed(i, via, of):
            region = half(half(part(i, slots[of]), c), via)
            return remote(i, 4 + via, region, region, (chips[via][0], chips[via][1], c))

        def forward(i, j, h):
            region = half(part(i, slots[j]), h)
            return remote(i, 6 + j, region, region, sibling)

        def start():
            for i in range(n):
                for j in (x_nbr, y_nbr) if table[i][4] else (x_nbr, y_nbr, diagonal):
                    fetch(i, j, mine).start()
            for i in range(n):
                own(i).start()

        def relay():
            for i in range(n):
                if not table[i][4]:
                    for j in range(3):
                        fetch(i, j, slots[j]).wait_recv()
                    continue
                fetch(i, y_nbr, slots[y_nbr]).wait_recv()
                relayed(i, x_nbr, y_nbr).start()
                forward(i, y_nbr, c).start()
                fetch(i, x_nbr, slots[x_nbr]).wait_recv()
                relayed(i, y_nbr, x_nbr).start()
                forward(i, x_nbr, c).start()

        def relay_diagonal():
            for i in range(n):
                if table[i][4]:
                    relayed(i, x_nbr, diagonal).wait_recv()
                    relayed(i, y_nbr, diagonal).wait_recv()
                    forward(i, diagonal, c).start()

        def finish():
            for i in range(n):
                own(i).wait()
                for j in range(3):
                    if table[i][4]:
                        forward(i, j, 1 - c).wait_recv()
                        forward(i, j, c).wait_send()
                    if j != diagonal or not table[i][4]:
                        fetch(i, j, mine).wait_send()
                if table[i][4]:
                    relayed(i, x_nbr, y_nbr).wait_send()
                    relayed(i, y_nbr, x_nbr).wait_send()

        return start, relay, relay_diagonal, finish

    out_shape = [jax.ShapeDtypeStruct(_full_shape(name, shape), dtype) for name, shape, _, dtype, _ in table]
    return _Exchange(shards, out_shape, SEMS_PER_GATHER * n, ops)


STREAM_ROWS = 256


def _stream_rows(src, dst, dst_first, buf, sem_in, sem_out):
    bsz, n_rows, _ = src.shape
    chunks = [(b, r) for b in range(bsz) for r in range(0, n_rows, STREAM_ROWS)]

    def load(i):
        b, r = chunks[i]
        return pltpu.make_async_copy(src.at[b, pl.ds(r, STREAM_ROWS), :], buf.at[i % 2], sem_in.at[i % 2])

    def store(i):
        b, r = chunks[i]
        return pltpu.make_async_copy(buf.at[i % 2], dst.at[b, pl.ds(dst_first + r, STREAM_ROWS), :],
                                     sem_out.at[i % 2])

    load(0).start()
    for i in range(len(chunks)):
        if i + 1 < len(chunks):
            if i >= 1:
                store(i - 1).wait()
            load(i + 1).start()
        load(i).wait()
        store(i).start()
    for i in range(max(len(chunks) - 2, 0), len(chunks)):
        store(i).wait()


def _run_exchange(exchange, name):
    n = len(exchange.ins)

    def body(*refs):
        for phase in exchange.ops(refs[:n], refs[n:2 * n], *refs[2 * n:]):
            phase()

    return pl.pallas_call(
        body,
        name=name,
        in_specs=[HBM_SPEC] * n,
        out_specs=[HBM_SPEC] * n,
        out_shape=exchange.out_shape,
        scratch_shapes=exchange.scratch,
    )(*exchange.ins)


CAST_ROWS = 64


def _cast_hosting(arrays, exchange, name, stream=None):
    n_a, n_x = len(arrays), len(exchange.ins)
    k = 1 if stream else 0

    def body(*refs):
        a_in, x_in = refs[:n_a], refs[n_a:n_a + n_x]
        outs = refs[n_a + n_x + k:]
        a_out, x_out = outs[:n_a], outs[n_a:n_a + n_x]
        scratch = outs[n_a + n_x + k:]
        start, *rest = exchange.ops(x_in, x_out, *scratch[:2])
        start()
        for src, dst in zip(a_in, a_out):
            def rows(i, carry, src=src, dst=dst):
                window = pl.ds(pl.multiple_of(i * CAST_ROWS, CAST_ROWS), CAST_ROWS)
                cols = src.shape[1]
                if dst.shape[1] != cols:
                    dst[window, dst.shape[1] - LANES:] = jnp.zeros((CAST_ROWS, LANES), BF16)
                dst[window, :cols] = src[window, :].astype(BF16)
                return carry

            lax.fori_loop(0, src.shape[0] // CAST_ROWS, rows, 0)
        if stream:
            x_ref, h_ref = refs[n_a + n_x], outs[n_a + n_x]
            buf, sem_in, sem_out = scratch[2:]
            _stream_rows(x_ref, h_ref, PREFIX, buf, sem_in, sem_out)
        for phase in rest:
            phase()
        if stream:
            meta = pltpu.make_async_copy(x_out[stream[1]], buf.at[1, pl.ds(0, N_META), :], sem_in.at[0])
            meta.start()
            buf[0, 0:N_PAD, :] = jnp.zeros((N_PAD, buf.shape[2]), F32)
            meta.wait()
            buf[0, N_PAD:PREFIX, :] = buf[1, 0:N_META, :]
            puts = [pltpu.make_async_copy(buf.at[0, pl.ds(0, PREFIX), :], h_ref.at[b, pl.ds(0, PREFIX), :], sem_out.at[b])
                    for b in range(h_ref.shape[0])]
            for put in puts:
                put.start()
            for put in puts:
                put.wait()

    extra_in, extra_out, extra_scratch = [], [], []
    if stream:
        x = stream[0]
        assert x.shape[0] <= 2 and x.shape[1] % STREAM_ROWS == 0 and x.dtype == F32
        extra_in = [x]
        extra_out = [jax.ShapeDtypeStruct((x.shape[0], PREFIX + x.shape[1], x.shape[2]), F32)]
        extra_scratch = [pltpu.VMEM((2, STREAM_ROWS, x.shape[2]), F32),
                         pltpu.SemaphoreType.DMA((2,)), pltpu.SemaphoreType.DMA((2,))]
    outs = pl.pallas_call(
        body,
        name=name,
        in_specs=[VMEM_SPEC] * n_a + [HBM_SPEC] * (n_x + k),
        out_specs=[VMEM_SPEC] * n_a + [HBM_SPEC] * (n_x + k),
        out_shape=[jax.ShapeDtypeStruct((a.shape[0], -(-a.shape[1] // LANES) * LANES), BF16) for a in arrays]
        + exchange.out_shape + extra_out,
        scratch_shapes=exchange.scratch + extra_scratch,
        compiler_params=pltpu.CompilerParams(vmem_limit_bytes=VMEM_LIMIT),
    )(*arrays, *exchange.ins, *extra_in)
    return outs[:n_a], outs[n_a:]


def _host_exchange(exchange, in_refs, out_refs, sem_refs, first, middle, last, late=None):
    start, *relays, finish = exchange.ops(in_refs, out_refs, *sem_refs)
    pl.when(first)(start)
    pl.when(middle)(relays[0])
    if len(relays) > 1:
        pl.when(last if late is None else late)(relays[1])
    pl.when(last)(finish)


def _halves_view(name, shape, axis):
    r, c = shape
    if name == STACKED:
        return (N_CHIPS, 2, r // 2, c // N_CHIPS), lambda ref, h: ref.at[:, h]
    if axis == 1:
        return (2, r // 2, c), lambda ref, h: ref.at[h]
    return (N_CHIPS, 2, r // N_CHIPS // 2, c), lambda ref, h: ref.at[:, h]


def _halves_exchange(grads, entries):
    n_w = len(entries)
    views = [_halves_view(*entry) for entry in entries]

    def ops(ins, outs, send_sems, recv_sems):
        x, y, c = _coords()
        copies = [pltpu.make_async_remote_copy(views[i][1](ins[i], 1 - c), outs[i], send_sems.at[i], recv_sems.at[i],
                                               device_id=(x, y, 1 - c), device_id_type=MESH_ID) for i in range(n_w)]

        def start():
            for cp in copies:
                cp.start()

        def finish():
            for cp in copies:
                cp.wait()

        return start, lambda: None, finish

    half_shape = lambda v: tuple(d for i, d in enumerate(v) if i != (1 if len(v) == 4 else 0))
    out_shape = [jax.ShapeDtypeStruct(half_shape(v[0]), F32) for v in views]
    return _Exchange([g.reshape(v[0]) for g, v in zip(grads, views)], out_shape, n_w, ops)


def _add_sibling(g_view, recv, c, name):
    shape = recv.shape
    if len(shape) == 2:
        tr = _tile(shape[0], 128, 16)
        grid = (shape[0] // tr,)
        g_spec = pl.BlockSpec((None, tr, shape[1]), lambda i, c_ref: (c_ref[0], i, 0))
        r_spec = pl.BlockSpec((tr, shape[1]), lambda i, c_ref: (i, 0))
    else:
        tr = _tile(shape[1], 256, 16)
        grid = (N_CHIPS, shape[1] // tr)
        g_spec = pl.BlockSpec((None, None, tr, shape[2]), lambda k, i, c_ref: (k, c_ref[0], i, 0))
        r_spec = pl.BlockSpec((None, tr, shape[2]), lambda k, i, c_ref: (k, i, 0))

    def body(c_ref, g_ref, r_ref, o_ref):
        o_ref[...] = (g_ref[...] + r_ref[...]).astype(BF16)

    return pl.pallas_call(
        body,
        name="add_sibling_" + name,
        grid_spec=pltpu.PrefetchScalarGridSpec(num_scalar_prefetch=1, grid=grid, in_specs=[g_spec, r_spec],
                                               out_specs=r_spec),
        out_shape=jax.ShapeDtypeStruct(shape, BF16),
        compiler_params=_cparams(("parallel",) * len(grid)),
    )(c, g_view, recv)


def _piece_of(ref, name, axis, k):
    if name == STACKED or axis == 0:
        return ref.at[k]
    size = ref.shape[1] // N_CHIPS
    return ref.at[:, pl.ds(pl.multiple_of(k * size, size), size)]


def _piece_shape(name, shape, axis):
    r, c = shape
    return (r // 2, c // N_CHIPS) if (axis == 1) else (r // N_CHIPS // 2, c)


def _scatter_exchange(partials, entries):
    n_w = len(entries)

    def ops(ins, outs, send_sems, recv_sems):
        x, y, c = _coords()
        chips = _other_chips(x, y)
        copies = []
        for i, (name, _, axis) in enumerate(entries):
            for j, chip in enumerate(chips):
                sem = 3 * i + j
                copies.append(pltpu.make_async_remote_copy(
                    _piece_of(ins[i], name, axis, 2 * chip[0] + chip[1]), outs[i].at[j], send_sems.at[sem],
                    recv_sems.at[sem], device_id=(chip[0], chip[1], c), device_id_type=MESH_ID))

        def start():
            for cp in copies:
                cp.start()

        def finish():
            for cp in copies:
                cp.wait()

        return start, lambda: None, finish

    out_shape = [jax.ShapeDtypeStruct((3,) + _piece_shape(*entry), BF16) for entry in entries]
    return _Exchange(partials, out_shape, 3 * n_w, ops)


def _add_chips(partial, recv, mine, name, axis):
    rows, cols = recv.shape[1:]
    tr = _tile(rows, 256, 16)
    if name == STACKED or axis == 0:
        p_spec = pl.BlockSpec((None, tr, cols), lambda i, k_ref: (k_ref[0], i, 0))
    else:
        p_spec = pl.BlockSpec((tr, cols), lambda i, k_ref: (i, k_ref[0]))

    def body(k_ref, p_ref, r_ref, o_ref):
        f32 = lambda a: a.astype(F32)
        o_ref[...] = ((f32(p_ref[...]) + f32(r_ref[0])) + f32(r_ref[1])) + f32(r_ref[2])

    return pl.pallas_call(
        body,
        name="add_chips_" + name,
        grid_spec=pltpu.PrefetchScalarGridSpec(
            num_scalar_prefetch=1, grid=(rows // tr,),
            in_specs=[p_spec, pl.BlockSpec((3, tr, cols), lambda i, k_ref: (0, i, 0))],
            out_specs=pl.BlockSpec((tr, cols), lambda i, k_ref: (i, 0))),
        out_shape=jax.ShapeDtypeStruct((rows, cols), F32),
        compiler_params=_cparams(("parallel",)),
    )(mine, partial, recv)


def _share_with_sibling(halves):
    n_w = len(halves)

    def body(*refs):
        ins, outs = refs[:n_w], refs[n_w:2 * n_w]
        send_sems, recv_sems = refs[2 * n_w:]
        x, y, c = _coords()
        copies = [pltpu.make_async_remote_copy(ins[i], outs[i], send_sems.at[i], recv_sems.at[i],
                                               device_id=(x, y, 1 - c), device_id_type=MESH_ID) for i in range(n_w)]
        for cp in copies:
            cp.start()
        for cp in copies:
            cp.wait()

    return pl.pallas_call(
        body,
        name="share_with_sibling",
        in_specs=[HBM_SPEC] * n_w,
        out_specs=[HBM_SPEC] * n_w,
        out_shape=[jax.ShapeDtypeStruct(h.shape, F32) for h in halves],
        scratch_shapes=[pltpu.SemaphoreType.DMA((n_w,)), pltpu.SemaphoreType.DMA((n_w,))],
    )(*halves)


SMALL_ROWS = 168


def _small_exchange(buf):
    def ops(ins, outs, send_sems, recv_sems):
        x, y, c = _coords()
        me = 4 * x + 2 * y + c
        peers = [(x ^ fx, y ^ fy, c ^ fc) for fx in (0, 1) for fy in (0, 1) for fc in (0, 1)][1:]

        def copy(j, slot, dev):
            return pltpu.make_async_remote_copy(ins[0], outs[0].at[slot], send_sems.at[j], recv_sems.at[j],
                                                device_id=dev, device_id_type=MESH_ID)

        own = pltpu.make_async_copy(ins[0], outs[0].at[me], send_sems.at[N_DEV - 1])

        def start():
            own.start()
            for j, dev in enumerate(peers):
                copy(j, me, dev).start()

        def finish():
            for j, dev in enumerate(peers):
                copy(j, 4 * dev[0] + 2 * dev[1] + dev[2], dev).wait()
            own.wait()

        return start, lambda: None, finish

    return _Exchange([buf], [jax.ShapeDtypeStruct((N_DEV,) + buf.shape, F32)], N_DEV, ops)


def _sum_devices(gathered):
    def body(g_ref, out_ref):
        acc = g_ref[0]
        for d in range(1, N_DEV):
            acc = acc + g_ref[d]
        out_ref[...] = acc

    return pl.pallas_call(
        body,
        name="sum_devices",
        in_specs=[VMEM_SPEC],
        out_specs=VMEM_SPEC,
        out_shape=jax.ShapeDtypeStruct(gathered.shape[1:], F32),
    )(gathered)


def _adamw(w, g, m, v, copy_g=False):
    r, rest = w.shape[0], w.shape[1:]
    per_row = 1
    for dim in rest:
        per_row *= dim
    tr = _tile(r, max(8, (5 << 19) // (4 * per_row)), 8 if len(rest) == 1 else 1)

    def body(w_ref, g_ref, m_ref, v_ref, *out_refs):
        d_ref, mo_ref, vo_ref = out_refs[-3:]
        gg = g_ref[...]
        if copy_g:
            out_refs[0][...] = gg
        mm = ADAM_B1 * m_ref[...] + (1.0 - ADAM_B1) * gg
        vv = ADAM_B2 * v_ref[...] + (1.0 - ADAM_B2) * (gg * gg)
        m_hat = mm / (1.0 - ADAM_B1 ** ADAM_STEP)
        v_hat = vv / (1.0 - ADAM_B2 ** ADAM_STEP)
        d_ref[...] = -ADAM_LR * (m_hat / (jnp.sqrt(v_hat) + ADAM_EPS) + ADAM_WD * w_ref[...])
        mo_ref[...] = mm
        vo_ref[...] = vv

    n_out = 4 if copy_g else 3
    spec = pl.BlockSpec((tr,) + rest, lambda i: (i,) + (0,) * len(rest))
    return pl.pallas_call(
        body,
        name="adamw",
        grid=(r // tr,),
        in_specs=[spec] * 4,
        out_specs=[spec] * n_out,
        out_shape=[jax.ShapeDtypeStruct(w.shape, F32)] * n_out,
        compiler_params=_cparams(("parallel",)),
    )(w, g, m, v)


def _adamw_halves(w, own, other, m, v, c, name):
    r, cols = w.shape
    half = r // 2
    tr = _tile(half, 256, 8)
    nt = half // tr
    whole = pl.BlockSpec((tr, cols), lambda h, i, c_ref: (h * nt + i, 0))
    part = pl.BlockSpec((tr, cols), lambda h, i, c_ref: (i, 0))

    def body(c_ref, w_ref, own_ref, other_ref, m_ref, v_ref, g_ref, d_ref, mo_ref, vo_ref):
        gg = jnp.where(pl.program_id(0) == c_ref[0], own_ref[...], other_ref[...])
        g_ref[...] = gg
        mm = ADAM_B1 * m_ref[...] + (1.0 - ADAM_B1) * gg
        vv = ADAM_B2 * v_ref[...] + (1.0 - ADAM_B2) * (gg * gg)
        m_hat = mm / (1.0 - ADAM_B1 ** ADAM_STEP)
        v_hat = vv / (1.0 - ADAM_B2 ** ADAM_STEP)
        d_ref[...] = -ADAM_LR * (m_hat / (jnp.sqrt(v_hat) + ADAM_EPS) + ADAM_WD * w_ref[...])
        mo_ref[...] = mm
        vo_ref[...] = vv

    return pl.pallas_call(
        body,
        name="adamw_" + name,
        grid_spec=pltpu.PrefetchScalarGridSpec(
            num_scalar_prefetch=1, grid=(2, nt),
            in_specs=[whole, part, part, whole, whole], out_specs=[whole] * 4),
        out_shape=[jax.ShapeDtypeStruct((r, cols), F32)] * 4,
        compiler_params=_cparams(("parallel", "parallel")),
    )(c, w, own, other, m, v)


GATHER_FIRST = ("ffn1_w_in", "ffn1_w_out")
GATHER_PROJ = ("w_in",)
GATHER_LATE = ("w_branch_a", "w_branch_b", "w_out", "ffn2_w_in", "ffn2_w_out")


class _Comm:
    def __init__(self, shards, c_arr, mine_arr):
        self.shards, self.c, self.mine = shards, c_arr, mine_arr
        self.groups, self.halves = {}, {}
        self.by_name = {entry[0]: entry for entry in BIG}

    def small_gather(self, loss, small):
        pad_lanes = lambda a: jnp.concatenate([a, jnp.zeros((1, LANES - a.shape[1]), F32)], axis=1)
        buf = jnp.concatenate([
            small["meta_tokens"].reshape(128, LANES),
            small["ffn1_norm"].reshape(8, LANES), small["mix_norm"].reshape(8, LANES),
            small["ffn2_norm"].reshape(8, LANES), small["final_norm"].reshape(8, LANES),
            loss, pad_lanes(small["b_forget"]), pad_lanes(small["attn_sinks"]),
            jnp.zeros((SMALL_ROWS - 163, LANES), F32)], axis=0)
        return _small_exchange(buf)

    def small_gathered(self, outs):
        self.reduced = _sum_devices(outs[0])

    def gather(self, names):
        padded = (STACKED, (D_MODEL, N_CHIPS * SHARD_PAD_COLS), 1)
        table = [(padded if n == STACKED else self.by_name[n]) + (BF16, True) for n in names]
        return _gather_exchange([self.shards[n] for n in names], table)

    def swap(self, tag, grads):
        entries = [self.by_name[n] for n in grads]
        arrays = list(grads.values())
        self.groups[tag] = (entries, arrays)
        return _halves_exchange(arrays, entries)

    def scatter(self, tag, received):
        entries, arrays = self.groups[tag]
        views = [_halves_view(*entry) for entry in entries]
        partials = [_add_sibling(g.reshape(v[0]), r, self.c, name)
                    for g, v, r, (name, _, _) in zip(arrays, views, received, entries)]
        self.groups[tag] = (entries, partials)
        return _scatter_exchange(partials, entries)

    def received(self, tag, pieces):
        entries, partials = self.groups[tag]
        for p, r, (name, _, axis) in zip(partials, pieces, entries):
            self.halves[name] = _add_chips(p, r, self.mine, name, axis)

    def finish(self):
        names = [n for n, _, _ in BIG]
        own = [self.halves[n] for n in names]
        return dict(zip(names, zip(own, _share_with_sibling(own))))


def kernel(x, meta_tokens, ffn1_norm, ffn1_w_in, ffn1_w_out, mix_norm, w_in, b_forget, attn_sinks, w_branch_a, w_branch_b, w_out, ffn2_norm, ffn2_w_in, ffn2_w_out, final_norm, loss_target, m_meta_tokens, m_ffn1_norm, m_ffn1_w_in, m_ffn1_w_out, m_mix_norm, m_w_in, m_b_forget, m_attn_sinks, m_w_branch_a, m_w_branch_b, m_w_out, m_ffn2_norm, m_ffn2_w_in, m_ffn2_w_out, m_final_norm, v_meta_tokens, v_ffn1_norm, v_ffn1_w_in, v_ffn1_w_out, v_mix_norm, v_w_in, v_b_forget, v_attn_sinks, v_w_branch_a, v_w_branch_b, v_w_out, v_ffn2_norm, v_ffn2_w_in, v_ffn2_w_out, v_final_norm):
    given = dict(locals())
    names = ["meta_tokens", "ffn1_norm", "ffn1_w_in", "ffn1_w_out", "mix_norm", "w_in", "b_forget", "attn_sinks",
             "w_branch_a", "w_branch_b", "w_out", "ffn2_norm", "ffn2_w_in", "ffn2_w_out", "final_norm"]
    big_names = [n for n, _, _ in BIG]
    cx, cy, cc = _coords()
    c_arr = cc.reshape(1).astype(jnp.int32)
    mine_arr = (2 * cx + cy).reshape(1).astype(jnp.int32)

    by_name = {entry[0]: entry for entry in BIG}
    shards = {n: given[n][0].astype(BF16) for n in GATHER_FIRST}
    table = [by_name[n] + (BF16, True) for n in GATHER_FIRST] + [("meta_tokens", (N_META, D_MODEL), 1, F32, False)]
    first = _gather_exchange([shards[n] for n in GATHER_FIRST] + [meta_tokens], table)
    late_names = [n for n in big_names if n not in GATHER_FIRST]
    casts, (w1i, w1o, _, stream) = _cast_hosting([given[n][0] for n in late_names], first, "gather_first",
                                                 stream=(x, len(GATHER_FIRST)))
    shards.update(zip(late_names, casts))
    comm = _Comm(shards, c_arr, mine_arr)
    norms = (ffn1_norm, mix_norm, ffn2_norm, final_norm.reshape(1, D_MODEL))
    loss, grad_x, small, big = _local_step(stream, loss_target, None, norms, b_forget, attn_sinks, (w1i, w1o), comm)

    swap = comm.swap("ffn1_in", dict(ffn1_w_in=big["ffn1_w_in"]))
    last = comm.scatter("ffn1_in", _run_exchange(swap, "exchange_halves_ffn1_in"))
    comm.received("ffn1_in", _run_exchange(last, "scatter_chip_sums"))
    grad_halves = comm.finish()
    grads = {}

    red = comm.reduced
    meta_cols = red[:128].reshape(N_META, D_MODEL)
    grads["meta_tokens"] = lax.dynamic_slice_in_dim(meta_cols, (2 * cx + cy) * (D_MODEL // N_CHIPS),
                                                    D_MODEL // N_CHIPS, axis=1)
    grads["ffn1_norm"] = red[128:136].reshape(1, D_MODEL)
    grads["mix_norm"] = red[136:144].reshape(1, D_MODEL)
    grads["ffn2_norm"] = red[144:152].reshape(1, D_MODEL)
    grads["final_norm"] = red[152:160].reshape(1, D_MODEL)
    loss_out = red[160, 0]
    grads["b_forget"] = red[161:162, :B_HEADS]
    grads["attn_sinks"] = red[162:163, :A_HEADS]

    out_g, out_d, out_m, out_v = [], [], [], []
    for n in names:
        w_full = given[n]
        shape = w_full.shape
        two_d = (lambda a: a.reshape(shape[-2], shape[-1])) if len(shape) >= 2 else (lambda a: a.reshape(1, shape[0]))
        if n == STACKED:
            own, other = grad_halves[n]
            g_nat = jnp.concatenate([jnp.where(cc == 0, own, other), jnp.where(cc == 0, other, own)], axis=0)
            rows = lambda a: a.reshape(1, shape[-2], shape[-1]).transpose(2, 0, 1)
            unrows = lambda a: a.transpose(1, 2, 0)
            g2, d2, m2, v2 = [unrows(a) for a in _adamw(rows(w_full), rows(g_nat), rows(given["m_" + n]),
                                                         rows(given["v_" + n]), copy_g=True)]
        elif n in grad_halves:
            own, other = grad_halves[n]
            g2, d2, m2, v2 = _adamw_halves(two_d(w_full), own, other, two_d(given["m_" + n]),
                                           two_d(given["v_" + n]), c_arr, n)
        else:
            g2 = two_d(grads[n])
            d2, m2, v2 = _adamw(two_d(w_full), g2, two_d(given["m_" + n]), two_d(given["v_" + n]))
        out_g.append(g2.reshape(shape))
        out_d.append(d2.reshape(shape))
        out_m.append(m2.reshape(shape))
        out_v.append(v2.reshape(shape))
    return (loss_out, grad_x, *out_g, *out_d, *out_m, *out_v)
```

```python
import jax
import jax.numpy as jnp
from jax import lax
from jax.experimental import pallas as pl
from jax.experimental.pallas import tpu as pltpu

F32 = jnp.float32
BF16 = jnp.bfloat16

D_MODEL = 1024
N_META = 16
BLOCK = 128
LANES = 128
PREFIX = BLOCK
N_PAD = PREFIX - N_META
HEAD_DIM = 64
A_HEADS = 8
A_KV_HEADS = 2
A_GROUP = 4
B_HEADS = 8
B_PAIRS = B_HEADS // 2
A_WIDTH = A_HEADS * HEAD_DIM
A_KV_WIDTH = A_KV_HEADS * HEAD_DIM
B_WIDTH = B_HEADS * HEAD_DIM
W_IN_COLS = A_WIDTH + 2 * A_KV_WIDTH + 3 * B_WIDTH + B_HEADS + 2 * D_MODEL
SRC_KA = A_WIDTH
SRC_VA = SRC_KA + A_KV_WIDTH
SRC_QB = SRC_VA + A_KV_WIDTH
SRC_KB = SRC_QB + B_WIDTH
SRC_VB = SRC_KB + B_WIDTH
SRC_F = SRC_VB + B_WIDTH
SRC_GA = SRC_F + B_HEADS
SRC_GB = SRC_GA + D_MODEL
A_PAD_WIDTH = A_HEADS * LANES
B_PAD_WIDTH = B_HEADS * LANES
F_COLS = LANES
OFF_QA = 0
OFF_KA = SRC_KA
OFF_VA = SRC_VA
OFF_F = OFF_VA + A_KV_WIDTH
OFF_QB = OFF_F + F_COLS
OFF_KB = OFF_QB + B_WIDTH
OFF_VB = OFF_KB + B_WIDTH
OFF_GA = OFF_VB + B_WIDTH
OFF_GB = OFF_GA + D_MODEL
P_COLS = OFF_GB + D_MODEL
P_PIECES = ((0, OFF_QB), (OFF_QB, OFF_GA - OFF_QB), (OFF_GA, P_COLS - OFF_GA))
EPS = 1e-6
NEG = -1e30
SCALE = HEAD_DIM ** -0.5
KEY_BLOCKS = 4

ADAM_LR = 0.001
ADAM_B1 = 0.9
ADAM_B2 = 0.999
ADAM_EPS = 1e-08
ADAM_WD = 0.01
ADAM_STEP = 10

N_CHIPS = 4
N_DEV = 8
VMEM_LIMIT = 56 * 1024 * 1024

NT_DIMS = (((1,), (1,)), ((), ()))
TN_DIMS = (((0,), (0,)), ((), ()))
MESH_ID = pl.DeviceIdType.MESH
HBM_SPEC = pl.BlockSpec(memory_space=pltpu.HBM)
VMEM_SPEC = pl.BlockSpec(memory_space=pltpu.VMEM)


def _tile(n, target, mult=16):
    best = None
    for t in range(mult, min(n, target) + 1, mult):
        if n % t == 0:
            best = t
    return best if best is not None else n


def _cparams(sem):
    return pltpu.CompilerParams(dimension_semantics=sem, vmem_limit_bytes=VMEM_LIMIT)


def _rms_scale(h):
    return lax.rsqrt(jnp.mean(h * h, axis=-1, keepdims=True) + EPS)


def _rms_bwd(dn, h, w):
    r = _rms_scale(h)
    dw = jnp.sum(dn * (h * r), axis=0, keepdims=True)
    z = dn * w
    dh = r * z - h * ((r * r * r) * jnp.mean(z * h, axis=-1, keepdims=True))
    return dh, dw


def _ffn_fwd(h, norm_w, w_in, w_out, exchange=None):
    t, d = h.shape
    f = w_out.shape[0]
    tm = _tile(t, 272)
    tc = _tile(f, 256, 128)
    nj = f // tc
    ni = t // tm
    n_x = len(exchange.ins) if exchange else 0

    def body(*refs):
        h_ref, nw_ref, wi_ref, wo_ref = refs[:4]
        hout_ref, g_ref, u_ref = refs[4 + n_x:7 + n_x]
        a_scr = refs[7 + 2 * n_x]
        i = pl.program_id(0)
        if exchange:
            _host_exchange(exchange, refs[4:4 + n_x], refs[7 + n_x:7 + 2 * n_x], refs[8 + 2 * n_x:],
                           i == 0, i == ni // 3, i == ni - 1, late=i == 2 * ni // 3)
        hh = h_ref[...]
        n = ((hh * _rms_scale(hh)) * nw_ref[...]).astype(BF16)
        for j in range(nj):
            cols = slice(j * tc, (j + 1) * tc)
            g = jnp.dot(n, wi_ref[:, j * tc:(j + 1) * tc], preferred_element_type=F32)
            u = jnp.dot(n, wi_ref[:, f + j * tc:f + (j + 1) * tc], preferred_element_type=F32)
            g_ref[:, cols] = g
            u_ref[:, cols] = u
            a_scr[:, cols] = ((g * jax.nn.sigmoid(g)) * u).astype(BF16)
        hout_ref[...] = hh + 0.5 * jnp.dot(a_scr[...], wo_ref[...], preferred_element_type=F32)

    resident = lambda a: pl.BlockSpec(a.shape, lambda i: (0, 0), pipeline_mode=pl.Buffered(1))
    row = lambda w: pl.BlockSpec((tm, w), lambda i: (i, 0))
    outs = pl.pallas_call(
        body,
        name="ffn_fwd",
        grid=(ni,),
        in_specs=[row(d), pl.BlockSpec((1, d), lambda i: (0, 0)), resident(w_in), resident(w_out)] + [HBM_SPEC] * n_x,
        out_specs=[row(d), row(f), row(f)] + [HBM_SPEC] * n_x,
        out_shape=[
            jax.ShapeDtypeStruct((t, d), F32),
            jax.ShapeDtypeStruct((t, f), F32),
            jax.ShapeDtypeStruct((t, f), F32),
        ] + (exchange.out_shape if exchange else []),
        scratch_shapes=[pltpu.VMEM((tm, f), BF16)] + (exchange.scratch if exchange else []),
        compiler_params=_cparams(("arbitrary",) if exchange else ("parallel",)),
    )(h, norm_w, w_in, w_out, *(exchange.ins if exchange else []))
    return outs[:3], outs[3:]


def _ffn_bwd(dh_out, h, norm_w, g, u, w_in, w_out, exchange=None):
    t, d = h.shape
    f = w_out.shape[0]
    tm = _tile(t, 272)
    tc = _tile(f, 256, 128)
    nj = f // tc
    ni = t // tm
    n_x = len(exchange.ins) if exchange else 0

    def body(*refs):
        dho_ref, h_ref, nw_ref, g_ref, u_ref, wi_ref, wo_ref = refs[:7]
        dhin_ref, n_ref, a_ref, dgu_ref, df_ref, dnw_ref = refs[7 + n_x:13 + n_x]
        i = pl.program_id(0)
        if exchange:
            _host_exchange(exchange, refs[7:7 + n_x], refs[13 + n_x:13 + 2 * n_x], refs[13 + 2 * n_x:],
                           i == 0, i == ni - 1, i == ni - 1)
        hh = h_ref[...]
        nw = nw_ref[...]
        n_ref[...] = ((hh * _rms_scale(hh)) * nw).astype(BF16)
        dho = dho_ref[...]
        df = (0.5 * dho).astype(BF16)
        df_ref[...] = df
        for j in range(nj):
            cols = slice(j * tc, (j + 1) * tc)
            da = lax.dot_general(df, wo_ref[cols, :], NT_DIMS, preferred_element_type=F32)
            gg = g_ref[:, cols]
            uu = u_ref[:, cols]
            sig = jax.nn.sigmoid(gg)
            sl = gg * sig
            a_ref[:, cols] = (sl * uu).astype(BF16)
            dgu_ref[0, :, cols] = ((da * uu) * (sig * (1.0 + gg * (1.0 - sig)))).astype(BF16)
            dgu_ref[1, :, cols] = (da * sl).astype(BF16)
        dn = (lax.dot_general(dgu_ref[0], wi_ref[:, :f], NT_DIMS, preferred_element_type=F32)
              + lax.dot_general(dgu_ref[1], wi_ref[:, f:], NT_DIMS, preferred_element_type=F32))
        dh, dw = _rms_bwd(dn, hh, nw)
        dhin_ref[...] = dho + dh
        dnw_ref[0] = dw

    resident = lambda a: pl.BlockSpec(a.shape, lambda i: (0, 0), pipeline_mode=pl.Buffered(1))
    row = lambda w: pl.BlockSpec((tm, w), lambda i: (i, 0))
    outs = pl.pallas_call(
        body,
        name="ffn_bwd",
        grid=(ni,),
        in_specs=[row(d), row(d), pl.BlockSpec((1, d), lambda i: (0, 0)), row(f), row(f),
                  resident(w_in), resident(w_out)] + [HBM_SPEC] * n_x,
        out_specs=[row(d), row(d), row(f), pl.BlockSpec((2, tm, f), lambda i: (0, i, 0)), row(d),
                   pl.BlockSpec((1, 1, d), lambda i: (i, 0, 0))] + [HBM_SPEC] * n_x,
        out_shape=[
            jax.ShapeDtypeStruct((t, d), F32),
            jax.ShapeDtypeStruct((t, d), BF16),
            jax.ShapeDtypeStruct((t, f), BF16),
            jax.ShapeDtypeStruct((2, t, f), BF16),
            jax.ShapeDtypeStruct((t, d), BF16),
            jax.ShapeDtypeStruct((ni, 1, d), F32),
        ] + (exchange.out_shape if exchange else []),
        scratch_shapes=exchange.scratch if exchange else [],
        compiler_params=_cparams(("arbitrary",) if exchange else ("parallel",)),
    )(dh_out, h, norm_w, g, u, w_in, w_out, *(exchange.ins if exchange else []))
    return outs[:6], outs[6:]


def _tn_matmul(a, b, name, exchange=None):
    t, k = a.shape
    split = b.ndim == 3
    n = 2 * b.shape[2] if split else b.shape[1]
    tk = _tile(k, 512, 128)
    tn = _tile(b.shape[-1], 1408, 128)
    per_half = b.shape[-1] // tn
    ni, nj = k // tk, n // tn
    n_x = len(exchange.ins) if exchange else 0

    def body(*refs):
        a_ref, b_ref, o_ref = refs[0], refs[1], refs[2 + n_x]
        if exchange:
            i, j = pl.program_id(0), pl.program_id(1)
            at_end = (i == ni - 1) & (j == nj - 1)
            _host_exchange(exchange, refs[2:2 + n_x], refs[3 + n_x:3 + 2 * n_x], refs[3 + 2 * n_x:],
                           (i == 0) & (j == 0), at_end, at_end)
        o_ref[...] = lax.dot_general(a_ref[...], b_ref[...], TN_DIMS, preferred_element_type=F32)

    if split:
        b_spec = pl.BlockSpec((None, t, tn), lambda i, j: (j // per_half, 0, j % per_half))
    else:
        b_spec = pl.BlockSpec((t, tn), lambda i, j: (0, j))
    outs = pl.pallas_call(
        body,
        name=name,
        grid=(ni, nj),
        in_specs=[pl.BlockSpec((t, tk), lambda i, j: (0, i)), b_spec] + [HBM_SPEC] * n_x,
        out_specs=[pl.BlockSpec((tk, tn), lambda i, j: (i, j))] + [HBM_SPEC] * n_x,
        out_shape=[jax.ShapeDtypeStruct((k, n), F32)] + (exchange.out_shape if exchange else []),
        scratch_shapes=exchange.scratch if exchange else [],
        compiler_params=_cparams(("arbitrary", "arbitrary") if exchange else ("parallel", "parallel")),
    )(a, b, *(exchange.ins if exchange else []))
    return (outs[0], outs[1:]) if exchange else outs[0]


A_SLOT = lambda h: h // A_GROUP
B_SLOT = lambda h: h % 2

PROJ_PARTS = (
    (OFF_QA, A_WIDTH, A_PAD_WIDTH, True, A_SLOT), (OFF_KA, A_KV_WIDTH, A_KV_WIDTH, True, None),
    (OFF_VA, A_KV_WIDTH, A_KV_WIDTH, True, None), (OFF_QB, B_WIDTH, B_WIDTH, True, None),
    (OFF_KB, B_WIDTH, B_PAD_WIDTH, True, B_SLOT), (OFF_VB, B_WIDTH, B_PAD_WIDTH, True, B_SLOT),
    (OFF_GA, D_MODEL, D_MODEL, False, None), (OFF_GB, D_MODEL, D_MODEL, False, None), (OFF_F, F_COLS, F_COLS, False, None),
)


def _head_tile(pair, head, slot):
    lane_slot = lax.broadcasted_iota(jnp.int32, pair.shape, 1) // HEAD_DIM
    moved = pair if head % 2 == slot else pltpu.roll(pair, HEAD_DIM, 1)
    return jnp.where(lane_slot == slot, moved, 0.0)


def _proj_fwd(h, norm_w, w_p):
    t, d = h.shape
    tm = _tile(t, 272)

    def body(h_ref, nw_ref, w_ref, u_ref, *part_refs):
        hh = h_ref[...]
        un = ((hh * _rms_scale(hh)) * nw_ref[...]).astype(BF16)
        u_ref[...] = un
        for (off, width, _, _, slot), p_ref in zip(PROJ_PARTS, part_refs):
            if slot is None:
                p_ref[...] = jnp.dot(un, w_ref[:, off:off + width], preferred_element_type=F32).astype(p_ref.dtype)
                continue
            part = jnp.dot(un, w_ref[:, off:off + width], preferred_element_type=F32)
            for pair in range(width // LANES):
                x = part[:, pair * LANES:(pair + 1) * LANES]
                for head in (2 * pair, 2 * pair + 1):
                    p_ref[:, head * LANES:(head + 1) * LANES] = _head_tile(x, head, slot(head)).astype(p_ref.dtype)

    row = lambda w: pl.BlockSpec((tm, w), lambda i: (i, 0))
    return pl.pallas_call(
        body,
        name="proj_fwd",
        grid=(t // tm,),
        in_specs=[row(d), pl.BlockSpec((1, d), lambda i: (0, 0)),
                  pl.BlockSpec(w_p.shape, lambda i: (0, 0), pipeline_mode=pl.Buffered(1))],
        out_specs=[row(d)] + [row(width) for _, _, width, _, _ in PROJ_PARTS],
        out_shape=[jax.ShapeDtypeStruct((t, d), BF16)]
        + [jax.ShapeDtypeStruct((t, width), BF16 if is_bf else F32) for _, _, width, is_bf, _ in PROJ_PARTS],
        compiler_params=_cparams(("parallel",)),
    )(h, norm_w, w_p)


def _proj_bwd(dh_out, h, norm_w, dproj, w_p, exchange=None):
    t, d = h.shape
    tm = _tile(t, 272)
    ni = t // tm
    n_p = len(P_PIECES)
    n_in = 4 + n_p
    n_x = len(exchange.ins) if exchange else 0

    def body(*refs):
        dho_ref, h_ref, nw_ref = refs[:3]
        dp_refs, w_ref = refs[3:3 + n_p], refs[3 + n_p]
        dhin_ref, dnw_ref = refs[n_in + n_x:n_in + 2 + n_x]
        if exchange:
            i = pl.program_id(0)
            _host_exchange(exchange, refs[n_in:n_in + n_x], refs[n_in + 2 + n_x:n_in + 2 + 2 * n_x],
                           refs[n_in + 2 + 2 * n_x:], i == 0, i == ni - 1, i == ni - 1)
        dn = None
        for dp_ref, (off, width) in zip(dp_refs, P_PIECES):
            part = lax.dot_general(dp_ref[...], w_ref[:, off:off + width], NT_DIMS, preferred_element_type=F32)
            dn = part if dn is None else dn + part
        dh, dw = _rms_bwd(dn, h_ref[...], nw_ref[...])
        dhin_ref[...] = dho_ref[...] + dh
        dnw_ref[0] = dw

    row = lambda w: pl.BlockSpec((tm, w), lambda i: (i, 0))
    outs = pl.pallas_call(
        body,
        name="proj_bwd",
        grid=(ni,),
        in_specs=[row(d), row(d), pl.BlockSpec((1, d), lambda i: (0, 0))] + [row(width) for _, width in P_PIECES]
        + [pl.BlockSpec(w_p.shape, lambda i: (0, 0), pipeline_mode=pl.Buffered(1))] + [HBM_SPEC] * n_x,
        out_specs=[row(d), pl.BlockSpec((1, 1, d), lambda i: (i, 0, 0))] + [HBM_SPEC] * n_x,
        out_shape=[jax.ShapeDtypeStruct((t, d), F32), jax.ShapeDtypeStruct((ni, 1, d), F32)]
        + (exchange.out_shape if exchange else []),
        scratch_shapes=exchange.scratch if exchange else [],
        compiler_params=_cparams(("arbitrary",) if exchange else ("parallel",)),
    )(dh_out, h, norm_w, *dproj, w_p, *(exchange.ins if exchange else []))
    return outs[:2], outs[2:]


def _merge_fwd(h, oa, ob, ga, gb, wa, wb, wo):
    t, d = h.shape
    tm = _tile(t, 544)

    def body(h_ref, oa_ref, ob_ref, ga_ref, gb_ref, wa_ref, wb_ref, wo_ref, hout_ref, mix_ref):
        ya = jnp.dot(oa_ref[...], wa_ref[...], preferred_element_type=F32)
        yb = jnp.dot(ob_ref[...], wb_ref[...], preferred_element_type=F32)
        mixed = (jax.nn.sigmoid(ga_ref[...]) * ya + jax.nn.sigmoid(gb_ref[...]) * yb).astype(BF16)
        mix_ref[...] = mixed
        hout_ref[...] = h_ref[...] + jnp.dot(mixed, wo_ref[...], preferred_element_type=F32)

    row = lambda w: pl.BlockSpec((tm, w), lambda i: (i, 0))
    full = lambda a: pl.BlockSpec(a.shape, lambda i: (0, 0))
    return pl.pallas_call(
        body,
        name="merge_fwd",
        grid=(t // tm,),
        in_specs=[row(d), row(oa.shape[1]), row(ob.shape[1]), row(d), row(d), full(wa), full(wb), full(wo)],
        out_specs=[row(d), row(d)],
        out_shape=[jax.ShapeDtypeStruct((t, d), F32), jax.ShapeDtypeStruct((t, d), BF16)],
        compiler_params=_cparams(("parallel",)),
    )(h, oa, ob, ga, gb, wa, wb, wo)


def _merge_bwd(dh, oa, ob, ga, gb, wa, wb, wo, exchange=None):
    t, d = dh.shape
    tm = _tile(t, 544)
    ni = t // tm
    n_x = len(exchange.ins) if exchange else 0

    def body(*refs):
        dh_ref, oa_ref, ob_ref, ga_ref, gb_ref, wa_ref, wb_ref, wo_ref = refs[:8]
        dya_ref, dyb_ref, doa_ref, dob_ref, dg_ref, dhb_ref = refs[8 + n_x:14 + n_x]
        if exchange:
            i = pl.program_id(0)
            _host_exchange(exchange, refs[8:8 + n_x], refs[14 + n_x:14 + 2 * n_x], refs[14 + 2 * n_x:],
                           i == 0, i == ni - 1, i == ni - 1)
        dhb = dh_ref[...].astype(BF16)
        dhb_ref[...] = dhb
        dmix = lax.dot_general(dhb, wo_ref[...], NT_DIMS, preferred_element_type=F32)
        for branch, (o_ref, g_ref, w_ref, dy_ref, do_ref) in enumerate((
                (oa_ref, ga_ref, wa_ref, dya_ref, doa_ref),
                (ob_ref, gb_ref, wb_ref, dyb_ref, dob_ref))):
            y = jnp.dot(o_ref[...], w_ref[...], preferred_element_type=F32)
            s = jax.nn.sigmoid(g_ref[...])
            dy = (dmix * s).astype(BF16)
            dy_ref[...] = dy
            dg_ref[:, branch * d:(branch + 1) * d] = ((dmix * y) * (s * (1.0 - s))).astype(BF16)
            do_ref[...] = lax.dot_general(dy, w_ref[...], NT_DIMS, preferred_element_type=F32).astype(BF16)

    row = lambda w: pl.BlockSpec((tm, w), lambda i: (i, 0))
    full = lambda a: pl.BlockSpec(a.shape, lambda i: (0, 0))
    wa_w, wb_w = oa.shape[1], ob.shape[1]
    outs = pl.pallas_call(
        body,
        name="merge_bwd",
        grid=(ni,),
        in_specs=[row(d), row(wa_w), row(wb_w), row(d), row(d), full(wa), full(wb), full(wo)] + [HBM_SPEC] * n_x,
        out_specs=[row(d), row(d), row(wa_w), row(wb_w), row(2 * d), row(d)] + [HBM_SPEC] * n_x,
        out_shape=[
            jax.ShapeDtypeStruct((t, d), BF16), jax.ShapeDtypeStruct((t, d), BF16),
            jax.ShapeDtypeStruct((t, wa_w), BF16), jax.ShapeDtypeStruct((t, wb_w), BF16),
            jax.ShapeDtypeStruct((t, 2 * d), BF16), jax.ShapeDtypeStruct((t, d), BF16),
        ] + (exchange.out_shape if exchange else []),
        scratch_shapes=exchange.scratch if exchange else [],
        compiler_params=_cparams(("arbitrary",) if exchange else ("parallel",)),
    )(dh, oa, ob, ga, gb, wa, wb, wo, *(exchange.ins if exchange else []))
    return outs[:6], outs[6:]


def _tri_dot(tri, x):
    hi = x.astype(BF16)
    r1 = x - hi.astype(F32)
    mid = r1.astype(BF16)
    lo = (r1 - mid.astype(F32)).astype(BF16)
    return (jnp.dot(tri, hi, preferred_element_type=F32)
            + jnp.dot(tri, mid, preferred_element_type=F32)
            + jnp.dot(tri, lo, preferred_element_type=F32))


def _forget_cumsum(f_logit, b_pad, nb):
    t, w = f_logit.shape
    bsz = t // (nb * BLOCK)

    def body(f_ref, b_ref, c_ref, carry):
        @pl.when(pl.program_id(0) == 0)
        def _():
            carry[...] = jnp.zeros_like(carry)

        rows = lax.broadcasted_iota(jnp.int32, (BLOCK, BLOCK), 0)
        cols = lax.broadcasted_iota(jnp.int32, (BLOCK, BLOCK), 1)
        tri = (cols <= rows).astype(BF16)
        for b in range(bsz):
            x = jax.nn.log_sigmoid(f_ref[b] + b_ref[...])
            c = _tri_dot(tri, x) + carry[b]
            c_ref[b] = c
            carry[b] = c[BLOCK - 1:BLOCK, :]

    block = pl.BlockSpec((bsz, BLOCK, w), lambda n: (0, n, 0))
    return pl.pallas_call(
        body,
        name="forget_cumsum",
        grid=(nb,),
        in_specs=[block, pl.BlockSpec((1, w), lambda n: (0, 0))],
        out_specs=block,
        out_shape=jax.ShapeDtypeStruct((bsz, nb * BLOCK, w), F32),
        scratch_shapes=[pltpu.VMEM((bsz, 1, w), F32)],
        compiler_params=_cparams(("arbitrary",)),
    )(f_logit.reshape(bsz, nb * BLOCK, w), b_pad).reshape(t, w)


def _forget_cumsum_bwd(dc, f_logit, b_pad, nb):
    t, w = f_logit.shape
    bsz = t // (nb * BLOCK)

    def body(dc_ref, f_ref, b_ref, df_ref, db_ref, carry):
        @pl.when(pl.program_id(0) == 0)
        def _():
            carry[...] = jnp.zeros_like(carry)
            db_ref[...] = jnp.zeros_like(db_ref)

        rows = lax.broadcasted_iota(jnp.int32, (BLOCK, BLOCK), 0)
        cols = lax.broadcasted_iota(jnp.int32, (BLOCK, BLOCK), 1)
        tri = (cols >= rows).astype(BF16)
        for b in range(bsz):
            dlf = _tri_dot(tri, dc_ref[b]) + carry[b]
            carry[b] = dlf[0:1, :]
            df = dlf * jax.nn.sigmoid(-(f_ref[b] + b_ref[...]))
            df_ref[b] = df.astype(BF16)
            db_ref[b] += jnp.sum(df, axis=0, keepdims=True)

    l = nb * BLOCK
    rev = pl.BlockSpec((bsz, BLOCK, w), lambda n: (0, nb - 1 - n, 0))
    df, db = pl.pallas_call(
        body,
        name="forget_cumsum_bwd",
        grid=(nb,),
        in_specs=[rev, rev, pl.BlockSpec((1, w), lambda n: (0, 0))],
        out_specs=[rev, pl.BlockSpec((bsz, 1, w), lambda n: (0, 0, 0))],
        out_shape=[jax.ShapeDtypeStruct((bsz, l, w), BF16), jax.ShapeDtypeStruct((bsz, 1, w), F32)],
        scratch_shapes=[pltpu.VMEM((bsz, 1, w), F32)],
        compiler_params=_cparams(("arbitrary",)),
    )(dc.reshape(bsz, l, w), f_logit.reshape(bsz, l, w), b_pad)
    return df.reshape(t, w), db


GROUP_ROWS = A_GROUP * BLOCK


def _stack_heads(ref, g):
    return jnp.concatenate([ref[:, (A_GROUP * g + i) * LANES:(A_GROUP * g + i + 1) * LANES] for i in range(A_GROUP)],
                           axis=0)


def _unstack_heads(ref, g, x):
    for i in range(A_GROUP):
        ref[:, (A_GROUP * g + i) * LANES:(A_GROUP * g + i + 1) * LANES] = x[i * BLOCK:(i + 1) * BLOCK].astype(ref.dtype)


def _swa_logits(qk, slope, n):
    qi = lax.broadcasted_iota(jnp.int32, (GROUP_ROWS, BLOCK), 0) & (BLOCK - 1)
    kj = lax.broadcasted_iota(jnp.int32, (GROUP_ROWS, BLOCK), 1)
    s_all = qk * SCALE
    out = []
    for i, (dist, ok) in enumerate((
            (n * BLOCK + qi - kj, (kj >= N_PAD) & (n * BLOCK + qi - kj >= 0)),
            (BLOCK + qi - kj, (kj > qi) & (n >= 2)),
            (qi - kj, (kj <= qi) & (n >= 1)))):
        s = s_all[:, i * BLOCK:(i + 1) * BLOCK] - slope * dist.astype(F32)
        out.append(jnp.where(ok, s, NEG))
    return out


def _three_blocks(m_ref, p_ref, c_ref):
    return jnp.concatenate([m_ref[...], p_ref[...], c_ref[...]], axis=0)


def _swa_specs(bsz, nb):
    qspec = pl.BlockSpec((bsz, BLOCK, A_PAD_WIDTH), lambda n: (0, n, 0))
    kv_m = pl.BlockSpec((bsz, BLOCK, LANES), lambda n: (0, 0, 0))
    kv_p = pl.BlockSpec((bsz, BLOCK, LANES), lambda n: (0, jnp.maximum(n - 1, 0), 0))
    kv_c = pl.BlockSpec((bsz, BLOCK, LANES), lambda n: (0, n, 0))
    rowspec = pl.BlockSpec((A_KV_HEADS, GROUP_ROWS, 1), lambda n: (0, 0, 0))
    lsespec = pl.BlockSpec((bsz, 1, A_KV_HEADS, GROUP_ROWS, 1), lambda n: (0, n, 0, 0, 0))
    return qspec, kv_m, kv_p, kv_c, rowspec, lsespec


def _swa_fwd(q, k, v, sink_rows, slope_rows, nb):
    t = q.shape[0]
    l = nb * BLOCK
    bsz = t // l

    def body(q_ref, km_ref, kp_ref, kc_ref, vm_ref, vp_ref, vc_ref, sink_ref, slope_ref, o_ref, lse_ref):
        n = pl.program_id(0)
        lane_group = lax.broadcasted_iota(jnp.int32, (GROUP_ROWS, LANES), 1) // HEAD_DIM
        products = {}
        for b in range(bsz):
            keys = _three_blocks(km_ref.at[b], kp_ref.at[b], kc_ref.at[b])
            for g in range(A_KV_HEADS):
                products[b, g] = lax.dot_general(_stack_heads(q_ref.at[b], g), keys, NT_DIMS,
                                                 preferred_element_type=F32)
        for b in range(bsz):
            values = _three_blocks(vm_ref.at[b], vp_ref.at[b], vc_ref.at[b])
            for g in range(A_KV_HEADS):
                sink = sink_ref[g]
                s_m, s_p, s_c = _swa_logits(products[b, g], slope_ref[g], n)
                m = jnp.maximum(jnp.max(jnp.maximum(jnp.maximum(s_m, s_p), s_c), axis=-1, keepdims=True), sink)
                m_wide = jnp.broadcast_to(m, (GROUP_ROWS, BLOCK))
                e_m = jnp.exp(s_m - m_wide)
                e_p = jnp.exp(s_p - m_wide)
                e_c = jnp.exp(s_c - m_wide)
                z = jnp.sum((e_m + e_p) + e_c, axis=-1, keepdims=True) + jnp.exp(sink - m)
                inv = jnp.broadcast_to(1.0 / z, (GROUP_ROWS, BLOCK))
                probs = jnp.concatenate([(e_m * inv).astype(BF16), (e_p * inv).astype(BF16),
                                         (e_c * inv).astype(BF16)], axis=1)
                o = jnp.dot(probs, values, preferred_element_type=F32)
                _unstack_heads(o_ref.at[b], g, jnp.where(lane_group == g, o, 0.0))
                lse_ref[b, 0, g] = m + jnp.log(z)

    qspec, kv_m, kv_p, kv_c, rowspec, lsespec = _swa_specs(bsz, nb)
    by_example = lambda a: a.reshape(bsz, l, a.shape[1])
    q3, k3, v3 = by_example(q), by_example(k), by_example(v)
    o, lse = pl.pallas_call(
        body,
        name="swa_fwd",
        grid=(nb,),
        in_specs=[qspec, kv_m, kv_p, kv_c, kv_m, kv_p, kv_c, rowspec, rowspec],
        out_specs=[qspec, lsespec],
        out_shape=[jax.ShapeDtypeStruct((bsz, l, A_PAD_WIDTH), BF16),
                   jax.ShapeDtypeStruct((bsz, nb, A_KV_HEADS, GROUP_ROWS, 1), F32)],
        compiler_params=_cparams(("arbitrary",)),
    )(q3, k3, k3, k3, v3, v3, v3, sink_rows, slope_rows)
    return o.reshape(t, A_PAD_WIDTH), lse


def _swa_bwd(q, k, v, do, lse, sink_rows, slope_rows, nb):
    t = q.shape[0]
    l = nb * BLOCK
    bsz = t // l

    def body(q_ref, km_ref, kp_ref, kc_ref, vm_ref, vp_ref, vc_ref, do_ref, lse_ref, sink_ref, slope_ref,
             dq_ref, dk_ref, dv_ref, dsink_ref, dk_acc, dv_acc):
        n = pl.program_id(0)

        @pl.when(n == 0)
        def _():
            dk_acc[...] = jnp.zeros_like(dk_acc)
            dv_acc[...] = jnp.zeros_like(dv_acc)
            dsink_ref[...] = jnp.zeros_like(dsink_ref)

        first_half = lax.broadcasted_iota(jnp.int32, (BLOCK, LANES), 1) < HEAD_DIM
        prev = jnp.maximum(n - 1, 0)
        products = {}
        for b in range(bsz):
            keys = _three_blocks(km_ref.at[b], kp_ref.at[b], kc_ref.at[b])
            values = _three_blocks(vm_ref.at[b], vp_ref.at[b], vc_ref.at[b])
            for g in range(A_KV_HEADS):
                products[b, g] = (
                    lax.dot_general(_stack_heads(q_ref.at[b], g), keys, NT_DIMS, preferred_element_type=F32),
                    lax.dot_general(_stack_heads(do_ref.at[b], g), values, NT_DIMS, preferred_element_type=F32))
        for b in range(bsz):
            keys = _three_blocks(km_ref.at[b], kp_ref.at[b], kc_ref.at[b])
            for g in range(A_KV_HEADS):
                qq = _stack_heads(q_ref.at[b], g)
                dob = _stack_heads(do_ref.at[b], g)
                lse_g = lse_ref[b, 0, g]
                lse_wide = jnp.broadcast_to(lse_g, (GROUP_ROWS, BLOCK))
                qk, dp_all = products[b, g]
                probs = [jnp.exp(s - lse_wide) for s in _swa_logits(qk, slope_ref[g], n)]
                dps = [dp_all[:, i * BLOCK:(i + 1) * BLOCK] for i in range(3)]
                delta = jnp.sum((probs[0] * dps[0] + probs[1] * dps[1]) + probs[2] * dps[2], axis=-1, keepdims=True)
                delta_wide = jnp.broadcast_to(delta, (GROUP_ROWS, BLOCK))
                ds = jnp.concatenate([(p * (dp - delta_wide)).astype(BF16) for p, dp in zip(probs, dps)], axis=1)
                pb = jnp.concatenate([p.astype(BF16) for p in probs], axis=1)
                dq = jnp.dot(ds, keys, preferred_element_type=F32) * SCALE
                dk_all = lax.dot_general(ds, qq, TN_DIMS, preferred_element_type=F32) * SCALE
                dv_all = lax.dot_general(pb, dob, TN_DIMS, preferred_element_type=F32)
                for i, start in enumerate((0, prev * BLOCK, n * BLOCK)):
                    rows = pl.ds(pl.multiple_of(start, BLOCK), BLOCK)
                    dk_acc[b, rows, :] += dk_all[i * BLOCK:(i + 1) * BLOCK]
                    dv_acc[b, rows, :] += dv_all[i * BLOCK:(i + 1) * BLOCK]
                for pair in range(A_GROUP // 2):
                    even = dq[2 * pair * BLOCK:(2 * pair + 1) * BLOCK]
                    odd = dq[(2 * pair + 1) * BLOCK:(2 * pair + 2) * BLOCK]
                    left = even if g == 0 else pltpu.roll(even, HEAD_DIM, 1)
                    right = pltpu.roll(odd, HEAD_DIM, 1) if g == 0 else odd
                    tile = (A_GROUP // 2) * g + pair
                    dq_ref[b, :, tile * LANES:(tile + 1) * LANES] = jnp.where(first_half, left, right).astype(BF16)
                dsink_ref[b, g] += -(jnp.exp(sink_ref[g] - lse_g) * delta)

        @pl.when(n == nb - 1)
        def _():
            dk_ref[...] = dk_acc[...].astype(BF16)
            dv_ref[...] = dv_acc[...].astype(BF16)

    qspec, kv_m, kv_p, kv_c, rowspec, lsespec = _swa_specs(bsz, nb)
    kv_all = pl.BlockSpec((bsz, l, LANES), lambda n: (0, 0, 0))
    by_example = lambda a: a.reshape(bsz, l, a.shape[1])
    q3, k3, v3 = by_example(q), by_example(k), by_example(v)
    dq, dk, dv, dsink = pl.pallas_call(
        body,
        name="swa_bwd",
        grid=(nb,),
        in_specs=[qspec, kv_m, kv_p, kv_c, kv_m, kv_p, kv_c, qspec, lsespec, rowspec, rowspec],
        out_specs=[pl.BlockSpec((bsz, BLOCK, A_WIDTH), lambda n: (0, n, 0)), kv_all, kv_all,
                   pl.BlockSpec((bsz, A_KV_HEADS, GROUP_ROWS, 1), lambda n: (0, 0, 0, 0))],
        out_shape=[jax.ShapeDtypeStruct((bsz, l, A_WIDTH), BF16),
                   jax.ShapeDtypeStruct((bsz, l, LANES), BF16),
                   jax.ShapeDtypeStruct((bsz, l, LANES), BF16),
                   jax.ShapeDtypeStruct((bsz, A_KV_HEADS, GROUP_ROWS, 1), F32)],
        scratch_shapes=[pltpu.VMEM((bsz, l, LANES), F32), pltpu.VMEM((bsz, l, LANES), F32)],
        compiler_params=_cparams(("arbitrary",)),
    )(q3, k3, k3, k3, v3, v3, v3, by_example(do), lse, sink_rows, slope_rows)
    return dq.reshape(t, A_WIDTH), dk.reshape(t, LANES), dv.reshape(t, LANES), dsink


CHUNK = KEY_BLOCKS * BLOCK


def _fox_chunk(qb, ci):
    sb = jnp.maximum(jnp.minimum(KEY_BLOCKS * ci, qb + 1 - KEY_BLOCKS), 0)
    lo = jnp.maximum(ci * CHUNK, N_PAD)
    return sb, lo, pl.ds(pl.multiple_of(sb * BLOCK, BLOCK), CHUNK)


def _fox_logits(s_ref, cr_ref, e, j, sb, lo, qb):
    lane = lax.broadcasted_iota(jnp.int32, (BLOCK, BLOCK), 1)
    ahead = lane - lax.broadcasted_iota(jnp.int32, (BLOCK, BLOCK), 0)
    first = (sb + j) * BLOCK
    s = s_ref[e, :, j * BLOCK:(j + 1) * BLOCK] - cr_ref[e, sb + j]
    return jnp.where((ahead <= qb * BLOCK - first) & (lane >= lo - first), s, NEG)


FOX_PAIRS = 4
FOX_HEADS = 2 * FOX_PAIRS
FOX_STEPS = B_PAIRS // FOX_PAIRS


def _fox_specs(nb):
    l = nb * BLOCK
    q_spec = pl.BlockSpec((BLOCK, FOX_PAIRS * LANES), lambda b, p, i: (b * nb + i, p))
    kv_spec = pl.BlockSpec((l, FOX_HEADS * LANES), lambda b, p, i: (b, p))
    cc_spec = pl.BlockSpec((FOX_HEADS, BLOCK, 1), lambda b, p, i: (b * FOX_STEPS + p, i, 0))
    cr_spec = pl.BlockSpec((FOX_HEADS, nb, 1, BLOCK), lambda b, p, i: (b * FOX_STEPS + p, 0, 0, 0))
    return q_spec, kv_spec, cc_spec, cr_spec


def _fox_fwd(q, k, v, c_row, nb, exchange=None):
    t = q.shape[0]
    bsz = t // (nb * BLOCK)
    assert nb >= KEY_BLOCKS

    n_x = len(exchange.ins) if exchange else 0

    def body(*refs):
        q_ref, k_ref, v_ref, cr_ref = refs[:4]
        o_ref, ox_ref, lse_ref = refs[4 + n_x:7 + n_x]
        s_scr, hi_scr, lo_scr = refs[7 + 2 * n_x:10 + 2 * n_x]
        qb = pl.program_id(2)
        if exchange:
            first = (pl.program_id(0) == 0) & (pl.program_id(1) == 0)
            last = (pl.program_id(0) == bsz - 1) & (pl.program_id(1) == FOX_STEPS - 1)
            _host_exchange(exchange, refs[4:4 + n_x], refs[7 + n_x:7 + 2 * n_x], refs[10 + 2 * n_x:],
                           first & (qb == 0), first & (qb == 2 * nb // 3), last & (qb == nb - 1),
                           late=last & (qb == 0))
        qs = [q_ref[:, a * LANES:(a + 1) * LANES] * SCALE for a in range(FOX_PAIRS)]
        first_half = lax.broadcasted_iota(jnp.int32, (BLOCK, LANES), 1) < HEAD_DIM

        def step(ci, carry):
            stats, accs = carry[:2 * FOX_HEADS], carry[2 * FOX_HEADS:]
            sb, lo, krows = _fox_chunk(qb, ci)
            for e in range(FOX_HEADS):
                s_scr[e] = lax.dot_general(qs[e // 2], k_ref[krows, e * LANES:(e + 1) * LANES], NT_DIMS,
                                           preferred_element_type=F32)
            new_stats, new_accs = [], []
            for a in range(FOX_PAIRS):
                alphas = []
                pv = jnp.zeros((BLOCK, LANES), F32)
                pv_lo = jnp.zeros((BLOCK, LANES), F32)
                for e in (2 * a, 2 * a + 1):
                    m, z = stats[2 * e], stats[2 * e + 1]
                    tile = slice(e * LANES, (e + 1) * LANES)
                    top = None
                    for j in range(KEY_BLOCKS):
                        s = _fox_logits(s_scr, cr_ref, e, j, sb, lo, qb)
                        s_scr[e, :, j * BLOCK:(j + 1) * BLOCK] = s
                        top = s if top is None else jnp.maximum(top, s)
                    m_new = jnp.maximum(m, jnp.max(top, axis=-1, keepdims=True))
                    alpha = jnp.exp(m - m_new)
                    m_wide = jnp.broadcast_to(m_new, (BLOCK, BLOCK))
                    total = None
                    for j in range(KEY_BLOCKS):
                        cols = slice(j * BLOCK, (j + 1) * BLOCK)
                        p = jnp.exp(s_scr[e, :, cols] - m_wide)
                        total = p if total is None else total + p
                        hi = p.astype(BF16)
                        hi_scr[e, :, cols] = hi
                        lo_scr[e, :, cols] = (p - hi.astype(F32)).astype(BF16)
                    z = alpha * z + jnp.sum(total, axis=-1, keepdims=True)
                    vv = v_ref[krows, tile]
                    pv = pv + jnp.dot(hi_scr[e], vv, preferred_element_type=F32)
                    pv_lo = pv_lo + jnp.dot(lo_scr[e], vv, preferred_element_type=F32)
                    new_stats += [m_new, z]
                    alphas.append(alpha)
                alpha = jnp.where(first_half, alphas[0], alphas[1])
                new_accs += [alpha * accs[2 * a] + pv, alpha * accs[2 * a + 1] + pv_lo]
            return (*new_stats, *new_accs)

        col = lambda val: jnp.full((BLOCK, 1), val, F32)
        done = lax.fori_loop(
            0, (qb + KEY_BLOCKS) // KEY_BLOCKS, step,
            (col(NEG), col(0.0)) * FOX_HEADS + (jnp.zeros((BLOCK, LANES), F32),) * (2 * FOX_PAIRS))
        for a in range(FOX_PAIRS):
            m0, z0, m1, z1 = done[4 * a:4 * a + 4]
            acc, acc_lo = done[2 * FOX_HEADS + 2 * a:2 * FOX_HEADS + 2 * a + 2]
            inv = 1.0 / jnp.where(first_half, z0, z1)
            tile = slice(a * LANES, (a + 1) * LANES)
            o_ref[:, tile] = (acc * inv).astype(BF16)
            ox_ref[:, tile] = (acc + acc_lo) * inv
            lse_ref[2 * a] = m0 + jnp.log(z0)
            lse_ref[2 * a + 1] = m1 + jnp.log(z1)

    q_spec, kv_spec, cc_spec, cr_spec = _fox_specs(nb)
    outs = pl.pallas_call(
        body,
        name="fox_fwd",
        grid=(bsz, FOX_STEPS, nb),
        in_specs=[q_spec, kv_spec, kv_spec, cr_spec] + [HBM_SPEC] * n_x,
        out_specs=[q_spec, q_spec, cc_spec] + [HBM_SPEC] * n_x,
        out_shape=[jax.ShapeDtypeStruct((t, B_WIDTH), BF16), jax.ShapeDtypeStruct((t, B_WIDTH), F32),
                   jax.ShapeDtypeStruct((bsz * B_HEADS, nb * BLOCK, 1), F32)] + (exchange.out_shape if exchange else []),
        scratch_shapes=[pltpu.VMEM((FOX_HEADS, BLOCK, CHUNK), F32), pltpu.VMEM((FOX_HEADS, BLOCK, CHUNK), BF16),
                        pltpu.VMEM((FOX_HEADS, BLOCK, CHUNK), BF16)] + (exchange.scratch if exchange else []),
        compiler_params=_cparams(("arbitrary",) * 3 if exchange else ("parallel", "parallel", "arbitrary")),
    )(q, k, v, c_row, *(exchange.ins if exchange else []))
    return outs[:3], outs[3:]


def _fox_bwd(q, k, v, o_exact, do, lse, c_row, nb, exchange=None):
    t = q.shape[0]
    l = nb * BLOCK
    bsz = t // l

    n_x = len(exchange.ins) if exchange else 0

    def body(*refs):
        q_ref, k_ref, v_ref, ox_ref, do_ref, lse_ref, cr_ref = refs[:7]
        dq_ref, dk_ref, dv_ref, dc_ref = refs[7 + n_x:11 + n_x]
        dk_acc, dv_acc, s_scr, dp_scr, p_scr, ds_scr, dq_scr = refs[11 + 2 * n_x:18 + 2 * n_x]
        qb = pl.program_id(2)
        if exchange:
            first = (pl.program_id(0) == 0) & (pl.program_id(1) == 0)
            last = (pl.program_id(0) == bsz - 1) & (pl.program_id(1) == FOX_STEPS - 1)
            _host_exchange(exchange, refs[7:7 + n_x], refs[11 + n_x:11 + 2 * n_x], refs[18 + 2 * n_x:],
                           first & (qb == 0), last & (qb == 0), last & (qb == nb - 1))

        @pl.when(qb == 0)
        def _():
            dk_acc[...] = jnp.zeros_like(dk_acc)
            dv_acc[...] = jnp.zeros_like(dv_acc)
            dc_ref[...] = jnp.zeros_like(dc_ref)

        top_half = lax.broadcasted_iota(jnp.int32, (LANES, BLOCK), 0) < HEAD_DIM
        pair_t = lambda x: jnp.concatenate([jnp.where(top_half, x.T, 0), jnp.where(top_half, 0, x.T)], axis=1)
        first_half = lax.broadcasted_iota(jnp.int32, (BLOCK, LANES), 1) < HEAD_DIM
        wide = lambda col: jnp.broadcast_to(col, (BLOCK, BLOCK))
        qs, dobs, qs_t, dob_t, deltas = [], [], [], [], []
        for a in range(FOX_PAIRS):
            tile = slice(a * LANES, (a + 1) * LANES)
            qs.append(q_ref[:, tile] * SCALE)
            dobs.append(do_ref[:, tile])
            qs_t.append(pair_t(qs[a]))
            dob_t.append(pair_t(dobs[a]))
            weighted = dobs[a].astype(F32) * ox_ref[:, tile]
            deltas += [wide(jnp.sum(jnp.where(first_half, weighted, 0.0), axis=-1, keepdims=True)),
                       wide(jnp.sum(jnp.where(first_half, 0.0, weighted), axis=-1, keepdims=True))]
        lses = [wide(lse_ref[e]) for e in range(FOX_HEADS)]

        dq_scr[...] = jnp.zeros(dq_scr.shape, F32)

        def step(ci, carry):
            sb, lo, krows = _fox_chunk(qb, ci)
            for e in range(FOX_HEADS):
                tile = slice(e * LANES, (e + 1) * LANES)
                s_scr[e] = lax.dot_general(qs[e // 2], k_ref[krows, tile], NT_DIMS, preferred_element_type=F32)
                dp_scr[e] = lax.dot_general(dobs[e // 2], v_ref[krows, tile], NT_DIMS, preferred_element_type=F32)
            for a in range(FOX_PAIRS):
                for e in (2 * a, 2 * a + 1):
                    tile = slice(e * LANES, (e + 1) * LANES)
                    kk = k_ref[krows, tile]
                    for j in range(KEY_BLOCKS):
                        cols = slice(j * BLOCK, (j + 1) * BLOCK)
                        p = jnp.exp(_fox_logits(s_scr, cr_ref, e, j, sb, lo, qb) - lses[e])
                        ds = p * (dp_scr[e, :, cols] - deltas[e])
                        dc_ref[e, sb + j] -= jnp.sum(ds, axis=0, keepdims=True)
                        p_scr[e, :, cols] = p.astype(BF16)
                        ds_scr[e, :, cols] = ds.astype(BF16)
                    dq_scr[a] += jnp.dot(ds_scr[e], kk, preferred_element_type=F32)
                both = slice(2 * a, 2 * a + 2)
                dk_t = jnp.dot(qs_t[a], ds_scr[both].reshape(2 * BLOCK, CHUNK), preferred_element_type=F32)
                dv_t = jnp.dot(dob_t[a], p_scr[both].reshape(2 * BLOCK, CHUNK), preferred_element_type=F32)
                for j in range(KEY_BLOCKS):
                    cols = slice(j * BLOCK, (j + 1) * BLOCK)
                    dk_acc[a * nb + sb + j] += dk_t[:, cols]
                    dv_acc[a * nb + sb + j] += dv_t[:, cols]
            return carry

        lax.fori_loop(0, (qb + KEY_BLOCKS) // KEY_BLOCKS, step, 0)
        for a in range(FOX_PAIRS):
            dq_ref[:, a * LANES:(a + 1) * LANES] = (dq_scr[a] * SCALE).astype(BF16)

        @pl.when(qb == nb - 1)
        def _():
            for a in range(FOX_PAIRS):
                for kb in range(nb):
                    rows = slice(kb * BLOCK, (kb + 1) * BLOCK)
                    for acc, out_ref in ((dk_acc, dk_ref), (dv_acc, dv_ref)):
                        out_ref[rows, a * LANES:(a + 1) * LANES] = acc[a * nb + kb].T.astype(BF16)

    q_spec, kv_spec, cc_spec, cr_spec = _fox_specs(nb)
    dkv_spec = pl.BlockSpec((l, FOX_PAIRS * LANES), lambda b, p, i: (b, p))
    outs = pl.pallas_call(
        body,
        name="fox_bwd",
        grid=(bsz, FOX_STEPS, nb),
        in_specs=[q_spec, kv_spec, kv_spec, q_spec, q_spec, cc_spec, cr_spec] + [HBM_SPEC] * n_x,
        out_specs=[q_spec, dkv_spec, dkv_spec, cr_spec] + [HBM_SPEC] * n_x,
        out_shape=[jax.ShapeDtypeStruct((t, B_WIDTH), BF16), jax.ShapeDtypeStruct((t, B_WIDTH), BF16),
                   jax.ShapeDtypeStruct((t, B_WIDTH), BF16),
                   jax.ShapeDtypeStruct((bsz * B_HEADS, nb, 1, BLOCK), F32)] + (exchange.out_shape if exchange else []),
        scratch_shapes=[pltpu.VMEM((FOX_PAIRS * nb, LANES, BLOCK), F32), pltpu.VMEM((FOX_PAIRS * nb, LANES, BLOCK), F32),
                        pltpu.VMEM((FOX_HEADS, BLOCK, CHUNK), F32), pltpu.VMEM((FOX_HEADS, BLOCK, CHUNK), F32),
                        pltpu.VMEM((FOX_HEADS, BLOCK, CHUNK), BF16), pltpu.VMEM((FOX_HEADS, BLOCK, CHUNK), BF16),
                        pltpu.VMEM((FOX_PAIRS, BLOCK, LANES), F32)]
        + (exchange.scratch if exchange else []),
        compiler_params=_cparams(("arbitrary",) * 3 if exchange else ("parallel", "parallel", "arbitrary")),
    )(q, k, v, o_exact, do, lse, c_row, *(exchange.ins if exchange else []))
    return outs[:4], outs[4:]


def _loss_head(h, final_w, target):
    bsz, l, d = h.shape
    nb = l // BLOCK

    def body(h_ref, w_ref, t_ref, loss_ref, dh_ref, dw_ref):
        n = pl.program_id(0)

        @pl.when(n == 0)
        def _():
            loss_ref[...] = jnp.zeros_like(loss_ref)
            dw_ref[...] = jnp.zeros_like(dw_ref)
            dh_ref[...] = jnp.zeros_like(dh_ref)

        @pl.when(n > 0)
        def _():
            w = w_ref[...]
            for b in range(bsz):
                hh = h_ref[b]
                r = _rms_scale(hh)
                err = (hh * r) * w - t_ref[b]
                loss_ref[...] += 0.5 * jnp.sum(jnp.mean(err * err, axis=-1, keepdims=True), axis=0, keepdims=True)
                dy = err * (1.0 / d)
                dh, dw = _rms_bwd(dy, hh, w)
                dh_ref[b] = dh
                dw_ref[...] += dw

    return pl.pallas_call(
        body,
        name="loss_head",
        grid=(nb,),
        in_specs=[
            pl.BlockSpec((bsz, BLOCK, d), lambda n: (0, n, 0)),
            pl.BlockSpec((1, d), lambda n: (0, 0)),
            pl.BlockSpec((bsz, BLOCK, d), lambda n: (0, jnp.maximum(n - 1, 0), 0)),
        ],
        out_specs=[
            pl.BlockSpec((1, 128), lambda n: (0, 0)),
            pl.BlockSpec((bsz, BLOCK, d), lambda n: (0, n, 0)),
            pl.BlockSpec((1, d), lambda n: (0, 0)),
        ],
        out_shape=[jax.ShapeDtypeStruct((1, 128), F32), jax.ShapeDtypeStruct((bsz, l, d), F32),
                   jax.ShapeDtypeStruct((1, d), F32)],
        compiler_params=_cparams(("arbitrary",)),
    )(h, final_w, target)


def _pad_tiles(w, src, heads, lane_slot, axis):
    pieces = []
    for h in range(heads):
        x = lax.slice_in_dim(w, src + HEAD_DIM * h, src + HEAD_DIM * (h + 1), axis=axis)
        z = jnp.zeros_like(x)
        pieces += [x, z] if lane_slot(h) == 0 else [z, x]
    return pieces


def _unpad_tiles(g, off, heads, lane_slot, axis):
    return [lax.slice_in_dim(g, off + LANES * h + HEAD_DIM * lane_slot(h),
                             off + LANES * h + HEAD_DIM * (lane_slot(h) + 1), axis=axis) for h in range(heads)]


REF_RUNS = ((0, SRC_QB, 0, 0), (SRC_QB, SRC_F, 1, 0), (SRC_F, SRC_GA, 0, OFF_F), (SRC_GA, W_IN_COLS, 2, 0))
P_RUNS = ((0, OFF_F, 0), (OFF_F, OFF_F + B_HEADS, SRC_F), (OFF_QB, OFF_GA, SRC_QB), (OFF_GA, P_COLS, SRC_GA))
SHARD_COLS = W_IN_COLS // N_CHIPS
SHARD_PAD_COLS = -(-SHARD_COLS // LANES) * LANES


def _place(tile, lane, src_ref, src, dst, length):
    for t in range(src // LANES, (src + length - 1) // LANES + 1):
        x = src_ref[:, t * LANES:(t + 1) * LANES].astype(F32)
        shift = (dst - src) % LANES
        moved = pltpu.roll(x, shift, 1) if shift else x
        from_t = (lane >= max(dst, dst + t * LANES - src)) & (lane < min(dst + length, dst + (t + 1) * LANES - src))
        tile = jnp.where(from_t, moved, tile)
    return tile


def _layout_w_in(stacked):
    d = stacked.shape[1]
    rows = _tile(d, 256)

    def body(s_ref, o_ref):
        lane = lax.broadcasted_iota(jnp.int32, (rows, LANES), 1)
        for lo in range(0, P_COLS, LANES):
            tile = jnp.zeros((rows, LANES), F32)
            for first, end, ref_col in P_RUNS:
                start, stop = max(lo, first), min(lo + LANES, end)
                while start < stop:
                    k, src = divmod(ref_col + start - first, SHARD_COLS)
                    length = min(stop - start, SHARD_COLS - src)
                    tile = _place(tile, lane, s_ref.at[k], src, start - lo, length)
                    start += length
            o_ref[:, lo:lo + LANES] = tile.astype(o_ref.dtype)

    return pl.pallas_call(
        body,
        name="layout_w_in",
        grid=(d // rows,),
        in_specs=[pl.BlockSpec((N_CHIPS, rows, SHARD_PAD_COLS), lambda i: (0, i, 0))],
        out_specs=pl.BlockSpec((rows, P_COLS), lambda i: (i, 0)),
        out_shape=jax.ShapeDtypeStruct((d, P_COLS), stacked.dtype),
        compiler_params=_cparams(("parallel",)),
    )(stacked)


def _stack_w_in_grad(pieces):
    d = pieces[0].shape[0]
    rows = _tile(d, 256)

    def body(*refs):
        piece_refs, o_ref = refs[:-1], refs[-1]
        lane = lax.broadcasted_iota(jnp.int32, (rows, LANES), 1)
        for k in range(N_CHIPS):
            for j in range(0, SHARD_COLS, LANES):
                width = min(LANES, SHARD_COLS - j)
                lo = k * SHARD_COLS + j
                tile = jnp.zeros((rows, LANES), F32)
                for first, end, piece, at in REF_RUNS:
                    start, stop = max(lo, first), min(lo + width, end)
                    if start >= stop:
                        continue
                    tile = _place(tile, lane, piece_refs[piece], at + start - first, start - lo, stop - start)
                o_ref[k, :, j:j + width] = tile[:, :width]

    return pl.pallas_call(
        body,
        name="stack_w_in_grad",
        grid=(d // rows,),
        in_specs=[pl.BlockSpec((rows, p.shape[1]), lambda i: (i, 0)) for p in pieces],
        out_specs=pl.BlockSpec((N_CHIPS, rows, SHARD_COLS), lambda i: (0, i, 0)),
        out_shape=jax.ShapeDtypeStruct((N_CHIPS, d, SHARD_COLS), F32),
        compiler_params=_cparams(("parallel",)),
    )(*pieces)


def _local_step(x, target, meta, norms, b_forget, sinks, w, comm=None):
    n1, nmix, n2, nfin = norms
    w1i, w1o = w[:2]
    if meta is not None:
        x = jnp.concatenate([jnp.zeros((x.shape[0], N_PAD, x.shape[2]), F32),
                             jnp.broadcast_to(meta[None], (x.shape[0], N_META, x.shape[2])), x], axis=1)
    bsz, l, d = x.shape
    nb = l // BLOCK
    t = bsz * l
    h0 = x.reshape(t, d)

    if comm is None:
        (h1, g1, u1), _ = _ffn_fwd(h0, n1, w1i, w1o)
        w_in, wa, wb, wo, w2i, w2o = w[2:]
        w_in = jnp.pad(w_in.reshape(d, N_CHIPS, SHARD_COLS).transpose(1, 0, 2),
                       ((0, 0), (0, 0), (0, SHARD_PAD_COLS - SHARD_COLS)))
    else:
        (h1, g1, u1), (w_in,) = _ffn_fwd(h0, n1, w1i, w1o, comm.gather(GATHER_PROJ))
    wp = _layout_w_in(w_in)
    un, qa, ka, va, qb, kb, vb, ga, gb, f_logit = _proj_fwd(h1, nmix, wp)
    b_pad = jnp.concatenate([b_forget, jnp.zeros((1, F_COLS - B_HEADS), F32)], axis=1)
    c = _forget_cumsum(f_logit, b_pad, nb)
    c_heads = c[:, :B_HEADS].reshape(bsz, l, B_HEADS).transpose(0, 2, 1).reshape(bsz * B_HEADS, l)
    c_row = c_heads.reshape(bsz * B_HEADS, nb, 1, BLOCK)

    slopes = jnp.exp2(-8.0 * jnp.arange(1, A_HEADS + 1, dtype=F32) / A_HEADS)
    slope_rows = jnp.repeat(slopes.reshape(A_KV_HEADS, A_GROUP), BLOCK, axis=1)[:, :, None]
    sink_rows = jnp.repeat(sinks.reshape(A_KV_HEADS, A_GROUP), BLOCK, axis=1)[:, :, None]

    oa, lse_a = _swa_fwd(qa, ka, va, sink_rows, slope_rows, nb)
    if comm is None:
        (ob, ob_exact, lse_b), _ = _fox_fwd(qb, kb, vb, c_row, nb)
    else:
        (ob, ob_exact, lse_b), (wa, wb, wo, w2i, w2o) = _fox_fwd(qb, kb, vb, c_row, nb, comm.gather(GATHER_LATE))
    wa_p = jnp.concatenate(_pad_tiles(wa, 0, A_HEADS, A_SLOT, 0), axis=0)
    h2, mixed = _merge_fwd(h1, oa, ob, ga, gb, wa_p, wb, wo)
    (h3, g2, u2), _ = _ffn_fwd(h2, n2, w2i, w2o)
    loss, dh3, d_nfin = _loss_head(h3.reshape(bsz, l, d), nfin, target)

    (dh2, n2b, a2, dgu2, df2, dn2_parts), _ = _ffn_bwd(dh3.reshape(t, d), h2, n2, g2, u2, w2i, w2o)
    g_w2o = _tn_matmul(a2, df2, "grad_ffn2_w_out")
    g_w2i = _tn_matmul(n2b, dgu2, "grad_ffn2_w_in")

    hosted = comm.swap("ffn2", dict(ffn2_w_in=g_w2i, ffn2_w_out=g_w2o)) if comm else None
    (dya, dyb, doa, dob, dgates, dh2b), swapped = _merge_bwd(dh2, oa, ob, ga, gb, wa_p, wb, wo, hosted)
    g_wo = _tn_matmul(mixed, dh2b, "grad_w_out")
    g_wa = jnp.concatenate(_unpad_tiles(_tn_matmul(oa, dya, "grad_w_branch_a"), 0, A_HEADS, A_SLOT, 0), axis=0)
    g_wb = _tn_matmul(ob, dyb, "grad_w_branch_b")

    dqa, dka, dva, dsink_rows = _swa_bwd(qa, ka, va, doa, lse_a, sink_rows, slope_rows, nb)
    hosted = comm.scatter("ffn2", swapped) if comm else None
    (dqb, dkb, dvb, dc_row), pieces = _fox_bwd(qb, kb, vb, ob_exact, dob, lse_b, c_row, nb, hosted)
    if comm:
        comm.received("ffn2", pieces)
    dc = dc_row.reshape(bsz, B_HEADS, l).transpose(0, 2, 1).reshape(t, B_HEADS)
    dc = jnp.concatenate([dc, jnp.zeros((t, F_COLS - B_HEADS), F32)], axis=1)
    df_logit, db_parts = _forget_cumsum_bwd(dc, f_logit, b_pad, nb)

    dproj = (jnp.concatenate([dqa, dka, dva, df_logit], axis=1), jnp.concatenate([dqb, dkb, dvb], axis=1), dgates)
    g_win = _stack_w_in_grad([_tn_matmul(un, piece, "grad_w_in_" + tag) for piece, tag in zip(dproj, ("a", "b", "gates"))])
    if comm is None:
        g_win = g_win.transpose(1, 0, 2).reshape(d, W_IN_COLS)
    hosted = comm.swap("mixer", dict(w_in=g_win, w_branch_a=g_wa, w_branch_b=g_wb, w_out=g_wo)) if comm else None
    (dh1, dnmix_parts), swapped = _proj_bwd(dh2, h1, nmix, dproj, wp, hosted)
    hosted = comm.scatter("mixer", swapped) if comm else None
    (dh0, n1b, a1, dgu1, df1, dn1_parts), pieces = _ffn_bwd(dh1, h0, n1, g1, u1, w1i, w1o, hosted)
    dh0 = dh0.reshape(bsz, l, d)
    grad_x = dh0[:, PREFIX:]
    small = dict(
        meta_tokens=jnp.sum(dh0[:, N_PAD:PREFIX], axis=0),
        ffn1_norm=jnp.sum(dn1_parts, axis=0),
        mix_norm=jnp.sum(dnmix_parts, axis=0),
        ffn2_norm=jnp.sum(dn2_parts, axis=0),
        final_norm=d_nfin,
        b_forget=jnp.sum(db_parts, axis=0)[:, :B_HEADS],
        attn_sinks=jnp.sum(dsink_rows.reshape(bsz, A_HEADS, BLOCK), axis=(0, 2)).reshape(1, A_HEADS),
    )
    if comm is None:
        g_w1o = _tn_matmul(a1, df1, "grad_ffn1_w_out")
        g_w1i = _tn_matmul(n1b, dgu1, "grad_ffn1_w_in")
    else:
        comm.received("mixer", pieces)
        g_w1o, gathered = _tn_matmul(a1, df1, "grad_ffn1_w_out", comm.small_gather(loss, small))
        comm.small_gathered(gathered)
        swapped = _run_exchange(comm.swap("ffn1_out", dict(ffn1_w_out=g_w1o)), "exchange_halves_ffn1_out")
        g_w1i, pieces = _tn_matmul(n1b, dgu1, "grad_ffn1_w_in", comm.scatter("ffn1_out", swapped))
        comm.received("ffn1_out", pieces)
    big = dict(ffn1_w_in=g_w1i, ffn1_w_out=g_w1o, w_in=g_win, w_branch_a=g_wa, w_branch_b=g_wb,
               w_out=g_wo, ffn2_w_in=g_w2i, ffn2_w_out=g_w2o)
    return loss, grad_x, small, big


BIG = (
    ("ffn1_w_in", (D_MODEL, 5632), 1),
    ("ffn1_w_out", (2816, D_MODEL), 0),
    ("w_in", (D_MODEL, W_IN_COLS), 1),
    ("w_branch_a", (A_WIDTH, D_MODEL), 1),
    ("w_branch_b", (B_WIDTH, D_MODEL), 1),
    ("w_out", (D_MODEL, D_MODEL), 0),
    ("ffn2_w_in", (D_MODEL, 5632), 1),
    ("ffn2_w_out", (2816, D_MODEL), 0),
)
STACKED = "w_in"


def _coords():
    return lax.axis_index("x"), lax.axis_index("y"), lax.axis_index("c")


def _other_chips(x, y):
    return ((1 - x, y), (x, 1 - y), (1 - x, 1 - y))


def _chip_part(ref, name, shape, axis, k):
    if name == STACKED:
        return ref.at[k]
    size = shape[axis] // N_CHIPS
    start = pl.multiple_of(k * size, size)
    return ref.at[pl.ds(start, size), :] if axis == 0 else ref.at[:, pl.ds(start, size)]


def _full_shape(name, shape):
    return (N_CHIPS, shape[0], shape[1] // N_CHIPS) if name == STACKED else shape


class _Exchange:
    def __init__(self, ins, out_shape, n_sems, ops):
        self.ins, self.out_shape, self.n_sems, self.ops = list(ins), list(out_shape), n_sems, ops

    @property
    def scratch(self):
        return [pltpu.SemaphoreType.DMA((self.n_sems,)), pltpu.SemaphoreType.DMA((self.n_sems,))]


SEMS_PER_GATHER = 9


def _gather_exchange(shards, table):
    n = len(table)
    x_nbr, y_nbr, diagonal = 0, 1, 2

    def ops(ins, outs, send_sems, recv_sems):
        x, y, c = _coords()
        mine = 2 * x + y
        sibling = (x, y, 1 - c)
        chips = _other_chips(x, y)
        slots = [2 * chip[0] + chip[1] for chip in chips]

        def part(i, k):
            name, shape, axis = table[i][:3]
            return _chip_part(outs[i], name, shape, axis, k)

        def half(ref, h):
            rows = ref.shape[0] // 2
            return ref.at[pl.ds(pl.multiple_of(h * rows, rows), rows), :]

        def remote(i, sem, src, dst, device):
            sem = SEMS_PER_GATHER * i + sem
            return pltpu.make_async_remote_copy(src, dst, send_sems.at[sem], recv_sems.at[sem],
                                                device_id=device, device_id_type=MESH_ID)

        def own(i):
            return remote(i, 0, ins[i], part(i, mine), sibling)

        def fetch(i, j, slot):
            if table[i][4]:
                src, dst = half(ins[i], c), half(part(i, slot), c)
            else:
                src, dst = ins[i], part(i, slot)
            return remote(i, 1 + j, src, dst, (chips[j][0], chips[j][1], c))

        def relayed(i, via, of):
            region = half(half(part(i, slots[of]), c), via)
            return remote(i, 4 + via, region, region, (chips[via][0], chips[via][1], c))

        def forward(i, j, h):
            region = half(part(i, slots[j]), h)
            return remote(i, 6 + j, region, region, sibling)

        def start():
            for i in range(n):
                for j in (x_nbr, y_nbr) if table[i][4] else (x_nbr, y_nbr, diagonal):
                    fetch(i, j, mine).start()
            for i in range(n):
                own(i).start()

        def relay():
            for i in range(n):
                if not table[i][4]:
                    for j in range(3):
                        fetch(i, j, slots[j]).wait_recv()
                    continue
                fetch(i, y_nbr, slots[y_nbr]).wait_recv()
                relayed(i, x_nbr, y_nbr).start()
                forward(i, y_nbr, c).start()
                fetch(i, x_nbr, slots[x_nbr]).wait_recv()
                relayed(i, y_nbr, x_nbr).start()
                forward(i, x_nbr, c).start()

        def relay_diagonal():
            for i in range(n):
                if table[i][4]:
                    relayed(i, x_nbr, diagonal).wait_recv()
                    relayed(i, y_nbr, diagonal).wait_recv()
                    forward(i, diagonal, c).start()

        def finish():
            for i in range(n):
                own(i).wait()
                for j in range(3):
                    if table[i][4]:
                        forward(i, j, 1 - c).wait_recv()
                        forward(i, j, c).wait_send()
                    if j != diagonal or not table[i][4]:
                        fetch(i, j, mine).wait_send()
                if table[i][4]:
                    relayed(i, x_nbr, y_nbr).wait_send()
                    relayed(i, y_nbr, x_nbr).wait_send()

        return start, relay, relay_diagonal, finish

    out_shape = [jax.ShapeDtypeStruct(_full_shape(name, shape), dtype) for name, shape, _, dtype, _ in table]
    return _Exchange(shards, out_shape, SEMS_PER_GATHER * n, ops)


STREAM_ROWS = 256


def _stream_rows(src, dst, dst_first, buf, sem_in, sem_out):
    bsz, n_rows, _ = src.shape
    chunks = [(b, r) for b in range(bsz) for r in range(0, n_rows, STREAM_ROWS)]

    def load(i):
        b, r = chunks[i]
        return pltpu.make_async_copy(src.at[b, pl.ds(r, STREAM_ROWS), :], buf.at[i % 2], sem_in.at[i % 2])

    def store(i):
        b, r = chunks[i]
        return pltpu.make_async_copy(buf.at[i % 2], dst.at[b, pl.ds(dst_first + r, STREAM_ROWS), :],
                                     sem_out.at[i % 2])

    load(0).start()
    for i in range(len(chunks)):
        if i + 1 < len(chunks):
            if i >= 1:
                store(i - 1).wait()
            load(i + 1).start()
        load(i).wait()
        store(i).start()
    for i in range(max(len(chunks) - 2, 0), len(chunks)):
        store(i).wait()


def _run_exchange(exchange, name):
    n = len(exchange.ins)

    def body(*refs):
        for phase in exchange.ops(refs[:n], refs[n:2 * n], *refs[2 * n:]):
            phase()

    return pl.pallas_call(
        body,
        name=name,
        in_specs=[HBM_SPEC] * n,
        out_specs=[HBM_SPEC] * n,
        out_shape=exchange.out_shape,
        scratch_shapes=exchange.scratch,
    )(*exchange.ins)


CAST_ROWS = 64


def _cast_hosting(arrays, exchange, name, stream=None):
    n_a, n_x = len(arrays), len(exchange.ins)
    k = 1 if stream else 0

    def body(*refs):
        a_in, x_in = refs[:n_a], refs[n_a:n_a + n_x]
        outs = refs[n_a + n_x + k:]
        a_out, x_out = outs[:n_a], outs[n_a:n_a + n_x]
        scratch = outs[n_a + n_x + k:]
        start, *rest = exchange.ops(x_in, x_out, *scratch[:2])
        start()
        for src, dst in zip(a_in, a_out):
            def rows(i, carry, src=src, dst=dst):
                window = pl.ds(pl.multiple_of(i * CAST_ROWS, CAST_ROWS), CAST_ROWS)
                cols = src.shape[1]
                if dst.shape[1] != cols:
                    dst[window, dst.shape[1] - LANES:] = jnp.zeros((CAST_ROWS, LANES), BF16)
                dst[window, :cols] = src[window, :].astype(BF16)
                return carry

            lax.fori_loop(0, src.shape[0] // CAST_ROWS, rows, 0)
        if stream:
            x_ref, h_ref = refs[n_a + n_x], outs[n_a + n_x]
            buf, sem_in, sem_out = scratch[2:]
            _stream_rows(x_ref, h_ref, PREFIX, buf, sem_in, sem_out)
        for phase in rest:
            phase()
        if stream:
            meta = pltpu.make_async_copy(x_out[stream[1]], buf.at[1, pl.ds(0, N_META), :], sem_in.at[0])
            meta.start()
            buf[0, 0:N_PAD, :] = jnp.zeros((N_PAD, buf.shape[2]), F32)
            meta.wait()
            buf[0, N_PAD:PREFIX, :] = buf[1, 0:N_META, :]
            puts = [pltpu.make_async_copy(buf.at[0, pl.ds(0, PREFIX), :], h_ref.at[b, pl.ds(0, PREFIX), :], sem_out.at[b])
                    for b in range(h_ref.shape[0])]
            for put in puts:
                put.start()
            for put in puts:
                put.wait()

    extra_in, extra_out, extra_scratch = [], [], []
    if stream:
        x = stream[0]
        assert x.shape[0] <= 2 and x.shape[1] % STREAM_ROWS == 0 and x.dtype == F32
        extra_in = [x]
        extra_out = [jax.ShapeDtypeStruct((x.shape[0], PREFIX + x.shape[1], x.shape[2]), F32)]
        extra_scratch = [pltpu.VMEM((2, STREAM_ROWS, x.shape[2]), F32),
                         pltpu.SemaphoreType.DMA((2,)), pltpu.SemaphoreType.DMA((2,))]
    outs = pl.pallas_call(
        body,
        name=name,
        in_specs=[VMEM_SPEC] * n_a + [HBM_SPEC] * (n_x + k),
        out_specs=[VMEM_SPEC] * n_a + [HBM_SPEC] * (n_x + k),
        out_shape=[jax.ShapeDtypeStruct((a.shape[0], -(-a.shape[1] // LANES) * LANES), BF16) for a in arrays]
        + exchange.out_shape + extra_out,
        scratch_shapes=exchange.scratch + extra_scratch,
        compiler_params=pltpu.CompilerParams(vmem_limit_bytes=VMEM_LIMIT),
    )(*arrays, *exchange.ins, *extra_in)
    return outs[:n_a], outs[n_a:]


def _host_exchange(exchange, in_refs, out_refs, sem_refs, first, middle, last, late=None):
    start, *relays, finish = exchange.ops(in_refs, out_refs, *sem_refs)
    pl.when(first)(start)
    pl.when(middle)(relays[0])
    if len(relays) > 1:
        pl.when(last if late is None else late)(relays[1])
    pl.when(last)(finish)


def _halves_view(name, shape, axis):
    r, c = shape
    if name == STACKED:
        return (N_CHIPS, 2, r // 2, c // N_CHIPS), lambda ref, h: ref.at[:, h]
    if axis == 1:
        return (2, r // 2, c), lambda ref, h: ref.at[h]
    return (N_CHIPS, 2, r // N_CHIPS // 2, c), lambda ref, h: ref.at[:, h]


def _halves_exchange(grads, entries):
    n_w = len(entries)
    views = [_halves_view(*entry) for entry in entries]

    def ops(ins, outs, send_sems, recv_sems):
        x, y, c = _coords()
        copies = [pltpu.make_async_remote_copy(views[i][1](ins[i], 1 - c), outs[i], send_sems.at[i], recv_sems.at[i],
                                               device_id=(x, y, 1 - c), device_id_type=MESH_ID) for i in range(n_w)]

        def start():
            for cp in copies:
                cp.start()

        def finish():
            for cp in copies:
                cp.wait()

        return start, lambda: None, finish

    half_shape = lambda v: tuple(d for i, d in enumerate(v) if i != (1 if len(v) == 4 else 0))
    out_shape = [jax.ShapeDtypeStruct(half_shape(v[0]), F32) for v in views]
    return _Exchange([g.reshape(v[0]) for g, v in zip(grads, views)], out_shape, n_w, ops)


def _add_sibling(g_view, recv, c, name):
    shape = recv.shape
    if len(shape) == 2:
        tr = _tile(shape[0], 128, 16)
        grid = (shape[0] // tr,)
        g_spec = pl.BlockSpec((None, tr, shape[1]), lambda i, c_ref: (c_ref[0], i, 0))
        r_spec = pl.BlockSpec((tr, shape[1]), lambda i, c_ref: (i, 0))
    else:
        tr = _tile(shape[1], 256, 16)
        grid = (N_CHIPS, shape[1] // tr)
        g_spec = pl.BlockSpec((None, None, tr, shape[2]), lambda k, i, c_ref: (k, c_ref[0], i, 0))
        r_spec = pl.BlockSpec((None, tr, shape[2]), lambda k, i, c_ref: (k, i, 0))

    def body(c_ref, g_ref, r_ref, o_ref):
        o_ref[...] = (g_ref[...] + r_ref[...]).astype(BF16)

    return pl.pallas_call(
        body,
        name="add_sibling_" + name,
        grid_spec=pltpu.PrefetchScalarGridSpec(num_scalar_prefetch=1, grid=grid, in_specs=[g_spec, r_spec],
                                               out_specs=r_spec),
        out_shape=jax.ShapeDtypeStruct(shape, BF16),
        compiler_params=_cparams(("parallel",) * len(grid)),
    )(c, g_view, recv)


def _piece_of(ref, name, axis, k):
    if name == STACKED or axis == 0:
        return ref.at[k]
    size = ref.shape[1] // N_CHIPS
    return ref.at[:, pl.ds(pl.multiple_of(k * size, size), size)]


def _piece_shape(name, shape, axis):
    r, c = shape
    return (r // 2, c // N_CHIPS) if (axis == 1) else (r // N_CHIPS // 2, c)


def _scatter_exchange(partials, entries):
    n_w = len(entries)

    def ops(ins, outs, send_sems, recv_sems):
        x, y, c = _coords()
        chips = _other_chips(x, y)
        copies = []
        for i, (name, _, axis) in enumerate(entries):
            for j, chip in enumerate(chips):
                sem = 3 * i + j
                copies.append(pltpu.make_async_remote_copy(
                    _piece_of(ins[i], name, axis, 2 * chip[0] + chip[1]), outs[i].at[j], send_sems.at[sem],
                    recv_sems.at[sem], device_id=(chip[0], chip[1], c), device_id_type=MESH_ID))

        def start():
            for cp in copies:
                cp.start()

        def finish():
            for cp in copies:
                cp.wait()

        return start, lambda: None, finish

    out_shape = [jax.ShapeDtypeStruct((3,) + _piece_shape(*entry), BF16) for entry in entries]
    return _Exchange(partials, out_shape, 3 * n_w, ops)


def _add_chips(partial, recv, mine, name, axis):
    rows, cols = recv.shape[1:]
    tr = _tile(rows, 256, 16)
    if name == STACKED or axis == 0:
        p_spec = pl.BlockSpec((None, tr, cols), lambda i, k_ref: (k_ref[0], i, 0))
    else:
        p_spec = pl.BlockSpec((tr, cols), lambda i, k_ref: (i, k_ref[0]))

    def body(k_ref, p_ref, r_ref, o_ref):
        f32 = lambda a: a.astype(F32)
        o_ref[...] = ((f32(p_ref[...]) + f32(r_ref[0])) + f32(r_ref[1])) + f32(r_ref[2])

    return pl.pallas_call(
        body,
        name="add_chips_" + name,
        grid_spec=pltpu.PrefetchScalarGridSpec(
            num_scalar_prefetch=1, grid=(rows // tr,),
            in_specs=[p_spec, pl.BlockSpec((3, tr, cols), lambda i, k_ref: (0, i, 0))],
            out_specs=pl.BlockSpec((tr, cols), lambda i, k_ref: (i, 0))),
        out_shape=jax.ShapeDtypeStruct((rows, cols), F32),
        compiler_params=_cparams(("parallel",)),
    )(mine, partial, recv)


def _share_with_sibling(halves):
    n_w = len(halves)

    def body(*refs):
        ins, outs = refs[:n_w], refs[n_w:2 * n_w]
        send_sems, recv_sems = refs[2 * n_w:]
        x, y, c = _coords()
        copies = [pltpu.make_async_remote_copy(ins[i], outs[i], send_sems.at[i], recv_sems.at[i],
                                               device_id=(x, y, 1 - c), device_id_type=MESH_ID) for i in range(n_w)]
        for cp in copies:
            cp.start()
        for cp in copies:
            cp.wait()

    return pl.pallas_call(
        body,
        name="share_with_sibling",
        in_specs=[HBM_SPEC] * n_w,
        out_specs=[HBM_SPEC] * n_w,
        out_shape=[jax.ShapeDtypeStruct(h.shape, F32) for h in halves],
        scratch_shapes=[pltpu.SemaphoreType.DMA((n_w,)), pltpu.SemaphoreType.DMA((n_w,))],
    )(*halves)


SMALL_ROWS = 168


def _small_exchange(buf):
    def ops(ins, outs, send_sems, recv_sems):
        x, y, c = _coords()
        me = 4 * x + 2 * y + c
        peers = [(x ^ fx, y ^ fy, c ^ fc) for fx in (0, 1) for fy in (0, 1) for fc in (0, 1)][1:]

        def copy(j, slot, dev):
            return pltpu.make_async_remote_copy(ins[0], outs[0].at[slot], send_sems.at[j], recv_sems.at[j],
                                                device_id=dev, device_id_type=MESH_ID)

        own = pltpu.make_async_copy(ins[0], outs[0].at[me], send_sems.at[N_DEV - 1])

        def start():
            own.start()
            for j, dev in enumerate(peers):
                copy(j, me, dev).start()

        def finish():
            for j, dev in enumerate(peers):
                copy(j, 4 * dev[0] + 2 * dev[1] + dev[2], dev).wait()
            own.wait()

        return start, lambda: None, finish

    return _Exchange([buf], [jax.ShapeDtypeStruct((N_DEV,) + buf.shape, F32)], N_DEV, ops)


def _sum_devices(gathered):
    def body(g_ref, out_ref):
        acc = g_ref[0]
        for d in range(1, N_DEV):
            acc = acc + g_ref[d]
        out_ref[...] = acc

    return pl.pallas_call(
        body,
        name="sum_devices",
        in_specs=[VMEM_SPEC],
        out_specs=VMEM_SPEC,
        out_shape=jax.ShapeDtypeStruct(gathered.shape[1:], F32),
    )(gathered)


def _adamw(w, g, m, v, copy_g=False):
    r, rest = w.shape[0], w.shape[1:]
    per_row = 1
    for dim in rest:
        per_row *= dim
    tr = _tile(r, max(8, (5 << 19) // (4 * per_row)), 8 if len(rest) == 1 else 1)

    def body(w_ref, g_ref, m_ref, v_ref, *out_refs):
        d_ref, mo_ref, vo_ref = out_refs[-3:]
        gg = g_ref[...]
        if copy_g:
            out_refs[0][...] = gg
        mm = ADAM_B1 * m_ref[...] + (1.0 - ADAM_B1) * gg
        vv = ADAM_B2 * v_ref[...] + (1.0 - ADAM_B2) * (gg * gg)
        m_hat = mm / (1.0 - ADAM_B1 ** ADAM_STEP)
        v_hat = vv / (1.0 - ADAM_B2 ** ADAM_STEP)
        d_ref[...] = -ADAM_LR * (m_hat / (jnp.sqrt(v_hat) + ADAM_EPS) + ADAM_WD * w_ref[...])
        mo_ref[...] = mm
        vo_ref[...] = vv

    n_out = 4 if copy_g else 3
    spec = pl.BlockSpec((tr,) + rest, lambda i: (i,) + (0,) * len(rest))
    return pl.pallas_call(
        body,
        name="adamw",
        grid=(r // tr,),
        in_specs=[spec] * 4,
        out_specs=[spec] * n_out,
        out_shape=[jax.ShapeDtypeStruct(w.shape, F32)] * n_out,
        compiler_params=_cparams(("parallel",)),
    )(w, g, m, v)


def _adamw_halves(w, own, other, m, v, c, name):
    r, cols = w.shape
    half = r // 2
    tr = _tile(half, 256, 8)
    nt = half // tr
    whole = pl.BlockSpec((tr, cols), lambda h, i, c_ref: (h * nt + i, 0))
    part = pl.BlockSpec((tr, cols), lambda h, i, c_ref: (i, 0))

    def body(c_ref, w_ref, own_ref, other_ref, m_ref, v_ref, g_ref, d_ref, mo_ref, vo_ref):
        gg = jnp.where(pl.program_id(0) == c_ref[0], own_ref[...], other_ref[...])
        g_ref[...] = gg
        mm = ADAM_B1 * m_ref[...] + (1.0 - ADAM_B1) * gg
        vv = ADAM_B2 * v_ref[...] + (1.0 - ADAM_B2) * (gg * gg)
        m_hat = mm / (1.0 - ADAM_B1 ** ADAM_STEP)
        v_hat = vv / (1.0 - ADAM_B2 ** ADAM_STEP)
        d_ref[...] = -ADAM_LR * (m_hat / (jnp.sqrt(v_hat) + ADAM_EPS) + ADAM_WD * w_ref[...])
        mo_ref[...] = mm
        vo_ref[...] = vv

    return pl.pallas_call(
        body,
        name="adamw_" + name,
        grid_spec=pltpu.PrefetchScalarGridSpec(
            num_scalar_prefetch=1, grid=(2, nt),
            in_specs=[whole, part, part, whole, whole], out_specs=[whole] * 4),
        out_shape=[jax.ShapeDtypeStruct((r, cols), F32)] * 4,
        compiler_params=_cparams(("parallel", "parallel")),
    )(c, w, own, other, m, v)


def _adamw_group(items, c, exchange):
    n_w, n_x = len(items), len(exchange.ins)
    plans, n_steps = [], 0
    for _, w, *_ in items:
        half = w.shape[0] // 2
        tr = _tile(half, 128, 8)
        plans.append((n_steps, half // tr, tr, w.shape[1]))
        n_steps += 2 * (half // tr)

    def body(c_ref, *refs):
        ins, x_in = refs[:5 * n_w], refs[5 * n_w:5 * n_w + n_x]
        outs, x_out = refs[5 * n_w + n_x:9 * n_w + n_x], refs[9 * n_w + n_x:9 * n_w + 2 * n_x]
        s = pl.program_id(0)
        _host_exchange(exchange, x_in, x_out, refs[9 * n_w + 2 * n_x:], s == 0, s == n_steps - 1, s == n_steps - 1)
        for q, (start, nt, _, _) in enumerate(plans):
            def update(start=start, nt=nt, ins=ins[5 * q:5 * q + 5], outs=outs[4 * q:4 * q + 4]):
                w_ref, own_ref, other_ref, m_ref, v_ref = ins
                g_ref, d_ref, mo_ref, vo_ref = outs
                gg = jnp.where((s - start) // nt == c_ref[0], own_ref[...], other_ref[...])
                g_ref[...] = gg
                mm = ADAM_B1 * m_ref[...] + (1.0 - ADAM_B1) * gg
                vv = ADAM_B2 * v_ref[...] + (1.0 - ADAM_B2) * (gg * gg)
                m_hat = mm / (1.0 - ADAM_B1 ** ADAM_STEP)
                v_hat = vv / (1.0 - ADAM_B2 ** ADAM_STEP)
                d_ref[...] = -ADAM_LR * (m_hat / (jnp.sqrt(v_hat) + ADAM_EPS) + ADAM_WD * w_ref[...])
                mo_ref[...] = mm
                vo_ref[...] = vv

            pl.when((s >= start) & (s < start + 2 * nt))(update)

    in_specs, out_specs, out_shape = [], [], []
    for (_, w, *_), (start, nt, tr, cols) in zip(items, plans):
        local = lambda s, start=start, nt=nt: jnp.clip(s - start, 0, 2 * nt - 1)
        whole = pl.BlockSpec((tr, cols), lambda s, c_ref, local=local: (local(s), 0))
        part = pl.BlockSpec((tr, cols), lambda s, c_ref, local=local, nt=nt: (local(s) % nt, 0))
        in_specs += [whole, part, part, whole, whole]
        out_specs += [whole] * 4
        out_shape += [jax.ShapeDtypeStruct(w.shape, F32)] * 4
    outs = pl.pallas_call(
        body,
        name="adamw_group",
        grid_spec=pltpu.PrefetchScalarGridSpec(
            num_scalar_prefetch=1, grid=(n_steps,),
            in_specs=in_specs + [HBM_SPEC] * n_x, out_specs=out_specs + [HBM_SPEC] * n_x,
            scratch_shapes=exchange.scratch),
        out_shape=out_shape + exchange.out_shape,
        compiler_params=_cparams(("arbitrary",)),
    )(c, *[a for item in items for a in item[1:]], *exchange.ins)
    updates = {item[0]: tuple(outs[4 * q:4 * q + 4]) for q, item in enumerate(items)}
    return updates, outs[4 * n_w:]


GATHER_FIRST = ("ffn1_w_in", "ffn1_w_out")
GATHER_PROJ = ("w_in",)
GATHER_LATE = ("w_branch_a", "w_branch_b", "w_out", "ffn2_w_in", "ffn2_w_out")


class _Comm:
    def __init__(self, shards, c_arr, mine_arr):
        self.shards, self.c, self.mine = shards, c_arr, mine_arr
        self.groups, self.halves = {}, {}
        self.by_name = {entry[0]: entry for entry in BIG}

    def small_gather(self, loss, small):
        pad_lanes = lambda a: jnp.concatenate([a, jnp.zeros((1, LANES - a.shape[1]), F32)], axis=1)
        buf = jnp.concatenate([
            small["meta_tokens"].reshape(128, LANES),
            small["ffn1_norm"].reshape(8, LANES), small["mix_norm"].reshape(8, LANES),
            small["ffn2_norm"].reshape(8, LANES), small["final_norm"].reshape(8, LANES),
            loss, pad_lanes(small["b_forget"]), pad_lanes(small["attn_sinks"]),
            jnp.zeros((SMALL_ROWS - 163, LANES), F32)], axis=0)
        return _small_exchange(buf)

    def small_gathered(self, outs):
        self.reduced = _sum_devices(outs[0])

    def gather(self, names):
        padded = (STACKED, (D_MODEL, N_CHIPS * SHARD_PAD_COLS), 1)
        table = [(padded if n == STACKED else self.by_name[n]) + (BF16, True) for n in names]
        return _gather_exchange([self.shards[n] for n in names], table)

    def swap(self, tag, grads):
        entries = [self.by_name[n] for n in grads]
        arrays = list(grads.values())
        self.groups[tag] = (entries, arrays)
        return _halves_exchange(arrays, entries)

    def scatter(self, tag, received):
        entries, arrays = self.groups[tag]
        views = [_halves_view(*entry) for entry in entries]
        partials = [_add_sibling(g.reshape(v[0]), r, self.c, name)
                    for g, v, r, (name, _, _) in zip(arrays, views, received, entries)]
        self.groups[tag] = (entries, partials)
        return _scatter_exchange(partials, entries)

    def received(self, tag, pieces):
        entries, partials = self.groups[tag]
        for p, r, (name, _, axis) in zip(partials, pieces, entries):
            self.halves[name] = _add_chips(p, r, self.mine, name, axis)

    def share(self, names):
        own = [self.halves[n] for n in names]
        return dict(zip(names, zip(own, _share_with_sibling(own))))


def kernel(x, meta_tokens, ffn1_norm, ffn1_w_in, ffn1_w_out, mix_norm, w_in, b_forget, attn_sinks, w_branch_a, w_branch_b, w_out, ffn2_norm, ffn2_w_in, ffn2_w_out, final_norm, loss_target, m_meta_tokens, m_ffn1_norm, m_ffn1_w_in, m_ffn1_w_out, m_mix_norm, m_w_in, m_b_forget, m_attn_sinks, m_w_branch_a, m_w_branch_b, m_w_out, m_ffn2_norm, m_ffn2_w_in, m_ffn2_w_out, m_final_norm, v_meta_tokens, v_ffn1_norm, v_ffn1_w_in, v_ffn1_w_out, v_mix_norm, v_w_in, v_b_forget, v_attn_sinks, v_w_branch_a, v_w_branch_b, v_w_out, v_ffn2_norm, v_ffn2_w_in, v_ffn2_w_out, v_final_norm):
    given = dict(locals())
    names = ["meta_tokens", "ffn1_norm", "ffn1_w_in", "ffn1_w_out", "mix_norm", "w_in", "b_forget", "attn_sinks",
             "w_branch_a", "w_branch_b", "w_out", "ffn2_norm", "ffn2_w_in", "ffn2_w_out", "final_norm"]
    big_names = [n for n, _, _ in BIG]
    cx, cy, cc = _coords()
    c_arr = cc.reshape(1).astype(jnp.int32)
    mine_arr = (2 * cx + cy).reshape(1).astype(jnp.int32)

    by_name = {entry[0]: entry for entry in BIG}
    shards = {n: given[n][0].astype(BF16) for n in GATHER_FIRST}
    table = [by_name[n] + (BF16, True) for n in GATHER_FIRST] + [("meta_tokens", (N_META, D_MODEL), 1, F32, False)]
    first = _gather_exchange([shards[n] for n in GATHER_FIRST] + [meta_tokens], table)
    late_names = [n for n in big_names if n not in GATHER_FIRST]
    casts, (w1i, w1o, _, stream) = _cast_hosting([given[n][0] for n in late_names], first, "gather_first",
                                                 stream=(x, len(GATHER_FIRST)))
    shards.update(zip(late_names, casts))
    comm = _Comm(shards, c_arr, mine_arr)
    norms = (ffn1_norm, mix_norm, ffn2_norm, final_norm.reshape(1, D_MODEL))
    loss, grad_x, small, big = _local_step(stream, loss_target, None, norms, b_forget, attn_sinks, (w1i, w1o), comm)

    swap = comm.swap("ffn1_in", dict(ffn1_w_in=big["ffn1_w_in"]))
    last = comm.scatter("ffn1_in", _run_exchange(swap, "exchange_halves_ffn1_in"))
    early = [n for n in big_names if n not in (STACKED, "ffn1_w_in")]
    grad_halves = comm.share(early + [STACKED])
    flat = lambda a: a.reshape(a.shape[-2], a.shape[-1])
    updates, pieces = _adamw_group([(n, flat(given[n]), *grad_halves[n], flat(given["m_" + n]), flat(given["v_" + n]))
                                    for n in early], c_arr, last)
    comm.received("ffn1_in", pieces)
    grad_halves.update(comm.share(["ffn1_w_in"]))
    grads = {}

    red = comm.reduced
    meta_cols = red[:128].reshape(N_META, D_MODEL)
    grads["meta_tokens"] = lax.dynamic_slice_in_dim(meta_cols, (2 * cx + cy) * (D_MODEL // N_CHIPS),
                                                    D_MODEL // N_CHIPS, axis=1)
    grads["ffn1_norm"] = red[128:136].reshape(1, D_MODEL)
    grads["mix_norm"] = red[136:144].reshape(1, D_MODEL)
    grads["ffn2_norm"] = red[144:152].reshape(1, D_MODEL)
    grads["final_norm"] = red[152:160].reshape(1, D_MODEL)
    loss_out = red[160, 0]
    grads["b_forget"] = red[161:162, :B_HEADS]
    grads["attn_sinks"] = red[162:163, :A_HEADS]

    out_g, out_d, out_m, out_v = [], [], [], []
    for n in names:
        w_full = given[n]
        shape = w_full.shape
        two_d = (lambda a: a.reshape(shape[-2], shape[-1])) if len(shape) >= 2 else (lambda a: a.reshape(1, shape[0]))
        if n == STACKED:
            own, other = grad_halves[n]
            g_nat = jnp.concatenate([jnp.where(cc == 0, own, other), jnp.where(cc == 0, other, own)], axis=0)
            rows = lambda a: a.reshape(1, shape[-2], shape[-1]).transpose(2, 0, 1)
            unrows = lambda a: a.transpose(1, 2, 0)
            g2, d2, m2, v2 = [unrows(a) for a in _adamw(rows(w_full), rows(g_nat), rows(given["m_" + n]),
                                                         rows(given["v_" + n]), copy_g=True)]
        elif n in updates:
            g2, d2, m2, v2 = updates[n]
        elif n in grad_halves:
            own, other = grad_halves[n]
            g2, d2, m2, v2 = _adamw_halves(two_d(w_full), own, other, two_d(given["m_" + n]),
                                           two_d(given["v_" + n]), c_arr, n)
        else:
            g2 = two_d(grads[n])
            d2, m2, v2 = _adamw(two_d(w_full), g2, two_d(given["m_" + n]), two_d(given["v_" + n]))
        out_g.append(g2.reshape(shape))
        out_d.append(d2.reshape(shape))
        out_m.append(m2.reshape(shape))
        out_v.append(v2.reshape(shape))
    return (loss_out, grad_x, *out_g, *out_d, *out_m, *out_v)
```

```python
import jax
import jax.numpy as jnp
from jax import lax
from jax.experimental import pallas as pl
from jax.experimental.pallas import tpu as pltpu

F32 = jnp.float32
BF16 = jnp.bfloat16

D_MODEL = 1024
N_META = 16
BLOCK = 128
LANES = 128
PREFIX = BLOCK
N_PAD = PREFIX - N_META
HEAD_DIM = 64
A_HEADS = 8
A_KV_HEADS = 2
A_GROUP = 4
B_HEADS = 8
B_PAIRS = B_HEADS // 2
A_WIDTH = A_HEADS * HEAD_DIM
A_KV_WIDTH = A_KV_HEADS * HEAD_DIM
B_WIDTH = B_HEADS * HEAD_DIM
W_IN_COLS = A_WIDTH + 2 * A_KV_WIDTH + 3 * B_WIDTH + B_HEADS + 2 * D_MODEL
SRC_KA = A_WIDTH
SRC_VA = SRC_KA + A_KV_WIDTH
SRC_QB = SRC_VA + A_KV_WIDTH
SRC_KB = SRC_QB + B_WIDTH
SRC_VB = SRC_KB + B_WIDTH
SRC_F = SRC_VB + B_WIDTH
SRC_GA = SRC_F + B_HEADS
SRC_GB = SRC_GA + D_MODEL
A_PAD_WIDTH = A_HEADS * LANES
B_PAD_WIDTH = B_HEADS * LANES
F_COLS = LANES
OFF_QA = 0
OFF_KA = SRC_KA
OFF_VA = SRC_VA
OFF_F = OFF_VA + A_KV_WIDTH
OFF_QB = OFF_F + F_COLS
OFF_KB = OFF_QB + B_WIDTH
OFF_VB = OFF_KB + B_WIDTH
OFF_GA = OFF_VB + B_WIDTH
OFF_GB = OFF_GA + D_MODEL
P_COLS = OFF_GB + D_MODEL
P_PIECES = ((0, OFF_QB), (OFF_QB, OFF_GA - OFF_QB), (OFF_GA, P_COLS - OFF_GA))
EPS = 1e-6
NEG = -1e30
SCALE = HEAD_DIM ** -0.5
KEY_BLOCKS = 4

ADAM_LR = 0.001
ADAM_B1 = 0.9
ADAM_B2 = 0.999
ADAM_EPS = 1e-08
ADAM_WD = 0.01
ADAM_STEP = 10

N_CHIPS = 4
N_DEV = 8
VMEM_LIMIT = 56 * 1024 * 1024

NT_DIMS = (((1,), (1,)), ((), ()))
TN_DIMS = (((0,), (0,)), ((), ()))
MESH_ID = pl.DeviceIdType.MESH
HBM_SPEC = pl.BlockSpec(memory_space=pltpu.HBM)
VMEM_SPEC = pl.BlockSpec(memory_space=pltpu.VMEM)


def _tile(n, target, mult=16):
    best = None
    for t in range(mult, min(n, target) + 1, mult):
        if n % t == 0:
            best = t
    return best if best is not None else n


def _cparams(sem):
    return pltpu.CompilerParams(dimension_semantics=sem, vmem_limit_bytes=VMEM_LIMIT)


def _rms_scale(h):
    return lax.rsqrt(jnp.mean(h * h, axis=-1, keepdims=True) + EPS)


def _rms_bwd(dn, h, w):
    r = _rms_scale(h)
    dw = jnp.sum(dn * (h * r), axis=0, keepdims=True)
    z = dn * w
    dh = r * z - h * ((r * r * r) * jnp.mean(z * h, axis=-1, keepdims=True))
    return dh, dw


def _ffn_fwd(h, norm_w, w_in, w_out, exchange=None):
    t, d = h.shape
    f = w_out.shape[0]
    tm = _tile(t, 272)
    tc = _tile(f, 256, 128)
    nj = f // tc
    ni = t // tm
    n_x = len(exchange.ins) if exchange else 0

    def body(*refs):
        h_ref, nw_ref, wi_ref, wo_ref = refs[:4]
        hout_ref, g_ref, u_ref = refs[4 + n_x:7 + n_x]
        a_scr = refs[7 + 2 * n_x]
        i = pl.program_id(0)
        if exchange:
            _host_exchange(exchange, refs[4:4 + n_x], refs[7 + n_x:7 + 2 * n_x], refs[8 + 2 * n_x:],
                           i == 0, i == ni // 3, i == ni - 1, late=i == 2 * ni // 3)
        hh = h_ref[...]
        n = ((hh * _rms_scale(hh)) * nw_ref[...]).astype(BF16)
        for j in range(nj):
            cols = slice(j * tc, (j + 1) * tc)
            g = jnp.dot(n, wi_ref[:, j * tc:(j + 1) * tc], preferred_element_type=F32)
            u = jnp.dot(n, wi_ref[:, f + j * tc:f + (j + 1) * tc], preferred_element_type=F32)
            g_ref[:, cols] = g
            u_ref[:, cols] = u
            a_scr[:, cols] = ((g * jax.nn.sigmoid(g)) * u).astype(BF16)
        hout_ref[...] = hh + 0.5 * jnp.dot(a_scr[...], wo_ref[...], preferred_element_type=F32)

    resident = lambda a: pl.BlockSpec(a.shape, lambda i: (0, 0), pipeline_mode=pl.Buffered(1))
    row = lambda w: pl.BlockSpec((tm, w), lambda i: (i, 0))
    outs = pl.pallas_call(
        body,
        name="ffn_fwd",
        grid=(ni,),
        in_specs=[row(d), pl.BlockSpec((1, d), lambda i: (0, 0)), resident(w_in), resident(w_out)] + [HBM_SPEC] * n_x,
        out_specs=[row(d), row(f), row(f)] + [HBM_SPEC] * n_x,
        out_shape=[
            jax.ShapeDtypeStruct((t, d), F32),
            jax.ShapeDtypeStruct((t, f), F32),
            jax.ShapeDtypeStruct((t, f), F32),
        ] + (exchange.out_shape if exchange else []),
        scratch_shapes=[pltpu.VMEM((tm, f), BF16)] + (exchange.scratch if exchange else []),
        compiler_params=_cparams(("arbitrary",) if exchange else ("parallel",)),
    )(h, norm_w, w_in, w_out, *(exchange.ins if exchange else []))
    return outs[:3], outs[3:]


def _ffn_bwd(dh_out, h, norm_w, g, u, w_in, w_out, exchange=None):
    t, d = h.shape
    f = w_out.shape[0]
    tm = _tile(t, 272)
    tc = _tile(f, 256, 128)
    nj = f // tc
    ni = t // tm
    n_x = len(exchange.ins) if exchange else 0

    def body(*refs):
        dho_ref, h_ref, nw_ref, g_ref, u_ref, wi_ref, wo_ref = refs[:7]
        dhin_ref, n_ref, a_ref, dgu_ref, df_ref, dnw_ref = refs[7 + n_x:13 + n_x]
        i = pl.program_id(0)
        if exchange:
            _host_exchange(exchange, refs[7:7 + n_x], refs[13 + n_x:13 + 2 * n_x], refs[13 + 2 * n_x:],
                           i == 0, i == ni - 1, i == ni - 1)
        hh = h_ref[...]
        nw = nw_ref[...]
        n_ref[...] = ((hh * _rms_scale(hh)) * nw).astype(BF16)
        dho = dho_ref[...]
        df = (0.5 * dho).astype(BF16)
        df_ref[...] = df
        for j in range(nj):
            cols = slice(j * tc, (j + 1) * tc)
            da = lax.dot_general(df, wo_ref[cols, :], NT_DIMS, preferred_element_type=F32)
            gg = g_ref[:, cols]
            uu = u_ref[:, cols]
            sig = jax.nn.sigmoid(gg)
            sl = gg * sig
            a_ref[:, cols] = (sl * uu).astype(BF16)
            dgu_ref[0, :, cols] = ((da * uu) * (sig * (1.0 + gg * (1.0 - sig)))).astype(BF16)
            dgu_ref[1, :, cols] = (da * sl).astype(BF16)
        dn = (lax.dot_general(dgu_ref[0], wi_ref[:, :f], NT_DIMS, preferred_element_type=F32)
              + lax.dot_general(dgu_ref[1], wi_ref[:, f:], NT_DIMS, preferred_element_type=F32))
        dh, dw = _rms_bwd(dn, hh, nw)
        dhin_ref[...] = dho + dh
        dnw_ref[0] = dw

    resident = lambda a: pl.BlockSpec(a.shape, lambda i: (0, 0), pipeline_mode=pl.Buffered(1))
    row = lambda w: pl.BlockSpec((tm, w), lambda i: (i, 0))
    outs = pl.pallas_call(
        body,
        name="ffn_bwd",
        grid=(ni,),
        in_specs=[row(d), row(d), pl.BlockSpec((1, d), lambda i: (0, 0)), row(f), row(f),
                  resident(w_in), resident(w_out)] + [HBM_SPEC] * n_x,
        out_specs=[row(d), row(d), row(f), pl.BlockSpec((2, tm, f), lambda i: (0, i, 0)), row(d),
                   pl.BlockSpec((1, 1, d), lambda i: (i, 0, 0))] + [HBM_SPEC] * n_x,
        out_shape=[
            jax.ShapeDtypeStruct((t, d), F32),
            jax.ShapeDtypeStruct((t, d), BF16),
            jax.ShapeDtypeStruct((t, f), BF16),
            jax.ShapeDtypeStruct((2, t, f), BF16),
            jax.ShapeDtypeStruct((t, d), BF16),
            jax.ShapeDtypeStruct((ni, 1, d), F32),
        ] + (exchange.out_shape if exchange else []),
        scratch_shapes=exchange.scratch if exchange else [],
        compiler_params=_cparams(("arbitrary",) if exchange else ("parallel",)),
    )(dh_out, h, norm_w, g, u, w_in, w_out, *(exchange.ins if exchange else []))
    return outs[:6], outs[6:]


def _tn_matmul(a, b, name, exchange=None, cols=None):
    t, k = a.shape
    split = b.ndim == 3
    n = 2 * b.shape[2] if split else b.shape[1]
    tk = _tile(k, 512, 128)
    tn = _tile(b.shape[-1], 1408, 128)
    per_half = b.shape[-1] // tn
    ni, nj = k // tk, n // tn
    n_x = len(exchange.ins) if exchange else 0
    first, width = cols or (0, tn)

    def body(*refs):
        a_ref, b_ref, o_ref = refs[0], refs[1], refs[2 + n_x]
        if exchange:
            i, j = pl.program_id(0), pl.program_id(1)
            at_end = (i == ni - 1) & (j == nj - 1)
            _host_exchange(exchange, refs[2:2 + n_x], refs[3 + n_x:3 + 2 * n_x], refs[3 + 2 * n_x:],
                           (i == 0) & (j == 0), at_end, at_end)
        o_ref[...] = lax.dot_general(a_ref[...], b_ref[:, first:first + width], TN_DIMS, preferred_element_type=F32)

    if split:
        b_spec = pl.BlockSpec((None, t, tn), lambda i, j: (j // per_half, 0, j % per_half))
    else:
        b_spec = pl.BlockSpec((t, tn), lambda i, j: (0, j))
    outs = pl.pallas_call(
        body,
        name=name,
        grid=(ni, nj),
        in_specs=[pl.BlockSpec((t, tk), lambda i, j: (0, i)), b_spec] + [HBM_SPEC] * n_x,
        out_specs=[pl.BlockSpec((tk, width), lambda i, j: (i, j))] + [HBM_SPEC] * n_x,
        out_shape=[jax.ShapeDtypeStruct((k, nj * width), F32)] + (exchange.out_shape if exchange else []),
        scratch_shapes=exchange.scratch if exchange else [],
        compiler_params=_cparams(("arbitrary", "arbitrary") if exchange else ("parallel", "parallel")),
    )(a, b, *(exchange.ins if exchange else []))
    return (outs[0], outs[1:]) if exchange else outs[0]


A_SLOT = lambda h: h // A_GROUP
B_SLOT = lambda h: h % 2

PROJ_PARTS = (
    (OFF_QA, A_WIDTH, A_PAD_WIDTH, True, A_SLOT), (OFF_KA, A_KV_WIDTH, A_KV_WIDTH, True, None),
    (OFF_VA, A_KV_WIDTH, A_KV_WIDTH, True, None), (OFF_QB, B_WIDTH, B_WIDTH, True, None),
    (OFF_KB, B_WIDTH, B_PAD_WIDTH, True, B_SLOT), (OFF_VB, B_WIDTH, B_PAD_WIDTH, True, B_SLOT),
    (OFF_GA, D_MODEL, D_MODEL, False, None), (OFF_GB, D_MODEL, D_MODEL, False, None), (OFF_F, F_COLS, F_COLS, False, None),
)


def _head_tile(pair, head, slot):
    lane_slot = lax.broadcasted_iota(jnp.int32, pair.shape, 1) // HEAD_DIM
    moved = pair if head % 2 == slot else pltpu.roll(pair, HEAD_DIM, 1)
    return jnp.where(lane_slot == slot, moved, 0.0)


def _proj_fwd(h, norm_w, w_p):
    t, d = h.shape
    tm = _tile(t, 272)

    def body(h_ref, nw_ref, w_ref, u_ref, *part_refs):
        hh = h_ref[...]
        un = ((hh * _rms_scale(hh)) * nw_ref[...]).astype(BF16)
        u_ref[...] = un
        for (off, width, _, _, slot), p_ref in zip(PROJ_PARTS, part_refs):
            if slot is None:
                p_ref[...] = jnp.dot(un, w_ref[:, off:off + width], preferred_element_type=F32).astype(p_ref.dtype)
                continue
            part = jnp.dot(un, w_ref[:, off:off + width], preferred_element_type=F32)
            for pair in range(width // LANES):
                x = part[:, pair * LANES:(pair + 1) * LANES]
                for head in (2 * pair, 2 * pair + 1):
                    p_ref[:, head * LANES:(head + 1) * LANES] = _head_tile(x, head, slot(head)).astype(p_ref.dtype)

    row = lambda w: pl.BlockSpec((tm, w), lambda i: (i, 0))
    return pl.pallas_call(
        body,
        name="proj_fwd",
        grid=(t // tm,),
        in_specs=[row(d), pl.BlockSpec((1, d), lambda i: (0, 0)),
                  pl.BlockSpec(w_p.shape, lambda i: (0, 0), pipeline_mode=pl.Buffered(1))],
        out_specs=[row(d)] + [row(width) for _, _, width, _, _ in PROJ_PARTS],
        out_shape=[jax.ShapeDtypeStruct((t, d), BF16)]
        + [jax.ShapeDtypeStruct((t, width), BF16 if is_bf else F32) for _, _, width, is_bf, _ in PROJ_PARTS],
        compiler_params=_cparams(("parallel",)),
    )(h, norm_w, w_p)


def _proj_bwd(dh_out, h, norm_w, dproj, w_p, exchange=None):
    t, d = h.shape
    tm = _tile(t, 272)
    ni = t // tm
    n_p = len(P_PIECES)
    n_in = 4 + n_p
    n_x = len(exchange.ins) if exchange else 0

    def body(*refs):
        dho_ref, h_ref, nw_ref = refs[:3]
        dp_refs, w_ref = refs[3:3 + n_p], refs[3 + n_p]
        dhin_ref, dnw_ref = refs[n_in + n_x:n_in + 2 + n_x]
        if exchange:
            i = pl.program_id(0)
            _host_exchange(exchange, refs[n_in:n_in + n_x], refs[n_in + 2 + n_x:n_in + 2 + 2 * n_x],
                           refs[n_in + 2 + 2 * n_x:], i == 0, i == ni - 1, i == ni - 1)
        dn = None
        for dp_ref, (off, width) in zip(dp_refs, P_PIECES):
            part = lax.dot_general(dp_ref[...], w_ref[:, off:off + width], NT_DIMS, preferred_element_type=F32)
            dn = part if dn is None else dn + part
        dh, dw = _rms_bwd(dn, h_ref[...], nw_ref[...])
        dhin_ref[...] = dho_ref[...] + dh
        dnw_ref[0] = dw

    row = lambda w: pl.BlockSpec((tm, w), lambda i: (i, 0))
    outs = pl.pallas_call(
        body,
        name="proj_bwd",
        grid=(ni,),
        in_specs=[row(d), row(d), pl.BlockSpec((1, d), lambda i: (0, 0))] + [row(width) for _, width in P_PIECES]
        + [pl.BlockSpec(w_p.shape, lambda i: (0, 0), pipeline_mode=pl.Buffered(1))] + [HBM_SPEC] * n_x,
        out_specs=[row(d), pl.BlockSpec((1, 1, d), lambda i: (i, 0, 0))] + [HBM_SPEC] * n_x,
        out_shape=[jax.ShapeDtypeStruct((t, d), F32), jax.ShapeDtypeStruct((ni, 1, d), F32)]
        + (exchange.out_shape if exchange else []),
        scratch_shapes=exchange.scratch if exchange else [],
        compiler_params=_cparams(("arbitrary",) if exchange else ("parallel",)),
    )(dh_out, h, norm_w, *dproj, w_p, *(exchange.ins if exchange else []))
    return outs[:2], outs[2:]


def _merge_fwd(h, oa, ob, ga, gb, wa, wb, wo):
    t, d = h.shape
    tm = _tile(t, 544)

    def body(h_ref, oa_ref, ob_ref, ga_ref, gb_ref, wa_ref, wb_ref, wo_ref, hout_ref, mix_ref):
        ya = jnp.dot(oa_ref[...], wa_ref[...], preferred_element_type=F32)
        yb = jnp.dot(ob_ref[...], wb_ref[...], preferred_element_type=F32)
        mixed = (jax.nn.sigmoid(ga_ref[...]) * ya + jax.nn.sigmoid(gb_ref[...]) * yb).astype(BF16)
        mix_ref[...] = mixed
        hout_ref[...] = h_ref[...] + jnp.dot(mixed, wo_ref[...], preferred_element_type=F32)

    row = lambda w: pl.BlockSpec((tm, w), lambda i: (i, 0))
    full = lambda a: pl.BlockSpec(a.shape, lambda i: (0, 0))
    return pl.pallas_call(
        body,
        name="merge_fwd",
        grid=(t // tm,),
        in_specs=[row(d), row(oa.shape[1]), row(ob.shape[1]), row(d), row(d), full(wa), full(wb), full(wo)],
        out_specs=[row(d), row(d)],
        out_shape=[jax.ShapeDtypeStruct((t, d), F32), jax.ShapeDtypeStruct((t, d), BF16)],
        compiler_params=_cparams(("parallel",)),
    )(h, oa, ob, ga, gb, wa, wb, wo)


def _merge_bwd(dh, oa, ob, ga, gb, wa, wb, wo, exchange=None):
    t, d = dh.shape
    tm = _tile(t, 544)
    ni = t // tm
    n_x = len(exchange.ins) if exchange else 0

    def body(*refs):
        dh_ref, oa_ref, ob_ref, ga_ref, gb_ref, wa_ref, wb_ref, wo_ref = refs[:8]
        dya_ref, dyb_ref, doa_ref, dob_ref, dg_ref, dhb_ref = refs[8 + n_x:14 + n_x]
        if exchange:
            i = pl.program_id(0)
            _host_exchange(exchange, refs[8:8 + n_x], refs[14 + n_x:14 + 2 * n_x], refs[14 + 2 * n_x:],
                           i == 0, i == ni - 1, i == ni - 1)
        dhb = dh_ref[...].astype(BF16)
        dhb_ref[...] = dhb
        dmix = lax.dot_general(dhb, wo_ref[...], NT_DIMS, preferred_element_type=F32)
        for branch, (o_ref, g_ref, w_ref, dy_ref, do_ref) in enumerate((
                (oa_ref, ga_ref, wa_ref, dya_ref, doa_ref),
                (ob_ref, gb_ref, wb_ref, dyb_ref, dob_ref))):
            y = jnp.dot(o_ref[...], w_ref[...], preferred_element_type=F32)
            s = jax.nn.sigmoid(g_ref[...])
            dy = (dmix * s).astype(BF16)
            dy_ref[...] = dy
            dg_ref[:, branch * d:(branch + 1) * d] = ((dmix * y) * (s * (1.0 - s))).astype(BF16)
            do_ref[...] = lax.dot_general(dy, w_ref[...], NT_DIMS, preferred_element_type=F32).astype(BF16)

    row = lambda w: pl.BlockSpec((tm, w), lambda i: (i, 0))
    full = lambda a: pl.BlockSpec(a.shape, lambda i: (0, 0))
    wa_w, wb_w = oa.shape[1], ob.shape[1]
    outs = pl.pallas_call(
        body,
        name="merge_bwd",
        grid=(ni,),
        in_specs=[row(d), row(wa_w), row(wb_w), row(d), row(d), full(wa), full(wb), full(wo)] + [HBM_SPEC] * n_x,
        out_specs=[row(d), row(d), row(wa_w), row(wb_w), row(2 * d), row(d)] + [HBM_SPEC] * n_x,
        out_shape=[
            jax.ShapeDtypeStruct((t, d), BF16), jax.ShapeDtypeStruct((t, d), BF16),
            jax.ShapeDtypeStruct((t, wa_w), BF16), jax.ShapeDtypeStruct((t, wb_w), BF16),
            jax.ShapeDtypeStruct((t, 2 * d), BF16), jax.ShapeDtypeStruct((t, d), BF16),
        ] + (exchange.out_shape if exchange else []),
        scratch_shapes=exchange.scratch if exchange else [],
        compiler_params=_cparams(("arbitrary",) if exchange else ("parallel",)),
    )(dh, oa, ob, ga, gb, wa, wb, wo, *(exchange.ins if exchange else []))
    return outs[:6], outs[6:]


def _tri_dot(tri, x):
    hi = x.astype(BF16)
    r1 = x - hi.astype(F32)
    mid = r1.astype(BF16)
    lo = (r1 - mid.astype(F32)).astype(BF16)
    return (jnp.dot(tri, hi, preferred_element_type=F32)
            + jnp.dot(tri, mid, preferred_element_type=F32)
            + jnp.dot(tri, lo, preferred_element_type=F32))


def _forget_cumsum(f_logit, b_pad, nb):
    t, w = f_logit.shape
    bsz = t // (nb * BLOCK)

    def body(f_ref, b_ref, c_ref, carry):
        @pl.when(pl.program_id(0) == 0)
        def _():
            carry[...] = jnp.zeros_like(carry)

        rows = lax.broadcasted_iota(jnp.int32, (BLOCK, BLOCK), 0)
        cols = lax.broadcasted_iota(jnp.int32, (BLOCK, BLOCK), 1)
        tri = (cols <= rows).astype(BF16)
        for b in range(bsz):
            x = jax.nn.log_sigmoid(f_ref[b] + b_ref[...])
            c = _tri_dot(tri, x) + carry[b]
            c_ref[b] = c
            carry[b] = c[BLOCK - 1:BLOCK, :]

    block = pl.BlockSpec((bsz, BLOCK, w), lambda n: (0, n, 0))
    return pl.pallas_call(
        body,
        name="forget_cumsum",
        grid=(nb,),
        in_specs=[block, pl.BlockSpec((1, w), lambda n: (0, 0))],
        out_specs=block,
        out_shape=jax.ShapeDtypeStruct((bsz, nb * BLOCK, w), F32),
        scratch_shapes=[pltpu.VMEM((bsz, 1, w), F32)],
        compiler_params=_cparams(("arbitrary",)),
    )(f_logit.reshape(bsz, nb * BLOCK, w), b_pad).reshape(t, w)


def _forget_cumsum_bwd(dc, f_logit, b_pad, nb):
    t, w = f_logit.shape
    bsz = t // (nb * BLOCK)

    def body(dc_ref, f_ref, b_ref, df_ref, db_ref, carry):
        @pl.when(pl.program_id(0) == 0)
        def _():
            carry[...] = jnp.zeros_like(carry)
            db_ref[...] = jnp.zeros_like(db_ref)

        rows = lax.broadcasted_iota(jnp.int32, (BLOCK, BLOCK), 0)
        cols = lax.broadcasted_iota(jnp.int32, (BLOCK, BLOCK), 1)
        tri = (cols >= rows).astype(BF16)
        for b in range(bsz):
            dlf = _tri_dot(tri, dc_ref[b]) + carry[b]
            carry[b] = dlf[0:1, :]
            df = dlf * jax.nn.sigmoid(-(f_ref[b] + b_ref[...]))
            df_ref[b] = df.astype(BF16)
            db_ref[b] += jnp.sum(df, axis=0, keepdims=True)

    l = nb * BLOCK
    rev = pl.BlockSpec((bsz, BLOCK, w), lambda n: (0, nb - 1 - n, 0))
    df, db = pl.pallas_call(
        body,
        name="forget_cumsum_bwd",
        grid=(nb,),
        in_specs=[rev, rev, pl.BlockSpec((1, w), lambda n: (0, 0))],
        out_specs=[rev, pl.BlockSpec((bsz, 1, w), lambda n: (0, 0, 0))],
        out_shape=[jax.ShapeDtypeStruct((bsz, l, w), BF16), jax.ShapeDtypeStruct((bsz, 1, w), F32)],
        scratch_shapes=[pltpu.VMEM((bsz, 1, w), F32)],
        compiler_params=_cparams(("arbitrary",)),
    )(dc.reshape(bsz, l, w), f_logit.reshape(bsz, l, w), b_pad)
    return df.reshape(t, w), db


GROUP_ROWS = A_GROUP * BLOCK


def _stack_heads(ref, g):
    return jnp.concatenate([ref[:, (A_GROUP * g + i) * LANES:(A_GROUP * g + i + 1) * LANES] for i in range(A_GROUP)],
                           axis=0)


def _unstack_heads(ref, g, x):
    for i in range(A_GROUP):
        ref[:, (A_GROUP * g + i) * LANES:(A_GROUP * g + i + 1) * LANES] = x[i * BLOCK:(i + 1) * BLOCK].astype(ref.dtype)


def _swa_logits(qk, slope, n):
    qi = lax.broadcasted_iota(jnp.int32, (GROUP_ROWS, BLOCK), 0) & (BLOCK - 1)
    kj = lax.broadcasted_iota(jnp.int32, (GROUP_ROWS, BLOCK), 1)
    s_all = qk * SCALE
    out = []
    for i, (dist, ok) in enumerate((
            (n * BLOCK + qi - kj, (kj >= N_PAD) & (n * BLOCK + qi - kj >= 0)),
            (BLOCK + qi - kj, (kj > qi) & (n >= 2)),
            (qi - kj, (kj <= qi) & (n >= 1)))):
        s = s_all[:, i * BLOCK:(i + 1) * BLOCK] - slope * dist.astype(F32)
        out.append(jnp.where(ok, s, NEG))
    return out


def _three_blocks(m_ref, p_ref, c_ref):
    return jnp.concatenate([m_ref[...], p_ref[...], c_ref[...]], axis=0)


def _swa_specs(bsz, nb):
    qspec = pl.BlockSpec((bsz, BLOCK, A_PAD_WIDTH), lambda n: (0, n, 0))
    kv_m = pl.BlockSpec((bsz, BLOCK, LANES), lambda n: (0, 0, 0))
    kv_p = pl.BlockSpec((bsz, BLOCK, LANES), lambda n: (0, jnp.maximum(n - 1, 0), 0))
    kv_c = pl.BlockSpec((bsz, BLOCK, LANES), lambda n: (0, n, 0))
    rowspec = pl.BlockSpec((A_KV_HEADS, GROUP_ROWS, 1), lambda n: (0, 0, 0))
    lsespec = pl.BlockSpec((bsz, 1, A_KV_HEADS, GROUP_ROWS, 1), lambda n: (0, n, 0, 0, 0))
    return qspec, kv_m, kv_p, kv_c, rowspec, lsespec


def _swa_fwd(q, k, v, sink_rows, slope_rows, nb):
    t = q.shape[0]
    l = nb * BLOCK
    bsz = t // l

    def body(q_ref, km_ref, kp_ref, kc_ref, vm_ref, vp_ref, vc_ref, sink_ref, slope_ref, o_ref, lse_ref):
        n = pl.program_id(0)
        lane_group = lax.broadcasted_iota(jnp.int32, (GROUP_ROWS, LANES), 1) // HEAD_DIM
        products = {}
        for b in range(bsz):
            keys = _three_blocks(km_ref.at[b], kp_ref.at[b], kc_ref.at[b])
            for g in range(A_KV_HEADS):
                products[b, g] = lax.dot_general(_stack_heads(q_ref.at[b], g), keys, NT_DIMS,
                                                 preferred_element_type=F32)
        for b in range(bsz):
            values = _three_blocks(vm_ref.at[b], vp_ref.at[b], vc_ref.at[b])
            for g in range(A_KV_HEADS):
                sink = sink_ref[g]
                s_m, s_p, s_c = _swa_logits(products[b, g], slope_ref[g], n)
                m = jnp.maximum(jnp.max(jnp.maximum(jnp.maximum(s_m, s_p), s_c), axis=-1, keepdims=True), sink)
                m_wide = jnp.broadcast_to(m, (GROUP_ROWS, BLOCK))
                e_m = jnp.exp(s_m - m_wide)
                e_p = jnp.exp(s_p - m_wide)
                e_c = jnp.exp(s_c - m_wide)
                z = jnp.sum((e_m + e_p) + e_c, axis=-1, keepdims=True) + jnp.exp(sink - m)
                inv = jnp.broadcast_to(1.0 / z, (GROUP_ROWS, BLOCK))
                probs = jnp.concatenate([(e_m * inv).astype(BF16), (e_p * inv).astype(BF16),
                                         (e_c * inv).astype(BF16)], axis=1)
                o = jnp.dot(probs, values, preferred_element_type=F32)
                _unstack_heads(o_ref.at[b], g, jnp.where(lane_group == g, o, 0.0))
                lse_ref[b, 0, g] = m + jnp.log(z)

    qspec, kv_m, kv_p, kv_c, rowspec, lsespec = _swa_specs(bsz, nb)
    by_example = lambda a: a.reshape(bsz, l, a.shape[1])
    q3, k3, v3 = by_example(q), by_example(k), by_example(v)
    o, lse = pl.pallas_call(
        body,
        name="swa_fwd",
        grid=(nb,),
        in_specs=[qspec, kv_m, kv_p, kv_c, kv_m, kv_p, kv_c, rowspec, rowspec],
        out_specs=[qspec, lsespec],
        out_shape=[jax.ShapeDtypeStruct((bsz, l, A_PAD_WIDTH), BF16),
                   jax.ShapeDtypeStruct((bsz, nb, A_KV_HEADS, GROUP_ROWS, 1), F32)],
        compiler_params=_cparams(("arbitrary",)),
    )(q3, k3, k3, k3, v3, v3, v3, sink_rows, slope_rows)
    return o.reshape(t, A_PAD_WIDTH), lse


def _swa_bwd(q, k, v, do, lse, sink_rows, slope_rows, nb):
    t = q.shape[0]
    l = nb * BLOCK
    bsz = t // l

    def body(q_ref, km_ref, kp_ref, kc_ref, vm_ref, vp_ref, vc_ref, do_ref, lse_ref, sink_ref, slope_ref,
             dq_ref, dk_ref, dv_ref, dsink_ref, dk_acc, dv_acc):
        n = pl.program_id(0)

        @pl.when(n == 0)
        def _():
            dk_acc[...] = jnp.zeros_like(dk_acc)
            dv_acc[...] = jnp.zeros_like(dv_acc)
            dsink_ref[...] = jnp.zeros_like(dsink_ref)

        first_half = lax.broadcasted_iota(jnp.int32, (BLOCK, LANES), 1) < HEAD_DIM
        prev = jnp.maximum(n - 1, 0)
        products = {}
        for b in range(bsz):
            keys = _three_blocks(km_ref.at[b], kp_ref.at[b], kc_ref.at[b])
            values = _three_blocks(vm_ref.at[b], vp_ref.at[b], vc_ref.at[b])
            for g in range(A_KV_HEADS):
                products[b, g] = (
                    lax.dot_general(_stack_heads(q_ref.at[b], g), keys, NT_DIMS, preferred_element_type=F32),
                    lax.dot_general(_stack_heads(do_ref.at[b], g), values, NT_DIMS, preferred_element_type=F32))
        for b in range(bsz):
            keys = _three_blocks(km_ref.at[b], kp_ref.at[b], kc_ref.at[b])
            for g in range(A_KV_HEADS):
                qq = _stack_heads(q_ref.at[b], g)
                dob = _stack_heads(do_ref.at[b], g)
                lse_g = lse_ref[b, 0, g]
                lse_wide = jnp.broadcast_to(lse_g, (GROUP_ROWS, BLOCK))
                qk, dp_all = products[b, g]
                probs = [jnp.exp(s - lse_wide) for s in _swa_logits(qk, slope_ref[g], n)]
                dps = [dp_all[:, i * BLOCK:(i + 1) * BLOCK] for i in range(3)]
                delta = jnp.sum((probs[0] * dps[0] + probs[1] * dps[1]) + probs[2] * dps[2], axis=-1, keepdims=True)
                delta_wide = jnp.broadcast_to(delta, (GROUP_ROWS, BLOCK))
                ds = jnp.concatenate([(p * (dp - delta_wide)).astype(BF16) for p, dp in zip(probs, dps)], axis=1)
                pb = jnp.concatenate([p.astype(BF16) for p in probs], axis=1)
                dq = jnp.dot(ds, keys, preferred_element_type=F32) * SCALE
                dk_all = lax.dot_general(ds, qq, TN_DIMS, preferred_element_type=F32) * SCALE
                dv_all = lax.dot_general(pb, dob, TN_DIMS, preferred_element_type=F32)
                for i, start in enumerate((0, prev * BLOCK, n * BLOCK)):
                    rows = pl.ds(pl.multiple_of(start, BLOCK), BLOCK)
                    dk_acc[b, rows, :] += dk_all[i * BLOCK:(i + 1) * BLOCK]
                    dv_acc[b, rows, :] += dv_all[i * BLOCK:(i + 1) * BLOCK]
                for pair in range(A_GROUP // 2):
                    even = dq[2 * pair * BLOCK:(2 * pair + 1) * BLOCK]
                    odd = dq[(2 * pair + 1) * BLOCK:(2 * pair + 2) * BLOCK]
                    left = even if g == 0 else pltpu.roll(even, HEAD_DIM, 1)
                    right = pltpu.roll(odd, HEAD_DIM, 1) if g == 0 else odd
                    tile = (A_GROUP // 2) * g + pair
                    dq_ref[b, :, tile * LANES:(tile + 1) * LANES] = jnp.where(first_half, left, right).astype(BF16)
                dsink_ref[b, g] += -(jnp.exp(sink_ref[g] - lse_g) * delta)

        @pl.when(n == nb - 1)
        def _():
            dk_ref[...] = dk_acc[...].astype(BF16)
            dv_ref[...] = dv_acc[...].astype(BF16)

    qspec, kv_m, kv_p, kv_c, rowspec, lsespec = _swa_specs(bsz, nb)
    kv_all = pl.BlockSpec((bsz, l, LANES), lambda n: (0, 0, 0))
    by_example = lambda a: a.reshape(bsz, l, a.shape[1])
    q3, k3, v3 = by_example(q), by_example(k), by_example(v)
    dq, dk, dv, dsink = pl.pallas_call(
        body,
        name="swa_bwd",
        grid=(nb,),
        in_specs=[qspec, kv_m, kv_p, kv_c, kv_m, kv_p, kv_c, qspec, lsespec, rowspec, rowspec],
        out_specs=[pl.BlockSpec((bsz, BLOCK, A_WIDTH), lambda n: (0, n, 0)), kv_all, kv_all,
                   pl.BlockSpec((bsz, A_KV_HEADS, GROUP_ROWS, 1), lambda n: (0, 0, 0, 0))],
        out_shape=[jax.ShapeDtypeStruct((bsz, l, A_WIDTH), BF16),
                   jax.ShapeDtypeStruct((bsz, l, LANES), BF16),
                   jax.ShapeDtypeStruct((bsz, l, LANES), BF16),
                   jax.ShapeDtypeStruct((bsz, A_KV_HEADS, GROUP_ROWS, 1), F32)],
        scratch_shapes=[pltpu.VMEM((bsz, l, LANES), F32), pltpu.VMEM((bsz, l, LANES), F32)],
        compiler_params=_cparams(("arbitrary",)),
    )(q3, k3, k3, k3, v3, v3, v3, by_example(do), lse, sink_rows, slope_rows)
    return dq.reshape(t, A_WIDTH), dk.reshape(t, LANES), dv.reshape(t, LANES), dsink


CHUNK = KEY_BLOCKS * BLOCK


def _fox_chunk(qb, ci):
    sb = jnp.maximum(jnp.minimum(KEY_BLOCKS * ci, qb + 1 - KEY_BLOCKS), 0)
    lo = jnp.maximum(ci * CHUNK, N_PAD)
    return sb, lo, pl.ds(pl.multiple_of(sb * BLOCK, BLOCK), CHUNK)


def _fox_logits(s_ref, cr_ref, e, j, sb, lo, qb):
    lane = lax.broadcasted_iota(jnp.int32, (BLOCK, BLOCK), 1)
    ahead = lane - lax.broadcasted_iota(jnp.int32, (BLOCK, BLOCK), 0)
    first = (sb + j) * BLOCK
    s = s_ref[e, :, j * BLOCK:(j + 1) * BLOCK] - cr_ref[e, sb + j]
    return jnp.where((ahead <= qb * BLOCK - first) & (lane >= lo - first), s, NEG)


FOX_PAIRS = 4
FOX_HEADS = 2 * FOX_PAIRS
FOX_STEPS = B_PAIRS // FOX_PAIRS


def _fox_specs(nb):
    l = nb * BLOCK
    q_spec = pl.BlockSpec((BLOCK, FOX_PAIRS * LANES), lambda b, p, i: (b * nb + i, p))
    kv_spec = pl.BlockSpec((l, FOX_HEADS * LANES), lambda b, p, i: (b, p))
    cc_spec = pl.BlockSpec((FOX_HEADS, BLOCK, 1), lambda b, p, i: (b * FOX_STEPS + p, i, 0))
    cr_spec = pl.BlockSpec((FOX_HEADS, nb, 1, BLOCK), lambda b, p, i: (b * FOX_STEPS + p, 0, 0, 0))
    return q_spec, kv_spec, cc_spec, cr_spec


def _fox_fwd(q, k, v, c_row, nb, exchange=None):
    t = q.shape[0]
    bsz = t // (nb * BLOCK)
    assert nb >= KEY_BLOCKS

    n_x = len(exchange.ins) if exchange else 0

    def body(*refs):
        q_ref, k_ref, v_ref, cr_ref = refs[:4]
        o_ref, ox_ref, lse_ref = refs[4 + n_x:7 + n_x]
        s_scr, hi_scr, lo_scr = refs[7 + 2 * n_x:10 + 2 * n_x]
        qb = pl.program_id(2)
        if exchange:
            first = (pl.program_id(0) == 0) & (pl.program_id(1) == 0)
            last = (pl.program_id(0) == bsz - 1) & (pl.program_id(1) == FOX_STEPS - 1)
            _host_exchange(exchange, refs[4:4 + n_x], refs[7 + n_x:7 + 2 * n_x], refs[10 + 2 * n_x:],
                           first & (qb == 0), first & (qb == 2 * nb // 3), last & (qb == nb - 1),
                           late=last & (qb == 0))
        qs = [q_ref[:, a * LANES:(a + 1) * LANES] * SCALE for a in range(FOX_PAIRS)]
        first_half = lax.broadcasted_iota(jnp.int32, (BLOCK, LANES), 1) < HEAD_DIM

        def step(ci, carry):
            stats, accs = carry[:2 * FOX_HEADS], carry[2 * FOX_HEADS:]
            sb, lo, krows = _fox_chunk(qb, ci)
            for e in range(FOX_HEADS):
                s_scr[e] = lax.dot_general(qs[e // 2], k_ref[krows, e * LANES:(e + 1) * LANES], NT_DIMS,
                                           preferred_element_type=F32)
            new_stats, new_accs = [], []
            for a in range(FOX_PAIRS):
                alphas = []
                pv = jnp.zeros((BLOCK, LANES), F32)
                pv_lo = jnp.zeros((BLOCK, LANES), F32)
                for e in (2 * a, 2 * a + 1):
                    m, z = stats[2 * e], stats[2 * e + 1]
                    tile = slice(e * LANES, (e + 1) * LANES)
                    top = None
                    for j in range(KEY_BLOCKS):
                        s = _fox_logits(s_scr, cr_ref, e, j, sb, lo, qb)
                        s_scr[e, :, j * BLOCK:(j + 1) * BLOCK] = s
                        top = s if top is None else jnp.maximum(top, s)
                    m_new = jnp.maximum(m, jnp.max(top, axis=-1, keepdims=True))
                    alpha = jnp.exp(m - m_new)
                    m_wide = jnp.broadcast_to(m_new, (BLOCK, BLOCK))
                    total = None
                    for j in range(KEY_BLOCKS):
                        cols = slice(j * BLOCK, (j + 1) * BLOCK)
                        p = jnp.exp(s_scr[e, :, cols] - m_wide)
                        total = p if total is None else total + p
                        hi = p.astype(BF16)
                        hi_scr[e, :, cols] = hi
                        lo_scr[e, :, cols] = (p - hi.astype(F32)).astype(BF16)
                    z = alpha * z + jnp.sum(total, axis=-1, keepdims=True)
                    vv = v_ref[krows, tile]
                    pv = pv + jnp.dot(hi_scr[e], vv, preferred_element_type=F32)
                    pv_lo = pv_lo + jnp.dot(lo_scr[e], vv, preferred_element_type=F32)
                    new_stats += [m_new, z]
                    alphas.append(alpha)
                alpha = jnp.where(first_half, alphas[0], alphas[1])
                new_accs += [alpha * accs[2 * a] + pv, alpha * accs[2 * a + 1] + pv_lo]
            return (*new_stats, *new_accs)

        col = lambda val: jnp.full((BLOCK, 1), val, F32)
        done = lax.fori_loop(
            0, (qb + KEY_BLOCKS) // KEY_BLOCKS, step,
            (col(NEG), col(0.0)) * FOX_HEADS + (jnp.zeros((BLOCK, LANES), F32),) * (2 * FOX_PAIRS))
        for a in range(FOX_PAIRS):
            m0, z0, m1, z1 = done[4 * a:4 * a + 4]
            acc, acc_lo = done[2 * FOX_HEADS + 2 * a:2 * FOX_HEADS + 2 * a + 2]
            inv = 1.0 / jnp.where(first_half, z0, z1)
            tile = slice(a * LANES, (a + 1) * LANES)
            o_ref[:, tile] = (acc * inv).astype(BF16)
            ox_ref[:, tile] = (acc + acc_lo) * inv
            lse_ref[2 * a] = m0 + jnp.log(z0)
            lse_ref[2 * a + 1] = m1 + jnp.log(z1)

    q_spec, kv_spec, cc_spec, cr_spec = _fox_specs(nb)
    outs = pl.pallas_call(
        body,
        name="fox_fwd",
        grid=(bsz, FOX_STEPS, nb),
        in_specs=[q_spec, kv_spec, kv_spec, cr_spec] + [HBM_SPEC] * n_x,
        out_specs=[q_spec, q_spec, cc_spec] + [HBM_SPEC] * n_x,
        out_shape=[jax.ShapeDtypeStruct((t, B_WIDTH), BF16), jax.ShapeDtypeStruct((t, B_WIDTH), F32),
                   jax.ShapeDtypeStruct((bsz * B_HEADS, nb * BLOCK, 1), F32)] + (exchange.out_shape if exchange else []),
        scratch_shapes=[pltpu.VMEM((FOX_HEADS, BLOCK, CHUNK), F32), pltpu.VMEM((FOX_HEADS, BLOCK, CHUNK), BF16),
                        pltpu.VMEM((FOX_HEADS, BLOCK, CHUNK), BF16)] + (exchange.scratch if exchange else []),
        compiler_params=_cparams(("arbitrary",) * 3 if exchange else ("parallel", "parallel", "arbitrary")),
    )(q, k, v, c_row, *(exchange.ins if exchange else []))
    return outs[:3], outs[3:]


def _fox_bwd(q, k, v, o_exact, do, lse, c_row, nb, exchange=None):
    t = q.shape[0]
    l = nb * BLOCK
    bsz = t // l

    n_x = len(exchange.ins) if exchange else 0

    def body(*refs):
        q_ref, k_ref, v_ref, ox_ref, do_ref, lse_ref, cr_ref = refs[:7]
        dq_ref, dk_ref, dv_ref, dc_ref = refs[7 + n_x:11 + n_x]
        dk_acc, dv_acc, s_scr, dp_scr, p_scr, ds_scr, dq_scr = refs[11 + 2 * n_x:18 + 2 * n_x]
        qb = pl.program_id(2)
        if exchange:
            first = (pl.program_id(0) == 0) & (pl.program_id(1) == 0)
            last = (pl.program_id(0) == bsz - 1) & (pl.program_id(1) == FOX_STEPS - 1)
            _host_exchange(exchange, refs[7:7 + n_x], refs[11 + n_x:11 + 2 * n_x], refs[18 + 2 * n_x:],
                           first & (qb == 0), last & (qb == 0), last & (qb == nb - 1))

        @pl.when(qb == 0)
        def _():
            dk_acc[...] = jnp.zeros_like(dk_acc)
            dv_acc[...] = jnp.zeros_like(dv_acc)
            dc_ref[...] = jnp.zeros_like(dc_ref)

        top_half = lax.broadcasted_iota(jnp.int32, (LANES, BLOCK), 0) < HEAD_DIM
        pair_t = lambda x: jnp.concatenate([jnp.where(top_half, x.T, 0), jnp.where(top_half, 0, x.T)], axis=1)
        first_half = lax.broadcasted_iota(jnp.int32, (BLOCK, LANES), 1) < HEAD_DIM
        wide = lambda col: jnp.broadcast_to(col, (BLOCK, BLOCK))
        qs, dobs, qs_t, dob_t, deltas = [], [], [], [], []
        for a in range(FOX_PAIRS):
            tile = slice(a * LANES, (a + 1) * LANES)
            qs.append(q_ref[:, tile] * SCALE)
            dobs.append(do_ref[:, tile])
            qs_t.append(pair_t(qs[a]))
            dob_t.append(pair_t(dobs[a]))
            weighted = dobs[a].astype(F32) * ox_ref[:, tile]
            deltas += [wide(jnp.sum(jnp.where(first_half, weighted, 0.0), axis=-1, keepdims=True)),
                       wide(jnp.sum(jnp.where(first_half, 0.0, weighted), axis=-1, keepdims=True))]
        lses = [wide(lse_ref[e]) for e in range(FOX_HEADS)]

        dq_scr[...] = jnp.zeros(dq_scr.shape, F32)

        def step(ci, carry):
            sb, lo, krows = _fox_chunk(qb, ci)
            for e in range(FOX_HEADS):
                tile = slice(e * LANES, (e + 1) * LANES)
                s_scr[e] = lax.dot_general(qs[e // 2], k_ref[krows, tile], NT_DIMS, preferred_element_type=F32)
                dp_scr[e] = lax.dot_general(dobs[e // 2], v_ref[krows, tile], NT_DIMS, preferred_element_type=F32)
            for a in range(FOX_PAIRS):
                for e in (2 * a, 2 * a + 1):
                    tile = slice(e * LANES, (e + 1) * LANES)
                    kk = k_ref[krows, tile]
                    for j in range(KEY_BLOCKS):
                        cols = slice(j * BLOCK, (j + 1) * BLOCK)
                        p = jnp.exp(_fox_logits(s_scr, cr_ref, e, j, sb, lo, qb) - lses[e])
                        ds = p * (dp_scr[e, :, cols] - deltas[e])
                        dc_ref[e, sb + j] -= jnp.sum(ds, axis=0, keepdims=True)
                        p_scr[e, :, cols] = p.astype(BF16)
                        ds_scr[e, :, cols] = ds.astype(BF16)
                    dq_scr[a] += jnp.dot(ds_scr[e], kk, preferred_element_type=F32)
                both = slice(2 * a, 2 * a + 2)
                dk_t = jnp.dot(qs_t[a], ds_scr[both].reshape(2 * BLOCK, CHUNK), preferred_element_type=F32)
                dv_t = jnp.dot(dob_t[a], p_scr[both].reshape(2 * BLOCK, CHUNK), preferred_element_type=F32)
                for j in range(KEY_BLOCKS):
                    cols = slice(j * BLOCK, (j + 1) * BLOCK)
                    dk_acc[a * nb + sb + j] += dk_t[:, cols]
                    dv_acc[a * nb + sb + j] += dv_t[:, cols]
            return carry

        lax.fori_loop(0, (qb + KEY_BLOCKS) // KEY_BLOCKS, step, 0)
        for a in range(FOX_PAIRS):
            dq_ref[:, a * LANES:(a + 1) * LANES] = (dq_scr[a] * SCALE).astype(BF16)

        @pl.when(qb == nb - 1)
        def _():
            for a in range(FOX_PAIRS):
                for kb in range(nb):
                    rows = slice(kb * BLOCK, (kb + 1) * BLOCK)
                    for acc, out_ref in ((dk_acc, dk_ref), (dv_acc, dv_ref)):
                        out_ref[rows, a * LANES:(a + 1) * LANES] = acc[a * nb + kb].T.astype(BF16)

    q_spec, kv_spec, cc_spec, cr_spec = _fox_specs(nb)
    dkv_spec = pl.BlockSpec((l, FOX_PAIRS * LANES), lambda b, p, i: (b, p))
    outs = pl.pallas_call(
        body,
        name="fox_bwd",
        grid=(bsz, FOX_STEPS, nb),
        in_specs=[q_spec, kv_spec, kv_spec, q_spec, q_spec, cc_spec, cr_spec] + [HBM_SPEC] * n_x,
        out_specs=[q_spec, dkv_spec, dkv_spec, cr_spec] + [HBM_SPEC] * n_x,
        out_shape=[jax.ShapeDtypeStruct((t, B_WIDTH), BF16), jax.ShapeDtypeStruct((t, B_WIDTH), BF16),
                   jax.ShapeDtypeStruct((t, B_WIDTH), BF16),
                   jax.ShapeDtypeStruct((bsz * B_HEADS, nb, 1, BLOCK), F32)] + (exchange.out_shape if exchange else []),
        scratch_shapes=[pltpu.VMEM((FOX_PAIRS * nb, LANES, BLOCK), F32), pltpu.VMEM((FOX_PAIRS * nb, LANES, BLOCK), F32),
                        pltpu.VMEM((FOX_HEADS, BLOCK, CHUNK), F32), pltpu.VMEM((FOX_HEADS, BLOCK, CHUNK), F32),
                        pltpu.VMEM((FOX_HEADS, BLOCK, CHUNK), BF16), pltpu.VMEM((FOX_HEADS, BLOCK, CHUNK), BF16),
                        pltpu.VMEM((FOX_PAIRS, BLOCK, LANES), F32)]
        + (exchange.scratch if exchange else []),
        compiler_params=_cparams(("arbitrary",) * 3 if exchange else ("parallel", "parallel", "arbitrary")),
    )(q, k, v, o_exact, do, lse, c_row, *(exchange.ins if exchange else []))
    return outs[:4], outs[4:]


def _loss_head(h, final_w, target):
    bsz, l, d = h.shape
    nb = l // BLOCK

    def body(h_ref, w_ref, t_ref, loss_ref, dh_ref, dw_ref):
        n = pl.program_id(0)

        @pl.when(n == 0)
        def _():
            loss_ref[...] = jnp.zeros_like(loss_ref)
            dw_ref[...] = jnp.zeros_like(dw_ref)
            dh_ref[...] = jnp.zeros_like(dh_ref)

        @pl.when(n > 0)
        def _():
            w = w_ref[...]
            for b in range(bsz):
                hh = h_ref[b]
                r = _rms_scale(hh)
                err = (hh * r) * w - t_ref[b]
                loss_ref[...] += 0.5 * jnp.sum(jnp.mean(err * err, axis=-1, keepdims=True), axis=0, keepdims=True)
                dy = err * (1.0 / d)
                dh, dw = _rms_bwd(dy, hh, w)
                dh_ref[b] = dh
                dw_ref[...] += dw

    return pl.pallas_call(
        body,
        name="loss_head",
        grid=(nb,),
        in_specs=[
            pl.BlockSpec((bsz, BLOCK, d), lambda n: (0, n, 0)),
            pl.BlockSpec((1, d), lambda n: (0, 0)),
            pl.BlockSpec((bsz, BLOCK, d), lambda n: (0, jnp.maximum(n - 1, 0), 0)),
        ],
        out_specs=[
            pl.BlockSpec((1, 128), lambda n: (0, 0)),
            pl.BlockSpec((bsz, BLOCK, d), lambda n: (0, n, 0)),
            pl.BlockSpec((1, d), lambda n: (0, 0)),
        ],
        out_shape=[jax.ShapeDtypeStruct((1, 128), F32), jax.ShapeDtypeStruct((bsz, l, d), F32),
                   jax.ShapeDtypeStruct((1, d), F32)],
        compiler_params=_cparams(("arbitrary",)),
    )(h, final_w, target)


def _pad_tiles(w, src, heads, lane_slot, axis):
    pieces = []
    for h in range(heads):
        x = lax.slice_in_dim(w, src + HEAD_DIM * h, src + HEAD_DIM * (h + 1), axis=axis)
        z = jnp.zeros_like(x)
        pieces += [x, z] if lane_slot(h) == 0 else [z, x]
    return pieces


def _unpad_tiles(g, off, heads, lane_slot, axis):
    return [lax.slice_in_dim(g, off + LANES * h + HEAD_DIM * lane_slot(h),
                             off + LANES * h + HEAD_DIM * (lane_slot(h) + 1), axis=axis) for h in range(heads)]


REF_RUNS = ((0, SRC_QB, 0, 0), (SRC_QB, SRC_F, 1, 0), (SRC_F, SRC_GA, 0, OFF_F), (SRC_GA, W_IN_COLS, 2, 0))
P_RUNS = ((0, OFF_F, 0), (OFF_F, OFF_F + B_HEADS, SRC_F), (OFF_QB, OFF_GA, SRC_QB), (OFF_GA, P_COLS, SRC_GA))
SHARD_COLS = W_IN_COLS // N_CHIPS
SHARD_PAD_COLS = -(-SHARD_COLS // LANES) * LANES


def _place(tile, lane, src_ref, src, dst, length):
    for t in range(src // LANES, (src + length - 1) // LANES + 1):
        x = src_ref[:, t * LANES:(t + 1) * LANES].astype(F32)
        shift = (dst - src) % LANES
        moved = pltpu.roll(x, shift, 1) if shift else x
        from_t = (lane >= max(dst, dst + t * LANES - src)) & (lane < min(dst + length, dst + (t + 1) * LANES - src))
        tile = jnp.where(from_t, moved, tile)
    return tile


def _layout_w_in(stacked):
    d = stacked.shape[1]
    rows = _tile(d, 256)

    def body(s_ref, o_ref):
        lane = lax.broadcasted_iota(jnp.int32, (rows, LANES), 1)
        for lo in range(0, P_COLS, LANES):
            tile = jnp.zeros((rows, LANES), F32)
            for first, end, ref_col in P_RUNS:
                start, stop = max(lo, first), min(lo + LANES, end)
                while start < stop:
                    k, src = divmod(ref_col + start - first, SHARD_COLS)
                    length = min(stop - start, SHARD_COLS - src)
                    tile = _place(tile, lane, s_ref.at[k], src, start - lo, length)
                    start += length
            o_ref[:, lo:lo + LANES] = tile.astype(o_ref.dtype)

    return pl.pallas_call(
        body,
        name="layout_w_in",
        grid=(d // rows,),
        in_specs=[pl.BlockSpec((N_CHIPS, rows, SHARD_PAD_COLS), lambda i: (0, i, 0))],
        out_specs=pl.BlockSpec((rows, P_COLS), lambda i: (i, 0)),
        out_shape=jax.ShapeDtypeStruct((d, P_COLS), stacked.dtype),
        compiler_params=_cparams(("parallel",)),
    )(stacked)


def _stack_w_in_grad(pieces):
    d = pieces[0].shape[0]
    rows = _tile(d, 256)

    def body(*refs):
        piece_refs, o_ref = refs[:-1], refs[-1]
        lane = lax.broadcasted_iota(jnp.int32, (rows, LANES), 1)
        for k in range(N_CHIPS):
            for j in range(0, SHARD_COLS, LANES):
                width = min(LANES, SHARD_COLS - j)
                lo = k * SHARD_COLS + j
                tile = jnp.zeros((rows, LANES), F32)
                for first, end, piece, at in REF_RUNS:
                    start, stop = max(lo, first), min(lo + width, end)
                    if start >= stop:
                        continue
                    tile = _place(tile, lane, piece_refs[piece], at + start - first, start - lo, stop - start)
                o_ref[k, :, j:j + width] = tile[:, :width]

    return pl.pallas_call(
        body,
        name="stack_w_in_grad",
        grid=(d // rows,),
        in_specs=[pl.BlockSpec((rows, p.shape[1]), lambda i: (i, 0)) for p in pieces],
        out_specs=pl.BlockSpec((N_CHIPS, rows, SHARD_COLS), lambda i: (0, i, 0)),
        out_shape=jax.ShapeDtypeStruct((N_CHIPS, d, SHARD_COLS), F32),
        compiler_params=_cparams(("parallel",)),
    )(*pieces)


def _local_step(x, target, meta, norms, b_forget, sinks, w, comm=None):
    n1, nmix, n2, nfin = norms
    w1i, w1o = w[:2]
    if meta is not None:
        x = jnp.concatenate([jnp.zeros((x.shape[0], N_PAD, x.shape[2]), F32),
                             jnp.broadcast_to(meta[None], (x.shape[0], N_META, x.shape[2])), x], axis=1)
    bsz, l, d = x.shape
    nb = l // BLOCK
    t = bsz * l
    h0 = x.reshape(t, d)

    if comm is None:
        (h1, g1, u1), _ = _ffn_fwd(h0, n1, w1i, w1o)
        w_in, wa, wb, wo, w2i, w2o = w[2:]
        w_in = jnp.pad(w_in.reshape(d, N_CHIPS, SHARD_COLS).transpose(1, 0, 2),
                       ((0, 0), (0, 0), (0, SHARD_PAD_COLS - SHARD_COLS)))
    else:
        (h1, g1, u1), (w_in,) = _ffn_fwd(h0, n1, w1i, w1o, comm.gather(GATHER_PROJ))
    wp = _layout_w_in(w_in)
    un, qa, ka, va, qb, kb, vb, ga, gb, f_logit = _proj_fwd(h1, nmix, wp)
    b_pad = jnp.concatenate([b_forget, jnp.zeros((1, F_COLS - B_HEADS), F32)], axis=1)
    c = _forget_cumsum(f_logit, b_pad, nb)
    c_heads = c[:, :B_HEADS].reshape(bsz, l, B_HEADS).transpose(0, 2, 1).reshape(bsz * B_HEADS, l)
    c_row = c_heads.reshape(bsz * B_HEADS, nb, 1, BLOCK)

    slopes = jnp.exp2(-8.0 * jnp.arange(1, A_HEADS + 1, dtype=F32) / A_HEADS)
    slope_rows = jnp.repeat(slopes.reshape(A_KV_HEADS, A_GROUP), BLOCK, axis=1)[:, :, None]
    sink_rows = jnp.repeat(sinks.reshape(A_KV_HEADS, A_GROUP), BLOCK, axis=1)[:, :, None]

    oa, lse_a = _swa_fwd(qa, ka, va, sink_rows, slope_rows, nb)
    if comm is None:
        (ob, ob_exact, lse_b), _ = _fox_fwd(qb, kb, vb, c_row, nb)
    else:
        (ob, ob_exact, lse_b), (wa, wb, wo, w2i, w2o) = _fox_fwd(qb, kb, vb, c_row, nb, comm.gather(GATHER_LATE))
    wa_p = jnp.concatenate(_pad_tiles(wa, 0, A_HEADS, A_SLOT, 0), axis=0)
    h2, mixed = _merge_fwd(h1, oa, ob, ga, gb, wa_p, wb, wo)
    (h3, g2, u2), _ = _ffn_fwd(h2, n2, w2i, w2o)
    loss, dh3, d_nfin = _loss_head(h3.reshape(bsz, l, d), nfin, target)

    (dh2, n2b, a2, dgu2, df2, dn2_parts), _ = _ffn_bwd(dh3.reshape(t, d), h2, n2, g2, u2, w2i, w2o)
    g_w2o = _tn_matmul(a2, df2, "grad_ffn2_w_out")
    g_w2i = _tn_matmul(n2b, dgu2, "grad_ffn2_w_in")

    hosted = comm.swap("ffn2", dict(ffn2_w_in=g_w2i, ffn2_w_out=g_w2o)) if comm else None
    (dya, dyb, doa, dob, dgates, dh2b), swapped = _merge_bwd(dh2, oa, ob, ga, gb, wa_p, wb, wo, hosted)
    g_wo = _tn_matmul(mixed, dh2b, "grad_w_out")
    g_wa = jnp.concatenate(_unpad_tiles(_tn_matmul(oa, dya, "grad_w_branch_a"), 0, A_HEADS, A_SLOT, 0), axis=0)
    g_wb = _tn_matmul(ob, dyb, "grad_w_branch_b")

    dqa, dka, dva, dsink_rows = _swa_bwd(qa, ka, va, doa, lse_a, sink_rows, slope_rows, nb)
    hosted = comm.scatter("ffn2", swapped) if comm else None
    (dqb, dkb, dvb, dc_row), pieces = _fox_bwd(qb, kb, vb, ob_exact, dob, lse_b, c_row, nb, hosted)
    if comm:
        comm.received("ffn2", pieces)
    dc = dc_row.reshape(bsz, B_HEADS, l).transpose(0, 2, 1).reshape(t, B_HEADS)
    dc = jnp.concatenate([dc, jnp.zeros((t, F_COLS - B_HEADS), F32)], axis=1)
    df_logit, db_parts = _forget_cumsum_bwd(dc, f_logit, b_pad, nb)

    dproj = (jnp.concatenate([dqa, dka, dva, df_logit], axis=1), jnp.concatenate([dqb, dkb, dvb], axis=1), dgates)
    g_win = _stack_w_in_grad([_tn_matmul(un, piece, "grad_w_in_" + tag) for piece, tag in zip(dproj, ("a", "b", "gates"))])
    if comm is None:
        g_win = g_win.transpose(1, 0, 2).reshape(d, W_IN_COLS)
    hosted = comm.swap("mixer", dict(w_in=g_win, w_branch_a=g_wa, w_branch_b=g_wb, w_out=g_wo)) if comm else None
    (dh1, dnmix_parts), swapped = _proj_bwd(dh2, h1, nmix, dproj, wp, hosted)
    hosted = comm.scatter("mixer", swapped) if comm else None
    (dh0, n1b, a1, dgu1, df1, dn1_parts), pieces = _ffn_bwd(dh1, h0, n1, g1, u1, w1i, w1o, hosted)
    dh0 = dh0.reshape(bsz, l, d)
    grad_x = dh0[:, PREFIX:]
    small = dict(
        meta_tokens=jnp.sum(dh0[:, N_PAD:PREFIX], axis=0),
        ffn1_norm=jnp.sum(dn1_parts, axis=0),
        mix_norm=jnp.sum(dnmix_parts, axis=0),
        ffn2_norm=jnp.sum(dn2_parts, axis=0),
        final_norm=d_nfin,
        b_forget=jnp.sum(db_parts, axis=0)[:, :B_HEADS],
        attn_sinks=jnp.sum(dsink_rows.reshape(bsz, A_HEADS, BLOCK), axis=(0, 2)).reshape(1, A_HEADS),
    )
    if comm is None:
        g_w1o = _tn_matmul(a1, df1, "grad_ffn1_w_out")
        g_w1i = _tn_matmul(n1b, dgu1, "grad_ffn1_w_in")
    else:
        comm.received("mixer", pieces)
        g_w1o, gathered = _tn_matmul(a1, df1, "grad_ffn1_w_out", comm.small_gather(loss, small))
        comm.small_gathered(gathered)
        swapped = _run_exchange(comm.swap("ffn1_out", dict(ffn1_w_out=g_w1o)), "exchange_halves_ffn1_out")
        left, pieces = _tn_matmul(n1b, dgu1, "grad_ffn1_w_in_left", comm.scatter("ffn1_out", swapped),
                                  cols=(0, LAST_SPLIT))
        comm.received("ffn1_out", pieces)
        swapped = _run_exchange(comm.swap("ffn1_in_left", dict(ffn1_w_in_left=left)), "exchange_halves_ffn1_in_left")
        g_w1i, pieces = _tn_matmul(n1b, dgu1, "grad_ffn1_w_in_right", comm.scatter("ffn1_in_left", swapped),
                                   cols=(LAST_SPLIT, 5632 // N_CHIPS - LAST_SPLIT))
        comm.received("ffn1_in_left", pieces)
    big = dict(ffn1_w_in=g_w1i, ffn1_w_out=g_w1o, w_in=g_win, w_branch_a=g_wa, w_branch_b=g_wb,
               w_out=g_wo, ffn2_w_in=g_w2i, ffn2_w_out=g_w2o)
    return loss, grad_x, small, big


BIG = (
    ("ffn1_w_in", (D_MODEL, 5632), 1),
    ("ffn1_w_out", (2816, D_MODEL), 0),
    ("w_in", (D_MODEL, W_IN_COLS), 1),
    ("w_branch_a", (A_WIDTH, D_MODEL), 1),
    ("w_branch_b", (B_WIDTH, D_MODEL), 1),
    ("w_out", (D_MODEL, D_MODEL), 0),
    ("ffn2_w_in", (D_MODEL, 5632), 1),
    ("ffn2_w_out", (2816, D_MODEL), 0),
)
STACKED = "w_in"
LAST_SPLIT = 640
SPLIT = (
    ("ffn1_w_in_left", (D_MODEL, N_CHIPS * LAST_SPLIT), 1),
    ("ffn1_w_in_right", (D_MODEL, 5632 - N_CHIPS * LAST_SPLIT), 1),
)


def _coords():
    return lax.axis_index("x"), lax.axis_index("y"), lax.axis_index("c")


def _other_chips(x, y):
    return ((1 - x, y), (x, 1 - y), (1 - x, 1 - y))


def _chip_part(ref, name, shape, axis, k):
    if name == STACKED:
        return ref.at[k]
    size = shape[axis] // N_CHIPS
    start = pl.multiple_of(k * size, size)
    return ref.at[pl.ds(start, size), :] if axis == 0 else ref.at[:, pl.ds(start, size)]


def _full_shape(name, shape):
    return (N_CHIPS, shape[0], shape[1] // N_CHIPS) if name == STACKED else shape


class _Exchange:
    def __init__(self, ins, out_shape, n_sems, ops):
        self.ins, self.out_shape, self.n_sems, self.ops = list(ins), list(out_shape), n_sems, ops

    @property
    def scratch(self):
        return [pltpu.SemaphoreType.DMA((self.n_sems,)), pltpu.SemaphoreType.DMA((self.n_sems,))]


SEMS_PER_GATHER = 9


def _gather_exchange(shards, table):
    n = len(table)
    x_nbr, y_nbr, diagonal = 0, 1, 2

    def ops(ins, outs, send_sems, recv_sems):
        x, y, c = _coords()
        mine = 2 * x + y
        sibling = (x, y, 1 - c)
        chips = _other_chips(x, y)
        slots = [2 * chip[0] + chip[1] for chip in chips]

        def part(i, k):
            name, shape, axis = table[i][:3]
            return _chip_part(outs[i], name, shape, axis, k)

        def half(ref, h):
            rows = ref.shape[0] // 2
            return ref.at[pl.ds(pl.multiple_of(h * rows, rows), rows), :]

        def remote(i, sem, src, dst, device):
            sem = SEMS_PER_GATHER * i + sem
            return pltpu.make_async_remote_copy(src, dst, send_sems.at[sem], recv_sems.at[sem],
                                                device_id=device, device_id_type=MESH_ID)

        def own(i):
            return remote(i, 0, ins[i], part(i, mine), sibling)

        def fetch(i, j, slot):
            if table[i][4]:
                src, dst = half(ins[i], c), half(part(i, slot), c)
            else:
                src, dst = ins[i], part(i, slot)
            return remote(i, 1 + j, src, dst, (chips[j][0], chips[j][1], c))

        def relayed(i, via, of):
            region = half(half(part(i, slots[of]), c), via)
            return remote(i, 4 + via, region, region, (chips[via][0], chips[via][1], c))

        def forward(i, j, h):
            region = half(part(i, slots[j]), h)
            return remote(i, 6 + j, region, region, sibling)

        def start():
            for i in range(n):
                for j in (x_nbr, y_nbr) if table[i][4] else (x_nbr, y_nbr, diagonal):
                    fetch(i, j, mine).start()
            for i in range(n):
                own(i).start()

        def relay():
            for i in range(n):
                if not table[i][4]:
                    for j in range(3):
                        fetch(i, j, slots[j]).wait_recv()
                    continue
                fetch(i, y_nbr, slots[y_nbr]).wait_recv()
                relayed(i, x_nbr, y_nbr).start()
                forward(i, y_nbr, c).start()
                fetch(i, x_nbr, slots[x_nbr]).wait_recv()
                relayed(i, y_nbr, x_nbr).start()
                forward(i, x_nbr, c).start()

        def relay_diagonal():
            for i in range(n):
                if table[i][4]:
                    relayed(i, x_nbr, diagonal).wait_recv()
                    relayed(i, y_nbr, diagonal).wait_recv()
                    forward(i, diagonal, c).start()

        def finish():
            for i in range(n):
                own(i).wait()
                for j in range(3):
                    if table[i][4]:
                        forward(i, j, 1 - c).wait_recv()
                        forward(i, j, c).wait_send()
                    if j != diagonal or not table[i][4]:
                        fetch(i, j, mine).wait_send()
                if table[i][4]:
                    relayed(i, x_nbr, y_nbr).wait_send()
                    relayed(i, y_nbr, x_nbr).wait_send()

        return start, relay, relay_diagonal, finish

    out_shape = [jax.ShapeDtypeStruct(_full_shape(name, shape), dtype) for name, shape, _, dtype, _ in table]
    return _Exchange(shards, out_shape, SEMS_PER_GATHER * n, ops)


STREAM_ROWS = 256


def _stream_rows(src, dst, dst_first, buf, sem_in, sem_out):
    bsz, n_rows, _ = src.shape
    chunks = [(b, r) for b in range(bsz) for r in range(0, n_rows, STREAM_ROWS)]

    def load(i):
        b, r = chunks[i]
        return pltpu.make_async_copy(src.at[b, pl.ds(r, STREAM_ROWS), :], buf.at[i % 2], sem_in.at[i % 2])

    def store(i):
        b, r = chunks[i]
        return pltpu.make_async_copy(buf.at[i % 2], dst.at[b, pl.ds(dst_first + r, STREAM_ROWS), :],
                                     sem_out.at[i % 2])

    load(0).start()
    for i in range(len(chunks)):
        if i + 1 < len(chunks):
            if i >= 1:
                store(i - 1).wait()
            load(i + 1).start()
        load(i).wait()
        store(i).start()
    for i in range(max(len(chunks) - 2, 0), len(chunks)):
        store(i).wait()


def _run_exchange(exchange, name):
    n = len(exchange.ins)

    def body(*refs):
        for phase in exchange.ops(refs[:n], refs[n:2 * n], *refs[2 * n:]):
            phase()

    return pl.pallas_call(
        body,
        name=name,
        in_specs=[HBM_SPEC] * n,
        out_specs=[HBM_SPEC] * n,
        out_shape=exchange.out_shape,
        scratch_shapes=exchange.scratch,
    )(*exchange.ins)


CAST_ROWS = 64


def _cast_hosting(arrays, exchange, name, stream=None):
    n_a, n_x = len(arrays), len(exchange.ins)
    k = 1 if stream else 0

    def body(*refs):
        a_in, x_in = refs[:n_a], refs[n_a:n_a + n_x]
        outs = refs[n_a + n_x + k:]
        a_out, x_out = outs[:n_a], outs[n_a:n_a + n_x]
        scratch = outs[n_a + n_x + k:]
        start, *rest = exchange.ops(x_in, x_out, *scratch[:2])
        start()
        for src, dst in zip(a_in, a_out):
            def rows(i, carry, src=src, dst=dst):
                window = pl.ds(pl.multiple_of(i * CAST_ROWS, CAST_ROWS), CAST_ROWS)
                cols = src.shape[1]
                if dst.shape[1] != cols:
                    dst[window, dst.shape[1] - LANES:] = jnp.zeros((CAST_ROWS, LANES), BF16)
                dst[window, :cols] = src[window, :].astype(BF16)
                return carry

            lax.fori_loop(0, src.shape[0] // CAST_ROWS, rows, 0)
        if stream:
            x_ref, h_ref = refs[n_a + n_x], outs[n_a + n_x]
            buf, sem_in, sem_out = scratch[2:]
            _stream_rows(x_ref, h_ref, PREFIX, buf, sem_in, sem_out)
        for phase in rest:
            phase()
        if stream:
            meta = pltpu.make_async_copy(x_out[stream[1]], buf.at[1, pl.ds(0, N_META), :], sem_in.at[0])
            meta.start()
            buf[0, 0:N_PAD, :] = jnp.zeros((N_PAD, buf.shape[2]), F32)
            meta.wait()
            buf[0, N_PAD:PREFIX, :] = buf[1, 0:N_META, :]
            puts = [pltpu.make_async_copy(buf.at[0, pl.ds(0, PREFIX), :], h_ref.at[b, pl.ds(0, PREFIX), :], sem_out.at[b])
                    for b in range(h_ref.shape[0])]
            for put in puts:
                put.start()
            for put in puts:
                put.wait()

    extra_in, extra_out, extra_scratch = [], [], []
    if stream:
        x = stream[0]
        assert x.shape[0] <= 2 and x.shape[1] % STREAM_ROWS == 0 and x.dtype == F32
        extra_in = [x]
        extra_out = [jax.ShapeDtypeStruct((x.shape[0], PREFIX + x.shape[1], x.shape[2]), F32)]
        extra_scratch = [pltpu.VMEM((2, STREAM_ROWS, x.shape[2]), F32),
                         pltpu.SemaphoreType.DMA((2,)), pltpu.SemaphoreType.DMA((2,))]
    outs = pl.pallas_call(
        body,
        name=name,
        in_specs=[VMEM_SPEC] * n_a + [HBM_SPEC] * (n_x + k),
        out_specs=[VMEM_SPEC] * n_a + [HBM_SPEC] * (n_x + k),
        out_shape=[jax.ShapeDtypeStruct((a.shape[0], -(-a.shape[1] // LANES) * LANES), BF16) for a in arrays]
        + exchange.out_shape + extra_out,
        scratch_shapes=exchange.scratch + extra_scratch,
        compiler_params=pltpu.CompilerParams(vmem_limit_bytes=VMEM_LIMIT),
    )(*arrays, *exchange.ins, *extra_in)
    return outs[:n_a], outs[n_a:]


def _host_exchange(exchange, in_refs, out_refs, sem_refs, first, middle, last, late=None):
    start, *relays, finish = exchange.ops(in_refs, out_refs, *sem_refs)
    pl.when(first)(start)
    pl.when(middle)(relays[0])
    if len(relays) > 1:
        pl.when(last if late is None else late)(relays[1])
    pl.when(last)(finish)


def _halves_view(name, shape, axis):
    r, c = shape
    if name == STACKED:
        return (N_CHIPS, 2, r // 2, c // N_CHIPS), lambda ref, h: ref.at[:, h]
    if axis == 1:
        return (2, r // 2, c), lambda ref, h: ref.at[h]
    return (N_CHIPS, 2, r // N_CHIPS // 2, c), lambda ref, h: ref.at[:, h]


def _halves_exchange(grads, entries):
    n_w = len(entries)
    views = [_halves_view(*entry) for entry in entries]

    def ops(ins, outs, send_sems, recv_sems):
        x, y, c = _coords()
        copies = [pltpu.make_async_remote_copy(views[i][1](ins[i], 1 - c), outs[i], send_sems.at[i], recv_sems.at[i],
                                               device_id=(x, y, 1 - c), device_id_type=MESH_ID) for i in range(n_w)]

        def start():
            for cp in copies:
                cp.start()

        def finish():
            for cp in copies:
                cp.wait()

        return start, lambda: None, finish

    half_shape = lambda v: tuple(d for i, d in enumerate(v) if i != (1 if len(v) == 4 else 0))
    out_shape = [jax.ShapeDtypeStruct(half_shape(v[0]), F32) for v in views]
    return _Exchange([g.reshape(v[0]) for g, v in zip(grads, views)], out_shape, n_w, ops)


def _add_sibling(g_view, recv, c, name):
    shape = recv.shape
    if len(shape) == 2:
        tr = _tile(shape[0], 128, 16)
        grid = (shape[0] // tr,)
        g_spec = pl.BlockSpec((None, tr, shape[1]), lambda i, c_ref: (c_ref[0], i, 0))
        r_spec = pl.BlockSpec((tr, shape[1]), lambda i, c_ref: (i, 0))
    else:
        tr = _tile(shape[1], 256, 16)
        grid = (N_CHIPS, shape[1] // tr)
        g_spec = pl.BlockSpec((None, None, tr, shape[2]), lambda k, i, c_ref: (k, c_ref[0], i, 0))
        r_spec = pl.BlockSpec((None, tr, shape[2]), lambda k, i, c_ref: (k, i, 0))

    def body(c_ref, g_ref, r_ref, o_ref):
        o_ref[...] = (g_ref[...] + r_ref[...]).astype(BF16)

    return pl.pallas_call(
        body,
        name="add_sibling_" + name,
        grid_spec=pltpu.PrefetchScalarGridSpec(num_scalar_prefetch=1, grid=grid, in_specs=[g_spec, r_spec],
                                               out_specs=r_spec),
        out_shape=jax.ShapeDtypeStruct(shape, BF16),
        compiler_params=_cparams(("parallel",) * len(grid)),
    )(c, g_view, recv)


def _piece_of(ref, name, axis, k):
    if name == STACKED or axis == 0:
        return ref.at[k]
    size = ref.shape[1] // N_CHIPS
    return ref.at[:, pl.ds(pl.multiple_of(k * size, size), size)]


def _piece_shape(name, shape, axis):
    r, c = shape
    return (r // 2, c // N_CHIPS) if (axis == 1) else (r // N_CHIPS // 2, c)


def _scatter_exchange(partials, entries):
    n_w = len(entries)

    def ops(ins, outs, send_sems, recv_sems):
        x, y, c = _coords()
        chips = _other_chips(x, y)
        copies = []
        for i, (name, _, axis) in enumerate(entries):
            for j, chip in enumerate(chips):
                sem = 3 * i + j
                copies.append(pltpu.make_async_remote_copy(
                    _piece_of(ins[i], name, axis, 2 * chip[0] + chip[1]), outs[i].at[j], send_sems.at[sem],
                    recv_sems.at[sem], device_id=(chip[0], chip[1], c), device_id_type=MESH_ID))

        def start():
            for cp in copies:
                cp.start()

        def finish():
            for cp in copies:
                cp.wait()

        return start, lambda: None, finish

    out_shape = [jax.ShapeDtypeStruct((3,) + _piece_shape(*entry), BF16) for entry in entries]
    return _Exchange(partials, out_shape, 3 * n_w, ops)


def _add_chips(partial, recv, mine, name, axis):
    rows, cols = recv.shape[1:]
    tr = _tile(rows, 256, 16)
    if name == STACKED or axis == 0:
        p_spec = pl.BlockSpec((None, tr, cols), lambda i, k_ref: (k_ref[0], i, 0))
    else:
        p_spec = pl.BlockSpec((tr, cols), lambda i, k_ref: (i, k_ref[0]))

    def body(k_ref, p_ref, r_ref, o_ref):
        f32 = lambda a: a.astype(F32)
        o_ref[...] = ((f32(p_ref[...]) + f32(r_ref[0])) + f32(r_ref[1])) + f32(r_ref[2])

    return pl.pallas_call(
        body,
        name="add_chips_" + name,
        grid_spec=pltpu.PrefetchScalarGridSpec(
            num_scalar_prefetch=1, grid=(rows // tr,),
            in_specs=[p_spec, pl.BlockSpec((3, tr, cols), lambda i, k_ref: (0, i, 0))],
            out_specs=pl.BlockSpec((tr, cols), lambda i, k_ref: (i, 0))),
        out_shape=jax.ShapeDtypeStruct((rows, cols), F32),
        compiler_params=_cparams(("parallel",)),
    )(mine, partial, recv)


def _share_with_sibling(halves):
    n_w = len(halves)

    def body(*refs):
        ins, outs = refs[:n_w], refs[n_w:2 * n_w]
        send_sems, recv_sems = refs[2 * n_w:]
        x, y, c = _coords()
        copies = [pltpu.make_async_remote_copy(ins[i], outs[i], send_sems.at[i], recv_sems.at[i],
                                               device_id=(x, y, 1 - c), device_id_type=MESH_ID) for i in range(n_w)]
        for cp in copies:
            cp.start()
        for cp in copies:
            cp.wait()

    return pl.pallas_call(
        body,
        name="share_with_sibling",
        in_specs=[HBM_SPEC] * n_w,
        out_specs=[HBM_SPEC] * n_w,
        out_shape=[jax.ShapeDtypeStruct(h.shape, F32) for h in halves],
        scratch_shapes=[pltpu.SemaphoreType.DMA((n_w,)), pltpu.SemaphoreType.DMA((n_w,))],
    )(*halves)


SMALL_ROWS = 168


def _small_exchange(buf):
    def ops(ins, outs, send_sems, recv_sems):
        x, y, c = _coords()
        me = 4 * x + 2 * y + c
        peers = [(x ^ fx, y ^ fy, c ^ fc) for fx in (0, 1) for fy in (0, 1) for fc in (0, 1)][1:]

        def copy(j, slot, dev):
            return pltpu.make_async_remote_copy(ins[0], outs[0].at[slot], send_sems.at[j], recv_sems.at[j],
                                                device_id=dev, device_id_type=MESH_ID)

        own = pltpu.make_async_copy(ins[0], outs[0].at[me], send_sems.at[N_DEV - 1])

        def start():
            own.start()
            for j, dev in enumerate(peers):
                copy(j, me, dev).start()

        def finish():
            for j, dev in enumerate(peers):
                copy(j, 4 * dev[0] + 2 * dev[1] + dev[2], dev).wait()
            own.wait()

        return start, lambda: None, finish

    return _Exchange([buf], [jax.ShapeDtypeStruct((N_DEV,) + buf.shape, F32)], N_DEV, ops)


def _sum_devices(gathered):
    def body(g_ref, out_ref):
        acc = g_ref[0]
        for d in range(1, N_DEV):
            acc = acc + g_ref[d]
        out_ref[...] = acc

    return pl.pallas_call(
        body,
        name="sum_devices",
        in_specs=[VMEM_SPEC],
        out_specs=VMEM_SPEC,
        out_shape=jax.ShapeDtypeStruct(gathered.shape[1:], F32),
    )(gathered)


def _adamw(w, g, m, v, copy_g=False):
    r, rest = w.shape[0], w.shape[1:]
    per_row = 1
    for dim in rest:
        per_row *= dim
    tr = _tile(r, max(8, (5 << 19) // (4 * per_row)), 8 if len(rest) == 1 else 1)

    def body(w_ref, g_ref, m_ref, v_ref, *out_refs):
        d_ref, mo_ref, vo_ref = out_refs[-3:]
        gg = g_ref[...]
        if copy_g:
            out_refs[0][...] = gg
        mm = ADAM_B1 * m_ref[...] + (1.0 - ADAM_B1) * gg
        vv = ADAM_B2 * v_ref[...] + (1.0 - ADAM_B2) * (gg * gg)
        m_hat = mm / (1.0 - ADAM_B1 ** ADAM_STEP)
        v_hat = vv / (1.0 - ADAM_B2 ** ADAM_STEP)
        d_ref[...] = -ADAM_LR * (m_hat / (jnp.sqrt(v_hat) + ADAM_EPS) + ADAM_WD * w_ref[...])
        mo_ref[...] = mm
        vo_ref[...] = vv

    n_out = 4 if copy_g else 3
    spec = pl.BlockSpec((tr,) + rest, lambda i: (i,) + (0,) * len(rest))
    return pl.pallas_call(
        body,
        name="adamw",
        grid=(r // tr,),
        in_specs=[spec] * 4,
        out_specs=[spec] * n_out,
        out_shape=[jax.ShapeDtypeStruct(w.shape, F32)] * n_out,
        compiler_params=_cparams(("parallel",)),
    )(w, g, m, v)


def _adamw_halves(w, own, other, m, v, c, name):
    r, cols = w.shape
    half = r // 2
    tr = _tile(half, 256, 8)
    nt = half // tr
    whole = pl.BlockSpec((tr, cols), lambda h, i, c_ref: (h * nt + i, 0))
    part = pl.BlockSpec((tr, cols), lambda h, i, c_ref: (i, 0))

    def body(c_ref, w_ref, own_ref, other_ref, m_ref, v_ref, g_ref, d_ref, mo_ref, vo_ref):
        gg = jnp.where(pl.program_id(0) == c_ref[0], own_ref[...], other_ref[...])
        g_ref[...] = gg
        mm = ADAM_B1 * m_ref[...] + (1.0 - ADAM_B1) * gg
        vv = ADAM_B2 * v_ref[...] + (1.0 - ADAM_B2) * (gg * gg)
        m_hat = mm / (1.0 - ADAM_B1 ** ADAM_STEP)
        v_hat = vv / (1.0 - ADAM_B2 ** ADAM_STEP)
        d_ref[...] = -ADAM_LR * (m_hat / (jnp.sqrt(v_hat) + ADAM_EPS) + ADAM_WD * w_ref[...])
        mo_ref[...] = mm
        vo_ref[...] = vv

    return pl.pallas_call(
        body,
        name="adamw_" + name,
        grid_spec=pltpu.PrefetchScalarGridSpec(
            num_scalar_prefetch=1, grid=(2, nt),
            in_specs=[whole, part, part, whole, whole], out_specs=[whole] * 4),
        out_shape=[jax.ShapeDtypeStruct((r, cols), F32)] * 4,
        compiler_params=_cparams(("parallel", "parallel")),
    )(c, w, own, other, m, v)


GATHER_FIRST = ("ffn1_w_in", "ffn1_w_out")
GATHER_PROJ = ("w_in",)
GATHER_LATE = ("w_branch_a", "w_branch_b", "w_out", "ffn2_w_in", "ffn2_w_out")


class _Comm:
    def __init__(self, shards, c_arr, mine_arr):
        self.shards, self.c, self.mine = shards, c_arr, mine_arr
        self.groups, self.halves = {}, {}
        self.by_name = {entry[0]: entry for entry in BIG + SPLIT}

    def small_gather(self, loss, small):
        pad_lanes = lambda a: jnp.concatenate([a, jnp.zeros((1, LANES - a.shape[1]), F32)], axis=1)
        buf = jnp.concatenate([
            small["meta_tokens"].reshape(128, LANES),
            small["ffn1_norm"].reshape(8, LANES), small["mix_norm"].reshape(8, LANES),
            small["ffn2_norm"].reshape(8, LANES), small["final_norm"].reshape(8, LANES),
            loss, pad_lanes(small["b_forget"]), pad_lanes(small["attn_sinks"]),
            jnp.zeros((SMALL_ROWS - 163, LANES), F32)], axis=0)
        return _small_exchange(buf)

    def small_gathered(self, outs):
        self.reduced = _sum_devices(outs[0])

    def gather(self, names):
        padded = (STACKED, (D_MODEL, N_CHIPS * SHARD_PAD_COLS), 1)
        table = [(padded if n == STACKED else self.by_name[n]) + (BF16, True) for n in names]
        return _gather_exchange([self.shards[n] for n in names], table)

    def swap(self, tag, grads):
        entries = [self.by_name[n] for n in grads]
        arrays = list(grads.values())
        self.groups[tag] = (entries, arrays)
        return _halves_exchange(arrays, entries)

    def scatter(self, tag, received):
        entries, arrays = self.groups[tag]
        views = [_halves_view(*entry) for entry in entries]
        partials = [_add_sibling(g.reshape(v[0]), r, self.c, name)
                    for g, v, r, (name, _, _) in zip(arrays, views, received, entries)]
        self.groups[tag] = (entries, partials)
        return _scatter_exchange(partials, entries)

    def received(self, tag, pieces):
        entries, partials = self.groups[tag]
        for p, r, (name, _, axis) in zip(partials, pieces, entries):
            self.halves[name] = _add_chips(p, r, self.mine, name, axis)

    def finish(self):
        names = [n for n, _, _ in BIG + SPLIT if n != "ffn1_w_in"]
        own = [self.halves[n] for n in names]
        both = dict(zip(names, zip(own, _share_with_sibling(own))))
        left, right = [both.pop(n) for n, _, _ in SPLIT]
        both["ffn1_w_in"] = tuple(jnp.concatenate(parts, axis=1) for parts in zip(left, right))
        return both


def kernel(x, meta_tokens, ffn1_norm, ffn1_w_in, ffn1_w_out, mix_norm, w_in, b_forget, attn_sinks, w_branch_a, w_branch_b, w_out, ffn2_norm, ffn2_w_in, ffn2_w_out, final_norm, loss_target, m_meta_tokens, m_ffn1_norm, m_ffn1_w_in, m_ffn1_w_out, m_mix_norm, m_w_in, m_b_forget, m_attn_sinks, m_w_branch_a, m_w_branch_b, m_w_out, m_ffn2_norm, m_ffn2_w_in, m_ffn2_w_out, m_final_norm, v_meta_tokens, v_ffn1_norm, v_ffn1_w_in, v_ffn1_w_out, v_mix_norm, v_w_in, v_b_forget, v_attn_sinks, v_w_branch_a, v_w_branch_b, v_w_out, v_ffn2_norm, v_ffn2_w_in, v_ffn2_w_out, v_final_norm):
    given = dict(locals())
    names = ["meta_tokens", "ffn1_norm", "ffn1_w_in", "ffn1_w_out", "mix_norm", "w_in", "b_forget", "attn_sinks",
             "w_branch_a", "w_branch_b", "w_out", "ffn2_norm", "ffn2_w_in", "ffn2_w_out", "final_norm"]
    big_names = [n for n, _, _ in BIG]
    cx, cy, cc = _coords()
    c_arr = cc.reshape(1).astype(jnp.int32)
    mine_arr = (2 * cx + cy).reshape(1).astype(jnp.int32)

    by_name = {entry[0]: entry for entry in BIG}
    shards = {n: given[n][0].astype(BF16) for n in GATHER_FIRST}
    table = [by_name[n] + (BF16, True) for n in GATHER_FIRST] + [("meta_tokens", (N_META, D_MODEL), 1, F32, False)]
    first = _gather_exchange([shards[n] for n in GATHER_FIRST] + [meta_tokens], table)
    late_names = [n for n in big_names if n not in GATHER_FIRST]
    casts, (w1i, w1o, _, stream) = _cast_hosting([given[n][0] for n in late_names], first, "gather_first",
                                                 stream=(x, len(GATHER_FIRST)))
    shards.update(zip(late_names, casts))
    comm = _Comm(shards, c_arr, mine_arr)
    norms = (ffn1_norm, mix_norm, ffn2_norm, final_norm.reshape(1, D_MODEL))
    loss, grad_x, small, big = _local_step(stream, loss_target, None, norms, b_forget, attn_sinks, (w1i, w1o), comm)

    swap = comm.swap("ffn1_in", dict(ffn1_w_in_right=big["ffn1_w_in"]))
    last = comm.scatter("ffn1_in", _run_exchange(swap, "exchange_halves_ffn1_in"))
    comm.received("ffn1_in", _run_exchange(last, "scatter_chip_sums"))
    grad_halves = comm.finish()
    grads = {}

    red = comm.reduced
    meta_cols = red[:128].reshape(N_META, D_MODEL)
    grads["meta_tokens"] = lax.dynamic_slice_in_dim(meta_cols, (2 * cx + cy) * (D_MODEL // N_CHIPS),
                                                    D_MODEL // N_CHIPS, axis=1)
    grads["ffn1_norm"] = red[128:136].reshape(1, D_MODEL)
    grads["mix_norm"] = red[136:144].reshape(1, D_MODEL)
    grads["ffn2_norm"] = red[144:152].reshape(1, D_MODEL)
    grads["final_norm"] = red[152:160].reshape(1, D_MODEL)
    loss_out = red[160, 0]
    grads["b_forget"] = red[161:162, :B_HEADS]
    grads["attn_sinks"] = red[162:163, :A_HEADS]

    out_g, out_d, out_m, out_v = [], [], [], []
    for n in names:
        w_full = given[n]
        shape = w_full.shape
        two_d = (lambda a: a.reshape(shape[-2], shape[-1])) if len(shape) >= 2 else (lambda a: a.reshape(1, shape[0]))
        if n == STACKED:
            own, other = grad_halves[n]
            g_nat = jnp.concatenate([jnp.where(cc == 0, own, other), jnp.where(cc == 0, other, own)], axis=0)
            rows = lambda a: a.reshape(1, shape[-2], shape[-1]).transpose(2, 0, 1)
            unrows = lambda a: a.transpose(1, 2, 0)
            g2, d2, m2, v2 = [unrows(a) for a in _adamw(rows(w_full), rows(g_nat), rows(given["m_" + n]),
                                                         rows(given["v_" + n]), copy_g=True)]
        elif n in grad_halves:
            own, other = grad_halves[n]
            g2, d2, m2, v2 = _adamw_halves(two_d(w_full), own, other, two_d(given["m_" + n]),
                                           two_d(given["v_" + n]), c_arr, n)
        else:
            g2 = two_d(grads[n])
            d2, m2, v2 = _adamw(two_d(w_full), g2, two_d(given["m_" + n]), two_d(given["v_" + n]))
        out_g.append(g2.reshape(shape))
        out_d.append(d2.reshape(shape))
        out_m.append(m2.reshape(shape))
        out_v.append(v2.reshape(shape))
    return (loss_out, grad_x, *out_g, *out_d, *out_m, *out_v)
```

```python
import jax
import jax.numpy as jnp
from jax import lax
from jax.experimental import pallas as pl
from jax.experimental.pallas import tpu as pltpu

F32 = jnp.float32
BF16 = jnp.bfloat16

D_MODEL = 1024
N_META = 16
BLOCK = 128
LANES = 128
PREFIX = BLOCK
N_PAD = PREFIX - N_META
HEAD_DIM = 64
A_HEADS = 8
A_KV_HEADS = 2
A_GROUP = 4
B_HEADS = 8
B_PAIRS = B_HEADS // 2
A_WIDTH = A_HEADS * HEAD_DIM
A_KV_WIDTH = A_KV_HEADS * HEAD_DIM
B_WIDTH = B_HEADS * HEAD_DIM
W_IN_COLS = A_WIDTH + 2 * A_KV_WIDTH + 3 * B_WIDTH + B_HEADS + 2 * D_MODEL
SRC_KA = A_WIDTH
SRC_VA = SRC_KA + A_KV_WIDTH
SRC_QB = SRC_VA + A_KV_WIDTH
SRC_KB = SRC_QB + B_WIDTH
SRC_VB = SRC_KB + B_WIDTH
SRC_F = SRC_VB + B_WIDTH
SRC_GA = SRC_F + B_HEADS
SRC_GB = SRC_GA + D_MODEL
A_PAD_WIDTH = A_HEADS * LANES
B_PAD_WIDTH = B_HEADS * LANES
F_COLS = LANES
OFF_QA = 0
OFF_KA = SRC_KA
OFF_VA = SRC_VA
OFF_F = OFF_VA + A_KV_WIDTH
OFF_QB = OFF_F + F_COLS
OFF_KB = OFF_QB + B_WIDTH
OFF_VB = OFF_KB + B_WIDTH
OFF_GA = OFF_VB + B_WIDTH
OFF_GB = OFF_GA + D_MODEL
P_COLS = OFF_GB + D_MODEL
P_PIECES = ((0, OFF_QB), (OFF_QB, OFF_GA - OFF_QB), (OFF_GA, P_COLS - OFF_GA))
EPS = 1e-6
NEG = -1e30
SCALE = HEAD_DIM ** -0.5
KEY_BLOCKS = 4

ADAM_LR = 0.001
ADAM_B1 = 0.9
ADAM_B2 = 0.999
ADAM_EPS = 1e-08
ADAM_WD = 0.01
ADAM_STEP = 10

N_CHIPS = 4
N_DEV = 8
VMEM_LIMIT = 56 * 1024 * 1024

NT_DIMS = (((1,), (1,)), ((), ()))
TN_DIMS = (((0,), (0,)), ((), ()))
MESH_ID = pl.DeviceIdType.MESH
HBM_SPEC = pl.BlockSpec(memory_space=pltpu.HBM)
VMEM_SPEC = pl.BlockSpec(memory_space=pltpu.VMEM)


def _tile(n, target, mult=16):
    best = None
    for t in range(mult, min(n, target) + 1, mult):
        if n % t == 0:
            best = t
    return best if best is not None else n


def _cparams(sem):
    return pltpu.CompilerParams(dimension_semantics=sem, vmem_limit_bytes=VMEM_LIMIT)


def _rms_scale(h):
    return lax.rsqrt(jnp.mean(h * h, axis=-1, keepdims=True) + EPS)


def _rms_bwd(dn, h, w):
    r = _rms_scale(h)
    dw = jnp.sum(dn * (h * r), axis=0, keepdims=True)
    z = dn * w
    dh = r * z - h * ((r * r * r) * jnp.mean(z * h, axis=-1, keepdims=True))
    return dh, dw


def _ffn_fwd(h, norm_w, w_in, w_out, exchange=None):
    t, d = h.shape
    f = w_out.shape[0]
    tm = _tile(t, 272)
    tc = _tile(f, 256, 128)
    nj = f // tc
    ni = t // tm
    n_x = len(exchange.ins) if exchange else 0

    def body(*refs):
        h_ref, nw_ref, wi_ref, wo_ref = refs[:4]
        hout_ref, g_ref, u_ref = refs[4 + n_x:7 + n_x]
        a_scr = refs[7 + 2 * n_x]
        i = pl.program_id(0)
        if exchange:
            _host_exchange(exchange, refs[4:4 + n_x], refs[7 + n_x:7 + 2 * n_x], refs[8 + 2 * n_x:],
                           i == 0, i == ni // 3, i == ni - 1, late=i == 2 * ni // 3)
        hh = h_ref[...]
        n = ((hh * _rms_scale(hh)) * nw_ref[...]).astype(BF16)
        for j in range(nj):
            cols = slice(j * tc, (j + 1) * tc)
            g = jnp.dot(n, wi_ref[:, j * tc:(j + 1) * tc], preferred_element_type=F32)
            u = jnp.dot(n, wi_ref[:, f + j * tc:f + (j + 1) * tc], preferred_element_type=F32)
            g_ref[:, cols] = g
            u_ref[:, cols] = u
            a_scr[:, cols] = ((g * jax.nn.sigmoid(g)) * u).astype(BF16)
        hout_ref[...] = hh + 0.5 * jnp.dot(a_scr[...], wo_ref[...], preferred_element_type=F32)

    resident = lambda a: pl.BlockSpec(a.shape, lambda i: (0, 0), pipeline_mode=pl.Buffered(1))
    row = lambda w: pl.BlockSpec((tm, w), lambda i: (i, 0))
    outs = pl.pallas_call(
        body,
        name="ffn_fwd",
        grid=(ni,),
        in_specs=[row(d), pl.BlockSpec((1, d), lambda i: (0, 0)), resident(w_in), resident(w_out)] + [HBM_SPEC] * n_x,
        out_specs=[row(d), row(f), row(f)] + [HBM_SPEC] * n_x,
        out_shape=[
            jax.ShapeDtypeStruct((t, d), F32),
            jax.ShapeDtypeStruct((t, f), F32),
            jax.ShapeDtypeStruct((t, f), F32),
        ] + (exchange.out_shape if exchange else []),
        scratch_shapes=[pltpu.VMEM((tm, f), BF16)] + (exchange.scratch if exchange else []),
        compiler_params=_cparams(("arbitrary",) if exchange else ("parallel",)),
    )(h, norm_w, w_in, w_out, *(exchange.ins if exchange else []))
    return outs[:3], outs[3:]


def _ffn_bwd(dh_out, h, norm_w, g, u, w_in, w_out, exchange=None):
    t, d = h.shape
    f = w_out.shape[0]
    tm = _tile(t, 272)
    tc = _tile(f, 256, 128)
    nj = f // tc
    ni = t // tm
    n_x = len(exchange.ins) if exchange else 0

    def body(*refs):
        dho_ref, h_ref, nw_ref, g_ref, u_ref, wi_ref, wo_ref = refs[:7]
        dhin_ref, n_ref, a_ref, dgu_ref, df_ref, dnw_ref = refs[7 + n_x:13 + n_x]
        i = pl.program_id(0)
        if exchange:
            _host_exchange(exchange, refs[7:7 + n_x], refs[13 + n_x:13 + 2 * n_x], refs[13 + 2 * n_x:],
                           i == 0, i == ni - 1, i == ni - 1)
        hh = h_ref[...]
        nw = nw_ref[...]
        n_ref[...] = ((hh * _rms_scale(hh)) * nw).astype(BF16)
        dho = dho_ref[...]
        df = (0.5 * dho).astype(BF16)
        df_ref[...] = df
        for j in range(nj):
            cols = slice(j * tc, (j + 1) * tc)
            da = lax.dot_general(df, wo_ref[cols, :], NT_DIMS, preferred_element_type=F32)
            gg = g_ref[:, cols]
            uu = u_ref[:, cols]
            sig = jax.nn.sigmoid(gg)
            sl = gg * sig
            a_ref[:, cols] = (sl * uu).astype(BF16)
            dgu_ref[0, :, cols] = ((da * uu) * (sig * (1.0 + gg * (1.0 - sig)))).astype(BF16)
            dgu_ref[1, :, cols] = (da * sl).astype(BF16)
        dn = (lax.dot_general(dgu_ref[0], wi_ref[:, :f], NT_DIMS, preferred_element_type=F32)
              + lax.dot_general(dgu_ref[1], wi_ref[:, f:], NT_DIMS, preferred_element_type=F32))
        dh, dw = _rms_bwd(dn, hh, nw)
        dhin_ref[...] = dho + dh
        dnw_ref[0] = dw

    resident = lambda a: pl.BlockSpec(a.shape, lambda i: (0, 0), pipeline_mode=pl.Buffered(1))
    row = lambda w: pl.BlockSpec((tm, w), lambda i: (i, 0))
    outs = pl.pallas_call(
        body,
        name="ffn_bwd",
        grid=(ni,),
        in_specs=[row(d), row(d), pl.BlockSpec((1, d), lambda i: (0, 0)), row(f), row(f),
                  resident(w_in), resident(w_out)] + [HBM_SPEC] * n_x,
        out_specs=[row(d), row(d), row(f), pl.BlockSpec((2, tm, f), lambda i: (0, i, 0)), row(d),
                   pl.BlockSpec((1, 1, d), lambda i: (i, 0, 0))] + [HBM_SPEC] * n_x,
        out_shape=[
            jax.ShapeDtypeStruct((t, d), F32),
            jax.ShapeDtypeStruct((t, d), BF16),
            jax.ShapeDtypeStruct((t, f), BF16),
            jax.ShapeDtypeStruct((2, t, f), BF16),
            jax.ShapeDtypeStruct((t, d), BF16),
            jax.ShapeDtypeStruct((ni, 1, d), F32),
        ] + (exchange.out_shape if exchange else []),
        scratch_shapes=exchange.scratch if exchange else [],
        compiler_params=_cparams(("arbitrary",) if exchange else ("parallel",)),
    )(dh_out, h, norm_w, g, u, w_in, w_out, *(exchange.ins if exchange else []))
    return outs[:6], outs[6:]


def _tn_matmul(a, b, name, exchange=None, row_tiles=None):
    t, k = a.shape
    split = b.ndim == 3
    n = 2 * b.shape[2] if split else b.shape[1]
    tk = _tile(k, 512, 128)
    tn = _tile(b.shape[-1], 1408, 128)
    per_half = b.shape[-1] // tn
    first, ni = row_tiles or (0, k // tk)
    nj = n // tn
    n_x = len(exchange.ins) if exchange else 0

    def body(*refs):
        a_ref, b_ref, o_ref = refs[0], refs[1], refs[2 + n_x]
        if exchange:
            i, j = pl.program_id(0), pl.program_id(1)
            at_end = (i == ni - 1) & (j == nj - 1)
            _host_exchange(exchange, refs[2:2 + n_x], refs[3 + n_x:3 + 2 * n_x], refs[3 + 2 * n_x:],
                           (i == 0) & (j == 0), at_end, at_end)
        o_ref[...] = lax.dot_general(a_ref[...], b_ref[...], TN_DIMS, preferred_element_type=F32)

    if split:
        b_spec = pl.BlockSpec((None, t, tn), lambda i, j: (j // per_half, 0, j % per_half))
    else:
        b_spec = pl.BlockSpec((t, tn), lambda i, j: (0, j))
    outs = pl.pallas_call(
        body,
        name=name,
        grid=(ni, nj),
        in_specs=[pl.BlockSpec((t, tk), lambda i, j: (0, first + i)), b_spec] + [HBM_SPEC] * n_x,
        out_specs=[pl.BlockSpec((tk, tn), lambda i, j: (i, j))] + [HBM_SPEC] * n_x,
        out_shape=[jax.ShapeDtypeStruct((ni * tk, n), F32)] + (exchange.out_shape if exchange else []),
        scratch_shapes=exchange.scratch if exchange else [],
        compiler_params=_cparams(("arbitrary", "arbitrary") if exchange else ("parallel", "parallel")),
    )(a, b, *(exchange.ins if exchange else []))
    return (outs[0], outs[1:]) if exchange else outs[0]


A_SLOT = lambda h: h // A_GROUP
B_SLOT = lambda h: h % 2

PROJ_PARTS = (
    (OFF_QA, A_WIDTH, A_PAD_WIDTH, True, A_SLOT), (OFF_KA, A_KV_WIDTH, A_KV_WIDTH, True, None),
    (OFF_VA, A_KV_WIDTH, A_KV_WIDTH, True, None), (OFF_QB, B_WIDTH, B_WIDTH, True, None),
    (OFF_KB, B_WIDTH, B_PAD_WIDTH, True, B_SLOT), (OFF_VB, B_WIDTH, B_PAD_WIDTH, True, B_SLOT),
    (OFF_GA, D_MODEL, D_MODEL, False, None), (OFF_GB, D_MODEL, D_MODEL, False, None), (OFF_F, F_COLS, F_COLS, False, None),
)


def _head_tile(pair, head, slot):
    lane_slot = lax.broadcasted_iota(jnp.int32, pair.shape, 1) // HEAD_DIM
    moved = pair if head % 2 == slot else pltpu.roll(pair, HEAD_DIM, 1)
    return jnp.where(lane_slot == slot, moved, 0.0)


def _proj_fwd(h, norm_w, w_p):
    t, d = h.shape
    tm = _tile(t, 272)

    def body(h_ref, nw_ref, w_ref, u_ref, *part_refs):
        hh = h_ref[...]
        un = ((hh * _rms_scale(hh)) * nw_ref[...]).astype(BF16)
        u_ref[...] = un
        for (off, width, _, _, slot), p_ref in zip(PROJ_PARTS, part_refs):
            if slot is None:
                p_ref[...] = jnp.dot(un, w_ref[:, off:off + width], preferred_element_type=F32).astype(p_ref.dtype)
                continue
            part = jnp.dot(un, w_ref[:, off:off + width], preferred_element_type=F32)
            for pair in range(width // LANES):
                x = part[:, pair * LANES:(pair + 1) * LANES]
                for head in (2 * pair, 2 * pair + 1):
                    p_ref[:, head * LANES:(head + 1) * LANES] = _head_tile(x, head, slot(head)).astype(p_ref.dtype)

    row = lambda w: pl.BlockSpec((tm, w), lambda i: (i, 0))
    return pl.pallas_call(
        body,
        name="proj_fwd",
        grid=(t // tm,),
        in_specs=[row(d), pl.BlockSpec((1, d), lambda i: (0, 0)),
                  pl.BlockSpec(w_p.shape, lambda i: (0, 0), pipeline_mode=pl.Buffered(1))],
        out_specs=[row(d)] + [row(width) for _, _, width, _, _ in PROJ_PARTS],
        out_shape=[jax.ShapeDtypeStruct((t, d), BF16)]
        + [jax.ShapeDtypeStruct((t, width), BF16 if is_bf else F32) for _, _, width, is_bf, _ in PROJ_PARTS],
        compiler_params=_cparams(("parallel",)),
    )(h, norm_w, w_p)


def _proj_bwd(dh_out, h, norm_w, dproj, w_p, exchange=None):
    t, d = h.shape
    tm = _tile(t, 272)
    ni = t // tm
    n_p = len(P_PIECES)
    n_in = 4 + n_p
    n_x = len(exchange.ins) if exchange else 0

    def body(*refs):
        dho_ref, h_ref, nw_ref = refs[:3]
        dp_refs, w_ref = refs[3:3 + n_p], refs[3 + n_p]
        dhin_ref, dnw_ref = refs[n_in + n_x:n_in + 2 + n_x]
        if exchange:
            i = pl.program_id(0)
            _host_exchange(exchange, refs[n_in:n_in + n_x], refs[n_in + 2 + n_x:n_in + 2 + 2 * n_x],
                           refs[n_in + 2 + 2 * n_x:], i == 0, i == ni - 1, i == ni - 1)
        dn = None
        for dp_ref, (off, width) in zip(dp_refs, P_PIECES):
            part = lax.dot_general(dp_ref[...], w_ref[:, off:off + width], NT_DIMS, preferred_element_type=F32)
            dn = part if dn is None else dn + part
        dh, dw = _rms_bwd(dn, h_ref[...], nw_ref[...])
        dhin_ref[...] = dho_ref[...] + dh
        dnw_ref[0] = dw

    row = lambda w: pl.BlockSpec((tm, w), lambda i: (i, 0))
    outs = pl.pallas_call(
        body,
        name="proj_bwd",
        grid=(ni,),
        in_specs=[row(d), row(d), pl.BlockSpec((1, d), lambda i: (0, 0))] + [row(width) for _, width in P_PIECES]
        + [pl.BlockSpec(w_p.shape, lambda i: (0, 0), pipeline_mode=pl.Buffered(1))] + [HBM_SPEC] * n_x,
        out_specs=[row(d), pl.BlockSpec((1, 1, d), lambda i: (i, 0, 0))] + [HBM_SPEC] * n_x,
        out_shape=[jax.ShapeDtypeStruct((t, d), F32), jax.ShapeDtypeStruct((ni, 1, d), F32)]
        + (exchange.out_shape if exchange else []),
        scratch_shapes=exchange.scratch if exchange else [],
        compiler_params=_cparams(("arbitrary",) if exchange else ("parallel",)),
    )(dh_out, h, norm_w, *dproj, w_p, *(exchange.ins if exchange else []))
    return outs[:2], outs[2:]


def _merge_fwd(h, oa, ob, ga, gb, wa, wb, wo):
    t, d = h.shape
    tm = _tile(t, 544)

    def body(h_ref, oa_ref, ob_ref, ga_ref, gb_ref, wa_ref, wb_ref, wo_ref, hout_ref, mix_ref):
        ya = jnp.dot(oa_ref[...], wa_ref[...], preferred_element_type=F32)
        yb = jnp.dot(ob_ref[...], wb_ref[...], preferred_element_type=F32)
        mixed = (jax.nn.sigmoid(ga_ref[...]) * ya + jax.nn.sigmoid(gb_ref[...]) * yb).astype(BF16)
        mix_ref[...] = mixed
        hout_ref[...] = h_ref[...] + jnp.dot(mixed, wo_ref[...], preferred_element_type=F32)

    row = lambda w: pl.BlockSpec((tm, w), lambda i: (i, 0))
    full = lambda a: pl.BlockSpec(a.shape, lambda i: (0, 0))
    return pl.pallas_call(
        body,
        name="merge_fwd",
        grid=(t // tm,),
        in_specs=[row(d), row(oa.shape[1]), row(ob.shape[1]), row(d), row(d), full(wa), full(wb), full(wo)],
        out_specs=[row(d), row(d)],
        out_shape=[jax.ShapeDtypeStruct((t, d), F32), jax.ShapeDtypeStruct((t, d), BF16)],
        compiler_params=_cparams(("parallel",)),
    )(h, oa, ob, ga, gb, wa, wb, wo)


def _merge_bwd(dh, oa, ob, ga, gb, wa, wb, wo, exchange=None):
    t, d = dh.shape
    tm = _tile(t, 544)
    ni = t // tm
    n_x = len(exchange.ins) if exchange else 0

    def body(*refs):
        dh_ref, oa_ref, ob_ref, ga_ref, gb_ref, wa_ref, wb_ref, wo_ref = refs[:8]
        dya_ref, dyb_ref, doa_ref, dob_ref, dg_ref, dhb_ref = refs[8 + n_x:14 + n_x]
        if exchange:
            i = pl.program_id(0)
            _host_exchange(exchange, refs[8:8 + n_x], refs[14 + n_x:14 + 2 * n_x], refs[14 + 2 * n_x:],
                           i == 0, i == ni - 1, i == ni - 1)
        dhb = dh_ref[...].astype(BF16)
        dhb_ref[...] = dhb
        dmix = lax.dot_general(dhb, wo_ref[...], NT_DIMS, preferred_element_type=F32)
        for branch, (o_ref, g_ref, w_ref, dy_ref, do_ref) in enumerate((
                (oa_ref, ga_ref, wa_ref, dya_ref, doa_ref),
                (ob_ref, gb_ref, wb_ref, dyb_ref, dob_ref))):
            y = jnp.dot(o_ref[...], w_ref[...], preferred_element_type=F32)
            s = jax.nn.sigmoid(g_ref[...])
            dy = (dmix * s).astype(BF16)
            dy_ref[...] = dy
            dg_ref[:, branch * d:(branch + 1) * d] = ((dmix * y) * (s * (1.0 - s))).astype(BF16)
            do_ref[...] = lax.dot_general(dy, w_ref[...], NT_DIMS, preferred_element_type=F32).astype(BF16)

    row = lambda w: pl.BlockSpec((tm, w), lambda i: (i, 0))
    full = lambda a: pl.BlockSpec(a.shape, lambda i: (0, 0))
    wa_w, wb_w = oa.shape[1], ob.shape[1]
    outs = pl.pallas_call(
        body,
        name="merge_bwd",
        grid=(ni,),
        in_specs=[row(d), row(wa_w), row(wb_w), row(d), row(d), full(wa), full(wb), full(wo)] + [HBM_SPEC] * n_x,
        out_specs=[row(d), row(d), row(wa_w), row(wb_w), row(2 * d), row(d)] + [HBM_SPEC] * n_x,
        out_shape=[
            jax.ShapeDtypeStruct((t, d), BF16), jax.ShapeDtypeStruct((t, d), BF16),
            jax.ShapeDtypeStruct((t, wa_w), BF16), jax.ShapeDtypeStruct((t, wb_w), BF16),
            jax.ShapeDtypeStruct((t, 2 * d), BF16), jax.ShapeDtypeStruct((t, d), BF16),
        ] + (exchange.out_shape if exchange else []),
        scratch_shapes=exchange.scratch if exchange else [],
        compiler_params=_cparams(("arbitrary",) if exchange else ("parallel",)),
    )(dh, oa, ob, ga, gb, wa, wb, wo, *(exchange.ins if exchange else []))
    return outs[:6], outs[6:]


def _tri_dot(tri, x):
    hi = x.astype(BF16)
    r1 = x - hi.astype(F32)
    mid = r1.astype(BF16)
    lo = (r1 - mid.astype(F32)).astype(BF16)
    return (jnp.dot(tri, hi, preferred_element_type=F32)
            + jnp.dot(tri, mid, preferred_element_type=F32)
            + jnp.dot(tri, lo, preferred_element_type=F32))


def _forget_cumsum(f_logit, b_pad, nb):
    t, w = f_logit.shape
    bsz = t // (nb * BLOCK)

    def body(f_ref, b_ref, c_ref, carry):
        @pl.when(pl.program_id(0) == 0)
        def _():
            carry[...] = jnp.zeros_like(carry)

        rows = lax.broadcasted_iota(jnp.int32, (BLOCK, BLOCK), 0)
        cols = lax.broadcasted_iota(jnp.int32, (BLOCK, BLOCK), 1)
        tri = (cols <= rows).astype(BF16)
        for b in range(bsz):
            x = jax.nn.log_sigmoid(f_ref[b] + b_ref[...])
            c = _tri_dot(tri, x) + carry[b]
            c_ref[b] = c
            carry[b] = c[BLOCK - 1:BLOCK, :]

    block = pl.BlockSpec((bsz, BLOCK, w), lambda n: (0, n, 0))
    return pl.pallas_call(
        body,
        name="forget_cumsum",
        grid=(nb,),
        in_specs=[block, pl.BlockSpec((1, w), lambda n: (0, 0))],
        out_specs=block,
        out_shape=jax.ShapeDtypeStruct((bsz, nb * BLOCK, w), F32),
        scratch_shapes=[pltpu.VMEM((bsz, 1, w), F32)],
        compiler_params=_cparams(("arbitrary",)),
    )(f_logit.reshape(bsz, nb * BLOCK, w), b_pad).reshape(t, w)


def _forget_cumsum_bwd(dc, f_logit, b_pad, nb):
    t, w = f_logit.shape
    bsz = t // (nb * BLOCK)

    def body(dc_ref, f_ref, b_ref, df_ref, db_ref, carry):
        @pl.when(pl.program_id(0) == 0)
        def _():
            carry[...] = jnp.zeros_like(carry)
            db_ref[...] = jnp.zeros_like(db_ref)

        rows = lax.broadcasted_iota(jnp.int32, (BLOCK, BLOCK), 0)
        cols = lax.broadcasted_iota(jnp.int32, (BLOCK, BLOCK), 1)
        tri = (cols >= rows).astype(BF16)
        for b in range(bsz):
            dlf = _tri_dot(tri, dc_ref[b]) + carry[b]
            carry[b] = dlf[0:1, :]
            df = dlf * jax.nn.sigmoid(-(f_ref[b] + b_ref[...]))
            df_ref[b] = df.astype(BF16)
            db_ref[b] += jnp.sum(df, axis=0, keepdims=True)

    l = nb * BLOCK
    rev = pl.BlockSpec((bsz, BLOCK, w), lambda n: (0, nb - 1 - n, 0))
    df, db = pl.pallas_call(
        body,
        name="forget_cumsum_bwd",
        grid=(nb,),
        in_specs=[rev, rev, pl.BlockSpec((1, w), lambda n: (0, 0))],
        out_specs=[rev, pl.BlockSpec((bsz, 1, w), lambda n: (0, 0, 0))],
        out_shape=[jax.ShapeDtypeStruct((bsz, l, w), BF16), jax.ShapeDtypeStruct((bsz, 1, w), F32)],
        scratch_shapes=[pltpu.VMEM((bsz, 1, w), F32)],
        compiler_params=_cparams(("arbitrary",)),
    )(dc.reshape(bsz, l, w), f_logit.reshape(bsz, l, w), b_pad)
    return df.reshape(t, w), db


GROUP_ROWS = A_GROUP * BLOCK


def _stack_heads(ref, g):
    return jnp.concatenate([ref[:, (A_GROUP * g + i) * LANES:(A_GROUP * g + i + 1) * LANES] for i in range(A_GROUP)],
                           axis=0)


def _unstack_heads(ref, g, x):
    for i in range(A_GROUP):
        ref[:, (A_GROUP * g + i) * LANES:(A_GROUP * g + i + 1) * LANES] = x[i * BLOCK:(i + 1) * BLOCK].astype(ref.dtype)


def _swa_logits(qk, slope, n):
    qi = lax.broadcasted_iota(jnp.int32, (GROUP_ROWS, BLOCK), 0) & (BLOCK - 1)
    kj = lax.broadcasted_iota(jnp.int32, (GROUP_ROWS, BLOCK), 1)
    s_all = qk * SCALE
    out = []
    for i, (dist, ok) in enumerate((
            (n * BLOCK + qi - kj, (kj >= N_PAD) & (n * BLOCK + qi - kj >= 0)),
            (BLOCK + qi - kj, (kj > qi) & (n >= 2)),
            (qi - kj, (kj <= qi) & (n >= 1)))):
        s = s_all[:, i * BLOCK:(i + 1) * BLOCK] - slope * dist.astype(F32)
        out.append(jnp.where(ok, s, NEG))
    return out


def _three_blocks(m_ref, p_ref, c_ref):
    return jnp.concatenate([m_ref[...], p_ref[...], c_ref[...]], axis=0)


def _swa_specs(bsz, nb):
    qspec = pl.BlockSpec((bsz, BLOCK, A_PAD_WIDTH), lambda n: (0, n, 0))
    kv_m = pl.BlockSpec((bsz, BLOCK, LANES), lambda n: (0, 0, 0))
    kv_p = pl.BlockSpec((bsz, BLOCK, LANES), lambda n: (0, jnp.maximum(n - 1, 0), 0))
    kv_c = pl.BlockSpec((bsz, BLOCK, LANES), lambda n: (0, n, 0))
    rowspec = pl.BlockSpec((A_KV_HEADS, GROUP_ROWS, 1), lambda n: (0, 0, 0))
    lsespec = pl.BlockSpec((bsz, 1, A_KV_HEADS, GROUP_ROWS, 1), lambda n: (0, n, 0, 0, 0))
    return qspec, kv_m, kv_p, kv_c, rowspec, lsespec


def _swa_fwd(q, k, v, sink_rows, slope_rows, nb):
    t = q.shape[0]
    l = nb * BLOCK
    bsz = t // l

    def body(q_ref, km_ref, kp_ref, kc_ref, vm_ref, vp_ref, vc_ref, sink_ref, slope_ref, o_ref, lse_ref):
        n = pl.program_id(0)
        lane_group = lax.broadcasted_iota(jnp.int32, (GROUP_ROWS, LANES), 1) // HEAD_DIM
        products = {}
        for b in range(bsz):
            keys = _three_blocks(km_ref.at[b], kp_ref.at[b], kc_ref.at[b])
            for g in range(A_KV_HEADS):
                products[b, g] = lax.dot_general(_stack_heads(q_ref.at[b], g), keys, NT_DIMS,
                                                 preferred_element_type=F32)
        for b in range(bsz):
            values = _three_blocks(vm_ref.at[b], vp_ref.at[b], vc_ref.at[b])
            for g in range(A_KV_HEADS):
                sink = sink_ref[g]
                s_m, s_p, s_c = _swa_logits(products[b, g], slope_ref[g], n)
                m = jnp.maximum(jnp.max(jnp.maximum(jnp.maximum(s_m, s_p), s_c), axis=-1, keepdims=True), sink)
                m_wide = jnp.broadcast_to(m, (GROUP_ROWS, BLOCK))
                e_m = jnp.exp(s_m - m_wide)
                e_p = jnp.exp(s_p - m_wide)
                e_c = jnp.exp(s_c - m_wide)
                z = jnp.sum((e_m + e_p) + e_c, axis=-1, keepdims=True) + jnp.exp(sink - m)
                inv = jnp.broadcast_to(1.0 / z, (GROUP_ROWS, BLOCK))
                probs = jnp.concatenate([(e_m * inv).astype(BF16), (e_p * inv).astype(BF16),
                                         (e_c * inv).astype(BF16)], axis=1)
                o = jnp.dot(probs, values, preferred_element_type=F32)
                _unstack_heads(o_ref.at[b], g, jnp.where(lane_group == g, o, 0.0))
                lse_ref[b, 0, g] = m + jnp.log(z)

    qspec, kv_m, kv_p, kv_c, rowspec, lsespec = _swa_specs(bsz, nb)
    by_example = lambda a: a.reshape(bsz, l, a.shape[1])
    q3, k3, v3 = by_example(q), by_example(k), by_example(v)
    o, lse = pl.pallas_call(
        body,
        name="swa_fwd",
        grid=(nb,),
        in_specs=[qspec, kv_m, kv_p, kv_c, kv_m, kv_p, kv_c, rowspec, rowspec],
        out_specs=[qspec, lsespec],
        out_shape=[jax.ShapeDtypeStruct((bsz, l, A_PAD_WIDTH), BF16),
                   jax.ShapeDtypeStruct((bsz, nb, A_KV_HEADS, GROUP_ROWS, 1), F32)],
        compiler_params=_cparams(("arbitrary",)),
    )(q3, k3, k3, k3, v3, v3, v3, sink_rows, slope_rows)
    return o.reshape(t, A_PAD_WIDTH), lse


def _swa_bwd(q, k, v, do, lse, sink_rows, slope_rows, nb):
    t = q.shape[0]
    l = nb * BLOCK
    bsz = t // l

    def body(q_ref, km_ref, kp_ref, kc_ref, vm_ref, vp_ref, vc_ref, do_ref, lse_ref, sink_ref, slope_ref,
             dq_ref, dk_ref, dv_ref, dsink_ref, dk_acc, dv_acc):
        n = pl.program_id(0)

        @pl.when(n == 0)
        def _():
            dk_acc[...] = jnp.zeros_like(dk_acc)
            dv_acc[...] = jnp.zeros_like(dv_acc)
            dsink_ref[...] = jnp.zeros_like(dsink_ref)

        first_half = lax.broadcasted_iota(jnp.int32, (BLOCK, LANES), 1) < HEAD_DIM
        prev = jnp.maximum(n - 1, 0)
        products = {}
        for b in range(bsz):
            keys = _three_blocks(km_ref.at[b], kp_ref.at[b], kc_ref.at[b])
            values = _three_blocks(vm_ref.at[b], vp_ref.at[b], vc_ref.at[b])
            for g in range(A_KV_HEADS):
                products[b, g] = (
                    lax.dot_general(_stack_heads(q_ref.at[b], g), keys, NT_DIMS, preferred_element_type=F32),
                    lax.dot_general(_stack_heads(do_ref.at[b], g), values, NT_DIMS, preferred_element_type=F32))
        for b in range(bsz):
            keys = _three_blocks(km_ref.at[b], kp_ref.at[b], kc_ref.at[b])
            for g in range(A_KV_HEADS):
                qq = _stack_heads(q_ref.at[b], g)
                dob = _stack_heads(do_ref.at[b], g)
                lse_g = lse_ref[b, 0, g]
                lse_wide = jnp.broadcast_to(lse_g, (GROUP_ROWS, BLOCK))
                qk, dp_all = products[b, g]
                probs = [jnp.exp(s - lse_wide) for s in _swa_logits(qk, slope_ref[g], n)]
                dps = [dp_all[:, i * BLOCK:(i + 1) * BLOCK] for i in range(3)]
                delta = jnp.sum((probs[0] * dps[0] + probs[1] * dps[1]) + probs[2] * dps[2], axis=-1, keepdims=True)
                delta_wide = jnp.broadcast_to(delta, (GROUP_ROWS, BLOCK))
                ds = jnp.concatenate([(p * (dp - delta_wide)).astype(BF16) for p, dp in zip(probs, dps)], axis=1)
                pb = jnp.concatenate([p.astype(BF16) for p in probs], axis=1)
                dq = jnp.dot(ds, keys, preferred_element_type=F32) * SCALE
                dk_all = lax.dot_general(ds, qq, TN_DIMS, preferred_element_type=F32) * SCALE
                dv_all = lax.dot_general(pb, dob, TN_DIMS, preferred_element_type=F32)
                for i, start in enumerate((0, prev * BLOCK, n * BLOCK)):
                    rows = pl.ds(pl.multiple_of(start, BLOCK), BLOCK)
                    dk_acc[b, rows, :] += dk_all[i * BLOCK:(i + 1) * BLOCK]
                    dv_acc[b, rows, :] += dv_all[i * BLOCK:(i + 1) * BLOCK]
                for pair in range(A_GROUP // 2):
                    even = dq[2 * pair * BLOCK:(2 * pair + 1) * BLOCK]
                    odd = dq[(2 * pair + 1) * BLOCK:(2 * pair + 2) * BLOCK]
                    left = even if g == 0 else pltpu.roll(even, HEAD_DIM, 1)
                    right = pltpu.roll(odd, HEAD_DIM, 1) if g == 0 else odd
                    tile = (A_GROUP // 2) * g + pair
                    dq_ref[b, :, tile * LANES:(tile + 1) * LANES] = jnp.where(first_half, left, right).astype(BF16)
                dsink_ref[b, g] += -(jnp.exp(sink_ref[g] - lse_g) * delta)

        @pl.when(n == nb - 1)
        def _():
            dk_ref[...] = dk_acc[...].astype(BF16)
            dv_ref[...] = dv_acc[...].astype(BF16)

    qspec, kv_m, kv_p, kv_c, rowspec, lsespec = _swa_specs(bsz, nb)
    kv_all = pl.BlockSpec((bsz, l, LANES), lambda n: (0, 0, 0))
    by_example = lambda a: a.reshape(bsz, l, a.shape[1])
    q3, k3, v3 = by_example(q), by_example(k), by_example(v)
    dq, dk, dv, dsink = pl.pallas_call(
        body,
        name="swa_bwd",
        grid=(nb,),
        in_specs=[qspec, kv_m, kv_p, kv_c, kv_m, kv_p, kv_c, qspec, lsespec, rowspec, rowspec],
        out_specs=[pl.BlockSpec((bsz, BLOCK, A_WIDTH), lambda n: (0, n, 0)), kv_all, kv_all,
                   pl.BlockSpec((bsz, A_KV_HEADS, GROUP_ROWS, 1), lambda n: (0, 0, 0, 0))],
        out_shape=[jax.ShapeDtypeStruct((bsz, l, A_WIDTH), BF16),
                   jax.ShapeDtypeStruct((bsz, l, LANES), BF16),
                   jax.ShapeDtypeStruct((bsz, l, LANES), BF16),
                   jax.ShapeDtypeStruct((bsz, A_KV_HEADS, GROUP_ROWS, 1), F32)],
        scratch_shapes=[pltpu.VMEM((bsz, l, LANES), F32), pltpu.VMEM((bsz, l, LANES), F32)],
        compiler_params=_cparams(("arbitrary",)),
    )(q3, k3, k3, k3, v3, v3, v3, by_example(do), lse, sink_rows, slope_rows)
    return dq.reshape(t, A_WIDTH), dk.reshape(t, LANES), dv.reshape(t, LANES), dsink


CHUNK = KEY_BLOCKS * BLOCK


def _fox_chunk(qb, ci):
    sb = jnp.maximum(jnp.minimum(KEY_BLOCKS * ci, qb + 1 - KEY_BLOCKS), 0)
    lo = jnp.maximum(ci * CHUNK, N_PAD)
    return sb, lo, pl.ds(pl.multiple_of(sb * BLOCK, BLOCK), CHUNK)


def _fox_logits(s_ref, cr_ref, e, j, sb, lo, qb):
    lane = lax.broadcasted_iota(jnp.int32, (BLOCK, BLOCK), 1)
    ahead = lane - lax.broadcasted_iota(jnp.int32, (BLOCK, BLOCK), 0)
    first = (sb + j) * BLOCK
    s = s_ref[e, :, j * BLOCK:(j + 1) * BLOCK] - cr_ref[e, sb + j]
    return jnp.where((ahead <= qb * BLOCK - first) & (lane >= lo - first), s, NEG)


FOX_PAIRS = 4
FOX_HEADS = 2 * FOX_PAIRS
FOX_STEPS = B_PAIRS // FOX_PAIRS


def _fox_specs(nb):
    l = nb * BLOCK
    q_spec = pl.BlockSpec((BLOCK, FOX_PAIRS * LANES), lambda b, p, i: (b * nb + i, p))
    kv_spec = pl.BlockSpec((l, FOX_HEADS * LANES), lambda b, p, i: (b, p))
    cc_spec = pl.BlockSpec((FOX_HEADS, BLOCK, 1), lambda b, p, i: (b * FOX_STEPS + p, i, 0))
    cr_spec = pl.BlockSpec((FOX_HEADS, nb, 1, BLOCK), lambda b, p, i: (b * FOX_STEPS + p, 0, 0, 0))
    return q_spec, kv_spec, cc_spec, cr_spec


def _fox_fwd(q, k, v, c_row, nb, exchange=None):
    t = q.shape[0]
    bsz = t // (nb * BLOCK)
    assert nb >= KEY_BLOCKS

    n_x = len(exchange.ins) if exchange else 0

    def body(*refs):
        q_ref, k_ref, v_ref, cr_ref = refs[:4]
        o_ref, ox_ref, lse_ref = refs[4 + n_x:7 + n_x]
        s_scr, hi_scr, lo_scr = refs[7 + 2 * n_x:10 + 2 * n_x]
        qb = pl.program_id(2)
        if exchange:
            first = (pl.program_id(0) == 0) & (pl.program_id(1) == 0)
            last = (pl.program_id(0) == bsz - 1) & (pl.program_id(1) == FOX_STEPS - 1)
            _host_exchange(exchange, refs[4:4 + n_x], refs[7 + n_x:7 + 2 * n_x], refs[10 + 2 * n_x:],
                           first & (qb == 0), first & (qb == 2 * nb // 3), last & (qb == nb - 1),
                           late=last & (qb == 0))
        qs = [q_ref[:, a * LANES:(a + 1) * LANES] * SCALE for a in range(FOX_PAIRS)]
        first_half = lax.broadcasted_iota(jnp.int32, (BLOCK, LANES), 1) < HEAD_DIM

        def step(ci, carry):
            stats, accs = carry[:2 * FOX_HEADS], carry[2 * FOX_HEADS:]
            sb, lo, krows = _fox_chunk(qb, ci)
            for e in range(FOX_HEADS):
                s_scr[e] = lax.dot_general(qs[e // 2], k_ref[krows, e * LANES:(e + 1) * LANES], NT_DIMS,
                                           preferred_element_type=F32)
            new_stats, new_accs = [], []
            for a in range(FOX_PAIRS):
                alphas = []
                pv = jnp.zeros((BLOCK, LANES), F32)
                pv_lo = jnp.zeros((BLOCK, LANES), F32)
                for e in (2 * a, 2 * a + 1):
                    m, z = stats[2 * e], stats[2 * e + 1]
                    tile = slice(e * LANES, (e + 1) * LANES)
                    top = None
                    for j in range(KEY_BLOCKS):
                        s = _fox_logits(s_scr, cr_ref, e, j, sb, lo, qb)
                        s_scr[e, :, j * BLOCK:(j + 1) * BLOCK] = s
                        top = s if top is None else jnp.maximum(top, s)
                    m_new = jnp.maximum(m, jnp.max(top, axis=-1, keepdims=True))
                    alpha = jnp.exp(m - m_new)
                    m_wide = jnp.broadcast_to(m_new, (BLOCK, BLOCK))
                    total = None
                    for j in range(KEY_BLOCKS):
                        cols = slice(j * BLOCK, (j + 1) * BLOCK)
                        p = jnp.exp(s_scr[e, :, cols] - m_wide)
                        total = p if total is None else total + p
                        hi = p.astype(BF16)
                        hi_scr[e, :, cols] = hi
                        lo_scr[e, :, cols] = (p - hi.astype(F32)).astype(BF16)
                    z = alpha * z + jnp.sum(total, axis=-1, keepdims=True)
                    vv = v_ref[krows, tile]
                    pv = pv + jnp.dot(hi_scr[e], vv, preferred_element_type=F32)
                    pv_lo = pv_lo + jnp.dot(lo_scr[e], vv, preferred_element_type=F32)
                    new_stats += [m_new, z]
                    alphas.append(alpha)
                alpha = jnp.where(first_half, alphas[0], alphas[1])
                new_accs += [alpha * accs[2 * a] + pv, alpha * accs[2 * a + 1] + pv_lo]
            return (*new_stats, *new_accs)

        col = lambda val: jnp.full((BLOCK, 1), val, F32)
        done = lax.fori_loop(
            0, (qb + KEY_BLOCKS) // KEY_BLOCKS, step,
            (col(NEG), col(0.0)) * FOX_HEADS + (jnp.zeros((BLOCK, LANES), F32),) * (2 * FOX_PAIRS))
        for a in range(FOX_PAIRS):
            m0, z0, m1, z1 = done[4 * a:4 * a + 4]
            acc, acc_lo = done[2 * FOX_HEADS + 2 * a:2 * FOX_HEADS + 2 * a + 2]
            inv = 1.0 / jnp.where(first_half, z0, z1)
            tile = slice(a * LANES, (a + 1) * LANES)
            o_ref[:, tile] = (acc * inv).astype(BF16)
            ox_ref[:, tile] = (acc + acc_lo) * inv
            lse_ref[2 * a] = m0 + jnp.log(z0)
            lse_ref[2 * a + 1] = m1 + jnp.log(z1)

    q_spec, kv_spec, cc_spec, cr_spec = _fox_specs(nb)
    outs = pl.pallas_call(
        body,
        name="fox_fwd",
        grid=(bsz, FOX_STEPS, nb),
        in_specs=[q_spec, kv_spec, kv_spec, cr_spec] + [HBM_SPEC] * n_x,
        out_specs=[q_spec, q_spec, cc_spec] + [HBM_SPEC] * n_x,
        out_shape=[jax.ShapeDtypeStruct((t, B_WIDTH), BF16), jax.ShapeDtypeStruct((t, B_WIDTH), F32),
                   jax.ShapeDtypeStruct((bsz * B_HEADS, nb * BLOCK, 1), F32)] + (exchange.out_shape if exchange else []),
        scratch_shapes=[pltpu.VMEM((FOX_HEADS, BLOCK, CHUNK), F32), pltpu.VMEM((FOX_HEADS, BLOCK, CHUNK), BF16),
                        pltpu.VMEM((FOX_HEADS, BLOCK, CHUNK), BF16)] + (exchange.scratch if exchange else []),
        compiler_params=_cparams(("arbitrary",) * 3 if exchange else ("parallel", "parallel", "arbitrary")),
    )(q, k, v, c_row, *(exchange.ins if exchange else []))
    return outs[:3], outs[3:]


def _fox_bwd(q, k, v, o_exact, do, lse, c_row, nb, exchange=None):
    t = q.shape[0]
    l = nb * BLOCK
    bsz = t // l

    n_x = len(exchange.ins) if exchange else 0

    def body(*refs):
        q_ref, k_ref, v_ref, ox_ref, do_ref, lse_ref, cr_ref = refs[:7]
        dq_ref, dk_ref, dv_ref, dc_ref = refs[7 + n_x:11 + n_x]
        dk_acc, dv_acc, s_scr, dp_scr, p_scr, ds_scr, dq_scr = refs[11 + 2 * n_x:18 + 2 * n_x]
        qb = pl.program_id(2)
        if exchange:
            first = (pl.program_id(0) == 0) & (pl.program_id(1) == 0)
            last = (pl.program_id(0) == bsz - 1) & (pl.program_id(1) == FOX_STEPS - 1)
            _host_exchange(exchange, refs[7:7 + n_x], refs[11 + n_x:11 + 2 * n_x], refs[18 + 2 * n_x:],
                           first & (qb == 0), last & (qb == 0), last & (qb == nb - 1))

        @pl.when(qb == 0)
        def _():
            dk_acc[...] = jnp.zeros_like(dk_acc)
            dv_acc[...] = jnp.zeros_like(dv_acc)
            dc_ref[...] = jnp.zeros_like(dc_ref)

        top_half = lax.broadcasted_iota(jnp.int32, (LANES, BLOCK), 0) < HEAD_DIM
        pair_t = lambda x: jnp.concatenate([jnp.where(top_half, x.T, 0), jnp.where(top_half, 0, x.T)], axis=1)
        first_half = lax.broadcasted_iota(jnp.int32, (BLOCK, LANES), 1) < HEAD_DIM
        wide = lambda col: jnp.broadcast_to(col, (BLOCK, BLOCK))
        qs, dobs, qs_t, dob_t, deltas = [], [], [], [], []
        for a in range(FOX_PAIRS):
            tile = slice(a * LANES, (a + 1) * LANES)
            qs.append(q_ref[:, tile] * SCALE)
            dobs.append(do_ref[:, tile])
            qs_t.append(pair_t(qs[a]))
            dob_t.append(pair_t(dobs[a]))
            weighted = dobs[a].astype(F32) * ox_ref[:, tile]
            deltas += [wide(jnp.sum(jnp.where(first_half, weighted, 0.0), axis=-1, keepdims=True)),
                       wide(jnp.sum(jnp.where(first_half, 0.0, weighted), axis=-1, keepdims=True))]
        lses = [wide(lse_ref[e]) for e in range(FOX_HEADS)]

        dq_scr[...] = jnp.zeros(dq_scr.shape, F32)

        def step(ci, carry):
            sb, lo, krows = _fox_chunk(qb, ci)
            for e in range(FOX_HEADS):
                tile = slice(e * LANES, (e + 1) * LANES)
                s_scr[e] = lax.dot_general(qs[e // 2], k_ref[krows, tile], NT_DIMS, preferred_element_type=F32)
                dp_scr[e] = lax.dot_general(dobs[e // 2], v_ref[krows, tile], NT_DIMS, preferred_element_type=F32)
            for a in range(FOX_PAIRS):
                for e in (2 * a, 2 * a + 1):
                    tile = slice(e * LANES, (e + 1) * LANES)
                    kk = k_ref[krows, tile]
                    for j in range(KEY_BLOCKS):
                        cols = slice(j * BLOCK, (j + 1) * BLOCK)
                        p = jnp.exp(_fox_logits(s_scr, cr_ref, e, j, sb, lo, qb) - lses[e])
                        ds = p * (dp_scr[e, :, cols] - deltas[e])
                        dc_ref[e, sb + j] -= jnp.sum(ds, axis=0, keepdims=True)
                        p_scr[e, :, cols] = p.astype(BF16)
                        ds_scr[e, :, cols] = ds.astype(BF16)
                    dq_scr[a] += jnp.dot(ds_scr[e], kk, preferred_element_type=F32)
                both = slice(2 * a, 2 * a + 2)
                dk_t = jnp.dot(qs_t[a], ds_scr[both].reshape(2 * BLOCK, CHUNK), preferred_element_type=F32)
                dv_t = jnp.dot(dob_t[a], p_scr[both].reshape(2 * BLOCK, CHUNK), preferred_element_type=F32)
                for j in range(KEY_BLOCKS):
                    cols = slice(j * BLOCK, (j + 1) * BLOCK)
                    dk_acc[a * nb + sb + j] += dk_t[:, cols]
                    dv_acc[a * nb + sb + j] += dv_t[:, cols]
            return carry

        lax.fori_loop(0, (qb + KEY_BLOCKS) // KEY_BLOCKS, step, 0)
        for a in range(FOX_PAIRS):
            dq_ref[:, a * LANES:(a + 1) * LANES] = (dq_scr[a] * SCALE).astype(BF16)

        @pl.when(qb == nb - 1)
        def _():
            for a in range(FOX_PAIRS):
                for kb in range(nb):
                    rows = slice(kb * BLOCK, (kb + 1) * BLOCK)
                    for acc, out_ref in ((dk_acc, dk_ref), (dv_acc, dv_ref)):
                        out_ref[rows, a * LANES:(a + 1) * LANES] = acc[a * nb + kb].T.astype(BF16)

    q_spec, kv_spec, cc_spec, cr_spec = _fox_specs(nb)
    dkv_spec = pl.BlockSpec((l, FOX_PAIRS * LANES), lambda b, p, i: (b, p))
    outs = pl.pallas_call(
        body,
        name="fox_bwd",
        grid=(bsz, FOX_STEPS, nb),
        in_specs=[q_spec, kv_spec, kv_spec, q_spec, q_spec, cc_spec, cr_spec] + [HBM_SPEC] * n_x,
        out_specs=[q_spec, dkv_spec, dkv_spec, cr_spec] + [HBM_SPEC] * n_x,
        out_shape=[jax.ShapeDtypeStruct((t, B_WIDTH), BF16), jax.ShapeDtypeStruct((t, B_WIDTH), BF16),
                   jax.ShapeDtypeStruct((t, B_WIDTH), BF16),
                   jax.ShapeDtypeStruct((bsz * B_HEADS, nb, 1, BLOCK), F32)] + (exchange.out_shape if exchange else []),
        scratch_shapes=[pltpu.VMEM((FOX_PAIRS * nb, LANES, BLOCK), F32), pltpu.VMEM((FOX_PAIRS * nb, LANES, BLOCK), F32),
                        pltpu.VMEM((FOX_HEADS, BLOCK, CHUNK), F32), pltpu.VMEM((FOX_HEADS, BLOCK, CHUNK), F32),
                        pltpu.VMEM((FOX_HEADS, BLOCK, CHUNK), BF16), pltpu.VMEM((FOX_HEADS, BLOCK, CHUNK), BF16),
                        pltpu.VMEM((FOX_PAIRS, BLOCK, LANES), F32)]
        + (exchange.scratch if exchange else []),
        compiler_params=_cparams(("arbitrary",) * 3 if exchange else ("parallel", "parallel", "arbitrary")),
    )(q, k, v, o_exact, do, lse, c_row, *(exchange.ins if exchange else []))
    return outs[:4], outs[4:]


def _loss_head(h, final_w, target):
    bsz, l, d = h.shape
    nb = l // BLOCK

    def body(h_ref, w_ref, t_ref, loss_ref, dh_ref, dw_ref):
        n = pl.program_id(0)

        @pl.when(n == 0)
        def _():
            loss_ref[...] = jnp.zeros_like(loss_ref)
            dw_ref[...] = jnp.zeros_like(dw_ref)
            dh_ref[...] = jnp.zeros_like(dh_ref)

        @pl.when(n > 0)
        def _():
            w = w_ref[...]
            for b in range(bsz):
                hh = h_ref[b]
                r = _rms_scale(hh)
                err = (hh * r) * w - t_ref[b]
                loss_ref[...] += 0.5 * jnp.sum(jnp.mean(err * err, axis=-1, keepdims=True), axis=0, keepdims=True)
                dy = err * (1.0 / d)
                dh, dw = _rms_bwd(dy, hh, w)
                dh_ref[b] = dh
                dw_ref[...] += dw

    return pl.pallas_call(
        body,
        name="loss_head",
        grid=(nb,),
        in_specs=[
            pl.BlockSpec((bsz, BLOCK, d), lambda n: (0, n, 0)),
            pl.BlockSpec((1, d), lambda n: (0, 0)),
            pl.BlockSpec((bsz, BLOCK, d), lambda n: (0, jnp.maximum(n - 1, 0), 0)),
        ],
        out_specs=[
            pl.BlockSpec((1, 128), lambda n: (0, 0)),
            pl.BlockSpec((bsz, BLOCK, d), lambda n: (0, n, 0)),
            pl.BlockSpec((1, d), lambda n: (0, 0)),
        ],
        out_shape=[jax.ShapeDtypeStruct((1, 128), F32), jax.ShapeDtypeStruct((bsz, l, d), F32),
                   jax.ShapeDtypeStruct((1, d), F32)],
        compiler_params=_cparams(("arbitrary",)),
    )(h, final_w, target)


def _pad_tiles(w, src, heads, lane_slot, axis):
    pieces = []
    for h in range(heads):
        x = lax.slice_in_dim(w, src + HEAD_DIM * h, src + HEAD_DIM * (h + 1), axis=axis)
        z = jnp.zeros_like(x)
        pieces += [x, z] if lane_slot(h) == 0 else [z, x]
    return pieces


def _unpad_tiles(g, off, heads, lane_slot, axis):
    return [lax.slice_in_dim(g, off + LANES * h + HEAD_DIM * lane_slot(h),
                             off + LANES * h + HEAD_DIM * (lane_slot(h) + 1), axis=axis) for h in range(heads)]


REF_RUNS = ((0, SRC_QB, 0, 0), (SRC_QB, SRC_F, 1, 0), (SRC_F, SRC_GA, 0, OFF_F), (SRC_GA, W_IN_COLS, 2, 0))
P_RUNS = ((0, OFF_F, 0), (OFF_F, OFF_F + B_HEADS, SRC_F), (OFF_QB, OFF_GA, SRC_QB), (OFF_GA, P_COLS, SRC_GA))
SHARD_COLS = W_IN_COLS // N_CHIPS
SHARD_PAD_COLS = -(-SHARD_COLS // LANES) * LANES


def _place(tile, lane, src_ref, src, dst, length):
    for t in range(src // LANES, (src + length - 1) // LANES + 1):
        x = src_ref[:, t * LANES:(t + 1) * LANES].astype(F32)
        shift = (dst - src) % LANES
        moved = pltpu.roll(x, shift, 1) if shift else x
        from_t = (lane >= max(dst, dst + t * LANES - src)) & (lane < min(dst + length, dst + (t + 1) * LANES - src))
        tile = jnp.where(from_t, moved, tile)
    return tile


def _layout_w_in(stacked):
    d = stacked.shape[1]
    rows = _tile(d, 256)

    def body(s_ref, o_ref):
        lane = lax.broadcasted_iota(jnp.int32, (rows, LANES), 1)
        for lo in range(0, P_COLS, LANES):
            tile = jnp.zeros((rows, LANES), F32)
            for first, end, ref_col in P_RUNS:
                start, stop = max(lo, first), min(lo + LANES, end)
                while start < stop:
                    k, src = divmod(ref_col + start - first, SHARD_COLS)
                    length = min(stop - start, SHARD_COLS - src)
                    tile = _place(tile, lane, s_ref.at[k], src, start - lo, length)
                    start += length
            o_ref[:, lo:lo + LANES] = tile.astype(o_ref.dtype)

    return pl.pallas_call(
        body,
        name="layout_w_in",
        grid=(d // rows,),
        in_specs=[pl.BlockSpec((N_CHIPS, rows, SHARD_PAD_COLS), lambda i: (0, i, 0))],
        out_specs=pl.BlockSpec((rows, P_COLS), lambda i: (i, 0)),
        out_shape=jax.ShapeDtypeStruct((d, P_COLS), stacked.dtype),
        compiler_params=_cparams(("parallel",)),
    )(stacked)


def _stack_w_in_grad(pieces):
    d = pieces[0].shape[0]
    rows = _tile(d, 256)

    def body(*refs):
        piece_refs, o_ref = refs[:-1], refs[-1]
        lane = lax.broadcasted_iota(jnp.int32, (rows, LANES), 1)
        for k in range(N_CHIPS):
            for j in range(0, SHARD_COLS, LANES):
                width = min(LANES, SHARD_COLS - j)
                lo = k * SHARD_COLS + j
                tile = jnp.zeros((rows, LANES), F32)
                for first, end, piece, at in REF_RUNS:
                    start, stop = max(lo, first), min(lo + width, end)
                    if start >= stop:
                        continue
                    tile = _place(tile, lane, piece_refs[piece], at + start - first, start - lo, stop - start)
                o_ref[k, :, j:j + width] = tile[:, :width]

    return pl.pallas_call(
        body,
        name="stack_w_in_grad",
        grid=(d // rows,),
        in_specs=[pl.BlockSpec((rows, p.shape[1]), lambda i: (i, 0)) for p in pieces],
        out_specs=pl.BlockSpec((N_CHIPS, rows, SHARD_COLS), lambda i: (0, i, 0)),
        out_shape=jax.ShapeDtypeStruct((N_CHIPS, d, SHARD_COLS), F32),
        compiler_params=_cparams(("parallel",)),
    )(*pieces)


def _local_step(x, target, meta, norms, b_forget, sinks, w, comm=None):
    n1, nmix, n2, nfin = norms
    w1i, w1o = w[:2]
    if meta is not None:
        x = jnp.concatenate([jnp.zeros((x.shape[0], N_PAD, x.shape[2]), F32),
                             jnp.broadcast_to(meta[None], (x.shape[0], N_META, x.shape[2])), x], axis=1)
    bsz, l, d = x.shape
    nb = l // BLOCK
    t = bsz * l
    h0 = x.reshape(t, d)

    if comm is None:
        (h1, g1, u1), _ = _ffn_fwd(h0, n1, w1i, w1o)
        w_in, wa, wb, wo, w2i, w2o = w[2:]
        w_in = jnp.pad(w_in.reshape(d, N_CHIPS, SHARD_COLS).transpose(1, 0, 2),
                       ((0, 0), (0, 0), (0, SHARD_PAD_COLS - SHARD_COLS)))
    else:
        (h1, g1, u1), (w_in,) = _ffn_fwd(h0, n1, w1i, w1o, comm.gather(GATHER_PROJ))
    wp = _layout_w_in(w_in)
    un, qa, ka, va, qb, kb, vb, ga, gb, f_logit = _proj_fwd(h1, nmix, wp)
    b_pad = jnp.concatenate([b_forget, jnp.zeros((1, F_COLS - B_HEADS), F32)], axis=1)
    c = _forget_cumsum(f_logit, b_pad, nb)
    c_heads = c[:, :B_HEADS].reshape(bsz, l, B_HEADS).transpose(0, 2, 1).reshape(bsz * B_HEADS, l)
    c_row = c_heads.reshape(bsz * B_HEADS, nb, 1, BLOCK)

    slopes = jnp.exp2(-8.0 * jnp.arange(1, A_HEADS + 1, dtype=F32) / A_HEADS)
    slope_rows = jnp.repeat(slopes.reshape(A_KV_HEADS, A_GROUP), BLOCK, axis=1)[:, :, None]
    sink_rows = jnp.repeat(sinks.reshape(A_KV_HEADS, A_GROUP), BLOCK, axis=1)[:, :, None]

    oa, lse_a = _swa_fwd(qa, ka, va, sink_rows, slope_rows, nb)
    if comm is None:
        (ob, ob_exact, lse_b), _ = _fox_fwd(qb, kb, vb, c_row, nb)
    else:
        (ob, ob_exact, lse_b), (wa, wb, wo, w2i, w2o) = _fox_fwd(qb, kb, vb, c_row, nb, comm.gather(GATHER_LATE))
    wa_p = jnp.concatenate(_pad_tiles(wa, 0, A_HEADS, A_SLOT, 0), axis=0)
    h2, mixed = _merge_fwd(h1, oa, ob, ga, gb, wa_p, wb, wo)
    (h3, g2, u2), _ = _ffn_fwd(h2, n2, w2i, w2o)
    loss, dh3, d_nfin = _loss_head(h3.reshape(bsz, l, d), nfin, target)

    (dh2, n2b, a2, dgu2, df2, dn2_parts), _ = _ffn_bwd(dh3.reshape(t, d), h2, n2, g2, u2, w2i, w2o)
    g_w2o = _tn_matmul(a2, df2, "grad_ffn2_w_out")
    g_w2i = _tn_matmul(n2b, dgu2, "grad_ffn2_w_in")

    hosted = comm.swap("ffn2", dict(ffn2_w_in=g_w2i, ffn2_w_out=g_w2o)) if comm else None
    (dya, dyb, doa, dob, dgates, dh2b), swapped = _merge_bwd(dh2, oa, ob, ga, gb, wa_p, wb, wo, hosted)
    g_wo = _tn_matmul(mixed, dh2b, "grad_w_out")
    g_wa = jnp.concatenate(_unpad_tiles(_tn_matmul(oa, dya, "grad_w_branch_a"), 0, A_HEADS, A_SLOT, 0), axis=0)
    g_wb = _tn_matmul(ob, dyb, "grad_w_branch_b")

    dqa, dka, dva, dsink_rows = _swa_bwd(qa, ka, va, doa, lse_a, sink_rows, slope_rows, nb)
    hosted = comm.scatter("ffn2", swapped) if comm else None
    (dqb, dkb, dvb, dc_row), pieces = _fox_bwd(qb, kb, vb, ob_exact, dob, lse_b, c_row, nb, hosted)
    if comm:
        comm.received("ffn2", pieces)
    dc = dc_row.reshape(bsz, B_HEADS, l).transpose(0, 2, 1).reshape(t, B_HEADS)
    dc = jnp.concatenate([dc, jnp.zeros((t, F_COLS - B_HEADS), F32)], axis=1)
    df_logit, db_parts = _forget_cumsum_bwd(dc, f_logit, b_pad, nb)

    dproj = (jnp.concatenate([dqa, dka, dva, df_logit], axis=1), jnp.concatenate([dqb, dkb, dvb], axis=1), dgates)
    g_win = _stack_w_in_grad([_tn_matmul(un, piece, "grad_w_in_" + tag) for piece, tag in zip(dproj, ("a", "b", "gates"))])
    if comm is None:
        g_win = g_win.transpose(1, 0, 2).reshape(d, W_IN_COLS)
    hosted = comm.swap("mixer", dict(w_in=g_win, w_branch_a=g_wa, w_branch_b=g_wb, w_out=g_wo)) if comm else None
    (dh1, dnmix_parts), swapped = _proj_bwd(dh2, h1, nmix, dproj, wp, hosted)
    hosted = comm.scatter("mixer", swapped) if comm else None
    (dh0, n1b, a1, dgu1, df1, dn1_parts), pieces = _ffn_bwd(dh1, h0, n1, g1, u1, w1i, w1o, hosted)
    dh0 = dh0.reshape(bsz, l, d)
    grad_x = dh0[:, PREFIX:]
    small = dict(
        meta_tokens=jnp.sum(dh0[:, N_PAD:PREFIX], axis=0),
        ffn1_norm=jnp.sum(dn1_parts, axis=0),
        mix_norm=jnp.sum(dnmix_parts, axis=0),
        ffn2_norm=jnp.sum(dn2_parts, axis=0),
        final_norm=d_nfin,
        b_forget=jnp.sum(db_parts, axis=0)[:, :B_HEADS],
        attn_sinks=jnp.sum(dsink_rows.reshape(bsz, A_HEADS, BLOCK), axis=(0, 2)).reshape(1, A_HEADS),
    )
    if comm is None:
        g_w1o = _tn_matmul(a1, df1, "grad_ffn1_w_out")
        g_w1i = _tn_matmul(n1b, dgu1, "grad_ffn1_w_in")
    else:
        comm.received("mixer", pieces)
        g_w1o, gathered = _tn_matmul(a1, df1, "grad_ffn1_w_out", comm.small_gather(loss, small))
        comm.small_gathered(gathered)
        swapped = _run_exchange(comm.swap("ffn1_out", dict(ffn1_w_out=g_w1o)), "exchange_halves_ffn1_out")
        top, pieces = _tn_matmul(n1b, dgu1, "grad_ffn1_w_in_top", comm.scatter("ffn1_out", swapped), row_tiles=(0, 1))
        comm.received("ffn1_out", pieces)
        swapped = _run_exchange(comm.swap("ffn1_in_top", dict(ffn1_w_in_top=top)), "exchange_halves_ffn1_in_top")
        g_w1i, pieces = _tn_matmul(n1b, dgu1, "grad_ffn1_w_in_bottom", comm.scatter("ffn1_in_top", swapped),
                                   row_tiles=(1, 1))
        comm.received("ffn1_in_top", pieces)
    big = dict(ffn1_w_in=g_w1i, ffn1_w_out=g_w1o, w_in=g_win, w_branch_a=g_wa, w_branch_b=g_wb,
               w_out=g_wo, ffn2_w_in=g_w2i, ffn2_w_out=g_w2o)
    return loss, grad_x, small, big


BIG = (
    ("ffn1_w_in", (D_MODEL, 5632), 1),
    ("ffn1_w_out", (2816, D_MODEL), 0),
    ("w_in", (D_MODEL, W_IN_COLS), 1),
    ("w_branch_a", (A_WIDTH, D_MODEL), 1),
    ("w_branch_b", (B_WIDTH, D_MODEL), 1),
    ("w_out", (D_MODEL, D_MODEL), 0),
    ("ffn2_w_in", (D_MODEL, 5632), 1),
    ("ffn2_w_out", (2816, D_MODEL), 0),
)
STACKED = "w_in"
SPLIT = (
    ("ffn1_w_in_top", (D_MODEL // 2, 5632), 1),
    ("ffn1_w_in_bottom", (D_MODEL // 2, 5632), 1),
)


def _coords():
    return lax.axis_index("x"), lax.axis_index("y"), lax.axis_index("c")


def _other_chips(x, y):
    return ((1 - x, y), (x, 1 - y), (1 - x, 1 - y))


def _chip_part(ref, name, shape, axis, k):
    if name == STACKED:
        return ref.at[k]
    size = shape[axis] // N_CHIPS
    start = pl.multiple_of(k * size, size)
    return ref.at[pl.ds(start, size), :] if axis == 0 else ref.at[:, pl.ds(start, size)]


def _full_shape(name, shape):
    return (N_CHIPS, shape[0], shape[1] // N_CHIPS) if name == STACKED else shape


class _Exchange:
    def __init__(self, ins, out_shape, n_sems, ops):
        self.ins, self.out_shape, self.n_sems, self.ops = list(ins), list(out_shape), n_sems, ops

    @property
    def scratch(self):
        return [pltpu.SemaphoreType.DMA((self.n_sems,)), pltpu.SemaphoreType.DMA((self.n_sems,))]


SEMS_PER_GATHER = 9


def _gather_exchange(shards, table):
    n = len(table)
    x_nbr, y_nbr, diagonal = 0, 1, 2

    def ops(ins, outs, send_sems, recv_sems):
        x, y, c = _coords()
        mine = 2 * x + y
        sibling = (x, y, 1 - c)
        chips = _other_chips(x, y)
        slots = [2 * chip[0] + chip[1] for chip in chips]

        def part(i, k):
            name, shape, axis = table[i][:3]
            return _chip_part(outs[i], name, shape, axis, k)

        def half(ref, h):
            rows = ref.shape[0] // 2
            return ref.at[pl.ds(pl.multiple_of(h * rows, rows), rows), :]

        def remote(i, sem, src, dst, device):
            sem = SEMS_PER_GATHER * i + sem
            return pltpu.make_async_remote_copy(src, dst, send_sems.at[sem], recv_sems.at[sem],
                                                device_id=device, device_id_type=MESH_ID)

        def own(i):
            return remote(i, 0, ins[i], part(i, mine), sibling)

        def fetch(i, j, slot):
            if table[i][4]:
                src, dst = half(ins[i], c), half(part(i, slot), c)
            else:
                src, dst = ins[i], part(i, slot)
            return remote(i, 1 + j, src, dst, (chips[j][0], chips[j][1], c))

        def relayed(i, via, of):
            region = half(half(part(i, slots[of]), c), via)
            return remote(i, 4 + via, region, region, (chips[via][0], chips[via][1], c))

        def forward(i, j, h):
            region = half(part(i, slots[j]), h)
            return remote(i, 6 + j, region, region, sibling)

        def start():
            for i in range(n):
                for j in (x_nbr, y_nbr) if table[i][4] else (x_nbr, y_nbr, diagonal):
                    fetch(i, j, mine).start()
            for i in range(n):
                own(i).start()

        def relay():
            for i in range(n):
                if not table[i][4]:
                    for j in range(3):
                        fetch(i, j, slots[j]).wait_recv()
                    continue
                fetch(i, y_nbr, slots[y_nbr]).wait_recv()
                relayed(i, x_nbr, y_nbr).start()
                forward(i, y_nbr, c).start()
                fetch(i, x_nbr, slots[x_nbr]).wait_recv()
                relayed(i, y_nbr, x_nbr).start()
                forward(i, x_nbr, c).start()

        def relay_diagonal():
            for i in range(n):
                if table[i][4]:
                    relayed(i, x_nbr, diagonal).wait_recv()
                    relayed(i, y_nbr, diagonal).wait_recv()
                    forward(i, diagonal, c).start()

        def finish():
            for i in range(n):
                own(i).wait()
                for j in range(3):
                    if table[i][4]:
                        forward(i, j, 1 - c).wait_recv()
                        forward(i, j, c).wait_send()
                    if j != diagonal or not table[i][4]:
                        fetch(i, j, mine).wait_send()
                if table[i][4]:
                    relayed(i, x_nbr, y_nbr).wait_send()
                    relayed(i, y_nbr, x_nbr).wait_send()

        return start, relay, relay_diagonal, finish

    out_shape = [jax.ShapeDtypeStruct(_full_shape(name, shape), dtype) for name, shape, _, dtype, _ in table]
    return _Exchange(shards, out_shape, SEMS_PER_GATHER * n, ops)


STREAM_ROWS = 256


def _stream_rows(src, dst, dst_first, buf, sem_in, sem_out):
    bsz, n_rows, _ = src.shape
    chunks = [(b, r) for b in range(bsz) for r in range(0, n_rows, STREAM_ROWS)]

    def load(i):
        b, r = chunks[i]
        return pltpu.make_async_copy(src.at[b, pl.ds(r, STREAM_ROWS), :], buf.at[i % 2], sem_in.at[i % 2])

    def store(i):
        b, r = chunks[i]
        return pltpu.make_async_copy(buf.at[i % 2], dst.at[b, pl.ds(dst_first + r, STREAM_ROWS), :],
                                     sem_out.at[i % 2])

    load(0).start()
    for i in range(len(chunks)):
        if i + 1 < len(chunks):
            if i >= 1:
                store(i - 1).wait()
            load(i + 1).start()
        load(i).wait()
        store(i).start()
    for i in range(max(len(chunks) - 2, 0), len(chunks)):
        store(i).wait()


def _run_exchange(exchange, name):
    n = len(exchange.ins)

    def body(*refs):
        for phase in exchange.ops(refs[:n], refs[n:2 * n], *refs[2 * n:]):
            phase()

    return pl.pallas_call(
        body,
        name=name,
        in_specs=[HBM_SPEC] * n,
        out_specs=[HBM_SPEC] * n,
        out_shape=exchange.out_shape,
        scratch_shapes=exchange.scratch,
    )(*exchange.ins)


CAST_ROWS = 64


def _cast_hosting(arrays, exchange, name, stream=None):
    n_a, n_x = len(arrays), len(exchange.ins)
    k = 1 if stream else 0

    def body(*refs):
        a_in, x_in = refs[:n_a], refs[n_a:n_a + n_x]
        outs = refs[n_a + n_x + k:]
        a_out, x_out = outs[:n_a], outs[n_a:n_a + n_x]
        scratch = outs[n_a + n_x + k:]
        start, *rest = exchange.ops(x_in, x_out, *scratch[:2])
        start()
        for src, dst in zip(a_in, a_out):
            def rows(i, carry, src=src, dst=dst):
                window = pl.ds(pl.multiple_of(i * CAST_ROWS, CAST_ROWS), CAST_ROWS)
                cols = src.shape[1]
                if dst.shape[1] != cols:
                    dst[window, dst.shape[1] - LANES:] = jnp.zeros((CAST_ROWS, LANES), BF16)
                dst[window, :cols] = src[window, :].astype(BF16)
                return carry

            lax.fori_loop(0, src.shape[0] // CAST_ROWS, rows, 0)
        if stream:
            x_ref, h_ref = refs[n_a + n_x], outs[n_a + n_x]
            buf, sem_in, sem_out = scratch[2:]
            _stream_rows(x_ref, h_ref, PREFIX, buf, sem_in, sem_out)
        for phase in rest:
            phase()
        if stream:
            meta = pltpu.make_async_copy(x_out[stream[1]], buf.at[1, pl.ds(0, N_META), :], sem_in.at[0])
            meta.start()
            buf[0, 0:N_PAD, :] = jnp.zeros((N_PAD, buf.shape[2]), F32)
            meta.wait()
            buf[0, N_PAD:PREFIX, :] = buf[1, 0:N_META, :]
            puts = [pltpu.make_async_copy(buf.at[0, pl.ds(0, PREFIX), :], h_ref.at[b, pl.ds(0, PREFIX), :], sem_out.at[b])
                    for b in range(h_ref.shape[0])]
            for put in puts:
                put.start()
            for put in puts:
                put.wait()

    extra_in, extra_out, extra_scratch = [], [], []
    if stream:
        x = stream[0]
        assert x.shape[0] <= 2 and x.shape[1] % STREAM_ROWS == 0 and x.dtype == F32
        extra_in = [x]
        extra_out = [jax.ShapeDtypeStruct((x.shape[0], PREFIX + x.shape[1], x.shape[2]), F32)]
        extra_scratch = [pltpu.VMEM((2, STREAM_ROWS, x.shape[2]), F32),
                         pltpu.SemaphoreType.DMA((2,)), pltpu.SemaphoreType.DMA((2,))]
    outs = pl.pallas_call(
        body,
        name=name,
        in_specs=[VMEM_SPEC] * n_a + [HBM_SPEC] * (n_x + k),
        out_specs=[VMEM_SPEC] * n_a + [HBM_SPEC] * (n_x + k),
        out_shape=[jax.ShapeDtypeStruct((a.shape[0], -(-a.shape[1] // LANES) * LANES), BF16) for a in arrays]
        + exchange.out_shape + extra_out,
        scratch_shapes=exchange.scratch + extra_scratch,
        compiler_params=pltpu.CompilerParams(vmem_limit_bytes=VMEM_LIMIT),
    )(*arrays, *exchange.ins, *extra_in)
    return outs[:n_a], outs[n_a:]


def _host_exchange(exchange, in_refs, out_refs, sem_refs, first, middle, last, late=None):
    start, *relays, finish = exchange.ops(in_refs, out_refs, *sem_refs)
    pl.when(first)(start)
    pl.when(middle)(relays[0])
    if len(relays) > 1:
        pl.when(last if late is None else late)(relays[1])
    pl.when(last)(finish)


def _halves_view(name, shape, axis):
    r, c = shape
    if name == STACKED:
        return (N_CHIPS, 2, r // 2, c // N_CHIPS), lambda ref, h: ref.at[:, h]
    if axis == 1:
        return (2, r // 2, c), lambda ref, h: ref.at[h]
    return (N_CHIPS, 2, r // N_CHIPS // 2, c), lambda ref, h: ref.at[:, h]


def _halves_exchange(grads, entries):
    n_w = len(entries)
    views = [_halves_view(*entry) for entry in entries]

    def ops(ins, outs, send_sems, recv_sems):
        x, y, c = _coords()
        copies = [pltpu.make_async_remote_copy(views[i][1](ins[i], 1 - c), outs[i], send_sems.at[i], recv_sems.at[i],
                                               device_id=(x, y, 1 - c), device_id_type=MESH_ID) for i in range(n_w)]

        def start():
            for cp in copies:
                cp.start()

        def finish():
            for cp in copies:
                cp.wait()

        return start, lambda: None, finish

    half_shape = lambda v: tuple(d for i, d in enumerate(v) if i != (1 if len(v) == 4 else 0))
    out_shape = [jax.ShapeDtypeStruct(half_shape(v[0]), F32) for v in views]
    return _Exchange([g.reshape(v[0]) for g, v in zip(grads, views)], out_shape, n_w, ops)


def _add_sibling(g_view, recv, c, name):
    shape = recv.shape
    if len(shape) == 2:
        tr = _tile(shape[0], 128, 16)
        grid = (shape[0] // tr,)
        g_spec = pl.BlockSpec((None, tr, shape[1]), lambda i, c_ref: (c_ref[0], i, 0))
        r_spec = pl.BlockSpec((tr, shape[1]), lambda i, c_ref: (i, 0))
    else:
        tr = _tile(shape[1], 256, 16)
        grid = (N_CHIPS, shape[1] // tr)
        g_spec = pl.BlockSpec((None, None, tr, shape[2]), lambda k, i, c_ref: (k, c_ref[0], i, 0))
        r_spec = pl.BlockSpec((None, tr, shape[2]), lambda k, i, c_ref: (k, i, 0))

    def body(c_ref, g_ref, r_ref, o_ref):
        o_ref[...] = (g_ref[...] + r_ref[...]).astype(BF16)

    return pl.pallas_call(
        body,
        name="add_sibling_" + name,
        grid_spec=pltpu.PrefetchScalarGridSpec(num_scalar_prefetch=1, grid=grid, in_specs=[g_spec, r_spec],
                                               out_specs=r_spec),
        out_shape=jax.ShapeDtypeStruct(shape, BF16),
        compiler_params=_cparams(("parallel",) * len(grid)),
    )(c, g_view, recv)


def _piece_of(ref, name, axis, k):
    if name == STACKED or axis == 0:
        return ref.at[k]
    size = ref.shape[1] // N_CHIPS
    return ref.at[:, pl.ds(pl.multiple_of(k * size, size), size)]


def _piece_shape(name, shape, axis):
    r, c = shape
    return (r // 2, c // N_CHIPS) if (axis == 1) else (r // N_CHIPS // 2, c)


def _scatter_exchange(partials, entries):
    n_w = len(entries)

    def ops(ins, outs, send_sems, recv_sems):
        x, y, c = _coords()
        chips = _other_chips(x, y)
        copies = []
        for i, (name, _, axis) in enumerate(entries):
            for j, chip in enumerate(chips):
                sem = 3 * i + j
                copies.append(pltpu.make_async_remote_copy(
                    _piece_of(ins[i], name, axis, 2 * chip[0] + chip[1]), outs[i].at[j], send_sems.at[sem],
                    recv_sems.at[sem], device_id=(chip[0], chip[1], c), device_id_type=MESH_ID))

        def start():
            for cp in copies:
                cp.start()

        def finish():
            for cp in copies:
                cp.wait()

        return start, lambda: None, finish

    out_shape = [jax.ShapeDtypeStruct((3,) + _piece_shape(*entry), BF16) for entry in entries]
    return _Exchange(partials, out_shape, 3 * n_w, ops)


def _add_chips(partial, recv, mine, name, axis):
    rows, cols = recv.shape[1:]
    tr = _tile(rows, 256, 16)
    if name == STACKED or axis == 0:
        p_spec = pl.BlockSpec((None, tr, cols), lambda i, k_ref: (k_ref[0], i, 0))
    else:
        p_spec = pl.BlockSpec((tr, cols), lambda i, k_ref: (i, k_ref[0]))

    def body(k_ref, p_ref, r_ref, o_ref):
        f32 = lambda a: a.astype(F32)
        o_ref[...] = ((f32(p_ref[...]) + f32(r_ref[0])) + f32(r_ref[1])) + f32(r_ref[2])

    return pl.pallas_call(
        body,
        name="add_chips_" + name,
        grid_spec=pltpu.PrefetchScalarGridSpec(
            num_scalar_prefetch=1, grid=(rows // tr,),
            in_specs=[p_spec, pl.BlockSpec((3, tr, cols), lambda i, k_ref: (0, i, 0))],
            out_specs=pl.BlockSpec((tr, cols), lambda i, k_ref: (i, 0))),
        out_shape=jax.ShapeDtypeStruct((rows, cols), F32),
        compiler_params=_cparams(("parallel",)),
    )(mine, partial, recv)


def _share_with_sibling(halves):
    n_w = len(halves)

    def body(*refs):
        ins, outs = refs[:n_w], refs[n_w:2 * n_w]
        send_sems, recv_sems = refs[2 * n_w:]
        x, y, c = _coords()
        copies = [pltpu.make_async_remote_copy(ins[i], outs[i], send_sems.at[i], recv_sems.at[i],
                                               device_id=(x, y, 1 - c), device_id_type=MESH_ID) for i in range(n_w)]
        for cp in copies:
            cp.start()
        for cp in copies:
            cp.wait()

    return pl.pallas_call(
        body,
        name="share_with_sibling",
        in_specs=[HBM_SPEC] * n_w,
        out_specs=[HBM_SPEC] * n_w,
        out_shape=[jax.ShapeDtypeStruct(h.shape, F32) for h in halves],
        scratch_shapes=[pltpu.SemaphoreType.DMA((n_w,)), pltpu.SemaphoreType.DMA((n_w,))],
    )(*halves)


SMALL_ROWS = 168


def _small_exchange(buf):
    def ops(ins, outs, send_sems, recv_sems):
        x, y, c = _coords()
        me = 4 * x + 2 * y + c
        peers = [(x ^ fx, y ^ fy, c ^ fc) for fx in (0, 1) for fy in (0, 1) for fc in (0, 1)][1:]

        def copy(j, slot, dev):
            return pltpu.make_async_remote_copy(ins[0], outs[0].at[slot], send_sems.at[j], recv_sems.at[j],
                                                device_id=dev, device_id_type=MESH_ID)

        own = pltpu.make_async_copy(ins[0], outs[0].at[me], send_sems.at[N_DEV - 1])

        def start():
            own.start()
            for j, dev in enumerate(peers):
                copy(j, me, dev).start()

        def finish():
            for j, dev in enumerate(peers):
                copy(j, 4 * dev[0] + 2 * dev[1] + dev[2], dev).wait()
            own.wait()

        return start, lambda: None, finish

    return _Exchange([buf], [jax.ShapeDtypeStruct((N_DEV,) + buf.shape, F32)], N_DEV, ops)


def _sum_devices(gathered):
    def body(g_ref, out_ref):
        acc = g_ref[0]
        for d in range(1, N_DEV):
            acc = acc + g_ref[d]
        out_ref[...] = acc

    return pl.pallas_call(
        body,
        name="sum_devices",
        in_specs=[VMEM_SPEC],
        out_specs=VMEM_SPEC,
        out_shape=jax.ShapeDtypeStruct(gathered.shape[1:], F32),
    )(gathered)


def _adamw(w, g, m, v, copy_g=False):
    r, rest = w.shape[0], w.shape[1:]
    per_row = 1
    for dim in rest:
        per_row *= dim
    tr = _tile(r, max(8, (5 << 19) // (4 * per_row)), 8 if len(rest) == 1 else 1)

    def body(w_ref, g_ref, m_ref, v_ref, *out_refs):
        d_ref, mo_ref, vo_ref = out_refs[-3:]
        gg = g_ref[...]
        if copy_g:
            out_refs[0][...] = gg
        mm = ADAM_B1 * m_ref[...] + (1.0 - ADAM_B1) * gg
        vv = ADAM_B2 * v_ref[...] + (1.0 - ADAM_B2) * (gg * gg)
        m_hat = mm / (1.0 - ADAM_B1 ** ADAM_STEP)
        v_hat = vv / (1.0 - ADAM_B2 ** ADAM_STEP)
        d_ref[...] = -ADAM_LR * (m_hat / (jnp.sqrt(v_hat) + ADAM_EPS) + ADAM_WD * w_ref[...])
        mo_ref[...] = mm
        vo_ref[...] = vv

    n_out = 4 if copy_g else 3
    spec = pl.BlockSpec((tr,) + rest, lambda i: (i,) + (0,) * len(rest))
    return pl.pallas_call(
        body,
        name="adamw",
        grid=(r // tr,),
        in_specs=[spec] * 4,
        out_specs=[spec] * n_out,
        out_shape=[jax.ShapeDtypeStruct(w.shape, F32)] * n_out,
        compiler_params=_cparams(("parallel",)),
    )(w, g, m, v)


def _adamw_halves(w, own, other, m, v, c, name):
    r, cols = w.shape
    half = r // 2
    tr = _tile(half, 256, 8)
    nt = half // tr
    whole = pl.BlockSpec((tr, cols), lambda h, i, c_ref: (h * nt + i, 0))
    part = pl.BlockSpec((tr, cols), lambda h, i, c_ref: (i, 0))

    def body(c_ref, w_ref, own_ref, other_ref, m_ref, v_ref, g_ref, d_ref, mo_ref, vo_ref):
        gg = jnp.where(pl.program_id(0) == c_ref[0], own_ref[...], other_ref[...])
        g_ref[...] = gg
        mm = ADAM_B1 * m_ref[...] + (1.0 - ADAM_B1) * gg
        vv = ADAM_B2 * v_ref[...] + (1.0 - ADAM_B2) * (gg * gg)
        m_hat = mm / (1.0 - ADAM_B1 ** ADAM_STEP)
        v_hat = vv / (1.0 - ADAM_B2 ** ADAM_STEP)
        d_ref[...] = -ADAM_LR * (m_hat / (jnp.sqrt(v_hat) + ADAM_EPS) + ADAM_WD * w_ref[...])
        mo_ref[...] = mm
        vo_ref[...] = vv

    return pl.pallas_call(
        body,
        name="adamw_" + name,
        grid_spec=pltpu.PrefetchScalarGridSpec(
            num_scalar_prefetch=1, grid=(2, nt),
            in_specs=[whole, part, part, whole, whole], out_specs=[whole] * 4),
        out_shape=[jax.ShapeDtypeStruct((r, cols), F32)] * 4,
        compiler_params=_cparams(("parallel", "parallel")),
    )(c, w, own, other, m, v)


GATHER_FIRST = ("ffn1_w_in", "ffn1_w_out")
GATHER_PROJ = ("w_in",)
GATHER_LATE = ("w_branch_a", "w_branch_b", "w_out", "ffn2_w_in", "ffn2_w_out")


class _Comm:
    def __init__(self, shards, c_arr, mine_arr):
        self.shards, self.c, self.mine = shards, c_arr, mine_arr
        self.groups, self.halves = {}, {}
        self.by_name = {entry[0]: entry for entry in BIG + SPLIT}

    def small_gather(self, loss, small):
        pad_lanes = lambda a: jnp.concatenate([a, jnp.zeros((1, LANES - a.shape[1]), F32)], axis=1)
        buf = jnp.concatenate([
            small["meta_tokens"].reshape(128, LANES),
            small["ffn1_norm"].reshape(8, LANES), small["mix_norm"].reshape(8, LANES),
            small["ffn2_norm"].reshape(8, LANES), small["final_norm"].reshape(8, LANES),
            loss, pad_lanes(small["b_forget"]), pad_lanes(small["attn_sinks"]),
            jnp.zeros((SMALL_ROWS - 163, LANES), F32)], axis=0)
        return _small_exchange(buf)

    def small_gathered(self, outs):
        self.reduced = _sum_devices(outs[0])

    def gather(self, names):
        padded = (STACKED, (D_MODEL, N_CHIPS * SHARD_PAD_COLS), 1)
        table = [(padded if n == STACKED else self.by_name[n]) + (BF16, True) for n in names]
        return _gather_exchange([self.shards[n] for n in names], table)

    def swap(self, tag, grads):
        entries = [self.by_name[n] for n in grads]
        arrays = list(grads.values())
        self.groups[tag] = (entries, arrays)
        return _halves_exchange(arrays, entries)

    def scatter(self, tag, received):
        entries, arrays = self.groups[tag]
        views = [_halves_view(*entry) for entry in entries]
        partials = [_add_sibling(g.reshape(v[0]), r, self.c, name)
                    for g, v, r, (name, _, _) in zip(arrays, views, received, entries)]
        self.groups[tag] = (entries, partials)
        return _scatter_exchange(partials, entries)

    def received(self, tag, pieces):
        entries, partials = self.groups[tag]
        for p, r, (name, _, axis) in zip(partials, pieces, entries):
            self.halves[name] = _add_chips(p, r, self.mine, name, axis)

    def finish(self):
        names = [n for n, _, _ in BIG + SPLIT if n != "ffn1_w_in"]
        own = [self.halves[n] for n in names]
        return dict(zip(names, zip(own, _share_with_sibling(own))))


def kernel(x, meta_tokens, ffn1_norm, ffn1_w_in, ffn1_w_out, mix_norm, w_in, b_forget, attn_sinks, w_branch_a, w_branch_b, w_out, ffn2_norm, ffn2_w_in, ffn2_w_out, final_norm, loss_target, m_meta_tokens, m_ffn1_norm, m_ffn1_w_in, m_ffn1_w_out, m_mix_norm, m_w_in, m_b_forget, m_attn_sinks, m_w_branch_a, m_w_branch_b, m_w_out, m_ffn2_norm, m_ffn2_w_in, m_ffn2_w_out, m_final_norm, v_meta_tokens, v_ffn1_norm, v_ffn1_w_in, v_ffn1_w_out, v_mix_norm, v_w_in, v_b_forget, v_attn_sinks, v_w_branch_a, v_w_branch_b, v_w_out, v_ffn2_norm, v_ffn2_w_in, v_ffn2_w_out, v_final_norm):
    given = dict(locals())
    names = ["meta_tokens", "ffn1_norm", "ffn1_w_in", "ffn1_w_out", "mix_norm", "w_in", "b_forget", "attn_sinks",
             "w_branch_a", "w_branch_b", "w_out", "ffn2_norm", "ffn2_w_in", "ffn2_w_out", "final_norm"]
    big_names = [n for n, _, _ in BIG]
    cx, cy, cc = _coords()
    c_arr = cc.reshape(1).astype(jnp.int32)
    mine_arr = (2 * cx + cy).reshape(1).astype(jnp.int32)

    by_name = {entry[0]: entry for entry in BIG}
    shards = {n: given[n][0].astype(BF16) for n in GATHER_FIRST}
    table = [by_name[n] + (BF16, True) for n in GATHER_FIRST] + [("meta_tokens", (N_META, D_MODEL), 1, F32, False)]
    first = _gather_exchange([shards[n] for n in GATHER_FIRST] + [meta_tokens], table)
    late_names = [n for n in big_names if n not in GATHER_FIRST]
    casts, (w1i, w1o, _, stream) = _cast_hosting([given[n][0] for n in late_names], first, "gather_first",
                                                 stream=(x, len(GATHER_FIRST)))
    shards.update(zip(late_names, casts))
    comm = _Comm(shards, c_arr, mine_arr)
    norms = (ffn1_norm, mix_norm, ffn2_norm, final_norm.reshape(1, D_MODEL))
    loss, grad_x, small, big = _local_step(stream, loss_target, None, norms, b_forget, attn_sinks, (w1i, w1o), comm)

    swap = comm.swap("ffn1_in", dict(ffn1_w_in_bottom=big["ffn1_w_in"]))
    last = comm.scatter("ffn1_in", _run_exchange(swap, "exchange_halves_ffn1_in"))
    comm.received("ffn1_in", _run_exchange(last, "scatter_chip_sums"))
    grad_halves = comm.finish()
    grads = {}

    red = comm.reduced
    meta_cols = red[:128].reshape(N_META, D_MODEL)
    grads["meta_tokens"] = lax.dynamic_slice_in_dim(meta_cols, (2 * cx + cy) * (D_MODEL // N_CHIPS),
                                                    D_MODEL // N_CHIPS, axis=1)
    grads["ffn1_norm"] = red[128:136].reshape(1, D_MODEL)
    grads["mix_norm"] = red[136:144].reshape(1, D_MODEL)
    grads["ffn2_norm"] = red[144:152].reshape(1, D_MODEL)
    grads["final_norm"] = red[152:160].reshape(1, D_MODEL)
    loss_out = red[160, 0]
    grads["b_forget"] = red[161:162, :B_HEADS]
    grads["attn_sinks"] = red[162:163, :A_HEADS]

    out_g, out_d, out_m, out_v = [], [], [], []
    for n in names:
        w_full = given[n]
        shape = w_full.shape
        two_d = (lambda a: a.reshape(shape[-2], shape[-1])) if len(shape) >= 2 else (lambda a: a.reshape(1, shape[0]))
        if n == STACKED:
            own, other = grad_halves[n]
            g_nat = jnp.concatenate([jnp.where(cc == 0, own, other), jnp.where(cc == 0, other, own)], axis=0)
            rows = lambda a: a.reshape(1, shape[-2], shape[-1]).transpose(2, 0, 1)
            unrows = lambda a: a.transpose(1, 2, 0)
            g2, d2, m2, v2 = [unrows(a) for a in _adamw(rows(w_full), rows(g_nat), rows(given["m_" + n]),
                                                         rows(given["v_" + n]), copy_g=True)]
        elif n == "ffn1_w_in":
            quarters = []
            for part, _, _ in SPLIT:
                own, other = grad_halves[part]
                quarters += [jnp.where(cc == 0, own, other), jnp.where(cc == 0, other, own)]
            g2 = jnp.concatenate(quarters, axis=0)
            d2, m2, v2 = _adamw(two_d(w_full), g2, two_d(given["m_" + n]), two_d(given["v_" + n]))
        elif n in grad_halves:
            own, other = grad_halves[n]
            g2, d2, m2, v2 = _adamw_halves(two_d(w_full), own, other, two_d(given["m_" + n]),
                                           two_d(given["v_" + n]), c_arr, n)
        else:
            g2 = two_d(grads[n])
            d2, m2, v2 = _adamw(two_d(w_full), g2, two_d(given["m_" + n]), two_d(given["v_" + n]))
        out_g.append(g2.reshape(shape))
        out_d.append(d2.reshape(shape))
        out_m.append(m2.reshape(shape))
        out_v.append(v2.reshape(shape))
    return (loss_out, grad_x, *out_g, *out_d, *out_m, *out_v)
```

```python
import jax
import jax.numpy as jnp
from jax import lax
from jax.experimental import pallas as pl
from jax.experimental.pallas import tpu as pltpu

F32 = jnp.float32
BF16 = jnp.bfloat16

D_MODEL = 1024
N_META = 16
BLOCK = 128
LANES = 128
PREFIX = BLOCK
N_PAD = PREFIX - N_META
HEAD_DIM = 64
A_HEADS = 8
A_KV_HEADS = 2
A_GROUP = 4
B_HEADS = 8
B_PAIRS = B_HEADS // 2
A_WIDTH = A_HEADS * HEAD_DIM
A_KV_WIDTH = A_KV_HEADS * HEAD_DIM
B_WIDTH = B_HEADS * HEAD_DIM
W_IN_COLS = A_WIDTH + 2 * A_KV_WIDTH + 3 * B_WIDTH + B_HEADS + 2 * D_MODEL
SRC_KA = A_WIDTH
SRC_VA = SRC_KA + A_KV_WIDTH
SRC_QB = SRC_VA + A_KV_WIDTH
SRC_KB = SRC_QB + B_WIDTH
SRC_VB = SRC_KB + B_WIDTH
SRC_F = SRC_VB + B_WIDTH
SRC_GA = SRC_F + B_HEADS
SRC_GB = SRC_GA + D_MODEL
A_PAD_WIDTH = A_HEADS * LANES
B_PAD_WIDTH = B_HEADS * LANES
F_COLS = LANES
OFF_QA = 0
OFF_KA = SRC_KA
OFF_VA = SRC_VA
OFF_F = OFF_VA + A_KV_WIDTH
OFF_QB = OFF_F + F_COLS
OFF_KB = OFF_QB + B_WIDTH
OFF_VB = OFF_KB + B_WIDTH
OFF_GA = OFF_VB + B_WIDTH
OFF_GB = OFF_GA + D_MODEL
P_COLS = OFF_GB + D_MODEL
P_PIECES = ((0, OFF_QB), (OFF_QB, OFF_GA - OFF_QB), (OFF_GA, P_COLS - OFF_GA))
EPS = 1e-6
NEG = -1e30
SCALE = HEAD_DIM ** -0.5
KEY_BLOCKS = 4

ADAM_LR = 0.001
ADAM_B1 = 0.9
ADAM_B2 = 0.999
ADAM_EPS = 1e-08
ADAM_WD = 0.01
ADAM_STEP = 10

N_CHIPS = 4
N_DEV = 8
VMEM_LIMIT = 56 * 1024 * 1024

NT_DIMS = (((1,), (1,)), ((), ()))
TN_DIMS = (((0,), (0,)), ((), ()))
MESH_ID = pl.DeviceIdType.MESH
HBM_SPEC = pl.BlockSpec(memory_space=pltpu.HBM)
VMEM_SPEC = pl.BlockSpec(memory_space=pltpu.VMEM)


def _tile(n, target, mult=16):
    best = None
    for t in range(mult, min(n, target) + 1, mult):
        if n % t == 0:
            best = t
    return best if best is not None else n


def _cparams(sem):
    return pltpu.CompilerParams(dimension_semantics=sem, vmem_limit_bytes=VMEM_LIMIT)


def _rms_scale(h):
    return lax.rsqrt(jnp.mean(h * h, axis=-1, keepdims=True) + EPS)


def _rms_bwd(dn, h, w):
    r = _rms_scale(h)
    dw = jnp.sum(dn * (h * r), axis=0, keepdims=True)
    z = dn * w
    dh = r * z - h * ((r * r * r) * jnp.mean(z * h, axis=-1, keepdims=True))
    return dh, dw


def _ffn_fwd(h, norm_w, w_in, w_out, exchange=None):
    t, d = h.shape
    f = w_out.shape[0]
    tm = _tile(t, 272)
    tc = _tile(f, 256, 128)
    nj = f // tc
    ni = t // tm
    n_x = len(exchange.ins) if exchange else 0

    def body(*refs):
        h_ref, nw_ref, wi_ref, wo_ref = refs[:4]
        hout_ref, g_ref, u_ref = refs[4 + n_x:7 + n_x]
        a_scr = refs[7 + 2 * n_x]
        i = pl.program_id(0)
        if exchange:
            _host_exchange(exchange, refs[4:4 + n_x], refs[7 + n_x:7 + 2 * n_x], refs[8 + 2 * n_x:],
                           i == 0, i == ni // 3, i == ni - 1, late=i == 2 * ni // 3)
        hh = h_ref[...]
        n = ((hh * _rms_scale(hh)) * nw_ref[...]).astype(BF16)
        for j in range(nj):
            cols = slice(j * tc, (j + 1) * tc)
            g = jnp.dot(n, wi_ref[:, j * tc:(j + 1) * tc], preferred_element_type=F32)
            u = jnp.dot(n, wi_ref[:, f + j * tc:f + (j + 1) * tc], preferred_element_type=F32)
            g_ref[:, cols] = g
            u_ref[:, cols] = u
            a_scr[:, cols] = ((g * jax.nn.sigmoid(g)) * u).astype(BF16)
        hout_ref[...] = hh + 0.5 * jnp.dot(a_scr[...], wo_ref[...], preferred_element_type=F32)

    resident = lambda a: pl.BlockSpec(a.shape, lambda i: (0, 0), pipeline_mode=pl.Buffered(1))
    row = lambda w: pl.BlockSpec((tm, w), lambda i: (i, 0))
    outs = pl.pallas_call(
        body,
        name="ffn_fwd",
        grid=(ni,),
        in_specs=[row(d), pl.BlockSpec((1, d), lambda i: (0, 0)), resident(w_in), resident(w_out)] + [HBM_SPEC] * n_x,
        out_specs=[row(d), row(f), row(f)] + [HBM_SPEC] * n_x,
        out_shape=[
            jax.ShapeDtypeStruct((t, d), F32),
            jax.ShapeDtypeStruct((t, f), F32),
            jax.ShapeDtypeStruct((t, f), F32),
        ] + (exchange.out_shape if exchange else []),
        scratch_shapes=[pltpu.VMEM((tm, f), BF16)] + (exchange.scratch if exchange else []),
        compiler_params=_cparams(("arbitrary",) if exchange else ("parallel",)),
    )(h, norm_w, w_in, w_out, *(exchange.ins if exchange else []))
    return outs[:3], outs[3:]


def _ffn_bwd(dh_out, h, norm_w, g, u, w_in, w_out, exchange=None):
    t, d = h.shape
    f = w_out.shape[0]
    tm = _tile(t, 272)
    tc = _tile(f, 256, 128)
    nj = f // tc
    ni = t // tm
    n_x = len(exchange.ins) if exchange else 0

    def body(*refs):
        dho_ref, h_ref, nw_ref, g_ref, u_ref, wi_ref, wo_ref = refs[:7]
        dhin_ref, n_ref, a_ref, dgu_ref, df_ref, dnw_ref = refs[7 + n_x:13 + n_x]
        i = pl.program_id(0)
        if exchange:
            _host_exchange(exchange, refs[7:7 + n_x], refs[13 + n_x:13 + 2 * n_x], refs[13 + 2 * n_x:],
                           i == 0, i == ni - 1, i == ni - 1)
        hh = h_ref[...]
        nw = nw_ref[...]
        n_ref[...] = ((hh * _rms_scale(hh)) * nw).astype(BF16)
        dho = dho_ref[...]
        df = (0.5 * dho).astype(BF16)
        df_ref[...] = df
        for j in range(nj):
            cols = slice(j * tc, (j + 1) * tc)
            da = lax.dot_general(df, wo_ref[cols, :], NT_DIMS, preferred_element_type=F32)
            gg = g_ref[:, cols]
            uu = u_ref[:, cols]
            sig = jax.nn.sigmoid(gg)
            sl = gg * sig
            a_ref[:, cols] = (sl * uu).astype(BF16)
            dgu_ref[0, :, cols] = ((da * uu) * (sig * (1.0 + gg * (1.0 - sig)))).astype(BF16)
            dgu_ref[1, :, cols] = (da * sl).astype(BF16)
        dn = (lax.dot_general(dgu_ref[0], wi_ref[:, :f], NT_DIMS, preferred_element_type=F32)
              + lax.dot_general(dgu_ref[1], wi_ref[:, f:], NT_DIMS, preferred_element_type=F32))
        dh, dw = _rms_bwd(dn, hh, nw)
        dhin_ref[...] = dho + dh
        dnw_ref[0] = dw

    resident = lambda a: pl.BlockSpec(a.shape, lambda i: (0, 0), pipeline_mode=pl.Buffered(1))
    row = lambda w: pl.BlockSpec((tm, w), lambda i: (i, 0))
    outs = pl.pallas_call(
        body,
        name="ffn_bwd",
        grid=(ni,),
        in_specs=[row(d), row(d), pl.BlockSpec((1, d), lambda i: (0, 0)), row(f), row(f),
                  resident(w_in), resident(w_out)] + [HBM_SPEC] * n_x,
        out_specs=[row(d), row(d), row(f), pl.BlockSpec((2, tm, f), lambda i: (0, i, 0)), row(d),
                   pl.BlockSpec((1, 1, d), lambda i: (i, 0, 0))] + [HBM_SPEC] * n_x,
        out_shape=[
            jax.ShapeDtypeStruct((t, d), F32),
            jax.ShapeDtypeStruct((t, d), BF16),
            jax.ShapeDtypeStruct((t, f), BF16),
            jax.ShapeDtypeStruct((2, t, f), BF16),
            jax.ShapeDtypeStruct((t, d), BF16),
            jax.ShapeDtypeStruct((ni, 1, d), F32),
        ] + (exchange.out_shape if exchange else []),
        scratch_shapes=exchange.scratch if exchange else [],
        compiler_params=_cparams(("arbitrary",) if exchange else ("parallel",)),
    )(dh_out, h, norm_w, g, u, w_in, w_out, *(exchange.ins if exchange else []))
    return outs[:6], outs[6:]


def _tn_matmul(a, b, name, exchange=None):
    t, k = a.shape
    split = b.ndim == 3
    n = 2 * b.shape[2] if split else b.shape[1]
    tk = _tile(k, 512, 128)
    tn = _tile(b.shape[-1], 1408, 128)
    per_half = b.shape[-1] // tn
    ni, nj = k // tk, n // tn
    n_x = len(exchange.ins) if exchange else 0

    def body(*refs):
        a_ref, b_ref, o_ref = refs[0], refs[1], refs[2 + n_x]
        if exchange:
            i, j = pl.program_id(0), pl.program_id(1)
            at_end = (i == ni - 1) & (j == nj - 1)
            _host_exchange(exchange, refs[2:2 + n_x], refs[3 + n_x:3 + 2 * n_x], refs[3 + 2 * n_x:],
                           (i == 0) & (j == 0), at_end, at_end)
        o_ref[...] = lax.dot_general(a_ref[...], b_ref[...], TN_DIMS, preferred_element_type=F32)

    if split:
        b_spec = pl.BlockSpec((None, t, tn), lambda i, j: (j // per_half, 0, j % per_half))
    else:
        b_spec = pl.BlockSpec((t, tn), lambda i, j: (0, j))
    outs = pl.pallas_call(
        body,
        name=name,
        grid=(ni, nj),
        in_specs=[pl.BlockSpec((t, tk), lambda i, j: (0, i)), b_spec] + [HBM_SPEC] * n_x,
        out_specs=[pl.BlockSpec((tk, tn), lambda i, j: (i, j))] + [HBM_SPEC] * n_x,
        out_shape=[jax.ShapeDtypeStruct((k, n), F32)] + (exchange.out_shape if exchange else []),
        scratch_shapes=exchange.scratch if exchange else [],
        compiler_params=_cparams(("arbitrary", "arbitrary") if exchange else ("parallel", "parallel")),
    )(a, b, *(exchange.ins if exchange else []))
    return (outs[0], outs[1:]) if exchange else outs[0]


A_SLOT = lambda h: h // A_GROUP
B_SLOT = lambda h: h % 2

PROJ_PARTS = (
    (OFF_QA, A_WIDTH, A_PAD_WIDTH, True, A_SLOT), (OFF_KA, A_KV_WIDTH, A_KV_WIDTH, True, None),
    (OFF_VA, A_KV_WIDTH, A_KV_WIDTH, True, None), (OFF_QB, B_WIDTH, B_WIDTH, True, None),
    (OFF_KB, B_WIDTH, B_PAD_WIDTH, True, B_SLOT), (OFF_VB, B_WIDTH, B_PAD_WIDTH, True, B_SLOT),
    (OFF_GA, D_MODEL, D_MODEL, False, None), (OFF_GB, D_MODEL, D_MODEL, False, None), (OFF_F, F_COLS, F_COLS, False, None),
)


def _head_tile(pair, head, slot):
    lane_slot = lax.broadcasted_iota(jnp.int32, pair.shape, 1) // HEAD_DIM
    moved = pair if head % 2 == slot else pltpu.roll(pair, HEAD_DIM, 1)
    return jnp.where(lane_slot == slot, moved, 0.0)


def _proj_fwd(h, norm_w, w_p):
    t, d = h.shape
    tm = _tile(t, 272)

    def body(h_ref, nw_ref, w_ref, u_ref, *part_refs):
        hh = h_ref[...]
        un = ((hh * _rms_scale(hh)) * nw_ref[...]).astype(BF16)
        u_ref[...] = un
        for (off, width, _, _, slot), p_ref in zip(PROJ_PARTS, part_refs):
            if slot is None:
                p_ref[...] = jnp.dot(un, w_ref[:, off:off + width], preferred_element_type=F32).astype(p_ref.dtype)
                continue
            part = jnp.dot(un, w_ref[:, off:off + width], preferred_element_type=F32)
            for pair in range(width // LANES):
                x = part[:, pair * LANES:(pair + 1) * LANES]
                for head in (2 * pair, 2 * pair + 1):
                    p_ref[:, head * LANES:(head + 1) * LANES] = _head_tile(x, head, slot(head)).astype(p_ref.dtype)

    row = lambda w: pl.BlockSpec((tm, w), lambda i: (i, 0))
    return pl.pallas_call(
        body,
        name="proj_fwd",
        grid=(t // tm,),
        in_specs=[row(d), pl.BlockSpec((1, d), lambda i: (0, 0)),
                  pl.BlockSpec(w_p.shape, lambda i: (0, 0), pipeline_mode=pl.Buffered(1))],
        out_specs=[row(d)] + [row(width) for _, _, width, _, _ in PROJ_PARTS],
        out_shape=[jax.ShapeDtypeStruct((t, d), BF16)]
        + [jax.ShapeDtypeStruct((t, width), BF16 if is_bf else F32) for _, _, width, is_bf, _ in PROJ_PARTS],
        compiler_params=_cparams(("parallel",)),
    )(h, norm_w, w_p)


def _proj_bwd(dh_out, h, norm_w, dproj, w_p, exchange=None):
    t, d = h.shape
    tm = _tile(t, 272)
    ni = t // tm
    n_p = len(P_PIECES)
    n_in = 4 + n_p
    n_x = len(exchange.ins) if exchange else 0

    def body(*refs):
        dho_ref, h_ref, nw_ref = refs[:3]
        dp_refs, w_ref = refs[3:3 + n_p], refs[3 + n_p]
        dhin_ref, dnw_ref = refs[n_in + n_x:n_in + 2 + n_x]
        if exchange:
            i = pl.program_id(0)
            _host_exchange(exchange, refs[n_in:n_in + n_x], refs[n_in + 2 + n_x:n_in + 2 + 2 * n_x],
                           refs[n_in + 2 + 2 * n_x:], i == 0, i == ni - 1, i == ni - 1)
        dn = None
        for dp_ref, (off, width) in zip(dp_refs, P_PIECES):
            part = lax.dot_general(dp_ref[...], w_ref[:, off:off + width], NT_DIMS, preferred_element_type=F32)
            dn = part if dn is None else dn + part
        dh, dw = _rms_bwd(dn, h_ref[...], nw_ref[...])
        dhin_ref[...] = dho_ref[...] + dh
        dnw_ref[0] = dw

    row = lambda w: pl.BlockSpec((tm, w), lambda i: (i, 0))
    outs = pl.pallas_call(
        body,
        name="proj_bwd",
        grid=(ni,),
        in_specs=[row(d), row(d), pl.BlockSpec((1, d), lambda i: (0, 0))] + [row(width) for _, width in P_PIECES]
        + [pl.BlockSpec(w_p.shape, lambda i: (0, 0), pipeline_mode=pl.Buffered(1))] + [HBM_SPEC] * n_x,
        out_specs=[row(d), pl.BlockSpec((1, 1, d), lambda i: (i, 0, 0))] + [HBM_SPEC] * n_x,
        out_shape=[jax.ShapeDtypeStruct((t, d), F32), jax.ShapeDtypeStruct((ni, 1, d), F32)]
        + (exchange.out_shape if exchange else []),
        scratch_shapes=exchange.scratch if exchange else [],
        compiler_params=_cparams(("arbitrary",) if exchange else ("parallel",)),
    )(dh_out, h, norm_w, *dproj, w_p, *(exchange.ins if exchange else []))
    return outs[:2], outs[2:]


def _merge_fwd(h, oa, ob, ga, gb, wa, wb, wo):
    t, d = h.shape
    tm = _tile(t, 544)

    def body(h_ref, oa_ref, ob_ref, ga_ref, gb_ref, wa_ref, wb_ref, wo_ref, hout_ref, mix_ref):
        ya = jnp.dot(oa_ref[...], wa_ref[...], preferred_element_type=F32)
        yb = jnp.dot(ob_ref[...], wb_ref[...], preferred_element_type=F32)
        mixed = (jax.nn.sigmoid(ga_ref[...]) * ya + jax.nn.sigmoid(gb_ref[...]) * yb).astype(BF16)
        mix_ref[...] = mixed
        hout_ref[...] = h_ref[...] + jnp.dot(mixed, wo_ref[...], preferred_element_type=F32)

    row = lambda w: pl.BlockSpec((tm, w), lambda i: (i, 0))
    full = lambda a: pl.BlockSpec(a.shape, lambda i: (0, 0))
    return pl.pallas_call(
        body,
        name="merge_fwd",
        grid=(t // tm,),
        in_specs=[row(d), row(oa.shape[1]), row(ob.shape[1]), row(d), row(d), full(wa), full(wb), full(wo)],
        out_specs=[row(d), row(d)],
        out_shape=[jax.ShapeDtypeStruct((t, d), F32), jax.ShapeDtypeStruct((t, d), BF16)],
        compiler_params=_cparams(("parallel",)),
    )(h, oa, ob, ga, gb, wa, wb, wo)


def _merge_bwd(dh, oa, ob, ga, gb, wa, wb, wo, exchange=None):
    t, d = dh.shape
    tm = _tile(t, 544)
    ni = t // tm
    n_x = len(exchange.ins) if exchange else 0

    def body(*refs):
        dh_ref, oa_ref, ob_ref, ga_ref, gb_ref, wa_ref, wb_ref, wo_ref = refs[:8]
        dya_ref, dyb_ref, doa_ref, dob_ref, dg_ref, dhb_ref = refs[8 + n_x:14 + n_x]
        if exchange:
            i = pl.program_id(0)
            _host_exchange(exchange, refs[8:8 + n_x], refs[14 + n_x:14 + 2 * n_x], refs[14 + 2 * n_x:],
                           i == 0, i == ni - 1, i == ni - 1)
        dhb = dh_ref[...].astype(BF16)
        dhb_ref[...] = dhb
        dmix = lax.dot_general(dhb, wo_ref[...], NT_DIMS, preferred_element_type=F32)
        for branch, (o_ref, g_ref, w_ref, dy_ref, do_ref) in enumerate((
                (oa_ref, ga_ref, wa_ref, dya_ref, doa_ref),
                (ob_ref, gb_ref, wb_ref, dyb_ref, dob_ref))):
            y = jnp.dot(o_ref[...], w_ref[...], preferred_element_type=F32)
            s = jax.nn.sigmoid(g_ref[...])
            dy = (dmix * s).astype(BF16)
            dy_ref[...] = dy
            dg_ref[:, branch * d:(branch + 1) * d] = ((dmix * y) * (s * (1.0 - s))).astype(BF16)
            do_ref[...] = lax.dot_general(dy, w_ref[...], NT_DIMS, preferred_element_type=F32).astype(BF16)

    row = lambda w: pl.BlockSpec((tm, w), lambda i: (i, 0))
    full = lambda a: pl.BlockSpec(a.shape, lambda i: (0, 0))
    wa_w, wb_w = oa.shape[1], ob.shape[1]
    outs = pl.pallas_call(
        body,
        name="merge_bwd",
        grid=(ni,),
        in_specs=[row(d), row(wa_w), row(wb_w), row(d), row(d), full(wa), full(wb), full(wo)] + [HBM_SPEC] * n_x,
        out_specs=[row(d), row(d), row(wa_w), row(wb_w), row(2 * d), row(d)] + [HBM_SPEC] * n_x,
        out_shape=[
            jax.ShapeDtypeStruct((t, d), BF16), jax.ShapeDtypeStruct((t, d), BF16),
            jax.ShapeDtypeStruct((t, wa_w), BF16), jax.ShapeDtypeStruct((t, wb_w), BF16),
            jax.ShapeDtypeStruct((t, 2 * d), BF16), jax.ShapeDtypeStruct((t, d), BF16),
        ] + (exchange.out_shape if exchange else []),
        scratch_shapes=exchange.scratch if exchange else [],
        compiler_params=_cparams(("arbitrary",) if exchange else ("parallel",)),
    )(dh, oa, ob, ga, gb, wa, wb, wo, *(exchange.ins if exchange else []))
    return outs[:6], outs[6:]


def _tri_dot(tri, x):
    hi = x.astype(BF16)
    r1 = x - hi.astype(F32)
    mid = r1.astype(BF16)
    lo = (r1 - mid.astype(F32)).astype(BF16)
    return (jnp.dot(tri, hi, preferred_element_type=F32)
            + jnp.dot(tri, mid, preferred_element_type=F32)
            + jnp.dot(tri, lo, preferred_element_type=F32))


def _forget_cumsum(f_logit, b_pad, nb):
    t, w = f_logit.shape
    bsz = t // (nb * BLOCK)

    def body(f_ref, b_ref, c_ref, carry):
        @pl.when(pl.program_id(0) == 0)
        def _():
            carry[...] = jnp.zeros_like(carry)

        rows = lax.broadcasted_iota(jnp.int32, (BLOCK, BLOCK), 0)
        cols = lax.broadcasted_iota(jnp.int32, (BLOCK, BLOCK), 1)
        tri = (cols <= rows).astype(BF16)
        for b in range(bsz):
            x = jax.nn.log_sigmoid(f_ref[b] + b_ref[...])
            c = _tri_dot(tri, x) + carry[b]
            c_ref[b] = c
            carry[b] = c[BLOCK - 1:BLOCK, :]

    block = pl.BlockSpec((bsz, BLOCK, w), lambda n: (0, n, 0))
    return pl.pallas_call(
        body,
        name="forget_cumsum",
        grid=(nb,),
        in_specs=[block, pl.BlockSpec((1, w), lambda n: (0, 0))],
        out_specs=block,
        out_shape=jax.ShapeDtypeStruct((bsz, nb * BLOCK, w), F32),
        scratch_shapes=[pltpu.VMEM((bsz, 1, w), F32)],
        compiler_params=_cparams(("arbitrary",)),
    )(f_logit.reshape(bsz, nb * BLOCK, w), b_pad).reshape(t, w)


def _forget_cumsum_bwd(dc, f_logit, b_pad, nb):
    t, w = f_logit.shape
    bsz = t // (nb * BLOCK)

    def body(dc_ref, f_ref, b_ref, df_ref, db_ref, carry):
        @pl.when(pl.program_id(0) == 0)
        def _():
            carry[...] = jnp.zeros_like(carry)
            db_ref[...] = jnp.zeros_like(db_ref)

        rows = lax.broadcasted_iota(jnp.int32, (BLOCK, BLOCK), 0)
        cols = lax.broadcasted_iota(jnp.int32, (BLOCK, BLOCK), 1)
        tri = (cols >= rows).astype(BF16)
        for b in range(bsz):
            dlf = _tri_dot(tri, dc_ref[b]) + carry[b]
            carry[b] = dlf[0:1, :]
            df = dlf * jax.nn.sigmoid(-(f_ref[b] + b_ref[...]))
            df_ref[b] = df.astype(BF16)
            db_ref[b] += jnp.sum(df, axis=0, keepdims=True)

    l = nb * BLOCK
    rev = pl.BlockSpec((bsz, BLOCK, w), lambda n: (0, nb - 1 - n, 0))
    df, db = pl.pallas_call(
        body,
        name="forget_cumsum_bwd",
        grid=(nb,),
        in_specs=[rev, rev, pl.BlockSpec((1, w), lambda n: (0, 0))],
        out_specs=[rev, pl.BlockSpec((bsz, 1, w), lambda n: (0, 0, 0))],
        out_shape=[jax.ShapeDtypeStruct((bsz, l, w), BF16), jax.ShapeDtypeStruct((bsz, 1, w), F32)],
        scratch_shapes=[pltpu.VMEM((bsz, 1, w), F32)],
        compiler_params=_cparams(("arbitrary",)),
    )(dc.reshape(bsz, l, w), f_logit.reshape(bsz, l, w), b_pad)
    return df.reshape(t, w), db


GROUP_ROWS = A_GROUP * BLOCK


def _stack_heads(ref, g):
    return jnp.concatenate([ref[:, (A_GROUP * g + i) * LANES:(A_GROUP * g + i + 1) * LANES] for i in range(A_GROUP)],
                           axis=0)


def _unstack_heads(ref, g, x):
    for i in range(A_GROUP):
        ref[:, (A_GROUP * g + i) * LANES:(A_GROUP * g + i + 1) * LANES] = x[i * BLOCK:(i + 1) * BLOCK].astype(ref.dtype)


def _swa_logits(qk, slope, n):
    qi = lax.broadcasted_iota(jnp.int32, (GROUP_ROWS, BLOCK), 0) & (BLOCK - 1)
    kj = lax.broadcasted_iota(jnp.int32, (GROUP_ROWS, BLOCK), 1)
    s_all = qk * SCALE
    out = []
    for i, (dist, ok) in enumerate((
            (n * BLOCK + qi - kj, (kj >= N_PAD) & (n * BLOCK + qi - kj >= 0)),
            (BLOCK + qi - kj, (kj > qi) & (n >= 2)),
            (qi - kj, (kj <= qi) & (n >= 1)))):
        s = s_all[:, i * BLOCK:(i + 1) * BLOCK] - slope * dist.astype(F32)
        out.append(jnp.where(ok, s, NEG))
    return out


def _three_blocks(m_ref, p_ref, c_ref):
    return jnp.concatenate([m_ref[...], p_ref[...], c_ref[...]], axis=0)


def _swa_specs(bsz, nb):
    qspec = pl.BlockSpec((bsz, BLOCK, A_PAD_WIDTH), lambda n: (0, n, 0))
    kv_m = pl.BlockSpec((bsz, BLOCK, LANES), lambda n: (0, 0, 0))
    kv_p = pl.BlockSpec((bsz, BLOCK, LANES), lambda n: (0, jnp.maximum(n - 1, 0), 0))
    kv_c = pl.BlockSpec((bsz, BLOCK, LANES), lambda n: (0, n, 0))
    rowspec = pl.BlockSpec((A_KV_HEADS, GROUP_ROWS, 1), lambda n: (0, 0, 0))
    lsespec = pl.BlockSpec((bsz, 1, A_KV_HEADS, GROUP_ROWS, 1), lambda n: (0, n, 0, 0, 0))
    return qspec, kv_m, kv_p, kv_c, rowspec, lsespec


def _swa_fwd(q, k, v, sink_rows, slope_rows, nb):
    t = q.shape[0]
    l = nb * BLOCK
    bsz = t // l

    def body(q_ref, km_ref, kp_ref, kc_ref, vm_ref, vp_ref, vc_ref, sink_ref, slope_ref, o_ref, lse_ref):
        n = pl.program_id(0)
        lane_group = lax.broadcasted_iota(jnp.int32, (GROUP_ROWS, LANES), 1) // HEAD_DIM
        products = {}
        for b in range(bsz):
            keys = _three_blocks(km_ref.at[b], kp_ref.at[b], kc_ref.at[b])
            for g in range(A_KV_HEADS):
                products[b, g] = lax.dot_general(_stack_heads(q_ref.at[b], g), keys, NT_DIMS,
                                                 preferred_element_type=F32)
        for b in range(bsz):
            values = _three_blocks(vm_ref.at[b], vp_ref.at[b], vc_ref.at[b])
            for g in range(A_KV_HEADS):
                sink = sink_ref[g]
                s_m, s_p, s_c = _swa_logits(products[b, g], slope_ref[g], n)
                m = jnp.maximum(jnp.max(jnp.maximum(jnp.maximum(s_m, s_p), s_c), axis=-1, keepdims=True), sink)
                m_wide = jnp.broadcast_to(m, (GROUP_ROWS, BLOCK))
                e_m = jnp.exp(s_m - m_wide)
                e_p = jnp.exp(s_p - m_wide)
                e_c = jnp.exp(s_c - m_wide)
                z = jnp.sum((e_m + e_p) + e_c, axis=-1, keepdims=True) + jnp.exp(sink - m)
                inv = jnp.broadcast_to(1.0 / z, (GROUP_ROWS, BLOCK))
                probs = jnp.concatenate([(e_m * inv).astype(BF16), (e_p * inv).astype(BF16),
                                         (e_c * inv).astype(BF16)], axis=1)
                o = jnp.dot(probs, values, preferred_element_type=F32)
                _unstack_heads(o_ref.at[b], g, jnp.where(lane_group == g, o, 0.0))
                lse_ref[b, 0, g] = m + jnp.log(z)

    qspec, kv_m, kv_p, kv_c, rowspec, lsespec = _swa_specs(bsz, nb)
    by_example = lambda a: a.reshape(bsz, l, a.shape[1])
    q3, k3, v3 = by_example(q), by_example(k), by_example(v)
    o, lse = pl.pallas_call(
        body,
        name="swa_fwd",
        grid=(nb,),
        in_specs=[qspec, kv_m, kv_p, kv_c, kv_m, kv_p, kv_c, rowspec, rowspec],
        out_specs=[qspec, lsespec],
        out_shape=[jax.ShapeDtypeStruct((bsz, l, A_PAD_WIDTH), BF16),
                   jax.ShapeDtypeStruct((bsz, nb, A_KV_HEADS, GROUP_ROWS, 1), F32)],
        compiler_params=_cparams(("arbitrary",)),
    )(q3, k3, k3, k3, v3, v3, v3, sink_rows, slope_rows)
    return o.reshape(t, A_PAD_WIDTH), lse


def _swa_bwd(q, k, v, do, lse, sink_rows, slope_rows, nb):
    t = q.shape[0]
    l = nb * BLOCK
    bsz = t // l

    def body(q_ref, km_ref, kp_ref, kc_ref, vm_ref, vp_ref, vc_ref, do_ref, lse_ref, sink_ref, slope_ref,
             dq_ref, dk_ref, dv_ref, dsink_ref, dk_acc, dv_acc):
        n = pl.program_id(0)

        @pl.when(n == 0)
        def _():
            dk_acc[...] = jnp.zeros_like(dk_acc)
            dv_acc[...] = jnp.zeros_like(dv_acc)
            dsink_ref[...] = jnp.zeros_like(dsink_ref)

        first_half = lax.broadcasted_iota(jnp.int32, (BLOCK, LANES), 1) < HEAD_DIM
        prev = jnp.maximum(n - 1, 0)
        products = {}
        for b in range(bsz):
            keys = _three_blocks(km_ref.at[b], kp_ref.at[b], kc_ref.at[b])
            values = _three_blocks(vm_ref.at[b], vp_ref.at[b], vc_ref.at[b])
            for g in range(A_KV_HEADS):
                products[b, g] = (
                    lax.dot_general(_stack_heads(q_ref.at[b], g), keys, NT_DIMS, preferred_element_type=F32),
                    lax.dot_general(_stack_heads(do_ref.at[b], g), values, NT_DIMS, preferred_element_type=F32))
        for b in range(bsz):
            keys = _three_blocks(km_ref.at[b], kp_ref.at[b], kc_ref.at[b])
            for g in range(A_KV_HEADS):
                qq = _stack_heads(q_ref.at[b], g)
                dob = _stack_heads(do_ref.at[b], g)
                lse_g = lse_ref[b, 0, g]
                lse_wide = jnp.broadcast_to(lse_g, (GROUP_ROWS, BLOCK))
                qk, dp_all = products[b, g]
                probs = [jnp.exp(s - lse_wide) for s in _swa_logits(qk, slope_ref[g], n)]
                dps = [dp_all[:, i * BLOCK:(i + 1) * BLOCK] for i in range(3)]
                delta = jnp.sum((probs[0] * dps[0] + probs[1] * dps[1]) + probs[2] * dps[2], axis=-1, keepdims=True)
                delta_wide = jnp.broadcast_to(delta, (GROUP_ROWS, BLOCK))
                ds = jnp.concatenate([(p * (dp - delta_wide)).astype(BF16) for p, dp in zip(probs, dps)], axis=1)
                pb = jnp.concatenate([p.astype(BF16) for p in probs], axis=1)
                dq = jnp.dot(ds, keys, preferred_element_type=F32) * SCALE
                dk_all = lax.dot_general(ds, qq, TN_DIMS, preferred_element_type=F32) * SCALE
                dv_all = lax.dot_general(pb, dob, TN_DIMS, preferred_element_type=F32)
                for i, start in enumerate((0, prev * BLOCK, n * BLOCK)):
                    rows = pl.ds(pl.multiple_of(start, BLOCK), BLOCK)
                    dk_acc[b, rows, :] += dk_all[i * BLOCK:(i + 1) * BLOCK]
                    dv_acc[b, rows, :] += dv_all[i * BLOCK:(i + 1) * BLOCK]
                for pair in range(A_GROUP // 2):
                    even = dq[2 * pair * BLOCK:(2 * pair + 1) * BLOCK]
                    odd = dq[(2 * pair + 1) * BLOCK:(2 * pair + 2) * BLOCK]
                    left = even if g == 0 else pltpu.roll(even, HEAD_DIM, 1)
                    right = pltpu.roll(odd, HEAD_DIM, 1) if g == 0 else odd
                    tile = (A_GROUP // 2) * g + pair
                    dq_ref[b, :, tile * LANES:(tile + 1) * LANES] = jnp.where(first_half, left, right).astype(BF16)
                dsink_ref[b, g] += -(jnp.exp(sink_ref[g] - lse_g) * delta)

        @pl.when(n == nb - 1)
        def _():
            dk_ref[...] = dk_acc[...].astype(BF16)
            dv_ref[...] = dv_acc[...].astype(BF16)

    qspec, kv_m, kv_p, kv_c, rowspec, lsespec = _swa_specs(bsz, nb)
    kv_all = pl.BlockSpec((bsz, l, LANES), lambda n: (0, 0, 0))
    by_example = lambda a: a.reshape(bsz, l, a.shape[1])
    q3, k3, v3 = by_example(q), by_example(k), by_example(v)
    dq, dk, dv, dsink = pl.pallas_call(
        body,
        name="swa_bwd",
        grid=(nb,),
        in_specs=[qspec, kv_m, kv_p, kv_c, kv_m, kv_p, kv_c, qspec, lsespec, rowspec, rowspec],
        out_specs=[pl.BlockSpec((bsz, BLOCK, A_WIDTH), lambda n: (0, n, 0)), kv_all, kv_all,
                   pl.BlockSpec((bsz, A_KV_HEADS, GROUP_ROWS, 1), lambda n: (0, 0, 0, 0))],
        out_shape=[jax.ShapeDtypeStruct((bsz, l, A_WIDTH), BF16),
                   jax.ShapeDtypeStruct((bsz, l, LANES), BF16),
                   jax.ShapeDtypeStruct((bsz, l, LANES), BF16),
                   jax.ShapeDtypeStruct((bsz, A_KV_HEADS, GROUP_ROWS, 1), F32)],
        scratch_shapes=[pltpu.VMEM((bsz, l, LANES), F32), pltpu.VMEM((bsz, l, LANES), F32)],
        compiler_params=_cparams(("arbitrary",)),
    )(q3, k3, k3, k3, v3, v3, v3, by_example(do), lse, sink_rows, slope_rows)
    return dq.reshape(t, A_WIDTH), dk.reshape(t, LANES), dv.reshape(t, LANES), dsink


CHUNK = KEY_BLOCKS * BLOCK


def _fox_chunk(qb, ci):
    sb = jnp.maximum(jnp.minimum(KEY_BLOCKS * ci, qb + 1 - KEY_BLOCKS), 0)
    lo = jnp.maximum(ci * CHUNK, N_PAD)
    return sb, lo, pl.ds(pl.multiple_of(sb * BLOCK, BLOCK), CHUNK)


def _fox_logits(s_ref, cr_ref, e, j, sb, lo, qb):
    lane = lax.broadcasted_iota(jnp.int32, (BLOCK, BLOCK), 1)
    ahead = lane - lax.broadcasted_iota(jnp.int32, (BLOCK, BLOCK), 0)
    first = (sb + j) * BLOCK
    s = s_ref[e, :, j * BLOCK:(j + 1) * BLOCK] - cr_ref[e, sb + j]
    return jnp.where((ahead <= qb * BLOCK - first) & (lane >= lo - first), s, NEG)


FOX_PAIRS = 4
FOX_HEADS = 2 * FOX_PAIRS
FOX_STEPS = B_PAIRS // FOX_PAIRS


def _fox_specs(nb):
    l = nb * BLOCK
    q_spec = pl.BlockSpec((BLOCK, FOX_PAIRS * LANES), lambda b, p, i: (b * nb + i, p))
    kv_spec = pl.BlockSpec((l, FOX_HEADS * LANES), lambda b, p, i: (b, p))
    cc_spec = pl.BlockSpec((FOX_HEADS, BLOCK, 1), lambda b, p, i: (b * FOX_STEPS + p, i, 0))
    cr_spec = pl.BlockSpec((FOX_HEADS, nb, 1, BLOCK), lambda b, p, i: (b * FOX_STEPS + p, 0, 0, 0))
    return q_spec, kv_spec, cc_spec, cr_spec


def _fox_fwd(q, k, v, c_row, nb, exchange=None):
    t = q.shape[0]
    bsz = t // (nb * BLOCK)
    assert nb >= KEY_BLOCKS

    n_x = len(exchange.ins) if exchange else 0

    def body(*refs):
        q_ref, k_ref, v_ref, cr_ref = refs[:4]
        o_ref, ox_ref, lse_ref = refs[4 + n_x:7 + n_x]
        s_scr, hi_scr, lo_scr = refs[7 + 2 * n_x:10 + 2 * n_x]
        qb = pl.program_id(2)
        if exchange:
            first = (pl.program_id(0) == 0) & (pl.program_id(1) == 0)
            last = (pl.program_id(0) == bsz - 1) & (pl.program_id(1) == FOX_STEPS - 1)
            _host_exchange(exchange, refs[4:4 + n_x], refs[7 + n_x:7 + 2 * n_x], refs[10 + 2 * n_x:],
                           first & (qb == 0), first & (qb == 2 * nb // 3), last & (qb == nb - 1),
                           late=last & (qb == 0))
        qs = [q_ref[:, a * LANES:(a + 1) * LANES] * SCALE for a in range(FOX_PAIRS)]
        first_half = lax.broadcasted_iota(jnp.int32, (BLOCK, LANES), 1) < HEAD_DIM

        def step(ci, carry):
            stats, accs = carry[:2 * FOX_HEADS], carry[2 * FOX_HEADS:]
            sb, lo, krows = _fox_chunk(qb, ci)
            for e in range(FOX_HEADS):
                s_scr[e] = lax.dot_general(qs[e // 2], k_ref[krows, e * LANES:(e + 1) * LANES], NT_DIMS,
                                           preferred_element_type=F32)
            new_stats, new_accs = [], []
            for a in range(FOX_PAIRS):
                alphas = []
                pv = jnp.zeros((BLOCK, LANES), F32)
                pv_lo = jnp.zeros((BLOCK, LANES), F32)
                for e in (2 * a, 2 * a + 1):
                    m, z = stats[2 * e], stats[2 * e + 1]
                    tile = slice(e * LANES, (e + 1) * LANES)
                    top = None
                    for j in range(KEY_BLOCKS):
                        s = _fox_logits(s_scr, cr_ref, e, j, sb, lo, qb)
                        s_scr[e, :, j * BLOCK:(j + 1) * BLOCK] = s
                        top = s if top is None else jnp.maximum(top, s)
                    m_new = jnp.maximum(m, jnp.max(top, axis=-1, keepdims=True))
                    alpha = jnp.exp(m - m_new)
                    m_wide = jnp.broadcast_to(m_new, (BLOCK, BLOCK))
                    total = None
                    for j in range(KEY_BLOCKS):
                        cols = slice(j * BLOCK, (j + 1) * BLOCK)
                        p = jnp.exp(s_scr[e, :, cols] - m_wide)
                        total = p if total is None else total + p
                        hi = p.astype(BF16)
                        hi_scr[e, :, cols] = hi
                        lo_scr[e, :, cols] = (p - hi.astype(F32)).astype(BF16)
                    z = alpha * z + jnp.sum(total, axis=-1, keepdims=True)
                    vv = v_ref[krows, tile]
                    pv = pv + jnp.dot(hi_scr[e], vv, preferred_element_type=F32)
                    pv_lo = pv_lo + jnp.dot(lo_scr[e], vv, preferred_element_type=F32)
                    new_stats += [m_new, z]
                    alphas.append(alpha)
                alpha = jnp.where(first_half, alphas[0], alphas[1])
                new_accs += [alpha * accs[2 * a] + pv, alpha * accs[2 * a + 1] + pv_lo]
            return (*new_stats, *new_accs)

        col = lambda val: jnp.full((BLOCK, 1), val, F32)
        done = lax.fori_loop(
            0, (qb + KEY_BLOCKS) // KEY_BLOCKS, step,
            (col(NEG), col(0.0)) * FOX_HEADS + (jnp.zeros((BLOCK, LANES), F32),) * (2 * FOX_PAIRS))
        for a in range(FOX_PAIRS):
            m0, z0, m1, z1 = done[4 * a:4 * a + 4]
            acc, acc_lo = done[2 * FOX_HEADS + 2 * a:2 * FOX_HEADS + 2 * a + 2]
            inv = 1.0 / jnp.where(first_half, z0, z1)
            tile = slice(a * LANES, (a + 1) * LANES)
            o_ref[:, tile] = (acc * inv).astype(BF16)
            ox_ref[:, tile] = (acc + acc_lo) * inv
            lse_ref[2 * a] = m0 + jnp.log(z0)
            lse_ref[2 * a + 1] = m1 + jnp.log(z1)

    q_spec, kv_spec, cc_spec, cr_spec = _fox_specs(nb)
    outs = pl.pallas_call(
        body,
        name="fox_fwd",
        grid=(bsz, FOX_STEPS, nb),
        in_specs=[q_spec, kv_spec, kv_spec, cr_spec] + [HBM_SPEC] * n_x,
        out_specs=[q_spec, q_spec, cc_spec] + [HBM_SPEC] * n_x,
        out_shape=[jax.ShapeDtypeStruct((t, B_WIDTH), BF16), jax.ShapeDtypeStruct((t, B_WIDTH), F32),
                   jax.ShapeDtypeStruct((bsz * B_HEADS, nb * BLOCK, 1), F32)] + (exchange.out_shape if exchange else []),
        scratch_shapes=[pltpu.VMEM((FOX_HEADS, BLOCK, CHUNK), F32), pltpu.VMEM((FOX_HEADS, BLOCK, CHUNK), BF16),
                        pltpu.VMEM((FOX_HEADS, BLOCK, CHUNK), BF16)] + (exchange.scratch if exchange else []),
        compiler_params=_cparams(("arbitrary",) * 3 if exchange else ("parallel", "parallel", "arbitrary")),
    )(q, k, v, c_row, *(exchange.ins if exchange else []))
    return outs[:3], outs[3:]


def _fox_bwd(q, k, v, o_exact, do, lse, c_row, nb, exchange=None):
    t = q.shape[0]
    l = nb * BLOCK
    bsz = t // l

    n_x = len(exchange.ins) if exchange else 0

    def body(*refs):
        q_ref, k_ref, v_ref, ox_ref, do_ref, lse_ref, cr_ref = refs[:7]
        dq_ref, dk_ref, dv_ref, dc_ref = refs[7 + n_x:11 + n_x]
        dk_acc, dv_acc, s_scr, dp_scr, p_scr, ds_scr, dq_scr = refs[11 + 2 * n_x:18 + 2 * n_x]
        qb = pl.program_id(2)
        if exchange:
            first = (pl.program_id(0) == 0) & (pl.program_id(1) == 0)
            last = (pl.program_id(0) == bsz - 1) & (pl.program_id(1) == FOX_STEPS - 1)
            _host_exchange(exchange, refs[7:7 + n_x], refs[11 + n_x:11 + 2 * n_x], refs[18 + 2 * n_x:],
                           first & (qb == 0), last & (qb == 0), last & (qb == nb - 1))

        @pl.when(qb == 0)
        def _():
            dk_acc[...] = jnp.zeros_like(dk_acc)
            dv_acc[...] = jnp.zeros_like(dv_acc)
            dc_ref[...] = jnp.zeros_like(dc_ref)

        top_half = lax.broadcasted_iota(jnp.int32, (LANES, BLOCK), 0) < HEAD_DIM
        pair_t = lambda x: jnp.concatenate([jnp.where(top_half, x.T, 0), jnp.where(top_half, 0, x.T)], axis=1)
        first_half = lax.broadcasted_iota(jnp.int32, (BLOCK, LANES), 1) < HEAD_DIM
        wide = lambda col: jnp.broadcast_to(col, (BLOCK, BLOCK))
        qs, dobs, qs_t, dob_t, deltas = [], [], [], [], []
        for a in range(FOX_PAIRS):
            tile = slice(a * LANES, (a + 1) * LANES)
            qs.append(q_ref[:, tile] * SCALE)
            dobs.append(do_ref[:, tile])
            qs_t.append(pair_t(qs[a]))
            dob_t.append(pair_t(dobs[a]))
            weighted = dobs[a].astype(F32) * ox_ref[:, tile]
            deltas += [wide(jnp.sum(jnp.where(first_half, weighted, 0.0), axis=-1, keepdims=True)),
                       wide(jnp.sum(jnp.where(first_half, 0.0, weighted), axis=-1, keepdims=True))]
        lses = [wide(lse_ref[e]) for e in range(FOX_HEADS)]

        dq_scr[...] = jnp.zeros(dq_scr.shape, F32)

        def step(ci, carry):
            sb, lo, krows = _fox_chunk(qb, ci)
            for e in range(FOX_HEADS):
                tile = slice(e * LANES, (e + 1) * LANES)
                s_scr[e] = lax.dot_general(qs[e // 2], k_ref[krows, tile], NT_DIMS, preferred_element_type=F32)
                dp_scr[e] = lax.dot_general(dobs[e // 2], v_ref[krows, tile], NT_DIMS, preferred_element_type=F32)
            for a in range(FOX_PAIRS):
                for e in (2 * a, 2 * a + 1):
                    tile = slice(e * LANES, (e + 1) * LANES)
                    kk = k_ref[krows, tile]
                    for j in range(KEY_BLOCKS):
                        cols = slice(j * BLOCK, (j + 1) * BLOCK)
                        p = jnp.exp(_fox_logits(s_scr, cr_ref, e, j, sb, lo, qb) - lses[e])
                        ds = p * (dp_scr[e, :, cols] - deltas[e])
                        dc_ref[e, sb + j] -= jnp.sum(ds, axis=0, keepdims=True)
                        p_scr[e, :, cols] = p.astype(BF16)
                        ds_scr[e, :, cols] = ds.astype(BF16)
                    dq_scr[a] += jnp.dot(ds_scr[e], kk, preferred_element_type=F32)
                both = slice(2 * a, 2 * a + 2)
                dk_t = jnp.dot(qs_t[a], ds_scr[both].reshape(2 * BLOCK, CHUNK), preferred_element_type=F32)
                dv_t = jnp.dot(dob_t[a], p_scr[both].reshape(2 * BLOCK, CHUNK), preferred_element_type=F32)
                for j in range(KEY_BLOCKS):
                    cols = slice(j * BLOCK, (j + 1) * BLOCK)
                    dk_acc[a * nb + sb + j] += dk_t[:, cols]
                    dv_acc[a * nb + sb + j] += dv_t[:, cols]
            return carry

        lax.fori_loop(0, (qb + KEY_BLOCKS) // KEY_BLOCKS, step, 0)
        for a in range(FOX_PAIRS):
            dq_ref[:, a * LANES:(a + 1) * LANES] = (dq_scr[a] * SCALE).astype(BF16)

        @pl.when(qb == nb - 1)
        def _():
            for a in range(FOX_PAIRS):
                for kb in range(nb):
                    rows = slice(kb * BLOCK, (kb + 1) * BLOCK)
                    for acc, out_ref in ((dk_acc, dk_ref), (dv_acc, dv_ref)):
                        out_ref[rows, a * LANES:(a + 1) * LANES] = acc[a * nb + kb].T.astype(BF16)

    q_spec, kv_spec, cc_spec, cr_spec = _fox_specs(nb)
    dkv_spec = pl.BlockSpec((l, FOX_PAIRS * LANES), lambda b, p, i: (b, p))
    outs = pl.pallas_call(
        body,
        name="fox_bwd",
        grid=(bsz, FOX_STEPS, nb),
        in_specs=[q_spec, kv_spec, kv_spec, q_spec, q_spec, cc_spec, cr_spec] + [HBM_SPEC] * n_x,
        out_specs=[q_spec, dkv_spec, dkv_spec, cr_spec] + [HBM_SPEC] * n_x,
        out_shape=[jax.ShapeDtypeStruct((t, B_WIDTH), BF16), jax.ShapeDtypeStruct((t, B_WIDTH), BF16),
                   jax.ShapeDtypeStruct((t, B_WIDTH), BF16),
                   jax.ShapeDtypeStruct((bsz * B_HEADS, nb, 1, BLOCK), F32)] + (exchange.out_shape if exchange else []),
        scratch_shapes=[pltpu.VMEM((FOX_PAIRS * nb, LANES, BLOCK), F32), pltpu.VMEM((FOX_PAIRS * nb, LANES, BLOCK), F32),
                        pltpu.VMEM((FOX_HEADS, BLOCK, CHUNK), F32), pltpu.VMEM((FOX_HEADS, BLOCK, CHUNK), F32),
                        pltpu.VMEM((FOX_HEADS, BLOCK, CHUNK), BF16), pltpu.VMEM((FOX_HEADS, BLOCK, CHUNK), BF16),
                        pltpu.VMEM((FOX_PAIRS, BLOCK, LANES), F32)]
        + (exchange.scratch if exchange else []),
        compiler_params=_cparams(("arbitrary",) * 3 if exchange else ("parallel", "parallel", "arbitrary")),
    )(q, k, v, o_exact, do, lse, c_row, *(exchange.ins if exchange else []))
    return outs[:4], outs[4:]


def _loss_head(h, final_w, target):
    bsz, l, d = h.shape
    nb = l // BLOCK

    def body(h_ref, w_ref, t_ref, loss_ref, dh_ref, dw_ref):
        n = pl.program_id(0)

        @pl.when(n == 0)
        def _():
            loss_ref[...] = jnp.zeros_like(loss_ref)
            dw_ref[...] = jnp.zeros_like(dw_ref)
            dh_ref[...] = jnp.zeros_like(dh_ref)

        @pl.when(n > 0)
        def _():
            w = w_ref[...]
            for b in range(bsz):
                hh = h_ref[b]
                r = _rms_scale(hh)
                err = (hh * r) * w - t_ref[b]
                loss_ref[...] += 0.5 * jnp.sum(jnp.mean(err * err, axis=-1, keepdims=True), axis=0, keepdims=True)
                dy = err * (1.0 / d)
                dh, dw = _rms_bwd(dy, hh, w)
                dh_ref[b] = dh
                dw_ref[...] += dw

    return pl.pallas_call(
        body,
        name="loss_head",
        grid=(nb,),
        in_specs=[
            pl.BlockSpec((bsz, BLOCK, d), lambda n: (0, n, 0)),
            pl.BlockSpec((1, d), lambda n: (0, 0)),
            pl.BlockSpec((bsz, BLOCK, d), lambda n: (0, jnp.maximum(n - 1, 0), 0)),
        ],
        out_specs=[
            pl.BlockSpec((1, 128), lambda n: (0, 0)),
            pl.BlockSpec((bsz, BLOCK, d), lambda n: (0, n, 0)),
            pl.BlockSpec((1, d), lambda n: (0, 0)),
        ],
        out_shape=[jax.ShapeDtypeStruct((1, 128), F32), jax.ShapeDtypeStruct((bsz, l, d), F32),
                   jax.ShapeDtypeStruct((1, d), F32)],
        compiler_params=_cparams(("arbitrary",)),
    )(h, final_w, target)


def _pad_tiles(w, src, heads, lane_slot, axis):
    pieces = []
    for h in range(heads):
        x = lax.slice_in_dim(w, src + HEAD_DIM * h, src + HEAD_DIM * (h + 1), axis=axis)
        z = jnp.zeros_like(x)
        pieces += [x, z] if lane_slot(h) == 0 else [z, x]
    return pieces


def _unpad_tiles(g, off, heads, lane_slot, axis):
    return [lax.slice_in_dim(g, off + LANES * h + HEAD_DIM * lane_slot(h),
                             off + LANES * h + HEAD_DIM * (lane_slot(h) + 1), axis=axis) for h in range(heads)]


REF_RUNS = ((0, SRC_QB, 0, 0), (SRC_QB, SRC_F, 1, 0), (SRC_F, SRC_GA, 0, OFF_F), (SRC_GA, W_IN_COLS, 2, 0))
P_RUNS = ((0, OFF_F, 0), (OFF_F, OFF_F + B_HEADS, SRC_F), (OFF_QB, OFF_GA, SRC_QB), (OFF_GA, P_COLS, SRC_GA))
SHARD_COLS = W_IN_COLS // N_CHIPS
SHARD_PAD_COLS = -(-SHARD_COLS // LANES) * LANES


def _place(tile, lane, src_ref, src, dst, length):
    for t in range(src // LANES, (src + length - 1) // LANES + 1):
        x = src_ref[:, t * LANES:(t + 1) * LANES].astype(F32)
        shift = (dst - src) % LANES
        moved = pltpu.roll(x, shift, 1) if shift else x
        from_t = (lane >= max(dst, dst + t * LANES - src)) & (lane < min(dst + length, dst + (t + 1) * LANES - src))
        tile = jnp.where(from_t, moved, tile)
    return tile


def _layout_w_in(stacked):
    d = stacked.shape[1]
    rows = _tile(d, 256)

    def body(s_ref, o_ref):
        lane = lax.broadcasted_iota(jnp.int32, (rows, LANES), 1)
        for lo in range(0, P_COLS, LANES):
            tile = jnp.zeros((rows, LANES), F32)
            for first, end, ref_col in P_RUNS:
                start, stop = max(lo, first), min(lo + LANES, end)
                while start < stop:
                    k, src = divmod(ref_col + start - first, SHARD_COLS)
                    length = min(stop - start, SHARD_COLS - src)
                    tile = _place(tile, lane, s_ref.at[k], src, start - lo, length)
                    start += length
            o_ref[:, lo:lo + LANES] = tile.astype(o_ref.dtype)

    return pl.pallas_call(
        body,
        name="layout_w_in",
        grid=(d // rows,),
        in_specs=[pl.BlockSpec((N_CHIPS, rows, SHARD_PAD_COLS), lambda i: (0, i, 0))],
        out_specs=pl.BlockSpec((rows, P_COLS), lambda i: (i, 0)),
        out_shape=jax.ShapeDtypeStruct((d, P_COLS), stacked.dtype),
        compiler_params=_cparams(("parallel",)),
    )(stacked)


def _stack_w_in_grad(pieces):
    d = pieces[0].shape[0]
    rows = _tile(d, 256)

    def body(*refs):
        piece_refs, o_ref = refs[:-1], refs[-1]
        lane = lax.broadcasted_iota(jnp.int32, (rows, LANES), 1)
        for k in range(N_CHIPS):
            for j in range(0, SHARD_COLS, LANES):
                width = min(LANES, SHARD_COLS - j)
                lo = k * SHARD_COLS + j
                tile = jnp.zeros((rows, LANES), F32)
                for first, end, piece, at in REF_RUNS:
                    start, stop = max(lo, first), min(lo + width, end)
                    if start >= stop:
                        continue
                    tile = _place(tile, lane, piece_refs[piece], at + start - first, start - lo, stop - start)
                o_ref[k, :, j:j + width] = tile[:, :width]

    return pl.pallas_call(
        body,
        name="stack_w_in_grad",
        grid=(d // rows,),
        in_specs=[pl.BlockSpec((rows, p.shape[1]), lambda i: (i, 0)) for p in pieces],
        out_specs=pl.BlockSpec((N_CHIPS, rows, SHARD_COLS), lambda i: (0, i, 0)),
        out_shape=jax.ShapeDtypeStruct((N_CHIPS, d, SHARD_COLS), F32),
        compiler_params=_cparams(("parallel",)),
    )(*pieces)


def _local_step(x, target, meta, norms, b_forget, sinks, w, comm=None):
    n1, nmix, n2, nfin = norms
    w1i, w1o = w[:2]
    if meta is not None:
        x = jnp.concatenate([jnp.zeros((x.shape[0], N_PAD, x.shape[2]), F32),
                             jnp.broadcast_to(meta[None], (x.shape[0], N_META, x.shape[2])), x], axis=1)
    bsz, l, d = x.shape
    nb = l // BLOCK
    t = bsz * l
    h0 = x.reshape(t, d)

    if comm is None:
        (h1, g1, u1), _ = _ffn_fwd(h0, n1, w1i, w1o)
        w_in, wa, wb, wo, w2i, w2o = w[2:]
        w_in = jnp.pad(w_in.reshape(d, N_CHIPS, SHARD_COLS).transpose(1, 0, 2),
                       ((0, 0), (0, 0), (0, SHARD_PAD_COLS - SHARD_COLS)))
    else:
        (h1, g1, u1), (w_in,) = _ffn_fwd(h0, n1, w1i, w1o, comm.gather(GATHER_PROJ))
    wp = _layout_w_in(w_in)
    un, qa, ka, va, qb, kb, vb, ga, gb, f_logit = _proj_fwd(h1, nmix, wp)
    b_pad = jnp.concatenate([b_forget, jnp.zeros((1, F_COLS - B_HEADS), F32)], axis=1)
    c = _forget_cumsum(f_logit, b_pad, nb)
    c_heads = c[:, :B_HEADS].reshape(bsz, l, B_HEADS).transpose(0, 2, 1).reshape(bsz * B_HEADS, l)
    c_row = c_heads.reshape(bsz * B_HEADS, nb, 1, BLOCK)

    slopes = jnp.exp2(-8.0 * jnp.arange(1, A_HEADS + 1, dtype=F32) / A_HEADS)
    slope_rows = jnp.repeat(slopes.reshape(A_KV_HEADS, A_GROUP), BLOCK, axis=1)[:, :, None]
    sink_rows = jnp.repeat(sinks.reshape(A_KV_HEADS, A_GROUP), BLOCK, axis=1)[:, :, None]

    oa, lse_a = _swa_fwd(qa, ka, va, sink_rows, slope_rows, nb)
    if comm is None:
        (ob, ob_exact, lse_b), _ = _fox_fwd(qb, kb, vb, c_row, nb)
    else:
        (ob, ob_exact, lse_b), (wa, wb, wo, w2i, w2o) = _fox_fwd(qb, kb, vb, c_row, nb, comm.gather(GATHER_LATE))
    wa_p = jnp.concatenate(_pad_tiles(wa, 0, A_HEADS, A_SLOT, 0), axis=0)
    h2, mixed = _merge_fwd(h1, oa, ob, ga, gb, wa_p, wb, wo)
    (h3, g2, u2), _ = _ffn_fwd(h2, n2, w2i, w2o)
    loss, dh3, d_nfin = _loss_head(h3.reshape(bsz, l, d), nfin, target)

    (dh2, n2b, a2, dgu2, df2, dn2_parts), _ = _ffn_bwd(dh3.reshape(t, d), h2, n2, g2, u2, w2i, w2o)
    g_w2o = _tn_matmul(a2, df2, "grad_ffn2_w_out")
    g_w2i = _tn_matmul(n2b, dgu2, "grad_ffn2_w_in")

    hosted = comm.swap("ffn2", dict(ffn2_w_in=g_w2i, ffn2_w_out=g_w2o)) if comm else None
    (dya, dyb, doa, dob, dgates, dh2b), swapped = _merge_bwd(dh2, oa, ob, ga, gb, wa_p, wb, wo, hosted)
    g_wo = _tn_matmul(mixed, dh2b, "grad_w_out")
    g_wa = jnp.concatenate(_unpad_tiles(_tn_matmul(oa, dya, "grad_w_branch_a"), 0, A_HEADS, A_SLOT, 0), axis=0)
    g_wb = _tn_matmul(ob, dyb, "grad_w_branch_b")

    dqa, dka, dva, dsink_rows = _swa_bwd(qa, ka, va, doa, lse_a, sink_rows, slope_rows, nb)
    hosted = comm.scatter("ffn2", swapped) if comm else None
    (dqb, dkb, dvb, dc_row), pieces = _fox_bwd(qb, kb, vb, ob_exact, dob, lse_b, c_row, nb, hosted)
    if comm:
        comm.received("ffn2", pieces)
    dc = dc_row.reshape(bsz, B_HEADS, l).transpose(0, 2, 1).reshape(t, B_HEADS)
    dc = jnp.concatenate([dc, jnp.zeros((t, F_COLS - B_HEADS), F32)], axis=1)
    df_logit, db_parts = _forget_cumsum_bwd(dc, f_logit, b_pad, nb)

    dproj = (jnp.concatenate([dqa, dka, dva, df_logit], axis=1), jnp.concatenate([dqb, dkb, dvb], axis=1), dgates)
    g_win = _stack_w_in_grad([_tn_matmul(un, piece, "grad_w_in_" + tag) for piece, tag in zip(dproj, ("a", "b", "gates"))])
    if comm is None:
        g_win = g_win.transpose(1, 0, 2).reshape(d, W_IN_COLS)
    hosted = comm.swap("mixer", dict(w_in=g_win, w_branch_a=g_wa, w_branch_b=g_wb, w_out=g_wo)) if comm else None
    (dh1, dnmix_parts), swapped = _proj_bwd(dh2, h1, nmix, dproj, wp, hosted)
    hosted = comm.scatter("mixer", swapped) if comm else None
    (dh0, n1b, a1, dgu1, df1, dn1_parts), pieces = _ffn_bwd(dh1, h0, n1, g1, u1, w1i, w1o, hosted)
    dh0 = dh0.reshape(bsz, l, d)
    grad_x = dh0[:, PREFIX:]
    small = dict(
        meta_tokens=jnp.sum(dh0[:, N_PAD:PREFIX], axis=0),
        ffn1_norm=jnp.sum(dn1_parts, axis=0),
        mix_norm=jnp.sum(dnmix_parts, axis=0),
        ffn2_norm=jnp.sum(dn2_parts, axis=0),
        final_norm=d_nfin,
        b_forget=jnp.sum(db_parts, axis=0)[:, :B_HEADS],
        attn_sinks=jnp.sum(dsink_rows.reshape(bsz, A_HEADS, BLOCK), axis=(0, 2)).reshape(1, A_HEADS),
    )
    if comm is None:
        g_w1o = _tn_matmul(a1, df1, "grad_ffn1_w_out")
        g_w1i = _tn_matmul(n1b, dgu1, "grad_ffn1_w_in")
    else:
        comm.received("mixer", pieces)
        g_w1o, gathered = _tn_matmul(a1, df1, "grad_ffn1_w_out", comm.small_gather(loss, small))
        comm.small_gathered(gathered)
        swapped = _run_exchange(comm.swap("ffn1_out", dict(ffn1_w_out=g_w1o)), "exchange_halves_ffn1_out")
        g_w1i, pieces = _tn_matmul(n1b, dgu1, "grad_ffn1_w_in", comm.scatter("ffn1_out", swapped))
        comm.received("ffn1_out", pieces)
    big = dict(ffn1_w_in=g_w1i, ffn1_w_out=g_w1o, w_in=g_win, w_branch_a=g_wa, w_branch_b=g_wb,
               w_out=g_wo, ffn2_w_in=g_w2i, ffn2_w_out=g_w2o)
    return loss, grad_x, small, big


BIG = (
    ("ffn1_w_in", (D_MODEL, 5632), 1),
    ("ffn1_w_out", (2816, D_MODEL), 0),
    ("w_in", (D_MODEL, W_IN_COLS), 1),
    ("w_branch_a", (A_WIDTH, D_MODEL), 1),
    ("w_branch_b", (B_WIDTH, D_MODEL), 1),
    ("w_out", (D_MODEL, D_MODEL), 0),
    ("ffn2_w_in", (D_MODEL, 5632), 1),
    ("ffn2_w_out", (2816, D_MODEL), 0),
)
STACKED = "w_in"


def _coords():
    return lax.axis_index("x"), lax.axis_index("y"), lax.axis_index("c")


def _other_chips(x, y):
    return ((1 - x, y), (x, 1 - y), (1 - x, 1 - y))


def _chip_part(ref, name, shape, axis, k):
    if name == STACKED:
        return ref.at[k]
    size = shape[axis] // N_CHIPS
    start = pl.multiple_of(k * size, size)
    return ref.at[pl.ds(start, size), :] if axis == 0 else ref.at[:, pl.ds(start, size)]


def _full_shape(name, shape):
    return (N_CHIPS, shape[0], shape[1] // N_CHIPS) if name == STACKED else shape


class _Exchange:
    def __init__(self, ins, out_shape, n_sems, ops):
        self.ins, self.out_shape, self.n_sems, self.ops = list(ins), list(out_shape), n_sems, ops

    @property
    def scratch(self):
        return [pltpu.SemaphoreType.DMA((self.n_sems,)), pltpu.SemaphoreType.DMA((self.n_sems,))]


SEMS_PER_GATHER = 9


def _gather_exchange(shards, table):
    n = len(table)
    x_nbr, y_nbr, diagonal = 0, 1, 2

    def ops(ins, outs, send_sems, recv_sems):
        x, y, c = _coords()
        mine = 2 * x + y
        sibling = (x, y, 1 - c)
        chips = _other_chips(x, y)
        slots = [2 * chip[0] + chip[1] for chip in chips]

        def part(i, k):
            name, shape, axis = table[i][:3]
            return _chip_part(outs[i], name, shape, axis, k)

        def half(ref, h):
            rows = ref.shape[0] // 2
            return ref.at[pl.ds(pl.multiple_of(h * rows, rows), rows), :]

        def remote(i, sem, src, dst, device):
            sem = SEMS_PER_GATHER * i + sem
            return pltpu.make_async_remote_copy(src, dst, send_sems.at[sem], recv_sems.at[sem],
                                                device_id=device, device_id_type=MESH_ID)

        def own(i):
            return remote(i, 0, ins[i], part(i, mine), sibling)

        def fetch(i, j, slot):
            if table[i][4]:
                src, dst = half(ins[i], c), half(part(i, slot), c)
            else:
                src, dst = ins[i], part(i, slot)
            return remote(i, 1 + j, src, dst, (chips[j][0], chips[j][1], c))

        def relayed(i, via, of):
            region = half(half(part(i, slots[of]), c), via)
            return remote(i, 4 + via, region, region, (chips[via][0], chips[via][1], c))

        def forward(i, j, h):
            region = half(part(i, slots[j]), h)
            return remote(i, 6 + j, region, region, sibling)

        def start():
            for i in range(n):
                for j in (x_nbr, y_nbr) if table[i][4] else (x_nbr, y_nbr, diagonal):
                    fetch(i, j, mine).start()
            for i in range(n):
                own(i).start()

        def relay():
            for i in range(n):
                if not table[i][4]:
                    for j in range(3):
                        fetch(i, j, slots[j]).wait_recv()
                    continue
                fetch(i, y_nbr, slots[y_nbr]).wait_recv()
                relayed(i, x_nbr, y_nbr).start()
                forward(i, y_nbr, c).start()
                fetch(i, x_nbr, slots[x_nbr]).wait_recv()
                relayed(i, y_nbr, x_nbr).start()
                forward(i, x_nbr, c).start()

        def relay_diagonal():
            for i in range(n):
                if table[i][4]:
                    relayed(i, x_nbr, diagonal).wait_recv()
                    relayed(i, y_nbr, diagonal).wait_recv()
                    forward(i, diagonal, c).start()

        def finish():
            for i in range(n):
                own(i).wait()
                for j in range(3):
                    if table[i][4]:
                        forward(i, j, 1 - c).wait_recv()
                        forward(i, j, c).wait_send()
                    if j != diagonal or not table[i][4]:
                        fetch(i, j, mine).wait_send()
                if table[i][4]:
                    relayed(i, x_nbr, y_nbr).wait_send()
                    relayed(i, y_nbr, x_nbr).wait_send()

        return start, relay, relay_diagonal, finish

    out_shape = [jax.ShapeDtypeStruct(_full_shape(name, shape), dtype) for name, shape, _, dtype, _ in table]
    return _Exchange(shards, out_shape, SEMS_PER_GATHER * n, ops)


STREAM_ROWS = 256


def _stream_rows(src, dst, dst_first, buf, sem_in, sem_out):
    bsz, n_rows, _ = src.shape
    chunks = [(b, r) for b in range(bsz) for r in range(0, n_rows, STREAM_ROWS)]

    def load(i):
        b, r = chunks[i]
        return pltpu.make_async_copy(src.at[b, pl.ds(r, STREAM_ROWS), :], buf.at[i % 2], sem_in.at[i % 2])

    def store(i):
        b, r = chunks[i]
        return pltpu.make_async_copy(buf.at[i % 2], dst.at[b, pl.ds(dst_first + r, STREAM_ROWS), :],
                                     sem_out.at[i % 2])

    load(0).start()
    for i in range(len(chunks)):
        if i + 1 < len(chunks):
            if i >= 1:
                store(i - 1).wait()
            load(i + 1).start()
        load(i).wait()
        store(i).start()
    for i in range(max(len(chunks) - 2, 0), len(chunks)):
        store(i).wait()


def _run_exchange(exchange, name):
    n = len(exchange.ins)

    def body(*refs):
        for phase in exchange.ops(refs[:n], refs[n:2 * n], *refs[2 * n:]):
            phase()

    return pl.pallas_call(
        body,
        name=name,
        in_specs=[HBM_SPEC] * n,
        out_specs=[HBM_SPEC] * n,
        out_shape=exchange.out_shape,
        scratch_shapes=exchange.scratch,
    )(*exchange.ins)


CAST_ROWS = 64


def _cast_hosting(arrays, exchange, name, stream=None):
    n_a, n_x = len(arrays), len(exchange.ins)
    k = 1 if stream else 0

    def body(*refs):
        a_in, x_in = refs[:n_a], refs[n_a:n_a + n_x]
        outs = refs[n_a + n_x + k:]
        a_out, x_out = outs[:n_a], outs[n_a:n_a + n_x]
        scratch = outs[n_a + n_x + k:]
        start, *rest = exchange.ops(x_in, x_out, *scratch[:2])
        start()
        for src, dst in zip(a_in, a_out):
            def rows(i, carry, src=src, dst=dst):
                window = pl.ds(pl.multiple_of(i * CAST_ROWS, CAST_ROWS), CAST_ROWS)
                cols = src.shape[1]
                if dst.shape[1] != cols:
                    dst[window, dst.shape[1] - LANES:] = jnp.zeros((CAST_ROWS, LANES), BF16)
                dst[window, :cols] = src[window, :].astype(BF16)
                return carry

            lax.fori_loop(0, src.shape[0] // CAST_ROWS, rows, 0)
        if stream:
            x_ref, h_ref = refs[n_a + n_x], outs[n_a + n_x]
            buf, sem_in, sem_out = scratch[2:]
            _stream_rows(x_ref, h_ref, PREFIX, buf, sem_in, sem_out)
        for phase in rest:
            phase()
        if stream:
            meta = pltpu.make_async_copy(x_out[stream[1]], buf.at[1, pl.ds(0, N_META), :], sem_in.at[0])
            meta.start()
            buf[0, 0:N_PAD, :] = jnp.zeros((N_PAD, buf.shape[2]), F32)
            meta.wait()
            buf[0, N_PAD:PREFIX, :] = buf[1, 0:N_META, :]
            puts = [pltpu.make_async_copy(buf.at[0, pl.ds(0, PREFIX), :], h_ref.at[b, pl.ds(0, PREFIX), :], sem_out.at[b])
                    for b in range(h_ref.shape[0])]
            for put in puts:
                put.start()
            for put in puts:
                put.wait()

    extra_in, extra_out, extra_scratch = [], [], []
    if stream:
        x = stream[0]
        assert x.shape[0] <= 2 and x.shape[1] % STREAM_ROWS == 0 and x.dtype == F32
        extra_in = [x]
        extra_out = [jax.ShapeDtypeStruct((x.shape[0], PREFIX + x.shape[1], x.shape[2]), F32)]
        extra_scratch = [pltpu.VMEM((2, STREAM_ROWS, x.shape[2]), F32),
                         pltpu.SemaphoreType.DMA((2,)), pltpu.SemaphoreType.DMA((2,))]
    outs = pl.pallas_call(
        body,
        name=name,
        in_specs=[VMEM_SPEC] * n_a + [HBM_SPEC] * (n_x + k),
        out_specs=[VMEM_SPEC] * n_a + [HBM_SPEC] * (n_x + k),
        out_shape=[jax.ShapeDtypeStruct((a.shape[0], -(-a.shape[1] // LANES) * LANES), BF16) for a in arrays]
        + exchange.out_shape + extra_out,
        scratch_shapes=exchange.scratch + extra_scratch,
        compiler_params=pltpu.CompilerParams(vmem_limit_bytes=VMEM_LIMIT),
    )(*arrays, *exchange.ins, *extra_in)
    return outs[:n_a], outs[n_a:]


def _host_exchange(exchange, in_refs, out_refs, sem_refs, first, middle, last, late=None):
    start, *relays, finish = exchange.ops(in_refs, out_refs, *sem_refs)
    pl.when(first)(start)
    pl.when(middle)(relays[0])
    if len(relays) > 1:
        pl.when(last if late is None else late)(relays[1])
    pl.when(last)(finish)


def _halves_view(name, shape, axis):
    r, c = shape
    if name == STACKED:
        return (N_CHIPS, 2, r // 2, c // N_CHIPS), lambda ref, h: ref.at[:, h]
    if axis == 1:
        return (2, r // 2, c), lambda ref, h: ref.at[h]
    return (N_CHIPS, 2, r // N_CHIPS // 2, c), lambda ref, h: ref.at[:, h]


def _halves_exchange(grads, entries):
    n_w = len(entries)
    views = [_halves_view(*entry) for entry in entries]

    def ops(ins, outs, send_sems, recv_sems):
        x, y, c = _coords()
        copies = [pltpu.make_async_remote_copy(views[i][1](ins[i], 1 - c), outs[i], send_sems.at[i], recv_sems.at[i],
                                               device_id=(x, y, 1 - c), device_id_type=MESH_ID) for i in range(n_w)]

        def start():
            for cp in copies:
                cp.start()

        def finish():
            for cp in copies:
                cp.wait()

        return start, lambda: None, finish

    half_shape = lambda v: tuple(d for i, d in enumerate(v) if i != (1 if len(v) == 4 else 0))
    out_shape = [jax.ShapeDtypeStruct(half_shape(v[0]), F32) for v in views]
    return _Exchange([g.reshape(v[0]) for g, v in zip(grads, views)], out_shape, n_w, ops)


def _add_sibling(g_view, recv, c, name):
    shape = recv.shape
    if len(shape) == 2:
        tr = _tile(shape[0], 128, 16)
        grid = (shape[0] // tr,)
        g_spec = pl.BlockSpec((None, tr, shape[1]), lambda i, c_ref: (c_ref[0], i, 0))
        r_spec = pl.BlockSpec((tr, shape[1]), lambda i, c_ref: (i, 0))
    else:
        tr = _tile(shape[1], 256, 16)
        grid = (N_CHIPS, shape[1] // tr)
        g_spec = pl.BlockSpec((None, None, tr, shape[2]), lambda k, i, c_ref: (k, c_ref[0], i, 0))
        r_spec = pl.BlockSpec((None, tr, shape[2]), lambda k, i, c_ref: (k, i, 0))

    def body(c_ref, g_ref, r_ref, o_ref):
        o_ref[...] = (g_ref[...] + r_ref[...]).astype(BF16)

    return pl.pallas_call(
        body,
        name="add_sibling_" + name,
        grid_spec=pltpu.PrefetchScalarGridSpec(num_scalar_prefetch=1, grid=grid, in_specs=[g_spec, r_spec],
                                               out_specs=r_spec),
        out_shape=jax.ShapeDtypeStruct(shape, BF16),
        compiler_params=_cparams(("parallel",) * len(grid)),
    )(c, g_view, recv)


def _piece_of(ref, name, axis, k):
    if name == STACKED or axis == 0:
        return ref.at[k]
    size = ref.shape[1] // N_CHIPS
    return ref.at[:, pl.ds(pl.multiple_of(k * size, size), size)]


def _piece_shape(name, shape, axis):
    r, c = shape
    return (r // 2, c // N_CHIPS) if (axis == 1) else (r // N_CHIPS // 2, c)


def _scatter_exchange(partials, entries):
    n_w = len(entries)

    def ops(ins, outs, send_sems, recv_sems):
        x, y, c = _coords()
        chips = _other_chips(x, y)
        copies = []
        for i, (name, _, axis) in enumerate(entries):
            for j, chip in enumerate(chips):
                sem = 3 * i + j
                copies.append(pltpu.make_async_remote_copy(
                    _piece_of(ins[i], name, axis, 2 * chip[0] + chip[1]), outs[i].at[j], send_sems.at[sem],
                    recv_sems.at[sem], device_id=(chip[0], chip[1], c), device_id_type=MESH_ID))

        def start():
            for cp in copies:
                cp.start()

        def finish():
            for cp in copies:
                cp.wait()

        return start, lambda: None, finish

    out_shape = [jax.ShapeDtypeStruct((3,) + _piece_shape(*entry), BF16) for entry in entries]
    return _Exchange(partials, out_shape, 3 * n_w, ops)


def _add_chips(partial, recv, mine, name, axis):
    rows, cols = recv.shape[1:]
    tr = _tile(rows, 256, 16)
    if name == STACKED or axis == 0:
        p_spec = pl.BlockSpec((None, tr, cols), lambda i, k_ref: (k_ref[0], i, 0))
    else:
        p_spec = pl.BlockSpec((tr, cols), lambda i, k_ref: (i, k_ref[0]))

    def body(k_ref, p_ref, r_ref, o_ref):
        f32 = lambda a: a.astype(F32)
        o_ref[...] = ((f32(p_ref[...]) + f32(r_ref[0])) + f32(r_ref[1])) + f32(r_ref[2])

    return pl.pallas_call(
        body,
        name="add_chips_" + name,
        grid_spec=pltpu.PrefetchScalarGridSpec(
            num_scalar_prefetch=1, grid=(rows // tr,),
            in_specs=[p_spec, pl.BlockSpec((3, tr, cols), lambda i, k_ref: (0, i, 0))],
            out_specs=pl.BlockSpec((tr, cols), lambda i, k_ref: (i, 0))),
        out_shape=jax.ShapeDtypeStruct((rows, cols), F32),
        compiler_params=_cparams(("parallel",)),
    )(mine, partial, recv)


def _share_exchange(halves):
    n_w = len(halves)

    def ops(ins, outs, send_sems, recv_sems):
        x, y, c = _coords()
        copies = [pltpu.make_async_remote_copy(ins[i], outs[i], send_sems.at[i], recv_sems.at[i],
                                               device_id=(x, y, 1 - c), device_id_type=MESH_ID) for i in range(n_w)]

        def start():
            for cp in copies:
                cp.start()

        def finish():
            for cp in copies:
                cp.wait()

        return start, lambda: None, finish

    return _Exchange(halves, [jax.ShapeDtypeStruct(h.shape, F32) for h in halves], n_w, ops)


class _SemsFrom:
    def __init__(self, sems, first):
        self.sems, self.first = sems, first

    @property
    def at(self):
        return self

    def __getitem__(self, i):
        return self.sems.at[self.first + i]


def _joined(a, b):
    n_a = len(a.ins)

    def ops(ins, outs, send_sems, recv_sems):
        phases_a = a.ops(ins[:n_a], outs[:n_a], send_sems, recv_sems)
        phases_b = b.ops(ins[n_a:], outs[n_a:], _SemsFrom(send_sems, a.n_sems), _SemsFrom(recv_sems, a.n_sems))
        assert len(phases_a) == len(phases_b)
        return tuple((lambda pa=pa, pb=pb: (pa(), pb())) for pa, pb in zip(phases_a, phases_b))

    return _Exchange(a.ins + b.ins, a.out_shape + b.out_shape, a.n_sems + b.n_sems, ops)


SMALL_ROWS = 168


def _small_exchange(buf):
    def ops(ins, outs, send_sems, recv_sems):
        x, y, c = _coords()
        me = 4 * x + 2 * y + c
        peers = [(x ^ fx, y ^ fy, c ^ fc) for fx in (0, 1) for fy in (0, 1) for fc in (0, 1)][1:]

        def copy(j, slot, dev):
            return pltpu.make_async_remote_copy(ins[0], outs[0].at[slot], send_sems.at[j], recv_sems.at[j],
                                                device_id=dev, device_id_type=MESH_ID)

        own = pltpu.make_async_copy(ins[0], outs[0].at[me], send_sems.at[N_DEV - 1])

        def start():
            own.start()
            for j, dev in enumerate(peers):
                copy(j, me, dev).start()

        def finish():
            for j, dev in enumerate(peers):
                copy(j, 4 * dev[0] + 2 * dev[1] + dev[2], dev).wait()
            own.wait()

        return start, lambda: None, finish

    return _Exchange([buf], [jax.ShapeDtypeStruct((N_DEV,) + buf.shape, F32)], N_DEV, ops)


def _sum_devices(gathered):
    def body(g_ref, out_ref):
        acc = g_ref[0]
        for d in range(1, N_DEV):
            acc = acc + g_ref[d]
        out_ref[...] = acc

    return pl.pallas_call(
        body,
        name="sum_devices",
        in_specs=[VMEM_SPEC],
        out_specs=VMEM_SPEC,
        out_shape=jax.ShapeDtypeStruct(gathered.shape[1:], F32),
    )(gathered)


def _adamw(w, g, m, v, copy_g=False):
    r, rest = w.shape[0], w.shape[1:]
    per_row = 1
    for dim in rest:
        per_row *= dim
    tr = _tile(r, max(8, (5 << 19) // (4 * per_row)), 8 if len(rest) == 1 else 1)

    def body(w_ref, g_ref, m_ref, v_ref, *out_refs):
        d_ref, mo_ref, vo_ref = out_refs[-3:]
        gg = g_ref[...]
        if copy_g:
            out_refs[0][...] = gg
        mm = ADAM_B1 * m_ref[...] + (1.0 - ADAM_B1) * gg
        vv = ADAM_B2 * v_ref[...] + (1.0 - ADAM_B2) * (gg * gg)
        m_hat = mm / (1.0 - ADAM_B1 ** ADAM_STEP)
        v_hat = vv / (1.0 - ADAM_B2 ** ADAM_STEP)
        d_ref[...] = -ADAM_LR * (m_hat / (jnp.sqrt(v_hat) + ADAM_EPS) + ADAM_WD * w_ref[...])
        mo_ref[...] = mm
        vo_ref[...] = vv

    n_out = 4 if copy_g else 3
    spec = pl.BlockSpec((tr,) + rest, lambda i: (i,) + (0,) * len(rest))
    return pl.pallas_call(
        body,
        name="adamw",
        grid=(r // tr,),
        in_specs=[spec] * 4,
        out_specs=[spec] * n_out,
        out_shape=[jax.ShapeDtypeStruct(w.shape, F32)] * n_out,
        compiler_params=_cparams(("parallel",)),
    )(w, g, m, v)


def _adamw_halves(w, own, other, m, v, c, name):
    r, cols = w.shape
    half = r // 2
    tr = _tile(half, 256, 8)
    nt = half // tr
    whole = pl.BlockSpec((tr, cols), lambda h, i, c_ref: (h * nt + i, 0))
    part = pl.BlockSpec((tr, cols), lambda h, i, c_ref: (i, 0))

    def body(c_ref, w_ref, own_ref, other_ref, m_ref, v_ref, g_ref, d_ref, mo_ref, vo_ref):
        gg = jnp.where(pl.program_id(0) == c_ref[0], own_ref[...], other_ref[...])
        g_ref[...] = gg
        mm = ADAM_B1 * m_ref[...] + (1.0 - ADAM_B1) * gg
        vv = ADAM_B2 * v_ref[...] + (1.0 - ADAM_B2) * (gg * gg)
        m_hat = mm / (1.0 - ADAM_B1 ** ADAM_STEP)
        v_hat = vv / (1.0 - ADAM_B2 ** ADAM_STEP)
        d_ref[...] = -ADAM_LR * (m_hat / (jnp.sqrt(v_hat) + ADAM_EPS) + ADAM_WD * w_ref[...])
        mo_ref[...] = mm
        vo_ref[...] = vv

    return pl.pallas_call(
        body,
        name="adamw_" + name,
        grid_spec=pltpu.PrefetchScalarGridSpec(
            num_scalar_prefetch=1, grid=(2, nt),
            in_specs=[whole, part, part, whole, whole], out_specs=[whole] * 4),
        out_shape=[jax.ShapeDtypeStruct((r, cols), F32)] * 4,
        compiler_params=_cparams(("parallel", "parallel")),
    )(c, w, own, other, m, v)


GATHER_FIRST = ("ffn1_w_in", "ffn1_w_out")
GATHER_PROJ = ("w_in",)
GATHER_LATE = ("w_branch_a", "w_branch_b", "w_out", "ffn2_w_in", "ffn2_w_out")


class _Comm:
    def __init__(self, shards, c_arr, mine_arr):
        self.shards, self.c, self.mine = shards, c_arr, mine_arr
        self.groups, self.halves = {}, {}
        self.by_name = {entry[0]: entry for entry in BIG}

    def small_gather(self, loss, small):
        pad_lanes = lambda a: jnp.concatenate([a, jnp.zeros((1, LANES - a.shape[1]), F32)], axis=1)
        buf = jnp.concatenate([
            small["meta_tokens"].reshape(128, LANES),
            small["ffn1_norm"].reshape(8, LANES), small["mix_norm"].reshape(8, LANES),
            small["ffn2_norm"].reshape(8, LANES), small["final_norm"].reshape(8, LANES),
            loss, pad_lanes(small["b_forget"]), pad_lanes(small["attn_sinks"]),
            jnp.zeros((SMALL_ROWS - 163, LANES), F32)], axis=0)
        return _small_exchange(buf)

    def small_gathered(self, outs):
        self.reduced = _sum_devices(outs[0])

    def gather(self, names):
        padded = (STACKED, (D_MODEL, N_CHIPS * SHARD_PAD_COLS), 1)
        table = [(padded if n == STACKED else self.by_name[n]) + (BF16, True) for n in names]
        return _gather_exchange([self.shards[n] for n in names], table)

    def swap(self, tag, grads):
        entries = [self.by_name[n] for n in grads]
        arrays = list(grads.values())
        self.groups[tag] = (entries, arrays)
        return _halves_exchange(arrays, entries)

    def scatter(self, tag, received):
        entries, arrays = self.groups[tag]
        views = [_halves_view(*entry) for entry in entries]
        partials = [_add_sibling(g.reshape(v[0]), r, self.c, name)
                    for g, v, r, (name, _, _) in zip(arrays, views, received, entries)]
        self.groups[tag] = (entries, partials)
        return _scatter_exchange(partials, entries)

    def received(self, tag, pieces):
        entries, partials = self.groups[tag]
        for p, r, (name, _, axis) in zip(partials, pieces, entries):
            self.halves[name] = _add_chips(p, r, self.mine, name, axis)

    def finish(self, tag, name, scatter):
        ready = [n for n, _, _ in BIG if n != name]
        outs = _run_exchange(_joined(scatter, _share_exchange([self.halves[n] for n in ready])), "scatter_chip_sums")
        self.received(tag, outs[:len(scatter.ins)])
        others = dict(zip(ready, outs[len(scatter.ins):]))
        others[name], = _run_exchange(_share_exchange([self.halves[name]]), "share_with_sibling")
        return {n: (self.halves[n], others[n]) for n, _, _ in BIG}


def kernel(x, meta_tokens, ffn1_norm, ffn1_w_in, ffn1_w_out, mix_norm, w_in, b_forget, attn_sinks, w_branch_a, w_branch_b, w_out, ffn2_norm, ffn2_w_in, ffn2_w_out, final_norm, loss_target, m_meta_tokens, m_ffn1_norm, m_ffn1_w_in, m_ffn1_w_out, m_mix_norm, m_w_in, m_b_forget, m_attn_sinks, m_w_branch_a, m_w_branch_b, m_w_out, m_ffn2_norm, m_ffn2_w_in, m_ffn2_w_out, m_final_norm, v_meta_tokens, v_ffn1_norm, v_ffn1_w_in, v_ffn1_w_out, v_mix_norm, v_w_in, v_b_forget, v_attn_sinks, v_w_branch_a, v_w_branch_b, v_w_out, v_ffn2_norm, v_ffn2_w_in, v_ffn2_w_out, v_final_norm):
    given = dict(locals())
    names = ["meta_tokens", "ffn1_norm", "ffn1_w_in", "ffn1_w_out", "mix_norm", "w_in", "b_forget", "attn_sinks",
             "w_branch_a", "w_branch_b", "w_out", "ffn2_norm", "ffn2_w_in", "ffn2_w_out", "final_norm"]
    big_names = [n for n, _, _ in BIG]
    cx, cy, cc = _coords()
    c_arr = cc.reshape(1).astype(jnp.int32)
    mine_arr = (2 * cx + cy).reshape(1).astype(jnp.int32)

    by_name = {entry[0]: entry for entry in BIG}
    shards = {n: given[n][0].astype(BF16) for n in GATHER_FIRST}
    table = [by_name[n] + (BF16, True) for n in GATHER_FIRST] + [("meta_tokens", (N_META, D_MODEL), 1, F32, False)]
    first = _gather_exchange([shards[n] for n in GATHER_FIRST] + [meta_tokens], table)
    late_names = [n for n in big_names if n not in GATHER_FIRST]
    casts, (w1i, w1o, _, stream) = _cast_hosting([given[n][0] for n in late_names], first, "gather_first",
                                                 stream=(x, len(GATHER_FIRST)))
    shards.update(zip(late_names, casts))
    comm = _Comm(shards, c_arr, mine_arr)
    norms = (ffn1_norm, mix_norm, ffn2_norm, final_norm.reshape(1, D_MODEL))
    loss, grad_x, small, big = _local_step(stream, loss_target, None, norms, b_forget, attn_sinks, (w1i, w1o), comm)

    swap = comm.swap("ffn1_in", dict(ffn1_w_in=big["ffn1_w_in"]))
    last = comm.scatter("ffn1_in", _run_exchange(swap, "exchange_halves_ffn1_in"))
    grad_halves = comm.finish("ffn1_in", "ffn1_w_in", last)
    grads = {}

    red = comm.reduced
    meta_cols = red[:128].reshape(N_META, D_MODEL)
    grads["meta_tokens"] = lax.dynamic_slice_in_dim(meta_cols, (2 * cx + cy) * (D_MODEL // N_CHIPS),
                                                    D_MODEL // N_CHIPS, axis=1)
    grads["ffn1_norm"] = red[128:136].reshape(1, D_MODEL)
    grads["mix_norm"] = red[136:144].reshape(1, D_MODEL)
    grads["ffn2_norm"] = red[144:152].reshape(1, D_MODEL)
    grads["final_norm"] = red[152:160].reshape(1, D_MODEL)
    loss_out = red[160, 0]
    grads["b_forget"] = red[161:162, :B_HEADS]
    grads["attn_sinks"] = red[162:163, :A_HEADS]

    out_g, out_d, out_m, out_v = [], [], [], []
    for n in names:
        w_full = given[n]
        shape = w_full.shape
        two_d = (lambda a: a.reshape(shape[-2], shape[-1])) if len(shape) >= 2 else (lambda a: a.reshape(1, shape[0]))
        if n == STACKED:
            own, other = grad_halves[n]
            g_nat = jnp.concatenate([jnp.where(cc == 0, own, other), jnp.where(cc == 0, other, own)], axis=0)
            rows = lambda a: a.reshape(1, shape[-2], shape[-1]).transpose(2, 0, 1)
            unrows = lambda a: a.transpose(1, 2, 0)
            g2, d2, m2, v2 = [unrows(a) for a in _adamw(rows(w_full), rows(g_nat), rows(given["m_" + n]),
                                                         rows(given["v_" + n]), copy_g=True)]
        elif n in grad_halves:
            own, other = grad_halves[n]
            g2, d2, m2, v2 = _adamw_halves(two_d(w_full), own, other, two_d(given["m_" + n]),
                                           two_d(given["v_" + n]), c_arr, n)
        else:
            g2 = two_d(grads[n])
            d2, m2, v2 = _adamw(two_d(w_full), g2, two_d(given["m_" + n]), two_d(given["v_" + n]))
        out_g.append(g2.reshape(shape))
        out_d.append(d2.reshape(shape))
        out_m.append(m2.reshape(shape))
        out_v.append(v2.reshape(shape))
    return (loss_out, grad_x, *out_g, *out_d, *out_m, *out_v)
```

```python
import jax
import jax.numpy as jnp
from jax import lax
from jax.experimental import pallas as pl
from jax.experimental.pallas import tpu as pltpu

F32 = jnp.float32
BF16 = jnp.bfloat16

D_MODEL = 1024
N_META = 16
BLOCK = 128
LANES = 128
PREFIX = BLOCK
N_PAD = PREFIX - N_META
HEAD_DIM = 64
A_HEADS = 8
A_KV_HEADS = 2
A_GROUP = 4
B_HEADS = 8
B_PAIRS = B_HEADS // 2
A_WIDTH = A_HEADS * HEAD_DIM
A_KV_WIDTH = A_KV_HEADS * HEAD_DIM
B_WIDTH = B_HEADS * HEAD_DIM
W_IN_COLS = A_WIDTH + 2 * A_KV_WIDTH + 3 * B_WIDTH + B_HEADS + 2 * D_MODEL
SRC_KA = A_WIDTH
SRC_VA = SRC_KA + A_KV_WIDTH
SRC_QB = SRC_VA + A_KV_WIDTH
SRC_KB = SRC_QB + B_WIDTH
SRC_VB = SRC_KB + B_WIDTH
SRC_F = SRC_VB + B_WIDTH
SRC_GA = SRC_F + B_HEADS
SRC_GB = SRC_GA + D_MODEL
A_PAD_WIDTH = A_HEADS * LANES
B_PAD_WIDTH = B_HEADS * LANES
F_COLS = LANES
OFF_QA = 0
OFF_KA = SRC_KA
OFF_VA = SRC_VA
OFF_F = OFF_VA + A_KV_WIDTH
OFF_QB = OFF_F + F_COLS
OFF_KB = OFF_QB + B_WIDTH
OFF_VB = OFF_KB + B_WIDTH
OFF_GA = OFF_VB + B_WIDTH
OFF_GB = OFF_GA + D_MODEL
P_COLS = OFF_GB + D_MODEL
P_PIECES = ((0, OFF_QB), (OFF_QB, OFF_GA - OFF_QB), (OFF_GA, P_COLS - OFF_GA))
EPS = 1e-6
NEG = -1e30
SCALE = HEAD_DIM ** -0.5
KEY_BLOCKS = 4

ADAM_LR = 0.001
ADAM_B1 = 0.9
ADAM_B2 = 0.999
ADAM_EPS = 1e-08
ADAM_WD = 0.01
ADAM_STEP = 10

N_CHIPS = 4
N_DEV = 8
VMEM_LIMIT = 56 * 1024 * 1024

NT_DIMS = (((1,), (1,)), ((), ()))
TN_DIMS = (((0,), (0,)), ((), ()))
MESH_ID = pl.DeviceIdType.MESH
HBM_SPEC = pl.BlockSpec(memory_space=pltpu.HBM)
VMEM_SPEC = pl.BlockSpec(memory_space=pltpu.VMEM)


def _tile(n, target, mult=16):
    best = None
    for t in range(mult, min(n, target) + 1, mult):
        if n % t == 0:
            best = t
    return best if best is not None else n


def _cparams(sem):
    return pltpu.CompilerParams(dimension_semantics=sem, vmem_limit_bytes=VMEM_LIMIT)


def _rms_scale(h):
    return lax.rsqrt(jnp.mean(h * h, axis=-1, keepdims=True) + EPS)


def _rms_bwd(dn, h, w):
    r = _rms_scale(h)
    dw = jnp.sum(dn * (h * r), axis=0, keepdims=True)
    z = dn * w
    dh = r * z - h * ((r * r * r) * jnp.mean(z * h, axis=-1, keepdims=True))
    return dh, dw


def _ffn_fwd(h, norm_w, w_in, w_out, exchange=None):
    t, d = h.shape
    f = w_out.shape[0]
    tm = _tile(t, 272)
    tc = _tile(f, 256, 128)
    nj = f // tc
    ni = t // tm
    n_x = len(exchange.ins) if exchange else 0

    def body(*refs):
        h_ref, nw_ref, wi_ref, wo_ref = refs[:4]
        hout_ref, g_ref, u_ref = refs[4 + n_x:7 + n_x]
        a_scr = refs[7 + 2 * n_x]
        i = pl.program_id(0)
        if exchange:
            _host_exchange(exchange, refs[4:4 + n_x], refs[7 + n_x:7 + 2 * n_x], refs[8 + 2 * n_x:],
                           i == 0, i == ni // 3, i == ni - 1, late=i == 2 * ni // 3)
        hh = h_ref[...]
        n = ((hh * _rms_scale(hh)) * nw_ref[...]).astype(BF16)
        for j in range(nj):
            cols = slice(j * tc, (j + 1) * tc)
            g = jnp.dot(n, wi_ref[:, j * tc:(j + 1) * tc], preferred_element_type=F32)
            u = jnp.dot(n, wi_ref[:, f + j * tc:f + (j + 1) * tc], preferred_element_type=F32)
            g_ref[:, cols] = g
            u_ref[:, cols] = u
            a_scr[:, cols] = ((g * jax.nn.sigmoid(g)) * u).astype(BF16)
        hout_ref[...] = hh + 0.5 * jnp.dot(a_scr[...], wo_ref[...], preferred_element_type=F32)

    resident = lambda a: pl.BlockSpec(a.shape, lambda i: (0, 0), pipeline_mode=pl.Buffered(1))
    row = lambda w: pl.BlockSpec((tm, w), lambda i: (i, 0))
    outs = pl.pallas_call(
        body,
        name="ffn_fwd",
        grid=(ni,),
        in_specs=[row(d), pl.BlockSpec((1, d), lambda i: (0, 0)), resident(w_in), resident(w_out)] + [HBM_SPEC] * n_x,
        out_specs=[row(d), row(f), row(f)] + [HBM_SPEC] * n_x,
        out_shape=[
            jax.ShapeDtypeStruct((t, d), F32),
            jax.ShapeDtypeStruct((t, f), F32),
            jax.ShapeDtypeStruct((t, f), F32),
        ] + (exchange.out_shape if exchange else []),
        scratch_shapes=[pltpu.VMEM((tm, f), BF16)] + (exchange.scratch if exchange else []),
        compiler_params=_cparams(("arbitrary",) if exchange else ("parallel",)),
    )(h, norm_w, w_in, w_out, *(exchange.ins if exchange else []))
    return outs[:3], outs[3:]


def _ffn_bwd(dh_out, h, norm_w, g, u, w_in, w_out, exchange=None):
    t, d = h.shape
    f = w_out.shape[0]
    tm = _tile(t, 272)
    tc = _tile(f, 256, 128)
    nj = f // tc
    ni = t // tm
    n_x = len(exchange.ins) if exchange else 0

    def body(*refs):
        dho_ref, h_ref, nw_ref, g_ref, u_ref, wi_ref, wo_ref = refs[:7]
        dhin_ref, n_ref, a_ref, dgu_ref, df_ref, dnw_ref = refs[7 + n_x:13 + n_x]
        i = pl.program_id(0)
        if exchange:
            _host_exchange(exchange, refs[7:7 + n_x], refs[13 + n_x:13 + 2 * n_x], refs[13 + 2 * n_x:],
                           i == 0, i == ni - 1, i == ni - 1)
        hh = h_ref[...]
        nw = nw_ref[...]
        n_ref[...] = ((hh * _rms_scale(hh)) * nw).astype(BF16)
        dho = dho_ref[...]
        df = (0.5 * dho).astype(BF16)
        df_ref[...] = df
        for j in range(nj):
            cols = slice(j * tc, (j + 1) * tc)
            da = lax.dot_general(df, wo_ref[cols, :], NT_DIMS, preferred_element_type=F32)
            gg = g_ref[:, cols]
            uu = u_ref[:, cols]
            sig = jax.nn.sigmoid(gg)
            sl = gg * sig
            a_ref[:, cols] = (sl * uu).astype(BF16)
            dgu_ref[0, :, cols] = ((da * uu) * (sig * (1.0 + gg * (1.0 - sig)))).astype(BF16)
            dgu_ref[1, :, cols] = (da * sl).astype(BF16)
        dn = (lax.dot_general(dgu_ref[0], wi_ref[:, :f], NT_DIMS, preferred_element_type=F32)
              + lax.dot_general(dgu_ref[1], wi_ref[:, f:], NT_DIMS, preferred_element_type=F32))
        dh, dw = _rms_bwd(dn, hh, nw)
        dhin_ref[...] = dho + dh
        dnw_ref[0] = dw

    resident = lambda a: pl.BlockSpec(a.shape, lambda i: (0, 0), pipeline_mode=pl.Buffered(1))
    row = lambda w: pl.BlockSpec((tm, w), lambda i: (i, 0))
    outs = pl.pallas_call(
        body,
        name="ffn_bwd",
        grid=(ni,),
        in_specs=[row(d), row(d), pl.BlockSpec((1, d), lambda i: (0, 0)), row(f), row(f),
                  resident(w_in), resident(w_out)] + [HBM_SPEC] * n_x,
        out_specs=[row(d), row(d), row(f), pl.BlockSpec((2, tm, f), lambda i: (0, i, 0)), row(d),
                   pl.BlockSpec((1, 1, d), lambda i: (i, 0, 0))] + [HBM_SPEC] * n_x,
        out_shape=[
            jax.ShapeDtypeStruct((t, d), F32),
            jax.ShapeDtypeStruct((t, d), BF16),
            jax.ShapeDtypeStruct((t, f), BF16),
            jax.ShapeDtypeStruct((2, t, f), BF16),
            jax.ShapeDtypeStruct((t, d), BF16),
            jax.ShapeDtypeStruct((ni, 1, d), F32),
        ] + (exchange.out_shape if exchange else []),
        scratch_shapes=exchange.scratch if exchange else [],
        compiler_params=_cparams(("arbitrary",) if exchange else ("parallel",)),
    )(dh_out, h, norm_w, g, u, w_in, w_out, *(exchange.ins if exchange else []))
    return outs[:6], outs[6:]


def _tn_matmul(a, b, name, exchange=None):
    t, k = a.shape
    split = b.ndim == 3
    n = 2 * b.shape[2] if split else b.shape[1]
    tk = _tile(k, 512, 128)
    tn = _tile(b.shape[-1], 1408, 128)
    per_half = b.shape[-1] // tn
    ni, nj = k // tk, n // tn
    n_x = len(exchange.ins) if exchange else 0

    def body(*refs):
        a_ref, b_ref, o_ref = refs[0], refs[1], refs[2 + n_x]
        if exchange:
            i, j = pl.program_id(0), pl.program_id(1)
            at_end = (i == ni - 1) & (j == nj - 1)
            _host_exchange(exchange, refs[2:2 + n_x], refs[3 + n_x:3 + 2 * n_x], refs[3 + 2 * n_x:],
                           (i == 0) & (j == 0), at_end, at_end)
        o_ref[...] = lax.dot_general(a_ref[...], b_ref[...], TN_DIMS, preferred_element_type=F32)

    if split:
        b_spec = pl.BlockSpec((None, t, tn), lambda i, j: (j // per_half, 0, j % per_half))
    else:
        b_spec = pl.BlockSpec((t, tn), lambda i, j: (0, j))
    outs = pl.pallas_call(
        body,
        name=name,
        grid=(ni, nj),
        in_specs=[pl.BlockSpec((t, tk), lambda i, j: (0, i)), b_spec] + [HBM_SPEC] * n_x,
        out_specs=[pl.BlockSpec((tk, tn), lambda i, j: (i, j))] + [HBM_SPEC] * n_x,
        out_shape=[jax.ShapeDtypeStruct((k, n), F32)] + (exchange.out_shape if exchange else []),
        scratch_shapes=exchange.scratch if exchange else [],
        compiler_params=_cparams(("arbitrary", "arbitrary") if exchange else ("parallel", "parallel")),
    )(a, b, *(exchange.ins if exchange else []))
    return (outs[0], outs[1:]) if exchange else outs[0]


A_SLOT = lambda h: h // A_GROUP
B_SLOT = lambda h: h % 2

PROJ_PARTS = (
    (OFF_QA, A_WIDTH, A_PAD_WIDTH, True, A_SLOT), (OFF_KA, A_KV_WIDTH, A_KV_WIDTH, True, None),
    (OFF_VA, A_KV_WIDTH, A_KV_WIDTH, True, None), (OFF_QB, B_WIDTH, B_WIDTH, True, None),
    (OFF_KB, B_WIDTH, B_PAD_WIDTH, True, B_SLOT), (OFF_VB, B_WIDTH, B_PAD_WIDTH, True, B_SLOT),
    (OFF_GA, D_MODEL, D_MODEL, False, None), (OFF_GB, D_MODEL, D_MODEL, False, None), (OFF_F, F_COLS, F_COLS, False, None),
)


def _head_tile(pair, head, slot):
    lane_slot = lax.broadcasted_iota(jnp.int32, pair.shape, 1) // HEAD_DIM
    moved = pair if head % 2 == slot else pltpu.roll(pair, HEAD_DIM, 1)
    return jnp.where(lane_slot == slot, moved, 0.0)


def _proj_fwd(h, norm_w, w_p):
    t, d = h.shape
    tm = _tile(t, 272)

    def body(h_ref, nw_ref, w_ref, u_ref, *part_refs):
        hh = h_ref[...]
        un = ((hh * _rms_scale(hh)) * nw_ref[...]).astype(BF16)
        u_ref[...] = un
        for (off, width, _, _, slot), p_ref in zip(PROJ_PARTS, part_refs):
            if slot is None:
                p_ref[...] = jnp.dot(un, w_ref[:, off:off + width], preferred_element_type=F32).astype(p_ref.dtype)
                continue
            part = jnp.dot(un, w_ref[:, off:off + width], preferred_element_type=F32)
            for pair in range(width // LANES):
                x = part[:, pair * LANES:(pair + 1) * LANES]
                for head in (2 * pair, 2 * pair + 1):
                    p_ref[:, head * LANES:(head + 1) * LANES] = _head_tile(x, head, slot(head)).astype(p_ref.dtype)

    row = lambda w: pl.BlockSpec((tm, w), lambda i: (i, 0))
    return pl.pallas_call(
        body,
        name="proj_fwd",
        grid=(t // tm,),
        in_specs=[row(d), pl.BlockSpec((1, d), lambda i: (0, 0)),
                  pl.BlockSpec(w_p.shape, lambda i: (0, 0), pipeline_mode=pl.Buffered(1))],
        out_specs=[row(d)] + [row(width) for _, _, width, _, _ in PROJ_PARTS],
        out_shape=[jax.ShapeDtypeStruct((t, d), BF16)]
        + [jax.ShapeDtypeStruct((t, width), BF16 if is_bf else F32) for _, _, width, is_bf, _ in PROJ_PARTS],
        compiler_params=_cparams(("parallel",)),
    )(h, norm_w, w_p)


def _proj_bwd(dh_out, h, norm_w, dproj, w_p, exchange=None):
    t, d = h.shape
    tm = _tile(t, 272)
    ni = t // tm
    n_p = len(P_PIECES)
    n_in = 4 + n_p
    n_x = len(exchange.ins) if exchange else 0

    def body(*refs):
        dho_ref, h_ref, nw_ref = refs[:3]
        dp_refs, w_ref = refs[3:3 + n_p], refs[3 + n_p]
        dhin_ref, dnw_ref = refs[n_in + n_x:n_in + 2 + n_x]
        if exchange:
            i = pl.program_id(0)
            _host_exchange(exchange, refs[n_in:n_in + n_x], refs[n_in + 2 + n_x:n_in + 2 + 2 * n_x],
                           refs[n_in + 2 + 2 * n_x:], i == 0, i == ni - 1, i == ni - 1)
        dn = None
        for dp_ref, (off, width) in zip(dp_refs, P_PIECES):
            part = lax.dot_general(dp_ref[...], w_ref[:, off:off + width], NT_DIMS, preferred_element_type=F32)
            dn = part if dn is None else dn + part
        dh, dw = _rms_bwd(dn, h_ref[...], nw_ref[...])
        dhin_ref[...] = dho_ref[...] + dh
        dnw_ref[0] = dw

    row = lambda w: pl.BlockSpec((tm, w), lambda i: (i, 0))
    outs = pl.pallas_call(
        body,
        name="proj_bwd",
        grid=(ni,),
        in_specs=[row(d), row(d), pl.BlockSpec((1, d), lambda i: (0, 0))] + [row(width) for _, width in P_PIECES]
        + [pl.BlockSpec(w_p.shape, lambda i: (0, 0), pipeline_mode=pl.Buffered(1))] + [HBM_SPEC] * n_x,
        out_specs=[row(d), pl.BlockSpec((1, 1, d), lambda i: (i, 0, 0))] + [HBM_SPEC] * n_x,
        out_shape=[jax.ShapeDtypeStruct((t, d), F32), jax.ShapeDtypeStruct((ni, 1, d), F32)]
        + (exchange.out_shape if exchange else []),
        scratch_shapes=exchange.scratch if exchange else [],
        compiler_params=_cparams(("arbitrary",) if exchange else ("parallel",)),
    )(dh_out, h, norm_w, *dproj, w_p, *(exchange.ins if exchange else []))
    return outs[:2], outs[2:]


def _merge_fwd(h, oa, ob, ga, gb, wa, wb, wo):
    t, d = h.shape
    tm = _tile(t, 544)

    def body(h_ref, oa_ref, ob_ref, ga_ref, gb_ref, wa_ref, wb_ref, wo_ref, hout_ref, mix_ref):
        ya = jnp.dot(oa_ref[...], wa_ref[...], preferred_element_type=F32)
        yb = jnp.dot(ob_ref[...], wb_ref[...], preferred_element_type=F32)
        mixed = (jax.nn.sigmoid(ga_ref[...]) * ya + jax.nn.sigmoid(gb_ref[...]) * yb).astype(BF16)
        mix_ref[...] = mixed
        hout_ref[...] = h_ref[...] + jnp.dot(mixed, wo_ref[...], preferred_element_type=F32)

    row = lambda w: pl.BlockSpec((tm, w), lambda i: (i, 0))
    full = lambda a: pl.BlockSpec(a.shape, lambda i: (0, 0))
    return pl.pallas_call(
        body,
        name="merge_fwd",
        grid=(t // tm,),
        in_specs=[row(d), row(oa.shape[1]), row(ob.shape[1]), row(d), row(d), full(wa), full(wb), full(wo)],
        out_specs=[row(d), row(d)],
        out_shape=[jax.ShapeDtypeStruct((t, d), F32), jax.ShapeDtypeStruct((t, d), BF16)],
        compiler_params=_cparams(("parallel",)),
    )(h, oa, ob, ga, gb, wa, wb, wo)


def _merge_bwd(dh, oa, ob, ga, gb, wa, wb, wo, exchange=None):
    t, d = dh.shape
    tm = _tile(t, 544)
    ni = t // tm
    n_x = len(exchange.ins) if exchange else 0
    n_s = 2 if exchange else 0

    def body(*refs):
        dh_ref, oa_ref, ob_ref, ga_ref, gb_ref, wa_ref, wb_ref, wo_ref = refs[:8]
        doa_ref, dob_ref, dg_ref = refs[8 + n_x:11 + n_x]
        grad_refs = refs[11 + n_x:14 + n_x]
        acc_refs = refs[14 + 2 * n_x + n_s:17 + 2 * n_x + n_s]
        sem = refs[17 + 2 * n_x + n_s]
        i = pl.program_id(0)
        if exchange:
            _host_exchange(exchange, refs[8:8 + n_x], refs[14 + n_x:14 + 2 * n_x],
                           refs[14 + 2 * n_x:14 + 2 * n_x + n_s], i == 0, i == ni - 1, i == ni - 1)

        @pl.when(i == 0)
        def _():
            for acc in acc_refs:
                acc[...] = jnp.zeros_like(acc)

        dhb = dh_ref[...].astype(BF16)
        dmix = lax.dot_general(dhb, wo_ref[...], NT_DIMS, preferred_element_type=F32)
        mixed = None
        for branch, (o_ref, g_ref, w_ref, do_ref, acc) in enumerate((
                (oa_ref, ga_ref, wa_ref, doa_ref, acc_refs[0]),
                (ob_ref, gb_ref, wb_ref, dob_ref, acc_refs[1]))):
            o = o_ref[...]
            y = jnp.dot(o, w_ref[...], preferred_element_type=F32)
            s = jax.nn.sigmoid(g_ref[...])
            mixed = s * y if mixed is None else mixed + s * y
            dy = (dmix * s).astype(BF16)
            dg_ref[:, branch * d:(branch + 1) * d] = ((dmix * y) * (s * (1.0 - s))).astype(BF16)
            do_ref[...] = lax.dot_general(dy, w_ref[...], NT_DIMS, preferred_element_type=F32).astype(BF16)
            acc[...] += lax.dot_general(o, dy, TN_DIMS, preferred_element_type=F32)
        acc_refs[2][...] += lax.dot_general(mixed.astype(BF16), dhb, TN_DIMS, preferred_element_type=F32)

        @pl.when(i == ni - 1)
        def _():
            copies = [pltpu.make_async_copy(acc, out, sem.at[q]) for q, (acc, out) in enumerate(zip(acc_refs, grad_refs))]
            for cp in copies:
                cp.start()
            for cp in copies:
                cp.wait()

    row = lambda w: pl.BlockSpec((tm, w), lambda i: (i, 0))
    full = lambda a: pl.BlockSpec(a.shape, lambda i: (0, 0))
    wa_w, wb_w = oa.shape[1], ob.shape[1]
    grad_shapes = [(wa_w, d), (wb_w, d), (d, d)]
    outs = pl.pallas_call(
        body,
        name="merge_bwd",
        grid=(ni,),
        in_specs=[row(d), row(wa_w), row(wb_w), row(d), row(d), full(wa), full(wb), full(wo)] + [HBM_SPEC] * n_x,
        out_specs=[row(wa_w), row(wb_w), row(2 * d)] + [HBM_SPEC] * (3 + n_x),
        out_shape=[
            jax.ShapeDtypeStruct((t, wa_w), BF16), jax.ShapeDtypeStruct((t, wb_w), BF16),
            jax.ShapeDtypeStruct((t, 2 * d), BF16),
        ] + [jax.ShapeDtypeStruct(s, F32) for s in grad_shapes] + (exchange.out_shape if exchange else []),
        scratch_shapes=(exchange.scratch if exchange else []) + [pltpu.VMEM(s, F32) for s in grad_shapes]
        + [pltpu.SemaphoreType.DMA((3,))],
        compiler_params=_cparams(("arbitrary",)),
    )(dh, oa, ob, ga, gb, wa, wb, wo, *(exchange.ins if exchange else []))
    return outs[:6], outs[6:]


def _tri_dot(tri, x):
    hi = x.astype(BF16)
    r1 = x - hi.astype(F32)
    mid = r1.astype(BF16)
    lo = (r1 - mid.astype(F32)).astype(BF16)
    return (jnp.dot(tri, hi, preferred_element_type=F32)
            + jnp.dot(tri, mid, preferred_element_type=F32)
            + jnp.dot(tri, lo, preferred_element_type=F32))


def _forget_cumsum(f_logit, b_pad, nb):
    t, w = f_logit.shape
    bsz = t // (nb * BLOCK)

    def body(f_ref, b_ref, c_ref, carry):
        @pl.when(pl.program_id(0) == 0)
        def _():
            carry[...] = jnp.zeros_like(carry)

        rows = lax.broadcasted_iota(jnp.int32, (BLOCK, BLOCK), 0)
        cols = lax.broadcasted_iota(jnp.int32, (BLOCK, BLOCK), 1)
        tri = (cols <= rows).astype(BF16)
        for b in range(bsz):
            x = jax.nn.log_sigmoid(f_ref[b] + b_ref[...])
            c = _tri_dot(tri, x) + carry[b]
            c_ref[b] = c
            carry[b] = c[BLOCK - 1:BLOCK, :]

    block = pl.BlockSpec((bsz, BLOCK, w), lambda n: (0, n, 0))
    return pl.pallas_call(
        body,
        name="forget_cumsum",
        grid=(nb,),
        in_specs=[block, pl.BlockSpec((1, w), lambda n: (0, 0))],
        out_specs=block,
        out_shape=jax.ShapeDtypeStruct((bsz, nb * BLOCK, w), F32),
        scratch_shapes=[pltpu.VMEM((bsz, 1, w), F32)],
        compiler_params=_cparams(("arbitrary",)),
    )(f_logit.reshape(bsz, nb * BLOCK, w), b_pad).reshape(t, w)


def _forget_cumsum_bwd(dc, f_logit, b_pad, nb):
    t, w = f_logit.shape
    bsz = t // (nb * BLOCK)

    def body(dc_ref, f_ref, b_ref, df_ref, db_ref, carry):
        @pl.when(pl.program_id(0) == 0)
        def _():
            carry[...] = jnp.zeros_like(carry)
            db_ref[...] = jnp.zeros_like(db_ref)

        rows = lax.broadcasted_iota(jnp.int32, (BLOCK, BLOCK), 0)
        cols = lax.broadcasted_iota(jnp.int32, (BLOCK, BLOCK), 1)
        tri = (cols >= rows).astype(BF16)
        for b in range(bsz):
            dlf = _tri_dot(tri, dc_ref[b]) + carry[b]
            carry[b] = dlf[0:1, :]
            df = dlf * jax.nn.sigmoid(-(f_ref[b] + b_ref[...]))
            df_ref[b] = df.astype(BF16)
            db_ref[b] += jnp.sum(df, axis=0, keepdims=True)

    l = nb * BLOCK
    rev = pl.BlockSpec((bsz, BLOCK, w), lambda n: (0, nb - 1 - n, 0))
    df, db = pl.pallas_call(
        body,
        name="forget_cumsum_bwd",
        grid=(nb,),
        in_specs=[rev, rev, pl.BlockSpec((1, w), lambda n: (0, 0))],
        out_specs=[rev, pl.BlockSpec((bsz, 1, w), lambda n: (0, 0, 0))],
        out_shape=[jax.ShapeDtypeStruct((bsz, l, w), BF16), jax.ShapeDtypeStruct((bsz, 1, w), F32)],
        scratch_shapes=[pltpu.VMEM((bsz, 1, w), F32)],
        compiler_params=_cparams(("arbitrary",)),
    )(dc.reshape(bsz, l, w), f_logit.reshape(bsz, l, w), b_pad)
    return df.reshape(t, w), db


GROUP_ROWS = A_GROUP * BLOCK


def _stack_heads(ref, g):
    return jnp.concatenate([ref[:, (A_GROUP * g + i) * LANES:(A_GROUP * g + i + 1) * LANES] for i in range(A_GROUP)],
                           axis=0)


def _unstack_heads(ref, g, x):
    for i in range(A_GROUP):
        ref[:, (A_GROUP * g + i) * LANES:(A_GROUP * g + i + 1) * LANES] = x[i * BLOCK:(i + 1) * BLOCK].astype(ref.dtype)


def _swa_logits(qk, slope, n):
    qi = lax.broadcasted_iota(jnp.int32, (GROUP_ROWS, BLOCK), 0) & (BLOCK - 1)
    kj = lax.broadcasted_iota(jnp.int32, (GROUP_ROWS, BLOCK), 1)
    s_all = qk * SCALE
    out = []
    for i, (dist, ok) in enumerate((
            (n * BLOCK + qi - kj, (kj >= N_PAD) & (n * BLOCK + qi - kj >= 0)),
            (BLOCK + qi - kj, (kj > qi) & (n >= 2)),
            (qi - kj, (kj <= qi) & (n >= 1)))):
        s = s_all[:, i * BLOCK:(i + 1) * BLOCK] - slope * dist.astype(F32)
        out.append(jnp.where(ok, s, NEG))
    return out


def _three_blocks(m_ref, p_ref, c_ref):
    return jnp.concatenate([m_ref[...], p_ref[...], c_ref[...]], axis=0)


def _swa_specs(bsz, nb):
    qspec = pl.BlockSpec((bsz, BLOCK, A_PAD_WIDTH), lambda n: (0, n, 0))
    kv_m = pl.BlockSpec((bsz, BLOCK, LANES), lambda n: (0, 0, 0))
    kv_p = pl.BlockSpec((bsz, BLOCK, LANES), lambda n: (0, jnp.maximum(n - 1, 0), 0))
    kv_c = pl.BlockSpec((bsz, BLOCK, LANES), lambda n: (0, n, 0))
    rowspec = pl.BlockSpec((A_KV_HEADS, GROUP_ROWS, 1), lambda n: (0, 0, 0))
    lsespec = pl.BlockSpec((bsz, 1, A_KV_HEADS, GROUP_ROWS, 1), lambda n: (0, n, 0, 0, 0))
    return qspec, kv_m, kv_p, kv_c, rowspec, lsespec


def _swa_fwd(q, k, v, sink_rows, slope_rows, nb):
    t = q.shape[0]
    l = nb * BLOCK
    bsz = t // l

    def body(q_ref, km_ref, kp_ref, kc_ref, vm_ref, vp_ref, vc_ref, sink_ref, slope_ref, o_ref, lse_ref):
        n = pl.program_id(0)
        lane_group = lax.broadcasted_iota(jnp.int32, (GROUP_ROWS, LANES), 1) // HEAD_DIM
        products = {}
        for b in range(bsz):
            keys = _three_blocks(km_ref.at[b], kp_ref.at[b], kc_ref.at[b])
            for g in range(A_KV_HEADS):
                products[b, g] = lax.dot_general(_stack_heads(q_ref.at[b], g), keys, NT_DIMS,
                                                 preferred_element_type=F32)
        for b in range(bsz):
            values = _three_blocks(vm_ref.at[b], vp_ref.at[b], vc_ref.at[b])
            for g in range(A_KV_HEADS):
                sink = sink_ref[g]
                s_m, s_p, s_c = _swa_logits(products[b, g], slope_ref[g], n)
                m = jnp.maximum(jnp.max(jnp.maximum(jnp.maximum(s_m, s_p), s_c), axis=-1, keepdims=True), sink)
                m_wide = jnp.broadcast_to(m, (GROUP_ROWS, BLOCK))
                e_m = jnp.exp(s_m - m_wide)
                e_p = jnp.exp(s_p - m_wide)
                e_c = jnp.exp(s_c - m_wide)
                z = jnp.sum((e_m + e_p) + e_c, axis=-1, keepdims=True) + jnp.exp(sink - m)
                inv = jnp.broadcast_to(1.0 / z, (GROUP_ROWS, BLOCK))
                probs = jnp.concatenate([(e_m * inv).astype(BF16), (e_p * inv).astype(BF16),
                                         (e_c * inv).astype(BF16)], axis=1)
                o = jnp.dot(probs, values, preferred_element_type=F32)
                _unstack_heads(o_ref.at[b], g, jnp.where(lane_group == g, o, 0.0))
                lse_ref[b, 0, g] = m + jnp.log(z)

    qspec, kv_m, kv_p, kv_c, rowspec, lsespec = _swa_specs(bsz, nb)
    by_example = lambda a: a.reshape(bsz, l, a.shape[1])
    q3, k3, v3 = by_example(q), by_example(k), by_example(v)
    o, lse = pl.pallas_call(
        body,
        name="swa_fwd",
        grid=(nb,),
        in_specs=[qspec, kv_m, kv_p, kv_c, kv_m, kv_p, kv_c, rowspec, rowspec],
        out_specs=[qspec, lsespec],
        out_shape=[jax.ShapeDtypeStruct((bsz, l, A_PAD_WIDTH), BF16),
                   jax.ShapeDtypeStruct((bsz, nb, A_KV_HEADS, GROUP_ROWS, 1), F32)],
        compiler_params=_cparams(("arbitrary",)),
    )(q3, k3, k3, k3, v3, v3, v3, sink_rows, slope_rows)
    return o.reshape(t, A_PAD_WIDTH), lse


def _swa_bwd(q, k, v, do, lse, sink_rows, slope_rows, nb):
    t = q.shape[0]
    l = nb * BLOCK
    bsz = t // l

    def body(q_ref, km_ref, kp_ref, kc_ref, vm_ref, vp_ref, vc_ref, do_ref, lse_ref, sink_ref, slope_ref,
             dq_ref, dk_ref, dv_ref, dsink_ref, dk_acc, dv_acc):
        n = pl.program_id(0)

        @pl.when(n == 0)
        def _():
            dk_acc[...] = jnp.zeros_like(dk_acc)
            dv_acc[...] = jnp.zeros_like(dv_acc)
            dsink_ref[...] = jnp.zeros_like(dsink_ref)

        first_half = lax.broadcasted_iota(jnp.int32, (BLOCK, LANES), 1) < HEAD_DIM
        prev = jnp.maximum(n - 1, 0)
        products = {}
        for b in range(bsz):
            keys = _three_blocks(km_ref.at[b], kp_ref.at[b], kc_ref.at[b])
            values = _three_blocks(vm_ref.at[b], vp_ref.at[b], vc_ref.at[b])
            for g in range(A_KV_HEADS):
                products[b, g] = (
                    lax.dot_general(_stack_heads(q_ref.at[b], g), keys, NT_DIMS, preferred_element_type=F32),
                    lax.dot_general(_stack_heads(do_ref.at[b], g), values, NT_DIMS, preferred_element_type=F32))
        for b in range(bsz):
            keys = _three_blocks(km_ref.at[b], kp_ref.at[b], kc_ref.at[b])
            for g in range(A_KV_HEADS):
                qq = _stack_heads(q_ref.at[b], g)
                dob = _stack_heads(do_ref.at[b], g)
                lse_g = lse_ref[b, 0, g]
                lse_wide = jnp.broadcast_to(lse_g, (GROUP_ROWS, BLOCK))
                qk, dp_all = products[b, g]
                probs = [jnp.exp(s - lse_wide) for s in _swa_logits(qk, slope_ref[g], n)]
                dps = [dp_all[:, i * BLOCK:(i + 1) * BLOCK] for i in range(3)]
                delta = jnp.sum((probs[0] * dps[0] + probs[1] * dps[1]) + probs[2] * dps[2], axis=-1, keepdims=True)
                delta_wide = jnp.broadcast_to(delta, (GROUP_ROWS, BLOCK))
                ds = jnp.concatenate([(p * (dp - delta_wide)).astype(BF16) for p, dp in zip(probs, dps)], axis=1)
                pb = jnp.concatenate([p.astype(BF16) for p in probs], axis=1)
                dq = jnp.dot(ds, keys, preferred_element_type=F32) * SCALE
                dk_all = lax.dot_general(ds, qq, TN_DIMS, preferred_element_type=F32) * SCALE
                dv_all = lax.dot_general(pb, dob, TN_DIMS, preferred_element_type=F32)
                for i, start in enumerate((0, prev * BLOCK, n * BLOCK)):
                    rows = pl.ds(pl.multiple_of(start, BLOCK), BLOCK)
                    dk_acc[b, rows, :] += dk_all[i * BLOCK:(i + 1) * BLOCK]
                    dv_acc[b, rows, :] += dv_all[i * BLOCK:(i + 1) * BLOCK]
                for pair in range(A_GROUP // 2):
                    even = dq[2 * pair * BLOCK:(2 * pair + 1) * BLOCK]
                    odd = dq[(2 * pair + 1) * BLOCK:(2 * pair + 2) * BLOCK]
                    left = even if g == 0 else pltpu.roll(even, HEAD_DIM, 1)
                    right = pltpu.roll(odd, HEAD_DIM, 1) if g == 0 else odd
                    tile = (A_GROUP // 2) * g + pair
                    dq_ref[b, :, tile * LANES:(tile + 1) * LANES] = jnp.where(first_half, left, right).astype(BF16)
                dsink_ref[b, g] += -(jnp.exp(sink_ref[g] - lse_g) * delta)

        @pl.when(n == nb - 1)
        def _():
            dk_ref[...] = dk_acc[...].astype(BF16)
            dv_ref[...] = dv_acc[...].astype(BF16)

    qspec, kv_m, kv_p, kv_c, rowspec, lsespec = _swa_specs(bsz, nb)
    kv_all = pl.BlockSpec((bsz, l, LANES), lambda n: (0, 0, 0))
    by_example = lambda a: a.reshape(bsz, l, a.shape[1])
    q3, k3, v3 = by_example(q), by_example(k), by_example(v)
    dq, dk, dv, dsink = pl.pallas_call(
        body,
        name="swa_bwd",
        grid=(nb,),
        in_specs=[qspec, kv_m, kv_p, kv_c, kv_m, kv_p, kv_c, qspec, lsespec, rowspec, rowspec],
        out_specs=[pl.BlockSpec((bsz, BLOCK, A_WIDTH), lambda n: (0, n, 0)), kv_all, kv_all,
                   pl.BlockSpec((bsz, A_KV_HEADS, GROUP_ROWS, 1), lambda n: (0, 0, 0, 0))],
        out_shape=[jax.ShapeDtypeStruct((bsz, l, A_WIDTH), BF16),
                   jax.ShapeDtypeStruct((bsz, l, LANES), BF16),
                   jax.ShapeDtypeStruct((bsz, l, LANES), BF16),
                   jax.ShapeDtypeStruct((bsz, A_KV_HEADS, GROUP_ROWS, 1), F32)],
        scratch_shapes=[pltpu.VMEM((bsz, l, LANES), F32), pltpu.VMEM((bsz, l, LANES), F32)],
        compiler_params=_cparams(("arbitrary",)),
    )(q3, k3, k3, k3, v3, v3, v3, by_example(do), lse, sink_rows, slope_rows)
    return dq.reshape(t, A_WIDTH), dk.reshape(t, LANES), dv.reshape(t, LANES), dsink


CHUNK = KEY_BLOCKS * BLOCK


def _fox_chunk(qb, ci):
    sb = jnp.maximum(jnp.minimum(KEY_BLOCKS * ci, qb + 1 - KEY_BLOCKS), 0)
    lo = jnp.maximum(ci * CHUNK, N_PAD)
    return sb, lo, pl.ds(pl.multiple_of(sb * BLOCK, BLOCK), CHUNK)


def _fox_logits(s_ref, cr_ref, e, j, sb, lo, qb):
    lane = lax.broadcasted_iota(jnp.int32, (BLOCK, BLOCK), 1)
    ahead = lane - lax.broadcasted_iota(jnp.int32, (BLOCK, BLOCK), 0)
    first = (sb + j) * BLOCK
    s = s_ref[e, :, j * BLOCK:(j + 1) * BLOCK] - cr_ref[e, sb + j]
    return jnp.where((ahead <= qb * BLOCK - first) & (lane >= lo - first), s, NEG)


FOX_PAIRS = 4
FOX_HEADS = 2 * FOX_PAIRS
FOX_STEPS = B_PAIRS // FOX_PAIRS


def _fox_specs(nb):
    l = nb * BLOCK
    q_spec = pl.BlockSpec((BLOCK, FOX_PAIRS * LANES), lambda b, p, i: (b * nb + i, p))
    kv_spec = pl.BlockSpec((l, FOX_HEADS * LANES), lambda b, p, i: (b, p))
    cc_spec = pl.BlockSpec((FOX_HEADS, BLOCK, 1), lambda b, p, i: (b * FOX_STEPS + p, i, 0))
    cr_spec = pl.BlockSpec((FOX_HEADS, nb, 1, BLOCK), lambda b, p, i: (b * FOX_STEPS + p, 0, 0, 0))
    return q_spec, kv_spec, cc_spec, cr_spec


def _fox_fwd(q, k, v, c_row, nb, exchange=None):
    t = q.shape[0]
    bsz = t // (nb * BLOCK)
    assert nb >= KEY_BLOCKS

    n_x = len(exchange.ins) if exchange else 0

    def body(*refs):
        q_ref, k_ref, v_ref, cr_ref = refs[:4]
        o_ref, ox_ref, lse_ref = refs[4 + n_x:7 + n_x]
        s_scr, hi_scr, lo_scr = refs[7 + 2 * n_x:10 + 2 * n_x]
        qb = pl.program_id(2)
        if exchange:
            first = (pl.program_id(0) == 0) & (pl.program_id(1) == 0)
            last = (pl.program_id(0) == bsz - 1) & (pl.program_id(1) == FOX_STEPS - 1)
            _host_exchange(exchange, refs[4:4 + n_x], refs[7 + n_x:7 + 2 * n_x], refs[10 + 2 * n_x:],
                           first & (qb == 0), first & (qb == 2 * nb // 3), last & (qb == nb - 1),
                           late=last & (qb == 0))
        qs = [q_ref[:, a * LANES:(a + 1) * LANES] * SCALE for a in range(FOX_PAIRS)]
        first_half = lax.broadcasted_iota(jnp.int32, (BLOCK, LANES), 1) < HEAD_DIM

        def step(ci, carry):
            stats, accs = carry[:2 * FOX_HEADS], carry[2 * FOX_HEADS:]
            sb, lo, krows = _fox_chunk(qb, ci)
            for e in range(FOX_HEADS):
                s_scr[e] = lax.dot_general(qs[e // 2], k_ref[krows, e * LANES:(e + 1) * LANES], NT_DIMS,
                                           preferred_element_type=F32)
            new_stats, new_accs = [], []
            for a in range(FOX_PAIRS):
                alphas = []
                pv = jnp.zeros((BLOCK, LANES), F32)
                pv_lo = jnp.zeros((BLOCK, LANES), F32)
                for e in (2 * a, 2 * a + 1):
                    m, z = stats[2 * e], stats[2 * e + 1]
                    tile = slice(e * LANES, (e + 1) * LANES)
                    top = None
                    for j in range(KEY_BLOCKS):
                        s = _fox_logits(s_scr, cr_ref, e, j, sb, lo, qb)
                        s_scr[e, :, j * BLOCK:(j + 1) * BLOCK] = s
                        top = s if top is None else jnp.maximum(top, s)
                    m_new = jnp.maximum(m, jnp.max(top, axis=-1, keepdims=True))
                    alpha = jnp.exp(m - m_new)
                    m_wide = jnp.broadcast_to(m_new, (BLOCK, BLOCK))
                    total = None
                    for j in range(KEY_BLOCKS):
                        cols = slice(j * BLOCK, (j + 1) * BLOCK)
                        p = jnp.exp(s_scr[e, :, cols] - m_wide)
                        total = p if total is None else total + p
                        hi = p.astype(BF16)
                        hi_scr[e, :, cols] = hi
                        lo_scr[e, :, cols] = (p - hi.astype(F32)).astype(BF16)
                    z = alpha * z + jnp.sum(total, axis=-1, keepdims=True)
                    vv = v_ref[krows, tile]
                    pv = pv + jnp.dot(hi_scr[e], vv, preferred_element_type=F32)
                    pv_lo = pv_lo + jnp.dot(lo_scr[e], vv, preferred_element_type=F32)
                    new_stats += [m_new, z]
                    alphas.append(alpha)
                alpha = jnp.where(first_half, alphas[0], alphas[1])
                new_accs += [alpha * accs[2 * a] + pv, alpha * accs[2 * a + 1] + pv_lo]
            return (*new_stats, *new_accs)

        col = lambda val: jnp.full((BLOCK, 1), val, F32)
        done = lax.fori_loop(
            0, (qb + KEY_BLOCKS) // KEY_BLOCKS, step,
            (col(NEG), col(0.0)) * FOX_HEADS + (jnp.zeros((BLOCK, LANES), F32),) * (2 * FOX_PAIRS))
        for a in range(FOX_PAIRS):
            m0, z0, m1, z1 = done[4 * a:4 * a + 4]
            acc, acc_lo = done[2 * FOX_HEADS + 2 * a:2 * FOX_HEADS + 2 * a + 2]
            inv = 1.0 / jnp.where(first_half, z0, z1)
            tile = slice(a * LANES, (a + 1) * LANES)
            o_ref[:, tile] = (acc * inv).astype(BF16)
            ox_ref[:, tile] = (acc + acc_lo) * inv
            lse_ref[2 * a] = m0 + jnp.log(z0)
            lse_ref[2 * a + 1] = m1 + jnp.log(z1)

    q_spec, kv_spec, cc_spec, cr_spec = _fox_specs(nb)
    outs = pl.pallas_call(
        body,
        name="fox_fwd",
        grid=(bsz, FOX_STEPS, nb),
        in_specs=[q_spec, kv_spec, kv_spec, cr_spec] + [HBM_SPEC] * n_x,
        out_specs=[q_spec, q_spec, cc_spec] + [HBM_SPEC] * n_x,
        out_shape=[jax.ShapeDtypeStruct((t, B_WIDTH), BF16), jax.ShapeDtypeStruct((t, B_WIDTH), F32),
                   jax.ShapeDtypeStruct((bsz * B_HEADS, nb * BLOCK, 1), F32)] + (exchange.out_shape if exchange else []),
        scratch_shapes=[pltpu.VMEM((FOX_HEADS, BLOCK, CHUNK), F32), pltpu.VMEM((FOX_HEADS, BLOCK, CHUNK), BF16),
                        pltpu.VMEM((FOX_HEADS, BLOCK, CHUNK), BF16)] + (exchange.scratch if exchange else []),
        compiler_params=_cparams(("arbitrary",) * 3 if exchange else ("parallel", "parallel", "arbitrary")),
    )(q, k, v, c_row, *(exchange.ins if exchange else []))
    return outs[:3], outs[3:]


def _fox_bwd(q, k, v, o_exact, do, lse, c_row, nb, exchange=None):
    t = q.shape[0]
    l = nb * BLOCK
    bsz = t // l

    n_x = len(exchange.ins) if exchange else 0

    def body(*refs):
        q_ref, k_ref, v_ref, ox_ref, do_ref, lse_ref, cr_ref = refs[:7]
        dq_ref, dk_ref, dv_ref, dc_ref = refs[7 + n_x:11 + n_x]
        dk_acc, dv_acc, s_scr, dp_scr, p_scr, ds_scr, dq_scr = refs[11 + 2 * n_x:18 + 2 * n_x]
        qb = pl.program_id(2)
        if exchange:
            first = (pl.program_id(0) == 0) & (pl.program_id(1) == 0)
            last = (pl.program_id(0) == bsz - 1) & (pl.program_id(1) == FOX_STEPS - 1)
            _host_exchange(exchange, refs[7:7 + n_x], refs[11 + n_x:11 + 2 * n_x], refs[18 + 2 * n_x:],
                           first & (qb == 0), last & (qb == 0), last & (qb == nb - 1))

        @pl.when(qb == 0)
        def _():
            dk_acc[...] = jnp.zeros_like(dk_acc)
            dv_acc[...] = jnp.zeros_like(dv_acc)
            dc_ref[...] = jnp.zeros_like(dc_ref)

        top_half = lax.broadcasted_iota(jnp.int32, (LANES, BLOCK), 0) < HEAD_DIM
        pair_t = lambda x: jnp.concatenate([jnp.where(top_half, x.T, 0), jnp.where(top_half, 0, x.T)], axis=1)
        first_half = lax.broadcasted_iota(jnp.int32, (BLOCK, LANES), 1) < HEAD_DIM
        wide = lambda col: jnp.broadcast_to(col, (BLOCK, BLOCK))
        qs, dobs, qs_t, dob_t, deltas = [], [], [], [], []
        for a in range(FOX_PAIRS):
            tile = slice(a * LANES, (a + 1) * LANES)
            qs.append(q_ref[:, tile] * SCALE)
            dobs.append(do_ref[:, tile])
            qs_t.append(pair_t(qs[a]))
            dob_t.append(pair_t(dobs[a]))
            weighted = dobs[a].astype(F32) * ox_ref[:, tile]
            deltas += [wide(jnp.sum(jnp.where(first_half, weighted, 0.0), axis=-1, keepdims=True)),
                       wide(jnp.sum(jnp.where(first_half, 0.0, weighted), axis=-1, keepdims=True))]
        lses = [wide(lse_ref[e]) for e in range(FOX_HEADS)]

        dq_scr[...] = jnp.zeros(dq_scr.shape, F32)

        def step(ci, carry):
            sb, lo, krows = _fox_chunk(qb, ci)
            for e in range(FOX_HEADS):
                tile = slice(e * LANES, (e + 1) * LANES)
                s_scr[e] = lax.dot_general(qs[e // 2], k_ref[krows, tile], NT_DIMS, preferred_element_type=F32)
                dp_scr[e] = lax.dot_general(dobs[e // 2], v_ref[krows, tile], NT_DIMS, preferred_element_type=F32)
            for a in range(FOX_PAIRS):
                for e in (2 * a, 2 * a + 1):
                    tile = slice(e * LANES, (e + 1) * LANES)
                    kk = k_ref[krows, tile]
                    for j in range(KEY_BLOCKS):
                        cols = slice(j * BLOCK, (j + 1) * BLOCK)
                        p = jnp.exp(_fox_logits(s_scr, cr_ref, e, j, sb, lo, qb) - lses[e])
                        ds = p * (dp_scr[e, :, cols] - deltas[e])
                        dc_ref[e, sb + j] -= jnp.sum(ds, axis=0, keepdims=True)
                        p_scr[e, :, cols] = p.astype(BF16)
                        ds_scr[e, :, cols] = ds.astype(BF16)
                    dq_scr[a] += jnp.dot(ds_scr[e], kk, preferred_element_type=F32)
                both = slice(2 * a, 2 * a + 2)
                dk_t = jnp.dot(qs_t[a], ds_scr[both].reshape(2 * BLOCK, CHUNK), preferred_element_type=F32)
                dv_t = jnp.dot(dob_t[a], p_scr[both].reshape(2 * BLOCK, CHUNK), preferred_element_type=F32)
                for j in range(KEY_BLOCKS):
                    cols = slice(j * BLOCK, (j + 1) * BLOCK)
                    dk_acc[a * nb + sb + j] += dk_t[:, cols]
                    dv_acc[a * nb + sb + j] += dv_t[:, cols]
            return carry

        lax.fori_loop(0, (qb + KEY_BLOCKS) // KEY_BLOCKS, step, 0)
        for a in range(FOX_PAIRS):
            dq_ref[:, a * LANES:(a + 1) * LANES] = (dq_scr[a] * SCALE).astype(BF16)

        @pl.when(qb == nb - 1)
        def _():
            for a in range(FOX_PAIRS):
                for kb in range(nb):
                    rows = slice(kb * BLOCK, (kb + 1) * BLOCK)
                    for acc, out_ref in ((dk_acc, dk_ref), (dv_acc, dv_ref)):
                        out_ref[rows, a * LANES:(a + 1) * LANES] = acc[a * nb + kb].T.astype(BF16)

    q_spec, kv_spec, cc_spec, cr_spec = _fox_specs(nb)
    dkv_spec = pl.BlockSpec((l, FOX_PAIRS * LANES), lambda b, p, i: (b, p))
    outs = pl.pallas_call(
        body,
        name="fox_bwd",
        grid=(bsz, FOX_STEPS, nb),
        in_specs=[q_spec, kv_spec, kv_spec, q_spec, q_spec, cc_spec, cr_spec] + [HBM_SPEC] * n_x,
        out_specs=[q_spec, dkv_spec, dkv_spec, cr_spec] + [HBM_SPEC] * n_x,
        out_shape=[jax.ShapeDtypeStruct((t, B_WIDTH), BF16), jax.ShapeDtypeStruct((t, B_WIDTH), BF16),
                   jax.ShapeDtypeStruct((t, B_WIDTH), BF16),
                   jax.ShapeDtypeStruct((bsz * B_HEADS, nb, 1, BLOCK), F32)] + (exchange.out_shape if exchange else []),
        scratch_shapes=[pltpu.VMEM((FOX_PAIRS * nb, LANES, BLOCK), F32), pltpu.VMEM((FOX_PAIRS * nb, LANES, BLOCK), F32),
                        pltpu.VMEM((FOX_HEADS, BLOCK, CHUNK), F32), pltpu.VMEM((FOX_HEADS, BLOCK, CHUNK), F32),
                        pltpu.VMEM((FOX_HEADS, BLOCK, CHUNK), BF16), pltpu.VMEM((FOX_HEADS, BLOCK, CHUNK), BF16),
                        pltpu.VMEM((FOX_PAIRS, BLOCK, LANES), F32)]
        + (exchange.scratch if exchange else []),
        compiler_params=_cparams(("arbitrary",) * 3 if exchange else ("parallel", "parallel", "arbitrary")),
    )(q, k, v, o_exact, do, lse, c_row, *(exchange.ins if exchange else []))
    return outs[:4], outs[4:]


def _loss_head(h, final_w, target):
    bsz, l, d = h.shape
    nb = l // BLOCK

    def body(h_ref, w_ref, t_ref, loss_ref, dh_ref, dw_ref):
        n = pl.program_id(0)

        @pl.when(n == 0)
        def _():
            loss_ref[...] = jnp.zeros_like(loss_ref)
            dw_ref[...] = jnp.zeros_like(dw_ref)
            dh_ref[...] = jnp.zeros_like(dh_ref)

        @pl.when(n > 0)
        def _():
            w = w_ref[...]
            for b in range(bsz):
                hh = h_ref[b]
                r = _rms_scale(hh)
                err = (hh * r) * w - t_ref[b]
                loss_ref[...] += 0.5 * jnp.sum(jnp.mean(err * err, axis=-1, keepdims=True), axis=0, keepdims=True)
                dy = err * (1.0 / d)
                dh, dw = _rms_bwd(dy, hh, w)
                dh_ref[b] = dh
                dw_ref[...] += dw

    return pl.pallas_call(
        body,
        name="loss_head",
        grid=(nb,),
        in_specs=[
            pl.BlockSpec((bsz, BLOCK, d), lambda n: (0, n, 0)),
            pl.BlockSpec((1, d), lambda n: (0, 0)),
            pl.BlockSpec((bsz, BLOCK, d), lambda n: (0, jnp.maximum(n - 1, 0), 0)),
        ],
        out_specs=[
            pl.BlockSpec((1, 128), lambda n: (0, 0)),
            pl.BlockSpec((bsz, BLOCK, d), lambda n: (0, n, 0)),
            pl.BlockSpec((1, d), lambda n: (0, 0)),
        ],
        out_shape=[jax.ShapeDtypeStruct((1, 128), F32), jax.ShapeDtypeStruct((bsz, l, d), F32),
                   jax.ShapeDtypeStruct((1, d), F32)],
        compiler_params=_cparams(("arbitrary",)),
    )(h, final_w, target)


def _pad_tiles(w, src, heads, lane_slot, axis):
    pieces = []
    for h in range(heads):
        x = lax.slice_in_dim(w, src + HEAD_DIM * h, src + HEAD_DIM * (h + 1), axis=axis)
        z = jnp.zeros_like(x)
        pieces += [x, z] if lane_slot(h) == 0 else [z, x]
    return pieces


def _unpad_tiles(g, off, heads, lane_slot, axis):
    return [lax.slice_in_dim(g, off + LANES * h + HEAD_DIM * lane_slot(h),
                             off + LANES * h + HEAD_DIM * (lane_slot(h) + 1), axis=axis) for h in range(heads)]


REF_RUNS = ((0, SRC_QB, 0, 0), (SRC_QB, SRC_F, 1, 0), (SRC_F, SRC_GA, 0, OFF_F), (SRC_GA, W_IN_COLS, 2, 0))
P_RUNS = ((0, OFF_F, 0), (OFF_F, OFF_F + B_HEADS, SRC_F), (OFF_QB, OFF_GA, SRC_QB), (OFF_GA, P_COLS, SRC_GA))
SHARD_COLS = W_IN_COLS // N_CHIPS
SHARD_PAD_COLS = -(-SHARD_COLS // LANES) * LANES


def _place(tile, lane, src_ref, src, dst, length):
    for t in range(src // LANES, (src + length - 1) // LANES + 1):
        x = src_ref[:, t * LANES:(t + 1) * LANES].astype(F32)
        shift = (dst - src) % LANES
        moved = pltpu.roll(x, shift, 1) if shift else x
        from_t = (lane >= max(dst, dst + t * LANES - src)) & (lane < min(dst + length, dst + (t + 1) * LANES - src))
        tile = jnp.where(from_t, moved, tile)
    return tile


def _layout_w_in(stacked):
    d = stacked.shape[1]
    rows = _tile(d, 256)

    def body(s_ref, o_ref):
        lane = lax.broadcasted_iota(jnp.int32, (rows, LANES), 1)
        for lo in range(0, P_COLS, LANES):
            tile = jnp.zeros((rows, LANES), F32)
            for first, end, ref_col in P_RUNS:
                start, stop = max(lo, first), min(lo + LANES, end)
                while start < stop:
                    k, src = divmod(ref_col + start - first, SHARD_COLS)
                    length = min(stop - start, SHARD_COLS - src)
                    tile = _place(tile, lane, s_ref.at[k], src, start - lo, length)
                    start += length
            o_ref[:, lo:lo + LANES] = tile.astype(o_ref.dtype)

    return pl.pallas_call(
        body,
        name="layout_w_in",
        grid=(d // rows,),
        in_specs=[pl.BlockSpec((N_CHIPS, rows, SHARD_PAD_COLS), lambda i: (0, i, 0))],
        out_specs=pl.BlockSpec((rows, P_COLS), lambda i: (i, 0)),
        out_shape=jax.ShapeDtypeStruct((d, P_COLS), stacked.dtype),
        compiler_params=_cparams(("parallel",)),
    )(stacked)


def _stack_w_in_grad(pieces):
    d = pieces[0].shape[0]
    rows = _tile(d, 256)

    def body(*refs):
        piece_refs, o_ref = refs[:-1], refs[-1]
        lane = lax.broadcasted_iota(jnp.int32, (rows, LANES), 1)
        for k in range(N_CHIPS):
            for j in range(0, SHARD_COLS, LANES):
                width = min(LANES, SHARD_COLS - j)
                lo = k * SHARD_COLS + j
                tile = jnp.zeros((rows, LANES), F32)
                for first, end, piece, at in REF_RUNS:
                    start, stop = max(lo, first), min(lo + width, end)
                    if start >= stop:
                        continue
                    tile = _place(tile, lane, piece_refs[piece], at + start - first, start - lo, stop - start)
                o_ref[k, :, j:j + width] = tile[:, :width]

    return pl.pallas_call(
        body,
        name="stack_w_in_grad",
        grid=(d // rows,),
        in_specs=[pl.BlockSpec((rows, p.shape[1]), lambda i: (i, 0)) for p in pieces],
        out_specs=pl.BlockSpec((N_CHIPS, rows, SHARD_COLS), lambda i: (0, i, 0)),
        out_shape=jax.ShapeDtypeStruct((N_CHIPS, d, SHARD_COLS), F32),
        compiler_params=_cparams(("parallel",)),
    )(*pieces)


def _local_step(x, target, meta, norms, b_forget, sinks, w, comm=None):
    n1, nmix, n2, nfin = norms
    w1i, w1o = w[:2]
    if meta is not None:
        x = jnp.concatenate([jnp.zeros((x.shape[0], N_PAD, x.shape[2]), F32),
                             jnp.broadcast_to(meta[None], (x.shape[0], N_META, x.shape[2])), x], axis=1)
    bsz, l, d = x.shape
    nb = l // BLOCK
    t = bsz * l
    h0 = x.reshape(t, d)

    if comm is None:
        (h1, g1, u1), _ = _ffn_fwd(h0, n1, w1i, w1o)
        w_in, wa, wb, wo, w2i, w2o = w[2:]
        w_in = jnp.pad(w_in.reshape(d, N_CHIPS, SHARD_COLS).transpose(1, 0, 2),
                       ((0, 0), (0, 0), (0, SHARD_PAD_COLS - SHARD_COLS)))
    else:
        (h1, g1, u1), (w_in,) = _ffn_fwd(h0, n1, w1i, w1o, comm.gather(GATHER_PROJ))
    wp = _layout_w_in(w_in)
    un, qa, ka, va, qb, kb, vb, ga, gb, f_logit = _proj_fwd(h1, nmix, wp)
    b_pad = jnp.concatenate([b_forget, jnp.zeros((1, F_COLS - B_HEADS), F32)], axis=1)
    c = _forget_cumsum(f_logit, b_pad, nb)
    c_heads = c[:, :B_HEADS].reshape(bsz, l, B_HEADS).transpose(0, 2, 1).reshape(bsz * B_HEADS, l)
    c_row = c_heads.reshape(bsz * B_HEADS, nb, 1, BLOCK)

    slopes = jnp.exp2(-8.0 * jnp.arange(1, A_HEADS + 1, dtype=F32) / A_HEADS)
    slope_rows = jnp.repeat(slopes.reshape(A_KV_HEADS, A_GROUP), BLOCK, axis=1)[:, :, None]
    sink_rows = jnp.repeat(sinks.reshape(A_KV_HEADS, A_GROUP), BLOCK, axis=1)[:, :, None]

    oa, lse_a = _swa_fwd(qa, ka, va, sink_rows, slope_rows, nb)
    if comm is None:
        (ob, ob_exact, lse_b), _ = _fox_fwd(qb, kb, vb, c_row, nb)
    else:
        (ob, ob_exact, lse_b), (wa, wb, wo, w2i, w2o) = _fox_fwd(qb, kb, vb, c_row, nb, comm.gather(GATHER_LATE))
    wa_p = jnp.concatenate(_pad_tiles(wa, 0, A_HEADS, A_SLOT, 0), axis=0)
    h2, mixed = _merge_fwd(h1, oa, ob, ga, gb, wa_p, wb, wo)
    (h3, g2, u2), _ = _ffn_fwd(h2, n2, w2i, w2o)
    loss, dh3, d_nfin = _loss_head(h3.reshape(bsz, l, d), nfin, target)

    (dh2, n2b, a2, dgu2, df2, dn2_parts), _ = _ffn_bwd(dh3.reshape(t, d), h2, n2, g2, u2, w2i, w2o)
    g_w2o = _tn_matmul(a2, df2, "grad_ffn2_w_out")
    g_w2i = _tn_matmul(n2b, dgu2, "grad_ffn2_w_in")

    hosted = comm.swap("ffn2", dict(ffn2_w_in=g_w2i, ffn2_w_out=g_w2o)) if comm else None
    (doa, dob, dgates, g_wa, g_wb, g_wo), swapped = _merge_bwd(dh2, oa, ob, ga, gb, wa_p, wb, wo, hosted)
    g_wa = jnp.concatenate(_unpad_tiles(g_wa, 0, A_HEADS, A_SLOT, 0), axis=0)

    dqa, dka, dva, dsink_rows = _swa_bwd(qa, ka, va, doa, lse_a, sink_rows, slope_rows, nb)
    hosted = comm.scatter("ffn2", swapped) if comm else None
    (dqb, dkb, dvb, dc_row), pieces = _fox_bwd(qb, kb, vb, ob_exact, dob, lse_b, c_row, nb, hosted)
    if comm:
        comm.received("ffn2", pieces)
    dc = dc_row.reshape(bsz, B_HEADS, l).transpose(0, 2, 1).reshape(t, B_HEADS)
    dc = jnp.concatenate([dc, jnp.zeros((t, F_COLS - B_HEADS), F32)], axis=1)
    df_logit, db_parts = _forget_cumsum_bwd(dc, f_logit, b_pad, nb)

    dproj = (jnp.concatenate([dqa, dka, dva, df_logit], axis=1), jnp.concatenate([dqb, dkb, dvb], axis=1), dgates)
    g_win = _stack_w_in_grad([_tn_matmul(un, piece, "grad_w_in_" + tag) for piece, tag in zip(dproj, ("a", "b", "gates"))])
    if comm is None:
        g_win = g_win.transpose(1, 0, 2).reshape(d, W_IN_COLS)
    hosted = comm.swap("mixer", dict(w_in=g_win, w_branch_a=g_wa, w_branch_b=g_wb, w_out=g_wo)) if comm else None
    (dh1, dnmix_parts), swapped = _proj_bwd(dh2, h1, nmix, dproj, wp, hosted)
    hosted = comm.scatter("mixer", swapped) if comm else None
    (dh0, n1b, a1, dgu1, df1, dn1_parts), pieces = _ffn_bwd(dh1, h0, n1, g1, u1, w1i, w1o, hosted)
    dh0 = dh0.reshape(bsz, l, d)
    grad_x = dh0[:, PREFIX:]
    small = dict(
        meta_tokens=jnp.sum(dh0[:, N_PAD:PREFIX], axis=0),
        ffn1_norm=jnp.sum(dn1_parts, axis=0),
        mix_norm=jnp.sum(dnmix_parts, axis=0),
        ffn2_norm=jnp.sum(dn2_parts, axis=0),
        final_norm=d_nfin,
        b_forget=jnp.sum(db_parts, axis=0)[:, :B_HEADS],
        attn_sinks=jnp.sum(dsink_rows.reshape(bsz, A_HEADS, BLOCK), axis=(0, 2)).reshape(1, A_HEADS),
    )
    if comm is None:
        g_w1o = _tn_matmul(a1, df1, "grad_ffn1_w_out")
        g_w1i = _tn_matmul(n1b, dgu1, "grad_ffn1_w_in")
    else:
        comm.received("mixer", pieces)
        g_w1o, gathered = _tn_matmul(a1, df1, "grad_ffn1_w_out", comm.small_gather(loss, small))
        comm.small_gathered(gathered)
        swapped = _run_exchange(comm.swap("ffn1_out", dict(ffn1_w_out=g_w1o)), "exchange_halves_ffn1_out")
        g_w1i, pieces = _tn_matmul(n1b, dgu1, "grad_ffn1_w_in", comm.scatter("ffn1_out", swapped))
        comm.received("ffn1_out", pieces)
    big = dict(ffn1_w_in=g_w1i, ffn1_w_out=g_w1o, w_in=g_win, w_branch_a=g_wa, w_branch_b=g_wb,
               w_out=g_wo, ffn2_w_in=g_w2i, ffn2_w_out=g_w2o)
    return loss, grad_x, small, big


BIG = (
    ("ffn1_w_in", (D_MODEL, 5632), 1),
    ("ffn1_w_out", (2816, D_MODEL), 0),
    ("w_in", (D_MODEL, W_IN_COLS), 1),
    ("w_branch_a", (A_WIDTH, D_MODEL), 1),
    ("w_branch_b", (B_WIDTH, D_MODEL), 1),
    ("w_out", (D_MODEL, D_MODEL), 0),
    ("ffn2_w_in", (D_MODEL, 5632), 1),
    ("ffn2_w_out", (2816, D_MODEL), 0),
)
STACKED = "w_in"


def _coords():
    return lax.axis_index("x"), lax.axis_index("y"), lax.axis_index("c")


def _other_chips(x, y):
    return ((1 - x, y), (x, 1 - y), (1 - x, 1 - y))


def _chip_part(ref, name, shape, axis, k):
    if name == STACKED:
        return ref.at[k]
    size = shape[axis] // N_CHIPS
    start = pl.multiple_of(k * size, size)
    return ref.at[pl.ds(start, size), :] if axis == 0 else ref.at[:, pl.ds(start, size)]


def _full_shape(name, shape):
    return (N_CHIPS, shape[0], shape[1] // N_CHIPS) if name == STACKED else shape


class _Exchange:
    def __init__(self, ins, out_shape, n_sems, ops):
        self.ins, self.out_shape, self.n_sems, self.ops = list(ins), list(out_shape), n_sems, ops

    @property
    def scratch(self):
        return [pltpu.SemaphoreType.DMA((self.n_sems,)), pltpu.SemaphoreType.DMA((self.n_sems,))]


SEMS_PER_GATHER = 9


def _gather_exchange(shards, table):
    n = len(table)
    x_nbr, y_nbr, diagonal = 0, 1, 2

    def ops(ins, outs, send_sems, recv_sems):
        x, y, c = _coords()
        mine = 2 * x + y
        sibling = (x, y, 1 - c)
        chips = _other_chips(x, y)
        slots = [2 * chip[0] + chip[1] for chip in chips]

        def part(i, k):
            name, shape, axis = table[i][:3]
            return _chip_part(outs[i], name, shape, axis, k)

        def half(ref, h):
            rows = ref.shape[0] // 2
            return ref.at[pl.ds(pl.multiple_of(h * rows, rows), rows), :]

        def remote(i, sem, src, dst, device):
            sem = SEMS_PER_GATHER * i + sem
            return pltpu.make_async_remote_copy(src, dst, send_sems.at[sem], recv_sems.at[sem],
                                                device_id=device, device_id_type=MESH_ID)

        def own(i):
            return remote(i, 0, ins[i], part(i, mine), sibling)

        def fetch(i, j, slot):
            if table[i][4]:
                src, dst = half(ins[i], c), half(part(i, slot), c)
            else:
                src, dst = ins[i], part(i, slot)
            return remote(i, 1 + j, src, dst, (chips[j][0], chips[j][1], c))

        def relayed(i, via, of):
            region = half(half(part(i, slots[of]), c), via)
            return remote(i, 4 + via, region, region, (chips[via][0], chips[via][1], c))

        def forward(i, j, h):
            region = half(part(i, slots[j]), h)
            return remote(i, 6 + j, region, region, sibling)

        def start():
            for i in range(n):
                for j in (x_nbr, y_nbr) if table[i][4] else (x_nbr, y_nbr, diagonal):
                    fetch(i, j, mine).start()
            for i in range(n):
                own(i).start()

        def relay():
            for i in range(n):
                if not table[i][4]:
                    for j in range(3):
                        fetch(i, j, slots[j]).wait_recv()
                    continue
                fetch(i, y_nbr, slots[y_nbr]).wait_recv()
                relayed(i, x_nbr, y_nbr).start()
                forward(i, y_nbr, c).start()
                fetch(i, x_nbr, slots[x_nbr]).wait_recv()
                relayed(i, y_nbr, x_nbr).start()
                forward(i, x_nbr, c).start()

        def relay_diagonal():
            for i in range(n):
                if table[i][4]:
                    relayed(i, x_nbr, diagonal).wait_recv()
                    relayed(i, y_nbr, diagonal).wait_recv()
                    forward(i, diagonal, c).start()

        def finish():
            for i in range(n):
                own(i).wait()
                for j in range(3):
                    if table[i][4]:
                        forward(i, j, 1 - c).wait_recv()
                        forward(i, j, c).wait_send()
                    if j != diagonal or not table[i][4]:
                        fetch(i, j, mine).wait_send()
                if table[i][4]:
                    relayed(i, x_nbr, y_nbr).wait_send()
                    relayed(i, y_nbr, x_nbr).wait_send()

        return start, relay, relay_diagonal, finish

    out_shape = [jax.ShapeDtypeStruct(_full_shape(name, shape), dtype) for name, shape, _, dtype, _ in table]
    return _Exchange(shards, out_shape, SEMS_PER_GATHER * n, ops)


STREAM_ROWS = 256


def _stream_rows(src, dst, dst_first, buf, sem_in, sem_out):
    bsz, n_rows, _ = src.shape
    chunks = [(b, r) for b in range(bsz) for r in range(0, n_rows, STREAM_ROWS)]

    def load(i):
        b, r = chunks[i]
        return pltpu.make_async_copy(src.at[b, pl.ds(r, STREAM_ROWS), :], buf.at[i % 2], sem_in.at[i % 2])

    def store(i):
        b, r = chunks[i]
        return pltpu.make_async_copy(buf.at[i % 2], dst.at[b, pl.ds(dst_first + r, STREAM_ROWS), :],
                                     sem_out.at[i % 2])

    load(0).start()
    for i in range(len(chunks)):
        if i + 1 < len(chunks):
            if i >= 1:
                store(i - 1).wait()
            load(i + 1).start()
        load(i).wait()
        store(i).start()
    for i in range(max(len(chunks) - 2, 0), len(chunks)):
        store(i).wait()


def _run_exchange(exchange, name):
    n = len(exchange.ins)

    def body(*refs):
        for phase in exchange.ops(refs[:n], refs[n:2 * n], *refs[2 * n:]):
            phase()

    return pl.pallas_call(
        body,
        name=name,
        in_specs=[HBM_SPEC] * n,
        out_specs=[HBM_SPEC] * n,
        out_shape=exchange.out_shape,
        scratch_shapes=exchange.scratch,
    )(*exchange.ins)


CAST_ROWS = 64


def _cast_hosting(arrays, exchange, name, stream=None):
    n_a, n_x = len(arrays), len(exchange.ins)
    k = 1 if stream else 0

    def body(*refs):
        a_in, x_in = refs[:n_a], refs[n_a:n_a + n_x]
        outs = refs[n_a + n_x + k:]
        a_out, x_out = outs[:n_a], outs[n_a:n_a + n_x]
        scratch = outs[n_a + n_x + k:]
        start, *rest = exchange.ops(x_in, x_out, *scratch[:2])
        start()
        for src, dst in zip(a_in, a_out):
            def rows(i, carry, src=src, dst=dst):
                window = pl.ds(pl.multiple_of(i * CAST_ROWS, CAST_ROWS), CAST_ROWS)
                cols = src.shape[1]
                if dst.shape[1] != cols:
                    dst[window, dst.shape[1] - LANES:] = jnp.zeros((CAST_ROWS, LANES), BF16)
                dst[window, :cols] = src[window, :].astype(BF16)
                return carry

            lax.fori_loop(0, src.shape[0] // CAST_ROWS, rows, 0)
        if stream:
            x_ref, h_ref = refs[n_a + n_x], outs[n_a + n_x]
            buf, sem_in, sem_out = scratch[2:]
            _stream_rows(x_ref, h_ref, PREFIX, buf, sem_in, sem_out)
        for phase in rest:
            phase()
        if stream:
            meta = pltpu.make_async_copy(x_out[stream[1]], buf.at[1, pl.ds(0, N_META), :], sem_in.at[0])
            meta.start()
            buf[0, 0:N_PAD, :] = jnp.zeros((N_PAD, buf.shape[2]), F32)
            meta.wait()
            buf[0, N_PAD:PREFIX, :] = buf[1, 0:N_META, :]
            puts = [pltpu.make_async_copy(buf.at[0, pl.ds(0, PREFIX), :], h_ref.at[b, pl.ds(0, PREFIX), :], sem_out.at[b])
                    for b in range(h_ref.shape[0])]
            for put in puts:
                put.start()
            for put in puts:
                put.wait()

    extra_in, extra_out, extra_scratch = [], [], []
    if stream:
        x = stream[0]
        assert x.shape[0] <= 2 and x.shape[1] % STREAM_ROWS == 0 and x.dtype == F32
        extra_in = [x]
        extra_out = [jax.ShapeDtypeStruct((x.shape[0], PREFIX + x.shape[1], x.shape[2]), F32)]
        extra_scratch = [pltpu.VMEM((2, STREAM_ROWS, x.shape[2]), F32),
                         pltpu.SemaphoreType.DMA((2,)), pltpu.SemaphoreType.DMA((2,))]
    outs = pl.pallas_call(
        body,
        name=name,
        in_specs=[VMEM_SPEC] * n_a + [HBM_SPEC] * (n_x + k),
        out_specs=[VMEM_SPEC] * n_a + [HBM_SPEC] * (n_x + k),
        out_shape=[jax.ShapeDtypeStruct((a.shape[0], -(-a.shape[1] // LANES) * LANES), BF16) for a in arrays]
        + exchange.out_shape + extra_out,
        scratch_shapes=exchange.scratch + extra_scratch,
        compiler_params=pltpu.CompilerParams(vmem_limit_bytes=VMEM_LIMIT),
    )(*arrays, *exchange.ins, *extra_in)
    return outs[:n_a], outs[n_a:]


def _host_exchange(exchange, in_refs, out_refs, sem_refs, first, middle, last, late=None):
    start, *relays, finish = exchange.ops(in_refs, out_refs, *sem_refs)
    pl.when(first)(start)
    pl.when(middle)(relays[0])
    if len(relays) > 1:
        pl.when(last if late is None else late)(relays[1])
    pl.when(last)(finish)


def _halves_view(name, shape, axis):
    r, c = shape
    if name == STACKED:
        return (N_CHIPS, 2, r // 2, c // N_CHIPS), lambda ref, h: ref.at[:, h]
    if axis == 1:
        return (2, r // 2, c), lambda ref, h: ref.at[h]
    return (N_CHIPS, 2, r // N_CHIPS // 2, c), lambda ref, h: ref.at[:, h]


def _halves_exchange(grads, entries):
    n_w = len(entries)
    views = [_halves_view(*entry) for entry in entries]

    def ops(ins, outs, send_sems, recv_sems):
        x, y, c = _coords()
        copies = [pltpu.make_async_remote_copy(views[i][1](ins[i], 1 - c), outs[i], send_sems.at[i], recv_sems.at[i],
                                               device_id=(x, y, 1 - c), device_id_type=MESH_ID) for i in range(n_w)]

        def start():
            for cp in copies:
                cp.start()

        def finish():
            for cp in copies:
                cp.wait()

        return start, lambda: None, finish

    half_shape = lambda v: tuple(d for i, d in enumerate(v) if i != (1 if len(v) == 4 else 0))
    out_shape = [jax.ShapeDtypeStruct(half_shape(v[0]), F32) for v in views]
    return _Exchange([g.reshape(v[0]) for g, v in zip(grads, views)], out_shape, n_w, ops)


def _add_sibling(g_view, recv, c, name):
    shape = recv.shape
    if len(shape) == 2:
        tr = _tile(shape[0], 128, 16)
        grid = (shape[0] // tr,)
        g_spec = pl.BlockSpec((None, tr, shape[1]), lambda i, c_ref: (c_ref[0], i, 0))
        r_spec = pl.BlockSpec((tr, shape[1]), lambda i, c_ref: (i, 0))
    else:
        tr = _tile(shape[1], 256, 16)
        grid = (N_CHIPS, shape[1] // tr)
        g_spec = pl.BlockSpec((None, None, tr, shape[2]), lambda k, i, c_ref: (k, c_ref[0], i, 0))
        r_spec = pl.BlockSpec((None, tr, shape[2]), lambda k, i, c_ref: (k, i, 0))

    def body(c_ref, g_ref, r_ref, o_ref):
        o_ref[...] = (g_ref[...] + r_ref[...]).astype(BF16)

    return pl.pallas_call(
        body,
        name="add_sibling_" + name,
        grid_spec=pltpu.PrefetchScalarGridSpec(num_scalar_prefetch=1, grid=grid, in_specs=[g_spec, r_spec],
                                               out_specs=r_spec),
        out_shape=jax.ShapeDtypeStruct(shape, BF16),
        compiler_params=_cparams(("parallel",) * len(grid)),
    )(c, g_view, recv)


def _piece_of(ref, name, axis, k):
    if name == STACKED or axis == 0:
        return ref.at[k]
    size = ref.shape[1] // N_CHIPS
    return ref.at[:, pl.ds(pl.multiple_of(k * size, size), size)]


def _piece_shape(name, shape, axis):
    r, c = shape
    return (r // 2, c // N_CHIPS) if (axis == 1) else (r // N_CHIPS // 2, c)


def _scatter_exchange(partials, entries):
    n_w = len(entries)

    def ops(ins, outs, send_sems, recv_sems):
        x, y, c = _coords()
        chips = _other_chips(x, y)
        copies = []
        for i, (name, _, axis) in enumerate(entries):
            for j, chip in enumerate(chips):
                sem = 3 * i + j
                copies.append(pltpu.make_async_remote_copy(
                    _piece_of(ins[i], name, axis, 2 * chip[0] + chip[1]), outs[i].at[j], send_sems.at[sem],
                    recv_sems.at[sem], device_id=(chip[0], chip[1], c), device_id_type=MESH_ID))

        def start():
            for cp in copies:
                cp.start()

        def finish():
            for cp in copies:
                cp.wait()

        return start, lambda: None, finish

    out_shape = [jax.ShapeDtypeStruct((3,) + _piece_shape(*entry), BF16) for entry in entries]
    return _Exchange(partials, out_shape, 3 * n_w, ops)


def _add_chips(partial, recv, mine, name, axis):
    rows, cols = recv.shape[1:]
    tr = _tile(rows, 256, 16)
    if name == STACKED or axis == 0:
        p_spec = pl.BlockSpec((None, tr, cols), lambda i, k_ref: (k_ref[0], i, 0))
    else:
        p_spec = pl.BlockSpec((tr, cols), lambda i, k_ref: (i, k_ref[0]))

    def body(k_ref, p_ref, r_ref, o_ref):
        f32 = lambda a: a.astype(F32)
        o_ref[...] = ((f32(p_ref[...]) + f32(r_ref[0])) + f32(r_ref[1])) + f32(r_ref[2])

    return pl.pallas_call(
        body,
        name="add_chips_" + name,
        grid_spec=pltpu.PrefetchScalarGridSpec(
            num_scalar_prefetch=1, grid=(rows // tr,),
            in_specs=[p_spec, pl.BlockSpec((3, tr, cols), lambda i, k_ref: (0, i, 0))],
            out_specs=pl.BlockSpec((tr, cols), lambda i, k_ref: (i, 0))),
        out_shape=jax.ShapeDtypeStruct((rows, cols), F32),
        compiler_params=_cparams(("parallel",)),
    )(mine, partial, recv)


def _share_exchange(halves):
    n_w = len(halves)

    def ops(ins, outs, send_sems, recv_sems):
        x, y, c = _coords()
        copies = [pltpu.make_async_remote_copy(ins[i], outs[i], send_sems.at[i], recv_sems.at[i],
                                               device_id=(x, y, 1 - c), device_id_type=MESH_ID) for i in range(n_w)]

        def start():
            for cp in copies:
                cp.start()

        def finish():
            for cp in copies:
                cp.wait()

        return start, lambda: None, finish

    return _Exchange(halves, [jax.ShapeDtypeStruct(h.shape, F32) for h in halves], n_w, ops)


class _SemsFrom:
    def __init__(self, sems, first):
        self.sems, self.first = sems, first

    @property
    def at(self):
        return self

    def __getitem__(self, i):
        return self.sems.at[self.first + i]


def _joined(a, b):
    n_a = len(a.ins)

    def ops(ins, outs, send_sems, recv_sems):
        phases_a = a.ops(ins[:n_a], outs[:n_a], send_sems, recv_sems)
        phases_b = b.ops(ins[n_a:], outs[n_a:], _SemsFrom(send_sems, a.n_sems), _SemsFrom(recv_sems, a.n_sems))
        assert len(phases_a) == len(phases_b)
        return tuple((lambda pa=pa, pb=pb: (pa(), pb())) for pa, pb in zip(phases_a, phases_b))

    return _Exchange(a.ins + b.ins, a.out_shape + b.out_shape, a.n_sems + b.n_sems, ops)


SMALL_ROWS = 168


def _small_exchange(buf):
    def ops(ins, outs, send_sems, recv_sems):
        x, y, c = _coords()
        me = 4 * x + 2 * y + c
        peers = [(x ^ fx, y ^ fy, c ^ fc) for fx in (0, 1) for fy in (0, 1) for fc in (0, 1)][1:]

        def copy(j, slot, dev):
            return pltpu.make_async_remote_copy(ins[0], outs[0].at[slot], send_sems.at[j], recv_sems.at[j],
                                                device_id=dev, device_id_type=MESH_ID)

        own = pltpu.make_async_copy(ins[0], outs[0].at[me], send_sems.at[N_DEV - 1])

        def start():
            own.start()
            for j, dev in enumerate(peers):
                copy(j, me, dev).start()

        def finish():
            for j, dev in enumerate(peers):
                copy(j, 4 * dev[0] + 2 * dev[1] + dev[2], dev).wait()
            own.wait()

        return start, lambda: None, finish

    return _Exchange([buf], [jax.ShapeDtypeStruct((N_DEV,) + buf.shape, F32)], N_DEV, ops)


def _sum_devices(gathered):
    def body(g_ref, out_ref):
        acc = g_ref[0]
        for d in range(1, N_DEV):
            acc = acc + g_ref[d]
        out_ref[...] = acc

    return pl.pallas_call(
        body,
        name="sum_devices",
        in_specs=[VMEM_SPEC],
        out_specs=VMEM_SPEC,
        out_shape=jax.ShapeDtypeStruct(gathered.shape[1:], F32),
    )(gathered)


def _adamw(w, g, m, v, copy_g=False):
    r, rest = w.shape[0], w.shape[1:]
    per_row = 1
    for dim in rest:
        per_row *= dim
    tr = _tile(r, max(8, (5 << 19) // (4 * per_row)), 8 if len(rest) == 1 else 1)

    def body(w_ref, g_ref, m_ref, v_ref, *out_refs):
        d_ref, mo_ref, vo_ref = out_refs[-3:]
        gg = g_ref[...]
        if copy_g:
            out_refs[0][...] = gg
        mm = ADAM_B1 * m_ref[...] + (1.0 - ADAM_B1) * gg
        vv = ADAM_B2 * v_ref[...] + (1.0 - ADAM_B2) * (gg * gg)
        m_hat = mm / (1.0 - ADAM_B1 ** ADAM_STEP)
        v_hat = vv / (1.0 - ADAM_B2 ** ADAM_STEP)
        d_ref[...] = -ADAM_LR * (m_hat / (jnp.sqrt(v_hat) + ADAM_EPS) + ADAM_WD * w_ref[...])
        mo_ref[...] = mm
        vo_ref[...] = vv

    n_out = 4 if copy_g else 3
    spec = pl.BlockSpec((tr,) + rest, lambda i: (i,) + (0,) * len(rest))
    return pl.pallas_call(
        body,
        name="adamw",
        grid=(r // tr,),
        in_specs=[spec] * 4,
        out_specs=[spec] * n_out,
        out_shape=[jax.ShapeDtypeStruct(w.shape, F32)] * n_out,
        compiler_params=_cparams(("parallel",)),
    )(w, g, m, v)


def _adamw_halves(w, own, other, m, v, c, name):
    r, cols = w.shape
    half = r // 2
    tr = _tile(half, 256, 8)
    nt = half // tr
    whole = pl.BlockSpec((tr, cols), lambda h, i, c_ref: (h * nt + i, 0))
    part = pl.BlockSpec((tr, cols), lambda h, i, c_ref: (i, 0))

    def body(c_ref, w_ref, own_ref, other_ref, m_ref, v_ref, g_ref, d_ref, mo_ref, vo_ref):
        gg = jnp.where(pl.program_id(0) == c_ref[0], own_ref[...], other_ref[...])
        g_ref[...] = gg
        mm = ADAM_B1 * m_ref[...] + (1.0 - ADAM_B1) * gg
        vv = ADAM_B2 * v_ref[...] + (1.0 - ADAM_B2) * (gg * gg)
        m_hat = mm / (1.0 - ADAM_B1 ** ADAM_STEP)
        v_hat = vv / (1.0 - ADAM_B2 ** ADAM_STEP)
        d_ref[...] = -ADAM_LR * (m_hat / (jnp.sqrt(v_hat) + ADAM_EPS) + ADAM_WD * w_ref[...])
        mo_ref[...] = mm
        vo_ref[...] = vv

    return pl.pallas_call(
        body,
        name="adamw_" + name,
        grid_spec=pltpu.PrefetchScalarGridSpec(
            num_scalar_prefetch=1, grid=(2, nt),
            in_specs=[whole, part, part, whole, whole], out_specs=[whole] * 4),
        out_shape=[jax.ShapeDtypeStruct((r, cols), F32)] * 4,
        compiler_params=_cparams(("parallel", "parallel")),
    )(c, w, own, other, m, v)


GATHER_FIRST = ("ffn1_w_in", "ffn1_w_out")
GATHER_PROJ = ("w_in",)
GATHER_LATE = ("w_branch_a", "w_branch_b", "w_out", "ffn2_w_in", "ffn2_w_out")


class _Comm:
    def __init__(self, shards, c_arr, mine_arr):
        self.shards, self.c, self.mine = shards, c_arr, mine_arr
        self.groups, self.halves = {}, {}
        self.by_name = {entry[0]: entry for entry in BIG}

    def small_gather(self, loss, small):
        pad_lanes = lambda a: jnp.concatenate([a, jnp.zeros((1, LANES - a.shape[1]), F32)], axis=1)
        buf = jnp.concatenate([
            small["meta_tokens"].reshape(128, LANES),
            small["ffn1_norm"].reshape(8, LANES), small["mix_norm"].reshape(8, LANES),
            small["ffn2_norm"].reshape(8, LANES), small["final_norm"].reshape(8, LANES),
            loss, pad_lanes(small["b_forget"]), pad_lanes(small["attn_sinks"]),
            jnp.zeros((SMALL_ROWS - 163, LANES), F32)], axis=0)
        return _small_exchange(buf)

    def small_gathered(self, outs):
        self.reduced = _sum_devices(outs[0])

    def gather(self, names):
        padded = (STACKED, (D_MODEL, N_CHIPS * SHARD_PAD_COLS), 1)
        table = [(padded if n == STACKED else self.by_name[n]) + (BF16, True) for n in names]
        return _gather_exchange([self.shards[n] for n in names], table)

    def swap(self, tag, grads):
        entries = [self.by_name[n] for n in grads]
        arrays = list(grads.values())
        self.groups[tag] = (entries, arrays)
        return _halves_exchange(arrays, entries)

    def scatter(self, tag, received):
        entries, arrays = self.groups[tag]
        views = [_halves_view(*entry) for entry in entries]
        partials = [_add_sibling(g.reshape(v[0]), r, self.c, name)
                    for g, v, r, (name, _, _) in zip(arrays, views, received, entries)]
        self.groups[tag] = (entries, partials)
        return _scatter_exchange(partials, entries)

    def received(self, tag, pieces):
        entries, partials = self.groups[tag]
        for p, r, (name, _, axis) in zip(partials, pieces, entries):
            self.halves[name] = _add_chips(p, r, self.mine, name, axis)

    def finish(self, tag, name, scatter):
        ready = [n for n, _, _ in BIG if n != name]
        outs = _run_exchange(_joined(scatter, _share_exchange([self.halves[n] for n in ready])), "scatter_chip_sums")
        self.received(tag, outs[:len(scatter.ins)])
        others = dict(zip(ready, outs[len(scatter.ins):]))
        others[name], = _run_exchange(_share_exchange([self.halves[name]]), "share_with_sibling")
        return {n: (self.halves[n], others[n]) for n, _, _ in BIG}


def kernel(x, meta_tokens, ffn1_norm, ffn1_w_in, ffn1_w_out, mix_norm, w_in, b_forget, attn_sinks, w_branch_a, w_branch_b, w_out, ffn2_norm, ffn2_w_in, ffn2_w_out, final_norm, loss_target, m_meta_tokens, m_ffn1_norm, m_ffn1_w_in, m_ffn1_w_out, m_mix_norm, m_w_in, m_b_forget, m_attn_sinks, m_w_branch_a, m_w_branch_b, m_w_out, m_ffn2_norm, m_ffn2_w_in, m_ffn2_w_out, m_final_norm, v_meta_tokens, v_ffn1_norm, v_ffn1_w_in, v_ffn1_w_out, v_mix_norm, v_w_in, v_b_forget, v_attn_sinks, v_w_branch_a, v_w_branch_b, v_w_out, v_ffn2_norm, v_ffn2_w_in, v_ffn2_w_out, v_final_norm):
    given = dict(locals())
    names = ["meta_tokens", "ffn1_norm", "ffn1_w_in", "ffn1_w_out", "mix_norm", "w_in", "b_forget", "attn_sinks",
             "w_branch_a", "w_branch_b", "w_out", "ffn2_norm", "ffn2_w_in", "ffn2_w_out", "final_norm"]
    big_names = [n for n, _, _ in BIG]
    cx, cy, cc = _coords()
    c_arr = cc.reshape(1).astype(jnp.int32)
    mine_arr = (2 * cx + cy).reshape(1).astype(jnp.int32)

    by_name = {entry[0]: entry for entry in BIG}
    shards = {n: given[n][0].astype(BF16) for n in GATHER_FIRST}
    table = [by_name[n] + (BF16, True) for n in GATHER_FIRST] + [("meta_tokens", (N_META, D_MODEL), 1, F32, False)]
    first = _gather_exchange([shards[n] for n in GATHER_FIRST] + [meta_tokens], table)
    late_names = [n for n in big_names if n not in GATHER_FIRST]
    casts, (w1i, w1o, _, stream) = _cast_hosting([given[n][0] for n in late_names], first, "gather_first",
                                                 stream=(x, len(GATHER_FIRST)))
    shards.update(zip(late_names, casts))
    comm = _Comm(shards, c_arr, mine_arr)
    norms = (ffn1_norm, mix_norm, ffn2_norm, final_norm.reshape(1, D_MODEL))
    loss, grad_x, small, big = _local_step(stream, loss_target, None, norms, b_forget, attn_sinks, (w1i, w1o), comm)

    swap = comm.swap("ffn1_in", dict(ffn1_w_in=big["ffn1_w_in"]))
    last = comm.scatter("ffn1_in", _run_exchange(swap, "exchange_halves_ffn1_in"))
    grad_halves = comm.finish("ffn1_in", "ffn1_w_in", last)
    grads = {}

    red = comm.reduced
    meta_cols = red[:128].reshape(N_META, D_MODEL)
    grads["meta_tokens"] = lax.dynamic_slice_in_dim(meta_cols, (2 * cx + cy) * (D_MODEL // N_CHIPS),
                                                    D_MODEL // N_CHIPS, axis=1)
    grads["ffn1_norm"] = red[128:136].reshape(1, D_MODEL)
    grads["mix_norm"] = red[136:144].reshape(1, D_MODEL)
    grads["ffn2_norm"] = red[144:152].reshape(1, D_MODEL)
    grads["final_norm"] = red[152:160].reshape(1, D_MODEL)
    loss_out = red[160, 0]
    grads["b_forget"] = red[161:162, :B_HEADS]
    grads["attn_sinks"] = red[162:163, :A_HEADS]

    out_g, out_d, out_m, out_v = [], [], [], []
    for n in names:
        w_full = given[n]
        shape = w_full.shape
        two_d = (lambda a: a.reshape(shape[-2], shape[-1])) if len(shape) >= 2 else (lambda a: a.reshape(1, shape[0]))
        if n == STACKED:
            own, other = grad_halves[n]
            g_nat = jnp.concatenate([jnp.where(cc == 0, own, other), jnp.where(cc == 0, other, own)], axis=0)
            rows = lambda a: a.reshape(1, shape[-2], shape[-1]).transpose(2, 0, 1)
            unrows = lambda a: a.transpose(1, 2, 0)
            g2, d2, m2, v2 = [unrows(a) for a in _adamw(rows(w_full), rows(g_nat), rows(given["m_" + n]),
                                                         rows(given["v_" + n]), copy_g=True)]
        elif n in grad_halves:
            own, other = grad_halves[n]
            g2, d2, m2, v2 = _adamw_halves(two_d(w_full), own, other, two_d(given["m_" + n]),
                                           two_d(given["v_" + n]), c_arr, n)
        else:
            g2 = two_d(grads[n])
            d2, m2, v2 = _adamw(two_d(w_full), g2, two_d(given["m_" + n]), two_d(given["v_" + n]))
        out_g.append(g2.reshape(shape))
        out_d.append(d2.reshape(shape))
        out_m.append(m2.reshape(shape))
        out_v.append(v2.reshape(shape))
    return (loss_out, grad_x, *out_g, *out_d, *out_m, *out_v)
```

```python
import jax
import jax.numpy as jnp
from jax import lax
from jax.experimental import pallas as pl
from jax.experimental.pallas import tpu as pltpu

F32 = jnp.float32
BF16 = jnp.bfloat16

D_MODEL = 1024
N_META = 16
BLOCK = 128
LANES = 128
PREFIX = BLOCK
N_PAD = PREFIX - N_META
HEAD_DIM = 64
A_HEADS = 8
A_KV_HEADS = 2
A_GROUP = 4
B_HEADS = 8
B_PAIRS = B_HEADS // 2
A_WIDTH = A_HEADS * HEAD_DIM
A_KV_WIDTH = A_KV_HEADS * HEAD_DIM
B_WIDTH = B_HEADS * HEAD_DIM
W_IN_COLS = A_WIDTH + 2 * A_KV_WIDTH + 3 * B_WIDTH + B_HEADS + 2 * D_MODEL
SRC_KA = A_WIDTH
SRC_VA = SRC_KA + A_KV_WIDTH
SRC_QB = SRC_VA + A_KV_WIDTH
SRC_KB = SRC_QB + B_WIDTH
SRC_VB = SRC_KB + B_WIDTH
SRC_F = SRC_VB + B_WIDTH
SRC_GA = SRC_F + B_HEADS
SRC_GB = SRC_GA + D_MODEL
A_PAD_WIDTH = A_HEADS * LANES
B_PAD_WIDTH = B_HEADS * LANES
F_COLS = LANES
OFF_QA = 0
OFF_KA = SRC_KA
OFF_VA = SRC_VA
OFF_F = OFF_VA + A_KV_WIDTH
OFF_QB = OFF_F + F_COLS
OFF_KB = OFF_QB + B_WIDTH
OFF_VB = OFF_KB + B_WIDTH
OFF_GA = OFF_VB + B_WIDTH
OFF_GB = OFF_GA + D_MODEL
P_COLS = OFF_GB + D_MODEL
P_PIECES = ((0, OFF_QB), (OFF_QB, OFF_GA - OFF_QB), (OFF_GA, P_COLS - OFF_GA))
EPS = 1e-6
NEG = -1e30
SCALE = HEAD_DIM ** -0.5
KEY_BLOCKS = 4

ADAM_LR = 0.001
ADAM_B1 = 0.9
ADAM_B2 = 0.999
ADAM_EPS = 1e-08
ADAM_WD = 0.01
ADAM_STEP = 10

N_CHIPS = 4
N_DEV = 8
VMEM_LIMIT = 56 * 1024 * 1024

NT_DIMS = (((1,), (1,)), ((), ()))
TN_DIMS = (((0,), (0,)), ((), ()))
MESH_ID = pl.DeviceIdType.MESH
HBM_SPEC = pl.BlockSpec(memory_space=pltpu.HBM)
VMEM_SPEC = pl.BlockSpec(memory_space=pltpu.VMEM)


def _tile(n, target, mult=16):
    best = None
    for t in range(mult, min(n, target) + 1, mult):
        if n % t == 0:
            best = t
    return best if best is not None else n


def _cparams(sem):
    return pltpu.CompilerParams(dimension_semantics=sem, vmem_limit_bytes=VMEM_LIMIT)


def _rms_scale(h):
    return lax.rsqrt(jnp.mean(h * h, axis=-1, keepdims=True) + EPS)


def _rms_bwd(dn, h, w):
    r = _rms_scale(h)
    dw = jnp.sum(dn * (h * r), axis=0, keepdims=True)
    z = dn * w
    dh = r * z - h * ((r * r * r) * jnp.mean(z * h, axis=-1, keepdims=True))
    return dh, dw


def _ffn_fwd(h, norm_w, w_in, w_out, exchange=None):
    t, d = h.shape
    f = w_out.shape[0]
    tm = _tile(t, 272)
    tc = _tile(f, 256, 128)
    nj = f // tc
    ni = t // tm
    n_x = len(exchange.ins) if exchange else 0

    def body(*refs):
        h_ref, nw_ref, wi_ref, wo_ref = refs[:4]
        hout_ref, g_ref, u_ref = refs[4 + n_x:7 + n_x]
        a_scr = refs[7 + 2 * n_x]
        i = pl.program_id(0)
        if exchange:
            _host_exchange(exchange, refs[4:4 + n_x], refs[7 + n_x:7 + 2 * n_x], refs[8 + 2 * n_x:],
                           i == 0, i == ni // 3, i == ni - 1, late=i == 2 * ni // 3)
        hh = h_ref[...]
        n = ((hh * _rms_scale(hh)) * nw_ref[...]).astype(BF16)
        for j in range(nj):
            cols = slice(j * tc, (j + 1) * tc)
            g = jnp.dot(n, wi_ref[:, j * tc:(j + 1) * tc], preferred_element_type=F32)
            u = jnp.dot(n, wi_ref[:, f + j * tc:f + (j + 1) * tc], preferred_element_type=F32)
            g_ref[:, cols] = g
            u_ref[:, cols] = u
            a_scr[:, cols] = ((g * jax.nn.sigmoid(g)) * u).astype(BF16)
        hout_ref[...] = hh + 0.5 * jnp.dot(a_scr[...], wo_ref[...], preferred_element_type=F32)

    resident = lambda a: pl.BlockSpec(a.shape, lambda i: (0, 0), pipeline_mode=pl.Buffered(1))
    row = lambda w: pl.BlockSpec((tm, w), lambda i: (i, 0))
    outs = pl.pallas_call(
        body,
        name="ffn_fwd",
        grid=(ni,),
        in_specs=[row(d), pl.BlockSpec((1, d), lambda i: (0, 0)), resident(w_in), resident(w_out)] + [HBM_SPEC] * n_x,
        out_specs=[row(d), row(f), row(f)] + [HBM_SPEC] * n_x,
        out_shape=[
            jax.ShapeDtypeStruct((t, d), F32),
            jax.ShapeDtypeStruct((t, f), F32),
            jax.ShapeDtypeStruct((t, f), F32),
        ] + (exchange.out_shape if exchange else []),
        scratch_shapes=[pltpu.VMEM((tm, f), BF16)] + (exchange.scratch if exchange else []),
        compiler_params=_cparams(("arbitrary",) if exchange else ("parallel",)),
    )(h, norm_w, w_in, w_out, *(exchange.ins if exchange else []))
    return outs[:3], outs[3:]


def _ffn_bwd(dh_out, h, norm_w, g, u, w_in, w_out, exchange=None):
    t, d = h.shape
    f = w_out.shape[0]
    tm = _tile(t, 272)
    tc = _tile(f, 256, 128)
    nj = f // tc
    ni = t // tm
    n_x = len(exchange.ins) if exchange else 0

    def body(*refs):
        dho_ref, h_ref, nw_ref, g_ref, u_ref, wi_ref, wo_ref = refs[:7]
        dhin_ref, n_ref, a_ref, dgu_ref, df_ref, dnw_ref = refs[7 + n_x:13 + n_x]
        i = pl.program_id(0)
        if exchange:
            _host_exchange(exchange, refs[7:7 + n_x], refs[13 + n_x:13 + 2 * n_x], refs[13 + 2 * n_x:],
                           i == 0, i == ni - 1, i == ni - 1)
        hh = h_ref[...]
        nw = nw_ref[...]
        n_ref[...] = ((hh * _rms_scale(hh)) * nw).astype(BF16)
        dho = dho_ref[...]
        df = (0.5 * dho).astype(BF16)
        df_ref[...] = df
        for j in range(nj):
            cols = slice(j * tc, (j + 1) * tc)
            da = lax.dot_general(df, wo_ref[cols, :], NT_DIMS, preferred_element_type=F32)
            gg = g_ref[:, cols]
            uu = u_ref[:, cols]
            sig = jax.nn.sigmoid(gg)
            sl = gg * sig
            a_ref[:, cols] = (sl * uu).astype(BF16)
            dgu_ref[0, :, cols] = ((da * uu) * (sig * (1.0 + gg * (1.0 - sig)))).astype(BF16)
            dgu_ref[1, :, cols] = (da * sl).astype(BF16)
        dn = (lax.dot_general(dgu_ref[0], wi_ref[:, :f], NT_DIMS, preferred_element_type=F32)
              + lax.dot_general(dgu_ref[1], wi_ref[:, f:], NT_DIMS, preferred_element_type=F32))
        dh, dw = _rms_bwd(dn, hh, nw)
        dhin_ref[...] = dho + dh
        dnw_ref[0] = dw

    resident = lambda a: pl.BlockSpec(a.shape, lambda i: (0, 0), pipeline_mode=pl.Buffered(1))
    row = lambda w: pl.BlockSpec((tm, w), lambda i: (i, 0))
    outs = pl.pallas_call(
        body,
        name="ffn_bwd",
        grid=(ni,),
        in_specs=[row(d), row(d), pl.BlockSpec((1, d), lambda i: (0, 0)), row(f), row(f),
                  resident(w_in), resident(w_out)] + [HBM_SPEC] * n_x,
        out_specs=[row(d), row(d), row(f), pl.BlockSpec((2, tm, f), lambda i: (0, i, 0)), row(d),
                   pl.BlockSpec((1, 1, d), lambda i: (i, 0, 0))] + [HBM_SPEC] * n_x,
        out_shape=[
            jax.ShapeDtypeStruct((t, d), F32),
            jax.ShapeDtypeStruct((t, d), BF16),
            jax.ShapeDtypeStruct((t, f), BF16),
            jax.ShapeDtypeStruct((2, t, f), BF16),
            jax.ShapeDtypeStruct((t, d), BF16),
            jax.ShapeDtypeStruct((ni, 1, d), F32),
        ] + (exchange.out_shape if exchange else []),
        scratch_shapes=exchange.scratch if exchange else [],
        compiler_params=_cparams(("arbitrary",) if exchange else ("parallel",)),
    )(dh_out, h, norm_w, g, u, w_in, w_out, *(exchange.ins if exchange else []))
    return outs[:6], outs[6:]


def _tn_matmul(a, b, name, exchange=None):
    t, k = a.shape
    split = b.ndim == 3
    n = 2 * b.shape[2] if split else b.shape[1]
    tk = _tile(k, 512, 128)
    tn = _tile(b.shape[-1], 1408, 128)
    per_half = b.shape[-1] // tn
    ni, nj = k // tk, n // tn
    n_x = len(exchange.ins) if exchange else 0

    def body(*refs):
        a_ref, b_ref, o_ref = refs[0], refs[1], refs[2 + n_x]
        if exchange:
            i, j = pl.program_id(0), pl.program_id(1)
            at_end = (i == ni - 1) & (j == nj - 1)
            _host_exchange(exchange, refs[2:2 + n_x], refs[3 + n_x:3 + 2 * n_x], refs[3 + 2 * n_x:],
                           (i == 0) & (j == 0), at_end, at_end)
        o_ref[...] = lax.dot_general(a_ref[...], b_ref[...], TN_DIMS, preferred_element_type=F32)

    if split:
        b_spec = pl.BlockSpec((None, t, tn), lambda i, j: (j // per_half, 0, j % per_half))
    else:
        b_spec = pl.BlockSpec((t, tn), lambda i, j: (0, j))
    outs = pl.pallas_call(
        body,
        name=name,
        grid=(ni, nj),
        in_specs=[pl.BlockSpec((t, tk), lambda i, j: (0, i)), b_spec] + [HBM_SPEC] * n_x,
        out_specs=[pl.BlockSpec((tk, tn), lambda i, j: (i, j))] + [HBM_SPEC] * n_x,
        out_shape=[jax.ShapeDtypeStruct((k, n), F32)] + (exchange.out_shape if exchange else []),
        scratch_shapes=exchange.scratch if exchange else [],
        compiler_params=_cparams(("arbitrary", "arbitrary") if exchange else ("parallel", "parallel")),
    )(a, b, *(exchange.ins if exchange else []))
    return (outs[0], outs[1:]) if exchange else outs[0]


A_SLOT = lambda h: h // A_GROUP
B_SLOT = lambda h: h % 2

PROJ_PARTS = (
    (OFF_QA, A_WIDTH, A_PAD_WIDTH, True, A_SLOT), (OFF_KA, A_KV_WIDTH, A_KV_WIDTH, True, None),
    (OFF_VA, A_KV_WIDTH, A_KV_WIDTH, True, None), (OFF_QB, B_WIDTH, B_WIDTH, True, None),
    (OFF_KB, B_WIDTH, B_PAD_WIDTH, True, B_SLOT), (OFF_VB, B_WIDTH, B_PAD_WIDTH, True, B_SLOT),
    (OFF_GA, D_MODEL, D_MODEL, False, None), (OFF_GB, D_MODEL, D_MODEL, False, None), (OFF_F, F_COLS, F_COLS, False, None),
)


def _head_tile(pair, head, slot):
    lane_slot = lax.broadcasted_iota(jnp.int32, pair.shape, 1) // HEAD_DIM
    moved = pair if head % 2 == slot else pltpu.roll(pair, HEAD_DIM, 1)
    return jnp.where(lane_slot == slot, moved, 0.0)


def _proj_fwd(h, norm_w, w_p):
    t, d = h.shape
    tm = _tile(t, 272)

    def body(h_ref, nw_ref, w_ref, u_ref, *part_refs):
        hh = h_ref[...]
        un = ((hh * _rms_scale(hh)) * nw_ref[...]).astype(BF16)
        u_ref[...] = un
        for (off, width, _, _, slot), p_ref in zip(PROJ_PARTS, part_refs):
            if slot is None:
                p_ref[...] = jnp.dot(un, w_ref[:, off:off + width], preferred_element_type=F32).astype(p_ref.dtype)
                continue
            part = jnp.dot(un, w_ref[:, off:off + width], preferred_element_type=F32)
            for pair in range(width // LANES):
                x = part[:, pair * LANES:(pair + 1) * LANES]
                for head in (2 * pair, 2 * pair + 1):
                    p_ref[:, head * LANES:(head + 1) * LANES] = _head_tile(x, head, slot(head)).astype(p_ref.dtype)

    row = lambda w: pl.BlockSpec((tm, w), lambda i: (i, 0))
    return pl.pallas_call(
        body,
        name="proj_fwd",
        grid=(t // tm,),
        in_specs=[row(d), pl.BlockSpec((1, d), lambda i: (0, 0)),
                  pl.BlockSpec(w_p.shape, lambda i: (0, 0), pipeline_mode=pl.Buffered(1))],
        out_specs=[row(d)] + [row(width) for _, _, width, _, _ in PROJ_PARTS],
        out_shape=[jax.ShapeDtypeStruct((t, d), BF16)]
        + [jax.ShapeDtypeStruct((t, width), BF16 if is_bf else F32) for _, _, width, is_bf, _ in PROJ_PARTS],
        compiler_params=_cparams(("parallel",)),
    )(h, norm_w, w_p)


def _proj_bwd(dh_out, h, norm_w, dproj, w_p, exchange=None):
    t, d = h.shape
    tm = _tile(t, 272)
    ni = t // tm
    n_p = len(P_PIECES)
    n_in = 4 + n_p
    n_x = len(exchange.ins) if exchange else 0

    def body(*refs):
        dho_ref, h_ref, nw_ref = refs[:3]
        dp_refs, w_ref = refs[3:3 + n_p], refs[3 + n_p]
        dhin_ref, dnw_ref = refs[n_in + n_x:n_in + 2 + n_x]
        if exchange:
            i = pl.program_id(0)
            _host_exchange(exchange, refs[n_in:n_in + n_x], refs[n_in + 2 + n_x:n_in + 2 + 2 * n_x],
                           refs[n_in + 2 + 2 * n_x:], i == 0, i == ni - 1, i == ni - 1)
        dn = None
        for dp_ref, (off, width) in zip(dp_refs, P_PIECES):
            part = lax.dot_general(dp_ref[...], w_ref[:, off:off + width], NT_DIMS, preferred_element_type=F32)
            dn = part if dn is None else dn + part
        dh, dw = _rms_bwd(dn, h_ref[...], nw_ref[...])
        dhin_ref[...] = dho_ref[...] + dh
        dnw_ref[0] = dw

    row = lambda w: pl.BlockSpec((tm, w), lambda i: (i, 0))
    outs = pl.pallas_call(
        body,
        name="proj_bwd",
        grid=(ni,),
        in_specs=[row(d), row(d), pl.BlockSpec((1, d), lambda i: (0, 0))] + [row(width) for _, width in P_PIECES]
        + [pl.BlockSpec(w_p.shape, lambda i: (0, 0), pipeline_mode=pl.Buffered(1))] + [HBM_SPEC] * n_x,
        out_specs=[row(d), pl.BlockSpec((1, 1, d), lambda i: (i, 0, 0))] + [HBM_SPEC] * n_x,
        out_shape=[jax.ShapeDtypeStruct((t, d), F32), jax.ShapeDtypeStruct((ni, 1, d), F32)]
        + (exchange.out_shape if exchange else []),
        scratch_shapes=exchange.scratch if exchange else [],
        compiler_params=_cparams(("arbitrary",) if exchange else ("parallel",)),
    )(dh_out, h, norm_w, *dproj, w_p, *(exchange.ins if exchange else []))
    return outs[:2], outs[2:]


def _merge_fwd(h, oa, ob, ga, gb, wa, wb, wo):
    t, d = h.shape
    tm = _tile(t, 544)

    def body(h_ref, oa_ref, ob_ref, ga_ref, gb_ref, wa_ref, wb_ref, wo_ref, hout_ref):
        ya = jnp.dot(oa_ref[...], wa_ref[...], preferred_element_type=F32)
        yb = jnp.dot(ob_ref[...], wb_ref[...], preferred_element_type=F32)
        mixed = (jax.nn.sigmoid(ga_ref[...]) * ya + jax.nn.sigmoid(gb_ref[...]) * yb).astype(BF16)
        hout_ref[...] = h_ref[...] + jnp.dot(mixed, wo_ref[...], preferred_element_type=F32)

    row = lambda w: pl.BlockSpec((tm, w), lambda i: (i, 0))
    full = lambda a: pl.BlockSpec(a.shape, lambda i: (0, 0))
    return pl.pallas_call(
        body,
        name="merge_fwd",
        grid=(t // tm,),
        in_specs=[row(d), row(oa.shape[1]), row(ob.shape[1]), row(d), row(d), full(wa), full(wb), full(wo)],
        out_specs=row(d),
        out_shape=jax.ShapeDtypeStruct((t, d), F32),
        compiler_params=_cparams(("parallel",)),
    )(h, oa, ob, ga, gb, wa, wb, wo)


def _merge_bwd(dh, oa, ob, ga, gb, wa, wb, wo, exchange=None):
    t, d = dh.shape
    tm = _tile(t, 544)
    ni = t // tm
    n_x = len(exchange.ins) if exchange else 0
    n_s = 2 if exchange else 0

    def body(*refs):
        dh_ref, oa_ref, ob_ref, ga_ref, gb_ref, wa_ref, wb_ref, wo_ref = refs[:8]
        doa_ref, dob_ref, dg_ref = refs[8 + n_x:11 + n_x]
        grad_refs = refs[11 + n_x:14 + n_x]
        acc_refs = refs[14 + 2 * n_x + n_s:17 + 2 * n_x + n_s]
        sem = refs[17 + 2 * n_x + n_s]
        i = pl.program_id(0)
        if exchange:
            _host_exchange(exchange, refs[8:8 + n_x], refs[14 + n_x:14 + 2 * n_x],
                           refs[14 + 2 * n_x:14 + 2 * n_x + n_s], i == 0, i == ni - 1, i == ni - 1)

        @pl.when(i == 0)
        def _():
            for acc in acc_refs:
                acc[...] = jnp.zeros_like(acc)

        dhb = dh_ref[...].astype(BF16)
        dmix = lax.dot_general(dhb, wo_ref[...], NT_DIMS, preferred_element_type=F32)
        mixed = None
        for branch, (o_ref, g_ref, w_ref, do_ref, acc) in enumerate((
                (oa_ref, ga_ref, wa_ref, doa_ref, acc_refs[0]),
                (ob_ref, gb_ref, wb_ref, dob_ref, acc_refs[1]))):
            o = o_ref[...]
            y = jnp.dot(o, w_ref[...], preferred_element_type=F32)
            s = jax.nn.sigmoid(g_ref[...])
            mixed = s * y if mixed is None else mixed + s * y
            dy = (dmix * s).astype(BF16)
            dg_ref[:, branch * d:(branch + 1) * d] = ((dmix * y) * (s * (1.0 - s))).astype(BF16)
            do_ref[...] = lax.dot_general(dy, w_ref[...], NT_DIMS, preferred_element_type=F32).astype(BF16)
            acc[...] += lax.dot_general(o, dy, TN_DIMS, preferred_element_type=F32)
        acc_refs[2][...] += lax.dot_general(mixed.astype(BF16), dhb, TN_DIMS, preferred_element_type=F32)

        @pl.when(i == ni - 1)
        def _():
            copies = [pltpu.make_async_copy(acc, out, sem.at[q]) for q, (acc, out) in enumerate(zip(acc_refs, grad_refs))]
            for cp in copies:
                cp.start()
            for cp in copies:
                cp.wait()

    row = lambda w: pl.BlockSpec((tm, w), lambda i: (i, 0))
    full = lambda a: pl.BlockSpec(a.shape, lambda i: (0, 0))
    wa_w, wb_w = oa.shape[1], ob.shape[1]
    grad_shapes = [(wa_w, d), (wb_w, d), (d, d)]
    outs = pl.pallas_call(
        body,
        name="merge_bwd",
        grid=(ni,),
        in_specs=[row(d), row(wa_w), row(wb_w), row(d), row(d), full(wa), full(wb), full(wo)] + [HBM_SPEC] * n_x,
        out_specs=[row(wa_w), row(wb_w), row(2 * d)] + [HBM_SPEC] * (3 + n_x),
        out_shape=[
            jax.ShapeDtypeStruct((t, wa_w), BF16), jax.ShapeDtypeStruct((t, wb_w), BF16),
            jax.ShapeDtypeStruct((t, 2 * d), BF16),
        ] + [jax.ShapeDtypeStruct(s, F32) for s in grad_shapes] + (exchange.out_shape if exchange else []),
        scratch_shapes=(exchange.scratch if exchange else []) + [pltpu.VMEM(s, F32) for s in grad_shapes]
        + [pltpu.SemaphoreType.DMA((3,))],
        compiler_params=_cparams(("arbitrary",)),
    )(dh, oa, ob, ga, gb, wa, wb, wo, *(exchange.ins if exchange else []))
    return outs[:6], outs[6:]


def _tri_dot(tri, x):
    hi = x.astype(BF16)
    r1 = x - hi.astype(F32)
    mid = r1.astype(BF16)
    lo = (r1 - mid.astype(F32)).astype(BF16)
    return (jnp.dot(tri, hi, preferred_element_type=F32)
            + jnp.dot(tri, mid, preferred_element_type=F32)
            + jnp.dot(tri, lo, preferred_element_type=F32))


def _forget_cumsum(f_logit, b_pad, nb):
    t, w = f_logit.shape
    bsz = t // (nb * BLOCK)

    def body(f_ref, b_ref, c_ref, carry):
        @pl.when(pl.program_id(0) == 0)
        def _():
            carry[...] = jnp.zeros_like(carry)

        rows = lax.broadcasted_iota(jnp.int32, (BLOCK, BLOCK), 0)
        cols = lax.broadcasted_iota(jnp.int32, (BLOCK, BLOCK), 1)
        tri = (cols <= rows).astype(BF16)
        for b in range(bsz):
            x = jax.nn.log_sigmoid(f_ref[b] + b_ref[...])
            c = _tri_dot(tri, x) + carry[b]
            c_ref[b] = c
            carry[b] = c[BLOCK - 1:BLOCK, :]

    block = pl.BlockSpec((bsz, BLOCK, w), lambda n: (0, n, 0))
    return pl.pallas_call(
        body,
        name="forget_cumsum",
        grid=(nb,),
        in_specs=[block, pl.BlockSpec((1, w), lambda n: (0, 0))],
        out_specs=block,
        out_shape=jax.ShapeDtypeStruct((bsz, nb * BLOCK, w), F32),
        scratch_shapes=[pltpu.VMEM((bsz, 1, w), F32)],
        compiler_params=_cparams(("arbitrary",)),
    )(f_logit.reshape(bsz, nb * BLOCK, w), b_pad).reshape(t, w)


def _forget_cumsum_bwd(dc, f_logit, b_pad, nb):
    t, w = f_logit.shape
    bsz = t // (nb * BLOCK)

    def body(dc_ref, f_ref, b_ref, df_ref, db_ref, carry):
        @pl.when(pl.program_id(0) == 0)
        def _():
            carry[...] = jnp.zeros_like(carry)
            db_ref[...] = jnp.zeros_like(db_ref)

        rows = lax.broadcasted_iota(jnp.int32, (BLOCK, BLOCK), 0)
        cols = lax.broadcasted_iota(jnp.int32, (BLOCK, BLOCK), 1)
        tri = (cols >= rows).astype(BF16)
        for b in range(bsz):
            dlf = _tri_dot(tri, dc_ref[b]) + carry[b]
            carry[b] = dlf[0:1, :]
            df = dlf * jax.nn.sigmoid(-(f_ref[b] + b_ref[...]))
            df_ref[b] = df.astype(BF16)
            db_ref[b] += jnp.sum(df, axis=0, keepdims=True)

    l = nb * BLOCK
    rev = pl.BlockSpec((bsz, BLOCK, w), lambda n: (0, nb - 1 - n, 0))
    df, db = pl.pallas_call(
        body,
        name="forget_cumsum_bwd",
        grid=(nb,),
        in_specs=[rev, rev, pl.BlockSpec((1, w), lambda n: (0, 0))],
        out_specs=[rev, pl.BlockSpec((bsz, 1, w), lambda n: (0, 0, 0))],
        out_shape=[jax.ShapeDtypeStruct((bsz, l, w), BF16), jax.ShapeDtypeStruct((bsz, 1, w), F32)],
        scratch_shapes=[pltpu.VMEM((bsz, 1, w), F32)],
        compiler_params=_cparams(("arbitrary",)),
    )(dc.reshape(bsz, l, w), f_logit.reshape(bsz, l, w), b_pad)
    return df.reshape(t, w), db


GROUP_ROWS = A_GROUP * BLOCK


def _stack_heads(ref, g):
    return jnp.concatenate([ref[:, (A_GROUP * g + i) * LANES:(A_GROUP * g + i + 1) * LANES] for i in range(A_GROUP)],
                           axis=0)


def _unstack_heads(ref, g, x):
    for i in range(A_GROUP):
        ref[:, (A_GROUP * g + i) * LANES:(A_GROUP * g + i + 1) * LANES] = x[i * BLOCK:(i + 1) * BLOCK].astype(ref.dtype)


def _swa_logits(qk, slope, n):
    qi = lax.broadcasted_iota(jnp.int32, (GROUP_ROWS, BLOCK), 0) & (BLOCK - 1)
    kj = lax.broadcasted_iota(jnp.int32, (GROUP_ROWS, BLOCK), 1)
    s_all = qk * SCALE
    out = []
    for i, (dist, ok) in enumerate((
            (n * BLOCK + qi - kj, (kj >= N_PAD) & (n * BLOCK + qi - kj >= 0)),
            (BLOCK + qi - kj, (kj > qi) & (n >= 2)),
            (qi - kj, (kj <= qi) & (n >= 1)))):
        s = s_all[:, i * BLOCK:(i + 1) * BLOCK] - slope * dist.astype(F32)
        out.append(jnp.where(ok, s, NEG))
    return out


def _three_blocks(m_ref, p_ref, c_ref):
    return jnp.concatenate([m_ref[...], p_ref[...], c_ref[...]], axis=0)


def _swa_specs(bsz, nb):
    qspec = pl.BlockSpec((bsz, BLOCK, A_PAD_WIDTH), lambda n: (0, n, 0))
    kv_m = pl.BlockSpec((bsz, BLOCK, LANES), lambda n: (0, 0, 0))
    kv_p = pl.BlockSpec((bsz, BLOCK, LANES), lambda n: (0, jnp.maximum(n - 1, 0), 0))
    kv_c = pl.BlockSpec((bsz, BLOCK, LANES), lambda n: (0, n, 0))
    rowspec = pl.BlockSpec((A_KV_HEADS, GROUP_ROWS, 1), lambda n: (0, 0, 0))
    lsespec = pl.BlockSpec((bsz, 1, A_KV_HEADS, GROUP_ROWS, 1), lambda n: (0, n, 0, 0, 0))
    return qspec, kv_m, kv_p, kv_c, rowspec, lsespec


def _swa_fwd(q, k, v, sink_rows, slope_rows, nb):
    t = q.shape[0]
    l = nb * BLOCK
    bsz = t // l

    def body(q_ref, km_ref, kp_ref, kc_ref, vm_ref, vp_ref, vc_ref, sink_ref, slope_ref, o_ref, lse_ref):
        n = pl.program_id(0)
        lane_group = lax.broadcasted_iota(jnp.int32, (GROUP_ROWS, LANES), 1) // HEAD_DIM
        products = {}
        for b in range(bsz):
            keys = _three_blocks(km_ref.at[b], kp_ref.at[b], kc_ref.at[b])
            for g in range(A_KV_HEADS):
                products[b, g] = lax.dot_general(_stack_heads(q_ref.at[b], g), keys, NT_DIMS,
                                                 preferred_element_type=F32)
        for b in range(bsz):
            values = _three_blocks(vm_ref.at[b], vp_ref.at[b], vc_ref.at[b])
            for g in range(A_KV_HEADS):
                sink = sink_ref[g]
                s_m, s_p, s_c = _swa_logits(products[b, g], slope_ref[g], n)
                m = jnp.maximum(jnp.max(jnp.maximum(jnp.maximum(s_m, s_p), s_c), axis=-1, keepdims=True), sink)
                m_wide = jnp.broadcast_to(m, (GROUP_ROWS, BLOCK))
                e_m = jnp.exp(s_m - m_wide)
                e_p = jnp.exp(s_p - m_wide)
                e_c = jnp.exp(s_c - m_wide)
                z = jnp.sum((e_m + e_p) + e_c, axis=-1, keepdims=True) + jnp.exp(sink - m)
                inv = jnp.broadcast_to(1.0 / z, (GROUP_ROWS, BLOCK))
                probs = jnp.concatenate([(e_m * inv).astype(BF16), (e_p * inv).astype(BF16),
                                         (e_c * inv).astype(BF16)], axis=1)
                o = jnp.dot(probs, values, preferred_element_type=F32)
                _unstack_heads(o_ref.at[b], g, jnp.where(lane_group == g, o, 0.0))
                lse_ref[b, 0, g] = m + jnp.log(z)

    qspec, kv_m, kv_p, kv_c, rowspec, lsespec = _swa_specs(bsz, nb)
    by_example = lambda a: a.reshape(bsz, l, a.shape[1])
    q3, k3, v3 = by_example(q), by_example(k), by_example(v)
    o, lse = pl.pallas_call(
        body,
        name="swa_fwd",
        grid=(nb,),
        in_specs=[qspec, kv_m, kv_p, kv_c, kv_m, kv_p, kv_c, rowspec, rowspec],
        out_specs=[qspec, lsespec],
        out_shape=[jax.ShapeDtypeStruct((bsz, l, A_PAD_WIDTH), BF16),
                   jax.ShapeDtypeStruct((bsz, nb, A_KV_HEADS, GROUP_ROWS, 1), F32)],
        compiler_params=_cparams(("arbitrary",)),
    )(q3, k3, k3, k3, v3, v3, v3, sink_rows, slope_rows)
    return o.reshape(t, A_PAD_WIDTH), lse


def _swa_bwd(q, k, v, do, lse, sink_rows, slope_rows, nb):
    t = q.shape[0]
    l = nb * BLOCK
    bsz = t // l

    def body(q_ref, km_ref, kp_ref, kc_ref, vm_ref, vp_ref, vc_ref, do_ref, lse_ref, sink_ref, slope_ref,
             dq_ref, dk_ref, dv_ref, dsink_ref, dk_acc, dv_acc):
        n = pl.program_id(0)

        @pl.when(n == 0)
        def _():
            dk_acc[...] = jnp.zeros_like(dk_acc)
            dv_acc[...] = jnp.zeros_like(dv_acc)
            dsink_ref[...] = jnp.zeros_like(dsink_ref)

        first_half = lax.broadcasted_iota(jnp.int32, (BLOCK, LANES), 1) < HEAD_DIM
        prev = jnp.maximum(n - 1, 0)
        products = {}
        for b in range(bsz):
            keys = _three_blocks(km_ref.at[b], kp_ref.at[b], kc_ref.at[b])
            values = _three_blocks(vm_ref.at[b], vp_ref.at[b], vc_ref.at[b])
            for g in range(A_KV_HEADS):
                products[b, g] = (
                    lax.dot_general(_stack_heads(q_ref.at[b], g), keys, NT_DIMS, preferred_element_type=F32),
                    lax.dot_general(_stack_heads(do_ref.at[b], g), values, NT_DIMS, preferred_element_type=F32))
        for b in range(bsz):
            keys = _three_blocks(km_ref.at[b], kp_ref.at[b], kc_ref.at[b])
            for g in range(A_KV_HEADS):
                qq = _stack_heads(q_ref.at[b], g)
                dob = _stack_heads(do_ref.at[b], g)
                lse_g = lse_ref[b, 0, g]
                lse_wide = jnp.broadcast_to(lse_g, (GROUP_ROWS, BLOCK))
                qk, dp_all = products[b, g]
                probs = [jnp.exp(s - lse_wide) for s in _swa_logits(qk, slope_ref[g], n)]
                dps = [dp_all[:, i * BLOCK:(i + 1) * BLOCK] for i in range(3)]
                delta = jnp.sum((probs[0] * dps[0] + probs[1] * dps[1]) + probs[2] * dps[2], axis=-1, keepdims=True)
                delta_wide = jnp.broadcast_to(delta, (GROUP_ROWS, BLOCK))
                ds = jnp.concatenate([(p * (dp - delta_wide)).astype(BF16) for p, dp in zip(probs, dps)], axis=1)
                pb = jnp.concatenate([p.astype(BF16) for p in probs], axis=1)
                dq = jnp.dot(ds, keys, preferred_element_type=F32) * SCALE
                dk_all = lax.dot_general(ds, qq, TN_DIMS, preferred_element_type=F32) * SCALE
                dv_all = lax.dot_general(pb, dob, TN_DIMS, preferred_element_type=F32)
                for i, start in enumerate((0, prev * BLOCK, n * BLOCK)):
                    rows = pl.ds(pl.multiple_of(start, BLOCK), BLOCK)
                    dk_acc[b, rows, :] += dk_all[i * BLOCK:(i + 1) * BLOCK]
                    dv_acc[b, rows, :] += dv_all[i * BLOCK:(i + 1) * BLOCK]
                for pair in range(A_GROUP // 2):
                    even = dq[2 * pair * BLOCK:(2 * pair + 1) * BLOCK]
                    odd = dq[(2 * pair + 1) * BLOCK:(2 * pair + 2) * BLOCK]
                    left = even if g == 0 else pltpu.roll(even, HEAD_DIM, 1)
                    right = pltpu.roll(odd, HEAD_DIM, 1) if g == 0 else odd
                    tile = (A_GROUP // 2) * g + pair
                    dq_ref[b, :, tile * LANES:(tile + 1) * LANES] = jnp.where(first_half, left, right).astype(BF16)
                dsink_ref[b, g] += -(jnp.exp(sink_ref[g] - lse_g) * delta)

        @pl.when(n == nb - 1)
        def _():
            dk_ref[...] = dk_acc[...].astype(BF16)
            dv_ref[...] = dv_acc[...].astype(BF16)

    qspec, kv_m, kv_p, kv_c, rowspec, lsespec = _swa_specs(bsz, nb)
    kv_all = pl.BlockSpec((bsz, l, LANES), lambda n: (0, 0, 0))
    by_example = lambda a: a.reshape(bsz, l, a.shape[1])
    q3, k3, v3 = by_example(q), by_example(k), by_example(v)
    dq, dk, dv, dsink = pl.pallas_call(
        body,
        name="swa_bwd",
        grid=(nb,),
        in_specs=[qspec, kv_m, kv_p, kv_c, kv_m, kv_p, kv_c, qspec, lsespec, rowspec, rowspec],
        out_specs=[pl.BlockSpec((bsz, BLOCK, A_WIDTH), lambda n: (0, n, 0)), kv_all, kv_all,
                   pl.BlockSpec((bsz, A_KV_HEADS, GROUP_ROWS, 1), lambda n: (0, 0, 0, 0))],
        out_shape=[jax.ShapeDtypeStruct((bsz, l, A_WIDTH), BF16),
                   jax.ShapeDtypeStruct((bsz, l, LANES), BF16),
                   jax.ShapeDtypeStruct((bsz, l, LANES), BF16),
                   jax.ShapeDtypeStruct((bsz, A_KV_HEADS, GROUP_ROWS, 1), F32)],
        scratch_shapes=[pltpu.VMEM((bsz, l, LANES), F32), pltpu.VMEM((bsz, l, LANES), F32)],
        compiler_params=_cparams(("arbitrary",)),
    )(q3, k3, k3, k3, v3, v3, v3, by_example(do), lse, sink_rows, slope_rows)
    return dq.reshape(t, A_WIDTH), dk.reshape(t, LANES), dv.reshape(t, LANES), dsink


CHUNK = KEY_BLOCKS * BLOCK


def _fox_chunk(qb, ci):
    sb = jnp.maximum(jnp.minimum(KEY_BLOCKS * ci, qb + 1 - KEY_BLOCKS), 0)
    lo = jnp.maximum(ci * CHUNK, N_PAD)
    return sb, lo, pl.ds(pl.multiple_of(sb * BLOCK, BLOCK), CHUNK)


def _fox_logits(s_ref, cr_ref, e, j, sb, lo, qb):
    lane = lax.broadcasted_iota(jnp.int32, (BLOCK, BLOCK), 1)
    ahead = lane - lax.broadcasted_iota(jnp.int32, (BLOCK, BLOCK), 0)
    first = (sb + j) * BLOCK
    s = s_ref[e, :, j * BLOCK:(j + 1) * BLOCK] - cr_ref[e, sb + j]
    return jnp.where((ahead <= qb * BLOCK - first) & (lane >= lo - first), s, NEG)


FOX_PAIRS = 4
FOX_HEADS = 2 * FOX_PAIRS
FOX_STEPS = B_PAIRS // FOX_PAIRS


def _fox_specs(nb):
    l = nb * BLOCK
    q_spec = pl.BlockSpec((BLOCK, FOX_PAIRS * LANES), lambda b, p, i: (b * nb + i, p))
    kv_spec = pl.BlockSpec((l, FOX_HEADS * LANES), lambda b, p, i: (b, p))
    cc_spec = pl.BlockSpec((FOX_HEADS, BLOCK, 1), lambda b, p, i: (b * FOX_STEPS + p, i, 0))
    cr_spec = pl.BlockSpec((FOX_HEADS, nb, 1, BLOCK), lambda b, p, i: (b * FOX_STEPS + p, 0, 0, 0))
    return q_spec, kv_spec, cc_spec, cr_spec


def _fox_fwd(q, k, v, c_row, nb, exchange=None):
    t = q.shape[0]
    bsz = t // (nb * BLOCK)
    assert nb >= KEY_BLOCKS

    n_x = len(exchange.ins) if exchange else 0

    def body(*refs):
        q_ref, k_ref, v_ref, cr_ref = refs[:4]
        o_ref, ox_ref, lse_ref = refs[4 + n_x:7 + n_x]
        s_scr, hi_scr, lo_scr = refs[7 + 2 * n_x:10 + 2 * n_x]
        qb = pl.program_id(2)
        if exchange:
            first = (pl.program_id(0) == 0) & (pl.program_id(1) == 0)
            last = (pl.program_id(0) == bsz - 1) & (pl.program_id(1) == FOX_STEPS - 1)
            _host_exchange(exchange, refs[4:4 + n_x], refs[7 + n_x:7 + 2 * n_x], refs[10 + 2 * n_x:],
                           first & (qb == 0), first & (qb == 2 * nb // 3), last & (qb == nb - 1),
                           late=last & (qb == 0))
        qs = [q_ref[:, a * LANES:(a + 1) * LANES] * SCALE for a in range(FOX_PAIRS)]
        first_half = lax.broadcasted_iota(jnp.int32, (BLOCK, LANES), 1) < HEAD_DIM

        def step(ci, carry):
            stats, accs = carry[:2 * FOX_HEADS], carry[2 * FOX_HEADS:]
            sb, lo, krows = _fox_chunk(qb, ci)
            for e in range(FOX_HEADS):
                s_scr[e] = lax.dot_general(qs[e // 2], k_ref[krows, e * LANES:(e + 1) * LANES], NT_DIMS,
                                           preferred_element_type=F32)
            new_stats, new_accs = [], []
            for a in range(FOX_PAIRS):
                alphas = []
                pv = jnp.zeros((BLOCK, LANES), F32)
                pv_lo = jnp.zeros((BLOCK, LANES), F32)
                for e in (2 * a, 2 * a + 1):
                    m, z = stats[2 * e], stats[2 * e + 1]
                    tile = slice(e * LANES, (e + 1) * LANES)
                    top = None
                    for j in range(KEY_BLOCKS):
                        s = _fox_logits(s_scr, cr_ref, e, j, sb, lo, qb)
                        s_scr[e, :, j * BLOCK:(j + 1) * BLOCK] = s
                        top = s if top is None else jnp.maximum(top, s)
                    m_new = jnp.maximum(m, jnp.max(top, axis=-1, keepdims=True))
                    alpha = jnp.exp(m - m_new)
                    m_wide = jnp.broadcast_to(m_new, (BLOCK, BLOCK))
                    total = None
                    for j in range(KEY_BLOCKS):
                        cols = slice(j * BLOCK, (j + 1) * BLOCK)
                        p = jnp.exp(s_scr[e, :, cols] - m_wide)
                        total = p if total is None else total + p
                        hi = p.astype(BF16)
                        hi_scr[e, :, cols] = hi
                        lo_scr[e, :, cols] = (p - hi.astype(F32)).astype(BF16)
                    z = alpha * z + jnp.sum(total, axis=-1, keepdims=True)
                    vv = v_ref[krows, tile]
                    pv = pv + jnp.dot(hi_scr[e], vv, preferred_element_type=F32)
                    pv_lo = pv_lo + jnp.dot(lo_scr[e], vv, preferred_element_type=F32)
                    new_stats += [m_new, z]
                    alphas.append(alpha)
                alpha = jnp.where(first_half, alphas[0], alphas[1])
                new_accs += [alpha * accs[2 * a] + pv, alpha * accs[2 * a + 1] + pv_lo]
            return (*new_stats, *new_accs)

        col = lambda val: jnp.full((BLOCK, 1), val, F32)
        done = lax.fori_loop(
            0, (qb + KEY_BLOCKS) // KEY_BLOCKS, step,
            (col(NEG), col(0.0)) * FOX_HEADS + (jnp.zeros((BLOCK, LANES), F32),) * (2 * FOX_PAIRS))
        for a in range(FOX_PAIRS):
            m0, z0, m1, z1 = done[4 * a:4 * a + 4]
            acc, acc_lo = done[2 * FOX_HEADS + 2 * a:2 * FOX_HEADS + 2 * a + 2]
            inv = 1.0 / jnp.where(first_half, z0, z1)
            tile = slice(a * LANES, (a + 1) * LANES)
            o_ref[:, tile] = (acc * inv).astype(BF16)
            ox_ref[:, tile] = (acc + acc_lo) * inv
            lse_ref[2 * a] = m0 + jnp.log(z0)
            lse_ref[2 * a + 1] = m1 + jnp.log(z1)

    q_spec, kv_spec, cc_spec, cr_spec = _fox_specs(nb)
    outs = pl.pallas_call(
        body,
        name="fox_fwd",
        grid=(bsz, FOX_STEPS, nb),
        in_specs=[q_spec, kv_spec, kv_spec, cr_spec] + [HBM_SPEC] * n_x,
        out_specs=[q_spec, q_spec, cc_spec] + [HBM_SPEC] * n_x,
        out_shape=[jax.ShapeDtypeStruct((t, B_WIDTH), BF16), jax.ShapeDtypeStruct((t, B_WIDTH), F32),
                   jax.ShapeDtypeStruct((bsz * B_HEADS, nb * BLOCK, 1), F32)] + (exchange.out_shape if exchange else []),
        scratch_shapes=[pltpu.VMEM((FOX_HEADS, BLOCK, CHUNK), F32), pltpu.VMEM((FOX_HEADS, BLOCK, CHUNK), BF16),
                        pltpu.VMEM((FOX_HEADS, BLOCK, CHUNK), BF16)] + (exchange.scratch if exchange else []),
        compiler_params=_cparams(("arbitrary",) * 3 if exchange else ("parallel", "parallel", "arbitrary")),
    )(q, k, v, c_row, *(exchange.ins if exchange else []))
    return outs[:3], outs[3:]


def _fox_bwd(q, k, v, o_exact, do, lse, c_row, nb, exchange=None):
    t = q.shape[0]
    l = nb * BLOCK
    bsz = t // l

    n_x = len(exchange.ins) if exchange else 0

    def body(*refs):
        q_ref, k_ref, v_ref, ox_ref, do_ref, lse_ref, cr_ref = refs[:7]
        dq_ref, dk_ref, dv_ref, dc_ref = refs[7 + n_x:11 + n_x]
        dk_acc, dv_acc, s_scr, dp_scr, p_scr, ds_scr, dq_scr = refs[11 + 2 * n_x:18 + 2 * n_x]
        qb = pl.program_id(2)
        if exchange:
            first = (pl.program_id(0) == 0) & (pl.program_id(1) == 0)
            last = (pl.program_id(0) == bsz - 1) & (pl.program_id(1) == FOX_STEPS - 1)
            _host_exchange(exchange, refs[7:7 + n_x], refs[11 + n_x:11 + 2 * n_x], refs[18 + 2 * n_x:],
                           first & (qb == 0), last & (qb == 0), last & (qb == nb - 1))

        @pl.when(qb == 0)
        def _():
            dk_acc[...] = jnp.zeros_like(dk_acc)
            dv_acc[...] = jnp.zeros_like(dv_acc)
            dc_ref[...] = jnp.zeros_like(dc_ref)

        top_half = lax.broadcasted_iota(jnp.int32, (LANES, BLOCK), 0) < HEAD_DIM
        pair_t = lambda x: jnp.concatenate([jnp.where(top_half, x.T, 0), jnp.where(top_half, 0, x.T)], axis=1)
        first_half = lax.broadcasted_iota(jnp.int32, (BLOCK, LANES), 1) < HEAD_DIM
        wide = lambda col: jnp.broadcast_to(col, (BLOCK, BLOCK))
        qs, dobs, qs_t, dob_t, deltas = [], [], [], [], []
        for a in range(FOX_PAIRS):
            tile = slice(a * LANES, (a + 1) * LANES)
            qs.append(q_ref[:, tile] * SCALE)
            dobs.append(do_ref[:, tile])
            qs_t.append(pair_t(qs[a]))
            dob_t.append(pair_t(dobs[a]))
            weighted = dobs[a].astype(F32) * ox_ref[:, tile]
            deltas += [wide(jnp.sum(jnp.where(first_half, weighted, 0.0), axis=-1, keepdims=True)),
                       wide(jnp.sum(jnp.where(first_half, 0.0, weighted), axis=-1, keepdims=True))]
        lses = [wide(lse_ref[e]) for e in range(FOX_HEADS)]

        dq_scr[...] = jnp.zeros(dq_scr.shape, F32)

        def step(ci, carry):
            sb, lo, krows = _fox_chunk(qb, ci)
            for e in range(FOX_HEADS):
                tile = slice(e * LANES, (e + 1) * LANES)
                s_scr[e] = lax.dot_general(qs[e // 2], k_ref[krows, tile], NT_DIMS, preferred_element_type=F32)
                dp_scr[e] = lax.dot_general(dobs[e // 2], v_ref[krows, tile], NT_DIMS, preferred_element_type=F32)
            for a in range(FOX_PAIRS):
                for e in (2 * a, 2 * a + 1):
                    tile = slice(e * LANES, (e + 1) * LANES)
                    kk = k_ref[krows, tile]
                    for j in range(KEY_BLOCKS):
                        cols = slice(j * BLOCK, (j + 1) * BLOCK)
                        p = jnp.exp(_fox_logits(s_scr, cr_ref, e, j, sb, lo, qb) - lses[e])
                        ds = p * (dp_scr[e, :, cols] - deltas[e])
                        dc_ref[e, sb + j] -= jnp.sum(ds, axis=0, keepdims=True)
                        p_scr[e, :, cols] = p.astype(BF16)
                        ds_scr[e, :, cols] = ds.astype(BF16)
                    dq_scr[a] += jnp.dot(ds_scr[e], kk, preferred_element_type=F32)
                both = slice(2 * a, 2 * a + 2)
                dk_t = jnp.dot(qs_t[a], ds_scr[both].reshape(2 * BLOCK, CHUNK), preferred_element_type=F32)
                dv_t = jnp.dot(dob_t[a], p_scr[both].reshape(2 * BLOCK, CHUNK), preferred_element_type=F32)
                for j in range(KEY_BLOCKS):
                    cols = slice(j * BLOCK, (j + 1) * BLOCK)
                    dk_acc[a * nb + sb + j] += dk_t[:, cols]
                    dv_acc[a * nb + sb + j] += dv_t[:, cols]
            return carry

        lax.fori_loop(0, (qb + KEY_BLOCKS) // KEY_BLOCKS, step, 0)
        for a in range(FOX_PAIRS):
            dq_ref[:, a * LANES:(a + 1) * LANES] = (dq_scr[a] * SCALE).astype(BF16)

        @pl.when(qb == nb - 1)
        def _():
            for a in range(FOX_PAIRS):
                for kb in range(nb):
                    rows = slice(kb * BLOCK, (kb + 1) * BLOCK)
                    for acc, out_ref in ((dk_acc, dk_ref), (dv_acc, dv_ref)):
                        out_ref[rows, a * LANES:(a + 1) * LANES] = acc[a * nb + kb].T.astype(BF16)

    q_spec, kv_spec, cc_spec, cr_spec = _fox_specs(nb)
    dkv_spec = pl.BlockSpec((l, FOX_PAIRS * LANES), lambda b, p, i: (b, p))
    outs = pl.pallas_call(
        body,
        name="fox_bwd",
        grid=(bsz, FOX_STEPS, nb),
        in_specs=[q_spec, kv_spec, kv_spec, q_spec, q_spec, cc_spec, cr_spec] + [HBM_SPEC] * n_x,
        out_specs=[q_spec, dkv_spec, dkv_spec, cr_spec] + [HBM_SPEC] * n_x,
        out_shape=[jax.ShapeDtypeStruct((t, B_WIDTH), BF16), jax.ShapeDtypeStruct((t, B_WIDTH), BF16),
                   jax.ShapeDtypeStruct((t, B_WIDTH), BF16),
                   jax.ShapeDtypeStruct((bsz * B_HEADS, nb, 1, BLOCK), F32)] + (exchange.out_shape if exchange else []),
        scratch_shapes=[pltpu.VMEM((FOX_PAIRS * nb, LANES, BLOCK), F32), pltpu.VMEM((FOX_PAIRS * nb, LANES, BLOCK), F32),
                        pltpu.VMEM((FOX_HEADS, BLOCK, CHUNK), F32), pltpu.VMEM((FOX_HEADS, BLOCK, CHUNK), F32),
                        pltpu.VMEM((FOX_HEADS, BLOCK, CHUNK), BF16), pltpu.VMEM((FOX_HEADS, BLOCK, CHUNK), BF16),
                        pltpu.VMEM((FOX_PAIRS, BLOCK, LANES), F32)]
        + (exchange.scratch if exchange else []),
        compiler_params=_cparams(("arbitrary",) * 3 if exchange else ("parallel", "parallel", "arbitrary")),
    )(q, k, v, o_exact, do, lse, c_row, *(exchange.ins if exchange else []))
    return outs[:4], outs[4:]


def _loss_head(h, final_w, target):
    bsz, l, d = h.shape
    nb = l // BLOCK

    def body(h_ref, w_ref, t_ref, loss_ref, dh_ref, dw_ref):
        n = pl.program_id(0)

        @pl.when(n == 0)
        def _():
            loss_ref[...] = jnp.zeros_like(loss_ref)
            dw_ref[...] = jnp.zeros_like(dw_ref)
            dh_ref[...] = jnp.zeros_like(dh_ref)

        @pl.when(n > 0)
        def _():
            w = w_ref[...]
            for b in range(bsz):
                hh = h_ref[b]
                r = _rms_scale(hh)
                err = (hh * r) * w - t_ref[b]
                loss_ref[...] += 0.5 * jnp.sum(jnp.mean(err * err, axis=-1, keepdims=True), axis=0, keepdims=True)
                dy = err * (1.0 / d)
                dh, dw = _rms_bwd(dy, hh, w)
                dh_ref[b] = dh
                dw_ref[...] += dw

    return pl.pallas_call(
        body,
        name="loss_head",
        grid=(nb,),
        in_specs=[
            pl.BlockSpec((bsz, BLOCK, d), lambda n: (0, n, 0)),
            pl.BlockSpec((1, d), lambda n: (0, 0)),
            pl.BlockSpec((bsz, BLOCK, d), lambda n: (0, jnp.maximum(n - 1, 0), 0)),
        ],
        out_specs=[
            pl.BlockSpec((1, 128), lambda n: (0, 0)),
            pl.BlockSpec((bsz, BLOCK, d), lambda n: (0, n, 0)),
            pl.BlockSpec((1, d), lambda n: (0, 0)),
        ],
        out_shape=[jax.ShapeDtypeStruct((1, 128), F32), jax.ShapeDtypeStruct((bsz, l, d), F32),
                   jax.ShapeDtypeStruct((1, d), F32)],
        compiler_params=_cparams(("arbitrary",)),
    )(h, final_w, target)


def _pad_tiles(w, src, heads, lane_slot, axis):
    pieces = []
    for h in range(heads):
        x = lax.slice_in_dim(w, src + HEAD_DIM * h, src + HEAD_DIM * (h + 1), axis=axis)
        z = jnp.zeros_like(x)
        pieces += [x, z] if lane_slot(h) == 0 else [z, x]
    return pieces


def _unpad_tiles(g, off, heads, lane_slot, axis):
    return [lax.slice_in_dim(g, off + LANES * h + HEAD_DIM * lane_slot(h),
                             off + LANES * h + HEAD_DIM * (lane_slot(h) + 1), axis=axis) for h in range(heads)]


REF_RUNS = ((0, SRC_QB, 0, 0), (SRC_QB, SRC_F, 1, 0), (SRC_F, SRC_GA, 0, OFF_F), (SRC_GA, W_IN_COLS, 2, 0))
P_RUNS = ((0, OFF_F, 0), (OFF_F, OFF_F + B_HEADS, SRC_F), (OFF_QB, OFF_GA, SRC_QB), (OFF_GA, P_COLS, SRC_GA))
SHARD_COLS = W_IN_COLS // N_CHIPS
SHARD_PAD_COLS = -(-SHARD_COLS // LANES) * LANES


def _place(tile, lane, src_ref, src, dst, length):
    for t in range(src // LANES, (src + length - 1) // LANES + 1):
        x = src_ref[:, t * LANES:(t + 1) * LANES].astype(F32)
        shift = (dst - src) % LANES
        moved = pltpu.roll(x, shift, 1) if shift else x
        from_t = (lane >= max(dst, dst + t * LANES - src)) & (lane < min(dst + length, dst + (t + 1) * LANES - src))
        tile = jnp.where(from_t, moved, tile)
    return tile


def _layout_w_in(stacked):
    d = stacked.shape[1]
    rows = _tile(d, 256)

    def body(s_ref, o_ref):
        lane = lax.broadcasted_iota(jnp.int32, (rows, LANES), 1)
        for lo in range(0, P_COLS, LANES):
            tile = jnp.zeros((rows, LANES), F32)
            for first, end, ref_col in P_RUNS:
                start, stop = max(lo, first), min(lo + LANES, end)
                while start < stop:
                    k, src = divmod(ref_col + start - first, SHARD_COLS)
                    length = min(stop - start, SHARD_COLS - src)
                    tile = _place(tile, lane, s_ref.at[k], src, start - lo, length)
                    start += length
            o_ref[:, lo:lo + LANES] = tile.astype(o_ref.dtype)

    return pl.pallas_call(
        body,
        name="layout_w_in",
        grid=(d // rows,),
        in_specs=[pl.BlockSpec((N_CHIPS, rows, SHARD_PAD_COLS), lambda i: (0, i, 0))],
        out_specs=pl.BlockSpec((rows, P_COLS), lambda i: (i, 0)),
        out_shape=jax.ShapeDtypeStruct((d, P_COLS), stacked.dtype),
        compiler_params=_cparams(("parallel",)),
    )(stacked)


def _stack_w_in_grad(pieces):
    d = pieces[0].shape[0]
    rows = _tile(d, 256)

    def body(*refs):
        piece_refs, o_ref = refs[:-1], refs[-1]
        lane = lax.broadcasted_iota(jnp.int32, (rows, LANES), 1)
        for k in range(N_CHIPS):
            for j in range(0, SHARD_COLS, LANES):
                width = min(LANES, SHARD_COLS - j)
                lo = k * SHARD_COLS + j
                tile = jnp.zeros((rows, LANES), F32)
                for first, end, piece, at in REF_RUNS:
                    start, stop = max(lo, first), min(lo + width, end)
                    if start >= stop:
                        continue
                    tile = _place(tile, lane, piece_refs[piece], at + start - first, start - lo, stop - start)
                o_ref[k, :, j:j + width] = tile[:, :width]

    return pl.pallas_call(
        body,
        name="stack_w_in_grad",
        grid=(d // rows,),
        in_specs=[pl.BlockSpec((rows, p.shape[1]), lambda i: (i, 0)) for p in pieces],
        out_specs=pl.BlockSpec((N_CHIPS, rows, SHARD_COLS), lambda i: (0, i, 0)),
        out_shape=jax.ShapeDtypeStruct((N_CHIPS, d, SHARD_COLS), F32),
        compiler_params=_cparams(("parallel",)),
    )(*pieces)


def _local_step(x, target, meta, norms, b_forget, sinks, w, comm=None):
    n1, nmix, n2, nfin = norms
    w1i, w1o = w[:2]
    if meta is not None:
        x = jnp.concatenate([jnp.zeros((x.shape[0], N_PAD, x.shape[2]), F32),
                             jnp.broadcast_to(meta[None], (x.shape[0], N_META, x.shape[2])), x], axis=1)
    bsz, l, d = x.shape
    nb = l // BLOCK
    t = bsz * l
    h0 = x.reshape(t, d)

    if comm is None:
        (h1, g1, u1), _ = _ffn_fwd(h0, n1, w1i, w1o)
        w_in, wa, wb, wo, w2i, w2o = w[2:]
        w_in = jnp.pad(w_in.reshape(d, N_CHIPS, SHARD_COLS).transpose(1, 0, 2),
                       ((0, 0), (0, 0), (0, SHARD_PAD_COLS - SHARD_COLS)))
    else:
        (h1, g1, u1), (w_in,) = _ffn_fwd(h0, n1, w1i, w1o, comm.gather(GATHER_PROJ))
    wp = _layout_w_in(w_in)
    un, qa, ka, va, qb, kb, vb, ga, gb, f_logit = _proj_fwd(h1, nmix, wp)
    b_pad = jnp.concatenate([b_forget, jnp.zeros((1, F_COLS - B_HEADS), F32)], axis=1)
    c = _forget_cumsum(f_logit, b_pad, nb)
    c_heads = c[:, :B_HEADS].reshape(bsz, l, B_HEADS).transpose(0, 2, 1).reshape(bsz * B_HEADS, l)
    c_row = c_heads.reshape(bsz * B_HEADS, nb, 1, BLOCK)

    slopes = jnp.exp2(-8.0 * jnp.arange(1, A_HEADS + 1, dtype=F32) / A_HEADS)
    slope_rows = jnp.repeat(slopes.reshape(A_KV_HEADS, A_GROUP), BLOCK, axis=1)[:, :, None]
    sink_rows = jnp.repeat(sinks.reshape(A_KV_HEADS, A_GROUP), BLOCK, axis=1)[:, :, None]

    oa, lse_a = _swa_fwd(qa, ka, va, sink_rows, slope_rows, nb)
    if comm is None:
        (ob, ob_exact, lse_b), _ = _fox_fwd(qb, kb, vb, c_row, nb)
    else:
        (ob, ob_exact, lse_b), (wa, wb, wo, w2i, w2o) = _fox_fwd(qb, kb, vb, c_row, nb, comm.gather(GATHER_LATE))
    wa_p = jnp.concatenate(_pad_tiles(wa, 0, A_HEADS, A_SLOT, 0), axis=0)
    h2 = _merge_fwd(h1, oa, ob, ga, gb, wa_p, wb, wo)
    (h3, g2, u2), _ = _ffn_fwd(h2, n2, w2i, w2o)
    loss, dh3, d_nfin = _loss_head(h3.reshape(bsz, l, d), nfin, target)

    (dh2, n2b, a2, dgu2, df2, dn2_parts), _ = _ffn_bwd(dh3.reshape(t, d), h2, n2, g2, u2, w2i, w2o)
    g_w2o = _tn_matmul(a2, df2, "grad_ffn2_w_out")
    g_w2i = _tn_matmul(n2b, dgu2, "grad_ffn2_w_in")

    hosted = comm.swap("ffn2", dict(ffn2_w_in=g_w2i, ffn2_w_out=g_w2o)) if comm else None
    (doa, dob, dgates, g_wa, g_wb, g_wo), swapped = _merge_bwd(dh2, oa, ob, ga, gb, wa_p, wb, wo, hosted)
    g_wa = jnp.concatenate(_unpad_tiles(g_wa, 0, A_HEADS, A_SLOT, 0), axis=0)

    dqa, dka, dva, dsink_rows = _swa_bwd(qa, ka, va, doa, lse_a, sink_rows, slope_rows, nb)
    hosted = comm.scatter("ffn2", swapped) if comm else None
    (dqb, dkb, dvb, dc_row), pieces = _fox_bwd(qb, kb, vb, ob_exact, dob, lse_b, c_row, nb, hosted)
    if comm:
        comm.received("ffn2", pieces)
    dc = dc_row.reshape(bsz, B_HEADS, l).transpose(0, 2, 1).reshape(t, B_HEADS)
    dc = jnp.concatenate([dc, jnp.zeros((t, F_COLS - B_HEADS), F32)], axis=1)
    df_logit, db_parts = _forget_cumsum_bwd(dc, f_logit, b_pad, nb)

    dproj = (jnp.concatenate([dqa, dka, dva, df_logit], axis=1), jnp.concatenate([dqb, dkb, dvb], axis=1), dgates)
    g_win = _stack_w_in_grad([_tn_matmul(un, piece, "grad_w_in_" + tag) for piece, tag in zip(dproj, ("a", "b", "gates"))])
    if comm is None:
        g_win = g_win.transpose(1, 0, 2).reshape(d, W_IN_COLS)
    hosted = comm.swap("mixer", dict(w_in=g_win, w_branch_a=g_wa, w_branch_b=g_wb, w_out=g_wo)) if comm else None
    (dh1, dnmix_parts), swapped = _proj_bwd(dh2, h1, nmix, dproj, wp, hosted)
    hosted = comm.scatter("mixer", swapped) if comm else None
    (dh0, n1b, a1, dgu1, df1, dn1_parts), pieces = _ffn_bwd(dh1, h0, n1, g1, u1, w1i, w1o, hosted)
    dh0 = dh0.reshape(bsz, l, d)
    grad_x = dh0[:, PREFIX:]
    small = dict(
        meta_tokens=jnp.sum(dh0[:, N_PAD:PREFIX], axis=0),
        ffn1_norm=jnp.sum(dn1_parts, axis=0),
        mix_norm=jnp.sum(dnmix_parts, axis=0),
        ffn2_norm=jnp.sum(dn2_parts, axis=0),
        final_norm=d_nfin,
        b_forget=jnp.sum(db_parts, axis=0)[:, :B_HEADS],
        attn_sinks=jnp.sum(dsink_rows.reshape(bsz, A_HEADS, BLOCK), axis=(0, 2)).reshape(1, A_HEADS),
    )
    if comm is None:
        g_w1o = _tn_matmul(a1, df1, "grad_ffn1_w_out")
        g_w1i = _tn_matmul(n1b, dgu1, "grad_ffn1_w_in")
    else:
        comm.received("mixer", pieces)
        g_w1o, gathered = _tn_matmul(a1, df1, "grad_ffn1_w_out", comm.small_gather(loss, small))
        comm.small_gathered(gathered)
        swapped = _run_exchange(comm.swap("ffn1_out", dict(ffn1_w_out=g_w1o)), "exchange_halves_ffn1_out")
        g_w1i, pieces = _tn_matmul(n1b, dgu1, "grad_ffn1_w_in", comm.scatter("ffn1_out", swapped))
        comm.received("ffn1_out", pieces)
    big = dict(ffn1_w_in=g_w1i, ffn1_w_out=g_w1o, w_in=g_win, w_branch_a=g_wa, w_branch_b=g_wb,
               w_out=g_wo, ffn2_w_in=g_w2i, ffn2_w_out=g_w2o)
    return loss, grad_x, small, big


BIG = (
    ("ffn1_w_in", (D_MODEL, 5632), 1),
    ("ffn1_w_out", (2816, D_MODEL), 0),
    ("w_in", (D_MODEL, W_IN_COLS), 1),
    ("w_branch_a", (A_WIDTH, D_MODEL), 1),
    ("w_branch_b", (B_WIDTH, D_MODEL), 1),
    ("w_out", (D_MODEL, D_MODEL), 0),
    ("ffn2_w_in", (D_MODEL, 5632), 1),
    ("ffn2_w_out", (2816, D_MODEL), 0),
)
STACKED = "w_in"


def _coords():
    return lax.axis_index("x"), lax.axis_index("y"), lax.axis_index("c")


def _other_chips(x, y):
    return ((1 - x, y), (x, 1 - y), (1 - x, 1 - y))


def _chip_part(ref, name, shape, axis, k):
    if name == STACKED:
        return ref.at[k]
    size = shape[axis] // N_CHIPS
    start = pl.multiple_of(k * size, size)
    return ref.at[pl.ds(start, size), :] if axis == 0 else ref.at[:, pl.ds(start, size)]


def _full_shape(name, shape):
    return (N_CHIPS, shape[0], shape[1] // N_CHIPS) if name == STACKED else shape


class _Exchange:
    def __init__(self, ins, out_shape, n_sems, ops):
        self.ins, self.out_shape, self.n_sems, self.ops = list(ins), list(out_shape), n_sems, ops

    @property
    def scratch(self):
        return [pltpu.SemaphoreType.DMA((self.n_sems,)), pltpu.SemaphoreType.DMA((self.n_sems,))]


SEMS_PER_GATHER = 9


def _gather_exchange(shards, table):
    n = len(table)
    x_nbr, y_nbr, diagonal = 0, 1, 2

    def ops(ins, outs, send_sems, recv_sems):
        x, y, c = _coords()
        mine = 2 * x + y
        sibling = (x, y, 1 - c)
        chips = _other_chips(x, y)
        slots = [2 * chip[0] + chip[1] for chip in chips]

        def part(i, k):
            name, shape, axis = table[i][:3]
            return _chip_part(outs[i], name, shape, axis, k)

        def half(ref, h):
            rows = ref.shape[0] // 2
            return ref.at[pl.ds(pl.multiple_of(h * rows, rows), rows), :]

        def remote(i, sem, src, dst, device):
            sem = SEMS_PER_GATHER * i + sem
            return pltpu.make_async_remote_copy(src, dst, send_sems.at[sem], recv_sems.at[sem],
                                                device_id=device, device_id_type=MESH_ID)

        def own(i):
            return remote(i, 0, ins[i], part(i, mine), sibling)

        def fetch(i, j, slot):
            if table[i][4]:
                src, dst = half(ins[i], c), half(part(i, slot), c)
            else:
                src, dst = ins[i], part(i, slot)
            return remote(i, 1 + j, src, dst, (chips[j][0], chips[j][1], c))

        def relayed(i, via, of):
            region = half(half(part(i, slots[of]), c), via)
            return remote(i, 4 + via, region, region, (chips[via][0], chips[via][1], c))

        def forward(i, j, h):
            region = half(part(i, slots[j]), h)
            return remote(i, 6 + j, region, region, sibling)

        def start():
            for i in range(n):
                for j in (x_nbr, y_nbr) if table[i][4] else (x_nbr, y_nbr, diagonal):
                    fetch(i, j, mine).start()
            for i in range(n):
                own(i).start()

        def relay():
            for i in range(n):
                if not table[i][4]:
                    for j in range(3):
                        fetch(i, j, slots[j]).wait_recv()
                    continue
                fetch(i, y_nbr, slots[y_nbr]).wait_recv()
                relayed(i, x_nbr, y_nbr).start()
                forward(i, y_nbr, c).start()
                fetch(i, x_nbr, slots[x_nbr]).wait_recv()
                relayed(i, y_nbr, x_nbr).start()
                forward(i, x_nbr, c).start()

        def relay_diagonal():
            for i in range(n):
                if table[i][4]:
                    relayed(i, x_nbr, diagonal).wait_recv()
                    relayed(i, y_nbr, diagonal).wait_recv()
                    forward(i, diagonal, c).start()

        def finish():
            for i in range(n):
                own(i).wait()
                for j in range(3):
                    if table[i][4]:
                        forward(i, j, 1 - c).wait_recv()
                        forward(i, j, c).wait_send()
                    if j != diagonal or not table[i][4]:
                        fetch(i, j, mine).wait_send()
                if table[i][4]:
                    relayed(i, x_nbr, y_nbr).wait_send()
                    relayed(i, y_nbr, x_nbr).wait_send()

        return start, relay, relay_diagonal, finish

    out_shape = [jax.ShapeDtypeStruct(_full_shape(name, shape), dtype) for name, shape, _, dtype, _ in table]
    return _Exchange(shards, out_shape, SEMS_PER_GATHER * n, ops)


STREAM_ROWS = 256


def _stream_rows(src, dst, dst_first, buf, sem_in, sem_out):
    bsz, n_rows, _ = src.shape
    chunks = [(b, r) for b in range(bsz) for r in range(0, n_rows, STREAM_ROWS)]

    def load(i):
        b, r = chunks[i]
        return pltpu.make_async_copy(src.at[b, pl.ds(r, STREAM_ROWS), :], buf.at[i % 2], sem_in.at[i % 2])

    def store(i):
        b, r = chunks[i]
        return pltpu.make_async_copy(buf.at[i % 2], dst.at[b, pl.ds(dst_first + r, STREAM_ROWS), :],
                                     sem_out.at[i % 2])

    load(0).start()
    for i in range(len(chunks)):
        if i + 1 < len(chunks):
            if i >= 1:
                store(i - 1).wait()
            load(i + 1).start()
        load(i).wait()
        store(i).start()
    for i in range(max(len(chunks) - 2, 0), len(chunks)):
        store(i).wait()


def _run_exchange(exchange, name):
    n = len(exchange.ins)

    def body(*refs):
        for phase in exchange.ops(refs[:n], refs[n:2 * n], *refs[2 * n:]):
            phase()

    return pl.pallas_call(
        body,
        name=name,
        in_specs=[HBM_SPEC] * n,
        out_specs=[HBM_SPEC] * n,
        out_shape=exchange.out_shape,
        scratch_shapes=exchange.scratch,
    )(*exchange.ins)


CAST_ROWS = 64


def _cast_hosting(arrays, exchange, name, stream=None):
    n_a, n_x = len(arrays), len(exchange.ins)
    k = 1 if stream else 0

    def body(*refs):
        a_in, x_in = refs[:n_a], refs[n_a:n_a + n_x]
        outs = refs[n_a + n_x + k:]
        a_out, x_out = outs[:n_a], outs[n_a:n_a + n_x]
        scratch = outs[n_a + n_x + k:]
        start, *rest = exchange.ops(x_in, x_out, *scratch[:2])
        start()
        for src, dst in zip(a_in, a_out):
            def rows(i, carry, src=src, dst=dst):
                window = pl.ds(pl.multiple_of(i * CAST_ROWS, CAST_ROWS), CAST_ROWS)
                cols = src.shape[1]
                if dst.shape[1] != cols:
                    dst[window, dst.shape[1] - LANES:] = jnp.zeros((CAST_ROWS, LANES), BF16)
                dst[window, :cols] = src[window, :].astype(BF16)
                return carry

            lax.fori_loop(0, src.shape[0] // CAST_ROWS, rows, 0)
        if stream:
            x_ref, h_ref = refs[n_a + n_x], outs[n_a + n_x]
            buf, sem_in, sem_out = scratch[2:]
            _stream_rows(x_ref, h_ref, PREFIX, buf, sem_in, sem_out)
        for phase in rest:
            phase()
        if stream:
            meta = pltpu.make_async_copy(x_out[stream[1]], buf.at[1, pl.ds(0, N_META), :], sem_in.at[0])
            meta.start()
            buf[0, 0:N_PAD, :] = jnp.zeros((N_PAD, buf.shape[2]), F32)
            meta.wait()
            buf[0, N_PAD:PREFIX, :] = buf[1, 0:N_META, :]
            puts = [pltpu.make_async_copy(buf.at[0, pl.ds(0, PREFIX), :], h_ref.at[b, pl.ds(0, PREFIX), :], sem_out.at[b])
                    for b in range(h_ref.shape[0])]
            for put in puts:
                put.start()
            for put in puts:
                put.wait()

    extra_in, extra_out, extra_scratch = [], [], []
    if stream:
        x = stream[0]
        assert x.shape[0] <= 2 and x.shape[1] % STREAM_ROWS == 0 and x.dtype == F32
        extra_in = [x]
        extra_out = [jax.ShapeDtypeStruct((x.shape[0], PREFIX + x.shape[1], x.shape[2]), F32)]
        extra_scratch = [pltpu.VMEM((2, STREAM_ROWS, x.shape[2]), F32),
                         pltpu.SemaphoreType.DMA((2,)), pltpu.SemaphoreType.DMA((2,))]
    outs = pl.pallas_call(
        body,
        name=name,
        in_specs=[VMEM_SPEC] * n_a + [HBM_SPEC] * (n_x + k),
        out_specs=[VMEM_SPEC] * n_a + [HBM_SPEC] * (n_x + k),
        out_shape=[jax.ShapeDtypeStruct((a.shape[0], -(-a.shape[1] // LANES) * LANES), BF16) for a in arrays]
        + exchange.out_shape + extra_out,
        scratch_shapes=exchange.scratch + extra_scratch,
        compiler_params=pltpu.CompilerParams(vmem_limit_bytes=VMEM_LIMIT),
    )(*arrays, *exchange.ins, *extra_in)
    return outs[:n_a], outs[n_a:]


def _host_exchange(exchange, in_refs, out_refs, sem_refs, first, middle, last, late=None):
    start, *relays, finish = exchange.ops(in_refs, out_refs, *sem_refs)
    pl.when(first)(start)
    pl.when(middle)(relays[0])
    if len(relays) > 1:
        pl.when(last if late is None else late)(relays[1])
    pl.when(last)(finish)


def _halves_view(name, shape, axis):
    r, c = shape
    if name == STACKED:
        return (N_CHIPS, 2, r // 2, c // N_CHIPS), lambda ref, h: ref.at[:, h]
    if axis == 1:
        return (2, r // 2, c), lambda ref, h: ref.at[h]
    return (N_CHIPS, 2, r // N_CHIPS // 2, c), lambda ref, h: ref.at[:, h]


def _halves_exchange(grads, entries):
    n_w = len(entries)
    views = [_halves_view(*entry) for entry in entries]

    def ops(ins, outs, send_sems, recv_sems):
        x, y, c = _coords()
        copies = [pltpu.make_async_remote_copy(views[i][1](ins[i], 1 - c), outs[i], send_sems.at[i], recv_sems.at[i],
                                               device_id=(x, y, 1 - c), device_id_type=MESH_ID) for i in range(n_w)]

        def start():
            for cp in copies:
                cp.start()

        def finish():
            for cp in copies:
                cp.wait()

        return start, lambda: None, finish

    half_shape = lambda v: tuple(d for i, d in enumerate(v) if i != (1 if len(v) == 4 else 0))
    out_shape = [jax.ShapeDtypeStruct(half_shape(v[0]), F32) for v in views]
    return _Exchange([g.reshape(v[0]) for g, v in zip(grads, views)], out_shape, n_w, ops)


def _add_sibling(g_view, recv, c, name):
    shape = recv.shape
    if len(shape) == 2:
        tr = _tile(shape[0], 128, 16)
        grid = (shape[0] // tr,)
        g_spec = pl.BlockSpec((None, tr, shape[1]), lambda i, c_ref: (c_ref[0], i, 0))
        r_spec = pl.BlockSpec((tr, shape[1]), lambda i, c_ref: (i, 0))
    else:
        tr = _tile(shape[1], 256, 16)
        grid = (N_CHIPS, shape[1] // tr)
        g_spec = pl.BlockSpec((None, None, tr, shape[2]), lambda k, i, c_ref: (k, c_ref[0], i, 0))
        r_spec = pl.BlockSpec((None, tr, shape[2]), lambda k, i, c_ref: (k, i, 0))

    def body(c_ref, g_ref, r_ref, o_ref):
        o_ref[...] = (g_ref[...] + r_ref[...]).astype(BF16)

    return pl.pallas_call(
        body,
        name="add_sibling_" + name,
        grid_spec=pltpu.PrefetchScalarGridSpec(num_scalar_prefetch=1, grid=grid, in_specs=[g_spec, r_spec],
                                               out_specs=r_spec),
        out_shape=jax.ShapeDtypeStruct(shape, BF16),
        compiler_params=_cparams(("parallel",) * len(grid)),
    )(c, g_view, recv)


def _piece_of(ref, name, axis, k):
    if name == STACKED or axis == 0:
        return ref.at[k]
    size = ref.shape[1] // N_CHIPS
    return ref.at[:, pl.ds(pl.multiple_of(k * size, size), size)]


def _piece_shape(name, shape, axis):
    r, c = shape
    return (r // 2, c // N_CHIPS) if (axis == 1) else (r // N_CHIPS // 2, c)


def _scatter_exchange(partials, entries):
    n_w = len(entries)

    def ops(ins, outs, send_sems, recv_sems):
        x, y, c = _coords()
        chips = _other_chips(x, y)
        copies = []
        for i, (name, _, axis) in enumerate(entries):
            for j, chip in enumerate(chips):
                sem = 3 * i + j
                copies.append(pltpu.make_async_remote_copy(
                    _piece_of(ins[i], name, axis, 2 * chip[0] + chip[1]), outs[i].at[j], send_sems.at[sem],
                    recv_sems.at[sem], device_id=(chip[0], chip[1], c), device_id_type=MESH_ID))

        def start():
            for cp in copies:
                cp.start()

        def finish():
            for cp in copies:
                cp.wait()

        return start, lambda: None, finish

    out_shape = [jax.ShapeDtypeStruct((3,) + _piece_shape(*entry), BF16) for entry in entries]
    return _Exchange(partials, out_shape, 3 * n_w, ops)


def _add_chips(partial, recv, mine, name, axis):
    rows, cols = recv.shape[1:]
    tr = _tile(rows, 256, 16)
    if name == STACKED or axis == 0:
        p_spec = pl.BlockSpec((None, tr, cols), lambda i, k_ref: (k_ref[0], i, 0))
    else:
        p_spec = pl.BlockSpec((tr, cols), lambda i, k_ref: (i, k_ref[0]))

    def body(k_ref, p_ref, r_ref, o_ref):
        f32 = lambda a: a.astype(F32)
        o_ref[...] = ((f32(p_ref[...]) + f32(r_ref[0])) + f32(r_ref[1])) + f32(r_ref[2])

    return pl.pallas_call(
        body,
        name="add_chips_" + name,
        grid_spec=pltpu.PrefetchScalarGridSpec(
            num_scalar_prefetch=1, grid=(rows // tr,),
            in_specs=[p_spec, pl.BlockSpec((3, tr, cols), lambda i, k_ref: (0, i, 0))],
            out_specs=pl.BlockSpec((tr, cols), lambda i, k_ref: (i, 0))),
        out_shape=jax.ShapeDtypeStruct((rows, cols), F32),
        compiler_params=_cparams(("parallel",)),
    )(mine, partial, recv)


def _share_exchange(halves):
    n_w = len(halves)

    def ops(ins, outs, send_sems, recv_sems):
        x, y, c = _coords()
        copies = [pltpu.make_async_remote_copy(ins[i], outs[i], send_sems.at[i], recv_sems.at[i],
                                               device_id=(x, y, 1 - c), device_id_type=MESH_ID) for i in range(n_w)]

        def start():
            for cp in copies:
                cp.start()

        def finish():
            for cp in copies:
                cp.wait()

        return start, lambda: None, finish

    return _Exchange(halves, [jax.ShapeDtypeStruct(h.shape, F32) for h in halves], n_w, ops)


class _SemsFrom:
    def __init__(self, sems, first):
        self.sems, self.first = sems, first

    @property
    def at(self):
        return self

    def __getitem__(self, i):
        return self.sems.at[self.first + i]


def _joined(a, b):
    n_a = len(a.ins)

    def ops(ins, outs, send_sems, recv_sems):
        phases_a = a.ops(ins[:n_a], outs[:n_a], send_sems, recv_sems)
        phases_b = b.ops(ins[n_a:], outs[n_a:], _SemsFrom(send_sems, a.n_sems), _SemsFrom(recv_sems, a.n_sems))
        assert len(phases_a) == len(phases_b)
        return tuple((lambda pa=pa, pb=pb: (pa(), pb())) for pa, pb in zip(phases_a, phases_b))

    return _Exchange(a.ins + b.ins, a.out_shape + b.out_shape, a.n_sems + b.n_sems, ops)


SMALL_ROWS = 168


def _small_exchange(buf):
    def ops(ins, outs, send_sems, recv_sems):
        x, y, c = _coords()
        me = 4 * x + 2 * y + c
        peers = [(x ^ fx, y ^ fy, c ^ fc) for fx in (0, 1) for fy in (0, 1) for fc in (0, 1)][1:]

        def copy(j, slot, dev):
            return pltpu.make_async_remote_copy(ins[0], outs[0].at[slot], send_sems.at[j], recv_sems.at[j],
                                                device_id=dev, device_id_type=MESH_ID)

        own = pltpu.make_async_copy(ins[0], outs[0].at[me], send_sems.at[N_DEV - 1])

        def start():
            own.start()
            for j, dev in enumerate(peers):
                copy(j, me, dev).start()

        def finish():
            for j, dev in enumerate(peers):
                copy(j, 4 * dev[0] + 2 * dev[1] + dev[2], dev).wait()
            own.wait()

        return start, lambda: None, finish

    return _Exchange([buf], [jax.ShapeDtypeStruct((N_DEV,) + buf.shape, F32)], N_DEV, ops)


def _sum_devices(gathered):
    def body(g_ref, out_ref):
        acc = g_ref[0]
        for d in range(1, N_DEV):
            acc = acc + g_ref[d]
        out_ref[...] = acc

    return pl.pallas_call(
        body,
        name="sum_devices",
        in_specs=[VMEM_SPEC],
        out_specs=VMEM_SPEC,
        out_shape=jax.ShapeDtypeStruct(gathered.shape[1:], F32),
    )(gathered)


def _adamw(w, g, m, v, copy_g=False):
    r, rest = w.shape[0], w.shape[1:]
    per_row = 1
    for dim in rest:
        per_row *= dim
    tr = _tile(r, max(8, (5 << 19) // (4 * per_row)), 8 if len(rest) == 1 else 1)

    def body(w_ref, g_ref, m_ref, v_ref, *out_refs):
        d_ref, mo_ref, vo_ref = out_refs[-3:]
        gg = g_ref[...]
        if copy_g:
            out_refs[0][...] = gg
        mm = ADAM_B1 * m_ref[...] + (1.0 - ADAM_B1) * gg
        vv = ADAM_B2 * v_ref[...] + (1.0 - ADAM_B2) * (gg * gg)
        m_hat = mm / (1.0 - ADAM_B1 ** ADAM_STEP)
        v_hat = vv / (1.0 - ADAM_B2 ** ADAM_STEP)
        d_ref[...] = -ADAM_LR * (m_hat / (jnp.sqrt(v_hat) + ADAM_EPS) + ADAM_WD * w_ref[...])
        mo_ref[...] = mm
        vo_ref[...] = vv

    n_out = 4 if copy_g else 3
    spec = pl.BlockSpec((tr,) + rest, lambda i: (i,) + (0,) * len(rest))
    return pl.pallas_call(
        body,
        name="adamw",
        grid=(r // tr,),
        in_specs=[spec] * 4,
        out_specs=[spec] * n_out,
        out_shape=[jax.ShapeDtypeStruct(w.shape, F32)] * n_out,
        compiler_params=_cparams(("parallel",)),
    )(w, g, m, v)


def _adamw_halves(w, own, other, m, v, c, name):
    r, cols = w.shape
    half = r // 2
    tr = _tile(half, 256, 8)
    nt = half // tr
    whole = pl.BlockSpec((tr, cols), lambda h, i, c_ref: (h * nt + i, 0))
    part = pl.BlockSpec((tr, cols), lambda h, i, c_ref: (i, 0))

    def body(c_ref, w_ref, own_ref, other_ref, m_ref, v_ref, g_ref, d_ref, mo_ref, vo_ref):
        gg = jnp.where(pl.program_id(0) == c_ref[0], own_ref[...], other_ref[...])
        g_ref[...] = gg
        mm = ADAM_B1 * m_ref[...] + (1.0 - ADAM_B1) * gg
        vv = ADAM_B2 * v_ref[...] + (1.0 - ADAM_B2) * (gg * gg)
        m_hat = mm / (1.0 - ADAM_B1 ** ADAM_STEP)
        v_hat = vv / (1.0 - ADAM_B2 ** ADAM_STEP)
        d_ref[...] = -ADAM_LR * (m_hat / (jnp.sqrt(v_hat) + ADAM_EPS) + ADAM_WD * w_ref[...])
        mo_ref[...] = mm
        vo_ref[...] = vv

    return pl.pallas_call(
        body,
        name="adamw_" + name,
        grid_spec=pltpu.PrefetchScalarGridSpec(
            num_scalar_prefetch=1, grid=(2, nt),
            in_specs=[whole, part, part, whole, whole], out_specs=[whole] * 4),
        out_shape=[jax.ShapeDtypeStruct((r, cols), F32)] * 4,
        compiler_params=_cparams(("parallel", "parallel")),
    )(c, w, own, other, m, v)


GATHER_FIRST = ("ffn1_w_in", "ffn1_w_out")
GATHER_PROJ = ("w_in",)
GATHER_LATE = ("w_branch_a", "w_branch_b", "w_out", "ffn2_w_in", "ffn2_w_out")


class _Comm:
    def __init__(self, shards, c_arr, mine_arr):
        self.shards, self.c, self.mine = shards, c_arr, mine_arr
        self.groups, self.halves = {}, {}
        self.by_name = {entry[0]: entry for entry in BIG}

    def small_gather(self, loss, small):
        pad_lanes = lambda a: jnp.concatenate([a, jnp.zeros((1, LANES - a.shape[1]), F32)], axis=1)
        buf = jnp.concatenate([
            small["meta_tokens"].reshape(128, LANES),
            small["ffn1_norm"].reshape(8, LANES), small["mix_norm"].reshape(8, LANES),
            small["ffn2_norm"].reshape(8, LANES), small["final_norm"].reshape(8, LANES),
            loss, pad_lanes(small["b_forget"]), pad_lanes(small["attn_sinks"]),
            jnp.zeros((SMALL_ROWS - 163, LANES), F32)], axis=0)
        return _small_exchange(buf)

    def small_gathered(self, outs):
        self.reduced = _sum_devices(outs[0])

    def gather(self, names):
        padded = (STACKED, (D_MODEL, N_CHIPS * SHARD_PAD_COLS), 1)
        table = [(padded if n == STACKED else self.by_name[n]) + (BF16, True) for n in names]
        return _gather_exchange([self.shards[n] for n in names], table)

    def swap(self, tag, grads):
        entries = [self.by_name[n] for n in grads]
        arrays = list(grads.values())
        self.groups[tag] = (entries, arrays)
        return _halves_exchange(arrays, entries)

    def scatter(self, tag, received):
        entries, arrays = self.groups[tag]
        views = [_halves_view(*entry) for entry in entries]
        partials = [_add_sibling(g.reshape(v[0]), r, self.c, name)
                    for g, v, r, (name, _, _) in zip(arrays, views, received, entries)]
        self.groups[tag] = (entries, partials)
        return _scatter_exchange(partials, entries)

    def received(self, tag, pieces):
        entries, partials = self.groups[tag]
        for p, r, (name, _, axis) in zip(partials, pieces, entries):
            self.halves[name] = _add_chips(p, r, self.mine, name, axis)

    def finish(self, tag, name, scatter):
        ready = [n for n, _, _ in BIG if n != name]
        outs = _run_exchange(_joined(scatter, _share_exchange([self.halves[n] for n in ready])), "scatter_chip_sums")
        self.received(tag, outs[:len(scatter.ins)])
        others = dict(zip(ready, outs[len(scatter.ins):]))
        others[name], = _run_exchange(_share_exchange([self.halves[name]]), "share_with_sibling")
        return {n: (self.halves[n], others[n]) for n, _, _ in BIG}


def kernel(x, meta_tokens, ffn1_norm, ffn1_w_in, ffn1_w_out, mix_norm, w_in, b_forget, attn_sinks, w_branch_a, w_branch_b, w_out, ffn2_norm, ffn2_w_in, ffn2_w_out, final_norm, loss_target, m_meta_tokens, m_ffn1_norm, m_ffn1_w_in, m_ffn1_w_out, m_mix_norm, m_w_in, m_b_forget, m_attn_sinks, m_w_branch_a, m_w_branch_b, m_w_out, m_ffn2_norm, m_ffn2_w_in, m_ffn2_w_out, m_final_norm, v_meta_tokens, v_ffn1_norm, v_ffn1_w_in, v_ffn1_w_out, v_mix_norm, v_w_in, v_b_forget, v_attn_sinks, v_w_branch_a, v_w_branch_b, v_w_out, v_ffn2_norm, v_ffn2_w_in, v_ffn2_w_out, v_final_norm):
    given = dict(locals())
    names = ["meta_tokens", "ffn1_norm", "ffn1_w_in", "ffn1_w_out", "mix_norm", "w_in", "b_forget", "attn_sinks",
             "w_branch_a", "w_branch_b", "w_out", "ffn2_norm", "ffn2_w_in", "ffn2_w_out", "final_norm"]
    big_names = [n for n, _, _ in BIG]
    cx, cy, cc = _coords()
    c_arr = cc.reshape(1).astype(jnp.int32)
    mine_arr = (2 * cx + cy).reshape(1).astype(jnp.int32)

    by_name = {entry[0]: entry for entry in BIG}
    shards = {n: given[n][0].astype(BF16) for n in GATHER_FIRST}
    table = [by_name[n] + (BF16, True) for n in GATHER_FIRST] + [("meta_tokens", (N_META, D_MODEL), 1, F32, False)]
    first = _gather_exchange([shards[n] for n in GATHER_FIRST] + [meta_tokens], table)
    late_names = [n for n in big_names if n not in GATHER_FIRST]
    casts, (w1i, w1o, _, stream) = _cast_hosting([given[n][0] for n in late_names], first, "gather_first",
                                                 stream=(x, len(GATHER_FIRST)))
    shards.update(zip(late_names, casts))
    comm = _Comm(shards, c_arr, mine_arr)
    norms = (ffn1_norm, mix_norm, ffn2_norm, final_norm.reshape(1, D_MODEL))
    loss, grad_x, small, big = _local_step(stream, loss_target, None, norms, b_forget, attn_sinks, (w1i, w1o), comm)

    swap = comm.swap("ffn1_in", dict(ffn1_w_in=big["ffn1_w_in"]))
    last = comm.scatter("ffn1_in", _run_exchange(swap, "exchange_halves_ffn1_in"))
    grad_halves = comm.finish("ffn1_in", "ffn1_w_in", last)
    grads = {}

    red = comm.reduced
    meta_cols = red[:128].reshape(N_META, D_MODEL)
    grads["meta_tokens"] = lax.dynamic_slice_in_dim(meta_cols, (2 * cx + cy) * (D_MODEL // N_CHIPS),
                                                    D_MODEL // N_CHIPS, axis=1)
    grads["ffn1_norm"] = red[128:136].reshape(1, D_MODEL)
    grads["mix_norm"] = red[136:144].reshape(1, D_MODEL)
    grads["ffn2_norm"] = red[144:152].reshape(1, D_MODEL)
    grads["final_norm"] = red[152:160].reshape(1, D_MODEL)
    loss_out = red[160, 0]
    grads["b_forget"] = red[161:162, :B_HEADS]
    grads["attn_sinks"] = red[162:163, :A_HEADS]

    out_g, out_d, out_m, out_v = [], [], [], []
    for n in names:
        w_full = given[n]
        shape = w_full.shape
        two_d = (lambda a: a.reshape(shape[-2], shape[-1])) if len(shape) >= 2 else (lambda a: a.reshape(1, shape[0]))
        if n == STACKED:
            own, other = grad_halves[n]
            g_nat = jnp.concatenate([jnp.where(cc == 0, own, other), jnp.where(cc == 0, other, own)], axis=0)
            rows = lambda a: a.reshape(1, shape[-2], shape[-1]).transpose(2, 0, 1)
            unrows = lambda a: a.transpose(1, 2, 0)
            g2, d2, m2, v2 = [unrows(a) for a in _adamw(rows(w_full), rows(g_nat), rows(given["m_" + n]),
                                                         rows(given["v_" + n]), copy_g=True)]
        elif n in grad_halves:
            own, other = grad_halves[n]
            g2, d2, m2, v2 = _adamw_halves(two_d(w_full), own, other, two_d(given["m_" + n]),
                                           two_d(given["v_" + n]), c_arr, n)
        else:
            g2 = two_d(grads[n])
            d2, m2, v2 = _adamw(two_d(w_full), g2, two_d(given["m_" + n]), two_d(given["v_" + n]))
        out_g.append(g2.reshape(shape))
        out_d.append(d2.reshape(shape))
        out_m.append(m2.reshape(shape))
        out_v.append(v2.reshape(shape))
    return (loss_out, grad_x, *out_g, *out_d, *out_m, *out_v)
```
